```python
import jax, jax.numpy as jnp
from jax import lax
import numpy as np

D_MODEL = 2048
BATCH = 8
SEQ = 4096
DEPTH = 2

A_HEADS = 16
A_KV_HEADS = 2
A_HEAD_DIM = 64
WINDOW = 128
B_HEADS = 8
B_HEAD_DIM = 128
CONV_K = 4
DN_CHUNK = 64
C_WIDTH = D_MODEL
C_GROUPS = 8
C_CHUNK = 128
C_GROUP_DIM = C_WIDTH // C_GROUPS
D_FF = -(-8 * D_MODEL // (3 * 256)) * 256
EPS = 1e-6

A_Q = A_HEADS * A_HEAD_DIM
A_KV = A_KV_HEADS * A_HEAD_DIM
B_W = B_HEADS * B_HEAD_DIM
EVEN_IN = A_Q + 2 * A_KV + 4 * B_W + 2 * B_HEADS
MIX_OUT = A_Q + B_W
EVEN_SPLITS = [int(s) for s in np.cumsum([A_Q, A_KV, A_KV, 3 * B_W, B_W, B_HEADS])]

kernel_name = "hybrid_swa_sink_gdn_gmlp_block"


def rms_norm(x, g):
    xf = x.astype(jnp.float32)
    y = xf * lax.rsqrt(jnp.mean(xf * xf, axis=-1, keepdims=True) + EPS)
    return (y * g.astype(jnp.float32)).astype(x.dtype)


def layer_norm(x, g, b):
    xf = x.astype(jnp.float32)
    mu = jnp.mean(xf, axis=-1, keepdims=True)
    var = jnp.mean(jnp.square(xf - mu), axis=-1, keepdims=True)
    y = (xf - mu) * lax.rsqrt(var + EPS)
    return (y * g.astype(jnp.float32) + b.astype(jnp.float32)).astype(x.dtype)


def l2_norm(x):
    return x * lax.rsqrt(jnp.sum(x * x, axis=-1, keepdims=True) + EPS)


def sliding_window_attention(q, k, v, sinks):
    b, t, hq, dh = q.shape
    hkv = k.shape[2]
    grp = hq // hkv
    nb = t // WINDOW
    qb = q.astype(jnp.float32).reshape(b, nb, WINDOW, hkv, grp, dh)

    def with_prev(x):
        xb = x.astype(jnp.float32).reshape(b, nb, WINDOW, hkv, dh)
        prev = jnp.pad(xb, ((0, 0), (1, 0), (0, 0), (0, 0), (0, 0)))[:, :-1]
        return jnp.concatenate([prev, xb], axis=2)

    kb, vb = with_prev(k), with_prev(v)
    s = jnp.einsum('bnqhgd,bnkhd->bnhgqk', qb, kb) * (dh ** -0.5)
    r = jnp.arange(WINDOW)[:, None]
    c = jnp.arange(2 * WINDOW)[None, :]
    rel = r + WINDOW - c
    blk = jnp.arange(nb)[:, None, None]
    valid = (rel >= 0) & (rel < WINDOW) & (blk * WINDOW - WINDOW + c >= 0)
    s = jnp.where(valid[None, :, None, None], s, -jnp.inf)
    sink = sinks.astype(jnp.float32).reshape(1, 1, hkv, grp, 1, 1)
    m = jnp.maximum(jnp.max(s, axis=-1, keepdims=True), sink)
    p = jnp.exp(s - m)
    denom = jnp.sum(p, axis=-1, keepdims=True) + jnp.exp(sink - m)
    o = jnp.einsum('bnhgqk,bnkhd->bnqhgd', p / denom, vb)
    return o.reshape(b, t, hq * dh).astype(q.dtype)


def causal_conv_silu(x, w):
    kk, ch = w.shape
    y = lax.conv_general_dilated(
        x, w[:, None, :].astype(x.dtype), window_strides=(1,),
        padding=((kk - 1, 0),), dimension_numbers=('NWC', 'WIO', 'NWC'),
        feature_group_count=ch)
    return jax.nn.silu(y)


def gated_delta_rule(q, k, v, beta, g):
    b, t, h, dk = q.shape
    dv = v.shape[-1]
    c = DN_CHUNK
    n = t // c

    def chunks(x):
        x = jnp.moveaxis(x, 2, 1)
        return x.reshape(b, h, n, c, *x.shape[3:])

    q, k, v, beta, g = (chunks(a) for a in (q, k, v, beta, g))
    gam = jnp.cumsum(g, axis=-1)
    idx = jnp.arange(c)
    incl = idx[:, None] >= idx[None, :]
    strict = idx[:, None] > idx[None, :]
    decay = jnp.exp(jnp.where(incl, gam[..., :, None] - gam[..., None, :], -jnp.inf))
    kk = jnp.einsum('bhnid,bhnjd->bhnij', k, k)
    a_mat = jnp.where(strict, beta[..., :, None] * kk * decay, 0.0) + jnp.eye(c, dtype=q.dtype)
    rhs = jnp.concatenate([v * beta[..., None], k * (beta * jnp.exp(gam))[..., None]], axis=-1)
    sol = lax.linalg.triangular_solve(a_mat, rhs, left_side=True, lower=True, unit_diagonal=True)
    u, w = sol[..., :dv], sol[..., dv:]
    qk = jnp.einsum('bhnid,bhnjd->bhnij', q, k) * decay
    q_dec = q * jnp.exp(gam)[..., None]
    k_dec = k * jnp.exp(gam[..., -1:] - gam)[..., None]
    g_last = jnp.exp(gam[..., -1])

    def step(state, xs):
        qd, kd, wc, uc, qkc, gl = xs
        v_new = uc - jnp.einsum('bhck,bhkv->bhcv', wc, state)
        o = jnp.einsum('bhck,bhkv->bhcv', qd, state) + jnp.einsum('bhij,bhjv->bhiv', qkc, v_new)
        state = state * gl[..., None, None] + jnp.einsum('bhck,bhcv->bhkv', kd, v_new)
        return state, o

    xs = tuple(jnp.moveaxis(a, 2, 0) for a in (q_dec, k_dec, w, u, qk, g_last))
    s0 = jnp.zeros((b, h, dk, dv), q.dtype)
    _, o = lax.scan(step, s0, xs)
    return jnp.transpose(o, (1, 0, 3, 2, 4)).reshape(b, t, h, dv)


def even_mixer(hn, w_in, conv_w, a_log, dt_bias, sinks, onorm, w_out):
    b, t, _ = hn.shape
    proj = hn @ w_in
    qa, ka, va, qkv_b, z, beta_raw, a_raw = jnp.split(proj, EVEN_SPLITS, axis=-1)
    out_a = sliding_window_attention(
        qa.reshape(b, t, A_HEADS, A_HEAD_DIM),
        ka.reshape(b, t, A_KV_HEADS, A_HEAD_DIM),
        va.reshape(b, t, A_KV_HEADS, A_HEAD_DIM), sinks)
    qkv_b = causal_conv_silu(qkv_b, conv_w).astype(jnp.float32)
    qb, kb, vb = jnp.split(qkv_b, 3, axis=-1)
    qb = l2_norm(qb.reshape(b, t, B_HEADS, B_HEAD_DIM)) * (B_HEAD_DIM ** -0.5)
    kb = l2_norm(kb.reshape(b, t, B_HEADS, B_HEAD_DIM))
    vb = vb.reshape(b, t, B_HEADS, B_HEAD_DIM)
    beta = jax.nn.sigmoid(beta_raw.astype(jnp.float32))
    g = -jnp.exp(a_log.astype(jnp.float32)) * jax.nn.softplus(
        a_raw.astype(jnp.float32) + dt_bias.astype(jnp.float32))
    o = gated_delta_rule(qb, kb, vb, beta, g)
    o = o * lax.rsqrt(jnp.mean(o * o, axis=-1, keepdims=True) + EPS) * onorm.astype(jnp.float32)
    o = o * jax.nn.silu(z.astype(jnp.float32).reshape(b, t, B_HEADS, B_HEAD_DIM))
    out_b = o.reshape(b, t, B_W).astype(hn.dtype)
    return jnp.concatenate([out_a, out_b], axis=-1) @ w_out


def odd_mixer(hn, w_in, ln_g, ln_b, w_s, b_s, w_out):
    b, t, _ = hn.shape
    zz = jax.nn.gelu(hn @ w_in, approximate=False)
    u, v = jnp.split(zz, 2, axis=-1)
    v = layer_norm(v, ln_g, ln_b)
    nb = t // C_CHUNK
    vb = v.reshape(b, nb, C_CHUNK, C_GROUPS, C_GROUP_DIM)
    ws = jnp.tril(w_s)
    mixed = jnp.einsum('gts,bnsgc->bntgc', ws, vb) + b_s.T[None, None, :, :, None]
    return (u * mixed.reshape(b, t, C_WIDTH)) @ w_out


def swiglu(hn, w_gate, w_up, w_down):
    return (jax.nn.silu(hn @ w_gate) * (hn @ w_up)) @ w_down


def _fwd_setup_inputs(seed: int = 0) -> dict:
    key = jax.random.key(seed)
    ks = jax.random.split(key, 22)
    ne, no = (DEPTH + 1) // 2, DEPTH // 2
    f32 = jnp.float32

    def nrm(k, shape, scale):
        return jax.random.normal(k, shape, f32) * scale

    def gain(k, shape):
        return 1.0 + 0.05 * jax.random.normal(k, shape, f32)

    return {
        "x": nrm(ks[0], (BATCH, SEQ, D_MODEL), 1.0),
        "even_norm": gain(ks[1], (ne, D_MODEL)),
        "even_w_in": nrm(ks[2], (ne, D_MODEL, EVEN_IN), D_MODEL ** -0.5),
        "even_conv": nrm(ks[3], (ne, CONV_K, 3 * B_W), CONV_K ** -0.5),
        "even_a_log": jnp.log(jax.random.uniform(ks[4], (ne, B_HEADS), f32, 1.0, 16.0)),
        "even_dt_bias": nrm(ks[5], (ne, B_HEADS), 0.1),
        "even_sinks": nrm(ks[6], (ne, A_HEADS), 1.0),
        "even_onorm": gain(ks[7], (ne, B_HEAD_DIM)),
        "even_w_out": nrm(ks[8], (ne, MIX_OUT, D_MODEL), MIX_OUT ** -0.5),
        "odd_norm": gain(ks[9], (no, D_MODEL)),
        "odd_w_in": nrm(ks[10], (no, D_MODEL, 2 * C_WIDTH), D_MODEL ** -0.5),
        "odd_ln_g": gain(ks[11], (no, C_WIDTH)),
        "odd_ln_b": nrm(ks[12], (no, C_WIDTH), 0.02),
        "odd_w_s": nrm(ks[13], (no, C_GROUPS, C_CHUNK, C_CHUNK), C_CHUNK ** -0.5),
        "odd_b_s": 1.0 + nrm(ks[14], (no, C_GROUPS, C_CHUNK), 0.1),
        "odd_w_out": nrm(ks[15], (no, C_WIDTH, D_MODEL), C_WIDTH ** -0.5),
        "ffn_norm": gain(ks[16], (DEPTH, D_MODEL)),
        "ffn_w_gate": nrm(ks[17], (DEPTH, D_MODEL, D_FF), D_MODEL ** -0.5),
        "ffn_w_up": nrm(ks[18], (DEPTH, D_MODEL, D_FF), D_MODEL ** -0.5),
        "ffn_w_down": nrm(ks[19], (DEPTH, D_FF, D_MODEL), D_FF ** -0.5),
        "final_norm": gain(ks[20], (D_MODEL,)),
    }


def _fwd_reference(x, even_norm, even_w_in, even_conv, even_a_log, even_dt_bias, even_sinks,
              even_onorm, even_w_out, odd_norm, odd_w_in, odd_ln_g, odd_ln_b, odd_w_s,
              odd_b_s, odd_w_out, ffn_norm, ffn_w_gate, ffn_w_up, ffn_w_down, final_norm):
    h = x
    for i in range(DEPTH):
        j = i // 2
        if i % 2 == 0:
            h = h + even_mixer(rms_norm(h, even_norm[j]), even_w_in[j], even_conv[j],
                               even_a_log[j], even_dt_bias[j], even_sinks[j],
                               even_onorm[j], even_w_out[j])
        else:
            h = h + odd_mixer(rms_norm(h, odd_norm[j]), odd_w_in[j], odd_ln_g[j],
                              odd_ln_b[j], odd_w_s[j], odd_b_s[j], odd_w_out[j])
        h = h + swiglu(rms_norm(h, ffn_norm[i]), ffn_w_gate[i], ffn_w_up[i], ffn_w_down[i])
    return rms_norm(h, final_norm)


import jax as _jax
import jax.numpy as _jnp

TWIN_FORMAT = 'train_step'
FWD_PARAMS = ['x', 'even_norm', 'even_w_in', 'even_conv', 'even_a_log', 'even_dt_bias', 'even_sinks', 'even_onorm', 'even_w_out', 'odd_norm', 'odd_w_in', 'odd_ln_g', 'odd_ln_b', 'odd_w_s', 'odd_b_s', 'odd_w_out', 'ffn_norm', 'ffn_w_gate', 'ffn_w_up', 'ffn_w_down', 'final_norm']
TWIN_WEIGHTS = ['even_norm', 'even_w_in', 'even_conv', 'even_a_log', 'even_dt_bias', 'even_sinks', 'even_onorm', 'even_w_out', 'odd_norm', 'odd_w_in', 'odd_ln_g', 'odd_ln_b', 'odd_w_s', 'odd_b_s', 'odd_w_out', 'ffn_norm', 'ffn_w_gate', 'ffn_w_up', 'ffn_w_down', 'final_norm']
TWIN_DIFF_INPUT = 'x'
TWIN_INPUTS = ['x', 'even_norm', 'even_w_in', 'even_conv', 'even_a_log', 'even_dt_bias', 'even_sinks', 'even_onorm', 'even_w_out', 'odd_norm', 'odd_w_in', 'odd_ln_g', 'odd_ln_b', 'odd_w_s', 'odd_b_s', 'odd_w_out', 'ffn_norm', 'ffn_w_gate', 'ffn_w_up', 'ffn_w_down', 'final_norm', 'loss_target', 'm_even_norm', 'm_even_w_in', 'm_even_conv', 'm_even_a_log', 'm_even_dt_bias', 'm_even_sinks', 'm_even_onorm', 'm_even_w_out', 'm_odd_norm', 'm_odd_w_in', 'm_odd_ln_g', 'm_odd_ln_b', 'm_odd_w_s', 'm_odd_b_s', 'm_odd_w_out', 'm_ffn_norm', 'm_ffn_w_gate', 'm_ffn_w_up', 'm_ffn_w_down', 'm_final_norm', 'v_even_norm', 'v_even_w_in', 'v_even_conv', 'v_even_a_log', 'v_even_dt_bias', 'v_even_sinks', 'v_even_onorm', 'v_even_w_out', 'v_odd_norm', 'v_odd_w_in', 'v_odd_ln_g', 'v_odd_ln_b', 'v_odd_w_s', 'v_odd_b_s', 'v_odd_w_out', 'v_ffn_norm', 'v_ffn_w_gate', 'v_ffn_w_up', 'v_ffn_w_down', 'v_final_norm']
TWIN_OUTPUTS = ['loss', 'grad_x', 'grad_even_norm', 'grad_even_w_in', 'grad_even_conv', 'grad_even_a_log', 'grad_even_dt_bias', 'grad_even_sinks', 'grad_even_onorm', 'grad_even_w_out', 'grad_odd_norm', 'grad_odd_w_in', 'grad_odd_ln_g', 'grad_odd_ln_b', 'grad_odd_w_s', 'grad_odd_b_s', 'grad_odd_w_out', 'grad_ffn_norm', 'grad_ffn_w_gate', 'grad_ffn_w_up', 'grad_ffn_w_down', 'grad_final_norm', 'delta_even_norm', 'delta_even_w_in', 'delta_even_conv', 'delta_even_a_log', 'delta_even_dt_bias', 'delta_even_sinks', 'delta_even_onorm', 'delta_even_w_out', 'delta_odd_norm', 'delta_odd_w_in', 'delta_odd_ln_g', 'delta_odd_ln_b', 'delta_odd_w_s', 'delta_odd_b_s', 'delta_odd_w_out', 'delta_ffn_norm', 'delta_ffn_w_gate', 'delta_ffn_w_up', 'delta_ffn_w_down', 'delta_final_norm', 'new_m_even_norm', 'new_m_even_w_in', 'new_m_even_conv', 'new_m_even_a_log', 'new_m_even_dt_bias', 'new_m_even_sinks', 'new_m_even_onorm', 'new_m_even_w_out', 'new_m_odd_norm', 'new_m_odd_w_in', 'new_m_odd_ln_g', 'new_m_odd_ln_b', 'new_m_odd_w_s', 'new_m_odd_b_s', 'new_m_odd_w_out', 'new_m_ffn_norm', 'new_m_ffn_w_gate', 'new_m_ffn_w_up', 'new_m_ffn_w_down', 'new_m_final_norm', 'new_v_even_norm', 'new_v_even_w_in', 'new_v_even_conv', 'new_v_even_a_log', 'new_v_even_dt_bias', 'new_v_even_sinks', 'new_v_even_onorm', 'new_v_even_w_out', 'new_v_odd_norm', 'new_v_odd_w_in', 'new_v_odd_ln_g', 'new_v_odd_ln_b', 'new_v_odd_w_s', 'new_v_odd_b_s', 'new_v_odd_w_out', 'new_v_ffn_norm', 'new_v_ffn_w_gate', 'new_v_ffn_w_up', 'new_v_ffn_w_down', 'new_v_final_norm']
TWIN_LEAF_KINDS = {'loss': 'loss', 'grad_x': 'grad_x', 'grad_even_norm': 'grad_w', 'grad_even_w_in': 'grad_w', 'grad_even_conv': 'grad_w', 'grad_even_a_log': 'grad_w', 'grad_even_dt_bias': 'grad_w', 'grad_even_sinks': 'grad_w', 'grad_even_onorm': 'grad_w', 'grad_even_w_out': 'grad_w', 'grad_odd_norm': 'grad_w', 'grad_odd_w_in': 'grad_w', 'grad_odd_ln_g': 'grad_w', 'grad_odd_ln_b': 'grad_w', 'grad_odd_w_s': 'grad_w', 'grad_odd_b_s': 'grad_w', 'grad_odd_w_out': 'grad_w', 'grad_ffn_norm': 'grad_w', 'grad_ffn_w_gate': 'grad_w', 'grad_ffn_w_up': 'grad_w', 'grad_ffn_w_down': 'grad_w', 'grad_final_norm': 'grad_w', 'delta_even_norm': 'delta_w', 'delta_even_w_in': 'delta_w', 'delta_even_conv': 'delta_w', 'delta_even_a_log': 'delta_w', 'delta_even_dt_bias': 'delta_w', 'delta_even_sinks': 'delta_w', 'delta_even_onorm': 'delta_w', 'delta_even_w_out': 'delta_w', 'delta_odd_norm': 'delta_w', 'delta_odd_w_in': 'delta_w', 'delta_odd_ln_g': 'delta_w', 'delta_odd_ln_b': 'delta_w', 'delta_odd_w_s': 'delta_w', 'delta_odd_b_s': 'delta_w', 'delta_odd_w_out': 'delta_w', 'delta_ffn_norm': 'delta_w', 'delta_ffn_w_gate': 'delta_w', 'delta_ffn_w_up': 'delta_w', 'delta_ffn_w_down': 'delta_w', 'delta_final_norm': 'delta_w', 'new_m_even_norm': 'new_m', 'new_m_even_w_in': 'new_m', 'new_m_even_conv': 'new_m', 'new_m_even_a_log': 'new_m', 'new_m_even_dt_bias': 'new_m', 'new_m_even_sinks': 'new_m', 'new_m_even_onorm': 'new_m', 'new_m_even_w_out': 'new_m', 'new_m_odd_norm': 'new_m', 'new_m_odd_w_in': 'new_m', 'new_m_odd_ln_g': 'new_m', 'new_m_odd_ln_b': 'new_m', 'new_m_odd_w_s': 'new_m', 'new_m_odd_b_s': 'new_m', 'new_m_odd_w_out': 'new_m', 'new_m_ffn_norm': 'new_m', 'new_m_ffn_w_gate': 'new_m', 'new_m_ffn_w_up': 'new_m', 'new_m_ffn_w_down': 'new_m', 'new_m_final_norm': 'new_m', 'new_v_even_norm': 'new_v', 'new_v_even_w_in': 'new_v', 'new_v_even_conv': 'new_v', 'new_v_even_a_log': 'new_v', 'new_v_even_dt_bias': 'new_v', 'new_v_even_sinks': 'new_v', 'new_v_even_onorm': 'new_v', 'new_v_even_w_out': 'new_v', 'new_v_odd_norm': 'new_v', 'new_v_odd_w_in': 'new_v', 'new_v_odd_ln_g': 'new_v', 'new_v_odd_ln_b': 'new_v', 'new_v_odd_w_s': 'new_v', 'new_v_odd_b_s': 'new_v', 'new_v_odd_w_out': 'new_v', 'new_v_ffn_norm': 'new_v', 'new_v_ffn_w_gate': 'new_v', 'new_v_ffn_w_up': 'new_v', 'new_v_ffn_w_down': 'new_v', 'new_v_final_norm': 'new_v'}


def _forward(args):
    return _fwd_reference(*[args[k] for k in FWD_PARAMS])


def _output_shape():
    def fwd():
        inp = _fwd_setup_inputs(0)
        return _fwd_reference(*[inp[k] for k in FWD_PARAMS])
    out = _jax.eval_shape(fwd)
    return out.shape, out.dtype

N_MICROBATCH = 1
ADAM_LR = 0.001
ADAM_B1 = 0.9
ADAM_B2 = 0.999
ADAM_EPS = 1e-08
ADAM_WD = 0.01
ADAM_STEP = 10
PER_EXAMPLE_BATCH_AXIS = {'x': 0, 'loss_target': 0}
SHARED_INPUTS = []
_WEIGHT_DTYPES = {'even_norm': _jnp.float32, 'even_w_in': _jnp.float32, 'even_conv': _jnp.float32, 'even_a_log': _jnp.float32, 'even_dt_bias': _jnp.float32, 'even_sinks': _jnp.float32, 'even_onorm': _jnp.float32, 'even_w_out': _jnp.float32, 'odd_norm': _jnp.float32, 'odd_w_in': _jnp.float32, 'odd_ln_g': _jnp.float32, 'odd_ln_b': _jnp.float32, 'odd_w_s': _jnp.float32, 'odd_b_s': _jnp.float32, 'odd_w_out': _jnp.float32, 'ffn_norm': _jnp.float32, 'ffn_w_gate': _jnp.float32, 'ffn_w_up': _jnp.float32, 'ffn_w_down': _jnp.float32, 'final_norm': _jnp.float32}
MOMENT_SCALE = {'even_norm': 7.901486e-02, 'even_w_in': 4.629933e-02, 'even_conv': 4.891015e-02, 'even_a_log': 1.753430e-01, 'even_dt_bias': 1.118938e-01, 'even_sinks': 2.189842e-02, 'even_onorm': 1.471488e-01, 'even_w_out': 4.126267e-02, 'odd_norm': 6.377849e-02, 'odd_w_in': 4.564494e-02, 'odd_ln_g': 3.008400e-02, 'odd_ln_b': 2.983320e-02, 'odd_w_s': 4.234112e-02, 'odd_b_s': 6.078127e-02, 'odd_w_out': 6.432783e-02, 'ffn_norm': 6.145585e-02, 'ffn_w_gate': 2.679918e-02, 'ffn_w_up': 2.610914e-02, 'ffn_w_down': 4.335344e-02, 'final_norm': 1.606791e+01}


def _to_microbatches(a, axis):
    t = _jnp.moveaxis(a, axis, 0)
    t = t.reshape((N_MICROBATCH, t.shape[0] // N_MICROBATCH) + t.shape[1:])
    return _jnp.moveaxis(t, 1, axis + 1)


def setup_inputs(seed: int = 0) -> dict:
    inp = _fwd_setup_inputs(seed)
    key = _jax.random.fold_in(_jax.random.key(seed), 7919)
    shape, _ = _output_shape()
    out = dict(inp)
    out["loss_target"] = _jax.random.normal(_jax.random.fold_in(key, 0), shape, _jnp.float32)
    for i, name in enumerate(TWIN_WEIGHTS):
        w = inp[name].astype(_jnp.float32)
        if MOMENT_SCALE is None:
            s = _jnp.sqrt(_jnp.mean(_jnp.square(w)) + 1e-30)
        else:
            s = MOMENT_SCALE[name]
        km, kv = _jax.random.split(_jax.random.fold_in(key, i + 1))
        out[name] = w
        out["m_" + name] = s * _jax.random.normal(km, w.shape, _jnp.float32)
        out["v_" + name] = (s * s) * _jax.random.uniform(kv, w.shape, _jnp.float32, 0.5, 1.5)
    if N_MICROBATCH > 1:
        for name, axis in PER_EXAMPLE_BATCH_AXIS.items():
            out[name] = _to_microbatches(out[name], axis)
    return {'x': out['x'], 'even_norm': out['even_norm'], 'even_w_in': out['even_w_in'], 'even_conv': out['even_conv'], 'even_a_log': out['even_a_log'], 'even_dt_bias': out['even_dt_bias'], 'even_sinks': out['even_sinks'], 'even_onorm': out['even_onorm'], 'even_w_out': out['even_w_out'], 'odd_norm': out['odd_norm'], 'odd_w_in': out['odd_w_in'], 'odd_ln_g': out['odd_ln_g'], 'odd_ln_b': out['odd_ln_b'], 'odd_w_s': out['odd_w_s'], 'odd_b_s': out['odd_b_s'], 'odd_w_out': out['odd_w_out'], 'ffn_norm': out['ffn_norm'], 'ffn_w_gate': out['ffn_w_gate'], 'ffn_w_up': out['ffn_w_up'], 'ffn_w_down': out['ffn_w_down'], 'final_norm': out['final_norm'], 'loss_target': out['loss_target'], 'm_even_norm': out['m_even_norm'], 'm_even_w_in': out['m_even_w_in'], 'm_even_conv': out['m_even_conv'], 'm_even_a_log': out['m_even_a_log'], 'm_even_dt_bias': out['m_even_dt_bias'], 'm_even_sinks': out['m_even_sinks'], 'm_even_onorm': out['m_even_onorm'], 'm_even_w_out': out['m_even_w_out'], 'm_odd_norm': out['m_odd_norm'], 'm_odd_w_in': out['m_odd_w_in'], 'm_odd_ln_g': out['m_odd_ln_g'], 'm_odd_ln_b': out['m_odd_ln_b'], 'm_odd_w_s': out['m_odd_w_s'], 'm_odd_b_s': out['m_odd_b_s'], 'm_odd_w_out': out['m_odd_w_out'], 'm_ffn_norm': out['m_ffn_norm'], 'm_ffn_w_gate': out['m_ffn_w_gate'], 'm_ffn_w_up': out['m_ffn_w_up'], 'm_ffn_w_down': out['m_ffn_w_down'], 'm_final_norm': out['m_final_norm'], 'v_even_norm': out['v_even_norm'], 'v_even_w_in': out['v_even_w_in'], 'v_even_conv': out['v_even_conv'], 'v_even_a_log': out['v_even_a_log'], 'v_even_dt_bias': out['v_even_dt_bias'], 'v_even_sinks': out['v_even_sinks'], 'v_even_onorm': out['v_even_onorm'], 'v_even_w_out': out['v_even_w_out'], 'v_odd_norm': out['v_odd_norm'], 'v_odd_w_in': out['v_odd_w_in'], 'v_odd_ln_g': out['v_odd_ln_g'], 'v_odd_ln_b': out['v_odd_ln_b'], 'v_odd_w_s': out['v_odd_w_s'], 'v_odd_b_s': out['v_odd_b_s'], 'v_odd_w_out': out['v_odd_w_out'], 'v_ffn_norm': out['v_ffn_norm'], 'v_ffn_w_gate': out['v_ffn_w_gate'], 'v_ffn_w_up': out['v_ffn_w_up'], 'v_ffn_w_down': out['v_ffn_w_down'], 'v_final_norm': out['v_final_norm']}


def _loss(weights, diff, rest, loss_target):
    with _jax.named_scope("forward"):
        args = {**rest, TWIN_DIFF_INPUT: diff, **{k: w.astype(_WEIGHT_DTYPES[k]) for k, w in weights.items()}}
        y = _forward(args)
    with _jax.named_scope("loss_head"):
        err = _jnp.square(y.astype(_jnp.float32) - loss_target)
        return 0.5 * _jnp.sum(_jnp.mean(err, axis=-1)) if err.ndim else 0.5 * err


def _adamw(w, g, m, v):
    m = ADAM_B1 * m + (1.0 - ADAM_B1) * g
    v = ADAM_B2 * v + (1.0 - ADAM_B2) * _jnp.square(g)
    m_hat = m / (1.0 - ADAM_B1 ** ADAM_STEP)
    v_hat = v / (1.0 - ADAM_B2 ** ADAM_STEP)
    delta = -ADAM_LR * (m_hat / (_jnp.sqrt(v_hat) + ADAM_EPS) + ADAM_WD * w)
    return delta, m, v


def reference(x, even_norm, even_w_in, even_conv, even_a_log, even_dt_bias, even_sinks, even_onorm, even_w_out, odd_norm, odd_w_in, odd_ln_g, odd_ln_b, odd_w_s, odd_b_s, odd_w_out, ffn_norm, ffn_w_gate, ffn_w_up, ffn_w_down, final_norm, loss_target, m_even_norm, m_even_w_in, m_even_conv, m_even_a_log, m_even_dt_bias, m_even_sinks, m_even_onorm, m_even_w_out, m_odd_norm, m_odd_w_in, m_odd_ln_g, m_odd_ln_b, m_odd_w_s, m_odd_b_s, m_odd_w_out, m_ffn_norm, m_ffn_w_gate, m_ffn_w_up, m_ffn_w_down, m_final_norm, v_even_norm, v_even_w_in, v_even_conv, v_even_a_log, v_even_dt_bias, v_even_sinks, v_even_onorm, v_even_w_out, v_odd_norm, v_odd_w_in, v_odd_ln_g, v_odd_ln_b, v_odd_w_s, v_odd_b_s, v_odd_w_out, v_ffn_norm, v_ffn_w_gate, v_ffn_w_up, v_ffn_w_down, v_final_norm):
    given = dict(x=x, even_norm=even_norm, even_w_in=even_w_in, even_conv=even_conv, even_a_log=even_a_log, even_dt_bias=even_dt_bias, even_sinks=even_sinks, even_onorm=even_onorm, even_w_out=even_w_out, odd_norm=odd_norm, odd_w_in=odd_w_in, odd_ln_g=odd_ln_g, odd_ln_b=odd_ln_b, odd_w_s=odd_w_s, odd_b_s=odd_b_s, odd_w_out=odd_w_out, ffn_norm=ffn_norm, ffn_w_gate=ffn_w_gate, ffn_w_up=ffn_w_up, ffn_w_down=ffn_w_down, final_norm=final_norm, loss_target=loss_target, m_even_norm=m_even_norm, m_even_w_in=m_even_w_in, m_even_conv=m_even_conv, m_even_a_log=m_even_a_log, m_even_dt_bias=m_even_dt_bias, m_even_sinks=m_even_sinks, m_even_onorm=m_even_onorm, m_even_w_out=m_even_w_out, m_odd_norm=m_odd_norm, m_odd_w_in=m_odd_w_in, m_odd_ln_g=m_odd_ln_g, m_odd_ln_b=m_odd_ln_b, m_odd_w_s=m_odd_w_s, m_odd_b_s=m_odd_b_s, m_odd_w_out=m_odd_w_out, m_ffn_norm=m_ffn_norm, m_ffn_w_gate=m_ffn_w_gate, m_ffn_w_up=m_ffn_w_up, m_ffn_w_down=m_ffn_w_down, m_final_norm=m_final_norm, v_even_norm=v_even_norm, v_even_w_in=v_even_w_in, v_even_conv=v_even_conv, v_even_a_log=v_even_a_log, v_even_dt_bias=v_even_dt_bias, v_even_sinks=v_even_sinks, v_even_onorm=v_even_onorm, v_even_w_out=v_even_w_out, v_odd_norm=v_odd_norm, v_odd_w_in=v_odd_w_in, v_odd_ln_g=v_odd_ln_g, v_odd_ln_b=v_odd_ln_b, v_odd_w_s=v_odd_w_s, v_odd_b_s=v_odd_b_s, v_odd_w_out=v_odd_w_out, v_ffn_norm=v_ffn_norm, v_ffn_w_gate=v_ffn_w_gate, v_ffn_w_up=v_ffn_w_up, v_ffn_w_down=v_ffn_w_down, v_final_norm=v_final_norm)
    weights = {n: given[n] for n in TWIN_WEIGHTS}
    shared = {n: given[n] for n in SHARED_INPUTS}
    per_example = {n: given[n] for n in ['x']}
    grad_fn = _jax.value_and_grad(_loss, argnums=(0, 1))

    def one_microbatch(ex, loss_target):
        ex = dict(ex)
        diff = ex.pop(TWIN_DIFF_INPUT)
        return grad_fn(weights, diff, {**shared, **ex}, loss_target)

    if N_MICROBATCH == 1:
        loss, (grad_w, grad_x) = one_microbatch(per_example, given["loss_target"])
    else:
        def body(carry, xs):
            loss_sum, grad_sum = carry
            l_k, (gw_k, gx_k) = one_microbatch(xs[0], xs[1])
            with _jax.named_scope("update"):
                return (loss_sum + l_k, _jax.tree.map(_jnp.add, grad_sum, gw_k)), gx_k

        init = (_jnp.zeros((), _jnp.float32), _jax.tree.map(_jnp.zeros_like, weights))
        (loss, grad_w), grad_x = _jax.lax.scan(body, init, (per_example, given["loss_target"]))
    with _jax.named_scope("update"):
        delta_w, new_m, new_v = {}, {}, {}
        for n in TWIN_WEIGHTS:
            delta_w[n], new_m[n], new_v[n] = _adamw(weights[n], grad_w[n], given["m_" + n], given["v_" + n])
    return (loss, grad_x, *[grad_w[n] for n in TWIN_WEIGHTS], *[delta_w[n] for n in TWIN_WEIGHTS],
            *[new_m[n] for n in TWIN_WEIGHTS], *[new_v[n] for n in TWIN_WEIGHTS])
```

```python
import functools

import jax
import jax.numpy as jnp
from jax import lax
from jax.experimental import pallas as pl
from jax.experimental.pallas import tpu as pltpu

F32 = jnp.float32
BF16 = jnp.bfloat16
NEG_INF = float("-inf")

D_MODEL = 2048
A_HEADS, A_KV_HEADS, A_HEAD_DIM, WINDOW = 16, 2, 64, 128
B_HEADS, B_HEAD_DIM, CONV_K, DN_CHUNK = 8, 128, 4, 64
C_GROUPS, C_CHUNK = 8, 128
C_GROUP_DIM = D_MODEL // C_GROUPS
D_FF = 5632
EPS = 1e-6
A_Q = A_HEADS * A_HEAD_DIM
A_KV = A_KV_HEADS * A_HEAD_DIM
B_W = B_HEADS * B_HEAD_DIM
EVEN_IN = A_Q + 2 * A_KV + 4 * B_W + 2 * B_HEADS
EVEN_IN_PAD = 5632
COL_KV = A_Q
COL_QKVB = A_Q + 2 * A_KV
COL_Z = COL_QKVB + 3 * B_W
COL_GATE = COL_Z + B_W
N_SHARD = 4

ADAM_LR, ADAM_B1, ADAM_B2, ADAM_EPS, ADAM_WD, ADAM_STEP = 0.001, 0.9, 0.999, 1e-08, 0.01, 10

VMEM_LIMIT_V7X = 56 * 1024 * 1024
MESH_ID = pl.DeviceIdType.MESH


def _params(sem=None):
    return pltpu.CompilerParams(dimension_semantics=sem, vmem_limit_bytes=VMEM_LIMIT_V7X)


def _sigmoid(x):
    return 1.0 / (1.0 + jnp.exp(-x))


def _silu(x):
    return x * _sigmoid(x)


def _dsilu(x):
    s = _sigmoid(x)
    return s * (1.0 + x * (1.0 - s))


def _gelu(x):
    return 0.5 * x * (1.0 + lax.erf(x * 0.7071067811865476))


def _dgelu(x):
    return 0.5 * (1.0 + lax.erf(x * 0.7071067811865476)) + x * jnp.exp(-0.5 * x * x) * 0.3989422804014327


def _dot(a, b, dims, precision=None):
    return lax.dot_general(a, b, (dims, ((), ())), preferred_element_type=F32, precision=precision)


NN = ((1,), (0,))
NT = ((1,), (1,))
TN = ((0,), (0,))


def _as3(b):
    return b if b.ndim == 3 else b[None]


def mm_nn(a, b, *, tm, tn, tk, out_dtype, name, res=None, act=None):
    b3 = _as3(b)
    m, k = a.shape
    s, k2, ns = b3.shape
    assert k2 == k and m % tm == 0 and ns % tn == 0 and k % tk == 0, (a.shape, b3.shape, tm, tn, tk)
    nps, nk = ns // tn, k // tk

    def body(*refs):
        if res is None:
            a_ref, b_ref, o_ref, acc = refs
        else:
            a_ref, b_ref, r_ref, o_ref, acc = refs
        kk = pl.program_id(2)

        @pl.when(kk == 0)
        def _():
            acc[...] = jnp.zeros_like(acc)

        acc[...] += _dot(a_ref[...].astype(BF16), b_ref[...].astype(BF16), NN)

        @pl.when(kk == nk - 1)
        def _():
            r = acc[...]
            if res is not None:
                r = r + r_ref[...].astype(F32)
            o_ref[...] = r.astype(out_dtype)

    in_specs = [pl.BlockSpec((tm, tk), lambda i, j, kk: (i, kk)),
                pl.BlockSpec((None, tk, tn), lambda i, j, kk: (j // nps, kk, j % nps))]
    args = [a, b3]
    if res is not None:
        in_specs.append(pl.BlockSpec((tm, tn), lambda i, j, kk: (i, j)))
        args.append(res)
    return pl.pallas_call(
        body, name=name, grid=(m // tm, s * nps, nk), in_specs=in_specs,
        out_specs=pl.BlockSpec((tm, tn), lambda i, j, kk: (i, j)),
        out_shape=jax.ShapeDtypeStruct((m, s * ns), out_dtype),
        scratch_shapes=[pltpu.VMEM((tm, tn), F32)],
        compiler_params=_params(("parallel", "parallel", "arbitrary")))(*args)


def mm_nt(a, b, *, tm, tn, tk, out_dtype, name, res=None):
    b3 = _as3(b)
    m, n = a.shape
    s, k, ns = b3.shape
    assert n == s * ns and m % tm == 0 and k % tn == 0 and ns % tk == 0, (a.shape, b3.shape, tm, tn, tk)
    rps = ns // tk
    nr = s * rps

    def body(*refs):
        if res is None:
            a_ref, b_ref, o_ref, acc = refs
        else:
            a_ref, b_ref, r_ref, o_ref, acc = refs
        r_id = pl.program_id(2)

        @pl.when(r_id == 0)
        def _():
            acc[...] = jnp.zeros_like(acc)

        acc[...] += _dot(a_ref[...].astype(BF16), b_ref[...].astype(BF16), NT)

        @pl.when(r_id == nr - 1)
        def _():
            r = acc[...]
            if res is not None:
                r = r + r_ref[...].astype(F32)
            o_ref[...] = r.astype(out_dtype)

    in_specs = [pl.BlockSpec((tm, tk), lambda i, j, r: (i, r)),
                pl.BlockSpec((None, tn, tk), lambda i, j, r: (r // rps, j, r % rps))]
    args = [a, b3]
    if res is not None:
        in_specs.append(pl.BlockSpec((tm, tn), lambda i, j, r: (i, j)))
        args.append(res)
    return pl.pallas_call(
        body, name=name, grid=(m // tm, k // tn, nr), in_specs=in_specs,
        out_specs=pl.BlockSpec((tm, tn), lambda i, j, r: (i, j)),
        out_shape=jax.ShapeDtypeStruct((m, k), out_dtype),
        scratch_shapes=[pltpu.VMEM((tm, tn), F32)],
        compiler_params=_params(("parallel", "parallel", "arbitrary")))(*args)


def mm_tn(a, b, *, shards, tm, tn, tk, out_dtype, name):
    m, k = a.shape
    m2, n = b.shape
    ns = n // shards
    assert m2 == m and n == shards * ns and m % tm == 0 and k % tk == 0 and ns % tn == 0, (a.shape, b.shape)
    nps, nm = ns // tn, m // tm

    def body(a_ref, b_ref, o_ref, acc):
        mi = pl.program_id(2)

        @pl.when(mi == 0)
        def _():
            acc[...] = jnp.zeros_like(acc)

        acc[...] += _dot(a_ref[...].astype(BF16), b_ref[...].astype(BF16), TN)

        @pl.when(mi == nm - 1)
        def _():
            o_ref[...] = acc[...].astype(out_dtype)

    return pl.pallas_call(
        body, name=name, grid=(k // tk, shards * nps, nm),
        in_specs=[pl.BlockSpec((tm, tk), lambda i, j, mi: (mi, i)),
                  pl.BlockSpec((tm, tn), lambda i, j, mi: (mi, j))],
        out_specs=pl.BlockSpec((None, tk, tn), lambda i, j, mi: (j // nps, i, j % nps)),
        out_shape=jax.ShapeDtypeStruct((shards, k, ns), out_dtype),
        scratch_shapes=[pltpu.VMEM((tk, tn), F32)],
        compiler_params=_params(("parallel", "parallel", "arbitrary")))(a, b)


def mm_gate_up(hn, wg, wu, *, tm, tn, tk, name):
    wg3, wu3 = _as3(wg), _as3(wu)
    m, k = hn.shape
    s, _, ns = wg3.shape
    assert m % tm == 0 and ns % tn == 0 and k % tk == 0
    nps, nk = ns // tn, k // tk

    def body(a_ref, g_ref, u_ref, og_ref, ou_ref, oa_ref, accg, accu):
        kk = pl.program_id(2)

        @pl.when(kk == 0)
        def _():
            accg[...] = jnp.zeros_like(accg)
            accu[...] = jnp.zeros_like(accu)

        a = a_ref[...].astype(BF16)
        accg[...] += _dot(a, g_ref[...].astype(BF16), NN)
        accu[...] += _dot(a, u_ref[...].astype(BF16), NN)

        @pl.when(kk == nk - 1)
        def _():
            g, u = accg[...], accu[...]
            og_ref[...] = g.astype(BF16)
            ou_ref[...] = u.astype(BF16)
            oa_ref[...] = (_silu(g) * u).astype(BF16)

    wspec = pl.BlockSpec((None, tk, tn), lambda i, j, kk: (j // nps, kk, j % nps))
    ospec = pl.BlockSpec((tm, tn), lambda i, j, kk: (i, j))
    osh = jax.ShapeDtypeStruct((m, s * ns), BF16)
    return pl.pallas_call(
        body, name=name, grid=(m // tm, s * nps, nk),
        in_specs=[pl.BlockSpec((tm, tk), lambda i, j, kk: (i, kk)), wspec, wspec],
        out_specs=[ospec, ospec, ospec], out_shape=[osh, osh, osh],
        scratch_shapes=[pltpu.VMEM((tm, tn), F32), pltpu.VMEM((tm, tn), F32)],
        compiler_params=_params(("parallel", "parallel", "arbitrary")))(hn, wg3, wu3)


def mm_down_bwd(dh, wd, gate, up, *, tm, tn, tk, name):
    m, d = dh.shape
    f, d2 = wd.shape
    assert d2 == d and m % tm == 0 and f % tn == 0 and d % tk == 0
    nr = d // tk

    def body(a_ref, b_ref, g_ref, u_ref, og_ref, ou_ref, acc):
        r_id = pl.program_id(2)

        @pl.when(r_id == 0)
        def _():
            acc[...] = jnp.zeros_like(acc)

        acc[...] += _dot(a_ref[...].astype(BF16), b_ref[...].astype(BF16), NT)

        @pl.when(r_id == nr - 1)
        def _():
            da = acc[...]
            g, u = g_ref[...].astype(F32), u_ref[...].astype(F32)
            og_ref[...] = (da * u * _dsilu(g)).astype(BF16)
            ou_ref[...] = (da * _silu(g)).astype(BF16)

    ospec = pl.BlockSpec((tm, tn), lambda i, j, r: (i, j))
    osh = jax.ShapeDtypeStruct((m, f), BF16)
    return pl.pallas_call(
        body, name=name, grid=(m // tm, f // tn, nr),
        in_specs=[pl.BlockSpec((tm, tk), lambda i, j, r: (i, r)),
                  pl.BlockSpec((tn, tk), lambda i, j, r: (j, r)), ospec, ospec],
        out_specs=[ospec, ospec], out_shape=[osh, osh],
        scratch_shapes=[pltpu.VMEM((tm, tn), F32)],
        compiler_params=_params(("parallel", "parallel", "arbitrary")))(dh, wd, gate, up)


ROWS = 256


def rms_fwd(x, g, *, name):
    t, d = x.shape

    def body(x_ref, g_ref, o_ref):
        xv = x_ref[...]
        r = lax.rsqrt(jnp.mean(xv * xv, axis=-1, keepdims=True) + EPS)
        o_ref[...] = (xv * r * g_ref[...]).astype(BF16)

    return pl.pallas_call(
        body, name=name, grid=(t // ROWS,),
        in_specs=[pl.BlockSpec((ROWS, d), lambda i: (i, 0)), pl.BlockSpec((1, d), lambda i: (0, 0))],
        out_specs=pl.BlockSpec((ROWS, d), lambda i: (i, 0)),
        out_shape=jax.ShapeDtypeStruct((t, d), BF16), compiler_params=_params(("parallel",)))(x, g)


def rms_bwd(x, g, dy, dres, *, name):
    t, d = x.shape

    def body(x_ref, g_ref, dy_ref, dr_ref, dx_ref, dg_ref):
        @pl.when(pl.program_id(0) == 0)
        def _():
            dg_ref[...] = jnp.zeros_like(dg_ref)

        xv, dyv = x_ref[...], dy_ref[...].astype(F32)
        r = lax.rsqrt(jnp.mean(xv * xv, axis=-1, keepdims=True) + EPS)
        dyg = dyv * g_ref[...]
        dx = r * dyg - xv * (r * r * r) * jnp.mean(dyg * xv, axis=-1, keepdims=True)
        dx_ref[...] = dx + dr_ref[...]
        dg_ref[...] += jnp.sum(dyv * xv * r, axis=0, keepdims=True)

    row = pl.BlockSpec((ROWS, d), lambda i: (i, 0))
    vec = pl.BlockSpec((1, d), lambda i: (0, 0))
    return pl.pallas_call(
        body, name=name, grid=(t // ROWS,), in_specs=[row, vec, row, row], out_specs=[row, vec],
        out_shape=[jax.ShapeDtypeStruct((t, d), F32), jax.ShapeDtypeStruct((1, d), F32)],
        compiler_params=_params(("arbitrary",)))(x, g, dy, dres)


def loss_head(h, g, target, *, name):
    t, d = h.shape

    def body(x_ref, g_ref, t_ref, loss_ref, dx_ref, dg_ref):
        @pl.when(pl.program_id(0) == 0)
        def _():
            dg_ref[...] = jnp.zeros_like(dg_ref)
            loss_ref[...] = jnp.zeros_like(loss_ref)

        xv, gv = x_ref[...], g_ref[...]
        r = lax.rsqrt(jnp.mean(xv * xv, axis=-1, keepdims=True) + EPS)
        e = xv * r * gv - t_ref[...]
        loss_ref[...] += 0.5 * jnp.sum(jnp.mean(e * e, axis=-1, keepdims=True), axis=0, keepdims=True)
        dyv = e * (1.0 / d)
        dyg = dyv * gv
        dx_ref[...] = r * dyg - xv * (r * r * r) * jnp.mean(dyg * xv, axis=-1, keepdims=True)
        dg_ref[...] += jnp.sum(dyv * xv * r, axis=0, keepdims=True)

    row = pl.BlockSpec((ROWS, d), lambda i: (i, 0))
    vec = pl.BlockSpec((1, d), lambda i: (0, 0))
    return pl.pallas_call(
        body, name=name, grid=(t // ROWS,), in_specs=[row, vec, row],
        out_specs=[pl.BlockSpec((1, 128), lambda i: (0, 0)), row, vec],
        out_shape=[jax.ShapeDtypeStruct((1, 128), F32), jax.ShapeDtypeStruct((t, d), F32),
                   jax.ShapeDtypeStruct((1, d), F32)],
        compiler_params=_params(("arbitrary",)))(h, g, target)


def _tril_mask():
    r = lax.broadcasted_iota(jnp.int32, (C_CHUNK, C_CHUNK), 0)
    c = lax.broadcasted_iota(jnp.int32, (C_CHUNK, C_CHUNK), 1)
    return r >= c


def _layer_norm_parts(v):
    mu = jnp.mean(v, axis=-1, keepdims=True)
    vc = v - mu
    rstd = lax.rsqrt(jnp.mean(vc * vc, axis=-1, keepdims=True) + EPS)
    return vc * rstd, rstd


def gmlp_fwd(zpre, ln_g, ln_b, ws, bs_t, *, name):
    t = zpre.shape[0]
    d = D_MODEL

    def body(zu_ref, zv_ref, g_ref, b_ref, ws_ref, bs_ref, o_ref):
        u = _gelu(zu_ref[...])
        vhat, _ = _layer_norm_parts(_gelu(zv_ref[...]))
        vln = (vhat * g_ref[...] + b_ref[...]).astype(BF16)
        mask = _tril_mask()
        for gi in range(C_GROUPS):
            sl = slice(gi * C_GROUP_DIM, (gi + 1) * C_GROUP_DIM)
            w = jnp.where(mask, ws_ref[gi], 0.0).astype(BF16)
            mixed = _dot(w, vln[:, sl], NN) + bs_ref[:, gi:gi + 1]
            o_ref[:, sl] = (u[:, sl] * mixed).astype(BF16)

    vec = pl.BlockSpec((1, d), lambda i: (0, 0))
    return pl.pallas_call(
        body, name=name, grid=(t // C_CHUNK,),
        in_specs=[pl.BlockSpec((C_CHUNK, d), lambda i: (i, 0)), pl.BlockSpec((C_CHUNK, d), lambda i: (i, 1)),
                  vec, vec, pl.BlockSpec((C_GROUPS, C_CHUNK, C_CHUNK), lambda i: (0, 0, 0)),
                  pl.BlockSpec((C_CHUNK, 128), lambda i: (0, 0))],
        out_specs=pl.BlockSpec((C_CHUNK, d), lambda i: (i, 0)),
        out_shape=jax.ShapeDtypeStruct((t, d), BF16), compiler_params=_params(("parallel",)))(
            zpre, zpre, ln_g, ln_b, ws, bs_t)


def gmlp_bwd(zpre, dgated, ln_g, ln_b, ws, bs_t, *, name):
    t = zpre.shape[0]
    d = D_MODEL

    def body(zu_ref, zv_ref, dg_ref, g_ref, b_ref, ws_ref, bs_ref, dz_ref, dws_ref, dbs_ref, dlg_ref, dlb_ref):
        @pl.when(pl.program_id(0) == 0)
        def _():
            dws_ref[...] = jnp.zeros_like(dws_ref)
            dbs_ref[...] = jnp.zeros_like(dbs_ref)
            dlg_ref[...] = jnp.zeros_like(dlg_ref)
            dlb_ref[...] = jnp.zeros_like(dlb_ref)

        zu, zv = zu_ref[...], zv_ref[...]
        u = _gelu(zu)
        vhat, rstd = _layer_norm_parts(_gelu(zv))
        gam = g_ref[...]
        vln = (vhat * gam + b_ref[...]).astype(BF16)
        dgt = dg_ref[...].astype(F32)
        mask = _tril_mask()
        lane = lax.broadcasted_iota(jnp.int32, (C_CHUNK, 128), 1)
        dbs = jnp.zeros((C_CHUNK, 128), F32)
        du_parts, dvln_parts = [], []
        for gi in range(C_GROUPS):
            sl = slice(gi * C_GROUP_DIM, (gi + 1) * C_GROUP_DIM)
            w = jnp.where(mask, ws_ref[gi], 0.0).astype(BF16)
            mixed = _dot(w, vln[:, sl], NN) + bs_ref[:, gi:gi + 1]
            du_parts.append(dgt[:, sl] * mixed)
            dmixed = dgt[:, sl] * u[:, sl]
            dmb = dmixed.astype(BF16)
            dws_ref[gi] += jnp.where(mask, _dot(dmb, vln[:, sl], NT), 0.0)
            dbs = dbs + jnp.where(lane == gi, jnp.sum(dmixed, axis=-1, keepdims=True), 0.0)
            dvln_parts.append(_dot(w, dmb, TN))
        dbs_ref[...] += dbs
        du = jnp.concatenate(du_parts, axis=-1)
        dvln = jnp.concatenate(dvln_parts, axis=-1)
        dlg_ref[...] += jnp.sum(dvln * vhat, axis=0, keepdims=True)
        dlb_ref[...] += jnp.sum(dvln, axis=0, keepdims=True)
        dvhat = dvln * gam
        dv = rstd * (dvhat - jnp.mean(dvhat, axis=-1, keepdims=True)
                     - vhat * jnp.mean(dvhat * vhat, axis=-1, keepdims=True))
        dz_ref[:, :d] = (du * _dgelu(zu)).astype(BF16)
        dz_ref[:, d:] = (dv * _dgelu(zv)).astype(BF16)

    vec = pl.BlockSpec((1, d), lambda i: (0, 0))
    wsp = pl.BlockSpec((C_GROUPS, C_CHUNK, C_CHUNK), lambda i: (0, 0, 0))
    bsp = pl.BlockSpec((C_CHUNK, 128), lambda i: (0, 0))
    return pl.pallas_call(
        body, name=name, grid=(t // C_CHUNK,),
        in_specs=[pl.BlockSpec((C_CHUNK, d), lambda i: (i, 0)), pl.BlockSpec((C_CHUNK, d), lambda i: (i, 1)),
                  pl.BlockSpec((C_CHUNK, d), lambda i: (i, 0)), vec, vec, wsp, bsp],
        out_specs=[pl.BlockSpec((C_CHUNK, 2 * d), lambda i: (i, 0)), wsp, bsp, vec, vec],
        out_shape=[jax.ShapeDtypeStruct((t, 2 * d), BF16), jax.ShapeDtypeStruct((C_GROUPS, C_CHUNK, C_CHUNK), F32),
                   jax.ShapeDtypeStruct((C_CHUNK, 128), F32), jax.ShapeDtypeStruct((1, d), F32),
                   jax.ShapeDtypeStruct((1, d), F32)],
        compiler_params=_params(("arbitrary",)))(zpre, zpre, dgated, ln_g, ln_b, ws, bs_t)


ATT_SCALE = A_HEAD_DIM ** -0.5
PAIRS = A_HEADS // 2
PAIRS_PER_KV = PAIRS // A_KV_HEADS


def _att_padded(tile):
    lo = lax.broadcasted_iota(jnp.int32, tile.shape, 1) < A_HEAD_DIM
    rolled = pltpu.roll(tile, A_HEAD_DIM, 1)
    zero = jnp.zeros_like(tile)
    return {(0, 0): jnp.where(lo, tile, zero).astype(BF16), (0, 1): jnp.where(lo, zero, rolled).astype(BF16),
            (1, 0): jnp.where(lo, rolled, zero).astype(BF16), (1, 1): jnp.where(lo, zero, tile).astype(BF16)}


def _att_valid(n):
    r = lax.broadcasted_iota(jnp.int32, (WINDOW, 2 * WINDOW), 0)
    c = lax.broadcasted_iota(jnp.int32, (WINDOW, 2 * WINDOW), 1)
    rel = r + WINDOW - c
    return (rel >= 0) & (rel < WINDOW) & ((c >= WINDOW) | (n > 0))


def _att_probs(qp, kpad, sink, valid):
    s = jnp.where(valid, _dot(qp, kpad, NT), NEG_INF)
    m = jnp.maximum(jnp.max(s, axis=-1, keepdims=True), sink)
    p = jnp.exp(s - m)
    e_sink = jnp.exp(sink - m)
    inv = 1.0 / (jnp.sum(p, axis=-1, keepdims=True) + e_sink)
    return p * inv, e_sink * inv


def _att_specs(t):
    return [pl.BlockSpec((WINDOW, A_Q), lambda n: (n, 0)),
            pl.BlockSpec((WINDOW, 2 * A_KV), lambda n: (n, COL_KV // (2 * A_KV))),
            pl.BlockSpec((WINDOW, 2 * A_KV), lambda n: (jnp.maximum(n - 1, 0), COL_KV // (2 * A_KV))),
            pl.BlockSpec((1, 128), lambda n: (0, 0))]


def att_fwd(proj, sinks, *, name):
    t = proj.shape[0]

    def body(q_ref, kvc_ref, kvp_ref, s_ref, o_ref):
        n = pl.program_id(0)
        kv = jnp.concatenate([kvp_ref[...], kvc_ref[...]], axis=0)
        kpad, vpad = _att_padded(kv[:, :128]), _att_padded(kv[:, 128:])
        valid = _att_valid(n)
        for j in range(PAIRS):
            qp = (q_ref[:, j * 128:(j + 1) * 128] * ATT_SCALE).astype(BF16)
            acc = jnp.zeros((WINDOW, 128), F32)
            for half in range(2):
                key = (j // PAIRS_PER_KV, half)
                h = 2 * j + half
                w, _ = _att_probs(qp, kpad[key], s_ref[:, h:h + 1], valid)
                acc = acc + _dot(w.astype(BF16), vpad[key], NN)
            o_ref[:, j * 128:(j + 1) * 128] = acc.astype(BF16)

    return pl.pallas_call(
        body, name=name, grid=(t // WINDOW,), in_specs=_att_specs(t),
        out_specs=pl.BlockSpec((WINDOW, A_Q), lambda n: (n, 0)),
        out_shape=jax.ShapeDtypeStruct((t, A_Q), BF16), compiler_params=_params(("parallel",)))(
            proj, proj, proj, sinks)


def att_bwd(proj, sinks, dout, *, name):
    t = proj.shape[0]

    def body(q_ref, kvc_ref, kvp_ref, s_ref, do_ref, dq_ref, dkc_ref, dkp_ref, ds_ref):
        n = pl.program_id(0)

        @pl.when(n == 0)
        def _():
            ds_ref[...] = jnp.zeros_like(ds_ref)

        kv = jnp.concatenate([kvp_ref[...], kvc_ref[...]], axis=0)
        kpad, vpad = _att_padded(kv[:, :128]), _att_padded(kv[:, 128:])
        valid = _att_valid(n)
        lane = lax.broadcasted_iota(jnp.int32, (1, 128), 1)
        dsink = jnp.zeros((1, 128), F32)
        zero = jnp.zeros((2 * WINDOW, 128), F32)
        dk_acc = {key: zero for key in kpad}
        dv_acc = {key: zero for key in kpad}
        for j in range(PAIRS):
            qp = (q_ref[:, j * 128:(j + 1) * 128] * ATT_SCALE).astype(BF16)
            dop = do_ref[:, j * 128:(j + 1) * 128].astype(BF16)
            dq = jnp.zeros((WINDOW, 128), F32)
            for half in range(2):
                key = (j // PAIRS_PER_KV, half)
                h = 2 * j + half
                w, w_sink = _att_probs(qp, kpad[key], s_ref[:, h:h + 1], valid)
                dw = _dot(dop, vpad[key], NT)
                delta = jnp.sum(w * dw, axis=-1, keepdims=True)
                dsc = (w * (dw - delta)).astype(BF16)
                dsink = dsink + jnp.where(lane == h, -jnp.sum(w_sink * delta, axis=0, keepdims=True), 0.0)
                dq = dq + _dot(dsc, kpad[key], NN)
                dk_acc[key] = dk_acc[key] + _dot(dsc, qp, TN)
                dv_acc[key] = dv_acc[key] + _dot(w.astype(BF16), dop, TN)
            dq_ref[:, j * 128:(j + 1) * 128] = (dq * ATT_SCALE).astype(BF16)
        ds_ref[...] += dsink
        lo = lax.broadcasted_iota(jnp.int32, (2 * WINDOW, 128), 1) < A_HEAD_DIM

        def tile(acc):
            return jnp.where(lo, acc[(0, 0)] + pltpu.roll(acc[(0, 1)], A_HEAD_DIM, 1),
                             pltpu.roll(acc[(1, 0)], A_HEAD_DIM, 1) + acc[(1, 1)])

        dkv = jnp.concatenate([tile(dk_acc), tile(dv_acc)], axis=1)
        dkp_ref[...] = dkv[:WINDOW]
        dkc_ref[...] = dkv[WINDOW:]

    kvo = pl.BlockSpec((WINDOW, 2 * A_KV), lambda n: (n, 0))
    return pl.pallas_call(
        body, name=name, grid=(t // WINDOW,),
        in_specs=_att_specs(t) + [pl.BlockSpec((WINDOW, A_Q), lambda n: (n, 0))],
        out_specs=[pl.BlockSpec((WINDOW, A_Q), lambda n: (n, 0)), kvo, kvo, pl.BlockSpec((1, 128), lambda n: (0, 0))],
        out_shape=[jax.ShapeDtypeStruct((t, A_Q), BF16), jax.ShapeDtypeStruct((t, 2 * A_KV), F32),
                   jax.ShapeDtypeStruct((t, 2 * A_KV), F32), jax.ShapeDtypeStruct((1, 128), F32)],
        compiler_params=_params(("arbitrary",)))(proj, proj, proj, sinks, dout)


QK_SCALE = B_HEAD_DIM ** -0.5
PREP_COLS = 256
PREP_NCB = 3 * B_W // PREP_COLS
HALO = 8


def _roll_rows(x, shift):
    n = x.shape[0]
    return x if shift % n == 0 else pltpu.roll(x, shift % n, 0)


def _conv_taps(xe, w):
    xs = [_roll_rows(xe, CONV_K - 1 - i) for i in range(CONV_K)]
    c = w[0:1] * xs[0]
    for i in range(1, CONV_K):
        c = c + w[i:i + 1] * xs[i]
    return xs, c


def dprep_fwd(proj, conv_w, *, name):
    t = proj.shape[0]
    tt = ROWS
    col0 = COL_QKVB // PREP_COLS

    def body(x_ref, h_ref, w_ref, o_ref):
        cb, n = pl.program_id(0), pl.program_id(1)
        halo = jnp.where(n > 0, h_ref[...], 0.0)
        xe = jnp.concatenate([halo, x_ref[...]], axis=0)
        _, c = _conv_taps(xe, w_ref[...])
        y = _silu(c)[HALO:]
        parts = []
        for hh in range(PREP_COLS // B_HEAD_DIM):
            yh = y[:, hh * B_HEAD_DIM:(hh + 1) * B_HEAD_DIM]
            parts.append(yh * lax.rsqrt(jnp.sum(yh * yh, axis=-1, keepdims=True) + EPS))
        nrm = jnp.concatenate(parts, axis=-1)
        o_ref[...] = jnp.where(cb < 4, nrm * QK_SCALE, jnp.where(cb < 8, nrm, y))

    return pl.pallas_call(
        body, name=name, grid=(PREP_NCB, t // tt),
        in_specs=[pl.BlockSpec((tt, PREP_COLS), lambda cb, n: (n, col0 + cb)),
                  pl.BlockSpec((HALO, PREP_COLS), lambda cb, n: (jnp.maximum(n * (tt // HALO) - 1, 0), col0 + cb)),
                  pl.BlockSpec((CONV_K, PREP_COLS), lambda cb, n: (0, cb))],
        out_specs=pl.BlockSpec((tt, PREP_COLS), lambda cb, n: (n, cb)),
        out_shape=jax.ShapeDtypeStruct((t, 3 * B_W), F32), compiler_params=_params(("parallel", "parallel")))(
            proj, proj, conv_w)


def dprep_bwd(proj, conv_w, dqkvn, *, name):
    t = proj.shape[0]
    tt = ROWS
    nb = t // tt
    col0 = COL_QKVB // PREP_COLS
    n8 = t // HALO

    def body(xc_ref, xb_ref, xa_ref, dc_ref, da_ref, w_ref, dx_ref, dw_ref):
        cb, n = pl.program_id(0), pl.program_id(1)

        @pl.when(n == 0)
        def _():
            dw_ref[...] = jnp.zeros_like(dw_ref)

        w = w_ref[...]
        xe = jnp.concatenate([jnp.where(n > 0, xb_ref[...], 0.0), xc_ref[...], xa_ref[...]], axis=0)
        xs, c = _conv_taps(xe, w)
        sg = _sigmoid(c)
        y = c * sg
        dout = jnp.concatenate([jnp.zeros((HALO, PREP_COLS), F32), dc_ref[...],
                                jnp.where(n < nb - 1, da_ref[...], 0.0)], axis=0)
        dsc = jnp.where(cb < 4, QK_SCALE, 1.0)
        parts = []
        for hh in range(PREP_COLS // B_HEAD_DIM):
            sl = slice(hh * B_HEAD_DIM, (hh + 1) * B_HEAD_DIM)
            yh, doh = y[:, sl], dout[:, sl] * dsc
            r = lax.rsqrt(jnp.sum(yh * yh, axis=-1, keepdims=True) + EPS)
            parts.append(doh * r - yh * (r * r * r) * jnp.sum(doh * yh, axis=-1, keepdims=True))
        dy = jnp.where(cb < 8, jnp.concatenate(parts, axis=-1), dout)
        dcv = dy * sg * (1.0 + c * (1.0 - sg))
        dxe = w[CONV_K - 1:CONV_K] * dcv
        for i in range(CONV_K - 1):
            dxe = dxe + w[i:i + 1] * _roll_rows(dcv, -(CONV_K - 1 - i))
        dx_ref[...] = dxe[HALO:HALO + tt].astype(BF16)
        for i in range(CONV_K):
            dw_ref[i:i + 1, :] += jnp.sum((dcv * xs[i])[HALO:HALO + tt], axis=0, keepdims=True)

    def after(n):
        return jnp.minimum((n + 1) * (tt // HALO), n8 - 1)

    return pl.pallas_call(
        body, name=name, grid=(PREP_NCB, nb),
        in_specs=[pl.BlockSpec((tt, PREP_COLS), lambda cb, n: (n, col0 + cb)),
                  pl.BlockSpec((HALO, PREP_COLS), lambda cb, n: (jnp.maximum(n * (tt // HALO) - 1, 0), col0 + cb)),
                  pl.BlockSpec((HALO, PREP_COLS), lambda cb, n: (after(n), col0 + cb)),
                  pl.BlockSpec((tt, PREP_COLS), lambda cb, n: (n, cb)),
                  pl.BlockSpec((HALO, PREP_COLS), lambda cb, n: (after(n), cb)),
                  pl.BlockSpec((CONV_K, PREP_COLS), lambda cb, n: (0, cb))],
        out_specs=[pl.BlockSpec((tt, PREP_COLS), lambda cb, n: (n, cb)),
                   pl.BlockSpec((CONV_K, PREP_COLS), lambda cb, n: (0, cb))],
        out_shape=[jax.ShapeDtypeStruct((t, 3 * B_W), BF16), jax.ShapeDtypeStruct((CONV_K, 3 * B_W), F32)],
        compiler_params=_params(("parallel", "arbitrary")))(proj, proj, proj, dqkvn, dqkvn, conv_w)


def _softplus(z):
    return jnp.maximum(z, 0.0) + jnp.log(1.0 + jnp.exp(-jnp.abs(z)))


def gates_fwd(proj, alog_pad, dtb_pad, *, name):
    t = proj.shape[0]

    def body(x_ref, a_ref, b_ref, o_ref):
        raw = x_ref[...]
        lane = lax.broadcasted_iota(jnp.int32, raw.shape, 1)
        g = -jnp.exp(a_ref[...]) * _softplus(raw + b_ref[...])
        o_ref[...] = jnp.where(lane < B_HEADS, _sigmoid(raw), jnp.where(lane < 2 * B_HEADS, g, 0.0))

    vec = pl.BlockSpec((1, 128), lambda n: (0, 0))
    return pl.pallas_call(
        body, name=name, grid=(t // ROWS,),
        in_specs=[pl.BlockSpec((ROWS, 128), lambda n: (n, COL_GATE // 128)), vec, vec],
        out_specs=pl.BlockSpec((ROWS, 128), lambda n: (n, 0)),
        out_shape=jax.ShapeDtypeStruct((t, 128), F32), compiler_params=_params(("parallel",)))(
            proj, alog_pad, dtb_pad)


def gates_bwd(proj, alog_pad, dtb_pad, dgates, *, name):
    t = proj.shape[0]

    def body(x_ref, a_ref, b_ref, dg_ref, dx_ref, da_ref, db_ref):
        @pl.when(pl.program_id(0) == 0)
        def _():
            da_ref[...] = jnp.zeros_like(da_ref)
            db_ref[...] = jnp.zeros_like(db_ref)

        raw, dgt = x_ref[...], dg_ref[...]
        lane = lax.broadcasted_iota(jnp.int32, raw.shape, 1)
        is_beta, is_g = lane < B_HEADS, (lane >= B_HEADS) & (lane < 2 * B_HEADS)
        beta = _sigmoid(raw)
        z = raw + b_ref[...]
        neg_a = -jnp.exp(a_ref[...])
        d_z = jnp.where(is_g, dgt * neg_a * _sigmoid(z), 0.0)
        dx_ref[...] = jnp.where(is_beta, dgt * beta * (1.0 - beta), d_z).astype(BF16)
        db_ref[...] += jnp.sum(d_z, axis=0, keepdims=True)
        da_ref[...] += jnp.sum(jnp.where(is_g, dgt * neg_a * _softplus(z), 0.0), axis=0, keepdims=True)

    vec = pl.BlockSpec((1, 128), lambda n: (0, 0))
    row = pl.BlockSpec((ROWS, 128), lambda n: (n, 0))
    return pl.pallas_call(
        body, name=name, grid=(t // ROWS,),
        in_specs=[pl.BlockSpec((ROWS, 128), lambda n: (n, COL_GATE // 128)), vec, vec, row],
        out_specs=[row, vec, vec],
        out_shape=[jax.ShapeDtypeStruct((t, 128), BF16), jax.ShapeDtypeStruct((1, 128), F32),
                   jax.ShapeDtypeStruct((1, 128), F32)],
        compiler_params=_params(("arbitrary",)))(proj, alog_pad, dtb_pad, dgates)


HI = lax.Precision.HIGHEST


def _delta_chunk(s0, q, k, v, beta, g):
    c = DN_CHUNK
    r = lax.broadcasted_iota(jnp.int32, (c, c), 0)
    cc = lax.broadcasted_iota(jnp.int32, (c, c), 1)
    incl, strict = r >= cc, r > cc
    ones_lower = incl.astype(F32)
    ones_upper = (r <= cc).astype(F32)
    eye = (r == cc).astype(F32)
    gb = jnp.broadcast_to(g, (c, c))
    gam_i = _dot(ones_lower, gb, NN, HI)
    gam_j = _dot(gb, ones_upper, TN, HI)
    decay = jnp.exp(jnp.where(incl, gam_i - gam_j, NEG_INF))
    gam = gam_i[:, 0:1]
    g_last = gam[c - 1:c, :]
    e_gam, e_rest, e_last = jnp.exp(gam), jnp.exp(g_last - gam), jnp.exp(g_last)
    qb, kb = q.astype(BF16), k.astype(BF16)
    a_neg = -jnp.where(strict, beta * _dot(kb, kb, NT) * decay, 0.0)
    inv = eye + a_neg
    pw = a_neg
    for _ in range(5):
        pw = _dot(pw, pw, NN, HI)
        inv = inv + _dot(inv, pw, NN, HI)
    u = _dot(inv, v * beta, NN, HI)
    w = _dot(inv, k * (beta * e_gam), NN, HI)
    qk = _dot(qb, kb, NT) * decay
    sb = s0.astype(BF16)
    v_new = u - _dot(w.astype(BF16), sb, NN)
    vb = v_new.astype(BF16)
    o = _dot((q * e_gam).astype(BF16), sb, NN) + _dot(qk.astype(BF16), vb, NN)
    s1 = s0 * e_last + _dot((k * e_rest).astype(BF16), vb, TN)
    return s1, o


def delta_fwd(qkvn, gates, *, name):
    t = qkvn.shape[0]
    nc = t // DN_CHUNK

    def body(q_ref, k_ref, v_ref, g_ref, o_ref, ss_ref, state):
        @pl.when(pl.program_id(0) == 0)
        def _():
            state[...] = jnp.zeros_like(state)

        gt = g_ref[...]
        for h in range(B_HEADS):
            sl = slice(h * B_HEAD_DIM, (h + 1) * B_HEAD_DIM)
            s0 = state[h]
            ss_ref[h] = s0
            s1, o = _delta_chunk(s0, q_ref[:, sl], k_ref[:, sl], v_ref[:, sl],
                                 gt[:, h:h + 1], gt[:, B_HEADS + h:B_HEADS + h + 1])
            state[h] = s1
            o_ref[:, sl] = o

    blk = lambda j: pl.BlockSpec((DN_CHUNK, B_W), lambda n: (n, j))
    return pl.pallas_call(
        body, name=name, grid=(nc,),
        in_specs=[blk(0), blk(1), blk(2), pl.BlockSpec((DN_CHUNK, 128), lambda n: (n, 0))],
        out_specs=[blk(0), pl.BlockSpec((None, B_HEADS, B_HEAD_DIM, B_HEAD_DIM), lambda n: (n, 0, 0, 0))],
        out_shape=[jax.ShapeDtypeStruct((t, B_W), F32),
                   jax.ShapeDtypeStruct((nc, B_HEADS, B_HEAD_DIM, B_HEAD_DIM), F32)],
        scratch_shapes=[pltpu.VMEM((B_HEADS, B_HEAD_DIM, B_HEAD_DIM), F32)],
        compiler_params=_params(("arbitrary",)))(qkvn, qkvn, qkvn, gates)


def delta_bwd(qkvn, gates, ssave, do, *, name):
    t = qkvn.shape[0]
    nc = t // DN_CHUNK

    def body(q_ref, k_ref, v_ref, g_ref, ss_ref, do_ref, dx_ref, dg_ref, dstate):
        @pl.when(pl.program_id(0) == 0)
        def _():
            dstate[...] = jnp.zeros_like(dstate)

        gt = g_ref[...]
        lane = lax.broadcasted_iota(jnp.int32, (DN_CHUNK, 128), 1)
        dgt = jnp.zeros((DN_CHUNK, 128), F32)
        for h in range(B_HEADS):
            sl = slice(h * B_HEAD_DIM, (h + 1) * B_HEAD_DIM)
            _, vjp = jax.vjp(_delta_chunk, ss_ref[h], q_ref[:, sl], k_ref[:, sl], v_ref[:, sl],
                             gt[:, h:h + 1], gt[:, B_HEADS + h:B_HEADS + h + 1])
            ds0, dq, dk, dv, dbeta, dg = vjp((dstate[h], do_ref[:, sl]))
            dstate[h] = ds0
            dx_ref[:, sl] = dq
            dx_ref[:, B_W + h * B_HEAD_DIM:B_W + (h + 1) * B_HEAD_DIM] = dk
            dx_ref[:, 2 * B_W + h * B_HEAD_DIM:2 * B_W + (h + 1) * B_HEAD_DIM] = dv
            dgt = dgt + jnp.where(lane == h, dbeta, 0.0) + jnp.where(lane == B_HEADS + h, dg, 0.0)
        dg_ref[...] = dgt

    blk = lambda j: pl.BlockSpec((DN_CHUNK, B_W), lambda n: (nc - 1 - n, j))
    gsp = pl.BlockSpec((DN_CHUNK, 128), lambda n: (nc - 1 - n, 0))
    return pl.pallas_call(
        body, name=name, grid=(nc,),
        in_specs=[blk(0), blk(1), blk(2), gsp,
                  pl.BlockSpec((None, B_HEADS, B_HEAD_DIM, B_HEAD_DIM), lambda n: (nc - 1 - n, 0, 0, 0)), blk(0)],
        out_specs=[pl.BlockSpec((DN_CHUNK, 3 * B_W), lambda n: (nc - 1 - n, 0)), gsp],
        out_shape=[jax.ShapeDtypeStruct((t, 3 * B_W), F32), jax.ShapeDtypeStruct((t, 128), F32)],
        scratch_shapes=[pltpu.VMEM((B_HEADS, B_HEAD_DIM, B_HEAD_DIM), F32)],
        compiler_params=_params(("arbitrary",)))(qkvn, qkvn, qkvn, gates, ssave, do)


def gnorm_fwd(o, proj, onorm, *, name):
    t = o.shape[0]

    def body(o_ref, z_ref, w_ref, out_ref):
        ov = o_ref[...]
        r = lax.rsqrt(jnp.mean(ov * ov, axis=-1, keepdims=True) + EPS)
        out_ref[...] = (ov * r * w_ref[...] * _silu(z_ref[...])).astype(BF16)

    blk = pl.BlockSpec((ROWS, B_HEAD_DIM), lambda n, h: (n, h))
    return pl.pallas_call(
        body, name=name, grid=(t // ROWS, B_HEADS),
        in_specs=[blk, pl.BlockSpec((ROWS, B_HEAD_DIM), lambda n, h: (n, COL_Z // B_HEAD_DIM + h)),
                  pl.BlockSpec((1, B_HEAD_DIM), lambda n, h: (0, 0))],
        out_specs=blk, out_shape=jax.ShapeDtypeStruct((t, B_W), BF16),
        compiler_params=_params(("parallel", "parallel")))(o, proj, onorm)


def gnorm_bwd(o, proj, onorm, dout, *, dcol0, name):
    t = o.shape[0]

    def body(o_ref, z_ref, w_ref, d_ref, do_ref, dz_ref, dw_ref):
        @pl.when((pl.program_id(0) == 0) & (pl.program_id(1) == 0))
        def _():
            dw_ref[...] = jnp.zeros_like(dw_ref)

        ov, zv, wv, dv = o_ref[...], z_ref[...], w_ref[...], d_ref[...].astype(F32)
        r = lax.rsqrt(jnp.mean(ov * ov, axis=-1, keepdims=True) + EPS)
        nrm = ov * r
        dz_ref[...] = (dv * nrm * wv * _dsilu(zv)).astype(BF16)
        da = dv * _silu(zv)
        dw_ref[...] += jnp.sum(da * nrm, axis=0, keepdims=True)
        dn = da * wv
        do_ref[...] = r * dn - ov * (r * r * r) * jnp.mean(dn * ov, axis=-1, keepdims=True)

    blk = pl.BlockSpec((ROWS, B_HEAD_DIM), lambda n, h: (n, h))
    vec = pl.BlockSpec((1, B_HEAD_DIM), lambda n, h: (0, 0))
    return pl.pallas_call(
        body, name=name, grid=(t // ROWS, B_HEADS),
        in_specs=[blk, pl.BlockSpec((ROWS, B_HEAD_DIM), lambda n, h: (n, COL_Z // B_HEAD_DIM + h)), vec,
                  pl.BlockSpec((ROWS, B_HEAD_DIM), lambda n, h: (n, dcol0 // B_HEAD_DIM + h))],
        out_specs=[blk, blk, vec],
        out_shape=[jax.ShapeDtypeStruct((t, B_W), F32), jax.ShapeDtypeStruct((t, B_W), BF16),
                   jax.ShapeDtypeStruct((1, B_HEAD_DIM), F32)],
        compiler_params=_params(("arbitrary", "arbitrary")))(o, proj, onorm, dout)


def _ffn_fwd(h, norm_g, wg, wu, wd, tm, tag):
    hn = rms_fwd(h, norm_g, name=f"ffn{tag}_norm")
    gate, up, act = mm_gate_up(hn, wg, wu, tm=tm, tn=1408, tk=512, name=f"ffn{tag}_gate_up")
    h_out = mm_nn(act, wd, tm=tm, tn=512, tk=1408, out_dtype=F32, res=h, name=f"ffn{tag}_down")
    return h_out, (hn, gate, up, act)


def _ffn_bwd(dh, h, norm_g, wg, wu, wd, saved, tm, tag):
    hn, gate, up, act = saved
    dwd = mm_tn(act, dh, shards=1, tm=tm, tn=512, tk=1408, out_dtype=BF16, name=f"ffn{tag}_dwd")[0]
    dgate, dup = mm_down_bwd(dh, wd, gate, up, tm=tm, tn=512, tk=2048, name=f"ffn{tag}_dact")
    dwg = mm_tn(hn, dgate, shards=N_SHARD, tm=tm, tn=1408, tk=1024, out_dtype=BF16, name=f"ffn{tag}_dwg")
    dwu = mm_tn(hn, dup, shards=N_SHARD, tm=tm, tn=1408, tk=1024, out_dtype=BF16, name=f"ffn{tag}_dwu")
    dhn = mm_nt(dgate, wg, tm=tm, tn=512, tk=1408, out_dtype=F32, name=f"ffn{tag}_dhn_g")
    dhn = mm_nt(dup, wu, tm=tm, tn=512, tk=1408, out_dtype=F32, res=dhn, name=f"ffn{tag}_dhn_u")
    dh_in, dnorm = rms_bwd(h, norm_g, dhn, dh, name=f"ffn{tag}_dnorm")
    return dh_in, dnorm, dwg, dwu, dwd


def _local_step(x, target, w):
    t = x.shape[0]
    tm = min(1024, t)
    g = {}

    hn0 = rms_fwd(x, w["even_norm"], name="l0_norm")
    proj = mm_nn(hn0, w["even_w_in"], tm=tm, tn=512, tk=2048, out_dtype=F32, name="l0_w_in")
    out_a = att_fwd(proj, w["sinks"], name="l0_att")
    qkvn = dprep_fwd(proj, w["even_conv"], name="l0_prep")
    gates = gates_fwd(proj, w["a_log"], w["dt_bias"], name="l0_gates")
    o_delta, ssave = delta_fwd(qkvn, gates, name="l0_delta")
    out_b = gnorm_fwd(o_delta, proj, w["onorm"], name="l0_gnorm")
    mix0 = jnp.concatenate([out_a, out_b], axis=-1)
    h1 = mm_nn(mix0, w["even_w_out"], tm=tm, tn=512, tk=2048, out_dtype=F32, res=x, name="l0_w_out")
    h2, ffn0 = _ffn_fwd(h1, w["ffn_norm"][0:1], w["ffn_w_gate"][0], w["ffn_w_up"][0], w["ffn_w_down"][0], tm, 0)
    hn2 = rms_fwd(h2, w["odd_norm"], name="l1_norm")
    zpre = mm_nn(hn2, w["odd_w_in"], tm=tm, tn=1024, tk=2048, out_dtype=F32, name="l1_w_in")
    gated = gmlp_fwd(zpre, w["odd_ln_g"], w["odd_ln_b"], w["odd_w_s"], w["odd_b_s"], name="l1_gmlp")
    h3 = mm_nn(gated, w["odd_w_out"], tm=tm, tn=512, tk=2048, out_dtype=F32, res=h2, name="l1_w_out")
    h4, ffn1 = _ffn_fwd(h3, w["ffn_norm"][1:2], w["ffn_w_gate"][1], w["ffn_w_up"][1], w["ffn_w_down"][1], tm, 1)
    loss, dh4, g["final_norm"] = loss_head(h4, w["final_norm"], target, name="loss_head")

    dh3, dn1, dwg1, dwu1, dwd1 = _ffn_bwd(dh4, h3, w["ffn_norm"][1:2], w["ffn_w_gate"][1], w["ffn_w_up"][1],
                                          w["ffn_w_down"][1], ffn1, tm, 1)
    g["odd_w_out"] = mm_tn(gated, dh3, shards=1, tm=tm, tn=512, tk=1024, out_dtype=BF16, name="l1_dw_out")[0]
    dgated = mm_nt(dh3, w["odd_w_out"], tm=tm, tn=512, tk=2048, out_dtype=BF16, name="l1_dgated")
    dzpre, g["odd_w_s"], g["odd_b_s"], g["odd_ln_g"], g["odd_ln_b"] = gmlp_bwd(
        zpre, dgated, w["odd_ln_g"], w["odd_ln_b"], w["odd_w_s"], w["odd_b_s"], name="l1_dgmlp")
    g["odd_w_in"] = mm_tn(hn2, dzpre, shards=N_SHARD, tm=tm, tn=1024, tk=1024, out_dtype=BF16, name="l1_dw_in")
    dhn2 = mm_nt(dzpre, w["odd_w_in"], tm=tm, tn=512, tk=1024, out_dtype=F32, name="l1_dhn")
    dh2, g["odd_norm"] = rms_bwd(h2, w["odd_norm"], dhn2, dh3, name="l1_dnorm")
    dh1, dn0, dwg0, dwu0, dwd0 = _ffn_bwd(dh2, h1, w["ffn_norm"][0:1], w["ffn_w_gate"][0], w["ffn_w_up"][0],
                                          w["ffn_w_down"][0], ffn0, tm, 0)
    g["ffn_norm"] = jnp.concatenate([dn0, dn1], axis=0)
    g["ffn_w_gate"], g["ffn_w_up"], g["ffn_w_down"] = [dwg0, dwg1], [dwu0, dwu1], [dwd0, dwd1]
    g["even_w_out"] = mm_tn(mix0, dh1, shards=1, tm=tm, tn=512, tk=1024, out_dtype=BF16, name="l0_dw_out")[0]
    dmix = mm_nt(dh1, w["even_w_out"], tm=tm, tn=512, tk=2048, out_dtype=F32, name="l0_dmix")
    dq_a, dkv_cur, dkv_prev, g["sinks"] = att_bwd(proj, w["sinks"], dmix, name="l0_datt")
    dkv = dkv_cur + jnp.concatenate([dkv_prev[WINDOW:], jnp.zeros((WINDOW, 2 * A_KV), F32)], axis=0)
    do_delta, dz, g["onorm"] = gnorm_bwd(o_delta, proj, w["onorm"], dmix, dcol0=A_Q, name="l0_dgnorm")
    dqkvn, dgates = delta_bwd(qkvn, gates, ssave, do_delta, name="l0_ddelta")
    dqkv_b, g["even_conv"] = dprep_bwd(proj, w["even_conv"], dqkvn, name="l0_dprep")
    draw, g["a_log"], g["dt_bias"] = gates_bwd(proj, w["a_log"], w["dt_bias"], dgates, name="l0_dgates")
    dproj = jnp.concatenate([dq_a, dkv.astype(BF16), dqkv_b, dz, draw,
                             jnp.zeros((t, EVEN_IN_PAD - COL_GATE - 128), BF16)], axis=-1)
    g["even_w_in"] = mm_tn(hn0, dproj, shards=1, tm=tm, tn=512, tk=1024, out_dtype=BF16, name="l0_dw_in")[0]
    dhn0 = mm_nt(dproj, w["even_w_in"], tm=tm, tn=512, tk=2816, out_dtype=F32, name="l0_dhn")
    grad_x, g["even_norm"] = rms_bwd(x, w["even_norm"], dhn0, dh1, name="l0_dnorm")
    return loss, grad_x, g


ANY = pl.BlockSpec(memory_space=pl.ANY)
N_DEV = 8


def _place():
    return lax.axis_index("x"), lax.axis_index("y"), lax.axis_index("c")


def _chip_peers(x, y, c):
    return [((1 - x, y, c), 2 * (1 - x) + y), ((x, 1 - y, c), 2 * x + 1 - y), ((1 - x, 1 - y, c), 2 * (1 - x) + 1 - y)]


def gather_shards(arrs, *, name):
    n = len(arrs)

    def body(*refs):
        ins, outs = refs[:n], refs[n:2 * n]
        send, recv, loc = refs[2 * n:]
        x, y, c = _place()
        me = 2 * x + y
        local = [pltpu.make_async_copy(ins[i], outs[i].at[me], loc.at[i]) for i in range(n)]
        remote = [pltpu.make_async_remote_copy(src_ref=ins[i], dst_ref=outs[i].at[me], send_sem=send.at[i, k],
                                               recv_sem=recv.at[i, k], device_id=peer, device_id_type=MESH_ID)
                  for i in range(n) for k, (peer, _) in enumerate(_chip_peers(x, y, c))]
        for cp in local + remote:
            cp.start()
        for cp in remote:
            cp.wait()
        for cp in local:
            cp.wait()

    return pl.pallas_call(
        body, name=name, in_specs=[ANY] * n, out_specs=[ANY] * n,
        out_shape=[jax.ShapeDtypeStruct((N_SHARD,) + a.shape, a.dtype) for a in arrs],
        scratch_shapes=[pltpu.SemaphoreType.DMA((n, 3)), pltpu.SemaphoreType.DMA((n, 3)),
                        pltpu.SemaphoreType.DMA((n,))])(*arrs)


def scatter_grads(grads, small, *, name):
    n = len(grads)

    def body(*refs):
        ins, small_ref = refs[:n], refs[n]
        own, got, small_all = refs[n + 1:2 * n + 1], refs[2 * n + 1:3 * n + 1], refs[3 * n + 1]
        send, recv, loc, ssend, srecv, sloc = refs[3 * n + 2:]
        x, y, c = _place()
        me = 2 * x + y
        dev = 4 * x + 2 * y + c
        local = [pltpu.make_async_copy(ins[i].at[me], own[i], loc.at[i]) for i in range(n)]
        local.append(pltpu.make_async_copy(small_ref, small_all.at[dev], sloc))
        remote = [pltpu.make_async_remote_copy(src_ref=ins[i].at[idx], dst_ref=got[i].at[k], send_sem=send.at[i, k],
                                               recv_sem=recv.at[i, k], device_id=peer, device_id_type=MESH_ID)
                  for i in range(n) for k, (peer, idx) in enumerate(_chip_peers(x, y, c))]
        for r in range(1, N_DEV):
            fx, fy, fc = (r >> 2) & 1, (r >> 1) & 1, r & 1
            peer = (1 - x if fx else x, 1 - y if fy else y, 1 - c if fc else c)
            remote.append(pltpu.make_async_remote_copy(
                src_ref=small_ref, dst_ref=small_all.at[dev], send_sem=ssend.at[r - 1], recv_sem=srecv.at[r - 1],
                device_id=peer, device_id_type=MESH_ID))
        for cp in local + remote:
            cp.start()
        for cp in remote:
            cp.wait()
        for cp in local:
            cp.wait()

    return pl.pallas_call(
        body, name=name, in_specs=[ANY] * (n + 1), out_specs=[ANY] * (2 * n + 1),
        out_shape=[jax.ShapeDtypeStruct(a.shape[1:], a.dtype) for a in grads]
        + [jax.ShapeDtypeStruct((3,) + a.shape[1:], a.dtype) for a in grads]
        + [jax.ShapeDtypeStruct((N_DEV,) + small.shape, small.dtype)],
        scratch_shapes=[pltpu.SemaphoreType.DMA((n, 3)), pltpu.SemaphoreType.DMA((n, 3)), pltpu.SemaphoreType.DMA((n,)),
                        pltpu.SemaphoreType.DMA((N_DEV - 1,)), pltpu.SemaphoreType.DMA((N_DEV - 1,)),
                        pltpu.SemaphoreType.DMA(())])(*grads, small)


def swap_cores(arrs, *, name):
    n = len(arrs)

    def body(*refs):
        ins, outs = refs[:n], refs[n:2 * n]
        send, recv = refs[2 * n:]
        x, y, c = _place()
        copies = [pltpu.make_async_remote_copy(src_ref=ins[i], dst_ref=outs[i], send_sem=send.at[i], recv_sem=recv.at[i],
                                               device_id=(x, y, 1 - c), device_id_type=MESH_ID) for i in range(n)]
        for cp in copies:
            cp.start()
        for cp in copies:
            cp.wait()

    return pl.pallas_call(
        body, name=name, in_specs=[ANY] * n, out_specs=[ANY] * n,
        out_shape=[jax.ShapeDtypeStruct(a.shape, a.dtype) for a in arrs],
        scratch_shapes=[pltpu.SemaphoreType.DMA((n,)), pltpu.SemaphoreType.DMA((n,))])(*arrs)


RED_ROWS = 128


def sum_chips(own, got, *, name):
    r, c = own.shape
    rb = RED_ROWS if r % RED_ROWS == 0 else r

    def body(o_ref, a_ref, b_ref, c_ref, out_ref):
        out_ref[...] = ((o_ref[...].astype(F32) + a_ref[...].astype(F32)) + b_ref[...].astype(F32)) + c_ref[...].astype(F32)

    gk = lambda k: pl.BlockSpec((None, rb, c), lambda i: (k, i, 0))
    row = pl.BlockSpec((rb, c), lambda i: (i, 0))
    return pl.pallas_call(
        body, name=name, grid=(r // rb,), in_specs=[row, gk(0), gk(1), gk(2)], out_specs=row,
        out_shape=jax.ShapeDtypeStruct((r, c), F32), compiler_params=_params(("parallel",)))(own, got, got, got)


def sum_devices(small_all, *, name):
    _, p, c = small_all.shape

    def body(a_ref, out_ref):
        acc = a_ref[0]
        for d in range(1, N_DEV):
            acc = acc + a_ref[d]
        out_ref[...] = acc

    return pl.pallas_call(
        body, name=name, grid=(1,), in_specs=[pl.BlockSpec((N_DEV, p, c), lambda i: (0, 0, 0))],
        out_specs=pl.BlockSpec((p, c), lambda i: (0, 0)), out_shape=jax.ShapeDtypeStruct((p, c), F32),
        compiler_params=_params(("arbitrary",)))(small_all)


def adamw(parts, w, m, v, *, name):
    nl, r, c = w.shape
    assert len(parts) == nl
    npart = len(parts[0])
    rb = RED_ROWS if r % RED_ROWS == 0 else r
    flat = [a for layer in parts for a in layer]

    def body(*refs):
        p_refs, (w_ref, m_ref, v_ref) = refs[:nl * npart], refs[nl * npart:nl * npart + 3]
        g_ref, d_ref, nm_ref, nv_ref = refs[nl * npart + 3:]
        layer = pl.program_id(0)
        grad = None
        for l in range(nl):
            gl = p_refs[l * npart][...]
            for j in range(1, npart):
                gl = gl + p_refs[l * npart + j][...]
            grad = gl if grad is None else jnp.where(layer == l, gl, grad)
        wv, mv, vv = w_ref[...], m_ref[...], v_ref[...]
        nm = ADAM_B1 * mv + (1.0 - ADAM_B1) * grad
        nv = ADAM_B2 * vv + (1.0 - ADAM_B2) * (grad * grad)
        m_hat = nm / (1.0 - ADAM_B1 ** ADAM_STEP)
        v_hat = nv / (1.0 - ADAM_B2 ** ADAM_STEP)
        g_ref[...] = grad
        d_ref[...] = -ADAM_LR * (m_hat / (jnp.sqrt(v_hat) + ADAM_EPS) + ADAM_WD * wv)
        nm_ref[...] = nm
        nv_ref[...] = nv

    pspec = pl.BlockSpec((rb, c), lambda l, i: (i, 0))
    wspec = pl.BlockSpec((None, rb, c), lambda l, i: (l, i, 0))
    osh = jax.ShapeDtypeStruct((nl, r, c), F32)
    return pl.pallas_call(
        body, name=name, grid=(nl, r // rb), in_specs=[pspec] * (nl * npart) + [wspec] * 3,
        out_specs=[wspec] * 4, out_shape=[osh] * 4, compiler_params=_params(("parallel", "parallel")))(*flat, w, m, v)


def _rows128(a):
    flat = a.reshape(-1)
    pad = (-flat.shape[0]) % 128
    return jnp.pad(flat, (0, pad)).reshape(-1, 128)


def _pack_rows(arrs, multiple=8):
    rows = jnp.concatenate([_rows128(a.astype(F32)) for a in arrs], axis=0)
    return jnp.pad(rows, ((0, (-rows.shape[0]) % multiple), (0, 0)))


def _unpack_rows(rows, shapes):
    out, r0 = [], 0
    for shp in shapes:
        size = 1
        for s in shp:
            size *= s
        nr = -(-size // 128)
        out.append(rows[r0:r0 + nr].reshape(-1)[:size].reshape(shp))
        r0 += nr
    return out


SMALL_LOCAL_GRADS = ["even_norm", "even_conv", "a_log", "dt_bias", "sinks", "onorm", "odd_norm", "odd_ln_g",
                     "odd_ln_b", "odd_w_s", "odd_b_s", "ffn_norm", "final_norm"]
BIG = ["even_w_in", "even_w_out", "odd_w_in", "odd_w_out", "ffn_w_gate", "ffn_w_up", "ffn_w_down"]
WEIGHTS = ["even_norm", "even_w_in", "even_conv", "even_a_log", "even_dt_bias", "even_sinks", "even_onorm",
           "even_w_out", "odd_norm", "odd_w_in", "odd_ln_g", "odd_ln_b", "odd_w_s", "odd_b_s", "odd_w_out",
           "ffn_norm", "ffn_w_gate", "ffn_w_up", "ffn_w_down", "final_norm"]
SMALL = [n for n in WEIGHTS if n not in BIG]


def kernel(x, even_norm, even_w_in, even_conv, even_a_log, even_dt_bias, even_sinks, even_onorm, even_w_out, odd_norm, odd_w_in, odd_ln_g, odd_ln_b, odd_w_s, odd_b_s, odd_w_out, ffn_norm, ffn_w_gate, ffn_w_up, ffn_w_down, final_norm, loss_target, m_even_norm, m_even_w_in, m_even_conv, m_even_a_log, m_even_dt_bias, m_even_sinks, m_even_onorm, m_even_w_out, m_odd_norm, m_odd_w_in, m_odd_ln_g, m_odd_ln_b, m_odd_w_s, m_odd_b_s, m_odd_w_out, m_ffn_norm, m_ffn_w_gate, m_ffn_w_up, m_ffn_w_down, m_final_norm, v_even_norm, v_even_w_in, v_even_conv, v_even_a_log, v_even_dt_bias, v_even_sinks, v_even_onorm, v_even_w_out, v_odd_norm, v_odd_w_in, v_odd_ln_g, v_odd_ln_b, v_odd_w_s, v_odd_b_s, v_odd_w_out, v_ffn_norm, v_ffn_w_gate, v_ffn_w_up, v_ffn_w_down, v_final_norm):
    args = dict(locals())
    wl = {n: args[n] for n in WEIGHTS}
    ml = {n: args["m_" + n] for n in WEIGHTS}
    vl = {n: args["v_" + n] for n in WEIGHTS}
    me = 2 * lax.axis_index("x") + lax.axis_index("y")

    shard_small = _pack_rows([even_conv[0], odd_norm, odd_ln_g, odd_ln_b])
    gathered = gather_shards(
        [even_w_in[0].astype(BF16), even_w_out[0].astype(BF16), odd_w_in[0].astype(BF16), odd_w_out[0].astype(BF16)]
        + [wl[n][l].astype(BF16) for n in ("ffn_w_gate", "ffn_w_up", "ffn_w_down") for l in range(2)]
        + [shard_small], name="gather_weights")
    g_win, g_wout, g_owin, g_owout = gathered[:4]
    g_ffn, g_small = gathered[4:10], gathered[10]
    conv_sh, onorm_sh, lng_sh, lnb_sh = zip(*[_unpack_rows(g_small[s], [(CONV_K, 768), (1, 512), (1, 512), (1, 512)])
                                              for s in range(N_SHARD)])
    pad816 = lambda a: jnp.pad(a, ((0, 0), (B_HEADS, 128 - 2 * B_HEADS)))
    w = {
        "even_norm": even_norm,
        "even_w_in": jnp.pad(jnp.transpose(g_win, (1, 0, 2)).reshape(D_MODEL, EVEN_IN),
                             ((0, 0), (0, EVEN_IN_PAD - EVEN_IN))),
        "even_conv": jnp.concatenate(conv_sh, axis=1),
        "a_log": pad816(even_a_log), "dt_bias": pad816(even_dt_bias),
        "sinks": jnp.pad(even_sinks, ((0, 0), (0, 128 - A_HEADS))),
        "onorm": even_onorm,
        "even_w_out": g_wout.reshape(D_MODEL, D_MODEL),
        "odd_norm": jnp.concatenate(onorm_sh, axis=1),
        "odd_w_in": g_owin,
        "odd_ln_g": jnp.concatenate(lng_sh, axis=1), "odd_ln_b": jnp.concatenate(lnb_sh, axis=1),
        "odd_w_s": odd_w_s[0],
        "odd_b_s": jnp.pad(odd_b_s[0].T, ((0, 0), (0, 128 - C_GROUPS))),
        "odd_w_out": g_owout.reshape(D_MODEL, D_MODEL),
        "ffn_norm": ffn_norm,
        "ffn_w_gate": g_ffn[0:2], "ffn_w_up": g_ffn[2:4],
        "ffn_w_down": [a.reshape(D_FF, D_MODEL) for a in g_ffn[4:6]],
        "final_norm": final_norm[None],
    }

    loss_l, grad_x, g = _local_step(x[0], loss_target[0], w)
    loss = lax.psum(loss_l[0, 0], ("x", "y", "c"))

    d_ff_s = D_FF // N_SHARD
    by_owner = [jnp.transpose(g["even_w_in"][:, :EVEN_IN].reshape(D_MODEL, N_SHARD, EVEN_IN // N_SHARD), (1, 0, 2)),
                g["even_w_out"].reshape(N_SHARD, D_MODEL // N_SHARD, D_MODEL),
                g["odd_w_in"],
                g["odd_w_out"].reshape(N_SHARD, D_MODEL // N_SHARD, D_MODEL)]
    by_owner += g["ffn_w_gate"] + g["ffn_w_up"] + [a.reshape(N_SHARD, d_ff_s, D_MODEL) for a in g["ffn_w_down"]]
    small_local = _pack_rows([g[n] for n in SMALL_LOCAL_GRADS])
    res = scatter_grads(by_owner, small_local, name="scatter_grads")
    nb = len(by_owner)
    own, got, small_all = res[:nb], res[nb:2 * nb], res[2 * nb]
    partial = [sum_chips(own[i], got[i], name=f"sum_chips_{i}") for i in range(nb)]
    other = swap_cores(partial, name="swap_cores")

    outs = {}
    layers_of = {"even_w_in": [0], "even_w_out": [1], "odd_w_in": [2], "odd_w_out": [3],
                 "ffn_w_gate": [4, 5], "ffn_w_up": [6, 7], "ffn_w_down": [8, 9]}
    for n in BIG:
        outs[n] = adamw([(partial[i], other[i]) for i in layers_of[n]], wl[n], ml[n], vl[n], name=f"adamw_{n}")

    small_sum = sum_devices(small_all, name="sum_devices")
    sg = dict(zip(SMALL_LOCAL_GRADS, _unpack_rows(small_sum, [g[n].shape for n in SMALL_LOCAL_GRADS])))
    own_cols = lambda a, width: lax.dynamic_slice_in_dim(a, me * width, width, axis=a.ndim - 1)
    small_grads = {
        "even_norm": sg["even_norm"], "even_conv": own_cols(sg["even_conv"], 768)[None],
        "even_a_log": sg["a_log"][:, B_HEADS:2 * B_HEADS], "even_dt_bias": sg["dt_bias"][:, B_HEADS:2 * B_HEADS],
        "even_sinks": sg["sinks"][:, :A_HEADS], "even_onorm": sg["onorm"],
        "odd_norm": own_cols(sg["odd_norm"], 512), "odd_ln_g": own_cols(sg["odd_ln_g"], 512),
        "odd_ln_b": own_cols(sg["odd_ln_b"], 512), "odd_w_s": sg["odd_w_s"][None],
        "odd_b_s": sg["odd_b_s"][:, :C_GROUPS].T[None], "ffn_norm": sg["ffn_norm"], "final_norm": sg["final_norm"][0],
    }
    packed = [_pack_rows([d[n] for n in SMALL])[None] for d in (small_grads, wl, ml, vl)]
    small_out = adamw([(packed[0][0],)], packed[1], packed[2], packed[3], name="adamw_small")
    shapes = [wl[n].shape for n in SMALL]
    for j in range(4):
        for n, a in zip(SMALL, _unpack_rows(small_out[j][0], shapes)):
            outs.setdefault(n, [None] * 4)[j] = a

    return (loss, grad_x[None], *[outs[n][0] for n in WEIGHTS], *[outs[n][1] for n in WEIGHTS],
            *[outs[n][2] for n in WEIGHTS], *[outs[n][3] for n in WEIGHTS])
```

```python
import functools

import jax
import jax.numpy as jnp
from jax import lax
from jax.experimental import pallas as pl
from jax.experimental.pallas import tpu as pltpu

F32 = jnp.float32
BF16 = jnp.bfloat16
NEG_INF = float("-inf")

D_MODEL = 2048
A_HEADS, A_KV_HEADS, A_HEAD_DIM, WINDOW = 16, 2, 64, 128
B_HEADS, B_HEAD_DIM, CONV_K, DN_CHUNK = 8, 128, 4, 64
C_GROUPS, C_CHUNK = 8, 128
C_GROUP_DIM = D_MODEL // C_GROUPS
D_FF = 5632
EPS = 1e-6
A_Q = A_HEADS * A_HEAD_DIM
A_KV = A_KV_HEADS * A_HEAD_DIM
B_W = B_HEADS * B_HEAD_DIM
EVEN_IN = A_Q + 2 * A_KV + 4 * B_W + 2 * B_HEADS
EVEN_IN_PAD = 5632
COL_KV = A_Q
COL_QKVB = A_Q + 2 * A_KV
COL_Z = COL_QKVB + 3 * B_W
COL_GATE = COL_Z + B_W
N_SHARD = 4

ADAM_LR, ADAM_B1, ADAM_B2, ADAM_EPS, ADAM_WD, ADAM_STEP = 0.001, 0.9, 0.999, 1e-08, 0.01, 10

VMEM_LIMIT_V7X = 56 * 1024 * 1024
MESH_ID = pl.DeviceIdType.MESH


def _params(sem=None):
    return pltpu.CompilerParams(dimension_semantics=sem, vmem_limit_bytes=VMEM_LIMIT_V7X)


def _sigmoid(x):
    return 1.0 / (1.0 + jnp.exp(-x))


def _silu(x):
    return x * _sigmoid(x)


def _dsilu(x):
    s = _sigmoid(x)
    return s * (1.0 + x * (1.0 - s))


def _gelu(x):
    return 0.5 * x * (1.0 + lax.erf(x * 0.7071067811865476))


def _dgelu(x):
    return 0.5 * (1.0 + lax.erf(x * 0.7071067811865476)) + x * jnp.exp(-0.5 * x * x) * 0.3989422804014327


def _dot(a, b, dims):
    if a.ndim == 3:
        (ca,), (cb,) = dims
        return lax.dot_general(a, b, (((ca + 1,), (cb + 1,)), ((0,), (0,))), preferred_element_type=F32)
    return lax.dot_general(a, b, (dims, ((), ())), preferred_element_type=F32)


NN = ((1,), (0,))
NT = ((1,), (1,))
TN = ((0,), (0,))


def _as3(b):
    return b if b.ndim == 3 else b[None]


def mm_nn(a, b, *, tm, tn, tk, out_dtype, name, res=None, act=None):
    b3 = _as3(b)
    m, k = a.shape
    s, k2, ns = b3.shape
    assert k2 == k and m % tm == 0 and ns % tn == 0 and k % tk == 0, (a.shape, b3.shape, tm, tn, tk)
    nps, nk = ns // tn, k // tk

    def body(*refs):
        if res is None:
            a_ref, b_ref, o_ref, acc = refs
        else:
            a_ref, b_ref, r_ref, o_ref, acc = refs
        kk = pl.program_id(2)

        @pl.when(kk == 0)
        def _():
            acc[...] = jnp.zeros_like(acc)

        acc[...] += _dot(a_ref[...].astype(BF16), b_ref[...].astype(BF16), NN)

        @pl.when(kk == nk - 1)
        def _():
            r = acc[...]
            if res is not None:
                r = r + r_ref[...].astype(F32)
            o_ref[...] = r.astype(out_dtype)

    in_specs = [pl.BlockSpec((tm, tk), lambda i, j, kk: (i, kk)),
                pl.BlockSpec((None, tk, tn), lambda i, j, kk: (j // nps, kk, j % nps))]
    args = [a, b3]
    if res is not None:
        in_specs.append(pl.BlockSpec((tm, tn), lambda i, j, kk: (i, j)))
        args.append(res)
    return pl.pallas_call(
        body, name=name, grid=(m // tm, s * nps, nk), in_specs=in_specs,
        out_specs=pl.BlockSpec((tm, tn), lambda i, j, kk: (i, j)),
        out_shape=jax.ShapeDtypeStruct((m, s * ns), out_dtype),
        scratch_shapes=[pltpu.VMEM((tm, tn), F32)],
        compiler_params=_params(("parallel", "parallel", "arbitrary")))(*args)


def mm_nt(a, b, *, tm, tn, tk, out_dtype, name, res=None):
    b3 = _as3(b)
    m, n = a.shape
    s, k, ns = b3.shape
    assert n == s * ns and m % tm == 0 and k % tn == 0 and ns % tk == 0, (a.shape, b3.shape, tm, tn, tk)
    rps = ns // tk
    nr = s * rps

    def body(*refs):
        if res is None:
            a_ref, b_ref, o_ref, acc = refs
        else:
            a_ref, b_ref, r_ref, o_ref, acc = refs
        r_id = pl.program_id(2)

        @pl.when(r_id == 0)
        def _():
            acc[...] = jnp.zeros_like(acc)

        acc[...] += _dot(a_ref[...].astype(BF16), b_ref[...].astype(BF16), NT)

        @pl.when(r_id == nr - 1)
        def _():
            r = acc[...]
            if res is not None:
                r = r + r_ref[...].astype(F32)
            o_ref[...] = r.astype(out_dtype)

    in_specs = [pl.BlockSpec((tm, tk), lambda i, j, r: (i, r)),
                pl.BlockSpec((None, tn, tk), lambda i, j, r: (r // rps, j, r % rps))]
    args = [a, b3]
    if res is not None:
        in_specs.append(pl.BlockSpec((tm, tn), lambda i, j, r: (i, j)))
        args.append(res)
    return pl.pallas_call(
        body, name=name, grid=(m // tm, k // tn, nr), in_specs=in_specs,
        out_specs=pl.BlockSpec((tm, tn), lambda i, j, r: (i, j)),
        out_shape=jax.ShapeDtypeStruct((m, k), out_dtype),
        scratch_shapes=[pltpu.VMEM((tm, tn), F32)],
        compiler_params=_params(("parallel", "parallel", "arbitrary")))(*args)


def mm_tn(a, b, *, shards, tm, tn, tk, out_dtype, name):
    m, k = a.shape
    m2, n = b.shape
    ns = n // shards
    assert m2 == m and n == shards * ns and m % tm == 0 and k % tk == 0 and ns % tn == 0, (a.shape, b.shape)
    nps, nm = ns // tn, m // tm

    def body(a_ref, b_ref, o_ref, acc):
        mi = pl.program_id(2)

        @pl.when(mi == 0)
        def _():
            acc[...] = jnp.zeros_like(acc)

        acc[...] += _dot(a_ref[...].astype(BF16), b_ref[...].astype(BF16), TN)

        @pl.when(mi == nm - 1)
        def _():
            o_ref[...] = acc[...].astype(out_dtype)

    return pl.pallas_call(
        body, name=name, grid=(k // tk, shards * nps, nm),
        in_specs=[pl.BlockSpec((tm, tk), lambda i, j, mi: (mi, i)),
                  pl.BlockSpec((tm, tn), lambda i, j, mi: (mi, j))],
        out_specs=pl.BlockSpec((None, tk, tn), lambda i, j, mi: (j // nps, i, j % nps)),
        out_shape=jax.ShapeDtypeStruct((shards, k, ns), out_dtype),
        scratch_shapes=[pltpu.VMEM((tk, tn), F32)],
        compiler_params=_params(("parallel", "parallel", "arbitrary")))(a, b)


def mm_gate_up(hn, wg, wu, *, tm, tn, tk, name):
    wg3, wu3 = _as3(wg), _as3(wu)
    m, k = hn.shape
    s, _, ns = wg3.shape
    assert m % tm == 0 and ns % tn == 0 and k % tk == 0
    nps, nk = ns // tn, k // tk

    def body(a_ref, g_ref, u_ref, og_ref, ou_ref, oa_ref, accg, accu):
        kk = pl.program_id(2)

        @pl.when(kk == 0)
        def _():
            accg[...] = jnp.zeros_like(accg)
            accu[...] = jnp.zeros_like(accu)

        a = a_ref[...].astype(BF16)
        accg[...] += _dot(a, g_ref[...].astype(BF16), NN)
        accu[...] += _dot(a, u_ref[...].astype(BF16), NN)

        @pl.when(kk == nk - 1)
        def _():
            g, u = accg[...], accu[...]
            og_ref[...] = g.astype(BF16)
            ou_ref[...] = u.astype(BF16)
            oa_ref[...] = (_silu(g) * u).astype(BF16)

    wspec = pl.BlockSpec((None, tk, tn), lambda i, j, kk: (j // nps, kk, j % nps))
    ospec = pl.BlockSpec((tm, tn), lambda i, j, kk: (i, j))
    osh = jax.ShapeDtypeStruct((m, s * ns), BF16)
    return pl.pallas_call(
        body, name=name, grid=(m // tm, s * nps, nk),
        in_specs=[pl.BlockSpec((tm, tk), lambda i, j, kk: (i, kk)), wspec, wspec],
        out_specs=[ospec, ospec, ospec], out_shape=[osh, osh, osh],
        scratch_shapes=[pltpu.VMEM((tm, tn), F32), pltpu.VMEM((tm, tn), F32)],
        compiler_params=_params(("parallel", "parallel", "arbitrary")))(hn, wg3, wu3)


def mm_down_bwd(dh, wd, gate, up, *, tm, tn, tk, name):
    m, d = dh.shape
    f, d2 = wd.shape
    assert d2 == d and m % tm == 0 and f % tn == 0 and d % tk == 0
    nr = d // tk

    def body(a_ref, b_ref, g_ref, u_ref, og_ref, ou_ref, acc):
        r_id = pl.program_id(2)

        @pl.when(r_id == 0)
        def _():
            acc[...] = jnp.zeros_like(acc)

        acc[...] += _dot(a_ref[...].astype(BF16), b_ref[...].astype(BF16), NT)

        @pl.when(r_id == nr - 1)
        def _():
            da = acc[...]
            g, u = g_ref[...].astype(F32), u_ref[...].astype(F32)
            og_ref[...] = (da * u * _dsilu(g)).astype(BF16)
            ou_ref[...] = (da * _silu(g)).astype(BF16)

    ospec = pl.BlockSpec((tm, tn), lambda i, j, r: (i, j))
    osh = jax.ShapeDtypeStruct((m, f), BF16)
    return pl.pallas_call(
        body, name=name, grid=(m // tm, f // tn, nr),
        in_specs=[pl.BlockSpec((tm, tk), lambda i, j, r: (i, r)),
                  pl.BlockSpec((tn, tk), lambda i, j, r: (j, r)), ospec, ospec],
        out_specs=[ospec, ospec], out_shape=[osh, osh],
        scratch_shapes=[pltpu.VMEM((tm, tn), F32)],
        compiler_params=_params(("parallel", "parallel", "arbitrary")))(dh, wd, gate, up)


ROWS = 256


def rms_fwd(x, g, *, name):
    t, d = x.shape

    def body(x_ref, g_ref, o_ref):
        xv = x_ref[...]
        r = lax.rsqrt(jnp.mean(xv * xv, axis=-1, keepdims=True) + EPS)
        o_ref[...] = (xv * r * g_ref[...]).astype(BF16)

    return pl.pallas_call(
        body, name=name, grid=(t // ROWS,),
        in_specs=[pl.BlockSpec((ROWS, d), lambda i: (i, 0)), pl.BlockSpec((1, d), lambda i: (0, 0))],
        out_specs=pl.BlockSpec((ROWS, d), lambda i: (i, 0)),
        out_shape=jax.ShapeDtypeStruct((t, d), BF16), compiler_params=_params(("parallel",)))(x, g)


def rms_bwd(x, g, dy, dres, *, name):
    t, d = x.shape

    def body(x_ref, g_ref, dy_ref, dr_ref, dx_ref, dg_ref):
        @pl.when(pl.program_id(0) == 0)
        def _():
            dg_ref[...] = jnp.zeros_like(dg_ref)

        xv, dyv = x_ref[...], dy_ref[...].astype(F32)
        r = lax.rsqrt(jnp.mean(xv * xv, axis=-1, keepdims=True) + EPS)
        dyg = dyv * g_ref[...]
        dx = r * dyg - xv * (r * r * r) * jnp.mean(dyg * xv, axis=-1, keepdims=True)
        dx_ref[...] = dx + dr_ref[...]
        dg_ref[...] += jnp.sum(dyv * xv * r, axis=0, keepdims=True)

    row = pl.BlockSpec((ROWS, d), lambda i: (i, 0))
    vec = pl.BlockSpec((1, d), lambda i: (0, 0))
    return pl.pallas_call(
        body, name=name, grid=(t // ROWS,), in_specs=[row, vec, row, row], out_specs=[row, vec],
        out_shape=[jax.ShapeDtypeStruct((t, d), F32), jax.ShapeDtypeStruct((1, d), F32)],
        compiler_params=_params(("arbitrary",)))(x, g, dy, dres)


def loss_head(h, g, target, *, name):
    t, d = h.shape

    def body(x_ref, g_ref, t_ref, loss_ref, dx_ref, dg_ref):
        @pl.when(pl.program_id(0) == 0)
        def _():
            dg_ref[...] = jnp.zeros_like(dg_ref)
            loss_ref[...] = jnp.zeros_like(loss_ref)

        xv, gv = x_ref[...], g_ref[...]
        r = lax.rsqrt(jnp.mean(xv * xv, axis=-1, keepdims=True) + EPS)
        e = xv * r * gv - t_ref[...]
        loss_ref[...] += 0.5 * jnp.sum(jnp.mean(e * e, axis=-1, keepdims=True), axis=0, keepdims=True)
        dyv = e * (1.0 / d)
        dyg = dyv * gv
        dx_ref[...] = r * dyg - xv * (r * r * r) * jnp.mean(dyg * xv, axis=-1, keepdims=True)
        dg_ref[...] += jnp.sum(dyv * xv * r, axis=0, keepdims=True)

    row = pl.BlockSpec((ROWS, d), lambda i: (i, 0))
    vec = pl.BlockSpec((1, d), lambda i: (0, 0))
    return pl.pallas_call(
        body, name=name, grid=(t // ROWS,), in_specs=[row, vec, row],
        out_specs=[pl.BlockSpec((1, 128), lambda i: (0, 0)), row, vec],
        out_shape=[jax.ShapeDtypeStruct((1, 128), F32), jax.ShapeDtypeStruct((t, d), F32),
                   jax.ShapeDtypeStruct((1, d), F32)],
        compiler_params=_params(("arbitrary",)))(h, g, target)


def _tril_mask():
    r = lax.broadcasted_iota(jnp.int32, (C_CHUNK, C_CHUNK), 0)
    c = lax.broadcasted_iota(jnp.int32, (C_CHUNK, C_CHUNK), 1)
    return r >= c


def _layer_norm_parts(v):
    mu = jnp.mean(v, axis=-1, keepdims=True)
    vc = v - mu
    rstd = lax.rsqrt(jnp.mean(vc * vc, axis=-1, keepdims=True) + EPS)
    return vc * rstd, rstd


def gmlp_fwd(zpre, ln_g, ln_b, ws, bs_t, *, name):
    t = zpre.shape[0]
    d = D_MODEL

    def body(zu_ref, zv_ref, g_ref, b_ref, ws_ref, bs_ref, o_ref):
        u = _gelu(zu_ref[...])
        vhat, _ = _layer_norm_parts(_gelu(zv_ref[...]))
        vln = (vhat * g_ref[...] + b_ref[...]).astype(BF16)
        mask = _tril_mask()
        for gi in range(C_GROUPS):
            sl = slice(gi * C_GROUP_DIM, (gi + 1) * C_GROUP_DIM)
            w = jnp.where(mask, ws_ref[gi], 0.0).astype(BF16)
            mixed = _dot(w, vln[:, sl], NN) + bs_ref[:, gi:gi + 1]
            o_ref[:, sl] = (u[:, sl] * mixed).astype(BF16)

    vec = pl.BlockSpec((1, d), lambda i: (0, 0))
    return pl.pallas_call(
        body, name=name, grid=(t // C_CHUNK,),
        in_specs=[pl.BlockSpec((C_CHUNK, d), lambda i: (i, 0)), pl.BlockSpec((C_CHUNK, d), lambda i: (i, 1)),
                  vec, vec, pl.BlockSpec((C_GROUPS, C_CHUNK, C_CHUNK), lambda i: (0, 0, 0)),
                  pl.BlockSpec((C_CHUNK, 128), lambda i: (0, 0))],
        out_specs=pl.BlockSpec((C_CHUNK, d), lambda i: (i, 0)),
        out_shape=jax.ShapeDtypeStruct((t, d), BF16), compiler_params=_params(("parallel",)))(
            zpre, zpre, ln_g, ln_b, ws, bs_t)


def gmlp_bwd(zpre, dgated, ln_g, ln_b, ws, bs_t, *, name):
    t = zpre.shape[0]
    d = D_MODEL

    def body(zu_ref, zv_ref, dg_ref, g_ref, b_ref, ws_ref, bs_ref, dz_ref, dws_ref, dbs_ref, dlg_ref, dlb_ref):
        @pl.when(pl.program_id(0) == 0)
        def _():
            dws_ref[...] = jnp.zeros_like(dws_ref)
            dbs_ref[...] = jnp.zeros_like(dbs_ref)
            dlg_ref[...] = jnp.zeros_like(dlg_ref)
            dlb_ref[...] = jnp.zeros_like(dlb_ref)

        zu, zv = zu_ref[...], zv_ref[...]
        u = _gelu(zu)
        vhat, rstd = _layer_norm_parts(_gelu(zv))
        gam = g_ref[...]
        vln = (vhat * gam + b_ref[...]).astype(BF16)
        dgt = dg_ref[...].astype(F32)
        mask = _tril_mask()
        lane = lax.broadcasted_iota(jnp.int32, (C_CHUNK, 128), 1)
        dbs = jnp.zeros((C_CHUNK, 128), F32)
        du_parts, dvln_parts = [], []
        for gi in range(C_GROUPS):
            sl = slice(gi * C_GROUP_DIM, (gi + 1) * C_GROUP_DIM)
            w = jnp.where(mask, ws_ref[gi], 0.0).astype(BF16)
            mixed = _dot(w, vln[:, sl], NN) + bs_ref[:, gi:gi + 1]
            du_parts.append(dgt[:, sl] * mixed)
            dmixed = dgt[:, sl] * u[:, sl]
            dmb = dmixed.astype(BF16)
            dws_ref[gi] += jnp.where(mask, _dot(dmb, vln[:, sl], NT), 0.0)
            dbs = dbs + jnp.where(lane == gi, jnp.sum(dmixed, axis=-1, keepdims=True), 0.0)
            dvln_parts.append(_dot(w, dmb, TN))
        dbs_ref[...] += dbs
        du = jnp.concatenate(du_parts, axis=-1)
        dvln = jnp.concatenate(dvln_parts, axis=-1)
        dlg_ref[...] += jnp.sum(dvln * vhat, axis=0, keepdims=True)
        dlb_ref[...] += jnp.sum(dvln, axis=0, keepdims=True)
        dvhat = dvln * gam
        dv = rstd * (dvhat - jnp.mean(dvhat, axis=-1, keepdims=True)
                     - vhat * jnp.mean(dvhat * vhat, axis=-1, keepdims=True))
        dz_ref[:, :d] = (du * _dgelu(zu)).astype(BF16)
        dz_ref[:, d:] = (dv * _dgelu(zv)).astype(BF16)

    vec = pl.BlockSpec((1, d), lambda i: (0, 0))
    wsp = pl.BlockSpec((C_GROUPS, C_CHUNK, C_CHUNK), lambda i: (0, 0, 0))
    bsp = pl.BlockSpec((C_CHUNK, 128), lambda i: (0, 0))
    return pl.pallas_call(
        body, name=name, grid=(t // C_CHUNK,),
        in_specs=[pl.BlockSpec((C_CHUNK, d), lambda i: (i, 0)), pl.BlockSpec((C_CHUNK, d), lambda i: (i, 1)),
                  pl.BlockSpec((C_CHUNK, d), lambda i: (i, 0)), vec, vec, wsp, bsp],
        out_specs=[pl.BlockSpec((C_CHUNK, 2 * d), lambda i: (i, 0)), wsp, bsp, vec, vec],
        out_shape=[jax.ShapeDtypeStruct((t, 2 * d), BF16), jax.ShapeDtypeStruct((C_GROUPS, C_CHUNK, C_CHUNK), F32),
                   jax.ShapeDtypeStruct((C_CHUNK, 128), F32), jax.ShapeDtypeStruct((1, d), F32),
                   jax.ShapeDtypeStruct((1, d), F32)],
        compiler_params=_params(("arbitrary",)))(zpre, zpre, dgated, ln_g, ln_b, ws, bs_t)


ATT_SCALE = A_HEAD_DIM ** -0.5
PAIRS = A_HEADS // 2
PAIRS_PER_KV = PAIRS // A_KV_HEADS


def _att_padded(tile):
    lo = lax.broadcasted_iota(jnp.int32, tile.shape, 1) < A_HEAD_DIM
    rolled = pltpu.roll(tile, A_HEAD_DIM, 1)
    zero = jnp.zeros_like(tile)
    return {(0, 0): jnp.where(lo, tile, zero).astype(BF16), (0, 1): jnp.where(lo, zero, rolled).astype(BF16),
            (1, 0): jnp.where(lo, rolled, zero).astype(BF16), (1, 1): jnp.where(lo, zero, tile).astype(BF16)}


def _att_valid(n):
    r = lax.broadcasted_iota(jnp.int32, (WINDOW, 2 * WINDOW), 0)
    c = lax.broadcasted_iota(jnp.int32, (WINDOW, 2 * WINDOW), 1)
    rel = r + WINDOW - c
    return (rel >= 0) & (rel < WINDOW) & ((c >= WINDOW) | (n > 0))


def _att_probs(qp, kpad, sink, valid):
    s = jnp.where(valid, _dot(qp, kpad, NT), NEG_INF)
    m = jnp.maximum(jnp.max(s, axis=-1, keepdims=True), sink)
    p = jnp.exp(s - m)
    e_sink = jnp.exp(sink - m)
    inv = 1.0 / (jnp.sum(p, axis=-1, keepdims=True) + e_sink)
    return p * inv, e_sink * inv


def _att_specs(t):
    return [pl.BlockSpec((WINDOW, A_Q), lambda n: (n, 0)),
            pl.BlockSpec((WINDOW, 2 * A_KV), lambda n: (n, COL_KV // (2 * A_KV))),
            pl.BlockSpec((WINDOW, 2 * A_KV), lambda n: (jnp.maximum(n - 1, 0), COL_KV // (2 * A_KV))),
            pl.BlockSpec((1, 128), lambda n: (0, 0))]


def att_fwd(proj, sinks, *, name):
    t = proj.shape[0]

    def body(q_ref, kvc_ref, kvp_ref, s_ref, o_ref):
        n = pl.program_id(0)
        kv = jnp.concatenate([kvp_ref[...], kvc_ref[...]], axis=0)
        kpad, vpad = _att_padded(kv[:, :128]), _att_padded(kv[:, 128:])
        valid = _att_valid(n)
        for j in range(PAIRS):
            qp = (q_ref[:, j * 128:(j + 1) * 128] * ATT_SCALE).astype(BF16)
            acc = jnp.zeros((WINDOW, 128), F32)
            for half in range(2):
                key = (j // PAIRS_PER_KV, half)
                h = 2 * j + half
                w, _ = _att_probs(qp, kpad[key], s_ref[:, h:h + 1], valid)
                acc = acc + _dot(w.astype(BF16), vpad[key], NN)
            o_ref[:, j * 128:(j + 1) * 128] = acc.astype(BF16)

    return pl.pallas_call(
        body, name=name, grid=(t // WINDOW,), in_specs=_att_specs(t),
        out_specs=pl.BlockSpec((WINDOW, A_Q), lambda n: (n, 0)),
        out_shape=jax.ShapeDtypeStruct((t, A_Q), BF16), compiler_params=_params(("parallel",)))(
            proj, proj, proj, sinks)


def att_bwd(proj, sinks, dout, *, name):
    t = proj.shape[0]

    def body(q_ref, kvc_ref, kvp_ref, s_ref, do_ref, dq_ref, dkc_ref, dkp_ref, ds_ref):
        n = pl.program_id(0)

        @pl.when(n == 0)
        def _():
            ds_ref[...] = jnp.zeros_like(ds_ref)

        kv = jnp.concatenate([kvp_ref[...], kvc_ref[...]], axis=0)
        kpad, vpad = _att_padded(kv[:, :128]), _att_padded(kv[:, 128:])
        valid = _att_valid(n)
        lane = lax.broadcasted_iota(jnp.int32, (1, 128), 1)
        dsink = jnp.zeros((1, 128), F32)
        zero = jnp.zeros((2 * WINDOW, 128), F32)
        dk_acc = {key: zero for key in kpad}
        dv_acc = {key: zero for key in kpad}
        for j in range(PAIRS):
            qp = (q_ref[:, j * 128:(j + 1) * 128] * ATT_SCALE).astype(BF16)
            dop = do_ref[:, j * 128:(j + 1) * 128].astype(BF16)
            dq = jnp.zeros((WINDOW, 128), F32)
            for half in range(2):
                key = (j // PAIRS_PER_KV, half)
                h = 2 * j + half
                w, w_sink = _att_probs(qp, kpad[key], s_ref[:, h:h + 1], valid)
                dw = _dot(dop, vpad[key], NT)
                delta = jnp.sum(w * dw, axis=-1, keepdims=True)
                dsc = (w * (dw - delta)).astype(BF16)
                dsink = dsink + jnp.where(lane == h, -jnp.sum(w_sink * delta, axis=0, keepdims=True), 0.0)
                dq = dq + _dot(dsc, kpad[key], NN)
                dk_acc[key] = dk_acc[key] + _dot(dsc, qp, TN)
                dv_acc[key] = dv_acc[key] + _dot(w.astype(BF16), dop, TN)
            dq_ref[:, j * 128:(j + 1) * 128] = (dq * ATT_SCALE).astype(BF16)
        ds_ref[...] += dsink
        lo = lax.broadcasted_iota(jnp.int32, (2 * WINDOW, 128), 1) < A_HEAD_DIM

        def tile(acc):
            return jnp.where(lo, acc[(0, 0)] + pltpu.roll(acc[(0, 1)], A_HEAD_DIM, 1),
                             pltpu.roll(acc[(1, 0)], A_HEAD_DIM, 1) + acc[(1, 1)])

        dkv = jnp.concatenate([tile(dk_acc), tile(dv_acc)], axis=1)
        dkp_ref[...] = dkv[:WINDOW]
        dkc_ref[...] = dkv[WINDOW:]

    kvo = pl.BlockSpec((WINDOW, 2 * A_KV), lambda n: (n, 0))
    return pl.pallas_call(
        body, name=name, grid=(t // WINDOW,),
        in_specs=_att_specs(t) + [pl.BlockSpec((WINDOW, A_Q), lambda n: (n, 0))],
        out_specs=[pl.BlockSpec((WINDOW, A_Q), lambda n: (n, 0)), kvo, kvo, pl.BlockSpec((1, 128), lambda n: (0, 0))],
        out_shape=[jax.ShapeDtypeStruct((t, A_Q), BF16), jax.ShapeDtypeStruct((t, 2 * A_KV), F32),
                   jax.ShapeDtypeStruct((t, 2 * A_KV), F32), jax.ShapeDtypeStruct((1, 128), F32)],
        compiler_params=_params(("arbitrary",)))(proj, proj, proj, sinks, dout)


QK_SCALE = B_HEAD_DIM ** -0.5
PREP_COLS = 256
PREP_NCB = 3 * B_W // PREP_COLS
HALO = 8


def _roll_rows(x, shift):
    n = x.shape[0]
    return x if shift % n == 0 else pltpu.roll(x, shift % n, 0)


def _conv_taps(xe, w):
    xs = [_roll_rows(xe, CONV_K - 1 - i) for i in range(CONV_K)]
    c = w[0:1] * xs[0]
    for i in range(1, CONV_K):
        c = c + w[i:i + 1] * xs[i]
    return xs, c


def dprep_fwd(proj, conv_w, *, name):
    t = proj.shape[0]
    tt = ROWS
    col0 = COL_QKVB // PREP_COLS

    def body(x_ref, h_ref, w_ref, o_ref):
        cb, n = pl.program_id(0), pl.program_id(1)
        halo = jnp.where(n > 0, h_ref[...], 0.0)
        xe = jnp.concatenate([halo, x_ref[...]], axis=0)
        _, c = _conv_taps(xe, w_ref[...])
        y = _silu(c)[HALO:]
        parts = []
        for hh in range(PREP_COLS // B_HEAD_DIM):
            yh = y[:, hh * B_HEAD_DIM:(hh + 1) * B_HEAD_DIM]
            parts.append(yh * lax.rsqrt(jnp.sum(yh * yh, axis=-1, keepdims=True) + EPS))
        nrm = jnp.concatenate(parts, axis=-1)
        o_ref[...] = jnp.where(cb < 4, nrm * QK_SCALE, jnp.where(cb < 8, nrm, y))

    return pl.pallas_call(
        body, name=name, grid=(PREP_NCB, t // tt),
        in_specs=[pl.BlockSpec((tt, PREP_COLS), lambda cb, n: (n, col0 + cb)),
                  pl.BlockSpec((HALO, PREP_COLS), lambda cb, n: (jnp.maximum(n * (tt // HALO) - 1, 0), col0 + cb)),
                  pl.BlockSpec((CONV_K, PREP_COLS), lambda cb, n: (0, cb))],
        out_specs=pl.BlockSpec((tt, PREP_COLS), lambda cb, n: (n, cb)),
        out_shape=jax.ShapeDtypeStruct((t, 3 * B_W), F32), compiler_params=_params(("parallel", "parallel")))(
            proj, proj, conv_w)


def dprep_bwd(proj, conv_w, dqkvn, *, name):
    t = proj.shape[0]
    tt = ROWS
    nb = t // tt
    col0 = COL_QKVB // PREP_COLS
    n8 = t // HALO

    def body(xc_ref, xb_ref, xa_ref, dc_ref, da_ref, w_ref, dx_ref, dw_ref):
        cb, n = pl.program_id(0), pl.program_id(1)

        @pl.when(n == 0)
        def _():
            dw_ref[...] = jnp.zeros_like(dw_ref)

        w = w_ref[...]
        xe = jnp.concatenate([jnp.where(n > 0, xb_ref[...], 0.0), xc_ref[...], xa_ref[...]], axis=0)
        xs, c = _conv_taps(xe, w)
        sg = _sigmoid(c)
        y = c * sg
        dout = jnp.concatenate([jnp.zeros((HALO, PREP_COLS), F32), dc_ref[...],
                                jnp.where(n < nb - 1, da_ref[...], 0.0)], axis=0)
        dsc = jnp.where(cb < 4, QK_SCALE, 1.0)
        parts = []
        for hh in range(PREP_COLS // B_HEAD_DIM):
            sl = slice(hh * B_HEAD_DIM, (hh + 1) * B_HEAD_DIM)
            yh, doh = y[:, sl], dout[:, sl] * dsc
            r = lax.rsqrt(jnp.sum(yh * yh, axis=-1, keepdims=True) + EPS)
            parts.append(doh * r - yh * (r * r * r) * jnp.sum(doh * yh, axis=-1, keepdims=True))
        dy = jnp.where(cb < 8, jnp.concatenate(parts, axis=-1), dout)
        dcv = dy * sg * (1.0 + c * (1.0 - sg))
        dxe = w[CONV_K - 1:CONV_K] * dcv
        for i in range(CONV_K - 1):
            dxe = dxe + w[i:i + 1] * _roll_rows(dcv, -(CONV_K - 1 - i))
        dx_ref[...] = dxe[HALO:HALO + tt].astype(BF16)
        for i in range(CONV_K):
            dw_ref[i:i + 1, :] += jnp.sum((dcv * xs[i])[HALO:HALO + tt], axis=0, keepdims=True)

    def after(n):
        return jnp.minimum((n + 1) * (tt // HALO), n8 - 1)

    return pl.pallas_call(
        body, name=name, grid=(PREP_NCB, nb),
        in_specs=[pl.BlockSpec((tt, PREP_COLS), lambda cb, n: (n, col0 + cb)),
                  pl.BlockSpec((HALO, PREP_COLS), lambda cb, n: (jnp.maximum(n * (tt // HALO) - 1, 0), col0 + cb)),
                  pl.BlockSpec((HALO, PREP_COLS), lambda cb, n: (after(n), col0 + cb)),
                  pl.BlockSpec((tt, PREP_COLS), lambda cb, n: (n, cb)),
                  pl.BlockSpec((HALO, PREP_COLS), lambda cb, n: (after(n), cb)),
                  pl.BlockSpec((CONV_K, PREP_COLS), lambda cb, n: (0, cb))],
        out_specs=[pl.BlockSpec((tt, PREP_COLS), lambda cb, n: (n, cb)),
                   pl.BlockSpec((CONV_K, PREP_COLS), lambda cb, n: (0, cb))],
        out_shape=[jax.ShapeDtypeStruct((t, 3 * B_W), BF16), jax.ShapeDtypeStruct((CONV_K, 3 * B_W), F32)],
        compiler_params=_params(("parallel", "arbitrary")))(proj, proj, proj, dqkvn, dqkvn, conv_w)


def _softplus(z):
    return jnp.maximum(z, 0.0) + jnp.log(1.0 + jnp.exp(-jnp.abs(z)))


def gates_fwd(proj, alog_pad, dtb_pad, *, name):
    t = proj.shape[0]

    def body(x_ref, a_ref, b_ref, o_ref):
        raw = x_ref[...]
        lane = lax.broadcasted_iota(jnp.int32, raw.shape, 1)
        g = -jnp.exp(a_ref[...]) * _softplus(raw + b_ref[...])
        o_ref[...] = jnp.where(lane < B_HEADS, _sigmoid(raw), jnp.where(lane < 2 * B_HEADS, g, 0.0))

    vec = pl.BlockSpec((1, 128), lambda n: (0, 0))
    return pl.pallas_call(
        body, name=name, grid=(t // ROWS,),
        in_specs=[pl.BlockSpec((ROWS, 128), lambda n: (n, COL_GATE // 128)), vec, vec],
        out_specs=pl.BlockSpec((ROWS, 128), lambda n: (n, 0)),
        out_shape=jax.ShapeDtypeStruct((t, 128), F32), compiler_params=_params(("parallel",)))(
            proj, alog_pad, dtb_pad)


def gates_bwd(proj, alog_pad, dtb_pad, dgates, *, name):
    t = proj.shape[0]

    def body(x_ref, a_ref, b_ref, dg_ref, dx_ref, da_ref, db_ref):
        @pl.when(pl.program_id(0) == 0)
        def _():
            da_ref[...] = jnp.zeros_like(da_ref)
            db_ref[...] = jnp.zeros_like(db_ref)

        raw, dgt = x_ref[...], dg_ref[...]
        lane = lax.broadcasted_iota(jnp.int32, raw.shape, 1)
        is_beta, is_g = lane < B_HEADS, (lane >= B_HEADS) & (lane < 2 * B_HEADS)
        beta = _sigmoid(raw)
        z = raw + b_ref[...]
        neg_a = -jnp.exp(a_ref[...])
        d_z = jnp.where(is_g, dgt * neg_a * _sigmoid(z), 0.0)
        dx_ref[...] = jnp.where(is_beta, dgt * beta * (1.0 - beta), d_z).astype(BF16)
        db_ref[...] += jnp.sum(d_z, axis=0, keepdims=True)
        da_ref[...] += jnp.sum(jnp.where(is_g, dgt * neg_a * _softplus(z), 0.0), axis=0, keepdims=True)

    vec = pl.BlockSpec((1, 128), lambda n: (0, 0))
    row = pl.BlockSpec((ROWS, 128), lambda n: (n, 0))
    return pl.pallas_call(
        body, name=name, grid=(t // ROWS,),
        in_specs=[pl.BlockSpec((ROWS, 128), lambda n: (n, COL_GATE // 128)), vec, vec, row],
        out_specs=[row, vec, vec],
        out_shape=[jax.ShapeDtypeStruct((t, 128), BF16), jax.ShapeDtypeStruct((1, 128), F32),
                   jax.ShapeDtypeStruct((1, 128), F32)],
        compiler_params=_params(("arbitrary",)))(proj, alog_pad, dtb_pad, dgates)


def _split2(a):
    hi = a.astype(BF16)
    return hi, (a - hi.astype(F32)).astype(BF16)


def _dotp(a, b, dims, passes):
    if passes == 1:
        return _dot(a.astype(BF16), b.astype(BF16), dims)
    ah, al = _split2(a)
    bh, bl = _split2(b)
    return _dot(ah, bh, dims) + (_dot(ah, bl, dims) + _dot(al, bh, dims))


_GRAD_DIMS = {NN: ((NT, False), (TN, False)), NT: ((NN, False), (TN, True)), TN: ((NT, True), (NN, False))}


def _make_mm(dims, passes):
    (da_dims, da_swap), (db_dims, db_swap) = _GRAD_DIMS[dims]

    @jax.custom_vjp
    def mm(a, b):
        return _dotp(a, b, dims, passes)

    def fwd(a, b):
        return _dotp(a, b, dims, passes), (a, b)

    def bwd(saved, ct):
        a, b = saved
        da = _dotp(b, ct, da_dims, passes) if da_swap else _dotp(ct, b, da_dims, passes)
        db = _dotp(ct, a, db_dims, passes) if db_swap else _dotp(a, ct, db_dims, passes)
        return da, db

    mm.defvjp(fwd, bwd)
    return mm


MM1 = {d: _make_mm(d, 1) for d in (NN, NT, TN)}
MM3 = {d: _make_mm(d, 3) for d in (NN, NT, TN)}


def _tri_ones(lower):
    r = lax.broadcasted_iota(jnp.int32, (DN_CHUNK, DN_CHUNK), 0)
    c = lax.broadcasted_iota(jnp.int32, (DN_CHUNK, DN_CHUNK), 1)
    return (r >= c if lower else r <= c).astype(BF16)


def _tri_sum(x, lower):
    tri = _tri_ones(lower)
    hi = x.astype(BF16)
    r1 = x - hi.astype(F32)
    mid = r1.astype(BF16)
    lo = (r1 - mid.astype(F32)).astype(BF16)
    return _dot(tri, hi, NN) + (_dot(tri, mid, NN) + _dot(tri, lo, NN))


def _delta_chunk(s0, q, k, v, beta, gam_c, gam_r):
    c = DN_CHUNK
    r = lax.broadcasted_iota(jnp.int32, (c, c), 0)
    cc = lax.broadcasted_iota(jnp.int32, (c, c), 1)
    incl, strict = r >= cc, r > cc
    eye = (r == cc).astype(F32)
    decay = jnp.exp(jnp.where(incl, gam_c - gam_r, NEG_INF))
    g_last = gam_c[:, c - 1:c, :]
    e_gam, e_rest, e_last = jnp.exp(gam_c), jnp.exp(g_last - gam_c), jnp.exp(g_last)
    a_neg = -jnp.where(strict, beta * MM1[NT](k, k) * decay, 0.0)
    inv = eye + a_neg
    pw = a_neg
    for _ in range(5):
        pw = MM3[NN](pw, pw)
        inv = inv + MM3[NN](inv, pw)
    uw = MM3[NN](inv, jnp.concatenate([v * beta, k * (beta * e_gam)], axis=-1))
    u, w = uw[..., :B_HEAD_DIM], uw[..., B_HEAD_DIM:]
    qk = MM1[NT](q, k) * decay
    v_new = u - MM1[NN](w, s0)
    o = MM1[NN](q * e_gam, s0) + MM1[NN](qk, v_new)
    s1 = s0 * e_last + MM1[TN](k * e_rest, v_new)
    return s1, o


def _delta_operands(q_ref, k_ref, v_ref, gt):
    heads = lambda ref: jnp.stack([ref[:, h * B_HEAD_DIM:(h + 1) * B_HEAD_DIM] for h in range(B_HEADS)])
    gam = _tri_sum(gt, True)
    gam_t = gam.T
    beta = jnp.stack([gt[:, h:h + 1] for h in range(B_HEADS)])
    gam_c = jnp.stack([gam[:, B_HEADS + h:B_HEADS + h + 1] for h in range(B_HEADS)])
    gam_r = jnp.stack([gam_t[B_HEADS + h:B_HEADS + h + 1, :] for h in range(B_HEADS)])
    return heads(q_ref), heads(k_ref), heads(v_ref), beta, gam_c, gam_r


def delta_fwd(qkvn, gates, *, name):
    t = qkvn.shape[0]
    nc = t // DN_CHUNK

    def body(q_ref, k_ref, v_ref, g_ref, o_ref, ss_ref, state):
        @pl.when(pl.program_id(0) == 0)
        def _():
            state[...] = jnp.zeros_like(state)

        s0 = state[...]
        ss_ref[...] = s0
        s1, o = _delta_chunk(s0, *_delta_operands(q_ref, k_ref, v_ref, g_ref[...]))
        state[...] = s1
        for h in range(B_HEADS):
            o_ref[:, h * B_HEAD_DIM:(h + 1) * B_HEAD_DIM] = o[h]

    blk = lambda j: pl.BlockSpec((DN_CHUNK, B_W), lambda n: (n, j))
    return pl.pallas_call(
        body, name=name, grid=(nc,),
        in_specs=[blk(0), blk(1), blk(2), pl.BlockSpec((DN_CHUNK, 128), lambda n: (n, 0))],
        out_specs=[blk(0), pl.BlockSpec((None, B_HEADS, B_HEAD_DIM, B_HEAD_DIM), lambda n: (n, 0, 0, 0))],
        out_shape=[jax.ShapeDtypeStruct((t, B_W), F32),
                   jax.ShapeDtypeStruct((nc, B_HEADS, B_HEAD_DIM, B_HEAD_DIM), F32)],
        scratch_shapes=[pltpu.VMEM((B_HEADS, B_HEAD_DIM, B_HEAD_DIM), F32)],
        compiler_params=_params(("arbitrary",)))(qkvn, qkvn, qkvn, gates)


def delta_bwd(qkvn, gates, ssave, do, *, name):
    t = qkvn.shape[0]
    nc = t // DN_CHUNK

    def body(q_ref, k_ref, v_ref, g_ref, ss_ref, do_ref, dx_ref, dg_ref, dstate):
        @pl.when(pl.program_id(0) == 0)
        def _():
            dstate[...] = jnp.zeros_like(dstate)

        lane = lax.broadcasted_iota(jnp.int32, (DN_CHUNK, 128), 1)
        row = lax.broadcasted_iota(jnp.int32, (128, DN_CHUNK), 0)
        dbeta_all = jnp.zeros((DN_CHUNK, 128), F32)
        dgam_c_all = jnp.zeros((DN_CHUNK, 128), F32)
        dgam_r_all = jnp.zeros((128, DN_CHUNK), F32)
        _, vjp = jax.vjp(_delta_chunk, ss_ref[...], *_delta_operands(q_ref, k_ref, v_ref, g_ref[...]))
        do = jnp.stack([do_ref[:, h * B_HEAD_DIM:(h + 1) * B_HEAD_DIM] for h in range(B_HEADS)])
        ds0, dq, dk, dv, dbeta, dgam_c, dgam_r = vjp((dstate[...], do))
        dstate[...] = ds0
        for h in range(B_HEADS):
            dx_ref[:, h * B_HEAD_DIM:(h + 1) * B_HEAD_DIM] = dq[h]
            dx_ref[:, B_W + h * B_HEAD_DIM:B_W + (h + 1) * B_HEAD_DIM] = dk[h]
            dx_ref[:, 2 * B_W + h * B_HEAD_DIM:2 * B_W + (h + 1) * B_HEAD_DIM] = dv[h]
            dbeta_all = dbeta_all + jnp.where(lane == h, dbeta[h], 0.0)
            dgam_c_all = dgam_c_all + jnp.where(lane == B_HEADS + h, dgam_c[h], 0.0)
            dgam_r_all = dgam_r_all + jnp.where(row == B_HEADS + h, dgam_r[h], 0.0)
        dg_ref[...] = dbeta_all + _tri_sum(dgam_c_all + dgam_r_all.T, False)

    blk = lambda j: pl.BlockSpec((DN_CHUNK, B_W), lambda n: (nc - 1 - n, j))
    gsp = pl.BlockSpec((DN_CHUNK, 128), lambda n: (nc - 1 - n, 0))
    return pl.pallas_call(
        body, name=name, grid=(nc,),
        in_specs=[blk(0), blk(1), blk(2), gsp,
                  pl.BlockSpec((None, B_HEADS, B_HEAD_DIM, B_HEAD_DIM), lambda n: (nc - 1 - n, 0, 0, 0)), blk(0)],
        out_specs=[pl.BlockSpec((DN_CHUNK, 3 * B_W), lambda n: (nc - 1 - n, 0)), gsp],
        out_shape=[jax.ShapeDtypeStruct((t, 3 * B_W), F32), jax.ShapeDtypeStruct((t, 128), F32)],
        scratch_shapes=[pltpu.VMEM((B_HEADS, B_HEAD_DIM, B_HEAD_DIM), F32)],
        compiler_params=_params(("arbitrary",)))(qkvn, qkvn, qkvn, gates, ssave, do)


def gnorm_fwd(o, proj, onorm, *, name):
    t = o.shape[0]

    def body(o_ref, z_ref, w_ref, out_ref):
        ov = o_ref[...]
        r = lax.rsqrt(jnp.mean(ov * ov, axis=-1, keepdims=True) + EPS)
        out_ref[...] = (ov * r * w_ref[...] * _silu(z_ref[...])).astype(BF16)

    blk = pl.BlockSpec((ROWS, B_HEAD_DIM), lambda n, h: (n, h))
    return pl.pallas_call(
        body, name=name, grid=(t // ROWS, B_HEADS),
        in_specs=[blk, pl.BlockSpec((ROWS, B_HEAD_DIM), lambda n, h: (n, COL_Z // B_HEAD_DIM + h)),
                  pl.BlockSpec((1, B_HEAD_DIM), lambda n, h: (0, 0))],
        out_specs=blk, out_shape=jax.ShapeDtypeStruct((t, B_W), BF16),
        compiler_params=_params(("parallel", "parallel")))(o, proj, onorm)


def gnorm_bwd(o, proj, onorm, dout, *, dcol0, name):
    t = o.shape[0]

    def body(o_ref, z_ref, w_ref, d_ref, do_ref, dz_ref, dw_ref):
        @pl.when((pl.program_id(0) == 0) & (pl.program_id(1) == 0))
        def _():
            dw_ref[...] = jnp.zeros_like(dw_ref)

        ov, zv, wv, dv = o_ref[...], z_ref[...], w_ref[...], d_ref[...].astype(F32)
        r = lax.rsqrt(jnp.mean(ov * ov, axis=-1, keepdims=True) + EPS)
        nrm = ov * r
        dz_ref[...] = (dv * nrm * wv * _dsilu(zv)).astype(BF16)
        da = dv * _silu(zv)
        dw_ref[...] += jnp.sum(da * nrm, axis=0, keepdims=True)
        dn = da * wv
        do_ref[...] = r * dn - ov * (r * r * r) * jnp.mean(dn * ov, axis=-1, keepdims=True)

    blk = pl.BlockSpec((ROWS, B_HEAD_DIM), lambda n, h: (n, h))
    vec = pl.BlockSpec((1, B_HEAD_DIM), lambda n, h: (0, 0))
    return pl.pallas_call(
        body, name=name, grid=(t // ROWS, B_HEADS),
        in_specs=[blk, pl.BlockSpec((ROWS, B_HEAD_DIM), lambda n, h: (n, COL_Z // B_HEAD_DIM + h)), vec,
                  pl.BlockSpec((ROWS, B_HEAD_DIM), lambda n, h: (n, dcol0 // B_HEAD_DIM + h))],
        out_specs=[blk, blk, vec],
        out_shape=[jax.ShapeDtypeStruct((t, B_W), F32), jax.ShapeDtypeStruct((t, B_W), BF16),
                   jax.ShapeDtypeStruct((1, B_HEAD_DIM), F32)],
        compiler_params=_params(("arbitrary", "arbitrary")))(o, proj, onorm, dout)


def _ffn_fwd(h, norm_g, wg, wu, wd, tm, tag):
    hn = rms_fwd(h, norm_g, name=f"ffn{tag}_norm")
    gate, up, act = mm_gate_up(hn, wg, wu, tm=tm, tn=1408, tk=512, name=f"ffn{tag}_gate_up")
    h_out = mm_nn(act, wd, tm=tm, tn=512, tk=1408, out_dtype=F32, res=h, name=f"ffn{tag}_down")
    return h_out, (hn, gate, up, act)


def _ffn_bwd(dh, h, norm_g, wg, wu, wd, saved, tm, tag):
    hn, gate, up, act = saved
    dwd = mm_tn(act, dh, shards=1, tm=tm, tn=512, tk=1408, out_dtype=BF16, name=f"ffn{tag}_dwd")[0]
    dgate, dup = mm_down_bwd(dh, wd, gate, up, tm=tm, tn=512, tk=2048, name=f"ffn{tag}_dact")
    dwg = mm_tn(hn, dgate, shards=N_SHARD, tm=tm, tn=1408, tk=1024, out_dtype=BF16, name=f"ffn{tag}_dwg")
    dwu = mm_tn(hn, dup, shards=N_SHARD, tm=tm, tn=1408, tk=1024, out_dtype=BF16, name=f"ffn{tag}_dwu")
    dhn = mm_nt(dgate, wg, tm=tm, tn=512, tk=1408, out_dtype=F32, name=f"ffn{tag}_dhn_g")
    dhn = mm_nt(dup, wu, tm=tm, tn=512, tk=1408, out_dtype=F32, res=dhn, name=f"ffn{tag}_dhn_u")
    dh_in, dnorm = rms_bwd(h, norm_g, dhn, dh, name=f"ffn{tag}_dnorm")
    return dh_in, dnorm, dwg, dwu, dwd


def _local_step(x, target, w):
    t = x.shape[0]
    tm = min(1024, t)
    g = {}

    hn0 = rms_fwd(x, w["even_norm"], name="l0_norm")
    proj = mm_nn(hn0, w["even_w_in"], tm=tm, tn=512, tk=2048, out_dtype=F32, name="l0_w_in")
    out_a = att_fwd(proj, w["sinks"], name="l0_att")
    qkvn = dprep_fwd(proj, w["even_conv"], name="l0_prep")
    gates = gates_fwd(proj, w["a_log"], w["dt_bias"], name="l0_gates")
    o_delta, ssave = delta_fwd(qkvn, gates, name="l0_delta")
    out_b = gnorm_fwd(o_delta, proj, w["onorm"], name="l0_gnorm")
    mix0 = jnp.concatenate([out_a, out_b], axis=-1)
    h1 = mm_nn(mix0, w["even_w_out"], tm=tm, tn=512, tk=2048, out_dtype=F32, res=x, name="l0_w_out")
    h2, ffn0 = _ffn_fwd(h1, w["ffn_norm"][0:1], w["ffn_w_gate"][0], w["ffn_w_up"][0], w["ffn_w_down"][0], tm, 0)
    hn2 = rms_fwd(h2, w["odd_norm"], name="l1_norm")
    zpre = mm_nn(hn2, w["odd_w_in"], tm=tm, tn=1024, tk=2048, out_dtype=F32, name="l1_w_in")
    gated = gmlp_fwd(zpre, w["odd_ln_g"], w["odd_ln_b"], w["odd_w_s"], w["odd_b_s"], name="l1_gmlp")
    h3 = mm_nn(gated, w["odd_w_out"], tm=tm, tn=512, tk=2048, out_dtype=F32, res=h2, name="l1_w_out")
    h4, ffn1 = _ffn_fwd(h3, w["ffn_norm"][1:2], w["ffn_w_gate"][1], w["ffn_w_up"][1], w["ffn_w_down"][1], tm, 1)
    loss, dh4, g["final_norm"] = loss_head(h4, w["final_norm"], target, name="loss_head")

    dh3, dn1, dwg1, dwu1, dwd1 = _ffn_bwd(dh4, h3, w["ffn_norm"][1:2], w["ffn_w_gate"][1], w["ffn_w_up"][1],
                                          w["ffn_w_down"][1], ffn1, tm, 1)
    g["odd_w_out"] = mm_tn(gated, dh3, shards=1, tm=tm, tn=512, tk=1024, out_dtype=BF16, name="l1_dw_out")[0]
    dgated = mm_nt(dh3, w["odd_w_out"], tm=tm, tn=512, tk=2048, out_dtype=BF16, name="l1_dgated")
    dzpre, g["odd_w_s"], g["odd_b_s"], g["odd_ln_g"], g["odd_ln_b"] = gmlp_bwd(
        zpre, dgated, w["odd_ln_g"], w["odd_ln_b"], w["odd_w_s"], w["odd_b_s"], name="l1_dgmlp")
    g["odd_w_in"] = mm_tn(hn2, dzpre, shards=N_SHARD, tm=tm, tn=1024, tk=1024, out_dtype=BF16, name="l1_dw_in")
    dhn2 = mm_nt(dzpre, w["odd_w_in"], tm=tm, tn=512, tk=1024, out_dtype=F32, name="l1_dhn")
    dh2, g["odd_norm"] = rms_bwd(h2, w["odd_norm"], dhn2, dh3, name="l1_dnorm")
    dh1, dn0, dwg0, dwu0, dwd0 = _ffn_bwd(dh2, h1, w["ffn_norm"][0:1], w["ffn_w_gate"][0], w["ffn_w_up"][0],
                                          w["ffn_w_down"][0], ffn0, tm, 0)
    g["ffn_norm"] = jnp.concatenate([dn0, dn1], axis=0)
    g["ffn_w_gate"], g["ffn_w_up"], g["ffn_w_down"] = [dwg0, dwg1], [dwu0, dwu1], [dwd0, dwd1]
    g["even_w_out"] = mm_tn(mix0, dh1, shards=1, tm=tm, tn=512, tk=1024, out_dtype=BF16, name="l0_dw_out")[0]
    dmix = mm_nt(dh1, w["even_w_out"], tm=tm, tn=512, tk=2048, out_dtype=F32, name="l0_dmix")
    dq_a, dkv_cur, dkv_prev, g["sinks"] = att_bwd(proj, w["sinks"], dmix, name="l0_datt")
    dkv = dkv_cur + jnp.concatenate([dkv_prev[WINDOW:], jnp.zeros((WINDOW, 2 * A_KV), F32)], axis=0)
    do_delta, dz, g["onorm"] = gnorm_bwd(o_delta, proj, w["onorm"], dmix, dcol0=A_Q, name="l0_dgnorm")
    dqkvn, dgates = delta_bwd(qkvn, gates, ssave, do_delta, name="l0_ddelta")
    dqkv_b, g["even_conv"] = dprep_bwd(proj, w["even_conv"], dqkvn, name="l0_dprep")
    draw, g["a_log"], g["dt_bias"] = gates_bwd(proj, w["a_log"], w["dt_bias"], dgates, name="l0_dgates")
    dproj = jnp.concatenate([dq_a, dkv.astype(BF16), dqkv_b, dz, draw,
                             jnp.zeros((t, EVEN_IN_PAD - COL_GATE - 128), BF16)], axis=-1)
    g["even_w_in"] = mm_tn(hn0, dproj, shards=1, tm=tm, tn=512, tk=1024, out_dtype=BF16, name="l0_dw_in")[0]
    dhn0 = mm_nt(dproj, w["even_w_in"], tm=tm, tn=512, tk=2816, out_dtype=F32, name="l0_dhn")
    grad_x, g["even_norm"] = rms_bwd(x, w["even_norm"], dhn0, dh1, name="l0_dnorm")
    return loss, grad_x, g


ANY = pl.BlockSpec(memory_space=pl.ANY)
N_DEV = 8


def _place():
    return lax.axis_index("x"), lax.axis_index("y"), lax.axis_index("c")


def _chip_peers(x, y, c):
    return [((1 - x, y, c), 2 * (1 - x) + y), ((x, 1 - y, c), 2 * x + 1 - y), ((1 - x, 1 - y, c), 2 * (1 - x) + 1 - y)]


def gather_shards(arrs, *, name):
    n = len(arrs)

    def body(*refs):
        ins, outs = refs[:n], refs[n:2 * n]
        send, recv, loc = refs[2 * n:]
        x, y, c = _place()
        me = 2 * x + y
        local = [pltpu.make_async_copy(ins[i], outs[i].at[me], loc.at[i]) for i in range(n)]
        remote = [pltpu.make_async_remote_copy(src_ref=ins[i], dst_ref=outs[i].at[me], send_sem=send.at[i, k],
                                               recv_sem=recv.at[i, k], device_id=peer, device_id_type=MESH_ID)
                  for i in range(n) for k, (peer, _) in enumerate(_chip_peers(x, y, c))]
        for cp in local + remote:
            cp.start()
        for cp in remote:
            cp.wait()
        for cp in local:
            cp.wait()

    return pl.pallas_call(
        body, name=name, in_specs=[ANY] * n, out_specs=[ANY] * n,
        out_shape=[jax.ShapeDtypeStruct((N_SHARD,) + a.shape, a.dtype) for a in arrs],
        scratch_shapes=[pltpu.SemaphoreType.DMA((n, 3)), pltpu.SemaphoreType.DMA((n, 3)),
                        pltpu.SemaphoreType.DMA((n,))])(*arrs)


def scatter_grads(grads, small, *, name):
    n = len(grads)

    def body(*refs):
        ins, small_ref = refs[:n], refs[n]
        own, got, small_all = refs[n + 1:2 * n + 1], refs[2 * n + 1:3 * n + 1], refs[3 * n + 1]
        send, recv, loc, ssend, srecv, sloc = refs[3 * n + 2:]
        x, y, c = _place()
        me = 2 * x + y
        dev = 4 * x + 2 * y + c
        local = [pltpu.make_async_copy(ins[i].at[me], own[i], loc.at[i]) for i in range(n)]
        local.append(pltpu.make_async_copy(small_ref, small_all.at[dev], sloc))
        remote = [pltpu.make_async_remote_copy(src_ref=ins[i].at[idx], dst_ref=got[i].at[k], send_sem=send.at[i, k],
                                               recv_sem=recv.at[i, k], device_id=peer, device_id_type=MESH_ID)
                  for i in range(n) for k, (peer, idx) in enumerate(_chip_peers(x, y, c))]
        for r in range(1, N_DEV):
            fx, fy, fc = (r >> 2) & 1, (r >> 1) & 1, r & 1
            peer = (1 - x if fx else x, 1 - y if fy else y, 1 - c if fc else c)
            remote.append(pltpu.make_async_remote_copy(
                src_ref=small_ref, dst_ref=small_all.at[dev], send_sem=ssend.at[r - 1], recv_sem=srecv.at[r - 1],
                device_id=peer, device_id_type=MESH_ID))
        for cp in local + remote:
            cp.start()
        for cp in remote:
            cp.wait()
        for cp in local:
            cp.wait()

    return pl.pallas_call(
        body, name=name, in_specs=[ANY] * (n + 1), out_specs=[ANY] * (2 * n + 1),
        out_shape=[jax.ShapeDtypeStruct(a.shape[1:], a.dtype) for a in grads]
        + [jax.ShapeDtypeStruct((3,) + a.shape[1:], a.dtype) for a in grads]
        + [jax.ShapeDtypeStruct((N_DEV,) + small.shape, small.dtype)],
        scratch_shapes=[pltpu.SemaphoreType.DMA((n, 3)), pltpu.SemaphoreType.DMA((n, 3)), pltpu.SemaphoreType.DMA((n,)),
                        pltpu.SemaphoreType.DMA((N_DEV - 1,)), pltpu.SemaphoreType.DMA((N_DEV - 1,)),
                        pltpu.SemaphoreType.DMA(())])(*grads, small)


def swap_cores(arrs, *, name):
    n = len(arrs)

    def body(*refs):
        ins, outs = refs[:n], refs[n:2 * n]
        send, recv = refs[2 * n:]
        x, y, c = _place()
        copies = [pltpu.make_async_remote_copy(src_ref=ins[i], dst_ref=outs[i], send_sem=send.at[i], recv_sem=recv.at[i],
                                               device_id=(x, y, 1 - c), device_id_type=MESH_ID) for i in range(n)]
        for cp in copies:
            cp.start()
        for cp in copies:
            cp.wait()

    return pl.pallas_call(
        body, name=name, in_specs=[ANY] * n, out_specs=[ANY] * n,
        out_shape=[jax.ShapeDtypeStruct(a.shape, a.dtype) for a in arrs],
        scratch_shapes=[pltpu.SemaphoreType.DMA((n,)), pltpu.SemaphoreType.DMA((n,))])(*arrs)


RED_ROWS = 128


def sum_chips(own, got, *, name):
    r, c = own.shape
    rb = RED_ROWS if r % RED_ROWS == 0 else r

    def body(o_ref, a_ref, b_ref, c_ref, out_ref):
        out_ref[...] = ((o_ref[...].astype(F32) + a_ref[...].astype(F32)) + b_ref[...].astype(F32)) + c_ref[...].astype(F32)

    gk = lambda k: pl.BlockSpec((None, rb, c), lambda i: (k, i, 0))
    row = pl.BlockSpec((rb, c), lambda i: (i, 0))
    return pl.pallas_call(
        body, name=name, grid=(r // rb,), in_specs=[row, gk(0), gk(1), gk(2)], out_specs=row,
        out_shape=jax.ShapeDtypeStruct((r, c), F32), compiler_params=_params(("parallel",)))(own, got, got, got)


def sum_devices(small_all, *, name):
    _, p, c = small_all.shape

    def body(a_ref, out_ref):
        acc = a_ref[0]
        for d in range(1, N_DEV):
            acc = acc + a_ref[d]
        out_ref[...] = acc

    return pl.pallas_call(
        body, name=name, grid=(1,), in_specs=[pl.BlockSpec((N_DEV, p, c), lambda i: (0, 0, 0))],
        out_specs=pl.BlockSpec((p, c), lambda i: (0, 0)), out_shape=jax.ShapeDtypeStruct((p, c), F32),
        compiler_params=_params(("arbitrary",)))(small_all)


def adamw(parts, w, m, v, *, name):
    nl, r, c = w.shape
    assert len(parts) == nl
    npart = len(parts[0])
    rb = RED_ROWS if r % RED_ROWS == 0 else r
    flat = [a for layer in parts for a in layer]

    def body(*refs):
        p_refs, (w_ref, m_ref, v_ref) = refs[:nl * npart], refs[nl * npart:nl * npart + 3]
        g_ref, d_ref, nm_ref, nv_ref = refs[nl * npart + 3:]
        layer = pl.program_id(0)
        grad = None
        for l in range(nl):
            gl = p_refs[l * npart][...]
            for j in range(1, npart):
                gl = gl + p_refs[l * npart + j][...]
            grad = gl if grad is None else jnp.where(layer == l, gl, grad)
        wv, mv, vv = w_ref[...], m_ref[...], v_ref[...]
        nm = ADAM_B1 * mv + (1.0 - ADAM_B1) * grad
        nv = ADAM_B2 * vv + (1.0 - ADAM_B2) * (grad * grad)
        m_hat = nm / (1.0 - ADAM_B1 ** ADAM_STEP)
        v_hat = nv / (1.0 - ADAM_B2 ** ADAM_STEP)
        g_ref[...] = grad
        d_ref[...] = -ADAM_LR * (m_hat / (jnp.sqrt(v_hat) + ADAM_EPS) + ADAM_WD * wv)
        nm_ref[...] = nm
        nv_ref[...] = nv

    pspec = pl.BlockSpec((rb, c), lambda l, i: (i, 0))
    wspec = pl.BlockSpec((None, rb, c), lambda l, i: (l, i, 0))
    osh = jax.ShapeDtypeStruct((nl, r, c), F32)
    return pl.pallas_call(
        body, name=name, grid=(nl, r // rb), in_specs=[pspec] * (nl * npart) + [wspec] * 3,
        out_specs=[wspec] * 4, out_shape=[osh] * 4, compiler_params=_params(("parallel", "parallel")))(*flat, w, m, v)


def _rows128(a):
    flat = a.reshape(-1)
    pad = (-flat.shape[0]) % 128
    return jnp.pad(flat, (0, pad)).reshape(-1, 128)


def _pack_rows(arrs, multiple=8):
    rows = jnp.concatenate([_rows128(a.astype(F32)) for a in arrs], axis=0)
    return jnp.pad(rows, ((0, (-rows.shape[0]) % multiple), (0, 0)))


def _unpack_rows(rows, shapes):
    out, r0 = [], 0
    for shp in shapes:
        size = 1
        for s in shp:
            size *= s
        nr = -(-size // 128)
        out.append(rows[r0:r0 + nr].reshape(-1)[:size].reshape(shp))
        r0 += nr
    return out


SMALL_LOCAL_GRADS = ["even_norm", "even_conv", "a_log", "dt_bias", "sinks", "onorm", "odd_norm", "odd_ln_g",
                     "odd_ln_b", "odd_w_s", "odd_b_s", "ffn_norm", "final_norm"]
BIG = ["even_w_in", "even_w_out", "odd_w_in", "odd_w_out", "ffn_w_gate", "ffn_w_up", "ffn_w_down"]
WEIGHTS = ["even_norm", "even_w_in", "even_conv", "even_a_log", "even_dt_bias", "even_sinks", "even_onorm",
           "even_w_out", "odd_norm", "odd_w_in", "odd_ln_g", "odd_ln_b", "odd_w_s", "odd_b_s", "odd_w_out",
           "ffn_norm", "ffn_w_gate", "ffn_w_up", "ffn_w_down", "final_norm"]
SMALL = [n for n in WEIGHTS if n not in BIG]


def kernel(x, even_norm, even_w_in, even_conv, even_a_log, even_dt_bias, even_sinks, even_onorm, even_w_out, odd_norm, odd_w_in, odd_ln_g, odd_ln_b, odd_w_s, odd_b_s, odd_w_out, ffn_norm, ffn_w_gate, ffn_w_up, ffn_w_down, final_norm, loss_target, m_even_norm, m_even_w_in, m_even_conv, m_even_a_log, m_even_dt_bias, m_even_sinks, m_even_onorm, m_even_w_out, m_odd_norm, m_odd_w_in, m_odd_ln_g, m_odd_ln_b, m_odd_w_s, m_odd_b_s, m_odd_w_out, m_ffn_norm, m_ffn_w_gate, m_ffn_w_up, m_ffn_w_down, m_final_norm, v_even_norm, v_even_w_in, v_even_conv, v_even_a_log, v_even_dt_bias, v_even_sinks, v_even_onorm, v_even_w_out, v_odd_norm, v_odd_w_in, v_odd_ln_g, v_odd_ln_b, v_odd_w_s, v_odd_b_s, v_odd_w_out, v_ffn_norm, v_ffn_w_gate, v_ffn_w_up, v_ffn_w_down, v_final_norm):
    args = dict(locals())
    wl = {n: args[n] for n in WEIGHTS}
    ml = {n: args["m_" + n] for n in WEIGHTS}
    vl = {n: args["v_" + n] for n in WEIGHTS}
    me = 2 * lax.axis_index("x") + lax.axis_index("y")

    shard_small = _pack_rows([even_conv[0], odd_norm, odd_ln_g, odd_ln_b])
    gathered = gather_shards(
        [even_w_in[0].astype(BF16), even_w_out[0].astype(BF16), odd_w_in[0].astype(BF16), odd_w_out[0].astype(BF16)]
        + [wl[n][l].astype(BF16) for n in ("ffn_w_gate", "ffn_w_up", "ffn_w_down") for l in range(2)]
        + [shard_small], name="gather_weights")
    g_win, g_wout, g_owin, g_owout = gathered[:4]
    g_ffn, g_small = gathered[4:10], gathered[10]
    conv_sh, onorm_sh, lng_sh, lnb_sh = zip(*[_unpack_rows(g_small[s], [(CONV_K, 768), (1, 512), (1, 512), (1, 512)])
                                              for s in range(N_SHARD)])
    pad816 = lambda a: jnp.pad(a, ((0, 0), (B_HEADS, 128 - 2 * B_HEADS)))
    w = {
        "even_norm": even_norm,
        "even_w_in": jnp.pad(jnp.transpose(g_win, (1, 0, 2)).reshape(D_MODEL, EVEN_IN),
                             ((0, 0), (0, EVEN_IN_PAD - EVEN_IN))),
        "even_conv": jnp.concatenate(conv_sh, axis=1),
        "a_log": pad816(even_a_log), "dt_bias": pad816(even_dt_bias),
        "sinks": jnp.pad(even_sinks, ((0, 0), (0, 128 - A_HEADS))),
        "onorm": even_onorm,
        "even_w_out": g_wout.reshape(D_MODEL, D_MODEL),
        "odd_norm": jnp.concatenate(onorm_sh, axis=1),
        "odd_w_in": g_owin,
        "odd_ln_g": jnp.concatenate(lng_sh, axis=1), "odd_ln_b": jnp.concatenate(lnb_sh, axis=1),
        "odd_w_s": odd_w_s[0],
        "odd_b_s": jnp.pad(odd_b_s[0].T, ((0, 0), (0, 128 - C_GROUPS))),
        "odd_w_out": g_owout.reshape(D_MODEL, D_MODEL),
        "ffn_norm": ffn_norm,
        "ffn_w_gate": g_ffn[0:2], "ffn_w_up": g_ffn[2:4],
        "ffn_w_down": [a.reshape(D_FF, D_MODEL) for a in g_ffn[4:6]],
        "final_norm": final_norm[None],
    }

    loss_l, grad_x, g = _local_step(x[0], loss_target[0], w)
    loss = lax.psum(loss_l[0, 0], ("x", "y", "c"))

    d_ff_s = D_FF // N_SHARD
    by_owner = [jnp.transpose(g["even_w_in"][:, :EVEN_IN].reshape(D_MODEL, N_SHARD, EVEN_IN // N_SHARD), (1, 0, 2)),
                g["even_w_out"].reshape(N_SHARD, D_MODEL // N_SHARD, D_MODEL),
                g["odd_w_in"],
                g["odd_w_out"].reshape(N_SHARD, D_MODEL // N_SHARD, D_MODEL)]
    by_owner += g["ffn_w_gate"] + g["ffn_w_up"] + [a.reshape(N_SHARD, d_ff_s, D_MODEL) for a in g["ffn_w_down"]]
    small_local = _pack_rows([g[n] for n in SMALL_LOCAL_GRADS])
    res = scatter_grads(by_owner, small_local, name="scatter_grads")
    nb = len(by_owner)
    own, got, small_all = res[:nb], res[nb:2 * nb], res[2 * nb]
    partial = [sum_chips(own[i], got[i], name=f"sum_chips_{i}") for i in range(nb)]
    other = swap_cores(partial, name="swap_cores")

    outs = {}
    layers_of = {"even_w_in": [0], "even_w_out": [1], "odd_w_in": [2], "odd_w_out": [3],
                 "ffn_w_gate": [4, 5], "ffn_w_up": [6, 7], "ffn_w_down": [8, 9]}
    for n in BIG:
        outs[n] = adamw([(partial[i], other[i]) for i in layers_of[n]], wl[n], ml[n], vl[n], name=f"adamw_{n}")

    small_sum = sum_devices(small_all, name="sum_devices")
    sg = dict(zip(SMALL_LOCAL_GRADS, _unpack_rows(small_sum, [g[n].shape for n in SMALL_LOCAL_GRADS])))
    own_cols = lambda a, width: lax.dynamic_slice_in_dim(a, me * width, width, axis=a.ndim - 1)
    small_grads = {
        "even_norm": sg["even_norm"], "even_conv": own_cols(sg["even_conv"], 768)[None],
        "even_a_log": sg["a_log"][:, B_HEADS:2 * B_HEADS], "even_dt_bias": sg["dt_bias"][:, B_HEADS:2 * B_HEADS],
        "even_sinks": sg["sinks"][:, :A_HEADS], "even_onorm": sg["onorm"],
        "odd_norm": own_cols(sg["odd_norm"], 512), "odd_ln_g": own_cols(sg["odd_ln_g"], 512),
        "odd_ln_b": own_cols(sg["odd_ln_b"], 512), "odd_w_s": sg["odd_w_s"][None],
        "odd_b_s": sg["odd_b_s"][:, :C_GROUPS].T[None], "ffn_norm": sg["ffn_norm"], "final_norm": sg["final_norm"][0],
    }
    packed = [_pack_rows([d[n] for n in SMALL])[None] for d in (small_grads, wl, ml, vl)]
    small_out = adamw([(packed[0][0],)], packed[1], packed[2], packed[3], name="adamw_small")
    shapes = [wl[n].shape for n in SMALL]
    for j in range(4):
        for n, a in zip(SMALL, _unpack_rows(small_out[j][0], shapes)):
            outs.setdefault(n, [None] * 4)[j] = a

    return (loss, grad_x[None], *[outs[n][0] for n in WEIGHTS], *[outs[n][1] for n in WEIGHTS],
            *[outs[n][2] for n in WEIGHTS], *[outs[n][3] for n in WEIGHTS])
```

```python
import functools

import jax
import jax.numpy as jnp
from jax import lax
from jax.experimental import pallas as pl
from jax.experimental.pallas import tpu as pltpu

F32 = jnp.float32
BF16 = jnp.bfloat16
NEG_INF = float("-inf")

D_MODEL = 2048
A_HEADS, A_KV_HEADS, A_HEAD_DIM, WINDOW = 16, 2, 64, 128
B_HEADS, B_HEAD_DIM, CONV_K, DN_CHUNK = 8, 128, 4, 64
C_GROUPS, C_CHUNK = 8, 128
C_GROUP_DIM = D_MODEL // C_GROUPS
D_FF = 5632
EPS = 1e-6
A_Q = A_HEADS * A_HEAD_DIM
A_KV = A_KV_HEADS * A_HEAD_DIM
B_W = B_HEADS * B_HEAD_DIM
EVEN_IN = A_Q + 2 * A_KV + 4 * B_W + 2 * B_HEADS
EVEN_IN_PAD = 5632
COL_KV = A_Q
COL_QKVB = A_Q + 2 * A_KV
COL_Z = COL_QKVB + 3 * B_W
COL_GATE = COL_Z + B_W
N_SHARD = 4

ADAM_LR, ADAM_B1, ADAM_B2, ADAM_EPS, ADAM_WD, ADAM_STEP = 0.001, 0.9, 0.999, 1e-08, 0.01, 10

VMEM_LIMIT_V7X = 56 * 1024 * 1024
MESH_ID = pl.DeviceIdType.MESH


def _params(sem=None):
    return pltpu.CompilerParams(dimension_semantics=sem, vmem_limit_bytes=VMEM_LIMIT_V7X)


def _sigmoid(x):
    return 1.0 / (1.0 + jnp.exp(-x))


def _silu(x):
    return x * _sigmoid(x)


def _dsilu(x):
    s = _sigmoid(x)
    return s * (1.0 + x * (1.0 - s))


def _gelu(x):
    return 0.5 * x * (1.0 + lax.erf(x * 0.7071067811865476))


def _dgelu(x):
    return 0.5 * (1.0 + lax.erf(x * 0.7071067811865476)) + x * jnp.exp(-0.5 * x * x) * 0.3989422804014327


def _dot(a, b, dims):
    if a.ndim == 3:
        (ca,), (cb,) = dims
        return lax.dot_general(a, b, (((ca + 1,), (cb + 1,)), ((0,), (0,))), preferred_element_type=F32)
    return lax.dot_general(a, b, (dims, ((), ())), preferred_element_type=F32)


NN = ((1,), (0,))
NT = ((1,), (1,))
TN = ((0,), (0,))


def _as3(b):
    return b if b.ndim == 3 else b[None]


def mm_nn(a, b, *, tm, tn, tk, out_dtype, name, res=None, act=None):
    b3 = _as3(b)
    m, k = a.shape
    s, k2, ns = b3.shape
    assert k2 == k and m % tm == 0 and ns % tn == 0 and k % tk == 0, (a.shape, b3.shape, tm, tn, tk)
    nps, nk = ns // tn, k // tk

    def body(*refs):
        if res is None:
            a_ref, b_ref, o_ref, acc = refs
        else:
            a_ref, b_ref, r_ref, o_ref, acc = refs
        kk = pl.program_id(2)

        @pl.when(kk == 0)
        def _():
            acc[...] = jnp.zeros_like(acc)

        acc[...] += _dot(a_ref[...].astype(BF16), b_ref[...].astype(BF16), NN)

        @pl.when(kk == nk - 1)
        def _():
            r = acc[...]
            if res is not None:
                r = r + r_ref[...].astype(F32)
            o_ref[...] = r.astype(out_dtype)

    in_specs = [pl.BlockSpec((tm, tk), lambda i, j, kk: (i, kk)),
                pl.BlockSpec((None, tk, tn), lambda i, j, kk: (j // nps, kk, j % nps))]
    args = [a, b3]
    if res is not None:
        in_specs.append(pl.BlockSpec((tm, tn), lambda i, j, kk: (i, j)))
        args.append(res)
    return pl.pallas_call(
        body, name=name, grid=(m // tm, s * nps, nk), in_specs=in_specs,
        out_specs=pl.BlockSpec((tm, tn), lambda i, j, kk: (i, j)),
        out_shape=jax.ShapeDtypeStruct((m, s * ns), out_dtype),
        scratch_shapes=[pltpu.VMEM((tm, tn), F32)],
        compiler_params=_params(("parallel", "parallel", "arbitrary")))(*args)


def mm_nt(a, b, *, tm, tn, tk, out_dtype, name, res=None):
    b3 = _as3(b)
    m, n = a.shape
    s, k, ns = b3.shape
    assert n == s * ns and m % tm == 0 and k % tn == 0 and ns % tk == 0, (a.shape, b3.shape, tm, tn, tk)
    rps = ns // tk
    nr = s * rps

    def body(*refs):
        if res is None:
            a_ref, b_ref, o_ref, acc = refs
        else:
            a_ref, b_ref, r_ref, o_ref, acc = refs
        r_id = pl.program_id(2)

        @pl.when(r_id == 0)
        def _():
            acc[...] = jnp.zeros_like(acc)

        acc[...] += _dot(a_ref[...].astype(BF16), b_ref[...].astype(BF16), NT)

        @pl.when(r_id == nr - 1)
        def _():
            r = acc[...]
            if res is not None:
                r = r + r_ref[...].astype(F32)
            o_ref[...] = r.astype(out_dtype)

    in_specs = [pl.BlockSpec((tm, tk), lambda i, j, r: (i, r)),
                pl.BlockSpec((None, tn, tk), lambda i, j, r: (r // rps, j, r % rps))]
    args = [a, b3]
    if res is not None:
        in_specs.append(pl.BlockSpec((tm, tn), lambda i, j, r: (i, j)))
        args.append(res)
    return pl.pallas_call(
        body, name=name, grid=(m // tm, k // tn, nr), in_specs=in_specs,
        out_specs=pl.BlockSpec((tm, tn), lambda i, j, r: (i, j)),
        out_shape=jax.ShapeDtypeStruct((m, k), out_dtype),
        scratch_shapes=[pltpu.VMEM((tm, tn), F32)],
        compiler_params=_params(("parallel", "parallel", "arbitrary")))(*args)


def mm_tn(a, b, *, shards, tm, tn, tk, out_dtype, name):
    m, k = a.shape
    m2, n = b.shape
    ns = n // shards
    assert m2 == m and n == shards * ns and m % tm == 0 and k % tk == 0 and ns % tn == 0, (a.shape, b.shape)
    nps, nm = ns // tn, m // tm

    def body(a_ref, b_ref, o_ref, acc):
        mi = pl.program_id(2)

        @pl.when(mi == 0)
        def _():
            acc[...] = jnp.zeros_like(acc)

        acc[...] += _dot(a_ref[...].astype(BF16), b_ref[...].astype(BF16), TN)

        @pl.when(mi == nm - 1)
        def _():
            o_ref[...] = acc[...].astype(out_dtype)

    return pl.pallas_call(
        body, name=name, grid=(k // tk, shards * nps, nm),
        in_specs=[pl.BlockSpec((tm, tk), lambda i, j, mi: (mi, i)),
                  pl.BlockSpec((tm, tn), lambda i, j, mi: (mi, j))],
        out_specs=pl.BlockSpec((None, tk, tn), lambda i, j, mi: (j // nps, i, j % nps)),
        out_shape=jax.ShapeDtypeStruct((shards, k, ns), out_dtype),
        scratch_shapes=[pltpu.VMEM((tk, tn), F32)],
        compiler_params=_params(("parallel", "parallel", "arbitrary")))(a, b)


def mm_gate_up(hn, wg, wu, *, tm, tn, tk, name):
    wg3, wu3 = _as3(wg), _as3(wu)
    m, k = hn.shape
    s, _, ns = wg3.shape
    assert m % tm == 0 and ns % tn == 0 and k % tk == 0
    nps, nk = ns // tn, k // tk

    def body(a_ref, g_ref, u_ref, og_ref, ou_ref, oa_ref, accg, accu):
        kk = pl.program_id(2)

        @pl.when(kk == 0)
        def _():
            accg[...] = jnp.zeros_like(accg)
            accu[...] = jnp.zeros_like(accu)

        a = a_ref[...].astype(BF16)
        accg[...] += _dot(a, g_ref[...].astype(BF16), NN)
        accu[...] += _dot(a, u_ref[...].astype(BF16), NN)

        @pl.when(kk == nk - 1)
        def _():
            g, u = accg[...], accu[...]
            og_ref[...] = g.astype(BF16)
            ou_ref[...] = u.astype(BF16)
            oa_ref[...] = (_silu(g) * u).astype(BF16)

    wspec = pl.BlockSpec((None, tk, tn), lambda i, j, kk: (j // nps, kk, j % nps))
    ospec = pl.BlockSpec((tm, tn), lambda i, j, kk: (i, j))
    osh = jax.ShapeDtypeStruct((m, s * ns), BF16)
    return pl.pallas_call(
        body, name=name, grid=(m // tm, s * nps, nk),
        in_specs=[pl.BlockSpec((tm, tk), lambda i, j, kk: (i, kk)), wspec, wspec],
        out_specs=[ospec, ospec, ospec], out_shape=[osh, osh, osh],
        scratch_shapes=[pltpu.VMEM((tm, tn), F32), pltpu.VMEM((tm, tn), F32)],
        compiler_params=_params(("parallel", "parallel", "arbitrary")))(hn, wg3, wu3)


def mm_down_bwd(dh, wd, gate, up, *, tm, tn, tk, name):
    m, d = dh.shape
    f, d2 = wd.shape
    assert d2 == d and m % tm == 0 and f % tn == 0 and d % tk == 0
    nr = d // tk

    def body(a_ref, b_ref, g_ref, u_ref, og_ref, ou_ref, acc):
        r_id = pl.program_id(2)

        @pl.when(r_id == 0)
        def _():
            acc[...] = jnp.zeros_like(acc)

        acc[...] += _dot(a_ref[...].astype(BF16), b_ref[...].astype(BF16), NT)

        @pl.when(r_id == nr - 1)
        def _():
            da = acc[...]
            g, u = g_ref[...].astype(F32), u_ref[...].astype(F32)
            og_ref[...] = (da * u * _dsilu(g)).astype(BF16)
            ou_ref[...] = (da * _silu(g)).astype(BF16)

    ospec = pl.BlockSpec((tm, tn), lambda i, j, r: (i, j))
    osh = jax.ShapeDtypeStruct((m, f), BF16)
    return pl.pallas_call(
        body, name=name, grid=(m // tm, f // tn, nr),
        in_specs=[pl.BlockSpec((tm, tk), lambda i, j, r: (i, r)),
                  pl.BlockSpec((tn, tk), lambda i, j, r: (j, r)), ospec, ospec],
        out_specs=[ospec, ospec], out_shape=[osh, osh],
        scratch_shapes=[pltpu.VMEM((tm, tn), F32)],
        compiler_params=_params(("parallel", "parallel", "arbitrary")))(dh, wd, gate, up)


ROWS = 256


def rms_fwd(x, g, *, name):
    t, d = x.shape

    def body(x_ref, g_ref, o_ref):
        xv = x_ref[...]
        r = lax.rsqrt(jnp.mean(xv * xv, axis=-1, keepdims=True) + EPS)
        o_ref[...] = (xv * r * g_ref[...]).astype(BF16)

    return pl.pallas_call(
        body, name=name, grid=(t // ROWS,),
        in_specs=[pl.BlockSpec((ROWS, d), lambda i: (i, 0)), pl.BlockSpec((1, d), lambda i: (0, 0))],
        out_specs=pl.BlockSpec((ROWS, d), lambda i: (i, 0)),
        out_shape=jax.ShapeDtypeStruct((t, d), BF16), compiler_params=_params(("parallel",)))(x, g)


def rms_bwd(x, g, dy, dres, *, name):
    t, d = x.shape

    def body(x_ref, g_ref, dy_ref, dr_ref, dx_ref, dg_ref):
        @pl.when(pl.program_id(0) == 0)
        def _():
            dg_ref[...] = jnp.zeros_like(dg_ref)

        xv, dyv = x_ref[...], dy_ref[...].astype(F32)
        r = lax.rsqrt(jnp.mean(xv * xv, axis=-1, keepdims=True) + EPS)
        dyg = dyv * g_ref[...]
        dx = r * dyg - xv * (r * r * r) * jnp.mean(dyg * xv, axis=-1, keepdims=True)
        dx_ref[...] = dx + dr_ref[...]
        dg_ref[...] += jnp.sum(dyv * xv * r, axis=0, keepdims=True)

    row = pl.BlockSpec((ROWS, d), lambda i: (i, 0))
    vec = pl.BlockSpec((1, d), lambda i: (0, 0))
    return pl.pallas_call(
        body, name=name, grid=(t // ROWS,), in_specs=[row, vec, row, row], out_specs=[row, vec],
        out_shape=[jax.ShapeDtypeStruct((t, d), F32), jax.ShapeDtypeStruct((1, d), F32)],
        compiler_params=_params(("arbitrary",)))(x, g, dy, dres)


def loss_head(h, g, target, *, name):
    t, d = h.shape

    def body(x_ref, g_ref, t_ref, loss_ref, dx_ref, dg_ref):
        @pl.when(pl.program_id(0) == 0)
        def _():
            dg_ref[...] = jnp.zeros_like(dg_ref)
            loss_ref[...] = jnp.zeros_like(loss_ref)

        xv, gv = x_ref[...], g_ref[...]
        r = lax.rsqrt(jnp.mean(xv * xv, axis=-1, keepdims=True) + EPS)
        e = xv * r * gv - t_ref[...]
        loss_ref[...] += 0.5 * jnp.sum(jnp.mean(e * e, axis=-1, keepdims=True), axis=0, keepdims=True)
        dyv = e * (1.0 / d)
        dyg = dyv * gv
        dx_ref[...] = r * dyg - xv * (r * r * r) * jnp.mean(dyg * xv, axis=-1, keepdims=True)
        dg_ref[...] += jnp.sum(dyv * xv * r, axis=0, keepdims=True)

    row = pl.BlockSpec((ROWS, d), lambda i: (i, 0))
    vec = pl.BlockSpec((1, d), lambda i: (0, 0))
    return pl.pallas_call(
        body, name=name, grid=(t // ROWS,), in_specs=[row, vec, row],
        out_specs=[pl.BlockSpec((1, 128), lambda i: (0, 0)), row, vec],
        out_shape=[jax.ShapeDtypeStruct((1, 128), F32), jax.ShapeDtypeStruct((t, d), F32),
                   jax.ShapeDtypeStruct((1, d), F32)],
        compiler_params=_params(("arbitrary",)))(h, g, target)


def _tril_mask():
    r = lax.broadcasted_iota(jnp.int32, (C_CHUNK, C_CHUNK), 0)
    c = lax.broadcasted_iota(jnp.int32, (C_CHUNK, C_CHUNK), 1)
    return r >= c


def _layer_norm_parts(v):
    mu = jnp.mean(v, axis=-1, keepdims=True)
    vc = v - mu
    rstd = lax.rsqrt(jnp.mean(vc * vc, axis=-1, keepdims=True) + EPS)
    return vc * rstd, rstd


def gmlp_fwd(zpre, ln_g, ln_b, ws, bs_t, *, name):
    t = zpre.shape[0]
    d = D_MODEL

    def body(zu_ref, zv_ref, g_ref, b_ref, ws_ref, bs_ref, o_ref):
        u = _gelu(zu_ref[...])
        vhat, _ = _layer_norm_parts(_gelu(zv_ref[...]))
        vln = (vhat * g_ref[...] + b_ref[...]).astype(BF16)
        mask = _tril_mask()
        for gi in range(C_GROUPS):
            sl = slice(gi * C_GROUP_DIM, (gi + 1) * C_GROUP_DIM)
            w = jnp.where(mask, ws_ref[gi], 0.0).astype(BF16)
            mixed = _dot(w, vln[:, sl], NN) + bs_ref[:, gi:gi + 1]
            o_ref[:, sl] = (u[:, sl] * mixed).astype(BF16)

    vec = pl.BlockSpec((1, d), lambda i: (0, 0))
    return pl.pallas_call(
        body, name=name, grid=(t // C_CHUNK,),
        in_specs=[pl.BlockSpec((C_CHUNK, d), lambda i: (i, 0)), pl.BlockSpec((C_CHUNK, d), lambda i: (i, 1)),
                  vec, vec, pl.BlockSpec((C_GROUPS, C_CHUNK, C_CHUNK), lambda i: (0, 0, 0)),
                  pl.BlockSpec((C_CHUNK, 128), lambda i: (0, 0))],
        out_specs=pl.BlockSpec((C_CHUNK, d), lambda i: (i, 0)),
        out_shape=jax.ShapeDtypeStruct((t, d), BF16), compiler_params=_params(("parallel",)))(
            zpre, zpre, ln_g, ln_b, ws, bs_t)


def gmlp_bwd(zpre, dgated, ln_g, ln_b, ws, bs_t, *, name):
    t = zpre.shape[0]
    d = D_MODEL

    def body(zu_ref, zv_ref, dg_ref, g_ref, b_ref, ws_ref, bs_ref, dz_ref, dws_ref, dbs_ref, dlg_ref, dlb_ref):
        @pl.when(pl.program_id(0) == 0)
        def _():
            dws_ref[...] = jnp.zeros_like(dws_ref)
            dbs_ref[...] = jnp.zeros_like(dbs_ref)
            dlg_ref[...] = jnp.zeros_like(dlg_ref)
            dlb_ref[...] = jnp.zeros_like(dlb_ref)

        zu, zv = zu_ref[...], zv_ref[...]
        u = _gelu(zu)
        vhat, rstd = _layer_norm_parts(_gelu(zv))
        gam = g_ref[...]
        vln = (vhat * gam + b_ref[...]).astype(BF16)
        dgt = dg_ref[...].astype(F32)
        mask = _tril_mask()
        lane = lax.broadcasted_iota(jnp.int32, (C_CHUNK, 128), 1)
        dbs = jnp.zeros((C_CHUNK, 128), F32)
        du_parts, dvln_parts = [], []
        for gi in range(C_GROUPS):
            sl = slice(gi * C_GROUP_DIM, (gi + 1) * C_GROUP_DIM)
            w = jnp.where(mask, ws_ref[gi], 0.0).astype(BF16)
            mixed = _dot(w, vln[:, sl], NN) + bs_ref[:, gi:gi + 1]
            du_parts.append(dgt[:, sl] * mixed)
            dmixed = dgt[:, sl] * u[:, sl]
            dmb = dmixed.astype(BF16)
            dws_ref[gi] += jnp.where(mask, _dot(dmb, vln[:, sl], NT), 0.0)
            dbs = dbs + jnp.where(lane == gi, jnp.sum(dmixed, axis=-1, keepdims=True), 0.0)
            dvln_parts.append(_dot(w, dmb, TN))
        dbs_ref[...] += dbs
        du = jnp.concatenate(du_parts, axis=-1)
        dvln = jnp.concatenate(dvln_parts, axis=-1)
        dlg_ref[...] += jnp.sum(dvln * vhat, axis=0, keepdims=True)
        dlb_ref[...] += jnp.sum(dvln, axis=0, keepdims=True)
        dvhat = dvln * gam
        dv = rstd * (dvhat - jnp.mean(dvhat, axis=-1, keepdims=True)
                     - vhat * jnp.mean(dvhat * vhat, axis=-1, keepdims=True))
        dz_ref[:, :d] = (du * _dgelu(zu)).astype(BF16)
        dz_ref[:, d:] = (dv * _dgelu(zv)).astype(BF16)

    vec = pl.BlockSpec((1, d), lambda i: (0, 0))
    wsp = pl.BlockSpec((C_GROUPS, C_CHUNK, C_CHUNK), lambda i: (0, 0, 0))
    bsp = pl.BlockSpec((C_CHUNK, 128), lambda i: (0, 0))
    return pl.pallas_call(
        body, name=name, grid=(t // C_CHUNK,),
        in_specs=[pl.BlockSpec((C_CHUNK, d), lambda i: (i, 0)), pl.BlockSpec((C_CHUNK, d), lambda i: (i, 1)),
                  pl.BlockSpec((C_CHUNK, d), lambda i: (i, 0)), vec, vec, wsp, bsp],
        out_specs=[pl.BlockSpec((C_CHUNK, 2 * d), lambda i: (i, 0)), wsp, bsp, vec, vec],
        out_shape=[jax.ShapeDtypeStruct((t, 2 * d), BF16), jax.ShapeDtypeStruct((C_GROUPS, C_CHUNK, C_CHUNK), F32),
                   jax.ShapeDtypeStruct((C_CHUNK, 128), F32), jax.ShapeDtypeStruct((1, d), F32),
                   jax.ShapeDtypeStruct((1, d), F32)],
        compiler_params=_params(("arbitrary",)))(zpre, zpre, dgated, ln_g, ln_b, ws, bs_t)


ATT_SCALE = A_HEAD_DIM ** -0.5
PAIRS = A_HEADS // 2
PAIRS_PER_KV = PAIRS // A_KV_HEADS


def _att_padded(tile):
    lo = lax.broadcasted_iota(jnp.int32, tile.shape, 1) < A_HEAD_DIM
    rolled = pltpu.roll(tile, A_HEAD_DIM, 1)
    zero = jnp.zeros_like(tile)
    return {(0, 0): jnp.where(lo, tile, zero).astype(BF16), (0, 1): jnp.where(lo, zero, rolled).astype(BF16),
            (1, 0): jnp.where(lo, rolled, zero).astype(BF16), (1, 1): jnp.where(lo, zero, tile).astype(BF16)}


def _att_valid(n):
    r = lax.broadcasted_iota(jnp.int32, (WINDOW, 2 * WINDOW), 0)
    c = lax.broadcasted_iota(jnp.int32, (WINDOW, 2 * WINDOW), 1)
    rel = r + WINDOW - c
    return (rel >= 0) & (rel < WINDOW) & ((c >= WINDOW) | (n > 0))


def _att_probs(qp, kpad, sink, valid):
    s = jnp.where(valid, _dot(qp, kpad, NT), NEG_INF)
    m = jnp.maximum(jnp.max(s, axis=-1, keepdims=True), sink)
    p = jnp.exp(s - m)
    e_sink = jnp.exp(sink - m)
    inv = 1.0 / (jnp.sum(p, axis=-1, keepdims=True) + e_sink)
    return p * inv, e_sink * inv


def _att_specs(t):
    return [pl.BlockSpec((WINDOW, A_Q), lambda n: (n, 0)),
            pl.BlockSpec((WINDOW, 2 * A_KV), lambda n: (n, COL_KV // (2 * A_KV))),
            pl.BlockSpec((WINDOW, 2 * A_KV), lambda n: (jnp.maximum(n - 1, 0), COL_KV // (2 * A_KV))),
            pl.BlockSpec((1, 128), lambda n: (0, 0))]


def att_fwd(proj, sinks, *, name):
    t = proj.shape[0]

    def body(q_ref, kvc_ref, kvp_ref, s_ref, o_ref):
        n = pl.program_id(0)
        kv = jnp.concatenate([kvp_ref[...], kvc_ref[...]], axis=0)
        kpad, vpad = _att_padded(kv[:, :128]), _att_padded(kv[:, 128:])
        valid = _att_valid(n)
        for j in range(PAIRS):
            qp = (q_ref[:, j * 128:(j + 1) * 128] * ATT_SCALE).astype(BF16)
            acc = jnp.zeros((WINDOW, 128), F32)
            for half in range(2):
                key = (j // PAIRS_PER_KV, half)
                h = 2 * j + half
                w, _ = _att_probs(qp, kpad[key], s_ref[:, h:h + 1], valid)
                acc = acc + _dot(w.astype(BF16), vpad[key], NN)
            o_ref[:, j * 128:(j + 1) * 128] = acc.astype(BF16)

    return pl.pallas_call(
        body, name=name, grid=(t // WINDOW,), in_specs=_att_specs(t),
        out_specs=pl.BlockSpec((WINDOW, A_Q), lambda n: (n, 0)),
        out_shape=jax.ShapeDtypeStruct((t, A_Q), BF16), compiler_params=_params(("parallel",)))(
            proj, proj, proj, sinks)


def att_bwd(proj, sinks, dout, *, name):
    t = proj.shape[0]

    def body(q_ref, kvc_ref, kvp_ref, s_ref, do_ref, dq_ref, dkc_ref, dkp_ref, ds_ref):
        n = pl.program_id(0)

        @pl.when(n == 0)
        def _():
            ds_ref[...] = jnp.zeros_like(ds_ref)

        kv = jnp.concatenate([kvp_ref[...], kvc_ref[...]], axis=0)
        kpad, vpad = _att_padded(kv[:, :128]), _att_padded(kv[:, 128:])
        valid = _att_valid(n)
        lane = lax.broadcasted_iota(jnp.int32, (1, 128), 1)
        dsink = jnp.zeros((1, 128), F32)
        zero = jnp.zeros((2 * WINDOW, 128), F32)
        dk_acc = {key: zero for key in kpad}
        dv_acc = {key: zero for key in kpad}
        for j in range(PAIRS):
            qp = (q_ref[:, j * 128:(j + 1) * 128] * ATT_SCALE).astype(BF16)
            dop = do_ref[:, j * 128:(j + 1) * 128].astype(BF16)
            dq = jnp.zeros((WINDOW, 128), F32)
            for half in range(2):
                key = (j // PAIRS_PER_KV, half)
                h = 2 * j + half
                w, w_sink = _att_probs(qp, kpad[key], s_ref[:, h:h + 1], valid)
                dw = _dot(dop, vpad[key], NT)
                delta = jnp.sum(w * dw, axis=-1, keepdims=True)
                dsc = (w * (dw - delta)).astype(BF16)
                dsink = dsink + jnp.where(lane == h, -jnp.sum(w_sink * delta, axis=0, keepdims=True), 0.0)
                dq = dq + _dot(dsc, kpad[key], NN)
                dk_acc[key] = dk_acc[key] + _dot(dsc, qp, TN)
                dv_acc[key] = dv_acc[key] + _dot(w.astype(BF16), dop, TN)
            dq_ref[:, j * 128:(j + 1) * 128] = (dq * ATT_SCALE).astype(BF16)
        ds_ref[...] += dsink
        lo = lax.broadcasted_iota(jnp.int32, (2 * WINDOW, 128), 1) < A_HEAD_DIM

        def tile(acc):
            return jnp.where(lo, acc[(0, 0)] + pltpu.roll(acc[(0, 1)], A_HEAD_DIM, 1),
                             pltpu.roll(acc[(1, 0)], A_HEAD_DIM, 1) + acc[(1, 1)])

        dkv = jnp.concatenate([tile(dk_acc), tile(dv_acc)], axis=1)
        dkp_ref[...] = dkv[:WINDOW]
        dkc_ref[...] = dkv[WINDOW:]

    kvo = pl.BlockSpec((WINDOW, 2 * A_KV), lambda n: (n, 0))
    return pl.pallas_call(
        body, name=name, grid=(t // WINDOW,),
        in_specs=_att_specs(t) + [pl.BlockSpec((WINDOW, A_Q), lambda n: (n, 0))],
        out_specs=[pl.BlockSpec((WINDOW, A_Q), lambda n: (n, 0)), kvo, kvo, pl.BlockSpec((1, 128), lambda n: (0, 0))],
        out_shape=[jax.ShapeDtypeStruct((t, A_Q), BF16), jax.ShapeDtypeStruct((t, 2 * A_KV), F32),
                   jax.ShapeDtypeStruct((t, 2 * A_KV), F32), jax.ShapeDtypeStruct((1, 128), F32)],
        compiler_params=_params(("arbitrary",)))(proj, proj, proj, sinks, dout)


QK_SCALE = B_HEAD_DIM ** -0.5
PREP_COLS = 256
PREP_NCB = 3 * B_W // PREP_COLS
HALO = 8


def _roll_rows(x, shift):
    n = x.shape[0]
    return x if shift % n == 0 else pltpu.roll(x, shift % n, 0)


def _conv_taps(xe, w):
    xs = [_roll_rows(xe, CONV_K - 1 - i) for i in range(CONV_K)]
    c = w[0:1] * xs[0]
    for i in range(1, CONV_K):
        c = c + w[i:i + 1] * xs[i]
    return xs, c


def dprep_fwd(proj, conv_w, *, name):
    t = proj.shape[0]
    tt = ROWS
    col0 = COL_QKVB // PREP_COLS

    def body(x_ref, h_ref, w_ref, o_ref):
        cb, n = pl.program_id(0), pl.program_id(1)
        halo = jnp.where(n > 0, h_ref[...], 0.0)
        xe = jnp.concatenate([halo, x_ref[...]], axis=0)
        _, c = _conv_taps(xe, w_ref[...])
        y = _silu(c)[HALO:]
        parts = []
        for hh in range(PREP_COLS // B_HEAD_DIM):
            yh = y[:, hh * B_HEAD_DIM:(hh + 1) * B_HEAD_DIM]
            parts.append(yh * lax.rsqrt(jnp.sum(yh * yh, axis=-1, keepdims=True) + EPS))
        nrm = jnp.concatenate(parts, axis=-1)
        o_ref[...] = jnp.where(cb < 4, nrm * QK_SCALE, jnp.where(cb < 8, nrm, y))

    return pl.pallas_call(
        body, name=name, grid=(PREP_NCB, t // tt),
        in_specs=[pl.BlockSpec((tt, PREP_COLS), lambda cb, n: (n, col0 + cb)),
                  pl.BlockSpec((HALO, PREP_COLS), lambda cb, n: (jnp.maximum(n * (tt // HALO) - 1, 0), col0 + cb)),
                  pl.BlockSpec((CONV_K, PREP_COLS), lambda cb, n: (0, cb))],
        out_specs=pl.BlockSpec((tt, PREP_COLS), lambda cb, n: (n, cb)),
        out_shape=jax.ShapeDtypeStruct((t, 3 * B_W), F32), compiler_params=_params(("parallel", "parallel")))(
            proj, proj, conv_w)


def dprep_bwd(proj, conv_w, dqkvn, *, name):
    t = proj.shape[0]
    tt = ROWS
    nb = t // tt
    col0 = COL_QKVB // PREP_COLS
    n8 = t // HALO

    def body(xc_ref, xb_ref, xa_ref, dc_ref, da_ref, w_ref, dx_ref, dw_ref):
        cb, n = pl.program_id(0), pl.program_id(1)

        @pl.when(n == 0)
        def _():
            dw_ref[...] = jnp.zeros_like(dw_ref)

        w = w_ref[...]
        xe = jnp.concatenate([jnp.where(n > 0, xb_ref[...], 0.0), xc_ref[...], xa_ref[...]], axis=0)
        xs, c = _conv_taps(xe, w)
        sg = _sigmoid(c)
        y = c * sg
        dout = jnp.concatenate([jnp.zeros((HALO, PREP_COLS), F32), dc_ref[...],
                                jnp.where(n < nb - 1, da_ref[...], 0.0)], axis=0)
        dsc = jnp.where(cb < 4, QK_SCALE, 1.0)
        parts = []
        for hh in range(PREP_COLS // B_HEAD_DIM):
            sl = slice(hh * B_HEAD_DIM, (hh + 1) * B_HEAD_DIM)
            yh, doh = y[:, sl], dout[:, sl] * dsc
            r = lax.rsqrt(jnp.sum(yh * yh, axis=-1, keepdims=True) + EPS)
            parts.append(doh * r - yh * (r * r * r) * jnp.sum(doh * yh, axis=-1, keepdims=True))
        dy = jnp.where(cb < 8, jnp.concatenate(parts, axis=-1), dout)
        dcv = dy * sg * (1.0 + c * (1.0 - sg))
        dxe = w[CONV_K - 1:CONV_K] * dcv
        for i in range(CONV_K - 1):
            dxe = dxe + w[i:i + 1] * _roll_rows(dcv, -(CONV_K - 1 - i))
        dx_ref[...] = dxe[HALO:HALO + tt].astype(BF16)
        for i in range(CONV_K):
            dw_ref[i:i + 1, :] += jnp.sum((dcv * xs[i])[HALO:HALO + tt], axis=0, keepdims=True)

    def after(n):
        return jnp.minimum((n + 1) * (tt // HALO), n8 - 1)

    return pl.pallas_call(
        body, name=name, grid=(PREP_NCB, nb),
        in_specs=[pl.BlockSpec((tt, PREP_COLS), lambda cb, n: (n, col0 + cb)),
                  pl.BlockSpec((HALO, PREP_COLS), lambda cb, n: (jnp.maximum(n * (tt // HALO) - 1, 0), col0 + cb)),
                  pl.BlockSpec((HALO, PREP_COLS), lambda cb, n: (after(n), col0 + cb)),
                  pl.BlockSpec((tt, PREP_COLS), lambda cb, n: (n, cb)),
                  pl.BlockSpec((HALO, PREP_COLS), lambda cb, n: (after(n), cb)),
                  pl.BlockSpec((CONV_K, PREP_COLS), lambda cb, n: (0, cb))],
        out_specs=[pl.BlockSpec((tt, PREP_COLS), lambda cb, n: (n, cb)),
                   pl.BlockSpec((CONV_K, PREP_COLS), lambda cb, n: (0, cb))],
        out_shape=[jax.ShapeDtypeStruct((t, 3 * B_W), BF16), jax.ShapeDtypeStruct((CONV_K, 3 * B_W), F32)],
        compiler_params=_params(("parallel", "arbitrary")))(proj, proj, proj, dqkvn, dqkvn, conv_w)


def _softplus(z):
    return jnp.maximum(z, 0.0) + jnp.log(1.0 + jnp.exp(-jnp.abs(z)))


def gates_fwd(proj, alog_pad, dtb_pad, *, name):
    t = proj.shape[0]

    def body(x_ref, a_ref, b_ref, o_ref):
        raw = x_ref[...]
        lane = lax.broadcasted_iota(jnp.int32, raw.shape, 1)
        g = -jnp.exp(a_ref[...]) * _softplus(raw + b_ref[...])
        o_ref[...] = jnp.where(lane < B_HEADS, _sigmoid(raw), jnp.where(lane < 2 * B_HEADS, g, 0.0))

    vec = pl.BlockSpec((1, 128), lambda n: (0, 0))
    return pl.pallas_call(
        body, name=name, grid=(t // ROWS,),
        in_specs=[pl.BlockSpec((ROWS, 128), lambda n: (n, COL_GATE // 128)), vec, vec],
        out_specs=pl.BlockSpec((ROWS, 128), lambda n: (n, 0)),
        out_shape=jax.ShapeDtypeStruct((t, 128), F32), compiler_params=_params(("parallel",)))(
            proj, alog_pad, dtb_pad)


def gates_bwd(proj, alog_pad, dtb_pad, dgates, *, name):
    t = proj.shape[0]

    def body(x_ref, a_ref, b_ref, dg_ref, dx_ref, da_ref, db_ref):
        @pl.when(pl.program_id(0) == 0)
        def _():
            da_ref[...] = jnp.zeros_like(da_ref)
            db_ref[...] = jnp.zeros_like(db_ref)

        raw, dgt = x_ref[...], dg_ref[...]
        lane = lax.broadcasted_iota(jnp.int32, raw.shape, 1)
        is_beta, is_g = lane < B_HEADS, (lane >= B_HEADS) & (lane < 2 * B_HEADS)
        beta = _sigmoid(raw)
        z = raw + b_ref[...]
        neg_a = -jnp.exp(a_ref[...])
        d_z = jnp.where(is_g, dgt * neg_a * _sigmoid(z), 0.0)
        dx_ref[...] = jnp.where(is_beta, dgt * beta * (1.0 - beta), d_z).astype(BF16)
        db_ref[...] += jnp.sum(d_z, axis=0, keepdims=True)
        da_ref[...] += jnp.sum(jnp.where(is_g, dgt * neg_a * _softplus(z), 0.0), axis=0, keepdims=True)

    vec = pl.BlockSpec((1, 128), lambda n: (0, 0))
    row = pl.BlockSpec((ROWS, 128), lambda n: (n, 0))
    return pl.pallas_call(
        body, name=name, grid=(t // ROWS,),
        in_specs=[pl.BlockSpec((ROWS, 128), lambda n: (n, COL_GATE // 128)), vec, vec, row],
        out_specs=[row, vec, vec],
        out_shape=[jax.ShapeDtypeStruct((t, 128), BF16), jax.ShapeDtypeStruct((1, 128), F32),
                   jax.ShapeDtypeStruct((1, 128), F32)],
        compiler_params=_params(("arbitrary",)))(proj, alog_pad, dtb_pad, dgates)


def _split2(a):
    hi = a.astype(BF16)
    return hi, (a - hi.astype(F32)).astype(BF16)


def _dotp(a, b, dims, passes):
    if passes == 1:
        return _dot(a.astype(BF16), b.astype(BF16), dims)
    ah, al = _split2(a)
    bh, bl = _split2(b)
    return _dot(ah, bh, dims) + (_dot(ah, bl, dims) + _dot(al, bh, dims))


_GRAD_DIMS = {NN: ((NT, False), (TN, False)), NT: ((NN, False), (TN, True)), TN: ((NT, True), (NN, False))}


def _make_mm(dims, passes):
    (da_dims, da_swap), (db_dims, db_swap) = _GRAD_DIMS[dims]

    @jax.custom_vjp
    def mm(a, b):
        return _dotp(a, b, dims, passes)

    def fwd(a, b):
        return _dotp(a, b, dims, passes), (a, b)

    def bwd(saved, ct):
        a, b = saved
        da = _dotp(b, ct, da_dims, passes) if da_swap else _dotp(ct, b, da_dims, passes)
        db = _dotp(ct, a, db_dims, passes) if db_swap else _dotp(a, ct, db_dims, passes)
        return da, db

    mm.defvjp(fwd, bwd)
    return mm


MM1 = {d: _make_mm(d, 1) for d in (NN, NT, TN)}
MM3 = {d: _make_mm(d, 3) for d in (NN, NT, TN)}


def _tri_ones(lower):
    r = lax.broadcasted_iota(jnp.int32, (DN_CHUNK, DN_CHUNK), 0)
    c = lax.broadcasted_iota(jnp.int32, (DN_CHUNK, DN_CHUNK), 1)
    return (r >= c if lower else r <= c).astype(BF16)


def _tri_sum(x, lower):
    tri = _tri_ones(lower)
    hi = x.astype(BF16)
    r1 = x - hi.astype(F32)
    mid = r1.astype(BF16)
    lo = (r1 - mid.astype(F32)).astype(BF16)
    return _dot(tri, hi, NN) + (_dot(tri, mid, NN) + _dot(tri, lo, NN))


def _delta_chunk(s0, q, k, v, beta, gam_c, gam_r):
    c = DN_CHUNK
    r = lax.broadcasted_iota(jnp.int32, (c, c), 0)
    cc = lax.broadcasted_iota(jnp.int32, (c, c), 1)
    incl, strict = r >= cc, r > cc
    eye = (r == cc).astype(F32)
    decay = jnp.exp(jnp.where(incl, gam_c - gam_r, NEG_INF))
    g_last = gam_c[:, c - 1:c, :]
    e_gam, e_rest, e_last = jnp.exp(gam_c), jnp.exp(g_last - gam_c), jnp.exp(g_last)
    a_neg = -jnp.where(strict, beta * MM1[NT](k, k) * decay, 0.0)
    inv = eye + a_neg
    pw = a_neg
    for _ in range(5):
        pw = MM3[NN](pw, pw)
        inv = inv + MM3[NN](inv, pw)
    uw = MM3[NN](inv, jnp.concatenate([v * beta, k * (beta * e_gam)], axis=-1))
    u, w = uw[..., :B_HEAD_DIM], uw[..., B_HEAD_DIM:]
    qk = MM1[NT](q, k) * decay
    v_new = u - MM1[NN](w, s0)
    o = MM1[NN](q * e_gam, s0) + MM1[NN](qk, v_new)
    s1 = s0 * e_last + MM1[TN](k * e_rest, v_new)
    return s1, o


def _delta_operands(q_ref, k_ref, v_ref, gt):
    heads = lambda ref: jnp.stack([ref[:, h * B_HEAD_DIM:(h + 1) * B_HEAD_DIM] for h in range(B_HEADS)])
    gam = _tri_sum(gt, True)
    gam_t = gam.T
    beta = jnp.stack([gt[:, h:h + 1] for h in range(B_HEADS)])
    gam_c = jnp.stack([gam[:, B_HEADS + h:B_HEADS + h + 1] for h in range(B_HEADS)])
    gam_r = jnp.stack([gam_t[B_HEADS + h:B_HEADS + h + 1, :] for h in range(B_HEADS)])
    return heads(q_ref), heads(k_ref), heads(v_ref), beta, gam_c, gam_r


def delta_fwd(qkvn, gates, *, name):
    t = qkvn.shape[0]
    nc = t // DN_CHUNK

    def body(q_ref, k_ref, v_ref, g_ref, o_ref, ss_ref, state):
        @pl.when(pl.program_id(0) == 0)
        def _():
            state[...] = jnp.zeros_like(state)

        s0 = state[...]
        ss_ref[...] = s0
        s1, o = _delta_chunk(s0, *_delta_operands(q_ref, k_ref, v_ref, g_ref[...]))
        state[...] = s1
        for h in range(B_HEADS):
            o_ref[:, h * B_HEAD_DIM:(h + 1) * B_HEAD_DIM] = o[h]

    blk = lambda j: pl.BlockSpec((DN_CHUNK, B_W), lambda n: (n, j))
    return pl.pallas_call(
        body, name=name, grid=(nc,),
        in_specs=[blk(0), blk(1), blk(2), pl.BlockSpec((DN_CHUNK, 128), lambda n: (n, 0))],
        out_specs=[blk(0), pl.BlockSpec((None, B_HEADS, B_HEAD_DIM, B_HEAD_DIM), lambda n: (n, 0, 0, 0))],
        out_shape=[jax.ShapeDtypeStruct((t, B_W), F32),
                   jax.ShapeDtypeStruct((nc, B_HEADS, B_HEAD_DIM, B_HEAD_DIM), F32)],
        scratch_shapes=[pltpu.VMEM((B_HEADS, B_HEAD_DIM, B_HEAD_DIM), F32)],
        compiler_params=_params(("arbitrary",)))(qkvn, qkvn, qkvn, gates)


def delta_bwd(qkvn, gates, ssave, do, *, name):
    t = qkvn.shape[0]
    nc = t // DN_CHUNK

    def body(q_ref, k_ref, v_ref, g_ref, ss_ref, do_ref, dx_ref, dg_ref, dstate):
        @pl.when(pl.program_id(0) == 0)
        def _():
            dstate[...] = jnp.zeros_like(dstate)

        lane = lax.broadcasted_iota(jnp.int32, (DN_CHUNK, 128), 1)
        row = lax.broadcasted_iota(jnp.int32, (128, DN_CHUNK), 0)
        dbeta_all = jnp.zeros((DN_CHUNK, 128), F32)
        dgam_c_all = jnp.zeros((DN_CHUNK, 128), F32)
        dgam_r_all = jnp.zeros((128, DN_CHUNK), F32)
        _, vjp = jax.vjp(_delta_chunk, ss_ref[...], *_delta_operands(q_ref, k_ref, v_ref, g_ref[...]))
        do = jnp.stack([do_ref[:, h * B_HEAD_DIM:(h + 1) * B_HEAD_DIM] for h in range(B_HEADS)])
        ds0, dq, dk, dv, dbeta, dgam_c, dgam_r = vjp((dstate[...], do))
        dstate[...] = ds0
        for h in range(B_HEADS):
            dx_ref[:, h * B_HEAD_DIM:(h + 1) * B_HEAD_DIM] = dq[h]
            dx_ref[:, B_W + h * B_HEAD_DIM:B_W + (h + 1) * B_HEAD_DIM] = dk[h]
            dx_ref[:, 2 * B_W + h * B_HEAD_DIM:2 * B_W + (h + 1) * B_HEAD_DIM] = dv[h]
            dbeta_all = dbeta_all + jnp.where(lane == h, dbeta[h], 0.0)
            dgam_c_all = dgam_c_all + jnp.where(lane == B_HEADS + h, dgam_c[h], 0.0)
            dgam_r_all = dgam_r_all + jnp.where(row == B_HEADS + h, dgam_r[h], 0.0)
        dg_ref[...] = dbeta_all + _tri_sum(dgam_c_all + dgam_r_all.T, False)

    blk = lambda j: pl.BlockSpec((DN_CHUNK, B_W), lambda n: (nc - 1 - n, j))
    gsp = pl.BlockSpec((DN_CHUNK, 128), lambda n: (nc - 1 - n, 0))
    return pl.pallas_call(
        body, name=name, grid=(nc,),
        in_specs=[blk(0), blk(1), blk(2), gsp,
                  pl.BlockSpec((None, B_HEADS, B_HEAD_DIM, B_HEAD_DIM), lambda n: (nc - 1 - n, 0, 0, 0)), blk(0)],
        out_specs=[pl.BlockSpec((DN_CHUNK, 3 * B_W), lambda n: (nc - 1 - n, 0)), gsp],
        out_shape=[jax.ShapeDtypeStruct((t, 3 * B_W), F32), jax.ShapeDtypeStruct((t, 128), F32)],
        scratch_shapes=[pltpu.VMEM((B_HEADS, B_HEAD_DIM, B_HEAD_DIM), F32)],
        compiler_params=_params(("arbitrary",)))(qkvn, qkvn, qkvn, gates, ssave, do)


def gnorm_fwd(o, proj, onorm, *, name):
    t = o.shape[0]

    def body(o_ref, z_ref, w_ref, out_ref):
        ov = o_ref[...]
        r = lax.rsqrt(jnp.mean(ov * ov, axis=-1, keepdims=True) + EPS)
        out_ref[...] = (ov * r * w_ref[...] * _silu(z_ref[...])).astype(BF16)

    blk = pl.BlockSpec((ROWS, B_HEAD_DIM), lambda n, h: (n, h))
    return pl.pallas_call(
        body, name=name, grid=(t // ROWS, B_HEADS),
        in_specs=[blk, pl.BlockSpec((ROWS, B_HEAD_DIM), lambda n, h: (n, COL_Z // B_HEAD_DIM + h)),
                  pl.BlockSpec((1, B_HEAD_DIM), lambda n, h: (0, 0))],
        out_specs=blk, out_shape=jax.ShapeDtypeStruct((t, B_W), BF16),
        compiler_params=_params(("parallel", "parallel")))(o, proj, onorm)


def gnorm_bwd(o, proj, onorm, dout, *, dcol0, name):
    t = o.shape[0]

    def body(o_ref, z_ref, w_ref, d_ref, do_ref, dz_ref, dw_ref):
        @pl.when((pl.program_id(0) == 0) & (pl.program_id(1) == 0))
        def _():
            dw_ref[...] = jnp.zeros_like(dw_ref)

        ov, zv, wv, dv = o_ref[...], z_ref[...], w_ref[...], d_ref[...].astype(F32)
        r = lax.rsqrt(jnp.mean(ov * ov, axis=-1, keepdims=True) + EPS)
        nrm = ov * r
        dz_ref[...] = (dv * nrm * wv * _dsilu(zv)).astype(BF16)
        da = dv * _silu(zv)
        dw_ref[...] += jnp.sum(da * nrm, axis=0, keepdims=True)
        dn = da * wv
        do_ref[...] = r * dn - ov * (r * r * r) * jnp.mean(dn * ov, axis=-1, keepdims=True)

    blk = pl.BlockSpec((ROWS, B_HEAD_DIM), lambda n, h: (n, h))
    vec = pl.BlockSpec((1, B_HEAD_DIM), lambda n, h: (0, 0))
    return pl.pallas_call(
        body, name=name, grid=(t // ROWS, B_HEADS),
        in_specs=[blk, pl.BlockSpec((ROWS, B_HEAD_DIM), lambda n, h: (n, COL_Z // B_HEAD_DIM + h)), vec,
                  pl.BlockSpec((ROWS, B_HEAD_DIM), lambda n, h: (n, dcol0 // B_HEAD_DIM + h))],
        out_specs=[blk, blk, vec],
        out_shape=[jax.ShapeDtypeStruct((t, B_W), F32), jax.ShapeDtypeStruct((t, B_W), BF16),
                   jax.ShapeDtypeStruct((1, B_HEAD_DIM), F32)],
        compiler_params=_params(("arbitrary", "arbitrary")))(o, proj, onorm, dout)


def _ffn_fwd(h, norm_g, wg, wu, wd, tm, tag):
    hn = rms_fwd(h, norm_g, name=f"ffn{tag}_norm")
    gate, up, act = mm_gate_up(hn, wg, wu, tm=tm, tn=1408, tk=512, name=f"ffn{tag}_gate_up")
    h_out = mm_nn(act, wd, tm=tm, tn=512, tk=1408, out_dtype=F32, res=h, name=f"ffn{tag}_down")
    return h_out, (hn, gate, up, act)


def _ffn_bwd(dh, h, norm_g, wg, wu, wd, saved, tm, tag, emit):
    hn, gate, up, act = saved
    dwd = mm_tn(act, dh, shards=1, tm=tm, tn=512, tk=1408, out_dtype=BF16, name=f"ffn{tag}_dwd")[0]
    dgate, dup = mm_down_bwd(dh, wd, gate, up, tm=tm, tn=512, tk=2048, name=f"ffn{tag}_dact")
    dwg = mm_tn(hn, dgate, shards=N_SHARD, tm=tm, tn=1408, tk=1024, out_dtype=BF16, name=f"ffn{tag}_dwg")
    dwu = mm_tn(hn, dup, shards=N_SHARD, tm=tm, tn=1408, tk=1024, out_dtype=BF16, name=f"ffn{tag}_dwu")
    emit(f"ffn{tag}", {"gate": dwg, "up": dwu, "down": dwd})
    dhn = mm_nt(dgate, wg, tm=tm, tn=512, tk=1408, out_dtype=F32, name=f"ffn{tag}_dhn_g")
    dhn = mm_nt(dup, wu, tm=tm, tn=512, tk=1408, out_dtype=F32, res=dhn, name=f"ffn{tag}_dhn_u")
    dh_in, dnorm = rms_bwd(h, norm_g, dhn, dh, name=f"ffn{tag}_dnorm")
    return dh_in, dnorm


def _local_step(x, target, w, get, emit):
    t = x.shape[0]
    tm = min(1024, t)
    g = {}

    hn0 = rms_fwd(x, w["even_norm"], name="l0_norm")
    w.update(get("even_in", hn0))
    proj = mm_nn(hn0, w["even_w_in"], tm=tm, tn=512, tk=2048, out_dtype=F32, name="l0_w_in")
    out_a = att_fwd(proj, w["sinks"], name="l0_att")
    qkvn = dprep_fwd(proj, w["even_conv"], name="l0_prep")
    gates = gates_fwd(proj, w["a_log"], w["dt_bias"], name="l0_gates")
    o_delta, ssave = delta_fwd(qkvn, gates, name="l0_delta")
    out_b = gnorm_fwd(o_delta, proj, w["onorm"], name="l0_gnorm")
    mix0 = jnp.concatenate([out_a, out_b], axis=-1)
    w.update(get("even_out", mix0))
    h1 = mm_nn(mix0, w["even_w_out"], tm=tm, tn=512, tk=2048, out_dtype=F32, res=x, name="l0_w_out")
    f0 = get("ffn0", h1)
    h2, ffn0 = _ffn_fwd(h1, w["ffn_norm"][0:1], f0["gate"], f0["up"], f0["down"], tm, 0)
    hn2 = rms_fwd(h2, w["odd_norm"], name="l1_norm")
    w.update(get("odd", hn2))
    zpre = mm_nn(hn2, w["odd_w_in"], tm=tm, tn=1024, tk=2048, out_dtype=F32, name="l1_w_in")
    gated = gmlp_fwd(zpre, w["odd_ln_g"], w["odd_ln_b"], w["odd_w_s"], w["odd_b_s"], name="l1_gmlp")
    h3 = mm_nn(gated, w["odd_w_out"], tm=tm, tn=512, tk=2048, out_dtype=F32, res=h2, name="l1_w_out")
    f1 = get("ffn1", h3)
    h4, ffn1 = _ffn_fwd(h3, w["ffn_norm"][1:2], f1["gate"], f1["up"], f1["down"], tm, 1)
    loss, dh4, g["final_norm"] = loss_head(h4, w["final_norm"], target, name="loss_head")

    dh3, dn1 = _ffn_bwd(dh4, h3, w["ffn_norm"][1:2], f1["gate"], f1["up"], f1["down"], ffn1, tm, 1, emit)
    dw_out_o = mm_tn(gated, dh3, shards=1, tm=tm, tn=512, tk=1024, out_dtype=BF16, name="l1_dw_out")[0]
    dgated = mm_nt(dh3, w["odd_w_out"], tm=tm, tn=512, tk=2048, out_dtype=BF16, name="l1_dgated")
    dzpre, g["odd_w_s"], g["odd_b_s"], g["odd_ln_g"], g["odd_ln_b"] = gmlp_bwd(
        zpre, dgated, w["odd_ln_g"], w["odd_ln_b"], w["odd_w_s"], w["odd_b_s"], name="l1_dgmlp")
    dw_in_o = mm_tn(hn2, dzpre, shards=N_SHARD, tm=tm, tn=1024, tk=1024, out_dtype=BF16, name="l1_dw_in")
    emit("odd", {"odd_w_in": dw_in_o, "odd_w_out": dw_out_o})
    dhn2 = mm_nt(dzpre, w["odd_w_in"], tm=tm, tn=512, tk=1024, out_dtype=F32, name="l1_dhn")
    dh2, g["odd_norm"] = rms_bwd(h2, w["odd_norm"], dhn2, dh3, name="l1_dnorm")
    dh1, dn0 = _ffn_bwd(dh2, h1, w["ffn_norm"][0:1], f0["gate"], f0["up"], f0["down"], ffn0, tm, 0, emit)
    g["ffn_norm"] = jnp.concatenate([dn0, dn1], axis=0)
    dw_out_e = mm_tn(mix0, dh1, shards=1, tm=tm, tn=512, tk=1024, out_dtype=BF16, name="l0_dw_out")[0]
    emit("even_out", {"even_w_out": dw_out_e})
    dmix = mm_nt(dh1, w["even_w_out"], tm=tm, tn=512, tk=2048, out_dtype=F32, name="l0_dmix")
    dq_a, dkv_cur, dkv_prev, g["sinks"] = att_bwd(proj, w["sinks"], dmix, name="l0_datt")
    dkv = dkv_cur + jnp.concatenate([dkv_prev[WINDOW:], jnp.zeros((WINDOW, 2 * A_KV), F32)], axis=0)
    do_delta, dz, g["onorm"] = gnorm_bwd(o_delta, proj, w["onorm"], dmix, dcol0=A_Q, name="l0_dgnorm")
    dqkvn, dgates = delta_bwd(qkvn, gates, ssave, do_delta, name="l0_ddelta")
    dqkv_b, g["even_conv"] = dprep_bwd(proj, w["even_conv"], dqkvn, name="l0_dprep")
    draw, g["a_log"], g["dt_bias"] = gates_bwd(proj, w["a_log"], w["dt_bias"], dgates, name="l0_dgates")
    dproj = jnp.concatenate([dq_a, dkv.astype(BF16), dqkv_b, dz, draw,
                             jnp.zeros((t, EVEN_IN_PAD - COL_GATE - 128), BF16)], axis=-1)
    emit("even_in", {"even_w_in": mm_tn(hn0, dproj, shards=1, tm=tm, tn=512, tk=1024, out_dtype=BF16,
                                        name="l0_dw_in")[0]})
    dhn0 = mm_nt(dproj, w["even_w_in"], tm=tm, tn=512, tk=2816, out_dtype=F32, name="l0_dhn")
    grad_x, g["even_norm"] = rms_bwd(x, w["even_norm"], dhn0, dh1, name="l0_dnorm")
    return loss, grad_x, g


ANY = pl.BlockSpec(memory_space=pl.ANY)
N_DEV = 8


def _place():
    return lax.axis_index("x"), lax.axis_index("y"), lax.axis_index("c")


def _chip_peers(x, y, c):
    return [((1 - x, y, c), 2 * (1 - x) + y), ((x, 1 - y, c), 2 * x + 1 - y), ((1 - x, 1 - y, c), 2 * (1 - x) + 1 - y)]


HBM = pl.BlockSpec(memory_space=pltpu.HBM)
SEM = pl.BlockSpec(memory_space=pltpu.SEMAPHORE)
EFFECT = pltpu.SideEffectType.DATAFLOW_SIDE_EFFECTING
N_PEER = 3


def _gather_plan(srcs, lands, send, recv):
    x, y, c = _place()
    return [pltpu.make_async_remote_copy(src_ref=srcs[i], dst_ref=lands[i].at[2 * x + y], send_sem=send.at[N_PEER * i + k],
                                         recv_sem=recv.at[N_PEER * i + k], device_id=peer, device_id_type=MESH_ID)
            for i in range(len(srcs)) for k, (peer, _) in enumerate(_chip_peers(x, y, c))]


def _scatter_plan(srcs, lands, send, recv):
    x, y, c = _place()
    return [pltpu.make_async_remote_copy(src_ref=srcs[i].at[idx], dst_ref=lands[i].at[k], send_sem=send.at[N_PEER * i + k],
                                         recv_sem=recv.at[N_PEER * i + k], device_id=peer, device_id_type=MESH_ID)
            for i in range(len(srcs)) for k, (peer, idx) in enumerate(_chip_peers(x, y, c))]


def copies_start(plan, srcs, lands, after, *, name):
    n = len(srcs)
    both = list(srcs) + list(lands)

    def body(*refs):
        src_refs, land_refs = refs[:n], refs[n:2 * n]
        send, recv = refs[2 * n + 1], refs[2 * n + 2]
        for cp in plan(src_refs, land_refs, send, recv):
            cp.start()
        refs[-1][...] = jnp.zeros_like(refs[-1])

    res = pl.pallas_call(
        body, name=name,
        out_shape=(pltpu.SemaphoreType.DMA((n * N_PEER,)), pltpu.SemaphoreType.DMA((n * N_PEER,)),
                   *[pltpu.HBM(a.shape, a.dtype) for a in both], jax.ShapeDtypeStruct((8, 128), F32)),
        in_specs=[HBM] * (2 * n) + [ANY],
        out_specs=(SEM, SEM, *[HBM] * (2 * n), pl.BlockSpec(memory_space=pltpu.VMEM)),
        input_output_aliases={i: 2 + i for i in range(2 * n)},
        compiler_params=pltpu.CompilerParams(has_side_effects=EFFECT))(
            *[pltpu.with_memory_space_constraint(a, pltpu.HBM) for a in both], after)
    return {"send": res[0], "recv": res[1], "srcs": list(res[2:2 + n]), "lands": list(res[2 + n:2 + 2 * n]),
            "token": res[-1]}


def copies_wait(plan, started, after, *, name):
    srcs, lands = started["srcs"], started["lands"]
    n = len(srcs)
    both = srcs + lands

    def body(*refs):
        src_refs, land_refs = refs[:n], refs[n:2 * n]
        send, recv = refs[2 * n], refs[2 * n + 1]
        for cp in plan(src_refs, land_refs, send, recv):
            cp.wait_send()
            cp.wait_recv()

    res = pl.pallas_call(
        body, name=name, out_shape=tuple(pltpu.HBM(a.shape, a.dtype) for a in both),
        in_specs=[HBM] * (2 * n) + [SEM, SEM, ANY], out_specs=(HBM,) * (2 * n),
        input_output_aliases={i: i for i in range(2 * n)},
        compiler_params=pltpu.CompilerParams(has_side_effects=EFFECT))(*both, started["send"], started["recv"], after)
    return list(res[:n]), list(res[n:])


def allgather_small(small, *, name):
    def body(small_ref, out_ref, send, recv, loc):
        x, y, c = _place()
        dev = 4 * x + 2 * y + c
        local = pltpu.make_async_copy(small_ref, out_ref.at[dev], loc)
        remote = []
        for r in range(1, N_DEV):
            fx, fy, fc = (r >> 2) & 1, (r >> 1) & 1, r & 1
            peer = (1 - x if fx else x, 1 - y if fy else y, 1 - c if fc else c)
            remote.append(pltpu.make_async_remote_copy(
                src_ref=small_ref, dst_ref=out_ref.at[dev], send_sem=send.at[r - 1], recv_sem=recv.at[r - 1],
                device_id=peer, device_id_type=MESH_ID))
        local.start()
        for cp in remote:
            cp.start()
        for cp in remote:
            cp.wait()
        local.wait()

    return pl.pallas_call(
        body, name=name, in_specs=[ANY], out_specs=ANY,
        out_shape=jax.ShapeDtypeStruct((N_DEV,) + small.shape, small.dtype),
        scratch_shapes=[pltpu.SemaphoreType.DMA((N_DEV - 1,)), pltpu.SemaphoreType.DMA((N_DEV - 1,)),
                        pltpu.SemaphoreType.DMA(())])(small)


def swap_cores(arrs, *, name):
    n = len(arrs)

    def body(*refs):
        ins, outs = refs[:n], refs[n:2 * n]
        send, recv = refs[2 * n:]
        x, y, c = _place()
        copies = [pltpu.make_async_remote_copy(src_ref=ins[i], dst_ref=outs[i], send_sem=send.at[i], recv_sem=recv.at[i],
                                               device_id=(x, y, 1 - c), device_id_type=MESH_ID) for i in range(n)]
        for cp in copies:
            cp.start()
        for cp in copies:
            cp.wait()

    return pl.pallas_call(
        body, name=name, in_specs=[ANY] * n, out_specs=[ANY] * n,
        out_shape=[jax.ShapeDtypeStruct(a.shape, a.dtype) for a in arrs],
        scratch_shapes=[pltpu.SemaphoreType.DMA((n,)), pltpu.SemaphoreType.DMA((n,))])(*arrs)


RED_ROWS = 128


def sum_chips(own, got, *, name):
    r, c = own.shape
    rb = RED_ROWS if r % RED_ROWS == 0 else r

    def body(o_ref, a_ref, b_ref, c_ref, out_ref):
        out_ref[...] = ((o_ref[...].astype(F32) + a_ref[...].astype(F32)) + b_ref[...].astype(F32)) + c_ref[...].astype(F32)

    gk = lambda k: pl.BlockSpec((None, rb, c), lambda i: (k, i, 0))
    row = pl.BlockSpec((rb, c), lambda i: (i, 0))
    return pl.pallas_call(
        body, name=name, grid=(r // rb,), in_specs=[row, gk(0), gk(1), gk(2)], out_specs=row,
        out_shape=jax.ShapeDtypeStruct((r, c), F32), compiler_params=_params(("parallel",)))(own, got, got, got)


def sum_devices(small_all, *, name):
    _, p, c = small_all.shape

    def body(a_ref, out_ref):
        acc = a_ref[0]
        for d in range(1, N_DEV):
            acc = acc + a_ref[d]
        out_ref[...] = acc

    return pl.pallas_call(
        body, name=name, grid=(1,), in_specs=[pl.BlockSpec((N_DEV, p, c), lambda i: (0, 0, 0))],
        out_specs=pl.BlockSpec((p, c), lambda i: (0, 0)), out_shape=jax.ShapeDtypeStruct((p, c), F32),
        compiler_params=_params(("arbitrary",)))(small_all)


def adamw(parts, w, m, v, *, name):
    nl, r, c = w.shape
    assert len(parts) == nl
    npart = len(parts[0])
    rb = RED_ROWS if r % RED_ROWS == 0 else r
    flat = [a for layer in parts for a in layer]

    def body(*refs):
        p_refs, (w_ref, m_ref, v_ref) = refs[:nl * npart], refs[nl * npart:nl * npart + 3]
        g_ref, d_ref, nm_ref, nv_ref = refs[nl * npart + 3:]
        layer = pl.program_id(0)
        grad = None
        for l in range(nl):
            gl = p_refs[l * npart][...]
            for j in range(1, npart):
                gl = gl + p_refs[l * npart + j][...]
            grad = gl if grad is None else jnp.where(layer == l, gl, grad)
        wv, mv, vv = w_ref[...], m_ref[...], v_ref[...]
        nm = ADAM_B1 * mv + (1.0 - ADAM_B1) * grad
        nv = ADAM_B2 * vv + (1.0 - ADAM_B2) * (grad * grad)
        m_hat = nm / (1.0 - ADAM_B1 ** ADAM_STEP)
        v_hat = nv / (1.0 - ADAM_B2 ** ADAM_STEP)
        g_ref[...] = grad
        d_ref[...] = -ADAM_LR * (m_hat / (jnp.sqrt(v_hat) + ADAM_EPS) + ADAM_WD * wv)
        nm_ref[...] = nm
        nv_ref[...] = nv

    pspec = pl.BlockSpec((rb, c), lambda l, i: (i, 0))
    wspec = pl.BlockSpec((None, rb, c), lambda l, i: (l, i, 0))
    osh = jax.ShapeDtypeStruct((nl, r, c), F32)
    return pl.pallas_call(
        body, name=name, grid=(nl, r // rb), in_specs=[pspec] * (nl * npart) + [wspec] * 3,
        out_specs=[wspec] * 4, out_shape=[osh] * 4, compiler_params=_params(("parallel", "parallel")))(*flat, w, m, v)


def _rows128(a):
    flat = a.reshape(-1)
    pad = (-flat.shape[0]) % 128
    return jnp.pad(flat, (0, pad)).reshape(-1, 128)


def _pack_rows(arrs, multiple=8):
    rows = jnp.concatenate([_rows128(a.astype(F32)) for a in arrs], axis=0)
    return jnp.pad(rows, ((0, (-rows.shape[0]) % multiple), (0, 0)))


def _unpack_rows(rows, shapes):
    out, r0 = [], 0
    for shp in shapes:
        size = 1
        for s in shp:
            size *= s
        nr = -(-size // 128)
        out.append(rows[r0:r0 + nr].reshape(-1)[:size].reshape(shp))
        r0 += nr
    return out


SMALL_LOCAL_GRADS = ["even_norm", "even_conv", "a_log", "dt_bias", "sinks", "onorm", "odd_norm", "odd_ln_g",
                     "odd_ln_b", "odd_w_s", "odd_b_s", "ffn_norm", "final_norm"]
BIG = ["even_w_in", "even_w_out", "odd_w_in", "odd_w_out", "ffn_w_gate", "ffn_w_up", "ffn_w_down"]
WEIGHTS = ["even_norm", "even_w_in", "even_conv", "even_a_log", "even_dt_bias", "even_sinks", "even_onorm",
           "even_w_out", "odd_norm", "odd_w_in", "odd_ln_g", "odd_ln_b", "odd_w_s", "odd_b_s", "odd_w_out",
           "ffn_norm", "ffn_w_gate", "ffn_w_up", "ffn_w_down", "final_norm"]
SMALL = [n for n in WEIGHTS if n not in BIG]


def kernel(x, even_norm, even_w_in, even_conv, even_a_log, even_dt_bias, even_sinks, even_onorm, even_w_out, odd_norm, odd_w_in, odd_ln_g, odd_ln_b, odd_w_s, odd_b_s, odd_w_out, ffn_norm, ffn_w_gate, ffn_w_up, ffn_w_down, final_norm, loss_target, m_even_norm, m_even_w_in, m_even_conv, m_even_a_log, m_even_dt_bias, m_even_sinks, m_even_onorm, m_even_w_out, m_odd_norm, m_odd_w_in, m_odd_ln_g, m_odd_ln_b, m_odd_w_s, m_odd_b_s, m_odd_w_out, m_ffn_norm, m_ffn_w_gate, m_ffn_w_up, m_ffn_w_down, m_final_norm, v_even_norm, v_even_w_in, v_even_conv, v_even_a_log, v_even_dt_bias, v_even_sinks, v_even_onorm, v_even_w_out, v_odd_norm, v_odd_w_in, v_odd_ln_g, v_odd_ln_b, v_odd_w_s, v_odd_b_s, v_odd_w_out, v_ffn_norm, v_ffn_w_gate, v_ffn_w_up, v_ffn_w_down, v_final_norm):
    args = dict(locals())
    wl = {n: args[n] for n in WEIGHTS}
    ml = {n: args["m_" + n] for n in WEIGHTS}
    vl = {n: args["v_" + n] for n in WEIGHTS}
    me = 2 * lax.axis_index("x") + lax.axis_index("y")

    def landing(a):
        return lax.dynamic_update_index_in_dim(lax.empty((N_SHARD,) + a.shape, a.dtype), a, me, 0)

    b16 = lambda *arrs: [a.astype(BF16) for a in arrs]
    gather_groups = {
        "even_in": b16(even_w_in[0]) + [_pack_rows([even_conv[0], odd_norm, odd_ln_g, odd_ln_b])],
        "even_out": b16(even_w_out[0]),
        "ffn0": b16(ffn_w_gate[0], ffn_w_up[0], ffn_w_down[0]),
        "odd": b16(odd_w_in[0], odd_w_out[0]),
        "ffn1": b16(ffn_w_gate[1], ffn_w_up[1], ffn_w_down[1]),
    }
    gathering, after = {}, even_norm
    for group, srcs in gather_groups.items():
        gathering[group] = copies_start(_gather_plan, srcs, [landing(a) for a in srcs], after,
                                        name=f"gather_{group}_start")
        after = gathering[group]["token"]

    def get(group, behind):
        _, lands = copies_wait(_gather_plan, gathering[group], behind, name=f"gather_{group}_wait")
        if group == "even_in":
            parts = zip(*[_unpack_rows(lands[1][s], [(CONV_K, 768), (1, 512), (1, 512), (1, 512)])
                          for s in range(N_SHARD)])
            conv, onorm, lng, lnb = [jnp.concatenate(p, axis=1) for p in parts]
            w_in = jnp.pad(jnp.transpose(lands[0], (1, 0, 2)).reshape(D_MODEL, EVEN_IN),
                           ((0, 0), (0, EVEN_IN_PAD - EVEN_IN)))
            return {"even_w_in": w_in, "even_conv": conv, "odd_norm": onorm, "odd_ln_g": lng, "odd_ln_b": lnb}
        if group == "even_out":
            return {"even_w_out": lands[0].reshape(D_MODEL, D_MODEL)}
        if group == "odd":
            return {"odd_w_in": lands[0], "odd_w_out": lands[1].reshape(D_MODEL, D_MODEL)}
        return {"gate": lands[0], "up": lands[1], "down": lands[2].reshape(D_FF, D_MODEL)}

    rows4 =lambda a: a.reshape(N_SHARD, a.shape[0] // N_SHARD, a.shape[1])
    scattering = {}

    def emit(group, grads):
        if group == "even_in":
            srcs = [jnp.transpose(grads["even_w_in"][:, :EVEN_IN].reshape(D_MODEL, N_SHARD, EVEN_IN // N_SHARD),
                                  (1, 0, 2))]
        elif group == "even_out":
            srcs = [rows4(grads["even_w_out"])]
        elif group == "odd":
            srcs = [grads["odd_w_in"], rows4(grads["odd_w_out"])]
        else:
            srcs = [grads["gate"], grads["up"], rows4(grads["down"])]
        lands = [lax.empty((N_PEER,) + a.shape[1:], a.dtype) for a in srcs]
        scattering[group] = copies_start(_scatter_plan, srcs, lands, even_norm, name=f"scatter_{group}_start")

    pad816 = lambda a: jnp.pad(a, ((0, 0), (B_HEADS, 128 - 2 * B_HEADS)))
    w = {
        "even_norm": even_norm + after[0:1, 0:1],
        "a_log": pad816(even_a_log), "dt_bias": pad816(even_dt_bias),
        "sinks": jnp.pad(even_sinks, ((0, 0), (0, 128 - A_HEADS))),
        "onorm": even_onorm,
        "odd_w_s": odd_w_s[0],
        "odd_b_s": jnp.pad(odd_b_s[0].T, ((0, 0), (0, 128 - C_GROUPS))),
        "ffn_norm": ffn_norm,
        "final_norm": final_norm[None],
    }
    loss_l, grad_x, g = _local_step(x[0], loss_target[0], w, get, emit)
    loss = lax.psum(loss_l[0, 0], ("x", "y", "c"))

    small_all = allgather_small(_pack_rows([g[n] for n in SMALL_LOCAL_GRADS]), name="allgather_small")
    sums = {}
    for group, started in scattering.items():
        srcs, lands = copies_wait(_scatter_plan, started, grad_x, name=f"scatter_{group}_wait")
        partial = [sum_chips(lax.dynamic_index_in_dim(srcs[i], me, 0, keepdims=False), lands[i],
                             name=f"sum_chips_{group}_{i}") for i in range(len(srcs))]
        other = swap_cores(partial, name=f"swap_cores_{group}")
        sums[group] = list(zip(partial, other))

    outs = {}
    parts_of = {"even_w_in": [sums["even_in"][0]], "even_w_out": [sums["even_out"][0]],
                "odd_w_in": [sums["odd"][0]], "odd_w_out": [sums["odd"][1]],
                "ffn_w_gate": [sums["ffn0"][0], sums["ffn1"][0]], "ffn_w_up": [sums["ffn0"][1], sums["ffn1"][1]],
                "ffn_w_down": [sums["ffn0"][2], sums["ffn1"][2]]}
    for n in BIG:
        outs[n] = adamw(parts_of[n], wl[n], ml[n], vl[n], name=f"adamw_{n}")

    small_sum = sum_devices(small_all, name="sum_devices")
    sg = dict(zip(SMALL_LOCAL_GRADS, _unpack_rows(small_sum, [g[n].shape for n in SMALL_LOCAL_GRADS])))
    own_cols = lambda a, width: lax.dynamic_slice_in_dim(a, me * width, width, axis=a.ndim - 1)
    small_grads = {
        "even_norm": sg["even_norm"], "even_conv": own_cols(sg["even_conv"], 768)[None],
        "even_a_log": sg["a_log"][:, B_HEADS:2 * B_HEADS], "even_dt_bias": sg["dt_bias"][:, B_HEADS:2 * B_HEADS],
        "even_sinks": sg["sinks"][:, :A_HEADS], "even_onorm": sg["onorm"],
        "odd_norm": own_cols(sg["odd_norm"], 512), "odd_ln_g": own_cols(sg["odd_ln_g"], 512),
        "odd_ln_b": own_cols(sg["odd_ln_b"], 512), "odd_w_s": sg["odd_w_s"][None],
        "odd_b_s": sg["odd_b_s"][:, :C_GROUPS].T[None], "ffn_norm": sg["ffn_norm"], "final_norm": sg["final_norm"][0],
    }
    packed = [_pack_rows([d[n] for n in SMALL])[None] for d in (small_grads, wl, ml, vl)]
    small_out = adamw([(packed[0][0],)], packed[1], packed[2], packed[3], name="adamw_small")
    shapes = [wl[n].shape for n in SMALL]
    for j in range(4):
        for n, a in zip(SMALL, _unpack_rows(small_out[j][0], shapes)):
            outs.setdefault(n, [None] * 4)[j] = a

    return (loss, grad_x[None], *[outs[n][0] for n in WEIGHTS], *[outs[n][1] for n in WEIGHTS],
            *[outs[n][2] for n in WEIGHTS], *[outs[n][3] for n in WEIGHTS])
```

```python
import functools

import jax
import jax.numpy as jnp
from jax import lax
from jax.experimental import pallas as pl
from jax.experimental.pallas import tpu as pltpu

F32 = jnp.float32
BF16 = jnp.bfloat16
NEG_INF = float("-inf")

D_MODEL = 2048
A_HEADS, A_KV_HEADS, A_HEAD_DIM, WINDOW = 16, 2, 64, 128
B_HEADS, B_HEAD_DIM, CONV_K, DN_CHUNK = 8, 128, 4, 64
C_GROUPS, C_CHUNK = 8, 128
C_GROUP_DIM = D_MODEL // C_GROUPS
D_FF = 5632
EPS = 1e-6
A_Q = A_HEADS * A_HEAD_DIM
A_KV = A_KV_HEADS * A_HEAD_DIM
B_W = B_HEADS * B_HEAD_DIM
EVEN_IN = A_Q + 2 * A_KV + 4 * B_W + 2 * B_HEADS
EVEN_IN_PAD = 5632
COL_KV = A_Q
COL_QKVB = A_Q + 2 * A_KV
COL_Z = COL_QKVB + 3 * B_W
COL_GATE = COL_Z + B_W
N_SHARD = 4

ADAM_LR, ADAM_B1, ADAM_B2, ADAM_EPS, ADAM_WD, ADAM_STEP = 0.001, 0.9, 0.999, 1e-08, 0.01, 10

VMEM_LIMIT_V7X = 56 * 1024 * 1024
MESH_ID = pl.DeviceIdType.MESH


def _params(sem=None):
    return pltpu.CompilerParams(dimension_semantics=sem, vmem_limit_bytes=VMEM_LIMIT_V7X)


def _sigmoid(x):
    return 1.0 / (1.0 + jnp.exp(-x))


def _silu(x):
    return x * _sigmoid(x)


def _dsilu(x):
    s = _sigmoid(x)
    return s * (1.0 + x * (1.0 - s))


def _gelu(x):
    return 0.5 * x * (1.0 + lax.erf(x * 0.7071067811865476))


def _dgelu(x):
    return 0.5 * (1.0 + lax.erf(x * 0.7071067811865476)) + x * jnp.exp(-0.5 * x * x) * 0.3989422804014327


def _dot(a, b, dims):
    if a.ndim == 3:
        (ca,), (cb,) = dims
        return lax.dot_general(a, b, (((ca + 1,), (cb + 1,)), ((0,), (0,))), preferred_element_type=F32)
    return lax.dot_general(a, b, (dims, ((), ())), preferred_element_type=F32)


NN = ((1,), (0,))
NT = ((1,), (1,))
TN = ((0,), (0,))


def _as3(b):
    return b if b.ndim == 3 else b[None]


def mm_nn(a, b, *, tm, tn, tk, out_dtype, name, res=None, act=None):
    b3 = _as3(b)
    m, k = a.shape
    s, k2, ns = b3.shape
    assert k2 == k and m % tm == 0 and ns % tn == 0 and k % tk == 0, (a.shape, b3.shape, tm, tn, tk)
    nps, nk = ns // tn, k // tk

    def body(*refs):
        if res is None:
            a_ref, b_ref, o_ref, acc = refs
        else:
            a_ref, b_ref, r_ref, o_ref, acc = refs
        kk = pl.program_id(2)

        @pl.when(kk == 0)
        def _():
            acc[...] = jnp.zeros_like(acc)

        acc[...] += _dot(a_ref[...].astype(BF16), b_ref[...].astype(BF16), NN)

        @pl.when(kk == nk - 1)
        def _():
            r = acc[...]
            if res is not None:
                r = r + r_ref[...].astype(F32)
            o_ref[...] = r.astype(out_dtype)

    in_specs = [pl.BlockSpec((tm, tk), lambda i, j, kk: (i, kk)),
                pl.BlockSpec((None, tk, tn), lambda i, j, kk: (j // nps, kk, j % nps))]
    args = [a, b3]
    if res is not None:
        in_specs.append(pl.BlockSpec((tm, tn), lambda i, j, kk: (i, j)))
        args.append(res)
    return pl.pallas_call(
        body, name=name, grid=(m // tm, s * nps, nk), in_specs=in_specs,
        out_specs=pl.BlockSpec((tm, tn), lambda i, j, kk: (i, j)),
        out_shape=jax.ShapeDtypeStruct((m, s * ns), out_dtype),
        scratch_shapes=[pltpu.VMEM((tm, tn), F32)],
        compiler_params=_params(("parallel", "parallel", "arbitrary")))(*args)


def mm_nt(a, b, *, tm, tn, tk, out_dtype, name, res=None):
    b3 = _as3(b)
    m, n = a.shape
    s, k, ns = b3.shape
    assert n == s * ns and m % tm == 0 and k % tn == 0 and ns % tk == 0, (a.shape, b3.shape, tm, tn, tk)
    rps = ns // tk
    nr = s * rps

    def body(*refs):
        if res is None:
            a_ref, b_ref, o_ref, acc = refs
        else:
            a_ref, b_ref, r_ref, o_ref, acc = refs
        r_id = pl.program_id(2)

        @pl.when(r_id == 0)
        def _():
            acc[...] = jnp.zeros_like(acc)

        acc[...] += _dot(a_ref[...].astype(BF16), b_ref[...].astype(BF16), NT)

        @pl.when(r_id == nr - 1)
        def _():
            r = acc[...]
            if res is not None:
                r = r + r_ref[...].astype(F32)
            o_ref[...] = r.astype(out_dtype)

    in_specs = [pl.BlockSpec((tm, tk), lambda i, j, r: (i, r)),
                pl.BlockSpec((None, tn, tk), lambda i, j, r: (r // rps, j, r % rps))]
    args = [a, b3]
    if res is not None:
        in_specs.append(pl.BlockSpec((tm, tn), lambda i, j, r: (i, j)))
        args.append(res)
    return pl.pallas_call(
        body, name=name, grid=(m // tm, k // tn, nr), in_specs=in_specs,
        out_specs=pl.BlockSpec((tm, tn), lambda i, j, r: (i, j)),
        out_shape=jax.ShapeDtypeStruct((m, k), out_dtype),
        scratch_shapes=[pltpu.VMEM((tm, tn), F32)],
        compiler_params=_params(("parallel", "parallel", "arbitrary")))(*args)


def mm_tn(a, b, *, shards, tm, tn, tk, out_dtype, name):
    m, k = a.shape
    m2, n = b.shape
    ns = n // shards
    assert m2 == m and n == shards * ns and m % tm == 0 and k % tk == 0 and ns % tn == 0, (a.shape, b.shape)
    nps, nm = ns // tn, m // tm

    def body(a_ref, b_ref, o_ref, acc):
        mi = pl.program_id(2)

        @pl.when(mi == 0)
        def _():
            acc[...] = jnp.zeros_like(acc)

        acc[...] += _dot(a_ref[...].astype(BF16), b_ref[...].astype(BF16), TN)

        @pl.when(mi == nm - 1)
        def _():
            o_ref[...] = acc[...].astype(out_dtype)

    return pl.pallas_call(
        body, name=name, grid=(k // tk, shards * nps, nm),
        in_specs=[pl.BlockSpec((tm, tk), lambda i, j, mi: (mi, i)),
                  pl.BlockSpec((tm, tn), lambda i, j, mi: (mi, j))],
        out_specs=pl.BlockSpec((None, tk, tn), lambda i, j, mi: (j // nps, i, j % nps)),
        out_shape=jax.ShapeDtypeStruct((shards, k, ns), out_dtype),
        scratch_shapes=[pltpu.VMEM((tk, tn), F32)],
        compiler_params=_params(("parallel", "parallel", "arbitrary")))(a, b)


def mm_gate_up(hn, wg, wu, *, tm, tn, tk, name):
    wg3, wu3 = _as3(wg), _as3(wu)
    m, k = hn.shape
    s, _, ns = wg3.shape
    assert m % tm == 0 and ns % tn == 0 and k % tk == 0
    nps, nk = ns // tn, k // tk

    def body(a_ref, g_ref, u_ref, og_ref, ou_ref, oa_ref, accg, accu):
        kk = pl.program_id(2)

        @pl.when(kk == 0)
        def _():
            accg[...] = jnp.zeros_like(accg)
            accu[...] = jnp.zeros_like(accu)

        a = a_ref[...].astype(BF16)
        accg[...] += _dot(a, g_ref[...].astype(BF16), NN)
        accu[...] += _dot(a, u_ref[...].astype(BF16), NN)

        @pl.when(kk == nk - 1)
        def _():
            g, u = accg[...], accu[...]
            og_ref[...] = g.astype(BF16)
            ou_ref[...] = u.astype(BF16)
            oa_ref[...] = (_silu(g) * u).astype(BF16)

    wspec = pl.BlockSpec((None, tk, tn), lambda i, j, kk: (j // nps, kk, j % nps))
    ospec = pl.BlockSpec((tm, tn), lambda i, j, kk: (i, j))
    osh = jax.ShapeDtypeStruct((m, s * ns), BF16)
    return pl.pallas_call(
        body, name=name, grid=(m // tm, s * nps, nk),
        in_specs=[pl.BlockSpec((tm, tk), lambda i, j, kk: (i, kk)), wspec, wspec],
        out_specs=[ospec, ospec, ospec], out_shape=[osh, osh, osh],
        scratch_shapes=[pltpu.VMEM((tm, tn), F32), pltpu.VMEM((tm, tn), F32)],
        compiler_params=_params(("parallel", "parallel", "arbitrary")))(hn, wg3, wu3)


def mm_down_bwd(dh, wd, gate, up, *, tm, tn, tk, name):
    m, d = dh.shape
    f, d2 = wd.shape
    assert d2 == d and m % tm == 0 and f % tn == 0 and d % tk == 0
    nr = d // tk

    def body(a_ref, b_ref, g_ref, u_ref, og_ref, ou_ref, acc):
        r_id = pl.program_id(2)

        @pl.when(r_id == 0)
        def _():
            acc[...] = jnp.zeros_like(acc)

        acc[...] += _dot(a_ref[...].astype(BF16), b_ref[...].astype(BF16), NT)

        @pl.when(r_id == nr - 1)
        def _():
            da = acc[...]
            g, u = g_ref[...].astype(F32), u_ref[...].astype(F32)
            og_ref[...] = (da * u * _dsilu(g)).astype(BF16)
            ou_ref[...] = (da * _silu(g)).astype(BF16)

    ospec = pl.BlockSpec((tm, tn), lambda i, j, r: (i, j))
    osh = jax.ShapeDtypeStruct((m, f), BF16)
    return pl.pallas_call(
        body, name=name, grid=(m // tm, f // tn, nr),
        in_specs=[pl.BlockSpec((tm, tk), lambda i, j, r: (i, r)),
                  pl.BlockSpec((tn, tk), lambda i, j, r: (j, r)), ospec, ospec],
        out_specs=[ospec, ospec], out_shape=[osh, osh],
        scratch_shapes=[pltpu.VMEM((tm, tn), F32)],
        compiler_params=_params(("parallel", "parallel", "arbitrary")))(dh, wd, gate, up)


ROWS = 256


def rms_fwd(x, g, *, name):
    t, d = x.shape

    def body(x_ref, g_ref, o_ref):
        xv = x_ref[...]
        r = lax.rsqrt(jnp.mean(xv * xv, axis=-1, keepdims=True) + EPS)
        o_ref[...] = (xv * r * g_ref[...]).astype(BF16)

    return pl.pallas_call(
        body, name=name, grid=(t // ROWS,),
        in_specs=[pl.BlockSpec((ROWS, d), lambda i: (i, 0)), pl.BlockSpec((1, d), lambda i: (0, 0))],
        out_specs=pl.BlockSpec((ROWS, d), lambda i: (i, 0)),
        out_shape=jax.ShapeDtypeStruct((t, d), BF16), compiler_params=_params(("parallel",)))(x, g)


def rms_bwd(x, g, dy, dres, *, name):
    t, d = x.shape

    def body(x_ref, g_ref, dy_ref, dr_ref, dx_ref, dg_ref):
        @pl.when(pl.program_id(0) == 0)
        def _():
            dg_ref[...] = jnp.zeros_like(dg_ref)

        xv, dyv = x_ref[...], dy_ref[...].astype(F32)
        r = lax.rsqrt(jnp.mean(xv * xv, axis=-1, keepdims=True) + EPS)
        dyg = dyv * g_ref[...]
        dx = r * dyg - xv * (r * r * r) * jnp.mean(dyg * xv, axis=-1, keepdims=True)
        dx_ref[...] = dx + dr_ref[...]
        dg_ref[...] += jnp.sum(dyv * xv * r, axis=0, keepdims=True)

    row = pl.BlockSpec((ROWS, d), lambda i: (i, 0))
    vec = pl.BlockSpec((1, d), lambda i: (0, 0))
    return pl.pallas_call(
        body, name=name, grid=(t // ROWS,), in_specs=[row, vec, row, row], out_specs=[row, vec],
        out_shape=[jax.ShapeDtypeStruct((t, d), F32), jax.ShapeDtypeStruct((1, d), F32)],
        compiler_params=_params(("arbitrary",)))(x, g, dy, dres)


def loss_head(h, g, target, *, name):
    t, d = h.shape

    def body(x_ref, g_ref, t_ref, loss_ref, dx_ref, dg_ref):
        @pl.when(pl.program_id(0) == 0)
        def _():
            dg_ref[...] = jnp.zeros_like(dg_ref)
            loss_ref[...] = jnp.zeros_like(loss_ref)

        xv, gv = x_ref[...], g_ref[...]
        r = lax.rsqrt(jnp.mean(xv * xv, axis=-1, keepdims=True) + EPS)
        e = xv * r * gv - t_ref[...]
        loss_ref[...] += 0.5 * jnp.sum(jnp.mean(e * e, axis=-1, keepdims=True), axis=0, keepdims=True)
        dyv = e * (1.0 / d)
        dyg = dyv * gv
        dx_ref[...] = r * dyg - xv * (r * r * r) * jnp.mean(dyg * xv, axis=-1, keepdims=True)
        dg_ref[...] += jnp.sum(dyv * xv * r, axis=0, keepdims=True)

    row = pl.BlockSpec((ROWS, d), lambda i: (i, 0))
    vec = pl.BlockSpec((1, d), lambda i: (0, 0))
    return pl.pallas_call(
        body, name=name, grid=(t // ROWS,), in_specs=[row, vec, row],
        out_specs=[pl.BlockSpec((1, 128), lambda i: (0, 0)), row, vec],
        out_shape=[jax.ShapeDtypeStruct((1, 128), F32), jax.ShapeDtypeStruct((t, d), F32),
                   jax.ShapeDtypeStruct((1, d), F32)],
        compiler_params=_params(("arbitrary",)))(h, g, target)


def _tril_mask():
    r = lax.broadcasted_iota(jnp.int32, (C_CHUNK, C_CHUNK), 0)
    c = lax.broadcasted_iota(jnp.int32, (C_CHUNK, C_CHUNK), 1)
    return r >= c


def _layer_norm_parts(v):
    mu = jnp.mean(v, axis=-1, keepdims=True)
    vc = v - mu
    rstd = lax.rsqrt(jnp.mean(vc * vc, axis=-1, keepdims=True) + EPS)
    return vc * rstd, rstd


def gmlp_fwd(zpre, ln_g, ln_b, ws, bs_t, *, name):
    t = zpre.shape[0]
    d = D_MODEL

    def body(zu_ref, zv_ref, g_ref, b_ref, ws_ref, bs_ref, o_ref):
        u = _gelu(zu_ref[...])
        vhat, _ = _layer_norm_parts(_gelu(zv_ref[...]))
        vln = (vhat * g_ref[...] + b_ref[...]).astype(BF16)
        mask = _tril_mask()
        for gi in range(C_GROUPS):
            sl = slice(gi * C_GROUP_DIM, (gi + 1) * C_GROUP_DIM)
            w = jnp.where(mask, ws_ref[gi], 0.0).astype(BF16)
            mixed = _dot(w, vln[:, sl], NN) + bs_ref[:, gi:gi + 1]
            o_ref[:, sl] = (u[:, sl] * mixed).astype(BF16)

    vec = pl.BlockSpec((1, d), lambda i: (0, 0))
    return pl.pallas_call(
        body, name=name, grid=(t // C_CHUNK,),
        in_specs=[pl.BlockSpec((C_CHUNK, d), lambda i: (i, 0)), pl.BlockSpec((C_CHUNK, d), lambda i: (i, 1)),
                  vec, vec, pl.BlockSpec((C_GROUPS, C_CHUNK, C_CHUNK), lambda i: (0, 0, 0)),
                  pl.BlockSpec((C_CHUNK, 128), lambda i: (0, 0))],
        out_specs=pl.BlockSpec((C_CHUNK, d), lambda i: (i, 0)),
        out_shape=jax.ShapeDtypeStruct((t, d), BF16), compiler_params=_params(("parallel",)))(
            zpre, zpre, ln_g, ln_b, ws, bs_t)


def gmlp_bwd(zpre, dgated, ln_g, ln_b, ws, bs_t, *, name):
    t = zpre.shape[0]
    d = D_MODEL

    def body(zu_ref, zv_ref, dg_ref, g_ref, b_ref, ws_ref, bs_ref, dz_ref, dws_ref, dbs_ref, dlg_ref, dlb_ref):
        @pl.when(pl.program_id(0) == 0)
        def _():
            dws_ref[...] = jnp.zeros_like(dws_ref)
            dbs_ref[...] = jnp.zeros_like(dbs_ref)
            dlg_ref[...] = jnp.zeros_like(dlg_ref)
            dlb_ref[...] = jnp.zeros_like(dlb_ref)

        zu, zv = zu_ref[...], zv_ref[...]
        u = _gelu(zu)
        vhat, rstd = _layer_norm_parts(_gelu(zv))
        gam = g_ref[...]
        vln = (vhat * gam + b_ref[...]).astype(BF16)
        dgt = dg_ref[...].astype(F32)
        mask = _tril_mask()
        lane = lax.broadcasted_iota(jnp.int32, (C_CHUNK, 128), 1)
        dbs = jnp.zeros((C_CHUNK, 128), F32)
        du_parts, dvln_parts = [], []
        for gi in range(C_GROUPS):
            sl = slice(gi * C_GROUP_DIM, (gi + 1) * C_GROUP_DIM)
            w = jnp.where(mask, ws_ref[gi], 0.0).astype(BF16)
            mixed = _dot(w, vln[:, sl], NN) + bs_ref[:, gi:gi + 1]
            du_parts.append(dgt[:, sl] * mixed)
            dmixed = dgt[:, sl] * u[:, sl]
            dmb = dmixed.astype(BF16)
            dws_ref[gi] += jnp.where(mask, _dot(dmb, vln[:, sl], NT), 0.0)
            dbs = dbs + jnp.where(lane == gi, jnp.sum(dmixed, axis=-1, keepdims=True), 0.0)
            dvln_parts.append(_dot(w, dmb, TN))
        dbs_ref[...] += dbs
        du = jnp.concatenate(du_parts, axis=-1)
        dvln = jnp.concatenate(dvln_parts, axis=-1)
        dlg_ref[...] += jnp.sum(dvln * vhat, axis=0, keepdims=True)
        dlb_ref[...] += jnp.sum(dvln, axis=0, keepdims=True)
        dvhat = dvln * gam
        dv = rstd * (dvhat - jnp.mean(dvhat, axis=-1, keepdims=True)
                     - vhat * jnp.mean(dvhat * vhat, axis=-1, keepdims=True))
        dz_ref[:, :d] = (du * _dgelu(zu)).astype(BF16)
        dz_ref[:, d:] = (dv * _dgelu(zv)).astype(BF16)

    vec = pl.BlockSpec((1, d), lambda i: (0, 0))
    wsp = pl.BlockSpec((C_GROUPS, C_CHUNK, C_CHUNK), lambda i: (0, 0, 0))
    bsp = pl.BlockSpec((C_CHUNK, 128), lambda i: (0, 0))
    return pl.pallas_call(
        body, name=name, grid=(t // C_CHUNK,),
        in_specs=[pl.BlockSpec((C_CHUNK, d), lambda i: (i, 0)), pl.BlockSpec((C_CHUNK, d), lambda i: (i, 1)),
                  pl.BlockSpec((C_CHUNK, d), lambda i: (i, 0)), vec, vec, wsp, bsp],
        out_specs=[pl.BlockSpec((C_CHUNK, 2 * d), lambda i: (i, 0)), wsp, bsp, vec, vec],
        out_shape=[jax.ShapeDtypeStruct((t, 2 * d), BF16), jax.ShapeDtypeStruct((C_GROUPS, C_CHUNK, C_CHUNK), F32),
                   jax.ShapeDtypeStruct((C_CHUNK, 128), F32), jax.ShapeDtypeStruct((1, d), F32),
                   jax.ShapeDtypeStruct((1, d), F32)],
        compiler_params=_params(("arbitrary",)))(zpre, zpre, dgated, ln_g, ln_b, ws, bs_t)


ATT_SCALE = A_HEAD_DIM ** -0.5
PAIRS = A_HEADS // 2
PAIRS_PER_KV = PAIRS // A_KV_HEADS


def _att_padded(tile):
    lo = lax.broadcasted_iota(jnp.int32, tile.shape, 1) < A_HEAD_DIM
    rolled = pltpu.roll(tile, A_HEAD_DIM, 1)
    zero = jnp.zeros_like(tile)
    return {(0, 0): jnp.where(lo, tile, zero).astype(BF16), (0, 1): jnp.where(lo, zero, rolled).astype(BF16),
            (1, 0): jnp.where(lo, rolled, zero).astype(BF16), (1, 1): jnp.where(lo, zero, tile).astype(BF16)}


def _att_valid(n):
    r = lax.broadcasted_iota(jnp.int32, (WINDOW, 2 * WINDOW), 0)
    c = lax.broadcasted_iota(jnp.int32, (WINDOW, 2 * WINDOW), 1)
    rel = r + WINDOW - c
    return (rel >= 0) & (rel < WINDOW) & ((c >= WINDOW) | (n > 0))


def _att_probs(qp, kpad, sink, valid):
    s = jnp.where(valid, _dot(qp, kpad, NT), NEG_INF)
    m = jnp.maximum(jnp.max(s, axis=-1, keepdims=True), sink)
    p = jnp.exp(s - m)
    e_sink = jnp.exp(sink - m)
    inv = 1.0 / (jnp.sum(p, axis=-1, keepdims=True) + e_sink)
    return p * inv, e_sink * inv


def _att_specs(t):
    return [pl.BlockSpec((WINDOW, A_Q), lambda n: (n, 0)),
            pl.BlockSpec((WINDOW, 2 * A_KV), lambda n: (n, COL_KV // (2 * A_KV))),
            pl.BlockSpec((WINDOW, 2 * A_KV), lambda n: (jnp.maximum(n - 1, 0), COL_KV // (2 * A_KV))),
            pl.BlockSpec((1, 128), lambda n: (0, 0))]


def att_fwd(proj, sinks, *, name):
    t = proj.shape[0]

    def body(q_ref, kvc_ref, kvp_ref, s_ref, o_ref):
        n = pl.program_id(0)
        kv = jnp.concatenate([kvp_ref[...], kvc_ref[...]], axis=0)
        kpad, vpad = _att_padded(kv[:, :128]), _att_padded(kv[:, 128:])
        valid = _att_valid(n)
        for j in range(PAIRS):
            qp = (q_ref[:, j * 128:(j + 1) * 128] * ATT_SCALE).astype(BF16)
            acc = jnp.zeros((WINDOW, 128), F32)
            for half in range(2):
                key = (j // PAIRS_PER_KV, half)
                h = 2 * j + half
                w, _ = _att_probs(qp, kpad[key], s_ref[:, h:h + 1], valid)
                acc = acc + _dot(w.astype(BF16), vpad[key], NN)
            o_ref[:, j * 128:(j + 1) * 128] = acc.astype(BF16)

    return pl.pallas_call(
        body, name=name, grid=(t // WINDOW,), in_specs=_att_specs(t),
        out_specs=pl.BlockSpec((WINDOW, A_Q), lambda n: (n, 0)),
        out_shape=jax.ShapeDtypeStruct((t, A_Q), BF16), compiler_params=_params(("parallel",)))(
            proj, proj, proj, sinks)


def att_bwd(proj, sinks, dout, *, name):
    t = proj.shape[0]

    def body(q_ref, kvc_ref, kvp_ref, s_ref, do_ref, dq_ref, dkc_ref, dkp_ref, ds_ref):
        n = pl.program_id(0)

        @pl.when(n == 0)
        def _():
            ds_ref[...] = jnp.zeros_like(ds_ref)

        kv = jnp.concatenate([kvp_ref[...], kvc_ref[...]], axis=0)
        kpad, vpad = _att_padded(kv[:, :128]), _att_padded(kv[:, 128:])
        valid = _att_valid(n)
        lane = lax.broadcasted_iota(jnp.int32, (1, 128), 1)
        dsink = jnp.zeros((1, 128), F32)
        zero = jnp.zeros((2 * WINDOW, 128), F32)
        dk_acc = {key: zero for key in kpad}
        dv_acc = {key: zero for key in kpad}
        for j in range(PAIRS):
            qp = (q_ref[:, j * 128:(j + 1) * 128] * ATT_SCALE).astype(BF16)
            dop = do_ref[:, j * 128:(j + 1) * 128].astype(BF16)
            dq = jnp.zeros((WINDOW, 128), F32)
            for half in range(2):
                key = (j // PAIRS_PER_KV, half)
                h = 2 * j + half
                w, w_sink = _att_probs(qp, kpad[key], s_ref[:, h:h + 1], valid)
                dw = _dot(dop, vpad[key], NT)
                delta = jnp.sum(w * dw, axis=-1, keepdims=True)
                dsc = (w * (dw - delta)).astype(BF16)
                dsink = dsink + jnp.where(lane == h, -jnp.sum(w_sink * delta, axis=0, keepdims=True), 0.0)
                dq = dq + _dot(dsc, kpad[key], NN)
                dk_acc[key] = dk_acc[key] + _dot(dsc, qp, TN)
                dv_acc[key] = dv_acc[key] + _dot(w.astype(BF16), dop, TN)
            dq_ref[:, j * 128:(j + 1) * 128] = (dq * ATT_SCALE).astype(BF16)
        ds_ref[...] += dsink
        lo = lax.broadcasted_iota(jnp.int32, (2 * WINDOW, 128), 1) < A_HEAD_DIM

        def tile(acc):
            return jnp.where(lo, acc[(0, 0)] + pltpu.roll(acc[(0, 1)], A_HEAD_DIM, 1),
                             pltpu.roll(acc[(1, 0)], A_HEAD_DIM, 1) + acc[(1, 1)])

        dkv = jnp.concatenate([tile(dk_acc), tile(dv_acc)], axis=1)
        dkp_ref[...] = dkv[:WINDOW]
        dkc_ref[...] = dkv[WINDOW:]

    kvo = pl.BlockSpec((WINDOW, 2 * A_KV), lambda n: (n, 0))
    return pl.pallas_call(
        body, name=name, grid=(t // WINDOW,),
        in_specs=_att_specs(t) + [pl.BlockSpec((WINDOW, A_Q), lambda n: (n, 0))],
        out_specs=[pl.BlockSpec((WINDOW, A_Q), lambda n: (n, 0)), kvo, kvo, pl.BlockSpec((1, 128), lambda n: (0, 0))],
        out_shape=[jax.ShapeDtypeStruct((t, A_Q), BF16), jax.ShapeDtypeStruct((t, 2 * A_KV), F32),
                   jax.ShapeDtypeStruct((t, 2 * A_KV), F32), jax.ShapeDtypeStruct((1, 128), F32)],
        compiler_params=_params(("arbitrary",)))(proj, proj, proj, sinks, dout)


QK_SCALE = B_HEAD_DIM ** -0.5
PREP_COLS = 256
PREP_NCB = 3 * B_W // PREP_COLS
HALO = 8


def _roll_rows(x, shift):
    n = x.shape[0]
    return x if shift % n == 0 else pltpu.roll(x, shift % n, 0)


def _conv_taps(xe, w):
    xs = [_roll_rows(xe, CONV_K - 1 - i) for i in range(CONV_K)]
    c = w[0:1] * xs[0]
    for i in range(1, CONV_K):
        c = c + w[i:i + 1] * xs[i]
    return xs, c


def dprep_fwd(proj, conv_w, *, name):
    t = proj.shape[0]
    tt = ROWS
    col0 = COL_QKVB // PREP_COLS

    def body(x_ref, h_ref, w_ref, o_ref):
        cb, n = pl.program_id(0), pl.program_id(1)
        halo = jnp.where(n > 0, h_ref[...], 0.0)
        xe = jnp.concatenate([halo, x_ref[...]], axis=0)
        _, c = _conv_taps(xe, w_ref[...])
        y = _silu(c)[HALO:]
        parts = []
        for hh in range(PREP_COLS // B_HEAD_DIM):
            yh = y[:, hh * B_HEAD_DIM:(hh + 1) * B_HEAD_DIM]
            parts.append(yh * lax.rsqrt(jnp.sum(yh * yh, axis=-1, keepdims=True) + EPS))
        nrm = jnp.concatenate(parts, axis=-1)
        o_ref[...] = jnp.where(cb < 4, nrm * QK_SCALE, jnp.where(cb < 8, nrm, y))

    return pl.pallas_call(
        body, name=name, grid=(PREP_NCB, t // tt),
        in_specs=[pl.BlockSpec((tt, PREP_COLS), lambda cb, n: (n, col0 + cb)),
                  pl.BlockSpec((HALO, PREP_COLS), lambda cb, n: (jnp.maximum(n * (tt // HALO) - 1, 0), col0 + cb)),
                  pl.BlockSpec((CONV_K, PREP_COLS), lambda cb, n: (0, cb))],
        out_specs=pl.BlockSpec((tt, PREP_COLS), lambda cb, n: (n, cb)),
        out_shape=jax.ShapeDtypeStruct((t, 3 * B_W), F32), compiler_params=_params(("parallel", "parallel")))(
            proj, proj, conv_w)


def dprep_bwd(proj, conv_w, dqkvn, *, name):
    t = proj.shape[0]
    tt = ROWS
    nb = t // tt
    col0 = COL_QKVB // PREP_COLS
    n8 = t // HALO

    def body(xc_ref, xb_ref, xa_ref, dc_ref, da_ref, w_ref, dx_ref, dw_ref):
        cb, n = pl.program_id(0), pl.program_id(1)

        @pl.when(n == 0)
        def _():
            dw_ref[...] = jnp.zeros_like(dw_ref)

        w = w_ref[...]
        xe = jnp.concatenate([jnp.where(n > 0, xb_ref[...], 0.0), xc_ref[...], xa_ref[...]], axis=0)
        xs, c = _conv_taps(xe, w)
        sg = _sigmoid(c)
        y = c * sg
        dout = jnp.concatenate([jnp.zeros((HALO, PREP_COLS), F32), dc_ref[...],
                                jnp.where(n < nb - 1, da_ref[...], 0.0)], axis=0)
        dsc = jnp.where(cb < 4, QK_SCALE, 1.0)
        parts = []
        for hh in range(PREP_COLS // B_HEAD_DIM):
            sl = slice(hh * B_HEAD_DIM, (hh + 1) * B_HEAD_DIM)
            yh, doh = y[:, sl], dout[:, sl] * dsc
            r = lax.rsqrt(jnp.sum(yh * yh, axis=-1, keepdims=True) + EPS)
            parts.append(doh * r - yh * (r * r * r) * jnp.sum(doh * yh, axis=-1, keepdims=True))
        dy = jnp.where(cb < 8, jnp.concatenate(parts, axis=-1), dout)
        dcv = dy * sg * (1.0 + c * (1.0 - sg))
        dxe = w[CONV_K - 1:CONV_K] * dcv
        for i in range(CONV_K - 1):
            dxe = dxe + w[i:i + 1] * _roll_rows(dcv, -(CONV_K - 1 - i))
        dx_ref[...] = dxe[HALO:HALO + tt].astype(BF16)
        for i in range(CONV_K):
            dw_ref[i:i + 1, :] += jnp.sum((dcv * xs[i])[HALO:HALO + tt], axis=0, keepdims=True)

    def after(n):
        return jnp.minimum((n + 1) * (tt // HALO), n8 - 1)

    return pl.pallas_call(
        body, name=name, grid=(PREP_NCB, nb),
        in_specs=[pl.BlockSpec((tt, PREP_COLS), lambda cb, n: (n, col0 + cb)),
                  pl.BlockSpec((HALO, PREP_COLS), lambda cb, n: (jnp.maximum(n * (tt // HALO) - 1, 0), col0 + cb)),
                  pl.BlockSpec((HALO, PREP_COLS), lambda cb, n: (after(n), col0 + cb)),
                  pl.BlockSpec((tt, PREP_COLS), lambda cb, n: (n, cb)),
                  pl.BlockSpec((HALO, PREP_COLS), lambda cb, n: (after(n), cb)),
                  pl.BlockSpec((CONV_K, PREP_COLS), lambda cb, n: (0, cb))],
        out_specs=[pl.BlockSpec((tt, PREP_COLS), lambda cb, n: (n, cb)),
                   pl.BlockSpec((CONV_K, PREP_COLS), lambda cb, n: (0, cb))],
        out_shape=[jax.ShapeDtypeStruct((t, 3 * B_W), BF16), jax.ShapeDtypeStruct((CONV_K, 3 * B_W), F32)],
        compiler_params=_params(("parallel", "arbitrary")))(proj, proj, proj, dqkvn, dqkvn, conv_w)


def _softplus(z):
    return jnp.maximum(z, 0.0) + jnp.log(1.0 + jnp.exp(-jnp.abs(z)))


def gates_fwd(proj, alog_pad, dtb_pad, *, name):
    t = proj.shape[0]

    def body(x_ref, a_ref, b_ref, o_ref):
        raw = x_ref[...]
        lane = lax.broadcasted_iota(jnp.int32, raw.shape, 1)
        g = -jnp.exp(a_ref[...]) * _softplus(raw + b_ref[...])
        o_ref[...] = jnp.where(lane < B_HEADS, _sigmoid(raw), jnp.where(lane < 2 * B_HEADS, g, 0.0))

    vec = pl.BlockSpec((1, 128), lambda n: (0, 0))
    return pl.pallas_call(
        body, name=name, grid=(t // ROWS,),
        in_specs=[pl.BlockSpec((ROWS, 128), lambda n: (n, COL_GATE // 128)), vec, vec],
        out_specs=pl.BlockSpec((ROWS, 128), lambda n: (n, 0)),
        out_shape=jax.ShapeDtypeStruct((t, 128), F32), compiler_params=_params(("parallel",)))(
            proj, alog_pad, dtb_pad)


def gates_bwd(proj, alog_pad, dtb_pad, dgates, *, name):
    t = proj.shape[0]

    def body(x_ref, a_ref, b_ref, dg_ref, dx_ref, da_ref, db_ref):
        @pl.when(pl.program_id(0) == 0)
        def _():
            da_ref[...] = jnp.zeros_like(da_ref)
            db_ref[...] = jnp.zeros_like(db_ref)

        raw, dgt = x_ref[...], dg_ref[...]
        lane = lax.broadcasted_iota(jnp.int32, raw.shape, 1)
        is_beta, is_g = lane < B_HEADS, (lane >= B_HEADS) & (lane < 2 * B_HEADS)
        beta = _sigmoid(raw)
        z = raw + b_ref[...]
        neg_a = -jnp.exp(a_ref[...])
        d_z = jnp.where(is_g, dgt * neg_a * _sigmoid(z), 0.0)
        dx_ref[...] = jnp.where(is_beta, dgt * beta * (1.0 - beta), d_z).astype(BF16)
        db_ref[...] += jnp.sum(d_z, axis=0, keepdims=True)
        da_ref[...] += jnp.sum(jnp.where(is_g, dgt * neg_a * _softplus(z), 0.0), axis=0, keepdims=True)

    vec = pl.BlockSpec((1, 128), lambda n: (0, 0))
    row = pl.BlockSpec((ROWS, 128), lambda n: (n, 0))
    return pl.pallas_call(
        body, name=name, grid=(t // ROWS,),
        in_specs=[pl.BlockSpec((ROWS, 128), lambda n: (n, COL_GATE // 128)), vec, vec, row],
        out_specs=[row, vec, vec],
        out_shape=[jax.ShapeDtypeStruct((t, 128), BF16), jax.ShapeDtypeStruct((1, 128), F32),
                   jax.ShapeDtypeStruct((1, 128), F32)],
        compiler_params=_params(("arbitrary",)))(proj, alog_pad, dtb_pad, dgates)


def _split2(a):
    hi = a.astype(BF16)
    return hi, (a - hi.astype(F32)).astype(BF16)


def _dotp(a, b, dims, passes):
    if passes == 1:
        return _dot(a.astype(BF16), b.astype(BF16), dims)
    ah, al = _split2(a)
    bh, bl = _split2(b)
    return _dot(ah, bh, dims) + (_dot(ah, bl, dims) + _dot(al, bh, dims))


_GRAD_DIMS = {NN: ((NT, False), (TN, False)), NT: ((NN, False), (TN, True)), TN: ((NT, True), (NN, False))}


def _make_mm(dims, passes):
    (da_dims, da_swap), (db_dims, db_swap) = _GRAD_DIMS[dims]

    @jax.custom_vjp
    def mm(a, b):
        return _dotp(a, b, dims, passes)

    def fwd(a, b):
        return _dotp(a, b, dims, passes), (a, b)

    def bwd(saved, ct):
        a, b = saved
        da = _dotp(b, ct, da_dims, passes) if da_swap else _dotp(ct, b, da_dims, passes)
        db = _dotp(ct, a, db_dims, passes) if db_swap else _dotp(a, ct, db_dims, passes)
        return da, db

    mm.defvjp(fwd, bwd)
    return mm


MM1 = {d: _make_mm(d, 1) for d in (NN, NT, TN)}
MM3 = {d: _make_mm(d, 3) for d in (NN, NT, TN)}


def _tri_ones(lower):
    r = lax.broadcasted_iota(jnp.int32, (DN_CHUNK, DN_CHUNK), 0)
    c = lax.broadcasted_iota(jnp.int32, (DN_CHUNK, DN_CHUNK), 1)
    return (r >= c if lower else r <= c).astype(BF16)


def _tri_sum(x, lower):
    tri = _tri_ones(lower)
    hi = x.astype(BF16)
    r1 = x - hi.astype(F32)
    mid = r1.astype(BF16)
    lo = (r1 - mid.astype(F32)).astype(BF16)
    return _dot(tri, hi, NN) + (_dot(tri, mid, NN) + _dot(tri, lo, NN))


def _delta_chunk(s0, q, k, v, beta, gam_c, gam_r):
    c = DN_CHUNK
    r = lax.broadcasted_iota(jnp.int32, (c, c), 0)
    cc = lax.broadcasted_iota(jnp.int32, (c, c), 1)
    incl, strict = r >= cc, r > cc
    eye = (r == cc).astype(F32)
    decay = jnp.exp(jnp.where(incl, gam_c - gam_r, NEG_INF))
    g_last = gam_c[:, c - 1:c, :]
    e_gam, e_rest, e_last = jnp.exp(gam_c), jnp.exp(g_last - gam_c), jnp.exp(g_last)
    a_neg = -jnp.where(strict, beta * MM1[NT](k, k) * decay, 0.0)
    inv = eye + a_neg
    pw = a_neg
    for _ in range(5):
        pw = MM3[NN](pw, pw)
        inv = inv + MM3[NN](inv, pw)
    uw = MM3[NN](inv, jnp.concatenate([v * beta, k * (beta * e_gam)], axis=-1))
    u, w = uw[..., :B_HEAD_DIM], uw[..., B_HEAD_DIM:]
    qk = MM1[NT](q, k) * decay
    v_new = u - MM1[NN](w, s0)
    o = MM1[NN](q * e_gam, s0) + MM1[NN](qk, v_new)
    s1 = s0 * e_last + MM1[TN](k * e_rest, v_new)
    return s1, o


def _delta_operands(q_ref, k_ref, v_ref, gt):
    heads = lambda ref: jnp.stack([ref[:, h * B_HEAD_DIM:(h + 1) * B_HEAD_DIM] for h in range(B_HEADS)])
    gam = _tri_sum(gt, True)
    gam_t = gam.T
    beta = jnp.stack([gt[:, h:h + 1] for h in range(B_HEADS)])
    gam_c = jnp.stack([gam[:, B_HEADS + h:B_HEADS + h + 1] for h in range(B_HEADS)])
    gam_r = jnp.stack([gam_t[B_HEADS + h:B_HEADS + h + 1, :] for h in range(B_HEADS)])
    return heads(q_ref), heads(k_ref), heads(v_ref), beta, gam_c, gam_r


def delta_fwd(qkvn, gates, *, name):
    t = qkvn.shape[0]
    nc = t // DN_CHUNK

    def body(q_ref, k_ref, v_ref, g_ref, o_ref, ss_ref, state):
        @pl.when(pl.program_id(0) == 0)
        def _():
            state[...] = jnp.zeros_like(state)

        s0 = state[...]
        ss_ref[...] = s0
        s1, o = _delta_chunk(s0, *_delta_operands(q_ref, k_ref, v_ref, g_ref[...]))
        state[...] = s1
        for h in range(B_HEADS):
            o_ref[:, h * B_HEAD_DIM:(h + 1) * B_HEAD_DIM] = o[h]

    blk = lambda j: pl.BlockSpec((DN_CHUNK, B_W), lambda n: (n, j))
    return pl.pallas_call(
        body, name=name, grid=(nc,),
        in_specs=[blk(0), blk(1), blk(2), pl.BlockSpec((DN_CHUNK, 128), lambda n: (n, 0))],
        out_specs=[blk(0), pl.BlockSpec((None, B_HEADS, B_HEAD_DIM, B_HEAD_DIM), lambda n: (n, 0, 0, 0))],
        out_shape=[jax.ShapeDtypeStruct((t, B_W), F32),
                   jax.ShapeDtypeStruct((nc, B_HEADS, B_HEAD_DIM, B_HEAD_DIM), F32)],
        scratch_shapes=[pltpu.VMEM((B_HEADS, B_HEAD_DIM, B_HEAD_DIM), F32)],
        compiler_params=_params(("arbitrary",)))(qkvn, qkvn, qkvn, gates)


def delta_bwd(qkvn, gates, ssave, do, *, name):
    t = qkvn.shape[0]
    nc = t // DN_CHUNK

    def body(q_ref, k_ref, v_ref, g_ref, ss_ref, do_ref, dx_ref, dg_ref, dstate):
        @pl.when(pl.program_id(0) == 0)
        def _():
            dstate[...] = jnp.zeros_like(dstate)

        lane = lax.broadcasted_iota(jnp.int32, (DN_CHUNK, 128), 1)
        row = lax.broadcasted_iota(jnp.int32, (128, DN_CHUNK), 0)
        dbeta_all = jnp.zeros((DN_CHUNK, 128), F32)
        dgam_c_all = jnp.zeros((DN_CHUNK, 128), F32)
        dgam_r_all = jnp.zeros((128, DN_CHUNK), F32)
        _, vjp = jax.vjp(_delta_chunk, ss_ref[...], *_delta_operands(q_ref, k_ref, v_ref, g_ref[...]))
        do = jnp.stack([do_ref[:, h * B_HEAD_DIM:(h + 1) * B_HEAD_DIM] for h in range(B_HEADS)])
        ds0, dq, dk, dv, dbeta, dgam_c, dgam_r = vjp((dstate[...], do))
        dstate[...] = ds0
        for h in range(B_HEADS):
            dx_ref[:, h * B_HEAD_DIM:(h + 1) * B_HEAD_DIM] = dq[h]
            dx_ref[:, B_W + h * B_HEAD_DIM:B_W + (h + 1) * B_HEAD_DIM] = dk[h]
            dx_ref[:, 2 * B_W + h * B_HEAD_DIM:2 * B_W + (h + 1) * B_HEAD_DIM] = dv[h]
            dbeta_all = dbeta_all + jnp.where(lane == h, dbeta[h], 0.0)
            dgam_c_all = dgam_c_all + jnp.where(lane == B_HEADS + h, dgam_c[h], 0.0)
            dgam_r_all = dgam_r_all + jnp.where(row == B_HEADS + h, dgam_r[h], 0.0)
        dg_ref[...] = dbeta_all + _tri_sum(dgam_c_all + dgam_r_all.T, False)

    blk = lambda j: pl.BlockSpec((DN_CHUNK, B_W), lambda n: (nc - 1 - n, j))
    gsp = pl.BlockSpec((DN_CHUNK, 128), lambda n: (nc - 1 - n, 0))
    return pl.pallas_call(
        body, name=name, grid=(nc,),
        in_specs=[blk(0), blk(1), blk(2), gsp,
                  pl.BlockSpec((None, B_HEADS, B_HEAD_DIM, B_HEAD_DIM), lambda n: (nc - 1 - n, 0, 0, 0)), blk(0)],
        out_specs=[pl.BlockSpec((DN_CHUNK, 3 * B_W), lambda n: (nc - 1 - n, 0)), gsp],
        out_shape=[jax.ShapeDtypeStruct((t, 3 * B_W), F32), jax.ShapeDtypeStruct((t, 128), F32)],
        scratch_shapes=[pltpu.VMEM((B_HEADS, B_HEAD_DIM, B_HEAD_DIM), F32)],
        compiler_params=_params(("arbitrary",)))(qkvn, qkvn, qkvn, gates, ssave, do)


def gnorm_fwd(o, proj, onorm, *, name):
    t = o.shape[0]

    def body(o_ref, z_ref, w_ref, out_ref):
        ov = o_ref[...]
        r = lax.rsqrt(jnp.mean(ov * ov, axis=-1, keepdims=True) + EPS)
        out_ref[...] = (ov * r * w_ref[...] * _silu(z_ref[...])).astype(BF16)

    blk = pl.BlockSpec((ROWS, B_HEAD_DIM), lambda n, h: (n, h))
    return pl.pallas_call(
        body, name=name, grid=(t // ROWS, B_HEADS),
        in_specs=[blk, pl.BlockSpec((ROWS, B_HEAD_DIM), lambda n, h: (n, COL_Z // B_HEAD_DIM + h)),
                  pl.BlockSpec((1, B_HEAD_DIM), lambda n, h: (0, 0))],
        out_specs=blk, out_shape=jax.ShapeDtypeStruct((t, B_W), BF16),
        compiler_params=_params(("parallel", "parallel")))(o, proj, onorm)


def gnorm_bwd(o, proj, onorm, dout, *, dcol0, name):
    t = o.shape[0]

    def body(o_ref, z_ref, w_ref, d_ref, do_ref, dz_ref, dw_ref):
        @pl.when((pl.program_id(0) == 0) & (pl.program_id(1) == 0))
        def _():
            dw_ref[...] = jnp.zeros_like(dw_ref)

        ov, zv, wv, dv = o_ref[...], z_ref[...], w_ref[...], d_ref[...].astype(F32)
        r = lax.rsqrt(jnp.mean(ov * ov, axis=-1, keepdims=True) + EPS)
        nrm = ov * r
        dz_ref[...] = (dv * nrm * wv * _dsilu(zv)).astype(BF16)
        da = dv * _silu(zv)
        dw_ref[...] += jnp.sum(da * nrm, axis=0, keepdims=True)
        dn = da * wv
        do_ref[...] = r * dn - ov * (r * r * r) * jnp.mean(dn * ov, axis=-1, keepdims=True)

    blk = pl.BlockSpec((ROWS, B_HEAD_DIM), lambda n, h: (n, h))
    vec = pl.BlockSpec((1, B_HEAD_DIM), lambda n, h: (0, 0))
    return pl.pallas_call(
        body, name=name, grid=(t // ROWS, B_HEADS),
        in_specs=[blk, pl.BlockSpec((ROWS, B_HEAD_DIM), lambda n, h: (n, COL_Z // B_HEAD_DIM + h)), vec,
                  pl.BlockSpec((ROWS, B_HEAD_DIM), lambda n, h: (n, dcol0 // B_HEAD_DIM + h))],
        out_specs=[blk, blk, vec],
        out_shape=[jax.ShapeDtypeStruct((t, B_W), F32), jax.ShapeDtypeStruct((t, B_W), BF16),
                   jax.ShapeDtypeStruct((1, B_HEAD_DIM), F32)],
        compiler_params=_params(("arbitrary", "arbitrary")))(o, proj, onorm, dout)


def _ffn_fwd(h, norm_g, wg, wu, wd, tm, tag):
    hn = rms_fwd(h, norm_g, name=f"ffn{tag}_norm")
    gate, up, act = mm_gate_up(hn, wg, wu, tm=tm, tn=1408, tk=512, name=f"ffn{tag}_gate_up")
    h_out = mm_nn(act, wd, tm=tm, tn=512, tk=1408, out_dtype=F32, res=h, name=f"ffn{tag}_down")
    return h_out, (hn, gate, up, act)


def _ffn_bwd(dh, h, norm_g, wg, wu, wd, saved, tm, tag, emit):
    hn, gate, up, act = saved
    dwd = mm_tn(act, dh, shards=1, tm=tm, tn=512, tk=1408, out_dtype=BF16, name=f"ffn{tag}_dwd")[0]
    dgate, dup = mm_down_bwd(dh, wd, gate, up, tm=tm, tn=512, tk=2048, name=f"ffn{tag}_dact")
    dwg = mm_tn(hn, dgate, shards=N_SHARD, tm=tm, tn=1408, tk=1024, out_dtype=BF16, name=f"ffn{tag}_dwg")
    dwu = mm_tn(hn, dup, shards=N_SHARD, tm=tm, tn=1408, tk=1024, out_dtype=BF16, name=f"ffn{tag}_dwu")
    started = emit(f"ffn{tag}", {"gate": dwg, "up": dwu, "down": dwd})
    dhn = mm_nt(dgate, wg, tm=tm, tn=512, tk=1408, out_dtype=F32, name=f"ffn{tag}_dhn_g")
    dhn = mm_nt(dup, wu, tm=tm, tn=512, tk=1408, out_dtype=F32, res=dhn, name=f"ffn{tag}_dhn_u")
    dh_in, dnorm = rms_bwd(h, norm_g + started, dhn, dh, name=f"ffn{tag}_dnorm")
    return dh_in, dnorm


def _local_step(x, target, w, get, emit):
    t = x.shape[0]
    tm = min(1024, t)
    g = {}

    hn0 = rms_fwd(x, w["even_norm"], name="l0_norm")
    w.update(get("even_in", hn0))
    proj = mm_nn(hn0, w["even_w_in"], tm=tm, tn=512, tk=2048, out_dtype=F32, name="l0_w_in")
    out_a = att_fwd(proj, w["sinks"], name="l0_att")
    qkvn = dprep_fwd(proj, w["even_conv"], name="l0_prep")
    gates = gates_fwd(proj, w["a_log"], w["dt_bias"], name="l0_gates")
    o_delta, ssave = delta_fwd(qkvn, gates, name="l0_delta")
    out_b = gnorm_fwd(o_delta, proj, w["onorm"], name="l0_gnorm")
    mix0 = jnp.concatenate([out_a, out_b], axis=-1)
    w.update(get("even_out", mix0))
    h1 = mm_nn(mix0, w["even_w_out"], tm=tm, tn=512, tk=2048, out_dtype=F32, res=x, name="l0_w_out")
    f0 = get("ffn0", h1)
    h2, ffn0 = _ffn_fwd(h1, w["ffn_norm"][0:1], f0["gate"], f0["up"], f0["down"], tm, 0)
    hn2 = rms_fwd(h2, w["odd_norm"], name="l1_norm")
    w.update(get("odd", hn2))
    zpre = mm_nn(hn2, w["odd_w_in"], tm=tm, tn=1024, tk=2048, out_dtype=F32, name="l1_w_in")
    gated = gmlp_fwd(zpre, w["odd_ln_g"], w["odd_ln_b"], w["odd_w_s"], w["odd_b_s"], name="l1_gmlp")
    h3 = mm_nn(gated, w["odd_w_out"], tm=tm, tn=512, tk=2048, out_dtype=F32, res=h2, name="l1_w_out")
    f1 = get("ffn1", h3)
    h4, ffn1 = _ffn_fwd(h3, w["ffn_norm"][1:2], f1["gate"], f1["up"], f1["down"], tm, 1)
    loss, dh4, g["final_norm"] = loss_head(h4, w["final_norm"], target, name="loss_head")

    dh3, dn1 = _ffn_bwd(dh4, h3, w["ffn_norm"][1:2], f1["gate"], f1["up"], f1["down"], ffn1, tm, 1, emit)
    dw_out_o = mm_tn(gated, dh3, shards=1, tm=tm, tn=512, tk=1024, out_dtype=BF16, name="l1_dw_out")[0]
    dgated = mm_nt(dh3, w["odd_w_out"], tm=tm, tn=512, tk=2048, out_dtype=BF16, name="l1_dgated")
    dzpre, g["odd_w_s"], g["odd_b_s"], g["odd_ln_g"], g["odd_ln_b"] = gmlp_bwd(
        zpre, dgated, w["odd_ln_g"], w["odd_ln_b"], w["odd_w_s"], w["odd_b_s"], name="l1_dgmlp")
    dw_in_o = mm_tn(hn2, dzpre, shards=N_SHARD, tm=tm, tn=1024, tk=1024, out_dtype=BF16, name="l1_dw_in")
    started = emit("odd", {"odd_w_in": dw_in_o, "odd_w_out": dw_out_o})
    dhn2 = mm_nt(dzpre, w["odd_w_in"], tm=tm, tn=512, tk=1024, out_dtype=F32, name="l1_dhn")
    dh2, g["odd_norm"] = rms_bwd(h2, w["odd_norm"] + started, dhn2, dh3, name="l1_dnorm")
    dh1, dn0 = _ffn_bwd(dh2, h1, w["ffn_norm"][0:1], f0["gate"], f0["up"], f0["down"], ffn0, tm, 0, emit)
    g["ffn_norm"] = jnp.concatenate([dn0, dn1], axis=0)
    dw_out_e = mm_tn(mix0, dh1, shards=1, tm=tm, tn=512, tk=1024, out_dtype=BF16, name="l0_dw_out")[0]
    started = emit("even_out", {"even_w_out": dw_out_e})
    dmix = mm_nt(dh1, w["even_w_out"], tm=tm, tn=512, tk=2048, out_dtype=F32, name="l0_dmix")
    dq_a, dkv_cur, dkv_prev, g["sinks"] = att_bwd(proj, w["sinks"] + started, dmix, name="l0_datt")
    dkv = dkv_cur + jnp.concatenate([dkv_prev[WINDOW:], jnp.zeros((WINDOW, 2 * A_KV), F32)], axis=0)
    do_delta, dz, g["onorm"] = gnorm_bwd(o_delta, proj, w["onorm"], dmix, dcol0=A_Q, name="l0_dgnorm")
    dqkvn, dgates = delta_bwd(qkvn, gates, ssave, do_delta, name="l0_ddelta")
    dqkv_b, g["even_conv"] = dprep_bwd(proj, w["even_conv"], dqkvn, name="l0_dprep")
    draw, g["a_log"], g["dt_bias"] = gates_bwd(proj, w["a_log"], w["dt_bias"], dgates, name="l0_dgates")
    dproj = jnp.concatenate([dq_a, dkv.astype(BF16), dqkv_b, dz, draw,
                             jnp.zeros((t, EVEN_IN_PAD - COL_GATE - 128), BF16)], axis=-1)
    started = emit("even_in", {"even_w_in": mm_tn(hn0, dproj, shards=1, tm=tm, tn=512, tk=1024, out_dtype=BF16,
                                                  name="l0_dw_in")[0]})
    dhn0 = mm_nt(dproj, w["even_w_in"], tm=tm, tn=512, tk=2816, out_dtype=F32, name="l0_dhn")
    grad_x, g["even_norm"] = rms_bwd(x, w["even_norm"] + started, dhn0, dh1, name="l0_dnorm")
    return loss, grad_x, g


ANY = pl.BlockSpec(memory_space=pl.ANY)
N_DEV = 8


def _place():
    return lax.axis_index("x"), lax.axis_index("y"), lax.axis_index("c")


def _chip_peers(x, y, c):
    return [((1 - x, y, c), 2 * (1 - x) + y), ((x, 1 - y, c), 2 * x + 1 - y), ((1 - x, 1 - y, c), 2 * (1 - x) + 1 - y)]


HBM = pl.BlockSpec(memory_space=pltpu.HBM)
SEM = pl.BlockSpec(memory_space=pltpu.SEMAPHORE)
EFFECT = pltpu.SideEffectType.DATAFLOW_SIDE_EFFECTING
N_PEER = 3


def _gather_plan(srcs, lands, send, recv):
    x, y, c = _place()
    return [pltpu.make_async_remote_copy(src_ref=srcs[i], dst_ref=lands[i].at[2 * x + y], send_sem=send.at[N_PEER * i + k],
                                         recv_sem=recv.at[N_PEER * i + k], device_id=peer, device_id_type=MESH_ID)
            for i in range(len(srcs)) for k, (peer, _) in enumerate(_chip_peers(x, y, c))]


def _scatter_plan(srcs, lands, send, recv):
    x, y, c = _place()
    return [pltpu.make_async_remote_copy(src_ref=srcs[i].at[idx], dst_ref=lands[i].at[k], send_sem=send.at[N_PEER * i + k],
                                         recv_sem=recv.at[N_PEER * i + k], device_id=peer, device_id_type=MESH_ID)
            for i in range(len(srcs)) for k, (peer, idx) in enumerate(_chip_peers(x, y, c))]


def copies_start(plan, srcs, lands, after, *, name):
    n = len(srcs)
    both = list(srcs) + list(lands)

    def body(*refs):
        src_refs, land_refs = refs[:n], refs[n:2 * n]
        send, recv = refs[2 * n + 1], refs[2 * n + 2]
        for cp in plan(src_refs, land_refs, send, recv):
            cp.start()
        refs[-1][...] = jnp.zeros_like(refs[-1])

    res = pl.pallas_call(
        body, name=name,
        out_shape=(pltpu.SemaphoreType.DMA((n * N_PEER,)), pltpu.SemaphoreType.DMA((n * N_PEER,)),
                   *[pltpu.HBM(a.shape, a.dtype) for a in both], jax.ShapeDtypeStruct((8, 128), F32)),
        in_specs=[HBM] * (2 * n) + [ANY],
        out_specs=(SEM, SEM, *[HBM] * (2 * n), pl.BlockSpec(memory_space=pltpu.VMEM)),
        input_output_aliases={i: 2 + i for i in range(2 * n)},
        compiler_params=pltpu.CompilerParams(has_side_effects=EFFECT))(
            *[pltpu.with_memory_space_constraint(a, pltpu.HBM) for a in both], after)
    return {"send": res[0], "recv": res[1], "srcs": list(res[2:2 + n]), "lands": list(res[2 + n:2 + 2 * n]),
            "token": res[-1]}


def copies_wait(plan, started, after, *, name):
    srcs, lands = started["srcs"], started["lands"]
    n = len(srcs)
    both = srcs + lands

    def body(*refs):
        src_refs, land_refs = refs[:n], refs[n:2 * n]
        send, recv = refs[2 * n], refs[2 * n + 1]
        for cp in plan(src_refs, land_refs, send, recv):
            cp.wait_send()
            cp.wait_recv()

    res = pl.pallas_call(
        body, name=name, out_shape=tuple(pltpu.HBM(a.shape, a.dtype) for a in both),
        in_specs=[HBM] * (2 * n) + [SEM, SEM, ANY], out_specs=(HBM,) * (2 * n),
        input_output_aliases={i: i for i in range(2 * n)},
        compiler_params=pltpu.CompilerParams(has_side_effects=EFFECT))(*both, started["send"], started["recv"], after)
    return list(res[:n]), list(res[n:])


def allgather_small(small, *, name):
    def body(small_ref, out_ref, send, recv, loc):
        x, y, c = _place()
        dev = 4 * x + 2 * y + c
        local = pltpu.make_async_copy(small_ref, out_ref.at[dev], loc)
        remote = []
        for r in range(1, N_DEV):
            fx, fy, fc = (r >> 2) & 1, (r >> 1) & 1, r & 1
            peer = (1 - x if fx else x, 1 - y if fy else y, 1 - c if fc else c)
            remote.append(pltpu.make_async_remote_copy(
                src_ref=small_ref, dst_ref=out_ref.at[dev], send_sem=send.at[r - 1], recv_sem=recv.at[r - 1],
                device_id=peer, device_id_type=MESH_ID))
        local.start()
        for cp in remote:
            cp.start()
        for cp in remote:
            cp.wait()
        local.wait()

    return pl.pallas_call(
        body, name=name, in_specs=[ANY], out_specs=ANY,
        out_shape=jax.ShapeDtypeStruct((N_DEV,) + small.shape, small.dtype),
        scratch_shapes=[pltpu.SemaphoreType.DMA((N_DEV - 1,)), pltpu.SemaphoreType.DMA((N_DEV - 1,)),
                        pltpu.SemaphoreType.DMA(())])(small)


def swap_cores(arrs, *, name):
    n = len(arrs)

    def body(*refs):
        ins, outs = refs[:n], refs[n:2 * n]
        send, recv = refs[2 * n:]
        x, y, c = _place()
        copies = [pltpu.make_async_remote_copy(src_ref=ins[i], dst_ref=outs[i], send_sem=send.at[i], recv_sem=recv.at[i],
                                               device_id=(x, y, 1 - c), device_id_type=MESH_ID) for i in range(n)]
        for cp in copies:
            cp.start()
        for cp in copies:
            cp.wait()

    return pl.pallas_call(
        body, name=name, in_specs=[ANY] * n, out_specs=[ANY] * n,
        out_shape=[jax.ShapeDtypeStruct(a.shape, a.dtype) for a in arrs],
        scratch_shapes=[pltpu.SemaphoreType.DMA((n,)), pltpu.SemaphoreType.DMA((n,))])(*arrs)


RED_ROWS = 128


def sum_chips(own, got, *, name):
    r, c = own.shape
    rb = RED_ROWS if r % RED_ROWS == 0 else r

    def body(o_ref, a_ref, b_ref, c_ref, out_ref):
        out_ref[...] = ((o_ref[...].astype(F32) + a_ref[...].astype(F32)) + b_ref[...].astype(F32)) + c_ref[...].astype(F32)

    gk = lambda k: pl.BlockSpec((None, rb, c), lambda i: (k, i, 0))
    row = pl.BlockSpec((rb, c), lambda i: (i, 0))
    return pl.pallas_call(
        body, name=name, grid=(r // rb,), in_specs=[row, gk(0), gk(1), gk(2)], out_specs=row,
        out_shape=jax.ShapeDtypeStruct((r, c), F32), compiler_params=_params(("parallel",)))(own, got, got, got)


def sum_devices(small_all, *, name):
    _, p, c = small_all.shape

    def body(a_ref, out_ref):
        acc = a_ref[0]
        for d in range(1, N_DEV):
            acc = acc + a_ref[d]
        out_ref[...] = acc

    return pl.pallas_call(
        body, name=name, grid=(1,), in_specs=[pl.BlockSpec((N_DEV, p, c), lambda i: (0, 0, 0))],
        out_specs=pl.BlockSpec((p, c), lambda i: (0, 0)), out_shape=jax.ShapeDtypeStruct((p, c), F32),
        compiler_params=_params(("arbitrary",)))(small_all)


def adamw(parts, w, m, v, *, name):
    nl, r, c = w.shape
    assert len(parts) == nl
    npart = len(parts[0])
    rb = RED_ROWS if r % RED_ROWS == 0 else r
    flat = [a for layer in parts for a in layer]

    def body(*refs):
        p_refs, (w_ref, m_ref, v_ref) = refs[:nl * npart], refs[nl * npart:nl * npart + 3]
        g_ref, d_ref, nm_ref, nv_ref = refs[nl * npart + 3:]
        layer = pl.program_id(0)
        grad = None
        for l in range(nl):
            gl = p_refs[l * npart][...]
            for j in range(1, npart):
                gl = gl + p_refs[l * npart + j][...]
            grad = gl if grad is None else jnp.where(layer == l, gl, grad)
        wv, mv, vv = w_ref[...], m_ref[...], v_ref[...]
        nm = ADAM_B1 * mv + (1.0 - ADAM_B1) * grad
        nv = ADAM_B2 * vv + (1.0 - ADAM_B2) * (grad * grad)
        m_hat = nm / (1.0 - ADAM_B1 ** ADAM_STEP)
        v_hat = nv / (1.0 - ADAM_B2 ** ADAM_STEP)
        g_ref[...] = grad
        d_ref[...] = -ADAM_LR * (m_hat / (jnp.sqrt(v_hat) + ADAM_EPS) + ADAM_WD * wv)
        nm_ref[...] = nm
        nv_ref[...] = nv

    pspec = pl.BlockSpec((rb, c), lambda l, i: (i, 0))
    wspec = pl.BlockSpec((None, rb, c), lambda l, i: (l, i, 0))
    osh = jax.ShapeDtypeStruct((nl, r, c), F32)
    return pl.pallas_call(
        body, name=name, grid=(nl, r // rb), in_specs=[pspec] * (nl * npart) + [wspec] * 3,
        out_specs=[wspec] * 4, out_shape=[osh] * 4, compiler_params=_params(("parallel", "parallel")))(*flat, w, m, v)


def _rows128(a):
    flat = a.reshape(-1)
    pad = (-flat.shape[0]) % 128
    return jnp.pad(flat, (0, pad)).reshape(-1, 128)


def _pack_rows(arrs, multiple=8):
    rows = jnp.concatenate([_rows128(a.astype(F32)) for a in arrs], axis=0)
    return jnp.pad(rows, ((0, (-rows.shape[0]) % multiple), (0, 0)))


def _unpack_rows(rows, shapes):
    out, r0 = [], 0
    for shp in shapes:
        size = 1
        for s in shp:
            size *= s
        nr = -(-size // 128)
        out.append(rows[r0:r0 + nr].reshape(-1)[:size].reshape(shp))
        r0 += nr
    return out


SMALL_LOCAL_GRADS = ["even_norm", "even_conv", "a_log", "dt_bias", "sinks", "onorm", "odd_norm", "odd_ln_g",
                     "odd_ln_b", "odd_w_s", "odd_b_s", "ffn_norm", "final_norm"]
BIG = ["even_w_in", "even_w_out", "odd_w_in", "odd_w_out", "ffn_w_gate", "ffn_w_up", "ffn_w_down"]
WEIGHTS = ["even_norm", "even_w_in", "even_conv", "even_a_log", "even_dt_bias", "even_sinks", "even_onorm",
           "even_w_out", "odd_norm", "odd_w_in", "odd_ln_g", "odd_ln_b", "odd_w_s", "odd_b_s", "odd_w_out",
           "ffn_norm", "ffn_w_gate", "ffn_w_up", "ffn_w_down", "final_norm"]
SMALL = [n for n in WEIGHTS if n not in BIG]


def kernel(x, even_norm, even_w_in, even_conv, even_a_log, even_dt_bias, even_sinks, even_onorm, even_w_out, odd_norm, odd_w_in, odd_ln_g, odd_ln_b, odd_w_s, odd_b_s, odd_w_out, ffn_norm, ffn_w_gate, ffn_w_up, ffn_w_down, final_norm, loss_target, m_even_norm, m_even_w_in, m_even_conv, m_even_a_log, m_even_dt_bias, m_even_sinks, m_even_onorm, m_even_w_out, m_odd_norm, m_odd_w_in, m_odd_ln_g, m_odd_ln_b, m_odd_w_s, m_odd_b_s, m_odd_w_out, m_ffn_norm, m_ffn_w_gate, m_ffn_w_up, m_ffn_w_down, m_final_norm, v_even_norm, v_even_w_in, v_even_conv, v_even_a_log, v_even_dt_bias, v_even_sinks, v_even_onorm, v_even_w_out, v_odd_norm, v_odd_w_in, v_odd_ln_g, v_odd_ln_b, v_odd_w_s, v_odd_b_s, v_odd_w_out, v_ffn_norm, v_ffn_w_gate, v_ffn_w_up, v_ffn_w_down, v_final_norm):
    args = dict(locals())
    wl = {n: args[n] for n in WEIGHTS}
    ml = {n: args["m_" + n] for n in WEIGHTS}
    vl = {n: args["v_" + n] for n in WEIGHTS}
    me = 2 * lax.axis_index("x") + lax.axis_index("y")

    def landing(a):
        return lax.dynamic_update_index_in_dim(lax.empty((N_SHARD,) + a.shape, a.dtype), a, me, 0)

    b16 = lambda *arrs: [a.astype(BF16) for a in arrs]
    gather_groups = {
        "even_in": b16(even_w_in[0]) + [_pack_rows([even_conv[0], odd_norm, odd_ln_g, odd_ln_b])],
        "even_out": b16(even_w_out[0]),
        "ffn0": b16(ffn_w_gate[0], ffn_w_up[0], ffn_w_down[0]),
        "odd": b16(odd_w_in[0], odd_w_out[0]),
        "ffn1": b16(ffn_w_gate[1], ffn_w_up[1], ffn_w_down[1]),
    }
    gathering, after = {}, even_norm
    for group, srcs in gather_groups.items():
        gathering[group] = copies_start(_gather_plan, srcs, [landing(a) for a in srcs], after,
                                        name=f"gather_{group}_start")
        after = gathering[group]["token"]

    def get(group, behind):
        _, lands = copies_wait(_gather_plan, gathering[group], behind, name=f"gather_{group}_wait")
        if group == "even_in":
            parts = zip(*[_unpack_rows(lands[1][s], [(CONV_K, 768), (1, 512), (1, 512), (1, 512)])
                          for s in range(N_SHARD)])
            conv, onorm, lng, lnb = [jnp.concatenate(p, axis=1) for p in parts]
            w_in = jnp.pad(jnp.transpose(lands[0], (1, 0, 2)).reshape(D_MODEL, EVEN_IN),
                           ((0, 0), (0, EVEN_IN_PAD - EVEN_IN)))
            return {"even_w_in": w_in, "even_conv": conv, "odd_norm": onorm, "odd_ln_g": lng, "odd_ln_b": lnb}
        if group == "even_out":
            return {"even_w_out": lands[0].reshape(D_MODEL, D_MODEL)}
        if group == "odd":
            return {"odd_w_in": lands[0], "odd_w_out": lands[1].reshape(D_MODEL, D_MODEL)}
        return {"gate": lands[0], "up": lands[1], "down": lands[2].reshape(D_FF, D_MODEL)}

    rows4 =lambda a: a.reshape(N_SHARD, a.shape[0] // N_SHARD, a.shape[1])
    scattering = {}

    def emit(group, grads):
        if group == "even_in":
            srcs = [jnp.transpose(grads["even_w_in"][:, :EVEN_IN].reshape(D_MODEL, N_SHARD, EVEN_IN // N_SHARD),
                                  (1, 0, 2))]
        elif group == "even_out":
            srcs = [rows4(grads["even_w_out"])]
        elif group == "odd":
            srcs = [grads["odd_w_in"], rows4(grads["odd_w_out"])]
        else:
            srcs = [grads["gate"], grads["up"], rows4(grads["down"])]
        lands = [lax.empty((N_PEER,) + a.shape[1:], a.dtype) for a in srcs]
        scattering[group] = copies_start(_scatter_plan, srcs, lands, even_norm, name=f"scatter_{group}_start")
        return scattering[group]["token"][0:1, 0:1]

    pad816 = lambda a: jnp.pad(a, ((0, 0), (B_HEADS, 128 - 2 * B_HEADS)))
    w = {
        "even_norm": even_norm + after[0:1, 0:1],
        "a_log": pad816(even_a_log), "dt_bias": pad816(even_dt_bias),
        "sinks": jnp.pad(even_sinks, ((0, 0), (0, 128 - A_HEADS))),
        "onorm": even_onorm,
        "odd_w_s": odd_w_s[0],
        "odd_b_s": jnp.pad(odd_b_s[0].T, ((0, 0), (0, 128 - C_GROUPS))),
        "ffn_norm": ffn_norm,
        "final_norm": final_norm[None],
    }
    loss_l, grad_x, g = _local_step(x[0], loss_target[0], w, get, emit)
    loss = lax.psum(loss_l[0, 0], ("x", "y", "c"))

    small_all = allgather_small(_pack_rows([g[n] for n in SMALL_LOCAL_GRADS]), name="allgather_small")
    sums = {}
    for group, started in scattering.items():
        srcs, lands = copies_wait(_scatter_plan, started, grad_x, name=f"scatter_{group}_wait")
        partial = [sum_chips(lax.dynamic_index_in_dim(srcs[i], me, 0, keepdims=False), lands[i],
                             name=f"sum_chips_{group}_{i}") for i in range(len(srcs))]
        other = swap_cores(partial, name=f"swap_cores_{group}")
        sums[group] = list(zip(partial, other))

    outs = {}
    parts_of = {"even_w_in": [sums["even_in"][0]], "even_w_out": [sums["even_out"][0]],
                "odd_w_in": [sums["odd"][0]], "odd_w_out": [sums["odd"][1]],
                "ffn_w_gate": [sums["ffn0"][0], sums["ffn1"][0]], "ffn_w_up": [sums["ffn0"][1], sums["ffn1"][1]],
                "ffn_w_down": [sums["ffn0"][2], sums["ffn1"][2]]}
    for n in BIG:
        outs[n] = adamw(parts_of[n], wl[n], ml[n], vl[n], name=f"adamw_{n}")

    small_sum = sum_devices(small_all, name="sum_devices")
    sg = dict(zip(SMALL_LOCAL_GRADS, _unpack_rows(small_sum, [g[n].shape for n in SMALL_LOCAL_GRADS])))
    own_cols = lambda a, width: lax.dynamic_slice_in_dim(a, me * width, width, axis=a.ndim - 1)
    small_grads = {
        "even_norm": sg["even_norm"], "even_conv": own_cols(sg["even_conv"], 768)[None],
        "even_a_log": sg["a_log"][:, B_HEADS:2 * B_HEADS], "even_dt_bias": sg["dt_bias"][:, B_HEADS:2 * B_HEADS],
        "even_sinks": sg["sinks"][:, :A_HEADS], "even_onorm": sg["onorm"],
        "odd_norm": own_cols(sg["odd_norm"], 512), "odd_ln_g": own_cols(sg["odd_ln_g"], 512),
        "odd_ln_b": own_cols(sg["odd_ln_b"], 512), "odd_w_s": sg["odd_w_s"][None],
        "odd_b_s": sg["odd_b_s"][:, :C_GROUPS].T[None], "ffn_norm": sg["ffn_norm"], "final_norm": sg["final_norm"][0],
    }
    packed = [_pack_rows([d[n] for n in SMALL])[None] for d in (small_grads, wl, ml, vl)]
    small_out = adamw([(packed[0][0],)], packed[1], packed[2], packed[3], name="adamw_small")
    shapes = [wl[n].shape for n in SMALL]
    for j in range(4):
        for n, a in zip(SMALL, _unpack_rows(small_out[j][0], shapes)):
            outs.setdefault(n, [None] * 4)[j] = a

    return (loss, grad_x[None], *[outs[n][0] for n in WEIGHTS], *[outs[n][1] for n in WEIGHTS],
            *[outs[n][2] for n in WEIGHTS], *[outs[n][3] for n in WEIGHTS])
```

```python
import functools

import jax
import jax.numpy as jnp
from jax import lax
from jax.experimental import pallas as pl
from jax.experimental.pallas import tpu as pltpu

F32 = jnp.float32
BF16 = jnp.bfloat16
NEG_INF = float("-inf")

D_MODEL = 2048
A_HEADS, A_KV_HEADS, A_HEAD_DIM, WINDOW = 16, 2, 64, 128
B_HEADS, B_HEAD_DIM, CONV_K, DN_CHUNK = 8, 128, 4, 64
C_GROUPS, C_CHUNK = 8, 128
C_GROUP_DIM = D_MODEL // C_GROUPS
D_FF = 5632
EPS = 1e-6
A_Q = A_HEADS * A_HEAD_DIM
A_KV = A_KV_HEADS * A_HEAD_DIM
B_W = B_HEADS * B_HEAD_DIM
EVEN_IN = A_Q + 2 * A_KV + 4 * B_W + 2 * B_HEADS
EVEN_IN_PAD = 5632
COL_KV = A_Q
COL_QKVB = A_Q + 2 * A_KV
COL_Z = COL_QKVB + 3 * B_W
COL_GATE = COL_Z + B_W
N_SHARD = 4

ADAM_LR, ADAM_B1, ADAM_B2, ADAM_EPS, ADAM_WD, ADAM_STEP = 0.001, 0.9, 0.999, 1e-08, 0.01, 10

VMEM_LIMIT_V7X = 56 * 1024 * 1024
MESH_ID = pl.DeviceIdType.MESH


def _params(sem=None):
    return pltpu.CompilerParams(dimension_semantics=sem, vmem_limit_bytes=VMEM_LIMIT_V7X)


def _sigmoid(x):
    return 1.0 / (1.0 + jnp.exp(-x))


def _silu(x):
    return x * _sigmoid(x)


def _dsilu(x):
    s = _sigmoid(x)
    return s * (1.0 + x * (1.0 - s))


def _gelu(x):
    return 0.5 * x * (1.0 + lax.erf(x * 0.7071067811865476))


def _dgelu(x):
    return 0.5 * (1.0 + lax.erf(x * 0.7071067811865476)) + x * jnp.exp(-0.5 * x * x) * 0.3989422804014327


def _dot(a, b, dims):
    if a.ndim == 3:
        (ca,), (cb,) = dims
        return lax.dot_general(a, b, (((ca + 1,), (cb + 1,)), ((0,), (0,))), preferred_element_type=F32)
    return lax.dot_general(a, b, (dims, ((), ())), preferred_element_type=F32)


NN = ((1,), (0,))
NT = ((1,), (1,))
TN = ((0,), (0,))


def _as3(b):
    return b if b.ndim == 3 else b[None]


def mm_nn(a, b, *, tm, tn, tk, out_dtype, name, res=None, act=None):
    b3 = _as3(b)
    m, k = a.shape
    s, k2, ns = b3.shape
    assert k2 == k and m % tm == 0 and ns % tn == 0 and k % tk == 0, (a.shape, b3.shape, tm, tn, tk)
    nps, nk = ns // tn, k // tk

    def body(*refs):
        if res is None:
            a_ref, b_ref, o_ref, acc = refs
        else:
            a_ref, b_ref, r_ref, o_ref, acc = refs
        kk = pl.program_id(2)

        @pl.when(kk == 0)
        def _():
            acc[...] = jnp.zeros_like(acc)

        acc[...] += _dot(a_ref[...].astype(BF16), b_ref[...].astype(BF16), NN)

        @pl.when(kk == nk - 1)
        def _():
            r = acc[...]
            if res is not None:
                r = r + r_ref[...].astype(F32)
            o_ref[...] = r.astype(out_dtype)

    in_specs = [pl.BlockSpec((tm, tk), lambda i, j, kk: (i, kk)),
                pl.BlockSpec((None, tk, tn), lambda i, j, kk: (j // nps, kk, j % nps))]
    args = [a, b3]
    if res is not None:
        in_specs.append(pl.BlockSpec((tm, tn), lambda i, j, kk: (i, j)))
        args.append(res)
    return pl.pallas_call(
        body, name=name, grid=(m // tm, s * nps, nk), in_specs=in_specs,
        out_specs=pl.BlockSpec((tm, tn), lambda i, j, kk: (i, j)),
        out_shape=jax.ShapeDtypeStruct((m, s * ns), out_dtype),
        scratch_shapes=[pltpu.VMEM((tm, tn), F32)],
        compiler_params=_params(("parallel", "parallel", "arbitrary")))(*args)


def mm_nt(a, b, *, tm, tn, tk, out_dtype, name, res=None):
    b3 = _as3(b)
    m, n = a.shape
    s, k, ns = b3.shape
    assert n == s * ns and m % tm == 0 and k % tn == 0 and ns % tk == 0, (a.shape, b3.shape, tm, tn, tk)
    rps = ns // tk
    nr = s * rps

    def body(*refs):
        if res is None:
            a_ref, b_ref, o_ref, acc = refs
        else:
            a_ref, b_ref, r_ref, o_ref, acc = refs
        r_id = pl.program_id(2)

        @pl.when(r_id == 0)
        def _():
            acc[...] = jnp.zeros_like(acc)

        acc[...] += _dot(a_ref[...].astype(BF16), b_ref[...].astype(BF16), NT)

        @pl.when(r_id == nr - 1)
        def _():
            r = acc[...]
            if res is not None:
                r = r + r_ref[...].astype(F32)
            o_ref[...] = r.astype(out_dtype)

    in_specs = [pl.BlockSpec((tm, tk), lambda i, j, r: (i, r)),
                pl.BlockSpec((None, tn, tk), lambda i, j, r: (r // rps, j, r % rps))]
    args = [a, b3]
    if res is not None:
        in_specs.append(pl.BlockSpec((tm, tn), lambda i, j, r: (i, j)))
        args.append(res)
    return pl.pallas_call(
        body, name=name, grid=(m // tm, k // tn, nr), in_specs=in_specs,
        out_specs=pl.BlockSpec((tm, tn), lambda i, j, r: (i, j)),
        out_shape=jax.ShapeDtypeStruct((m, k), out_dtype),
        scratch_shapes=[pltpu.VMEM((tm, tn), F32)],
        compiler_params=_params(("parallel", "parallel", "arbitrary")))(*args)


def mm_tn(a, b, *, shards, tm, tn, tk, out_dtype, name):
    m, k = a.shape
    m2, n = b.shape
    ns = n // shards
    assert m2 == m and n == shards * ns and m % tm == 0 and k % tk == 0 and ns % tn == 0, (a.shape, b.shape)
    nps, nm = ns // tn, m // tm

    def body(a_ref, b_ref, o_ref, acc):
        mi = pl.program_id(2)

        @pl.when(mi == 0)
        def _():
            acc[...] = jnp.zeros_like(acc)

        acc[...] += _dot(a_ref[...].astype(BF16), b_ref[...].astype(BF16), TN)

        @pl.when(mi == nm - 1)
        def _():
            o_ref[...] = acc[...].astype(out_dtype)

    return pl.pallas_call(
        body, name=name, grid=(k // tk, shards * nps, nm),
        in_specs=[pl.BlockSpec((tm, tk), lambda i, j, mi: (mi, i)),
                  pl.BlockSpec((tm, tn), lambda i, j, mi: (mi, j))],
        out_specs=pl.BlockSpec((None, tk, tn), lambda i, j, mi: (j // nps, i, j % nps)),
        out_shape=jax.ShapeDtypeStruct((shards, k, ns), out_dtype),
        scratch_shapes=[pltpu.VMEM((tk, tn), F32)],
        compiler_params=_params(("parallel", "parallel", "arbitrary")))(a, b)


def mm_gate_up(hn, wg, wu, *, tm, tn, tk, name):
    wg3, wu3 = _as3(wg), _as3(wu)
    m, k = hn.shape
    s, _, ns = wg3.shape
    assert m % tm == 0 and ns % tn == 0 and k % tk == 0
    nps, nk = ns // tn, k // tk

    def body(a_ref, g_ref, u_ref, og_ref, ou_ref, oa_ref, accg, accu):
        kk = pl.program_id(2)

        @pl.when(kk == 0)
        def _():
            accg[...] = jnp.zeros_like(accg)
            accu[...] = jnp.zeros_like(accu)

        a = a_ref[...].astype(BF16)
        accg[...] += _dot(a, g_ref[...].astype(BF16), NN)
        accu[...] += _dot(a, u_ref[...].astype(BF16), NN)

        @pl.when(kk == nk - 1)
        def _():
            g, u = accg[...], accu[...]
            og_ref[...] = g.astype(BF16)
            ou_ref[...] = u.astype(BF16)
            oa_ref[...] = (_silu(g) * u).astype(BF16)

    wspec = pl.BlockSpec((None, tk, tn), lambda i, j, kk: (j // nps, kk, j % nps))
    ospec = pl.BlockSpec((tm, tn), lambda i, j, kk: (i, j))
    osh = jax.ShapeDtypeStruct((m, s * ns), BF16)
    return pl.pallas_call(
        body, name=name, grid=(m // tm, s * nps, nk),
        in_specs=[pl.BlockSpec((tm, tk), lambda i, j, kk: (i, kk)), wspec, wspec],
        out_specs=[ospec, ospec, ospec], out_shape=[osh, osh, osh],
        scratch_shapes=[pltpu.VMEM((tm, tn), F32), pltpu.VMEM((tm, tn), F32)],
        compiler_params=_params(("parallel", "parallel", "arbitrary")))(hn, wg3, wu3)


def mm_down_bwd(dh, wd, gate, up, *, tm, tn, tk, name):
    m, d = dh.shape
    f, d2 = wd.shape
    assert d2 == d and m % tm == 0 and f % tn == 0 and d % tk == 0
    nr = d // tk

    def body(a_ref, b_ref, g_ref, u_ref, og_ref, ou_ref, acc):
        r_id = pl.program_id(2)

        @pl.when(r_id == 0)
        def _():
            acc[...] = jnp.zeros_like(acc)

        acc[...] += _dot(a_ref[...].astype(BF16), b_ref[...].astype(BF16), NT)

        @pl.when(r_id == nr - 1)
        def _():
            da = acc[...]
            g, u = g_ref[...].astype(F32), u_ref[...].astype(F32)
            og_ref[...] = (da * u * _dsilu(g)).astype(BF16)
            ou_ref[...] = (da * _silu(g)).astype(BF16)

    ospec = pl.BlockSpec((tm, tn), lambda i, j, r: (i, j))
    osh = jax.ShapeDtypeStruct((m, f), BF16)
    return pl.pallas_call(
        body, name=name, grid=(m // tm, f // tn, nr),
        in_specs=[pl.BlockSpec((tm, tk), lambda i, j, r: (i, r)),
                  pl.BlockSpec((tn, tk), lambda i, j, r: (j, r)), ospec, ospec],
        out_specs=[ospec, ospec], out_shape=[osh, osh],
        scratch_shapes=[pltpu.VMEM((tm, tn), F32)],
        compiler_params=_params(("parallel", "parallel", "arbitrary")))(dh, wd, gate, up)


ROWS = 256


def rms_fwd(x, g, *, name):
    t, d = x.shape

    def body(x_ref, g_ref, o_ref):
        xv = x_ref[...]
        r = lax.rsqrt(jnp.mean(xv * xv, axis=-1, keepdims=True) + EPS)
        o_ref[...] = (xv * r * g_ref[...]).astype(BF16)

    return pl.pallas_call(
        body, name=name, grid=(t // ROWS,),
        in_specs=[pl.BlockSpec((ROWS, d), lambda i: (i, 0)), pl.BlockSpec((1, d), lambda i: (0, 0))],
        out_specs=pl.BlockSpec((ROWS, d), lambda i: (i, 0)),
        out_shape=jax.ShapeDtypeStruct((t, d), BF16), compiler_params=_params(("parallel",)))(x, g)


def rms_bwd(x, g, dy, dres, *, name):
    t, d = x.shape

    def body(x_ref, g_ref, dy_ref, dr_ref, dx_ref, dg_ref):
        @pl.when(pl.program_id(0) == 0)
        def _():
            dg_ref[...] = jnp.zeros_like(dg_ref)

        xv, dyv = x_ref[...], dy_ref[...].astype(F32)
        r = lax.rsqrt(jnp.mean(xv * xv, axis=-1, keepdims=True) + EPS)
        dyg = dyv * g_ref[...]
        dx = r * dyg - xv * (r * r * r) * jnp.mean(dyg * xv, axis=-1, keepdims=True)
        dx_ref[...] = dx + dr_ref[...]
        dg_ref[...] += jnp.sum(dyv * xv * r, axis=0, keepdims=True)

    row = pl.BlockSpec((ROWS, d), lambda i: (i, 0))
    vec = pl.BlockSpec((1, d), lambda i: (0, 0))
    return pl.pallas_call(
        body, name=name, grid=(t // ROWS,), in_specs=[row, vec, row, row], out_specs=[row, vec],
        out_shape=[jax.ShapeDtypeStruct((t, d), F32), jax.ShapeDtypeStruct((1, d), F32)],
        compiler_params=_params(("arbitrary",)))(x, g, dy, dres)


def loss_head(h, g, target, *, name):
    t, d = h.shape

    def body(x_ref, g_ref, t_ref, loss_ref, dx_ref, dg_ref):
        @pl.when(pl.program_id(0) == 0)
        def _():
            dg_ref[...] = jnp.zeros_like(dg_ref)
            loss_ref[...] = jnp.zeros_like(loss_ref)

        xv, gv = x_ref[...], g_ref[...]
        r = lax.rsqrt(jnp.mean(xv * xv, axis=-1, keepdims=True) + EPS)
        e = xv * r * gv - t_ref[...]
        loss_ref[...] += 0.5 * jnp.sum(jnp.mean(e * e, axis=-1, keepdims=True), axis=0, keepdims=True)
        dyv = e * (1.0 / d)
        dyg = dyv * gv
        dx_ref[...] = r * dyg - xv * (r * r * r) * jnp.mean(dyg * xv, axis=-1, keepdims=True)
        dg_ref[...] += jnp.sum(dyv * xv * r, axis=0, keepdims=True)

    row = pl.BlockSpec((ROWS, d), lambda i: (i, 0))
    vec = pl.BlockSpec((1, d), lambda i: (0, 0))
    return pl.pallas_call(
        body, name=name, grid=(t // ROWS,), in_specs=[row, vec, row],
        out_specs=[pl.BlockSpec((1, 128), lambda i: (0, 0)), row, vec],
        out_shape=[jax.ShapeDtypeStruct((1, 128), F32), jax.ShapeDtypeStruct((t, d), F32),
                   jax.ShapeDtypeStruct((1, d), F32)],
        compiler_params=_params(("arbitrary",)))(h, g, target)


def _tril_mask():
    r = lax.broadcasted_iota(jnp.int32, (C_CHUNK, C_CHUNK), 0)
    c = lax.broadcasted_iota(jnp.int32, (C_CHUNK, C_CHUNK), 1)
    return r >= c


def _layer_norm_parts(v):
    mu = jnp.mean(v, axis=-1, keepdims=True)
    vc = v - mu
    rstd = lax.rsqrt(jnp.mean(vc * vc, axis=-1, keepdims=True) + EPS)
    return vc * rstd, rstd


def gmlp_fwd(zpre, ln_g, ln_b, ws, bs_t, *, name):
    t = zpre.shape[0]
    d = D_MODEL

    def body(zu_ref, zv_ref, g_ref, b_ref, ws_ref, bs_ref, o_ref):
        u = _gelu(zu_ref[...])
        vhat, _ = _layer_norm_parts(_gelu(zv_ref[...]))
        vln = (vhat * g_ref[...] + b_ref[...]).astype(BF16)
        mask = _tril_mask()
        for gi in range(C_GROUPS):
            sl = slice(gi * C_GROUP_DIM, (gi + 1) * C_GROUP_DIM)
            w = jnp.where(mask, ws_ref[gi], 0.0).astype(BF16)
            mixed = _dot(w, vln[:, sl], NN) + bs_ref[:, gi:gi + 1]
            o_ref[:, sl] = (u[:, sl] * mixed).astype(BF16)

    vec = pl.BlockSpec((1, d), lambda i: (0, 0))
    return pl.pallas_call(
        body, name=name, grid=(t // C_CHUNK,),
        in_specs=[pl.BlockSpec((C_CHUNK, d), lambda i: (i, 0)), pl.BlockSpec((C_CHUNK, d), lambda i: (i, 1)),
                  vec, vec, pl.BlockSpec((C_GROUPS, C_CHUNK, C_CHUNK), lambda i: (0, 0, 0)),
                  pl.BlockSpec((C_CHUNK, 128), lambda i: (0, 0))],
        out_specs=pl.BlockSpec((C_CHUNK, d), lambda i: (i, 0)),
        out_shape=jax.ShapeDtypeStruct((t, d), BF16), compiler_params=_params(("parallel",)))(
            zpre, zpre, ln_g, ln_b, ws, bs_t)


def gmlp_bwd(zpre, dgated, ln_g, ln_b, ws, bs_t, *, name):
    t = zpre.shape[0]
    d = D_MODEL

    def body(zu_ref, zv_ref, dg_ref, g_ref, b_ref, ws_ref, bs_ref, dz_ref, dws_ref, dbs_ref, dlg_ref, dlb_ref):
        @pl.when(pl.program_id(0) == 0)
        def _():
            dws_ref[...] = jnp.zeros_like(dws_ref)
            dbs_ref[...] = jnp.zeros_like(dbs_ref)
            dlg_ref[...] = jnp.zeros_like(dlg_ref)
            dlb_ref[...] = jnp.zeros_like(dlb_ref)

        zu, zv = zu_ref[...], zv_ref[...]
        u = _gelu(zu)
        vhat, rstd = _layer_norm_parts(_gelu(zv))
        gam = g_ref[...]
        vln = (vhat * gam + b_ref[...]).astype(BF16)
        dgt = dg_ref[...].astype(F32)
        mask = _tril_mask()
        lane = lax.broadcasted_iota(jnp.int32, (C_CHUNK, 128), 1)
        dbs = jnp.zeros((C_CHUNK, 128), F32)
        du_parts, dvln_parts = [], []
        for gi in range(C_GROUPS):
            sl = slice(gi * C_GROUP_DIM, (gi + 1) * C_GROUP_DIM)
            w = jnp.where(mask, ws_ref[gi], 0.0).astype(BF16)
            mixed = _dot(w, vln[:, sl], NN) + bs_ref[:, gi:gi + 1]
            du_parts.append(dgt[:, sl] * mixed)
            dmixed = dgt[:, sl] * u[:, sl]
            dmb = dmixed.astype(BF16)
            dws_ref[gi] += jnp.where(mask, _dot(dmb, vln[:, sl], NT), 0.0)
            dbs = dbs + jnp.where(lane == gi, jnp.sum(dmixed, axis=-1, keepdims=True), 0.0)
            dvln_parts.append(_dot(w, dmb, TN))
        dbs_ref[...] += dbs
        du = jnp.concatenate(du_parts, axis=-1)
        dvln = jnp.concatenate(dvln_parts, axis=-1)
        dlg_ref[...] += jnp.sum(dvln * vhat, axis=0, keepdims=True)
        dlb_ref[...] += jnp.sum(dvln, axis=0, keepdims=True)
        dvhat = dvln * gam
        dv = rstd * (dvhat - jnp.mean(dvhat, axis=-1, keepdims=True)
                     - vhat * jnp.mean(dvhat * vhat, axis=-1, keepdims=True))
        dz_ref[:, :d] = (du * _dgelu(zu)).astype(BF16)
        dz_ref[:, d:] = (dv * _dgelu(zv)).astype(BF16)

    vec = pl.BlockSpec((1, d), lambda i: (0, 0))
    wsp = pl.BlockSpec((C_GROUPS, C_CHUNK, C_CHUNK), lambda i: (0, 0, 0))
    bsp = pl.BlockSpec((C_CHUNK, 128), lambda i: (0, 0))
    return pl.pallas_call(
        body, name=name, grid=(t // C_CHUNK,),
        in_specs=[pl.BlockSpec((C_CHUNK, d), lambda i: (i, 0)), pl.BlockSpec((C_CHUNK, d), lambda i: (i, 1)),
                  pl.BlockSpec((C_CHUNK, d), lambda i: (i, 0)), vec, vec, wsp, bsp],
        out_specs=[pl.BlockSpec((C_CHUNK, 2 * d), lambda i: (i, 0)), wsp, bsp, vec, vec],
        out_shape=[jax.ShapeDtypeStruct((t, 2 * d), BF16), jax.ShapeDtypeStruct((C_GROUPS, C_CHUNK, C_CHUNK), F32),
                   jax.ShapeDtypeStruct((C_CHUNK, 128), F32), jax.ShapeDtypeStruct((1, d), F32),
                   jax.ShapeDtypeStruct((1, d), F32)],
        compiler_params=_params(("arbitrary",)))(zpre, zpre, dgated, ln_g, ln_b, ws, bs_t)


ATT_SCALE = A_HEAD_DIM ** -0.5
PAIRS = A_HEADS // 2
PAIRS_PER_KV = PAIRS // A_KV_HEADS


def _att_padded(tile):
    lo = lax.broadcasted_iota(jnp.int32, tile.shape, 1) < A_HEAD_DIM
    rolled = pltpu.roll(tile, A_HEAD_DIM, 1)
    zero = jnp.zeros_like(tile)
    return {(0, 0): jnp.where(lo, tile, zero).astype(BF16), (0, 1): jnp.where(lo, zero, rolled).astype(BF16),
            (1, 0): jnp.where(lo, rolled, zero).astype(BF16), (1, 1): jnp.where(lo, zero, tile).astype(BF16)}


def _att_valid(n):
    r = lax.broadcasted_iota(jnp.int32, (WINDOW, 2 * WINDOW), 0)
    c = lax.broadcasted_iota(jnp.int32, (WINDOW, 2 * WINDOW), 1)
    rel = r + WINDOW - c
    return (rel >= 0) & (rel < WINDOW) & ((c >= WINDOW) | (n > 0))


def _att_probs(qp, kpad, sink, valid):
    s = jnp.where(valid, _dot(qp, kpad, NT), NEG_INF)
    m = jnp.maximum(jnp.max(s, axis=-1, keepdims=True), sink)
    p = jnp.exp(s - m)
    e_sink = jnp.exp(sink - m)
    inv = 1.0 / (jnp.sum(p, axis=-1, keepdims=True) + e_sink)
    return p * inv, e_sink * inv


def _att_operands(q_ref, kvc_ref, kvp_ref, s_ref):
    kv = jnp.concatenate([kvp_ref[...], kvc_ref[...]], axis=0)
    kpad, vpad = _att_padded(kv[:, :128]), _att_padded(kv[:, 128:])
    key = lambda h: ((h // 2) // PAIRS_PER_KV, h % 2)
    pairs = [(q_ref[:, j * 128:(j + 1) * 128] * ATT_SCALE).astype(BF16) for j in range(PAIRS)]
    q = jnp.stack([pairs[h // 2] for h in range(A_HEADS)])
    k = jnp.stack([kpad[key(h)] for h in range(A_HEADS)])
    v = jnp.stack([vpad[key(h)] for h in range(A_HEADS)])
    sink = jnp.stack([s_ref[:, h:h + 1] for h in range(A_HEADS)])
    return q, k, v, sink


def _att_specs(t):
    return [pl.BlockSpec((WINDOW, A_Q), lambda n: (n, 0)),
            pl.BlockSpec((WINDOW, 2 * A_KV), lambda n: (n, COL_KV // (2 * A_KV))),
            pl.BlockSpec((WINDOW, 2 * A_KV), lambda n: (jnp.maximum(n - 1, 0), COL_KV // (2 * A_KV))),
            pl.BlockSpec((1, 128), lambda n: (0, 0))]


def att_fwd(proj, sinks, *, name):
    t = proj.shape[0]

    def body(q_ref, kvc_ref, kvp_ref, s_ref, o_ref):
        n = pl.program_id(0)
        q, k, v, sink = _att_operands(q_ref, kvc_ref, kvp_ref, s_ref)
        w, _ = _att_probs(q, k, sink, _att_valid(n))
        o = _dot(w.astype(BF16), v, NN)
        for j in range(PAIRS):
            o_ref[:, j * 128:(j + 1) * 128] = (o[2 * j] + o[2 * j + 1]).astype(BF16)

    return pl.pallas_call(
        body, name=name, grid=(t // WINDOW,), in_specs=_att_specs(t),
        out_specs=pl.BlockSpec((WINDOW, A_Q), lambda n: (n, 0)),
        out_shape=jax.ShapeDtypeStruct((t, A_Q), BF16), compiler_params=_params(("parallel",)))(
            proj, proj, proj, sinks)


def att_bwd(proj, sinks, dout, *, name):
    t = proj.shape[0]

    def body(q_ref, kvc_ref, kvp_ref, s_ref, do_ref, dq_ref, dkc_ref, dkp_ref, ds_ref):
        n = pl.program_id(0)

        @pl.when(n == 0)
        def _():
            ds_ref[...] = jnp.zeros_like(ds_ref)

        q, k, v, sink = _att_operands(q_ref, kvc_ref, kvp_ref, s_ref)
        dop = jnp.stack([do_ref[:, (h // 2) * 128:(h // 2 + 1) * 128] for h in range(A_HEADS)]).astype(BF16)
        w, w_sink = _att_probs(q, k, sink, _att_valid(n))
        dw = _dot(dop, v, NT)
        delta = jnp.sum(w * dw, axis=-1, keepdims=True)
        dsc = (w * (dw - delta)).astype(BF16)
        dsink_h = -jnp.sum(w_sink * delta, axis=1, keepdims=True)
        dq = _dot(dsc, k, NN)
        dk_h = _dot(dsc, q, TN)
        dv_h = _dot(w.astype(BF16), dop, TN)
        lane = lax.broadcasted_iota(jnp.int32, (1, 128), 1)
        dsink = jnp.zeros((1, 128), F32)
        for h in range(A_HEADS):
            dsink = dsink + jnp.where(lane == h, dsink_h[h], 0.0)
        ds_ref[...] += dsink
        for j in range(PAIRS):
            dq_ref[:, j * 128:(j + 1) * 128] = ((dq[2 * j] + dq[2 * j + 1]) * ATT_SCALE).astype(BF16)
        lo = lax.broadcasted_iota(jnp.int32, (2 * WINDOW, 128), 1) < A_HEAD_DIM
        heads_per_kv = A_HEADS // A_KV_HEADS

        def tile(per_head):
            acc = {}
            for kvh in range(A_KV_HEADS):
                for half in range(2):
                    hs = range(kvh * heads_per_kv + half, (kvh + 1) * heads_per_kv, 2)
                    acc[(kvh, half)] = functools.reduce(lambda a, b: a + b, [per_head[h] for h in hs])
            return jnp.where(lo, acc[(0, 0)] + pltpu.roll(acc[(0, 1)], A_HEAD_DIM, 1),
                             pltpu.roll(acc[(1, 0)], A_HEAD_DIM, 1) + acc[(1, 1)])

        dkv = jnp.concatenate([tile(dk_h), tile(dv_h)], axis=1)
        dkp_ref[...] = dkv[:WINDOW]
        dkc_ref[...] = dkv[WINDOW:]

    kvo = pl.BlockSpec((WINDOW, 2 * A_KV), lambda n: (n, 0))
    return pl.pallas_call(
        body, name=name, grid=(t // WINDOW,),
        in_specs=_att_specs(t) + [pl.BlockSpec((WINDOW, A_Q), lambda n: (n, 0))],
        out_specs=[pl.BlockSpec((WINDOW, A_Q), lambda n: (n, 0)), kvo, kvo, pl.BlockSpec((1, 128), lambda n: (0, 0))],
        out_shape=[jax.ShapeDtypeStruct((t, A_Q), BF16), jax.ShapeDtypeStruct((t, 2 * A_KV), F32),
                   jax.ShapeDtypeStruct((t, 2 * A_KV), F32), jax.ShapeDtypeStruct((1, 128), F32)],
        compiler_params=_params(("arbitrary",)))(proj, proj, proj, sinks, dout)


QK_SCALE = B_HEAD_DIM ** -0.5
PREP_COLS = 256
PREP_NCB = 3 * B_W // PREP_COLS
HALO = 8


def _roll_rows(x, shift):
    n = x.shape[0]
    return x if shift % n == 0 else pltpu.roll(x, shift % n, 0)


def _conv_taps(xe, w):
    xs = [_roll_rows(xe, CONV_K - 1 - i) for i in range(CONV_K)]
    c = w[0:1] * xs[0]
    for i in range(1, CONV_K):
        c = c + w[i:i + 1] * xs[i]
    return xs, c


def dprep_fwd(proj, conv_w, *, name):
    t = proj.shape[0]
    tt = ROWS
    col0 = COL_QKVB // PREP_COLS

    def body(x_ref, h_ref, w_ref, o_ref):
        cb, n = pl.program_id(0), pl.program_id(1)
        halo = jnp.where(n > 0, h_ref[...], 0.0)
        xe = jnp.concatenate([halo, x_ref[...]], axis=0)
        _, c = _conv_taps(xe, w_ref[...])
        y = _silu(c)[HALO:]
        parts = []
        for hh in range(PREP_COLS // B_HEAD_DIM):
            yh = y[:, hh * B_HEAD_DIM:(hh + 1) * B_HEAD_DIM]
            parts.append(yh * lax.rsqrt(jnp.sum(yh * yh, axis=-1, keepdims=True) + EPS))
        nrm = jnp.concatenate(parts, axis=-1)
        o_ref[...] = jnp.where(cb < 4, nrm * QK_SCALE, jnp.where(cb < 8, nrm, y))

    return pl.pallas_call(
        body, name=name, grid=(PREP_NCB, t // tt),
        in_specs=[pl.BlockSpec((tt, PREP_COLS), lambda cb, n: (n, col0 + cb)),
                  pl.BlockSpec((HALO, PREP_COLS), lambda cb, n: (jnp.maximum(n * (tt // HALO) - 1, 0), col0 + cb)),
                  pl.BlockSpec((CONV_K, PREP_COLS), lambda cb, n: (0, cb))],
        out_specs=pl.BlockSpec((tt, PREP_COLS), lambda cb, n: (n, cb)),
        out_shape=jax.ShapeDtypeStruct((t, 3 * B_W), F32), compiler_params=_params(("parallel", "parallel")))(
            proj, proj, conv_w)


def dprep_bwd(proj, conv_w, dqkvn, *, name):
    t = proj.shape[0]
    tt = ROWS
    nb = t // tt
    col0 = COL_QKVB // PREP_COLS
    n8 = t // HALO

    def body(xc_ref, xb_ref, xa_ref, dc_ref, da_ref, w_ref, dx_ref, dw_ref):
        cb, n = pl.program_id(0), pl.program_id(1)

        @pl.when(n == 0)
        def _():
            dw_ref[...] = jnp.zeros_like(dw_ref)

        w = w_ref[...]
        xe = jnp.concatenate([jnp.where(n > 0, xb_ref[...], 0.0), xc_ref[...], xa_ref[...]], axis=0)
        xs, c = _conv_taps(xe, w)
        sg = _sigmoid(c)
        y = c * sg
        dout = jnp.concatenate([jnp.zeros((HALO, PREP_COLS), F32), dc_ref[...],
                                jnp.where(n < nb - 1, da_ref[...], 0.0)], axis=0)
        dsc = jnp.where(cb < 4, QK_SCALE, 1.0)
        parts = []
        for hh in range(PREP_COLS // B_HEAD_DIM):
            sl = slice(hh * B_HEAD_DIM, (hh + 1) * B_HEAD_DIM)
            yh, doh = y[:, sl], dout[:, sl] * dsc
            r = lax.rsqrt(jnp.sum(yh * yh, axis=-1, keepdims=True) + EPS)
            parts.append(doh * r - yh * (r * r * r) * jnp.sum(doh * yh, axis=-1, keepdims=True))
        dy = jnp.where(cb < 8, jnp.concatenate(parts, axis=-1), dout)
        dcv = dy * sg * (1.0 + c * (1.0 - sg))
        dxe = w[CONV_K - 1:CONV_K] * dcv
        for i in range(CONV_K - 1):
            dxe = dxe + w[i:i + 1] * _roll_rows(dcv, -(CONV_K - 1 - i))
        dx_ref[...] = dxe[HALO:HALO + tt].astype(BF16)
        for i in range(CONV_K):
            dw_ref[i:i + 1, :] += jnp.sum((dcv * xs[i])[HALO:HALO + tt], axis=0, keepdims=True)

    def after(n):
        return jnp.minimum((n + 1) * (tt // HALO), n8 - 1)

    return pl.pallas_call(
        body, name=name, grid=(PREP_NCB, nb),
        in_specs=[pl.BlockSpec((tt, PREP_COLS), lambda cb, n: (n, col0 + cb)),
                  pl.BlockSpec((HALO, PREP_COLS), lambda cb, n: (jnp.maximum(n * (tt // HALO) - 1, 0), col0 + cb)),
                  pl.BlockSpec((HALO, PREP_COLS), lambda cb, n: (after(n), col0 + cb)),
                  pl.BlockSpec((tt, PREP_COLS), lambda cb, n: (n, cb)),
                  pl.BlockSpec((HALO, PREP_COLS), lambda cb, n: (after(n), cb)),
                  pl.BlockSpec((CONV_K, PREP_COLS), lambda cb, n: (0, cb))],
        out_specs=[pl.BlockSpec((tt, PREP_COLS), lambda cb, n: (n, cb)),
                   pl.BlockSpec((CONV_K, PREP_COLS), lambda cb, n: (0, cb))],
        out_shape=[jax.ShapeDtypeStruct((t, 3 * B_W), BF16), jax.ShapeDtypeStruct((CONV_K, 3 * B_W), F32)],
        compiler_params=_params(("parallel", "arbitrary")))(proj, proj, proj, dqkvn, dqkvn, conv_w)


def _softplus(z):
    return jnp.maximum(z, 0.0) + jnp.log(1.0 + jnp.exp(-jnp.abs(z)))


def gates_fwd(proj, alog_pad, dtb_pad, *, name):
    t = proj.shape[0]

    def body(x_ref, a_ref, b_ref, o_ref):
        raw = x_ref[...]
        lane = lax.broadcasted_iota(jnp.int32, raw.shape, 1)
        g = -jnp.exp(a_ref[...]) * _softplus(raw + b_ref[...])
        o_ref[...] = jnp.where(lane < B_HEADS, _sigmoid(raw), jnp.where(lane < 2 * B_HEADS, g, 0.0))

    vec = pl.BlockSpec((1, 128), lambda n: (0, 0))
    return pl.pallas_call(
        body, name=name, grid=(t // ROWS,),
        in_specs=[pl.BlockSpec((ROWS, 128), lambda n: (n, COL_GATE // 128)), vec, vec],
        out_specs=pl.BlockSpec((ROWS, 128), lambda n: (n, 0)),
        out_shape=jax.ShapeDtypeStruct((t, 128), F32), compiler_params=_params(("parallel",)))(
            proj, alog_pad, dtb_pad)


def gates_bwd(proj, alog_pad, dtb_pad, dgates, *, name):
    t = proj.shape[0]

    def body(x_ref, a_ref, b_ref, dg_ref, dx_ref, da_ref, db_ref):
        @pl.when(pl.program_id(0) == 0)
        def _():
            da_ref[...] = jnp.zeros_like(da_ref)
            db_ref[...] = jnp.zeros_like(db_ref)

        raw, dgt = x_ref[...], dg_ref[...]
        lane = lax.broadcasted_iota(jnp.int32, raw.shape, 1)
        is_beta, is_g = lane < B_HEADS, (lane >= B_HEADS) & (lane < 2 * B_HEADS)
        beta = _sigmoid(raw)
        z = raw + b_ref[...]
        neg_a = -jnp.exp(a_ref[...])
        d_z = jnp.where(is_g, dgt * neg_a * _sigmoid(z), 0.0)
        dx_ref[...] = jnp.where(is_beta, dgt * beta * (1.0 - beta), d_z).astype(BF16)
        db_ref[...] += jnp.sum(d_z, axis=0, keepdims=True)
        da_ref[...] += jnp.sum(jnp.where(is_g, dgt * neg_a * _softplus(z), 0.0), axis=0, keepdims=True)

    vec = pl.BlockSpec((1, 128), lambda n: (0, 0))
    row = pl.BlockSpec((ROWS, 128), lambda n: (n, 0))
    return pl.pallas_call(
        body, name=name, grid=(t // ROWS,),
        in_specs=[pl.BlockSpec((ROWS, 128), lambda n: (n, COL_GATE // 128)), vec, vec, row],
        out_specs=[row, vec, vec],
        out_shape=[jax.ShapeDtypeStruct((t, 128), BF16), jax.ShapeDtypeStruct((1, 128), F32),
                   jax.ShapeDtypeStruct((1, 128), F32)],
        compiler_params=_params(("arbitrary",)))(proj, alog_pad, dtb_pad, dgates)


def _split2(a):
    hi = a.astype(BF16)
    return hi, (a - hi.astype(F32)).astype(BF16)


def _dotp(a, b, dims, passes):
    if passes == 1:
        return _dot(a.astype(BF16), b.astype(BF16), dims)
    ah, al = _split2(a)
    bh, bl = _split2(b)
    return _dot(ah, bh, dims) + (_dot(ah, bl, dims) + _dot(al, bh, dims))


_GRAD_DIMS = {NN: ((NT, False), (TN, False)), NT: ((NN, False), (TN, True)), TN: ((NT, True), (NN, False))}


def _make_mm(dims, passes):
    (da_dims, da_swap), (db_dims, db_swap) = _GRAD_DIMS[dims]

    @jax.custom_vjp
    def mm(a, b):
        return _dotp(a, b, dims, passes)

    def fwd(a, b):
        return _dotp(a, b, dims, passes), (a, b)

    def bwd(saved, ct):
        a, b = saved
        da = _dotp(b, ct, da_dims, passes) if da_swap else _dotp(ct, b, da_dims, passes)
        db = _dotp(ct, a, db_dims, passes) if db_swap else _dotp(a, ct, db_dims, passes)
        return da, db

    mm.defvjp(fwd, bwd)
    return mm


MM1 = {d: _make_mm(d, 1) for d in (NN, NT, TN)}
MM3 = {d: _make_mm(d, 3) for d in (NN, NT, TN)}


def _tri_ones(lower):
    r = lax.broadcasted_iota(jnp.int32, (DN_CHUNK, DN_CHUNK), 0)
    c = lax.broadcasted_iota(jnp.int32, (DN_CHUNK, DN_CHUNK), 1)
    return (r >= c if lower else r <= c).astype(BF16)


def _tri_sum(x, lower):
    tri = _tri_ones(lower)
    hi = x.astype(BF16)
    r1 = x - hi.astype(F32)
    mid = r1.astype(BF16)
    lo = (r1 - mid.astype(F32)).astype(BF16)
    return _dot(tri, hi, NN) + (_dot(tri, mid, NN) + _dot(tri, lo, NN))


def _delta_chunk(s0, q, k, v, beta, gam_c, gam_r):
    c = DN_CHUNK
    r = lax.broadcasted_iota(jnp.int32, (c, c), 0)
    cc = lax.broadcasted_iota(jnp.int32, (c, c), 1)
    incl, strict = r >= cc, r > cc
    eye = (r == cc).astype(F32)
    decay = jnp.exp(jnp.where(incl, gam_c - gam_r, NEG_INF))
    g_last = gam_c[:, c - 1:c, :]
    e_gam, e_rest, e_last = jnp.exp(gam_c), jnp.exp(g_last - gam_c), jnp.exp(g_last)
    a_neg = -jnp.where(strict, beta * MM1[NT](k, k) * decay, 0.0)
    inv = eye + a_neg
    pw = a_neg
    for _ in range(5):
        pw = MM3[NN](pw, pw)
        inv = inv + MM3[NN](inv, pw)
    uw = MM3[NN](inv, jnp.concatenate([v * beta, k * (beta * e_gam)], axis=-1))
    u, w = uw[..., :B_HEAD_DIM], uw[..., B_HEAD_DIM:]
    qk = MM1[NT](q, k) * decay
    v_new = u - MM1[NN](w, s0)
    o = MM1[NN](q * e_gam, s0) + MM1[NN](qk, v_new)
    s1 = s0 * e_last + MM1[TN](k * e_rest, v_new)
    return s1, o


def _delta_operands(q_ref, k_ref, v_ref, gt):
    heads = lambda ref: jnp.stack([ref[:, h * B_HEAD_DIM:(h + 1) * B_HEAD_DIM] for h in range(B_HEADS)])
    gam = _tri_sum(gt, True)
    gam_t = gam.T
    beta = jnp.stack([gt[:, h:h + 1] for h in range(B_HEADS)])
    gam_c = jnp.stack([gam[:, B_HEADS + h:B_HEADS + h + 1] for h in range(B_HEADS)])
    gam_r = jnp.stack([gam_t[B_HEADS + h:B_HEADS + h + 1, :] for h in range(B_HEADS)])
    return heads(q_ref), heads(k_ref), heads(v_ref), beta, gam_c, gam_r


def delta_fwd(qkvn, gates, *, name):
    t = qkvn.shape[0]
    nc = t // DN_CHUNK

    def body(q_ref, k_ref, v_ref, g_ref, o_ref, ss_ref, state):
        @pl.when(pl.program_id(0) == 0)
        def _():
            state[...] = jnp.zeros_like(state)

        s0 = state[...]
        ss_ref[...] = s0
        s1, o = _delta_chunk(s0, *_delta_operands(q_ref, k_ref, v_ref, g_ref[...]))
        state[...] = s1
        for h in range(B_HEADS):
            o_ref[:, h * B_HEAD_DIM:(h + 1) * B_HEAD_DIM] = o[h]

    blk = lambda j: pl.BlockSpec((DN_CHUNK, B_W), lambda n: (n, j))
    return pl.pallas_call(
        body, name=name, grid=(nc,),
        in_specs=[blk(0), blk(1), blk(2), pl.BlockSpec((DN_CHUNK, 128), lambda n: (n, 0))],
        out_specs=[blk(0), pl.BlockSpec((None, B_HEADS, B_HEAD_DIM, B_HEAD_DIM), lambda n: (n, 0, 0, 0))],
        out_shape=[jax.ShapeDtypeStruct((t, B_W), F32),
                   jax.ShapeDtypeStruct((nc, B_HEADS, B_HEAD_DIM, B_HEAD_DIM), F32)],
        scratch_shapes=[pltpu.VMEM((B_HEADS, B_HEAD_DIM, B_HEAD_DIM), F32)],
        compiler_params=_params(("arbitrary",)))(qkvn, qkvn, qkvn, gates)


def delta_bwd(qkvn, gates, ssave, do, *, name):
    t = qkvn.shape[0]
    nc = t // DN_CHUNK

    def body(q_ref, k_ref, v_ref, g_ref, ss_ref, do_ref, dx_ref, dg_ref, dstate):
        @pl.when(pl.program_id(0) == 0)
        def _():
            dstate[...] = jnp.zeros_like(dstate)

        lane = lax.broadcasted_iota(jnp.int32, (DN_CHUNK, 128), 1)
        row = lax.broadcasted_iota(jnp.int32, (128, DN_CHUNK), 0)
        dbeta_all = jnp.zeros((DN_CHUNK, 128), F32)
        dgam_c_all = jnp.zeros((DN_CHUNK, 128), F32)
        dgam_r_all = jnp.zeros((128, DN_CHUNK), F32)
        _, vjp = jax.vjp(_delta_chunk, ss_ref[...], *_delta_operands(q_ref, k_ref, v_ref, g_ref[...]))
        do = jnp.stack([do_ref[:, h * B_HEAD_DIM:(h + 1) * B_HEAD_DIM] for h in range(B_HEADS)])
        ds0, dq, dk, dv, dbeta, dgam_c, dgam_r = vjp((dstate[...], do))
        dstate[...] = ds0
        for h in range(B_HEADS):
            dx_ref[:, h * B_HEAD_DIM:(h + 1) * B_HEAD_DIM] = dq[h]
            dx_ref[:, B_W + h * B_HEAD_DIM:B_W + (h + 1) * B_HEAD_DIM] = dk[h]
            dx_ref[:, 2 * B_W + h * B_HEAD_DIM:2 * B_W + (h + 1) * B_HEAD_DIM] = dv[h]
            dbeta_all = dbeta_all + jnp.where(lane == h, dbeta[h], 0.0)
            dgam_c_all = dgam_c_all + jnp.where(lane == B_HEADS + h, dgam_c[h], 0.0)
            dgam_r_all = dgam_r_all + jnp.where(row == B_HEADS + h, dgam_r[h], 0.0)
        dg_ref[...] = dbeta_all + _tri_sum(dgam_c_all + dgam_r_all.T, False)

    blk = lambda j: pl.BlockSpec((DN_CHUNK, B_W), lambda n: (nc - 1 - n, j))
    gsp = pl.BlockSpec((DN_CHUNK, 128), lambda n: (nc - 1 - n, 0))
    return pl.pallas_call(
        body, name=name, grid=(nc,),
        in_specs=[blk(0), blk(1), blk(2), gsp,
                  pl.BlockSpec((None, B_HEADS, B_HEAD_DIM, B_HEAD_DIM), lambda n: (nc - 1 - n, 0, 0, 0)), blk(0)],
        out_specs=[pl.BlockSpec((DN_CHUNK, 3 * B_W), lambda n: (nc - 1 - n, 0)), gsp],
        out_shape=[jax.ShapeDtypeStruct((t, 3 * B_W), F32), jax.ShapeDtypeStruct((t, 128), F32)],
        scratch_shapes=[pltpu.VMEM((B_HEADS, B_HEAD_DIM, B_HEAD_DIM), F32)],
        compiler_params=_params(("arbitrary",)))(qkvn, qkvn, qkvn, gates, ssave, do)


def gnorm_fwd(o, proj, onorm, *, name):
    t = o.shape[0]

    def body(o_ref, z_ref, w_ref, out_ref):
        ov = o_ref[...]
        r = lax.rsqrt(jnp.mean(ov * ov, axis=-1, keepdims=True) + EPS)
        out_ref[...] = (ov * r * w_ref[...] * _silu(z_ref[...])).astype(BF16)

    blk = pl.BlockSpec((ROWS, B_HEAD_DIM), lambda n, h: (n, h))
    return pl.pallas_call(
        body, name=name, grid=(t // ROWS, B_HEADS),
        in_specs=[blk, pl.BlockSpec((ROWS, B_HEAD_DIM), lambda n, h: (n, COL_Z // B_HEAD_DIM + h)),
                  pl.BlockSpec((1, B_HEAD_DIM), lambda n, h: (0, 0))],
        out_specs=blk, out_shape=jax.ShapeDtypeStruct((t, B_W), BF16),
        compiler_params=_params(("parallel", "parallel")))(o, proj, onorm)


def gnorm_bwd(o, proj, onorm, dout, *, dcol0, name):
    t = o.shape[0]

    def body(o_ref, z_ref, w_ref, d_ref, do_ref, dz_ref, dw_ref):
        @pl.when((pl.program_id(0) == 0) & (pl.program_id(1) == 0))
        def _():
            dw_ref[...] = jnp.zeros_like(dw_ref)

        ov, zv, wv, dv = o_ref[...], z_ref[...], w_ref[...], d_ref[...].astype(F32)
        r = lax.rsqrt(jnp.mean(ov * ov, axis=-1, keepdims=True) + EPS)
        nrm = ov * r
        dz_ref[...] = (dv * nrm * wv * _dsilu(zv)).astype(BF16)
        da = dv * _silu(zv)
        dw_ref[...] += jnp.sum(da * nrm, axis=0, keepdims=True)
        dn = da * wv
        do_ref[...] = r * dn - ov * (r * r * r) * jnp.mean(dn * ov, axis=-1, keepdims=True)

    blk = pl.BlockSpec((ROWS, B_HEAD_DIM), lambda n, h: (n, h))
    vec = pl.BlockSpec((1, B_HEAD_DIM), lambda n, h: (0, 0))
    return pl.pallas_call(
        body, name=name, grid=(t // ROWS, B_HEADS),
        in_specs=[blk, pl.BlockSpec((ROWS, B_HEAD_DIM), lambda n, h: (n, COL_Z // B_HEAD_DIM + h)), vec,
                  pl.BlockSpec((ROWS, B_HEAD_DIM), lambda n, h: (n, dcol0 // B_HEAD_DIM + h))],
        out_specs=[blk, blk, vec],
        out_shape=[jax.ShapeDtypeStruct((t, B_W), F32), jax.ShapeDtypeStruct((t, B_W), BF16),
                   jax.ShapeDtypeStruct((1, B_HEAD_DIM), F32)],
        compiler_params=_params(("arbitrary", "arbitrary")))(o, proj, onorm, dout)


def _ffn_fwd(h, norm_g, wg, wu, wd, tm, tag):
    hn = rms_fwd(h, norm_g, name=f"ffn{tag}_norm")
    gate, up, act = mm_gate_up(hn, wg, wu, tm=tm, tn=1408, tk=512, name=f"ffn{tag}_gate_up")
    h_out = mm_nn(act, wd, tm=tm, tn=512, tk=1408, out_dtype=F32, res=h, name=f"ffn{tag}_down")
    return h_out, (hn, gate, up, act)


def _ffn_bwd(dh, h, norm_g, wg, wu, wd, saved, tm, tag, emit):
    hn, gate, up, act = saved
    dwd = mm_tn(act, dh, shards=1, tm=tm, tn=512, tk=1408, out_dtype=BF16, name=f"ffn{tag}_dwd")[0]
    dgate, dup = mm_down_bwd(dh, wd, gate, up, tm=tm, tn=512, tk=2048, name=f"ffn{tag}_dact")
    dwg = mm_tn(hn, dgate, shards=N_SHARD, tm=tm, tn=1408, tk=1024, out_dtype=BF16, name=f"ffn{tag}_dwg")
    dwu = mm_tn(hn, dup, shards=N_SHARD, tm=tm, tn=1408, tk=1024, out_dtype=BF16, name=f"ffn{tag}_dwu")
    started = emit(f"ffn{tag}", {"gate": dwg, "up": dwu, "down": dwd})
    dhn = mm_nt(dgate, wg, tm=tm, tn=512, tk=1408, out_dtype=F32, name=f"ffn{tag}_dhn_g")
    dhn = mm_nt(dup, wu, tm=tm, tn=512, tk=1408, out_dtype=F32, res=dhn, name=f"ffn{tag}_dhn_u")
    dh_in, dnorm = rms_bwd(h, norm_g + started, dhn, dh, name=f"ffn{tag}_dnorm")
    return dh_in, dnorm


def _local_step(x, target, w, get, emit):
    t = x.shape[0]
    tm = min(1024, t)
    g = {}

    hn0 = rms_fwd(x, w["even_norm"], name="l0_norm")
    w.update(get("even_in", hn0))
    proj = mm_nn(hn0, w["even_w_in"], tm=tm, tn=512, tk=2048, out_dtype=F32, name="l0_w_in")
    out_a = att_fwd(proj, w["sinks"], name="l0_att")
    qkvn = dprep_fwd(proj, w["even_conv"], name="l0_prep")
    gates = gates_fwd(proj, w["a_log"], w["dt_bias"], name="l0_gates")
    o_delta, ssave = delta_fwd(qkvn, gates, name="l0_delta")
    out_b = gnorm_fwd(o_delta, proj, w["onorm"], name="l0_gnorm")
    mix0 = jnp.concatenate([out_a, out_b], axis=-1)
    w.update(get("even_out", mix0))
    h1 = mm_nn(mix0, w["even_w_out"], tm=tm, tn=512, tk=2048, out_dtype=F32, res=x, name="l0_w_out")
    f0 = get("ffn0", h1)
    h2, ffn0 = _ffn_fwd(h1, w["ffn_norm"][0:1], f0["gate"], f0["up"], f0["down"], tm, 0)
    hn2 = rms_fwd(h2, w["odd_norm"], name="l1_norm")
    w.update(get("odd", hn2))
    zpre = mm_nn(hn2, w["odd_w_in"], tm=tm, tn=1024, tk=2048, out_dtype=F32, name="l1_w_in")
    gated = gmlp_fwd(zpre, w["odd_ln_g"], w["odd_ln_b"], w["odd_w_s"], w["odd_b_s"], name="l1_gmlp")
    h3 = mm_nn(gated, w["odd_w_out"], tm=tm, tn=512, tk=2048, out_dtype=F32, res=h2, name="l1_w_out")
    f1 = get("ffn1", h3)
    h4, ffn1 = _ffn_fwd(h3, w["ffn_norm"][1:2], f1["gate"], f1["up"], f1["down"], tm, 1)
    loss, dh4, g["final_norm"] = loss_head(h4, w["final_norm"], target, name="loss_head")

    dh3, dn1 = _ffn_bwd(dh4, h3, w["ffn_norm"][1:2], f1["gate"], f1["up"], f1["down"], ffn1, tm, 1, emit)
    dw_out_o = mm_tn(gated, dh3, shards=1, tm=tm, tn=512, tk=1024, out_dtype=BF16, name="l1_dw_out")[0]
    dgated = mm_nt(dh3, w["odd_w_out"], tm=tm, tn=512, tk=2048, out_dtype=BF16, name="l1_dgated")
    dzpre, g["odd_w_s"], g["odd_b_s"], g["odd_ln_g"], g["odd_ln_b"] = gmlp_bwd(
        zpre, dgated, w["odd_ln_g"], w["odd_ln_b"], w["odd_w_s"], w["odd_b_s"], name="l1_dgmlp")
    dw_in_o = mm_tn(hn2, dzpre, shards=N_SHARD, tm=tm, tn=1024, tk=1024, out_dtype=BF16, name="l1_dw_in")
    started = emit("odd", {"odd_w_in": dw_in_o, "odd_w_out": dw_out_o})
    dhn2 = mm_nt(dzpre, w["odd_w_in"], tm=tm, tn=512, tk=1024, out_dtype=F32, name="l1_dhn")
    dh2, g["odd_norm"] = rms_bwd(h2, w["odd_norm"] + started, dhn2, dh3, name="l1_dnorm")
    dh1, dn0 = _ffn_bwd(dh2, h1, w["ffn_norm"][0:1], f0["gate"], f0["up"], f0["down"], ffn0, tm, 0, emit)
    g["ffn_norm"] = jnp.concatenate([dn0, dn1], axis=0)
    dw_out_e = mm_tn(mix0, dh1, shards=1, tm=tm, tn=512, tk=1024, out_dtype=BF16, name="l0_dw_out")[0]
    started = emit("even_out", {"even_w_out": dw_out_e})
    dmix = mm_nt(dh1, w["even_w_out"], tm=tm, tn=512, tk=2048, out_dtype=F32, name="l0_dmix")
    dq_a, dkv_cur, dkv_prev, g["sinks"] = att_bwd(proj, w["sinks"] + started, dmix, name="l0_datt")
    dkv = dkv_cur + jnp.concatenate([dkv_prev[WINDOW:], jnp.zeros((WINDOW, 2 * A_KV), F32)], axis=0)
    do_delta, dz, g["onorm"] = gnorm_bwd(o_delta, proj, w["onorm"], dmix, dcol0=A_Q, name="l0_dgnorm")
    dqkvn, dgates = delta_bwd(qkvn, gates, ssave, do_delta, name="l0_ddelta")
    dqkv_b, g["even_conv"] = dprep_bwd(proj, w["even_conv"], dqkvn, name="l0_dprep")
    draw, g["a_log"], g["dt_bias"] = gates_bwd(proj, w["a_log"], w["dt_bias"], dgates, name="l0_dgates")
    dproj = jnp.concatenate([dq_a, dkv.astype(BF16), dqkv_b, dz, draw,
                             jnp.zeros((t, EVEN_IN_PAD - COL_GATE - 128), BF16)], axis=-1)
    dw_in_e = mm_tn(hn0, dproj, shards=1, tm=tm, tn=512, tk=1024, out_dtype=BF16, name="l0_dw_in")[0]
    dhn0 = mm_nt(dproj, w["even_w_in"], tm=tm, tn=512, tk=2816, out_dtype=F32, name="l0_dhn")
    grad_x, g["even_norm"] = rms_bwd(x, w["even_norm"], dhn0, dh1, name="l0_dnorm")
    emit("even_in", {"even_w_in": dw_in_e, "small": g})
    return loss, grad_x


ANY = pl.BlockSpec(memory_space=pl.ANY)
N_DEV = 8


def _place():
    return lax.axis_index("x"), lax.axis_index("y"), lax.axis_index("c")


def _chip_peers(x, y, c):
    return [((1 - x, y, c), 2 * (1 - x) + y), ((x, 1 - y, c), 2 * x + 1 - y), ((1 - x, 1 - y, c), 2 * (1 - x) + 1 - y)]


HBM = pl.BlockSpec(memory_space=pltpu.HBM)
SEM = pl.BlockSpec(memory_space=pltpu.SEMAPHORE)
EFFECT = pltpu.SideEffectType.DATAFLOW_SIDE_EFFECTING
N_PEER = 3


def _gather_plan(srcs, lands, send, recv):
    x, y, c = _place()
    return [pltpu.make_async_remote_copy(src_ref=srcs[i], dst_ref=lands[i].at[2 * x + y], send_sem=send.at[N_PEER * i + k],
                                         recv_sem=recv.at[N_PEER * i + k], device_id=peer, device_id_type=MESH_ID)
            for i in range(len(srcs)) for k, (peer, _) in enumerate(_chip_peers(x, y, c))]


def _scatter_plan(srcs, lands, send, recv):
    x, y, c = _place()
    return [pltpu.make_async_remote_copy(src_ref=srcs[i].at[idx], dst_ref=lands[i].at[k], send_sem=send.at[N_PEER * i + k],
                                         recv_sem=recv.at[N_PEER * i + k], device_id=peer, device_id_type=MESH_ID)
            for i in range(len(srcs)) for k, (peer, idx) in enumerate(_chip_peers(x, y, c))]


def copies_start(plan, srcs, lands, after, *, name):
    n = len(srcs)
    both = list(srcs) + list(lands)

    def body(*refs):
        src_refs, land_refs = refs[:n], refs[n:2 * n]
        send, recv = refs[2 * n + 1], refs[2 * n + 2]
        for cp in plan(src_refs, land_refs, send, recv):
            cp.start()
        refs[-1][...] = jnp.zeros_like(refs[-1])

    res = pl.pallas_call(
        body, name=name,
        out_shape=(pltpu.SemaphoreType.DMA((n * N_PEER,)), pltpu.SemaphoreType.DMA((n * N_PEER,)),
                   *[pltpu.HBM(a.shape, a.dtype) for a in both], jax.ShapeDtypeStruct((8, 128), F32)),
        in_specs=[HBM] * (2 * n) + [ANY],
        out_specs=(SEM, SEM, *[HBM] * (2 * n), pl.BlockSpec(memory_space=pltpu.VMEM)),
        input_output_aliases={i: 2 + i for i in range(2 * n)},
        compiler_params=pltpu.CompilerParams(has_side_effects=EFFECT))(
            *[pltpu.with_memory_space_constraint(a, pltpu.HBM) for a in both], after)
    return {"send": res[0], "recv": res[1], "srcs": list(res[2:2 + n]), "lands": list(res[2 + n:2 + 2 * n]),
            "token": res[-1]}


def copies_wait(plan, started, after, *, name):
    srcs, lands = started["srcs"], started["lands"]
    n = len(srcs)
    both = srcs + lands

    def body(*refs):
        src_refs, land_refs = refs[:n], refs[n:2 * n]
        send, recv = refs[2 * n], refs[2 * n + 1]
        for cp in plan(src_refs, land_refs, send, recv):
            cp.wait_send()
            cp.wait_recv()

    res = pl.pallas_call(
        body, name=name, out_shape=tuple(pltpu.HBM(a.shape, a.dtype) for a in both),
        in_specs=[HBM] * (2 * n) + [SEM, SEM, ANY], out_specs=(HBM,) * (2 * n),
        input_output_aliases={i: i for i in range(2 * n)},
        compiler_params=pltpu.CompilerParams(has_side_effects=EFFECT))(*both, started["send"], started["recv"], after)
    return list(res[:n]), list(res[n:])


def allgather_small(small, *, name):
    def body(small_ref, out_ref, send, recv, loc):
        x, y, c = _place()
        dev = 4 * x + 2 * y + c
        local = pltpu.make_async_copy(small_ref, out_ref.at[dev], loc)
        remote = []
        for r in range(1, N_DEV):
            fx, fy, fc = (r >> 2) & 1, (r >> 1) & 1, r & 1
            peer = (1 - x if fx else x, 1 - y if fy else y, 1 - c if fc else c)
            remote.append(pltpu.make_async_remote_copy(
                src_ref=small_ref, dst_ref=out_ref.at[dev], send_sem=send.at[r - 1], recv_sem=recv.at[r - 1],
                device_id=peer, device_id_type=MESH_ID))
        local.start()
        for cp in remote:
            cp.start()
        for cp in remote:
            cp.wait()
        local.wait()

    return pl.pallas_call(
        body, name=name, in_specs=[ANY], out_specs=ANY,
        out_shape=jax.ShapeDtypeStruct((N_DEV,) + small.shape, small.dtype),
        scratch_shapes=[pltpu.SemaphoreType.DMA((N_DEV - 1,)), pltpu.SemaphoreType.DMA((N_DEV - 1,)),
                        pltpu.SemaphoreType.DMA(())])(small)


def swap_cores(arrs, *, name):
    n = len(arrs)

    def body(*refs):
        ins, outs = refs[:n], refs[n:2 * n]
        send, recv = refs[2 * n:]
        x, y, c = _place()
        copies = [pltpu.make_async_remote_copy(src_ref=ins[i], dst_ref=outs[i], send_sem=send.at[i], recv_sem=recv.at[i],
                                               device_id=(x, y, 1 - c), device_id_type=MESH_ID) for i in range(n)]
        for cp in copies:
            cp.start()
        for cp in copies:
            cp.wait()

    return pl.pallas_call(
        body, name=name, in_specs=[ANY] * n, out_specs=[ANY] * n,
        out_shape=[jax.ShapeDtypeStruct(a.shape, a.dtype) for a in arrs],
        scratch_shapes=[pltpu.SemaphoreType.DMA((n,)), pltpu.SemaphoreType.DMA((n,))])(*arrs)


RED_ROWS = 128


def sum_chips(own, got, *, name):
    r, c = own.shape
    rb = RED_ROWS if r % RED_ROWS == 0 else r

    def body(o_ref, a_ref, b_ref, c_ref, out_ref):
        out_ref[...] = ((o_ref[...].astype(F32) + a_ref[...].astype(F32)) + b_ref[...].astype(F32)) + c_ref[...].astype(F32)

    gk = lambda k: pl.BlockSpec((None, rb, c), lambda i: (k, i, 0))
    row = pl.BlockSpec((rb, c), lambda i: (i, 0))
    return pl.pallas_call(
        body, name=name, grid=(r // rb,), in_specs=[row, gk(0), gk(1), gk(2)], out_specs=row,
        out_shape=jax.ShapeDtypeStruct((r, c), F32), compiler_params=_params(("parallel",)))(own, got, got, got)


def sum_devices(small_all, *, name):
    _, p, c = small_all.shape

    def body(a_ref, out_ref):
        acc = a_ref[0]
        for d in range(1, N_DEV):
            acc = acc + a_ref[d]
        out_ref[...] = acc

    return pl.pallas_call(
        body, name=name, grid=(1,), in_specs=[pl.BlockSpec((N_DEV, p, c), lambda i: (0, 0, 0))],
        out_specs=pl.BlockSpec((p, c), lambda i: (0, 0)), out_shape=jax.ShapeDtypeStruct((p, c), F32),
        compiler_params=_params(("arbitrary",)))(small_all)


def adamw(parts, w, m, v, *, name):
    nl, r, c = w.shape
    assert len(parts) == nl
    npart = len(parts[0])
    rb = RED_ROWS if r % RED_ROWS == 0 else r
    flat = [a for layer in parts for a in layer]

    def body(*refs):
        p_refs, (w_ref, m_ref, v_ref) = refs[:nl * npart], refs[nl * npart:nl * npart + 3]
        g_ref, d_ref, nm_ref, nv_ref = refs[nl * npart + 3:]
        layer = pl.program_id(0)
        grad = None
        for l in range(nl):
            gl = p_refs[l * npart][...]
            for j in range(1, npart):
                gl = gl + p_refs[l * npart + j][...]
            grad = gl if grad is None else jnp.where(layer == l, gl, grad)
        wv, mv, vv = w_ref[...], m_ref[...], v_ref[...]
        nm = ADAM_B1 * mv + (1.0 - ADAM_B1) * grad
        nv = ADAM_B2 * vv + (1.0 - ADAM_B2) * (grad * grad)
        m_hat = nm / (1.0 - ADAM_B1 ** ADAM_STEP)
        v_hat = nv / (1.0 - ADAM_B2 ** ADAM_STEP)
        g_ref[...] = grad
        d_ref[...] = -ADAM_LR * (m_hat / (jnp.sqrt(v_hat) + ADAM_EPS) + ADAM_WD * wv)
        nm_ref[...] = nm
        nv_ref[...] = nv

    pspec = pl.BlockSpec((rb, c), lambda l, i: (i, 0))
    wspec = pl.BlockSpec((None, rb, c), lambda l, i: (l, i, 0))
    osh = jax.ShapeDtypeStruct((nl, r, c), F32)
    return pl.pallas_call(
        body, name=name, grid=(nl, r // rb), in_specs=[pspec] * (nl * npart) + [wspec] * 3,
        out_specs=[wspec] * 4, out_shape=[osh] * 4, compiler_params=_params(("parallel", "parallel")))(*flat, w, m, v)


def _rows128(a):
    flat = a.reshape(-1)
    pad = (-flat.shape[0]) % 128
    return jnp.pad(flat, (0, pad)).reshape(-1, 128)


def _pack_rows(arrs, multiple=8):
    rows = jnp.concatenate([_rows128(a.astype(F32)) for a in arrs], axis=0)
    return jnp.pad(rows, ((0, (-rows.shape[0]) % multiple), (0, 0)))


def _unpack_rows(rows, shapes):
    out, r0 = [], 0
    for shp in shapes:
        size = 1
        for s in shp:
            size *= s
        nr = -(-size // 128)
        out.append(rows[r0:r0 + nr].reshape(-1)[:size].reshape(shp))
        r0 += nr
    return out


SMALL_LOCAL_GRADS = ["even_norm", "even_conv", "a_log", "dt_bias", "sinks", "onorm", "odd_norm", "odd_ln_g",
                     "odd_ln_b", "odd_w_s", "odd_b_s", "ffn_norm", "final_norm"]
BIG = ["even_w_in", "even_w_out", "odd_w_in", "odd_w_out", "ffn_w_gate", "ffn_w_up", "ffn_w_down"]
WEIGHTS = ["even_norm", "even_w_in", "even_conv", "even_a_log", "even_dt_bias", "even_sinks", "even_onorm",
           "even_w_out", "odd_norm", "odd_w_in", "odd_ln_g", "odd_ln_b", "odd_w_s", "odd_b_s", "odd_w_out",
           "ffn_norm", "ffn_w_gate", "ffn_w_up", "ffn_w_down", "final_norm"]
SMALL = [n for n in WEIGHTS if n not in BIG]


def kernel(x, even_norm, even_w_in, even_conv, even_a_log, even_dt_bias, even_sinks, even_onorm, even_w_out, odd_norm, odd_w_in, odd_ln_g, odd_ln_b, odd_w_s, odd_b_s, odd_w_out, ffn_norm, ffn_w_gate, ffn_w_up, ffn_w_down, final_norm, loss_target, m_even_norm, m_even_w_in, m_even_conv, m_even_a_log, m_even_dt_bias, m_even_sinks, m_even_onorm, m_even_w_out, m_odd_norm, m_odd_w_in, m_odd_ln_g, m_odd_ln_b, m_odd_w_s, m_odd_b_s, m_odd_w_out, m_ffn_norm, m_ffn_w_gate, m_ffn_w_up, m_ffn_w_down, m_final_norm, v_even_norm, v_even_w_in, v_even_conv, v_even_a_log, v_even_dt_bias, v_even_sinks, v_even_onorm, v_even_w_out, v_odd_norm, v_odd_w_in, v_odd_ln_g, v_odd_ln_b, v_odd_w_s, v_odd_b_s, v_odd_w_out, v_ffn_norm, v_ffn_w_gate, v_ffn_w_up, v_ffn_w_down, v_final_norm):
    args = dict(locals())
    wl = {n: args[n] for n in WEIGHTS}
    ml = {n: args["m_" + n] for n in WEIGHTS}
    vl = {n: args["v_" + n] for n in WEIGHTS}
    me = 2 * lax.axis_index("x") + lax.axis_index("y")

    def landing(a):
        return lax.dynamic_update_index_in_dim(lax.empty((N_SHARD,) + a.shape, a.dtype), a, me, 0)

    b16 = lambda *arrs: [a.astype(BF16) for a in arrs]
    gather_groups = {
        "even_in": b16(even_w_in[0]) + [_pack_rows([even_conv[0], odd_norm, odd_ln_g, odd_ln_b])],
        "even_out": b16(even_w_out[0]),
        "ffn0": b16(ffn_w_gate[0], ffn_w_up[0], ffn_w_down[0]),
        "odd": b16(odd_w_in[0], odd_w_out[0]),
        "ffn1": b16(ffn_w_gate[1], ffn_w_up[1], ffn_w_down[1]),
    }
    gathering, after = {}, even_norm
    for group, srcs in gather_groups.items():
        gathering[group] = copies_start(_gather_plan, srcs, [landing(a) for a in srcs], after,
                                        name=f"gather_{group}_start")
        after = gathering[group]["token"]

    def get(group, behind):
        _, lands = copies_wait(_gather_plan, gathering[group], behind, name=f"gather_{group}_wait")
        if group == "even_in":
            parts = zip(*[_unpack_rows(lands[1][s], [(CONV_K, 768), (1, 512), (1, 512), (1, 512)])
                          for s in range(N_SHARD)])
            conv, onorm, lng, lnb = [jnp.concatenate(p, axis=1) for p in parts]
            w_in = jnp.pad(jnp.transpose(lands[0], (1, 0, 2)).reshape(D_MODEL, EVEN_IN),
                           ((0, 0), (0, EVEN_IN_PAD - EVEN_IN)))
            return {"even_w_in": w_in, "even_conv": conv, "odd_norm": onorm, "odd_ln_g": lng, "odd_ln_b": lnb}
        if group == "even_out":
            return {"even_w_out": lands[0].reshape(D_MODEL, D_MODEL)}
        if group == "odd":
            return {"odd_w_in": lands[0], "odd_w_out": lands[1].reshape(D_MODEL, D_MODEL)}
        return {"gate": lands[0], "up": lands[1], "down": lands[2].reshape(D_FF, D_MODEL)}

    rows4 =lambda a: a.reshape(N_SHARD, a.shape[0] // N_SHARD, a.shape[1])
    scattering, small = {}, {}

    def emit(group, grads):
        behind = even_norm
        if group == "even_in":
            small["local"] = grads["small"]
            small["all"] = behind = allgather_small(_pack_rows([grads["small"][n] for n in SMALL_LOCAL_GRADS]),
                                                    name="allgather_small")
            srcs = [jnp.transpose(grads["even_w_in"][:, :EVEN_IN].reshape(D_MODEL, N_SHARD, EVEN_IN // N_SHARD),
                                  (1, 0, 2))]
        elif group == "even_out":
            srcs = [rows4(grads["even_w_out"])]
        elif group == "odd":
            srcs = [grads["odd_w_in"], rows4(grads["odd_w_out"])]
        else:
            srcs = [grads["gate"], grads["up"], rows4(grads["down"])]
        lands = [lax.empty((N_PEER,) + a.shape[1:], a.dtype) for a in srcs]
        scattering[group] = copies_start(_scatter_plan, srcs, lands, behind, name=f"scatter_{group}_start")
        return scattering[group]["token"][0:1, 0:1]

    pad816 = lambda a: jnp.pad(a, ((0, 0), (B_HEADS, 128 - 2 * B_HEADS)))
    w = {
        "even_norm": even_norm + after[0:1, 0:1],
        "a_log": pad816(even_a_log), "dt_bias": pad816(even_dt_bias),
        "sinks": jnp.pad(even_sinks, ((0, 0), (0, 128 - A_HEADS))),
        "onorm": even_onorm,
        "odd_w_s": odd_w_s[0],
        "odd_b_s": jnp.pad(odd_b_s[0].T, ((0, 0), (0, 128 - C_GROUPS))),
        "ffn_norm": ffn_norm,
        "final_norm": final_norm[None],
    }
    loss_l, grad_x = _local_step(x[0], loss_target[0], w, get, emit)
    loss = lax.psum(loss_l[0, 0], ("x", "y", "c"))

    def finish(group, behind):
        srcs, lands = copies_wait(_scatter_plan, scattering[group], behind, name=f"scatter_{group}_wait")
        partial = [sum_chips(lax.dynamic_index_in_dim(srcs[i], me, 0, keepdims=False), lands[i],
                             name=f"sum_chips_{group}_{i}") for i in range(len(srcs))]
        other = swap_cores(partial, name=f"swap_cores_{group}")
        return list(zip(partial, other))

    sums = {group: finish(group, grad_x) for group in ("ffn1", "odd", "ffn0", "even_out")}
    outs = {}
    parts_of = {"even_w_out": [sums["even_out"][0]], "odd_w_in": [sums["odd"][0]], "odd_w_out": [sums["odd"][1]],
                "ffn_w_gate": [sums["ffn0"][0], sums["ffn1"][0]], "ffn_w_up": [sums["ffn0"][1], sums["ffn1"][1]],
                "ffn_w_down": [sums["ffn0"][2], sums["ffn1"][2]]}
    for n in parts_of:
        outs[n] = adamw(parts_of[n], wl[n], ml[n], vl[n], name=f"adamw_{n}")
    outs["even_w_in"] = adamw([finish("even_in", outs["ffn_w_down"][1])[0]], wl["even_w_in"], ml["even_w_in"],
                              vl["even_w_in"], name="adamw_even_w_in")

    g = small["local"]
    small_sum = sum_devices(small["all"], name="sum_devices")
    sg = dict(zip(SMALL_LOCAL_GRADS, _unpack_rows(small_sum, [g[n].shape for n in SMALL_LOCAL_GRADS])))
    own_cols = lambda a, width: lax.dynamic_slice_in_dim(a, me * width, width, axis=a.ndim - 1)
    small_grads = {
        "even_norm": sg["even_norm"], "even_conv": own_cols(sg["even_conv"], 768)[None],
        "even_a_log": sg["a_log"][:, B_HEADS:2 * B_HEADS], "even_dt_bias": sg["dt_bias"][:, B_HEADS:2 * B_HEADS],
        "even_sinks": sg["sinks"][:, :A_HEADS], "even_onorm": sg["onorm"],
        "odd_norm": own_cols(sg["odd_norm"], 512), "odd_ln_g": own_cols(sg["odd_ln_g"], 512),
        "odd_ln_b": own_cols(sg["odd_ln_b"], 512), "odd_w_s": sg["odd_w_s"][None],
        "odd_b_s": sg["odd_b_s"][:, :C_GROUPS].T[None], "ffn_norm": sg["ffn_norm"], "final_norm": sg["final_norm"][0],
    }
    packed = [_pack_rows([d[n] for n in SMALL])[None] for d in (small_grads, wl, ml, vl)]
    small_out = adamw([(packed[0][0],)], packed[1], packed[2], packed[3], name="adamw_small")
    shapes = [wl[n].shape for n in SMALL]
    for j in range(4):
        for n, a in zip(SMALL, _unpack_rows(small_out[j][0], shapes)):
            outs.setdefault(n, [None] * 4)[j] = a

    return (loss, grad_x[None], *[outs[n][0] for n in WEIGHTS], *[outs[n][1] for n in WEIGHTS],
            *[outs[n][2] for n in WEIGHTS], *[outs[n][3] for n in WEIGHTS])
```

```python
import functools

import jax
import jax.numpy as jnp
from jax import lax
from jax.experimental import pallas as pl
from jax.experimental.pallas import tpu as pltpu

F32 = jnp.float32
BF16 = jnp.bfloat16
NEG_INF = float("-inf")

D_MODEL = 2048
A_HEADS, A_KV_HEADS, A_HEAD_DIM, WINDOW = 16, 2, 64, 128
B_HEADS, B_HEAD_DIM, CONV_K, DN_CHUNK = 8, 128, 4, 64
C_GROUPS, C_CHUNK = 8, 128
C_GROUP_DIM = D_MODEL // C_GROUPS
D_FF = 5632
EPS = 1e-6
A_Q = A_HEADS * A_HEAD_DIM
A_KV = A_KV_HEADS * A_HEAD_DIM
B_W = B_HEADS * B_HEAD_DIM
EVEN_IN = A_Q + 2 * A_KV + 4 * B_W + 2 * B_HEADS
EVEN_IN_PAD = 5632
COL_KV = A_Q
COL_QKVB = A_Q + 2 * A_KV
COL_Z = COL_QKVB + 3 * B_W
COL_GATE = COL_Z + B_W
N_SHARD = 4

ADAM_LR, ADAM_B1, ADAM_B2, ADAM_EPS, ADAM_WD, ADAM_STEP = 0.001, 0.9, 0.999, 1e-08, 0.01, 10

VMEM_LIMIT_V7X = 56 * 1024 * 1024
MESH_ID = pl.DeviceIdType.MESH


def _params(sem=None):
    return pltpu.CompilerParams(dimension_semantics=sem, vmem_limit_bytes=VMEM_LIMIT_V7X)


def _sigmoid(x):
    return 1.0 / (1.0 + jnp.exp(-x))


def _silu(x):
    return x * _sigmoid(x)


def _dsilu(x):
    s = _sigmoid(x)
    return s * (1.0 + x * (1.0 - s))


def _gelu(x):
    return 0.5 * x * (1.0 + lax.erf(x * 0.7071067811865476))


def _dgelu(x):
    return 0.5 * (1.0 + lax.erf(x * 0.7071067811865476)) + x * jnp.exp(-0.5 * x * x) * 0.3989422804014327


def _dot(a, b, dims):
    if a.ndim == 3:
        (ca,), (cb,) = dims
        return lax.dot_general(a, b, (((ca + 1,), (cb + 1,)), ((0,), (0,))), preferred_element_type=F32)
    return lax.dot_general(a, b, (dims, ((), ())), preferred_element_type=F32)


NN = ((1,), (0,))
NT = ((1,), (1,))
TN = ((0,), (0,))


def _as3(b):
    return b if b.ndim == 3 else b[None]


def _accumulate(step, nsteps, accs, products, finish):
    if nsteps == 1:
        finish(products())
        return

    @pl.when(step == 0)
    def _():
        for acc, p in zip(accs, products()):
            acc[...] = p

    if nsteps > 2:
        @pl.when((step > 0) & (step < nsteps - 1))
        def _():
            for acc, p in zip(accs, products()):
                acc[...] += p

    @pl.when(step == nsteps - 1)
    def _():
        finish(tuple(acc[...] + p for acc, p in zip(accs, products())))


def mm_nn(a, b, *, tm, tn, tk, out_dtype, name, res=None, act=None):
    b3 = _as3(b)
    m, k = a.shape
    s, k2, ns = b3.shape
    assert k2 == k and m % tm == 0 and ns % tn == 0 and k % tk == 0, (a.shape, b3.shape, tm, tn, tk)
    nps, nk = ns // tn, k // tk

    def body(*refs):
        if res is None:
            a_ref, b_ref, o_ref, acc = refs
        else:
            a_ref, b_ref, r_ref, o_ref, acc = refs
        def finish(tiles):
            r = tiles[0] if res is None else tiles[0] + r_ref[...].astype(F32)
            o_ref[...] = r.astype(out_dtype)

        _accumulate(pl.program_id(2), nk, (acc,),
                    lambda: (_dot(a_ref[...].astype(BF16), b_ref[...].astype(BF16), NN),), finish)

    in_specs = [pl.BlockSpec((tm, tk), lambda i, j, kk: (i, kk)),
                pl.BlockSpec((None, tk, tn), lambda i, j, kk: (j // nps, kk, j % nps))]
    args = [a, b3]
    if res is not None:
        in_specs.append(pl.BlockSpec((tm, tn), lambda i, j, kk: (i, j)))
        args.append(res)
    return pl.pallas_call(
        body, name=name, grid=(m // tm, s * nps, nk), in_specs=in_specs,
        out_specs=pl.BlockSpec((tm, tn), lambda i, j, kk: (i, j)),
        out_shape=jax.ShapeDtypeStruct((m, s * ns), out_dtype),
        scratch_shapes=[pltpu.VMEM((tm, tn), F32)],
        compiler_params=_params(("parallel", "parallel", "arbitrary")))(*args)


def mm_nt(a, b, *, tm, tn, tk, out_dtype, name, res=None):
    b3 = _as3(b)
    m, n = a.shape
    s, k, ns = b3.shape
    assert n == s * ns and m % tm == 0 and k % tn == 0 and ns % tk == 0, (a.shape, b3.shape, tm, tn, tk)
    rps = ns // tk
    nr = s * rps

    def body(*refs):
        if res is None:
            a_ref, b_ref, o_ref, acc = refs
        else:
            a_ref, b_ref, r_ref, o_ref, acc = refs
        def finish(tiles):
            r = tiles[0] if res is None else tiles[0] + r_ref[...].astype(F32)
            o_ref[...] = r.astype(out_dtype)

        _accumulate(pl.program_id(2), nr, (acc,),
                    lambda: (_dot(a_ref[...].astype(BF16), b_ref[...].astype(BF16), NT),), finish)

    in_specs = [pl.BlockSpec((tm, tk), lambda i, j, r: (i, r)),
                pl.BlockSpec((None, tn, tk), lambda i, j, r: (r // rps, j, r % rps))]
    args = [a, b3]
    if res is not None:
        in_specs.append(pl.BlockSpec((tm, tn), lambda i, j, r: (i, j)))
        args.append(res)
    return pl.pallas_call(
        body, name=name, grid=(m // tm, k // tn, nr), in_specs=in_specs,
        out_specs=pl.BlockSpec((tm, tn), lambda i, j, r: (i, j)),
        out_shape=jax.ShapeDtypeStruct((m, k), out_dtype),
        scratch_shapes=[pltpu.VMEM((tm, tn), F32)],
        compiler_params=_params(("parallel", "parallel", "arbitrary")))(*args)


def mm_tn(a, b, *, shards, tm, tn, tk, out_dtype, name):
    m, k = a.shape
    m2, n = b.shape
    ns = n // shards
    assert m2 == m and n == shards * ns and m % tm == 0 and k % tk == 0 and ns % tn == 0, (a.shape, b.shape)
    nps, nm = ns // tn, m // tm

    def body(a_ref, b_ref, o_ref, acc):
        def finish(tiles):
            o_ref[...] = tiles[0].astype(out_dtype)

        _accumulate(pl.program_id(2), nm, (acc,),
                    lambda: (_dot(a_ref[...].astype(BF16), b_ref[...].astype(BF16), TN),), finish)

    return pl.pallas_call(
        body, name=name, grid=(k // tk, shards * nps, nm),
        in_specs=[pl.BlockSpec((tm, tk), lambda i, j, mi: (mi, i)),
                  pl.BlockSpec((tm, tn), lambda i, j, mi: (mi, j))],
        out_specs=pl.BlockSpec((None, tk, tn), lambda i, j, mi: (j // nps, i, j % nps)),
        out_shape=jax.ShapeDtypeStruct((shards, k, ns), out_dtype),
        scratch_shapes=[pltpu.VMEM((tk, tn), F32)],
        compiler_params=_params(("parallel", "parallel", "arbitrary")))(a, b)


def mm_gate_up(hn, wg, wu, *, tm, tn, tk, name):
    wg3, wu3 = _as3(wg), _as3(wu)
    m, k = hn.shape
    s, _, ns = wg3.shape
    assert m % tm == 0 and ns % tn == 0 and k % tk == 0
    nps, nk = ns // tn, k // tk

    def body(a_ref, g_ref, u_ref, og_ref, ou_ref, oa_ref, accg, accu):
        def products():
            a = a_ref[...].astype(BF16)
            return _dot(a, g_ref[...].astype(BF16), NN), _dot(a, u_ref[...].astype(BF16), NN)

        def finish(tiles):
            g, u = tiles
            og_ref[...] = g.astype(BF16)
            ou_ref[...] = u.astype(BF16)
            oa_ref[...] = (_silu(g) * u).astype(BF16)

        _accumulate(pl.program_id(2), nk, (accg, accu), products, finish)

    wspec = pl.BlockSpec((None, tk, tn), lambda i, j, kk: (j // nps, kk, j % nps))
    ospec = pl.BlockSpec((tm, tn), lambda i, j, kk: (i, j))
    osh = jax.ShapeDtypeStruct((m, s * ns), BF16)
    return pl.pallas_call(
        body, name=name, grid=(m // tm, s * nps, nk),
        in_specs=[pl.BlockSpec((tm, tk), lambda i, j, kk: (i, kk)), wspec, wspec],
        out_specs=[ospec, ospec, ospec], out_shape=[osh, osh, osh],
        scratch_shapes=[pltpu.VMEM((tm, tn) if nk > 1 else (8, 128), F32)] * 2,
        compiler_params=_params(("parallel", "parallel", "arbitrary")))(hn, wg3, wu3)


def mm_down_bwd(dh, wd, gate, up, *, tm, tn, tk, name):
    m, d = dh.shape
    f, d2 = wd.shape
    assert d2 == d and m % tm == 0 and f % tn == 0 and d % tk == 0
    nr = d // tk

    def body(a_ref, b_ref, g_ref, u_ref, og_ref, ou_ref, acc):
        def finish(tiles):
            da = tiles[0]
            g, u = g_ref[...].astype(F32), u_ref[...].astype(F32)
            og_ref[...] = (da * u * _dsilu(g)).astype(BF16)
            ou_ref[...] = (da * _silu(g)).astype(BF16)

        _accumulate(pl.program_id(2), nr, (acc,),
                    lambda: (_dot(a_ref[...].astype(BF16), b_ref[...].astype(BF16), NT),), finish)

    ospec = pl.BlockSpec((tm, tn), lambda i, j, r: (i, j))
    osh = jax.ShapeDtypeStruct((m, f), BF16)
    return pl.pallas_call(
        body, name=name, grid=(m // tm, f // tn, nr),
        in_specs=[pl.BlockSpec((tm, tk), lambda i, j, r: (i, r)),
                  pl.BlockSpec((tn, tk), lambda i, j, r: (j, r)), ospec, ospec],
        out_specs=[ospec, ospec], out_shape=[osh, osh],
        scratch_shapes=[pltpu.VMEM((tm, tn), F32)],
        compiler_params=_params(("parallel", "parallel", "arbitrary")))(dh, wd, gate, up)


ROWS = 256


def rms_fwd(x, g, *, name):
    t, d = x.shape

    def body(x_ref, g_ref, o_ref):
        xv = x_ref[...]
        r = lax.rsqrt(jnp.mean(xv * xv, axis=-1, keepdims=True) + EPS)
        o_ref[...] = (xv * r * g_ref[...]).astype(BF16)

    return pl.pallas_call(
        body, name=name, grid=(t // ROWS,),
        in_specs=[pl.BlockSpec((ROWS, d), lambda i: (i, 0)), pl.BlockSpec((1, d), lambda i: (0, 0))],
        out_specs=pl.BlockSpec((ROWS, d), lambda i: (i, 0)),
        out_shape=jax.ShapeDtypeStruct((t, d), BF16), compiler_params=_params(("parallel",)))(x, g)


def rms_bwd(x, g, dy, dres, *, name):
    t, d = x.shape

    def body(x_ref, g_ref, dy_ref, dr_ref, dx_ref, dg_ref):
        @pl.when(pl.program_id(0) == 0)
        def _():
            dg_ref[...] = jnp.zeros_like(dg_ref)

        xv, dyv = x_ref[...], dy_ref[...].astype(F32)
        r = lax.rsqrt(jnp.mean(xv * xv, axis=-1, keepdims=True) + EPS)
        dyg = dyv * g_ref[...]
        dx = r * dyg - xv * (r * r * r) * jnp.mean(dyg * xv, axis=-1, keepdims=True)
        dx_ref[...] = dx + dr_ref[...]
        dg_ref[...] += jnp.sum(dyv * xv * r, axis=0, keepdims=True)

    row = pl.BlockSpec((ROWS, d), lambda i: (i, 0))
    vec = pl.BlockSpec((1, d), lambda i: (0, 0))
    return pl.pallas_call(
        body, name=name, grid=(t // ROWS,), in_specs=[row, vec, row, row], out_specs=[row, vec],
        out_shape=[jax.ShapeDtypeStruct((t, d), F32), jax.ShapeDtypeStruct((1, d), F32)],
        compiler_params=_params(("arbitrary",)))(x, g, dy, dres)


def loss_head(h, g, target, *, name):
    t, d = h.shape

    def body(x_ref, g_ref, t_ref, loss_ref, dx_ref, dg_ref):
        @pl.when(pl.program_id(0) == 0)
        def _():
            dg_ref[...] = jnp.zeros_like(dg_ref)
            loss_ref[...] = jnp.zeros_like(loss_ref)

        xv, gv = x_ref[...], g_ref[...]
        r = lax.rsqrt(jnp.mean(xv * xv, axis=-1, keepdims=True) + EPS)
        e = xv * r * gv - t_ref[...]
        loss_ref[...] += 0.5 * jnp.sum(jnp.mean(e * e, axis=-1, keepdims=True), axis=0, keepdims=True)
        dyv = e * (1.0 / d)
        dyg = dyv * gv
        dx_ref[...] = r * dyg - xv * (r * r * r) * jnp.mean(dyg * xv, axis=-1, keepdims=True)
        dg_ref[...] += jnp.sum(dyv * xv * r, axis=0, keepdims=True)

    row = pl.BlockSpec((ROWS, d), lambda i: (i, 0))
    vec = pl.BlockSpec((1, d), lambda i: (0, 0))
    return pl.pallas_call(
        body, name=name, grid=(t // ROWS,), in_specs=[row, vec, row],
        out_specs=[pl.BlockSpec((1, 128), lambda i: (0, 0)), row, vec],
        out_shape=[jax.ShapeDtypeStruct((1, 128), F32), jax.ShapeDtypeStruct((t, d), F32),
                   jax.ShapeDtypeStruct((1, d), F32)],
        compiler_params=_params(("arbitrary",)))(h, g, target)


def _tril_mask():
    r = lax.broadcasted_iota(jnp.int32, (C_CHUNK, C_CHUNK), 0)
    c = lax.broadcasted_iota(jnp.int32, (C_CHUNK, C_CHUNK), 1)
    return r >= c


def _layer_norm_parts(v):
    mu = jnp.mean(v, axis=-1, keepdims=True)
    vc = v - mu
    rstd = lax.rsqrt(jnp.mean(vc * vc, axis=-1, keepdims=True) + EPS)
    return vc * rstd, rstd


def gmlp_fwd(zpre, ln_g, ln_b, ws, bs_t, *, name):
    t = zpre.shape[0]
    d = D_MODEL

    def body(zu_ref, zv_ref, g_ref, b_ref, ws_ref, bs_ref, o_ref):
        u = _gelu(zu_ref[...])
        vhat, _ = _layer_norm_parts(_gelu(zv_ref[...]))
        vln = (vhat * g_ref[...] + b_ref[...]).astype(BF16)
        mask = _tril_mask()
        for gi in range(C_GROUPS):
            sl = slice(gi * C_GROUP_DIM, (gi + 1) * C_GROUP_DIM)
            w = jnp.where(mask, ws_ref[gi], 0.0).astype(BF16)
            mixed = _dot(w, vln[:, sl], NN) + bs_ref[:, gi:gi + 1]
            o_ref[:, sl] = (u[:, sl] * mixed).astype(BF16)

    vec = pl.BlockSpec((1, d), lambda i: (0, 0))
    return pl.pallas_call(
        body, name=name, grid=(t // C_CHUNK,),
        in_specs=[pl.BlockSpec((C_CHUNK, d), lambda i: (i, 0)), pl.BlockSpec((C_CHUNK, d), lambda i: (i, 1)),
                  vec, vec, pl.BlockSpec((C_GROUPS, C_CHUNK, C_CHUNK), lambda i: (0, 0, 0)),
                  pl.BlockSpec((C_CHUNK, 128), lambda i: (0, 0))],
        out_specs=pl.BlockSpec((C_CHUNK, d), lambda i: (i, 0)),
        out_shape=jax.ShapeDtypeStruct((t, d), BF16), compiler_params=_params(("parallel",)))(
            zpre, zpre, ln_g, ln_b, ws, bs_t)


def gmlp_bwd(zpre, dgated, ln_g, ln_b, ws, bs_t, *, name):
    t = zpre.shape[0]
    d = D_MODEL

    def body(zu_ref, zv_ref, dg_ref, g_ref, b_ref, ws_ref, bs_ref, dz_ref, dws_ref, dbs_ref, dlg_ref, dlb_ref):
        @pl.when(pl.program_id(0) == 0)
        def _():
            dws_ref[...] = jnp.zeros_like(dws_ref)
            dbs_ref[...] = jnp.zeros_like(dbs_ref)
            dlg_ref[...] = jnp.zeros_like(dlg_ref)
            dlb_ref[...] = jnp.zeros_like(dlb_ref)

        zu, zv = zu_ref[...], zv_ref[...]
        u = _gelu(zu)
        vhat, rstd = _layer_norm_parts(_gelu(zv))
        gam = g_ref[...]
        vln = (vhat * gam + b_ref[...]).astype(BF16)
        dgt = dg_ref[...].astype(F32)
        mask = _tril_mask()
        lane = lax.broadcasted_iota(jnp.int32, (C_CHUNK, 128), 1)
        dbs = jnp.zeros((C_CHUNK, 128), F32)
        du_parts, dvln_parts = [], []
        for gi in range(C_GROUPS):
            sl = slice(gi * C_GROUP_DIM, (gi + 1) * C_GROUP_DIM)
            w = jnp.where(mask, ws_ref[gi], 0.0).astype(BF16)
            mixed = _dot(w, vln[:, sl], NN) + bs_ref[:, gi:gi + 1]
            du_parts.append(dgt[:, sl] * mixed)
            dmixed = dgt[:, sl] * u[:, sl]
            dmb = dmixed.astype(BF16)
            dws_ref[gi] += jnp.where(mask, _dot(dmb, vln[:, sl], NT), 0.0)
            dbs = dbs + jnp.where(lane == gi, jnp.sum(dmixed, axis=-1, keepdims=True), 0.0)
            dvln_parts.append(_dot(w, dmb, TN))
        dbs_ref[...] += dbs
        du = jnp.concatenate(du_parts, axis=-1)
        dvln = jnp.concatenate(dvln_parts, axis=-1)
        dlg_ref[...] += jnp.sum(dvln * vhat, axis=0, keepdims=True)
        dlb_ref[...] += jnp.sum(dvln, axis=0, keepdims=True)
        dvhat = dvln * gam
        dv = rstd * (dvhat - jnp.mean(dvhat, axis=-1, keepdims=True)
                     - vhat * jnp.mean(dvhat * vhat, axis=-1, keepdims=True))
        dz_ref[:, :d] = (du * _dgelu(zu)).astype(BF16)
        dz_ref[:, d:] = (dv * _dgelu(zv)).astype(BF16)

    vec = pl.BlockSpec((1, d), lambda i: (0, 0))
    wsp = pl.BlockSpec((C_GROUPS, C_CHUNK, C_CHUNK), lambda i: (0, 0, 0))
    bsp = pl.BlockSpec((C_CHUNK, 128), lambda i: (0, 0))
    return pl.pallas_call(
        body, name=name, grid=(t // C_CHUNK,),
        in_specs=[pl.BlockSpec((C_CHUNK, d), lambda i: (i, 0)), pl.BlockSpec((C_CHUNK, d), lambda i: (i, 1)),
                  pl.BlockSpec((C_CHUNK, d), lambda i: (i, 0)), vec, vec, wsp, bsp],
        out_specs=[pl.BlockSpec((C_CHUNK, 2 * d), lambda i: (i, 0)), wsp, bsp, vec, vec],
        out_shape=[jax.ShapeDtypeStruct((t, 2 * d), BF16), jax.ShapeDtypeStruct((C_GROUPS, C_CHUNK, C_CHUNK), F32),
                   jax.ShapeDtypeStruct((C_CHUNK, 128), F32), jax.ShapeDtypeStruct((1, d), F32),
                   jax.ShapeDtypeStruct((1, d), F32)],
        compiler_params=_params(("arbitrary",)))(zpre, zpre, dgated, ln_g, ln_b, ws, bs_t)


ATT_SCALE = A_HEAD_DIM ** -0.5
PAIRS = A_HEADS // 2
PAIRS_PER_KV = PAIRS // A_KV_HEADS


def _att_padded(tile):
    lo = lax.broadcasted_iota(jnp.int32, tile.shape, 1) < A_HEAD_DIM
    rolled = pltpu.roll(tile, A_HEAD_DIM, 1)
    zero = jnp.zeros_like(tile)
    return {(0, 0): jnp.where(lo, tile, zero).astype(BF16), (0, 1): jnp.where(lo, zero, rolled).astype(BF16),
            (1, 0): jnp.where(lo, rolled, zero).astype(BF16), (1, 1): jnp.where(lo, zero, tile).astype(BF16)}


def _att_valid(n):
    r = lax.broadcasted_iota(jnp.int32, (WINDOW, 2 * WINDOW), 0)
    c = lax.broadcasted_iota(jnp.int32, (WINDOW, 2 * WINDOW), 1)
    rel = r + WINDOW - c
    return (rel >= 0) & (rel < WINDOW) & ((c >= WINDOW) | (n > 0))


def _att_probs(qp, kpad, sink, valid):
    s = jnp.where(valid, _dot(qp, kpad, NT), NEG_INF)
    m = jnp.maximum(jnp.max(s, axis=-1, keepdims=True), sink)
    p = jnp.exp(s - m)
    e_sink = jnp.exp(sink - m)
    inv = 1.0 / (jnp.sum(p, axis=-1, keepdims=True) + e_sink)
    return p * inv, e_sink * inv


def _att_operands(q_ref, kvc_ref, kvp_ref, s_ref):
    kv = jnp.concatenate([kvp_ref[...], kvc_ref[...]], axis=0)
    kpad, vpad = _att_padded(kv[:, :128]), _att_padded(kv[:, 128:])
    key = lambda h: ((h // 2) // PAIRS_PER_KV, h % 2)
    pairs = [(q_ref[:, j * 128:(j + 1) * 128] * ATT_SCALE).astype(BF16) for j in range(PAIRS)]
    q = jnp.stack([pairs[h // 2] for h in range(A_HEADS)])
    k = jnp.stack([kpad[key(h)] for h in range(A_HEADS)])
    v = jnp.stack([vpad[key(h)] for h in range(A_HEADS)])
    sink = jnp.stack([s_ref[:, h:h + 1] for h in range(A_HEADS)])
    return q, k, v, sink


def _att_specs(t):
    return [pl.BlockSpec((WINDOW, A_Q), lambda n: (n, 0)),
            pl.BlockSpec((WINDOW, 2 * A_KV), lambda n: (n, COL_KV // (2 * A_KV))),
            pl.BlockSpec((WINDOW, 2 * A_KV), lambda n: (jnp.maximum(n - 1, 0), COL_KV // (2 * A_KV))),
            pl.BlockSpec((1, 128), lambda n: (0, 0))]


def att_fwd(proj, sinks, *, name):
    t = proj.shape[0]

    def body(q_ref, kvc_ref, kvp_ref, s_ref, o_ref):
        n = pl.program_id(0)
        q, k, v, sink = _att_operands(q_ref, kvc_ref, kvp_ref, s_ref)
        w, _ = _att_probs(q, k, sink, _att_valid(n))
        o = _dot(w.astype(BF16), v, NN)
        for j in range(PAIRS):
            o_ref[:, j * 128:(j + 1) * 128] = (o[2 * j] + o[2 * j + 1]).astype(BF16)

    return pl.pallas_call(
        body, name=name, grid=(t // WINDOW,), in_specs=_att_specs(t),
        out_specs=pl.BlockSpec((WINDOW, A_Q), lambda n: (n, 0)),
        out_shape=jax.ShapeDtypeStruct((t, A_Q), BF16), compiler_params=_params(("parallel",)))(
            proj, proj, proj, sinks)


def att_bwd(proj, sinks, dout, *, name):
    t = proj.shape[0]

    def body(q_ref, kvc_ref, kvp_ref, s_ref, do_ref, dq_ref, dkc_ref, dkp_ref, ds_ref):
        n = pl.program_id(0)

        @pl.when(n == 0)
        def _():
            ds_ref[...] = jnp.zeros_like(ds_ref)

        q, k, v, sink = _att_operands(q_ref, kvc_ref, kvp_ref, s_ref)
        dop = jnp.stack([do_ref[:, (h // 2) * 128:(h // 2 + 1) * 128] for h in range(A_HEADS)]).astype(BF16)
        w, w_sink = _att_probs(q, k, sink, _att_valid(n))
        dw = _dot(dop, v, NT)
        delta = jnp.sum(w * dw, axis=-1, keepdims=True)
        dsc = (w * (dw - delta)).astype(BF16)
        dsink_h = -jnp.sum(w_sink * delta, axis=1, keepdims=True)
        dq = _dot(dsc, k, NN)
        dk_h = _dot(dsc, q, TN)
        dv_h = _dot(w.astype(BF16), dop, TN)
        lane = lax.broadcasted_iota(jnp.int32, (1, 128), 1)
        dsink = jnp.zeros((1, 128), F32)
        for h in range(A_HEADS):
            dsink = dsink + jnp.where(lane == h, dsink_h[h], 0.0)
        ds_ref[...] += dsink
        for j in range(PAIRS):
            dq_ref[:, j * 128:(j + 1) * 128] = ((dq[2 * j] + dq[2 * j + 1]) * ATT_SCALE).astype(BF16)
        lo = lax.broadcasted_iota(jnp.int32, (2 * WINDOW, 128), 1) < A_HEAD_DIM
        heads_per_kv = A_HEADS // A_KV_HEADS

        def tile(per_head):
            acc = {}
            for kvh in range(A_KV_HEADS):
                for half in range(2):
                    hs = range(kvh * heads_per_kv + half, (kvh + 1) * heads_per_kv, 2)
                    acc[(kvh, half)] = functools.reduce(lambda a, b: a + b, [per_head[h] for h in hs])
            return jnp.where(lo, acc[(0, 0)] + pltpu.roll(acc[(0, 1)], A_HEAD_DIM, 1),
                             pltpu.roll(acc[(1, 0)], A_HEAD_DIM, 1) + acc[(1, 1)])

        dkv = jnp.concatenate([tile(dk_h), tile(dv_h)], axis=1)
        dkp_ref[...] = dkv[:WINDOW]
        dkc_ref[...] = dkv[WINDOW:]

    kvo = pl.BlockSpec((WINDOW, 2 * A_KV), lambda n: (n, 0))
    return pl.pallas_call(
        body, name=name, grid=(t // WINDOW,),
        in_specs=_att_specs(t) + [pl.BlockSpec((WINDOW, A_Q), lambda n: (n, 0))],
        out_specs=[pl.BlockSpec((WINDOW, A_Q), lambda n: (n, 0)), kvo, kvo, pl.BlockSpec((1, 128), lambda n: (0, 0))],
        out_shape=[jax.ShapeDtypeStruct((t, A_Q), BF16), jax.ShapeDtypeStruct((t, 2 * A_KV), F32),
                   jax.ShapeDtypeStruct((t, 2 * A_KV), F32), jax.ShapeDtypeStruct((1, 128), F32)],
        compiler_params=_params(("arbitrary",)))(proj, proj, proj, sinks, dout)


QK_SCALE = B_HEAD_DIM ** -0.5
PREP_COLS = 256
PREP_NCB = 3 * B_W // PREP_COLS
HALO = 8


def _roll_rows(x, shift):
    n = x.shape[0]
    return x if shift % n == 0 else pltpu.roll(x, shift % n, 0)


def _conv_taps(xe, w):
    xs = [_roll_rows(xe, CONV_K - 1 - i) for i in range(CONV_K)]
    c = w[0:1] * xs[0]
    for i in range(1, CONV_K):
        c = c + w[i:i + 1] * xs[i]
    return xs, c


def dprep_fwd(proj, conv_w, *, name):
    t = proj.shape[0]
    tt = ROWS
    col0 = COL_QKVB // PREP_COLS

    def body(x_ref, h_ref, w_ref, o_ref):
        cb, n = pl.program_id(0), pl.program_id(1)
        halo = jnp.where(n > 0, h_ref[...], 0.0)
        xe = jnp.concatenate([halo, x_ref[...]], axis=0)
        _, c = _conv_taps(xe, w_ref[...])
        y = _silu(c)[HALO:]
        parts = []
        for hh in range(PREP_COLS // B_HEAD_DIM):
            yh = y[:, hh * B_HEAD_DIM:(hh + 1) * B_HEAD_DIM]
            parts.append(yh * lax.rsqrt(jnp.sum(yh * yh, axis=-1, keepdims=True) + EPS))
        nrm = jnp.concatenate(parts, axis=-1)
        o_ref[...] = jnp.where(cb < 4, nrm * QK_SCALE, jnp.where(cb < 8, nrm, y))

    return pl.pallas_call(
        body, name=name, grid=(PREP_NCB, t // tt),
        in_specs=[pl.BlockSpec((tt, PREP_COLS), lambda cb, n: (n, col0 + cb)),
                  pl.BlockSpec((HALO, PREP_COLS), lambda cb, n: (jnp.maximum(n * (tt // HALO) - 1, 0), col0 + cb)),
                  pl.BlockSpec((CONV_K, PREP_COLS), lambda cb, n: (0, cb))],
        out_specs=pl.BlockSpec((tt, PREP_COLS), lambda cb, n: (n, cb)),
        out_shape=jax.ShapeDtypeStruct((t, 3 * B_W), F32), compiler_params=_params(("parallel", "parallel")))(
            proj, proj, conv_w)


def dprep_bwd(proj, conv_w, dqkvn, *, name):
    t = proj.shape[0]
    tt = ROWS
    nb = t // tt
    col0 = COL_QKVB // PREP_COLS
    n8 = t // HALO

    def body(xc_ref, xb_ref, xa_ref, dc_ref, da_ref, w_ref, dx_ref, dw_ref):
        cb, n = pl.program_id(0), pl.program_id(1)

        @pl.when(n == 0)
        def _():
            dw_ref[...] = jnp.zeros_like(dw_ref)

        w = w_ref[...]
        xe = jnp.concatenate([jnp.where(n > 0, xb_ref[...], 0.0), xc_ref[...], xa_ref[...]], axis=0)
        xs, c = _conv_taps(xe, w)
        sg = _sigmoid(c)
        y = c * sg
        dout = jnp.concatenate([jnp.zeros((HALO, PREP_COLS), F32), dc_ref[...],
                                jnp.where(n < nb - 1, da_ref[...], 0.0)], axis=0)
        dsc = jnp.where(cb < 4, QK_SCALE, 1.0)
        parts = []
        for hh in range(PREP_COLS // B_HEAD_DIM):
            sl = slice(hh * B_HEAD_DIM, (hh + 1) * B_HEAD_DIM)
            yh, doh = y[:, sl], dout[:, sl] * dsc
            r = lax.rsqrt(jnp.sum(yh * yh, axis=-1, keepdims=True) + EPS)
            parts.append(doh * r - yh * (r * r * r) * jnp.sum(doh * yh, axis=-1, keepdims=True))
        dy = jnp.where(cb < 8, jnp.concatenate(parts, axis=-1), dout)
        dcv = dy * sg * (1.0 + c * (1.0 - sg))
        dxe = w[CONV_K - 1:CONV_K] * dcv
        for i in range(CONV_K - 1):
            dxe = dxe + w[i:i + 1] * _roll_rows(dcv, -(CONV_K - 1 - i))
        dx_ref[...] = dxe[HALO:HALO + tt].astype(BF16)
        for i in range(CONV_K):
            dw_ref[i:i + 1, :] += jnp.sum((dcv * xs[i])[HALO:HALO + tt], axis=0, keepdims=True)

    def after(n):
        return jnp.minimum((n + 1) * (tt // HALO), n8 - 1)

    return pl.pallas_call(
        body, name=name, grid=(PREP_NCB, nb),
        in_specs=[pl.BlockSpec((tt, PREP_COLS), lambda cb, n: (n, col0 + cb)),
                  pl.BlockSpec((HALO, PREP_COLS), lambda cb, n: (jnp.maximum(n * (tt // HALO) - 1, 0), col0 + cb)),
                  pl.BlockSpec((HALO, PREP_COLS), lambda cb, n: (after(n), col0 + cb)),
                  pl.BlockSpec((tt, PREP_COLS), lambda cb, n: (n, cb)),
                  pl.BlockSpec((HALO, PREP_COLS), lambda cb, n: (after(n), cb)),
                  pl.BlockSpec((CONV_K, PREP_COLS), lambda cb, n: (0, cb))],
        out_specs=[pl.BlockSpec((tt, PREP_COLS), lambda cb, n: (n, cb)),
                   pl.BlockSpec((CONV_K, PREP_COLS), lambda cb, n: (0, cb))],
        out_shape=[jax.ShapeDtypeStruct((t, 3 * B_W), BF16), jax.ShapeDtypeStruct((CONV_K, 3 * B_W), F32)],
        compiler_params=_params(("parallel", "arbitrary")))(proj, proj, proj, dqkvn, dqkvn, conv_w)


def _softplus(z):
    return jnp.maximum(z, 0.0) + jnp.log(1.0 + jnp.exp(-jnp.abs(z)))


def gates_fwd(proj, alog_pad, dtb_pad, *, name):
    t = proj.shape[0]

    def body(x_ref, a_ref, b_ref, o_ref):
        raw = x_ref[...]
        lane = lax.broadcasted_iota(jnp.int32, raw.shape, 1)
        g = -jnp.exp(a_ref[...]) * _softplus(raw + b_ref[...])
        o_ref[...] = jnp.where(lane < B_HEADS, _sigmoid(raw), jnp.where(lane < 2 * B_HEADS, g, 0.0))

    vec = pl.BlockSpec((1, 128), lambda n: (0, 0))
    return pl.pallas_call(
        body, name=name, grid=(t // ROWS,),
        in_specs=[pl.BlockSpec((ROWS, 128), lambda n: (n, COL_GATE // 128)), vec, vec],
        out_specs=pl.BlockSpec((ROWS, 128), lambda n: (n, 0)),
        out_shape=jax.ShapeDtypeStruct((t, 128), F32), compiler_params=_params(("parallel",)))(
            proj, alog_pad, dtb_pad)


def gates_bwd(proj, alog_pad, dtb_pad, dgates, *, name):
    t = proj.shape[0]

    def body(x_ref, a_ref, b_ref, dg_ref, dx_ref, da_ref, db_ref):
        @pl.when(pl.program_id(0) == 0)
        def _():
            da_ref[...] = jnp.zeros_like(da_ref)
            db_ref[...] = jnp.zeros_like(db_ref)

        raw, dgt = x_ref[...], dg_ref[...]
        lane = lax.broadcasted_iota(jnp.int32, raw.shape, 1)
        is_beta, is_g = lane < B_HEADS, (lane >= B_HEADS) & (lane < 2 * B_HEADS)
        beta = _sigmoid(raw)
        z = raw + b_ref[...]
        neg_a = -jnp.exp(a_ref[...])
        d_z = jnp.where(is_g, dgt * neg_a * _sigmoid(z), 0.0)
        dx_ref[...] = jnp.where(is_beta, dgt * beta * (1.0 - beta), d_z).astype(BF16)
        db_ref[...] += jnp.sum(d_z, axis=0, keepdims=True)
        da_ref[...] += jnp.sum(jnp.where(is_g, dgt * neg_a * _softplus(z), 0.0), axis=0, keepdims=True)

    vec = pl.BlockSpec((1, 128), lambda n: (0, 0))
    row = pl.BlockSpec((ROWS, 128), lambda n: (n, 0))
    return pl.pallas_call(
        body, name=name, grid=(t // ROWS,),
        in_specs=[pl.BlockSpec((ROWS, 128), lambda n: (n, COL_GATE // 128)), vec, vec, row],
        out_specs=[row, vec, vec],
        out_shape=[jax.ShapeDtypeStruct((t, 128), BF16), jax.ShapeDtypeStruct((1, 128), F32),
                   jax.ShapeDtypeStruct((1, 128), F32)],
        compiler_params=_params(("arbitrary",)))(proj, alog_pad, dtb_pad, dgates)


def _split2(a):
    hi = a.astype(BF16)
    return hi, (a - hi.astype(F32)).astype(BF16)


def _dotp(a, b, dims, passes):
    if passes == 1:
        return _dot(a.astype(BF16), b.astype(BF16), dims)
    ah, al = _split2(a)
    bh, bl = _split2(b)
    return _dot(ah, bh, dims) + (_dot(ah, bl, dims) + _dot(al, bh, dims))


_GRAD_DIMS = {NN: ((NT, False), (TN, False)), NT: ((NN, False), (TN, True)), TN: ((NT, True), (NN, False))}


def _make_mm(dims, passes):
    (da_dims, da_swap), (db_dims, db_swap) = _GRAD_DIMS[dims]

    @jax.custom_vjp
    def mm(a, b):
        return _dotp(a, b, dims, passes)

    def fwd(a, b):
        return _dotp(a, b, dims, passes), (a, b)

    def bwd(saved, ct):
        a, b = saved
        da = _dotp(b, ct, da_dims, passes) if da_swap else _dotp(ct, b, da_dims, passes)
        db = _dotp(ct, a, db_dims, passes) if db_swap else _dotp(a, ct, db_dims, passes)
        return da, db

    mm.defvjp(fwd, bwd)
    return mm


MM1 = {d: _make_mm(d, 1) for d in (NN, NT, TN)}
MM3 = {d: _make_mm(d, 3) for d in (NN, NT, TN)}


def _tri_ones(lower):
    r = lax.broadcasted_iota(jnp.int32, (DN_CHUNK, DN_CHUNK), 0)
    c = lax.broadcasted_iota(jnp.int32, (DN_CHUNK, DN_CHUNK), 1)
    return (r >= c if lower else r <= c).astype(BF16)


def _tri_sum(x, lower):
    tri = _tri_ones(lower)
    hi = x.astype(BF16)
    r1 = x - hi.astype(F32)
    mid = r1.astype(BF16)
    lo = (r1 - mid.astype(F32)).astype(BF16)
    return _dot(tri, hi, NN) + (_dot(tri, mid, NN) + _dot(tri, lo, NN))


def _delta_chunk(s0, q, k, v, beta, gam_c, gam_r):
    c = DN_CHUNK
    r = lax.broadcasted_iota(jnp.int32, (c, c), 0)
    cc = lax.broadcasted_iota(jnp.int32, (c, c), 1)
    incl, strict = r >= cc, r > cc
    eye = (r == cc).astype(F32)
    decay = jnp.exp(jnp.where(incl, gam_c - gam_r, NEG_INF))
    g_last = gam_c[:, c - 1:c, :]
    e_gam, e_rest, e_last = jnp.exp(gam_c), jnp.exp(g_last - gam_c), jnp.exp(g_last)
    a_neg = -jnp.where(strict, beta * MM1[NT](k, k) * decay, 0.0)
    inv = eye + a_neg
    pw = a_neg
    for _ in range(5):
        pw = MM3[NN](pw, pw)
        inv = inv + MM3[NN](inv, pw)
    uw = MM3[NN](inv, jnp.concatenate([v * beta, k * (beta * e_gam)], axis=-1))
    u, w = uw[..., :B_HEAD_DIM], uw[..., B_HEAD_DIM:]
    qk = MM1[NT](q, k) * decay
    v_new = u - MM1[NN](w, s0)
    o = MM1[NN](q * e_gam, s0) + MM1[NN](qk, v_new)
    s1 = s0 * e_last + MM1[TN](k * e_rest, v_new)
    return s1, o


def _delta_operands(q_ref, k_ref, v_ref, gt):
    heads = lambda ref: jnp.stack([ref[:, h * B_HEAD_DIM:(h + 1) * B_HEAD_DIM] for h in range(B_HEADS)])
    gam = _tri_sum(gt, True)
    gam_t = gam.T
    beta = jnp.stack([gt[:, h:h + 1] for h in range(B_HEADS)])
    gam_c = jnp.stack([gam[:, B_HEADS + h:B_HEADS + h + 1] for h in range(B_HEADS)])
    gam_r = jnp.stack([gam_t[B_HEADS + h:B_HEADS + h + 1, :] for h in range(B_HEADS)])
    return heads(q_ref), heads(k_ref), heads(v_ref), beta, gam_c, gam_r


def delta_fwd(qkvn, gates, *, name):
    t = qkvn.shape[0]
    nc = t // DN_CHUNK

    def body(q_ref, k_ref, v_ref, g_ref, o_ref, ss_ref, state):
        @pl.when(pl.program_id(0) == 0)
        def _():
            state[...] = jnp.zeros_like(state)

        s0 = state[...]
        ss_ref[...] = s0
        s1, o = _delta_chunk(s0, *_delta_operands(q_ref, k_ref, v_ref, g_ref[...]))
        state[...] = s1
        for h in range(B_HEADS):
            o_ref[:, h * B_HEAD_DIM:(h + 1) * B_HEAD_DIM] = o[h]

    blk = lambda j: pl.BlockSpec((DN_CHUNK, B_W), lambda n: (n, j))
    return pl.pallas_call(
        body, name=name, grid=(nc,),
        in_specs=[blk(0), blk(1), blk(2), pl.BlockSpec((DN_CHUNK, 128), lambda n: (n, 0))],
        out_specs=[blk(0), pl.BlockSpec((None, B_HEADS, B_HEAD_DIM, B_HEAD_DIM), lambda n: (n, 0, 0, 0))],
        out_shape=[jax.ShapeDtypeStruct((t, B_W), F32),
                   jax.ShapeDtypeStruct((nc, B_HEADS, B_HEAD_DIM, B_HEAD_DIM), F32)],
        scratch_shapes=[pltpu.VMEM((B_HEADS, B_HEAD_DIM, B_HEAD_DIM), F32)],
        compiler_params=_params(("arbitrary",)))(qkvn, qkvn, qkvn, gates)


def delta_bwd(qkvn, gates, ssave, do, *, name):
    t = qkvn.shape[0]
    nc = t // DN_CHUNK

    def body(q_ref, k_ref, v_ref, g_ref, ss_ref, do_ref, dx_ref, dg_ref, dstate):
        @pl.when(pl.program_id(0) == 0)
        def _():
            dstate[...] = jnp.zeros_like(dstate)

        lane = lax.broadcasted_iota(jnp.int32, (DN_CHUNK, 128), 1)
        row = lax.broadcasted_iota(jnp.int32, (128, DN_CHUNK), 0)
        dbeta_all = jnp.zeros((DN_CHUNK, 128), F32)
        dgam_c_all = jnp.zeros((DN_CHUNK, 128), F32)
        dgam_r_all = jnp.zeros((128, DN_CHUNK), F32)
        _, vjp = jax.vjp(_delta_chunk, ss_ref[...], *_delta_operands(q_ref, k_ref, v_ref, g_ref[...]))
        do = jnp.stack([do_ref[:, h * B_HEAD_DIM:(h + 1) * B_HEAD_DIM] for h in range(B_HEADS)])
        ds0, dq, dk, dv, dbeta, dgam_c, dgam_r = vjp((dstate[...], do))
        dstate[...] = ds0
        for h in range(B_HEADS):
            dx_ref[:, h * B_HEAD_DIM:(h + 1) * B_HEAD_DIM] = dq[h]
            dx_ref[:, B_W + h * B_HEAD_DIM:B_W + (h + 1) * B_HEAD_DIM] = dk[h]
            dx_ref[:, 2 * B_W + h * B_HEAD_DIM:2 * B_W + (h + 1) * B_HEAD_DIM] = dv[h]
            dbeta_all = dbeta_all + jnp.where(lane == h, dbeta[h], 0.0)
            dgam_c_all = dgam_c_all + jnp.where(lane == B_HEADS + h, dgam_c[h], 0.0)
            dgam_r_all = dgam_r_all + jnp.where(row == B_HEADS + h, dgam_r[h], 0.0)
        dg_ref[...] = dbeta_all + _tri_sum(dgam_c_all + dgam_r_all.T, False)

    blk = lambda j: pl.BlockSpec((DN_CHUNK, B_W), lambda n: (nc - 1 - n, j))
    gsp = pl.BlockSpec((DN_CHUNK, 128), lambda n: (nc - 1 - n, 0))
    return pl.pallas_call(
        body, name=name, grid=(nc,),
        in_specs=[blk(0), blk(1), blk(2), gsp,
                  pl.BlockSpec((None, B_HEADS, B_HEAD_DIM, B_HEAD_DIM), lambda n: (nc - 1 - n, 0, 0, 0)), blk(0)],
        out_specs=[pl.BlockSpec((DN_CHUNK, 3 * B_W), lambda n: (nc - 1 - n, 0)), gsp],
        out_shape=[jax.ShapeDtypeStruct((t, 3 * B_W), F32), jax.ShapeDtypeStruct((t, 128), F32)],
        scratch_shapes=[pltpu.VMEM((B_HEADS, B_HEAD_DIM, B_HEAD_DIM), F32)],
        compiler_params=_params(("arbitrary",)))(qkvn, qkvn, qkvn, gates, ssave, do)


def gnorm_fwd(o, proj, onorm, *, name):
    t = o.shape[0]

    def body(o_ref, z_ref, w_ref, out_ref):
        ov = o_ref[...]
        r = lax.rsqrt(jnp.mean(ov * ov, axis=-1, keepdims=True) + EPS)
        out_ref[...] = (ov * r * w_ref[...] * _silu(z_ref[...])).astype(BF16)

    blk = pl.BlockSpec((ROWS, B_HEAD_DIM), lambda n, h: (n, h))
    return pl.pallas_call(
        body, name=name, grid=(t // ROWS, B_HEADS),
        in_specs=[blk, pl.BlockSpec((ROWS, B_HEAD_DIM), lambda n, h: (n, COL_Z // B_HEAD_DIM + h)),
                  pl.BlockSpec((1, B_HEAD_DIM), lambda n, h: (0, 0))],
        out_specs=blk, out_shape=jax.ShapeDtypeStruct((t, B_W), BF16),
        compiler_params=_params(("parallel", "parallel")))(o, proj, onorm)


def gnorm_bwd(o, proj, onorm, dout, *, dcol0, name):
    t = o.shape[0]

    def body(o_ref, z_ref, w_ref, d_ref, do_ref, dz_ref, dw_ref):
        @pl.when((pl.program_id(0) == 0) & (pl.program_id(1) == 0))
        def _():
            dw_ref[...] = jnp.zeros_like(dw_ref)

        ov, zv, wv, dv = o_ref[...], z_ref[...], w_ref[...], d_ref[...].astype(F32)
        r = lax.rsqrt(jnp.mean(ov * ov, axis=-1, keepdims=True) + EPS)
        nrm = ov * r
        dz_ref[...] = (dv * nrm * wv * _dsilu(zv)).astype(BF16)
        da = dv * _silu(zv)
        dw_ref[...] += jnp.sum(da * nrm, axis=0, keepdims=True)
        dn = da * wv
        do_ref[...] = r * dn - ov * (r * r * r) * jnp.mean(dn * ov, axis=-1, keepdims=True)

    blk = pl.BlockSpec((ROWS, B_HEAD_DIM), lambda n, h: (n, h))
    vec = pl.BlockSpec((1, B_HEAD_DIM), lambda n, h: (0, 0))
    return pl.pallas_call(
        body, name=name, grid=(t // ROWS, B_HEADS),
        in_specs=[blk, pl.BlockSpec((ROWS, B_HEAD_DIM), lambda n, h: (n, COL_Z // B_HEAD_DIM + h)), vec,
                  pl.BlockSpec((ROWS, B_HEAD_DIM), lambda n, h: (n, dcol0 // B_HEAD_DIM + h))],
        out_specs=[blk, blk, vec],
        out_shape=[jax.ShapeDtypeStruct((t, B_W), F32), jax.ShapeDtypeStruct((t, B_W), BF16),
                   jax.ShapeDtypeStruct((1, B_HEAD_DIM), F32)],
        compiler_params=_params(("arbitrary", "arbitrary")))(o, proj, onorm, dout)


def _ffn_fwd(h, norm_g, wg, wu, wd, tm, tag):
    hn = rms_fwd(h, norm_g, name=f"ffn{tag}_norm")
    gate, up, act = mm_gate_up(hn, wg, wu, tm=min(512, tm), tn=1408, tk=2048, name=f"ffn{tag}_gate_up")
    h_out = mm_nn(act, wd, tm=tm, tn=2048, tk=512, out_dtype=F32, res=h, name=f"ffn{tag}_down")
    return h_out, (hn, gate, up, act)


def _ffn_bwd(dh, h, norm_g, wg, wu, wd, saved, tm, tag, emit):
    hn, gate, up, act = saved
    dwd = mm_tn(act, dh, shards=1, tm=tm, tn=1024, tk=1408, out_dtype=BF16, name=f"ffn{tag}_dwd")[0]
    dgate, dup = mm_down_bwd(dh, wd, gate, up, tm=tm, tn=512, tk=2048, name=f"ffn{tag}_dact")
    dwg = mm_tn(hn, dgate, shards=N_SHARD, tm=tm, tn=1408, tk=1024, out_dtype=BF16, name=f"ffn{tag}_dwg")
    dwu = mm_tn(hn, dup, shards=N_SHARD, tm=tm, tn=1408, tk=1024, out_dtype=BF16, name=f"ffn{tag}_dwu")
    started = emit(f"ffn{tag}", {"gate": dwg, "up": dwu, "down": dwd})
    dhn = mm_nt(dgate, wg, tm=tm, tn=1024, tk=1408, out_dtype=F32, name=f"ffn{tag}_dhn_g")
    dhn = mm_nt(dup, wu, tm=tm, tn=1024, tk=1408, out_dtype=F32, res=dhn, name=f"ffn{tag}_dhn_u")
    dh_in, dnorm = rms_bwd(h, norm_g + started, dhn, dh, name=f"ffn{tag}_dnorm")
    return dh_in, dnorm


def _local_step(x, target, w, get, emit):
    t = x.shape[0]
    tm = min(1024, t)
    g = {}

    hn0 = rms_fwd(x, w["even_norm"], name="l0_norm")
    w.update(get("even_in", hn0))
    proj = mm_nn(hn0, w["even_w_in"], tm=tm, tn=512, tk=2048, out_dtype=F32, name="l0_w_in")
    out_a = att_fwd(proj, w["sinks"], name="l0_att")
    qkvn = dprep_fwd(proj, w["even_conv"], name="l0_prep")
    gates = gates_fwd(proj, w["a_log"], w["dt_bias"], name="l0_gates")
    o_delta, ssave = delta_fwd(qkvn, gates, name="l0_delta")
    out_b = gnorm_fwd(o_delta, proj, w["onorm"], name="l0_gnorm")
    mix0 = jnp.concatenate([out_a, out_b], axis=-1)
    w.update(get("even_out", mix0))
    h1 = mm_nn(mix0, w["even_w_out"], tm=tm, tn=512, tk=2048, out_dtype=F32, res=x, name="l0_w_out")
    f0 = get("ffn0", h1)
    h2, ffn0 = _ffn_fwd(h1, w["ffn_norm"][0:1], f0["gate"], f0["up"], f0["down"], tm, 0)
    hn2 = rms_fwd(h2, w["odd_norm"], name="l1_norm")
    w.update(get("odd", hn2))
    zpre = mm_nn(hn2, w["odd_w_in"], tm=tm, tn=1024, tk=2048, out_dtype=F32, name="l1_w_in")
    gated = gmlp_fwd(zpre, w["odd_ln_g"], w["odd_ln_b"], w["odd_w_s"], w["odd_b_s"], name="l1_gmlp")
    h3 = mm_nn(gated, w["odd_w_out"], tm=tm, tn=512, tk=2048, out_dtype=F32, res=h2, name="l1_w_out")
    f1 = get("ffn1", h3)
    h4, ffn1 = _ffn_fwd(h3, w["ffn_norm"][1:2], f1["gate"], f1["up"], f1["down"], tm, 1)
    loss, dh4, g["final_norm"] = loss_head(h4, w["final_norm"], target, name="loss_head")

    dh3, dn1 = _ffn_bwd(dh4, h3, w["ffn_norm"][1:2], f1["gate"], f1["up"], f1["down"], ffn1, tm, 1, emit)
    dw_out_o = mm_tn(gated, dh3, shards=1, tm=tm, tn=1024, tk=1024, out_dtype=BF16, name="l1_dw_out")[0]
    dgated = mm_nt(dh3, w["odd_w_out"], tm=tm, tn=512, tk=2048, out_dtype=BF16, name="l1_dgated")
    dzpre, g["odd_w_s"], g["odd_b_s"], g["odd_ln_g"], g["odd_ln_b"] = gmlp_bwd(
        zpre, dgated, w["odd_ln_g"], w["odd_ln_b"], w["odd_w_s"], w["odd_b_s"], name="l1_dgmlp")
    dw_in_o = mm_tn(hn2, dzpre, shards=N_SHARD, tm=tm, tn=1024, tk=1024, out_dtype=BF16, name="l1_dw_in")
    started = emit("odd", {"odd_w_in": dw_in_o, "odd_w_out": dw_out_o})
    dhn2 = mm_nt(dzpre, w["odd_w_in"], tm=tm, tn=1024, tk=1024, out_dtype=F32, name="l1_dhn")
    dh2, g["odd_norm"] = rms_bwd(h2, w["odd_norm"] + started, dhn2, dh3, name="l1_dnorm")
    dh1, dn0 = _ffn_bwd(dh2, h1, w["ffn_norm"][0:1], f0["gate"], f0["up"], f0["down"], ffn0, tm, 0, emit)
    g["ffn_norm"] = jnp.concatenate([dn0, dn1], axis=0)
    dw_out_e = mm_tn(mix0, dh1, shards=1, tm=tm, tn=1024, tk=1024, out_dtype=BF16, name="l0_dw_out")[0]
    started = emit("even_out", {"even_w_out": dw_out_e})
    dmix = mm_nt(dh1, w["even_w_out"], tm=tm, tn=512, tk=2048, out_dtype=F32, name="l0_dmix")
    dq_a, dkv_cur, dkv_prev, g["sinks"] = att_bwd(proj, w["sinks"] + started, dmix, name="l0_datt")
    dkv = dkv_cur + jnp.concatenate([dkv_prev[WINDOW:], jnp.zeros((WINDOW, 2 * A_KV), F32)], axis=0)
    do_delta, dz, g["onorm"] = gnorm_bwd(o_delta, proj, w["onorm"], dmix, dcol0=A_Q, name="l0_dgnorm")
    dqkvn, dgates = delta_bwd(qkvn, gates, ssave, do_delta, name="l0_ddelta")
    dqkv_b, g["even_conv"] = dprep_bwd(proj, w["even_conv"], dqkvn, name="l0_dprep")
    draw, g["a_log"], g["dt_bias"] = gates_bwd(proj, w["a_log"], w["dt_bias"], dgates, name="l0_dgates")
    dproj = jnp.concatenate([dq_a, dkv.astype(BF16), dqkv_b, dz, draw,
                             jnp.zeros((t, EVEN_IN_PAD - COL_GATE - 128), BF16)], axis=-1)
    dw_in_e = mm_tn(hn0, dproj, shards=1, tm=tm, tn=1408, tk=1024, out_dtype=BF16, name="l0_dw_in")[0]
    dhn0 = mm_nt(dproj, w["even_w_in"], tm=tm, tn=1024, tk=2816, out_dtype=F32, name="l0_dhn")
    grad_x, g["even_norm"] = rms_bwd(x, w["even_norm"], dhn0, dh1, name="l0_dnorm")
    emit("even_in", {"even_w_in": dw_in_e, "small": g})
    return loss, grad_x


ANY = pl.BlockSpec(memory_space=pl.ANY)
N_DEV = 8


def _place():
    return lax.axis_index("x"), lax.axis_index("y"), lax.axis_index("c")


def _chip_peers(x, y, c):
    return [((1 - x, y, c), 2 * (1 - x) + y), ((x, 1 - y, c), 2 * x + 1 - y), ((1 - x, 1 - y, c), 2 * (1 - x) + 1 - y)]


HBM = pl.BlockSpec(memory_space=pltpu.HBM)
SEM = pl.BlockSpec(memory_space=pltpu.SEMAPHORE)
EFFECT = pltpu.SideEffectType.DATAFLOW_SIDE_EFFECTING
N_PEER = 3


def _gather_plan(srcs, lands, send, recv):
    x, y, c = _place()
    return [pltpu.make_async_remote_copy(src_ref=srcs[i], dst_ref=lands[i].at[2 * x + y], send_sem=send.at[N_PEER * i + k],
                                         recv_sem=recv.at[N_PEER * i + k], device_id=peer, device_id_type=MESH_ID)
            for i in range(len(srcs)) for k, (peer, _) in enumerate(_chip_peers(x, y, c))]


def _scatter_plan(srcs, lands, send, recv):
    x, y, c = _place()
    return [pltpu.make_async_remote_copy(src_ref=srcs[i].at[idx], dst_ref=lands[i].at[k], send_sem=send.at[N_PEER * i + k],
                                         recv_sem=recv.at[N_PEER * i + k], device_id=peer, device_id_type=MESH_ID)
            for i in range(len(srcs)) for k, (peer, idx) in enumerate(_chip_peers(x, y, c))]


def copies_start(plan, srcs, lands, after, *, name):
    n = len(srcs)
    both = list(srcs) + list(lands)

    def body(*refs):
        src_refs, land_refs = refs[:n], refs[n:2 * n]
        send, recv = refs[2 * n + 1], refs[2 * n + 2]
        for cp in plan(src_refs, land_refs, send, recv):
            cp.start()
        refs[-1][...] = jnp.zeros_like(refs[-1])

    res = pl.pallas_call(
        body, name=name,
        out_shape=(pltpu.SemaphoreType.DMA((n * N_PEER,)), pltpu.SemaphoreType.DMA((n * N_PEER,)),
                   *[pltpu.HBM(a.shape, a.dtype) for a in both], jax.ShapeDtypeStruct((8, 128), F32)),
        in_specs=[HBM] * (2 * n) + [ANY],
        out_specs=(SEM, SEM, *[HBM] * (2 * n), pl.BlockSpec(memory_space=pltpu.VMEM)),
        input_output_aliases={i: 2 + i for i in range(2 * n)},
        compiler_params=pltpu.CompilerParams(has_side_effects=EFFECT))(
            *[pltpu.with_memory_space_constraint(a, pltpu.HBM) for a in both], after)
    return {"send": res[0], "recv": res[1], "srcs": list(res[2:2 + n]), "lands": list(res[2 + n:2 + 2 * n]),
            "token": res[-1]}


def copies_wait(plan, started, after, *, name):
    srcs, lands = started["srcs"], started["lands"]
    n = len(srcs)
    both = srcs + lands

    def body(*refs):
        src_refs, land_refs = refs[:n], refs[n:2 * n]
        send, recv = refs[2 * n], refs[2 * n + 1]
        for cp in plan(src_refs, land_refs, send, recv):
            cp.wait_send()
            cp.wait_recv()

    res = pl.pallas_call(
        body, name=name, out_shape=tuple(pltpu.HBM(a.shape, a.dtype) for a in both),
        in_specs=[HBM] * (2 * n) + [SEM, SEM, ANY], out_specs=(HBM,) * (2 * n),
        input_output_aliases={i: i for i in range(2 * n)},
        compiler_params=pltpu.CompilerParams(has_side_effects=EFFECT))(*both, started["send"], started["recv"], after)
    return list(res[:n]), list(res[n:])


def allgather_small(small, *, name):
    def body(small_ref, out_ref, send, recv, loc):
        x, y, c = _place()
        dev = 4 * x + 2 * y + c
        local = pltpu.make_async_copy(small_ref, out_ref.at[dev], loc)
        remote = []
        for r in range(1, N_DEV):
            fx, fy, fc = (r >> 2) & 1, (r >> 1) & 1, r & 1
            peer = (1 - x if fx else x, 1 - y if fy else y, 1 - c if fc else c)
            remote.append(pltpu.make_async_remote_copy(
                src_ref=small_ref, dst_ref=out_ref.at[dev], send_sem=send.at[r - 1], recv_sem=recv.at[r - 1],
                device_id=peer, device_id_type=MESH_ID))
        local.start()
        for cp in remote:
            cp.start()
        for cp in remote:
            cp.wait()
        local.wait()

    return pl.pallas_call(
        body, name=name, in_specs=[ANY], out_specs=ANY,
        out_shape=jax.ShapeDtypeStruct((N_DEV,) + small.shape, small.dtype),
        scratch_shapes=[pltpu.SemaphoreType.DMA((N_DEV - 1,)), pltpu.SemaphoreType.DMA((N_DEV - 1,)),
                        pltpu.SemaphoreType.DMA(())])(small)


def swap_cores(arrs, *, name):
    n = len(arrs)

    def body(*refs):
        ins, outs = refs[:n], refs[n:2 * n]
        send, recv = refs[2 * n:]
        x, y, c = _place()
        copies = [pltpu.make_async_remote_copy(src_ref=ins[i], dst_ref=outs[i], send_sem=send.at[i], recv_sem=recv.at[i],
                                               device_id=(x, y, 1 - c), device_id_type=MESH_ID) for i in range(n)]
        for cp in copies:
            cp.start()
        for cp in copies:
            cp.wait()

    return pl.pallas_call(
        body, name=name, in_specs=[ANY] * n, out_specs=[ANY] * n,
        out_shape=[jax.ShapeDtypeStruct(a.shape, a.dtype) for a in arrs],
        scratch_shapes=[pltpu.SemaphoreType.DMA((n,)), pltpu.SemaphoreType.DMA((n,))])(*arrs)


RED_ROWS = 128


def sum_chips(own, got, *, name):
    r, c = own.shape
    rb = RED_ROWS if r % RED_ROWS == 0 else r

    def body(o_ref, a_ref, b_ref, c_ref, out_ref):
        out_ref[...] = ((o_ref[...].astype(F32) + a_ref[...].astype(F32)) + b_ref[...].astype(F32)) + c_ref[...].astype(F32)

    gk = lambda k: pl.BlockSpec((None, rb, c), lambda i: (k, i, 0))
    row = pl.BlockSpec((rb, c), lambda i: (i, 0))
    return pl.pallas_call(
        body, name=name, grid=(r // rb,), in_specs=[row, gk(0), gk(1), gk(2)], out_specs=row,
        out_shape=jax.ShapeDtypeStruct((r, c), F32), compiler_params=_params(("parallel",)))(own, got, got, got)


def sum_devices(small_all, *, name):
    _, p, c = small_all.shape

    def body(a_ref, out_ref):
        acc = a_ref[0]
        for d in range(1, N_DEV):
            acc = acc + a_ref[d]
        out_ref[...] = acc

    return pl.pallas_call(
        body, name=name, grid=(1,), in_specs=[pl.BlockSpec((N_DEV, p, c), lambda i: (0, 0, 0))],
        out_specs=pl.BlockSpec((p, c), lambda i: (0, 0)), out_shape=jax.ShapeDtypeStruct((p, c), F32),
        compiler_params=_params(("arbitrary",)))(small_all)


def adamw(parts, w, m, v, *, name):
    nl, r, c = w.shape
    assert len(parts) == nl
    npart = len(parts[0])
    rb = RED_ROWS if r % RED_ROWS == 0 else r
    flat = [a for layer in parts for a in layer]

    def body(*refs):
        p_refs, (w_ref, m_ref, v_ref) = refs[:nl * npart], refs[nl * npart:nl * npart + 3]
        g_ref, d_ref, nm_ref, nv_ref = refs[nl * npart + 3:]
        layer = pl.program_id(0)
        grad = None
        for l in range(nl):
            gl = p_refs[l * npart][...]
            for j in range(1, npart):
                gl = gl + p_refs[l * npart + j][...]
            grad = gl if grad is None else jnp.where(layer == l, gl, grad)
        wv, mv, vv = w_ref[...], m_ref[...], v_ref[...]
        nm = ADAM_B1 * mv + (1.0 - ADAM_B1) * grad
        nv = ADAM_B2 * vv + (1.0 - ADAM_B2) * (grad * grad)
        m_hat = nm / (1.0 - ADAM_B1 ** ADAM_STEP)
        v_hat = nv / (1.0 - ADAM_B2 ** ADAM_STEP)
        g_ref[...] = grad
        d_ref[...] = -ADAM_LR * (m_hat / (jnp.sqrt(v_hat) + ADAM_EPS) + ADAM_WD * wv)
        nm_ref[...] = nm
        nv_ref[...] = nv

    pspec = pl.BlockSpec((rb, c), lambda l, i: (i, 0))
    wspec = pl.BlockSpec((None, rb, c), lambda l, i: (l, i, 0))
    osh = jax.ShapeDtypeStruct((nl, r, c), F32)
    return pl.pallas_call(
        body, name=name, grid=(nl, r // rb), in_specs=[pspec] * (nl * npart) + [wspec] * 3,
        out_specs=[wspec] * 4, out_shape=[osh] * 4, compiler_params=_params(("parallel", "parallel")))(*flat, w, m, v)


def _rows128(a):
    flat = a.reshape(-1)
    pad = (-flat.shape[0]) % 128
    return jnp.pad(flat, (0, pad)).reshape(-1, 128)


def _pack_rows(arrs, multiple=8):
    rows = jnp.concatenate([_rows128(a.astype(F32)) for a in arrs], axis=0)
    return jnp.pad(rows, ((0, (-rows.shape[0]) % multiple), (0, 0)))


def _unpack_rows(rows, shapes):
    out, r0 = [], 0
    for shp in shapes:
        size = 1
        for s in shp:
            size *= s
        nr = -(-size // 128)
        out.append(rows[r0:r0 + nr].reshape(-1)[:size].reshape(shp))
        r0 += nr
    return out


SMALL_LOCAL_GRADS = ["even_norm", "even_conv", "a_log", "dt_bias", "sinks", "onorm", "odd_norm", "odd_ln_g",
                     "odd_ln_b", "odd_w_s", "odd_b_s", "ffn_norm", "final_norm"]
BIG = ["even_w_in", "even_w_out", "odd_w_in", "odd_w_out", "ffn_w_gate", "ffn_w_up", "ffn_w_down"]
WEIGHTS = ["even_norm", "even_w_in", "even_conv", "even_a_log", "even_dt_bias", "even_sinks", "even_onorm",
           "even_w_out", "odd_norm", "odd_w_in", "odd_ln_g", "odd_ln_b", "odd_w_s", "odd_b_s", "odd_w_out",
           "ffn_norm", "ffn_w_gate", "ffn_w_up", "ffn_w_down", "final_norm"]
SMALL = [n for n in WEIGHTS if n not in BIG]


def kernel(x, even_norm, even_w_in, even_conv, even_a_log, even_dt_bias, even_sinks, even_onorm, even_w_out, odd_norm, odd_w_in, odd_ln_g, odd_ln_b, odd_w_s, odd_b_s, odd_w_out, ffn_norm, ffn_w_gate, ffn_w_up, ffn_w_down, final_norm, loss_target, m_even_norm, m_even_w_in, m_even_conv, m_even_a_log, m_even_dt_bias, m_even_sinks, m_even_onorm, m_even_w_out, m_odd_norm, m_odd_w_in, m_odd_ln_g, m_odd_ln_b, m_odd_w_s, m_odd_b_s, m_odd_w_out, m_ffn_norm, m_ffn_w_gate, m_ffn_w_up, m_ffn_w_down, m_final_norm, v_even_norm, v_even_w_in, v_even_conv, v_even_a_log, v_even_dt_bias, v_even_sinks, v_even_onorm, v_even_w_out, v_odd_norm, v_odd_w_in, v_odd_ln_g, v_odd_ln_b, v_odd_w_s, v_odd_b_s, v_odd_w_out, v_ffn_norm, v_ffn_w_gate, v_ffn_w_up, v_ffn_w_down, v_final_norm):
    args = dict(locals())
    wl = {n: args[n] for n in WEIGHTS}
    ml = {n: args["m_" + n] for n in WEIGHTS}
    vl = {n: args["v_" + n] for n in WEIGHTS}
    me = 2 * lax.axis_index("x") + lax.axis_index("y")

    def landing(a):
        return lax.dynamic_update_index_in_dim(lax.empty((N_SHARD,) + a.shape, a.dtype), a, me, 0)

    b16 = lambda *arrs: [a.astype(BF16) for a in arrs]
    gather_groups = {
        "even_in": b16(even_w_in[0]) + [_pack_rows([even_conv[0], odd_norm, odd_ln_g, odd_ln_b])],
        "even_out": b16(even_w_out[0]),
        "ffn0": b16(ffn_w_gate[0], ffn_w_up[0], ffn_w_down[0]),
        "odd": b16(odd_w_in[0], odd_w_out[0]),
        "ffn1": b16(ffn_w_gate[1], ffn_w_up[1], ffn_w_down[1]),
    }
    gathering, after = {}, even_norm
    for group, srcs in gather_groups.items():
        gathering[group] = copies_start(_gather_plan, srcs, [landing(a) for a in srcs], after,
                                        name=f"gather_{group}_start")
        after = gathering[group]["token"]

    def get(group, behind):
        _, lands = copies_wait(_gather_plan, gathering[group], behind, name=f"gather_{group}_wait")
        if group == "even_in":
            parts = zip(*[_unpack_rows(lands[1][s], [(CONV_K, 768), (1, 512), (1, 512), (1, 512)])
                          for s in range(N_SHARD)])
            conv, onorm, lng, lnb = [jnp.concatenate(p, axis=1) for p in parts]
            w_in = jnp.pad(jnp.transpose(lands[0], (1, 0, 2)).reshape(D_MODEL, EVEN_IN),
                           ((0, 0), (0, EVEN_IN_PAD - EVEN_IN)))
            return {"even_w_in": w_in, "even_conv": conv, "odd_norm": onorm, "odd_ln_g": lng, "odd_ln_b": lnb}
        if group == "even_out":
            return {"even_w_out": lands[0].reshape(D_MODEL, D_MODEL)}
        if group == "odd":
            return {"odd_w_in": lands[0], "odd_w_out": lands[1].reshape(D_MODEL, D_MODEL)}
        return {"gate": lands[0], "up": lands[1], "down": lands[2].reshape(D_FF, D_MODEL)}

    rows4 =lambda a: a.reshape(N_SHARD, a.shape[0] // N_SHARD, a.shape[1])
    scattering, small = {}, {}

    def emit(group, grads):
        behind = even_norm
        if group == "even_in":
            small["local"] = grads["small"]
            small["all"] = behind = allgather_small(_pack_rows([grads["small"][n] for n in SMALL_LOCAL_GRADS]),
                                                    name="allgather_small")
            srcs = [jnp.transpose(grads["even_w_in"][:, :EVEN_IN].reshape(D_MODEL, N_SHARD, EVEN_IN // N_SHARD),
                                  (1, 0, 2))]
        elif group == "even_out":
            srcs = [rows4(grads["even_w_out"])]
        elif group == "odd":
            srcs = [grads["odd_w_in"], rows4(grads["odd_w_out"])]
        else:
            srcs = [grads["gate"], grads["up"], rows4(grads["down"])]
        lands = [lax.empty((N_PEER,) + a.shape[1:], a.dtype) for a in srcs]
        scattering[group] = copies_start(_scatter_plan, srcs, lands, behind, name=f"scatter_{group}_start")
        return scattering[group]["token"][0:1, 0:1]

    pad816 = lambda a: jnp.pad(a, ((0, 0), (B_HEADS, 128 - 2 * B_HEADS)))
    w = {
        "even_norm": even_norm + after[0:1, 0:1],
        "a_log": pad816(even_a_log), "dt_bias": pad816(even_dt_bias),
        "sinks": jnp.pad(even_sinks, ((0, 0), (0, 128 - A_HEADS))),
        "onorm": even_onorm,
        "odd_w_s": odd_w_s[0],
        "odd_b_s": jnp.pad(odd_b_s[0].T, ((0, 0), (0, 128 - C_GROUPS))),
        "ffn_norm": ffn_norm,
        "final_norm": final_norm[None],
    }
    loss_l, grad_x = _local_step(x[0], loss_target[0], w, get, emit)
    loss = lax.psum(loss_l[0, 0], ("x", "y", "c"))

    def finish(group, behind):
        srcs, lands = copies_wait(_scatter_plan, scattering[group], behind, name=f"scatter_{group}_wait")
        partial = [sum_chips(lax.dynamic_index_in_dim(srcs[i], me, 0, keepdims=False), lands[i],
                             name=f"sum_chips_{group}_{i}") for i in range(len(srcs))]
        other = swap_cores(partial, name=f"swap_cores_{group}")
        return list(zip(partial, other))

    last_started = scattering["even_in"]["token"]
    sums = {group: finish(group, last_started) for group in ("ffn1", "odd", "ffn0", "even_out")}
    outs = {}
    parts_of = {"even_w_out": [sums["even_out"][0]], "odd_w_in": [sums["odd"][0]], "odd_w_out": [sums["odd"][1]],
                "ffn_w_gate": [sums["ffn0"][0], sums["ffn1"][0]], "ffn_w_up": [sums["ffn0"][1], sums["ffn1"][1]],
                "ffn_w_down": [sums["ffn0"][2], sums["ffn1"][2]]}
    for n in parts_of:
        outs[n] = adamw(parts_of[n], wl[n], ml[n], vl[n], name=f"adamw_{n}")
    outs["even_w_in"] = adamw([finish("even_in", outs["ffn_w_down"][1])[0]], wl["even_w_in"], ml["even_w_in"],
                              vl["even_w_in"], name="adamw_even_w_in")

    g = small["local"]
    small_sum = sum_devices(small["all"], name="sum_devices")
    sg = dict(zip(SMALL_LOCAL_GRADS, _unpack_rows(small_sum, [g[n].shape for n in SMALL_LOCAL_GRADS])))
    own_cols = lambda a, width: lax.dynamic_slice_in_dim(a, me * width, width, axis=a.ndim - 1)
    small_grads = {
        "even_norm": sg["even_norm"], "even_conv": own_cols(sg["even_conv"], 768)[None],
        "even_a_log": sg["a_log"][:, B_HEADS:2 * B_HEADS], "even_dt_bias": sg["dt_bias"][:, B_HEADS:2 * B_HEADS],
        "even_sinks": sg["sinks"][:, :A_HEADS], "even_onorm": sg["onorm"],
        "odd_norm": own_cols(sg["odd_norm"], 512), "odd_ln_g": own_cols(sg["odd_ln_g"], 512),
        "odd_ln_b": own_cols(sg["odd_ln_b"], 512), "odd_w_s": sg["odd_w_s"][None],
        "odd_b_s": sg["odd_b_s"][:, :C_GROUPS].T[None], "ffn_norm": sg["ffn_norm"], "final_norm": sg["final_norm"][0],
    }
    packed = [_pack_rows([d[n] for n in SMALL])[None] for d in (small_grads, wl, ml, vl)]
    small_out = adamw([(packed[0][0],)], packed[1], packed[2], packed[3], name="adamw_small")
    shapes = [wl[n].shape for n in SMALL]
    for j in range(4):
        for n, a in zip(SMALL, _unpack_rows(small_out[j][0], shapes)):
            outs.setdefault(n, [None] * 4)[j] = a

    return (loss, grad_x[None], *[outs[n][0] for n in WEIGHTS], *[outs[n][1] for n in WEIGHTS],
            *[outs[n][2] for n in WEIGHTS], *[outs[n][3] for n in WEIGHTS])
```

```python
import functools

import jax
import jax.numpy as jnp
from jax import lax
from jax.experimental import pallas as pl
from jax.experimental.pallas import tpu as pltpu

F32 = jnp.float32
BF16 = jnp.bfloat16
NEG_INF = float("-inf")

D_MODEL = 2048
A_HEADS, A_KV_HEADS, A_HEAD_DIM, WINDOW = 16, 2, 64, 128
B_HEADS, B_HEAD_DIM, CONV_K, DN_CHUNK = 8, 128, 4, 64
C_GROUPS, C_CHUNK = 8, 128
C_GROUP_DIM = D_MODEL // C_GROUPS
D_FF = 5632
EPS = 1e-6
A_Q = A_HEADS * A_HEAD_DIM
A_KV = A_KV_HEADS * A_HEAD_DIM
B_W = B_HEADS * B_HEAD_DIM
EVEN_IN = A_Q + 2 * A_KV + 4 * B_W + 2 * B_HEADS
EVEN_IN_PAD = 5632
COL_KV = A_Q
COL_QKVB = A_Q + 2 * A_KV
COL_Z = COL_QKVB + 3 * B_W
COL_GATE = COL_Z + B_W
N_SHARD = 4

ADAM_LR, ADAM_B1, ADAM_B2, ADAM_EPS, ADAM_WD, ADAM_STEP = 0.001, 0.9, 0.999, 1e-08, 0.01, 10

VMEM_LIMIT_V7X = 56 * 1024 * 1024
MESH_ID = pl.DeviceIdType.MESH


def _params(sem=None):
    return pltpu.CompilerParams(dimension_semantics=sem, vmem_limit_bytes=VMEM_LIMIT_V7X)


def _sigmoid(x):
    return 1.0 / (1.0 + jnp.exp(-x))


def _silu(x):
    return x * _sigmoid(x)


def _dsilu(x):
    s = _sigmoid(x)
    return s * (1.0 + x * (1.0 - s))


def _gelu(x):
    return 0.5 * x * (1.0 + lax.erf(x * 0.7071067811865476))


def _dgelu(x):
    return 0.5 * (1.0 + lax.erf(x * 0.7071067811865476)) + x * jnp.exp(-0.5 * x * x) * 0.3989422804014327


def _dot(a, b, dims):
    if a.ndim == 3:
        (ca,), (cb,) = dims
        return lax.dot_general(a, b, (((ca + 1,), (cb + 1,)), ((0,), (0,))), preferred_element_type=F32)
    return lax.dot_general(a, b, (dims, ((), ())), preferred_element_type=F32)


NN = ((1,), (0,))
NT = ((1,), (1,))
TN = ((0,), (0,))


def _as3(b):
    return b if b.ndim == 3 else b[None]


def _accumulate(step, nsteps, accs, products, finish):
    if nsteps == 1:
        finish(products())
        return

    @pl.when(step == 0)
    def _():
        for acc, p in zip(accs, products()):
            acc[...] = p

    if nsteps > 2:
        @pl.when((step > 0) & (step < nsteps - 1))
        def _():
            for acc, p in zip(accs, products()):
                acc[...] += p

    @pl.when(step == nsteps - 1)
    def _():
        finish(tuple(acc[...] + p for acc, p in zip(accs, products())))


def mm_nn(a, b, *, tm, tn, tk, out_dtype, name, res=None, act=None):
    b3 = _as3(b)
    m, k = a.shape
    s, k2, ns = b3.shape
    assert k2 == k and m % tm == 0 and ns % tn == 0 and k % tk == 0, (a.shape, b3.shape, tm, tn, tk)
    nps, nk = ns // tn, k // tk

    def body(*refs):
        if res is None:
            a_ref, b_ref, o_ref, acc = refs
        else:
            a_ref, b_ref, r_ref, o_ref, acc = refs
        def finish(tiles):
            r = tiles[0] if res is None else tiles[0] + r_ref[...].astype(F32)
            o_ref[...] = r.astype(out_dtype)

        _accumulate(pl.program_id(2), nk, (acc,),
                    lambda: (_dot(a_ref[...].astype(BF16), b_ref[...].astype(BF16), NN),), finish)

    in_specs = [pl.BlockSpec((tm, tk), lambda i, j, kk: (i, kk)),
                pl.BlockSpec((None, tk, tn), lambda i, j, kk: (j // nps, kk, j % nps))]
    args = [a, b3]
    if res is not None:
        in_specs.append(pl.BlockSpec((tm, tn), lambda i, j, kk: (i, j)))
        args.append(res)
    return pl.pallas_call(
        body, name=name, grid=(m // tm, s * nps, nk), in_specs=in_specs,
        out_specs=pl.BlockSpec((tm, tn), lambda i, j, kk: (i, j)),
        out_shape=jax.ShapeDtypeStruct((m, s * ns), out_dtype),
        scratch_shapes=[pltpu.VMEM((tm, tn), F32)],
        compiler_params=_params(("parallel", "parallel", "arbitrary")))(*args)


def mm_nt(a, b, *, tm, tn, tk, out_dtype, name, res=None):
    b3 = _as3(b)
    m, n = a.shape
    s, k, ns = b3.shape
    assert n == s * ns and m % tm == 0 and k % tn == 0 and ns % tk == 0, (a.shape, b3.shape, tm, tn, tk)
    rps = ns // tk
    nr = s * rps

    def body(*refs):
        if res is None:
            a_ref, b_ref, o_ref, acc = refs
        else:
            a_ref, b_ref, r_ref, o_ref, acc = refs
        def finish(tiles):
            r = tiles[0] if res is None else tiles[0] + r_ref[...].astype(F32)
            o_ref[...] = r.astype(out_dtype)

        _accumulate(pl.program_id(2), nr, (acc,),
                    lambda: (_dot(a_ref[...].astype(BF16), b_ref[...].astype(BF16), NT),), finish)

    in_specs = [pl.BlockSpec((tm, tk), lambda i, j, r: (i, r)),
                pl.BlockSpec((None, tn, tk), lambda i, j, r: (r // rps, j, r % rps))]
    args = [a, b3]
    if res is not None:
        in_specs.append(pl.BlockSpec((tm, tn), lambda i, j, r: (i, j)))
        args.append(res)
    return pl.pallas_call(
        body, name=name, grid=(m // tm, k // tn, nr), in_specs=in_specs,
        out_specs=pl.BlockSpec((tm, tn), lambda i, j, r: (i, j)),
        out_shape=jax.ShapeDtypeStruct((m, k), out_dtype),
        scratch_shapes=[pltpu.VMEM((tm, tn), F32)],
        compiler_params=_params(("parallel", "parallel", "arbitrary")))(*args)


def mm_tn(a, b, *, shards, tm, tn, tk, out_dtype, name):
    m, k = a.shape
    m2, n = b.shape
    ns = n // shards
    assert m2 == m and n == shards * ns and m % tm == 0 and k % tk == 0 and ns % tn == 0, (a.shape, b.shape)
    nps, nm = ns // tn, m // tm

    def body(a_ref, b_ref, o_ref, acc):
        def finish(tiles):
            o_ref[...] = tiles[0].astype(out_dtype)

        _accumulate(pl.program_id(2), nm, (acc,),
                    lambda: (_dot(a_ref[...].astype(BF16), b_ref[...].astype(BF16), TN),), finish)

    return pl.pallas_call(
        body, name=name, grid=(k // tk, shards * nps, nm),
        in_specs=[pl.BlockSpec((tm, tk), lambda i, j, mi: (mi, i)),
                  pl.BlockSpec((tm, tn), lambda i, j, mi: (mi, j))],
        out_specs=pl.BlockSpec((None, tk, tn), lambda i, j, mi: (j // nps, i, j % nps)),
        out_shape=jax.ShapeDtypeStruct((shards, k, ns), out_dtype),
        scratch_shapes=[pltpu.VMEM((tk, tn), F32)],
        compiler_params=_params(("parallel", "parallel", "arbitrary")))(a, b)


def mm_gate_up(hn, wg, wu, *, tm, tn, tk, name):
    wg3, wu3 = _as3(wg), _as3(wu)
    m, k = hn.shape
    s, _, ns = wg3.shape
    assert m % tm == 0 and ns % tn == 0 and k % tk == 0
    nps, nk = ns // tn, k // tk

    def body(a_ref, g_ref, u_ref, og_ref, ou_ref, oa_ref, accg, accu):
        def products():
            a = a_ref[...].astype(BF16)
            return _dot(a, g_ref[...].astype(BF16), NN), _dot(a, u_ref[...].astype(BF16), NN)

        def finish(tiles):
            g, u = tiles
            og_ref[...] = g.astype(BF16)
            ou_ref[...] = u.astype(BF16)
            oa_ref[...] = (_silu(g) * u).astype(BF16)

        _accumulate(pl.program_id(2), nk, (accg, accu), products, finish)

    wspec = pl.BlockSpec((None, tk, tn), lambda i, j, kk: (j // nps, kk, j % nps))
    ospec = pl.BlockSpec((tm, tn), lambda i, j, kk: (i, j))
    osh = jax.ShapeDtypeStruct((m, s * ns), BF16)
    return pl.pallas_call(
        body, name=name, grid=(m // tm, s * nps, nk),
        in_specs=[pl.BlockSpec((tm, tk), lambda i, j, kk: (i, kk)), wspec, wspec],
        out_specs=[ospec, ospec, ospec], out_shape=[osh, osh, osh],
        scratch_shapes=[pltpu.VMEM((tm, tn) if nk > 1 else (8, 128), F32)] * 2,
        compiler_params=_params(("parallel", "parallel", "arbitrary")))(hn, wg3, wu3)


def mm_down_bwd(dh, wd, gate, up, *, tm, tn, tk, name):
    m, d = dh.shape
    f, d2 = wd.shape
    assert d2 == d and m % tm == 0 and f % tn == 0 and d % tk == 0
    nr = d // tk

    def body(a_ref, b_ref, g_ref, u_ref, og_ref, ou_ref, acc):
        def finish(tiles):
            da = tiles[0]
            g, u = g_ref[...].astype(F32), u_ref[...].astype(F32)
            og_ref[...] = (da * u * _dsilu(g)).astype(BF16)
            ou_ref[...] = (da * _silu(g)).astype(BF16)

        _accumulate(pl.program_id(2), nr, (acc,),
                    lambda: (_dot(a_ref[...].astype(BF16), b_ref[...].astype(BF16), NT),), finish)

    ospec = pl.BlockSpec((tm, tn), lambda i, j, r: (i, j))
    osh = jax.ShapeDtypeStruct((m, f), BF16)
    return pl.pallas_call(
        body, name=name, grid=(m // tm, f // tn, nr),
        in_specs=[pl.BlockSpec((tm, tk), lambda i, j, r: (i, r)),
                  pl.BlockSpec((tn, tk), lambda i, j, r: (j, r)), ospec, ospec],
        out_specs=[ospec, ospec], out_shape=[osh, osh],
        scratch_shapes=[pltpu.VMEM((tm, tn), F32)],
        compiler_params=_params(("parallel", "parallel", "arbitrary")))(dh, wd, gate, up)


ROWS = 256


def rms_fwd(x, g, *, name):
    t, d = x.shape

    def body(x_ref, g_ref, o_ref):
        xv = x_ref[...]
        r = lax.rsqrt(jnp.mean(xv * xv, axis=-1, keepdims=True) + EPS)
        o_ref[...] = (xv * r * g_ref[...]).astype(BF16)

    return pl.pallas_call(
        body, name=name, grid=(t // ROWS,),
        in_specs=[pl.BlockSpec((ROWS, d), lambda i: (i, 0)), pl.BlockSpec((1, d), lambda i: (0, 0))],
        out_specs=pl.BlockSpec((ROWS, d), lambda i: (i, 0)),
        out_shape=jax.ShapeDtypeStruct((t, d), BF16), compiler_params=_params(("parallel",)))(x, g)


def rms_bwd(x, g, dy, dres, *, name):
    t, d = x.shape

    def body(x_ref, g_ref, dy_ref, dr_ref, dx_ref, dg_ref):
        @pl.when(pl.program_id(0) == 0)
        def _():
            dg_ref[...] = jnp.zeros_like(dg_ref)

        xv, dyv = x_ref[...], dy_ref[...].astype(F32)
        r = lax.rsqrt(jnp.mean(xv * xv, axis=-1, keepdims=True) + EPS)
        dyg = dyv * g_ref[...]
        dx = r * dyg - xv * (r * r * r) * jnp.mean(dyg * xv, axis=-1, keepdims=True)
        dx_ref[...] = dx + dr_ref[...]
        dg_ref[...] += jnp.sum(dyv * xv * r, axis=0, keepdims=True)

    row = pl.BlockSpec((ROWS, d), lambda i: (i, 0))
    vec = pl.BlockSpec((1, d), lambda i: (0, 0))
    return pl.pallas_call(
        body, name=name, grid=(t // ROWS,), in_specs=[row, vec, row, row], out_specs=[row, vec],
        out_shape=[jax.ShapeDtypeStruct((t, d), F32), jax.ShapeDtypeStruct((1, d), F32)],
        compiler_params=_params(("arbitrary",)))(x, g, dy, dres)


def loss_head(h, g, target, *, name):
    t, d = h.shape

    def body(x_ref, g_ref, t_ref, loss_ref, dx_ref, dg_ref):
        @pl.when(pl.program_id(0) == 0)
        def _():
            dg_ref[...] = jnp.zeros_like(dg_ref)
            loss_ref[...] = jnp.zeros_like(loss_ref)

        xv, gv = x_ref[...], g_ref[...]
        r = lax.rsqrt(jnp.mean(xv * xv, axis=-1, keepdims=True) + EPS)
        e = xv * r * gv - t_ref[...]
        loss_ref[...] += 0.5 * jnp.sum(jnp.mean(e * e, axis=-1, keepdims=True), axis=0, keepdims=True)
        dyv = e * (1.0 / d)
        dyg = dyv * gv
        dx_ref[...] = r * dyg - xv * (r * r * r) * jnp.mean(dyg * xv, axis=-1, keepdims=True)
        dg_ref[...] += jnp.sum(dyv * xv * r, axis=0, keepdims=True)

    row = pl.BlockSpec((ROWS, d), lambda i: (i, 0))
    vec = pl.BlockSpec((1, d), lambda i: (0, 0))
    return pl.pallas_call(
        body, name=name, grid=(t // ROWS,), in_specs=[row, vec, row],
        out_specs=[pl.BlockSpec((1, 128), lambda i: (0, 0)), row, vec],
        out_shape=[jax.ShapeDtypeStruct((1, 128), F32), jax.ShapeDtypeStruct((t, d), F32),
                   jax.ShapeDtypeStruct((1, d), F32)],
        compiler_params=_params(("arbitrary",)))(h, g, target)


def _tril_mask():
    r = lax.broadcasted_iota(jnp.int32, (C_CHUNK, C_CHUNK), 0)
    c = lax.broadcasted_iota(jnp.int32, (C_CHUNK, C_CHUNK), 1)
    return r >= c


def _layer_norm_parts(v):
    mu = jnp.mean(v, axis=-1, keepdims=True)
    vc = v - mu
    rstd = lax.rsqrt(jnp.mean(vc * vc, axis=-1, keepdims=True) + EPS)
    return vc * rstd, rstd


def gmlp_fwd(zpre, ln_g, ln_b, ws, bs_t, *, name):
    t = zpre.shape[0]
    d = D_MODEL

    def body(zu_ref, zv_ref, g_ref, b_ref, ws_ref, bs_ref, o_ref):
        u = _gelu(zu_ref[...])
        vhat, _ = _layer_norm_parts(_gelu(zv_ref[...]))
        vln = (vhat * g_ref[...] + b_ref[...]).astype(BF16)
        mask = _tril_mask()
        for gi in range(C_GROUPS):
            sl = slice(gi * C_GROUP_DIM, (gi + 1) * C_GROUP_DIM)
            w = jnp.where(mask, ws_ref[gi], 0.0).astype(BF16)
            mixed = _dot(w, vln[:, sl], NN) + bs_ref[:, gi:gi + 1]
            o_ref[:, sl] = (u[:, sl] * mixed).astype(BF16)

    vec = pl.BlockSpec((1, d), lambda i: (0, 0))
    return pl.pallas_call(
        body, name=name, grid=(t // C_CHUNK,),
        in_specs=[pl.BlockSpec((C_CHUNK, d), lambda i: (i, 0)), pl.BlockSpec((C_CHUNK, d), lambda i: (i, 1)),
                  vec, vec, pl.BlockSpec((C_GROUPS, C_CHUNK, C_CHUNK), lambda i: (0, 0, 0)),
                  pl.BlockSpec((C_CHUNK, 128), lambda i: (0, 0))],
        out_specs=pl.BlockSpec((C_CHUNK, d), lambda i: (i, 0)),
        out_shape=jax.ShapeDtypeStruct((t, d), BF16), compiler_params=_params(("parallel",)))(
            zpre, zpre, ln_g, ln_b, ws, bs_t)


def gmlp_bwd(zpre, dgated, ln_g, ln_b, ws, bs_t, *, name):
    t = zpre.shape[0]
    d = D_MODEL

    def body(zu_ref, zv_ref, dg_ref, g_ref, b_ref, ws_ref, bs_ref, dz_ref, dws_ref, dbs_ref, dlg_ref, dlb_ref):
        @pl.when(pl.program_id(0) == 0)
        def _():
            dws_ref[...] = jnp.zeros_like(dws_ref)
            dbs_ref[...] = jnp.zeros_like(dbs_ref)
            dlg_ref[...] = jnp.zeros_like(dlg_ref)
            dlb_ref[...] = jnp.zeros_like(dlb_ref)

        zu, zv = zu_ref[...], zv_ref[...]
        u = _gelu(zu)
        vhat, rstd = _layer_norm_parts(_gelu(zv))
        gam = g_ref[...]
        vln = (vhat * gam + b_ref[...]).astype(BF16)
        dgt = dg_ref[...].astype(F32)
        mask = _tril_mask()
        lane = lax.broadcasted_iota(jnp.int32, (C_CHUNK, 128), 1)
        dbs = jnp.zeros((C_CHUNK, 128), F32)
        du_parts, dvln_parts = [], []
        for gi in range(C_GROUPS):
            sl = slice(gi * C_GROUP_DIM, (gi + 1) * C_GROUP_DIM)
            w = jnp.where(mask, ws_ref[gi], 0.0).astype(BF16)
            mixed = _dot(w, vln[:, sl], NN) + bs_ref[:, gi:gi + 1]
            du_parts.append(dgt[:, sl] * mixed)
            dmixed = dgt[:, sl] * u[:, sl]
            dmb = dmixed.astype(BF16)
            dws_ref[gi] += jnp.where(mask, _dot(dmb, vln[:, sl], NT), 0.0)
            dbs = dbs + jnp.where(lane == gi, jnp.sum(dmixed, axis=-1, keepdims=True), 0.0)
            dvln_parts.append(_dot(w, dmb, TN))
        dbs_ref[...] += dbs
        du = jnp.concatenate(du_parts, axis=-1)
        dvln = jnp.concatenate(dvln_parts, axis=-1)
        dlg_ref[...] += jnp.sum(dvln * vhat, axis=0, keepdims=True)
        dlb_ref[...] += jnp.sum(dvln, axis=0, keepdims=True)
        dvhat = dvln * gam
        dv = rstd * (dvhat - jnp.mean(dvhat, axis=-1, keepdims=True)
                     - vhat * jnp.mean(dvhat * vhat, axis=-1, keepdims=True))
        dz_ref[:, :d] = (du * _dgelu(zu)).astype(BF16)
        dz_ref[:, d:] = (dv * _dgelu(zv)).astype(BF16)

    vec = pl.BlockSpec((1, d), lambda i: (0, 0))
    wsp = pl.BlockSpec((C_GROUPS, C_CHUNK, C_CHUNK), lambda i: (0, 0, 0))
    bsp = pl.BlockSpec((C_CHUNK, 128), lambda i: (0, 0))
    return pl.pallas_call(
        body, name=name, grid=(t // C_CHUNK,),
        in_specs=[pl.BlockSpec((C_CHUNK, d), lambda i: (i, 0)), pl.BlockSpec((C_CHUNK, d), lambda i: (i, 1)),
                  pl.BlockSpec((C_CHUNK, d), lambda i: (i, 0)), vec, vec, wsp, bsp],
        out_specs=[pl.BlockSpec((C_CHUNK, 2 * d), lambda i: (i, 0)), wsp, bsp, vec, vec],
        out_shape=[jax.ShapeDtypeStruct((t, 2 * d), BF16), jax.ShapeDtypeStruct((C_GROUPS, C_CHUNK, C_CHUNK), F32),
                   jax.ShapeDtypeStruct((C_CHUNK, 128), F32), jax.ShapeDtypeStruct((1, d), F32),
                   jax.ShapeDtypeStruct((1, d), F32)],
        compiler_params=_params(("arbitrary",)))(zpre, zpre, dgated, ln_g, ln_b, ws, bs_t)


ATT_SCALE = A_HEAD_DIM ** -0.5
PAIRS = A_HEADS // 2
PAIRS_PER_KV = PAIRS // A_KV_HEADS


def _att_padded(tile):
    lo = lax.broadcasted_iota(jnp.int32, tile.shape, 1) < A_HEAD_DIM
    rolled = pltpu.roll(tile, A_HEAD_DIM, 1)
    zero = jnp.zeros_like(tile)
    return {(0, 0): jnp.where(lo, tile, zero).astype(BF16), (0, 1): jnp.where(lo, zero, rolled).astype(BF16),
            (1, 0): jnp.where(lo, rolled, zero).astype(BF16), (1, 1): jnp.where(lo, zero, tile).astype(BF16)}


def _att_valid(n):
    r = lax.broadcasted_iota(jnp.int32, (WINDOW, 2 * WINDOW), 0)
    c = lax.broadcasted_iota(jnp.int32, (WINDOW, 2 * WINDOW), 1)
    rel = r + WINDOW - c
    return (rel >= 0) & (rel < WINDOW) & ((c >= WINDOW) | (n > 0))


def _att_probs(qp, kpad, sink, valid):
    s = jnp.where(valid, _dot(qp, kpad, NT), NEG_INF)
    m = jnp.maximum(jnp.max(s, axis=-1, keepdims=True), sink)
    p = jnp.exp(s - m)
    e_sink = jnp.exp(sink - m)
    inv = 1.0 / (jnp.sum(p, axis=-1, keepdims=True) + e_sink)
    return p * inv, e_sink * inv


def _att_operands(q_ref, kvc_ref, kvp_ref, s_ref):
    kv = jnp.concatenate([kvp_ref[...], kvc_ref[...]], axis=0)
    kpad, vpad = _att_padded(kv[:, :128]), _att_padded(kv[:, 128:])
    key = lambda h: ((h // 2) // PAIRS_PER_KV, h % 2)
    pairs = [(q_ref[:, j * 128:(j + 1) * 128] * ATT_SCALE).astype(BF16) for j in range(PAIRS)]
    q = jnp.stack([pairs[h // 2] for h in range(A_HEADS)])
    k = jnp.stack([kpad[key(h)] for h in range(A_HEADS)])
    v = jnp.stack([vpad[key(h)] for h in range(A_HEADS)])
    sink = jnp.stack([s_ref[:, h:h + 1] for h in range(A_HEADS)])
    return q, k, v, sink


def _att_specs(t):
    return [pl.BlockSpec((WINDOW, A_Q), lambda n: (n, 0)),
            pl.BlockSpec((WINDOW, 2 * A_KV), lambda n: (n, COL_KV // (2 * A_KV))),
            pl.BlockSpec((WINDOW, 2 * A_KV), lambda n: (jnp.maximum(n - 1, 0), COL_KV // (2 * A_KV))),
            pl.BlockSpec((1, 128), lambda n: (0, 0))]


def att_fwd(proj, sinks, *, name):
    t = proj.shape[0]

    def body(q_ref, kvc_ref, kvp_ref, s_ref, o_ref):
        n = pl.program_id(0)
        q, k, v, sink = _att_operands(q_ref, kvc_ref, kvp_ref, s_ref)
        w, _ = _att_probs(q, k, sink, _att_valid(n))
        o = _dot(w.astype(BF16), v, NN)
        for j in range(PAIRS):
            o_ref[:, j * 128:(j + 1) * 128] = (o[2 * j] + o[2 * j + 1]).astype(BF16)

    return pl.pallas_call(
        body, name=name, grid=(t // WINDOW,), in_specs=_att_specs(t),
        out_specs=pl.BlockSpec((WINDOW, A_Q), lambda n: (n, 0)),
        out_shape=jax.ShapeDtypeStruct((t, A_Q), BF16), compiler_params=_params(("parallel",)))(
            proj, proj, proj, sinks)


def att_bwd(proj, sinks, dout, *, name):
    t = proj.shape[0]

    def body(q_ref, kvc_ref, kvp_ref, s_ref, do_ref, dq_ref, dkc_ref, dkp_ref, ds_ref):
        n = pl.program_id(0)

        @pl.when(n == 0)
        def _():
            ds_ref[...] = jnp.zeros_like(ds_ref)

        q, k, v, sink = _att_operands(q_ref, kvc_ref, kvp_ref, s_ref)
        dop = jnp.stack([do_ref[:, (h // 2) * 128:(h // 2 + 1) * 128] for h in range(A_HEADS)]).astype(BF16)
        w, w_sink = _att_probs(q, k, sink, _att_valid(n))
        dw = _dot(dop, v, NT)
        delta = jnp.sum(w * dw, axis=-1, keepdims=True)
        dsc = (w * (dw - delta)).astype(BF16)
        dsink_h = -jnp.sum(w_sink * delta, axis=1, keepdims=True)
        dq = _dot(dsc, k, NN)
        dk_h = _dot(dsc, q, TN)
        dv_h = _dot(w.astype(BF16), dop, TN)
        lane = lax.broadcasted_iota(jnp.int32, (1, 128), 1)
        dsink = jnp.zeros((1, 128), F32)
        for h in range(A_HEADS):
            dsink = dsink + jnp.where(lane == h, dsink_h[h], 0.0)
        ds_ref[...] += dsink
        for j in range(PAIRS):
            dq_ref[:, j * 128:(j + 1) * 128] = ((dq[2 * j] + dq[2 * j + 1]) * ATT_SCALE).astype(BF16)
        lo = lax.broadcasted_iota(jnp.int32, (2 * WINDOW, 128), 1) < A_HEAD_DIM
        heads_per_kv = A_HEADS // A_KV_HEADS

        def tile(per_head):
            acc = {}
            for kvh in range(A_KV_HEADS):
                for half in range(2):
                    hs = range(kvh * heads_per_kv + half, (kvh + 1) * heads_per_kv, 2)
                    acc[(kvh, half)] = functools.reduce(lambda a, b: a + b, [per_head[h] for h in hs])
            return jnp.where(lo, acc[(0, 0)] + pltpu.roll(acc[(0, 1)], A_HEAD_DIM, 1),
                             pltpu.roll(acc[(1, 0)], A_HEAD_DIM, 1) + acc[(1, 1)])

        dkv = jnp.concatenate([tile(dk_h), tile(dv_h)], axis=1)
        dkp_ref[...] = dkv[:WINDOW]
        dkc_ref[...] = dkv[WINDOW:]

    kvo = pl.BlockSpec((WINDOW, 2 * A_KV), lambda n: (n, 0))
    return pl.pallas_call(
        body, name=name, grid=(t // WINDOW,),
        in_specs=_att_specs(t) + [pl.BlockSpec((WINDOW, A_Q), lambda n: (n, 0))],
        out_specs=[pl.BlockSpec((WINDOW, A_Q), lambda n: (n, 0)), kvo, kvo, pl.BlockSpec((1, 128), lambda n: (0, 0))],
        out_shape=[jax.ShapeDtypeStruct((t, A_Q), BF16), jax.ShapeDtypeStruct((t, 2 * A_KV), F32),
                   jax.ShapeDtypeStruct((t, 2 * A_KV), F32), jax.ShapeDtypeStruct((1, 128), F32)],
        compiler_params=_params(("arbitrary",)))(proj, proj, proj, sinks, dout)


QK_SCALE = B_HEAD_DIM ** -0.5
PREP_COLS = 256
PREP_NCB = 3 * B_W // PREP_COLS
HALO = 8


def _roll_rows(x, shift):
    n = x.shape[0]
    return x if shift % n == 0 else pltpu.roll(x, shift % n, 0)


def _conv_taps(xe, w):
    xs = [_roll_rows(xe, CONV_K - 1 - i) for i in range(CONV_K)]
    c = w[0:1] * xs[0]
    for i in range(1, CONV_K):
        c = c + w[i:i + 1] * xs[i]
    return xs, c


def dprep_fwd(proj, conv_w, *, name):
    t = proj.shape[0]
    tt = ROWS
    col0 = COL_QKVB // PREP_COLS

    def body(x_ref, h_ref, w_ref, o_ref):
        cb, n = pl.program_id(0), pl.program_id(1)
        halo = jnp.where(n > 0, h_ref[...], 0.0)
        xe = jnp.concatenate([halo, x_ref[...]], axis=0)
        _, c = _conv_taps(xe, w_ref[...])
        y = _silu(c)[HALO:]
        parts = []
        for hh in range(PREP_COLS // B_HEAD_DIM):
            yh = y[:, hh * B_HEAD_DIM:(hh + 1) * B_HEAD_DIM]
            parts.append(yh * lax.rsqrt(jnp.sum(yh * yh, axis=-1, keepdims=True) + EPS))
        nrm = jnp.concatenate(parts, axis=-1)
        o_ref[...] = jnp.where(cb < 4, nrm * QK_SCALE, jnp.where(cb < 8, nrm, y))

    return pl.pallas_call(
        body, name=name, grid=(PREP_NCB, t // tt),
        in_specs=[pl.BlockSpec((tt, PREP_COLS), lambda cb, n: (n, col0 + cb)),
                  pl.BlockSpec((HALO, PREP_COLS), lambda cb, n: (jnp.maximum(n * (tt // HALO) - 1, 0), col0 + cb)),
                  pl.BlockSpec((CONV_K, PREP_COLS), lambda cb, n: (0, cb))],
        out_specs=pl.BlockSpec((tt, PREP_COLS), lambda cb, n: (n, cb)),
        out_shape=jax.ShapeDtypeStruct((t, 3 * B_W), F32), compiler_params=_params(("parallel", "parallel")))(
            proj, proj, conv_w)


def dprep_bwd(proj, conv_w, dqkvn, *, name):
    t = proj.shape[0]
    tt = ROWS
    nb = t // tt
    col0 = COL_QKVB // PREP_COLS
    n8 = t // HALO

    def body(xc_ref, xb_ref, xa_ref, dc_ref, da_ref, w_ref, dx_ref, dw_ref):
        cb, n = pl.program_id(0), pl.program_id(1)

        @pl.when(n == 0)
        def _():
            dw_ref[...] = jnp.zeros_like(dw_ref)

        w = w_ref[...]
        xe = jnp.concatenate([jnp.where(n > 0, xb_ref[...], 0.0), xc_ref[...], xa_ref[...]], axis=0)
        xs, c = _conv_taps(xe, w)
        sg = _sigmoid(c)
        y = c * sg
        dout = jnp.concatenate([jnp.zeros((HALO, PREP_COLS), F32), dc_ref[...],
                                jnp.where(n < nb - 1, da_ref[...], 0.0)], axis=0)
        dsc = jnp.where(cb < 4, QK_SCALE, 1.0)
        parts = []
        for hh in range(PREP_COLS // B_HEAD_DIM):
            sl = slice(hh * B_HEAD_DIM, (hh + 1) * B_HEAD_DIM)
            yh, doh = y[:, sl], dout[:, sl] * dsc
            r = lax.rsqrt(jnp.sum(yh * yh, axis=-1, keepdims=True) + EPS)
            parts.append(doh * r - yh * (r * r * r) * jnp.sum(doh * yh, axis=-1, keepdims=True))
        dy = jnp.where(cb < 8, jnp.concatenate(parts, axis=-1), dout)
        dcv = dy * sg * (1.0 + c * (1.0 - sg))
        dxe = w[CONV_K - 1:CONV_K] * dcv
        for i in range(CONV_K - 1):
            dxe = dxe + w[i:i + 1] * _roll_rows(dcv, -(CONV_K - 1 - i))
        dx_ref[...] = dxe[HALO:HALO + tt].astype(BF16)
        for i in range(CONV_K):
            dw_ref[i:i + 1, :] += jnp.sum((dcv * xs[i])[HALO:HALO + tt], axis=0, keepdims=True)

    def after(n):
        return jnp.minimum((n + 1) * (tt // HALO), n8 - 1)

    return pl.pallas_call(
        body, name=name, grid=(PREP_NCB, nb),
        in_specs=[pl.BlockSpec((tt, PREP_COLS), lambda cb, n: (n, col0 + cb)),
                  pl.BlockSpec((HALO, PREP_COLS), lambda cb, n: (jnp.maximum(n * (tt // HALO) - 1, 0), col0 + cb)),
                  pl.BlockSpec((HALO, PREP_COLS), lambda cb, n: (after(n), col0 + cb)),
                  pl.BlockSpec((tt, PREP_COLS), lambda cb, n: (n, cb)),
                  pl.BlockSpec((HALO, PREP_COLS), lambda cb, n: (after(n), cb)),
                  pl.BlockSpec((CONV_K, PREP_COLS), lambda cb, n: (0, cb))],
        out_specs=[pl.BlockSpec((tt, PREP_COLS), lambda cb, n: (n, cb)),
                   pl.BlockSpec((CONV_K, PREP_COLS), lambda cb, n: (0, cb))],
        out_shape=[jax.ShapeDtypeStruct((t, 3 * B_W), BF16), jax.ShapeDtypeStruct((CONV_K, 3 * B_W), F32)],
        compiler_params=_params(("parallel", "arbitrary")))(proj, proj, proj, dqkvn, dqkvn, conv_w)


def _softplus(z):
    return jnp.maximum(z, 0.0) + jnp.log(1.0 + jnp.exp(-jnp.abs(z)))


def gates_fwd(proj, alog_pad, dtb_pad, *, name):
    t = proj.shape[0]

    def body(x_ref, a_ref, b_ref, o_ref):
        raw = x_ref[...]
        lane = lax.broadcasted_iota(jnp.int32, raw.shape, 1)
        g = -jnp.exp(a_ref[...]) * _softplus(raw + b_ref[...])
        o_ref[...] = jnp.where(lane < B_HEADS, _sigmoid(raw), jnp.where(lane < 2 * B_HEADS, g, 0.0))

    vec = pl.BlockSpec((1, 128), lambda n: (0, 0))
    return pl.pallas_call(
        body, name=name, grid=(t // ROWS,),
        in_specs=[pl.BlockSpec((ROWS, 128), lambda n: (n, COL_GATE // 128)), vec, vec],
        out_specs=pl.BlockSpec((ROWS, 128), lambda n: (n, 0)),
        out_shape=jax.ShapeDtypeStruct((t, 128), F32), compiler_params=_params(("parallel",)))(
            proj, alog_pad, dtb_pad)


def gates_bwd(proj, alog_pad, dtb_pad, dgates, *, name):
    t = proj.shape[0]

    def body(x_ref, a_ref, b_ref, dg_ref, dx_ref, da_ref, db_ref):
        @pl.when(pl.program_id(0) == 0)
        def _():
            da_ref[...] = jnp.zeros_like(da_ref)
            db_ref[...] = jnp.zeros_like(db_ref)

        raw, dgt = x_ref[...], dg_ref[...]
        lane = lax.broadcasted_iota(jnp.int32, raw.shape, 1)
        is_beta, is_g = lane < B_HEADS, (lane >= B_HEADS) & (lane < 2 * B_HEADS)
        beta = _sigmoid(raw)
        z = raw + b_ref[...]
        neg_a = -jnp.exp(a_ref[...])
        d_z = jnp.where(is_g, dgt * neg_a * _sigmoid(z), 0.0)
        dx_ref[...] = jnp.where(is_beta, dgt * beta * (1.0 - beta), d_z).astype(BF16)
        db_ref[...] += jnp.sum(d_z, axis=0, keepdims=True)
        da_ref[...] += jnp.sum(jnp.where(is_g, dgt * neg_a * _softplus(z), 0.0), axis=0, keepdims=True)

    vec = pl.BlockSpec((1, 128), lambda n: (0, 0))
    row = pl.BlockSpec((ROWS, 128), lambda n: (n, 0))
    return pl.pallas_call(
        body, name=name, grid=(t // ROWS,),
        in_specs=[pl.BlockSpec((ROWS, 128), lambda n: (n, COL_GATE // 128)), vec, vec, row],
        out_specs=[row, vec, vec],
        out_shape=[jax.ShapeDtypeStruct((t, 128), BF16), jax.ShapeDtypeStruct((1, 128), F32),
                   jax.ShapeDtypeStruct((1, 128), F32)],
        compiler_params=_params(("arbitrary",)))(proj, alog_pad, dtb_pad, dgates)


def _split2(a):
    hi = a.astype(BF16)
    return hi, (a - hi.astype(F32)).astype(BF16)


def _dotp(a, b, dims, passes):
    if passes == 1:
        return _dot(a.astype(BF16), b.astype(BF16), dims)
    ah, al = _split2(a)
    bh, bl = _split2(b)
    return _dot(ah, bh, dims) + (_dot(ah, bl, dims) + _dot(al, bh, dims))


_GRAD_DIMS = {NN: ((NT, False), (TN, False)), NT: ((NN, False), (TN, True)), TN: ((NT, True), (NN, False))}


def _make_mm(dims, passes):
    (da_dims, da_swap), (db_dims, db_swap) = _GRAD_DIMS[dims]

    @jax.custom_vjp
    def mm(a, b):
        return _dotp(a, b, dims, passes)

    def fwd(a, b):
        return _dotp(a, b, dims, passes), (a, b)

    def bwd(saved, ct):
        a, b = saved
        da = _dotp(b, ct, da_dims, passes) if da_swap else _dotp(ct, b, da_dims, passes)
        db = _dotp(ct, a, db_dims, passes) if db_swap else _dotp(a, ct, db_dims, passes)
        return da, db

    mm.defvjp(fwd, bwd)
    return mm


MM1 = {d: _make_mm(d, 1) for d in (NN, NT, TN)}
MM3 = {d: _make_mm(d, 3) for d in (NN, NT, TN)}


def _tri_ones(lower):
    r = lax.broadcasted_iota(jnp.int32, (DN_CHUNK, DN_CHUNK), 0)
    c = lax.broadcasted_iota(jnp.int32, (DN_CHUNK, DN_CHUNK), 1)
    return (r >= c if lower else r <= c).astype(BF16)


def _tri_sum(x, lower):
    tri = _tri_ones(lower)
    hi = x.astype(BF16)
    r1 = x - hi.astype(F32)
    mid = r1.astype(BF16)
    lo = (r1 - mid.astype(F32)).astype(BF16)
    return _dot(tri, hi, NN) + (_dot(tri, mid, NN) + _dot(tri, lo, NN))


def _delta_chunk(s0, q, k, v, beta, gam_c, gam_r):
    c = DN_CHUNK
    r = lax.broadcasted_iota(jnp.int32, (c, c), 0)
    cc = lax.broadcasted_iota(jnp.int32, (c, c), 1)
    incl, strict = r >= cc, r > cc
    eye = (r == cc).astype(F32)
    decay = jnp.exp(jnp.where(incl, gam_c - gam_r, NEG_INF))
    g_last = gam_c[:, c - 1:c, :]
    e_gam, e_rest, e_last = jnp.exp(gam_c), jnp.exp(g_last - gam_c), jnp.exp(g_last)
    a_neg = -jnp.where(strict, beta * MM1[NT](k, k) * decay, 0.0)
    inv = eye + a_neg
    pw = a_neg
    for _ in range(5):
        pw = MM3[NN](pw, pw)
        inv = inv + MM3[NN](inv, pw)
    uw = MM3[NN](inv, jnp.concatenate([v * beta, k * (beta * e_gam)], axis=-1))
    u, w = uw[..., :B_HEAD_DIM], uw[..., B_HEAD_DIM:]
    qk = MM1[NT](q, k) * decay
    v_new = u - MM1[NN](w, s0)
    o = MM1[NN](q * e_gam, s0) + MM1[NN](qk, v_new)
    s1 = s0 * e_last + MM1[TN](k * e_rest, v_new)
    return s1, o


def _delta_operands(q_ref, k_ref, v_ref, gt):
    heads = lambda ref: jnp.stack([ref[:, h * B_HEAD_DIM:(h + 1) * B_HEAD_DIM] for h in range(B_HEADS)])
    gam = _tri_sum(gt, True)
    gam_t = gam.T
    beta = jnp.stack([gt[:, h:h + 1] for h in range(B_HEADS)])
    gam_c = jnp.stack([gam[:, B_HEADS + h:B_HEADS + h + 1] for h in range(B_HEADS)])
    gam_r = jnp.stack([gam_t[B_HEADS + h:B_HEADS + h + 1, :] for h in range(B_HEADS)])
    return heads(q_ref), heads(k_ref), heads(v_ref), beta, gam_c, gam_r


def delta_fwd(qkvn, gates, *, name):
    t = qkvn.shape[0]
    nc = t // DN_CHUNK

    def body(q_ref, k_ref, v_ref, g_ref, o_ref, ss_ref, state):
        @pl.when(pl.program_id(0) == 0)
        def _():
            state[...] = jnp.zeros_like(state)

        s0 = state[...]
        ss_ref[...] = s0
        s1, o = _delta_chunk(s0, *_delta_operands(q_ref, k_ref, v_ref, g_ref[...]))
        state[...] = s1
        for h in range(B_HEADS):
            o_ref[:, h * B_HEAD_DIM:(h + 1) * B_HEAD_DIM] = o[h]

    blk = lambda j: pl.BlockSpec((DN_CHUNK, B_W), lambda n: (n, j))
    return pl.pallas_call(
        body, name=name, grid=(nc,),
        in_specs=[blk(0), blk(1), blk(2), pl.BlockSpec((DN_CHUNK, 128), lambda n: (n, 0))],
        out_specs=[blk(0), pl.BlockSpec((None, B_HEADS, B_HEAD_DIM, B_HEAD_DIM), lambda n: (n, 0, 0, 0))],
        out_shape=[jax.ShapeDtypeStruct((t, B_W), F32),
                   jax.ShapeDtypeStruct((nc, B_HEADS, B_HEAD_DIM, B_HEAD_DIM), F32)],
        scratch_shapes=[pltpu.VMEM((B_HEADS, B_HEAD_DIM, B_HEAD_DIM), F32)],
        compiler_params=_params(("arbitrary",)))(qkvn, qkvn, qkvn, gates)


def delta_bwd(qkvn, gates, ssave, do, *, name):
    t = qkvn.shape[0]
    nc = t // DN_CHUNK

    def body(q_ref, k_ref, v_ref, g_ref, ss_ref, do_ref, dx_ref, dg_ref, dstate):
        @pl.when(pl.program_id(0) == 0)
        def _():
            dstate[...] = jnp.zeros_like(dstate)

        lane = lax.broadcasted_iota(jnp.int32, (DN_CHUNK, 128), 1)
        row = lax.broadcasted_iota(jnp.int32, (128, DN_CHUNK), 0)
        dbeta_all = jnp.zeros((DN_CHUNK, 128), F32)
        dgam_c_all = jnp.zeros((DN_CHUNK, 128), F32)
        dgam_r_all = jnp.zeros((128, DN_CHUNK), F32)
        _, vjp = jax.vjp(_delta_chunk, ss_ref[...], *_delta_operands(q_ref, k_ref, v_ref, g_ref[...]))
        do = jnp.stack([do_ref[:, h * B_HEAD_DIM:(h + 1) * B_HEAD_DIM] for h in range(B_HEADS)])
        ds0, dq, dk, dv, dbeta, dgam_c, dgam_r = vjp((dstate[...], do))
        dstate[...] = ds0
        for h in range(B_HEADS):
            dx_ref[:, h * B_HEAD_DIM:(h + 1) * B_HEAD_DIM] = dq[h]
            dx_ref[:, B_W + h * B_HEAD_DIM:B_W + (h + 1) * B_HEAD_DIM] = dk[h]
            dx_ref[:, 2 * B_W + h * B_HEAD_DIM:2 * B_W + (h + 1) * B_HEAD_DIM] = dv[h]
            dbeta_all = dbeta_all + jnp.where(lane == h, dbeta[h], 0.0)
            dgam_c_all = dgam_c_all + jnp.where(lane == B_HEADS + h, dgam_c[h], 0.0)
            dgam_r_all = dgam_r_all + jnp.where(row == B_HEADS + h, dgam_r[h], 0.0)
        dg_ref[...] = dbeta_all + _tri_sum(dgam_c_all + dgam_r_all.T, False)

    blk = lambda j: pl.BlockSpec((DN_CHUNK, B_W), lambda n: (nc - 1 - n, j))
    gsp = pl.BlockSpec((DN_CHUNK, 128), lambda n: (nc - 1 - n, 0))
    return pl.pallas_call(
        body, name=name, grid=(nc,),
        in_specs=[blk(0), blk(1), blk(2), gsp,
                  pl.BlockSpec((None, B_HEADS, B_HEAD_DIM, B_HEAD_DIM), lambda n: (nc - 1 - n, 0, 0, 0)), blk(0)],
        out_specs=[pl.BlockSpec((DN_CHUNK, 3 * B_W), lambda n: (nc - 1 - n, 0)), gsp],
        out_shape=[jax.ShapeDtypeStruct((t, 3 * B_W), F32), jax.ShapeDtypeStruct((t, 128), F32)],
        scratch_shapes=[pltpu.VMEM((B_HEADS, B_HEAD_DIM, B_HEAD_DIM), F32)],
        compiler_params=_params(("arbitrary",)))(qkvn, qkvn, qkvn, gates, ssave, do)


def gnorm_fwd(o, proj, onorm, *, name):
    t = o.shape[0]

    def body(o_ref, z_ref, w_ref, out_ref):
        ov = o_ref[...]
        r = lax.rsqrt(jnp.mean(ov * ov, axis=-1, keepdims=True) + EPS)
        out_ref[...] = (ov * r * w_ref[...] * _silu(z_ref[...])).astype(BF16)

    blk = pl.BlockSpec((ROWS, B_HEAD_DIM), lambda n, h: (n, h))
    return pl.pallas_call(
        body, name=name, grid=(t // ROWS, B_HEADS),
        in_specs=[blk, pl.BlockSpec((ROWS, B_HEAD_DIM), lambda n, h: (n, COL_Z // B_HEAD_DIM + h)),
                  pl.BlockSpec((1, B_HEAD_DIM), lambda n, h: (0, 0))],
        out_specs=blk, out_shape=jax.ShapeDtypeStruct((t, B_W), BF16),
        compiler_params=_params(("parallel", "parallel")))(o, proj, onorm)


def gnorm_bwd(o, proj, onorm, dout, *, dcol0, name):
    t = o.shape[0]

    def body(o_ref, z_ref, w_ref, d_ref, do_ref, dz_ref, dw_ref):
        @pl.when((pl.program_id(0) == 0) & (pl.program_id(1) == 0))
        def _():
            dw_ref[...] = jnp.zeros_like(dw_ref)

        ov, zv, wv, dv = o_ref[...], z_ref[...], w_ref[...], d_ref[...].astype(F32)
        r = lax.rsqrt(jnp.mean(ov * ov, axis=-1, keepdims=True) + EPS)
        nrm = ov * r
        dz_ref[...] = (dv * nrm * wv * _dsilu(zv)).astype(BF16)
        da = dv * _silu(zv)
        dw_ref[...] += jnp.sum(da * nrm, axis=0, keepdims=True)
        dn = da * wv
        do_ref[...] = r * dn - ov * (r * r * r) * jnp.mean(dn * ov, axis=-1, keepdims=True)

    blk = pl.BlockSpec((ROWS, B_HEAD_DIM), lambda n, h: (n, h))
    vec = pl.BlockSpec((1, B_HEAD_DIM), lambda n, h: (0, 0))
    return pl.pallas_call(
        body, name=name, grid=(t // ROWS, B_HEADS),
        in_specs=[blk, pl.BlockSpec((ROWS, B_HEAD_DIM), lambda n, h: (n, COL_Z // B_HEAD_DIM + h)), vec,
                  pl.BlockSpec((ROWS, B_HEAD_DIM), lambda n, h: (n, dcol0 // B_HEAD_DIM + h))],
        out_specs=[blk, blk, vec],
        out_shape=[jax.ShapeDtypeStruct((t, B_W), F32), jax.ShapeDtypeStruct((t, B_W), BF16),
                   jax.ShapeDtypeStruct((1, B_HEAD_DIM), F32)],
        compiler_params=_params(("arbitrary", "arbitrary")))(o, proj, onorm, dout)


def _ffn_fwd(h, norm_g, wg, wu, wd, tm, tag):
    hn = rms_fwd(h, norm_g, name=f"ffn{tag}_norm")
    gate, up, act = mm_gate_up(hn, wg, wu, tm=min(512, tm), tn=1408, tk=2048, name=f"ffn{tag}_gate_up")
    h_out = mm_nn(act, wd, tm=tm, tn=2048, tk=512, out_dtype=F32, res=h, name=f"ffn{tag}_down")
    return h_out, (hn, gate, up, act)


def _ffn_bwd(dh, h, norm_g, wg, wu, wd, saved, tm, tag, emit):
    hn, gate, up, act = saved
    dwd = mm_tn(act, dh, shards=1, tm=tm, tn=1024, tk=1408, out_dtype=BF16, name=f"ffn{tag}_dwd")[0]
    dgate, dup = mm_down_bwd(dh, wd, gate, up, tm=tm, tn=512, tk=2048, name=f"ffn{tag}_dact")
    dwg = mm_tn(hn, dgate, shards=N_SHARD, tm=tm, tn=1408, tk=1024, out_dtype=BF16, name=f"ffn{tag}_dwg")
    dwu = mm_tn(hn, dup, shards=N_SHARD, tm=tm, tn=1408, tk=1024, out_dtype=BF16, name=f"ffn{tag}_dwu")
    started = emit(f"ffn{tag}", {"gate": dwg, "up": dwu, "down": dwd})
    dhn = mm_nt(dgate, wg, tm=tm, tn=1024, tk=1408, out_dtype=F32, name=f"ffn{tag}_dhn_g")
    dhn = mm_nt(dup, wu, tm=tm, tn=1024, tk=1408, out_dtype=F32, res=dhn, name=f"ffn{tag}_dhn_u")
    dh_in, dnorm = rms_bwd(h, norm_g + started, dhn, dh, name=f"ffn{tag}_dnorm")
    return dh_in, dnorm


def _local_step(x, target, w, get, emit):
    t = x.shape[0]
    tm = min(1024, t)
    g = {}

    hn0 = rms_fwd(x, w["even_norm"], name="l0_norm")
    w.update(get("even_in", hn0))
    proj = mm_nn(hn0, w["even_w_in"], tm=tm, tn=512, tk=2048, out_dtype=F32, name="l0_w_in")
    out_a = att_fwd(proj, w["sinks"], name="l0_att")
    qkvn = dprep_fwd(proj, w["even_conv"], name="l0_prep")
    gates = gates_fwd(proj, w["a_log"], w["dt_bias"], name="l0_gates")
    o_delta, ssave = delta_fwd(qkvn, gates, name="l0_delta")
    out_b = gnorm_fwd(o_delta, proj, w["onorm"], name="l0_gnorm")
    mix0 = jnp.concatenate([out_a, out_b], axis=-1)
    w.update(get("even_out", mix0))
    h1 = mm_nn(mix0, w["even_w_out"], tm=tm, tn=512, tk=2048, out_dtype=F32, res=x, name="l0_w_out")
    f0 = get("ffn0", h1)
    h2, ffn0 = _ffn_fwd(h1, w["ffn_norm"][0:1] + f0["tok"], f0["gate"], f0["up"], f0["down"], tm, 0)
    hn2 = rms_fwd(h2, w["odd_norm"], name="l1_norm")
    w.update(get("odd", hn2))
    zpre = mm_nn(hn2, w["odd_w_in"], tm=tm, tn=1024, tk=2048, out_dtype=F32, name="l1_w_in")
    gated = gmlp_fwd(zpre, w["odd_ln_g"], w["odd_ln_b"], w["odd_w_s"], w["odd_b_s"], name="l1_gmlp")
    h3 = mm_nn(gated, w["odd_w_out"], tm=tm, tn=512, tk=2048, out_dtype=F32, res=h2, name="l1_w_out")
    f1 = get("ffn1", h3)
    h4, ffn1 = _ffn_fwd(h3, w["ffn_norm"][1:2] + f1["tok"], f1["gate"], f1["up"], f1["down"], tm, 1)
    loss, dh4, g["final_norm"] = loss_head(h4, w["final_norm"], target, name="loss_head")

    dh3, dn1 = _ffn_bwd(dh4, h3, w["ffn_norm"][1:2], f1["gate"], f1["up"], f1["down"], ffn1, tm, 1, emit)
    dw_out_o = mm_tn(gated, dh3, shards=1, tm=tm, tn=1024, tk=1024, out_dtype=BF16, name="l1_dw_out")[0]
    dgated = mm_nt(dh3, w["odd_w_out"], tm=tm, tn=512, tk=2048, out_dtype=BF16, name="l1_dgated")
    dzpre, g["odd_w_s"], g["odd_b_s"], g["odd_ln_g"], g["odd_ln_b"] = gmlp_bwd(
        zpre, dgated, w["odd_ln_g"], w["odd_ln_b"], w["odd_w_s"], w["odd_b_s"], name="l1_dgmlp")
    dw_in_o = mm_tn(hn2, dzpre, shards=N_SHARD, tm=tm, tn=1024, tk=1024, out_dtype=BF16, name="l1_dw_in")
    started = emit("odd", {"odd_w_in": dw_in_o, "odd_w_out": dw_out_o})
    dhn2 = mm_nt(dzpre, w["odd_w_in"], tm=tm, tn=1024, tk=1024, out_dtype=F32, name="l1_dhn")
    dh2, g["odd_norm"] = rms_bwd(h2, w["odd_norm"] + started, dhn2, dh3, name="l1_dnorm")
    dh1, dn0 = _ffn_bwd(dh2, h1, w["ffn_norm"][0:1], f0["gate"], f0["up"], f0["down"], ffn0, tm, 0, emit)
    g["ffn_norm"] = jnp.concatenate([dn0, dn1], axis=0)
    dw_out_e = mm_tn(mix0, dh1, shards=1, tm=tm, tn=1024, tk=1024, out_dtype=BF16, name="l0_dw_out")[0]
    started = emit("even_out", {"even_w_out": dw_out_e})
    dmix = mm_nt(dh1, w["even_w_out"], tm=tm, tn=512, tk=2048, out_dtype=F32, name="l0_dmix")
    dq_a, dkv_cur, dkv_prev, g["sinks"] = att_bwd(proj, w["sinks"] + started, dmix, name="l0_datt")
    dkv = dkv_cur + jnp.concatenate([dkv_prev[WINDOW:], jnp.zeros((WINDOW, 2 * A_KV), F32)], axis=0)
    do_delta, dz, g["onorm"] = gnorm_bwd(o_delta, proj, w["onorm"], dmix, dcol0=A_Q, name="l0_dgnorm")
    dqkvn, dgates = delta_bwd(qkvn, gates, ssave, do_delta, name="l0_ddelta")
    dqkv_b, g["even_conv"] = dprep_bwd(proj, w["even_conv"], dqkvn, name="l0_dprep")
    draw, g["a_log"], g["dt_bias"] = gates_bwd(proj, w["a_log"], w["dt_bias"], dgates, name="l0_dgates")
    dproj = jnp.concatenate([dq_a, dkv.astype(BF16), dqkv_b, dz, draw,
                             jnp.zeros((t, EVEN_IN_PAD - COL_GATE - 128), BF16)], axis=-1)
    dw_in_e = mm_tn(hn0, dproj, shards=1, tm=tm, tn=1408, tk=1024, out_dtype=BF16, name="l0_dw_in")[0]
    dhn0 = mm_nt(dproj, w["even_w_in"], tm=tm, tn=1024, tk=2816, out_dtype=F32, name="l0_dhn")
    grad_x, g["even_norm"] = rms_bwd(x, w["even_norm"], dhn0, dh1, name="l0_dnorm")
    emit("even_in", {"even_w_in": dw_in_e, "small": g})
    return loss, grad_x


ANY = pl.BlockSpec(memory_space=pl.ANY)
N_DEV = 8


def _place():
    return lax.axis_index("x"), lax.axis_index("y"), lax.axis_index("c")


def _chip_peers(x, y, c):
    return [((1 - x, y, c), 2 * (1 - x) + y), ((x, 1 - y, c), 2 * x + 1 - y), ((1 - x, 1 - y, c), 2 * (1 - x) + 1 - y)]


HBM = pl.BlockSpec(memory_space=pltpu.HBM)
SEM = pl.BlockSpec(memory_space=pltpu.SEMAPHORE)
EFFECT = pltpu.SideEffectType.DATAFLOW_SIDE_EFFECTING
N_PEER = 3


def _half(ref, c):
    r = ref.shape[0] // 2
    return ref.at[pl.ds(c * r, r)]


def _gather_plan(srcs, lands, send, recv):
    x, y, c = _place()
    return [pltpu.make_async_remote_copy(src_ref=_half(srcs[i], c), dst_ref=_half(lands[i].at[2 * x + y], c),
                                         send_sem=send.at[N_PEER * i + k], recv_sem=recv.at[N_PEER * i + k],
                                         device_id=peer, device_id_type=MESH_ID)
            for i in range(len(srcs)) for k, (peer, _) in enumerate(_chip_peers(x, y, c))]


def _relay_plan(srcs, lands, send, recv):
    x, y, c = _place()
    return [pltpu.make_async_remote_copy(src_ref=_half(lands[i].at[idx], c), dst_ref=_half(lands[i].at[idx], c),
                                         send_sem=send.at[N_PEER * i + k], recv_sem=recv.at[N_PEER * i + k],
                                         device_id=(x, y, 1 - c), device_id_type=MESH_ID)
            for i in range(len(srcs)) for k, (_, idx) in enumerate(_chip_peers(x, y, c))]


def _scatter_plan(srcs, lands, send, recv):
    x, y, c = _place()
    return [pltpu.make_async_remote_copy(src_ref=srcs[i].at[idx], dst_ref=lands[i].at[k], send_sem=send.at[N_PEER * i + k],
                                         recv_sem=recv.at[N_PEER * i + k], device_id=peer, device_id_type=MESH_ID)
            for i in range(len(srcs)) for k, (peer, idx) in enumerate(_chip_peers(x, y, c))]


def copies_start(plan, srcs, lands, after, *, name):
    n = len(srcs)
    both = list(srcs) + list(lands)

    def body(*refs):
        src_refs, land_refs = refs[:n], refs[n:2 * n]
        send, recv = refs[2 * n + 1], refs[2 * n + 2]
        for cp in plan(src_refs, land_refs, send, recv):
            cp.start()
        refs[-1][...] = jnp.zeros_like(refs[-1])

    res = pl.pallas_call(
        body, name=name,
        out_shape=(pltpu.SemaphoreType.DMA((n * N_PEER,)), pltpu.SemaphoreType.DMA((n * N_PEER,)),
                   *[pltpu.HBM(a.shape, a.dtype) for a in both], jax.ShapeDtypeStruct((8, 128), F32)),
        in_specs=[HBM] * (2 * n) + [ANY],
        out_specs=(SEM, SEM, *[HBM] * (2 * n), pl.BlockSpec(memory_space=pltpu.VMEM)),
        input_output_aliases={i: 2 + i for i in range(2 * n)},
        compiler_params=pltpu.CompilerParams(has_side_effects=EFFECT))(
            *[pltpu.with_memory_space_constraint(a, pltpu.HBM) for a in both], after)
    return {"send": res[0], "recv": res[1], "srcs": list(res[2:2 + n]), "lands": list(res[2 + n:2 + 2 * n]),
            "token": res[-1]}


def copies_relay(arrived_plan, next_plan, started, after, *, name):
    srcs, lands = started["srcs"], started["lands"]
    n = len(srcs)
    both = srcs + lands

    def body(*refs):
        src_refs, land_refs = refs[:n], refs[n:2 * n]
        send1, recv1 = refs[2 * n], refs[2 * n + 1]
        send2, recv2 = refs[2 * n + 3], refs[2 * n + 4]
        for cp in arrived_plan(src_refs, land_refs, send1, recv1):
            cp.wait_send()
            cp.wait_recv()
        for cp in next_plan(src_refs, land_refs, send2, recv2):
            cp.start()
        refs[-1][...] = jnp.zeros_like(refs[-1])

    res = pl.pallas_call(
        body, name=name,
        out_shape=(pltpu.SemaphoreType.DMA((n * N_PEER,)), pltpu.SemaphoreType.DMA((n * N_PEER,)),
                   *[pltpu.HBM(a.shape, a.dtype) for a in both], jax.ShapeDtypeStruct((8, 128), F32)),
        in_specs=[HBM] * (2 * n) + [SEM, SEM, ANY],
        out_specs=(SEM, SEM, *[HBM] * (2 * n), pl.BlockSpec(memory_space=pltpu.VMEM)),
        input_output_aliases={i: 2 + i for i in range(2 * n)},
        compiler_params=pltpu.CompilerParams(has_side_effects=EFFECT))(*both, started["send"], started["recv"], after)
    return {"send": res[0], "recv": res[1], "srcs": list(res[2:2 + n]), "lands": list(res[2 + n:2 + 2 * n]),
            "token": res[-1]}


def copies_wait(plan, started, after, *, name):
    srcs, lands = started["srcs"], started["lands"]
    n = len(srcs)
    both = srcs + lands

    def body(*refs):
        src_refs, land_refs = refs[:n], refs[n:2 * n]
        send, recv = refs[2 * n], refs[2 * n + 1]
        for cp in plan(src_refs, land_refs, send, recv):
            cp.wait_send()
            cp.wait_recv()

    res = pl.pallas_call(
        body, name=name, out_shape=tuple(pltpu.HBM(a.shape, a.dtype) for a in both),
        in_specs=[HBM] * (2 * n) + [SEM, SEM, ANY], out_specs=(HBM,) * (2 * n),
        input_output_aliases={i: i for i in range(2 * n)},
        compiler_params=pltpu.CompilerParams(has_side_effects=EFFECT))(*both, started["send"], started["recv"], after)
    return list(res[:n]), list(res[n:])


def allgather_small(small, *, name):
    def body(small_ref, out_ref, send, recv, loc):
        x, y, c = _place()
        dev = 4 * x + 2 * y + c
        local = pltpu.make_async_copy(small_ref, out_ref.at[dev], loc)
        remote = []
        for r in range(1, N_DEV):
            fx, fy, fc = (r >> 2) & 1, (r >> 1) & 1, r & 1
            peer = (1 - x if fx else x, 1 - y if fy else y, 1 - c if fc else c)
            remote.append(pltpu.make_async_remote_copy(
                src_ref=small_ref, dst_ref=out_ref.at[dev], send_sem=send.at[r - 1], recv_sem=recv.at[r - 1],
                device_id=peer, device_id_type=MESH_ID))
        local.start()
        for cp in remote:
            cp.start()
        for cp in remote:
            cp.wait()
        local.wait()

    return pl.pallas_call(
        body, name=name, in_specs=[ANY], out_specs=ANY,
        out_shape=jax.ShapeDtypeStruct((N_DEV,) + small.shape, small.dtype),
        scratch_shapes=[pltpu.SemaphoreType.DMA((N_DEV - 1,)), pltpu.SemaphoreType.DMA((N_DEV - 1,)),
                        pltpu.SemaphoreType.DMA(())])(small)


def swap_cores(arrs, *, name):
    n = len(arrs)

    def body(*refs):
        ins, outs = refs[:n], refs[n:2 * n]
        send, recv = refs[2 * n:]
        x, y, c = _place()
        copies = [pltpu.make_async_remote_copy(src_ref=ins[i], dst_ref=outs[i], send_sem=send.at[i], recv_sem=recv.at[i],
                                               device_id=(x, y, 1 - c), device_id_type=MESH_ID) for i in range(n)]
        for cp in copies:
            cp.start()
        for cp in copies:
            cp.wait()

    return pl.pallas_call(
        body, name=name, in_specs=[ANY] * n, out_specs=[ANY] * n,
        out_shape=[jax.ShapeDtypeStruct(a.shape, a.dtype) for a in arrs],
        scratch_shapes=[pltpu.SemaphoreType.DMA((n,)), pltpu.SemaphoreType.DMA((n,))])(*arrs)


RED_ROWS = 128


def sum_chips(own, got, *, name):
    r, c = own.shape
    rb = RED_ROWS if r % RED_ROWS == 0 else r

    def body(o_ref, a_ref, b_ref, c_ref, out_ref):
        out_ref[...] = ((o_ref[...].astype(F32) + a_ref[...].astype(F32)) + b_ref[...].astype(F32)) + c_ref[...].astype(F32)

    gk = lambda k: pl.BlockSpec((None, rb, c), lambda i: (k, i, 0))
    row = pl.BlockSpec((rb, c), lambda i: (i, 0))
    return pl.pallas_call(
        body, name=name, grid=(r // rb,), in_specs=[row, gk(0), gk(1), gk(2)], out_specs=row,
        out_shape=jax.ShapeDtypeStruct((r, c), F32), compiler_params=_params(("parallel",)))(own, got, got, got)


def sum_devices(small_all, *, name):
    _, p, c = small_all.shape

    def body(a_ref, out_ref):
        acc = a_ref[0]
        for d in range(1, N_DEV):
            acc = acc + a_ref[d]
        out_ref[...] = acc

    return pl.pallas_call(
        body, name=name, grid=(1,), in_specs=[pl.BlockSpec((N_DEV, p, c), lambda i: (0, 0, 0))],
        out_specs=pl.BlockSpec((p, c), lambda i: (0, 0)), out_shape=jax.ShapeDtypeStruct((p, c), F32),
        compiler_params=_params(("arbitrary",)))(small_all)


def adamw(parts, w, m, v, *, name):
    nl, r, c = w.shape
    assert len(parts) == nl
    npart = len(parts[0])
    rb = RED_ROWS if r % RED_ROWS == 0 else r
    flat = [a for layer in parts for a in layer]

    def body(*refs):
        p_refs, (w_ref, m_ref, v_ref) = refs[:nl * npart], refs[nl * npart:nl * npart + 3]
        g_ref, d_ref, nm_ref, nv_ref = refs[nl * npart + 3:]
        layer = pl.program_id(0)
        grad = None
        for l in range(nl):
            gl = p_refs[l * npart][...]
            for j in range(1, npart):
                gl = gl + p_refs[l * npart + j][...]
            grad = gl if grad is None else jnp.where(layer == l, gl, grad)
        wv, mv, vv = w_ref[...], m_ref[...], v_ref[...]
        nm = ADAM_B1 * mv + (1.0 - ADAM_B1) * grad
        nv = ADAM_B2 * vv + (1.0 - ADAM_B2) * (grad * grad)
        m_hat = nm / (1.0 - ADAM_B1 ** ADAM_STEP)
        v_hat = nv / (1.0 - ADAM_B2 ** ADAM_STEP)
        g_ref[...] = grad
        d_ref[...] = -ADAM_LR * (m_hat / (jnp.sqrt(v_hat) + ADAM_EPS) + ADAM_WD * wv)
        nm_ref[...] = nm
        nv_ref[...] = nv

    pspec = pl.BlockSpec((rb, c), lambda l, i: (i, 0))
    wspec = pl.BlockSpec((None, rb, c), lambda l, i: (l, i, 0))
    osh = jax.ShapeDtypeStruct((nl, r, c), F32)
    return pl.pallas_call(
        body, name=name, grid=(nl, r // rb), in_specs=[pspec] * (nl * npart) + [wspec] * 3,
        out_specs=[wspec] * 4, out_shape=[osh] * 4, compiler_params=_params(("parallel", "parallel")))(*flat, w, m, v)


def _rows128(a):
    flat = a.reshape(-1)
    pad = (-flat.shape[0]) % 128
    return jnp.pad(flat, (0, pad)).reshape(-1, 128)


def _pack_rows(arrs, multiple=8):
    rows = jnp.concatenate([_rows128(a.astype(F32)) for a in arrs], axis=0)
    return jnp.pad(rows, ((0, (-rows.shape[0]) % multiple), (0, 0)))


def _unpack_rows(rows, shapes):
    out, r0 = [], 0
    for shp in shapes:
        size = 1
        for s in shp:
            size *= s
        nr = -(-size // 128)
        out.append(rows[r0:r0 + nr].reshape(-1)[:size].reshape(shp))
        r0 += nr
    return out


SMALL_LOCAL_GRADS = ["even_norm", "even_conv", "a_log", "dt_bias", "sinks", "onorm", "odd_norm", "odd_ln_g",
                     "odd_ln_b", "odd_w_s", "odd_b_s", "ffn_norm", "final_norm"]
BIG = ["even_w_in", "even_w_out", "odd_w_in", "odd_w_out", "ffn_w_gate", "ffn_w_up", "ffn_w_down"]
WEIGHTS = ["even_norm", "even_w_in", "even_conv", "even_a_log", "even_dt_bias", "even_sinks", "even_onorm",
           "even_w_out", "odd_norm", "odd_w_in", "odd_ln_g", "odd_ln_b", "odd_w_s", "odd_b_s", "odd_w_out",
           "ffn_norm", "ffn_w_gate", "ffn_w_up", "ffn_w_down", "final_norm"]
SMALL = [n for n in WEIGHTS if n not in BIG]


def kernel(x, even_norm, even_w_in, even_conv, even_a_log, even_dt_bias, even_sinks, even_onorm, even_w_out, odd_norm, odd_w_in, odd_ln_g, odd_ln_b, odd_w_s, odd_b_s, odd_w_out, ffn_norm, ffn_w_gate, ffn_w_up, ffn_w_down, final_norm, loss_target, m_even_norm, m_even_w_in, m_even_conv, m_even_a_log, m_even_dt_bias, m_even_sinks, m_even_onorm, m_even_w_out, m_odd_norm, m_odd_w_in, m_odd_ln_g, m_odd_ln_b, m_odd_w_s, m_odd_b_s, m_odd_w_out, m_ffn_norm, m_ffn_w_gate, m_ffn_w_up, m_ffn_w_down, m_final_norm, v_even_norm, v_even_w_in, v_even_conv, v_even_a_log, v_even_dt_bias, v_even_sinks, v_even_onorm, v_even_w_out, v_odd_norm, v_odd_w_in, v_odd_ln_g, v_odd_ln_b, v_odd_w_s, v_odd_b_s, v_odd_w_out, v_ffn_norm, v_ffn_w_gate, v_ffn_w_up, v_ffn_w_down, v_final_norm):
    args = dict(locals())
    wl = {n: args[n] for n in WEIGHTS}
    ml = {n: args["m_" + n] for n in WEIGHTS}
    vl = {n: args["v_" + n] for n in WEIGHTS}
    me = 2 * lax.axis_index("x") + lax.axis_index("y")

    def landing(a):
        return lax.dynamic_update_index_in_dim(lax.empty((N_SHARD,) + a.shape, a.dtype), a, me, 0)

    b16 = lambda *arrs: [a.astype(BF16) for a in arrs]
    gather_groups = {
        "even_in": b16(even_w_in[0]) + [_pack_rows([even_conv[0], odd_norm, odd_ln_g, odd_ln_b], multiple=16)],
        "even_out": b16(even_w_out[0]),
        "ffn0": b16(ffn_w_gate[0], ffn_w_up[0], ffn_w_down[0]),
        "odd": b16(odd_w_in[0], odd_w_out[0]),
        "ffn1": b16(ffn_w_gate[1], ffn_w_up[1], ffn_w_down[1]),
    }
    gathering, after = {}, even_norm
    for group, srcs in gather_groups.items():
        gathering[group] = copies_start(_gather_plan, srcs, [landing(a) for a in srcs], after,
                                        name=f"gather_{group}_start")
        after = gathering[group]["token"]

    order = list(gather_groups)
    relayed, kept = {}, {}

    def relay(group, behind):
        relayed[group] = copies_relay(_gather_plan, _relay_plan, gathering[group], behind,
                                      name=f"gather_{group}_relay")
        return relayed[group]["token"][0:1, 0:1]

    def get(group, behind):
        if group not in relayed:
            relay(group, behind)
        _, lands = copies_wait(_relay_plan, relayed[group], behind, name=f"gather_{group}_wait")
        nxt = order.index(group) + 1
        tok = relay(order[nxt], lands[0]) if nxt < len(order) else jnp.zeros((1, 1), F32)
        if group == "even_in":
            parts = zip(*[_unpack_rows(lands[1][s], [(CONV_K, 768), (1, 512), (1, 512), (1, 512)])
                          for s in range(N_SHARD)])
            conv, onorm, lng, lnb = [jnp.concatenate(p, axis=1) for p in parts]
            w_in = jnp.pad(jnp.transpose(lands[0], (1, 0, 2)).reshape(D_MODEL, EVEN_IN),
                           ((0, 0), (0, EVEN_IN_PAD - EVEN_IN)))
            kept["odd_ln_g"] = lng
            return {"even_w_in": w_in, "even_conv": conv + tok, "odd_norm": onorm, "odd_ln_b": lnb}
        if group == "even_out":
            return {"even_w_out": lands[0].reshape(D_MODEL, D_MODEL), "ffn_norm": ffn_norm + tok}
        if group == "odd":
            return {"odd_w_in": lands[0], "odd_w_out": lands[1].reshape(D_MODEL, D_MODEL),
                    "odd_ln_g": kept["odd_ln_g"] + tok}
        return {"gate": lands[0], "up": lands[1], "down": lands[2].reshape(D_FF, D_MODEL), "tok": tok}

    rows4 =lambda a: a.reshape(N_SHARD, a.shape[0] // N_SHARD, a.shape[1])
    scattering, small = {}, {}

    def emit(group, grads):
        behind = even_norm
        if group == "even_in":
            small["local"] = grads["small"]
            small["all"] = behind = allgather_small(_pack_rows([grads["small"][n] for n in SMALL_LOCAL_GRADS]),
                                                    name="allgather_small")
            srcs = [jnp.transpose(grads["even_w_in"][:, :EVEN_IN].reshape(D_MODEL, N_SHARD, EVEN_IN // N_SHARD),
                                  (1, 0, 2))]
        elif group == "even_out":
            srcs = [rows4(grads["even_w_out"])]
        elif group == "odd":
            srcs = [grads["odd_w_in"], rows4(grads["odd_w_out"])]
        else:
            srcs = [grads["gate"], grads["up"], rows4(grads["down"])]
        lands = [lax.empty((N_PEER,) + a.shape[1:], a.dtype) for a in srcs]
        scattering[group] = copies_start(_scatter_plan, srcs, lands, behind, name=f"scatter_{group}_start")
        return scattering[group]["token"][0:1, 0:1]

    pad816 = lambda a: jnp.pad(a, ((0, 0), (B_HEADS, 128 - 2 * B_HEADS)))
    w = {
        "even_norm": even_norm + after[0:1, 0:1],
        "a_log": pad816(even_a_log), "dt_bias": pad816(even_dt_bias),
        "sinks": jnp.pad(even_sinks, ((0, 0), (0, 128 - A_HEADS))),
        "onorm": even_onorm,
        "odd_w_s": odd_w_s[0],
        "odd_b_s": jnp.pad(odd_b_s[0].T, ((0, 0), (0, 128 - C_GROUPS))),
        "ffn_norm": ffn_norm,
        "final_norm": final_norm[None],
    }
    loss_l, grad_x = _local_step(x[0], loss_target[0], w, get, emit)
    loss = lax.psum(loss_l[0, 0], ("x", "y", "c"))

    def finish(group, behind):
        srcs, lands = copies_wait(_scatter_plan, scattering[group], behind, name=f"scatter_{group}_wait")
        partial = [sum_chips(lax.dynamic_index_in_dim(srcs[i], me, 0, keepdims=False), lands[i],
                             name=f"sum_chips_{group}_{i}") for i in range(len(srcs))]
        other = swap_cores(partial, name=f"swap_cores_{group}")
        return list(zip(partial, other))

    last_started = scattering["even_in"]["token"]
    sums = {group: finish(group, last_started) for group in ("ffn1", "odd", "ffn0", "even_out")}
    outs = {}
    parts_of = {"even_w_out": [sums["even_out"][0]], "odd_w_in": [sums["odd"][0]], "odd_w_out": [sums["odd"][1]],
                "ffn_w_gate": [sums["ffn0"][0], sums["ffn1"][0]], "ffn_w_up": [sums["ffn0"][1], sums["ffn1"][1]],
                "ffn_w_down": [sums["ffn0"][2], sums["ffn1"][2]]}
    for n in parts_of:
        outs[n] = adamw(parts_of[n], wl[n], ml[n], vl[n], name=f"adamw_{n}")
    outs["even_w_in"] = adamw([finish("even_in", outs["ffn_w_down"][1])[0]], wl["even_w_in"], ml["even_w_in"],
                              vl["even_w_in"], name="adamw_even_w_in")

    g = small["local"]
    small_sum = sum_devices(small["all"], name="sum_devices")
    sg = dict(zip(SMALL_LOCAL_GRADS, _unpack_rows(small_sum, [g[n].shape for n in SMALL_LOCAL_GRADS])))
    own_cols = lambda a, width: lax.dynamic_slice_in_dim(a, me * width, width, axis=a.ndim - 1)
    small_grads = {
        "even_norm": sg["even_norm"], "even_conv": own_cols(sg["even_conv"], 768)[None],
        "even_a_log": sg["a_log"][:, B_HEADS:2 * B_HEADS], "even_dt_bias": sg["dt_bias"][:, B_HEADS:2 * B_HEADS],
        "even_sinks": sg["sinks"][:, :A_HEADS], "even_onorm": sg["onorm"],
        "odd_norm": own_cols(sg["odd_norm"], 512), "odd_ln_g": own_cols(sg["odd_ln_g"], 512),
        "odd_ln_b": own_cols(sg["odd_ln_b"], 512), "odd_w_s": sg["odd_w_s"][None],
        "odd_b_s": sg["odd_b_s"][:, :C_GROUPS].T[None], "ffn_norm": sg["ffn_norm"], "final_norm": sg["final_norm"][0],
    }
    packed = [_pack_rows([d[n] for n in SMALL])[None] for d in (small_grads, wl, ml, vl)]
    small_out = adamw([(packed[0][0],)], packed[1], packed[2], packed[3], name="adamw_small")
    shapes = [wl[n].shape for n in SMALL]
    for j in range(4):
        for n, a in zip(SMALL, _unpack_rows(small_out[j][0], shapes)):
            outs.setdefault(n, [None] * 4)[j] = a

    return (loss, grad_x[None], *[outs[n][0] for n in WEIGHTS], *[outs[n][1] for n in WEIGHTS],
            *[outs[n][2] for n in WEIGHTS], *[outs[n][3] for n in WEIGHTS])
```

```python
import functools

import jax
import jax.numpy as jnp
from jax import lax
from jax.experimental import pallas as pl
from jax.experimental.pallas import tpu as pltpu

F32 = jnp.float32
BF16 = jnp.bfloat16
NEG_INF = float("-inf")

D_MODEL = 2048
A_HEADS, A_KV_HEADS, A_HEAD_DIM, WINDOW = 16, 2, 64, 128
B_HEADS, B_HEAD_DIM, CONV_K, DN_CHUNK = 8, 128, 4, 64
C_GROUPS, C_CHUNK = 8, 128
C_GROUP_DIM = D_MODEL // C_GROUPS
D_FF = 5632
EPS = 1e-6
A_Q = A_HEADS * A_HEAD_DIM
A_KV = A_KV_HEADS * A_HEAD_DIM
B_W = B_HEADS * B_HEAD_DIM
EVEN_IN = A_Q + 2 * A_KV + 4 * B_W + 2 * B_HEADS
EVEN_IN_PAD = 5632
COL_KV = A_Q
COL_QKVB = A_Q + 2 * A_KV
COL_Z = COL_QKVB + 3 * B_W
COL_GATE = COL_Z + B_W
N_SHARD = 4

ADAM_LR, ADAM_B1, ADAM_B2, ADAM_EPS, ADAM_WD, ADAM_STEP = 0.001, 0.9, 0.999, 1e-08, 0.01, 10

VMEM_LIMIT_V7X = 56 * 1024 * 1024
MESH_ID = pl.DeviceIdType.MESH


def _params(sem=None):
    return pltpu.CompilerParams(dimension_semantics=sem, vmem_limit_bytes=VMEM_LIMIT_V7X)


def _sigmoid(x):
    return 1.0 / (1.0 + jnp.exp(-x))


def _silu(x):
    return x * _sigmoid(x)


def _dsilu(x):
    s = _sigmoid(x)
    return s * (1.0 + x * (1.0 - s))


def _gelu(x):
    return 0.5 * x * (1.0 + lax.erf(x * 0.7071067811865476))


def _dgelu(x):
    return 0.5 * (1.0 + lax.erf(x * 0.7071067811865476)) + x * jnp.exp(-0.5 * x * x) * 0.3989422804014327


def _dot(a, b, dims):
    if a.ndim == 3:
        (ca,), (cb,) = dims
        return lax.dot_general(a, b, (((ca + 1,), (cb + 1,)), ((0,), (0,))), preferred_element_type=F32)
    return lax.dot_general(a, b, (dims, ((), ())), preferred_element_type=F32)


NN = ((1,), (0,))
NT = ((1,), (1,))
TN = ((0,), (0,))


def _as3(b):
    return b if b.ndim == 3 else b[None]


def _accumulate(step, nsteps, accs, products, finish):
    if nsteps == 1:
        finish(products())
        return

    @pl.when(step == 0)
    def _():
        for acc, p in zip(accs, products()):
            acc[...] = p

    if nsteps > 2:
        @pl.when((step > 0) & (step < nsteps - 1))
        def _():
            for acc, p in zip(accs, products()):
                acc[...] += p

    @pl.when(step == nsteps - 1)
    def _():
        finish(tuple(acc[...] + p for acc, p in zip(accs, products())))


def mm_nn(a, b, *, tm, tn, tk, out_dtype, name, res=None, act=None):
    b3 = _as3(b)
    m, k = a.shape
    s, k2, ns = b3.shape
    assert k2 == k and m % tm == 0 and ns % tn == 0 and k % tk == 0, (a.shape, b3.shape, tm, tn, tk)
    nps, nk = ns // tn, k // tk

    def body(*refs):
        if res is None:
            a_ref, b_ref, o_ref, acc = refs
        else:
            a_ref, b_ref, r_ref, o_ref, acc = refs
        def finish(tiles):
            r = tiles[0] if res is None else tiles[0] + r_ref[...].astype(F32)
            o_ref[...] = r.astype(out_dtype)

        _accumulate(pl.program_id(2), nk, (acc,),
                    lambda: (_dot(a_ref[...].astype(BF16), b_ref[...].astype(BF16), NN),), finish)

    in_specs = [pl.BlockSpec((tm, tk), lambda i, j, kk: (i, kk)),
                pl.BlockSpec((None, tk, tn), lambda i, j, kk: (j // nps, kk, j % nps))]
    args = [a, b3]
    if res is not None:
        in_specs.append(pl.BlockSpec((tm, tn), lambda i, j, kk: (i, j)))
        args.append(res)
    return pl.pallas_call(
        body, name=name, grid=(m // tm, s * nps, nk), in_specs=in_specs,
        out_specs=pl.BlockSpec((tm, tn), lambda i, j, kk: (i, j)),
        out_shape=jax.ShapeDtypeStruct((m, s * ns), out_dtype),
        scratch_shapes=[pltpu.VMEM((tm, tn), F32)],
        compiler_params=_params(("parallel", "parallel", "arbitrary")))(*args)


def mm_nt(a, b, *, tm, tn, tk, out_dtype, name, res=None):
    b3 = _as3(b)
    m, n = a.shape
    s, k, ns = b3.shape
    assert n == s * ns and m % tm == 0 and k % tn == 0 and ns % tk == 0, (a.shape, b3.shape, tm, tn, tk)
    rps = ns // tk
    nr = s * rps

    def body(*refs):
        if res is None:
            a_ref, b_ref, o_ref, acc = refs
        else:
            a_ref, b_ref, r_ref, o_ref, acc = refs
        def finish(tiles):
            r = tiles[0] if res is None else tiles[0] + r_ref[...].astype(F32)
            o_ref[...] = r.astype(out_dtype)

        _accumulate(pl.program_id(2), nr, (acc,),
                    lambda: (_dot(a_ref[...].astype(BF16), b_ref[...].astype(BF16), NT),), finish)

    in_specs = [pl.BlockSpec((tm, tk), lambda i, j, r: (i, r)),
                pl.BlockSpec((None, tn, tk), lambda i, j, r: (r // rps, j, r % rps))]
    args = [a, b3]
    if res is not None:
        in_specs.append(pl.BlockSpec((tm, tn), lambda i, j, r: (i, j)))
        args.append(res)
    return pl.pallas_call(
        body, name=name, grid=(m // tm, k // tn, nr), in_specs=in_specs,
        out_specs=pl.BlockSpec((tm, tn), lambda i, j, r: (i, j)),
        out_shape=jax.ShapeDtypeStruct((m, k), out_dtype),
        scratch_shapes=[pltpu.VMEM((tm, tn), F32)],
        compiler_params=_params(("parallel", "parallel", "arbitrary")))(*args)


def mm_tn(a, b, *, shards, tm, tn, tk, out_dtype, name):
    m, k = a.shape
    m2, n = b.shape
    ns = n // shards
    assert m2 == m and n == shards * ns and m % tm == 0 and k % tk == 0 and ns % tn == 0, (a.shape, b.shape)
    nps, nm = ns // tn, m // tm

    def body(a_ref, b_ref, o_ref, acc):
        def finish(tiles):
            o_ref[...] = tiles[0].astype(out_dtype)

        _accumulate(pl.program_id(2), nm, (acc,),
                    lambda: (_dot(a_ref[...].astype(BF16), b_ref[...].astype(BF16), TN),), finish)

    return pl.pallas_call(
        body, name=name, grid=(k // tk, shards * nps, nm),
        in_specs=[pl.BlockSpec((tm, tk), lambda i, j, mi: (mi, i)),
                  pl.BlockSpec((tm, tn), lambda i, j, mi: (mi, j))],
        out_specs=pl.BlockSpec((None, tk, tn), lambda i, j, mi: (j // nps, i, j % nps)),
        out_shape=jax.ShapeDtypeStruct((shards, k, ns), out_dtype),
        scratch_shapes=[pltpu.VMEM((tk, tn), F32)],
        compiler_params=_params(("parallel", "parallel", "arbitrary")))(a, b)


def mm_gate_up(hn, wg, wu, *, tm, tn, tk, name):
    wg3, wu3 = _as3(wg), _as3(wu)
    m, k = hn.shape
    s, _, ns = wg3.shape
    assert m % tm == 0 and ns % tn == 0 and k % tk == 0
    nps, nk = ns // tn, k // tk

    def body(a_ref, g_ref, u_ref, og_ref, ou_ref, oa_ref, accg, accu):
        def products():
            a = a_ref[...].astype(BF16)
            return _dot(a, g_ref[...].astype(BF16), NN), _dot(a, u_ref[...].astype(BF16), NN)

        def finish(tiles):
            g, u = tiles
            og_ref[...] = g.astype(BF16)
            ou_ref[...] = u.astype(BF16)
            oa_ref[...] = (_silu(g) * u).astype(BF16)

        _accumulate(pl.program_id(2), nk, (accg, accu), products, finish)

    wspec = pl.BlockSpec((None, tk, tn), lambda i, j, kk: (j // nps, kk, j % nps))
    ospec = pl.BlockSpec((tm, tn), lambda i, j, kk: (i, j))
    osh = jax.ShapeDtypeStruct((m, s * ns), BF16)
    return pl.pallas_call(
        body, name=name, grid=(m // tm, s * nps, nk),
        in_specs=[pl.BlockSpec((tm, tk), lambda i, j, kk: (i, kk)), wspec, wspec],
        out_specs=[ospec, ospec, ospec], out_shape=[osh, osh, osh],
        scratch_shapes=[pltpu.VMEM((tm, tn) if nk > 1 else (8, 128), F32)] * 2,
        compiler_params=_params(("parallel", "parallel", "arbitrary")))(hn, wg3, wu3)


def mm_down_bwd(dh, wd, gate, up, *, tm, tn, tk, name):
    m, d = dh.shape
    f, d2 = wd.shape
    assert d2 == d and m % tm == 0 and f % tn == 0 and d % tk == 0
    nr = d // tk

    def body(a_ref, b_ref, g_ref, u_ref, og_ref, ou_ref, acc):
        def finish(tiles):
            da = tiles[0]
            g, u = g_ref[...].astype(F32), u_ref[...].astype(F32)
            og_ref[...] = (da * u * _dsilu(g)).astype(BF16)
            ou_ref[...] = (da * _silu(g)).astype(BF16)

        _accumulate(pl.program_id(2), nr, (acc,),
                    lambda: (_dot(a_ref[...].astype(BF16), b_ref[...].astype(BF16), NT),), finish)

    ospec = pl.BlockSpec((tm, tn), lambda i, j, r: (i, j))
    osh = jax.ShapeDtypeStruct((m, f), BF16)
    return pl.pallas_call(
        body, name=name, grid=(m // tm, f // tn, nr),
        in_specs=[pl.BlockSpec((tm, tk), lambda i, j, r: (i, r)),
                  pl.BlockSpec((tn, tk), lambda i, j, r: (j, r)), ospec, ospec],
        out_specs=[ospec, ospec], out_shape=[osh, osh],
        scratch_shapes=[pltpu.VMEM((tm, tn), F32)],
        compiler_params=_params(("parallel", "parallel", "arbitrary")))(dh, wd, gate, up)


ROWS = 256


def rms_fwd(x, g, *, name):
    t, d = x.shape

    def body(x_ref, g_ref, o_ref):
        xv = x_ref[...]
        r = lax.rsqrt(jnp.mean(xv * xv, axis=-1, keepdims=True) + EPS)
        o_ref[...] = (xv * r * g_ref[...]).astype(BF16)

    return pl.pallas_call(
        body, name=name, grid=(t // ROWS,),
        in_specs=[pl.BlockSpec((ROWS, d), lambda i: (i, 0)), pl.BlockSpec((1, d), lambda i: (0, 0))],
        out_specs=pl.BlockSpec((ROWS, d), lambda i: (i, 0)),
        out_shape=jax.ShapeDtypeStruct((t, d), BF16), compiler_params=_params(("parallel",)))(x, g)


def rms_bwd(x, g, dy, dres, *, name):
    t, d = x.shape

    def body(x_ref, g_ref, dy_ref, dr_ref, dx_ref, dg_ref):
        @pl.when(pl.program_id(0) == 0)
        def _():
            dg_ref[...] = jnp.zeros_like(dg_ref)

        xv, dyv = x_ref[...], dy_ref[...].astype(F32)
        r = lax.rsqrt(jnp.mean(xv * xv, axis=-1, keepdims=True) + EPS)
        dyg = dyv * g_ref[...]
        dx = r * dyg - xv * (r * r * r) * jnp.mean(dyg * xv, axis=-1, keepdims=True)
        dx_ref[...] = dx + dr_ref[...]
        dg_ref[...] += jnp.sum(dyv * xv * r, axis=0, keepdims=True)

    row = pl.BlockSpec((ROWS, d), lambda i: (i, 0))
    vec = pl.BlockSpec((1, d), lambda i: (0, 0))
    return pl.pallas_call(
        body, name=name, grid=(t // ROWS,), in_specs=[row, vec, row, row], out_specs=[row, vec],
        out_shape=[jax.ShapeDtypeStruct((t, d), F32), jax.ShapeDtypeStruct((1, d), F32)],
        compiler_params=_params(("arbitrary",)))(x, g, dy, dres)


def loss_head(h, g, target, *, name):
    t, d = h.shape

    def body(x_ref, g_ref, t_ref, loss_ref, dx_ref, dg_ref):
        @pl.when(pl.program_id(0) == 0)
        def _():
            dg_ref[...] = jnp.zeros_like(dg_ref)
            loss_ref[...] = jnp.zeros_like(loss_ref)

        xv, gv = x_ref[...], g_ref[...]
        r = lax.rsqrt(jnp.mean(xv * xv, axis=-1, keepdims=True) + EPS)
        e = xv * r * gv - t_ref[...]
        loss_ref[...] += 0.5 * jnp.sum(jnp.mean(e * e, axis=-1, keepdims=True), axis=0, keepdims=True)
        dyv = e * (1.0 / d)
        dyg = dyv * gv
        dx_ref[...] = r * dyg - xv * (r * r * r) * jnp.mean(dyg * xv, axis=-1, keepdims=True)
        dg_ref[...] += jnp.sum(dyv * xv * r, axis=0, keepdims=True)

    row = pl.BlockSpec((ROWS, d), lambda i: (i, 0))
    vec = pl.BlockSpec((1, d), lambda i: (0, 0))
    return pl.pallas_call(
        body, name=name, grid=(t // ROWS,), in_specs=[row, vec, row],
        out_specs=[pl.BlockSpec((1, 128), lambda i: (0, 0)), row, vec],
        out_shape=[jax.ShapeDtypeStruct((1, 128), F32), jax.ShapeDtypeStruct((t, d), F32),
                   jax.ShapeDtypeStruct((1, d), F32)],
        compiler_params=_params(("arbitrary",)))(h, g, target)


def _tril_mask():
    r = lax.broadcasted_iota(jnp.int32, (C_CHUNK, C_CHUNK), 0)
    c = lax.broadcasted_iota(jnp.int32, (C_CHUNK, C_CHUNK), 1)
    return r >= c


def _layer_norm_parts(v):
    mu = jnp.mean(v, axis=-1, keepdims=True)
    vc = v - mu
    rstd = lax.rsqrt(jnp.mean(vc * vc, axis=-1, keepdims=True) + EPS)
    return vc * rstd, rstd


def gmlp_fwd(zpre, ln_g, ln_b, ws, bs_t, *, name):
    t = zpre.shape[0]
    d = D_MODEL

    def body(zu_ref, zv_ref, g_ref, b_ref, ws_ref, bs_ref, o_ref):
        u = _gelu(zu_ref[...])
        vhat, _ = _layer_norm_parts(_gelu(zv_ref[...]))
        vln = (vhat * g_ref[...] + b_ref[...]).astype(BF16)
        mask = _tril_mask()
        for gi in range(C_GROUPS):
            sl = slice(gi * C_GROUP_DIM, (gi + 1) * C_GROUP_DIM)
            w = jnp.where(mask, ws_ref[gi], 0.0).astype(BF16)
            mixed = _dot(w, vln[:, sl], NN) + bs_ref[:, gi:gi + 1]
            o_ref[:, sl] = (u[:, sl] * mixed).astype(BF16)

    vec = pl.BlockSpec((1, d), lambda i: (0, 0))
    return pl.pallas_call(
        body, name=name, grid=(t // C_CHUNK,),
        in_specs=[pl.BlockSpec((C_CHUNK, d), lambda i: (i, 0)), pl.BlockSpec((C_CHUNK, d), lambda i: (i, 1)),
                  vec, vec, pl.BlockSpec((C_GROUPS, C_CHUNK, C_CHUNK), lambda i: (0, 0, 0)),
                  pl.BlockSpec((C_CHUNK, 128), lambda i: (0, 0))],
        out_specs=pl.BlockSpec((C_CHUNK, d), lambda i: (i, 0)),
        out_shape=jax.ShapeDtypeStruct((t, d), BF16), compiler_params=_params(("parallel",)))(
            zpre, zpre, ln_g, ln_b, ws, bs_t)


def gmlp_bwd(zpre, dgated, ln_g, ln_b, ws, bs_t, *, name):
    t = zpre.shape[0]
    d = D_MODEL

    def body(zu_ref, zv_ref, dg_ref, g_ref, b_ref, ws_ref, bs_ref, dz_ref, dws_ref, dbs_ref, dlg_ref, dlb_ref):
        @pl.when(pl.program_id(0) == 0)
        def _():
            dws_ref[...] = jnp.zeros_like(dws_ref)
            dbs_ref[...] = jnp.zeros_like(dbs_ref)
            dlg_ref[...] = jnp.zeros_like(dlg_ref)
            dlb_ref[...] = jnp.zeros_like(dlb_ref)

        zu, zv = zu_ref[...], zv_ref[...]
        u = _gelu(zu)
        vhat, rstd = _layer_norm_parts(_gelu(zv))
        gam = g_ref[...]
        vln = (vhat * gam + b_ref[...]).astype(BF16)
        dgt = dg_ref[...].astype(F32)
        mask = _tril_mask()
        lane = lax.broadcasted_iota(jnp.int32, (C_CHUNK, 128), 1)
        dbs = jnp.zeros((C_CHUNK, 128), F32)
        du_parts, dvln_parts = [], []
        for gi in range(C_GROUPS):
            sl = slice(gi * C_GROUP_DIM, (gi + 1) * C_GROUP_DIM)
            w = jnp.where(mask, ws_ref[gi], 0.0).astype(BF16)
            mixed = _dot(w, vln[:, sl], NN) + bs_ref[:, gi:gi + 1]
            du_parts.append(dgt[:, sl] * mixed)
            dmixed = dgt[:, sl] * u[:, sl]
            dmb = dmixed.astype(BF16)
            dws_ref[gi] += jnp.where(mask, _dot(dmb, vln[:, sl], NT), 0.0)
            dbs = dbs + jnp.where(lane == gi, jnp.sum(dmixed, axis=-1, keepdims=True), 0.0)
            dvln_parts.append(_dot(w, dmb, TN))
        dbs_ref[...] += dbs
        du = jnp.concatenate(du_parts, axis=-1)
        dvln = jnp.concatenate(dvln_parts, axis=-1)
        dlg_ref[...] += jnp.sum(dvln * vhat, axis=0, keepdims=True)
        dlb_ref[...] += jnp.sum(dvln, axis=0, keepdims=True)
        dvhat = dvln * gam
        dv = rstd * (dvhat - jnp.mean(dvhat, axis=-1, keepdims=True)
                     - vhat * jnp.mean(dvhat * vhat, axis=-1, keepdims=True))
        dz_ref[:, :d] = (du * _dgelu(zu)).astype(BF16)
        dz_ref[:, d:] = (dv * _dgelu(zv)).astype(BF16)

    vec = pl.BlockSpec((1, d), lambda i: (0, 0))
    wsp = pl.BlockSpec((C_GROUPS, C_CHUNK, C_CHUNK), lambda i: (0, 0, 0))
    bsp = pl.BlockSpec((C_CHUNK, 128), lambda i: (0, 0))
    return pl.pallas_call(
        body, name=name, grid=(t // C_CHUNK,),
        in_specs=[pl.BlockSpec((C_CHUNK, d), lambda i: (i, 0)), pl.BlockSpec((C_CHUNK, d), lambda i: (i, 1)),
                  pl.BlockSpec((C_CHUNK, d), lambda i: (i, 0)), vec, vec, wsp, bsp],
        out_specs=[pl.BlockSpec((C_CHUNK, 2 * d), lambda i: (i, 0)), wsp, bsp, vec, vec],
        out_shape=[jax.ShapeDtypeStruct((t, 2 * d), BF16), jax.ShapeDtypeStruct((C_GROUPS, C_CHUNK, C_CHUNK), F32),
                   jax.ShapeDtypeStruct((C_CHUNK, 128), F32), jax.ShapeDtypeStruct((1, d), F32),
                   jax.ShapeDtypeStruct((1, d), F32)],
        compiler_params=_params(("arbitrary",)))(zpre, zpre, dgated, ln_g, ln_b, ws, bs_t)


ATT_SCALE = A_HEAD_DIM ** -0.5
PAIRS = A_HEADS // 2
PAIRS_PER_KV = PAIRS // A_KV_HEADS


def _att_padded(tile):
    lo = lax.broadcasted_iota(jnp.int32, tile.shape, 1) < A_HEAD_DIM
    rolled = pltpu.roll(tile, A_HEAD_DIM, 1)
    zero = jnp.zeros_like(tile)
    return {(0, 0): jnp.where(lo, tile, zero).astype(BF16), (0, 1): jnp.where(lo, zero, rolled).astype(BF16),
            (1, 0): jnp.where(lo, rolled, zero).astype(BF16), (1, 1): jnp.where(lo, zero, tile).astype(BF16)}


def _att_valid(n):
    r = lax.broadcasted_iota(jnp.int32, (WINDOW, 2 * WINDOW), 0)
    c = lax.broadcasted_iota(jnp.int32, (WINDOW, 2 * WINDOW), 1)
    rel = r + WINDOW - c
    return (rel >= 0) & (rel < WINDOW) & ((c >= WINDOW) | (n > 0))


def _att_probs(qp, kpad, sink, valid):
    s = jnp.where(valid, _dot(qp, kpad, NT), NEG_INF)
    m = jnp.maximum(jnp.max(s, axis=-1, keepdims=True), sink)
    p = jnp.exp(s - m)
    e_sink = jnp.exp(sink - m)
    inv = 1.0 / (jnp.sum(p, axis=-1, keepdims=True) + e_sink)
    return p * inv, e_sink * inv


def _att_operands(q_ref, kvc_ref, kvp_ref, s_ref):
    kv = jnp.concatenate([kvp_ref[...], kvc_ref[...]], axis=0)
    kpad, vpad = _att_padded(kv[:, :128]), _att_padded(kv[:, 128:])
    key = lambda h: ((h // 2) // PAIRS_PER_KV, h % 2)
    pairs = [(q_ref[:, j * 128:(j + 1) * 128] * ATT_SCALE).astype(BF16) for j in range(PAIRS)]
    q = jnp.stack([pairs[h // 2] for h in range(A_HEADS)])
    k = jnp.stack([kpad[key(h)] for h in range(A_HEADS)])
    v = jnp.stack([vpad[key(h)] for h in range(A_HEADS)])
    sink = jnp.stack([s_ref[:, h:h + 1] for h in range(A_HEADS)])
    return q, k, v, sink


def _att_specs(t):
    return [pl.BlockSpec((WINDOW, A_Q), lambda n: (n, 0)),
            pl.BlockSpec((WINDOW, 2 * A_KV), lambda n: (n, COL_KV // (2 * A_KV))),
            pl.BlockSpec((WINDOW, 2 * A_KV), lambda n: (jnp.maximum(n - 1, 0), COL_KV // (2 * A_KV))),
            pl.BlockSpec((1, 128), lambda n: (0, 0))]


def att_fwd(proj, sinks, *, name):
    t = proj.shape[0]

    def body(q_ref, kvc_ref, kvp_ref, s_ref, o_ref):
        n = pl.program_id(0)
        q, k, v, sink = _att_operands(q_ref, kvc_ref, kvp_ref, s_ref)
        w, _ = _att_probs(q, k, sink, _att_valid(n))
        o = _dot(w.astype(BF16), v, NN)
        for j in range(PAIRS):
            o_ref[:, j * 128:(j + 1) * 128] = (o[2 * j] + o[2 * j + 1]).astype(BF16)

    return pl.pallas_call(
        body, name=name, grid=(t // WINDOW,), in_specs=_att_specs(t),
        out_specs=pl.BlockSpec((WINDOW, A_Q), lambda n: (n, 0)),
        out_shape=jax.ShapeDtypeStruct((t, A_Q), BF16), compiler_params=_params(("parallel",)))(
            proj, proj, proj, sinks)


def att_bwd(proj, sinks, dout, *, name):
    t = proj.shape[0]

    def body(q_ref, kvc_ref, kvp_ref, s_ref, do_ref, dq_ref, dkc_ref, dkp_ref, ds_ref):
        n = pl.program_id(0)

        @pl.when(n == 0)
        def _():
            ds_ref[...] = jnp.zeros_like(ds_ref)

        q, k, v, sink = _att_operands(q_ref, kvc_ref, kvp_ref, s_ref)
        dop = jnp.stack([do_ref[:, (h // 2) * 128:(h // 2 + 1) * 128] for h in range(A_HEADS)]).astype(BF16)
        w, w_sink = _att_probs(q, k, sink, _att_valid(n))
        dw = _dot(dop, v, NT)
        delta = jnp.sum(w * dw, axis=-1, keepdims=True)
        dsc = (w * (dw - delta)).astype(BF16)
        dsink_h = -jnp.sum(w_sink * delta, axis=1, keepdims=True)
        dq = _dot(dsc, k, NN)
        dk_h = _dot(dsc, q, TN)
        dv_h = _dot(w.astype(BF16), dop, TN)
        lane = lax.broadcasted_iota(jnp.int32, (1, 128), 1)
        dsink = jnp.zeros((1, 128), F32)
        for h in range(A_HEADS):
            dsink = dsink + jnp.where(lane == h, dsink_h[h], 0.0)
        ds_ref[...] += dsink
        for j in range(PAIRS):
            dq_ref[:, j * 128:(j + 1) * 128] = ((dq[2 * j] + dq[2 * j + 1]) * ATT_SCALE).astype(BF16)
        lo = lax.broadcasted_iota(jnp.int32, (2 * WINDOW, 128), 1) < A_HEAD_DIM
        heads_per_kv = A_HEADS // A_KV_HEADS

        def tile(per_head):
            acc = {}
            for kvh in range(A_KV_HEADS):
                for half in range(2):
                    hs = range(kvh * heads_per_kv + half, (kvh + 1) * heads_per_kv, 2)
                    acc[(kvh, half)] = functools.reduce(lambda a, b: a + b, [per_head[h] for h in hs])
            return jnp.where(lo, acc[(0, 0)] + pltpu.roll(acc[(0, 1)], A_HEAD_DIM, 1),
                             pltpu.roll(acc[(1, 0)], A_HEAD_DIM, 1) + acc[(1, 1)])

        dkv = jnp.concatenate([tile(dk_h), tile(dv_h)], axis=1)
        dkp_ref[...] = dkv[:WINDOW]
        dkc_ref[...] = dkv[WINDOW:]

    kvo = pl.BlockSpec((WINDOW, 2 * A_KV), lambda n: (n, 0))
    return pl.pallas_call(
        body, name=name, grid=(t // WINDOW,),
        in_specs=_att_specs(t) + [pl.BlockSpec((WINDOW, A_Q), lambda n: (n, 0))],
        out_specs=[pl.BlockSpec((WINDOW, A_Q), lambda n: (n, 0)), kvo, kvo, pl.BlockSpec((1, 128), lambda n: (0, 0))],
        out_shape=[jax.ShapeDtypeStruct((t, A_Q), BF16), jax.ShapeDtypeStruct((t, 2 * A_KV), F32),
                   jax.ShapeDtypeStruct((t, 2 * A_KV), F32), jax.ShapeDtypeStruct((1, 128), F32)],
        compiler_params=_params(("arbitrary",)))(proj, proj, proj, sinks, dout)


QK_SCALE = B_HEAD_DIM ** -0.5
PREP_COLS = 256
PREP_NCB = 3 * B_W // PREP_COLS
HALO = 8
PREP_ROWS = 512


def _roll_rows(x, shift):
    n = x.shape[0]
    return x if shift % n == 0 else pltpu.roll(x, shift % n, 0)


def _conv_taps(xe, w):
    xs = [_roll_rows(xe, CONV_K - 1 - i) for i in range(CONV_K)]
    c = w[0:1] * xs[0]
    for i in range(1, CONV_K):
        c = c + w[i:i + 1] * xs[i]
    return xs, c


def dprep_fwd(proj, conv_w, *, name):
    t = proj.shape[0]
    tt = min(PREP_ROWS, t)
    col0 = COL_QKVB // PREP_COLS

    def body(x_ref, h_ref, w_ref, o_ref):
        cb, n = pl.program_id(0), pl.program_id(1)
        halo = jnp.where(n > 0, h_ref[...], 0.0)
        xe = jnp.concatenate([halo, x_ref[...]], axis=0)
        _, c = _conv_taps(xe, w_ref[...])
        y = _silu(c)[HALO:]
        parts = []
        for hh in range(PREP_COLS // B_HEAD_DIM):
            yh = y[:, hh * B_HEAD_DIM:(hh + 1) * B_HEAD_DIM]
            parts.append(yh * lax.rsqrt(jnp.sum(yh * yh, axis=-1, keepdims=True) + EPS))
        nrm = jnp.concatenate(parts, axis=-1)
        o_ref[...] = jnp.where(cb < 4, nrm * QK_SCALE, jnp.where(cb < 8, nrm, y))

    return pl.pallas_call(
        body, name=name, grid=(PREP_NCB, t // tt),
        in_specs=[pl.BlockSpec((tt, PREP_COLS), lambda cb, n: (n, col0 + cb)),
                  pl.BlockSpec((HALO, PREP_COLS), lambda cb, n: (jnp.maximum(n * (tt // HALO) - 1, 0), col0 + cb)),
                  pl.BlockSpec((CONV_K, PREP_COLS), lambda cb, n: (0, cb))],
        out_specs=pl.BlockSpec((tt, PREP_COLS), lambda cb, n: (n, cb)),
        out_shape=jax.ShapeDtypeStruct((t, 3 * B_W), F32), compiler_params=_params(("parallel", "parallel")))(
            proj, proj, conv_w)


def dprep_bwd(proj, conv_w, dqkvn, *, name):
    t = proj.shape[0]
    tt = min(PREP_ROWS, t)
    nb = t // tt
    col0 = COL_QKVB // PREP_COLS
    n8 = t // HALO

    def body(xc_ref, xb_ref, xa_ref, dc_ref, da_ref, w_ref, dx_ref, dw_ref):
        cb, n = pl.program_id(0), pl.program_id(1)

        @pl.when(n == 0)
        def _():
            dw_ref[...] = jnp.zeros_like(dw_ref)

        w = w_ref[...]
        xe = jnp.concatenate([jnp.where(n > 0, xb_ref[...], 0.0), xc_ref[...], xa_ref[...]], axis=0)
        xs, c = _conv_taps(xe, w)
        sg = _sigmoid(c)
        y = c * sg
        dout = jnp.concatenate([jnp.zeros((HALO, PREP_COLS), F32), dc_ref[...],
                                jnp.where(n < nb - 1, da_ref[...], 0.0)], axis=0)
        dsc = jnp.where(cb < 4, QK_SCALE, 1.0)
        parts = []
        for hh in range(PREP_COLS // B_HEAD_DIM):
            sl = slice(hh * B_HEAD_DIM, (hh + 1) * B_HEAD_DIM)
            yh, doh = y[:, sl], dout[:, sl] * dsc
            r = lax.rsqrt(jnp.sum(yh * yh, axis=-1, keepdims=True) + EPS)
            parts.append(doh * r - yh * (r * r * r) * jnp.sum(doh * yh, axis=-1, keepdims=True))
        dy = jnp.where(cb < 8, jnp.concatenate(parts, axis=-1), dout)
        dcv = dy * sg * (1.0 + c * (1.0 - sg))
        dxe = w[CONV_K - 1:CONV_K] * dcv
        for i in range(CONV_K - 1):
            dxe = dxe + w[i:i + 1] * _roll_rows(dcv, -(CONV_K - 1 - i))
        dx_ref[...] = dxe[HALO:HALO + tt].astype(BF16)
        for i in range(CONV_K):
            dw_ref[i:i + 1, :] += jnp.sum((dcv * xs[i])[HALO:HALO + tt], axis=0, keepdims=True)

    def after(n):
        return jnp.minimum((n + 1) * (tt // HALO), n8 - 1)

    return pl.pallas_call(
        body, name=name, grid=(PREP_NCB, nb),
        in_specs=[pl.BlockSpec((tt, PREP_COLS), lambda cb, n: (n, col0 + cb)),
                  pl.BlockSpec((HALO, PREP_COLS), lambda cb, n: (jnp.maximum(n * (tt // HALO) - 1, 0), col0 + cb)),
                  pl.BlockSpec((HALO, PREP_COLS), lambda cb, n: (after(n), col0 + cb)),
                  pl.BlockSpec((tt, PREP_COLS), lambda cb, n: (n, cb)),
                  pl.BlockSpec((HALO, PREP_COLS), lambda cb, n: (after(n), cb)),
                  pl.BlockSpec((CONV_K, PREP_COLS), lambda cb, n: (0, cb))],
        out_specs=[pl.BlockSpec((tt, PREP_COLS), lambda cb, n: (n, cb)),
                   pl.BlockSpec((CONV_K, PREP_COLS), lambda cb, n: (0, cb))],
        out_shape=[jax.ShapeDtypeStruct((t, 3 * B_W), BF16), jax.ShapeDtypeStruct((CONV_K, 3 * B_W), F32)],
        compiler_params=_params(("parallel", "arbitrary")))(proj, proj, proj, dqkvn, dqkvn, conv_w)


def _softplus(z):
    return jnp.maximum(z, 0.0) + jnp.log(1.0 + jnp.exp(-jnp.abs(z)))


def gates_fwd(proj, alog_pad, dtb_pad, *, name):
    t = proj.shape[0]

    def body(x_ref, a_ref, b_ref, o_ref):
        raw = x_ref[...]
        lane = lax.broadcasted_iota(jnp.int32, raw.shape, 1)
        g = -jnp.exp(a_ref[...]) * _softplus(raw + b_ref[...])
        o_ref[...] = jnp.where(lane < B_HEADS, _sigmoid(raw), jnp.where(lane < 2 * B_HEADS, g, 0.0))

    vec = pl.BlockSpec((1, 128), lambda n: (0, 0))
    return pl.pallas_call(
        body, name=name, grid=(t // ROWS,),
        in_specs=[pl.BlockSpec((ROWS, 128), lambda n: (n, COL_GATE // 128)), vec, vec],
        out_specs=pl.BlockSpec((ROWS, 128), lambda n: (n, 0)),
        out_shape=jax.ShapeDtypeStruct((t, 128), F32), compiler_params=_params(("parallel",)))(
            proj, alog_pad, dtb_pad)


def gates_bwd(proj, alog_pad, dtb_pad, dgates, *, name):
    t = proj.shape[0]

    def body(x_ref, a_ref, b_ref, dg_ref, dx_ref, da_ref, db_ref):
        @pl.when(pl.program_id(0) == 0)
        def _():
            da_ref[...] = jnp.zeros_like(da_ref)
            db_ref[...] = jnp.zeros_like(db_ref)

        raw, dgt = x_ref[...], dg_ref[...]
        lane = lax.broadcasted_iota(jnp.int32, raw.shape, 1)
        is_beta, is_g = lane < B_HEADS, (lane >= B_HEADS) & (lane < 2 * B_HEADS)
        beta = _sigmoid(raw)
        z = raw + b_ref[...]
        neg_a = -jnp.exp(a_ref[...])
        d_z = jnp.where(is_g, dgt * neg_a * _sigmoid(z), 0.0)
        dx_ref[...] = jnp.where(is_beta, dgt * beta * (1.0 - beta), d_z).astype(BF16)
        db_ref[...] += jnp.sum(d_z, axis=0, keepdims=True)
        da_ref[...] += jnp.sum(jnp.where(is_g, dgt * neg_a * _softplus(z), 0.0), axis=0, keepdims=True)

    vec = pl.BlockSpec((1, 128), lambda n: (0, 0))
    row = pl.BlockSpec((ROWS, 128), lambda n: (n, 0))
    return pl.pallas_call(
        body, name=name, grid=(t // ROWS,),
        in_specs=[pl.BlockSpec((ROWS, 128), lambda n: (n, COL_GATE // 128)), vec, vec, row],
        out_specs=[row, vec, vec],
        out_shape=[jax.ShapeDtypeStruct((t, 128), BF16), jax.ShapeDtypeStruct((1, 128), F32),
                   jax.ShapeDtypeStruct((1, 128), F32)],
        compiler_params=_params(("arbitrary",)))(proj, alog_pad, dtb_pad, dgates)


def _split2(a):
    hi = a.astype(BF16)
    return hi, (a - hi.astype(F32)).astype(BF16)


def _dotp(a, b, dims, passes):
    if passes == 1:
        return _dot(a.astype(BF16), b.astype(BF16), dims)
    ah, al = _split2(a)
    bh, bl = _split2(b)
    return _dot(ah, bh, dims) + (_dot(ah, bl, dims) + _dot(al, bh, dims))


_GRAD_DIMS = {NN: ((NT, False), (TN, False)), NT: ((NN, False), (TN, True)), TN: ((NT, True), (NN, False))}


def _make_mm(dims, passes, grad_passes):
    (da_dims, da_swap), (db_dims, db_swap) = _GRAD_DIMS[dims]

    @jax.custom_vjp
    def mm(a, b):
        return _dotp(a, b, dims, passes)

    def fwd(a, b):
        return _dotp(a, b, dims, passes), (a, b)

    def bwd(saved, ct):
        a, b = saved
        da = _dotp(b, ct, da_dims, grad_passes) if da_swap else _dotp(ct, b, da_dims, grad_passes)
        db = _dotp(ct, a, db_dims, grad_passes) if db_swap else _dotp(a, ct, db_dims, grad_passes)
        return da, db

    mm.defvjp(fwd, bwd)
    return mm


MM1 = {d: _make_mm(d, 1, 1) for d in (NN, NT, TN)}
MM3 = {d: _make_mm(d, 3, 1) for d in (NN, NT, TN)}


def _tri_ones(lower):
    r = lax.broadcasted_iota(jnp.int32, (DN_CHUNK, DN_CHUNK), 0)
    c = lax.broadcasted_iota(jnp.int32, (DN_CHUNK, DN_CHUNK), 1)
    return (r >= c if lower else r <= c).astype(BF16)


def _tri_sum(x, lower):
    tri = _tri_ones(lower)
    hi = x.astype(BF16)
    r1 = x - hi.astype(F32)
    mid = r1.astype(BF16)
    lo = (r1 - mid.astype(F32)).astype(BF16)
    return _dot(tri, hi, NN) + (_dot(tri, mid, NN) + _dot(tri, lo, NN))


def _delta_chunk(s0, q, k, v, beta, gam_c, gam_r):
    c = DN_CHUNK
    r = lax.broadcasted_iota(jnp.int32, (c, c), 0)
    cc = lax.broadcasted_iota(jnp.int32, (c, c), 1)
    incl, strict = r >= cc, r > cc
    eye = (r == cc).astype(F32)
    decay = jnp.exp(jnp.where(incl, gam_c - gam_r, NEG_INF))
    g_last = gam_c[:, c - 1:c, :]
    e_gam, e_rest, e_last = jnp.exp(gam_c), jnp.exp(g_last - gam_c), jnp.exp(g_last)
    a_neg = -jnp.where(strict, beta * MM1[NT](k, k) * decay, 0.0)
    inv = eye + a_neg
    pw = a_neg
    for _ in range(5):
        pw = MM3[NN](pw, pw)
        inv = inv + MM3[NN](inv, pw)
    uw = MM3[NN](inv, jnp.concatenate([v * beta, k * (beta * e_gam)], axis=-1))
    u, w = uw[..., :B_HEAD_DIM], uw[..., B_HEAD_DIM:]
    qk = MM1[NT](q, k) * decay
    v_new = u - MM1[NN](w, s0)
    o = MM1[NN](q * e_gam, s0) + MM1[NN](qk, v_new)
    s1 = s0 * e_last + MM1[TN](k * e_rest, v_new)
    return s1, o


def _delta_operands(q_ref, k_ref, v_ref, gt):
    heads = lambda ref: jnp.stack([ref[:, h * B_HEAD_DIM:(h + 1) * B_HEAD_DIM] for h in range(B_HEADS)])
    gam = _tri_sum(gt, True)
    gam_t = gam.T
    beta = jnp.stack([gt[:, h:h + 1] for h in range(B_HEADS)])
    gam_c = jnp.stack([gam[:, B_HEADS + h:B_HEADS + h + 1] for h in range(B_HEADS)])
    gam_r = jnp.stack([gam_t[B_HEADS + h:B_HEADS + h + 1, :] for h in range(B_HEADS)])
    return heads(q_ref), heads(k_ref), heads(v_ref), beta, gam_c, gam_r


def delta_fwd(qkvn, gates, *, name):
    t = qkvn.shape[0]
    nc = t // DN_CHUNK

    def body(q_ref, k_ref, v_ref, g_ref, o_ref, ss_ref, state):
        @pl.when(pl.program_id(0) == 0)
        def _():
            state[...] = jnp.zeros_like(state)

        s0 = state[...]
        ss_ref[...] = s0
        s1, o = _delta_chunk(s0, *_delta_operands(q_ref, k_ref, v_ref, g_ref[...]))
        state[...] = s1
        for h in range(B_HEADS):
            o_ref[:, h * B_HEAD_DIM:(h + 1) * B_HEAD_DIM] = o[h]

    blk = lambda j: pl.BlockSpec((DN_CHUNK, B_W), lambda n: (n, j))
    return pl.pallas_call(
        body, name=name, grid=(nc,),
        in_specs=[blk(0), blk(1), blk(2), pl.BlockSpec((DN_CHUNK, 128), lambda n: (n, 0))],
        out_specs=[blk(0), pl.BlockSpec((None, B_HEADS, B_HEAD_DIM, B_HEAD_DIM), lambda n: (n, 0, 0, 0))],
        out_shape=[jax.ShapeDtypeStruct((t, B_W), F32),
                   jax.ShapeDtypeStruct((nc, B_HEADS, B_HEAD_DIM, B_HEAD_DIM), F32)],
        scratch_shapes=[pltpu.VMEM((B_HEADS, B_HEAD_DIM, B_HEAD_DIM), F32)],
        compiler_params=_params(("arbitrary",)))(qkvn, qkvn, qkvn, gates)


def delta_bwd(qkvn, gates, ssave, do, *, name):
    t = qkvn.shape[0]
    nc = t // DN_CHUNK

    def body(q_ref, k_ref, v_ref, g_ref, ss_ref, do_ref, dx_ref, dg_ref, dstate):
        @pl.when(pl.program_id(0) == 0)
        def _():
            dstate[...] = jnp.zeros_like(dstate)

        lane = lax.broadcasted_iota(jnp.int32, (DN_CHUNK, 128), 1)
        row = lax.broadcasted_iota(jnp.int32, (128, DN_CHUNK), 0)
        dbeta_all = jnp.zeros((DN_CHUNK, 128), F32)
        dgam_c_all = jnp.zeros((DN_CHUNK, 128), F32)
        dgam_r_all = jnp.zeros((128, DN_CHUNK), F32)
        _, vjp = jax.vjp(_delta_chunk, ss_ref[...], *_delta_operands(q_ref, k_ref, v_ref, g_ref[...]))
        do = jnp.stack([do_ref[:, h * B_HEAD_DIM:(h + 1) * B_HEAD_DIM] for h in range(B_HEADS)])
        ds0, dq, dk, dv, dbeta, dgam_c, dgam_r = vjp((dstate[...], do))
        dstate[...] = ds0
        for h in range(B_HEADS):
            dx_ref[:, h * B_HEAD_DIM:(h + 1) * B_HEAD_DIM] = dq[h]
            dx_ref[:, B_W + h * B_HEAD_DIM:B_W + (h + 1) * B_HEAD_DIM] = dk[h]
            dx_ref[:, 2 * B_W + h * B_HEAD_DIM:2 * B_W + (h + 1) * B_HEAD_DIM] = dv[h]
            dbeta_all = dbeta_all + jnp.where(lane == h, dbeta[h], 0.0)
            dgam_c_all = dgam_c_all + jnp.where(lane == B_HEADS + h, dgam_c[h], 0.0)
            dgam_r_all = dgam_r_all + jnp.where(row == B_HEADS + h, dgam_r[h], 0.0)
        dg_ref[...] = dbeta_all + _tri_sum(dgam_c_all + dgam_r_all.T, False)

    blk = lambda j: pl.BlockSpec((DN_CHUNK, B_W), lambda n: (nc - 1 - n, j))
    gsp = pl.BlockSpec((DN_CHUNK, 128), lambda n: (nc - 1 - n, 0))
    return pl.pallas_call(
        body, name=name, grid=(nc,),
        in_specs=[blk(0), blk(1), blk(2), gsp,
                  pl.BlockSpec((None, B_HEADS, B_HEAD_DIM, B_HEAD_DIM), lambda n: (nc - 1 - n, 0, 0, 0)), blk(0)],
        out_specs=[pl.BlockSpec((DN_CHUNK, 3 * B_W), lambda n: (nc - 1 - n, 0)), gsp],
        out_shape=[jax.ShapeDtypeStruct((t, 3 * B_W), F32), jax.ShapeDtypeStruct((t, 128), F32)],
        scratch_shapes=[pltpu.VMEM((B_HEADS, B_HEAD_DIM, B_HEAD_DIM), F32)],
        compiler_params=_params(("arbitrary",)))(qkvn, qkvn, qkvn, gates, ssave, do)


GNORM_ROWS = 1024


def gnorm_fwd(o, proj, onorm, *, name):
    t = o.shape[0]

    def body(o_ref, z_ref, w_ref, out_ref):
        ov = o_ref[...]
        r = lax.rsqrt(jnp.mean(ov * ov, axis=-1, keepdims=True) + EPS)
        out_ref[...] = (ov * r * w_ref[...] * _silu(z_ref[...])).astype(BF16)

    rows = min(GNORM_ROWS, t)
    blk = pl.BlockSpec((rows, B_HEAD_DIM), lambda n, h: (n, h))
    return pl.pallas_call(
        body, name=name, grid=(t // rows, B_HEADS),
        in_specs=[blk, pl.BlockSpec((rows, B_HEAD_DIM), lambda n, h: (n, COL_Z // B_HEAD_DIM + h)),
                  pl.BlockSpec((1, B_HEAD_DIM), lambda n, h: (0, 0))],
        out_specs=blk, out_shape=jax.ShapeDtypeStruct((t, B_W), BF16),
        compiler_params=_params(("parallel", "parallel")))(o, proj, onorm)


def gnorm_bwd(o, proj, onorm, dout, *, dcol0, name):
    t = o.shape[0]

    def body(o_ref, z_ref, w_ref, d_ref, do_ref, dz_ref, dw_ref):
        @pl.when((pl.program_id(0) == 0) & (pl.program_id(1) == 0))
        def _():
            dw_ref[...] = jnp.zeros_like(dw_ref)

        ov, zv, wv, dv = o_ref[...], z_ref[...], w_ref[...], d_ref[...].astype(F32)
        r = lax.rsqrt(jnp.mean(ov * ov, axis=-1, keepdims=True) + EPS)
        nrm = ov * r
        dz_ref[...] = (dv * nrm * wv * _dsilu(zv)).astype(BF16)
        da = dv * _silu(zv)
        dw_ref[...] += jnp.sum(da * nrm, axis=0, keepdims=True)
        dn = da * wv
        do_ref[...] = r * dn - ov * (r * r * r) * jnp.mean(dn * ov, axis=-1, keepdims=True)

    rows = min(GNORM_ROWS, t)
    blk = pl.BlockSpec((rows, B_HEAD_DIM), lambda n, h: (n, h))
    vec = pl.BlockSpec((1, B_HEAD_DIM), lambda n, h: (0, 0))
    return pl.pallas_call(
        body, name=name, grid=(t // rows, B_HEADS),
        in_specs=[blk, pl.BlockSpec((rows, B_HEAD_DIM), lambda n, h: (n, COL_Z // B_HEAD_DIM + h)), vec,
                  pl.BlockSpec((rows, B_HEAD_DIM), lambda n, h: (n, dcol0 // B_HEAD_DIM + h))],
        out_specs=[blk, blk, vec],
        out_shape=[jax.ShapeDtypeStruct((t, B_W), F32), jax.ShapeDtypeStruct((t, B_W), BF16),
                   jax.ShapeDtypeStruct((1, B_HEAD_DIM), F32)],
        compiler_params=_params(("arbitrary", "arbitrary")))(o, proj, onorm, dout)


def _ffn_fwd(h, norm_g, wg, wu, wd, tm, tag):
    hn = rms_fwd(h, norm_g, name=f"ffn{tag}_norm")
    gate, up, act = mm_gate_up(hn, wg, wu, tm=min(512, tm), tn=1408, tk=2048, name=f"ffn{tag}_gate_up")
    h_out = mm_nn(act, wd, tm=tm, tn=2048, tk=512, out_dtype=F32, res=h, name=f"ffn{tag}_down")
    return h_out, (hn, gate, up, act)


def _ffn_bwd(dh, h, norm_g, wg, wu, wd, saved, tm, tag, emit):
    hn, gate, up, act = saved
    dwd = mm_tn(act, dh, shards=1, tm=tm, tn=1024, tk=1408, out_dtype=BF16, name=f"ffn{tag}_dwd")[0]
    dgate, dup = mm_down_bwd(dh, wd, gate, up, tm=tm, tn=512, tk=2048, name=f"ffn{tag}_dact")
    dwg = mm_tn(hn, dgate, shards=N_SHARD, tm=tm, tn=1408, tk=1024, out_dtype=BF16, name=f"ffn{tag}_dwg")
    dwu = mm_tn(hn, dup, shards=N_SHARD, tm=tm, tn=1408, tk=1024, out_dtype=BF16, name=f"ffn{tag}_dwu")
    started = emit(f"ffn{tag}", {"gate": dwg, "up": dwu, "down": dwd})
    dhn = mm_nt(dgate, wg, tm=tm, tn=1024, tk=1408, out_dtype=F32, name=f"ffn{tag}_dhn_g")
    dhn = mm_nt(dup, wu, tm=tm, tn=1024, tk=1408, out_dtype=F32, res=dhn, name=f"ffn{tag}_dhn_u")
    dh_in, dnorm = rms_bwd(h, norm_g + started, dhn, dh, name=f"ffn{tag}_dnorm")
    return dh_in, dnorm


def _local_step(x, target, w, get, emit):
    t = x.shape[0]
    tm = min(1024, t)
    g = {}

    hn0 = rms_fwd(x, w["even_norm"], name="l0_norm")
    w.update(get("even_in", hn0))
    proj = mm_nn(hn0, w["even_w_in"], tm=tm, tn=512, tk=2048, out_dtype=F32, name="l0_w_in")
    out_a = att_fwd(proj, w["sinks"], name="l0_att")
    qkvn = dprep_fwd(proj, w["even_conv"], name="l0_prep")
    gates = gates_fwd(proj, w["a_log"], w["dt_bias"], name="l0_gates")
    o_delta, ssave = delta_fwd(qkvn, gates, name="l0_delta")
    out_b = gnorm_fwd(o_delta, proj, w["onorm"], name="l0_gnorm")
    mix0 = jnp.concatenate([out_a, out_b], axis=-1)
    w.update(get("even_out", mix0))
    h1 = mm_nn(mix0, w["even_w_out"], tm=tm, tn=512, tk=2048, out_dtype=F32, res=x, name="l0_w_out")
    f0 = get("ffn0", h1)
    h2, ffn0 = _ffn_fwd(h1, w["ffn_norm"][0:1] + f0["tok"], f0["gate"], f0["up"], f0["down"], tm, 0)
    hn2 = rms_fwd(h2, w["odd_norm"], name="l1_norm")
    w.update(get("odd", hn2))
    zpre = mm_nn(hn2, w["odd_w_in"], tm=tm, tn=1024, tk=2048, out_dtype=F32, name="l1_w_in")
    gated = gmlp_fwd(zpre, w["odd_ln_g"], w["odd_ln_b"], w["odd_w_s"], w["odd_b_s"], name="l1_gmlp")
    h3 = mm_nn(gated, w["odd_w_out"], tm=tm, tn=512, tk=2048, out_dtype=F32, res=h2, name="l1_w_out")
    f1 = get("ffn1", h3)
    h4, ffn1 = _ffn_fwd(h3, w["ffn_norm"][1:2] + f1["tok"], f1["gate"], f1["up"], f1["down"], tm, 1)
    loss, dh4, g["final_norm"] = loss_head(h4, w["final_norm"], target, name="loss_head")

    dh3, dn1 = _ffn_bwd(dh4, h3, w["ffn_norm"][1:2], f1["gate"], f1["up"], f1["down"], ffn1, tm, 1, emit)
    dw_out_o = mm_tn(gated, dh3, shards=1, tm=tm, tn=1024, tk=1024, out_dtype=BF16, name="l1_dw_out")[0]
    dgated = mm_nt(dh3, w["odd_w_out"], tm=tm, tn=512, tk=2048, out_dtype=BF16, name="l1_dgated")
    dzpre, g["odd_w_s"], g["odd_b_s"], g["odd_ln_g"], g["odd_ln_b"] = gmlp_bwd(
        zpre, dgated, w["odd_ln_g"], w["odd_ln_b"], w["odd_w_s"], w["odd_b_s"], name="l1_dgmlp")
    dw_in_o = mm_tn(hn2, dzpre, shards=N_SHARD, tm=tm, tn=1024, tk=1024, out_dtype=BF16, name="l1_dw_in")
    started = emit("odd", {"odd_w_in": dw_in_o, "odd_w_out": dw_out_o})
    dhn2 = mm_nt(dzpre, w["odd_w_in"], tm=tm, tn=1024, tk=1024, out_dtype=F32, name="l1_dhn")
    dh2, g["odd_norm"] = rms_bwd(h2, w["odd_norm"] + started, dhn2, dh3, name="l1_dnorm")
    dh1, dn0 = _ffn_bwd(dh2, h1, w["ffn_norm"][0:1], f0["gate"], f0["up"], f0["down"], ffn0, tm, 0, emit)
    g["ffn_norm"] = jnp.concatenate([dn0, dn1], axis=0)
    dw_out_e = mm_tn(mix0, dh1, shards=1, tm=tm, tn=1024, tk=1024, out_dtype=BF16, name="l0_dw_out")[0]
    started = emit("even_out", {"even_w_out": dw_out_e})
    dmix = mm_nt(dh1, w["even_w_out"], tm=tm, tn=512, tk=2048, out_dtype=F32, name="l0_dmix")
    dq_a, dkv_cur, dkv_prev, g["sinks"] = att_bwd(proj, w["sinks"] + started, dmix, name="l0_datt")
    dkv = dkv_cur + jnp.concatenate([dkv_prev[WINDOW:], jnp.zeros((WINDOW, 2 * A_KV), F32)], axis=0)
    do_delta, dz, g["onorm"] = gnorm_bwd(o_delta, proj, w["onorm"], dmix, dcol0=A_Q, name="l0_dgnorm")
    dqkvn, dgates = delta_bwd(qkvn, gates, ssave, do_delta, name="l0_ddelta")
    dqkv_b, g["even_conv"] = dprep_bwd(proj, w["even_conv"], dqkvn, name="l0_dprep")
    draw, g["a_log"], g["dt_bias"] = gates_bwd(proj, w["a_log"], w["dt_bias"], dgates, name="l0_dgates")
    dproj = jnp.concatenate([dq_a, dkv.astype(BF16), dqkv_b, dz, draw,
                             jnp.zeros((t, EVEN_IN_PAD - COL_GATE - 128), BF16)], axis=-1)
    dw_in_e = mm_tn(hn0, dproj, shards=1, tm=tm, tn=1408, tk=1024, out_dtype=BF16, name="l0_dw_in")[0]
    dhn0 = mm_nt(dproj, w["even_w_in"], tm=tm, tn=1024, tk=2816, out_dtype=F32, name="l0_dhn")
    grad_x, g["even_norm"] = rms_bwd(x, w["even_norm"], dhn0, dh1, name="l0_dnorm")
    emit("even_in", {"even_w_in": dw_in_e, "small": g})
    return loss, grad_x


ANY = pl.BlockSpec(memory_space=pl.ANY)
N_DEV = 8


def _place():
    return lax.axis_index("x"), lax.axis_index("y"), lax.axis_index("c")


def _chip_peers(x, y, c):
    return [((1 - x, y, c), 2 * (1 - x) + y), ((x, 1 - y, c), 2 * x + 1 - y), ((1 - x, 1 - y, c), 2 * (1 - x) + 1 - y)]


HBM = pl.BlockSpec(memory_space=pltpu.HBM)
SEM = pl.BlockSpec(memory_space=pltpu.SEMAPHORE)
EFFECT = pltpu.SideEffectType.DATAFLOW_SIDE_EFFECTING
N_PEER = 3


def _half(ref, c):
    r = ref.shape[0] // 2
    return ref.at[pl.ds(c * r, r)]


def _gather_plan(srcs, lands, send, recv):
    x, y, c = _place()
    return [pltpu.make_async_remote_copy(src_ref=_half(srcs[i], c), dst_ref=_half(lands[i].at[2 * x + y], c),
                                         send_sem=send.at[N_PEER * i + k], recv_sem=recv.at[N_PEER * i + k],
                                         device_id=peer, device_id_type=MESH_ID)
            for i in range(len(srcs)) for k, (peer, _) in enumerate(_chip_peers(x, y, c))]


def _relay_plan(srcs, lands, send, recv):
    x, y, c = _place()
    return [pltpu.make_async_remote_copy(src_ref=_half(lands[i].at[idx], c), dst_ref=_half(lands[i].at[idx], c),
                                         send_sem=send.at[N_PEER * i + k], recv_sem=recv.at[N_PEER * i + k],
                                         device_id=(x, y, 1 - c), device_id_type=MESH_ID)
            for i in range(len(srcs)) for k, (_, idx) in enumerate(_chip_peers(x, y, c))]


def _scatter_plan(srcs, lands, send, recv):
    x, y, c = _place()
    return [pltpu.make_async_remote_copy(src_ref=srcs[i].at[idx], dst_ref=lands[i].at[k], send_sem=send.at[N_PEER * i + k],
                                         recv_sem=recv.at[N_PEER * i + k], device_id=peer, device_id_type=MESH_ID)
            for i in range(len(srcs)) for k, (peer, idx) in enumerate(_chip_peers(x, y, c))]


def _swap_plan(srcs, lands, send, recv):
    x, y, c = _place()
    return [pltpu.make_async_remote_copy(src_ref=srcs[i], dst_ref=lands[i], send_sem=send.at[N_PEER * i],
                                         recv_sem=recv.at[N_PEER * i], device_id=(x, y, 1 - c), device_id_type=MESH_ID)
            for i in range(len(srcs))]


def copies_start(plan, srcs, lands, after, *, name):
    n = len(srcs)
    both = list(srcs) + list(lands)

    def body(*refs):
        src_refs, land_refs = refs[:n], refs[n:2 * n]
        send, recv = refs[2 * n + 1], refs[2 * n + 2]
        for cp in plan(src_refs, land_refs, send, recv):
            cp.start()
        refs[-1][...] = jnp.zeros_like(refs[-1])

    res = pl.pallas_call(
        body, name=name,
        out_shape=(pltpu.SemaphoreType.DMA((n * N_PEER,)), pltpu.SemaphoreType.DMA((n * N_PEER,)),
                   *[pltpu.HBM(a.shape, a.dtype) for a in both], jax.ShapeDtypeStruct((8, 128), F32)),
        in_specs=[HBM] * (2 * n) + [ANY],
        out_specs=(SEM, SEM, *[HBM] * (2 * n), pl.BlockSpec(memory_space=pltpu.VMEM)),
        input_output_aliases={i: 2 + i for i in range(2 * n)},
        compiler_params=pltpu.CompilerParams(has_side_effects=EFFECT))(
            *[pltpu.with_memory_space_constraint(a, pltpu.HBM) for a in both], after)
    return {"send": res[0], "recv": res[1], "srcs": list(res[2:2 + n]), "lands": list(res[2 + n:2 + 2 * n]),
            "token": res[-1]}


def copies_relay(arrived_plan, next_plan, started, after, *, name):
    srcs, lands = started["srcs"], started["lands"]
    n = len(srcs)
    both = srcs + lands

    def body(*refs):
        src_refs, land_refs = refs[:n], refs[n:2 * n]
        send1, recv1 = refs[2 * n], refs[2 * n + 1]
        send2, recv2 = refs[2 * n + 3], refs[2 * n + 4]
        for cp in arrived_plan(src_refs, land_refs, send1, recv1):
            cp.wait_send()
            cp.wait_recv()
        for cp in next_plan(src_refs, land_refs, send2, recv2):
            cp.start()
        refs[-1][...] = jnp.zeros_like(refs[-1])

    res = pl.pallas_call(
        body, name=name,
        out_shape=(pltpu.SemaphoreType.DMA((n * N_PEER,)), pltpu.SemaphoreType.DMA((n * N_PEER,)),
                   *[pltpu.HBM(a.shape, a.dtype) for a in both], jax.ShapeDtypeStruct((8, 128), F32)),
        in_specs=[HBM] * (2 * n) + [SEM, SEM, ANY],
        out_specs=(SEM, SEM, *[HBM] * (2 * n), pl.BlockSpec(memory_space=pltpu.VMEM)),
        input_output_aliases={i: 2 + i for i in range(2 * n)},
        compiler_params=pltpu.CompilerParams(has_side_effects=EFFECT))(*both, started["send"], started["recv"], after)
    return {"send": res[0], "recv": res[1], "srcs": list(res[2:2 + n]), "lands": list(res[2 + n:2 + 2 * n]),
            "token": res[-1]}


def copies_wait(plan, started, after, *, name):
    srcs, lands = started["srcs"], started["lands"]
    n = len(srcs)
    both = srcs + lands

    def body(*refs):
        src_refs, land_refs = refs[:n], refs[n:2 * n]
        send, recv = refs[2 * n], refs[2 * n + 1]
        for cp in plan(src_refs, land_refs, send, recv):
            cp.wait_send()
            cp.wait_recv()

    res = pl.pallas_call(
        body, name=name, out_shape=tuple(pltpu.HBM(a.shape, a.dtype) for a in both),
        in_specs=[HBM] * (2 * n) + [SEM, SEM, ANY], out_specs=(HBM,) * (2 * n),
        input_output_aliases={i: i for i in range(2 * n)},
        compiler_params=pltpu.CompilerParams(has_side_effects=EFFECT))(*both, started["send"], started["recv"], after)
    return list(res[:n]), list(res[n:])


def allgather_small(small, *, name):
    def body(small_ref, out_ref, send, recv, loc):
        x, y, c = _place()
        dev = 4 * x + 2 * y + c
        local = pltpu.make_async_copy(small_ref, out_ref.at[dev], loc)
        remote = []
        for r in range(1, N_DEV):
            fx, fy, fc = (r >> 2) & 1, (r >> 1) & 1, r & 1
            peer = (1 - x if fx else x, 1 - y if fy else y, 1 - c if fc else c)
            remote.append(pltpu.make_async_remote_copy(
                src_ref=small_ref, dst_ref=out_ref.at[dev], send_sem=send.at[r - 1], recv_sem=recv.at[r - 1],
                device_id=peer, device_id_type=MESH_ID))
        local.start()
        for cp in remote:
            cp.start()
        for cp in remote:
            cp.wait()
        local.wait()

    return pl.pallas_call(
        body, name=name, in_specs=[ANY], out_specs=ANY,
        out_shape=jax.ShapeDtypeStruct((N_DEV,) + small.shape, small.dtype),
        scratch_shapes=[pltpu.SemaphoreType.DMA((N_DEV - 1,)), pltpu.SemaphoreType.DMA((N_DEV - 1,)),
                        pltpu.SemaphoreType.DMA(())])(small)


def swap_cores(arrs, *, name):
    n = len(arrs)

    def body(*refs):
        ins, outs = refs[:n], refs[n:2 * n]
        send, recv = refs[2 * n:]
        x, y, c = _place()
        copies = [pltpu.make_async_remote_copy(src_ref=ins[i], dst_ref=outs[i], send_sem=send.at[i], recv_sem=recv.at[i],
                                               device_id=(x, y, 1 - c), device_id_type=MESH_ID) for i in range(n)]
        for cp in copies:
            cp.start()
        for cp in copies:
            cp.wait()

    return pl.pallas_call(
        body, name=name, in_specs=[ANY] * n, out_specs=[ANY] * n,
        out_shape=[jax.ShapeDtypeStruct(a.shape, a.dtype) for a in arrs],
        scratch_shapes=[pltpu.SemaphoreType.DMA((n,)), pltpu.SemaphoreType.DMA((n,))])(*arrs)


RED_ROWS = 128


def sum_chips(by_owner, me, got, *, name):
    _, r, c = by_owner.shape
    rb = RED_ROWS if r % RED_ROWS == 0 else r

    def body(me_ref, o_ref, a_ref, b_ref, c_ref, out_ref):
        out_ref[...] = ((o_ref[...].astype(F32) + a_ref[...].astype(F32)) + b_ref[...].astype(F32)) + c_ref[...].astype(F32)

    gk = lambda k: pl.BlockSpec((None, rb, c), lambda i, me_ref: (k, i, 0))
    grid_spec = pltpu.PrefetchScalarGridSpec(
        num_scalar_prefetch=1, grid=(r // rb,),
        in_specs=[pl.BlockSpec((None, rb, c), lambda i, me_ref: (me_ref[0], i, 0)), gk(0), gk(1), gk(2)],
        out_specs=pl.BlockSpec((rb, c), lambda i, me_ref: (i, 0)))
    return pl.pallas_call(
        body, name=name, grid_spec=grid_spec, out_shape=jax.ShapeDtypeStruct((r, c), F32),
        compiler_params=_params(("parallel",)))(me, by_owner, got, got, got)


def sum_devices(small_all, *, name):
    _, p, c = small_all.shape

    def body(a_ref, out_ref):
        acc = a_ref[0]
        for d in range(1, N_DEV):
            acc = acc + a_ref[d]
        out_ref[...] = acc

    return pl.pallas_call(
        body, name=name, grid=(1,), in_specs=[pl.BlockSpec((N_DEV, p, c), lambda i: (0, 0, 0))],
        out_specs=pl.BlockSpec((p, c), lambda i: (0, 0)), out_shape=jax.ShapeDtypeStruct((p, c), F32),
        compiler_params=_params(("arbitrary",)))(small_all)


def adamw(parts, w, m, v, *, name):
    nl, r, c = w.shape
    assert len(parts) == nl
    npart = len(parts[0])
    rb = RED_ROWS if r % RED_ROWS == 0 else r
    flat = [a for layer in parts for a in layer]

    def body(*refs):
        p_refs, (w_ref, m_ref, v_ref) = refs[:nl * npart], refs[nl * npart:nl * npart + 3]
        g_ref, d_ref, nm_ref, nv_ref = refs[nl * npart + 3:]
        layer = pl.program_id(0)
        grad = None
        for l in range(nl):
            gl = p_refs[l * npart][...]
            for j in range(1, npart):
                gl = gl + p_refs[l * npart + j][...]
            grad = gl if grad is None else jnp.where(layer == l, gl, grad)
        wv, mv, vv = w_ref[...], m_ref[...], v_ref[...]
        nm = ADAM_B1 * mv + (1.0 - ADAM_B1) * grad
        nv = ADAM_B2 * vv + (1.0 - ADAM_B2) * (grad * grad)
        m_hat = nm / (1.0 - ADAM_B1 ** ADAM_STEP)
        v_hat = nv / (1.0 - ADAM_B2 ** ADAM_STEP)
        g_ref[...] = grad
        d_ref[...] = -ADAM_LR * (m_hat / (jnp.sqrt(v_hat) + ADAM_EPS) + ADAM_WD * wv)
        nm_ref[...] = nm
        nv_ref[...] = nv

    pspec = pl.BlockSpec((rb, c), lambda l, i: (i, 0))
    wspec = pl.BlockSpec((None, rb, c), lambda l, i: (l, i, 0))
    osh = jax.ShapeDtypeStruct((nl, r, c), F32)
    return pl.pallas_call(
        body, name=name, grid=(nl, r // rb), in_specs=[pspec] * (nl * npart) + [wspec] * 3,
        out_specs=[wspec] * 4, out_shape=[osh] * 4, compiler_params=_params(("parallel", "parallel")))(*flat, w, m, v)


def _rows128(a):
    flat = a.reshape(-1)
    pad = (-flat.shape[0]) % 128
    return jnp.pad(flat, (0, pad)).reshape(-1, 128)


def _pack_rows(arrs, multiple=8):
    rows = jnp.concatenate([_rows128(a.astype(F32)) for a in arrs], axis=0)
    return jnp.pad(rows, ((0, (-rows.shape[0]) % multiple), (0, 0)))


def _unpack_rows(rows, shapes):
    out, r0 = [], 0
    for shp in shapes:
        size = 1
        for s in shp:
            size *= s
        nr = -(-size // 128)
        out.append(rows[r0:r0 + nr].reshape(-1)[:size].reshape(shp))
        r0 += nr
    return out


SMALL_LOCAL_GRADS = ["even_norm", "even_conv", "a_log", "dt_bias", "sinks", "onorm", "odd_norm", "odd_ln_g",
                     "odd_ln_b", "odd_w_s", "odd_b_s", "ffn_norm", "final_norm"]
BIG = ["even_w_in", "even_w_out", "odd_w_in", "odd_w_out", "ffn_w_gate", "ffn_w_up", "ffn_w_down"]
WEIGHTS = ["even_norm", "even_w_in", "even_conv", "even_a_log", "even_dt_bias", "even_sinks", "even_onorm",
           "even_w_out", "odd_norm", "odd_w_in", "odd_ln_g", "odd_ln_b", "odd_w_s", "odd_b_s", "odd_w_out",
           "ffn_norm", "ffn_w_gate", "ffn_w_up", "ffn_w_down", "final_norm"]
SMALL = [n for n in WEIGHTS if n not in BIG]


def kernel(x, even_norm, even_w_in, even_conv, even_a_log, even_dt_bias, even_sinks, even_onorm, even_w_out, odd_norm, odd_w_in, odd_ln_g, odd_ln_b, odd_w_s, odd_b_s, odd_w_out, ffn_norm, ffn_w_gate, ffn_w_up, ffn_w_down, final_norm, loss_target, m_even_norm, m_even_w_in, m_even_conv, m_even_a_log, m_even_dt_bias, m_even_sinks, m_even_onorm, m_even_w_out, m_odd_norm, m_odd_w_in, m_odd_ln_g, m_odd_ln_b, m_odd_w_s, m_odd_b_s, m_odd_w_out, m_ffn_norm, m_ffn_w_gate, m_ffn_w_up, m_ffn_w_down, m_final_norm, v_even_norm, v_even_w_in, v_even_conv, v_even_a_log, v_even_dt_bias, v_even_sinks, v_even_onorm, v_even_w_out, v_odd_norm, v_odd_w_in, v_odd_ln_g, v_odd_ln_b, v_odd_w_s, v_odd_b_s, v_odd_w_out, v_ffn_norm, v_ffn_w_gate, v_ffn_w_up, v_ffn_w_down, v_final_norm):
    args = dict(locals())
    wl = {n: args[n] for n in WEIGHTS}
    ml = {n: args["m_" + n] for n in WEIGHTS}
    vl = {n: args["v_" + n] for n in WEIGHTS}
    me = 2 * lax.axis_index("x") + lax.axis_index("y")

    def landing(a):
        return lax.dynamic_update_index_in_dim(lax.empty((N_SHARD,) + a.shape, a.dtype), a, me, 0)

    b16 = lambda *arrs: [a.astype(BF16) for a in arrs]
    gather_groups = {
        "even_in": b16(even_w_in[0]) + [_pack_rows([even_conv[0], odd_norm, odd_ln_g, odd_ln_b], multiple=16)],
        "even_out": b16(even_w_out[0]),
        "ffn0": b16(ffn_w_gate[0], ffn_w_up[0], ffn_w_down[0]),
        "odd": b16(odd_w_in[0], odd_w_out[0]),
        "ffn1": b16(ffn_w_gate[1], ffn_w_up[1], ffn_w_down[1]),
    }
    gathering, after = {}, even_norm
    for group, srcs in gather_groups.items():
        gathering[group] = copies_start(_gather_plan, srcs, [landing(a) for a in srcs], after,
                                        name=f"gather_{group}_start")
        after = gathering[group]["token"]

    order = list(gather_groups)
    relayed, kept = {}, {}

    def relay(group, behind):
        relayed[group] = copies_relay(_gather_plan, _relay_plan, gathering[group], behind,
                                      name=f"gather_{group}_relay")
        return relayed[group]["token"][0:1, 0:1]

    def get(group, behind):
        if group not in relayed:
            relay(group, behind)
        _, lands = copies_wait(_relay_plan, relayed[group], behind, name=f"gather_{group}_wait")
        nxt = order.index(group) + 1
        tok = relay(order[nxt], lands[0]) if nxt < len(order) else jnp.zeros((1, 1), F32)
        if group == "even_in":
            parts = zip(*[_unpack_rows(lands[1][s], [(CONV_K, 768), (1, 512), (1, 512), (1, 512)])
                          for s in range(N_SHARD)])
            conv, onorm, lng, lnb = [jnp.concatenate(p, axis=1) for p in parts]
            w_in = jnp.pad(jnp.transpose(lands[0], (1, 0, 2)).reshape(D_MODEL, EVEN_IN),
                           ((0, 0), (0, EVEN_IN_PAD - EVEN_IN)))
            kept["odd_ln_g"] = lng
            return {"even_w_in": w_in, "even_conv": conv + tok, "odd_norm": onorm, "odd_ln_b": lnb}
        if group == "even_out":
            return {"even_w_out": lands[0].reshape(D_MODEL, D_MODEL), "ffn_norm": ffn_norm + tok}
        if group == "odd":
            return {"odd_w_in": lands[0], "odd_w_out": lands[1].reshape(D_MODEL, D_MODEL),
                    "odd_ln_g": kept["odd_ln_g"] + tok}
        return {"gate": lands[0], "up": lands[1], "down": lands[2].reshape(D_FF, D_MODEL), "tok": tok}

    rows4 =lambda a: a.reshape(N_SHARD, a.shape[0] // N_SHARD, a.shape[1])
    scattering, small = {}, {}

    def emit(group, grads):
        behind = even_norm
        if group == "even_in":
            small["local"] = grads["small"]
            small["all"] = behind = allgather_small(_pack_rows([grads["small"][n] for n in SMALL_LOCAL_GRADS]),
                                                    name="allgather_small")
            srcs = [jnp.transpose(grads["even_w_in"][:, :EVEN_IN].reshape(D_MODEL, N_SHARD, EVEN_IN // N_SHARD),
                                  (1, 0, 2))]
        elif group == "even_out":
            srcs = [rows4(grads["even_w_out"])]
        elif group == "odd":
            srcs = [grads["odd_w_in"], rows4(grads["odd_w_out"])]
        else:
            srcs = [grads["gate"], grads["up"], rows4(grads["down"])]
        lands = [lax.empty((N_PEER,) + a.shape[1:], a.dtype) for a in srcs]
        scattering[group] = copies_start(_scatter_plan, srcs, lands, behind, name=f"scatter_{group}_start")
        return scattering[group]["token"][0:1, 0:1]

    pad816 = lambda a: jnp.pad(a, ((0, 0), (B_HEADS, 128 - 2 * B_HEADS)))
    w = {
        "even_norm": even_norm + after[0:1, 0:1],
        "a_log": pad816(even_a_log), "dt_bias": pad816(even_dt_bias),
        "sinks": jnp.pad(even_sinks, ((0, 0), (0, 128 - A_HEADS))),
        "onorm": even_onorm,
        "odd_w_s": odd_w_s[0],
        "odd_b_s": jnp.pad(odd_b_s[0].T, ((0, 0), (0, 128 - C_GROUPS))),
        "ffn_norm": ffn_norm,
        "final_norm": final_norm[None],
    }
    loss_l, grad_x = _local_step(x[0], loss_target[0], w, get, emit)
    loss = lax.psum(loss_l[0, 0], ("x", "y", "c"))

    me1 = me.reshape(1).astype(jnp.int32)
    swapping = {}

    def reduce_chips(group, behind):
        srcs, lands = copies_wait(_scatter_plan, scattering[group], behind, name=f"scatter_{group}_wait")
        partial = [sum_chips(srcs[i], me1, lands[i], name=f"sum_chips_{group}_{i}") for i in range(len(srcs))]
        swapping[group] = copies_start(_swap_plan, partial, [lax.empty(p.shape, p.dtype) for p in partial],
                                       even_norm, name=f"swap_{group}_start")
        return swapping[group]["token"]

    def swapped(group, behind):
        mine, theirs = copies_wait(_swap_plan, swapping[group], behind, name=f"swap_{group}_wait")
        return list(zip(mine, theirs))

    behind = scattering["even_in"]["token"]
    for group in ("ffn1", "odd", "ffn0", "even_out"):
        behind = reduce_chips(group, behind)
    sums = {group: swapped(group, behind) for group in ("ffn1", "odd", "ffn0", "even_out")}
    outs = {}
    parts_of = {"even_w_out": [sums["even_out"][0]], "odd_w_in": [sums["odd"][0]], "odd_w_out": [sums["odd"][1]],
                "ffn_w_gate": [sums["ffn0"][0], sums["ffn1"][0]], "ffn_w_up": [sums["ffn0"][1], sums["ffn1"][1]],
                "ffn_w_down": [sums["ffn0"][2], sums["ffn1"][2]]}
    for n in parts_of:
        outs[n] = adamw(parts_of[n], wl[n], ml[n], vl[n], name=f"adamw_{n}")
    behind = reduce_chips("even_in", outs["ffn_w_down"][1])
    outs["even_w_in"] = adamw([swapped("even_in", behind)[0]], wl["even_w_in"], ml["even_w_in"], vl["even_w_in"],
                              name="adamw_even_w_in")

    g = small["local"]
    small_sum = sum_devices(small["all"], name="sum_devices")
    sg = dict(zip(SMALL_LOCAL_GRADS, _unpack_rows(small_sum, [g[n].shape for n in SMALL_LOCAL_GRADS])))
    own_cols = lambda a, width: lax.dynamic_slice_in_dim(a, me * width, width, axis=a.ndim - 1)
    small_grads = {
        "even_norm": sg["even_norm"], "even_conv": own_cols(sg["even_conv"], 768)[None],
        "even_a_log": sg["a_log"][:, B_HEADS:2 * B_HEADS], "even_dt_bias": sg["dt_bias"][:, B_HEADS:2 * B_HEADS],
        "even_sinks": sg["sinks"][:, :A_HEADS], "even_onorm": sg["onorm"],
        "odd_norm": own_cols(sg["odd_norm"], 512), "odd_ln_g": own_cols(sg["odd_ln_g"], 512),
        "odd_ln_b": own_cols(sg["odd_ln_b"], 512), "odd_w_s": sg["odd_w_s"][None],
        "odd_b_s": sg["odd_b_s"][:, :C_GROUPS].T[None], "ffn_norm": sg["ffn_norm"], "final_norm": sg["final_norm"][0],
    }
    packed = [_pack_rows([d[n] for n in SMALL])[None] for d in (small_grads, wl, ml, vl)]
    small_out = adamw([(packed[0][0],)], packed[1], packed[2], packed[3], name="adamw_small")
    shapes = [wl[n].shape for n in SMALL]
    for j in range(4):
        for n, a in zip(SMALL, _unpack_rows(small_out[j][0], shapes)):
            outs.setdefault(n, [None] * 4)[j] = a

    return (loss, grad_x[None], *[outs[n][0] for n in WEIGHTS], *[outs[n][1] for n in WEIGHTS],
            *[outs[n][2] for n in WEIGHTS], *[outs[n][3] for n in WEIGHTS])
```

```python
import functools

import jax
import jax.numpy as jnp
from jax import lax
from jax.experimental import pallas as pl
from jax.experimental.pallas import tpu as pltpu

F32 = jnp.float32
BF16 = jnp.bfloat16
NEG_INF = float("-inf")

D_MODEL = 2048
A_HEADS, A_KV_HEADS, A_HEAD_DIM, WINDOW = 16, 2, 64, 128
B_HEADS, B_HEAD_DIM, CONV_K, DN_CHUNK = 8, 128, 4, 64
C_GROUPS, C_CHUNK = 8, 128
C_GROUP_DIM = D_MODEL // C_GROUPS
D_FF = 5632
EPS = 1e-6
A_Q = A_HEADS * A_HEAD_DIM
A_KV = A_KV_HEADS * A_HEAD_DIM
B_W = B_HEADS * B_HEAD_DIM
EVEN_IN = A_Q + 2 * A_KV + 4 * B_W + 2 * B_HEADS
EVEN_IN_PAD = 5632
COL_KV = A_Q
COL_QKVB = A_Q + 2 * A_KV
COL_Z = COL_QKVB + 3 * B_W
COL_GATE = COL_Z + B_W
N_SHARD = 4

ADAM_LR, ADAM_B1, ADAM_B2, ADAM_EPS, ADAM_WD, ADAM_STEP = 0.001, 0.9, 0.999, 1e-08, 0.01, 10

VMEM_LIMIT_V7X = 56 * 1024 * 1024
MESH_ID = pl.DeviceIdType.MESH


def _params(sem=None):
    return pltpu.CompilerParams(dimension_semantics=sem, vmem_limit_bytes=VMEM_LIMIT_V7X)


def _sigmoid(x):
    return 1.0 / (1.0 + jnp.exp(-x))


def _silu(x):
    return x * _sigmoid(x)


def _dsilu(x):
    s = _sigmoid(x)
    return s * (1.0 + x * (1.0 - s))


def _gelu(x):
    return 0.5 * x * (1.0 + lax.erf(x * 0.7071067811865476))


def _dgelu(x):
    return 0.5 * (1.0 + lax.erf(x * 0.7071067811865476)) + x * jnp.exp(-0.5 * x * x) * 0.3989422804014327


def _dot(a, b, dims):
    if a.ndim == 3:
        (ca,), (cb,) = dims
        return lax.dot_general(a, b, (((ca + 1,), (cb + 1,)), ((0,), (0,))), preferred_element_type=F32)
    return lax.dot_general(a, b, (dims, ((), ())), preferred_element_type=F32)


NN = ((1,), (0,))
NT = ((1,), (1,))
TN = ((0,), (0,))


def _as3(b):
    return b if b.ndim == 3 else b[None]


def _accumulate(step, nsteps, accs, products, finish):
    if nsteps == 1:
        finish(products())
        return

    @pl.when(step == 0)
    def _():
        for acc, p in zip(accs, products()):
            acc[...] = p

    if nsteps > 2:
        @pl.when((step > 0) & (step < nsteps - 1))
        def _():
            for acc, p in zip(accs, products()):
                acc[...] += p

    @pl.when(step == nsteps - 1)
    def _():
        finish(tuple(acc[...] + p for acc, p in zip(accs, products())))


def mm_nn(a, b, *, tm, tn, tk, out_dtype, name, res=None, act=None):
    b3 = _as3(b)
    m, k = a.shape
    s, k2, ns = b3.shape
    assert k2 == k and m % tm == 0 and ns % tn == 0 and k % tk == 0, (a.shape, b3.shape, tm, tn, tk)
    nps, nk = ns // tn, k // tk

    def body(*refs):
        if res is None:
            a_ref, b_ref, o_ref, acc = refs
        else:
            a_ref, b_ref, r_ref, o_ref, acc = refs
        def finish(tiles):
            r = tiles[0] if res is None else tiles[0] + r_ref[...].astype(F32)
            o_ref[...] = r.astype(out_dtype)

        _accumulate(pl.program_id(2), nk, (acc,),
                    lambda: (_dot(a_ref[...].astype(BF16), b_ref[...].astype(BF16), NN),), finish)

    in_specs = [pl.BlockSpec((tm, tk), lambda i, j, kk: (i, kk)),
                pl.BlockSpec((None, tk, tn), lambda i, j, kk: (j // nps, kk, j % nps))]
    args = [a, b3]
    if res is not None:
        in_specs.append(pl.BlockSpec((tm, tn), lambda i, j, kk: (i, j)))
        args.append(res)
    return pl.pallas_call(
        body, name=name, grid=(m // tm, s * nps, nk), in_specs=in_specs,
        out_specs=pl.BlockSpec((tm, tn), lambda i, j, kk: (i, j)),
        out_shape=jax.ShapeDtypeStruct((m, s * ns), out_dtype),
        scratch_shapes=[pltpu.VMEM((tm, tn), F32)],
        compiler_params=_params(("parallel", "parallel", "arbitrary")))(*args)


def mm_nt(a, b, *, tm, tn, tk, out_dtype, name, res=None):
    b3 = _as3(b)
    m, n = a.shape
    s, k, ns = b3.shape
    assert n == s * ns and m % tm == 0 and k % tn == 0 and ns % tk == 0, (a.shape, b3.shape, tm, tn, tk)
    rps = ns // tk
    nr = s * rps

    def body(*refs):
        if res is None:
            a_ref, b_ref, o_ref, acc = refs
        else:
            a_ref, b_ref, r_ref, o_ref, acc = refs
        def finish(tiles):
            r = tiles[0] if res is None else tiles[0] + r_ref[...].astype(F32)
            o_ref[...] = r.astype(out_dtype)

        _accumulate(pl.program_id(2), nr, (acc,),
                    lambda: (_dot(a_ref[...].astype(BF16), b_ref[...].astype(BF16), NT),), finish)

    in_specs = [pl.BlockSpec((tm, tk), lambda i, j, r: (i, r)),
                pl.BlockSpec((None, tn, tk), lambda i, j, r: (r // rps, j, r % rps))]
    args = [a, b3]
    if res is not None:
        in_specs.append(pl.BlockSpec((tm, tn), lambda i, j, r: (i, j)))
        args.append(res)
    return pl.pallas_call(
        body, name=name, grid=(m // tm, k // tn, nr), in_specs=in_specs,
        out_specs=pl.BlockSpec((tm, tn), lambda i, j, r: (i, j)),
        out_shape=jax.ShapeDtypeStruct((m, k), out_dtype),
        scratch_shapes=[pltpu.VMEM((tm, tn), F32)],
        compiler_params=_params(("parallel", "parallel", "arbitrary")))(*args)


def mm_tn(a, b, *, shards, tm, tn, tk, out_dtype, name):
    m, k = a.shape
    m2, n = b.shape
    ns = n // shards
    assert m2 == m and n == shards * ns and m % tm == 0 and k % tk == 0 and ns % tn == 0, (a.shape, b.shape)
    nps, nm = ns // tn, m // tm

    def body(a_ref, b_ref, o_ref, acc):
        def finish(tiles):
            o_ref[...] = tiles[0].astype(out_dtype)

        _accumulate(pl.program_id(2), nm, (acc,),
                    lambda: (_dot(a_ref[...].astype(BF16), b_ref[...].astype(BF16), TN),), finish)

    return pl.pallas_call(
        body, name=name, grid=(k // tk, shards * nps, nm),
        in_specs=[pl.BlockSpec((tm, tk), lambda i, j, mi: (mi, i)),
                  pl.BlockSpec((tm, tn), lambda i, j, mi: (mi, j))],
        out_specs=pl.BlockSpec((None, tk, tn), lambda i, j, mi: (j // nps, i, j % nps)),
        out_shape=jax.ShapeDtypeStruct((shards, k, ns), out_dtype),
        scratch_shapes=[pltpu.VMEM((tk, tn), F32)],
        compiler_params=_params(("parallel", "parallel", "arbitrary")))(a, b)


def mm_gate_up(hn, wg, wu, *, tm, tn, tk, name):
    wg3, wu3 = _as3(wg), _as3(wu)
    m, k = hn.shape
    s, _, ns = wg3.shape
    assert m % tm == 0 and ns % tn == 0 and k % tk == 0
    nps, nk = ns // tn, k // tk

    def body(a_ref, g_ref, u_ref, og_ref, ou_ref, oa_ref, accg, accu):
        def products():
            a = a_ref[...].astype(BF16)
            return _dot(a, g_ref[...].astype(BF16), NN), _dot(a, u_ref[...].astype(BF16), NN)

        def finish(tiles):
            g, u = tiles
            og_ref[...] = g.astype(BF16)
            ou_ref[...] = u.astype(BF16)
            oa_ref[...] = (_silu(g) * u).astype(BF16)

        _accumulate(pl.program_id(2), nk, (accg, accu), products, finish)

    wspec = pl.BlockSpec((None, tk, tn), lambda i, j, kk: (j // nps, kk, j % nps))
    ospec = pl.BlockSpec((tm, tn), lambda i, j, kk: (i, j))
    osh = jax.ShapeDtypeStruct((m, s * ns), BF16)
    return pl.pallas_call(
        body, name=name, grid=(m // tm, s * nps, nk),
        in_specs=[pl.BlockSpec((tm, tk), lambda i, j, kk: (i, kk)), wspec, wspec],
        out_specs=[ospec, ospec, ospec], out_shape=[osh, osh, osh],
        scratch_shapes=[pltpu.VMEM((tm, tn) if nk > 1 else (8, 128), F32)] * 2,
        compiler_params=_params(("parallel", "parallel", "arbitrary")))(hn, wg3, wu3)


def mm_down_bwd(dh, wd, gate, up, *, tm, tn, tk, name):
    m, d = dh.shape
    f, d2 = wd.shape
    assert d2 == d and m % tm == 0 and f % tn == 0 and d % tk == 0
    nr = d // tk

    def body(a_ref, b_ref, g_ref, u_ref, og_ref, ou_ref, acc):
        def finish(tiles):
            da = tiles[0]
            g, u = g_ref[...].astype(F32), u_ref[...].astype(F32)
            s = _sigmoid(g)
            og_ref[...] = (da * u * (s * (1.0 + g * (1.0 - s)))).astype(BF16)
            ou_ref[...] = (da * (g * s)).astype(BF16)

        _accumulate(pl.program_id(2), nr, (acc,),
                    lambda: (_dot(a_ref[...].astype(BF16), b_ref[...].astype(BF16), NT),), finish)

    ospec = pl.BlockSpec((tm, tn), lambda i, j, r: (i, j))
    osh = jax.ShapeDtypeStruct((m, f), BF16)
    return pl.pallas_call(
        body, name=name, grid=(m // tm, f // tn, nr),
        in_specs=[pl.BlockSpec((tm, tk), lambda i, j, r: (i, r)),
                  pl.BlockSpec((tn, tk), lambda i, j, r: (j, r)), ospec, ospec],
        out_specs=[ospec, ospec], out_shape=[osh, osh],
        scratch_shapes=[pltpu.VMEM((tm, tn), F32)],
        compiler_params=_params(("parallel", "parallel", "arbitrary")))(dh, wd, gate, up)


ROWS = 256


def rms_fwd(x, g, *, name):
    t, d = x.shape

    def body(x_ref, g_ref, o_ref):
        xv = x_ref[...]
        r = lax.rsqrt(jnp.mean(xv * xv, axis=-1, keepdims=True) + EPS)
        o_ref[...] = (xv * r * g_ref[...]).astype(BF16)

    return pl.pallas_call(
        body, name=name, grid=(t // ROWS,),
        in_specs=[pl.BlockSpec((ROWS, d), lambda i: (i, 0)), pl.BlockSpec((1, d), lambda i: (0, 0))],
        out_specs=pl.BlockSpec((ROWS, d), lambda i: (i, 0)),
        out_shape=jax.ShapeDtypeStruct((t, d), BF16), compiler_params=_params(("parallel",)))(x, g)


def rms_bwd(x, g, dy, dres, *, name):
    t, d = x.shape

    def body(x_ref, g_ref, dy_ref, dr_ref, dx_ref, dg_ref):
        @pl.when(pl.program_id(0) == 0)
        def _():
            dg_ref[...] = jnp.zeros_like(dg_ref)

        xv, dyv = x_ref[...], dy_ref[...].astype(F32)
        r = lax.rsqrt(jnp.mean(xv * xv, axis=-1, keepdims=True) + EPS)
        dyg = dyv * g_ref[...]
        dx = r * dyg - xv * (r * r * r) * jnp.mean(dyg * xv, axis=-1, keepdims=True)
        dx_ref[...] = dx + dr_ref[...]
        dg_ref[...] += jnp.sum(dyv * xv * r, axis=0, keepdims=True)

    row = pl.BlockSpec((ROWS, d), lambda i: (i, 0))
    vec = pl.BlockSpec((1, d), lambda i: (0, 0))
    return pl.pallas_call(
        body, name=name, grid=(t // ROWS,), in_specs=[row, vec, row, row], out_specs=[row, vec],
        out_shape=[jax.ShapeDtypeStruct((t, d), F32), jax.ShapeDtypeStruct((1, d), F32)],
        compiler_params=_params(("arbitrary",)))(x, g, dy, dres)


def loss_head(h, g, target, *, name):
    t, d = h.shape

    def body(x_ref, g_ref, t_ref, loss_ref, dx_ref, dg_ref):
        @pl.when(pl.program_id(0) == 0)
        def _():
            dg_ref[...] = jnp.zeros_like(dg_ref)
            loss_ref[...] = jnp.zeros_like(loss_ref)

        xv, gv = x_ref[...], g_ref[...]
        r = lax.rsqrt(jnp.mean(xv * xv, axis=-1, keepdims=True) + EPS)
        e = xv * r * gv - t_ref[...]
        loss_ref[...] += 0.5 * jnp.sum(jnp.mean(e * e, axis=-1, keepdims=True), axis=0, keepdims=True)
        dyv = e * (1.0 / d)
        dyg = dyv * gv
        dx_ref[...] = r * dyg - xv * (r * r * r) * jnp.mean(dyg * xv, axis=-1, keepdims=True)
        dg_ref[...] += jnp.sum(dyv * xv * r, axis=0, keepdims=True)

    row = pl.BlockSpec((ROWS, d), lambda i: (i, 0))
    vec = pl.BlockSpec((1, d), lambda i: (0, 0))
    return pl.pallas_call(
        body, name=name, grid=(t // ROWS,), in_specs=[row, vec, row],
        out_specs=[pl.BlockSpec((1, 128), lambda i: (0, 0)), row, vec],
        out_shape=[jax.ShapeDtypeStruct((1, 128), F32), jax.ShapeDtypeStruct((t, d), F32),
                   jax.ShapeDtypeStruct((1, d), F32)],
        compiler_params=_params(("arbitrary",)))(h, g, target)


def _tril_mask():
    r = lax.broadcasted_iota(jnp.int32, (C_CHUNK, C_CHUNK), 0)
    c = lax.broadcasted_iota(jnp.int32, (C_CHUNK, C_CHUNK), 1)
    return r >= c


def _layer_norm_parts(v):
    mu = jnp.mean(v, axis=-1, keepdims=True)
    vc = v - mu
    rstd = lax.rsqrt(jnp.mean(vc * vc, axis=-1, keepdims=True) + EPS)
    return vc * rstd, rstd


def gmlp_fwd(zpre, ln_g, ln_b, ws, bs_t, *, name):
    t = zpre.shape[0]
    d = D_MODEL

    def body(zu_ref, zv_ref, g_ref, b_ref, ws_ref, bs_ref, o_ref):
        u = _gelu(zu_ref[...])
        vhat, _ = _layer_norm_parts(_gelu(zv_ref[...]))
        vln = (vhat * g_ref[...] + b_ref[...]).astype(BF16)
        mask = _tril_mask()
        for gi in range(C_GROUPS):
            sl = slice(gi * C_GROUP_DIM, (gi + 1) * C_GROUP_DIM)
            w = jnp.where(mask, ws_ref[gi], 0.0).astype(BF16)
            mixed = _dot(w, vln[:, sl], NN) + bs_ref[:, gi:gi + 1]
            o_ref[:, sl] = (u[:, sl] * mixed).astype(BF16)

    vec = pl.BlockSpec((1, d), lambda i: (0, 0))
    return pl.pallas_call(
        body, name=name, grid=(t // C_CHUNK,),
        in_specs=[pl.BlockSpec((C_CHUNK, d), lambda i: (i, 0)), pl.BlockSpec((C_CHUNK, d), lambda i: (i, 1)),
                  vec, vec, pl.BlockSpec((C_GROUPS, C_CHUNK, C_CHUNK), lambda i: (0, 0, 0)),
                  pl.BlockSpec((C_CHUNK, 128), lambda i: (0, 0))],
        out_specs=pl.BlockSpec((C_CHUNK, d), lambda i: (i, 0)),
        out_shape=jax.ShapeDtypeStruct((t, d), BF16), compiler_params=_params(("parallel",)))(
            zpre, zpre, ln_g, ln_b, ws, bs_t)


def gmlp_bwd(zpre, dgated, ln_g, ln_b, ws, bs_t, *, name):
    t = zpre.shape[0]
    d = D_MODEL

    def body(zu_ref, zv_ref, dg_ref, g_ref, b_ref, ws_ref, bs_ref, dz_ref, dws_ref, dbs_ref, dlg_ref, dlb_ref):
        @pl.when(pl.program_id(0) == 0)
        def _():
            dws_ref[...] = jnp.zeros_like(dws_ref)
            dbs_ref[...] = jnp.zeros_like(dbs_ref)
            dlg_ref[...] = jnp.zeros_like(dlg_ref)
            dlb_ref[...] = jnp.zeros_like(dlb_ref)

        zu, zv = zu_ref[...], zv_ref[...]
        u = _gelu(zu)
        vhat, rstd = _layer_norm_parts(_gelu(zv))
        gam = g_ref[...]
        vln = (vhat * gam + b_ref[...]).astype(BF16)
        dgt = dg_ref[...].astype(F32)
        mask = _tril_mask()
        lane = lax.broadcasted_iota(jnp.int32, (C_CHUNK, 128), 1)
        dbs = jnp.zeros((C_CHUNK, 128), F32)
        du_parts, dvln_parts = [], []
        for gi in range(C_GROUPS):
            sl = slice(gi * C_GROUP_DIM, (gi + 1) * C_GROUP_DIM)
            w = jnp.where(mask, ws_ref[gi], 0.0).astype(BF16)
            mixed = _dot(w, vln[:, sl], NN) + bs_ref[:, gi:gi + 1]
            du_parts.append(dgt[:, sl] * mixed)
            dmixed = dgt[:, sl] * u[:, sl]
            dmb = dmixed.astype(BF16)
            dws_ref[gi] += jnp.where(mask, _dot(dmb, vln[:, sl], NT), 0.0)
            dbs = dbs + jnp.where(lane == gi, jnp.sum(dmixed, axis=-1, keepdims=True), 0.0)
            dvln_parts.append(_dot(w, dmb, TN))
        dbs_ref[...] += dbs
        du = jnp.concatenate(du_parts, axis=-1)
        dvln = jnp.concatenate(dvln_parts, axis=-1)
        dlg_ref[...] += jnp.sum(dvln * vhat, axis=0, keepdims=True)
        dlb_ref[...] += jnp.sum(dvln, axis=0, keepdims=True)
        dvhat = dvln * gam
        dv = rstd * (dvhat - jnp.mean(dvhat, axis=-1, keepdims=True)
                     - vhat * jnp.mean(dvhat * vhat, axis=-1, keepdims=True))
        dz_ref[:, :d] = (du * _dgelu(zu)).astype(BF16)
        dz_ref[:, d:] = (dv * _dgelu(zv)).astype(BF16)

    vec = pl.BlockSpec((1, d), lambda i: (0, 0))
    wsp = pl.BlockSpec((C_GROUPS, C_CHUNK, C_CHUNK), lambda i: (0, 0, 0))
    bsp = pl.BlockSpec((C_CHUNK, 128), lambda i: (0, 0))
    return pl.pallas_call(
        body, name=name, grid=(t // C_CHUNK,),
        in_specs=[pl.BlockSpec((C_CHUNK, d), lambda i: (i, 0)), pl.BlockSpec((C_CHUNK, d), lambda i: (i, 1)),
                  pl.BlockSpec((C_CHUNK, d), lambda i: (i, 0)), vec, vec, wsp, bsp],
        out_specs=[pl.BlockSpec((C_CHUNK, 2 * d), lambda i: (i, 0)), wsp, bsp, vec, vec],
        out_shape=[jax.ShapeDtypeStruct((t, 2 * d), BF16), jax.ShapeDtypeStruct((C_GROUPS, C_CHUNK, C_CHUNK), F32),
                   jax.ShapeDtypeStruct((C_CHUNK, 128), F32), jax.ShapeDtypeStruct((1, d), F32),
                   jax.ShapeDtypeStruct((1, d), F32)],
        compiler_params=_params(("arbitrary",)))(zpre, zpre, dgated, ln_g, ln_b, ws, bs_t)


ATT_SCALE = A_HEAD_DIM ** -0.5
PAIRS = A_HEADS // 2
PAIRS_PER_KV = PAIRS // A_KV_HEADS


def _att_padded(tile):
    lo = lax.broadcasted_iota(jnp.int32, tile.shape, 1) < A_HEAD_DIM
    rolled = pltpu.roll(tile, A_HEAD_DIM, 1)
    zero = jnp.zeros_like(tile)
    return {(0, 0): jnp.where(lo, tile, zero).astype(BF16), (0, 1): jnp.where(lo, zero, rolled).astype(BF16),
            (1, 0): jnp.where(lo, rolled, zero).astype(BF16), (1, 1): jnp.where(lo, zero, tile).astype(BF16)}


def _att_valid(n):
    r = lax.broadcasted_iota(jnp.int32, (WINDOW, 2 * WINDOW), 0)
    c = lax.broadcasted_iota(jnp.int32, (WINDOW, 2 * WINDOW), 1)
    rel = r + WINDOW - c
    return (rel >= 0) & (rel < WINDOW) & ((c >= WINDOW) | (n > 0))


def _att_probs(qp, kpad, sink, valid):
    s = jnp.where(valid, _dot(qp, kpad, NT), NEG_INF)
    m = jnp.maximum(jnp.max(s, axis=-1, keepdims=True), sink)
    p = jnp.exp(s - m)
    e_sink = jnp.exp(sink - m)
    inv = 1.0 / (jnp.sum(p, axis=-1, keepdims=True) + e_sink)
    return p * inv, e_sink * inv


def _att_operands(q_ref, kvc_ref, kvp_ref, s_ref):
    kv = jnp.concatenate([kvp_ref[...], kvc_ref[...]], axis=0)
    kpad, vpad = _att_padded(kv[:, :128]), _att_padded(kv[:, 128:])
    key = lambda h: ((h // 2) // PAIRS_PER_KV, h % 2)
    pairs = [(q_ref[:, j * 128:(j + 1) * 128] * ATT_SCALE).astype(BF16) for j in range(PAIRS)]
    q = jnp.stack([pairs[h // 2] for h in range(A_HEADS)])
    k = jnp.stack([kpad[key(h)] for h in range(A_HEADS)])
    v = jnp.stack([vpad[key(h)] for h in range(A_HEADS)])
    sink = jnp.stack([s_ref[:, h:h + 1] for h in range(A_HEADS)])
    return q, k, v, sink


def _att_specs(t):
    return [pl.BlockSpec((WINDOW, A_Q), lambda n: (n, 0)),
            pl.BlockSpec((WINDOW, 2 * A_KV), lambda n: (n, COL_KV // (2 * A_KV))),
            pl.BlockSpec((WINDOW, 2 * A_KV), lambda n: (jnp.maximum(n - 1, 0), COL_KV // (2 * A_KV))),
            pl.BlockSpec((1, 128), lambda n: (0, 0))]


def att_fwd(proj, sinks, *, name):
    t = proj.shape[0]

    def body(q_ref, kvc_ref, kvp_ref, s_ref, o_ref):
        n = pl.program_id(0)
        q, k, v, sink = _att_operands(q_ref, kvc_ref, kvp_ref, s_ref)
        w, _ = _att_probs(q, k, sink, _att_valid(n))
        o = _dot(w.astype(BF16), v, NN)
        for j in range(PAIRS):
            o_ref[:, j * 128:(j + 1) * 128] = (o[2 * j] + o[2 * j + 1]).astype(BF16)

    return pl.pallas_call(
        body, name=name, grid=(t // WINDOW,), in_specs=_att_specs(t),
        out_specs=pl.BlockSpec((WINDOW, A_Q), lambda n: (n, 0)),
        out_shape=jax.ShapeDtypeStruct((t, A_Q), BF16), compiler_params=_params(("parallel",)))(
            proj, proj, proj, sinks)


def att_bwd(proj, sinks, dout, *, name):
    t = proj.shape[0]

    def body(q_ref, kvc_ref, kvp_ref, s_ref, do_ref, dq_ref, dkc_ref, dkp_ref, ds_ref):
        n = pl.program_id(0)

        @pl.when(n == 0)
        def _():
            ds_ref[...] = jnp.zeros_like(ds_ref)

        q, k, v, sink = _att_operands(q_ref, kvc_ref, kvp_ref, s_ref)
        dop = jnp.stack([do_ref[:, (h // 2) * 128:(h // 2 + 1) * 128] for h in range(A_HEADS)]).astype(BF16)
        w, w_sink = _att_probs(q, k, sink, _att_valid(n))
        dw = _dot(dop, v, NT)
        delta = jnp.sum(w * dw, axis=-1, keepdims=True)
        dsc = (w * (dw - delta)).astype(BF16)
        dsink_h = -jnp.sum(w_sink * delta, axis=1, keepdims=True)
        dq = _dot(dsc, k, NN)
        dk_h = _dot(dsc, q, TN)
        dv_h = _dot(w.astype(BF16), dop, TN)
        lane = lax.broadcasted_iota(jnp.int32, (1, 128), 1)
        dsink = jnp.zeros((1, 128), F32)
        for h in range(A_HEADS):
            dsink = dsink + jnp.where(lane == h, dsink_h[h], 0.0)
        ds_ref[...] += dsink
        for j in range(PAIRS):
            dq_ref[:, j * 128:(j + 1) * 128] = ((dq[2 * j] + dq[2 * j + 1]) * ATT_SCALE).astype(BF16)
        lo = lax.broadcasted_iota(jnp.int32, (2 * WINDOW, 128), 1) < A_HEAD_DIM
        heads_per_kv = A_HEADS // A_KV_HEADS

        def tile(per_head):
            acc = {}
            for kvh in range(A_KV_HEADS):
                for half in range(2):
                    hs = range(kvh * heads_per_kv + half, (kvh + 1) * heads_per_kv, 2)
                    acc[(kvh, half)] = functools.reduce(lambda a, b: a + b, [per_head[h] for h in hs])
            return jnp.where(lo, acc[(0, 0)] + pltpu.roll(acc[(0, 1)], A_HEAD_DIM, 1),
                             pltpu.roll(acc[(1, 0)], A_HEAD_DIM, 1) + acc[(1, 1)])

        dkv = jnp.concatenate([tile(dk_h), tile(dv_h)], axis=1)
        dkp_ref[...] = dkv[:WINDOW]
        dkc_ref[...] = dkv[WINDOW:]

    kvo = pl.BlockSpec((WINDOW, 2 * A_KV), lambda n: (n, 0))
    return pl.pallas_call(
        body, name=name, grid=(t // WINDOW,),
        in_specs=_att_specs(t) + [pl.BlockSpec((WINDOW, A_Q), lambda n: (n, 0))],
        out_specs=[pl.BlockSpec((WINDOW, A_Q), lambda n: (n, 0)), kvo, kvo, pl.BlockSpec((1, 128), lambda n: (0, 0))],
        out_shape=[jax.ShapeDtypeStruct((t, A_Q), BF16), jax.ShapeDtypeStruct((t, 2 * A_KV), F32),
                   jax.ShapeDtypeStruct((t, 2 * A_KV), F32), jax.ShapeDtypeStruct((1, 128), F32)],
        compiler_params=_params(("arbitrary",)))(proj, proj, proj, sinks, dout)


QK_SCALE = B_HEAD_DIM ** -0.5
PREP_COLS = 256
PREP_NCB = 3 * B_W // PREP_COLS
HALO = 8
PREP_ROWS = 512


def _roll_rows(x, shift):
    n = x.shape[0]
    return x if shift % n == 0 else pltpu.roll(x, shift % n, 0)


def _conv_taps(xe, w):
    xs = [_roll_rows(xe, CONV_K - 1 - i) for i in range(CONV_K)]
    c = w[0:1] * xs[0]
    for i in range(1, CONV_K):
        c = c + w[i:i + 1] * xs[i]
    return xs, c


def dprep_fwd(proj, conv_w, *, name):
    t = proj.shape[0]
    tt = min(PREP_ROWS, t)
    col0 = COL_QKVB // PREP_COLS

    def body(x_ref, h_ref, w_ref, o_ref):
        cb, n = pl.program_id(0), pl.program_id(1)
        halo = jnp.where(n > 0, h_ref[...], 0.0)
        xe = jnp.concatenate([halo, x_ref[...]], axis=0)
        _, c = _conv_taps(xe, w_ref[...])
        y = _silu(c)[HALO:]
        parts = []
        for hh in range(PREP_COLS // B_HEAD_DIM):
            yh = y[:, hh * B_HEAD_DIM:(hh + 1) * B_HEAD_DIM]
            parts.append(yh * lax.rsqrt(jnp.sum(yh * yh, axis=-1, keepdims=True) + EPS))
        nrm = jnp.concatenate(parts, axis=-1)
        o_ref[...] = jnp.where(cb < 4, nrm * QK_SCALE, jnp.where(cb < 8, nrm, y))

    return pl.pallas_call(
        body, name=name, grid=(PREP_NCB, t // tt),
        in_specs=[pl.BlockSpec((tt, PREP_COLS), lambda cb, n: (n, col0 + cb)),
                  pl.BlockSpec((HALO, PREP_COLS), lambda cb, n: (jnp.maximum(n * (tt // HALO) - 1, 0), col0 + cb)),
                  pl.BlockSpec((CONV_K, PREP_COLS), lambda cb, n: (0, cb))],
        out_specs=pl.BlockSpec((tt, PREP_COLS), lambda cb, n: (n, cb)),
        out_shape=jax.ShapeDtypeStruct((t, 3 * B_W), F32), compiler_params=_params(("parallel", "parallel")))(
            proj, proj, conv_w)


def dprep_bwd(proj, conv_w, dqkvn, *, name):
    t = proj.shape[0]
    tt = min(PREP_ROWS, t)
    nb = t // tt
    col0 = COL_QKVB // PREP_COLS
    n8 = t // HALO

    def body(xc_ref, xb_ref, xa_ref, dc_ref, da_ref, w_ref, dx_ref, dw_ref):
        cb, n = pl.program_id(0), pl.program_id(1)

        @pl.when(n == 0)
        def _():
            dw_ref[...] = jnp.zeros_like(dw_ref)

        w = w_ref[...]
        xe = jnp.concatenate([jnp.where(n > 0, xb_ref[...], 0.0), xc_ref[...], xa_ref[...]], axis=0)
        xs, c = _conv_taps(xe, w)
        sg = _sigmoid(c)
        y = c * sg
        dout = jnp.concatenate([jnp.zeros((HALO, PREP_COLS), F32), dc_ref[...],
                                jnp.where(n < nb - 1, da_ref[...], 0.0)], axis=0)
        dsc = jnp.where(cb < 4, QK_SCALE, 1.0)
        parts = []
        for hh in range(PREP_COLS // B_HEAD_DIM):
            sl = slice(hh * B_HEAD_DIM, (hh + 1) * B_HEAD_DIM)
            yh, doh = y[:, sl], dout[:, sl] * dsc
            r = lax.rsqrt(jnp.sum(yh * yh, axis=-1, keepdims=True) + EPS)
            parts.append(doh * r - yh * (r * r * r) * jnp.sum(doh * yh, axis=-1, keepdims=True))
        dy = jnp.where(cb < 8, jnp.concatenate(parts, axis=-1), dout)
        dcv = dy * sg * (1.0 + c * (1.0 - sg))
        dxe = w[CONV_K - 1:CONV_K] * dcv
        for i in range(CONV_K - 1):
            dxe = dxe + w[i:i + 1] * _roll_rows(dcv, -(CONV_K - 1 - i))
        dx_ref[...] = dxe[HALO:HALO + tt].astype(BF16)
        for i in range(CONV_K):
            dw_ref[i:i + 1, :] += jnp.sum((dcv * xs[i])[HALO:HALO + tt], axis=0, keepdims=True)

    def after(n):
        return jnp.minimum((n + 1) * (tt // HALO), n8 - 1)

    return pl.pallas_call(
        body, name=name, grid=(PREP_NCB, nb),
        in_specs=[pl.BlockSpec((tt, PREP_COLS), lambda cb, n: (n, col0 + cb)),
                  pl.BlockSpec((HALO, PREP_COLS), lambda cb, n: (jnp.maximum(n * (tt // HALO) - 1, 0), col0 + cb)),
                  pl.BlockSpec((HALO, PREP_COLS), lambda cb, n: (after(n), col0 + cb)),
                  pl.BlockSpec((tt, PREP_COLS), lambda cb, n: (n, cb)),
                  pl.BlockSpec((HALO, PREP_COLS), lambda cb, n: (after(n), cb)),
                  pl.BlockSpec((CONV_K, PREP_COLS), lambda cb, n: (0, cb))],
        out_specs=[pl.BlockSpec((tt, PREP_COLS), lambda cb, n: (n, cb)),
                   pl.BlockSpec((CONV_K, PREP_COLS), lambda cb, n: (0, cb))],
        out_shape=[jax.ShapeDtypeStruct((t, 3 * B_W), BF16), jax.ShapeDtypeStruct((CONV_K, 3 * B_W), F32)],
        compiler_params=_params(("parallel", "arbitrary")))(proj, proj, proj, dqkvn, dqkvn, conv_w)


def _softplus(z):
    return jnp.maximum(z, 0.0) + jnp.log(1.0 + jnp.exp(-jnp.abs(z)))


def gates_fwd(proj, alog_pad, dtb_pad, *, name):
    t = proj.shape[0]

    def body(x_ref, a_ref, b_ref, o_ref):
        raw = x_ref[...]
        lane = lax.broadcasted_iota(jnp.int32, raw.shape, 1)
        g = -jnp.exp(a_ref[...]) * _softplus(raw + b_ref[...])
        o_ref[...] = jnp.where(lane < B_HEADS, _sigmoid(raw), jnp.where(lane < 2 * B_HEADS, g, 0.0))

    vec = pl.BlockSpec((1, 128), lambda n: (0, 0))
    return pl.pallas_call(
        body, name=name, grid=(t // ROWS,),
        in_specs=[pl.BlockSpec((ROWS, 128), lambda n: (n, COL_GATE // 128)), vec, vec],
        out_specs=pl.BlockSpec((ROWS, 128), lambda n: (n, 0)),
        out_shape=jax.ShapeDtypeStruct((t, 128), F32), compiler_params=_params(("parallel",)))(
            proj, alog_pad, dtb_pad)


def gates_bwd(proj, alog_pad, dtb_pad, dgates, *, name):
    t = proj.shape[0]

    def body(x_ref, a_ref, b_ref, dg_ref, dx_ref, da_ref, db_ref):
        @pl.when(pl.program_id(0) == 0)
        def _():
            da_ref[...] = jnp.zeros_like(da_ref)
            db_ref[...] = jnp.zeros_like(db_ref)

        raw, dgt = x_ref[...], dg_ref[...]
        lane = lax.broadcasted_iota(jnp.int32, raw.shape, 1)
        is_beta, is_g = lane < B_HEADS, (lane >= B_HEADS) & (lane < 2 * B_HEADS)
        beta = _sigmoid(raw)
        z = raw + b_ref[...]
        neg_a = -jnp.exp(a_ref[...])
        d_z = jnp.where(is_g, dgt * neg_a * _sigmoid(z), 0.0)
        dx_ref[...] = jnp.where(is_beta, dgt * beta * (1.0 - beta), d_z).astype(BF16)
        db_ref[...] += jnp.sum(d_z, axis=0, keepdims=True)
        da_ref[...] += jnp.sum(jnp.where(is_g, dgt * neg_a * _softplus(z), 0.0), axis=0, keepdims=True)

    vec = pl.BlockSpec((1, 128), lambda n: (0, 0))
    row = pl.BlockSpec((ROWS, 128), lambda n: (n, 0))
    return pl.pallas_call(
        body, name=name, grid=(t // ROWS,),
        in_specs=[pl.BlockSpec((ROWS, 128), lambda n: (n, COL_GATE // 128)), vec, vec, row],
        out_specs=[row, vec, vec],
        out_shape=[jax.ShapeDtypeStruct((t, 128), BF16), jax.ShapeDtypeStruct((1, 128), F32),
                   jax.ShapeDtypeStruct((1, 128), F32)],
        compiler_params=_params(("arbitrary",)))(proj, alog_pad, dtb_pad, dgates)


def _split2(a):
    hi = a.astype(BF16)
    return hi, (a - hi.astype(F32)).astype(BF16)


def _dotp(a, b, dims, passes):
    if passes == 1:
        return _dot(a.astype(BF16), b.astype(BF16), dims)
    ah, al = _split2(a)
    bh, bl = _split2(b)
    return _dot(ah, bh, dims) + (_dot(ah, bl, dims) + _dot(al, bh, dims))


_GRAD_DIMS = {NN: ((NT, False), (TN, False)), NT: ((NN, False), (TN, True)), TN: ((NT, True), (NN, False))}


def _make_mm(dims, passes, grad_passes):
    (da_dims, da_swap), (db_dims, db_swap) = _GRAD_DIMS[dims]

    @jax.custom_vjp
    def mm(a, b):
        return _dotp(a, b, dims, passes)

    def fwd(a, b):
        return _dotp(a, b, dims, passes), (a, b)

    def bwd(saved, ct):
        a, b = saved
        da = _dotp(b, ct, da_dims, grad_passes) if da_swap else _dotp(ct, b, da_dims, grad_passes)
        db = _dotp(ct, a, db_dims, grad_passes) if db_swap else _dotp(a, ct, db_dims, grad_passes)
        return da, db

    mm.defvjp(fwd, bwd)
    return mm


MM1 = {d: _make_mm(d, 1, 1) for d in (NN, NT, TN)}
MM3 = {d: _make_mm(d, 3, 1) for d in (NN, NT, TN)}


def _tri_ones(lower):
    r = lax.broadcasted_iota(jnp.int32, (DN_CHUNK, DN_CHUNK), 0)
    c = lax.broadcasted_iota(jnp.int32, (DN_CHUNK, DN_CHUNK), 1)
    return (r >= c if lower else r <= c).astype(BF16)


def _tri_sum(x, lower):
    tri = _tri_ones(lower)
    hi = x.astype(BF16)
    r1 = x - hi.astype(F32)
    mid = r1.astype(BF16)
    lo = (r1 - mid.astype(F32)).astype(BF16)
    return _dot(tri, hi, NN) + (_dot(tri, mid, NN) + _dot(tri, lo, NN))


def _delta_chunk(s0, q, k, v, beta, gam_c, gam_r):
    c = DN_CHUNK
    r = lax.broadcasted_iota(jnp.int32, (c, c), 0)
    cc = lax.broadcasted_iota(jnp.int32, (c, c), 1)
    incl, strict = r >= cc, r > cc
    eye = (r == cc).astype(F32)
    decay = jnp.exp(jnp.where(incl, gam_c - gam_r, NEG_INF))
    g_last = gam_c[:, c - 1:c, :]
    e_gam, e_rest, e_last = jnp.exp(gam_c), jnp.exp(g_last - gam_c), jnp.exp(g_last)
    a_neg = -jnp.where(strict, beta * MM1[NT](k, k) * decay, 0.0)
    inv = eye + a_neg
    pw = a_neg
    for _ in range(5):
        pw = MM3[NN](pw, pw)
        inv = inv + MM3[NN](inv, pw)
    uw = MM3[NN](inv, jnp.concatenate([v * beta, k * (beta * e_gam)], axis=-1))
    u, w = uw[..., :B_HEAD_DIM], uw[..., B_HEAD_DIM:]
    qk = MM1[NT](q, k) * decay
    v_new = u - MM1[NN](w, s0)
    o = MM1[NN](q * e_gam, s0) + MM1[NN](qk, v_new)
    s1 = s0 * e_last + MM1[TN](k * e_rest, v_new)
    return s1, o


def _delta_operands(q_ref, k_ref, v_ref, gt):
    heads = lambda ref: jnp.stack([ref[:, h * B_HEAD_DIM:(h + 1) * B_HEAD_DIM] for h in range(B_HEADS)])
    gam = _tri_sum(gt, True)
    gam_t = gam.T
    beta = jnp.stack([gt[:, h:h + 1] for h in range(B_HEADS)])
    gam_c = jnp.stack([gam[:, B_HEADS + h:B_HEADS + h + 1] for h in range(B_HEADS)])
    gam_r = jnp.stack([gam_t[B_HEADS + h:B_HEADS + h + 1, :] for h in range(B_HEADS)])
    return heads(q_ref), heads(k_ref), heads(v_ref), beta, gam_c, gam_r


def delta_fwd(qkvn, gates, *, name):
    t = qkvn.shape[0]
    nc = t // DN_CHUNK

    def body(q_ref, k_ref, v_ref, g_ref, o_ref, ss_ref, state):
        @pl.when(pl.program_id(0) == 0)
        def _():
            state[...] = jnp.zeros_like(state)

        s0 = state[...]
        ss_ref[...] = s0
        s1, o = _delta_chunk(s0, *_delta_operands(q_ref, k_ref, v_ref, g_ref[...]))
        state[...] = s1
        for h in range(B_HEADS):
            o_ref[:, h * B_HEAD_DIM:(h + 1) * B_HEAD_DIM] = o[h]

    blk = lambda j: pl.BlockSpec((DN_CHUNK, B_W), lambda n: (n, j))
    return pl.pallas_call(
        body, name=name, grid=(nc,),
        in_specs=[blk(0), blk(1), blk(2), pl.BlockSpec((DN_CHUNK, 128), lambda n: (n, 0))],
        out_specs=[blk(0), pl.BlockSpec((None, B_HEADS, B_HEAD_DIM, B_HEAD_DIM), lambda n: (n, 0, 0, 0))],
        out_shape=[jax.ShapeDtypeStruct((t, B_W), F32),
                   jax.ShapeDtypeStruct((nc, B_HEADS, B_HEAD_DIM, B_HEAD_DIM), F32)],
        scratch_shapes=[pltpu.VMEM((B_HEADS, B_HEAD_DIM, B_HEAD_DIM), F32)],
        compiler_params=_params(("arbitrary",)))(qkvn, qkvn, qkvn, gates)


def delta_bwd(qkvn, gates, ssave, do, *, name):
    t = qkvn.shape[0]
    nc = t // DN_CHUNK

    def body(q_ref, k_ref, v_ref, g_ref, ss_ref, do_ref, dx_ref, dg_ref, dstate):
        @pl.when(pl.program_id(0) == 0)
        def _():
            dstate[...] = jnp.zeros_like(dstate)

        lane = lax.broadcasted_iota(jnp.int32, (DN_CHUNK, 128), 1)
        row = lax.broadcasted_iota(jnp.int32, (128, DN_CHUNK), 0)
        dbeta_all = jnp.zeros((DN_CHUNK, 128), F32)
        dgam_c_all = jnp.zeros((DN_CHUNK, 128), F32)
        dgam_r_all = jnp.zeros((128, DN_CHUNK), F32)
        _, vjp = jax.vjp(_delta_chunk, ss_ref[...], *_delta_operands(q_ref, k_ref, v_ref, g_ref[...]))
        do = jnp.stack([do_ref[:, h * B_HEAD_DIM:(h + 1) * B_HEAD_DIM] for h in range(B_HEADS)])
        ds0, dq, dk, dv, dbeta, dgam_c, dgam_r = vjp((dstate[...], do))
        dstate[...] = ds0
        for h in range(B_HEADS):
            dx_ref[:, h * B_HEAD_DIM:(h + 1) * B_HEAD_DIM] = dq[h]
            dx_ref[:, B_W + h * B_HEAD_DIM:B_W + (h + 1) * B_HEAD_DIM] = dk[h]
            dx_ref[:, 2 * B_W + h * B_HEAD_DIM:2 * B_W + (h + 1) * B_HEAD_DIM] = dv[h]
            dbeta_all = dbeta_all + jnp.where(lane == h, dbeta[h], 0.0)
            dgam_c_all = dgam_c_all + jnp.where(lane == B_HEADS + h, dgam_c[h], 0.0)
            dgam_r_all = dgam_r_all + jnp.where(row == B_HEADS + h, dgam_r[h], 0.0)
        dg_ref[...] = dbeta_all + _tri_sum(dgam_c_all + dgam_r_all.T, False)

    blk = lambda j: pl.BlockSpec((DN_CHUNK, B_W), lambda n: (nc - 1 - n, j))
    gsp = pl.BlockSpec((DN_CHUNK, 128), lambda n: (nc - 1 - n, 0))
    return pl.pallas_call(
        body, name=name, grid=(nc,),
        in_specs=[blk(0), blk(1), blk(2), gsp,
                  pl.BlockSpec((None, B_HEADS, B_HEAD_DIM, B_HEAD_DIM), lambda n: (nc - 1 - n, 0, 0, 0)), blk(0)],
        out_specs=[pl.BlockSpec((DN_CHUNK, 3 * B_W), lambda n: (nc - 1 - n, 0)), gsp],
        out_shape=[jax.ShapeDtypeStruct((t, 3 * B_W), F32), jax.ShapeDtypeStruct((t, 128), F32)],
        scratch_shapes=[pltpu.VMEM((B_HEADS, B_HEAD_DIM, B_HEAD_DIM), F32)],
        compiler_params=_params(("arbitrary",)))(qkvn, qkvn, qkvn, gates, ssave, do)


GNORM_ROWS = 1024


def gnorm_fwd(o, proj, onorm, *, name):
    t = o.shape[0]

    def body(o_ref, z_ref, w_ref, out_ref):
        ov = o_ref[...]
        r = lax.rsqrt(jnp.mean(ov * ov, axis=-1, keepdims=True) + EPS)
        out_ref[...] = (ov * r * w_ref[...] * _silu(z_ref[...])).astype(BF16)

    rows = min(GNORM_ROWS, t)
    blk = pl.BlockSpec((rows, B_HEAD_DIM), lambda n, h: (n, h))
    return pl.pallas_call(
        body, name=name, grid=(t // rows, B_HEADS),
        in_specs=[blk, pl.BlockSpec((rows, B_HEAD_DIM), lambda n, h: (n, COL_Z // B_HEAD_DIM + h)),
                  pl.BlockSpec((1, B_HEAD_DIM), lambda n, h: (0, 0))],
        out_specs=blk, out_shape=jax.ShapeDtypeStruct((t, B_W), BF16),
        compiler_params=_params(("parallel", "parallel")))(o, proj, onorm)


def gnorm_bwd(o, proj, onorm, dout, *, dcol0, name):
    t = o.shape[0]

    def body(o_ref, z_ref, w_ref, d_ref, do_ref, dz_ref, dw_ref):
        @pl.when((pl.program_id(0) == 0) & (pl.program_id(1) == 0))
        def _():
            dw_ref[...] = jnp.zeros_like(dw_ref)

        ov, zv, wv, dv = o_ref[...], z_ref[...], w_ref[...], d_ref[...].astype(F32)
        r = lax.rsqrt(jnp.mean(ov * ov, axis=-1, keepdims=True) + EPS)
        nrm = ov * r
        dz_ref[...] = (dv * nrm * wv * _dsilu(zv)).astype(BF16)
        da = dv * _silu(zv)
        dw_ref[...] += jnp.sum(da * nrm, axis=0, keepdims=True)
        dn = da * wv
        do_ref[...] = r * dn - ov * (r * r * r) * jnp.mean(dn * ov, axis=-1, keepdims=True)

    rows = min(GNORM_ROWS, t)
    blk = pl.BlockSpec((rows, B_HEAD_DIM), lambda n, h: (n, h))
    vec = pl.BlockSpec((1, B_HEAD_DIM), lambda n, h: (0, 0))
    return pl.pallas_call(
        body, name=name, grid=(t // rows, B_HEADS),
        in_specs=[blk, pl.BlockSpec((rows, B_HEAD_DIM), lambda n, h: (n, COL_Z // B_HEAD_DIM + h)), vec,
                  pl.BlockSpec((rows, B_HEAD_DIM), lambda n, h: (n, dcol0 // B_HEAD_DIM + h))],
        out_specs=[blk, blk, vec],
        out_shape=[jax.ShapeDtypeStruct((t, B_W), F32), jax.ShapeDtypeStruct((t, B_W), BF16),
                   jax.ShapeDtypeStruct((1, B_HEAD_DIM), F32)],
        compiler_params=_params(("arbitrary", "arbitrary")))(o, proj, onorm, dout)


def _ffn_fwd(h, norm_g, wg, wu, wd, tm, tag):
    hn = rms_fwd(h, norm_g, name=f"ffn{tag}_norm")
    gate, up, act = mm_gate_up(hn, wg, wu, tm=min(512, tm), tn=1408, tk=2048, name=f"ffn{tag}_gate_up")
    h_out = mm_nn(act, wd, tm=tm, tn=2048, tk=512, out_dtype=F32, res=h, name=f"ffn{tag}_down")
    return h_out, (hn, gate, up, act)


def _ffn_bwd(dh, h, norm_g, wg, wu, wd, saved, tm, tag, emit):
    hn, gate, up, act = saved
    dwd = mm_tn(act, dh, shards=1, tm=tm, tn=1024, tk=1408, out_dtype=BF16, name=f"ffn{tag}_dwd")[0]
    dgate, dup = mm_down_bwd(dh, wd, gate, up, tm=tm, tn=512, tk=2048, name=f"ffn{tag}_dact")
    dwg = mm_tn(hn, dgate, shards=N_SHARD, tm=tm, tn=1408, tk=1024, out_dtype=BF16, name=f"ffn{tag}_dwg")
    dwu = mm_tn(hn, dup, shards=N_SHARD, tm=tm, tn=1408, tk=1024, out_dtype=BF16, name=f"ffn{tag}_dwu")
    started = emit(f"ffn{tag}", {"gate": dwg, "up": dwu, "down": dwd})
    dhn = mm_nt(dgate, wg, tm=tm, tn=1024, tk=1408, out_dtype=F32, name=f"ffn{tag}_dhn_g")
    dhn = mm_nt(dup, wu, tm=tm, tn=1024, tk=1408, out_dtype=F32, res=dhn, name=f"ffn{tag}_dhn_u")
    dh_in, dnorm = rms_bwd(h, norm_g + started, dhn, dh, name=f"ffn{tag}_dnorm")
    return dh_in, dnorm


def _local_step(x, target, w, get, emit):
    t = x.shape[0]
    tm = min(1024, t)
    g = {}

    hn0 = rms_fwd(x, w["even_norm"], name="l0_norm")
    w.update(get("even_in", hn0))
    proj = mm_nt(hn0, w["even_w_in"], tm=tm, tn=512, tk=2048, out_dtype=F32, name="l0_w_in")
    w.update(get("even_out", proj))
    out_a = att_fwd(proj, w["sinks"], name="l0_att")
    qkvn = dprep_fwd(proj, w["even_conv"], name="l0_prep")
    gates = gates_fwd(proj, w["a_log"], w["dt_bias"], name="l0_gates")
    o_delta, ssave = delta_fwd(qkvn, gates, name="l0_delta")
    out_b = gnorm_fwd(o_delta, proj, w["onorm"], name="l0_gnorm")
    mix0 = jnp.concatenate([out_a, out_b], axis=-1)
    h1 =mm_nn(mix0, w["even_w_out"], tm=tm, tn=512, tk=2048, out_dtype=F32, res=x, name="l0_w_out")
    f0 = get("ffn0", h1)
    h2, ffn0 = _ffn_fwd(h1, w["ffn_norm"][0:1] + f0["tok"], f0["gate"], f0["up"], f0["down"], tm, 0)
    hn2 = rms_fwd(h2, w["odd_norm"], name="l1_norm")
    w.update(get("odd", hn2))
    zpre = mm_nn(hn2, w["odd_w_in"], tm=tm, tn=1024, tk=2048, out_dtype=F32, name="l1_w_in")
    gated = gmlp_fwd(zpre, w["odd_ln_g"], w["odd_ln_b"], w["odd_w_s"], w["odd_b_s"], name="l1_gmlp")
    h3 = mm_nn(gated, w["odd_w_out"], tm=tm, tn=512, tk=2048, out_dtype=F32, res=h2, name="l1_w_out")
    f1 = get("ffn1", h3)
    h4, ffn1 = _ffn_fwd(h3, w["ffn_norm"][1:2] + f1["tok"], f1["gate"], f1["up"], f1["down"], tm, 1)
    loss, dh4, g["final_norm"] = loss_head(h4, w["final_norm"], target, name="loss_head")

    dh3, dn1 = _ffn_bwd(dh4, h3, w["ffn_norm"][1:2], f1["gate"], f1["up"], f1["down"], ffn1, tm, 1, emit)
    dw_out_o = mm_tn(gated, dh3, shards=1, tm=tm, tn=1024, tk=1024, out_dtype=BF16, name="l1_dw_out")[0]
    dgated = mm_nt(dh3, w["odd_w_out"], tm=tm, tn=512, tk=2048, out_dtype=BF16, name="l1_dgated")
    dzpre, g["odd_w_s"], g["odd_b_s"], g["odd_ln_g"], g["odd_ln_b"] = gmlp_bwd(
        zpre, dgated, w["odd_ln_g"], w["odd_ln_b"], w["odd_w_s"], w["odd_b_s"], name="l1_dgmlp")
    dw_in_o = mm_tn(hn2, dzpre, shards=N_SHARD, tm=tm, tn=1024, tk=1024, out_dtype=BF16, name="l1_dw_in")
    started = emit("odd", {"odd_w_in": dw_in_o, "odd_w_out": dw_out_o})
    dhn2 = mm_nt(dzpre, w["odd_w_in"], tm=tm, tn=1024, tk=1024, out_dtype=F32, name="l1_dhn")
    dh2, g["odd_norm"] = rms_bwd(h2, w["odd_norm"] + started, dhn2, dh3, name="l1_dnorm")
    dh1, dn0 = _ffn_bwd(dh2, h1, w["ffn_norm"][0:1], f0["gate"], f0["up"], f0["down"], ffn0, tm, 0, emit)
    g["ffn_norm"] = jnp.concatenate([dn0, dn1], axis=0)
    dw_out_e = mm_tn(mix0, dh1, shards=1, tm=tm, tn=1024, tk=1024, out_dtype=BF16, name="l0_dw_out")[0]
    started = emit("even_out", {"even_w_out": dw_out_e})
    dmix = mm_nt(dh1, w["even_w_out"], tm=tm, tn=512, tk=2048, out_dtype=F32, name="l0_dmix")
    dq_a, dkv_cur, dkv_prev, g["sinks"] = att_bwd(proj, w["sinks"] + started, dmix, name="l0_datt")
    dkv = dkv_cur + jnp.concatenate([dkv_prev[WINDOW:], jnp.zeros((WINDOW, 2 * A_KV), F32)], axis=0)
    do_delta, dz, g["onorm"] = gnorm_bwd(o_delta, proj, w["onorm"], dmix, dcol0=A_Q, name="l0_dgnorm")
    dqkvn, dgates = delta_bwd(qkvn, gates, ssave, do_delta, name="l0_ddelta")
    dqkv_b, g["even_conv"] = dprep_bwd(proj, w["even_conv"], dqkvn, name="l0_dprep")
    draw, g["a_log"], g["dt_bias"] = gates_bwd(proj, w["a_log"], w["dt_bias"], dgates, name="l0_dgates")
    dproj = jnp.concatenate([dq_a, dkv.astype(BF16), dqkv_b, dz, draw,
                             jnp.zeros((t, EVEN_IN_PAD - COL_GATE - 128), BF16)], axis=-1)
    dw_in_e = mm_tn(dproj, hn0, shards=1, tm=tm, tn=1024, tk=1408, out_dtype=BF16, name="l0_dw_in")[0]
    dhn0 = mm_nn(dproj, w["even_w_in"], tm=tm, tn=2048, tk=512, out_dtype=F32, name="l0_dhn")
    grad_x, g["even_norm"] = rms_bwd(x, w["even_norm"], dhn0, dh1, name="l0_dnorm")
    emit("even_in", {"even_w_in": dw_in_e, "small": g})
    return loss, grad_x


ANY = pl.BlockSpec(memory_space=pl.ANY)
N_DEV = 8


def _place():
    return lax.axis_index("x"), lax.axis_index("y"), lax.axis_index("c")


def _chip_peers(x, y, c):
    return [((1 - x, y, c), 2 * (1 - x) + y), ((x, 1 - y, c), 2 * x + 1 - y), ((1 - x, 1 - y, c), 2 * (1 - x) + 1 - y)]


HBM = pl.BlockSpec(memory_space=pltpu.HBM)
SEM = pl.BlockSpec(memory_space=pltpu.SEMAPHORE)
EFFECT = pltpu.SideEffectType.DATAFLOW_SIDE_EFFECTING
N_PEER = 3


def _half(ref, c):
    r, cols = ref.shape
    tile_rows = 32 // jnp.dtype(ref.dtype).itemsize
    if (r // 2) % tile_rows == 0:
        return ref.at[pl.ds(c * (r // 2), r // 2)]
    assert (cols // 2) % 128 == 0, ref.shape
    return ref.at[:, pl.ds(c * (cols // 2), cols // 2)]


def _gather_plan(srcs, lands, send, recv):
    x, y, c = _place()
    return [pltpu.make_async_remote_copy(src_ref=_half(srcs[i], c), dst_ref=_half(lands[i].at[2 * x + y], c),
                                         send_sem=send.at[N_PEER * i + k], recv_sem=recv.at[N_PEER * i + k],
                                         device_id=peer, device_id_type=MESH_ID)
            for i in range(len(srcs)) for k, (peer, _) in enumerate(_chip_peers(x, y, c))]


def _relay_plan(srcs, lands, send, recv):
    x, y, c = _place()
    return [pltpu.make_async_remote_copy(src_ref=_half(lands[i].at[idx], c), dst_ref=_half(lands[i].at[idx], c),
                                         send_sem=send.at[N_PEER * i + k], recv_sem=recv.at[N_PEER * i + k],
                                         device_id=(x, y, 1 - c), device_id_type=MESH_ID)
            for i in range(len(srcs)) for k, (_, idx) in enumerate(_chip_peers(x, y, c))]


def _scatter_plan(srcs, lands, send, recv):
    x, y, c = _place()
    return [pltpu.make_async_remote_copy(src_ref=srcs[i].at[idx], dst_ref=lands[i].at[k], send_sem=send.at[N_PEER * i + k],
                                         recv_sem=recv.at[N_PEER * i + k], device_id=peer, device_id_type=MESH_ID)
            for i in range(len(srcs)) for k, (peer, idx) in enumerate(_chip_peers(x, y, c))]


def _swap_plan(srcs, lands, send, recv):
    x, y, c = _place()
    return [pltpu.make_async_remote_copy(src_ref=srcs[i], dst_ref=lands[i], send_sem=send.at[N_PEER * i],
                                         recv_sem=recv.at[N_PEER * i], device_id=(x, y, 1 - c), device_id_type=MESH_ID)
            for i in range(len(srcs))]


def copies_start(plan, srcs, lands, after, *, name):
    n = len(srcs)
    both = list(srcs) + list(lands)

    def body(*refs):
        src_refs, land_refs = refs[:n], refs[n:2 * n]
        send, recv = refs[2 * n + 1], refs[2 * n + 2]
        for cp in plan(src_refs, land_refs, send, recv):
            cp.start()
        refs[-1][...] = jnp.zeros_like(refs[-1])

    res = pl.pallas_call(
        body, name=name,
        out_shape=(pltpu.SemaphoreType.DMA((n * N_PEER,)), pltpu.SemaphoreType.DMA((n * N_PEER,)),
                   *[pltpu.HBM(a.shape, a.dtype) for a in both], jax.ShapeDtypeStruct((8, 128), F32)),
        in_specs=[HBM] * (2 * n) + [ANY],
        out_specs=(SEM, SEM, *[HBM] * (2 * n), pl.BlockSpec(memory_space=pltpu.VMEM)),
        input_output_aliases={i: 2 + i for i in range(2 * n)},
        compiler_params=pltpu.CompilerParams(has_side_effects=EFFECT))(
            *[pltpu.with_memory_space_constraint(a, pltpu.HBM) for a in both], after)
    return {"send": res[0], "recv": res[1], "srcs": list(res[2:2 + n]), "lands": list(res[2 + n:2 + 2 * n]),
            "token": res[-1]}


def copies_relay(arrived_plan, next_plan, started, after, *, name):
    srcs, lands = started["srcs"], started["lands"]
    n = len(srcs)
    both = srcs + lands

    def body(*refs):
        src_refs, land_refs = refs[:n], refs[n:2 * n]
        send1, recv1 = refs[2 * n], refs[2 * n + 1]
        send2, recv2 = refs[2 * n + 3], refs[2 * n + 4]
        for cp in arrived_plan(src_refs, land_refs, send1, recv1):
            cp.wait_send()
            cp.wait_recv()
        for cp in next_plan(src_refs, land_refs, send2, recv2):
            cp.start()
        refs[-1][...] = jnp.zeros_like(refs[-1])

    res = pl.pallas_call(
        body, name=name,
        out_shape=(pltpu.SemaphoreType.DMA((n * N_PEER,)), pltpu.SemaphoreType.DMA((n * N_PEER,)),
                   *[pltpu.HBM(a.shape, a.dtype) for a in both], jax.ShapeDtypeStruct((8, 128), F32)),
        in_specs=[HBM] * (2 * n) + [SEM, SEM, ANY],
        out_specs=(SEM, SEM, *[HBM] * (2 * n), pl.BlockSpec(memory_space=pltpu.VMEM)),
        input_output_aliases={i: 2 + i for i in range(2 * n)},
        compiler_params=pltpu.CompilerParams(has_side_effects=EFFECT))(*both, started["send"], started["recv"], after)
    return {"send": res[0], "recv": res[1], "srcs": list(res[2:2 + n]), "lands": list(res[2 + n:2 + 2 * n]),
            "token": res[-1]}


def copies_wait(plan, started, after, *, name):
    srcs, lands = started["srcs"], started["lands"]
    n = len(srcs)
    both = srcs + lands

    def body(*refs):
        src_refs, land_refs = refs[:n], refs[n:2 * n]
        send, recv = refs[2 * n], refs[2 * n + 1]
        for cp in plan(src_refs, land_refs, send, recv):
            cp.wait_send()
            cp.wait_recv()

    res = pl.pallas_call(
        body, name=name, out_shape=tuple(pltpu.HBM(a.shape, a.dtype) for a in both),
        in_specs=[HBM] * (2 * n) + [SEM, SEM, ANY], out_specs=(HBM,) * (2 * n),
        input_output_aliases={i: i for i in range(2 * n)},
        compiler_params=pltpu.CompilerParams(has_side_effects=EFFECT))(*both, started["send"], started["recv"], after)
    return list(res[:n]), list(res[n:])


def allgather_small(small, *, name):
    def body(small_ref, out_ref, send, recv, loc):
        x, y, c = _place()
        dev = 4 * x + 2 * y + c
        local = pltpu.make_async_copy(small_ref, out_ref.at[dev], loc)
        remote = []
        for r in range(1, N_DEV):
            fx, fy, fc = (r >> 2) & 1, (r >> 1) & 1, r & 1
            peer = (1 - x if fx else x, 1 - y if fy else y, 1 - c if fc else c)
            remote.append(pltpu.make_async_remote_copy(
                src_ref=small_ref, dst_ref=out_ref.at[dev], send_sem=send.at[r - 1], recv_sem=recv.at[r - 1],
                device_id=peer, device_id_type=MESH_ID))
        local.start()
        for cp in remote:
            cp.start()
        for cp in remote:
            cp.wait()
        local.wait()

    return pl.pallas_call(
        body, name=name, in_specs=[ANY], out_specs=ANY,
        out_shape=jax.ShapeDtypeStruct((N_DEV,) + small.shape, small.dtype),
        scratch_shapes=[pltpu.SemaphoreType.DMA((N_DEV - 1,)), pltpu.SemaphoreType.DMA((N_DEV - 1,)),
                        pltpu.SemaphoreType.DMA(())])(small)


def swap_cores(arrs, *, name):
    n = len(arrs)

    def body(*refs):
        ins, outs = refs[:n], refs[n:2 * n]
        send, recv = refs[2 * n:]
        x, y, c = _place()
        copies = [pltpu.make_async_remote_copy(src_ref=ins[i], dst_ref=outs[i], send_sem=send.at[i], recv_sem=recv.at[i],
                                               device_id=(x, y, 1 - c), device_id_type=MESH_ID) for i in range(n)]
        for cp in copies:
            cp.start()
        for cp in copies:
            cp.wait()

    return pl.pallas_call(
        body, name=name, in_specs=[ANY] * n, out_specs=[ANY] * n,
        out_shape=[jax.ShapeDtypeStruct(a.shape, a.dtype) for a in arrs],
        scratch_shapes=[pltpu.SemaphoreType.DMA((n,)), pltpu.SemaphoreType.DMA((n,))])(*arrs)


RED_ROWS = 128
RED_COLS = 256


def _red_block(r, c):
    if r % RED_ROWS == 0:
        return RED_ROWS, c
    if c > RED_COLS and c % RED_COLS == 0:
        return r, RED_COLS
    return r, c


def sum_chips(by_owner, me, got, *, name):
    _, r, c = by_owner.shape
    rb, cb = _red_block(r, c)

    def body(me_ref, o_ref, a_ref, b_ref, c_ref, out_ref):
        out_ref[...] = ((o_ref[...].astype(F32) + a_ref[...].astype(F32)) + b_ref[...].astype(F32)) + c_ref[...].astype(F32)

    gk = lambda k: pl.BlockSpec((None, rb, cb), lambda i, j, me_ref: (k, i, j))
    grid_spec = pltpu.PrefetchScalarGridSpec(
        num_scalar_prefetch=1, grid=(r // rb, c // cb),
        in_specs=[pl.BlockSpec((None, rb, cb), lambda i, j, me_ref: (me_ref[0], i, j)), gk(0), gk(1), gk(2)],
        out_specs=pl.BlockSpec((rb, cb), lambda i, j, me_ref: (i, j)))
    return pl.pallas_call(
        body, name=name, grid_spec=grid_spec, out_shape=jax.ShapeDtypeStruct((r, c), F32),
        compiler_params=_params(("parallel", "parallel")))(me, by_owner, got, got, got)


def sum_devices(small_all, *, name):
    _, p, c = small_all.shape

    def body(a_ref, out_ref):
        acc = a_ref[0]
        for d in range(1, N_DEV):
            acc = acc + a_ref[d]
        out_ref[...] = acc

    return pl.pallas_call(
        body, name=name, grid=(1,), in_specs=[pl.BlockSpec((N_DEV, p, c), lambda i: (0, 0, 0))],
        out_specs=pl.BlockSpec((p, c), lambda i: (0, 0)), out_shape=jax.ShapeDtypeStruct((p, c), F32),
        compiler_params=_params(("arbitrary",)))(small_all)


def adamw(parts, w, m, v, *, name):
    nl, r, c = w.shape
    assert len(parts) == nl
    npart = len(parts[0])
    rb, cb = _red_block(r, c)
    flat = [a for layer in parts for a in layer]

    def body(*refs):
        p_refs, (w_ref, m_ref, v_ref) = refs[:nl * npart], refs[nl * npart:nl * npart + 3]
        g_ref, d_ref, nm_ref, nv_ref = refs[nl * npart + 3:]
        layer = pl.program_id(0)
        grad = None
        for l in range(nl):
            gl = p_refs[l * npart][...]
            for j in range(1, npart):
                gl = gl + p_refs[l * npart + j][...]
            grad = gl if grad is None else jnp.where(layer == l, gl, grad)
        wv, mv, vv = w_ref[...], m_ref[...], v_ref[...]
        nm = ADAM_B1 * mv + (1.0 - ADAM_B1) * grad
        nv = ADAM_B2 * vv + (1.0 - ADAM_B2) * (grad * grad)
        m_hat = nm / (1.0 - ADAM_B1 ** ADAM_STEP)
        v_hat = nv / (1.0 - ADAM_B2 ** ADAM_STEP)
        g_ref[...] = grad
        d_ref[...] = -ADAM_LR * (m_hat / (jnp.sqrt(v_hat) + ADAM_EPS) + ADAM_WD * wv)
        nm_ref[...] = nm
        nv_ref[...] = nv

    pspec = pl.BlockSpec((rb, cb), lambda l, i, j: (i, j))
    wspec = pl.BlockSpec((None, rb, cb), lambda l, i, j: (l, i, j))
    osh = jax.ShapeDtypeStruct((nl, r, c), F32)
    return pl.pallas_call(
        body, name=name, grid=(nl, r // rb, c // cb), in_specs=[pspec] * (nl * npart) + [wspec] * 3,
        out_specs=[wspec] * 4, out_shape=[osh] * 4,
        compiler_params=_params(("parallel", "parallel", "parallel")))(*flat, w, m, v)


def _rows128(a):
    flat = a.reshape(-1)
    pad = (-flat.shape[0]) % 128
    return jnp.pad(flat, (0, pad)).reshape(-1, 128)


def _pack_rows(arrs, multiple=8):
    rows = jnp.concatenate([_rows128(a.astype(F32)) for a in arrs], axis=0)
    return jnp.pad(rows, ((0, (-rows.shape[0]) % multiple), (0, 0)))


def _unpack_rows(rows, shapes):
    out, r0 = [], 0
    for shp in shapes:
        size = 1
        for s in shp:
            size *= s
        nr = -(-size // 128)
        out.append(rows[r0:r0 + nr].reshape(-1)[:size].reshape(shp))
        r0 += nr
    return out


SMALL_LOCAL_GRADS = ["even_norm", "even_conv", "a_log", "dt_bias", "sinks", "onorm", "odd_norm", "odd_ln_g",
                     "odd_ln_b", "odd_w_s", "odd_b_s", "ffn_norm", "final_norm"]
BIG = ["even_w_in", "even_w_out", "odd_w_in", "odd_w_out", "ffn_w_gate", "ffn_w_up", "ffn_w_down"]
WEIGHTS = ["even_norm", "even_w_in", "even_conv", "even_a_log", "even_dt_bias", "even_sinks", "even_onorm",
           "even_w_out", "odd_norm", "odd_w_in", "odd_ln_g", "odd_ln_b", "odd_w_s", "odd_b_s", "odd_w_out",
           "ffn_norm", "ffn_w_gate", "ffn_w_up", "ffn_w_down", "final_norm"]
SMALL = [n for n in WEIGHTS if n not in BIG]


def kernel(x, even_norm, even_w_in, even_conv, even_a_log, even_dt_bias, even_sinks, even_onorm, even_w_out, odd_norm, odd_w_in, odd_ln_g, odd_ln_b, odd_w_s, odd_b_s, odd_w_out, ffn_norm, ffn_w_gate, ffn_w_up, ffn_w_down, final_norm, loss_target, m_even_norm, m_even_w_in, m_even_conv, m_even_a_log, m_even_dt_bias, m_even_sinks, m_even_onorm, m_even_w_out, m_odd_norm, m_odd_w_in, m_odd_ln_g, m_odd_ln_b, m_odd_w_s, m_odd_b_s, m_odd_w_out, m_ffn_norm, m_ffn_w_gate, m_ffn_w_up, m_ffn_w_down, m_final_norm, v_even_norm, v_even_w_in, v_even_conv, v_even_a_log, v_even_dt_bias, v_even_sinks, v_even_onorm, v_even_w_out, v_odd_norm, v_odd_w_in, v_odd_ln_g, v_odd_ln_b, v_odd_w_s, v_odd_b_s, v_odd_w_out, v_ffn_norm, v_ffn_w_gate, v_ffn_w_up, v_ffn_w_down, v_final_norm):
    args = dict(locals())
    wl = {n: args[n] for n in WEIGHTS}
    ml = {n: args["m_" + n] for n in WEIGHTS}
    vl = {n: args["v_" + n] for n in WEIGHTS}
    me = 2 * lax.axis_index("x") + lax.axis_index("y")

    def landing(a):
        return lax.dynamic_update_index_in_dim(lax.empty((N_SHARD,) + a.shape, a.dtype), a, me, 0)

    b16 = lambda *arrs: [a.astype(BF16) for a in arrs]
    gather_groups = {
        "even_in": b16(even_w_in[0].T) + [_pack_rows([even_conv[0], odd_norm, odd_ln_g, odd_ln_b], multiple=16)],
        "even_out": b16(even_w_out[0]),
        "ffn0": b16(ffn_w_gate[0], ffn_w_up[0], ffn_w_down[0]),
        "odd": b16(odd_w_in[0], odd_w_out[0]),
        "ffn1": b16(ffn_w_gate[1], ffn_w_up[1], ffn_w_down[1]),
    }
    gathering, after = {}, even_norm
    for group, srcs in gather_groups.items():
        gathering[group] = copies_start(_gather_plan, srcs, [landing(a) for a in srcs], after,
                                        name=f"gather_{group}_start")
        after = gathering[group]["token"]

    order = list(gather_groups)
    relayed, kept = {}, {}
    sinks_pad = jnp.pad(even_sinks, ((0, 0), (0, 128 - A_HEADS)))

    def relay(group, behind):
        relayed[group] = copies_relay(_gather_plan, _relay_plan, gathering[group], behind,
                                      name=f"gather_{group}_relay")
        return relayed[group]["token"][0:1, 0:1]

    def get(group, behind):
        if group not in relayed:
            relay(group, behind)
        _, lands = copies_wait(_relay_plan, relayed[group], behind, name=f"gather_{group}_wait")
        nxt = order.index(group) + 1
        tok = relay(order[nxt], lands[0]) if nxt < len(order) else jnp.zeros((1, 1), F32)
        if group == "even_in":
            parts = zip(*[_unpack_rows(lands[1][s], [(CONV_K, 768), (1, 512), (1, 512), (1, 512)])
                          for s in range(N_SHARD)])
            conv, onorm, lng, lnb = [jnp.concatenate(p, axis=1) for p in parts]
            w_in = jnp.pad(lands[0].reshape(EVEN_IN, D_MODEL), ((0, EVEN_IN_PAD - EVEN_IN), (0, 0)))
            kept["odd_ln_g"] = lng
            return {"even_w_in": w_in, "even_conv": conv + tok, "odd_norm": onorm, "odd_ln_b": lnb}
        if group == "even_out":
            return {"even_w_out": lands[0].reshape(D_MODEL, D_MODEL), "sinks": sinks_pad + tok}
        if group == "odd":
            return {"odd_w_in": lands[0], "odd_w_out": lands[1].reshape(D_MODEL, D_MODEL),
                    "odd_ln_g": kept["odd_ln_g"] + tok}
        return {"gate": lands[0], "up": lands[1], "down": lands[2].reshape(D_FF, D_MODEL), "tok": tok}

    rows4 =lambda a: a.reshape(N_SHARD, a.shape[0] // N_SHARD, a.shape[1])
    scattering, small = {}, {}

    def emit(group, grads):
        behind = even_norm
        if group == "even_in":
            small["local"] = grads["small"]
            small["all"] = behind = allgather_small(_pack_rows([grads["small"][n] for n in SMALL_LOCAL_GRADS]),
                                                    name="allgather_small")
            srcs = [grads["even_w_in"][:EVEN_IN].reshape(N_SHARD, EVEN_IN // N_SHARD, D_MODEL)]
        elif group == "even_out":
            srcs = [rows4(grads["even_w_out"])]
        elif group == "odd":
            srcs = [grads["odd_w_in"], rows4(grads["odd_w_out"])]
        else:
            srcs = [grads["gate"], grads["up"], rows4(grads["down"])]
        lands = [lax.empty((N_PEER,) + a.shape[1:], a.dtype) for a in srcs]
        scattering[group] = copies_start(_scatter_plan, srcs, lands, behind, name=f"scatter_{group}_start")
        return scattering[group]["token"][0:1, 0:1]

    pad816 = lambda a: jnp.pad(a, ((0, 0), (B_HEADS, 128 - 2 * B_HEADS)))
    w = {
        "even_norm": even_norm + after[0:1, 0:1],
        "a_log": pad816(even_a_log), "dt_bias": pad816(even_dt_bias),
        "sinks": sinks_pad,
        "onorm": even_onorm,
        "odd_w_s": odd_w_s[0],
        "odd_b_s": jnp.pad(odd_b_s[0].T, ((0, 0), (0, 128 - C_GROUPS))),
        "ffn_norm": ffn_norm,
        "final_norm": final_norm[None],
    }
    loss_l, grad_x = _local_step(x[0], loss_target[0], w, get, emit)
    loss = lax.psum(loss_l[0, 0], ("x", "y", "c"))

    me1 = me.reshape(1).astype(jnp.int32)
    swapping = {}

    def reduce_chips(group, behind):
        srcs, lands = copies_wait(_scatter_plan, scattering[group], behind, name=f"scatter_{group}_wait")
        partial = [sum_chips(srcs[i], me1, lands[i], name=f"sum_chips_{group}_{i}") for i in range(len(srcs))]
        swapping[group] = copies_start(_swap_plan, partial, [lax.empty(p.shape, p.dtype) for p in partial],
                                       even_norm, name=f"swap_{group}_start")
        return swapping[group]["token"]

    def swapped(group, behind):
        mine, theirs = copies_wait(_swap_plan, swapping[group], behind, name=f"swap_{group}_wait")
        return list(zip(mine, theirs))

    behind = scattering["even_in"]["token"]
    for group in ("ffn1", "odd", "ffn0", "even_out"):
        behind = reduce_chips(group, behind)
    sums = {group: swapped(group, behind) for group in ("ffn1", "odd", "ffn0", "even_out")}
    outs = {}
    parts_of = {"even_w_out": [sums["even_out"][0]], "odd_w_in": [sums["odd"][0]], "odd_w_out": [sums["odd"][1]],
                "ffn_w_gate": [sums["ffn0"][0], sums["ffn1"][0]], "ffn_w_up": [sums["ffn0"][1], sums["ffn1"][1]],
                "ffn_w_down": [sums["ffn0"][2], sums["ffn1"][2]]}
    for n in parts_of:
        outs[n] = adamw(parts_of[n], wl[n], ml[n], vl[n], name=f"adamw_{n}")
    behind = reduce_chips("even_in", outs["ffn_w_down"][1])
    flip = lambda a: jnp.transpose(a, (0, 2, 1))
    outs["even_w_in"] = [flip(o) for o in adamw([swapped("even_in", behind)[0]], flip(wl["even_w_in"]),
                                                flip(ml["even_w_in"]), flip(vl["even_w_in"]),
                                                name="adamw_even_w_in")]

    g = small["local"]
    small_sum = sum_devices(small["all"], name="sum_devices")
    sg = dict(zip(SMALL_LOCAL_GRADS, _unpack_rows(small_sum, [g[n].shape for n in SMALL_LOCAL_GRADS])))
    own_cols = lambda a, width: lax.dynamic_slice_in_dim(a, me * width, width, axis=a.ndim - 1)
    small_grads = {
        "even_norm": sg["even_norm"], "even_conv": own_cols(sg["even_conv"], 768)[None],
        "even_a_log": sg["a_log"][:, B_HEADS:2 * B_HEADS], "even_dt_bias": sg["dt_bias"][:, B_HEADS:2 * B_HEADS],
        "even_sinks": sg["sinks"][:, :A_HEADS], "even_onorm": sg["onorm"],
        "odd_norm": own_cols(sg["odd_norm"], 512), "odd_ln_g": own_cols(sg["odd_ln_g"], 512),
        "odd_ln_b": own_cols(sg["odd_ln_b"], 512), "odd_w_s": sg["odd_w_s"][None],
        "odd_b_s": sg["odd_b_s"][:, :C_GROUPS].T[None], "ffn_norm": sg["ffn_norm"], "final_norm": sg["final_norm"][0],
    }
    packed = [_pack_rows([d[n] for n in SMALL])[None] for d in (small_grads, wl, ml, vl)]
    small_out = adamw([(packed[0][0],)], packed[1], packed[2], packed[3], name="adamw_small")
    shapes = [wl[n].shape for n in SMALL]
    for j in range(4):
        for n, a in zip(SMALL, _unpack_rows(small_out[j][0], shapes)):
            outs.setdefault(n, [None] * 4)[j] = a

    return (loss, grad_x[None], *[outs[n][0] for n in WEIGHTS], *[outs[n][1] for n in WEIGHTS],
            *[outs[n][2] for n in WEIGHTS], *[outs[n][3] for n in WEIGHTS])
```

```python
import functools

import jax
import jax.numpy as jnp
from jax import lax
from jax.experimental import pallas as pl
from jax.experimental.pallas import tpu as pltpu

F32 = jnp.float32
BF16 = jnp.bfloat16
NEG_INF = float("-inf")

D_MODEL = 2048
A_HEADS, A_KV_HEADS, A_HEAD_DIM, WINDOW = 16, 2, 64, 128
B_HEADS, B_HEAD_DIM, CONV_K, DN_CHUNK = 8, 128, 4, 64
C_GROUPS, C_CHUNK = 8, 128
C_GROUP_DIM = D_MODEL // C_GROUPS
D_FF = 5632
EPS = 1e-6
A_Q = A_HEADS * A_HEAD_DIM
A_KV = A_KV_HEADS * A_HEAD_DIM
B_W = B_HEADS * B_HEAD_DIM
EVEN_IN = A_Q + 2 * A_KV + 4 * B_W + 2 * B_HEADS
EVEN_IN_PAD = 5632
COL_KV = A_Q
COL_QKVB = A_Q + 2 * A_KV
COL_Z = COL_QKVB + 3 * B_W
COL_GATE = COL_Z + B_W
N_SHARD = 4

ADAM_LR, ADAM_B1, ADAM_B2, ADAM_EPS, ADAM_WD, ADAM_STEP = 0.001, 0.9, 0.999, 1e-08, 0.01, 10

VMEM_LIMIT_V7X = 56 * 1024 * 1024
MESH_ID = pl.DeviceIdType.MESH


def _params(sem=None):
    return pltpu.CompilerParams(dimension_semantics=sem, vmem_limit_bytes=VMEM_LIMIT_V7X)


def _sigmoid(x):
    return 1.0 / (1.0 + jnp.exp(-x))


def _silu(x):
    return x * _sigmoid(x)


def _dsilu(x):
    s = _sigmoid(x)
    return s * (1.0 + x * (1.0 - s))


def _gelu(x):
    return 0.5 * x * (1.0 + lax.erf(x * 0.7071067811865476))


def _dgelu(x):
    return 0.5 * (1.0 + lax.erf(x * 0.7071067811865476)) + x * jnp.exp(-0.5 * x * x) * 0.3989422804014327


def _dot(a, b, dims):
    if a.ndim == 3:
        (ca,), (cb,) = dims
        return lax.dot_general(a, b, (((ca + 1,), (cb + 1,)), ((0,), (0,))), preferred_element_type=F32)
    return lax.dot_general(a, b, (dims, ((), ())), preferred_element_type=F32)


NN = ((1,), (0,))
NT = ((1,), (1,))
TN = ((0,), (0,))


def _as3(b):
    return b if b.ndim == 3 else b[None]


def _accumulate(step, nsteps, accs, products, finish):
    if nsteps == 1:
        finish(products())
        return

    @pl.when(step == 0)
    def _():
        for acc, p in zip(accs, products()):
            acc[...] = p

    if nsteps > 2:
        @pl.when((step > 0) & (step < nsteps - 1))
        def _():
            for acc, p in zip(accs, products()):
                acc[...] += p

    @pl.when(step == nsteps - 1)
    def _():
        finish(tuple(acc[...] + p for acc, p in zip(accs, products())))


def mm_nn(a, b, *, tm, tn, tk, out_dtype, name, res=None, act=None):
    b3 = _as3(b)
    m, k = a.shape
    s, k2, ns = b3.shape
    assert k2 == k and m % tm == 0 and ns % tn == 0 and k % tk == 0, (a.shape, b3.shape, tm, tn, tk)
    nps, nk = ns // tn, k // tk

    def body(*refs):
        if res is None:
            a_ref, b_ref, o_ref, acc = refs
        else:
            a_ref, b_ref, r_ref, o_ref, acc = refs
        def finish(tiles):
            r = tiles[0] if res is None else tiles[0] + r_ref[...].astype(F32)
            o_ref[...] = r.astype(out_dtype)

        _accumulate(pl.program_id(2), nk, (acc,),
                    lambda: (_dot(a_ref[...].astype(BF16), b_ref[...].astype(BF16), NN),), finish)

    in_specs = [pl.BlockSpec((tm, tk), lambda i, j, kk: (i, kk)),
                pl.BlockSpec((None, tk, tn), lambda i, j, kk: (j // nps, kk, j % nps))]
    args = [a, b3]
    if res is not None:
        in_specs.append(pl.BlockSpec((tm, tn), lambda i, j, kk: (i, j)))
        args.append(res)
    return pl.pallas_call(
        body, name=name, grid=(m // tm, s * nps, nk), in_specs=in_specs,
        out_specs=pl.BlockSpec((tm, tn), lambda i, j, kk: (i, j)),
        out_shape=jax.ShapeDtypeStruct((m, s * ns), out_dtype),
        scratch_shapes=[pltpu.VMEM((tm, tn), F32)],
        compiler_params=_params(("parallel", "parallel", "arbitrary")))(*args)


def mm_nt(a, b, *, tm, tn, tk, out_dtype, name, res=None):
    b3 = _as3(b)
    m, n = a.shape
    s, k, ns = b3.shape
    assert n == s * ns and m % tm == 0 and k % tn == 0 and ns % tk == 0, (a.shape, b3.shape, tm, tn, tk)
    rps = ns // tk
    nr = s * rps

    def body(*refs):
        if res is None:
            a_ref, b_ref, o_ref, acc = refs
        else:
            a_ref, b_ref, r_ref, o_ref, acc = refs
        def finish(tiles):
            r = tiles[0] if res is None else tiles[0] + r_ref[...].astype(F32)
            o_ref[...] = r.astype(out_dtype)

        _accumulate(pl.program_id(2), nr, (acc,),
                    lambda: (_dot(a_ref[...].astype(BF16), b_ref[...].astype(BF16), NT),), finish)

    in_specs = [pl.BlockSpec((tm, tk), lambda i, j, r: (i, r)),
                pl.BlockSpec((None, tn, tk), lambda i, j, r: (r // rps, j, r % rps))]
    args = [a, b3]
    if res is not None:
        in_specs.append(pl.BlockSpec((tm, tn), lambda i, j, r: (i, j)))
        args.append(res)
    return pl.pallas_call(
        body, name=name, grid=(m // tm, k // tn, nr), in_specs=in_specs,
        out_specs=pl.BlockSpec((tm, tn), lambda i, j, r: (i, j)),
        out_shape=jax.ShapeDtypeStruct((m, k), out_dtype),
        scratch_shapes=[pltpu.VMEM((tm, tn), F32)],
        compiler_params=_params(("parallel", "parallel", "arbitrary")))(*args)


def mm_tn(a, b, *, shards, tm, tn, tk, out_dtype, name):
    m, k = a.shape
    m2, n = b.shape
    ns = n // shards
    assert m2 == m and n == shards * ns and m % tm == 0 and k % tk == 0 and ns % tn == 0, (a.shape, b.shape)
    nps, nm = ns // tn, m // tm

    def body(a_ref, b_ref, o_ref, acc):
        def finish(tiles):
            o_ref[...] = tiles[0].astype(out_dtype)

        _accumulate(pl.program_id(2), nm, (acc,),
                    lambda: (_dot(a_ref[...].astype(BF16), b_ref[...].astype(BF16), TN),), finish)

    return pl.pallas_call(
        body, name=name, grid=(k // tk, shards * nps, nm),
        in_specs=[pl.BlockSpec((tm, tk), lambda i, j, mi: (mi, i)),
                  pl.BlockSpec((tm, tn), lambda i, j, mi: (mi, j))],
        out_specs=pl.BlockSpec((None, tk, tn), lambda i, j, mi: (j // nps, i, j % nps)),
        out_shape=jax.ShapeDtypeStruct((shards, k, ns), out_dtype),
        scratch_shapes=[pltpu.VMEM((tk, tn), F32)],
        compiler_params=_params(("parallel", "parallel", "arbitrary")))(a, b)


def mm_gate_up(hn, wg, wu, *, tm, tn, tk, name):
    wg3, wu3 = _as3(wg), _as3(wu)
    m, k = hn.shape
    s, _, ns = wg3.shape
    assert m % tm == 0 and ns % tn == 0 and k % tk == 0
    nps, nk = ns // tn, k // tk

    def body(a_ref, g_ref, u_ref, og_ref, ou_ref, oa_ref, accg, accu):
        def products():
            a = a_ref[...].astype(BF16)
            return _dot(a, g_ref[...].astype(BF16), NN), _dot(a, u_ref[...].astype(BF16), NN)

        def finish(tiles):
            g, u = tiles
            og_ref[...] = g.astype(BF16)
            ou_ref[...] = u.astype(BF16)
            oa_ref[...] = (_silu(g) * u).astype(BF16)

        _accumulate(pl.program_id(2), nk, (accg, accu), products, finish)

    wspec = pl.BlockSpec((None, tk, tn), lambda i, j, kk: (j // nps, kk, j % nps))
    ospec = pl.BlockSpec((tm, tn), lambda i, j, kk: (i, j))
    osh = jax.ShapeDtypeStruct((m, s * ns), BF16)
    return pl.pallas_call(
        body, name=name, grid=(m // tm, s * nps, nk),
        in_specs=[pl.BlockSpec((tm, tk), lambda i, j, kk: (i, kk)), wspec, wspec],
        out_specs=[ospec, ospec, ospec], out_shape=[osh, osh, osh],
        scratch_shapes=[pltpu.VMEM((tm, tn) if nk > 1 else (8, 128), F32)] * 2,
        compiler_params=_params(("parallel", "parallel", "arbitrary")))(hn, wg3, wu3)


def mm_down_bwd(dh, wd, gate, up, *, tm, tn, tk, name):
    m, d = dh.shape
    f, d2 = wd.shape
    assert d2 == d and m % tm == 0 and f % tn == 0 and d % tk == 0
    nr = d // tk

    def body(a_ref, b_ref, g_ref, u_ref, og_ref, ou_ref, acc):
        def finish(tiles):
            da = tiles[0]
            g, u = g_ref[...].astype(F32), u_ref[...].astype(F32)
            s = _sigmoid(g)
            og_ref[...] = (da * u * (s * (1.0 + g * (1.0 - s)))).astype(BF16)
            ou_ref[...] = (da * (g * s)).astype(BF16)

        _accumulate(pl.program_id(2), nr, (acc,),
                    lambda: (_dot(a_ref[...].astype(BF16), b_ref[...].astype(BF16), NT),), finish)

    ospec = pl.BlockSpec((tm, tn), lambda i, j, r: (i, j))
    osh = jax.ShapeDtypeStruct((m, f), BF16)
    return pl.pallas_call(
        body, name=name, grid=(m // tm, f // tn, nr),
        in_specs=[pl.BlockSpec((tm, tk), lambda i, j, r: (i, r)),
                  pl.BlockSpec((tn, tk), lambda i, j, r: (j, r)), ospec, ospec],
        out_specs=[ospec, ospec], out_shape=[osh, osh],
        scratch_shapes=[pltpu.VMEM((tm, tn), F32)],
        compiler_params=_params(("parallel", "parallel", "arbitrary")))(dh, wd, gate, up)


ROWS = 256


def rms_fwd(x, g, *, name):
    t, d = x.shape

    def body(x_ref, g_ref, o_ref):
        xv = x_ref[...]
        r = lax.rsqrt(jnp.mean(xv * xv, axis=-1, keepdims=True) + EPS)
        o_ref[...] = (xv * r * g_ref[...]).astype(BF16)

    return pl.pallas_call(
        body, name=name, grid=(t // ROWS,),
        in_specs=[pl.BlockSpec((ROWS, d), lambda i: (i, 0)), pl.BlockSpec((1, d), lambda i: (0, 0))],
        out_specs=pl.BlockSpec((ROWS, d), lambda i: (i, 0)),
        out_shape=jax.ShapeDtypeStruct((t, d), BF16), compiler_params=_params(("parallel",)))(x, g)


def rms_bwd(x, g, dy, dres, *, name):
    t, d = x.shape

    def body(x_ref, g_ref, dy_ref, dr_ref, dx_ref, dg_ref):
        @pl.when(pl.program_id(0) == 0)
        def _():
            dg_ref[...] = jnp.zeros_like(dg_ref)

        xv, dyv = x_ref[...], dy_ref[...].astype(F32)
        r = lax.rsqrt(jnp.mean(xv * xv, axis=-1, keepdims=True) + EPS)
        dyg = dyv * g_ref[...]
        dx = r * dyg - xv * (r * r * r) * jnp.mean(dyg * xv, axis=-1, keepdims=True)
        dx_ref[...] = dx + dr_ref[...]
        dg_ref[...] += jnp.sum(dyv * xv * r, axis=0, keepdims=True)

    row = pl.BlockSpec((ROWS, d), lambda i: (i, 0))
    vec = pl.BlockSpec((1, d), lambda i: (0, 0))
    return pl.pallas_call(
        body, name=name, grid=(t // ROWS,), in_specs=[row, vec, row, row], out_specs=[row, vec],
        out_shape=[jax.ShapeDtypeStruct((t, d), F32), jax.ShapeDtypeStruct((1, d), F32)],
        compiler_params=_params(("arbitrary",)))(x, g, dy, dres)


def loss_head(h, g, target, *, name):
    t, d = h.shape

    def body(x_ref, g_ref, t_ref, loss_ref, dx_ref, dg_ref):
        @pl.when(pl.program_id(0) == 0)
        def _():
            dg_ref[...] = jnp.zeros_like(dg_ref)
            loss_ref[...] = jnp.zeros_like(loss_ref)

        xv, gv = x_ref[...], g_ref[...]
        r = lax.rsqrt(jnp.mean(xv * xv, axis=-1, keepdims=True) + EPS)
        e = xv * r * gv - t_ref[...]
        loss_ref[...] += 0.5 * jnp.sum(jnp.mean(e * e, axis=-1, keepdims=True), axis=0, keepdims=True)
        dyv = e * (1.0 / d)
        dyg = dyv * gv
        dx_ref[...] = r * dyg - xv * (r * r * r) * jnp.mean(dyg * xv, axis=-1, keepdims=True)
        dg_ref[...] += jnp.sum(dyv * xv * r, axis=0, keepdims=True)

    row = pl.BlockSpec((ROWS, d), lambda i: (i, 0))
    vec = pl.BlockSpec((1, d), lambda i: (0, 0))
    return pl.pallas_call(
        body, name=name, grid=(t // ROWS,), in_specs=[row, vec, row],
        out_specs=[pl.BlockSpec((1, 128), lambda i: (0, 0)), row, vec],
        out_shape=[jax.ShapeDtypeStruct((1, 128), F32), jax.ShapeDtypeStruct((t, d), F32),
                   jax.ShapeDtypeStruct((1, d), F32)],
        compiler_params=_params(("arbitrary",)))(h, g, target)


def _tril_mask():
    r = lax.broadcasted_iota(jnp.int32, (C_CHUNK, C_CHUNK), 0)
    c = lax.broadcasted_iota(jnp.int32, (C_CHUNK, C_CHUNK), 1)
    return r >= c


def _layer_norm_parts(v):
    mu = jnp.mean(v, axis=-1, keepdims=True)
    vc = v - mu
    rstd = lax.rsqrt(jnp.mean(vc * vc, axis=-1, keepdims=True) + EPS)
    return vc * rstd, rstd


def gmlp_fwd(zpre, ln_g, ln_b, ws, bs_t, *, name):
    t = zpre.shape[0]
    d = D_MODEL

    def body(zu_ref, zv_ref, g_ref, b_ref, ws_ref, bs_ref, o_ref):
        u = _gelu(zu_ref[...])
        vhat, _ = _layer_norm_parts(_gelu(zv_ref[...]))
        vln = (vhat * g_ref[...] + b_ref[...]).astype(BF16)
        mask = _tril_mask()
        for gi in range(C_GROUPS):
            sl = slice(gi * C_GROUP_DIM, (gi + 1) * C_GROUP_DIM)
            w = jnp.where(mask, ws_ref[gi], 0.0).astype(BF16)
            mixed = _dot(w, vln[:, sl], NN) + bs_ref[:, gi:gi + 1]
            o_ref[:, sl] = (u[:, sl] * mixed).astype(BF16)

    vec = pl.BlockSpec((1, d), lambda i: (0, 0))
    return pl.pallas_call(
        body, name=name, grid=(t // C_CHUNK,),
        in_specs=[pl.BlockSpec((C_CHUNK, d), lambda i: (i, 0)), pl.BlockSpec((C_CHUNK, d), lambda i: (i, 1)),
                  vec, vec, pl.BlockSpec((C_GROUPS, C_CHUNK, C_CHUNK), lambda i: (0, 0, 0)),
                  pl.BlockSpec((C_CHUNK, 128), lambda i: (0, 0))],
        out_specs=pl.BlockSpec((C_CHUNK, d), lambda i: (i, 0)),
        out_shape=jax.ShapeDtypeStruct((t, d), BF16), compiler_params=_params(("parallel",)))(
            zpre, zpre, ln_g, ln_b, ws, bs_t)


def gmlp_bwd(zpre, dgated, ln_g, ln_b, ws, bs_t, *, name):
    t = zpre.shape[0]
    d = D_MODEL

    def body(zu_ref, zv_ref, dg_ref, g_ref, b_ref, ws_ref, bs_ref, dz_ref, dws_ref, dbs_ref, dlg_ref, dlb_ref):
        @pl.when(pl.program_id(0) == 0)
        def _():
            dws_ref[...] = jnp.zeros_like(dws_ref)
            dbs_ref[...] = jnp.zeros_like(dbs_ref)
            dlg_ref[...] = jnp.zeros_like(dlg_ref)
            dlb_ref[...] = jnp.zeros_like(dlb_ref)

        zu, zv = zu_ref[...], zv_ref[...]
        u = _gelu(zu)
        vhat, rstd = _layer_norm_parts(_gelu(zv))
        gam = g_ref[...]
        vln = (vhat * gam + b_ref[...]).astype(BF16)
        dgt = dg_ref[...].astype(F32)
        mask = _tril_mask()
        lane = lax.broadcasted_iota(jnp.int32, (C_CHUNK, 128), 1)
        dbs = jnp.zeros((C_CHUNK, 128), F32)
        du_parts, dvln_parts = [], []
        for gi in range(C_GROUPS):
            sl = slice(gi * C_GROUP_DIM, (gi + 1) * C_GROUP_DIM)
            w = jnp.where(mask, ws_ref[gi], 0.0).astype(BF16)
            mixed = _dot(w, vln[:, sl], NN) + bs_ref[:, gi:gi + 1]
            du_parts.append(dgt[:, sl] * mixed)
            dmixed = dgt[:, sl] * u[:, sl]
            dmb = dmixed.astype(BF16)
            dws_ref[gi] += jnp.where(mask, _dot(dmb, vln[:, sl], NT), 0.0)
            dbs = dbs + jnp.where(lane == gi, jnp.sum(dmixed, axis=-1, keepdims=True), 0.0)
            dvln_parts.append(_dot(w, dmb, TN))
        dbs_ref[...] += dbs
        du = jnp.concatenate(du_parts, axis=-1)
        dvln = jnp.concatenate(dvln_parts, axis=-1)
        dlg_ref[...] += jnp.sum(dvln * vhat, axis=0, keepdims=True)
        dlb_ref[...] += jnp.sum(dvln, axis=0, keepdims=True)
        dvhat = dvln * gam
        dv = rstd * (dvhat - jnp.mean(dvhat, axis=-1, keepdims=True)
                     - vhat * jnp.mean(dvhat * vhat, axis=-1, keepdims=True))
        dz_ref[:, :d] = (du * _dgelu(zu)).astype(BF16)
        dz_ref[:, d:] = (dv * _dgelu(zv)).astype(BF16)

    vec = pl.BlockSpec((1, d), lambda i: (0, 0))
    wsp = pl.BlockSpec((C_GROUPS, C_CHUNK, C_CHUNK), lambda i: (0, 0, 0))
    bsp = pl.BlockSpec((C_CHUNK, 128), lambda i: (0, 0))
    return pl.pallas_call(
        body, name=name, grid=(t // C_CHUNK,),
        in_specs=[pl.BlockSpec((C_CHUNK, d), lambda i: (i, 0)), pl.BlockSpec((C_CHUNK, d), lambda i: (i, 1)),
                  pl.BlockSpec((C_CHUNK, d), lambda i: (i, 0)), vec, vec, wsp, bsp],
        out_specs=[pl.BlockSpec((C_CHUNK, 2 * d), lambda i: (i, 0)), wsp, bsp, vec, vec],
        out_shape=[jax.ShapeDtypeStruct((t, 2 * d), BF16), jax.ShapeDtypeStruct((C_GROUPS, C_CHUNK, C_CHUNK), F32),
                   jax.ShapeDtypeStruct((C_CHUNK, 128), F32), jax.ShapeDtypeStruct((1, d), F32),
                   jax.ShapeDtypeStruct((1, d), F32)],
        compiler_params=_params(("arbitrary",)))(zpre, zpre, dgated, ln_g, ln_b, ws, bs_t)


ATT_SCALE = A_HEAD_DIM ** -0.5
PAIRS = A_HEADS // 2
PAIRS_PER_KV = PAIRS // A_KV_HEADS


def _att_padded(tile):
    lo = lax.broadcasted_iota(jnp.int32, tile.shape, 1) < A_HEAD_DIM
    rolled = pltpu.roll(tile, A_HEAD_DIM, 1)
    zero = jnp.zeros_like(tile)
    return {(0, 0): jnp.where(lo, tile, zero).astype(BF16), (0, 1): jnp.where(lo, zero, rolled).astype(BF16),
            (1, 0): jnp.where(lo, rolled, zero).astype(BF16), (1, 1): jnp.where(lo, zero, tile).astype(BF16)}


def _att_valid(n):
    r = lax.broadcasted_iota(jnp.int32, (WINDOW, 2 * WINDOW), 0)
    c = lax.broadcasted_iota(jnp.int32, (WINDOW, 2 * WINDOW), 1)
    rel = r + WINDOW - c
    return (rel >= 0) & (rel < WINDOW) & ((c >= WINDOW) | (n > 0))


def _att_probs(qp, kpad, sink, valid):
    s = jnp.where(valid, _dot(qp, kpad, NT), NEG_INF)
    m = jnp.maximum(jnp.max(s, axis=-1, keepdims=True), sink)
    p = jnp.exp(s - m)
    e_sink = jnp.exp(sink - m)
    inv = 1.0 / (jnp.sum(p, axis=-1, keepdims=True) + e_sink)
    return p * inv, e_sink * inv


def _att_operands(q_ref, kvc_ref, kvp_ref, s_ref):
    kv = jnp.concatenate([kvp_ref[...], kvc_ref[...]], axis=0)
    kpad, vpad = _att_padded(kv[:, :128]), _att_padded(kv[:, 128:])
    key = lambda h: ((h // 2) // PAIRS_PER_KV, h % 2)
    pairs = [(q_ref[:, j * 128:(j + 1) * 128] * ATT_SCALE).astype(BF16) for j in range(PAIRS)]
    q = jnp.stack([pairs[h // 2] for h in range(A_HEADS)])
    k = jnp.stack([kpad[key(h)] for h in range(A_HEADS)])
    v = jnp.stack([vpad[key(h)] for h in range(A_HEADS)])
    sink = jnp.stack([s_ref[:, h:h + 1] for h in range(A_HEADS)])
    return q, k, v, sink


def _att_specs(t):
    return [pl.BlockSpec((WINDOW, A_Q), lambda n: (n, 0)),
            pl.BlockSpec((WINDOW, 2 * A_KV), lambda n: (n, COL_KV // (2 * A_KV))),
            pl.BlockSpec((WINDOW, 2 * A_KV), lambda n: (jnp.maximum(n - 1, 0), COL_KV // (2 * A_KV))),
            pl.BlockSpec((1, 128), lambda n: (0, 0))]


def att_fwd(proj, sinks, *, name):
    t = proj.shape[0]

    def body(q_ref, kvc_ref, kvp_ref, s_ref, o_ref):
        n = pl.program_id(0)
        q, k, v, sink = _att_operands(q_ref, kvc_ref, kvp_ref, s_ref)
        w, _ = _att_probs(q, k, sink, _att_valid(n))
        o = _dot(w.astype(BF16), v, NN)
        for j in range(PAIRS):
            o_ref[:, j * 128:(j + 1) * 128] = (o[2 * j] + o[2 * j + 1]).astype(BF16)

    return pl.pallas_call(
        body, name=name, grid=(t // WINDOW,), in_specs=_att_specs(t),
        out_specs=pl.BlockSpec((WINDOW, A_Q), lambda n: (n, 0)),
        out_shape=jax.ShapeDtypeStruct((t, A_Q), BF16), compiler_params=_params(("parallel",)))(
            proj, proj, proj, sinks)


def att_bwd(proj, sinks, dout, *, name):
    t = proj.shape[0]

    def body(q_ref, kvc_ref, kvp_ref, s_ref, do_ref, dq_ref, dkc_ref, dkp_ref, ds_ref):
        n = pl.program_id(0)

        @pl.when(n == 0)
        def _():
            ds_ref[...] = jnp.zeros_like(ds_ref)

        q, k, v, sink = _att_operands(q_ref, kvc_ref, kvp_ref, s_ref)
        dop = jnp.stack([do_ref[:, (h // 2) * 128:(h // 2 + 1) * 128] for h in range(A_HEADS)]).astype(BF16)
        w, w_sink = _att_probs(q, k, sink, _att_valid(n))
        dw = _dot(dop, v, NT)
        delta = jnp.sum(w * dw, axis=-1, keepdims=True)
        dsc = (w * (dw - delta)).astype(BF16)
        dsink_h = -jnp.sum(w_sink * delta, axis=1, keepdims=True)
        dq = _dot(dsc, k, NN)
        dk_h = _dot(dsc, q, TN)
        dv_h = _dot(w.astype(BF16), dop, TN)
        lane = lax.broadcasted_iota(jnp.int32, (1, 128), 1)
        dsink = jnp.zeros((1, 128), F32)
        for h in range(A_HEADS):
            dsink = dsink + jnp.where(lane == h, dsink_h[h], 0.0)
        ds_ref[...] += dsink
        for j in range(PAIRS):
            dq_ref[:, j * 128:(j + 1) * 128] = ((dq[2 * j] + dq[2 * j + 1]) * ATT_SCALE).astype(BF16)
        lo = lax.broadcasted_iota(jnp.int32, (2 * WINDOW, 128), 1) < A_HEAD_DIM
        heads_per_kv = A_HEADS // A_KV_HEADS

        def tile(per_head):
            acc = {}
            for kvh in range(A_KV_HEADS):
                for half in range(2):
                    hs = range(kvh * heads_per_kv + half, (kvh + 1) * heads_per_kv, 2)
                    acc[(kvh, half)] = functools.reduce(lambda a, b: a + b, [per_head[h] for h in hs])
            return jnp.where(lo, acc[(0, 0)] + pltpu.roll(acc[(0, 1)], A_HEAD_DIM, 1),
                             pltpu.roll(acc[(1, 0)], A_HEAD_DIM, 1) + acc[(1, 1)])

        dkv = jnp.concatenate([tile(dk_h), tile(dv_h)], axis=1)
        dkp_ref[...] = dkv[:WINDOW]
        dkc_ref[...] = dkv[WINDOW:]

    kvo = pl.BlockSpec((WINDOW, 2 * A_KV), lambda n: (n, 0))
    return pl.pallas_call(
        body, name=name, grid=(t // WINDOW,),
        in_specs=_att_specs(t) + [pl.BlockSpec((WINDOW, A_Q), lambda n: (n, 0))],
        out_specs=[pl.BlockSpec((WINDOW, A_Q), lambda n: (n, 0)), kvo, kvo, pl.BlockSpec((1, 128), lambda n: (0, 0))],
        out_shape=[jax.ShapeDtypeStruct((t, A_Q), BF16), jax.ShapeDtypeStruct((t, 2 * A_KV), F32),
                   jax.ShapeDtypeStruct((t, 2 * A_KV), F32), jax.ShapeDtypeStruct((1, 128), F32)],
        compiler_params=_params(("arbitrary",)))(proj, proj, proj, sinks, dout)


QK_SCALE = B_HEAD_DIM ** -0.5
PREP_COLS = 256
PREP_NCB = 3 * B_W // PREP_COLS
HALO = 8
PREP_ROWS = 512


def _roll_rows(x, shift):
    n = x.shape[0]
    return x if shift % n == 0 else pltpu.roll(x, shift % n, 0)


def _conv_taps(xe, w):
    xs = [_roll_rows(xe, CONV_K - 1 - i) for i in range(CONV_K)]
    c = w[0:1] * xs[0]
    for i in range(1, CONV_K):
        c = c + w[i:i + 1] * xs[i]
    return xs, c


def dprep_fwd(proj, conv_w, *, name):
    t = proj.shape[0]
    tt = min(PREP_ROWS, t)
    col0 = COL_QKVB // PREP_COLS

    def body(x_ref, h_ref, w_ref, o_ref):
        cb, n = pl.program_id(0), pl.program_id(1)
        halo = jnp.where(n > 0, h_ref[...], 0.0)
        xe = jnp.concatenate([halo, x_ref[...]], axis=0)
        _, c = _conv_taps(xe, w_ref[...])
        y = _silu(c)[HALO:]
        parts = []
        for hh in range(PREP_COLS // B_HEAD_DIM):
            yh = y[:, hh * B_HEAD_DIM:(hh + 1) * B_HEAD_DIM]
            parts.append(yh * lax.rsqrt(jnp.sum(yh * yh, axis=-1, keepdims=True) + EPS))
        nrm = jnp.concatenate(parts, axis=-1)
        o_ref[...] = jnp.where(cb < 4, nrm * QK_SCALE, jnp.where(cb < 8, nrm, y))

    return pl.pallas_call(
        body, name=name, grid=(PREP_NCB, t // tt),
        in_specs=[pl.BlockSpec((tt, PREP_COLS), lambda cb, n: (n, col0 + cb)),
                  pl.BlockSpec((HALO, PREP_COLS), lambda cb, n: (jnp.maximum(n * (tt // HALO) - 1, 0), col0 + cb)),
                  pl.BlockSpec((CONV_K, PREP_COLS), lambda cb, n: (0, cb))],
        out_specs=pl.BlockSpec((tt, PREP_COLS), lambda cb, n: (n, cb)),
        out_shape=jax.ShapeDtypeStruct((t, 3 * B_W), F32), compiler_params=_params(("parallel", "parallel")))(
            proj, proj, conv_w)


def dprep_bwd(proj, conv_w, dqkvn, *, name):
    t = proj.shape[0]
    tt = min(PREP_ROWS, t)
    nb = t // tt
    col0 = COL_QKVB // PREP_COLS
    n8 = t // HALO

    def body(xc_ref, xb_ref, xa_ref, dc_ref, da_ref, w_ref, dx_ref, dw_ref):
        cb, n = pl.program_id(0), pl.program_id(1)

        @pl.when(n == 0)
        def _():
            dw_ref[...] = jnp.zeros_like(dw_ref)

        w = w_ref[...]
        xe = jnp.concatenate([jnp.where(n > 0, xb_ref[...], 0.0), xc_ref[...], xa_ref[...]], axis=0)
        xs, c = _conv_taps(xe, w)
        sg = _sigmoid(c)
        y = c * sg
        dout = jnp.concatenate([jnp.zeros((HALO, PREP_COLS), F32), dc_ref[...],
                                jnp.where(n < nb - 1, da_ref[...], 0.0)], axis=0)
        dsc = jnp.where(cb < 4, QK_SCALE, 1.0)
        parts = []
        for hh in range(PREP_COLS // B_HEAD_DIM):
            sl = slice(hh * B_HEAD_DIM, (hh + 1) * B_HEAD_DIM)
            yh, doh = y[:, sl], dout[:, sl] * dsc
            r = lax.rsqrt(jnp.sum(yh * yh, axis=-1, keepdims=True) + EPS)
            parts.append(doh * r - yh * (r * r * r) * jnp.sum(doh * yh, axis=-1, keepdims=True))
        dy = jnp.where(cb < 8, jnp.concatenate(parts, axis=-1), dout)
        dcv = dy * sg * (1.0 + c * (1.0 - sg))
        dxe = w[CONV_K - 1:CONV_K] * dcv
        for i in range(CONV_K - 1):
            dxe = dxe + w[i:i + 1] * _roll_rows(dcv, -(CONV_K - 1 - i))
        dx_ref[...] = dxe[HALO:HALO + tt].astype(BF16)
        for i in range(CONV_K):
            dw_ref[i:i + 1, :] += jnp.sum((dcv * xs[i])[HALO:HALO + tt], axis=0, keepdims=True)

    def after(n):
        return jnp.minimum((n + 1) * (tt // HALO), n8 - 1)

    return pl.pallas_call(
        body, name=name, grid=(PREP_NCB, nb),
        in_specs=[pl.BlockSpec((tt, PREP_COLS), lambda cb, n: (n, col0 + cb)),
                  pl.BlockSpec((HALO, PREP_COLS), lambda cb, n: (jnp.maximum(n * (tt // HALO) - 1, 0), col0 + cb)),
                  pl.BlockSpec((HALO, PREP_COLS), lambda cb, n: (after(n), col0 + cb)),
                  pl.BlockSpec((tt, PREP_COLS), lambda cb, n: (n, cb)),
                  pl.BlockSpec((HALO, PREP_COLS), lambda cb, n: (after(n), cb)),
                  pl.BlockSpec((CONV_K, PREP_COLS), lambda cb, n: (0, cb))],
        out_specs=[pl.BlockSpec((tt, PREP_COLS), lambda cb, n: (n, cb)),
                   pl.BlockSpec((CONV_K, PREP_COLS), lambda cb, n: (0, cb))],
        out_shape=[jax.ShapeDtypeStruct((t, 3 * B_W), BF16), jax.ShapeDtypeStruct((CONV_K, 3 * B_W), F32)],
        compiler_params=_params(("parallel", "arbitrary")))(proj, proj, proj, dqkvn, dqkvn, conv_w)


def _softplus(z):
    return jnp.maximum(z, 0.0) + jnp.log(1.0 + jnp.exp(-jnp.abs(z)))


def gates_fwd(proj, alog_pad, dtb_pad, *, name):
    t = proj.shape[0]

    def body(x_ref, a_ref, b_ref, o_ref):
        raw = x_ref[...]
        lane = lax.broadcasted_iota(jnp.int32, raw.shape, 1)
        g = -jnp.exp(a_ref[...]) * _softplus(raw + b_ref[...])
        o_ref[...] = jnp.where(lane < B_HEADS, _sigmoid(raw), jnp.where(lane < 2 * B_HEADS, g, 0.0))

    vec = pl.BlockSpec((1, 128), lambda n: (0, 0))
    return pl.pallas_call(
        body, name=name, grid=(t // ROWS,),
        in_specs=[pl.BlockSpec((ROWS, 128), lambda n: (n, COL_GATE // 128)), vec, vec],
        out_specs=pl.BlockSpec((ROWS, 128), lambda n: (n, 0)),
        out_shape=jax.ShapeDtypeStruct((t, 128), F32), compiler_params=_params(("parallel",)))(
            proj, alog_pad, dtb_pad)


def gates_bwd(proj, alog_pad, dtb_pad, dgates, *, name):
    t = proj.shape[0]

    def body(x_ref, a_ref, b_ref, dg_ref, dx_ref, da_ref, db_ref):
        @pl.when(pl.program_id(0) == 0)
        def _():
            da_ref[...] = jnp.zeros_like(da_ref)
            db_ref[...] = jnp.zeros_like(db_ref)

        raw, dgt = x_ref[...], dg_ref[...]
        lane = lax.broadcasted_iota(jnp.int32, raw.shape, 1)
        is_beta, is_g = lane < B_HEADS, (lane >= B_HEADS) & (lane < 2 * B_HEADS)
        beta = _sigmoid(raw)
        z = raw + b_ref[...]
        neg_a = -jnp.exp(a_ref[...])
        d_z = jnp.where(is_g, dgt * neg_a * _sigmoid(z), 0.0)
        dx_ref[...] = jnp.where(is_beta, dgt * beta * (1.0 - beta), d_z).astype(BF16)
        db_ref[...] += jnp.sum(d_z, axis=0, keepdims=True)
        da_ref[...] += jnp.sum(jnp.where(is_g, dgt * neg_a * _softplus(z), 0.0), axis=0, keepdims=True)

    vec = pl.BlockSpec((1, 128), lambda n: (0, 0))
    row = pl.BlockSpec((ROWS, 128), lambda n: (n, 0))
    return pl.pallas_call(
        body, name=name, grid=(t // ROWS,),
        in_specs=[pl.BlockSpec((ROWS, 128), lambda n: (n, COL_GATE // 128)), vec, vec, row],
        out_specs=[row, vec, vec],
        out_shape=[jax.ShapeDtypeStruct((t, 128), BF16), jax.ShapeDtypeStruct((1, 128), F32),
                   jax.ShapeDtypeStruct((1, 128), F32)],
        compiler_params=_params(("arbitrary",)))(proj, alog_pad, dtb_pad, dgates)


def _split2(a):
    hi = a.astype(BF16)
    return hi, (a - hi.astype(F32)).astype(BF16)


def _dotp(a, b, dims, passes):
    if passes == 1:
        return _dot(a.astype(BF16), b.astype(BF16), dims)
    ah, al = _split2(a)
    bh, bl = _split2(b)
    return _dot(ah, bh, dims) + (_dot(ah, bl, dims) + _dot(al, bh, dims))


_GRAD_DIMS = {NN: ((NT, False), (TN, False)), NT: ((NN, False), (TN, True)), TN: ((NT, True), (NN, False))}


def _make_mm(dims, passes, grad_passes):
    (da_dims, da_swap), (db_dims, db_swap) = _GRAD_DIMS[dims]

    @jax.custom_vjp
    def mm(a, b):
        return _dotp(a, b, dims, passes)

    def fwd(a, b):
        return _dotp(a, b, dims, passes), (a, b)

    def bwd(saved, ct):
        a, b = saved
        da = _dotp(b, ct, da_dims, grad_passes) if da_swap else _dotp(ct, b, da_dims, grad_passes)
        db = _dotp(ct, a, db_dims, grad_passes) if db_swap else _dotp(a, ct, db_dims, grad_passes)
        return da, db

    mm.defvjp(fwd, bwd)
    return mm


MM1 = {d: _make_mm(d, 1, 1) for d in (NN, NT, TN)}
MM3 = {d: _make_mm(d, 3, 1) for d in (NN, NT, TN)}


def _tri_ones(lower):
    r = lax.broadcasted_iota(jnp.int32, (DN_CHUNK, DN_CHUNK), 0)
    c = lax.broadcasted_iota(jnp.int32, (DN_CHUNK, DN_CHUNK), 1)
    return (r >= c if lower else r <= c).astype(BF16)


def _tri_sum(x, lower):
    tri = _tri_ones(lower)
    hi = x.astype(BF16)
    r1 = x - hi.astype(F32)
    mid = r1.astype(BF16)
    lo = (r1 - mid.astype(F32)).astype(BF16)
    return _dot(tri, hi, NN) + (_dot(tri, mid, NN) + _dot(tri, lo, NN))


def _delta_chunk(s0, q, k, v, beta, gam_c, gam_r):
    c = DN_CHUNK
    r = lax.broadcasted_iota(jnp.int32, (c, c), 0)
    cc = lax.broadcasted_iota(jnp.int32, (c, c), 1)
    incl, strict = r >= cc, r > cc
    eye = (r == cc).astype(F32)
    decay = jnp.exp(jnp.where(incl, gam_c - gam_r, NEG_INF))
    g_last = gam_c[:, c - 1:c, :]
    e_gam, e_rest, e_last = jnp.exp(gam_c), jnp.exp(g_last - gam_c), jnp.exp(g_last)
    a_neg = -jnp.where(strict, beta * MM1[NT](k, k) * decay, 0.0)
    inv = eye + a_neg
    pw = a_neg
    for _ in range(5):
        pw = MM3[NN](pw, pw)
        inv = inv + MM3[NN](inv, pw)
    uw = MM3[NN](inv, jnp.concatenate([v * beta, k * (beta * e_gam)], axis=-1))
    u, w = uw[..., :B_HEAD_DIM], uw[..., B_HEAD_DIM:]
    qk = MM1[NT](q, k) * decay
    v_new = u - MM1[NN](w, s0)
    o = MM1[NN](q * e_gam, s0) + MM1[NN](qk, v_new)
    s1 = s0 * e_last + MM1[TN](k * e_rest, v_new)
    return s1, o


def _delta_operands(q_ref, k_ref, v_ref, gt):
    heads = lambda ref: jnp.stack([ref[:, h * B_HEAD_DIM:(h + 1) * B_HEAD_DIM] for h in range(B_HEADS)])
    gam = _tri_sum(gt, True)
    gam_t = gam.T
    beta = jnp.stack([gt[:, h:h + 1] for h in range(B_HEADS)])
    gam_c = jnp.stack([gam[:, B_HEADS + h:B_HEADS + h + 1] for h in range(B_HEADS)])
    gam_r = jnp.stack([gam_t[B_HEADS + h:B_HEADS + h + 1, :] for h in range(B_HEADS)])
    return heads(q_ref), heads(k_ref), heads(v_ref), beta, gam_c, gam_r


def delta_fwd(qkvn, gates, *, name):
    t = qkvn.shape[0]
    nc = t // DN_CHUNK

    def body(q_ref, k_ref, v_ref, g_ref, o_ref, ss_ref, state):
        @pl.when(pl.program_id(0) == 0)
        def _():
            state[...] = jnp.zeros_like(state)

        s0 = state[...]
        ss_ref[...] = s0
        s1, o = _delta_chunk(s0, *_delta_operands(q_ref, k_ref, v_ref, g_ref[...]))
        state[...] = s1
        for h in range(B_HEADS):
            o_ref[:, h * B_HEAD_DIM:(h + 1) * B_HEAD_DIM] = o[h]

    blk = lambda j: pl.BlockSpec((DN_CHUNK, B_W), lambda n: (n, j))
    return pl.pallas_call(
        body, name=name, grid=(nc,),
        in_specs=[blk(0), blk(1), blk(2), pl.BlockSpec((DN_CHUNK, 128), lambda n: (n, 0))],
        out_specs=[blk(0), pl.BlockSpec((None, B_HEADS, B_HEAD_DIM, B_HEAD_DIM), lambda n: (n, 0, 0, 0))],
        out_shape=[jax.ShapeDtypeStruct((t, B_W), F32),
                   jax.ShapeDtypeStruct((nc, B_HEADS, B_HEAD_DIM, B_HEAD_DIM), F32)],
        scratch_shapes=[pltpu.VMEM((B_HEADS, B_HEAD_DIM, B_HEAD_DIM), F32)],
        compiler_params=_params(("arbitrary",)))(qkvn, qkvn, qkvn, gates)


def delta_bwd(qkvn, gates, ssave, do, *, name):
    t = qkvn.shape[0]
    nc = t // DN_CHUNK

    def body(q_ref, k_ref, v_ref, g_ref, ss_ref, do_ref, dx_ref, dg_ref, dstate):
        @pl.when(pl.program_id(0) == 0)
        def _():
            dstate[...] = jnp.zeros_like(dstate)

        lane = lax.broadcasted_iota(jnp.int32, (DN_CHUNK, 128), 1)
        row = lax.broadcasted_iota(jnp.int32, (128, DN_CHUNK), 0)
        dbeta_all = jnp.zeros((DN_CHUNK, 128), F32)
        dgam_c_all = jnp.zeros((DN_CHUNK, 128), F32)
        dgam_r_all = jnp.zeros((128, DN_CHUNK), F32)
        _, vjp = jax.vjp(_delta_chunk, ss_ref[...], *_delta_operands(q_ref, k_ref, v_ref, g_ref[...]))
        do = jnp.stack([do_ref[:, h * B_HEAD_DIM:(h + 1) * B_HEAD_DIM] for h in range(B_HEADS)])
        ds0, dq, dk, dv, dbeta, dgam_c, dgam_r = vjp((dstate[...], do))
        dstate[...] = ds0
        for h in range(B_HEADS):
            dx_ref[:, h * B_HEAD_DIM:(h + 1) * B_HEAD_DIM] = dq[h]
            dx_ref[:, B_W + h * B_HEAD_DIM:B_W + (h + 1) * B_HEAD_DIM] = dk[h]
            dx_ref[:, 2 * B_W + h * B_HEAD_DIM:2 * B_W + (h + 1) * B_HEAD_DIM] = dv[h]
            dbeta_all = dbeta_all + jnp.where(lane == h, dbeta[h], 0.0)
            dgam_c_all = dgam_c_all + jnp.where(lane == B_HEADS + h, dgam_c[h], 0.0)
            dgam_r_all = dgam_r_all + jnp.where(row == B_HEADS + h, dgam_r[h], 0.0)
        dg_ref[...] = dbeta_all + _tri_sum(dgam_c_all + dgam_r_all.T, False)

    blk = lambda j: pl.BlockSpec((DN_CHUNK, B_W), lambda n: (nc - 1 - n, j))
    gsp = pl.BlockSpec((DN_CHUNK, 128), lambda n: (nc - 1 - n, 0))
    return pl.pallas_call(
        body, name=name, grid=(nc,),
        in_specs=[blk(0), blk(1), blk(2), gsp,
                  pl.BlockSpec((None, B_HEADS, B_HEAD_DIM, B_HEAD_DIM), lambda n: (nc - 1 - n, 0, 0, 0)), blk(0)],
        out_specs=[pl.BlockSpec((DN_CHUNK, 3 * B_W), lambda n: (nc - 1 - n, 0)), gsp],
        out_shape=[jax.ShapeDtypeStruct((t, 3 * B_W), F32), jax.ShapeDtypeStruct((t, 128), F32)],
        scratch_shapes=[pltpu.VMEM((B_HEADS, B_HEAD_DIM, B_HEAD_DIM), F32)],
        compiler_params=_params(("arbitrary",)))(qkvn, qkvn, qkvn, gates, ssave, do)


GNORM_ROWS = 1024


def gnorm_fwd(o, proj, onorm, *, name):
    t = o.shape[0]

    def body(o_ref, z_ref, w_ref, out_ref):
        ov = o_ref[...]
        r = lax.rsqrt(jnp.mean(ov * ov, axis=-1, keepdims=True) + EPS)
        out_ref[...] = (ov * r * w_ref[...] * _silu(z_ref[...])).astype(BF16)

    rows = min(GNORM_ROWS, t)
    blk = pl.BlockSpec((rows, B_HEAD_DIM), lambda n, h: (n, h))
    return pl.pallas_call(
        body, name=name, grid=(t // rows, B_HEADS),
        in_specs=[blk, pl.BlockSpec((rows, B_HEAD_DIM), lambda n, h: (n, COL_Z // B_HEAD_DIM + h)),
                  pl.BlockSpec((1, B_HEAD_DIM), lambda n, h: (0, 0))],
        out_specs=blk, out_shape=jax.ShapeDtypeStruct((t, B_W), BF16),
        compiler_params=_params(("parallel", "parallel")))(o, proj, onorm)


def gnorm_bwd(o, proj, onorm, dout, *, dcol0, name):
    t = o.shape[0]

    def body(o_ref, z_ref, w_ref, d_ref, do_ref, dz_ref, dw_ref):
        @pl.when((pl.program_id(0) == 0) & (pl.program_id(1) == 0))
        def _():
            dw_ref[...] = jnp.zeros_like(dw_ref)

        ov, zv, wv, dv = o_ref[...], z_ref[...], w_ref[...], d_ref[...].astype(F32)
        r = lax.rsqrt(jnp.mean(ov * ov, axis=-1, keepdims=True) + EPS)
        nrm = ov * r
        dz_ref[...] = (dv * nrm * wv * _dsilu(zv)).astype(BF16)
        da = dv * _silu(zv)
        dw_ref[...] += jnp.sum(da * nrm, axis=0, keepdims=True)
        dn = da * wv
        do_ref[...] = r * dn - ov * (r * r * r) * jnp.mean(dn * ov, axis=-1, keepdims=True)

    rows = min(GNORM_ROWS, t)
    blk = pl.BlockSpec((rows, B_HEAD_DIM), lambda n, h: (n, h))
    vec = pl.BlockSpec((1, B_HEAD_DIM), lambda n, h: (0, 0))
    return pl.pallas_call(
        body, name=name, grid=(t // rows, B_HEADS),
        in_specs=[blk, pl.BlockSpec((rows, B_HEAD_DIM), lambda n, h: (n, COL_Z // B_HEAD_DIM + h)), vec,
                  pl.BlockSpec((rows, B_HEAD_DIM), lambda n, h: (n, dcol0 // B_HEAD_DIM + h))],
        out_specs=[blk, blk, vec],
        out_shape=[jax.ShapeDtypeStruct((t, B_W), F32), jax.ShapeDtypeStruct((t, B_W), BF16),
                   jax.ShapeDtypeStruct((1, B_HEAD_DIM), F32)],
        compiler_params=_params(("arbitrary", "arbitrary")))(o, proj, onorm, dout)


def _ffn_fwd(h, norm_g, wg, wu, wd, tm, tag):
    hn = rms_fwd(h, norm_g, name=f"ffn{tag}_norm")
    gate, up, act = mm_gate_up(hn, wg, wu, tm=min(512, tm), tn=1408, tk=2048, name=f"ffn{tag}_gate_up")
    h_out = mm_nn(act, wd, tm=tm, tn=2048, tk=512, out_dtype=F32, res=h, name=f"ffn{tag}_down")
    return h_out, (hn, gate, up, act)


def _ffn_bwd(dh, h, norm_g, wg, wu, wd, saved, tm, tag, emit):
    hn, gate, up, act = saved
    dwd = mm_tn(act, dh, shards=1, tm=tm, tn=1024, tk=1408, out_dtype=BF16, name=f"ffn{tag}_dwd")[0]
    dgate, dup = mm_down_bwd(dh, wd, gate, up, tm=tm, tn=512, tk=2048, name=f"ffn{tag}_dact")
    dwg = mm_tn(hn, dgate, shards=N_SHARD, tm=tm, tn=1408, tk=1024, out_dtype=BF16, name=f"ffn{tag}_dwg")
    dwu = mm_tn(hn, dup, shards=N_SHARD, tm=tm, tn=1408, tk=1024, out_dtype=BF16, name=f"ffn{tag}_dwu")
    started = emit(f"ffn{tag}", {"gate": dwg, "up": dwu, "down": dwd})
    dhn = mm_nt(dgate, wg, tm=tm, tn=1024, tk=1408, out_dtype=F32, name=f"ffn{tag}_dhn_g")
    dhn = mm_nt(dup, wu, tm=tm, tn=1024, tk=1408, out_dtype=F32, res=dhn, name=f"ffn{tag}_dhn_u")
    dh_in, dnorm = rms_bwd(h, norm_g + started, dhn, dh, name=f"ffn{tag}_dnorm")
    return dh_in, dnorm


def _local_step(x, target, w, get, emit):
    t = x.shape[0]
    tm = min(1024, t)
    g = {}

    hn0 = rms_fwd(x, w["even_norm"], name="l0_norm")
    w.update(get("even_in", hn0))
    proj = mm_nt(hn0, w["even_w_in"], tm=tm, tn=512, tk=2048, out_dtype=F32, name="l0_w_in")
    out_a = att_fwd(proj, w["sinks"], name="l0_att")
    qkvn = dprep_fwd(proj, w["even_conv"], name="l0_prep")
    gates = gates_fwd(proj, w["a_log"], w["dt_bias"], name="l0_gates")
    o_delta, ssave = delta_fwd(qkvn, gates, name="l0_delta")
    w.update(get("even_out", o_delta))
    out_b = gnorm_fwd(o_delta, proj, w["onorm"], name="l0_gnorm")
    mix0 = jnp.concatenate([out_a, out_b], axis=-1)
    h1 =mm_nn(mix0, w["even_w_out"], tm=tm, tn=512, tk=2048, out_dtype=F32, res=x, name="l0_w_out")
    f0 = get("ffn0", h1)
    h2, ffn0 = _ffn_fwd(h1, w["ffn_norm"][0:1] + f0["tok"], f0["gate"], f0["up"], f0["down"], tm, 0)
    hn2 = rms_fwd(h2, w["odd_norm"], name="l1_norm")
    w.update(get("odd", hn2))
    zpre = mm_nn(hn2, w["odd_w_in"], tm=tm, tn=1024, tk=2048, out_dtype=F32, name="l1_w_in")
    gated = gmlp_fwd(zpre, w["odd_ln_g"], w["odd_ln_b"], w["odd_w_s"], w["odd_b_s"], name="l1_gmlp")
    h3 = mm_nn(gated, w["odd_w_out"], tm=tm, tn=512, tk=2048, out_dtype=F32, res=h2, name="l1_w_out")
    f1 = get("ffn1", h3)
    h4, ffn1 = _ffn_fwd(h3, w["ffn_norm"][1:2] + f1["tok"], f1["gate"], f1["up"], f1["down"], tm, 1)
    loss, dh4, g["final_norm"] = loss_head(h4, w["final_norm"], target, name="loss_head")

    dh3, dn1 = _ffn_bwd(dh4, h3, w["ffn_norm"][1:2], f1["gate"], f1["up"], f1["down"], ffn1, tm, 1, emit)
    dw_out_o = mm_tn(gated, dh3, shards=1, tm=tm, tn=1024, tk=1024, out_dtype=BF16, name="l1_dw_out")[0]
    dgated = mm_nt(dh3, w["odd_w_out"], tm=tm, tn=512, tk=2048, out_dtype=BF16, name="l1_dgated")
    dzpre, g["odd_w_s"], g["odd_b_s"], g["odd_ln_g"], g["odd_ln_b"] = gmlp_bwd(
        zpre, dgated, w["odd_ln_g"], w["odd_ln_b"], w["odd_w_s"], w["odd_b_s"], name="l1_dgmlp")
    dw_in_o = mm_tn(hn2, dzpre, shards=N_SHARD, tm=tm, tn=1024, tk=1024, out_dtype=BF16, name="l1_dw_in")
    started = emit("odd", {"odd_w_in": dw_in_o, "odd_w_out": dw_out_o})
    dhn2 = mm_nt(dzpre, w["odd_w_in"], tm=tm, tn=1024, tk=1024, out_dtype=F32, name="l1_dhn")
    dh2, g["odd_norm"] = rms_bwd(h2, w["odd_norm"] + started, dhn2, dh3, name="l1_dnorm")
    dh1, dn0 = _ffn_bwd(dh2, h1, w["ffn_norm"][0:1], f0["gate"], f0["up"], f0["down"], ffn0, tm, 0, emit)
    g["ffn_norm"] = jnp.concatenate([dn0, dn1], axis=0)
    dw_out_e = mm_tn(mix0, dh1, shards=1, tm=tm, tn=1024, tk=1024, out_dtype=BF16, name="l0_dw_out")[0]
    started = emit("even_out", {"even_w_out": dw_out_e})
    dmix = mm_nt(dh1, w["even_w_out"], tm=tm, tn=512, tk=2048, out_dtype=F32, name="l0_dmix")
    dq_a, dkv_cur, dkv_prev, g["sinks"] = att_bwd(proj, w["sinks"] + started, dmix, name="l0_datt")
    dkv = dkv_cur + jnp.concatenate([dkv_prev[WINDOW:], jnp.zeros((WINDOW, 2 * A_KV), F32)], axis=0)
    do_delta, dz, g["onorm"] = gnorm_bwd(o_delta, proj, w["onorm"], dmix, dcol0=A_Q, name="l0_dgnorm")
    dqkvn, dgates = delta_bwd(qkvn, gates, ssave, do_delta, name="l0_ddelta")
    dqkv_b, g["even_conv"] = dprep_bwd(proj, w["even_conv"], dqkvn, name="l0_dprep")
    draw, g["a_log"], g["dt_bias"] = gates_bwd(proj, w["a_log"], w["dt_bias"], dgates, name="l0_dgates")
    dproj = jnp.concatenate([dq_a, dkv.astype(BF16), dqkv_b, dz, draw,
                             jnp.zeros((t, EVEN_IN_PAD - COL_GATE - 128), BF16)], axis=-1)
    dw_in_e = mm_tn(dproj, hn0, shards=1, tm=tm, tn=1024, tk=1408, out_dtype=BF16, name="l0_dw_in")[0]
    dhn0 = mm_nn(dproj, w["even_w_in"], tm=tm, tn=2048, tk=512, out_dtype=F32, name="l0_dhn")
    grad_x, g["even_norm"] = rms_bwd(x, w["even_norm"], dhn0, dh1, name="l0_dnorm")
    emit("even_in", {"even_w_in": dw_in_e, "small": g})
    return loss, grad_x


ANY = pl.BlockSpec(memory_space=pl.ANY)
N_DEV = 8


def _place():
    return lax.axis_index("x"), lax.axis_index("y"), lax.axis_index("c")


def _chip_peers(x, y, c):
    return [((1 - x, y, c), 2 * (1 - x) + y), ((x, 1 - y, c), 2 * x + 1 - y), ((1 - x, 1 - y, c), 2 * (1 - x) + 1 - y)]


HBM = pl.BlockSpec(memory_space=pltpu.HBM)
SEM = pl.BlockSpec(memory_space=pltpu.SEMAPHORE)
EFFECT = pltpu.SideEffectType.DATAFLOW_SIDE_EFFECTING
N_PEER = 3


def _half(ref, c):
    r, cols = ref.shape
    tile_rows = 32 // jnp.dtype(ref.dtype).itemsize
    if (r // 2) % tile_rows == 0:
        return ref.at[pl.ds(c * (r // 2), r // 2)]
    assert (cols // 2) % 128 == 0, ref.shape
    return ref.at[:, pl.ds(c * (cols // 2), cols // 2)]


def _gather_plan(srcs, lands, send, recv):
    x, y, c = _place()
    return [pltpu.make_async_remote_copy(src_ref=_half(srcs[i], c), dst_ref=_half(lands[i].at[2 * x + y], c),
                                         send_sem=send.at[N_PEER * i + k], recv_sem=recv.at[N_PEER * i + k],
                                         device_id=peer, device_id_type=MESH_ID)
            for i in range(len(srcs)) for k, (peer, _) in enumerate(_chip_peers(x, y, c))]


def _relay_plan(srcs, lands, send, recv):
    x, y, c = _place()
    return [pltpu.make_async_remote_copy(src_ref=_half(lands[i].at[idx], c), dst_ref=_half(lands[i].at[idx], c),
                                         send_sem=send.at[N_PEER * i + k], recv_sem=recv.at[N_PEER * i + k],
                                         device_id=(x, y, 1 - c), device_id_type=MESH_ID)
            for i in range(len(srcs)) for k, (_, idx) in enumerate(_chip_peers(x, y, c))]


def _scatter_plan(srcs, lands, send, recv):
    x, y, c = _place()
    return [pltpu.make_async_remote_copy(src_ref=srcs[i].at[idx], dst_ref=lands[i].at[k], send_sem=send.at[N_PEER * i + k],
                                         recv_sem=recv.at[N_PEER * i + k], device_id=peer, device_id_type=MESH_ID)
            for i in range(len(srcs)) for k, (peer, idx) in enumerate(_chip_peers(x, y, c))]


def _swap_plan(srcs, lands, send, recv):
    x, y, c = _place()
    return [pltpu.make_async_remote_copy(src_ref=srcs[i], dst_ref=lands[i], send_sem=send.at[N_PEER * i],
                                         recv_sem=recv.at[N_PEER * i], device_id=(x, y, 1 - c), device_id_type=MESH_ID)
            for i in range(len(srcs))]


def copies_start(plan, srcs, lands, after, *, name):
    n = len(srcs)
    both = list(srcs) + list(lands)

    def body(*refs):
        src_refs, land_refs = refs[:n], refs[n:2 * n]
        send, recv = refs[2 * n + 1], refs[2 * n + 2]
        for cp in plan(src_refs, land_refs, send, recv):
            cp.start()
        refs[-1][...] = jnp.zeros_like(refs[-1])

    res = pl.pallas_call(
        body, name=name,
        out_shape=(pltpu.SemaphoreType.DMA((n * N_PEER,)), pltpu.SemaphoreType.DMA((n * N_PEER,)),
                   *[pltpu.HBM(a.shape, a.dtype) for a in both], jax.ShapeDtypeStruct((8, 128), F32)),
        in_specs=[HBM] * (2 * n) + [ANY],
        out_specs=(SEM, SEM, *[HBM] * (2 * n), pl.BlockSpec(memory_space=pltpu.VMEM)),
        input_output_aliases={i: 2 + i for i in range(2 * n)},
        compiler_params=pltpu.CompilerParams(has_side_effects=EFFECT))(
            *[pltpu.with_memory_space_constraint(a, pltpu.HBM) for a in both], after)
    return {"send": res[0], "recv": res[1], "srcs": list(res[2:2 + n]), "lands": list(res[2 + n:2 + 2 * n]),
            "token": res[-1]}


def copies_relay(arrived_plan, next_plan, started, after, *, name):
    srcs, lands = started["srcs"], started["lands"]
    n = len(srcs)
    both = srcs + lands

    def body(*refs):
        src_refs, land_refs = refs[:n], refs[n:2 * n]
        send1, recv1 = refs[2 * n], refs[2 * n + 1]
        send2, recv2 = refs[2 * n + 3], refs[2 * n + 4]
        for cp in arrived_plan(src_refs, land_refs, send1, recv1):
            cp.wait_send()
            cp.wait_recv()
        for cp in next_plan(src_refs, land_refs, send2, recv2):
            cp.start()
        refs[-1][...] = jnp.zeros_like(refs[-1])

    res = pl.pallas_call(
        body, name=name,
        out_shape=(pltpu.SemaphoreType.DMA((n * N_PEER,)), pltpu.SemaphoreType.DMA((n * N_PEER,)),
                   *[pltpu.HBM(a.shape, a.dtype) for a in both], jax.ShapeDtypeStruct((8, 128), F32)),
        in_specs=[HBM] * (2 * n) + [SEM, SEM, ANY],
        out_specs=(SEM, SEM, *[HBM] * (2 * n), pl.BlockSpec(memory_space=pltpu.VMEM)),
        input_output_aliases={i: 2 + i for i in range(2 * n)},
        compiler_params=pltpu.CompilerParams(has_side_effects=EFFECT))(*both, started["send"], started["recv"], after)
    return {"send": res[0], "recv": res[1], "srcs": list(res[2:2 + n]), "lands": list(res[2 + n:2 + 2 * n]),
            "token": res[-1]}


def copies_wait(plan, started, after, *, name):
    srcs, lands = started["srcs"], started["lands"]
    n = len(srcs)
    both = srcs + lands

    def body(*refs):
        src_refs, land_refs = refs[:n], refs[n:2 * n]
        send, recv = refs[2 * n], refs[2 * n + 1]
        for cp in plan(src_refs, land_refs, send, recv):
            cp.wait_send()
            cp.wait_recv()

    res = pl.pallas_call(
        body, name=name, out_shape=tuple(pltpu.HBM(a.shape, a.dtype) for a in both),
        in_specs=[HBM] * (2 * n) + [SEM, SEM, ANY], out_specs=(HBM,) * (2 * n),
        input_output_aliases={i: i for i in range(2 * n)},
        compiler_params=pltpu.CompilerParams(has_side_effects=EFFECT))(*both, started["send"], started["recv"], after)
    return list(res[:n]), list(res[n:])


def allgather_small(small, *, name):
    def body(small_ref, out_ref, send, recv, loc):
        x, y, c = _place()
        dev = 4 * x + 2 * y + c
        local = pltpu.make_async_copy(small_ref, out_ref.at[dev], loc)
        remote = []
        for r in range(1, N_DEV):
            fx, fy, fc = (r >> 2) & 1, (r >> 1) & 1, r & 1
            peer = (1 - x if fx else x, 1 - y if fy else y, 1 - c if fc else c)
            remote.append(pltpu.make_async_remote_copy(
                src_ref=small_ref, dst_ref=out_ref.at[dev], send_sem=send.at[r - 1], recv_sem=recv.at[r - 1],
                device_id=peer, device_id_type=MESH_ID))
        local.start()
        for cp in remote:
            cp.start()
        for cp in remote:
            cp.wait()
        local.wait()

    return pl.pallas_call(
        body, name=name, in_specs=[ANY], out_specs=ANY,
        out_shape=jax.ShapeDtypeStruct((N_DEV,) + small.shape, small.dtype),
        scratch_shapes=[pltpu.SemaphoreType.DMA((N_DEV - 1,)), pltpu.SemaphoreType.DMA((N_DEV - 1,)),
                        pltpu.SemaphoreType.DMA(())])(small)


def swap_cores(arrs, *, name):
    n = len(arrs)

    def body(*refs):
        ins, outs = refs[:n], refs[n:2 * n]
        send, recv = refs[2 * n:]
        x, y, c = _place()
        copies = [pltpu.make_async_remote_copy(src_ref=ins[i], dst_ref=outs[i], send_sem=send.at[i], recv_sem=recv.at[i],
                                               device_id=(x, y, 1 - c), device_id_type=MESH_ID) for i in range(n)]
        for cp in copies:
            cp.start()
        for cp in copies:
            cp.wait()

    return pl.pallas_call(
        body, name=name, in_specs=[ANY] * n, out_specs=[ANY] * n,
        out_shape=[jax.ShapeDtypeStruct(a.shape, a.dtype) for a in arrs],
        scratch_shapes=[pltpu.SemaphoreType.DMA((n,)), pltpu.SemaphoreType.DMA((n,))])(*arrs)


RED_ROWS = 128
RED_COLS = 256


def _red_block(r, c):
    if r % RED_ROWS == 0:
        return RED_ROWS, c
    if c > RED_COLS and c % RED_COLS == 0:
        return r, RED_COLS
    return r, c


def sum_chips(by_owner, me, got, *, name):
    _, r, c = by_owner.shape
    rb, cb = _red_block(r, c)

    def body(me_ref, o_ref, a_ref, b_ref, c_ref, out_ref):
        total = ((o_ref[...].astype(F32) + a_ref[...].astype(F32)) + b_ref[...].astype(F32)) + c_ref[...].astype(F32)
        out_ref[...] = total.astype(BF16)

    gk = lambda k: pl.BlockSpec((None, rb, cb), lambda i, j, me_ref: (k, i, j))
    grid_spec = pltpu.PrefetchScalarGridSpec(
        num_scalar_prefetch=1, grid=(r // rb, c // cb),
        in_specs=[pl.BlockSpec((None, rb, cb), lambda i, j, me_ref: (me_ref[0], i, j)), gk(0), gk(1), gk(2)],
        out_specs=pl.BlockSpec((rb, cb), lambda i, j, me_ref: (i, j)))
    return pl.pallas_call(
        body, name=name, grid_spec=grid_spec, out_shape=jax.ShapeDtypeStruct((r, c), BF16),
        compiler_params=_params(("parallel", "parallel")))(me, by_owner, got, got, got)


def sum_devices(small_all, *, name):
    _, p, c = small_all.shape

    def body(a_ref, out_ref):
        acc = a_ref[0]
        for d in range(1, N_DEV):
            acc = acc + a_ref[d]
        out_ref[...] = acc

    return pl.pallas_call(
        body, name=name, grid=(1,), in_specs=[pl.BlockSpec((N_DEV, p, c), lambda i: (0, 0, 0))],
        out_specs=pl.BlockSpec((p, c), lambda i: (0, 0)), out_shape=jax.ShapeDtypeStruct((p, c), F32),
        compiler_params=_params(("arbitrary",)))(small_all)


def adamw(parts, w, m, v, *, name):
    nl, r, c = w.shape
    assert len(parts) == nl
    npart = len(parts[0])
    rb, cb = _red_block(r, c)
    flat = [a for layer in parts for a in layer]

    def body(*refs):
        p_refs, (w_ref, m_ref, v_ref) = refs[:nl * npart], refs[nl * npart:nl * npart + 3]
        g_ref, d_ref, nm_ref, nv_ref = refs[nl * npart + 3:]
        layer = pl.program_id(0)
        grad = None
        for l in range(nl):
            gl = p_refs[l * npart][...].astype(F32)
            for j in range(1, npart):
                gl = gl + p_refs[l * npart + j][...].astype(F32)
            grad = gl if grad is None else jnp.where(layer == l, gl, grad)
        wv, mv, vv = w_ref[...], m_ref[...], v_ref[...]
        nm = ADAM_B1 * mv + (1.0 - ADAM_B1) * grad
        nv = ADAM_B2 * vv + (1.0 - ADAM_B2) * (grad * grad)
        m_hat = nm / (1.0 - ADAM_B1 ** ADAM_STEP)
        v_hat = nv / (1.0 - ADAM_B2 ** ADAM_STEP)
        g_ref[...] = grad
        d_ref[...] = -ADAM_LR * (m_hat / (jnp.sqrt(v_hat) + ADAM_EPS) + ADAM_WD * wv)
        nm_ref[...] = nm
        nv_ref[...] = nv

    pspec = pl.BlockSpec((rb, cb), lambda l, i, j: (i, j))
    wspec = pl.BlockSpec((None, rb, cb), lambda l, i, j: (l, i, j))
    osh = jax.ShapeDtypeStruct((nl, r, c), F32)
    return pl.pallas_call(
        body, name=name, grid=(nl, r // rb, c // cb), in_specs=[pspec] * (nl * npart) + [wspec] * 3,
        out_specs=[wspec] * 4, out_shape=[osh] * 4,
        compiler_params=_params(("parallel", "parallel", "parallel")))(*flat, w, m, v)


def _rows128(a):
    flat = a.reshape(-1)
    pad = (-flat.shape[0]) % 128
    return jnp.pad(flat, (0, pad)).reshape(-1, 128)


def _pack_rows(arrs, multiple=8):
    rows = jnp.concatenate([_rows128(a.astype(F32)) for a in arrs], axis=0)
    return jnp.pad(rows, ((0, (-rows.shape[0]) % multiple), (0, 0)))


def _unpack_rows(rows, shapes):
    out, r0 = [], 0
    for shp in shapes:
        size = 1
        for s in shp:
            size *= s
        nr = -(-size // 128)
        out.append(rows[r0:r0 + nr].reshape(-1)[:size].reshape(shp))
        r0 += nr
    return out


SMALL_LOCAL_GRADS = ["even_norm", "even_conv", "a_log", "dt_bias", "sinks", "onorm", "odd_norm", "odd_ln_g",
                     "odd_ln_b", "odd_w_s", "odd_b_s", "ffn_norm", "final_norm"]
BIG = ["even_w_in", "even_w_out", "odd_w_in", "odd_w_out", "ffn_w_gate", "ffn_w_up", "ffn_w_down"]
WEIGHTS = ["even_norm", "even_w_in", "even_conv", "even_a_log", "even_dt_bias", "even_sinks", "even_onorm",
           "even_w_out", "odd_norm", "odd_w_in", "odd_ln_g", "odd_ln_b", "odd_w_s", "odd_b_s", "odd_w_out",
           "ffn_norm", "ffn_w_gate", "ffn_w_up", "ffn_w_down", "final_norm"]
SMALL = [n for n in WEIGHTS if n not in BIG]


def kernel(x, even_norm, even_w_in, even_conv, even_a_log, even_dt_bias, even_sinks, even_onorm, even_w_out, odd_norm, odd_w_in, odd_ln_g, odd_ln_b, odd_w_s, odd_b_s, odd_w_out, ffn_norm, ffn_w_gate, ffn_w_up, ffn_w_down, final_norm, loss_target, m_even_norm, m_even_w_in, m_even_conv, m_even_a_log, m_even_dt_bias, m_even_sinks, m_even_onorm, m_even_w_out, m_odd_norm, m_odd_w_in, m_odd_ln_g, m_odd_ln_b, m_odd_w_s, m_odd_b_s, m_odd_w_out, m_ffn_norm, m_ffn_w_gate, m_ffn_w_up, m_ffn_w_down, m_final_norm, v_even_norm, v_even_w_in, v_even_conv, v_even_a_log, v_even_dt_bias, v_even_sinks, v_even_onorm, v_even_w_out, v_odd_norm, v_odd_w_in, v_odd_ln_g, v_odd_ln_b, v_odd_w_s, v_odd_b_s, v_odd_w_out, v_ffn_norm, v_ffn_w_gate, v_ffn_w_up, v_ffn_w_down, v_final_norm):
    args = dict(locals())
    wl = {n: args[n] for n in WEIGHTS}
    ml = {n: args["m_" + n] for n in WEIGHTS}
    vl = {n: args["v_" + n] for n in WEIGHTS}
    me = 2 * lax.axis_index("x") + lax.axis_index("y")

    def landing(a):
        return lax.dynamic_update_index_in_dim(lax.empty((N_SHARD,) + a.shape, a.dtype), a, me, 0)

    b16 = lambda *arrs: [a.astype(BF16) for a in arrs]
    gather_groups = {
        "even_in": b16(even_w_in[0].T) + [_pack_rows([even_conv[0], odd_norm, odd_ln_g, odd_ln_b], multiple=16)],
        "even_out": b16(even_w_out[0]),
        "ffn0": b16(ffn_w_gate[0], ffn_w_up[0], ffn_w_down[0]),
        "odd": b16(odd_w_in[0], odd_w_out[0]),
        "ffn1": b16(ffn_w_gate[1], ffn_w_up[1], ffn_w_down[1]),
    }
    gathering, after = {}, even_norm
    for group, srcs in gather_groups.items():
        gathering[group] = copies_start(_gather_plan, srcs, [landing(a) for a in srcs], after,
                                        name=f"gather_{group}_start")
        after = gathering[group]["token"]

    order = list(gather_groups)
    relayed, kept = {}, {}
    sinks_pad = jnp.pad(even_sinks, ((0, 0), (0, 128 - A_HEADS)))

    def relay(group, behind):
        relayed[group] = copies_relay(_gather_plan, _relay_plan, gathering[group], behind,
                                      name=f"gather_{group}_relay")
        return relayed[group]["token"][0:1, 0:1]

    def get(group, behind):
        if group not in relayed:
            relay(group, behind)
        _, lands = copies_wait(_relay_plan, relayed[group], behind, name=f"gather_{group}_wait")
        nxt = order.index(group) + 1
        tok = relay(order[nxt], lands[0]) if nxt < len(order) else jnp.zeros((1, 1), F32)
        if group == "even_in":
            parts = zip(*[_unpack_rows(lands[1][s], [(CONV_K, 768), (1, 512), (1, 512), (1, 512)])
                          for s in range(N_SHARD)])
            conv, onorm, lng, lnb = [jnp.concatenate(p, axis=1) for p in parts]
            w_in = jnp.pad(lands[0].reshape(EVEN_IN, D_MODEL), ((0, EVEN_IN_PAD - EVEN_IN), (0, 0)))
            kept["odd_ln_g"] = lng
            return {"even_w_in": w_in, "even_conv": conv + tok, "odd_norm": onorm, "odd_ln_b": lnb}
        if group == "even_out":
            return {"even_w_out": lands[0].reshape(D_MODEL, D_MODEL), "onorm": even_onorm + tok}
        if group == "odd":
            return {"odd_w_in": lands[0], "odd_w_out": lands[1].reshape(D_MODEL, D_MODEL),
                    "odd_ln_g": kept["odd_ln_g"] + tok}
        return {"gate": lands[0], "up": lands[1], "down": lands[2].reshape(D_FF, D_MODEL), "tok": tok}

    rows4 =lambda a: a.reshape(N_SHARD, a.shape[0] // N_SHARD, a.shape[1])
    scattering, small = {}, {}

    def emit(group, grads):
        behind = even_norm
        if group == "even_in":
            small["local"] = grads["small"]
            small["all"] = behind = allgather_small(_pack_rows([grads["small"][n] for n in SMALL_LOCAL_GRADS]),
                                                    name="allgather_small")
            srcs = [grads["even_w_in"][:EVEN_IN].reshape(N_SHARD, EVEN_IN // N_SHARD, D_MODEL)]
        elif group == "even_out":
            srcs = [rows4(grads["even_w_out"])]
        elif group == "odd":
            srcs = [grads["odd_w_in"], rows4(grads["odd_w_out"])]
        else:
            srcs = [grads["gate"], grads["up"], rows4(grads["down"])]
        lands = [lax.empty((N_PEER,) + a.shape[1:], a.dtype) for a in srcs]
        scattering[group] = copies_start(_scatter_plan, srcs, lands, behind, name=f"scatter_{group}_start")
        return scattering[group]["token"][0:1, 0:1]

    pad816 = lambda a: jnp.pad(a, ((0, 0), (B_HEADS, 128 - 2 * B_HEADS)))
    w = {
        "even_norm": even_norm + after[0:1, 0:1],
        "a_log": pad816(even_a_log), "dt_bias": pad816(even_dt_bias),
        "sinks": sinks_pad,
        "onorm": even_onorm,
        "odd_w_s": odd_w_s[0],
        "odd_b_s": jnp.pad(odd_b_s[0].T, ((0, 0), (0, 128 - C_GROUPS))),
        "ffn_norm": ffn_norm,
        "final_norm": final_norm[None],
    }
    loss_l, grad_x = _local_step(x[0], loss_target[0], w, get, emit)
    loss = lax.psum(loss_l[0, 0], ("x", "y", "c"))

    me1 = me.reshape(1).astype(jnp.int32)
    swapping = {}

    def reduce_chips(group, behind):
        srcs, lands = copies_wait(_scatter_plan, scattering[group], behind, name=f"scatter_{group}_wait")
        partial = [sum_chips(srcs[i], me1, lands[i], name=f"sum_chips_{group}_{i}") for i in range(len(srcs))]
        swapping[group] = copies_start(_swap_plan, partial, [lax.empty(p.shape, p.dtype) for p in partial],
                                       even_norm, name=f"swap_{group}_start")
        return swapping[group]["token"]

    def swapped(group, behind):
        mine, theirs = copies_wait(_swap_plan, swapping[group], behind, name=f"swap_{group}_wait")
        return list(zip(mine, theirs))

    behind = scattering["even_in"]["token"]
    for group in ("ffn1", "ffn0", "odd", "even_out"):
        behind = reduce_chips(group, behind)
    sums = {group: swapped(group, behind) for group in ("ffn1", "ffn0", "odd", "even_out")}
    outs = {}
    parts_of = {"even_w_out": [sums["even_out"][0]], "odd_w_in": [sums["odd"][0]], "odd_w_out": [sums["odd"][1]],
                "ffn_w_gate": [sums["ffn0"][0], sums["ffn1"][0]], "ffn_w_up": [sums["ffn0"][1], sums["ffn1"][1]],
                "ffn_w_down": [sums["ffn0"][2], sums["ffn1"][2]]}
    for n in parts_of:
        outs[n] = adamw(parts_of[n], wl[n], ml[n], vl[n], name=f"adamw_{n}")
    behind = reduce_chips("even_in", outs["ffn_w_down"][1])
    flip = lambda a: jnp.transpose(a, (0, 2, 1))
    outs["even_w_in"] = [flip(o) for o in adamw([swapped("even_in", behind)[0]], flip(wl["even_w_in"]),
                                                flip(ml["even_w_in"]), flip(vl["even_w_in"]),
                                                name="adamw_even_w_in")]

    g = small["local"]
    small_sum = sum_devices(small["all"], name="sum_devices")
    sg = dict(zip(SMALL_LOCAL_GRADS, _unpack_rows(small_sum, [g[n].shape for n in SMALL_LOCAL_GRADS])))
    own_cols = lambda a, width: lax.dynamic_slice_in_dim(a, me * width, width, axis=a.ndim - 1)
    small_grads = {
        "even_norm": sg["even_norm"], "even_conv": own_cols(sg["even_conv"], 768)[None],
        "even_a_log": sg["a_log"][:, B_HEADS:2 * B_HEADS], "even_dt_bias": sg["dt_bias"][:, B_HEADS:2 * B_HEADS],
        "even_sinks": sg["sinks"][:, :A_HEADS], "even_onorm": sg["onorm"],
        "odd_norm": own_cols(sg["odd_norm"], 512), "odd_ln_g": own_cols(sg["odd_ln_g"], 512),
        "odd_ln_b": own_cols(sg["odd_ln_b"], 512), "odd_w_s": sg["odd_w_s"][None],
        "odd_b_s": sg["odd_b_s"][:, :C_GROUPS].T[None], "ffn_norm": sg["ffn_norm"], "final_norm": sg["final_norm"][0],
    }
    packed = [_pack_rows([d[n] for n in SMALL])[None] for d in (small_grads, wl, ml, vl)]
    small_out = adamw([(packed[0][0],)], packed[1], packed[2], packed[3], name="adamw_small")
    shapes = [wl[n].shape for n in SMALL]
    for j in range(4):
        for n, a in zip(SMALL, _unpack_rows(small_out[j][0], shapes)):
            outs.setdefault(n, [None] * 4)[j] = a

    return (loss, grad_x[None], *[outs[n][0] for n in WEIGHTS], *[outs[n][1] for n in WEIGHTS],
            *[outs[n][2] for n in WEIGHTS], *[outs[n][3] for n in WEIGHTS])
```

```python
import functools

import jax
import jax.numpy as jnp
from jax import lax
from jax.experimental import pallas as pl
from jax.experimental.pallas import tpu as pltpu

F32 = jnp.float32
BF16 = jnp.bfloat16
NEG_INF = float("-inf")

D_MODEL = 2048
A_HEADS, A_KV_HEADS, A_HEAD_DIM, WINDOW = 16, 2, 64, 128
B_HEADS, B_HEAD_DIM, CONV_K, DN_CHUNK = 8, 128, 4, 64
C_GROUPS, C_CHUNK = 8, 128
C_GROUP_DIM = D_MODEL // C_GROUPS
D_FF = 5632
EPS = 1e-6
A_Q = A_HEADS * A_HEAD_DIM
A_KV = A_KV_HEADS * A_HEAD_DIM
B_W = B_HEADS * B_HEAD_DIM
EVEN_IN = A_Q + 2 * A_KV + 4 * B_W + 2 * B_HEADS
EVEN_IN_PAD = 5632
COL_KV = A_Q
COL_QKVB = A_Q + 2 * A_KV
COL_Z = COL_QKVB + 3 * B_W
COL_GATE = COL_Z + B_W
N_SHARD = 4

ADAM_LR, ADAM_B1, ADAM_B2, ADAM_EPS, ADAM_WD, ADAM_STEP = 0.001, 0.9, 0.999, 1e-08, 0.01, 10

VMEM_LIMIT_V7X = 56 * 1024 * 1024
MXU_COLS = 256
MESH_ID = pl.DeviceIdType.MESH


def _params(sem=None):
    return pltpu.CompilerParams(dimension_semantics=sem, vmem_limit_bytes=VMEM_LIMIT_V7X)


def _sigmoid(x):
    return 1.0 / (1.0 + jnp.exp(-x))


def _silu(x):
    return x * _sigmoid(x)


def _dsilu(x):
    s = _sigmoid(x)
    return s * (1.0 + x * (1.0 - s))


def _gelu(x):
    return 0.5 * x * (1.0 + lax.erf(x * 0.7071067811865476))


def _dgelu(x):
    return 0.5 * (1.0 + lax.erf(x * 0.7071067811865476)) + x * jnp.exp(-0.5 * x * x) * 0.3989422804014327


def _dot(a, b, dims):
    if a.ndim == 3:
        (ca,), (cb,) = dims
        return lax.dot_general(a, b, (((ca + 1,), (cb + 1,)), ((0,), (0,))), preferred_element_type=F32)
    return lax.dot_general(a, b, (dims, ((), ())), preferred_element_type=F32)


NN = ((1,), (0,))
NT = ((1,), (1,))
TN = ((0,), (0,))


def _as3(b):
    return b if b.ndim == 3 else b[None]


def _accumulate(step, nsteps, accs, products, finish):
    if nsteps == 1:
        finish(products())
        return

    @pl.when(step == 0)
    def _():
        for acc, p in zip(accs, products()):
            acc[...] = p

    if nsteps > 2:
        @pl.when((step > 0) & (step < nsteps - 1))
        def _():
            for acc, p in zip(accs, products()):
                acc[...] += p

    @pl.when(step == nsteps - 1)
    def _():
        finish(tuple(acc[...] + p for acc, p in zip(accs, products())))


def mm_nn(a, b, *, tm, tn, tk, out_dtype, name, res=None, act=None):
    b3 = _as3(b)
    m, k = a.shape
    s, k2, ns = b3.shape
    assert k2 == k and m % tm == 0 and ns % tn == 0 and k % tk == 0, (a.shape, b3.shape, tm, tn, tk)
    nps, nk = ns // tn, k // tk

    def body(*refs):
        if res is None:
            a_ref, b_ref, o_ref, acc = refs
        else:
            a_ref, b_ref, r_ref, o_ref, acc = refs
        def finish(tiles):
            r = tiles[0] if res is None else tiles[0] + r_ref[...].astype(F32)
            o_ref[...] = r.astype(out_dtype)

        _accumulate(pl.program_id(2), nk, (acc,),
                    lambda: (_dot(a_ref[...].astype(BF16), b_ref[...].astype(BF16), NN),), finish)

    in_specs = [pl.BlockSpec((tm, tk), lambda i, j, kk: (i, kk)),
                pl.BlockSpec((None, tk, tn), lambda i, j, kk: (j // nps, kk, j % nps))]
    args = [a, b3]
    if res is not None:
        in_specs.append(pl.BlockSpec((tm, tn), lambda i, j, kk: (i, j)))
        args.append(res)
    return pl.pallas_call(
        body, name=name, grid=(m // tm, s * nps, nk), in_specs=in_specs,
        out_specs=pl.BlockSpec((tm, tn), lambda i, j, kk: (i, j)),
        out_shape=jax.ShapeDtypeStruct((m, s * ns), out_dtype),
        scratch_shapes=[pltpu.VMEM((tm, tn), F32)],
        compiler_params=_params(("parallel", "parallel", "arbitrary")))(*args)


def mm_nt(a, b, *, tm, tn, tk, out_dtype, name, res=None):
    b3 = _as3(b)
    m, n = a.shape
    s, k, ns = b3.shape
    assert n == s * ns and m % tm == 0 and k % tn == 0 and ns % tk == 0, (a.shape, b3.shape, tm, tn, tk)
    rps = ns // tk
    nr = s * rps

    def body(*refs):
        if res is None:
            a_ref, b_ref, o_ref, acc = refs
        else:
            a_ref, b_ref, r_ref, o_ref, acc = refs
        def finish(tiles):
            r = tiles[0] if res is None else tiles[0] + r_ref[...].astype(F32)
            o_ref[...] = r.astype(out_dtype)

        _accumulate(pl.program_id(2), nr, (acc,),
                    lambda: (_dot(a_ref[...].astype(BF16), b_ref[...].astype(BF16), NT),), finish)

    in_specs = [pl.BlockSpec((tm, tk), lambda i, j, r: (i, r)),
                pl.BlockSpec((None, tn, tk), lambda i, j, r: (r // rps, j, r % rps))]
    args = [a, b3]
    if res is not None:
        in_specs.append(pl.BlockSpec((tm, tn), lambda i, j, r: (i, j)))
        args.append(res)
    return pl.pallas_call(
        body, name=name, grid=(m // tm, k // tn, nr), in_specs=in_specs,
        out_specs=pl.BlockSpec((tm, tn), lambda i, j, r: (i, j)),
        out_shape=jax.ShapeDtypeStruct((m, k), out_dtype),
        scratch_shapes=[pltpu.VMEM((tm, tn), F32)],
        compiler_params=_params(("parallel", "parallel", "arbitrary")))(*args)


def mm_tn(a, b, *, shards, tm, tn, tk, out_dtype, name):
    m, k = a.shape
    m2, n = b.shape
    ns = n // shards
    assert m2 == m and n == shards * ns and m % tm == 0 and k % tk == 0 and ns % tn == 0, (a.shape, b.shape)
    nps, nm = ns // tn, m // tm

    def body(a_ref, b_ref, o_ref, acc):
        def finish(tiles):
            o_ref[...] = tiles[0].astype(out_dtype)

        _accumulate(pl.program_id(2), nm, (acc,),
                    lambda: (_dot(a_ref[...].astype(BF16), b_ref[...].astype(BF16), TN),), finish)

    return pl.pallas_call(
        body, name=name, grid=(k // tk, shards * nps, nm),
        in_specs=[pl.BlockSpec((tm, tk), lambda i, j, mi: (mi, i)),
                  pl.BlockSpec((tm, tn), lambda i, j, mi: (mi, j))],
        out_specs=pl.BlockSpec((None, tk, tn), lambda i, j, mi: (j // nps, i, j % nps)),
        out_shape=jax.ShapeDtypeStruct((shards, k, ns), out_dtype),
        scratch_shapes=[pltpu.VMEM((tk, tn), F32)],
        compiler_params=_params(("parallel", "parallel", "arbitrary")))(a, b)


def mm_gate_up(hn, wg, wu, *, tm, tn, tk, name):
    wg3, wu3 = _as3(wg), _as3(wu)
    m, k = hn.shape
    s, _, ns = wg3.shape
    assert m % tm == 0 and ns % tn == 0 and k % tk == 0
    nps, nk = ns // tn, k // tk

    def body(a_ref, g_ref, u_ref, og_ref, ou_ref, oa_ref, accg, accu):
        def products():
            a = a_ref[...].astype(BF16)
            return _dot(a, g_ref[...].astype(BF16), NN), _dot(a, u_ref[...].astype(BF16), NN)

        def finish(tiles):
            g, u = tiles
            og_ref[...] = g.astype(BF16)
            ou_ref[...] = u.astype(BF16)
            oa_ref[...] = (_silu(g) * u).astype(BF16)

        _accumulate(pl.program_id(2), nk, (accg, accu), products, finish)

    wspec = pl.BlockSpec((None, tk, tn), lambda i, j, kk: (j // nps, kk, j % nps))
    ospec = pl.BlockSpec((tm, tn), lambda i, j, kk: (i, j))
    osh = jax.ShapeDtypeStruct((m, s * ns), BF16)
    return pl.pallas_call(
        body, name=name, grid=(m // tm, s * nps, nk),
        in_specs=[pl.BlockSpec((tm, tk), lambda i, j, kk: (i, kk)), wspec, wspec],
        out_specs=[ospec, ospec, ospec], out_shape=[osh, osh, osh],
        scratch_shapes=[pltpu.VMEM((tm, tn) if nk > 1 else (8, 128), F32)] * 2,
        compiler_params=_params(("parallel", "parallel", "arbitrary")))(hn, wg3, wu3)


def mm_down_bwd(dh, wd, gate, up, *, tm, tn, tk, name):
    m, d = dh.shape
    f, d2 = wd.shape
    assert d2 == d and m % tm == 0 and f % tn == 0 and tk == d and tn % MXU_COLS == 0

    def body(a_ref, b_ref, g_ref, u_ref, og_ref, ou_ref):
        a = a_ref[...].astype(BF16)
        for jj in range(tn // MXU_COLS):
            sl = slice(jj * MXU_COLS, (jj + 1) * MXU_COLS)
            da = _dot(a, b_ref[sl, :].astype(BF16), NT)
            g, u = g_ref[:, sl].astype(F32), u_ref[:, sl].astype(F32)
            s = _sigmoid(g)
            og_ref[:, sl] = (da * u * (s * (1.0 + g * (1.0 - s)))).astype(BF16)
            ou_ref[:, sl] = (da * (g * s)).astype(BF16)

    ospec = pl.BlockSpec((tm, tn), lambda i, j: (i, j))
    osh = jax.ShapeDtypeStruct((m, f), BF16)
    return pl.pallas_call(
        body, name=name, grid=(m // tm, f // tn),
        in_specs=[pl.BlockSpec((tm, tk), lambda i, j: (i, 0)),
                  pl.BlockSpec((tn, tk), lambda i, j: (j, 0)), ospec, ospec],
        out_specs=[ospec, ospec], out_shape=[osh, osh],
        compiler_params=_params(("parallel", "parallel")))(dh, wd, gate, up)


ROWS = 256


def rms_fwd(x, g, *, name):
    t, d = x.shape

    def body(x_ref, g_ref, o_ref):
        xv = x_ref[...]
        r = lax.rsqrt(jnp.mean(xv * xv, axis=-1, keepdims=True) + EPS)
        o_ref[...] = (xv * r * g_ref[...]).astype(BF16)

    return pl.pallas_call(
        body, name=name, grid=(t // ROWS,),
        in_specs=[pl.BlockSpec((ROWS, d), lambda i: (i, 0)), pl.BlockSpec((1, d), lambda i: (0, 0))],
        out_specs=pl.BlockSpec((ROWS, d), lambda i: (i, 0)),
        out_shape=jax.ShapeDtypeStruct((t, d), BF16), compiler_params=_params(("parallel",)))(x, g)


def rms_bwd(x, g, dy, dres, *, name):
    t, d = x.shape

    def body(x_ref, g_ref, dy_ref, dr_ref, dx_ref, dg_ref):
        @pl.when(pl.program_id(0) == 0)
        def _():
            dg_ref[...] = jnp.zeros_like(dg_ref)

        xv, dyv = x_ref[...], dy_ref[...].astype(F32)
        r = lax.rsqrt(jnp.mean(xv * xv, axis=-1, keepdims=True) + EPS)
        dyg = dyv * g_ref[...]
        dx = r * dyg - xv * (r * r * r) * jnp.mean(dyg * xv, axis=-1, keepdims=True)
        dx_ref[...] = dx + dr_ref[...]
        dg_ref[...] += jnp.sum(dyv * xv * r, axis=0, keepdims=True)

    row = pl.BlockSpec((ROWS, d), lambda i: (i, 0))
    vec = pl.BlockSpec((1, d), lambda i: (0, 0))
    return pl.pallas_call(
        body, name=name, grid=(t // ROWS,), in_specs=[row, vec, row, row], out_specs=[row, vec],
        out_shape=[jax.ShapeDtypeStruct((t, d), F32), jax.ShapeDtypeStruct((1, d), F32)],
        compiler_params=_params(("arbitrary",)))(x, g, dy, dres)


def loss_head(h, g, target, *, name):
    t, d = h.shape

    def body(x_ref, g_ref, t_ref, loss_ref, dx_ref, dg_ref):
        @pl.when(pl.program_id(0) == 0)
        def _():
            dg_ref[...] = jnp.zeros_like(dg_ref)
            loss_ref[...] = jnp.zeros_like(loss_ref)

        xv, gv = x_ref[...], g_ref[...]
        r = lax.rsqrt(jnp.mean(xv * xv, axis=-1, keepdims=True) + EPS)
        e = xv * r * gv - t_ref[...]
        loss_ref[...] += 0.5 * jnp.sum(jnp.mean(e * e, axis=-1, keepdims=True), axis=0, keepdims=True)
        dyv = e * (1.0 / d)
        dyg = dyv * gv
        dx_ref[...] = r * dyg - xv * (r * r * r) * jnp.mean(dyg * xv, axis=-1, keepdims=True)
        dg_ref[...] += jnp.sum(dyv * xv * r, axis=0, keepdims=True)

    row = pl.BlockSpec((ROWS, d), lambda i: (i, 0))
    vec = pl.BlockSpec((1, d), lambda i: (0, 0))
    return pl.pallas_call(
        body, name=name, grid=(t // ROWS,), in_specs=[row, vec, row],
        out_specs=[pl.BlockSpec((1, 128), lambda i: (0, 0)), row, vec],
        out_shape=[jax.ShapeDtypeStruct((1, 128), F32), jax.ShapeDtypeStruct((t, d), F32),
                   jax.ShapeDtypeStruct((1, d), F32)],
        compiler_params=_params(("arbitrary",)))(h, g, target)


def _tril_mask():
    r = lax.broadcasted_iota(jnp.int32, (C_CHUNK, C_CHUNK), 0)
    c = lax.broadcasted_iota(jnp.int32, (C_CHUNK, C_CHUNK), 1)
    return r >= c


def _layer_norm_parts(v):
    mu = jnp.mean(v, axis=-1, keepdims=True)
    vc = v - mu
    rstd = lax.rsqrt(jnp.mean(vc * vc, axis=-1, keepdims=True) + EPS)
    return vc * rstd, rstd


def gmlp_fwd(zpre, ln_g, ln_b, ws, bs_t, *, name):
    t = zpre.shape[0]
    d = D_MODEL

    def body(zu_ref, zv_ref, g_ref, b_ref, ws_ref, bs_ref, o_ref):
        u = _gelu(zu_ref[...])
        vhat, _ = _layer_norm_parts(_gelu(zv_ref[...]))
        vln = (vhat * g_ref[...] + b_ref[...]).astype(BF16)
        mask = _tril_mask()
        for gi in range(C_GROUPS):
            sl = slice(gi * C_GROUP_DIM, (gi + 1) * C_GROUP_DIM)
            w = jnp.where(mask, ws_ref[gi], 0.0).astype(BF16)
            mixed = _dot(w, vln[:, sl], NN) + bs_ref[:, gi:gi + 1]
            o_ref[:, sl] = (u[:, sl] * mixed).astype(BF16)

    vec = pl.BlockSpec((1, d), lambda i: (0, 0))
    return pl.pallas_call(
        body, name=name, grid=(t // C_CHUNK,),
        in_specs=[pl.BlockSpec((C_CHUNK, d), lambda i: (i, 0)), pl.BlockSpec((C_CHUNK, d), lambda i: (i, 1)),
                  vec, vec, pl.BlockSpec((C_GROUPS, C_CHUNK, C_CHUNK), lambda i: (0, 0, 0)),
                  pl.BlockSpec((C_CHUNK, 128), lambda i: (0, 0))],
        out_specs=pl.BlockSpec((C_CHUNK, d), lambda i: (i, 0)),
        out_shape=jax.ShapeDtypeStruct((t, d), BF16), compiler_params=_params(("parallel",)))(
            zpre, zpre, ln_g, ln_b, ws, bs_t)


def gmlp_bwd(zpre, dgated, ln_g, ln_b, ws, bs_t, *, name):
    t = zpre.shape[0]
    d = D_MODEL

    def body(zu_ref, zv_ref, dg_ref, g_ref, b_ref, ws_ref, bs_ref, dz_ref, dws_ref, dbs_ref, dlg_ref, dlb_ref):
        @pl.when(pl.program_id(0) == 0)
        def _():
            dws_ref[...] = jnp.zeros_like(dws_ref)
            dbs_ref[...] = jnp.zeros_like(dbs_ref)
            dlg_ref[...] = jnp.zeros_like(dlg_ref)
            dlb_ref[...] = jnp.zeros_like(dlb_ref)

        zu, zv = zu_ref[...], zv_ref[...]
        u = _gelu(zu)
        vhat, rstd = _layer_norm_parts(_gelu(zv))
        gam = g_ref[...]
        vln = (vhat * gam + b_ref[...]).astype(BF16)
        dgt = dg_ref[...].astype(F32)
        mask = _tril_mask()
        lane = lax.broadcasted_iota(jnp.int32, (C_CHUNK, 128), 1)
        dbs = jnp.zeros((C_CHUNK, 128), F32)
        du_parts, dvln_parts = [], []
        for gi in range(C_GROUPS):
            sl = slice(gi * C_GROUP_DIM, (gi + 1) * C_GROUP_DIM)
            w = jnp.where(mask, ws_ref[gi], 0.0).astype(BF16)
            mixed = _dot(w, vln[:, sl], NN) + bs_ref[:, gi:gi + 1]
            du_parts.append(dgt[:, sl] * mixed)
            dmixed = dgt[:, sl] * u[:, sl]
            dmb = dmixed.astype(BF16)
            dws_ref[gi] += jnp.where(mask, _dot(dmb, vln[:, sl], NT), 0.0)
            dbs = dbs + jnp.where(lane == gi, jnp.sum(dmixed, axis=-1, keepdims=True), 0.0)
            dvln_parts.append(_dot(w, dmb, TN))
        dbs_ref[...] += dbs
        du = jnp.concatenate(du_parts, axis=-1)
        dvln = jnp.concatenate(dvln_parts, axis=-1)
        dlg_ref[...] += jnp.sum(dvln * vhat, axis=0, keepdims=True)
        dlb_ref[...] += jnp.sum(dvln, axis=0, keepdims=True)
        dvhat = dvln * gam
        dv = rstd * (dvhat - jnp.mean(dvhat, axis=-1, keepdims=True)
                     - vhat * jnp.mean(dvhat * vhat, axis=-1, keepdims=True))
        dz_ref[:, :d] = (du * _dgelu(zu)).astype(BF16)
        dz_ref[:, d:] = (dv * _dgelu(zv)).astype(BF16)

    vec = pl.BlockSpec((1, d), lambda i: (0, 0))
    wsp = pl.BlockSpec((C_GROUPS, C_CHUNK, C_CHUNK), lambda i: (0, 0, 0))
    bsp = pl.BlockSpec((C_CHUNK, 128), lambda i: (0, 0))
    return pl.pallas_call(
        body, name=name, grid=(t // C_CHUNK,),
        in_specs=[pl.BlockSpec((C_CHUNK, d), lambda i: (i, 0)), pl.BlockSpec((C_CHUNK, d), lambda i: (i, 1)),
                  pl.BlockSpec((C_CHUNK, d), lambda i: (i, 0)), vec, vec, wsp, bsp],
        out_specs=[pl.BlockSpec((C_CHUNK, 2 * d), lambda i: (i, 0)), wsp, bsp, vec, vec],
        out_shape=[jax.ShapeDtypeStruct((t, 2 * d), BF16), jax.ShapeDtypeStruct((C_GROUPS, C_CHUNK, C_CHUNK), F32),
                   jax.ShapeDtypeStruct((C_CHUNK, 128), F32), jax.ShapeDtypeStruct((1, d), F32),
                   jax.ShapeDtypeStruct((1, d), F32)],
        compiler_params=_params(("arbitrary",)))(zpre, zpre, dgated, ln_g, ln_b, ws, bs_t)


ATT_SCALE = A_HEAD_DIM ** -0.5
PAIRS = A_HEADS // 2
PAIRS_PER_KV = PAIRS // A_KV_HEADS


def _att_padded(tile):
    lo = lax.broadcasted_iota(jnp.int32, tile.shape, 1) < A_HEAD_DIM
    rolled = pltpu.roll(tile, A_HEAD_DIM, 1)
    zero = jnp.zeros_like(tile)
    return {(0, 0): jnp.where(lo, tile, zero).astype(BF16), (0, 1): jnp.where(lo, zero, rolled).astype(BF16),
            (1, 0): jnp.where(lo, rolled, zero).astype(BF16), (1, 1): jnp.where(lo, zero, tile).astype(BF16)}


def _att_valid(n):
    r = lax.broadcasted_iota(jnp.int32, (WINDOW, 2 * WINDOW), 0)
    c = lax.broadcasted_iota(jnp.int32, (WINDOW, 2 * WINDOW), 1)
    rel = r + WINDOW - c
    return (rel >= 0) & (rel < WINDOW) & ((c >= WINDOW) | (n > 0))


def _att_probs(qp, kpad, sink, valid):
    s = jnp.where(valid, _dot(qp, kpad, NT), NEG_INF)
    m = jnp.maximum(jnp.max(s, axis=-1, keepdims=True), sink)
    p = jnp.exp(s - m)
    e_sink = jnp.exp(sink - m)
    inv = 1.0 / (jnp.sum(p, axis=-1, keepdims=True) + e_sink)
    return p * inv, e_sink * inv


def _att_operands(q_ref, kvc_ref, kvp_ref, s_ref):
    kv = jnp.concatenate([kvp_ref[...], kvc_ref[...]], axis=0)
    kpad, vpad = _att_padded(kv[:, :128]), _att_padded(kv[:, 128:])
    key = lambda h: ((h // 2) // PAIRS_PER_KV, h % 2)
    pairs = [(q_ref[:, j * 128:(j + 1) * 128] * ATT_SCALE).astype(BF16) for j in range(PAIRS)]
    q = jnp.stack([pairs[h // 2] for h in range(A_HEADS)])
    k = jnp.stack([kpad[key(h)] for h in range(A_HEADS)])
    v = jnp.stack([vpad[key(h)] for h in range(A_HEADS)])
    sink = jnp.stack([s_ref[:, h:h + 1] for h in range(A_HEADS)])
    return q, k, v, sink


def _att_specs(t):
    return [pl.BlockSpec((WINDOW, A_Q), lambda n: (n, 0)),
            pl.BlockSpec((WINDOW, 2 * A_KV), lambda n: (n, COL_KV // (2 * A_KV))),
            pl.BlockSpec((WINDOW, 2 * A_KV), lambda n: (jnp.maximum(n - 1, 0), COL_KV // (2 * A_KV))),
            pl.BlockSpec((1, 128), lambda n: (0, 0))]


def att_fwd(proj, sinks, *, name):
    t = proj.shape[0]

    def body(q_ref, kvc_ref, kvp_ref, s_ref, o_ref):
        n = pl.program_id(0)
        q, k, v, sink = _att_operands(q_ref, kvc_ref, kvp_ref, s_ref)
        w, _ = _att_probs(q, k, sink, _att_valid(n))
        o = _dot(w.astype(BF16), v, NN)
        for j in range(PAIRS):
            o_ref[:, j * 128:(j + 1) * 128] = (o[2 * j] + o[2 * j + 1]).astype(BF16)

    return pl.pallas_call(
        body, name=name, grid=(t // WINDOW,), in_specs=_att_specs(t),
        out_specs=pl.BlockSpec((WINDOW, A_Q), lambda n: (n, 0)),
        out_shape=jax.ShapeDtypeStruct((t, A_Q), BF16), compiler_params=_params(("parallel",)))(
            proj, proj, proj, sinks)


def att_bwd(proj, sinks, dout, *, name):
    t = proj.shape[0]

    def body(q_ref, kvc_ref, kvp_ref, s_ref, do_ref, dq_ref, dkc_ref, dkp_ref, ds_ref):
        n = pl.program_id(0)

        @pl.when(n == 0)
        def _():
            ds_ref[...] = jnp.zeros_like(ds_ref)

        q, k, v, sink = _att_operands(q_ref, kvc_ref, kvp_ref, s_ref)
        dop = jnp.stack([do_ref[:, (h // 2) * 128:(h // 2 + 1) * 128] for h in range(A_HEADS)]).astype(BF16)
        w, w_sink = _att_probs(q, k, sink, _att_valid(n))
        dw = _dot(dop, v, NT)
        delta = jnp.sum(w * dw, axis=-1, keepdims=True)
        dsc = (w * (dw - delta)).astype(BF16)
        dsink_h = -jnp.sum(w_sink * delta, axis=1, keepdims=True)
        dq = _dot(dsc, k, NN)
        dk_h = _dot(dsc, q, TN)
        dv_h = _dot(w.astype(BF16), dop, TN)
        lane = lax.broadcasted_iota(jnp.int32, (1, 128), 1)
        dsink = jnp.zeros((1, 128), F32)
        for h in range(A_HEADS):
            dsink = dsink + jnp.where(lane == h, dsink_h[h], 0.0)
        ds_ref[...] += dsink
        for j in range(PAIRS):
            dq_ref[:, j * 128:(j + 1) * 128] = ((dq[2 * j] + dq[2 * j + 1]) * ATT_SCALE).astype(BF16)
        lo = lax.broadcasted_iota(jnp.int32, (2 * WINDOW, 128), 1) < A_HEAD_DIM
        heads_per_kv = A_HEADS // A_KV_HEADS

        def tile(per_head):
            acc = {}
            for kvh in range(A_KV_HEADS):
                for half in range(2):
                    hs = range(kvh * heads_per_kv + half, (kvh + 1) * heads_per_kv, 2)
                    acc[(kvh, half)] = functools.reduce(lambda a, b: a + b, [per_head[h] for h in hs])
            return jnp.where(lo, acc[(0, 0)] + pltpu.roll(acc[(0, 1)], A_HEAD_DIM, 1),
                             pltpu.roll(acc[(1, 0)], A_HEAD_DIM, 1) + acc[(1, 1)])

        dkv = jnp.concatenate([tile(dk_h), tile(dv_h)], axis=1)
        dkp_ref[...] = dkv[:WINDOW]
        dkc_ref[...] = dkv[WINDOW:]

    kvo = pl.BlockSpec((WINDOW, 2 * A_KV), lambda n: (n, 0))
    return pl.pallas_call(
        body, name=name, grid=(t // WINDOW,),
        in_specs=_att_specs(t) + [pl.BlockSpec((WINDOW, A_Q), lambda n: (n, 0))],
        out_specs=[pl.BlockSpec((WINDOW, A_Q), lambda n: (n, 0)), kvo, kvo, pl.BlockSpec((1, 128), lambda n: (0, 0))],
        out_shape=[jax.ShapeDtypeStruct((t, A_Q), BF16), jax.ShapeDtypeStruct((t, 2 * A_KV), F32),
                   jax.ShapeDtypeStruct((t, 2 * A_KV), F32), jax.ShapeDtypeStruct((1, 128), F32)],
        compiler_params=_params(("arbitrary",)))(proj, proj, proj, sinks, dout)


QK_SCALE = B_HEAD_DIM ** -0.5
PREP_COLS = 256
PREP_NCB = 3 * B_W // PREP_COLS
HALO = 8
PREP_ROWS = 512


def _roll_rows(x, shift):
    n = x.shape[0]
    return x if shift % n == 0 else pltpu.roll(x, shift % n, 0)


def _conv_taps(xe, w):
    xs = [_roll_rows(xe, CONV_K - 1 - i) for i in range(CONV_K)]
    c = w[0:1] * xs[0]
    for i in range(1, CONV_K):
        c = c + w[i:i + 1] * xs[i]
    return xs, c


def dprep_fwd(proj, conv_w, *, name):
    t = proj.shape[0]
    tt = min(PREP_ROWS, t)
    col0 = COL_QKVB // PREP_COLS

    def body(x_ref, h_ref, w_ref, o_ref):
        cb, n = pl.program_id(0), pl.program_id(1)
        halo = jnp.where(n > 0, h_ref[...], 0.0)
        xe = jnp.concatenate([halo, x_ref[...]], axis=0)
        _, c = _conv_taps(xe, w_ref[...])
        y = _silu(c)[HALO:]
        parts = []
        for hh in range(PREP_COLS // B_HEAD_DIM):
            yh = y[:, hh * B_HEAD_DIM:(hh + 1) * B_HEAD_DIM]
            parts.append(yh * lax.rsqrt(jnp.sum(yh * yh, axis=-1, keepdims=True) + EPS))
        nrm = jnp.concatenate(parts, axis=-1)
        o_ref[...] = jnp.where(cb < 4, nrm * QK_SCALE, jnp.where(cb < 8, nrm, y))

    return pl.pallas_call(
        body, name=name, grid=(PREP_NCB, t // tt),
        in_specs=[pl.BlockSpec((tt, PREP_COLS), lambda cb, n: (n, col0 + cb)),
                  pl.BlockSpec((HALO, PREP_COLS), lambda cb, n: (jnp.maximum(n * (tt // HALO) - 1, 0), col0 + cb)),
                  pl.BlockSpec((CONV_K, PREP_COLS), lambda cb, n: (0, cb))],
        out_specs=pl.BlockSpec((tt, PREP_COLS), lambda cb, n: (n, cb)),
        out_shape=jax.ShapeDtypeStruct((t, 3 * B_W), F32), compiler_params=_params(("parallel", "parallel")))(
            proj, proj, conv_w)


def dprep_bwd(proj, conv_w, dqkvn, *, name):
    t = proj.shape[0]
    tt = min(PREP_ROWS, t)
    nb = t // tt
    col0 = COL_QKVB // PREP_COLS
    n8 = t // HALO

    def body(xc_ref, xb_ref, xa_ref, dc_ref, da_ref, w_ref, dx_ref, dw_ref):
        cb, n = pl.program_id(0), pl.program_id(1)

        @pl.when(n == 0)
        def _():
            dw_ref[...] = jnp.zeros_like(dw_ref)

        w = w_ref[...]
        xe = jnp.concatenate([jnp.where(n > 0, xb_ref[...], 0.0), xc_ref[...], xa_ref[...]], axis=0)
        xs, c = _conv_taps(xe, w)
        sg = _sigmoid(c)
        y = c * sg
        dout = jnp.concatenate([jnp.zeros((HALO, PREP_COLS), F32), dc_ref[...],
                                jnp.where(n < nb - 1, da_ref[...], 0.0)], axis=0)
        dsc = jnp.where(cb < 4, QK_SCALE, 1.0)
        parts = []
        for hh in range(PREP_COLS // B_HEAD_DIM):
            sl = slice(hh * B_HEAD_DIM, (hh + 1) * B_HEAD_DIM)
            yh, doh = y[:, sl], dout[:, sl] * dsc
            r = lax.rsqrt(jnp.sum(yh * yh, axis=-1, keepdims=True) + EPS)
            parts.append(doh * r - yh * (r * r * r) * jnp.sum(doh * yh, axis=-1, keepdims=True))
        dy = jnp.where(cb < 8, jnp.concatenate(parts, axis=-1), dout)
        dcv = dy * sg * (1.0 + c * (1.0 - sg))
        dxe = w[CONV_K - 1:CONV_K] * dcv
        for i in range(CONV_K - 1):
            dxe = dxe + w[i:i + 1] * _roll_rows(dcv, -(CONV_K - 1 - i))
        dx_ref[...] = dxe[HALO:HALO + tt].astype(BF16)
        for i in range(CONV_K):
            dw_ref[i:i + 1, :] += jnp.sum((dcv * xs[i])[HALO:HALO + tt], axis=0, keepdims=True)

    def after(n):
        return jnp.minimum((n + 1) * (tt // HALO), n8 - 1)

    return pl.pallas_call(
        body, name=name, grid=(PREP_NCB, nb),
        in_specs=[pl.BlockSpec((tt, PREP_COLS), lambda cb, n: (n, col0 + cb)),
                  pl.BlockSpec((HALO, PREP_COLS), lambda cb, n: (jnp.maximum(n * (tt // HALO) - 1, 0), col0 + cb)),
                  pl.BlockSpec((HALO, PREP_COLS), lambda cb, n: (after(n), col0 + cb)),
                  pl.BlockSpec((tt, PREP_COLS), lambda cb, n: (n, cb)),
                  pl.BlockSpec((HALO, PREP_COLS), lambda cb, n: (after(n), cb)),
                  pl.BlockSpec((CONV_K, PREP_COLS), lambda cb, n: (0, cb))],
        out_specs=[pl.BlockSpec((tt, PREP_COLS), lambda cb, n: (n, cb)),
                   pl.BlockSpec((CONV_K, PREP_COLS), lambda cb, n: (0, cb))],
        out_shape=[jax.ShapeDtypeStruct((t, 3 * B_W), BF16), jax.ShapeDtypeStruct((CONV_K, 3 * B_W), F32)],
        compiler_params=_params(("parallel", "arbitrary")))(proj, proj, proj, dqkvn, dqkvn, conv_w)


def _softplus(z):
    return jnp.maximum(z, 0.0) + jnp.log(1.0 + jnp.exp(-jnp.abs(z)))


def gates_fwd(proj, alog_pad, dtb_pad, *, name):
    t = proj.shape[0]

    def body(x_ref, a_ref, b_ref, o_ref):
        raw = x_ref[...]
        lane = lax.broadcasted_iota(jnp.int32, raw.shape, 1)
        g = -jnp.exp(a_ref[...]) * _softplus(raw + b_ref[...])
        o_ref[...] = jnp.where(lane < B_HEADS, _sigmoid(raw), jnp.where(lane < 2 * B_HEADS, g, 0.0))

    vec = pl.BlockSpec((1, 128), lambda n: (0, 0))
    return pl.pallas_call(
        body, name=name, grid=(t // ROWS,),
        in_specs=[pl.BlockSpec((ROWS, 128), lambda n: (n, COL_GATE // 128)), vec, vec],
        out_specs=pl.BlockSpec((ROWS, 128), lambda n: (n, 0)),
        out_shape=jax.ShapeDtypeStruct((t, 128), F32), compiler_params=_params(("parallel",)))(
            proj, alog_pad, dtb_pad)


def gates_bwd(proj, alog_pad, dtb_pad, dgates, *, name):
    t = proj.shape[0]

    def body(x_ref, a_ref, b_ref, dg_ref, dx_ref, da_ref, db_ref):
        @pl.when(pl.program_id(0) == 0)
        def _():
            da_ref[...] = jnp.zeros_like(da_ref)
            db_ref[...] = jnp.zeros_like(db_ref)

        raw, dgt = x_ref[...], dg_ref[...]
        lane = lax.broadcasted_iota(jnp.int32, raw.shape, 1)
        is_beta, is_g = lane < B_HEADS, (lane >= B_HEADS) & (lane < 2 * B_HEADS)
        beta = _sigmoid(raw)
        z = raw + b_ref[...]
        neg_a = -jnp.exp(a_ref[...])
        d_z = jnp.where(is_g, dgt * neg_a * _sigmoid(z), 0.0)
        dx_ref[...] = jnp.where(is_beta, dgt * beta * (1.0 - beta), d_z).astype(BF16)
        db_ref[...] += jnp.sum(d_z, axis=0, keepdims=True)
        da_ref[...] += jnp.sum(jnp.where(is_g, dgt * neg_a * _softplus(z), 0.0), axis=0, keepdims=True)

    vec = pl.BlockSpec((1, 128), lambda n: (0, 0))
    row = pl.BlockSpec((ROWS, 128), lambda n: (n, 0))
    return pl.pallas_call(
        body, name=name, grid=(t // ROWS,),
        in_specs=[pl.BlockSpec((ROWS, 128), lambda n: (n, COL_GATE // 128)), vec, vec, row],
        out_specs=[row, vec, vec],
        out_shape=[jax.ShapeDtypeStruct((t, 128), BF16), jax.ShapeDtypeStruct((1, 128), F32),
                   jax.ShapeDtypeStruct((1, 128), F32)],
        compiler_params=_params(("arbitrary",)))(proj, alog_pad, dtb_pad, dgates)


def _split2(a):
    hi = a.astype(BF16)
    return hi, (a - hi.astype(F32)).astype(BF16)


def _dotp(a, b, dims, passes):
    if passes == 1:
        return _dot(a.astype(BF16), b.astype(BF16), dims)
    ah, al = _split2(a)
    bh, bl = _split2(b)
    return _dot(ah, bh, dims) + (_dot(ah, bl, dims) + _dot(al, bh, dims))


_GRAD_DIMS = {NN: ((NT, False), (TN, False)), NT: ((NN, False), (TN, True)), TN: ((NT, True), (NN, False))}


def _make_mm(dims, passes, grad_passes):
    (da_dims, da_swap), (db_dims, db_swap) = _GRAD_DIMS[dims]

    @jax.custom_vjp
    def mm(a, b):
        return _dotp(a, b, dims, passes)

    def fwd(a, b):
        return _dotp(a, b, dims, passes), (a, b)

    def bwd(saved, ct):
        a, b = saved
        da = _dotp(b, ct, da_dims, grad_passes) if da_swap else _dotp(ct, b, da_dims, grad_passes)
        db = _dotp(ct, a, db_dims, grad_passes) if db_swap else _dotp(a, ct, db_dims, grad_passes)
        return da, db

    mm.defvjp(fwd, bwd)
    return mm


MM1 = {d: _make_mm(d, 1, 1) for d in (NN, NT, TN)}
MM3 = {d: _make_mm(d, 3, 1) for d in (NN, NT, TN)}


def _tri_ones(lower):
    r = lax.broadcasted_iota(jnp.int32, (DN_CHUNK, DN_CHUNK), 0)
    c = lax.broadcasted_iota(jnp.int32, (DN_CHUNK, DN_CHUNK), 1)
    return (r >= c if lower else r <= c).astype(BF16)


def _tri_sum(x, lower):
    tri = _tri_ones(lower)
    hi = x.astype(BF16)
    r1 = x - hi.astype(F32)
    mid = r1.astype(BF16)
    lo = (r1 - mid.astype(F32)).astype(BF16)
    return _dot(tri, hi, NN) + (_dot(tri, mid, NN) + _dot(tri, lo, NN))


def _delta_chunk(s0, q, k, v, beta, gam_c, gam_r):
    c = DN_CHUNK
    r = lax.broadcasted_iota(jnp.int32, (c, c), 0)
    cc = lax.broadcasted_iota(jnp.int32, (c, c), 1)
    incl, strict = r >= cc, r > cc
    eye = (r == cc).astype(F32)
    decay = jnp.exp(jnp.where(incl, gam_c - gam_r, NEG_INF))
    g_last = gam_c[:, c - 1:c, :]
    e_gam, e_rest, e_last = jnp.exp(gam_c), jnp.exp(g_last - gam_c), jnp.exp(g_last)
    a_neg = -jnp.where(strict, beta * MM1[NT](k, k) * decay, 0.0)
    inv = eye + a_neg
    pw = a_neg
    for _ in range(5):
        pw = MM3[NN](pw, pw)
        inv = inv + MM3[NN](inv, pw)
    uw = MM3[NN](inv, jnp.concatenate([v * beta, k * (beta * e_gam)], axis=-1))
    u, w = uw[..., :B_HEAD_DIM], uw[..., B_HEAD_DIM:]
    qk = MM1[NT](q, k) * decay
    v_new = u - MM1[NN](w, s0)
    o = MM1[NN](q * e_gam, s0) + MM1[NN](qk, v_new)
    s1 = s0 * e_last + MM1[TN](k * e_rest, v_new)
    return s1, o


def _delta_operands(q_ref, k_ref, v_ref, gt):
    heads = lambda ref: jnp.stack([ref[:, h * B_HEAD_DIM:(h + 1) * B_HEAD_DIM] for h in range(B_HEADS)])
    gam = _tri_sum(gt, True)
    gam_t = gam.T
    beta = jnp.stack([gt[:, h:h + 1] for h in range(B_HEADS)])
    gam_c = jnp.stack([gam[:, B_HEADS + h:B_HEADS + h + 1] for h in range(B_HEADS)])
    gam_r = jnp.stack([gam_t[B_HEADS + h:B_HEADS + h + 1, :] for h in range(B_HEADS)])
    return heads(q_ref), heads(k_ref), heads(v_ref), beta, gam_c, gam_r


def delta_fwd(qkvn, gates, *, name):
    t = qkvn.shape[0]
    nc = t // DN_CHUNK

    def body(q_ref, k_ref, v_ref, g_ref, o_ref, ss_ref, state):
        @pl.when(pl.program_id(0) == 0)
        def _():
            state[...] = jnp.zeros_like(state)

        s0 = state[...]
        ss_ref[...] = s0
        s1, o = _delta_chunk(s0, *_delta_operands(q_ref, k_ref, v_ref, g_ref[...]))
        state[...] = s1
        for h in range(B_HEADS):
            o_ref[:, h * B_HEAD_DIM:(h + 1) * B_HEAD_DIM] = o[h]

    blk = lambda j: pl.BlockSpec((DN_CHUNK, B_W), lambda n: (n, j))
    return pl.pallas_call(
        body, name=name, grid=(nc,),
        in_specs=[blk(0), blk(1), blk(2), pl.BlockSpec((DN_CHUNK, 128), lambda n: (n, 0))],
        out_specs=[blk(0), pl.BlockSpec((None, B_HEADS, B_HEAD_DIM, B_HEAD_DIM), lambda n: (n, 0, 0, 0))],
        out_shape=[jax.ShapeDtypeStruct((t, B_W), F32),
                   jax.ShapeDtypeStruct((nc, B_HEADS, B_HEAD_DIM, B_HEAD_DIM), F32)],
        scratch_shapes=[pltpu.VMEM((B_HEADS, B_HEAD_DIM, B_HEAD_DIM), F32)],
        compiler_params=_params(("arbitrary",)))(qkvn, qkvn, qkvn, gates)


def delta_bwd(qkvn, gates, ssave, do, *, name):
    t = qkvn.shape[0]
    nc = t // DN_CHUNK

    def body(q_ref, k_ref, v_ref, g_ref, ss_ref, do_ref, dx_ref, dg_ref, dstate):
        @pl.when(pl.program_id(0) == 0)
        def _():
            dstate[...] = jnp.zeros_like(dstate)

        lane = lax.broadcasted_iota(jnp.int32, (DN_CHUNK, 128), 1)
        row = lax.broadcasted_iota(jnp.int32, (128, DN_CHUNK), 0)
        dbeta_all = jnp.zeros((DN_CHUNK, 128), F32)
        dgam_c_all = jnp.zeros((DN_CHUNK, 128), F32)
        dgam_r_all = jnp.zeros((128, DN_CHUNK), F32)
        _, vjp = jax.vjp(_delta_chunk, ss_ref[...], *_delta_operands(q_ref, k_ref, v_ref, g_ref[...]))
        do = jnp.stack([do_ref[:, h * B_HEAD_DIM:(h + 1) * B_HEAD_DIM] for h in range(B_HEADS)])
        ds0, dq, dk, dv, dbeta, dgam_c, dgam_r = vjp((dstate[...], do))
        dstate[...] = ds0
        for h in range(B_HEADS):
            dx_ref[:, h * B_HEAD_DIM:(h + 1) * B_HEAD_DIM] = dq[h]
            dx_ref[:, B_W + h * B_HEAD_DIM:B_W + (h + 1) * B_HEAD_DIM] = dk[h]
            dx_ref[:, 2 * B_W + h * B_HEAD_DIM:2 * B_W + (h + 1) * B_HEAD_DIM] = dv[h]
            dbeta_all = dbeta_all + jnp.where(lane == h, dbeta[h], 0.0)
            dgam_c_all = dgam_c_all + jnp.where(lane == B_HEADS + h, dgam_c[h], 0.0)
            dgam_r_all = dgam_r_all + jnp.where(row == B_HEADS + h, dgam_r[h], 0.0)
        dg_ref[...] = dbeta_all + _tri_sum(dgam_c_all + dgam_r_all.T, False)

    blk = lambda j: pl.BlockSpec((DN_CHUNK, B_W), lambda n: (nc - 1 - n, j))
    gsp = pl.BlockSpec((DN_CHUNK, 128), lambda n: (nc - 1 - n, 0))
    return pl.pallas_call(
        body, name=name, grid=(nc,),
        in_specs=[blk(0), blk(1), blk(2), gsp,
                  pl.BlockSpec((None, B_HEADS, B_HEAD_DIM, B_HEAD_DIM), lambda n: (nc - 1 - n, 0, 0, 0)), blk(0)],
        out_specs=[pl.BlockSpec((DN_CHUNK, 3 * B_W), lambda n: (nc - 1 - n, 0)), gsp],
        out_shape=[jax.ShapeDtypeStruct((t, 3 * B_W), F32), jax.ShapeDtypeStruct((t, 128), F32)],
        scratch_shapes=[pltpu.VMEM((B_HEADS, B_HEAD_DIM, B_HEAD_DIM), F32)],
        compiler_params=_params(("arbitrary",)))(qkvn, qkvn, qkvn, gates, ssave, do)


GNORM_ROWS = 1024


def gnorm_fwd(o, proj, onorm, *, name):
    t = o.shape[0]

    def body(o_ref, z_ref, w_ref, out_ref):
        ov = o_ref[...]
        r = lax.rsqrt(jnp.mean(ov * ov, axis=-1, keepdims=True) + EPS)
        out_ref[...] = (ov * r * w_ref[...] * _silu(z_ref[...])).astype(BF16)

    rows = min(GNORM_ROWS, t)
    blk = pl.BlockSpec((rows, B_HEAD_DIM), lambda n, h: (n, h))
    return pl.pallas_call(
        body, name=name, grid=(t // rows, B_HEADS),
        in_specs=[blk, pl.BlockSpec((rows, B_HEAD_DIM), lambda n, h: (n, COL_Z // B_HEAD_DIM + h)),
                  pl.BlockSpec((1, B_HEAD_DIM), lambda n, h: (0, 0))],
        out_specs=blk, out_shape=jax.ShapeDtypeStruct((t, B_W), BF16),
        compiler_params=_params(("parallel", "parallel")))(o, proj, onorm)


def gnorm_bwd(o, proj, onorm, dout, *, dcol0, name):
    t = o.shape[0]

    def body(o_ref, z_ref, w_ref, d_ref, do_ref, dz_ref, dw_ref):
        @pl.when((pl.program_id(0) == 0) & (pl.program_id(1) == 0))
        def _():
            dw_ref[...] = jnp.zeros_like(dw_ref)

        ov, zv, wv, dv = o_ref[...], z_ref[...], w_ref[...], d_ref[...].astype(F32)
        r = lax.rsqrt(jnp.mean(ov * ov, axis=-1, keepdims=True) + EPS)
        nrm = ov * r
        dz_ref[...] = (dv * nrm * wv * _dsilu(zv)).astype(BF16)
        da = dv * _silu(zv)
        dw_ref[...] += jnp.sum(da * nrm, axis=0, keepdims=True)
        dn = da * wv
        do_ref[...] = r * dn - ov * (r * r * r) * jnp.mean(dn * ov, axis=-1, keepdims=True)

    rows = min(GNORM_ROWS, t)
    blk = pl.BlockSpec((rows, B_HEAD_DIM), lambda n, h: (n, h))
    vec = pl.BlockSpec((1, B_HEAD_DIM), lambda n, h: (0, 0))
    return pl.pallas_call(
        body, name=name, grid=(t // rows, B_HEADS),
        in_specs=[blk, pl.BlockSpec((rows, B_HEAD_DIM), lambda n, h: (n, COL_Z // B_HEAD_DIM + h)), vec,
                  pl.BlockSpec((rows, B_HEAD_DIM), lambda n, h: (n, dcol0 // B_HEAD_DIM + h))],
        out_specs=[blk, blk, vec],
        out_shape=[jax.ShapeDtypeStruct((t, B_W), F32), jax.ShapeDtypeStruct((t, B_W), BF16),
                   jax.ShapeDtypeStruct((1, B_HEAD_DIM), F32)],
        compiler_params=_params(("arbitrary", "arbitrary")))(o, proj, onorm, dout)


def _ffn_fwd(h, norm_g, wg, wu, wd, tm, tag):
    hn = rms_fwd(h, norm_g, name=f"ffn{tag}_norm")
    gate, up, act = mm_gate_up(hn, wg, wu, tm=min(512, tm), tn=1408, tk=2048, name=f"ffn{tag}_gate_up")
    h_out = mm_nn(act, wd, tm=tm, tn=2048, tk=512, out_dtype=F32, res=h, name=f"ffn{tag}_down")
    return h_out, (hn, gate, up, act)


def _ffn_bwd(dh, h, norm_g, wg, wu, wd, saved, tm, tag, emit):
    hn, gate, up, act = saved
    dwd = mm_tn(act, dh, shards=1, tm=tm, tn=1024, tk=1408, out_dtype=BF16, name=f"ffn{tag}_dwd")[0]
    dgate, dup = mm_down_bwd(dh, wd, gate, up, tm=tm, tn=512, tk=2048, name=f"ffn{tag}_dact")
    dwg = mm_tn(hn, dgate, shards=N_SHARD, tm=tm, tn=1408, tk=1024, out_dtype=BF16, name=f"ffn{tag}_dwg")
    dwu = mm_tn(hn, dup, shards=N_SHARD, tm=tm, tn=1408, tk=1024, out_dtype=BF16, name=f"ffn{tag}_dwu")
    started = emit(f"ffn{tag}", {"gate": dwg, "up": dwu, "down": dwd})
    dhn = mm_nt(dgate, wg, tm=tm, tn=1024, tk=1408, out_dtype=F32, name=f"ffn{tag}_dhn_g")
    dhn = mm_nt(dup, wu, tm=tm, tn=1024, tk=1408, out_dtype=F32, res=dhn, name=f"ffn{tag}_dhn_u")
    dh_in, dnorm = rms_bwd(h, norm_g + started, dhn, dh, name=f"ffn{tag}_dnorm")
    return dh_in, dnorm


def _local_step(x, target, w, get, emit):
    t = x.shape[0]
    tm = min(1024, t)
    g = {}

    hn0 = rms_fwd(x, w["even_norm"], name="l0_norm")
    w.update(get("even_in", hn0))
    proj = mm_nt(hn0, w["even_w_in"], tm=tm, tn=512, tk=2048, out_dtype=F32, name="l0_w_in")
    out_a = att_fwd(proj, w["sinks"], name="l0_att")
    qkvn = dprep_fwd(proj, w["even_conv"], name="l0_prep")
    gates = gates_fwd(proj, w["a_log"], w["dt_bias"], name="l0_gates")
    o_delta, ssave = delta_fwd(qkvn, gates, name="l0_delta")
    w.update(get("even_out", o_delta))
    out_b = gnorm_fwd(o_delta, proj, w["onorm"], name="l0_gnorm")
    mix0 = jnp.concatenate([out_a, out_b], axis=-1)
    h1 =mm_nn(mix0, w["even_w_out"], tm=tm, tn=512, tk=2048, out_dtype=F32, res=x, name="l0_w_out")
    f0 = get("ffn0", h1)
    h2, ffn0 = _ffn_fwd(h1, w["ffn_norm"][0:1] + f0["tok"], f0["gate"], f0["up"], f0["down"], tm, 0)
    hn2 = rms_fwd(h2, w["odd_norm"], name="l1_norm")
    w.update(get("odd", hn2))
    zpre = mm_nn(hn2, w["odd_w_in"], tm=tm, tn=1024, tk=2048, out_dtype=F32, name="l1_w_in")
    gated = gmlp_fwd(zpre, w["odd_ln_g"], w["odd_ln_b"], w["odd_w_s"], w["odd_b_s"], name="l1_gmlp")
    h3 = mm_nn(gated, w["odd_w_out"], tm=tm, tn=512, tk=2048, out_dtype=F32, res=h2, name="l1_w_out")
    f1 = get("ffn1", h3)
    h4, ffn1 = _ffn_fwd(h3, w["ffn_norm"][1:2] + f1["tok"], f1["gate"], f1["up"], f1["down"], tm, 1)
    loss, dh4, g["final_norm"] = loss_head(h4, w["final_norm"], target, name="loss_head")

    dh3, dn1 = _ffn_bwd(dh4, h3, w["ffn_norm"][1:2], f1["gate"], f1["up"], f1["down"], ffn1, tm, 1, emit)
    dw_out_o = mm_tn(gated, dh3, shards=1, tm=tm, tn=1024, tk=1024, out_dtype=BF16, name="l1_dw_out")[0]
    dgated = mm_nt(dh3, w["odd_w_out"], tm=tm, tn=512, tk=2048, out_dtype=BF16, name="l1_dgated")
    dzpre, g["odd_w_s"], g["odd_b_s"], g["odd_ln_g"], g["odd_ln_b"] = gmlp_bwd(
        zpre, dgated, w["odd_ln_g"], w["odd_ln_b"], w["odd_w_s"], w["odd_b_s"], name="l1_dgmlp")
    dw_in_o = mm_tn(hn2, dzpre, shards=N_SHARD, tm=tm, tn=1024, tk=1024, out_dtype=BF16, name="l1_dw_in")
    started = emit("odd", {"odd_w_in": dw_in_o, "odd_w_out": dw_out_o})
    dhn2 = mm_nt(dzpre, w["odd_w_in"], tm=tm, tn=1024, tk=1024, out_dtype=F32, name="l1_dhn")
    dh2, g["odd_norm"] = rms_bwd(h2, w["odd_norm"] + started, dhn2, dh3, name="l1_dnorm")
    dh1, dn0 = _ffn_bwd(dh2, h1, w["ffn_norm"][0:1], f0["gate"], f0["up"], f0["down"], ffn0, tm, 0, emit)
    g["ffn_norm"] = jnp.concatenate([dn0, dn1], axis=0)
    dw_out_e = mm_tn(mix0, dh1, shards=1, tm=tm, tn=1024, tk=1024, out_dtype=BF16, name="l0_dw_out")[0]
    started = emit("even_out", {"even_w_out": dw_out_e})
    dmix = mm_nt(dh1, w["even_w_out"], tm=tm, tn=512, tk=2048, out_dtype=F32, name="l0_dmix")
    dq_a, dkv_cur, dkv_prev, g["sinks"] = att_bwd(proj, w["sinks"] + started, dmix, name="l0_datt")
    dkv = dkv_cur + jnp.concatenate([dkv_prev[WINDOW:], jnp.zeros((WINDOW, 2 * A_KV), F32)], axis=0)
    do_delta, dz, g["onorm"] = gnorm_bwd(o_delta, proj, w["onorm"], dmix, dcol0=A_Q, name="l0_dgnorm")
    dqkvn, dgates = delta_bwd(qkvn, gates, ssave, do_delta, name="l0_ddelta")
    dqkv_b, g["even_conv"] = dprep_bwd(proj, w["even_conv"], dqkvn, name="l0_dprep")
    draw, g["a_log"], g["dt_bias"] = gates_bwd(proj, w["a_log"], w["dt_bias"], dgates, name="l0_dgates")
    dproj = jnp.concatenate([dq_a, dkv.astype(BF16), dqkv_b, dz, draw,
                             jnp.zeros((t, EVEN_IN_PAD - COL_GATE - 128), BF16)], axis=-1)
    dw_in_e = mm_tn(dproj, hn0, shards=1, tm=tm, tn=1024, tk=1408, out_dtype=BF16, name="l0_dw_in")[0]
    dhn0 = mm_nn(dproj, w["even_w_in"], tm=tm, tn=2048, tk=512, out_dtype=F32, name="l0_dhn")
    grad_x, g["even_norm"] = rms_bwd(x, w["even_norm"], dhn0, dh1, name="l0_dnorm")
    emit("even_in", {"even_w_in": dw_in_e, "small": g})
    return loss, grad_x


ANY = pl.BlockSpec(memory_space=pl.ANY)
N_DEV = 8


def _place():
    return lax.axis_index("x"), lax.axis_index("y"), lax.axis_index("c")


def _chip_peers(x, y, c):
    return [((1 - x, y, c), 2 * (1 - x) + y), ((x, 1 - y, c), 2 * x + 1 - y), ((1 - x, 1 - y, c), 2 * (1 - x) + 1 - y)]


HBM = pl.BlockSpec(memory_space=pltpu.HBM)
SEM = pl.BlockSpec(memory_space=pltpu.SEMAPHORE)
EFFECT = pltpu.SideEffectType.DATAFLOW_SIDE_EFFECTING
N_PEER = 3


def _half(ref, c):
    r, cols = ref.shape
    tile_rows = 32 // jnp.dtype(ref.dtype).itemsize
    if (r // 2) % tile_rows == 0:
        return ref.at[pl.ds(c * (r // 2), r // 2)]
    assert (cols // 2) % 128 == 0, ref.shape
    return ref.at[:, pl.ds(c * (cols // 2), cols // 2)]


def _gather_plan(srcs, lands, send, recv):
    x, y, c = _place()
    return [pltpu.make_async_remote_copy(src_ref=_half(srcs[i], c), dst_ref=_half(lands[i].at[2 * x + y], c),
                                         send_sem=send.at[N_PEER * i + k], recv_sem=recv.at[N_PEER * i + k],
                                         device_id=peer, device_id_type=MESH_ID)
            for i in range(len(srcs)) for k, (peer, _) in enumerate(_chip_peers(x, y, c))]


def _relay_plan(srcs, lands, send, recv):
    x, y, c = _place()
    return [pltpu.make_async_remote_copy(src_ref=_half(lands[i].at[idx], c), dst_ref=_half(lands[i].at[idx], c),
                                         send_sem=send.at[N_PEER * i + k], recv_sem=recv.at[N_PEER * i + k],
                                         device_id=(x, y, 1 - c), device_id_type=MESH_ID)
            for i in range(len(srcs)) for k, (_, idx) in enumerate(_chip_peers(x, y, c))]


def _scatter_plan(srcs, lands, send, recv):
    x, y, c = _place()
    return [pltpu.make_async_remote_copy(src_ref=srcs[i].at[idx], dst_ref=lands[i].at[k], send_sem=send.at[N_PEER * i + k],
                                         recv_sem=recv.at[N_PEER * i + k], device_id=peer, device_id_type=MESH_ID)
            for i in range(len(srcs)) for k, (peer, idx) in enumerate(_chip_peers(x, y, c))]


def _swap_plan(srcs, lands, send, recv):
    x, y, c = _place()
    return [pltpu.make_async_remote_copy(src_ref=srcs[i], dst_ref=lands[i], send_sem=send.at[N_PEER * i],
                                         recv_sem=recv.at[N_PEER * i], device_id=(x, y, 1 - c), device_id_type=MESH_ID)
            for i in range(len(srcs))]


def copies_start(plan, srcs, lands, after, *, name):
    n = len(srcs)
    both = list(srcs) + list(lands)

    def body(*refs):
        src_refs, land_refs = refs[:n], refs[n:2 * n]
        send, recv = refs[2 * n + 1], refs[2 * n + 2]
        for cp in plan(src_refs, land_refs, send, recv):
            cp.start()
        refs[-1][...] = jnp.zeros_like(refs[-1])

    res = pl.pallas_call(
        body, name=name,
        out_shape=(pltpu.SemaphoreType.DMA((n * N_PEER,)), pltpu.SemaphoreType.DMA((n * N_PEER,)),
                   *[pltpu.HBM(a.shape, a.dtype) for a in both], jax.ShapeDtypeStruct((8, 128), F32)),
        in_specs=[HBM] * (2 * n) + [ANY],
        out_specs=(SEM, SEM, *[HBM] * (2 * n), pl.BlockSpec(memory_space=pltpu.VMEM)),
        input_output_aliases={i: 2 + i for i in range(2 * n)},
        compiler_params=pltpu.CompilerParams(has_side_effects=EFFECT))(
            *[pltpu.with_memory_space_constraint(a, pltpu.HBM) for a in both], after)
    return {"send": res[0], "recv": res[1], "srcs": list(res[2:2 + n]), "lands": list(res[2 + n:2 + 2 * n]),
            "token": res[-1]}


def copies_relay(arrived_plan, next_plan, started, after, *, name):
    srcs, lands = started["srcs"], started["lands"]
    n = len(srcs)
    both = srcs + lands

    def body(*refs):
        src_refs, land_refs = refs[:n], refs[n:2 * n]
        send1, recv1 = refs[2 * n], refs[2 * n + 1]
        send2, recv2 = refs[2 * n + 3], refs[2 * n + 4]
        for cp in arrived_plan(src_refs, land_refs, send1, recv1):
            cp.wait_send()
            cp.wait_recv()
        for cp in next_plan(src_refs, land_refs, send2, recv2):
            cp.start()
        refs[-1][...] = jnp.zeros_like(refs[-1])

    res = pl.pallas_call(
        body, name=name,
        out_shape=(pltpu.SemaphoreType.DMA((n * N_PEER,)), pltpu.SemaphoreType.DMA((n * N_PEER,)),
                   *[pltpu.HBM(a.shape, a.dtype) for a in both], jax.ShapeDtypeStruct((8, 128), F32)),
        in_specs=[HBM] * (2 * n) + [SEM, SEM, ANY],
        out_specs=(SEM, SEM, *[HBM] * (2 * n), pl.BlockSpec(memory_space=pltpu.VMEM)),
        input_output_aliases={i: 2 + i for i in range(2 * n)},
        compiler_params=pltpu.CompilerParams(has_side_effects=EFFECT))(*both, started["send"], started["recv"], after)
    return {"send": res[0], "recv": res[1], "srcs": list(res[2:2 + n]), "lands": list(res[2 + n:2 + 2 * n]),
            "token": res[-1]}


def copies_wait(plan, started, after, *, name):
    srcs, lands = started["srcs"], started["lands"]
    n = len(srcs)
    both = srcs + lands

    def body(*refs):
        src_refs, land_refs = refs[:n], refs[n:2 * n]
        send, recv = refs[2 * n], refs[2 * n + 1]
        for cp in plan(src_refs, land_refs, send, recv):
            cp.wait_send()
            cp.wait_recv()

    res = pl.pallas_call(
        body, name=name, out_shape=tuple(pltpu.HBM(a.shape, a.dtype) for a in both),
        in_specs=[HBM] * (2 * n) + [SEM, SEM, ANY], out_specs=(HBM,) * (2 * n),
        input_output_aliases={i: i for i in range(2 * n)},
        compiler_params=pltpu.CompilerParams(has_side_effects=EFFECT))(*both, started["send"], started["recv"], after)
    return list(res[:n]), list(res[n:])


def allgather_small(small, *, name):
    def body(small_ref, out_ref, send, recv, loc):
        x, y, c = _place()
        dev = 4 * x + 2 * y + c
        local = pltpu.make_async_copy(small_ref, out_ref.at[dev], loc)
        remote = []
        for r in range(1, N_DEV):
            fx, fy, fc = (r >> 2) & 1, (r >> 1) & 1, r & 1
            peer = (1 - x if fx else x, 1 - y if fy else y, 1 - c if fc else c)
            remote.append(pltpu.make_async_remote_copy(
                src_ref=small_ref, dst_ref=out_ref.at[dev], send_sem=send.at[r - 1], recv_sem=recv.at[r - 1],
                device_id=peer, device_id_type=MESH_ID))
        local.start()
        for cp in remote:
            cp.start()
        for cp in remote:
            cp.wait()
        local.wait()

    return pl.pallas_call(
        body, name=name, in_specs=[ANY], out_specs=ANY,
        out_shape=jax.ShapeDtypeStruct((N_DEV,) + small.shape, small.dtype),
        scratch_shapes=[pltpu.SemaphoreType.DMA((N_DEV - 1,)), pltpu.SemaphoreType.DMA((N_DEV - 1,)),
                        pltpu.SemaphoreType.DMA(())])(small)


def swap_cores(arrs, *, name):
    n = len(arrs)

    def body(*refs):
        ins, outs = refs[:n], refs[n:2 * n]
        send, recv = refs[2 * n:]
        x, y, c = _place()
        copies = [pltpu.make_async_remote_copy(src_ref=ins[i], dst_ref=outs[i], send_sem=send.at[i], recv_sem=recv.at[i],
                                               device_id=(x, y, 1 - c), device_id_type=MESH_ID) for i in range(n)]
        for cp in copies:
            cp.start()
        for cp in copies:
            cp.wait()

    return pl.pallas_call(
        body, name=name, in_specs=[ANY] * n, out_specs=[ANY] * n,
        out_shape=[jax.ShapeDtypeStruct(a.shape, a.dtype) for a in arrs],
        scratch_shapes=[pltpu.SemaphoreType.DMA((n,)), pltpu.SemaphoreType.DMA((n,))])(*arrs)


RED_ROWS = 256
RED_COLS = 256


def _red_block(r, c):
    if r % RED_ROWS == 0:
        return RED_ROWS, c
    if c > RED_COLS and c % RED_COLS == 0:
        return r, RED_COLS
    return r, c


def sum_chips(by_owner, me, got, *, name):
    _, r, c = by_owner.shape
    rb, cb = _red_block(r, c)

    def body(me_ref, o_ref, a_ref, b_ref, c_ref, out_ref):
        total = ((o_ref[...].astype(F32) + a_ref[...].astype(F32)) + b_ref[...].astype(F32)) + c_ref[...].astype(F32)
        out_ref[...] = total.astype(BF16)

    gk = lambda k: pl.BlockSpec((None, rb, cb), lambda i, j, me_ref: (k, i, j))
    grid_spec = pltpu.PrefetchScalarGridSpec(
        num_scalar_prefetch=1, grid=(r // rb, c // cb),
        in_specs=[pl.BlockSpec((None, rb, cb), lambda i, j, me_ref: (me_ref[0], i, j)), gk(0), gk(1), gk(2)],
        out_specs=pl.BlockSpec((rb, cb), lambda i, j, me_ref: (i, j)))
    return pl.pallas_call(
        body, name=name, grid_spec=grid_spec, out_shape=jax.ShapeDtypeStruct((r, c), BF16),
        compiler_params=_params(("parallel", "parallel")))(me, by_owner, got, got, got)


def sum_devices(small_all, *, name):
    _, p, c = small_all.shape

    def body(a_ref, out_ref):
        acc = a_ref[0]
        for d in range(1, N_DEV):
            acc = acc + a_ref[d]
        out_ref[...] = acc

    return pl.pallas_call(
        body, name=name, grid=(1,), in_specs=[pl.BlockSpec((N_DEV, p, c), lambda i: (0, 0, 0))],
        out_specs=pl.BlockSpec((p, c), lambda i: (0, 0)), out_shape=jax.ShapeDtypeStruct((p, c), F32),
        compiler_params=_params(("arbitrary",)))(small_all)


def adamw(parts, w, m, v, *, name):
    nl, r, c = w.shape
    assert len(parts) == nl
    npart = len(parts[0])
    rb, cb = _red_block(r, c)
    flat = [a for layer in parts for a in layer]

    def body(*refs):
        p_refs, (w_ref, m_ref, v_ref) = refs[:nl * npart], refs[nl * npart:nl * npart + 3]
        g_ref, d_ref, nm_ref, nv_ref = refs[nl * npart + 3:]
        layer = pl.program_id(0)
        grad = None
        for l in range(nl):
            gl = p_refs[l * npart][...].astype(F32)
            for j in range(1, npart):
                gl = gl + p_refs[l * npart + j][...].astype(F32)
            grad = gl if grad is None else jnp.where(layer == l, gl, grad)
        wv, mv, vv = w_ref[...], m_ref[...], v_ref[...]
        nm = ADAM_B1 * mv + (1.0 - ADAM_B1) * grad
        nv = ADAM_B2 * vv + (1.0 - ADAM_B2) * (grad * grad)
        m_hat = nm / (1.0 - ADAM_B1 ** ADAM_STEP)
        v_hat = nv / (1.0 - ADAM_B2 ** ADAM_STEP)
        g_ref[...] = grad
        d_ref[...] = -ADAM_LR * (m_hat / (jnp.sqrt(v_hat) + ADAM_EPS) + ADAM_WD * wv)
        nm_ref[...] = nm
        nv_ref[...] = nv

    pspec = pl.BlockSpec((rb, cb), lambda l, i, j: (i, j))
    wspec = pl.BlockSpec((None, rb, cb), lambda l, i, j: (l, i, j))
    osh = jax.ShapeDtypeStruct((nl, r, c), F32)
    return pl.pallas_call(
        body, name=name, grid=(nl, r // rb, c // cb), in_specs=[pspec] * (nl * npart) + [wspec] * 3,
        out_specs=[wspec] * 4, out_shape=[osh] * 4,
        compiler_params=_params(("parallel", "parallel", "parallel")))(*flat, w, m, v)


def _rows128(a):
    flat = a.reshape(-1)
    pad = (-flat.shape[0]) % 128
    return jnp.pad(flat, (0, pad)).reshape(-1, 128)


def _pack_rows(arrs, multiple=8):
    rows = jnp.concatenate([_rows128(a.astype(F32)) for a in arrs], axis=0)
    return jnp.pad(rows, ((0, (-rows.shape[0]) % multiple), (0, 0)))


def _unpack_rows(rows, shapes):
    out, r0 = [], 0
    for shp in shapes:
        size = 1
        for s in shp:
            size *= s
        nr = -(-size // 128)
        out.append(rows[r0:r0 + nr].reshape(-1)[:size].reshape(shp))
        r0 += nr
    return out


SMALL_LOCAL_GRADS = ["even_norm", "even_conv", "a_log", "dt_bias", "sinks", "onorm", "odd_norm", "odd_ln_g",
                     "odd_ln_b", "odd_w_s", "odd_b_s", "ffn_norm", "final_norm"]
BIG = ["even_w_in", "even_w_out", "odd_w_in", "odd_w_out", "ffn_w_gate", "ffn_w_up", "ffn_w_down"]
WEIGHTS = ["even_norm", "even_w_in", "even_conv", "even_a_log", "even_dt_bias", "even_sinks", "even_onorm",
           "even_w_out", "odd_norm", "odd_w_in", "odd_ln_g", "odd_ln_b", "odd_w_s", "odd_b_s", "odd_w_out",
           "ffn_norm", "ffn_w_gate", "ffn_w_up", "ffn_w_down", "final_norm"]
SMALL = [n for n in WEIGHTS if n not in BIG]


def kernel(x, even_norm, even_w_in, even_conv, even_a_log, even_dt_bias, even_sinks, even_onorm, even_w_out, odd_norm, odd_w_in, odd_ln_g, odd_ln_b, odd_w_s, odd_b_s, odd_w_out, ffn_norm, ffn_w_gate, ffn_w_up, ffn_w_down, final_norm, loss_target, m_even_norm, m_even_w_in, m_even_conv, m_even_a_log, m_even_dt_bias, m_even_sinks, m_even_onorm, m_even_w_out, m_odd_norm, m_odd_w_in, m_odd_ln_g, m_odd_ln_b, m_odd_w_s, m_odd_b_s, m_odd_w_out, m_ffn_norm, m_ffn_w_gate, m_ffn_w_up, m_ffn_w_down, m_final_norm, v_even_norm, v_even_w_in, v_even_conv, v_even_a_log, v_even_dt_bias, v_even_sinks, v_even_onorm, v_even_w_out, v_odd_norm, v_odd_w_in, v_odd_ln_g, v_odd_ln_b, v_odd_w_s, v_odd_b_s, v_odd_w_out, v_ffn_norm, v_ffn_w_gate, v_ffn_w_up, v_ffn_w_down, v_final_norm):
    args = dict(locals())
    wl = {n: args[n] for n in WEIGHTS}
    ml = {n: args["m_" + n] for n in WEIGHTS}
    vl = {n: args["v_" + n] for n in WEIGHTS}
    me = 2 * lax.axis_index("x") + lax.axis_index("y")

    def landing(a):
        return lax.dynamic_update_index_in_dim(lax.empty((N_SHARD,) + a.shape, a.dtype), a, me, 0)

    b16 = lambda *arrs: [a.astype(BF16) for a in arrs]
    gather_groups = {
        "even_in": b16(even_w_in[0].T) + [_pack_rows([even_conv[0], odd_norm, odd_ln_g, odd_ln_b], multiple=16)],
        "even_out": b16(even_w_out[0]),
        "ffn0": b16(ffn_w_gate[0], ffn_w_up[0], ffn_w_down[0]),
        "odd": b16(odd_w_in[0], odd_w_out[0]),
        "ffn1": b16(ffn_w_gate[1], ffn_w_up[1], ffn_w_down[1]),
    }
    gathering, after = {}, even_norm
    for group, srcs in gather_groups.items():
        gathering[group] = copies_start(_gather_plan, srcs, [landing(a) for a in srcs], after,
                                        name=f"gather_{group}_start")
        after = gathering[group]["token"]

    order = list(gather_groups)
    relayed, kept = {}, {}
    sinks_pad = jnp.pad(even_sinks, ((0, 0), (0, 128 - A_HEADS)))

    def relay(group, behind):
        relayed[group] = copies_relay(_gather_plan, _relay_plan, gathering[group], behind,
                                      name=f"gather_{group}_relay")
        return relayed[group]["token"][0:1, 0:1]

    def get(group, behind):
        if group not in relayed:
            relay(group, behind)
        _, lands = copies_wait(_relay_plan, relayed[group], behind, name=f"gather_{group}_wait")
        nxt = order.index(group) + 1
        tok = relay(order[nxt], lands[0]) if nxt < len(order) else jnp.zeros((1, 1), F32)
        if group == "even_in":
            parts = zip(*[_unpack_rows(lands[1][s], [(CONV_K, 768), (1, 512), (1, 512), (1, 512)])
                          for s in range(N_SHARD)])
            conv, onorm, lng, lnb = [jnp.concatenate(p, axis=1) for p in parts]
            w_in = jnp.pad(lands[0].reshape(EVEN_IN, D_MODEL), ((0, EVEN_IN_PAD - EVEN_IN), (0, 0)))
            kept["odd_ln_g"] = lng
            return {"even_w_in": w_in, "even_conv": conv + tok, "odd_norm": onorm, "odd_ln_b": lnb}
        if group == "even_out":
            return {"even_w_out": lands[0].reshape(D_MODEL, D_MODEL), "onorm": even_onorm + tok}
        if group == "odd":
            return {"odd_w_in": lands[0], "odd_w_out": lands[1].reshape(D_MODEL, D_MODEL),
                    "odd_ln_g": kept["odd_ln_g"] + tok}
        return {"gate": lands[0], "up": lands[1], "down": lands[2].reshape(D_FF, D_MODEL), "tok": tok}

    rows4 =lambda a: a.reshape(N_SHARD, a.shape[0] // N_SHARD, a.shape[1])
    scattering, small = {}, {}

    def emit(group, grads):
        behind = even_norm
        if group == "even_in":
            small["local"] = grads["small"]
            small["all"] = behind = allgather_small(_pack_rows([grads["small"][n] for n in SMALL_LOCAL_GRADS]),
                                                    name="allgather_small")
            srcs = [grads["even_w_in"][:EVEN_IN].reshape(N_SHARD, EVEN_IN // N_SHARD, D_MODEL)]
        elif group == "even_out":
            srcs = [rows4(grads["even_w_out"])]
        elif group == "odd":
            srcs = [grads["odd_w_in"], rows4(grads["odd_w_out"])]
        else:
            srcs = [grads["gate"], grads["up"], rows4(grads["down"])]
        lands = [lax.empty((N_PEER,) + a.shape[1:], a.dtype) for a in srcs]
        scattering[group] = copies_start(_scatter_plan, srcs, lands, behind, name=f"scatter_{group}_start")
        return scattering[group]["token"][0:1, 0:1]

    pad816 = lambda a: jnp.pad(a, ((0, 0), (B_HEADS, 128 - 2 * B_HEADS)))
    w = {
        "even_norm": even_norm + after[0:1, 0:1],
        "a_log": pad816(even_a_log), "dt_bias": pad816(even_dt_bias),
        "sinks": sinks_pad,
        "onorm": even_onorm,
        "odd_w_s": odd_w_s[0],
        "odd_b_s": jnp.pad(odd_b_s[0].T, ((0, 0), (0, 128 - C_GROUPS))),
        "ffn_norm": ffn_norm,
        "final_norm": final_norm[None],
    }
    loss_l, grad_x = _local_step(x[0], loss_target[0], w, get, emit)
    loss = lax.psum(loss_l[0, 0], ("x", "y", "c"))

    me1 = me.reshape(1).astype(jnp.int32)
    swapping = {}

    def reduce_chips(group, behind):
        srcs, lands = copies_wait(_scatter_plan, scattering[group], behind, name=f"scatter_{group}_wait")
        partial = [sum_chips(srcs[i], me1, lands[i], name=f"sum_chips_{group}_{i}") for i in range(len(srcs))]
        swapping[group] = copies_start(_swap_plan, partial, [lax.empty(p.shape, p.dtype) for p in partial],
                                       even_norm, name=f"swap_{group}_start")
        return swapping[group]["token"]

    def swapped(group, behind):
        mine, theirs = copies_wait(_swap_plan, swapping[group], behind, name=f"swap_{group}_wait")
        return list(zip(mine, theirs))

    behind = scattering["even_in"]["token"]
    for group in ("ffn1", "ffn0", "odd", "even_out"):
        behind = reduce_chips(group, behind)
    sums = {group: swapped(group, behind) for group in ("ffn1", "ffn0", "odd", "even_out")}
    outs = {}
    parts_of = {"even_w_out": [sums["even_out"][0]], "odd_w_in": [sums["odd"][0]], "odd_w_out": [sums["odd"][1]],
                "ffn_w_gate": [sums["ffn0"][0], sums["ffn1"][0]], "ffn_w_up": [sums["ffn0"][1], sums["ffn1"][1]],
                "ffn_w_down": [sums["ffn0"][2], sums["ffn1"][2]]}
    for n in parts_of:
        outs[n] = adamw(parts_of[n], wl[n], ml[n], vl[n], name=f"adamw_{n}")
    behind = reduce_chips("even_in", outs["ffn_w_down"][1])
    flip = lambda a: jnp.transpose(a, (0, 2, 1))
    outs["even_w_in"] = [flip(o) for o in adamw([swapped("even_in", behind)[0]], flip(wl["even_w_in"]),
                                                flip(ml["even_w_in"]), flip(vl["even_w_in"]),
                                                name="adamw_even_w_in")]

    g = small["local"]
    small_sum = sum_devices(small["all"], name="sum_devices")
    sg = dict(zip(SMALL_LOCAL_GRADS, _unpack_rows(small_sum, [g[n].shape for n in SMALL_LOCAL_GRADS])))
    own_cols = lambda a, width: lax.dynamic_slice_in_dim(a, me * width, width, axis=a.ndim - 1)
    small_grads = {
        "even_norm": sg["even_norm"], "even_conv": own_cols(sg["even_conv"], 768)[None],
        "even_a_log": sg["a_log"][:, B_HEADS:2 * B_HEADS], "even_dt_bias": sg["dt_bias"][:, B_HEADS:2 * B_HEADS],
        "even_sinks": sg["sinks"][:, :A_HEADS], "even_onorm": sg["onorm"],
        "odd_norm": own_cols(sg["odd_norm"], 512), "odd_ln_g": own_cols(sg["odd_ln_g"], 512),
        "odd_ln_b": own_cols(sg["odd_ln_b"], 512), "odd_w_s": sg["odd_w_s"][None],
        "odd_b_s": sg["odd_b_s"][:, :C_GROUPS].T[None], "ffn_norm": sg["ffn_norm"], "final_norm": sg["final_norm"][0],
    }
    packed = [_pack_rows([d[n] for n in SMALL])[None] for d in (small_grads, wl, ml, vl)]
    small_out = adamw([(packed[0][0],)], packed[1], packed[2], packed[3], name="adamw_small")
    shapes = [wl[n].shape for n in SMALL]
    for j in range(4):
        for n, a in zip(SMALL, _unpack_rows(small_out[j][0], shapes)):
            outs.setdefault(n, [None] * 4)[j] = a

    return (loss, grad_x[None], *[outs[n][0] for n in WEIGHTS], *[outs[n][1] for n in WEIGHTS],
            *[outs[n][2] for n in WEIGHTS], *[outs[n][3] for n in WEIGHTS])
```

```python
import functools

import jax
import jax.numpy as jnp
from jax import lax
from jax.experimental import pallas as pl
from jax.experimental.pallas import tpu as pltpu

F32 = jnp.float32
BF16 = jnp.bfloat16
NEG_INF = float("-inf")

D_MODEL = 2048
A_HEADS, A_KV_HEADS, A_HEAD_DIM, WINDOW = 16, 2, 64, 128
B_HEADS, B_HEAD_DIM, CONV_K, DN_CHUNK = 8, 128, 4, 64
C_GROUPS, C_CHUNK = 8, 128
C_GROUP_DIM = D_MODEL // C_GROUPS
D_FF = 5632
EPS = 1e-6
A_Q = A_HEADS * A_HEAD_DIM
A_KV = A_KV_HEADS * A_HEAD_DIM
B_W = B_HEADS * B_HEAD_DIM
EVEN_IN = A_Q + 2 * A_KV + 4 * B_W + 2 * B_HEADS
EVEN_IN_PAD = 5632
COL_KV = A_Q
COL_QKVB = A_Q + 2 * A_KV
COL_Z = COL_QKVB + 3 * B_W
COL_GATE = COL_Z + B_W
N_SHARD = 4

ADAM_LR, ADAM_B1, ADAM_B2, ADAM_EPS, ADAM_WD, ADAM_STEP = 0.001, 0.9, 0.999, 1e-08, 0.01, 10

VMEM_LIMIT_V7X = 56 * 1024 * 1024
MXU_COLS = 256
MESH_ID = pl.DeviceIdType.MESH


def _params(sem=None):
    return pltpu.CompilerParams(dimension_semantics=sem, vmem_limit_bytes=VMEM_LIMIT_V7X)


def _sigmoid(x):
    return 1.0 / (1.0 + jnp.exp(-x))


def _silu(x):
    return x * _sigmoid(x)


def _dsilu(x):
    s = _sigmoid(x)
    return s * (1.0 + x * (1.0 - s))


def _gelu(x):
    return 0.5 * x * (1.0 + lax.erf(x * 0.7071067811865476))


def _dgelu(x):
    return 0.5 * (1.0 + lax.erf(x * 0.7071067811865476)) + x * jnp.exp(-0.5 * x * x) * 0.3989422804014327


def _dot(a, b, dims):
    if a.ndim == 3:
        (ca,), (cb,) = dims
        return lax.dot_general(a, b, (((ca + 1,), (cb + 1,)), ((0,), (0,))), preferred_element_type=F32)
    return lax.dot_general(a, b, (dims, ((), ())), preferred_element_type=F32)


NN = ((1,), (0,))
NT = ((1,), (1,))
TN = ((0,), (0,))


def _as3(b):
    return b if b.ndim == 3 else b[None]


def _accumulate(step, nsteps, accs, products, finish):
    if nsteps == 1:
        finish(products())
        return

    @pl.when(step == 0)
    def _():
        for acc, p in zip(accs, products()):
            acc[...] = p

    if nsteps > 2:
        @pl.when((step > 0) & (step < nsteps - 1))
        def _():
            for acc, p in zip(accs, products()):
                acc[...] += p

    @pl.when(step == nsteps - 1)
    def _():
        finish(tuple(acc[...] + p for acc, p in zip(accs, products())))


def mm_nn(a, b, *, tm, tn, tk, out_dtype, name, res=None):
    b3 = _as3(b)
    m, k = a.shape
    s, k2, ns = b3.shape
    assert k2 == k and m % tm == 0 and ns % tn == 0 and k % tk == 0, (a.shape, b3.shape, tm, tn, tk)
    nps, nk = ns // tn, k // tk

    def body(*refs):
        if res is None:
            a_ref, b_ref, o_ref, acc = refs
        else:
            a_ref, b_ref, r_ref, o_ref, acc = refs
        def finish(tiles):
            r = tiles[0] if res is None else tiles[0] + r_ref[...].astype(F32)
            o_ref[...] = r.astype(out_dtype)

        _accumulate(pl.program_id(2), nk, (acc,),
                    lambda: (_dot(a_ref[...].astype(BF16), b_ref[...].astype(BF16), NN),), finish)

    in_specs = [pl.BlockSpec((tm, tk), lambda i, j, kk: (i, kk)),
                pl.BlockSpec((None, tk, tn), lambda i, j, kk: (j // nps, kk, j % nps))]
    args = [a, b3]
    if res is not None:
        in_specs.append(pl.BlockSpec((tm, tn), lambda i, j, kk: (i, j)))
        args.append(res)
    return pl.pallas_call(
        body, name=name, grid=(m // tm, s * nps, nk), in_specs=in_specs,
        out_specs=pl.BlockSpec((tm, tn), lambda i, j, kk: (i, j)),
        out_shape=jax.ShapeDtypeStruct((m, s * ns), out_dtype),
        scratch_shapes=[pltpu.VMEM((tm, tn), F32)],
        compiler_params=_params(("parallel", "parallel", "arbitrary")))(*args)


def mm_nt(a, b, *, tm, tn, tk, out_dtype, name, res=None):
    b3 = _as3(b)
    m, n = a.shape
    s, k, ns = b3.shape
    assert n == s * ns and m % tm == 0 and k % tn == 0 and ns % tk == 0, (a.shape, b3.shape, tm, tn, tk)
    rps = ns // tk
    nr = s * rps

    def body(*refs):
        if res is None:
            a_ref, b_ref, o_ref, acc = refs
        else:
            a_ref, b_ref, r_ref, o_ref, acc = refs
        def finish(tiles):
            r = tiles[0] if res is None else tiles[0] + r_ref[...].astype(F32)
            o_ref[...] = r.astype(out_dtype)

        _accumulate(pl.program_id(2), nr, (acc,),
                    lambda: (_dot(a_ref[...].astype(BF16), b_ref[...].astype(BF16), NT),), finish)

    in_specs = [pl.BlockSpec((tm, tk), lambda i, j, r: (i, r)),
                pl.BlockSpec((None, tn, tk), lambda i, j, r: (r // rps, j, r % rps))]
    args = [a, b3]
    if res is not None:
        in_specs.append(pl.BlockSpec((tm, tn), lambda i, j, r: (i, j)))
        args.append(res)
    return pl.pallas_call(
        body, name=name, grid=(m // tm, k // tn, nr), in_specs=in_specs,
        out_specs=pl.BlockSpec((tm, tn), lambda i, j, r: (i, j)),
        out_shape=jax.ShapeDtypeStruct((m, k), out_dtype),
        scratch_shapes=[pltpu.VMEM((tm, tn), F32)],
        compiler_params=_params(("parallel", "parallel", "arbitrary")))(*args)


def mm_tn(a, b, *, shards, tm, tn, tk, out_dtype, name):
    m, k = a.shape
    m2, n = b.shape
    ns = n // shards
    assert m2 == m and n == shards * ns and m % tm == 0 and k % tk == 0 and ns % tn == 0, (a.shape, b.shape)
    nps, nm = ns // tn, m // tm

    def body(a_ref, b_ref, o_ref, acc):
        def finish(tiles):
            o_ref[...] = tiles[0].astype(out_dtype)

        _accumulate(pl.program_id(2), nm, (acc,),
                    lambda: (_dot(a_ref[...].astype(BF16), b_ref[...].astype(BF16), TN),), finish)

    return pl.pallas_call(
        body, name=name, grid=(k // tk, shards * nps, nm),
        in_specs=[pl.BlockSpec((tm, tk), lambda i, j, mi: (mi, i)),
                  pl.BlockSpec((tm, tn), lambda i, j, mi: (mi, j))],
        out_specs=pl.BlockSpec((None, tk, tn), lambda i, j, mi: (j // nps, i, j % nps)),
        out_shape=jax.ShapeDtypeStruct((shards, k, ns), out_dtype),
        scratch_shapes=[pltpu.VMEM((tk, tn), F32)],
        compiler_params=_params(("parallel", "parallel", "arbitrary")))(a, b)


def mm_gate_up(hn, wg, wu, *, tm, tn, tk, name):
    wg3, wu3 = _as3(wg), _as3(wu)
    m, k = hn.shape
    s, _, ns = wg3.shape
    assert m % tm == 0 and ns % tn == 0 and k % tk == 0
    nps, nk = ns // tn, k // tk

    def body(a_ref, g_ref, u_ref, og_ref, ou_ref, oa_ref, accg, accu):
        def products():
            a = a_ref[...].astype(BF16)
            return _dot(a, g_ref[...].astype(BF16), NN), _dot(a, u_ref[...].astype(BF16), NN)

        def finish(tiles):
            g, u = tiles
            og_ref[...] = g.astype(BF16)
            ou_ref[...] = u.astype(BF16)
            oa_ref[...] = (_silu(g) * u).astype(BF16)

        _accumulate(pl.program_id(2), nk, (accg, accu), products, finish)

    wspec = pl.BlockSpec((None, tk, tn), lambda i, j, kk: (j // nps, kk, j % nps))
    ospec = pl.BlockSpec((tm, tn), lambda i, j, kk: (i, j))
    osh = jax.ShapeDtypeStruct((m, s * ns), BF16)
    return pl.pallas_call(
        body, name=name, grid=(m // tm, s * nps, nk),
        in_specs=[pl.BlockSpec((tm, tk), lambda i, j, kk: (i, kk)), wspec, wspec],
        out_specs=[ospec, ospec, ospec], out_shape=[osh, osh, osh],
        scratch_shapes=[pltpu.VMEM((tm, tn) if nk > 1 else (8, 128), F32)] * 2,
        compiler_params=_params(("parallel", "parallel", "arbitrary")))(hn, wg3, wu3)


def mm_down_bwd(dh, wd, gate, up, *, tm, tn, tk, name):
    m, d = dh.shape
    f, d2 = wd.shape
    assert d2 == d and m % tm == 0 and f % tn == 0 and tk == d and tn % MXU_COLS == 0

    def body(a_ref, b_ref, g_ref, u_ref, og_ref, ou_ref):
        a = a_ref[...].astype(BF16)
        for jj in range(tn // MXU_COLS):
            sl = slice(jj * MXU_COLS, (jj + 1) * MXU_COLS)
            da = _dot(a, b_ref[sl, :].astype(BF16), NT)
            g, u = g_ref[:, sl].astype(F32), u_ref[:, sl].astype(F32)
            s = _sigmoid(g)
            og_ref[:, sl] = (da * u * (s * (1.0 + g * (1.0 - s)))).astype(BF16)
            ou_ref[:, sl] = (da * (g * s)).astype(BF16)

    ospec = pl.BlockSpec((tm, tn), lambda i, j: (i, j))
    osh = jax.ShapeDtypeStruct((m, f), BF16)
    return pl.pallas_call(
        body, name=name, grid=(m // tm, f // tn),
        in_specs=[pl.BlockSpec((tm, tk), lambda i, j: (i, 0)),
                  pl.BlockSpec((tn, tk), lambda i, j: (j, 0)), ospec, ospec],
        out_specs=[ospec, ospec], out_shape=[osh, osh],
        compiler_params=_params(("parallel", "parallel")))(dh, wd, gate, up)


ROWS = 256


def rms_fwd(x, g, *, name):
    t, d = x.shape

    def body(x_ref, g_ref, o_ref):
        xv = x_ref[...]
        r = lax.rsqrt(jnp.mean(xv * xv, axis=-1, keepdims=True) + EPS)
        o_ref[...] = (xv * r * g_ref[...]).astype(BF16)

    return pl.pallas_call(
        body, name=name, grid=(t // ROWS,),
        in_specs=[pl.BlockSpec((ROWS, d), lambda i: (i, 0)), pl.BlockSpec((1, d), lambda i: (0, 0))],
        out_specs=pl.BlockSpec((ROWS, d), lambda i: (i, 0)),
        out_shape=jax.ShapeDtypeStruct((t, d), BF16), compiler_params=_params(("parallel",)))(x, g)


def rms_bwd(x, g, dy, dres, *, name):
    t, d = x.shape

    def body(x_ref, g_ref, dy_ref, dr_ref, dx_ref, dg_ref):
        @pl.when(pl.program_id(0) == 0)
        def _():
            dg_ref[...] = jnp.zeros_like(dg_ref)

        xv, dyv = x_ref[...], dy_ref[...].astype(F32)
        r = lax.rsqrt(jnp.mean(xv * xv, axis=-1, keepdims=True) + EPS)
        dyg = dyv * g_ref[...]
        dx = r * dyg - xv * (r * r * r) * jnp.mean(dyg * xv, axis=-1, keepdims=True)
        dx_ref[...] = dx + dr_ref[...]
        dg_ref[...] += jnp.sum(dyv * xv * r, axis=0, keepdims=True)

    row = pl.BlockSpec((ROWS, d), lambda i: (i, 0))
    vec = pl.BlockSpec((1, d), lambda i: (0, 0))
    return pl.pallas_call(
        body, name=name, grid=(t // ROWS,), in_specs=[row, vec, row, row], out_specs=[row, vec],
        out_shape=[jax.ShapeDtypeStruct((t, d), F32), jax.ShapeDtypeStruct((1, d), F32)],
        compiler_params=_params(("arbitrary",)))(x, g, dy, dres)


def dgrad_rms_bwd(a, b, form, x, g, dres, *, tm, tk, name, res=None):
    m, d = x.shape
    b3 = _as3(b)
    if form == NN:
        steps = a.shape[1] // tk
        a_spec = pl.BlockSpec((tm, tk), lambda i, r: (i, r))
        b_spec = pl.BlockSpec((None, tk, d), lambda i, r: (0, r, 0))
    else:
        s, d2, ns = b3.shape
        assert d2 == d and ns % tk == 0
        rps = ns // tk
        steps = s * rps
        a_spec = pl.BlockSpec((tm, tk), lambda i, r: (i, r))
        b_spec = pl.BlockSpec((None, d, tk), lambda i, r: (r // rps, 0, r % rps))
    assert m % tm == 0 and a.shape[1] == steps * tk

    def body(*refs):
        if res is None:
            a_ref, b_ref, x_ref, g_ref, dr_ref, dx_ref, dg_ref, acc = refs
        else:
            a_ref, b_ref, r_ref, x_ref, g_ref, dr_ref, dx_ref, dg_ref, acc = refs

        @pl.when((pl.program_id(0) == 0) & (pl.program_id(1) == 0))
        def _():
            dg_ref[...] = jnp.zeros_like(dg_ref)

        def finish(tiles):
            dyv = tiles[0] if res is None else tiles[0] + r_ref[...]
            xv = x_ref[...]
            r = lax.rsqrt(jnp.mean(xv * xv, axis=-1, keepdims=True) + EPS)
            dyg = dyv * g_ref[...]
            dx_ref[...] = r * dyg - xv * (r * r * r) * jnp.mean(dyg * xv, axis=-1, keepdims=True) + dr_ref[...]
            dg_ref[...] += jnp.sum(dyv * xv * r, axis=0, keepdims=True)

        _accumulate(pl.program_id(1), steps, (acc,),
                    lambda: (_dot(a_ref[...].astype(BF16), b_ref[...].astype(BF16), form),), finish)

    row = pl.BlockSpec((tm, d), lambda i, r: (i, 0))
    vec = pl.BlockSpec((1, d), lambda i, r: (0, 0))
    in_specs = [a_spec, b_spec] + ([row] if res is not None else []) + [row, vec, row]
    args = [a, b3] + ([res] if res is not None else []) + [x, g, dres]
    return pl.pallas_call(
        body, name=name, grid=(m // tm, steps), in_specs=in_specs, out_specs=[row, vec],
        out_shape=[jax.ShapeDtypeStruct((m, d), F32), jax.ShapeDtypeStruct((1, d), F32)],
        scratch_shapes=[pltpu.VMEM((tm, d), F32)],
        compiler_params=_params(("arbitrary", "arbitrary")))(*args)


def loss_head(h, g, target, *, name):
    t, d = h.shape

    def body(x_ref, g_ref, t_ref, loss_ref, dx_ref, dg_ref):
        @pl.when(pl.program_id(0) == 0)
        def _():
            dg_ref[...] = jnp.zeros_like(dg_ref)
            loss_ref[...] = jnp.zeros_like(loss_ref)

        xv, gv = x_ref[...], g_ref[...]
        r = lax.rsqrt(jnp.mean(xv * xv, axis=-1, keepdims=True) + EPS)
        e = xv * r * gv - t_ref[...]
        loss_ref[...] += 0.5 * jnp.sum(jnp.mean(e * e, axis=-1, keepdims=True), axis=0, keepdims=True)
        dyv = e * (1.0 / d)
        dyg = dyv * gv
        dx_ref[...] = r * dyg - xv * (r * r * r) * jnp.mean(dyg * xv, axis=-1, keepdims=True)
        dg_ref[...] += jnp.sum(dyv * xv * r, axis=0, keepdims=True)

    row = pl.BlockSpec((ROWS, d), lambda i: (i, 0))
    vec = pl.BlockSpec((1, d), lambda i: (0, 0))
    return pl.pallas_call(
        body, name=name, grid=(t // ROWS,), in_specs=[row, vec, row],
        out_specs=[pl.BlockSpec((1, 128), lambda i: (0, 0)), row, vec],
        out_shape=[jax.ShapeDtypeStruct((1, 128), F32), jax.ShapeDtypeStruct((t, d), F32),
                   jax.ShapeDtypeStruct((1, d), F32)],
        compiler_params=_params(("arbitrary",)))(h, g, target)


def _tril_mask():
    r = lax.broadcasted_iota(jnp.int32, (C_CHUNK, C_CHUNK), 0)
    c = lax.broadcasted_iota(jnp.int32, (C_CHUNK, C_CHUNK), 1)
    return r >= c


def _layer_norm_parts(v):
    mu = jnp.mean(v, axis=-1, keepdims=True)
    vc = v - mu
    rstd = lax.rsqrt(jnp.mean(vc * vc, axis=-1, keepdims=True) + EPS)
    return vc * rstd, rstd


def gmlp_fwd(zpre, ln_g, ln_b, ws, bs_t, *, name):
    t = zpre.shape[0]
    d = D_MODEL

    def body(zu_ref, zv_ref, g_ref, b_ref, ws_ref, bs_ref, o_ref):
        u = _gelu(zu_ref[...])
        vhat, _ = _layer_norm_parts(_gelu(zv_ref[...]))
        vln = (vhat * g_ref[...] + b_ref[...]).astype(BF16)
        mask = _tril_mask()
        for gi in range(C_GROUPS):
            sl = slice(gi * C_GROUP_DIM, (gi + 1) * C_GROUP_DIM)
            w = jnp.where(mask, ws_ref[gi], 0.0).astype(BF16)
            mixed = _dot(w, vln[:, sl], NN) + bs_ref[:, gi:gi + 1]
            o_ref[:, sl] = (u[:, sl] * mixed).astype(BF16)

    vec = pl.BlockSpec((1, d), lambda i: (0, 0))
    return pl.pallas_call(
        body, name=name, grid=(t // C_CHUNK,),
        in_specs=[pl.BlockSpec((C_CHUNK, d), lambda i: (i, 0)), pl.BlockSpec((C_CHUNK, d), lambda i: (i, 1)),
                  vec, vec, pl.BlockSpec((C_GROUPS, C_CHUNK, C_CHUNK), lambda i: (0, 0, 0)),
                  pl.BlockSpec((C_CHUNK, 128), lambda i: (0, 0))],
        out_specs=pl.BlockSpec((C_CHUNK, d), lambda i: (i, 0)),
        out_shape=jax.ShapeDtypeStruct((t, d), BF16), compiler_params=_params(("parallel",)))(
            zpre, zpre, ln_g, ln_b, ws, bs_t)


def gmlp_bwd(zpre, dgated, ln_g, ln_b, ws, bs_t, *, name):
    t = zpre.shape[0]
    d = D_MODEL

    def body(zu_ref, zv_ref, dg_ref, g_ref, b_ref, ws_ref, bs_ref, dz_ref, dws_ref, dbs_ref, dlg_ref, dlb_ref):
        @pl.when(pl.program_id(0) == 0)
        def _():
            dws_ref[...] = jnp.zeros_like(dws_ref)
            dbs_ref[...] = jnp.zeros_like(dbs_ref)
            dlg_ref[...] = jnp.zeros_like(dlg_ref)
            dlb_ref[...] = jnp.zeros_like(dlb_ref)

        zu, zv = zu_ref[...], zv_ref[...]
        u = _gelu(zu)
        vhat, rstd = _layer_norm_parts(_gelu(zv))
        gam = g_ref[...]
        vln = (vhat * gam + b_ref[...]).astype(BF16)
        dgt = dg_ref[...].astype(F32)
        mask = _tril_mask()
        lane = lax.broadcasted_iota(jnp.int32, (C_CHUNK, 128), 1)
        dbs = jnp.zeros((C_CHUNK, 128), F32)
        du_parts, dvln_parts = [], []
        for gi in range(C_GROUPS):
            sl = slice(gi * C_GROUP_DIM, (gi + 1) * C_GROUP_DIM)
            w = jnp.where(mask, ws_ref[gi], 0.0).astype(BF16)
            mixed = _dot(w, vln[:, sl], NN) + bs_ref[:, gi:gi + 1]
            du_parts.append(dgt[:, sl] * mixed)
            dmixed = dgt[:, sl] * u[:, sl]
            dmb = dmixed.astype(BF16)
            dws_ref[gi] += jnp.where(mask, _dot(dmb, vln[:, sl], NT), 0.0)
            dbs = dbs + jnp.where(lane == gi, jnp.sum(dmixed, axis=-1, keepdims=True), 0.0)
            dvln_parts.append(_dot(w, dmb, TN))
        dbs_ref[...] += dbs
        du = jnp.concatenate(du_parts, axis=-1)
        dvln = jnp.concatenate(dvln_parts, axis=-1)
        dlg_ref[...] += jnp.sum(dvln * vhat, axis=0, keepdims=True)
        dlb_ref[...] += jnp.sum(dvln, axis=0, keepdims=True)
        dvhat = dvln * gam
        dv = rstd * (dvhat - jnp.mean(dvhat, axis=-1, keepdims=True)
                     - vhat * jnp.mean(dvhat * vhat, axis=-1, keepdims=True))
        dz_ref[:, :d] = (du * _dgelu(zu)).astype(BF16)
        dz_ref[:, d:] = (dv * _dgelu(zv)).astype(BF16)

    vec = pl.BlockSpec((1, d), lambda i: (0, 0))
    wsp = pl.BlockSpec((C_GROUPS, C_CHUNK, C_CHUNK), lambda i: (0, 0, 0))
    bsp = pl.BlockSpec((C_CHUNK, 128), lambda i: (0, 0))
    return pl.pallas_call(
        body, name=name, grid=(t // C_CHUNK,),
        in_specs=[pl.BlockSpec((C_CHUNK, d), lambda i: (i, 0)), pl.BlockSpec((C_CHUNK, d), lambda i: (i, 1)),
                  pl.BlockSpec((C_CHUNK, d), lambda i: (i, 0)), vec, vec, wsp, bsp],
        out_specs=[pl.BlockSpec((C_CHUNK, 2 * d), lambda i: (i, 0)), wsp, bsp, vec, vec],
        out_shape=[jax.ShapeDtypeStruct((t, 2 * d), BF16), jax.ShapeDtypeStruct((C_GROUPS, C_CHUNK, C_CHUNK), F32),
                   jax.ShapeDtypeStruct((C_CHUNK, 128), F32), jax.ShapeDtypeStruct((1, d), F32),
                   jax.ShapeDtypeStruct((1, d), F32)],
        compiler_params=_params(("arbitrary",)))(zpre, zpre, dgated, ln_g, ln_b, ws, bs_t)


ATT_SCALE = A_HEAD_DIM ** -0.5
PAIRS = A_HEADS // 2
PAIRS_PER_KV = PAIRS // A_KV_HEADS


def _att_padded(tile):
    lo = lax.broadcasted_iota(jnp.int32, tile.shape, 1) < A_HEAD_DIM
    rolled = pltpu.roll(tile, A_HEAD_DIM, 1)
    zero = jnp.zeros_like(tile)
    return {(0, 0): jnp.where(lo, tile, zero).astype(BF16), (0, 1): jnp.where(lo, zero, rolled).astype(BF16),
            (1, 0): jnp.where(lo, rolled, zero).astype(BF16), (1, 1): jnp.where(lo, zero, tile).astype(BF16)}


def _att_valid(n):
    r = lax.broadcasted_iota(jnp.int32, (WINDOW, 2 * WINDOW), 0)
    c = lax.broadcasted_iota(jnp.int32, (WINDOW, 2 * WINDOW), 1)
    rel = r + WINDOW - c
    return (rel >= 0) & (rel < WINDOW) & ((c >= WINDOW) | (n > 0))


def _att_probs(qp, kpad, sink, valid):
    s = jnp.where(valid, _dot(qp, kpad, NT), NEG_INF)
    m = jnp.maximum(jnp.max(s, axis=-1, keepdims=True), sink)
    p = jnp.exp(s - m)
    e_sink = jnp.exp(sink - m)
    inv = 1.0 / (jnp.sum(p, axis=-1, keepdims=True) + e_sink)
    return p * inv, e_sink * inv


def _att_operands(q_ref, kvc_ref, kvp_ref, s_ref):
    kv = jnp.concatenate([kvp_ref[...], kvc_ref[...]], axis=0)
    kpad, vpad = _att_padded(kv[:, :128]), _att_padded(kv[:, 128:])
    key = lambda h: ((h // 2) // PAIRS_PER_KV, h % 2)
    pairs = [(q_ref[:, j * 128:(j + 1) * 128] * ATT_SCALE).astype(BF16) for j in range(PAIRS)]
    q = jnp.stack([pairs[h // 2] for h in range(A_HEADS)])
    k = jnp.stack([kpad[key(h)] for h in range(A_HEADS)])
    v = jnp.stack([vpad[key(h)] for h in range(A_HEADS)])
    sink = jnp.stack([s_ref[:, h:h + 1] for h in range(A_HEADS)])
    return q, k, v, sink


def _att_specs(t):
    return [pl.BlockSpec((WINDOW, A_Q), lambda n: (n, 0)),
            pl.BlockSpec((WINDOW, 2 * A_KV), lambda n: (n, COL_KV // (2 * A_KV))),
            pl.BlockSpec((WINDOW, 2 * A_KV), lambda n: (jnp.maximum(n - 1, 0), COL_KV // (2 * A_KV))),
            pl.BlockSpec((1, 128), lambda n: (0, 0))]


def att_fwd(proj, sinks, *, name):
    t = proj.shape[0]

    def body(q_ref, kvc_ref, kvp_ref, s_ref, o_ref):
        n = pl.program_id(0)
        q, k, v, sink = _att_operands(q_ref, kvc_ref, kvp_ref, s_ref)
        w, _ = _att_probs(q, k, sink, _att_valid(n))
        o = _dot(w.astype(BF16), v, NN)
        for j in range(PAIRS):
            o_ref[:, j * 128:(j + 1) * 128] = (o[2 * j] + o[2 * j + 1]).astype(BF16)

    return pl.pallas_call(
        body, name=name, grid=(t // WINDOW,), in_specs=_att_specs(t),
        out_specs=pl.BlockSpec((WINDOW, A_Q), lambda n: (n, 0)),
        out_shape=jax.ShapeDtypeStruct((t, A_Q), BF16), compiler_params=_params(("parallel",)))(
            proj, proj, proj, sinks)


def att_bwd(proj, sinks, dout, *, name):
    t = proj.shape[0]

    def body(q_ref, kvc_ref, kvp_ref, s_ref, do_ref, dq_ref, dkc_ref, dkp_ref, ds_ref):
        n = pl.program_id(0)

        @pl.when(n == 0)
        def _():
            ds_ref[...] = jnp.zeros_like(ds_ref)

        q, k, v, sink = _att_operands(q_ref, kvc_ref, kvp_ref, s_ref)
        dop = jnp.stack([do_ref[:, (h // 2) * 128:(h // 2 + 1) * 128] for h in range(A_HEADS)]).astype(BF16)
        w, w_sink = _att_probs(q, k, sink, _att_valid(n))
        dw = _dot(dop, v, NT)
        delta = jnp.sum(w * dw, axis=-1, keepdims=True)
        dsc = (w * (dw - delta)).astype(BF16)
        dsink_h = -jnp.sum(w_sink * delta, axis=1, keepdims=True)
        dq = _dot(dsc, k, NN)
        dk_h = _dot(dsc, q, TN)
        dv_h = _dot(w.astype(BF16), dop, TN)
        lane = lax.broadcasted_iota(jnp.int32, (1, 128), 1)
        dsink = jnp.zeros((1, 128), F32)
        for h in range(A_HEADS):
            dsink = dsink + jnp.where(lane == h, dsink_h[h], 0.0)
        ds_ref[...] += dsink
        for j in range(PAIRS):
            dq_ref[:, j * 128:(j + 1) * 128] = ((dq[2 * j] + dq[2 * j + 1]) * ATT_SCALE).astype(BF16)
        lo = lax.broadcasted_iota(jnp.int32, (2 * WINDOW, 128), 1) < A_HEAD_DIM
        heads_per_kv = A_HEADS // A_KV_HEADS

        def tile(per_head):
            acc = {}
            for kvh in range(A_KV_HEADS):
                for half in range(2):
                    hs = range(kvh * heads_per_kv + half, (kvh + 1) * heads_per_kv, 2)
                    acc[(kvh, half)] = functools.reduce(lambda a, b: a + b, [per_head[h] for h in hs])
            return jnp.where(lo, acc[(0, 0)] + pltpu.roll(acc[(0, 1)], A_HEAD_DIM, 1),
                             pltpu.roll(acc[(1, 0)], A_HEAD_DIM, 1) + acc[(1, 1)])

        dkv = jnp.concatenate([tile(dk_h), tile(dv_h)], axis=1)
        dkp_ref[...] = dkv[:WINDOW]
        dkc_ref[...] = dkv[WINDOW:]

    kvo = pl.BlockSpec((WINDOW, 2 * A_KV), lambda n: (n, 0))
    return pl.pallas_call(
        body, name=name, grid=(t // WINDOW,),
        in_specs=_att_specs(t) + [pl.BlockSpec((WINDOW, A_Q), lambda n: (n, 0))],
        out_specs=[pl.BlockSpec((WINDOW, A_Q), lambda n: (n, 0)), kvo, kvo, pl.BlockSpec((1, 128), lambda n: (0, 0))],
        out_shape=[jax.ShapeDtypeStruct((t, A_Q), BF16), jax.ShapeDtypeStruct((t, 2 * A_KV), F32),
                   jax.ShapeDtypeStruct((t, 2 * A_KV), F32), jax.ShapeDtypeStruct((1, 128), F32)],
        compiler_params=_params(("arbitrary",)))(proj, proj, proj, sinks, dout)


QK_SCALE = B_HEAD_DIM ** -0.5
PREP_COLS = 256
PREP_NCB = 3 * B_W // PREP_COLS
HALO = 8
PREP_ROWS = 512


def _roll_rows(x, shift):
    n = x.shape[0]
    return x if shift % n == 0 else pltpu.roll(x, shift % n, 0)


def _conv_taps(xe, w):
    xs = [_roll_rows(xe, CONV_K - 1 - i) for i in range(CONV_K)]
    c = w[0:1] * xs[0]
    for i in range(1, CONV_K):
        c = c + w[i:i + 1] * xs[i]
    return xs, c


def dprep_fwd(proj, conv_w, *, name):
    t = proj.shape[0]
    tt = min(PREP_ROWS, t)
    col0 = COL_QKVB // PREP_COLS

    def body(x_ref, h_ref, w_ref, o_ref):
        cb, n = pl.program_id(0), pl.program_id(1)
        halo = jnp.where(n > 0, h_ref[...], 0.0)
        xe = jnp.concatenate([halo, x_ref[...]], axis=0)
        _, c = _conv_taps(xe, w_ref[...])
        y = _silu(c)[HALO:]
        parts = []
        for hh in range(PREP_COLS // B_HEAD_DIM):
            yh = y[:, hh * B_HEAD_DIM:(hh + 1) * B_HEAD_DIM]
            parts.append(yh * lax.rsqrt(jnp.sum(yh * yh, axis=-1, keepdims=True) + EPS))
        nrm = jnp.concatenate(parts, axis=-1)
        o_ref[...] = jnp.where(cb < 4, nrm * QK_SCALE, jnp.where(cb < 8, nrm, y))

    return pl.pallas_call(
        body, name=name, grid=(PREP_NCB, t // tt),
        in_specs=[pl.BlockSpec((tt, PREP_COLS), lambda cb, n: (n, col0 + cb)),
                  pl.BlockSpec((HALO, PREP_COLS), lambda cb, n: (jnp.maximum(n * (tt // HALO) - 1, 0), col0 + cb)),
                  pl.BlockSpec((CONV_K, PREP_COLS), lambda cb, n: (0, cb))],
        out_specs=pl.BlockSpec((tt, PREP_COLS), lambda cb, n: (n, cb)),
        out_shape=jax.ShapeDtypeStruct((t, 3 * B_W), F32), compiler_params=_params(("parallel", "parallel")))(
            proj, proj, conv_w)


def dprep_bwd(proj, conv_w, dqkvn, *, name):
    t = proj.shape[0]
    tt = min(PREP_ROWS, t)
    nb = t // tt
    col0 = COL_QKVB // PREP_COLS
    n8 = t // HALO

    def body(xc_ref, xb_ref, xa_ref, dc_ref, da_ref, w_ref, dx_ref, dw_ref):
        cb, n = pl.program_id(0), pl.program_id(1)

        @pl.when(n == 0)
        def _():
            dw_ref[...] = jnp.zeros_like(dw_ref)

        w = w_ref[...]
        xe = jnp.concatenate([jnp.where(n > 0, xb_ref[...], 0.0), xc_ref[...], xa_ref[...]], axis=0)
        xs, c = _conv_taps(xe, w)
        sg = _sigmoid(c)
        y = c * sg
        dout = jnp.concatenate([jnp.zeros((HALO, PREP_COLS), F32), dc_ref[...],
                                jnp.where(n < nb - 1, da_ref[...], 0.0)], axis=0)
        dsc = jnp.where(cb < 4, QK_SCALE, 1.0)
        parts = []
        for hh in range(PREP_COLS // B_HEAD_DIM):
            sl = slice(hh * B_HEAD_DIM, (hh + 1) * B_HEAD_DIM)
            yh, doh = y[:, sl], dout[:, sl] * dsc
            r = lax.rsqrt(jnp.sum(yh * yh, axis=-1, keepdims=True) + EPS)
            parts.append(doh * r - yh * (r * r * r) * jnp.sum(doh * yh, axis=-1, keepdims=True))
        dy = jnp.where(cb < 8, jnp.concatenate(parts, axis=-1), dout)
        dcv = dy * sg * (1.0 + c * (1.0 - sg))
        dxe = w[CONV_K - 1:CONV_K] * dcv
        for i in range(CONV_K - 1):
            dxe = dxe + w[i:i + 1] * _roll_rows(dcv, -(CONV_K - 1 - i))
        dx_ref[...] = dxe[HALO:HALO + tt].astype(BF16)
        for i in range(CONV_K):
            dw_ref[i:i + 1, :] += jnp.sum((dcv * xs[i])[HALO:HALO + tt], axis=0, keepdims=True)

    def after(n):
        return jnp.minimum((n + 1) * (tt // HALO), n8 - 1)

    return pl.pallas_call(
        body, name=name, grid=(PREP_NCB, nb),
        in_specs=[pl.BlockSpec((tt, PREP_COLS), lambda cb, n: (n, col0 + cb)),
                  pl.BlockSpec((HALO, PREP_COLS), lambda cb, n: (jnp.maximum(n * (tt // HALO) - 1, 0), col0 + cb)),
                  pl.BlockSpec((HALO, PREP_COLS), lambda cb, n: (after(n), col0 + cb)),
                  pl.BlockSpec((tt, PREP_COLS), lambda cb, n: (n, cb)),
                  pl.BlockSpec((HALO, PREP_COLS), lambda cb, n: (after(n), cb)),
                  pl.BlockSpec((CONV_K, PREP_COLS), lambda cb, n: (0, cb))],
        out_specs=[pl.BlockSpec((tt, PREP_COLS), lambda cb, n: (n, cb)),
                   pl.BlockSpec((CONV_K, PREP_COLS), lambda cb, n: (0, cb))],
        out_shape=[jax.ShapeDtypeStruct((t, 3 * B_W), BF16), jax.ShapeDtypeStruct((CONV_K, 3 * B_W), F32)],
        compiler_params=_params(("parallel", "arbitrary")))(proj, proj, proj, dqkvn, dqkvn, conv_w)


def _softplus(z):
    return jnp.maximum(z, 0.0) + jnp.log(1.0 + jnp.exp(-jnp.abs(z)))


def gates_fwd(proj, alog_pad, dtb_pad, *, name):
    t = proj.shape[0]

    def body(x_ref, a_ref, b_ref, o_ref):
        raw = x_ref[...]
        lane = lax.broadcasted_iota(jnp.int32, raw.shape, 1)
        g = -jnp.exp(a_ref[...]) * _softplus(raw + b_ref[...])
        o_ref[...] = jnp.where(lane < B_HEADS, _sigmoid(raw), jnp.where(lane < 2 * B_HEADS, g, 0.0))

    vec = pl.BlockSpec((1, 128), lambda n: (0, 0))
    return pl.pallas_call(
        body, name=name, grid=(t // ROWS,),
        in_specs=[pl.BlockSpec((ROWS, 128), lambda n: (n, COL_GATE // 128)), vec, vec],
        out_specs=pl.BlockSpec((ROWS, 128), lambda n: (n, 0)),
        out_shape=jax.ShapeDtypeStruct((t, 128), F32), compiler_params=_params(("parallel",)))(
            proj, alog_pad, dtb_pad)


def gates_bwd(proj, alog_pad, dtb_pad, dgates, *, name):
    t = proj.shape[0]

    def body(x_ref, a_ref, b_ref, dg_ref, dx_ref, da_ref, db_ref):
        @pl.when(pl.program_id(0) == 0)
        def _():
            da_ref[...] = jnp.zeros_like(da_ref)
            db_ref[...] = jnp.zeros_like(db_ref)

        raw, dgt = x_ref[...], dg_ref[...]
        lane = lax.broadcasted_iota(jnp.int32, raw.shape, 1)
        is_beta, is_g = lane < B_HEADS, (lane >= B_HEADS) & (lane < 2 * B_HEADS)
        beta = _sigmoid(raw)
        z = raw + b_ref[...]
        neg_a = -jnp.exp(a_ref[...])
        d_z = jnp.where(is_g, dgt * neg_a * _sigmoid(z), 0.0)
        dx_ref[...] = jnp.where(is_beta, dgt * beta * (1.0 - beta), d_z).astype(BF16)
        db_ref[...] += jnp.sum(d_z, axis=0, keepdims=True)
        da_ref[...] += jnp.sum(jnp.where(is_g, dgt * neg_a * _softplus(z), 0.0), axis=0, keepdims=True)

    vec = pl.BlockSpec((1, 128), lambda n: (0, 0))
    row = pl.BlockSpec((ROWS, 128), lambda n: (n, 0))
    return pl.pallas_call(
        body, name=name, grid=(t // ROWS,),
        in_specs=[pl.BlockSpec((ROWS, 128), lambda n: (n, COL_GATE // 128)), vec, vec, row],
        out_specs=[row, vec, vec],
        out_shape=[jax.ShapeDtypeStruct((t, 128), BF16), jax.ShapeDtypeStruct((1, 128), F32),
                   jax.ShapeDtypeStruct((1, 128), F32)],
        compiler_params=_params(("arbitrary",)))(proj, alog_pad, dtb_pad, dgates)


def _split2(a):
    hi = a.astype(BF16)
    return hi, (a - hi.astype(F32)).astype(BF16)


def _dotp(a, b, dims, passes):
    if passes == 1:
        return _dot(a.astype(BF16), b.astype(BF16), dims)
    ah, al = _split2(a)
    bh, bl = _split2(b)
    return _dot(ah, bh, dims) + (_dot(ah, bl, dims) + _dot(al, bh, dims))


_GRAD_DIMS = {NN: ((NT, False), (TN, False)), NT: ((NN, False), (TN, True)), TN: ((NT, True), (NN, False))}


def _make_mm(dims, passes, grad_passes):
    (da_dims, da_swap), (db_dims, db_swap) = _GRAD_DIMS[dims]

    @jax.custom_vjp
    def mm(a, b):
        return _dotp(a, b, dims, passes)

    def fwd(a, b):
        return _dotp(a, b, dims, passes), (a, b)

    def bwd(saved, ct):
        a, b = saved
        da = _dotp(b, ct, da_dims, grad_passes) if da_swap else _dotp(ct, b, da_dims, grad_passes)
        db = _dotp(ct, a, db_dims, grad_passes) if db_swap else _dotp(a, ct, db_dims, grad_passes)
        return da, db

    mm.defvjp(fwd, bwd)
    return mm


MM1 = {d: _make_mm(d, 1, 1) for d in (NN, NT, TN)}
MM3 = {d: _make_mm(d, 3, 1) for d in (NN, NT, TN)}


def _tri_ones(lower):
    r = lax.broadcasted_iota(jnp.int32, (DN_CHUNK, DN_CHUNK), 0)
    c = lax.broadcasted_iota(jnp.int32, (DN_CHUNK, DN_CHUNK), 1)
    return (r >= c if lower else r <= c).astype(BF16)


def _tri_sum(x, lower):
    tri = _tri_ones(lower)
    hi = x.astype(BF16)
    r1 = x - hi.astype(F32)
    mid = r1.astype(BF16)
    lo = (r1 - mid.astype(F32)).astype(BF16)
    return _dot(tri, hi, NN) + (_dot(tri, mid, NN) + _dot(tri, lo, NN))


def _delta_chunk(s0, q, k, v, beta, gam_c, gam_r):
    c = DN_CHUNK
    r = lax.broadcasted_iota(jnp.int32, (c, c), 0)
    cc = lax.broadcasted_iota(jnp.int32, (c, c), 1)
    incl, strict = r >= cc, r > cc
    eye = (r == cc).astype(F32)
    decay = jnp.exp(jnp.where(incl, gam_c - gam_r, NEG_INF))
    g_last = gam_c[:, c - 1:c, :]
    e_gam, e_rest, e_last = jnp.exp(gam_c), jnp.exp(g_last - gam_c), jnp.exp(g_last)
    a_neg = -jnp.where(strict, beta * MM1[NT](k, k) * decay, 0.0)
    inv = eye + a_neg
    pw = a_neg
    for _ in range(5):
        pw = MM3[NN](pw, pw)
        inv = inv + MM3[NN](inv, pw)
    uw = MM3[NN](inv, jnp.concatenate([v * beta, k * (beta * e_gam)], axis=-1))
    u, w = uw[..., :B_HEAD_DIM], uw[..., B_HEAD_DIM:]
    qk = MM1[NT](q, k) * decay
    v_new = u - MM1[NN](w, s0)
    o = MM1[NN](q * e_gam, s0) + MM1[NN](qk, v_new)
    s1 = s0 * e_last + MM1[TN](k * e_rest, v_new)
    return s1, o


def _delta_operands(q_ref, k_ref, v_ref, gt):
    heads = lambda ref: jnp.stack([ref[:, h * B_HEAD_DIM:(h + 1) * B_HEAD_DIM] for h in range(B_HEADS)])
    gam = _tri_sum(gt, True)
    gam_t = gam.T
    beta = jnp.stack([gt[:, h:h + 1] for h in range(B_HEADS)])
    gam_c = jnp.stack([gam[:, B_HEADS + h:B_HEADS + h + 1] for h in range(B_HEADS)])
    gam_r = jnp.stack([gam_t[B_HEADS + h:B_HEADS + h + 1, :] for h in range(B_HEADS)])
    return heads(q_ref), heads(k_ref), heads(v_ref), beta, gam_c, gam_r


def delta_fwd(qkvn, gates, *, name):
    t = qkvn.shape[0]
    nc = t // DN_CHUNK

    def body(q_ref, k_ref, v_ref, g_ref, o_ref, ss_ref, state):
        @pl.when(pl.program_id(0) == 0)
        def _():
            state[...] = jnp.zeros_like(state)

        s0 = state[...]
        ss_ref[...] = s0
        s1, o = _delta_chunk(s0, *_delta_operands(q_ref, k_ref, v_ref, g_ref[...]))
        state[...] = s1
        for h in range(B_HEADS):
            o_ref[:, h * B_HEAD_DIM:(h + 1) * B_HEAD_DIM] = o[h]

    blk = lambda j: pl.BlockSpec((DN_CHUNK, B_W), lambda n: (n, j))
    return pl.pallas_call(
        body, name=name, grid=(nc,),
        in_specs=[blk(0), blk(1), blk(2), pl.BlockSpec((DN_CHUNK, 128), lambda n: (n, 0))],
        out_specs=[blk(0), pl.BlockSpec((None, B_HEADS, B_HEAD_DIM, B_HEAD_DIM), lambda n: (n, 0, 0, 0))],
        out_shape=[jax.ShapeDtypeStruct((t, B_W), F32),
                   jax.ShapeDtypeStruct((nc, B_HEADS, B_HEAD_DIM, B_HEAD_DIM), F32)],
        scratch_shapes=[pltpu.VMEM((B_HEADS, B_HEAD_DIM, B_HEAD_DIM), F32)],
        compiler_params=_params(("arbitrary",)))(qkvn, qkvn, qkvn, gates)


def delta_bwd(qkvn, gates, ssave, do, *, name):
    t = qkvn.shape[0]
    nc = t // DN_CHUNK

    def body(q_ref, k_ref, v_ref, g_ref, ss_ref, do_ref, dx_ref, dg_ref, dstate):
        @pl.when(pl.program_id(0) == 0)
        def _():
            dstate[...] = jnp.zeros_like(dstate)

        lane = lax.broadcasted_iota(jnp.int32, (DN_CHUNK, 128), 1)
        row = lax.broadcasted_iota(jnp.int32, (128, DN_CHUNK), 0)
        dbeta_all = jnp.zeros((DN_CHUNK, 128), F32)
        dgam_c_all = jnp.zeros((DN_CHUNK, 128), F32)
        dgam_r_all = jnp.zeros((128, DN_CHUNK), F32)
        _, vjp = jax.vjp(_delta_chunk, ss_ref[...], *_delta_operands(q_ref, k_ref, v_ref, g_ref[...]))
        do = jnp.stack([do_ref[:, h * B_HEAD_DIM:(h + 1) * B_HEAD_DIM] for h in range(B_HEADS)])
        ds0, dq, dk, dv, dbeta, dgam_c, dgam_r = vjp((dstate[...], do))
        dstate[...] = ds0
        for h in range(B_HEADS):
            dx_ref[:, h * B_HEAD_DIM:(h + 1) * B_HEAD_DIM] = dq[h]
            dx_ref[:, B_W + h * B_HEAD_DIM:B_W + (h + 1) * B_HEAD_DIM] = dk[h]
            dx_ref[:, 2 * B_W + h * B_HEAD_DIM:2 * B_W + (h + 1) * B_HEAD_DIM] = dv[h]
            dbeta_all = dbeta_all + jnp.where(lane == h, dbeta[h], 0.0)
            dgam_c_all = dgam_c_all + jnp.where(lane == B_HEADS + h, dgam_c[h], 0.0)
            dgam_r_all = dgam_r_all + jnp.where(row == B_HEADS + h, dgam_r[h], 0.0)
        dg_ref[...] = dbeta_all + _tri_sum(dgam_c_all + dgam_r_all.T, False)

    blk = lambda j: pl.BlockSpec((DN_CHUNK, B_W), lambda n: (nc - 1 - n, j))
    gsp = pl.BlockSpec((DN_CHUNK, 128), lambda n: (nc - 1 - n, 0))
    return pl.pallas_call(
        body, name=name, grid=(nc,),
        in_specs=[blk(0), blk(1), blk(2), gsp,
                  pl.BlockSpec((None, B_HEADS, B_HEAD_DIM, B_HEAD_DIM), lambda n: (nc - 1 - n, 0, 0, 0)), blk(0)],
        out_specs=[pl.BlockSpec((DN_CHUNK, 3 * B_W), lambda n: (nc - 1 - n, 0)), gsp],
        out_shape=[jax.ShapeDtypeStruct((t, 3 * B_W), F32), jax.ShapeDtypeStruct((t, 128), F32)],
        scratch_shapes=[pltpu.VMEM((B_HEADS, B_HEAD_DIM, B_HEAD_DIM), F32)],
        compiler_params=_params(("arbitrary",)))(qkvn, qkvn, qkvn, gates, ssave, do)


GNORM_ROWS = 1024


def gnorm_fwd(o, proj, onorm, *, name):
    t = o.shape[0]

    def body(o_ref, z_ref, w_ref, out_ref):
        ov = o_ref[...]
        r = lax.rsqrt(jnp.mean(ov * ov, axis=-1, keepdims=True) + EPS)
        out_ref[...] = (ov * r * w_ref[...] * _silu(z_ref[...])).astype(BF16)

    rows = min(GNORM_ROWS, t)
    blk = pl.BlockSpec((rows, B_HEAD_DIM), lambda n, h: (n, h))
    return pl.pallas_call(
        body, name=name, grid=(t // rows, B_HEADS),
        in_specs=[blk, pl.BlockSpec((rows, B_HEAD_DIM), lambda n, h: (n, COL_Z // B_HEAD_DIM + h)),
                  pl.BlockSpec((1, B_HEAD_DIM), lambda n, h: (0, 0))],
        out_specs=blk, out_shape=jax.ShapeDtypeStruct((t, B_W), BF16),
        compiler_params=_params(("parallel", "parallel")))(o, proj, onorm)


def gnorm_bwd(o, proj, onorm, dout, *, dcol0, name):
    t = o.shape[0]

    def body(o_ref, z_ref, w_ref, d_ref, do_ref, dz_ref, dw_ref):
        @pl.when((pl.program_id(0) == 0) & (pl.program_id(1) == 0))
        def _():
            dw_ref[...] = jnp.zeros_like(dw_ref)

        ov, zv, wv, dv = o_ref[...], z_ref[...], w_ref[...], d_ref[...].astype(F32)
        r = lax.rsqrt(jnp.mean(ov * ov, axis=-1, keepdims=True) + EPS)
        nrm = ov * r
        dz_ref[...] = (dv * nrm * wv * _dsilu(zv)).astype(BF16)
        da = dv * _silu(zv)
        dw_ref[...] += jnp.sum(da * nrm, axis=0, keepdims=True)
        dn = da * wv
        do_ref[...] = r * dn - ov * (r * r * r) * jnp.mean(dn * ov, axis=-1, keepdims=True)

    rows = min(GNORM_ROWS, t)
    blk = pl.BlockSpec((rows, B_HEAD_DIM), lambda n, h: (n, h))
    vec = pl.BlockSpec((1, B_HEAD_DIM), lambda n, h: (0, 0))
    return pl.pallas_call(
        body, name=name, grid=(t // rows, B_HEADS),
        in_specs=[blk, pl.BlockSpec((rows, B_HEAD_DIM), lambda n, h: (n, COL_Z // B_HEAD_DIM + h)), vec,
                  pl.BlockSpec((rows, B_HEAD_DIM), lambda n, h: (n, dcol0 // B_HEAD_DIM + h))],
        out_specs=[blk, blk, vec],
        out_shape=[jax.ShapeDtypeStruct((t, B_W), F32), jax.ShapeDtypeStruct((t, B_W), BF16),
                   jax.ShapeDtypeStruct((1, B_HEAD_DIM), F32)],
        compiler_params=_params(("arbitrary", "arbitrary")))(o, proj, onorm, dout)


def _ffn_fwd(h, norm_g, wg, wu, wd, tm, tag):
    hn = rms_fwd(h, norm_g, name=f"ffn{tag}_norm")
    gate, up, act = mm_gate_up(hn, wg, wu, tm=min(512, tm), tn=1408, tk=2048, name=f"ffn{tag}_gate_up")
    h_out = mm_nn(act, wd, tm=tm, tn=2048, tk=512, out_dtype=F32, res=h, name=f"ffn{tag}_down")
    return h_out, (hn, gate, up, act)


def _ffn_bwd(dh, h, norm_g, wg, wu, wd, saved, tm, tag, emit):
    hn, gate, up, act = saved
    dwd = mm_tn(act, dh, shards=1, tm=tm, tn=1024, tk=1408, out_dtype=BF16, name=f"ffn{tag}_dwd")[0]
    dgate, dup = mm_down_bwd(dh, wd, gate, up, tm=tm, tn=512, tk=2048, name=f"ffn{tag}_dact")
    dwg = mm_tn(hn, dgate, shards=N_SHARD, tm=tm, tn=1408, tk=1024, out_dtype=BF16, name=f"ffn{tag}_dwg")
    dwu = mm_tn(hn, dup, shards=N_SHARD, tm=tm, tn=1408, tk=1024, out_dtype=BF16, name=f"ffn{tag}_dwu")
    started = emit(f"ffn{tag}", {"gate": dwg, "up": dwu, "down": dwd})
    dhn = mm_nt(dgate, wg, tm=tm, tn=1024, tk=1408, out_dtype=F32, name=f"ffn{tag}_dhn_g")
    dh_in, dnorm = dgrad_rms_bwd(dup, wu, NT, h, norm_g + started, dh, tm=min(512, tm), tk=1408, res=dhn,
                                 name=f"ffn{tag}_dhn_u_dnorm")
    return dh_in, dnorm


def _local_step(x, target, w, get, emit):
    t = x.shape[0]
    tm = min(1024, t)
    g = {}

    hn0 = rms_fwd(x, w["even_norm"], name="l0_norm")
    w.update(get("even_in", hn0))
    proj = mm_nt(hn0, w["even_w_in"], tm=tm, tn=512, tk=2048, out_dtype=F32, name="l0_w_in")
    out_a = att_fwd(proj, w["sinks"], name="l0_att")
    qkvn = dprep_fwd(proj, w["even_conv"], name="l0_prep")
    gates = gates_fwd(proj, w["a_log"], w["dt_bias"], name="l0_gates")
    o_delta, ssave = delta_fwd(qkvn, gates, name="l0_delta")
    w.update(get("even_out", o_delta))
    out_b = gnorm_fwd(o_delta, proj, w["onorm"], name="l0_gnorm")
    mix0 = jnp.concatenate([out_a, out_b], axis=-1)
    h1 =mm_nn(mix0, w["even_w_out"], tm=tm, tn=512, tk=2048, out_dtype=F32, res=x, name="l0_w_out")
    f0 = get("ffn0", h1)
    h2, ffn0 = _ffn_fwd(h1, w["ffn_norm"][0:1] + f0["tok"], f0["gate"], f0["up"], f0["down"], tm, 0)
    hn2 = rms_fwd(h2, w["odd_norm"], name="l1_norm")
    w.update(get("odd", hn2))
    zpre = mm_nn(hn2, w["odd_w_in"], tm=tm, tn=1024, tk=2048, out_dtype=F32, name="l1_w_in")
    gated = gmlp_fwd(zpre, w["odd_ln_g"], w["odd_ln_b"], w["odd_w_s"], w["odd_b_s"], name="l1_gmlp")
    h3 = mm_nn(gated, w["odd_w_out"], tm=tm, tn=512, tk=2048, out_dtype=F32, res=h2, name="l1_w_out")
    f1 = get("ffn1", h3)
    h4, ffn1 = _ffn_fwd(h3, w["ffn_norm"][1:2] + f1["tok"], f1["gate"], f1["up"], f1["down"], tm, 1)
    loss, dh4, g["final_norm"] = loss_head(h4, w["final_norm"], target, name="loss_head")

    dh3, dn1 = _ffn_bwd(dh4, h3, w["ffn_norm"][1:2], f1["gate"], f1["up"], f1["down"], ffn1, tm, 1, emit)
    dw_out_o = mm_tn(gated, dh3, shards=1, tm=tm, tn=1024, tk=1024, out_dtype=BF16, name="l1_dw_out")[0]
    dgated = mm_nt(dh3, w["odd_w_out"], tm=tm, tn=512, tk=2048, out_dtype=BF16, name="l1_dgated")
    dzpre, g["odd_w_s"], g["odd_b_s"], g["odd_ln_g"], g["odd_ln_b"] = gmlp_bwd(
        zpre, dgated, w["odd_ln_g"], w["odd_ln_b"], w["odd_w_s"], w["odd_b_s"], name="l1_dgmlp")
    dw_in_o = mm_tn(hn2, dzpre, shards=N_SHARD, tm=tm, tn=1024, tk=1024, out_dtype=BF16, name="l1_dw_in")
    started = emit("odd", {"odd_w_in": dw_in_o, "odd_w_out": dw_out_o})
    dh2, g["odd_norm"] = dgrad_rms_bwd(dzpre, w["odd_w_in"], NT, h2, w["odd_norm"] + started, dh3, tm=min(512, tm),
                                       tk=1024, name="l1_dhn_dnorm")
    dh1, dn0 = _ffn_bwd(dh2, h1, w["ffn_norm"][0:1], f0["gate"], f0["up"], f0["down"], ffn0, tm, 0, emit)
    g["ffn_norm"] = jnp.concatenate([dn0, dn1], axis=0)
    dw_out_e = mm_tn(mix0, dh1, shards=1, tm=tm, tn=1024, tk=1024, out_dtype=BF16, name="l0_dw_out")[0]
    started = emit("even_out", {"even_w_out": dw_out_e})
    dmix = mm_nt(dh1, w["even_w_out"], tm=tm, tn=512, tk=2048, out_dtype=F32, name="l0_dmix")
    dq_a, dkv_cur, dkv_prev, g["sinks"] = att_bwd(proj, w["sinks"] + started, dmix, name="l0_datt")
    dkv = dkv_cur + jnp.concatenate([dkv_prev[WINDOW:], jnp.zeros((WINDOW, 2 * A_KV), F32)], axis=0)
    do_delta, dz, g["onorm"] = gnorm_bwd(o_delta, proj, w["onorm"], dmix, dcol0=A_Q, name="l0_dgnorm")
    dqkvn, dgates = delta_bwd(qkvn, gates, ssave, do_delta, name="l0_ddelta")
    dqkv_b, g["even_conv"] = dprep_bwd(proj, w["even_conv"], dqkvn, name="l0_dprep")
    draw, g["a_log"], g["dt_bias"] = gates_bwd(proj, w["a_log"], w["dt_bias"], dgates, name="l0_dgates")
    dproj = jnp.concatenate([dq_a, dkv.astype(BF16), dqkv_b, dz, draw,
                             jnp.zeros((t, EVEN_IN_PAD - COL_GATE - 128), BF16)], axis=-1)
    dw_in_e = mm_tn(dproj, hn0, shards=1, tm=tm, tn=1024, tk=1408, out_dtype=BF16, name="l0_dw_in")[0]
    grad_x, g["even_norm"] = dgrad_rms_bwd(dproj, w["even_w_in"], NN, x, w["even_norm"], dh1, tm=min(512, tm), tk=512,
                                           name="l0_dhn_dnorm")
    emit("even_in", {"even_w_in": dw_in_e, "small": g})
    return loss, grad_x


ANY = pl.BlockSpec(memory_space=pl.ANY)
N_DEV = 8


def _place():
    return lax.axis_index("x"), lax.axis_index("y"), lax.axis_index("c")


def _chip_peers(x, y, c):
    return [((1 - x, y, c), 2 * (1 - x) + y), ((x, 1 - y, c), 2 * x + 1 - y), ((1 - x, 1 - y, c), 2 * (1 - x) + 1 - y)]


HBM = pl.BlockSpec(memory_space=pltpu.HBM)
SEM = pl.BlockSpec(memory_space=pltpu.SEMAPHORE)
EFFECT = pltpu.SideEffectType.DATAFLOW_SIDE_EFFECTING
N_PEER = 3


def _half(ref, c):
    r, cols = ref.shape
    tile_rows = 32 // jnp.dtype(ref.dtype).itemsize
    if (r // 2) % tile_rows == 0:
        return ref.at[pl.ds(c * (r // 2), r // 2)]
    assert (cols // 2) % 128 == 0, ref.shape
    return ref.at[:, pl.ds(c * (cols // 2), cols // 2)]


def _gather_plan(srcs, lands, send, recv):
    x, y, c = _place()
    return [pltpu.make_async_remote_copy(src_ref=_half(srcs[i], c), dst_ref=_half(lands[i].at[2 * x + y], c),
                                         send_sem=send.at[N_PEER * i + k], recv_sem=recv.at[N_PEER * i + k],
                                         device_id=peer, device_id_type=MESH_ID)
            for i in range(len(srcs)) for k, (peer, _) in enumerate(_chip_peers(x, y, c))]


def _relay_plan(srcs, lands, send, recv):
    x, y, c = _place()
    return [pltpu.make_async_remote_copy(src_ref=_half(lands[i].at[idx], c), dst_ref=_half(lands[i].at[idx], c),
                                         send_sem=send.at[N_PEER * i + k], recv_sem=recv.at[N_PEER * i + k],
                                         device_id=(x, y, 1 - c), device_id_type=MESH_ID)
            for i in range(len(srcs)) for k, (_, idx) in enumerate(_chip_peers(x, y, c))]


def _scatter_plan(srcs, lands, send, recv):
    x, y, c = _place()
    return [pltpu.make_async_remote_copy(src_ref=srcs[i].at[idx], dst_ref=lands[i].at[k], send_sem=send.at[N_PEER * i + k],
                                         recv_sem=recv.at[N_PEER * i + k], device_id=peer, device_id_type=MESH_ID)
            for i in range(len(srcs)) for k, (peer, idx) in enumerate(_chip_peers(x, y, c))]


def _swap_plan(srcs, lands, send, recv):
    x, y, c = _place()
    return [pltpu.make_async_remote_copy(src_ref=srcs[i], dst_ref=lands[i], send_sem=send.at[N_PEER * i],
                                         recv_sem=recv.at[N_PEER * i], device_id=(x, y, 1 - c), device_id_type=MESH_ID)
            for i in range(len(srcs))]


def copies_start(plan, srcs, lands, after, *, name):
    n = len(srcs)
    both = list(srcs) + list(lands)

    def body(*refs):
        src_refs, land_refs = refs[:n], refs[n:2 * n]
        send, recv = refs[2 * n + 1], refs[2 * n + 2]
        for cp in plan(src_refs, land_refs, send, recv):
            cp.start()
        refs[-1][...] = jnp.zeros_like(refs[-1])

    res = pl.pallas_call(
        body, name=name,
        out_shape=(pltpu.SemaphoreType.DMA((n * N_PEER,)), pltpu.SemaphoreType.DMA((n * N_PEER,)),
                   *[pltpu.HBM(a.shape, a.dtype) for a in both], jax.ShapeDtypeStruct((8, 128), F32)),
        in_specs=[HBM] * (2 * n) + [ANY],
        out_specs=(SEM, SEM, *[HBM] * (2 * n), pl.BlockSpec(memory_space=pltpu.VMEM)),
        input_output_aliases={i: 2 + i for i in range(2 * n)},
        compiler_params=pltpu.CompilerParams(has_side_effects=EFFECT))(
            *[pltpu.with_memory_space_constraint(a, pltpu.HBM) for a in both], after)
    return {"send": res[0], "recv": res[1], "srcs": list(res[2:2 + n]), "lands": list(res[2 + n:2 + 2 * n]),
            "token": res[-1]}


def copies_relay(arrived_plan, next_plan, started, after, *, name):
    srcs, lands = started["srcs"], started["lands"]
    n = len(srcs)
    both = srcs + lands

    def body(*refs):
        src_refs, land_refs = refs[:n], refs[n:2 * n]
        send1, recv1 = refs[2 * n], refs[2 * n + 1]
        send2, recv2 = refs[2 * n + 3], refs[2 * n + 4]
        for cp in arrived_plan(src_refs, land_refs, send1, recv1):
            cp.wait_send()
            cp.wait_recv()
        for cp in next_plan(src_refs, land_refs, send2, recv2):
            cp.start()
        refs[-1][...] = jnp.zeros_like(refs[-1])

    res = pl.pallas_call(
        body, name=name,
        out_shape=(pltpu.SemaphoreType.DMA((n * N_PEER,)), pltpu.SemaphoreType.DMA((n * N_PEER,)),
                   *[pltpu.HBM(a.shape, a.dtype) for a in both], jax.ShapeDtypeStruct((8, 128), F32)),
        in_specs=[HBM] * (2 * n) + [SEM, SEM, ANY],
        out_specs=(SEM, SEM, *[HBM] * (2 * n), pl.BlockSpec(memory_space=pltpu.VMEM)),
        input_output_aliases={i: 2 + i for i in range(2 * n)},
        compiler_params=pltpu.CompilerParams(has_side_effects=EFFECT))(*both, started["send"], started["recv"], after)
    return {"send": res[0], "recv": res[1], "srcs": list(res[2:2 + n]), "lands": list(res[2 + n:2 + 2 * n]),
            "token": res[-1]}


def copies_wait(plan, started, after, *, name):
    srcs, lands = started["srcs"], started["lands"]
    n = len(srcs)
    both = srcs + lands

    def body(*refs):
        src_refs, land_refs = refs[:n], refs[n:2 * n]
        send, recv = refs[2 * n], refs[2 * n + 1]
        for cp in plan(src_refs, land_refs, send, recv):
            cp.wait_send()
            cp.wait_recv()

    res = pl.pallas_call(
        body, name=name, out_shape=tuple(pltpu.HBM(a.shape, a.dtype) for a in both),
        in_specs=[HBM] * (2 * n) + [SEM, SEM, ANY], out_specs=(HBM,) * (2 * n),
        input_output_aliases={i: i for i in range(2 * n)},
        compiler_params=pltpu.CompilerParams(has_side_effects=EFFECT))(*both, started["send"], started["recv"], after)
    return list(res[:n]), list(res[n:])


def allgather_small(small, *, name):
    def body(small_ref, out_ref, send, recv, loc):
        x, y, c = _place()
        dev = 4 * x + 2 * y + c
        local = pltpu.make_async_copy(small_ref, out_ref.at[dev], loc)
        remote = []
        for r in range(1, N_DEV):
            fx, fy, fc = (r >> 2) & 1, (r >> 1) & 1, r & 1
            peer = (1 - x if fx else x, 1 - y if fy else y, 1 - c if fc else c)
            remote.append(pltpu.make_async_remote_copy(
                src_ref=small_ref, dst_ref=out_ref.at[dev], send_sem=send.at[r - 1], recv_sem=recv.at[r - 1],
                device_id=peer, device_id_type=MESH_ID))
        local.start()
        for cp in remote:
            cp.start()
        for cp in remote:
            cp.wait()
        local.wait()

    return pl.pallas_call(
        body, name=name, in_specs=[ANY], out_specs=ANY,
        out_shape=jax.ShapeDtypeStruct((N_DEV,) + small.shape, small.dtype),
        scratch_shapes=[pltpu.SemaphoreType.DMA((N_DEV - 1,)), pltpu.SemaphoreType.DMA((N_DEV - 1,)),
                        pltpu.SemaphoreType.DMA(())])(small)


RED_ROWS = 256
RED_COLS = 256


def _red_block(r, c):
    if r % RED_ROWS == 0:
        return RED_ROWS, c
    if c > RED_COLS and c % RED_COLS == 0:
        return r, RED_COLS
    return r, c


def sum_chips(by_owner, me, got, *, name):
    _, r, c = by_owner.shape
    rb, cb = _red_block(r, c)

    def body(me_ref, o_ref, a_ref, b_ref, c_ref, out_ref):
        total = ((o_ref[...].astype(F32) + a_ref[...].astype(F32)) + b_ref[...].astype(F32)) + c_ref[...].astype(F32)
        out_ref[...] = total.astype(BF16)

    gk = lambda k: pl.BlockSpec((None, rb, cb), lambda i, j, me_ref: (k, i, j))
    grid_spec = pltpu.PrefetchScalarGridSpec(
        num_scalar_prefetch=1, grid=(r // rb, c // cb),
        in_specs=[pl.BlockSpec((None, rb, cb), lambda i, j, me_ref: (me_ref[0], i, j)), gk(0), gk(1), gk(2)],
        out_specs=pl.BlockSpec((rb, cb), lambda i, j, me_ref: (i, j)))
    return pl.pallas_call(
        body, name=name, grid_spec=grid_spec, out_shape=jax.ShapeDtypeStruct((r, c), BF16),
        compiler_params=_params(("parallel", "parallel")))(me, by_owner, got, got, got)


def sum_devices(small_all, *, name):
    _, p, c = small_all.shape

    def body(a_ref, out_ref):
        acc = a_ref[0]
        for d in range(1, N_DEV):
            acc = acc + a_ref[d]
        out_ref[...] = acc

    return pl.pallas_call(
        body, name=name, grid=(1,), in_specs=[pl.BlockSpec((N_DEV, p, c), lambda i: (0, 0, 0))],
        out_specs=pl.BlockSpec((p, c), lambda i: (0, 0)), out_shape=jax.ShapeDtypeStruct((p, c), F32),
        compiler_params=_params(("arbitrary",)))(small_all)


def adamw(parts, w, m, v, *, name):
    nl, r, c = w.shape
    assert len(parts) == nl
    npart = len(parts[0])
    rb, cb = _red_block(r, c)
    flat = [a for layer in parts for a in layer]

    def body(*refs):
        p_refs, (w_ref, m_ref, v_ref) = refs[:nl * npart], refs[nl * npart:nl * npart + 3]
        g_ref, d_ref, nm_ref, nv_ref = refs[nl * npart + 3:]
        layer = pl.program_id(0)
        grad = None
        for l in range(nl):
            gl = p_refs[l * npart][...].astype(F32)
            for j in range(1, npart):
                gl = gl + p_refs[l * npart + j][...].astype(F32)
            grad = gl if grad is None else jnp.where(layer == l, gl, grad)
        wv, mv, vv = w_ref[...], m_ref[...], v_ref[...]
        nm = ADAM_B1 * mv + (1.0 - ADAM_B1) * grad
        nv = ADAM_B2 * vv + (1.0 - ADAM_B2) * (grad * grad)
        m_hat = nm / (1.0 - ADAM_B1 ** ADAM_STEP)
        v_hat = nv / (1.0 - ADAM_B2 ** ADAM_STEP)
        g_ref[...] = grad
        d_ref[...] = -ADAM_LR * (m_hat / (jnp.sqrt(v_hat) + ADAM_EPS) + ADAM_WD * wv)
        nm_ref[...] = nm
        nv_ref[...] = nv

    pspec = pl.BlockSpec((rb, cb), lambda l, i, j: (i, j))
    wspec = pl.BlockSpec((None, rb, cb), lambda l, i, j: (l, i, j))
    osh = jax.ShapeDtypeStruct((nl, r, c), F32)
    return pl.pallas_call(
        body, name=name, grid=(nl, r // rb, c // cb), in_specs=[pspec] * (nl * npart) + [wspec] * 3,
        out_specs=[wspec] * 4, out_shape=[osh] * 4,
        compiler_params=_params(("parallel", "parallel", "parallel")))(*flat, w, m, v)


def _rows128(a):
    flat = a.reshape(-1)
    pad = (-flat.shape[0]) % 128
    return jnp.pad(flat, (0, pad)).reshape(-1, 128)


def _pack_rows(arrs, multiple=8):
    rows = jnp.concatenate([_rows128(a.astype(F32)) for a in arrs], axis=0)
    return jnp.pad(rows, ((0, (-rows.shape[0]) % multiple), (0, 0)))


def _unpack_rows(rows, shapes):
    out, r0 = [], 0
    for shp in shapes:
        size = 1
        for s in shp:
            size *= s
        nr = -(-size // 128)
        out.append(rows[r0:r0 + nr].reshape(-1)[:size].reshape(shp))
        r0 += nr
    return out


SMALL_LOCAL_GRADS = ["even_norm", "even_conv", "a_log", "dt_bias", "sinks", "onorm", "odd_norm", "odd_ln_g",
                     "odd_ln_b", "odd_w_s", "odd_b_s", "ffn_norm", "final_norm"]
BIG = ["even_w_in", "even_w_out", "odd_w_in", "odd_w_out", "ffn_w_gate", "ffn_w_up", "ffn_w_down"]
WEIGHTS = ["even_norm", "even_w_in", "even_conv", "even_a_log", "even_dt_bias", "even_sinks", "even_onorm",
           "even_w_out", "odd_norm", "odd_w_in", "odd_ln_g", "odd_ln_b", "odd_w_s", "odd_b_s", "odd_w_out",
           "ffn_norm", "ffn_w_gate", "ffn_w_up", "ffn_w_down", "final_norm"]
SMALL = [n for n in WEIGHTS if n not in BIG]


def kernel(x, even_norm, even_w_in, even_conv, even_a_log, even_dt_bias, even_sinks, even_onorm, even_w_out, odd_norm, odd_w_in, odd_ln_g, odd_ln_b, odd_w_s, odd_b_s, odd_w_out, ffn_norm, ffn_w_gate, ffn_w_up, ffn_w_down, final_norm, loss_target, m_even_norm, m_even_w_in, m_even_conv, m_even_a_log, m_even_dt_bias, m_even_sinks, m_even_onorm, m_even_w_out, m_odd_norm, m_odd_w_in, m_odd_ln_g, m_odd_ln_b, m_odd_w_s, m_odd_b_s, m_odd_w_out, m_ffn_norm, m_ffn_w_gate, m_ffn_w_up, m_ffn_w_down, m_final_norm, v_even_norm, v_even_w_in, v_even_conv, v_even_a_log, v_even_dt_bias, v_even_sinks, v_even_onorm, v_even_w_out, v_odd_norm, v_odd_w_in, v_odd_ln_g, v_odd_ln_b, v_odd_w_s, v_odd_b_s, v_odd_w_out, v_ffn_norm, v_ffn_w_gate, v_ffn_w_up, v_ffn_w_down, v_final_norm):
    args = dict(locals())
    wl = {n: args[n] for n in WEIGHTS}
    ml = {n: args["m_" + n] for n in WEIGHTS}
    vl = {n: args["v_" + n] for n in WEIGHTS}
    me = 2 * lax.axis_index("x") + lax.axis_index("y")

    def landing(a):
        return lax.dynamic_update_index_in_dim(lax.empty((N_SHARD,) + a.shape, a.dtype), a, me, 0)

    b16 = lambda *arrs: [a.astype(BF16) for a in arrs]
    gather_groups = {
        "even_in": b16(even_w_in[0].T) + [_pack_rows([even_conv[0], odd_norm, odd_ln_g, odd_ln_b], multiple=16)],
        "even_out": b16(even_w_out[0]),
        "ffn0": b16(ffn_w_gate[0], ffn_w_up[0], ffn_w_down[0]),
        "odd": b16(odd_w_in[0], odd_w_out[0]),
        "ffn1": b16(ffn_w_gate[1], ffn_w_up[1], ffn_w_down[1]),
    }
    gathering, after = {}, even_norm
    for group, srcs in gather_groups.items():
        gathering[group] = copies_start(_gather_plan, srcs, [landing(a) for a in srcs], after,
                                        name=f"gather_{group}_start")
        after = gathering[group]["token"]

    order = list(gather_groups)
    relayed, kept = {}, {}
    sinks_pad = jnp.pad(even_sinks, ((0, 0), (0, 128 - A_HEADS)))

    def relay(group, behind):
        relayed[group] = copies_relay(_gather_plan, _relay_plan, gathering[group], behind,
                                      name=f"gather_{group}_relay")
        return relayed[group]["token"][0:1, 0:1]

    def get(group, behind):
        if group not in relayed:
            relay(group, behind)
        _, lands = copies_wait(_relay_plan, relayed[group], behind, name=f"gather_{group}_wait")
        nxt = order.index(group) + 1
        tok = relay(order[nxt], lands[0]) if nxt < len(order) else jnp.zeros((1, 1), F32)
        if group == "even_in":
            parts = zip(*[_unpack_rows(lands[1][s], [(CONV_K, 768), (1, 512), (1, 512), (1, 512)])
                          for s in range(N_SHARD)])
            conv, onorm, lng, lnb = [jnp.concatenate(p, axis=1) for p in parts]
            w_in = jnp.pad(lands[0].reshape(EVEN_IN, D_MODEL), ((0, EVEN_IN_PAD - EVEN_IN), (0, 0)))
            kept["odd_ln_g"] = lng
            return {"even_w_in": w_in, "even_conv": conv + tok, "odd_norm": onorm, "odd_ln_b": lnb}
        if group == "even_out":
            return {"even_w_out": lands[0].reshape(D_MODEL, D_MODEL), "onorm": even_onorm + tok}
        if group == "odd":
            return {"odd_w_in": lands[0], "odd_w_out": lands[1].reshape(D_MODEL, D_MODEL),
                    "odd_ln_g": kept["odd_ln_g"] + tok}
        return {"gate": lands[0], "up": lands[1], "down": lands[2].reshape(D_FF, D_MODEL), "tok": tok}

    rows4 =lambda a: a.reshape(N_SHARD, a.shape[0] // N_SHARD, a.shape[1])
    scattering, small = {}, {}

    def emit(group, grads):
        behind = even_norm
        if group == "even_in":
            small["local"] = grads["small"]
            small["all"] = behind = allgather_small(_pack_rows([grads["small"][n] for n in SMALL_LOCAL_GRADS]),
                                                    name="allgather_small")
            srcs = [grads["even_w_in"][:EVEN_IN].reshape(N_SHARD, EVEN_IN // N_SHARD, D_MODEL)]
        elif group == "even_out":
            srcs = [rows4(grads["even_w_out"])]
        elif group == "odd":
            srcs = [grads["odd_w_in"], rows4(grads["odd_w_out"])]
        else:
            srcs = [grads["gate"], grads["up"], rows4(grads["down"])]
        lands = [lax.empty((N_PEER,) + a.shape[1:], a.dtype) for a in srcs]
        scattering[group] = copies_start(_scatter_plan, srcs, lands, behind, name=f"scatter_{group}_start")
        return scattering[group]["token"][0:1, 0:1]

    pad816 = lambda a: jnp.pad(a, ((0, 0), (B_HEADS, 128 - 2 * B_HEADS)))
    w = {
        "even_norm": even_norm + after[0:1, 0:1],
        "a_log": pad816(even_a_log), "dt_bias": pad816(even_dt_bias),
        "sinks": sinks_pad,
        "onorm": even_onorm,
        "odd_w_s": odd_w_s[0],
        "odd_b_s": jnp.pad(odd_b_s[0].T, ((0, 0), (0, 128 - C_GROUPS))),
        "ffn_norm": ffn_norm,
        "final_norm": final_norm[None],
    }
    loss_l, grad_x = _local_step(x[0], loss_target[0], w, get, emit)
    loss = lax.psum(loss_l[0, 0], ("x", "y", "c"))

    me1 = me.reshape(1).astype(jnp.int32)
    swapping = {}

    def reduce_chips(group, behind):
        srcs, lands = copies_wait(_scatter_plan, scattering[group], behind, name=f"scatter_{group}_wait")
        partial = [sum_chips(srcs[i], me1, lands[i], name=f"sum_chips_{group}_{i}") for i in range(len(srcs))]
        swapping[group] = copies_start(_swap_plan, partial, [lax.empty(p.shape, p.dtype) for p in partial],
                                       even_norm, name=f"swap_{group}_start")
        return swapping[group]["token"]

    def swapped(group, behind):
        mine, theirs = copies_wait(_swap_plan, swapping[group], behind, name=f"swap_{group}_wait")
        return list(zip(mine, theirs))

    behind = scattering["even_in"]["token"]
    for group in ("ffn1", "ffn0", "odd", "even_out"):
        behind = reduce_chips(group, behind)
    sums = {group: swapped(group, behind) for group in ("ffn1", "ffn0", "odd", "even_out")}
    outs = {}
    parts_of = {"even_w_out": [sums["even_out"][0]], "odd_w_in": [sums["odd"][0]], "odd_w_out": [sums["odd"][1]],
                "ffn_w_gate": [sums["ffn0"][0], sums["ffn1"][0]], "ffn_w_up": [sums["ffn0"][1], sums["ffn1"][1]],
                "ffn_w_down": [sums["ffn0"][2], sums["ffn1"][2]]}
    for n in parts_of:
        outs[n] = adamw(parts_of[n], wl[n], ml[n], vl[n], name=f"adamw_{n}")
    behind = reduce_chips("even_in", outs["ffn_w_down"][1])
    flip = lambda a: jnp.transpose(a, (0, 2, 1))
    outs["even_w_in"] = [flip(o) for o in adamw([swapped("even_in", behind)[0]], flip(wl["even_w_in"]),
                                                flip(ml["even_w_in"]), flip(vl["even_w_in"]),
                                                name="adamw_even_w_in")]

    g = small["local"]
    small_sum = sum_devices(small["all"], name="sum_devices")
    sg = dict(zip(SMALL_LOCAL_GRADS, _unpack_rows(small_sum, [g[n].shape for n in SMALL_LOCAL_GRADS])))
    own_cols = lambda a, width: lax.dynamic_slice_in_dim(a, me * width, width, axis=a.ndim - 1)
    small_grads = {
        "even_norm": sg["even_norm"], "even_conv": own_cols(sg["even_conv"], 768)[None],
        "even_a_log": sg["a_log"][:, B_HEADS:2 * B_HEADS], "even_dt_bias": sg["dt_bias"][:, B_HEADS:2 * B_HEADS],
        "even_sinks": sg["sinks"][:, :A_HEADS], "even_onorm": sg["onorm"],
        "odd_norm": own_cols(sg["odd_norm"], 512), "odd_ln_g": own_cols(sg["odd_ln_g"], 512),
        "odd_ln_b": own_cols(sg["odd_ln_b"], 512), "odd_w_s": sg["odd_w_s"][None],
        "odd_b_s": sg["odd_b_s"][:, :C_GROUPS].T[None], "ffn_norm": sg["ffn_norm"], "final_norm": sg["final_norm"][0],
    }
    packed = [_pack_rows([d[n] for n in SMALL])[None] for d in (small_grads, wl, ml, vl)]
    small_out = adamw([(packed[0][0],)], packed[1], packed[2], packed[3], name="adamw_small")
    shapes = [wl[n].shape for n in SMALL]
    for j in range(4):
        for n, a in zip(SMALL, _unpack_rows(small_out[j][0], shapes)):
            outs.setdefault(n, [None] * 4)[j] = a

    return (loss, grad_x[None], *[outs[n][0] for n in WEIGHTS], *[outs[n][1] for n in WEIGHTS],
            *[outs[n][2] for n in WEIGHTS], *[outs[n][3] for n in WEIGHTS])
```

```python
import functools

import jax
import jax.numpy as jnp
from jax import lax
from jax.experimental import pallas as pl
from jax.experimental.pallas import tpu as pltpu

F32 = jnp.float32
BF16 = jnp.bfloat16
NEG_INF = float("-inf")

D_MODEL = 2048
A_HEADS, A_KV_HEADS, A_HEAD_DIM, WINDOW = 16, 2, 64, 128
B_HEADS, B_HEAD_DIM, CONV_K, DN_CHUNK = 8, 128, 4, 64
C_GROUPS, C_CHUNK = 8, 128
C_GROUP_DIM = D_MODEL // C_GROUPS
D_FF = 5632
EPS = 1e-6
A_Q = A_HEADS * A_HEAD_DIM
A_KV = A_KV_HEADS * A_HEAD_DIM
B_W = B_HEADS * B_HEAD_DIM
EVEN_IN = A_Q + 2 * A_KV + 4 * B_W + 2 * B_HEADS
EVEN_IN_PAD = 5632
COL_KV = A_Q
COL_QKVB = A_Q + 2 * A_KV
COL_Z = COL_QKVB + 3 * B_W
COL_GATE = COL_Z + B_W
N_SHARD = 4

ADAM_LR, ADAM_B1, ADAM_B2, ADAM_EPS, ADAM_WD, ADAM_STEP = 0.001, 0.9, 0.999, 1e-08, 0.01, 10

VMEM_LIMIT_V7X = 56 * 1024 * 1024
MXU_COLS = 256
MESH_ID = pl.DeviceIdType.MESH


def _params(sem=None):
    return pltpu.CompilerParams(dimension_semantics=sem, vmem_limit_bytes=VMEM_LIMIT_V7X)


def _sigmoid(x):
    return 1.0 / (1.0 + jnp.exp(-x))


def _silu(x):
    return x * _sigmoid(x)


def _dsilu(x):
    s = _sigmoid(x)
    return s * (1.0 + x * (1.0 - s))


def _gelu(x):
    return 0.5 * x * (1.0 + lax.erf(x * 0.7071067811865476))


def _dgelu(x):
    return 0.5 * (1.0 + lax.erf(x * 0.7071067811865476)) + x * jnp.exp(-0.5 * x * x) * 0.3989422804014327


def _dot(a, b, dims):
    if a.ndim == 3:
        (ca,), (cb,) = dims
        return lax.dot_general(a, b, (((ca + 1,), (cb + 1,)), ((0,), (0,))), preferred_element_type=F32)
    return lax.dot_general(a, b, (dims, ((), ())), preferred_element_type=F32)


NN = ((1,), (0,))
NT = ((1,), (1,))
TN = ((0,), (0,))


def _as3(b):
    return b if b.ndim == 3 else b[None]


def _accumulate(step, nsteps, accs, products, finish):
    if nsteps == 1:
        finish(products())
        return

    @pl.when(step == 0)
    def _():
        for acc, p in zip(accs, products()):
            acc[...] = p

    if nsteps > 2:
        @pl.when((step > 0) & (step < nsteps - 1))
        def _():
            for acc, p in zip(accs, products()):
                acc[...] += p

    @pl.when(step == nsteps - 1)
    def _():
        finish(tuple(acc[...] + p for acc, p in zip(accs, products())))


def mm_nn(a, b, *, tm, tn, tk, out_dtype, name, res=None):
    b3 = _as3(b)
    m, k = a.shape
    s, k2, ns = b3.shape
    assert k2 == k and m % tm == 0 and ns % tn == 0 and k % tk == 0, (a.shape, b3.shape, tm, tn, tk)
    nps, nk = ns // tn, k // tk

    def body(*refs):
        if res is None:
            a_ref, b_ref, o_ref, acc = refs
        else:
            a_ref, b_ref, r_ref, o_ref, acc = refs
        def finish(tiles):
            r = tiles[0] if res is None else tiles[0] + r_ref[...].astype(F32)
            o_ref[...] = r.astype(out_dtype)

        _accumulate(pl.program_id(2), nk, (acc,),
                    lambda: (_dot(a_ref[...].astype(BF16), b_ref[...].astype(BF16), NN),), finish)

    in_specs = [pl.BlockSpec((tm, tk), lambda i, j, kk: (i, kk)),
                pl.BlockSpec((None, tk, tn), lambda i, j, kk: (j // nps, kk, j % nps))]
    args = [a, b3]
    if res is not None:
        in_specs.append(pl.BlockSpec((tm, tn), lambda i, j, kk: (i, j)))
        args.append(res)
    return pl.pallas_call(
        body, name=name, grid=(m // tm, s * nps, nk), in_specs=in_specs,
        out_specs=pl.BlockSpec((tm, tn), lambda i, j, kk: (i, j)),
        out_shape=jax.ShapeDtypeStruct((m, s * ns), out_dtype),
        scratch_shapes=[pltpu.VMEM((tm, tn), F32)],
        compiler_params=_params(("parallel", "parallel", "arbitrary")))(*args)


def mm_nt(a, b, *, tm, tn, tk, out_dtype, name, res=None):
    b3 = _as3(b)
    m, n = a.shape
    s, k, ns = b3.shape
    assert n == s * ns and m % tm == 0 and k % tn == 0 and ns % tk == 0, (a.shape, b3.shape, tm, tn, tk)
    rps = ns // tk
    nr = s * rps

    def body(*refs):
        if res is None:
            a_ref, b_ref, o_ref, acc = refs
        else:
            a_ref, b_ref, r_ref, o_ref, acc = refs
        def finish(tiles):
            r = tiles[0] if res is None else tiles[0] + r_ref[...].astype(F32)
            o_ref[...] = r.astype(out_dtype)

        _accumulate(pl.program_id(2), nr, (acc,),
                    lambda: (_dot(a_ref[...].astype(BF16), b_ref[...].astype(BF16), NT),), finish)

    in_specs = [pl.BlockSpec((tm, tk), lambda i, j, r: (i, r)),
                pl.BlockSpec((None, tn, tk), lambda i, j, r: (r // rps, j, r % rps))]
    args = [a, b3]
    if res is not None:
        in_specs.append(pl.BlockSpec((tm, tn), lambda i, j, r: (i, j)))
        args.append(res)
    return pl.pallas_call(
        body, name=name, grid=(m // tm, k // tn, nr), in_specs=in_specs,
        out_specs=pl.BlockSpec((tm, tn), lambda i, j, r: (i, j)),
        out_shape=jax.ShapeDtypeStruct((m, k), out_dtype),
        scratch_shapes=[pltpu.VMEM((tm, tn), F32)],
        compiler_params=_params(("parallel", "parallel", "arbitrary")))(*args)


def mm_tn(a, b, *, shards, tm, tn, tk, out_dtype, name):
    m, k = a.shape
    m2, n = b.shape
    ns = n // shards
    assert m2 == m and n == shards * ns and m % tm == 0 and k % tk == 0 and ns % tn == 0, (a.shape, b.shape)
    nps, nm = ns // tn, m // tm

    def body(a_ref, b_ref, o_ref, acc):
        def finish(tiles):
            o_ref[...] = tiles[0].astype(out_dtype)

        _accumulate(pl.program_id(2), nm, (acc,),
                    lambda: (_dot(a_ref[...].astype(BF16), b_ref[...].astype(BF16), TN),), finish)

    return pl.pallas_call(
        body, name=name, grid=(k // tk, shards * nps, nm),
        in_specs=[pl.BlockSpec((tm, tk), lambda i, j, mi: (mi, i)),
                  pl.BlockSpec((tm, tn), lambda i, j, mi: (mi, j))],
        out_specs=pl.BlockSpec((None, tk, tn), lambda i, j, mi: (j // nps, i, j % nps)),
        out_shape=jax.ShapeDtypeStruct((shards, k, ns), out_dtype),
        scratch_shapes=[pltpu.VMEM((tk, tn), F32)],
        compiler_params=_params(("parallel", "parallel", "arbitrary")))(a, b)


def mm_gate_up(hn, wg, wu, *, tm, tn, tk, name):
    wg3, wu3 = _as3(wg), _as3(wu)
    m, k = hn.shape
    s, _, ns = wg3.shape
    assert m % tm == 0 and ns % tn == 0 and k % tk == 0
    nps, nk = ns // tn, k // tk

    def body(a_ref, g_ref, u_ref, og_ref, ou_ref, oa_ref, accg, accu):
        def products():
            a = a_ref[...].astype(BF16)
            return _dot(a, g_ref[...].astype(BF16), NN), _dot(a, u_ref[...].astype(BF16), NN)

        def finish(tiles):
            g, u = tiles
            og_ref[...] = g.astype(BF16)
            ou_ref[...] = u.astype(BF16)
            oa_ref[...] = (_silu(g) * u).astype(BF16)

        _accumulate(pl.program_id(2), nk, (accg, accu), products, finish)

    wspec = pl.BlockSpec((None, tk, tn), lambda i, j, kk: (j // nps, kk, j % nps))
    ospec = pl.BlockSpec((tm, tn), lambda i, j, kk: (i, j))
    osh = jax.ShapeDtypeStruct((m, s * ns), BF16)
    return pl.pallas_call(
        body, name=name, grid=(m // tm, s * nps, nk),
        in_specs=[pl.BlockSpec((tm, tk), lambda i, j, kk: (i, kk)), wspec, wspec],
        out_specs=[ospec, ospec, ospec], out_shape=[osh, osh, osh],
        scratch_shapes=[pltpu.VMEM((tm, tn) if nk > 1 else (8, 128), F32)] * 2,
        compiler_params=_params(("parallel", "parallel", "arbitrary")))(hn, wg3, wu3)


def mm_down_bwd(dh, wd, gate, up, *, tm, tn, tk, name):
    m, d = dh.shape
    f, d2 = wd.shape
    assert d2 == d and m % tm == 0 and f % tn == 0 and tk == d and tn % MXU_COLS == 0

    def body(a_ref, b_ref, g_ref, u_ref, og_ref, ou_ref):
        a = a_ref[...].astype(BF16)
        for jj in range(tn // MXU_COLS):
            sl = slice(jj * MXU_COLS, (jj + 1) * MXU_COLS)
            da = _dot(a, b_ref[sl, :].astype(BF16), NT)
            g, u = g_ref[:, sl].astype(F32), u_ref[:, sl].astype(F32)
            s = _sigmoid(g)
            og_ref[:, sl] = (da * u * (s * (1.0 + g * (1.0 - s)))).astype(BF16)
            ou_ref[:, sl] = (da * (g * s)).astype(BF16)

    ospec = pl.BlockSpec((tm, tn), lambda i, j: (i, j))
    osh = jax.ShapeDtypeStruct((m, f), BF16)
    return pl.pallas_call(
        body, name=name, grid=(m // tm, f // tn),
        in_specs=[pl.BlockSpec((tm, tk), lambda i, j: (i, 0)),
                  pl.BlockSpec((tn, tk), lambda i, j: (j, 0)), ospec, ospec],
        out_specs=[ospec, ospec], out_shape=[osh, osh],
        compiler_params=_params(("parallel", "parallel")))(dh, wd, gate, up)


ROWS = 256


def rms_fwd(x, g, *, name):
    t, d = x.shape

    def body(x_ref, g_ref, o_ref):
        xv = x_ref[...]
        r = lax.rsqrt(jnp.mean(xv * xv, axis=-1, keepdims=True) + EPS)
        o_ref[...] = (xv * r * g_ref[...]).astype(BF16)

    return pl.pallas_call(
        body, name=name, grid=(t // ROWS,),
        in_specs=[pl.BlockSpec((ROWS, d), lambda i: (i, 0)), pl.BlockSpec((1, d), lambda i: (0, 0))],
        out_specs=pl.BlockSpec((ROWS, d), lambda i: (i, 0)),
        out_shape=jax.ShapeDtypeStruct((t, d), BF16), compiler_params=_params(("parallel",)))(x, g)


def rms_bwd(x, g, dy, dres, *, name):
    t, d = x.shape

    def body(x_ref, g_ref, dy_ref, dr_ref, dx_ref, dg_ref):
        @pl.when(pl.program_id(0) == 0)
        def _():
            dg_ref[...] = jnp.zeros_like(dg_ref)

        xv, dyv = x_ref[...], dy_ref[...].astype(F32)
        r = lax.rsqrt(jnp.mean(xv * xv, axis=-1, keepdims=True) + EPS)
        dyg = dyv * g_ref[...]
        dx = r * dyg - xv * (r * r * r) * jnp.mean(dyg * xv, axis=-1, keepdims=True)
        dx_ref[...] = dx + dr_ref[...]
        dg_ref[...] += jnp.sum(dyv * xv * r, axis=0, keepdims=True)

    row = pl.BlockSpec((ROWS, d), lambda i: (i, 0))
    vec = pl.BlockSpec((1, d), lambda i: (0, 0))
    return pl.pallas_call(
        body, name=name, grid=(t // ROWS,), in_specs=[row, vec, row, row], out_specs=[row, vec],
        out_shape=[jax.ShapeDtypeStruct((t, d), F32), jax.ShapeDtypeStruct((1, d), F32)],
        compiler_params=_params(("arbitrary",)))(x, g, dy, dres)


def dgrad_rms_bwd(a, b, form, x, g, dres, *, tm, tk, name, res=None):
    m, d = x.shape
    b3 = _as3(b)
    if form == NN:
        steps = a.shape[1] // tk
        a_spec = pl.BlockSpec((tm, tk), lambda i, r: (i, r))
        b_spec = pl.BlockSpec((None, tk, d), lambda i, r: (0, r, 0))
    else:
        s, d2, ns = b3.shape
        assert d2 == d and ns % tk == 0
        rps = ns // tk
        steps = s * rps
        a_spec = pl.BlockSpec((tm, tk), lambda i, r: (i, r))
        b_spec = pl.BlockSpec((None, d, tk), lambda i, r: (r // rps, 0, r % rps))
    assert m % tm == 0 and a.shape[1] == steps * tk

    def body(*refs):
        if res is None:
            a_ref, b_ref, x_ref, g_ref, dr_ref, dx_ref, dg_ref, acc = refs
        else:
            a_ref, b_ref, r_ref, x_ref, g_ref, dr_ref, dx_ref, dg_ref, acc = refs

        @pl.when((pl.program_id(0) == 0) & (pl.program_id(1) == 0))
        def _():
            dg_ref[...] = jnp.zeros_like(dg_ref)

        def finish(tiles):
            dyv = tiles[0] if res is None else tiles[0] + r_ref[...]
            xv = x_ref[...]
            r = lax.rsqrt(jnp.mean(xv * xv, axis=-1, keepdims=True) + EPS)
            dyg = dyv * g_ref[...]
            dx_ref[...] = r * dyg - xv * (r * r * r) * jnp.mean(dyg * xv, axis=-1, keepdims=True) + dr_ref[...]
            dg_ref[...] += jnp.sum(dyv * xv * r, axis=0, keepdims=True)

        _accumulate(pl.program_id(1), steps, (acc,),
                    lambda: (_dot(a_ref[...].astype(BF16), b_ref[...].astype(BF16), form),), finish)

    row = pl.BlockSpec((tm, d), lambda i, r: (i, 0))
    vec = pl.BlockSpec((1, d), lambda i, r: (0, 0))
    in_specs = [a_spec, b_spec] + ([row] if res is not None else []) + [row, vec, row]
    args = [a, b3] + ([res] if res is not None else []) + [x, g, dres]
    return pl.pallas_call(
        body, name=name, grid=(m // tm, steps), in_specs=in_specs, out_specs=[row, vec],
        out_shape=[jax.ShapeDtypeStruct((m, d), F32), jax.ShapeDtypeStruct((1, d), F32)],
        scratch_shapes=[pltpu.VMEM((tm, d), F32)],
        compiler_params=_params(("arbitrary", "arbitrary")))(*args)


def loss_head(h, g, target, *, name):
    t, d = h.shape

    def body(x_ref, g_ref, t_ref, loss_ref, dx_ref, dg_ref):
        @pl.when(pl.program_id(0) == 0)
        def _():
            dg_ref[...] = jnp.zeros_like(dg_ref)
            loss_ref[...] = jnp.zeros_like(loss_ref)

        xv, gv = x_ref[...], g_ref[...]
        r = lax.rsqrt(jnp.mean(xv * xv, axis=-1, keepdims=True) + EPS)
        e = xv * r * gv - t_ref[...]
        loss_ref[...] += 0.5 * jnp.sum(jnp.mean(e * e, axis=-1, keepdims=True), axis=0, keepdims=True)
        dyv = e * (1.0 / d)
        dyg = dyv * gv
        dx_ref[...] = r * dyg - xv * (r * r * r) * jnp.mean(dyg * xv, axis=-1, keepdims=True)
        dg_ref[...] += jnp.sum(dyv * xv * r, axis=0, keepdims=True)

    row = pl.BlockSpec((ROWS, d), lambda i: (i, 0))
    vec = pl.BlockSpec((1, d), lambda i: (0, 0))
    return pl.pallas_call(
        body, name=name, grid=(t // ROWS,), in_specs=[row, vec, row],
        out_specs=[pl.BlockSpec((1, 128), lambda i: (0, 0)), row, vec],
        out_shape=[jax.ShapeDtypeStruct((1, 128), F32), jax.ShapeDtypeStruct((t, d), F32),
                   jax.ShapeDtypeStruct((1, d), F32)],
        compiler_params=_params(("arbitrary",)))(h, g, target)


def _tril_mask():
    r = lax.broadcasted_iota(jnp.int32, (C_CHUNK, C_CHUNK), 0)
    c = lax.broadcasted_iota(jnp.int32, (C_CHUNK, C_CHUNK), 1)
    return r >= c


def _layer_norm_parts(v):
    mu = jnp.mean(v, axis=-1, keepdims=True)
    vc = v - mu
    rstd = lax.rsqrt(jnp.mean(vc * vc, axis=-1, keepdims=True) + EPS)
    return vc * rstd, rstd


def gmlp_fwd(zpre, ln_g, ln_b, ws, bs_t, *, name):
    t = zpre.shape[0]
    d = D_MODEL

    def body(zu_ref, zv_ref, g_ref, b_ref, ws_ref, bs_ref, o_ref):
        u = _gelu(zu_ref[...])
        vhat, _ = _layer_norm_parts(_gelu(zv_ref[...]))
        vln = (vhat * g_ref[...] + b_ref[...]).astype(BF16)
        mask = _tril_mask()
        for gi in range(C_GROUPS):
            sl = slice(gi * C_GROUP_DIM, (gi + 1) * C_GROUP_DIM)
            w = jnp.where(mask, ws_ref[gi], 0.0).astype(BF16)
            mixed = _dot(w, vln[:, sl], NN) + bs_ref[:, gi:gi + 1]
            o_ref[:, sl] = (u[:, sl] * mixed).astype(BF16)

    vec = pl.BlockSpec((1, d), lambda i: (0, 0))
    return pl.pallas_call(
        body, name=name, grid=(t // C_CHUNK,),
        in_specs=[pl.BlockSpec((C_CHUNK, d), lambda i: (i, 0)), pl.BlockSpec((C_CHUNK, d), lambda i: (i, 1)),
                  vec, vec, pl.BlockSpec((C_GROUPS, C_CHUNK, C_CHUNK), lambda i: (0, 0, 0)),
                  pl.BlockSpec((C_CHUNK, 128), lambda i: (0, 0))],
        out_specs=pl.BlockSpec((C_CHUNK, d), lambda i: (i, 0)),
        out_shape=jax.ShapeDtypeStruct((t, d), BF16), compiler_params=_params(("parallel",)))(
            zpre, zpre, ln_g, ln_b, ws, bs_t)


def gmlp_bwd(zpre, dgated, ln_g, ln_b, ws, bs_t, *, name):
    t = zpre.shape[0]
    d = D_MODEL

    def body(zu_ref, zv_ref, dg_ref, g_ref, b_ref, ws_ref, bs_ref, dz_ref, dws_ref, dbs_ref, dlg_ref, dlb_ref):
        @pl.when(pl.program_id(0) == 0)
        def _():
            dws_ref[...] = jnp.zeros_like(dws_ref)
            dbs_ref[...] = jnp.zeros_like(dbs_ref)
            dlg_ref[...] = jnp.zeros_like(dlg_ref)
            dlb_ref[...] = jnp.zeros_like(dlb_ref)

        zu, zv = zu_ref[...], zv_ref[...]
        u = _gelu(zu)
        vhat, rstd = _layer_norm_parts(_gelu(zv))
        gam = g_ref[...]
        vln = (vhat * gam + b_ref[...]).astype(BF16)
        dgt = dg_ref[...].astype(F32)
        mask = _tril_mask()
        lane = lax.broadcasted_iota(jnp.int32, (C_CHUNK, 128), 1)
        dbs = jnp.zeros((C_CHUNK, 128), F32)
        du_parts, dvln_parts = [], []
        for gi in range(C_GROUPS):
            sl = slice(gi * C_GROUP_DIM, (gi + 1) * C_GROUP_DIM)
            w = jnp.where(mask, ws_ref[gi], 0.0).astype(BF16)
            mixed = _dot(w, vln[:, sl], NN) + bs_ref[:, gi:gi + 1]
            du_parts.append(dgt[:, sl] * mixed)
            dmixed = dgt[:, sl] * u[:, sl]
            dmb = dmixed.astype(BF16)
            dws_ref[gi] += jnp.where(mask, _dot(dmb, vln[:, sl], NT), 0.0)
            dbs = dbs + jnp.where(lane == gi, jnp.sum(dmixed, axis=-1, keepdims=True), 0.0)
            dvln_parts.append(_dot(w, dmb, TN))
        dbs_ref[...] += dbs
        du = jnp.concatenate(du_parts, axis=-1)
        dvln = jnp.concatenate(dvln_parts, axis=-1)
        dlg_ref[...] += jnp.sum(dvln * vhat, axis=0, keepdims=True)
        dlb_ref[...] += jnp.sum(dvln, axis=0, keepdims=True)
        dvhat = dvln * gam
        dv = rstd * (dvhat - jnp.mean(dvhat, axis=-1, keepdims=True)
                     - vhat * jnp.mean(dvhat * vhat, axis=-1, keepdims=True))
        dz_ref[:, :d] = (du * _dgelu(zu)).astype(BF16)
        dz_ref[:, d:] = (dv * _dgelu(zv)).astype(BF16)

    vec = pl.BlockSpec((1, d), lambda i: (0, 0))
    wsp = pl.BlockSpec((C_GROUPS, C_CHUNK, C_CHUNK), lambda i: (0, 0, 0))
    bsp = pl.BlockSpec((C_CHUNK, 128), lambda i: (0, 0))
    return pl.pallas_call(
        body, name=name, grid=(t // C_CHUNK,),
        in_specs=[pl.BlockSpec((C_CHUNK, d), lambda i: (i, 0)), pl.BlockSpec((C_CHUNK, d), lambda i: (i, 1)),
                  pl.BlockSpec((C_CHUNK, d), lambda i: (i, 0)), vec, vec, wsp, bsp],
        out_specs=[pl.BlockSpec((C_CHUNK, 2 * d), lambda i: (i, 0)), wsp, bsp, vec, vec],
        out_shape=[jax.ShapeDtypeStruct((t, 2 * d), BF16), jax.ShapeDtypeStruct((C_GROUPS, C_CHUNK, C_CHUNK), F32),
                   jax.ShapeDtypeStruct((C_CHUNK, 128), F32), jax.ShapeDtypeStruct((1, d), F32),
                   jax.ShapeDtypeStruct((1, d), F32)],
        compiler_params=_params(("arbitrary",)))(zpre, zpre, dgated, ln_g, ln_b, ws, bs_t)


ATT_SCALE = A_HEAD_DIM ** -0.5
PAIRS = A_HEADS // 2
PAIRS_PER_KV = PAIRS // A_KV_HEADS


def _att_padded(tile):
    lo = lax.broadcasted_iota(jnp.int32, tile.shape, 1) < A_HEAD_DIM
    rolled = pltpu.roll(tile, A_HEAD_DIM, 1)
    zero = jnp.zeros_like(tile)
    return {(0, 0): jnp.where(lo, tile, zero).astype(BF16), (0, 1): jnp.where(lo, zero, rolled).astype(BF16),
            (1, 0): jnp.where(lo, rolled, zero).astype(BF16), (1, 1): jnp.where(lo, zero, tile).astype(BF16)}


def _att_valid(n):
    r = lax.broadcasted_iota(jnp.int32, (WINDOW, 2 * WINDOW), 0)
    c = lax.broadcasted_iota(jnp.int32, (WINDOW, 2 * WINDOW), 1)
    rel = r + WINDOW - c
    return (rel >= 0) & (rel < WINDOW) & ((c >= WINDOW) | (n > 0))


def _att_probs(qp, kpad, sink, valid):
    s = jnp.where(valid, _dot(qp, kpad, NT), NEG_INF)
    m = jnp.maximum(jnp.max(s, axis=-1, keepdims=True), sink)
    p = jnp.exp(s - m)
    e_sink = jnp.exp(sink - m)
    inv = 1.0 / (jnp.sum(p, axis=-1, keepdims=True) + e_sink)
    return p * inv, e_sink * inv


def _att_operands(q_ref, kvc_ref, kvp_ref, s_ref):
    kv = jnp.concatenate([kvp_ref[...], kvc_ref[...]], axis=0)
    kpad, vpad = _att_padded(kv[:, :128]), _att_padded(kv[:, 128:])
    key = lambda h: ((h // 2) // PAIRS_PER_KV, h % 2)
    pairs = [(q_ref[:, j * 128:(j + 1) * 128] * ATT_SCALE).astype(BF16) for j in range(PAIRS)]
    q = jnp.stack([pairs[h // 2] for h in range(A_HEADS)])
    k = jnp.stack([kpad[key(h)] for h in range(A_HEADS)])
    v = jnp.stack([vpad[key(h)] for h in range(A_HEADS)])
    sink = jnp.stack([s_ref[:, h:h + 1] for h in range(A_HEADS)])
    return q, k, v, sink


def _att_specs(t):
    return [pl.BlockSpec((WINDOW, A_Q), lambda n: (n, 0)),
            pl.BlockSpec((WINDOW, 2 * A_KV), lambda n: (n, COL_KV // (2 * A_KV))),
            pl.BlockSpec((WINDOW, 2 * A_KV), lambda n: (jnp.maximum(n - 1, 0), COL_KV // (2 * A_KV))),
            pl.BlockSpec((1, 128), lambda n: (0, 0))]


def att_fwd(proj, sinks, *, name):
    t = proj.shape[0]

    def body(q_ref, kvc_ref, kvp_ref, s_ref, o_ref):
        n = pl.program_id(0)
        q, k, v, sink = _att_operands(q_ref, kvc_ref, kvp_ref, s_ref)
        w, _ = _att_probs(q, k, sink, _att_valid(n))
        o = _dot(w.astype(BF16), v, NN)
        for j in range(PAIRS):
            o_ref[:, j * 128:(j + 1) * 128] = (o[2 * j] + o[2 * j + 1]).astype(BF16)

    return pl.pallas_call(
        body, name=name, grid=(t // WINDOW,), in_specs=_att_specs(t),
        out_specs=pl.BlockSpec((WINDOW, A_Q), lambda n: (n, 0)),
        out_shape=jax.ShapeDtypeStruct((t, A_Q), BF16), compiler_params=_params(("parallel",)))(
            proj, proj, proj, sinks)


def att_bwd(proj, sinks, dout, *, name):
    t = proj.shape[0]

    def body(q_ref, kvc_ref, kvp_ref, s_ref, do_ref, dq_ref, dkc_ref, dkp_ref, ds_ref):
        n = pl.program_id(0)

        @pl.when(n == 0)
        def _():
            ds_ref[...] = jnp.zeros_like(ds_ref)

        q, k, v, sink = _att_operands(q_ref, kvc_ref, kvp_ref, s_ref)
        dop = jnp.stack([do_ref[:, (h // 2) * 128:(h // 2 + 1) * 128] for h in range(A_HEADS)]).astype(BF16)
        w, w_sink = _att_probs(q, k, sink, _att_valid(n))
        dw = _dot(dop, v, NT)
        delta = jnp.sum(w * dw, axis=-1, keepdims=True)
        dsc = (w * (dw - delta)).astype(BF16)
        dsink_h = -jnp.sum(w_sink * delta, axis=1, keepdims=True)
        dq = _dot(dsc, k, NN)
        dk_h = _dot(dsc, q, TN)
        dv_h = _dot(w.astype(BF16), dop, TN)
        lane = lax.broadcasted_iota(jnp.int32, (1, 128), 1)
        dsink = jnp.zeros((1, 128), F32)
        for h in range(A_HEADS):
            dsink = dsink + jnp.where(lane == h, dsink_h[h], 0.0)
        ds_ref[...] += dsink
        for j in range(PAIRS):
            dq_ref[:, j * 128:(j + 1) * 128] = ((dq[2 * j] + dq[2 * j + 1]) * ATT_SCALE).astype(BF16)
        lo = lax.broadcasted_iota(jnp.int32, (2 * WINDOW, 128), 1) < A_HEAD_DIM
        heads_per_kv = A_HEADS // A_KV_HEADS

        def tile(per_head):
            acc = {}
            for kvh in range(A_KV_HEADS):
                for half in range(2):
                    hs = range(kvh * heads_per_kv + half, (kvh + 1) * heads_per_kv, 2)
                    acc[(kvh, half)] = functools.reduce(lambda a, b: a + b, [per_head[h] for h in hs])
            return jnp.where(lo, acc[(0, 0)] + pltpu.roll(acc[(0, 1)], A_HEAD_DIM, 1),
                             pltpu.roll(acc[(1, 0)], A_HEAD_DIM, 1) + acc[(1, 1)])

        dkv = jnp.concatenate([tile(dk_h), tile(dv_h)], axis=1)
        dkp_ref[...] = dkv[:WINDOW]
        dkc_ref[...] = dkv[WINDOW:]

    kvo = pl.BlockSpec((WINDOW, 2 * A_KV), lambda n: (n, 0))
    return pl.pallas_call(
        body, name=name, grid=(t // WINDOW,),
        in_specs=_att_specs(t) + [pl.BlockSpec((WINDOW, A_Q), lambda n: (n, 0))],
        out_specs=[pl.BlockSpec((WINDOW, A_Q), lambda n: (n, 0)), kvo, kvo, pl.BlockSpec((1, 128), lambda n: (0, 0))],
        out_shape=[jax.ShapeDtypeStruct((t, A_Q), BF16), jax.ShapeDtypeStruct((t, 2 * A_KV), F32),
                   jax.ShapeDtypeStruct((t, 2 * A_KV), F32), jax.ShapeDtypeStruct((1, 128), F32)],
        compiler_params=_params(("arbitrary",)))(proj, proj, proj, sinks, dout)


QK_SCALE = B_HEAD_DIM ** -0.5
PREP_COLS = 256
PREP_NCB = 3 * B_W // PREP_COLS
HALO = 8
PREP_ROWS = 512


def _roll_rows(x, shift):
    n = x.shape[0]
    return x if shift % n == 0 else pltpu.roll(x, shift % n, 0)


def _conv_taps(xe, w):
    xs = [_roll_rows(xe, CONV_K - 1 - i) for i in range(CONV_K)]
    c = w[0:1] * xs[0]
    for i in range(1, CONV_K):
        c = c + w[i:i + 1] * xs[i]
    return xs, c


def dprep_fwd(proj, conv_w, *, name):
    t = proj.shape[0]
    tt = min(PREP_ROWS, t)
    col0 = COL_QKVB // PREP_COLS

    def body(x_ref, h_ref, w_ref, o_ref):
        cb, n = pl.program_id(0), pl.program_id(1)
        halo = jnp.where(n > 0, h_ref[...], 0.0)
        xe = jnp.concatenate([halo, x_ref[...]], axis=0)
        _, c = _conv_taps(xe, w_ref[...])
        y = _silu(c)[HALO:]
        parts = []
        for hh in range(PREP_COLS // B_HEAD_DIM):
            yh = y[:, hh * B_HEAD_DIM:(hh + 1) * B_HEAD_DIM]
            parts.append(yh * lax.rsqrt(jnp.sum(yh * yh, axis=-1, keepdims=True) + EPS))
        nrm = jnp.concatenate(parts, axis=-1)
        o_ref[...] = jnp.where(cb < 4, nrm * QK_SCALE, jnp.where(cb < 8, nrm, y))

    return pl.pallas_call(
        body, name=name, grid=(PREP_NCB, t // tt),
        in_specs=[pl.BlockSpec((tt, PREP_COLS), lambda cb, n: (n, col0 + cb)),
                  pl.BlockSpec((HALO, PREP_COLS), lambda cb, n: (jnp.maximum(n * (tt // HALO) - 1, 0), col0 + cb)),
                  pl.BlockSpec((CONV_K, PREP_COLS), lambda cb, n: (0, cb))],
        out_specs=pl.BlockSpec((tt, PREP_COLS), lambda cb, n: (n, cb)),
        out_shape=jax.ShapeDtypeStruct((t, 3 * B_W), F32), compiler_params=_params(("parallel", "parallel")))(
            proj, proj, conv_w)


def dprep_bwd(proj, conv_w, dqkvn, *, name):
    t = proj.shape[0]
    tt = min(PREP_ROWS, t)
    nb = t // tt
    col0 = COL_QKVB // PREP_COLS
    n8 = t // HALO

    def body(xc_ref, xb_ref, xa_ref, dc_ref, da_ref, w_ref, dx_ref, dw_ref):
        cb, n = pl.program_id(0), pl.program_id(1)

        @pl.when(n == 0)
        def _():
            dw_ref[...] = jnp.zeros_like(dw_ref)

        w = w_ref[...]
        xe = jnp.concatenate([jnp.where(n > 0, xb_ref[...], 0.0), xc_ref[...], xa_ref[...]], axis=0)
        xs, c = _conv_taps(xe, w)
        sg = _sigmoid(c)
        y = c * sg
        dout = jnp.concatenate([jnp.zeros((HALO, PREP_COLS), F32), dc_ref[...],
                                jnp.where(n < nb - 1, da_ref[...], 0.0)], axis=0)
        dsc = jnp.where(cb < 4, QK_SCALE, 1.0)
        parts = []
        for hh in range(PREP_COLS // B_HEAD_DIM):
            sl = slice(hh * B_HEAD_DIM, (hh + 1) * B_HEAD_DIM)
            yh, doh = y[:, sl], dout[:, sl] * dsc
            r = lax.rsqrt(jnp.sum(yh * yh, axis=-1, keepdims=True) + EPS)
            parts.append(doh * r - yh * (r * r * r) * jnp.sum(doh * yh, axis=-1, keepdims=True))
        dy = jnp.where(cb < 8, jnp.concatenate(parts, axis=-1), dout)
        dcv = dy * sg * (1.0 + c * (1.0 - sg))
        dxe = w[CONV_K - 1:CONV_K] * dcv
        for i in range(CONV_K - 1):
            dxe = dxe + w[i:i + 1] * _roll_rows(dcv, -(CONV_K - 1 - i))
        dx_ref[...] = dxe[HALO:HALO + tt].astype(BF16)
        for i in range(CONV_K):
            dw_ref[i:i + 1, :] += jnp.sum((dcv * xs[i])[HALO:HALO + tt], axis=0, keepdims=True)

    def after(n):
        return jnp.minimum((n + 1) * (tt // HALO), n8 - 1)

    return pl.pallas_call(
        body, name=name, grid=(PREP_NCB, nb),
        in_specs=[pl.BlockSpec((tt, PREP_COLS), lambda cb, n: (n, col0 + cb)),
                  pl.BlockSpec((HALO, PREP_COLS), lambda cb, n: (jnp.maximum(n * (tt // HALO) - 1, 0), col0 + cb)),
                  pl.BlockSpec((HALO, PREP_COLS), lambda cb, n: (after(n), col0 + cb)),
                  pl.BlockSpec((tt, PREP_COLS), lambda cb, n: (n, cb)),
                  pl.BlockSpec((HALO, PREP_COLS), lambda cb, n: (after(n), cb)),
                  pl.BlockSpec((CONV_K, PREP_COLS), lambda cb, n: (0, cb))],
        out_specs=[pl.BlockSpec((tt, PREP_COLS), lambda cb, n: (n, cb)),
                   pl.BlockSpec((CONV_K, PREP_COLS), lambda cb, n: (0, cb))],
        out_shape=[jax.ShapeDtypeStruct((t, 3 * B_W), BF16), jax.ShapeDtypeStruct((CONV_K, 3 * B_W), F32)],
        compiler_params=_params(("parallel", "arbitrary")))(proj, proj, proj, dqkvn, dqkvn, conv_w)


def _softplus(z):
    return jnp.maximum(z, 0.0) + jnp.log(1.0 + jnp.exp(-jnp.abs(z)))


def gates_fwd(proj, alog_pad, dtb_pad, *, name):
    t = proj.shape[0]

    def body(x_ref, a_ref, b_ref, o_ref):
        raw = x_ref[...]
        lane = lax.broadcasted_iota(jnp.int32, raw.shape, 1)
        g = -jnp.exp(a_ref[...]) * _softplus(raw + b_ref[...])
        o_ref[...] = jnp.where(lane < B_HEADS, _sigmoid(raw), jnp.where(lane < 2 * B_HEADS, g, 0.0))

    vec = pl.BlockSpec((1, 128), lambda n: (0, 0))
    return pl.pallas_call(
        body, name=name, grid=(t // ROWS,),
        in_specs=[pl.BlockSpec((ROWS, 128), lambda n: (n, COL_GATE // 128)), vec, vec],
        out_specs=pl.BlockSpec((ROWS, 128), lambda n: (n, 0)),
        out_shape=jax.ShapeDtypeStruct((t, 128), F32), compiler_params=_params(("parallel",)))(
            proj, alog_pad, dtb_pad)


def gates_bwd(proj, alog_pad, dtb_pad, dgates, *, name):
    t = proj.shape[0]

    def body(x_ref, a_ref, b_ref, dg_ref, dx_ref, da_ref, db_ref):
        @pl.when(pl.program_id(0) == 0)
        def _():
            da_ref[...] = jnp.zeros_like(da_ref)
            db_ref[...] = jnp.zeros_like(db_ref)

        raw, dgt = x_ref[...], dg_ref[...]
        lane = lax.broadcasted_iota(jnp.int32, raw.shape, 1)
        is_beta, is_g = lane < B_HEADS, (lane >= B_HEADS) & (lane < 2 * B_HEADS)
        beta = _sigmoid(raw)
        z = raw + b_ref[...]
        neg_a = -jnp.exp(a_ref[...])
        d_z = jnp.where(is_g, dgt * neg_a * _sigmoid(z), 0.0)
        dx_ref[...] = jnp.where(is_beta, dgt * beta * (1.0 - beta), d_z).astype(BF16)
        db_ref[...] += jnp.sum(d_z, axis=0, keepdims=True)
        da_ref[...] += jnp.sum(jnp.where(is_g, dgt * neg_a * _softplus(z), 0.0), axis=0, keepdims=True)

    vec = pl.BlockSpec((1, 128), lambda n: (0, 0))
    row = pl.BlockSpec((ROWS, 128), lambda n: (n, 0))
    return pl.pallas_call(
        body, name=name, grid=(t // ROWS,),
        in_specs=[pl.BlockSpec((ROWS, 128), lambda n: (n, COL_GATE // 128)), vec, vec, row],
        out_specs=[row, vec, vec],
        out_shape=[jax.ShapeDtypeStruct((t, 128), BF16), jax.ShapeDtypeStruct((1, 128), F32),
                   jax.ShapeDtypeStruct((1, 128), F32)],
        compiler_params=_params(("arbitrary",)))(proj, alog_pad, dtb_pad, dgates)


def _split2(a):
    hi = a.astype(BF16)
    return hi, (a - hi.astype(F32)).astype(BF16)


def _dotp(a, b, dims, passes):
    if passes == 1:
        return _dot(a.astype(BF16), b.astype(BF16), dims)
    ah, al = _split2(a)
    bh, bl = _split2(b)
    return _dot(ah, bh, dims) + (_dot(ah, bl, dims) + _dot(al, bh, dims))


_GRAD_DIMS = {NN: ((NT, False), (TN, False)), NT: ((NN, False), (TN, True)), TN: ((NT, True), (NN, False))}


def _make_mm(dims, passes, grad_passes):
    (da_dims, da_swap), (db_dims, db_swap) = _GRAD_DIMS[dims]

    @jax.custom_vjp
    def mm(a, b):
        return _dotp(a, b, dims, passes)

    def fwd(a, b):
        return _dotp(a, b, dims, passes), (a, b)

    def bwd(saved, ct):
        a, b = saved
        da = _dotp(b, ct, da_dims, grad_passes) if da_swap else _dotp(ct, b, da_dims, grad_passes)
        db = _dotp(ct, a, db_dims, grad_passes) if db_swap else _dotp(a, ct, db_dims, grad_passes)
        return da, db

    mm.defvjp(fwd, bwd)
    return mm


MM1 = {d: _make_mm(d, 1, 1) for d in (NN, NT, TN)}
MM3 = {d: _make_mm(d, 3, 1) for d in (NN, NT, TN)}


def _neumann_value(n):
    c = n.shape[-1]
    eye = (lax.broadcasted_iota(jnp.int32, (c, c), 0) == lax.broadcasted_iota(jnp.int32, (c, c), 1)).astype(F32)
    inv, pw = eye + n, n
    for _ in range(5):
        pw = _dotp(pw, pw, NN, 3)
        inv = inv + _dotp(inv, pw, NN, 3)
    return inv


@jax.custom_vjp
def _neumann_inverse(n):
    return _neumann_value(n)


def _neumann_fwd(n):
    inv = _neumann_value(n)
    return inv, inv


def _neumann_bwd(inv, ct):
    return (_dotp(_dotp(inv, ct, TN, 1), inv, NT, 1),)


_neumann_inverse.defvjp(_neumann_fwd, _neumann_bwd)


def _tri_ones(lower):
    r = lax.broadcasted_iota(jnp.int32, (DN_CHUNK, DN_CHUNK), 0)
    c = lax.broadcasted_iota(jnp.int32, (DN_CHUNK, DN_CHUNK), 1)
    return (r >= c if lower else r <= c).astype(BF16)


def _tri_sum(x, lower):
    tri = _tri_ones(lower)
    hi = x.astype(BF16)
    r1 = x - hi.astype(F32)
    mid = r1.astype(BF16)
    lo = (r1 - mid.astype(F32)).astype(BF16)
    return _dot(tri, hi, NN) + (_dot(tri, mid, NN) + _dot(tri, lo, NN))


def _delta_chunk(s0, q, k, v, beta, gam_c, gam_r):
    c = DN_CHUNK
    r = lax.broadcasted_iota(jnp.int32, (c, c), 0)
    cc = lax.broadcasted_iota(jnp.int32, (c, c), 1)
    incl, strict = r >= cc, r > cc
    decay = jnp.exp(jnp.where(incl, gam_c - gam_r, NEG_INF))
    g_last = gam_c[:, c - 1:c, :]
    e_gam, e_rest, e_last = jnp.exp(gam_c), jnp.exp(g_last - gam_c), jnp.exp(g_last)
    a_neg = -jnp.where(strict, beta * MM1[NT](k, k) * decay, 0.0)
    inv = _neumann_inverse(a_neg)
    uw = MM3[NN](inv,jnp.concatenate([v * beta, k * (beta * e_gam)], axis=-1))
    u, w = uw[..., :B_HEAD_DIM], uw[..., B_HEAD_DIM:]
    qk = MM1[NT](q, k) * decay
    v_new = u - MM1[NN](w, s0)
    o = MM1[NN](q * e_gam, s0) + MM1[NN](qk, v_new)
    s1 = s0 * e_last + MM1[TN](k * e_rest, v_new)
    return s1, o


def _delta_operands(q_ref, k_ref, v_ref, gt):
    heads = lambda ref: jnp.stack([ref[:, h * B_HEAD_DIM:(h + 1) * B_HEAD_DIM] for h in range(B_HEADS)])
    gam = _tri_sum(gt, True)
    gam_t = gam.T
    beta = jnp.stack([gt[:, h:h + 1] for h in range(B_HEADS)])
    gam_c = jnp.stack([gam[:, B_HEADS + h:B_HEADS + h + 1] for h in range(B_HEADS)])
    gam_r = jnp.stack([gam_t[B_HEADS + h:B_HEADS + h + 1, :] for h in range(B_HEADS)])
    return heads(q_ref), heads(k_ref), heads(v_ref), beta, gam_c, gam_r


def delta_fwd(qkvn, gates, *, name):
    t = qkvn.shape[0]
    nc = t // DN_CHUNK

    def body(q_ref, k_ref, v_ref, g_ref, o_ref, ss_ref, state):
        @pl.when(pl.program_id(0) == 0)
        def _():
            state[...] = jnp.zeros_like(state)

        s0 = state[...]
        ss_ref[...] = s0
        s1, o = _delta_chunk(s0, *_delta_operands(q_ref, k_ref, v_ref, g_ref[...]))
        state[...] = s1
        for h in range(B_HEADS):
            o_ref[:, h * B_HEAD_DIM:(h + 1) * B_HEAD_DIM] = o[h]

    blk = lambda j: pl.BlockSpec((DN_CHUNK, B_W), lambda n: (n, j))
    return pl.pallas_call(
        body, name=name, grid=(nc,),
        in_specs=[blk(0), blk(1), blk(2), pl.BlockSpec((DN_CHUNK, 128), lambda n: (n, 0))],
        out_specs=[blk(0), pl.BlockSpec((None, B_HEADS, B_HEAD_DIM, B_HEAD_DIM), lambda n: (n, 0, 0, 0))],
        out_shape=[jax.ShapeDtypeStruct((t, B_W), F32),
                   jax.ShapeDtypeStruct((nc, B_HEADS, B_HEAD_DIM, B_HEAD_DIM), F32)],
        scratch_shapes=[pltpu.VMEM((B_HEADS, B_HEAD_DIM, B_HEAD_DIM), F32)],
        compiler_params=_params(("arbitrary",)))(qkvn, qkvn, qkvn, gates)


def delta_bwd(qkvn, gates, ssave, do, *, name):
    t = qkvn.shape[0]
    nc = t // DN_CHUNK

    def body(q_ref, k_ref, v_ref, g_ref, ss_ref, do_ref, dx_ref, dg_ref, dstate):
        @pl.when(pl.program_id(0) == 0)
        def _():
            dstate[...] = jnp.zeros_like(dstate)

        lane = lax.broadcasted_iota(jnp.int32, (DN_CHUNK, 128), 1)
        row = lax.broadcasted_iota(jnp.int32, (128, DN_CHUNK), 0)
        dbeta_all = jnp.zeros((DN_CHUNK, 128), F32)
        dgam_c_all = jnp.zeros((DN_CHUNK, 128), F32)
        dgam_r_all = jnp.zeros((128, DN_CHUNK), F32)
        _, vjp = jax.vjp(_delta_chunk, ss_ref[...], *_delta_operands(q_ref, k_ref, v_ref, g_ref[...]))
        do = jnp.stack([do_ref[:, h * B_HEAD_DIM:(h + 1) * B_HEAD_DIM] for h in range(B_HEADS)])
        ds0, dq, dk, dv, dbeta, dgam_c, dgam_r = vjp((dstate[...], do))
        dstate[...] = ds0
        for h in range(B_HEADS):
            dx_ref[:, h * B_HEAD_DIM:(h + 1) * B_HEAD_DIM] = dq[h]
            dx_ref[:, B_W + h * B_HEAD_DIM:B_W + (h + 1) * B_HEAD_DIM] = dk[h]
            dx_ref[:, 2 * B_W + h * B_HEAD_DIM:2 * B_W + (h + 1) * B_HEAD_DIM] = dv[h]
            dbeta_all = dbeta_all + jnp.where(lane == h, dbeta[h], 0.0)
            dgam_c_all = dgam_c_all + jnp.where(lane == B_HEADS + h, dgam_c[h], 0.0)
            dgam_r_all = dgam_r_all + jnp.where(row == B_HEADS + h, dgam_r[h], 0.0)
        dg_ref[...] = dbeta_all + _tri_sum(dgam_c_all + dgam_r_all.T, False)

    blk = lambda j: pl.BlockSpec((DN_CHUNK, B_W), lambda n: (nc - 1 - n, j))
    gsp = pl.BlockSpec((DN_CHUNK, 128), lambda n: (nc - 1 - n, 0))
    return pl.pallas_call(
        body, name=name, grid=(nc,),
        in_specs=[blk(0), blk(1), blk(2), gsp,
                  pl.BlockSpec((None, B_HEADS, B_HEAD_DIM, B_HEAD_DIM), lambda n: (nc - 1 - n, 0, 0, 0)), blk(0)],
        out_specs=[pl.BlockSpec((DN_CHUNK, 3 * B_W), lambda n: (nc - 1 - n, 0)), gsp],
        out_shape=[jax.ShapeDtypeStruct((t, 3 * B_W), F32), jax.ShapeDtypeStruct((t, 128), F32)],
        scratch_shapes=[pltpu.VMEM((B_HEADS, B_HEAD_DIM, B_HEAD_DIM), F32)],
        compiler_params=_params(("arbitrary",)))(qkvn, qkvn, qkvn, gates, ssave, do)


GNORM_ROWS = 1024


def gnorm_fwd(o, proj, onorm, *, name):
    t = o.shape[0]

    def body(o_ref, z_ref, w_ref, out_ref):
        ov = o_ref[...]
        r = lax.rsqrt(jnp.mean(ov * ov, axis=-1, keepdims=True) + EPS)
        out_ref[...] = (ov * r * w_ref[...] * _silu(z_ref[...])).astype(BF16)

    rows = min(GNORM_ROWS, t)
    blk = pl.BlockSpec((rows, B_HEAD_DIM), lambda n, h: (n, h))
    return pl.pallas_call(
        body, name=name, grid=(t // rows, B_HEADS),
        in_specs=[blk, pl.BlockSpec((rows, B_HEAD_DIM), lambda n, h: (n, COL_Z // B_HEAD_DIM + h)),
                  pl.BlockSpec((1, B_HEAD_DIM), lambda n, h: (0, 0))],
        out_specs=blk, out_shape=jax.ShapeDtypeStruct((t, B_W), BF16),
        compiler_params=_params(("parallel", "parallel")))(o, proj, onorm)


def gnorm_bwd(o, proj, onorm, dout, *, dcol0, name):
    t = o.shape[0]

    def body(o_ref, z_ref, w_ref, d_ref, do_ref, dz_ref, dw_ref):
        @pl.when((pl.program_id(0) == 0) & (pl.program_id(1) == 0))
        def _():
            dw_ref[...] = jnp.zeros_like(dw_ref)

        ov, zv, wv, dv = o_ref[...], z_ref[...], w_ref[...], d_ref[...].astype(F32)
        r = lax.rsqrt(jnp.mean(ov * ov, axis=-1, keepdims=True) + EPS)
        nrm = ov * r
        dz_ref[...] = (dv * nrm * wv * _dsilu(zv)).astype(BF16)
        da = dv * _silu(zv)
        dw_ref[...] += jnp.sum(da * nrm, axis=0, keepdims=True)
        dn = da * wv
        do_ref[...] = r * dn - ov * (r * r * r) * jnp.mean(dn * ov, axis=-1, keepdims=True)

    rows = min(GNORM_ROWS, t)
    blk = pl.BlockSpec((rows, B_HEAD_DIM), lambda n, h: (n, h))
    vec = pl.BlockSpec((1, B_HEAD_DIM), lambda n, h: (0, 0))
    return pl.pallas_call(
        body, name=name, grid=(t // rows, B_HEADS),
        in_specs=[blk, pl.BlockSpec((rows, B_HEAD_DIM), lambda n, h: (n, COL_Z // B_HEAD_DIM + h)), vec,
                  pl.BlockSpec((rows, B_HEAD_DIM), lambda n, h: (n, dcol0 // B_HEAD_DIM + h))],
        out_specs=[blk, blk, vec],
        out_shape=[jax.ShapeDtypeStruct((t, B_W), F32), jax.ShapeDtypeStruct((t, B_W), BF16),
                   jax.ShapeDtypeStruct((1, B_HEAD_DIM), F32)],
        compiler_params=_params(("arbitrary", "arbitrary")))(o, proj, onorm, dout)


def _ffn_fwd(h, norm_g, wg, wu, wd, tm, tag):
    hn = rms_fwd(h, norm_g, name=f"ffn{tag}_norm")
    gate, up, act = mm_gate_up(hn, wg, wu, tm=min(512, tm), tn=1408, tk=2048, name=f"ffn{tag}_gate_up")
    h_out = mm_nn(act, wd, tm=tm, tn=2048, tk=512, out_dtype=F32, res=h, name=f"ffn{tag}_down")
    return h_out, (hn, gate, up, act)


def _ffn_bwd(dh, h, norm_g, wg, wu, wd, saved, tm, tag, emit):
    hn, gate, up, act = saved
    dwd = mm_tn(act, dh, shards=1, tm=tm, tn=1024, tk=1408, out_dtype=BF16, name=f"ffn{tag}_dwd")[0]
    dgate, dup = mm_down_bwd(dh, wd, gate, up, tm=tm, tn=512, tk=2048, name=f"ffn{tag}_dact")
    dwg = mm_tn(hn, dgate, shards=N_SHARD, tm=tm, tn=1408, tk=1024, out_dtype=BF16, name=f"ffn{tag}_dwg")
    dwu = mm_tn(hn, dup, shards=N_SHARD, tm=tm, tn=1408, tk=1024, out_dtype=BF16, name=f"ffn{tag}_dwu")
    started = emit(f"ffn{tag}", {"gate": dwg, "up": dwu, "down": dwd})
    dhn = mm_nt(dgate, wg, tm=tm, tn=1024, tk=1408, out_dtype=F32, name=f"ffn{tag}_dhn_g")
    dh_in, dnorm = dgrad_rms_bwd(dup, wu, NT, h, norm_g + started, dh, tm=min(512, tm), tk=1408, res=dhn,
                                 name=f"ffn{tag}_dhn_u_dnorm")
    return dh_in, dnorm


def _local_step(x, target, w, get, emit):
    t = x.shape[0]
    tm = min(1024, t)
    g = {}

    hn0 = rms_fwd(x, w["even_norm"], name="l0_norm")
    w.update(get("even_in", hn0))
    proj = mm_nt(hn0, w["even_w_in"], tm=tm, tn=512, tk=2048, out_dtype=F32, name="l0_w_in")
    out_a = att_fwd(proj, w["sinks"], name="l0_att")
    qkvn = dprep_fwd(proj, w["even_conv"], name="l0_prep")
    gates = gates_fwd(proj, w["a_log"], w["dt_bias"], name="l0_gates")
    o_delta, ssave = delta_fwd(qkvn, gates, name="l0_delta")
    w.update(get("even_out", o_delta))
    out_b = gnorm_fwd(o_delta, proj, w["onorm"], name="l0_gnorm")
    mix0 = jnp.concatenate([out_a, out_b], axis=-1)
    h1 =mm_nn(mix0, w["even_w_out"], tm=tm, tn=512, tk=2048, out_dtype=F32, res=x, name="l0_w_out")
    f0 = get("ffn0", h1)
    h2, ffn0 = _ffn_fwd(h1, w["ffn_norm"][0:1] + f0["tok"], f0["gate"], f0["up"], f0["down"], tm, 0)
    hn2 = rms_fwd(h2, w["odd_norm"], name="l1_norm")
    w.update(get("odd", hn2))
    zpre = mm_nn(hn2, w["odd_w_in"], tm=tm, tn=1024, tk=2048, out_dtype=F32, name="l1_w_in")
    gated = gmlp_fwd(zpre, w["odd_ln_g"], w["odd_ln_b"], w["odd_w_s"], w["odd_b_s"], name="l1_gmlp")
    h3 = mm_nn(gated, w["odd_w_out"], tm=tm, tn=512, tk=2048, out_dtype=F32, res=h2, name="l1_w_out")
    f1 = get("ffn1", h3)
    h4, ffn1 = _ffn_fwd(h3, w["ffn_norm"][1:2] + f1["tok"], f1["gate"], f1["up"], f1["down"], tm, 1)
    loss, dh4, g["final_norm"] = loss_head(h4, w["final_norm"], target, name="loss_head")

    dh3, dn1 = _ffn_bwd(dh4, h3, w["ffn_norm"][1:2], f1["gate"], f1["up"], f1["down"], ffn1, tm, 1, emit)
    dw_out_o = mm_tn(gated, dh3, shards=1, tm=tm, tn=1024, tk=1024, out_dtype=BF16, name="l1_dw_out")[0]
    dgated = mm_nt(dh3, w["odd_w_out"], tm=tm, tn=512, tk=2048, out_dtype=BF16, name="l1_dgated")
    dzpre, g["odd_w_s"], g["odd_b_s"], g["odd_ln_g"], g["odd_ln_b"] = gmlp_bwd(
        zpre, dgated, w["odd_ln_g"], w["odd_ln_b"], w["odd_w_s"], w["odd_b_s"], name="l1_dgmlp")
    dw_in_o = mm_tn(hn2, dzpre, shards=N_SHARD, tm=tm, tn=1024, tk=1024, out_dtype=BF16, name="l1_dw_in")
    started = emit("odd", {"odd_w_in": dw_in_o, "odd_w_out": dw_out_o})
    dh2, g["odd_norm"] = dgrad_rms_bwd(dzpre, w["odd_w_in"], NT, h2, w["odd_norm"] + started, dh3, tm=min(512, tm),
                                       tk=1024, name="l1_dhn_dnorm")
    dh1, dn0 = _ffn_bwd(dh2, h1, w["ffn_norm"][0:1], f0["gate"], f0["up"], f0["down"], ffn0, tm, 0, emit)
    g["ffn_norm"] = jnp.concatenate([dn0, dn1], axis=0)
    dw_out_e = mm_tn(mix0, dh1, shards=1, tm=tm, tn=1024, tk=1024, out_dtype=BF16, name="l0_dw_out")[0]
    started = emit("even_out", {"even_w_out": dw_out_e})
    dmix = mm_nt(dh1, w["even_w_out"], tm=tm, tn=512, tk=2048, out_dtype=F32, name="l0_dmix")
    dq_a, dkv_cur, dkv_prev, g["sinks"] = att_bwd(proj, w["sinks"] + started, dmix, name="l0_datt")
    dkv = dkv_cur + jnp.concatenate([dkv_prev[WINDOW:], jnp.zeros((WINDOW, 2 * A_KV), F32)], axis=0)
    do_delta, dz, g["onorm"] = gnorm_bwd(o_delta, proj, w["onorm"], dmix, dcol0=A_Q, name="l0_dgnorm")
    dqkvn, dgates = delta_bwd(qkvn, gates, ssave, do_delta, name="l0_ddelta")
    dqkv_b, g["even_conv"] = dprep_bwd(proj, w["even_conv"], dqkvn, name="l0_dprep")
    draw, g["a_log"], g["dt_bias"] = gates_bwd(proj, w["a_log"], w["dt_bias"], dgates, name="l0_dgates")
    dproj = jnp.concatenate([dq_a, dkv.astype(BF16), dqkv_b, dz, draw,
                             jnp.zeros((t, EVEN_IN_PAD - COL_GATE - 128), BF16)], axis=-1)
    dw_in_e = mm_tn(dproj, hn0, shards=1, tm=tm, tn=1024, tk=1408, out_dtype=BF16, name="l0_dw_in")[0]
    grad_x, g["even_norm"] = dgrad_rms_bwd(dproj, w["even_w_in"], NN, x, w["even_norm"], dh1, tm=min(512, tm), tk=512,
                                           name="l0_dhn_dnorm")
    emit("even_in", {"even_w_in": dw_in_e, "small": g})
    return loss, grad_x


ANY = pl.BlockSpec(memory_space=pl.ANY)
N_DEV = 8


def _place():
    return lax.axis_index("x"), lax.axis_index("y"), lax.axis_index("c")


def _chip_peers(x, y, c):
    return [((1 - x, y, c), 2 * (1 - x) + y), ((x, 1 - y, c), 2 * x + 1 - y), ((1 - x, 1 - y, c), 2 * (1 - x) + 1 - y)]


HBM = pl.BlockSpec(memory_space=pltpu.HBM)
SEM = pl.BlockSpec(memory_space=pltpu.SEMAPHORE)
EFFECT = pltpu.SideEffectType.DATAFLOW_SIDE_EFFECTING
N_PEER = 3


def _half(ref, c):
    r, cols = ref.shape
    tile_rows = 32 // jnp.dtype(ref.dtype).itemsize
    if (r // 2) % tile_rows == 0:
        return ref.at[pl.ds(c * (r // 2), r // 2)]
    assert (cols // 2) % 128 == 0, ref.shape
    return ref.at[:, pl.ds(c * (cols // 2), cols // 2)]


def _gather_plan(srcs, lands, send, recv):
    x, y, c = _place()
    return [pltpu.make_async_remote_copy(src_ref=_half(srcs[i], c), dst_ref=_half(lands[i].at[2 * x + y], c),
                                         send_sem=send.at[N_PEER * i + k], recv_sem=recv.at[N_PEER * i + k],
                                         device_id=peer, device_id_type=MESH_ID)
            for i in range(len(srcs)) for k, (peer, _) in enumerate(_chip_peers(x, y, c))]


def _relay_plan(srcs, lands, send, recv):
    x, y, c = _place()
    return [pltpu.make_async_remote_copy(src_ref=_half(lands[i].at[idx], c), dst_ref=_half(lands[i].at[idx], c),
                                         send_sem=send.at[N_PEER * i + k], recv_sem=recv.at[N_PEER * i + k],
                                         device_id=(x, y, 1 - c), device_id_type=MESH_ID)
            for i in range(len(srcs)) for k, (_, idx) in enumerate(_chip_peers(x, y, c))]


def _scatter_plan(srcs, lands, send, recv):
    x, y, c = _place()
    return [pltpu.make_async_remote_copy(src_ref=srcs[i].at[idx], dst_ref=lands[i].at[k], send_sem=send.at[N_PEER * i + k],
                                         recv_sem=recv.at[N_PEER * i + k], device_id=peer, device_id_type=MESH_ID)
            for i in range(len(srcs)) for k, (peer, idx) in enumerate(_chip_peers(x, y, c))]


def _swap_plan(srcs, lands, send, recv):
    x, y, c = _place()
    return [pltpu.make_async_remote_copy(src_ref=srcs[i], dst_ref=lands[i], send_sem=send.at[N_PEER * i],
                                         recv_sem=recv.at[N_PEER * i], device_id=(x, y, 1 - c), device_id_type=MESH_ID)
            for i in range(len(srcs))]


def copies_start(plan, srcs, lands, after, *, name):
    n = len(srcs)
    both = list(srcs) + list(lands)

    def body(*refs):
        src_refs, land_refs = refs[:n], refs[n:2 * n]
        send, recv = refs[2 * n + 1], refs[2 * n + 2]
        for cp in plan(src_refs, land_refs, send, recv):
            cp.start()
        refs[-1][...] = jnp.zeros_like(refs[-1])

    res = pl.pallas_call(
        body, name=name,
        out_shape=(pltpu.SemaphoreType.DMA((n * N_PEER,)), pltpu.SemaphoreType.DMA((n * N_PEER,)),
                   *[pltpu.HBM(a.shape, a.dtype) for a in both], jax.ShapeDtypeStruct((8, 128), F32)),
        in_specs=[HBM] * (2 * n) + [ANY],
        out_specs=(SEM, SEM, *[HBM] * (2 * n), pl.BlockSpec(memory_space=pltpu.VMEM)),
        input_output_aliases={i: 2 + i for i in range(2 * n)},
        compiler_params=pltpu.CompilerParams(has_side_effects=EFFECT))(
            *[pltpu.with_memory_space_constraint(a, pltpu.HBM) for a in both], after)
    return {"send": res[0], "recv": res[1], "srcs": list(res[2:2 + n]), "lands": list(res[2 + n:2 + 2 * n]),
            "token": res[-1]}


def copies_relay(arrived_plan, next_plan, started, after, *, name):
    srcs, lands = started["srcs"], started["lands"]
    n = len(srcs)
    both = srcs + lands

    def body(*refs):
        src_refs, land_refs = refs[:n], refs[n:2 * n]
        send1, recv1 = refs[2 * n], refs[2 * n + 1]
        send2, recv2 = refs[2 * n + 3], refs[2 * n + 4]
        for cp in arrived_plan(src_refs, land_refs, send1, recv1):
            cp.wait_send()
            cp.wait_recv()
        for cp in next_plan(src_refs, land_refs, send2, recv2):
            cp.start()
        refs[-1][...] = jnp.zeros_like(refs[-1])

    res = pl.pallas_call(
        body, name=name,
        out_shape=(pltpu.SemaphoreType.DMA((n * N_PEER,)), pltpu.SemaphoreType.DMA((n * N_PEER,)),
                   *[pltpu.HBM(a.shape, a.dtype) for a in both], jax.ShapeDtypeStruct((8, 128), F32)),
        in_specs=[HBM] * (2 * n) + [SEM, SEM, ANY],
        out_specs=(SEM, SEM, *[HBM] * (2 * n), pl.BlockSpec(memory_space=pltpu.VMEM)),
        input_output_aliases={i: 2 + i for i in range(2 * n)},
        compiler_params=pltpu.CompilerParams(has_side_effects=EFFECT))(*both, started["send"], started["recv"], after)
    return {"send": res[0], "recv": res[1], "srcs": list(res[2:2 + n]), "lands": list(res[2 + n:2 + 2 * n]),
            "token": res[-1]}


def copies_wait(plan, started, after, *, name):
    srcs, lands = started["srcs"], started["lands"]
    n = len(srcs)
    both = srcs + lands

    def body(*refs):
        src_refs, land_refs = refs[:n], refs[n:2 * n]
        send, recv = refs[2 * n], refs[2 * n + 1]
        for cp in plan(src_refs, land_refs, send, recv):
            cp.wait_send()
            cp.wait_recv()

    res = pl.pallas_call(
        body, name=name, out_shape=tuple(pltpu.HBM(a.shape, a.dtype) for a in both),
        in_specs=[HBM] * (2 * n) + [SEM, SEM, ANY], out_specs=(HBM,) * (2 * n),
        input_output_aliases={i: i for i in range(2 * n)},
        compiler_params=pltpu.CompilerParams(has_side_effects=EFFECT))(*both, started["send"], started["recv"], after)
    return list(res[:n]), list(res[n:])


def allgather_small(small, *, name):
    def body(small_ref, out_ref, send, recv, loc):
        x, y, c = _place()
        dev = 4 * x + 2 * y + c
        local = pltpu.make_async_copy(small_ref, out_ref.at[dev], loc)
        remote = []
        for r in range(1, N_DEV):
            fx, fy, fc = (r >> 2) & 1, (r >> 1) & 1, r & 1
            peer = (1 - x if fx else x, 1 - y if fy else y, 1 - c if fc else c)
            remote.append(pltpu.make_async_remote_copy(
                src_ref=small_ref, dst_ref=out_ref.at[dev], send_sem=send.at[r - 1], recv_sem=recv.at[r - 1],
                device_id=peer, device_id_type=MESH_ID))
        local.start()
        for cp in remote:
            cp.start()
        for cp in remote:
            cp.wait()
        local.wait()

    return pl.pallas_call(
        body, name=name, in_specs=[ANY], out_specs=ANY,
        out_shape=jax.ShapeDtypeStruct((N_DEV,) + small.shape, small.dtype),
        scratch_shapes=[pltpu.SemaphoreType.DMA((N_DEV - 1,)), pltpu.SemaphoreType.DMA((N_DEV - 1,)),
                        pltpu.SemaphoreType.DMA(())])(small)


RED_ROWS = 256
RED_COLS = 256


def _red_block(r, c):
    if r % RED_ROWS == 0:
        return RED_ROWS, c
    if c > RED_COLS and c % RED_COLS == 0:
        return r, RED_COLS
    return r, c


def sum_chips(by_owner, me, got, *, name):
    _, r, c = by_owner.shape
    rb, cb = _red_block(r, c)

    def body(me_ref, o_ref, a_ref, b_ref, c_ref, out_ref):
        total = ((o_ref[...].astype(F32) + a_ref[...].astype(F32)) + b_ref[...].astype(F32)) + c_ref[...].astype(F32)
        out_ref[...] = total.astype(BF16)

    gk = lambda k: pl.BlockSpec((None, rb, cb), lambda i, j, me_ref: (k, i, j))
    grid_spec = pltpu.PrefetchScalarGridSpec(
        num_scalar_prefetch=1, grid=(r // rb, c // cb),
        in_specs=[pl.BlockSpec((None, rb, cb), lambda i, j, me_ref: (me_ref[0], i, j)), gk(0), gk(1), gk(2)],
        out_specs=pl.BlockSpec((rb, cb), lambda i, j, me_ref: (i, j)))
    return pl.pallas_call(
        body, name=name, grid_spec=grid_spec, out_shape=jax.ShapeDtypeStruct((r, c), BF16),
        compiler_params=_params(("parallel", "parallel")))(me, by_owner, got, got, got)


def sum_devices(small_all, *, name):
    _, p, c = small_all.shape

    def body(a_ref, out_ref):
        acc = a_ref[0]
        for d in range(1, N_DEV):
            acc = acc + a_ref[d]
        out_ref[...] = acc

    return pl.pallas_call(
        body, name=name, grid=(1,), in_specs=[pl.BlockSpec((N_DEV, p, c), lambda i: (0, 0, 0))],
        out_specs=pl.BlockSpec((p, c), lambda i: (0, 0)), out_shape=jax.ShapeDtypeStruct((p, c), F32),
        compiler_params=_params(("arbitrary",)))(small_all)


def adamw(parts, w, m, v, *, name):
    nl, r, c = w.shape
    assert len(parts) == nl
    npart = len(parts[0])
    rb, cb = _red_block(r, c)
    flat = [a for layer in parts for a in layer]

    def body(*refs):
        p_refs, (w_ref, m_ref, v_ref) = refs[:nl * npart], refs[nl * npart:nl * npart + 3]
        g_ref, d_ref, nm_ref, nv_ref = refs[nl * npart + 3:]
        layer = pl.program_id(0)
        grad = None
        for l in range(nl):
            gl = p_refs[l * npart][...].astype(F32)
            for j in range(1, npart):
                gl = gl + p_refs[l * npart + j][...].astype(F32)
            grad = gl if grad is None else jnp.where(layer == l, gl, grad)
        wv, mv, vv = w_ref[...], m_ref[...], v_ref[...]
        nm = ADAM_B1 * mv + (1.0 - ADAM_B1) * grad
        nv = ADAM_B2 * vv + (1.0 - ADAM_B2) * (grad * grad)
        m_hat = nm / (1.0 - ADAM_B1 ** ADAM_STEP)
        v_hat = nv / (1.0 - ADAM_B2 ** ADAM_STEP)
        g_ref[...] = grad
        d_ref[...] = -ADAM_LR * (m_hat / (jnp.sqrt(v_hat) + ADAM_EPS) + ADAM_WD * wv)
        nm_ref[...] = nm
        nv_ref[...] = nv

    pspec = pl.BlockSpec((rb, cb), lambda l, i, j: (i, j))
    wspec = pl.BlockSpec((None, rb, cb), lambda l, i, j: (l, i, j))
    osh = jax.ShapeDtypeStruct((nl, r, c), F32)
    return pl.pallas_call(
        body, name=name, grid=(nl, r // rb, c // cb), in_specs=[pspec] * (nl * npart) + [wspec] * 3,
        out_specs=[wspec] * 4, out_shape=[osh] * 4,
        compiler_params=_params(("parallel", "parallel", "parallel")))(*flat, w, m, v)


def _rows128(a):
    flat = a.reshape(-1)
    pad = (-flat.shape[0]) % 128
    return jnp.pad(flat, (0, pad)).reshape(-1, 128)


def _pack_rows(arrs, multiple=8):
    rows = jnp.concatenate([_rows128(a.astype(F32)) for a in arrs], axis=0)
    return jnp.pad(rows, ((0, (-rows.shape[0]) % multiple), (0, 0)))


def _unpack_rows(rows, shapes):
    out, r0 = [], 0
    for shp in shapes:
        size = 1
        for s in shp:
            size *= s
        nr = -(-size // 128)
        out.append(rows[r0:r0 + nr].reshape(-1)[:size].reshape(shp))
        r0 += nr
    return out


SMALL_LOCAL_GRADS = ["even_norm", "even_conv", "a_log", "dt_bias", "sinks", "onorm", "odd_norm", "odd_ln_g",
                     "odd_ln_b", "odd_w_s", "odd_b_s", "ffn_norm", "final_norm"]
BIG = ["even_w_in", "even_w_out", "odd_w_in", "odd_w_out", "ffn_w_gate", "ffn_w_up", "ffn_w_down"]
WEIGHTS = ["even_norm", "even_w_in", "even_conv", "even_a_log", "even_dt_bias", "even_sinks", "even_onorm",
           "even_w_out", "odd_norm", "odd_w_in", "odd_ln_g", "odd_ln_b", "odd_w_s", "odd_b_s", "odd_w_out",
           "ffn_norm", "ffn_w_gate", "ffn_w_up", "ffn_w_down", "final_norm"]
SMALL = [n for n in WEIGHTS if n not in BIG]


def kernel(x, even_norm, even_w_in, even_conv, even_a_log, even_dt_bias, even_sinks, even_onorm, even_w_out, odd_norm, odd_w_in, odd_ln_g, odd_ln_b, odd_w_s, odd_b_s, odd_w_out, ffn_norm, ffn_w_gate, ffn_w_up, ffn_w_down, final_norm, loss_target, m_even_norm, m_even_w_in, m_even_conv, m_even_a_log, m_even_dt_bias, m_even_sinks, m_even_onorm, m_even_w_out, m_odd_norm, m_odd_w_in, m_odd_ln_g, m_odd_ln_b, m_odd_w_s, m_odd_b_s, m_odd_w_out, m_ffn_norm, m_ffn_w_gate, m_ffn_w_up, m_ffn_w_down, m_final_norm, v_even_norm, v_even_w_in, v_even_conv, v_even_a_log, v_even_dt_bias, v_even_sinks, v_even_onorm, v_even_w_out, v_odd_norm, v_odd_w_in, v_odd_ln_g, v_odd_ln_b, v_odd_w_s, v_odd_b_s, v_odd_w_out, v_ffn_norm, v_ffn_w_gate, v_ffn_w_up, v_ffn_w_down, v_final_norm):
    args = dict(locals())
    wl = {n: args[n] for n in WEIGHTS}
    ml = {n: args["m_" + n] for n in WEIGHTS}
    vl = {n: args["v_" + n] for n in WEIGHTS}
    me = 2 * lax.axis_index("x") + lax.axis_index("y")

    def landing(a):
        return lax.dynamic_update_index_in_dim(lax.empty((N_SHARD,) + a.shape, a.dtype), a, me, 0)

    b16 = lambda *arrs: [a.astype(BF16) for a in arrs]
    gather_groups = {
        "even_in": b16(even_w_in[0].T) + [_pack_rows([even_conv[0], odd_norm, odd_ln_g, odd_ln_b], multiple=16)],
        "even_out": b16(even_w_out[0]),
        "ffn0": b16(ffn_w_gate[0], ffn_w_up[0], ffn_w_down[0]),
        "odd": b16(odd_w_in[0], odd_w_out[0]),
        "ffn1": b16(ffn_w_gate[1], ffn_w_up[1], ffn_w_down[1]),
    }
    gathering, after = {}, even_norm
    for group, srcs in gather_groups.items():
        gathering[group] = copies_start(_gather_plan, srcs, [landing(a) for a in srcs], after,
                                        name=f"gather_{group}_start")
        after = gathering[group]["token"]

    order = list(gather_groups)
    relayed, kept = {}, {}
    sinks_pad = jnp.pad(even_sinks, ((0, 0), (0, 128 - A_HEADS)))

    def relay(group, behind):
        relayed[group] = copies_relay(_gather_plan, _relay_plan, gathering[group], behind,
                                      name=f"gather_{group}_relay")
        return relayed[group]["token"][0:1, 0:1]

    def get(group, behind):
        if group not in relayed:
            relay(group, behind)
        _, lands = copies_wait(_relay_plan, relayed[group], behind, name=f"gather_{group}_wait")
        nxt = order.index(group) + 1
        tok = relay(order[nxt], lands[0]) if nxt < len(order) else jnp.zeros((1, 1), F32)
        if group == "even_in":
            parts = zip(*[_unpack_rows(lands[1][s], [(CONV_K, 768), (1, 512), (1, 512), (1, 512)])
                          for s in range(N_SHARD)])
            conv, onorm, lng, lnb = [jnp.concatenate(p, axis=1) for p in parts]
            w_in = jnp.pad(lands[0].reshape(EVEN_IN, D_MODEL), ((0, EVEN_IN_PAD - EVEN_IN), (0, 0)))
            kept["odd_ln_g"] = lng
            return {"even_w_in": w_in, "even_conv": conv + tok, "odd_norm": onorm, "odd_ln_b": lnb}
        if group == "even_out":
            return {"even_w_out": lands[0].reshape(D_MODEL, D_MODEL), "onorm": even_onorm + tok}
        if group == "odd":
            return {"odd_w_in": lands[0], "odd_w_out": lands[1].reshape(D_MODEL, D_MODEL),
                    "odd_ln_g": kept["odd_ln_g"] + tok}
        return {"gate": lands[0], "up": lands[1], "down": lands[2].reshape(D_FF, D_MODEL), "tok": tok}

    rows4 =lambda a: a.reshape(N_SHARD, a.shape[0] // N_SHARD, a.shape[1])
    scattering, small = {}, {}

    def emit(group, grads):
        behind = even_norm
        if group == "even_in":
            small["local"] = grads["small"]
            small["all"] = behind = allgather_small(_pack_rows([grads["small"][n] for n in SMALL_LOCAL_GRADS]),
                                                    name="allgather_small")
            srcs = [grads["even_w_in"][:EVEN_IN].reshape(N_SHARD, EVEN_IN // N_SHARD, D_MODEL)]
        elif group == "even_out":
            srcs = [rows4(grads["even_w_out"])]
        elif group == "odd":
            srcs = [grads["odd_w_in"], rows4(grads["odd_w_out"])]
        else:
            srcs = [grads["gate"], grads["up"], rows4(grads["down"])]
        lands = [lax.empty((N_PEER,) + a.shape[1:], a.dtype) for a in srcs]
        scattering[group] = copies_start(_scatter_plan, srcs, lands, behind, name=f"scatter_{group}_start")
        return scattering[group]["token"][0:1, 0:1]

    pad816 = lambda a: jnp.pad(a, ((0, 0), (B_HEADS, 128 - 2 * B_HEADS)))
    w = {
        "even_norm": even_norm + after[0:1, 0:1],
        "a_log": pad816(even_a_log), "dt_bias": pad816(even_dt_bias),
        "sinks": sinks_pad,
        "onorm": even_onorm,
        "odd_w_s": odd_w_s[0],
        "odd_b_s": jnp.pad(odd_b_s[0].T, ((0, 0), (0, 128 - C_GROUPS))),
        "ffn_norm": ffn_norm,
        "final_norm": final_norm[None],
    }
    loss_l, grad_x = _local_step(x[0], loss_target[0], w, get, emit)
    loss = lax.psum(loss_l[0, 0], ("x", "y", "c"))

    me1 = me.reshape(1).astype(jnp.int32)
    swapping = {}

    def reduce_chips(group, behind):
        srcs, lands = copies_wait(_scatter_plan, scattering[group], behind, name=f"scatter_{group}_wait")
        partial = [sum_chips(srcs[i], me1, lands[i], name=f"sum_chips_{group}_{i}") for i in range(len(srcs))]
        swapping[group] = copies_start(_swap_plan, partial, [lax.empty(p.shape, p.dtype) for p in partial],
                                       even_norm, name=f"swap_{group}_start")
        return swapping[group]["token"]

    def swapped(group, behind):
        mine, theirs = copies_wait(_swap_plan, swapping[group], behind, name=f"swap_{group}_wait")
        return list(zip(mine, theirs))

    behind = scattering["even_in"]["token"]
    for group in ("ffn1", "ffn0", "odd", "even_out"):
        behind = reduce_chips(group, behind)
    sums = {group: swapped(group, behind) for group in ("ffn1", "ffn0", "odd", "even_out")}
    outs = {}
    parts_of = {"even_w_out": [sums["even_out"][0]], "odd_w_in": [sums["odd"][0]], "odd_w_out": [sums["odd"][1]],
                "ffn_w_gate": [sums["ffn0"][0], sums["ffn1"][0]], "ffn_w_up": [sums["ffn0"][1], sums["ffn1"][1]],
                "ffn_w_down": [sums["ffn0"][2], sums["ffn1"][2]]}
    for n in parts_of:
        outs[n] = adamw(parts_of[n], wl[n], ml[n], vl[n], name=f"adamw_{n}")
    behind = reduce_chips("even_in", outs["ffn_w_down"][1])
    flip = lambda a: jnp.transpose(a, (0, 2, 1))
    outs["even_w_in"] = [flip(o) for o in adamw([swapped("even_in", behind)[0]], flip(wl["even_w_in"]),
                                                flip(ml["even_w_in"]), flip(vl["even_w_in"]),
                                                name="adamw_even_w_in")]

    g = small["local"]
    small_sum = sum_devices(small["all"], name="sum_devices")
    sg = dict(zip(SMALL_LOCAL_GRADS, _unpack_rows(small_sum, [g[n].shape for n in SMALL_LOCAL_GRADS])))
    own_cols = lambda a, width: lax.dynamic_slice_in_dim(a, me * width, width, axis=a.ndim - 1)
    small_grads = {
        "even_norm": sg["even_norm"], "even_conv": own_cols(sg["even_conv"], 768)[None],
        "even_a_log": sg["a_log"][:, B_HEADS:2 * B_HEADS], "even_dt_bias": sg["dt_bias"][:, B_HEADS:2 * B_HEADS],
        "even_sinks": sg["sinks"][:, :A_HEADS], "even_onorm": sg["onorm"],
        "odd_norm": own_cols(sg["odd_norm"], 512), "odd_ln_g": own_cols(sg["odd_ln_g"], 512),
        "odd_ln_b": own_cols(sg["odd_ln_b"], 512), "odd_w_s": sg["odd_w_s"][None],
        "odd_b_s": sg["odd_b_s"][:, :C_GROUPS].T[None], "ffn_norm": sg["ffn_norm"], "final_norm": sg["final_norm"][0],
    }
    packed = [_pack_rows([d[n] for n in SMALL])[None] for d in (small_grads, wl, ml, vl)]
    small_out = adamw([(packed[0][0],)], packed[1], packed[2], packed[3], name="adamw_small")
    shapes = [wl[n].shape for n in SMALL]
    for j in range(4):
        for n, a in zip(SMALL, _unpack_rows(small_out[j][0], shapes)):
            outs.setdefault(n, [None] * 4)[j] = a

    return (loss, grad_x[None], *[outs[n][0] for n in WEIGHTS], *[outs[n][1] for n in WEIGHTS],
            *[outs[n][2] for n in WEIGHTS], *[outs[n][3] for n in WEIGHTS])
```

```python
import functools

import jax
import jax.numpy as jnp
from jax import lax
from jax.experimental import pallas as pl
from jax.experimental.pallas import tpu as pltpu

F32 = jnp.float32
BF16 = jnp.bfloat16
NEG_INF = float("-inf")

D_MODEL = 2048
A_HEADS, A_KV_HEADS, A_HEAD_DIM, WINDOW = 16, 2, 64, 128
B_HEADS, B_HEAD_DIM, CONV_K, DN_CHUNK = 8, 128, 4, 64
C_GROUPS, C_CHUNK = 8, 128
C_GROUP_DIM = D_MODEL // C_GROUPS
D_FF = 5632
EPS = 1e-6
A_Q = A_HEADS * A_HEAD_DIM
A_KV = A_KV_HEADS * A_HEAD_DIM
B_W = B_HEADS * B_HEAD_DIM
EVEN_IN = A_Q + 2 * A_KV + 4 * B_W + 2 * B_HEADS
EVEN_IN_PAD = 5632
COL_KV = A_Q
COL_QKVB = A_Q + 2 * A_KV
COL_Z = COL_QKVB + 3 * B_W
COL_GATE = COL_Z + B_W
N_SHARD = 4

ADAM_LR, ADAM_B1, ADAM_B2, ADAM_EPS, ADAM_WD, ADAM_STEP = 0.001, 0.9, 0.999, 1e-08, 0.01, 10

VMEM_LIMIT_V7X = 56 * 1024 * 1024
MXU_COLS = 256
MESH_ID = pl.DeviceIdType.MESH


def _params(sem=None):
    return pltpu.CompilerParams(dimension_semantics=sem, vmem_limit_bytes=VMEM_LIMIT_V7X)


def _sigmoid(x):
    return 1.0 / (1.0 + jnp.exp(-x))


def _silu(x):
    return x * _sigmoid(x)


def _dsilu(x):
    s = _sigmoid(x)
    return s * (1.0 + x * (1.0 - s))


def _gelu(x):
    return 0.5 * x * (1.0 + lax.erf(x * 0.7071067811865476))


def _dgelu(x):
    return 0.5 * (1.0 + lax.erf(x * 0.7071067811865476)) + x * jnp.exp(-0.5 * x * x) * 0.3989422804014327


def _dot(a, b, dims):
    if a.ndim == 3:
        (ca,), (cb,) = dims
        return lax.dot_general(a, b, (((ca + 1,), (cb + 1,)), ((0,), (0,))), preferred_element_type=F32)
    return lax.dot_general(a, b, (dims, ((), ())), preferred_element_type=F32)


NN = ((1,), (0,))
NT = ((1,), (1,))
TN = ((0,), (0,))


def _as3(b):
    return b if b.ndim == 3 else b[None]


def _accumulate(step, nsteps, accs, products, finish):
    if nsteps == 1:
        finish(products())
        return

    @pl.when(step == 0)
    def _():
        for acc, p in zip(accs, products()):
            acc[...] = p

    if nsteps > 2:
        @pl.when((step > 0) & (step < nsteps - 1))
        def _():
            for acc, p in zip(accs, products()):
                acc[...] += p

    @pl.when(step == nsteps - 1)
    def _():
        finish(tuple(acc[...] + p for acc, p in zip(accs, products())))


def mm_nn(a, b, *, tm, tn, tk, out_dtype, name, res=None):
    b3 = _as3(b)
    m, k = a.shape
    s, k2, ns = b3.shape
    assert k2 == k and m % tm == 0 and ns % tn == 0 and k % tk == 0, (a.shape, b3.shape, tm, tn, tk)
    nps, nk = ns // tn, k // tk

    def body(*refs):
        if res is None:
            a_ref, b_ref, o_ref, acc = refs
        else:
            a_ref, b_ref, r_ref, o_ref, acc = refs
        def finish(tiles):
            r = tiles[0] if res is None else tiles[0] + r_ref[...].astype(F32)
            o_ref[...] = r.astype(out_dtype)

        _accumulate(pl.program_id(2), nk, (acc,),
                    lambda: (_dot(a_ref[...].astype(BF16), b_ref[...].astype(BF16), NN),), finish)

    in_specs = [pl.BlockSpec((tm, tk), lambda i, j, kk: (i, kk)),
                pl.BlockSpec((None, tk, tn), lambda i, j, kk: (j // nps, kk, j % nps))]
    args = [a, b3]
    if res is not None:
        in_specs.append(pl.BlockSpec((tm, tn), lambda i, j, kk: (i, j)))
        args.append(res)
    return pl.pallas_call(
        body, name=name, grid=(m // tm, s * nps, nk), in_specs=in_specs,
        out_specs=pl.BlockSpec((tm, tn), lambda i, j, kk: (i, j)),
        out_shape=jax.ShapeDtypeStruct((m, s * ns), out_dtype),
        scratch_shapes=[pltpu.VMEM((tm, tn), F32)],
        compiler_params=_params(("parallel", "parallel", "arbitrary")))(*args)


def mm_nt(a, b, *, tm, tn, tk, out_dtype, name, res=None):
    b3 = _as3(b)
    m, n = a.shape
    s, k, ns = b3.shape
    assert n == s * ns and m % tm == 0 and k % tn == 0 and ns % tk == 0, (a.shape, b3.shape, tm, tn, tk)
    rps = ns // tk
    nr = s * rps

    def body(*refs):
        if res is None:
            a_ref, b_ref, o_ref, acc = refs
        else:
            a_ref, b_ref, r_ref, o_ref, acc = refs
        def finish(tiles):
            r = tiles[0] if res is None else tiles[0] + r_ref[...].astype(F32)
            o_ref[...] = r.astype(out_dtype)

        _accumulate(pl.program_id(2), nr, (acc,),
                    lambda: (_dot(a_ref[...].astype(BF16), b_ref[...].astype(BF16), NT),), finish)

    in_specs = [pl.BlockSpec((tm, tk), lambda i, j, r: (i, r)),
                pl.BlockSpec((None, tn, tk), lambda i, j, r: (r // rps, j, r % rps))]
    args = [a, b3]
    if res is not None:
        in_specs.append(pl.BlockSpec((tm, tn), lambda i, j, r: (i, j)))
        args.append(res)
    return pl.pallas_call(
        body, name=name, grid=(m // tm, k // tn, nr), in_specs=in_specs,
        out_specs=pl.BlockSpec((tm, tn), lambda i, j, r: (i, j)),
        out_shape=jax.ShapeDtypeStruct((m, k), out_dtype),
        scratch_shapes=[pltpu.VMEM((tm, tn), F32)],
        compiler_params=_params(("parallel", "parallel", "arbitrary")))(*args)


def mm_tn(a, b, *, shards, tm, tn, tk, out_dtype, name):
    m, k = a.shape
    m2, n = b.shape
    ns = n // shards
    assert m2 == m and n == shards * ns and m % tm == 0 and k % tk == 0 and ns % tn == 0, (a.shape, b.shape)
    nps, nm = ns // tn, m // tm

    def body(a_ref, b_ref, o_ref, acc):
        def finish(tiles):
            o_ref[...] = tiles[0].astype(out_dtype)

        _accumulate(pl.program_id(2), nm, (acc,),
                    lambda: (_dot(a_ref[...].astype(BF16), b_ref[...].astype(BF16), TN),), finish)

    return pl.pallas_call(
        body, name=name, grid=(k // tk, shards * nps, nm),
        in_specs=[pl.BlockSpec((tm, tk), lambda i, j, mi: (mi, i)),
                  pl.BlockSpec((tm, tn), lambda i, j, mi: (mi, j))],
        out_specs=pl.BlockSpec((None, tk, tn), lambda i, j, mi: (j // nps, i, j % nps)),
        out_shape=jax.ShapeDtypeStruct((shards, k, ns), out_dtype),
        scratch_shapes=[pltpu.VMEM((tk, tn), F32)],
        compiler_params=_params(("parallel", "parallel", "arbitrary")))(a, b)


def mm_gate_up(hn, wg, wu, *, tm, tn, tk, name):
    wg3, wu3 = _as3(wg), _as3(wu)
    m, k = hn.shape
    s, _, ns = wg3.shape
    assert m % tm == 0 and ns % tn == 0 and k % tk == 0
    nps, nk = ns // tn, k // tk

    def body(a_ref, g_ref, u_ref, og_ref, ou_ref, oa_ref, accg, accu):
        def products():
            a = a_ref[...].astype(BF16)
            return _dot(a, g_ref[...].astype(BF16), NN), _dot(a, u_ref[...].astype(BF16), NN)

        def finish(tiles):
            g, u = tiles
            og_ref[...] = g.astype(BF16)
            ou_ref[...] = u.astype(BF16)
            oa_ref[...] = (_silu(g) * u).astype(BF16)

        _accumulate(pl.program_id(2), nk, (accg, accu), products, finish)

    wspec = pl.BlockSpec((None, tk, tn), lambda i, j, kk: (j // nps, kk, j % nps))
    ospec = pl.BlockSpec((tm, tn), lambda i, j, kk: (i, j))
    osh = jax.ShapeDtypeStruct((m, s * ns), BF16)
    return pl.pallas_call(
        body, name=name, grid=(m // tm, s * nps, nk),
        in_specs=[pl.BlockSpec((tm, tk), lambda i, j, kk: (i, kk)), wspec, wspec],
        out_specs=[ospec, ospec, ospec], out_shape=[osh, osh, osh],
        scratch_shapes=[pltpu.VMEM((tm, tn) if nk > 1 else (8, 128), F32)] * 2,
        compiler_params=_params(("parallel", "parallel", "arbitrary")))(hn, wg3, wu3)


def mm_down_bwd(dh, wd, gate, up, *, tm, tn, tk, name):
    m, d = dh.shape
    f, d2 = wd.shape
    assert d2 == d and m % tm == 0 and f % tn == 0 and tk == d and tn % MXU_COLS == 0

    def body(a_ref, b_ref, g_ref, u_ref, og_ref, ou_ref):
        a = a_ref[...].astype(BF16)
        for jj in range(tn // MXU_COLS):
            sl = slice(jj * MXU_COLS, (jj + 1) * MXU_COLS)
            da = _dot(a, b_ref[sl, :].astype(BF16), NT)
            g, u = g_ref[:, sl].astype(F32), u_ref[:, sl].astype(F32)
            s = _sigmoid(g)
            og_ref[:, sl] = (da * u * (s * (1.0 + g * (1.0 - s)))).astype(BF16)
            ou_ref[:, sl] = (da * (g * s)).astype(BF16)

    ospec = pl.BlockSpec((tm, tn), lambda i, j: (i, j))
    osh = jax.ShapeDtypeStruct((m, f), BF16)
    return pl.pallas_call(
        body, name=name, grid=(m // tm, f // tn),
        in_specs=[pl.BlockSpec((tm, tk), lambda i, j: (i, 0)),
                  pl.BlockSpec((tn, tk), lambda i, j: (j, 0)), ospec, ospec],
        out_specs=[ospec, ospec], out_shape=[osh, osh],
        compiler_params=_params(("parallel", "parallel")))(dh, wd, gate, up)


ROWS = 256


def rms_fwd(x, g, *, name):
    t, d = x.shape

    def body(x_ref, g_ref, o_ref):
        xv = x_ref[...]
        r = lax.rsqrt(jnp.mean(xv * xv, axis=-1, keepdims=True) + EPS)
        o_ref[...] = (xv * r * g_ref[...]).astype(BF16)

    return pl.pallas_call(
        body, name=name, grid=(t // ROWS,),
        in_specs=[pl.BlockSpec((ROWS, d), lambda i: (i, 0)), pl.BlockSpec((1, d), lambda i: (0, 0))],
        out_specs=pl.BlockSpec((ROWS, d), lambda i: (i, 0)),
        out_shape=jax.ShapeDtypeStruct((t, d), BF16), compiler_params=_params(("parallel",)))(x, g)


def dgrad_rms_bwd(a, b, form, x, g, dres, *, tm, tk, name, res=None):
    m, d = x.shape
    b3 = _as3(b)
    if form == NN:
        steps = a.shape[1] // tk
        a_spec = pl.BlockSpec((tm, tk), lambda i, r: (i, r))
        b_spec = pl.BlockSpec((None, tk, d), lambda i, r: (0, r, 0))
    else:
        s, d2, ns = b3.shape
        assert d2 == d and ns % tk == 0
        rps = ns // tk
        steps = s * rps
        a_spec = pl.BlockSpec((tm, tk), lambda i, r: (i, r))
        b_spec = pl.BlockSpec((None, d, tk), lambda i, r: (r // rps, 0, r % rps))
    assert m % tm == 0 and a.shape[1] == steps * tk

    def body(*refs):
        if res is None:
            a_ref, b_ref, x_ref, g_ref, dr_ref, dx_ref, dg_ref, acc = refs
        else:
            a_ref, b_ref, r_ref, x_ref, g_ref, dr_ref, dx_ref, dg_ref, acc = refs

        @pl.when((pl.program_id(0) == 0) & (pl.program_id(1) == 0))
        def _():
            dg_ref[...] = jnp.zeros_like(dg_ref)

        def finish(tiles):
            dyv = tiles[0] if res is None else tiles[0] + r_ref[...]
            xv = x_ref[...]
            r = lax.rsqrt(jnp.mean(xv * xv, axis=-1, keepdims=True) + EPS)
            dyg = dyv * g_ref[...]
            dx_ref[...] = r * dyg - xv * (r * r * r) * jnp.mean(dyg * xv, axis=-1, keepdims=True) + dr_ref[...]
            dg_ref[...] += jnp.sum(dyv * xv * r, axis=0, keepdims=True)

        _accumulate(pl.program_id(1), steps, (acc,),
                    lambda: (_dot(a_ref[...].astype(BF16), b_ref[...].astype(BF16), form),), finish)

    row = pl.BlockSpec((tm, d), lambda i, r: (i, 0))
    vec = pl.BlockSpec((1, d), lambda i, r: (0, 0))
    in_specs = [a_spec, b_spec] + ([row] if res is not None else []) + [row, vec, row]
    args = [a, b3] + ([res] if res is not None else []) + [x, g, dres]
    return pl.pallas_call(
        body, name=name, grid=(m // tm, steps), in_specs=in_specs, out_specs=[row, vec],
        out_shape=[jax.ShapeDtypeStruct((m, d), F32), jax.ShapeDtypeStruct((1, d), F32)],
        scratch_shapes=[pltpu.VMEM((tm, d), F32)],
        compiler_params=_params(("arbitrary", "arbitrary")))(*args)


def loss_head(h, g, target, *, name):
    t, d = h.shape

    def body(x_ref, g_ref, t_ref, loss_ref, dx_ref, dg_ref):
        @pl.when(pl.program_id(0) == 0)
        def _():
            dg_ref[...] = jnp.zeros_like(dg_ref)
            loss_ref[...] = jnp.zeros_like(loss_ref)

        xv, gv = x_ref[...], g_ref[...]
        r = lax.rsqrt(jnp.mean(xv * xv, axis=-1, keepdims=True) + EPS)
        e = xv * r * gv - t_ref[...]
        loss_ref[...] += 0.5 * jnp.sum(jnp.mean(e * e, axis=-1, keepdims=True), axis=0, keepdims=True)
        dyv = e * (1.0 / d)
        dyg = dyv * gv
        dx_ref[...] = r * dyg - xv * (r * r * r) * jnp.mean(dyg * xv, axis=-1, keepdims=True)
        dg_ref[...] += jnp.sum(dyv * xv * r, axis=0, keepdims=True)

    row = pl.BlockSpec((ROWS, d), lambda i: (i, 0))
    vec = pl.BlockSpec((1, d), lambda i: (0, 0))
    return pl.pallas_call(
        body, name=name, grid=(t // ROWS,), in_specs=[row, vec, row],
        out_specs=[pl.BlockSpec((1, 128), lambda i: (0, 0)), row, vec],
        out_shape=[jax.ShapeDtypeStruct((1, 128), F32), jax.ShapeDtypeStruct((t, d), F32),
                   jax.ShapeDtypeStruct((1, d), F32)],
        compiler_params=_params(("arbitrary",)))(h, g, target)


def _tril_mask():
    r = lax.broadcasted_iota(jnp.int32, (C_CHUNK, C_CHUNK), 0)
    c = lax.broadcasted_iota(jnp.int32, (C_CHUNK, C_CHUNK), 1)
    return r >= c


def _layer_norm_parts(v):
    mu = jnp.mean(v, axis=-1, keepdims=True)
    vc = v - mu
    rstd = lax.rsqrt(jnp.mean(vc * vc, axis=-1, keepdims=True) + EPS)
    return vc * rstd, rstd


def gmlp_fwd(zpre, ln_g, ln_b, ws, bs_t, *, name):
    t = zpre.shape[0]
    d = D_MODEL

    def body(zu_ref, zv_ref, g_ref, b_ref, ws_ref, bs_ref, o_ref):
        u = _gelu(zu_ref[...])
        vhat, _ = _layer_norm_parts(_gelu(zv_ref[...]))
        vln = (vhat * g_ref[...] + b_ref[...]).astype(BF16)
        mask = _tril_mask()
        for gi in range(C_GROUPS):
            sl = slice(gi * C_GROUP_DIM, (gi + 1) * C_GROUP_DIM)
            w = jnp.where(mask, ws_ref[gi], 0.0).astype(BF16)
            mixed = _dot(w, vln[:, sl], NN) + bs_ref[:, gi:gi + 1]
            o_ref[:, sl] = (u[:, sl] * mixed).astype(BF16)

    vec = pl.BlockSpec((1, d), lambda i: (0, 0))
    return pl.pallas_call(
        body, name=name, grid=(t // C_CHUNK,),
        in_specs=[pl.BlockSpec((C_CHUNK, d), lambda i: (i, 0)), pl.BlockSpec((C_CHUNK, d), lambda i: (i, 1)),
                  vec, vec, pl.BlockSpec((C_GROUPS, C_CHUNK, C_CHUNK), lambda i: (0, 0, 0)),
                  pl.BlockSpec((C_CHUNK, 128), lambda i: (0, 0))],
        out_specs=pl.BlockSpec((C_CHUNK, d), lambda i: (i, 0)),
        out_shape=jax.ShapeDtypeStruct((t, d), BF16), compiler_params=_params(("parallel",)))(
            zpre, zpre, ln_g, ln_b, ws, bs_t)


def gmlp_bwd(zpre, dgated, ln_g, ln_b, ws, bs_t, *, name):
    t = zpre.shape[0]
    d = D_MODEL

    def body(zu_ref, zv_ref, dg_ref, g_ref, b_ref, ws_ref, bs_ref, dz_ref, dws_ref, dbs_ref, dlg_ref, dlb_ref):
        @pl.when(pl.program_id(0) == 0)
        def _():
            dws_ref[...] = jnp.zeros_like(dws_ref)
            dbs_ref[...] = jnp.zeros_like(dbs_ref)
            dlg_ref[...] = jnp.zeros_like(dlg_ref)
            dlb_ref[...] = jnp.zeros_like(dlb_ref)

        zu, zv = zu_ref[...], zv_ref[...]
        u = _gelu(zu)
        vhat, rstd = _layer_norm_parts(_gelu(zv))
        gam = g_ref[...]
        vln = (vhat * gam + b_ref[...]).astype(BF16)
        dgt = dg_ref[...].astype(F32)
        mask = _tril_mask()
        lane = lax.broadcasted_iota(jnp.int32, (C_CHUNK, 128), 1)
        dbs = jnp.zeros((C_CHUNK, 128), F32)
        du_parts, dvln_parts = [], []
        for gi in range(C_GROUPS):
            sl = slice(gi * C_GROUP_DIM, (gi + 1) * C_GROUP_DIM)
            w = jnp.where(mask, ws_ref[gi], 0.0).astype(BF16)
            mixed = _dot(w, vln[:, sl], NN) + bs_ref[:, gi:gi + 1]
            du_parts.append(dgt[:, sl] * mixed)
            dmixed = dgt[:, sl] * u[:, sl]
            dmb = dmixed.astype(BF16)
            dws_ref[gi] += jnp.where(mask, _dot(dmb, vln[:, sl], NT), 0.0)
            dbs = dbs + jnp.where(lane == gi, jnp.sum(dmixed, axis=-1, keepdims=True), 0.0)
            dvln_parts.append(_dot(w, dmb, TN))
        dbs_ref[...] += dbs
        du = jnp.concatenate(du_parts, axis=-1)
        dvln = jnp.concatenate(dvln_parts, axis=-1)
        dlg_ref[...] += jnp.sum(dvln * vhat, axis=0, keepdims=True)
        dlb_ref[...] += jnp.sum(dvln, axis=0, keepdims=True)
        dvhat = dvln * gam
        dv = rstd * (dvhat - jnp.mean(dvhat, axis=-1, keepdims=True)
                     - vhat * jnp.mean(dvhat * vhat, axis=-1, keepdims=True))
        dz_ref[:, :d] = (du * _dgelu(zu)).astype(BF16)
        dz_ref[:, d:] = (dv * _dgelu(zv)).astype(BF16)

    vec = pl.BlockSpec((1, d), lambda i: (0, 0))
    wsp = pl.BlockSpec((C_GROUPS, C_CHUNK, C_CHUNK), lambda i: (0, 0, 0))
    bsp = pl.BlockSpec((C_CHUNK, 128), lambda i: (0, 0))
    return pl.pallas_call(
        body, name=name, grid=(t // C_CHUNK,),
        in_specs=[pl.BlockSpec((C_CHUNK, d), lambda i: (i, 0)), pl.BlockSpec((C_CHUNK, d), lambda i: (i, 1)),
                  pl.BlockSpec((C_CHUNK, d), lambda i: (i, 0)), vec, vec, wsp, bsp],
        out_specs=[pl.BlockSpec((C_CHUNK, 2 * d), lambda i: (i, 0)), wsp, bsp, vec, vec],
        out_shape=[jax.ShapeDtypeStruct((t, 2 * d), BF16), jax.ShapeDtypeStruct((C_GROUPS, C_CHUNK, C_CHUNK), F32),
                   jax.ShapeDtypeStruct((C_CHUNK, 128), F32), jax.ShapeDtypeStruct((1, d), F32),
                   jax.ShapeDtypeStruct((1, d), F32)],
        compiler_params=_params(("arbitrary",)))(zpre, zpre, dgated, ln_g, ln_b, ws, bs_t)


ATT_SCALE = A_HEAD_DIM ** -0.5
PAIRS = A_HEADS // 2
PAIRS_PER_KV = PAIRS // A_KV_HEADS


def _att_padded(tile):
    lo = lax.broadcasted_iota(jnp.int32, tile.shape, 1) < A_HEAD_DIM
    rolled = pltpu.roll(tile, A_HEAD_DIM, 1)
    zero = jnp.zeros_like(tile)
    return {(0, 0): jnp.where(lo, tile, zero).astype(BF16), (0, 1): jnp.where(lo, zero, rolled).astype(BF16),
            (1, 0): jnp.where(lo, rolled, zero).astype(BF16), (1, 1): jnp.where(lo, zero, tile).astype(BF16)}


def _att_valid(n):
    r = lax.broadcasted_iota(jnp.int32, (WINDOW, 2 * WINDOW), 0)
    c = lax.broadcasted_iota(jnp.int32, (WINDOW, 2 * WINDOW), 1)
    rel = r + WINDOW - c
    return (rel >= 0) & (rel < WINDOW) & ((c >= WINDOW) | (n > 0))


def _att_probs(qp, kpad, sink, valid):
    s = jnp.where(valid, _dot(qp, kpad, NT), NEG_INF)
    m = jnp.maximum(jnp.max(s, axis=-1, keepdims=True), sink)
    p = jnp.exp(s - m)
    e_sink = jnp.exp(sink - m)
    inv = 1.0 / (jnp.sum(p, axis=-1, keepdims=True) + e_sink)
    return p * inv, e_sink * inv


def _att_operands(q_ref, kvc_ref, kvp_ref, s_ref):
    kv = jnp.concatenate([kvp_ref[...], kvc_ref[...]], axis=0)
    kpad, vpad = _att_padded(kv[:, :128]), _att_padded(kv[:, 128:])
    key = lambda h: ((h // 2) // PAIRS_PER_KV, h % 2)
    pairs = [(q_ref[:, j * 128:(j + 1) * 128] * ATT_SCALE).astype(BF16) for j in range(PAIRS)]
    q = jnp.stack([pairs[h // 2] for h in range(A_HEADS)])
    k = jnp.stack([kpad[key(h)] for h in range(A_HEADS)])
    v = jnp.stack([vpad[key(h)] for h in range(A_HEADS)])
    sink = jnp.stack([s_ref[:, h:h + 1] for h in range(A_HEADS)])
    return q, k, v, sink


def _att_specs(t):
    return [pl.BlockSpec((WINDOW, A_Q), lambda n: (n, 0)),
            pl.BlockSpec((WINDOW, 2 * A_KV), lambda n: (n, COL_KV // (2 * A_KV))),
            pl.BlockSpec((WINDOW, 2 * A_KV), lambda n: (jnp.maximum(n - 1, 0), COL_KV // (2 * A_KV))),
            pl.BlockSpec((1, 128), lambda n: (0, 0))]


def att_fwd(proj, sinks, *, name):
    t = proj.shape[0]

    def body(q_ref, kvc_ref, kvp_ref, s_ref, o_ref):
        n = pl.program_id(0)
        q, k, v, sink = _att_operands(q_ref, kvc_ref, kvp_ref, s_ref)
        w, _ = _att_probs(q, k, sink, _att_valid(n))
        o = _dot(w.astype(BF16), v, NN)
        for j in range(PAIRS):
            o_ref[:, j * 128:(j + 1) * 128] = (o[2 * j] + o[2 * j + 1]).astype(BF16)

    return pl.pallas_call(
        body, name=name, grid=(t // WINDOW,), in_specs=_att_specs(t),
        out_specs=pl.BlockSpec((WINDOW, A_Q), lambda n: (n, 0)),
        out_shape=jax.ShapeDtypeStruct((t, A_Q), BF16), compiler_params=_params(("parallel",)))(
            proj, proj, proj, sinks)


def att_bwd(proj, sinks, dout, *, name):
    t = proj.shape[0]

    def body(q_ref, kvc_ref, kvp_ref, s_ref, do_ref, dq_ref, dkc_ref, dkp_ref, ds_ref):
        n = pl.program_id(0)

        @pl.when(n == 0)
        def _():
            ds_ref[...] = jnp.zeros_like(ds_ref)

        q, k, v, sink = _att_operands(q_ref, kvc_ref, kvp_ref, s_ref)
        dop = jnp.stack([do_ref[:, (h // 2) * 128:(h // 2 + 1) * 128] for h in range(A_HEADS)]).astype(BF16)
        w, w_sink = _att_probs(q, k, sink, _att_valid(n))
        dw = _dot(dop, v, NT)
        delta = jnp.sum(w * dw, axis=-1, keepdims=True)
        dsc = (w * (dw - delta)).astype(BF16)
        dsink_h = -jnp.sum(w_sink * delta, axis=1, keepdims=True)
        dq = _dot(dsc, k, NN)
        dk_h = _dot(dsc, q, TN)
        dv_h = _dot(w.astype(BF16), dop, TN)
        lane = lax.broadcasted_iota(jnp.int32, (1, 128), 1)
        dsink = jnp.zeros((1, 128), F32)
        for h in range(A_HEADS):
            dsink = dsink + jnp.where(lane == h, dsink_h[h], 0.0)
        ds_ref[...] += dsink
        for j in range(PAIRS):
            dq_ref[:, j * 128:(j + 1) * 128] = ((dq[2 * j] + dq[2 * j + 1]) * ATT_SCALE).astype(BF16)
        lo = lax.broadcasted_iota(jnp.int32, (2 * WINDOW, 128), 1) < A_HEAD_DIM
        heads_per_kv = A_HEADS // A_KV_HEADS

        def tile(per_head):
            acc = {}
            for kvh in range(A_KV_HEADS):
                for half in range(2):
                    hs = range(kvh * heads_per_kv + half, (kvh + 1) * heads_per_kv, 2)
                    acc[(kvh, half)] = functools.reduce(lambda a, b: a + b, [per_head[h] for h in hs])
            return jnp.where(lo, acc[(0, 0)] + pltpu.roll(acc[(0, 1)], A_HEAD_DIM, 1),
                             pltpu.roll(acc[(1, 0)], A_HEAD_DIM, 1) + acc[(1, 1)])

        dkv = jnp.concatenate([tile(dk_h), tile(dv_h)], axis=1)
        dkp_ref[...] = dkv[:WINDOW]
        dkc_ref[...] = dkv[WINDOW:]

    kvo = pl.BlockSpec((WINDOW, 2 * A_KV), lambda n: (n, 0))
    return pl.pallas_call(
        body, name=name, grid=(t // WINDOW,),
        in_specs=_att_specs(t) + [pl.BlockSpec((WINDOW, A_Q), lambda n: (n, 0))],
        out_specs=[pl.BlockSpec((WINDOW, A_Q), lambda n: (n, 0)), kvo, kvo, pl.BlockSpec((1, 128), lambda n: (0, 0))],
        out_shape=[jax.ShapeDtypeStruct((t, A_Q), BF16), jax.ShapeDtypeStruct((t, 2 * A_KV), F32),
                   jax.ShapeDtypeStruct((t, 2 * A_KV), F32), jax.ShapeDtypeStruct((1, 128), F32)],
        compiler_params=_params(("arbitrary",)))(proj, proj, proj, sinks, dout)


QK_SCALE = B_HEAD_DIM ** -0.5
PREP_COLS = 256
PREP_NCB = 3 * B_W // PREP_COLS
HALO = 8
PREP_ROWS = 512


def _roll_rows(x, shift):
    n = x.shape[0]
    return x if shift % n == 0 else pltpu.roll(x, shift % n, 0)


def _conv_taps(xe, w):
    xs = [_roll_rows(xe, CONV_K - 1 - i) for i in range(CONV_K)]
    c = w[0:1] * xs[0]
    for i in range(1, CONV_K):
        c = c + w[i:i + 1] * xs[i]
    return xs, c


def dprep_fwd(proj, conv_w, *, name):
    t = proj.shape[0]
    tt = min(PREP_ROWS, t)
    col0 = COL_QKVB // PREP_COLS

    def body(x_ref, h_ref, w_ref, o_ref):
        cb, n = pl.program_id(0), pl.program_id(1)
        halo = jnp.where(n > 0, h_ref[...], 0.0)
        xe = jnp.concatenate([halo, x_ref[...]], axis=0)
        _, c = _conv_taps(xe, w_ref[...])
        y = _silu(c)[HALO:]
        parts = []
        for hh in range(PREP_COLS // B_HEAD_DIM):
            yh = y[:, hh * B_HEAD_DIM:(hh + 1) * B_HEAD_DIM]
            parts.append(yh * lax.rsqrt(jnp.sum(yh * yh, axis=-1, keepdims=True) + EPS))
        nrm = jnp.concatenate(parts, axis=-1)
        o_ref[...] = jnp.where(cb < 4, nrm * QK_SCALE, jnp.where(cb < 8, nrm, y))

    return pl.pallas_call(
        body, name=name, grid=(PREP_NCB, t // tt),
        in_specs=[pl.BlockSpec((tt, PREP_COLS), lambda cb, n: (n, col0 + cb)),
                  pl.BlockSpec((HALO, PREP_COLS), lambda cb, n: (jnp.maximum(n * (tt // HALO) - 1, 0), col0 + cb)),
                  pl.BlockSpec((CONV_K, PREP_COLS), lambda cb, n: (0, cb))],
        out_specs=pl.BlockSpec((tt, PREP_COLS), lambda cb, n: (n, cb)),
        out_shape=jax.ShapeDtypeStruct((t, 3 * B_W), F32), compiler_params=_params(("parallel", "parallel")))(
            proj, proj, conv_w)


def dprep_bwd(proj, conv_w, dqkvn, *, name):
    t = proj.shape[0]
    tt = min(PREP_ROWS, t)
    nb = t // tt
    col0 = COL_QKVB // PREP_COLS
    n8 = t // HALO

    def body(xc_ref, xb_ref, xa_ref, dc_ref, da_ref, w_ref, dx_ref, dw_ref):
        cb, n = pl.program_id(0), pl.program_id(1)

        @pl.when(n == 0)
        def _():
            dw_ref[...] = jnp.zeros_like(dw_ref)

        w = w_ref[...]
        xe = jnp.concatenate([jnp.where(n > 0, xb_ref[...], 0.0), xc_ref[...], xa_ref[...]], axis=0)
        xs, c = _conv_taps(xe, w)
        sg = _sigmoid(c)
        y = c * sg
        dout = jnp.concatenate([jnp.zeros((HALO, PREP_COLS), F32), dc_ref[...],
                                jnp.where(n < nb - 1, da_ref[...], 0.0)], axis=0)
        dsc = jnp.where(cb < 4, QK_SCALE, 1.0)
        parts = []
        for hh in range(PREP_COLS // B_HEAD_DIM):
            sl = slice(hh * B_HEAD_DIM, (hh + 1) * B_HEAD_DIM)
            yh, doh = y[:, sl], dout[:, sl] * dsc
            r = lax.rsqrt(jnp.sum(yh * yh, axis=-1, keepdims=True) + EPS)
            parts.append(doh * r - yh * (r * r * r) * jnp.sum(doh * yh, axis=-1, keepdims=True))
        dy = jnp.where(cb < 8, jnp.concatenate(parts, axis=-1), dout)
        dcv = dy * sg * (1.0 + c * (1.0 - sg))
        dxe = w[CONV_K - 1:CONV_K] * dcv
        for i in range(CONV_K - 1):
            dxe = dxe + w[i:i + 1] * _roll_rows(dcv, -(CONV_K - 1 - i))
        dx_ref[...] = dxe[HALO:HALO + tt].astype(BF16)
        for i in range(CONV_K):
            dw_ref[i:i + 1, :] += jnp.sum((dcv * xs[i])[HALO:HALO + tt], axis=0, keepdims=True)

    def after(n):
        return jnp.minimum((n + 1) * (tt // HALO), n8 - 1)

    return pl.pallas_call(
        body, name=name, grid=(PREP_NCB, nb),
        in_specs=[pl.BlockSpec((tt, PREP_COLS), lambda cb, n: (n, col0 + cb)),
                  pl.BlockSpec((HALO, PREP_COLS), lambda cb, n: (jnp.maximum(n * (tt // HALO) - 1, 0), col0 + cb)),
                  pl.BlockSpec((HALO, PREP_COLS), lambda cb, n: (after(n), col0 + cb)),
                  pl.BlockSpec((tt, PREP_COLS), lambda cb, n: (n, cb)),
                  pl.BlockSpec((HALO, PREP_COLS), lambda cb, n: (after(n), cb)),
                  pl.BlockSpec((CONV_K, PREP_COLS), lambda cb, n: (0, cb))],
        out_specs=[pl.BlockSpec((tt, PREP_COLS), lambda cb, n: (n, cb)),
                   pl.BlockSpec((CONV_K, PREP_COLS), lambda cb, n: (0, cb))],
        out_shape=[jax.ShapeDtypeStruct((t, 3 * B_W), BF16), jax.ShapeDtypeStruct((CONV_K, 3 * B_W), F32)],
        compiler_params=_params(("parallel", "arbitrary")))(proj, proj, proj, dqkvn, dqkvn, conv_w)


def _softplus(z):
    return jnp.maximum(z, 0.0) + jnp.log(1.0 + jnp.exp(-jnp.abs(z)))


def gates_fwd(proj, alog_pad, dtb_pad, *, name):
    t = proj.shape[0]

    def body(x_ref, a_ref, b_ref, o_ref):
        raw = x_ref[...]
        lane = lax.broadcasted_iota(jnp.int32, raw.shape, 1)
        g = -jnp.exp(a_ref[...]) * _softplus(raw + b_ref[...])
        o_ref[...] = jnp.where(lane < B_HEADS, _sigmoid(raw), jnp.where(lane < 2 * B_HEADS, g, 0.0))

    vec = pl.BlockSpec((1, 128), lambda n: (0, 0))
    return pl.pallas_call(
        body, name=name, grid=(t // ROWS,),
        in_specs=[pl.BlockSpec((ROWS, 128), lambda n: (n, COL_GATE // 128)), vec, vec],
        out_specs=pl.BlockSpec((ROWS, 128), lambda n: (n, 0)),
        out_shape=jax.ShapeDtypeStruct((t, 128), F32), compiler_params=_params(("parallel",)))(
            proj, alog_pad, dtb_pad)


def gates_bwd(proj, alog_pad, dtb_pad, dgates, *, name):
    t = proj.shape[0]

    def body(x_ref, a_ref, b_ref, dg_ref, dx_ref, da_ref, db_ref):
        @pl.when(pl.program_id(0) == 0)
        def _():
            da_ref[...] = jnp.zeros_like(da_ref)
            db_ref[...] = jnp.zeros_like(db_ref)

        raw, dgt = x_ref[...], dg_ref[...]
        lane = lax.broadcasted_iota(jnp.int32, raw.shape, 1)
        is_beta, is_g = lane < B_HEADS, (lane >= B_HEADS) & (lane < 2 * B_HEADS)
        beta = _sigmoid(raw)
        z = raw + b_ref[...]
        neg_a = -jnp.exp(a_ref[...])
        d_z = jnp.where(is_g, dgt * neg_a * _sigmoid(z), 0.0)
        dx_ref[...] = jnp.where(is_beta, dgt * beta * (1.0 - beta), d_z).astype(BF16)
        db_ref[...] += jnp.sum(d_z, axis=0, keepdims=True)
        da_ref[...] += jnp.sum(jnp.where(is_g, dgt * neg_a * _softplus(z), 0.0), axis=0, keepdims=True)

    vec = pl.BlockSpec((1, 128), lambda n: (0, 0))
    row = pl.BlockSpec((ROWS, 128), lambda n: (n, 0))
    return pl.pallas_call(
        body, name=name, grid=(t // ROWS,),
        in_specs=[pl.BlockSpec((ROWS, 128), lambda n: (n, COL_GATE // 128)), vec, vec, row],
        out_specs=[row, vec, vec],
        out_shape=[jax.ShapeDtypeStruct((t, 128), BF16), jax.ShapeDtypeStruct((1, 128), F32),
                   jax.ShapeDtypeStruct((1, 128), F32)],
        compiler_params=_params(("arbitrary",)))(proj, alog_pad, dtb_pad, dgates)


def _split2(a):
    hi = a.astype(BF16)
    return hi, (a - hi.astype(F32)).astype(BF16)


def _dotp(a, b, dims, passes):
    if passes == 1:
        return _dot(a.astype(BF16), b.astype(BF16), dims)
    ah, al = _split2(a)
    bh, bl = _split2(b)
    return _dot(ah, bh, dims) + (_dot(ah, bl, dims) + _dot(al, bh, dims))


_GRAD_DIMS = {NN: ((NT, False), (TN, False)), NT: ((NN, False), (TN, True)), TN: ((NT, True), (NN, False))}


def _make_mm(dims, passes, grad_passes):
    (da_dims, da_swap), (db_dims, db_swap) = _GRAD_DIMS[dims]

    @jax.custom_vjp
    def mm(a, b):
        return _dotp(a, b, dims, passes)

    def fwd(a, b):
        return _dotp(a, b, dims, passes), (a, b)

    def bwd(saved, ct):
        a, b = saved
        da = _dotp(b, ct, da_dims, grad_passes) if da_swap else _dotp(ct, b, da_dims, grad_passes)
        db = _dotp(ct, a, db_dims, grad_passes) if db_swap else _dotp(a, ct, db_dims, grad_passes)
        return da, db

    mm.defvjp(fwd, bwd)
    return mm


MM1 = {d: _make_mm(d, 1, 1) for d in (NN, NT, TN)}
MM3 = {d: _make_mm(d, 3, 1) for d in (NN, NT, TN)}


def _neumann_value(n):
    c = n.shape[-1]
    eye = (lax.broadcasted_iota(jnp.int32, (c, c), 0) == lax.broadcasted_iota(jnp.int32, (c, c), 1)).astype(F32)
    inv, pw = eye + n, n
    for _ in range(5):
        pw = _dotp(pw, pw, NN, 3)
        inv = inv + _dotp(inv, pw, NN, 3)
    return inv


@jax.custom_vjp
def _neumann_inverse(n):
    return _neumann_value(n)


def _neumann_fwd(n):
    inv = _neumann_value(n)
    return inv, inv


def _neumann_bwd(inv, ct):
    return (_dotp(_dotp(inv, ct, TN, 1), inv, NT, 1),)


_neumann_inverse.defvjp(_neumann_fwd, _neumann_bwd)


def _tri_ones(lower):
    r = lax.broadcasted_iota(jnp.int32, (DN_CHUNK, DN_CHUNK), 0)
    c = lax.broadcasted_iota(jnp.int32, (DN_CHUNK, DN_CHUNK), 1)
    return (r >= c if lower else r <= c).astype(BF16)


def _tri_sum(x, lower):
    tri = _tri_ones(lower)
    hi = x.astype(BF16)
    r1 = x - hi.astype(F32)
    mid = r1.astype(BF16)
    lo = (r1 - mid.astype(F32)).astype(BF16)
    return _dot(tri, hi, NN) + (_dot(tri, mid, NN) + _dot(tri, lo, NN))


def _delta_chunk(s0, q, k, v, beta, gam_c, gam_r):
    c = DN_CHUNK
    r = lax.broadcasted_iota(jnp.int32, (c, c), 0)
    cc = lax.broadcasted_iota(jnp.int32, (c, c), 1)
    incl, strict = r >= cc, r > cc
    decay = jnp.exp(jnp.where(incl, gam_c - gam_r, NEG_INF))
    g_last = gam_c[:, c - 1:c, :]
    e_gam, e_rest, e_last = jnp.exp(gam_c), jnp.exp(g_last - gam_c), jnp.exp(g_last)
    a_neg = -jnp.where(strict, beta * MM1[NT](k, k) * decay, 0.0)
    inv = _neumann_inverse(a_neg)
    uw = MM3[NN](inv,jnp.concatenate([v * beta, k * (beta * e_gam)], axis=-1))
    u, w = uw[..., :B_HEAD_DIM], uw[..., B_HEAD_DIM:]
    qk = MM1[NT](q, k) * decay
    v_new = u - MM1[NN](w, s0)
    o = MM1[NN](q * e_gam, s0) + MM1[NN](qk, v_new)
    s1 = s0 * e_last + MM1[TN](k * e_rest, v_new)
    return s1, o


def _delta_operands(q_ref, k_ref, v_ref, gt):
    heads = lambda ref: jnp.stack([ref[:, h * B_HEAD_DIM:(h + 1) * B_HEAD_DIM] for h in range(B_HEADS)])
    gam = _tri_sum(gt, True)
    gam_t = gam.T
    beta = jnp.stack([gt[:, h:h + 1] for h in range(B_HEADS)])
    gam_c = jnp.stack([gam[:, B_HEADS + h:B_HEADS + h + 1] for h in range(B_HEADS)])
    gam_r = jnp.stack([gam_t[B_HEADS + h:B_HEADS + h + 1, :] for h in range(B_HEADS)])
    return heads(q_ref), heads(k_ref), heads(v_ref), beta, gam_c, gam_r


def delta_fwd(qkvn, gates, *, name):
    t = qkvn.shape[0]
    nc = t // DN_CHUNK

    def body(q_ref, k_ref, v_ref, g_ref, o_ref, ss_ref, state):
        @pl.when(pl.program_id(0) == 0)
        def _():
            state[...] = jnp.zeros_like(state)

        s0 = state[...]
        ss_ref[...] = s0
        s1, o = _delta_chunk(s0, *_delta_operands(q_ref, k_ref, v_ref, g_ref[...]))
        state[...] = s1
        for h in range(B_HEADS):
            o_ref[:, h * B_HEAD_DIM:(h + 1) * B_HEAD_DIM] = o[h]

    blk = lambda j: pl.BlockSpec((DN_CHUNK, B_W), lambda n: (n, j))
    return pl.pallas_call(
        body, name=name, grid=(nc,),
        in_specs=[blk(0), blk(1), blk(2), pl.BlockSpec((DN_CHUNK, 128), lambda n: (n, 0))],
        out_specs=[blk(0), pl.BlockSpec((None, B_HEADS, B_HEAD_DIM, B_HEAD_DIM), lambda n: (n, 0, 0, 0))],
        out_shape=[jax.ShapeDtypeStruct((t, B_W), F32),
                   jax.ShapeDtypeStruct((nc, B_HEADS, B_HEAD_DIM, B_HEAD_DIM), F32)],
        scratch_shapes=[pltpu.VMEM((B_HEADS, B_HEAD_DIM, B_HEAD_DIM), F32)],
        compiler_params=_params(("arbitrary",)))(qkvn, qkvn, qkvn, gates)


def delta_bwd(qkvn, gates, ssave, do, *, name):
    t = qkvn.shape[0]
    nc = t // DN_CHUNK

    def body(q_ref, k_ref, v_ref, g_ref, ss_ref, do_ref, dx_ref, dg_ref, dstate):
        @pl.when(pl.program_id(0) == 0)
        def _():
            dstate[...] = jnp.zeros_like(dstate)

        lane = lax.broadcasted_iota(jnp.int32, (DN_CHUNK, 128), 1)
        row = lax.broadcasted_iota(jnp.int32, (128, DN_CHUNK), 0)
        dbeta_all = jnp.zeros((DN_CHUNK, 128), F32)
        dgam_c_all = jnp.zeros((DN_CHUNK, 128), F32)
        dgam_r_all = jnp.zeros((128, DN_CHUNK), F32)
        _, vjp = jax.vjp(_delta_chunk, ss_ref[...], *_delta_operands(q_ref, k_ref, v_ref, g_ref[...]))
        do = jnp.stack([do_ref[:, h * B_HEAD_DIM:(h + 1) * B_HEAD_DIM] for h in range(B_HEADS)])
        ds0, dq, dk, dv, dbeta, dgam_c, dgam_r = vjp((dstate[...], do))
        dstate[...] = ds0
        for h in range(B_HEADS):
            dx_ref[:, h * B_HEAD_DIM:(h + 1) * B_HEAD_DIM] = dq[h]
            dx_ref[:, B_W + h * B_HEAD_DIM:B_W + (h + 1) * B_HEAD_DIM] = dk[h]
            dx_ref[:, 2 * B_W + h * B_HEAD_DIM:2 * B_W + (h + 1) * B_HEAD_DIM] = dv[h]
            dbeta_all = dbeta_all + jnp.where(lane == h, dbeta[h], 0.0)
            dgam_c_all = dgam_c_all + jnp.where(lane == B_HEADS + h, dgam_c[h], 0.0)
            dgam_r_all = dgam_r_all + jnp.where(row == B_HEADS + h, dgam_r[h], 0.0)
        dg_ref[...] = dbeta_all + _tri_sum(dgam_c_all + dgam_r_all.T, False)

    blk = lambda j: pl.BlockSpec((DN_CHUNK, B_W), lambda n: (nc - 1 - n, j))
    gsp = pl.BlockSpec((DN_CHUNK, 128), lambda n: (nc - 1 - n, 0))
    return pl.pallas_call(
        body, name=name, grid=(nc,),
        in_specs=[blk(0), blk(1), blk(2), gsp,
                  pl.BlockSpec((None, B_HEADS, B_HEAD_DIM, B_HEAD_DIM), lambda n: (nc - 1 - n, 0, 0, 0)), blk(0)],
        out_specs=[pl.BlockSpec((DN_CHUNK, 3 * B_W), lambda n: (nc - 1 - n, 0)), gsp],
        out_shape=[jax.ShapeDtypeStruct((t, 3 * B_W), F32), jax.ShapeDtypeStruct((t, 128), F32)],
        scratch_shapes=[pltpu.VMEM((B_HEADS, B_HEAD_DIM, B_HEAD_DIM), F32)],
        compiler_params=_params(("arbitrary",)))(qkvn, qkvn, qkvn, gates, ssave, do)


GNORM_ROWS = 1024


def gnorm_fwd(o, proj, onorm, *, name):
    t = o.shape[0]

    def body(o_ref, z_ref, w_ref, out_ref):
        ov = o_ref[...]
        r = lax.rsqrt(jnp.mean(ov * ov, axis=-1, keepdims=True) + EPS)
        out_ref[...] = (ov * r * w_ref[...] * _silu(z_ref[...])).astype(BF16)

    rows = min(GNORM_ROWS, t)
    blk = pl.BlockSpec((rows, B_HEAD_DIM), lambda n, h: (n, h))
    return pl.pallas_call(
        body, name=name, grid=(t // rows, B_HEADS),
        in_specs=[blk, pl.BlockSpec((rows, B_HEAD_DIM), lambda n, h: (n, COL_Z // B_HEAD_DIM + h)),
                  pl.BlockSpec((1, B_HEAD_DIM), lambda n, h: (0, 0))],
        out_specs=blk, out_shape=jax.ShapeDtypeStruct((t, B_W), BF16),
        compiler_params=_params(("parallel", "parallel")))(o, proj, onorm)


def gnorm_bwd(o, proj, onorm, dout, *, dcol0, name):
    t = o.shape[0]

    def body(o_ref, z_ref, w_ref, d_ref, do_ref, dz_ref, dw_ref):
        @pl.when((pl.program_id(0) == 0) & (pl.program_id(1) == 0))
        def _():
            dw_ref[...] = jnp.zeros_like(dw_ref)

        ov, zv, wv, dv = o_ref[...], z_ref[...], w_ref[...], d_ref[...].astype(F32)
        r = lax.rsqrt(jnp.mean(ov * ov, axis=-1, keepdims=True) + EPS)
        nrm = ov * r
        dz_ref[...] = (dv * nrm * wv * _dsilu(zv)).astype(BF16)
        da = dv * _silu(zv)
        dw_ref[...] += jnp.sum(da * nrm, axis=0, keepdims=True)
        dn = da * wv
        do_ref[...] = r * dn - ov * (r * r * r) * jnp.mean(dn * ov, axis=-1, keepdims=True)

    rows = min(GNORM_ROWS, t)
    blk = pl.BlockSpec((rows, B_HEAD_DIM), lambda n, h: (n, h))
    vec = pl.BlockSpec((1, B_HEAD_DIM), lambda n, h: (0, 0))
    return pl.pallas_call(
        body, name=name, grid=(t // rows, B_HEADS),
        in_specs=[blk, pl.BlockSpec((rows, B_HEAD_DIM), lambda n, h: (n, COL_Z // B_HEAD_DIM + h)), vec,
                  pl.BlockSpec((rows, B_HEAD_DIM), lambda n, h: (n, dcol0 // B_HEAD_DIM + h))],
        out_specs=[blk, blk, vec],
        out_shape=[jax.ShapeDtypeStruct((t, B_W), F32), jax.ShapeDtypeStruct((t, B_W), BF16),
                   jax.ShapeDtypeStruct((1, B_HEAD_DIM), F32)],
        compiler_params=_params(("arbitrary", "arbitrary")))(o, proj, onorm, dout)


def _ffn_fwd(h, norm_g, wg, wu, wd, tm, tag):
    hn = rms_fwd(h, norm_g, name=f"ffn{tag}_norm")
    gate, up, act = mm_gate_up(hn, wg, wu, tm=min(512, tm), tn=1408, tk=2048, name=f"ffn{tag}_gate_up")
    h_out = mm_nn(act, wd, tm=tm, tn=2048, tk=512, out_dtype=F32, res=h, name=f"ffn{tag}_down")
    return h_out, (hn, gate, up, act)


def _ffn_bwd(dh, h, norm_g, wg, wu, wd, saved, tm, tag, emit):
    hn, gate, up, act = saved
    dwd = mm_tn(act, dh, shards=1, tm=tm, tn=1024, tk=1408, out_dtype=BF16, name=f"ffn{tag}_dwd")[0]
    dgate, dup = mm_down_bwd(dh, wd, gate, up, tm=tm, tn=512, tk=2048, name=f"ffn{tag}_dact")
    dwg = mm_tn(hn, dgate, shards=N_SHARD, tm=tm, tn=1408, tk=1024, out_dtype=BF16, name=f"ffn{tag}_dwg")
    dwu = mm_tn(hn, dup, shards=N_SHARD, tm=tm, tn=1408, tk=1024, out_dtype=BF16, name=f"ffn{tag}_dwu")
    started = emit(f"ffn{tag}", {"gate": dwg, "up": dwu, "down": dwd})
    dhn = mm_nt(dgate, wg, tm=tm, tn=1024, tk=1408, out_dtype=F32, name=f"ffn{tag}_dhn_g")
    dh_in, dnorm = dgrad_rms_bwd(dup, wu, NT, h, norm_g + started, dh, tm=min(512, tm), tk=1408, res=dhn,
                                 name=f"ffn{tag}_dhn_u_dnorm")
    return dh_in, dnorm


def _local_step(x, target, w, get, emit):
    t = x.shape[0]
    tm = min(1024, t)
    g = {}

    hn0 = rms_fwd(x, w["even_norm"], name="l0_norm")
    w.update(get("even_in", hn0))
    proj = mm_nt(hn0, w["even_w_in"], tm=tm, tn=512, tk=2048, out_dtype=F32, name="l0_w_in")
    out_a = att_fwd(proj, w["sinks"], name="l0_att")
    qkvn = dprep_fwd(proj, w["even_conv"], name="l0_prep")
    gates = gates_fwd(proj, w["a_log"], w["dt_bias"], name="l0_gates")
    o_delta, ssave = delta_fwd(qkvn, gates, name="l0_delta")
    w.update(get("even_out", o_delta))
    out_b = gnorm_fwd(o_delta, proj, w["onorm"], name="l0_gnorm")
    mix0 = jnp.concatenate([out_a, out_b], axis=-1)
    h1 = mm_nn(mix0, w["even_w_out"], tm=tm, tn=1024, tk=2048, out_dtype=F32, res=x, name="l0_w_out")
    f0 = get("ffn0", h1)
    h2, ffn0 = _ffn_fwd(h1, w["ffn_norm"][0:1] + f0["tok"], f0["gate"], f0["up"], f0["down"], tm, 0)
    hn2 = rms_fwd(h2, w["odd_norm"], name="l1_norm")
    w.update(get("odd", hn2))
    zpre = mm_nn(hn2, w["odd_w_in"], tm=tm, tn=1024, tk=2048, out_dtype=F32, name="l1_w_in")
    gated = gmlp_fwd(zpre, w["odd_ln_g"], w["odd_ln_b"], w["odd_w_s"], w["odd_b_s"], name="l1_gmlp")
    h3 = mm_nn(gated, w["odd_w_out"], tm=tm, tn=1024, tk=2048, out_dtype=F32, res=h2, name="l1_w_out")
    f1 = get("ffn1", h3)
    h4, ffn1 = _ffn_fwd(h3, w["ffn_norm"][1:2] + f1["tok"], f1["gate"], f1["up"], f1["down"], tm, 1)
    loss, dh4, g["final_norm"] = loss_head(h4, w["final_norm"], target, name="loss_head")

    dh3, dn1 = _ffn_bwd(dh4, h3, w["ffn_norm"][1:2], f1["gate"], f1["up"], f1["down"], ffn1, tm, 1, emit)
    dw_out_o = mm_tn(gated, dh3, shards=1, tm=tm, tn=1024, tk=1024, out_dtype=BF16, name="l1_dw_out")[0]
    dgated = mm_nt(dh3, w["odd_w_out"], tm=tm, tn=1024, tk=2048, out_dtype=BF16, name="l1_dgated")
    dzpre, g["odd_w_s"], g["odd_b_s"], g["odd_ln_g"], g["odd_ln_b"] = gmlp_bwd(
        zpre, dgated, w["odd_ln_g"], w["odd_ln_b"], w["odd_w_s"], w["odd_b_s"], name="l1_dgmlp")
    dw_in_o = mm_tn(hn2, dzpre, shards=N_SHARD, tm=tm, tn=1024, tk=1024, out_dtype=BF16, name="l1_dw_in")
    started = emit("odd", {"odd_w_in": dw_in_o, "odd_w_out": dw_out_o})
    dh2, g["odd_norm"] = dgrad_rms_bwd(dzpre, w["odd_w_in"], NT, h2, w["odd_norm"] + started, dh3, tm=min(512, tm),
                                       tk=1024, name="l1_dhn_dnorm")
    dh1, dn0 = _ffn_bwd(dh2, h1, w["ffn_norm"][0:1], f0["gate"], f0["up"], f0["down"], ffn0, tm, 0, emit)
    g["ffn_norm"] = jnp.concatenate([dn0, dn1], axis=0)
    dw_out_e = mm_tn(mix0, dh1, shards=1, tm=tm, tn=1024, tk=1024, out_dtype=BF16, name="l0_dw_out")[0]
    started = emit("even_out", {"even_w_out": dw_out_e})
    dmix = mm_nt(dh1, w["even_w_out"], tm=tm, tn=1024, tk=2048, out_dtype=F32, name="l0_dmix")
    dq_a, dkv_cur, dkv_prev, g["sinks"] = att_bwd(proj, w["sinks"] + started, dmix, name="l0_datt")
    dkv = dkv_cur + jnp.concatenate([dkv_prev[WINDOW:], jnp.zeros((WINDOW, 2 * A_KV), F32)], axis=0)
    do_delta, dz, g["onorm"] = gnorm_bwd(o_delta, proj, w["onorm"], dmix, dcol0=A_Q, name="l0_dgnorm")
    dqkvn, dgates = delta_bwd(qkvn, gates, ssave, do_delta, name="l0_ddelta")
    dqkv_b, g["even_conv"] = dprep_bwd(proj, w["even_conv"], dqkvn, name="l0_dprep")
    draw, g["a_log"], g["dt_bias"] = gates_bwd(proj, w["a_log"], w["dt_bias"], dgates, name="l0_dgates")
    dproj = jnp.concatenate([dq_a, dkv.astype(BF16), dqkv_b, dz, draw,
                             jnp.zeros((t, EVEN_IN_PAD - COL_GATE - 128), BF16)], axis=-1)
    dw_in_e = mm_tn(dproj, hn0, shards=1, tm=tm, tn=1024, tk=1408, out_dtype=BF16, name="l0_dw_in")[0]
    grad_x, g["even_norm"] = dgrad_rms_bwd(dproj, w["even_w_in"], NN, x, w["even_norm"], dh1, tm=min(512, tm), tk=512,
                                           name="l0_dhn_dnorm")
    emit("even_in", {"even_w_in": dw_in_e, "small": g})
    return loss, grad_x


ANY = pl.BlockSpec(memory_space=pl.ANY)
N_DEV = 8


def _place():
    return lax.axis_index("x"), lax.axis_index("y"), lax.axis_index("c")


def _chip_peers(x, y, c):
    return [((1 - x, y, c), 2 * (1 - x) + y), ((x, 1 - y, c), 2 * x + 1 - y), ((1 - x, 1 - y, c), 2 * (1 - x) + 1 - y)]


HBM = pl.BlockSpec(memory_space=pltpu.HBM)
SEM = pl.BlockSpec(memory_space=pltpu.SEMAPHORE)
EFFECT = pltpu.SideEffectType.DATAFLOW_SIDE_EFFECTING
N_PEER = 3


def _half(ref, c):
    r, cols = ref.shape
    tile_rows = 32 // jnp.dtype(ref.dtype).itemsize
    if (r // 2) % tile_rows == 0:
        return ref.at[pl.ds(c * (r // 2), r // 2)]
    assert (cols // 2) % 128 == 0, ref.shape
    return ref.at[:, pl.ds(c * (cols // 2), cols // 2)]


def _gather_plan(srcs, lands, send, recv):
    x, y, c = _place()
    return [pltpu.make_async_remote_copy(src_ref=_half(srcs[i], c), dst_ref=_half(lands[i].at[2 * x + y], c),
                                         send_sem=send.at[N_PEER * i + k], recv_sem=recv.at[N_PEER * i + k],
                                         device_id=peer, device_id_type=MESH_ID)
            for i in range(len(srcs)) for k, (peer, _) in enumerate(_chip_peers(x, y, c))]


def _relay_plan(srcs, lands, send, recv):
    x, y, c = _place()
    return [pltpu.make_async_remote_copy(src_ref=_half(lands[i].at[idx], c), dst_ref=_half(lands[i].at[idx], c),
                                         send_sem=send.at[N_PEER * i + k], recv_sem=recv.at[N_PEER * i + k],
                                         device_id=(x, y, 1 - c), device_id_type=MESH_ID)
            for i in range(len(srcs)) for k, (_, idx) in enumerate(_chip_peers(x, y, c))]


def _scatter_plan(srcs, lands, send, recv):
    x, y, c = _place()
    return [pltpu.make_async_remote_copy(src_ref=srcs[i].at[idx], dst_ref=lands[i].at[k], send_sem=send.at[N_PEER * i + k],
                                         recv_sem=recv.at[N_PEER * i + k], device_id=peer, device_id_type=MESH_ID)
            for i in range(len(srcs)) for k, (peer, idx) in enumerate(_chip_peers(x, y, c))]


def _swap_plan(srcs, lands, send, recv):
    x, y, c = _place()
    return [pltpu.make_async_remote_copy(src_ref=srcs[i], dst_ref=lands[i], send_sem=send.at[N_PEER * i],
                                         recv_sem=recv.at[N_PEER * i], device_id=(x, y, 1 - c), device_id_type=MESH_ID)
            for i in range(len(srcs))]


def copies_start(plan, srcs, lands, after, *, name):
    n = len(srcs)
    both = list(srcs) + list(lands)

    def body(*refs):
        src_refs, land_refs = refs[:n], refs[n:2 * n]
        send, recv = refs[2 * n + 1], refs[2 * n + 2]
        for cp in plan(src_refs, land_refs, send, recv):
            cp.start()
        refs[-1][...] = jnp.zeros_like(refs[-1])

    res = pl.pallas_call(
        body, name=name,
        out_shape=(pltpu.SemaphoreType.DMA((n * N_PEER,)), pltpu.SemaphoreType.DMA((n * N_PEER,)),
                   *[pltpu.HBM(a.shape, a.dtype) for a in both], jax.ShapeDtypeStruct((8, 128), F32)),
        in_specs=[HBM] * (2 * n) + [ANY],
        out_specs=(SEM, SEM, *[HBM] * (2 * n), pl.BlockSpec(memory_space=pltpu.VMEM)),
        input_output_aliases={i: 2 + i for i in range(2 * n)},
        compiler_params=pltpu.CompilerParams(has_side_effects=EFFECT))(
            *[pltpu.with_memory_space_constraint(a, pltpu.HBM) for a in both], after)
    return {"send": res[0], "recv": res[1], "srcs": list(res[2:2 + n]), "lands": list(res[2 + n:2 + 2 * n]),
            "token": res[-1]}


def copies_relay(arrived_plan, next_plan, started, after, *, name):
    srcs, lands = started["srcs"], started["lands"]
    n = len(srcs)
    both = srcs + lands

    def body(*refs):
        src_refs, land_refs = refs[:n], refs[n:2 * n]
        send1, recv1 = refs[2 * n], refs[2 * n + 1]
        send2, recv2 = refs[2 * n + 3], refs[2 * n + 4]
        for cp in arrived_plan(src_refs, land_refs, send1, recv1):
            cp.wait_send()
            cp.wait_recv()
        for cp in next_plan(src_refs, land_refs, send2, recv2):
            cp.start()
        refs[-1][...] = jnp.zeros_like(refs[-1])

    res = pl.pallas_call(
        body, name=name,
        out_shape=(pltpu.SemaphoreType.DMA((n * N_PEER,)), pltpu.SemaphoreType.DMA((n * N_PEER,)),
                   *[pltpu.HBM(a.shape, a.dtype) for a in both], jax.ShapeDtypeStruct((8, 128), F32)),
        in_specs=[HBM] * (2 * n) + [SEM, SEM, ANY],
        out_specs=(SEM, SEM, *[HBM] * (2 * n), pl.BlockSpec(memory_space=pltpu.VMEM)),
        input_output_aliases={i: 2 + i for i in range(2 * n)},
        compiler_params=pltpu.CompilerParams(has_side_effects=EFFECT))(*both, started["send"], started["recv"], after)
    return {"send": res[0], "recv": res[1], "srcs": list(res[2:2 + n]), "lands": list(res[2 + n:2 + 2 * n]),
            "token": res[-1]}


def copies_wait(plan, started, after, *, name):
    srcs, lands = started["srcs"], started["lands"]
    n = len(srcs)
    both = srcs + lands

    def body(*refs):
        src_refs, land_refs = refs[:n], refs[n:2 * n]
        send, recv = refs[2 * n], refs[2 * n + 1]
        for cp in plan(src_refs, land_refs, send, recv):
            cp.wait_send()
            cp.wait_recv()

    res = pl.pallas_call(
        body, name=name, out_shape=tuple(pltpu.HBM(a.shape, a.dtype) for a in both),
        in_specs=[HBM] * (2 * n) + [SEM, SEM, ANY], out_specs=(HBM,) * (2 * n),
        input_output_aliases={i: i for i in range(2 * n)},
        compiler_params=pltpu.CompilerParams(has_side_effects=EFFECT))(*both, started["send"], started["recv"], after)
    return list(res[:n]), list(res[n:])


def allgather_small(small, *, name):
    def body(small_ref, out_ref, send, recv, loc):
        x, y, c = _place()
        dev = 4 * x + 2 * y + c
        local = pltpu.make_async_copy(small_ref, out_ref.at[dev], loc)
        remote = []
        for r in range(1, N_DEV):
            fx, fy, fc = (r >> 2) & 1, (r >> 1) & 1, r & 1
            peer = (1 - x if fx else x, 1 - y if fy else y, 1 - c if fc else c)
            remote.append(pltpu.make_async_remote_copy(
                src_ref=small_ref, dst_ref=out_ref.at[dev], send_sem=send.at[r - 1], recv_sem=recv.at[r - 1],
                device_id=peer, device_id_type=MESH_ID))
        local.start()
        for cp in remote:
            cp.start()
        for cp in remote:
            cp.wait()
        local.wait()

    return pl.pallas_call(
        body, name=name, in_specs=[ANY], out_specs=ANY,
        out_shape=jax.ShapeDtypeStruct((N_DEV,) + small.shape, small.dtype),
        scratch_shapes=[pltpu.SemaphoreType.DMA((N_DEV - 1,)), pltpu.SemaphoreType.DMA((N_DEV - 1,)),
                        pltpu.SemaphoreType.DMA(())])(small)


RED_ROWS = 256
RED_COLS = 256


def _red_block(r, c):
    if r % RED_ROWS == 0:
        return RED_ROWS, c
    if c > RED_COLS and c % RED_COLS == 0:
        return r, RED_COLS
    return r, c


def sum_chips(by_owner, me, got, *, name):
    _, r, c = by_owner.shape
    rb, cb = _red_block(r, c)

    def body(me_ref, o_ref, a_ref, b_ref, c_ref, out_ref):
        total = ((o_ref[...].astype(F32) + a_ref[...].astype(F32)) + b_ref[...].astype(F32)) + c_ref[...].astype(F32)
        out_ref[...] = total.astype(BF16)

    gk = lambda k: pl.BlockSpec((None, rb, cb), lambda i, j, me_ref: (k, i, j))
    grid_spec = pltpu.PrefetchScalarGridSpec(
        num_scalar_prefetch=1, grid=(r // rb, c // cb),
        in_specs=[pl.BlockSpec((None, rb, cb), lambda i, j, me_ref: (me_ref[0], i, j)), gk(0), gk(1), gk(2)],
        out_specs=pl.BlockSpec((rb, cb), lambda i, j, me_ref: (i, j)))
    return pl.pallas_call(
        body, name=name, grid_spec=grid_spec, out_shape=jax.ShapeDtypeStruct((r, c), BF16),
        compiler_params=_params(("parallel", "parallel")))(me, by_owner, got, got, got)


def sum_devices(small_all, *, name):
    _, p, c = small_all.shape

    def body(a_ref, out_ref):
        acc = a_ref[0]
        for d in range(1, N_DEV):
            acc = acc + a_ref[d]
        out_ref[...] = acc

    return pl.pallas_call(
        body, name=name, grid=(1,), in_specs=[pl.BlockSpec((N_DEV, p, c), lambda i: (0, 0, 0))],
        out_specs=pl.BlockSpec((p, c), lambda i: (0, 0)), out_shape=jax.ShapeDtypeStruct((p, c), F32),
        compiler_params=_params(("arbitrary",)))(small_all)


def adamw(parts, w, m, v, *, name):
    nl, r, c = w.shape
    assert len(parts) == nl
    npart = len(parts[0])
    rb, cb = _red_block(r, c)
    flat = [a for layer in parts for a in layer]

    def body(*refs):
        p_refs, (w_ref, m_ref, v_ref) = refs[:nl * npart], refs[nl * npart:nl * npart + 3]
        g_ref, d_ref, nm_ref, nv_ref = refs[nl * npart + 3:]
        layer = pl.program_id(0)
        grad = None
        for l in range(nl):
            gl = p_refs[l * npart][...].astype(F32)
            for j in range(1, npart):
                gl = gl + p_refs[l * npart + j][...].astype(F32)
            grad = gl if grad is None else jnp.where(layer == l, gl, grad)
        wv, mv, vv = w_ref[...], m_ref[...], v_ref[...]
        nm = ADAM_B1 * mv + (1.0 - ADAM_B1) * grad
        nv = ADAM_B2 * vv + (1.0 - ADAM_B2) * (grad * grad)
        m_hat = nm / (1.0 - ADAM_B1 ** ADAM_STEP)
        v_hat = nv / (1.0 - ADAM_B2 ** ADAM_STEP)
        g_ref[...] = grad
        d_ref[...] = -ADAM_LR * (m_hat / (jnp.sqrt(v_hat) + ADAM_EPS) + ADAM_WD * wv)
        nm_ref[...] = nm
        nv_ref[...] = nv

    pspec = pl.BlockSpec((rb, cb), lambda l, i, j: (i, j))
    wspec = pl.BlockSpec((None, rb, cb), lambda l, i, j: (l, i, j))
    osh = jax.ShapeDtypeStruct((nl, r, c), F32)
    return pl.pallas_call(
        body, name=name, grid=(nl, r // rb, c // cb), in_specs=[pspec] * (nl * npart) + [wspec] * 3,
        out_specs=[wspec] * 4, out_shape=[osh] * 4,
        compiler_params=_params(("parallel", "parallel", "parallel")))(*flat, w, m, v)


def _rows128(a):
    flat = a.reshape(-1)
    pad = (-flat.shape[0]) % 128
    return jnp.pad(flat, (0, pad)).reshape(-1, 128)


def _pack_rows(arrs, multiple=8):
    rows = jnp.concatenate([_rows128(a.astype(F32)) for a in arrs], axis=0)
    return jnp.pad(rows, ((0, (-rows.shape[0]) % multiple), (0, 0)))


def _unpack_rows(rows, shapes):
    out, r0 = [], 0
    for shp in shapes:
        size = 1
        for s in shp:
            size *= s
        nr = -(-size // 128)
        out.append(rows[r0:r0 + nr].reshape(-1)[:size].reshape(shp))
        r0 += nr
    return out


SMALL_LOCAL_GRADS = ["even_norm", "even_conv", "a_log", "dt_bias", "sinks", "onorm", "odd_norm", "odd_ln_g",
                     "odd_ln_b", "odd_w_s", "odd_b_s", "ffn_norm", "final_norm"]
BIG = ["even_w_in", "even_w_out", "odd_w_in", "odd_w_out", "ffn_w_gate", "ffn_w_up", "ffn_w_down"]
WEIGHTS = ["even_norm", "even_w_in", "even_conv", "even_a_log", "even_dt_bias", "even_sinks", "even_onorm",
           "even_w_out", "odd_norm", "odd_w_in", "odd_ln_g", "odd_ln_b", "odd_w_s", "odd_b_s", "odd_w_out",
           "ffn_norm", "ffn_w_gate", "ffn_w_up", "ffn_w_down", "final_norm"]
SMALL = [n for n in WEIGHTS if n not in BIG]


def kernel(x, even_norm, even_w_in, even_conv, even_a_log, even_dt_bias, even_sinks, even_onorm, even_w_out, odd_norm, odd_w_in, odd_ln_g, odd_ln_b, odd_w_s, odd_b_s, odd_w_out, ffn_norm, ffn_w_gate, ffn_w_up, ffn_w_down, final_norm, loss_target, m_even_norm, m_even_w_in, m_even_conv, m_even_a_log, m_even_dt_bias, m_even_sinks, m_even_onorm, m_even_w_out, m_odd_norm, m_odd_w_in, m_odd_ln_g, m_odd_ln_b, m_odd_w_s, m_odd_b_s, m_odd_w_out, m_ffn_norm, m_ffn_w_gate, m_ffn_w_up, m_ffn_w_down, m_final_norm, v_even_norm, v_even_w_in, v_even_conv, v_even_a_log, v_even_dt_bias, v_even_sinks, v_even_onorm, v_even_w_out, v_odd_norm, v_odd_w_in, v_odd_ln_g, v_odd_ln_b, v_odd_w_s, v_odd_b_s, v_odd_w_out, v_ffn_norm, v_ffn_w_gate, v_ffn_w_up, v_ffn_w_down, v_final_norm):
    args = dict(locals())
    wl = {n: args[n] for n in WEIGHTS}
    ml = {n: args["m_" + n] for n in WEIGHTS}
    vl = {n: args["v_" + n] for n in WEIGHTS}
    me = 2 * lax.axis_index("x") + lax.axis_index("y")

    def landing(a):
        return lax.dynamic_update_index_in_dim(lax.empty((N_SHARD,) + a.shape, a.dtype), a, me, 0)

    b16 = lambda *arrs: [a.astype(BF16) for a in arrs]
    gather_groups = {
        "even_in": b16(even_w_in[0].T) + [_pack_rows([even_conv[0], odd_norm, odd_ln_g, odd_ln_b], multiple=16)],
        "even_out": b16(even_w_out[0]),
        "ffn0": b16(ffn_w_gate[0], ffn_w_up[0], ffn_w_down[0]),
        "odd": b16(odd_w_in[0], odd_w_out[0]),
        "ffn1": b16(ffn_w_gate[1], ffn_w_up[1], ffn_w_down[1]),
    }
    gathering, after = {}, even_norm
    for group, srcs in gather_groups.items():
        gathering[group] = copies_start(_gather_plan, srcs, [landing(a) for a in srcs], after,
                                        name=f"gather_{group}_start")
        after = gathering[group]["token"]

    order = list(gather_groups)
    relayed, kept = {}, {}
    sinks_pad = jnp.pad(even_sinks, ((0, 0), (0, 128 - A_HEADS)))

    def relay(group, behind):
        relayed[group] = copies_relay(_gather_plan, _relay_plan, gathering[group], behind,
                                      name=f"gather_{group}_relay")
        return relayed[group]["token"][0:1, 0:1]

    def get(group, behind):
        if group not in relayed:
            relay(group, behind)
        _, lands = copies_wait(_relay_plan, relayed[group], behind, name=f"gather_{group}_wait")
        nxt = order.index(group) + 1
        tok = relay(order[nxt], lands[0]) if nxt < len(order) else jnp.zeros((1, 1), F32)
        if group == "even_in":
            parts = zip(*[_unpack_rows(lands[1][s], [(CONV_K, 768), (1, 512), (1, 512), (1, 512)])
                          for s in range(N_SHARD)])
            conv, onorm, lng, lnb = [jnp.concatenate(p, axis=1) for p in parts]
            w_in = jnp.pad(lands[0].reshape(EVEN_IN, D_MODEL), ((0, EVEN_IN_PAD - EVEN_IN), (0, 0)))
            kept["odd_ln_g"] = lng
            return {"even_w_in": w_in, "even_conv": conv + tok, "odd_norm": onorm, "odd_ln_b": lnb}
        if group == "even_out":
            return {"even_w_out": lands[0].reshape(D_MODEL, D_MODEL), "onorm": even_onorm + tok}
        if group == "odd":
            return {"odd_w_in": lands[0], "odd_w_out": lands[1].reshape(D_MODEL, D_MODEL),
                    "odd_ln_g": kept["odd_ln_g"] + tok}
        return {"gate": lands[0], "up": lands[1], "down": lands[2].reshape(D_FF, D_MODEL), "tok": tok}

    rows4 = lambda a: a.reshape(N_SHARD, a.shape[0] // N_SHARD, a.shape[1])
    scattering, small = {}, {}

    def emit(group, grads):
        behind = even_norm
        if group == "even_in":
            small["local"] = grads["small"]
            small["all"] = behind = allgather_small(_pack_rows([grads["small"][n] for n in SMALL_LOCAL_GRADS]),
                                                    name="allgather_small")
            srcs = [grads["even_w_in"][:EVEN_IN].reshape(N_SHARD, EVEN_IN // N_SHARD, D_MODEL)]
        elif group == "even_out":
            srcs = [rows4(grads["even_w_out"])]
        elif group == "odd":
            srcs = [grads["odd_w_in"], rows4(grads["odd_w_out"])]
        else:
            srcs = [grads["gate"], grads["up"], rows4(grads["down"])]
        lands = [lax.empty((N_PEER,) + a.shape[1:], a.dtype) for a in srcs]
        scattering[group] = copies_start(_scatter_plan, srcs, lands, behind, name=f"scatter_{group}_start")
        return scattering[group]["token"][0:1, 0:1]

    pad816 = lambda a: jnp.pad(a, ((0, 0), (B_HEADS, 128 - 2 * B_HEADS)))
    w = {
        "even_norm": even_norm + after[0:1, 0:1],
        "a_log": pad816(even_a_log), "dt_bias": pad816(even_dt_bias),
        "sinks": sinks_pad,
        "onorm": even_onorm,
        "odd_w_s": odd_w_s[0],
        "odd_b_s": jnp.pad(odd_b_s[0].T, ((0, 0), (0, 128 - C_GROUPS))),
        "ffn_norm": ffn_norm,
        "final_norm": final_norm[None],
    }
    loss_l, grad_x = _local_step(x[0], loss_target[0], w, get, emit)
    loss = lax.psum(loss_l[0, 0], ("x", "y", "c"))

    me1 = me.reshape(1).astype(jnp.int32)
    swapping = {}

    def reduce_chips(group, behind):
        srcs, lands = copies_wait(_scatter_plan, scattering[group], behind, name=f"scatter_{group}_wait")
        partial = [sum_chips(srcs[i], me1, lands[i], name=f"sum_chips_{group}_{i}") for i in range(len(srcs))]
        swapping[group] = copies_start(_swap_plan, partial, [lax.empty(p.shape, p.dtype) for p in partial],
                                       even_norm, name=f"swap_{group}_start")
        return swapping[group]["token"]

    def swapped(group, behind):
        mine, theirs = copies_wait(_swap_plan, swapping[group], behind, name=f"swap_{group}_wait")
        return list(zip(mine, theirs))

    behind = scattering["even_in"]["token"]
    for group in ("ffn1", "ffn0", "odd", "even_out"):
        behind = reduce_chips(group, behind)
    sums = {group: swapped(group, behind) for group in ("ffn1", "ffn0", "odd", "even_out")}
    outs = {}
    parts_of = {"even_w_out": [sums["even_out"][0]], "odd_w_in": [sums["odd"][0]], "odd_w_out": [sums["odd"][1]],
                "ffn_w_gate": [sums["ffn0"][0], sums["ffn1"][0]], "ffn_w_up": [sums["ffn0"][1], sums["ffn1"][1]],
                "ffn_w_down": [sums["ffn0"][2], sums["ffn1"][2]]}
    for n in parts_of:
        outs[n] = adamw(parts_of[n], wl[n], ml[n], vl[n], name=f"adamw_{n}")
    behind = reduce_chips("even_in", outs["ffn_w_down"][1])
    flip = lambda a: jnp.transpose(a, (0, 2, 1))
    outs["even_w_in"] = [flip(o) for o in adamw([swapped("even_in", behind)[0]], flip(wl["even_w_in"]),
                                                flip(ml["even_w_in"]), flip(vl["even_w_in"]),
                                                name="adamw_even_w_in")]

    g = small["local"]
    small_sum = sum_devices(small["all"], name="sum_devices")
    sg = dict(zip(SMALL_LOCAL_GRADS, _unpack_rows(small_sum, [g[n].shape for n in SMALL_LOCAL_GRADS])))
    own_cols = lambda a, width: lax.dynamic_slice_in_dim(a, me * width, width, axis=a.ndim - 1)
    small_grads = {
        "even_norm": sg["even_norm"], "even_conv": own_cols(sg["even_conv"], 768)[None],
        "even_a_log": sg["a_log"][:, B_HEADS:2 * B_HEADS], "even_dt_bias": sg["dt_bias"][:, B_HEADS:2 * B_HEADS],
        "even_sinks": sg["sinks"][:, :A_HEADS], "even_onorm": sg["onorm"],
        "odd_norm": own_cols(sg["odd_norm"], 512), "odd_ln_g": own_cols(sg["odd_ln_g"], 512),
        "odd_ln_b": own_cols(sg["odd_ln_b"], 512), "odd_w_s": sg["odd_w_s"][None],
        "odd_b_s": sg["odd_b_s"][:, :C_GROUPS].T[None], "ffn_norm": sg["ffn_norm"], "final_norm": sg["final_norm"][0],
    }
    packed = [_pack_rows([d[n] for n in SMALL])[None] for d in (small_grads, wl, ml, vl)]
    small_out = adamw([(packed[0][0],)], packed[1], packed[2], packed[3], name="adamw_small")
    shapes = [wl[n].shape for n in SMALL]
    for j in range(4):
        for n, a in zip(SMALL, _unpack_rows(small_out[j][0], shapes)):
            outs.setdefault(n, [None] * 4)[j] = a

    return (loss, grad_x[None], *[outs[n][0] for n in WEIGHTS], *[outs[n][1] for n in WEIGHTS],
            *[outs[n][2] for n in WEIGHTS], *[outs[n][3] for n in WEIGHTS])
```

```python
import functools

import jax
import jax.numpy as jnp
from jax import lax
from jax.experimental import pallas as pl
from jax.experimental.pallas import tpu as pltpu

F32 = jnp.float32
BF16 = jnp.bfloat16
NEG_INF = float("-inf")

D_MODEL = 2048
A_HEADS, A_KV_HEADS, A_HEAD_DIM, WINDOW = 16, 2, 64, 128
B_HEADS, B_HEAD_DIM, CONV_K, DN_CHUNK = 8, 128, 4, 64
C_GROUPS, C_CHUNK = 8, 128
C_GROUP_DIM = D_MODEL // C_GROUPS
D_FF = 5632
EPS = 1e-6
A_Q = A_HEADS * A_HEAD_DIM
A_KV = A_KV_HEADS * A_HEAD_DIM
B_W = B_HEADS * B_HEAD_DIM
EVEN_IN = A_Q + 2 * A_KV + 4 * B_W + 2 * B_HEADS
EVEN_IN_PAD = 5632
COL_KV = A_Q
COL_QKVB = A_Q + 2 * A_KV
COL_Z = COL_QKVB + 3 * B_W
COL_GATE = COL_Z + B_W
N_SHARD = 4

ADAM_LR, ADAM_B1, ADAM_B2, ADAM_EPS, ADAM_WD, ADAM_STEP = 0.001, 0.9, 0.999, 1e-08, 0.01, 10

VMEM_LIMIT_V7X = 56 * 1024 * 1024
MXU_COLS = 256
MESH_ID = pl.DeviceIdType.MESH


def _params(sem=None):
    return pltpu.CompilerParams(dimension_semantics=sem, vmem_limit_bytes=VMEM_LIMIT_V7X)


def _sigmoid(x):
    return 1.0 / (1.0 + jnp.exp(-x))


def _silu(x):
    return x * _sigmoid(x)


def _dsilu(x):
    s = _sigmoid(x)
    return s * (1.0 + x * (1.0 - s))


def _gelu(x):
    return 0.5 * x * (1.0 + lax.erf(x * 0.7071067811865476))


def _dgelu(x):
    return 0.5 * (1.0 + lax.erf(x * 0.7071067811865476)) + x * jnp.exp(-0.5 * x * x) * 0.3989422804014327


def _dot(a, b, dims):
    if a.ndim == 3:
        (ca,), (cb,) = dims
        return lax.dot_general(a, b, (((ca + 1,), (cb + 1,)), ((0,), (0,))), preferred_element_type=F32)
    return lax.dot_general(a, b, (dims, ((), ())), preferred_element_type=F32)


NN = ((1,), (0,))
NT = ((1,), (1,))
TN = ((0,), (0,))


def _as3(b):
    return b if b.ndim == 3 else b[None]


def _accumulate(step, nsteps, accs, products, finish):
    if nsteps == 1:
        finish(products())
        return

    @pl.when(step == 0)
    def _():
        for acc, p in zip(accs, products()):
            acc[...] = p

    if nsteps > 2:
        @pl.when((step > 0) & (step < nsteps - 1))
        def _():
            for acc, p in zip(accs, products()):
                acc[...] += p

    @pl.when(step == nsteps - 1)
    def _():
        finish(tuple(acc[...] + p for acc, p in zip(accs, products())))


def mm_nn(a, b, *, tm, tn, tk, out_dtype, name, res=None):
    b3 = _as3(b)
    m, k = a.shape
    s, k2, ns = b3.shape
    assert k2 == k and m % tm == 0 and ns % tn == 0 and k % tk == 0, (a.shape, b3.shape, tm, tn, tk)
    nps, nk = ns // tn, k // tk

    def body(*refs):
        if res is None:
            a_ref, b_ref, o_ref, acc = refs
        else:
            a_ref, b_ref, r_ref, o_ref, acc = refs
        def finish(tiles):
            r = tiles[0] if res is None else tiles[0] + r_ref[...].astype(F32)
            o_ref[...] = r.astype(out_dtype)

        _accumulate(pl.program_id(2), nk, (acc,),
                    lambda: (_dot(a_ref[...].astype(BF16), b_ref[...].astype(BF16), NN),), finish)

    in_specs = [pl.BlockSpec((tm, tk), lambda i, j, kk: (i, kk)),
                pl.BlockSpec((None, tk, tn), lambda i, j, kk: (j // nps, kk, j % nps))]
    args = [a, b3]
    if res is not None:
        in_specs.append(pl.BlockSpec((tm, tn), lambda i, j, kk: (i, j)))
        args.append(res)
    return pl.pallas_call(
        body, name=name, grid=(m // tm, s * nps, nk), in_specs=in_specs,
        out_specs=pl.BlockSpec((tm, tn), lambda i, j, kk: (i, j)),
        out_shape=jax.ShapeDtypeStruct((m, s * ns), out_dtype),
        scratch_shapes=[pltpu.VMEM((tm, tn), F32)],
        compiler_params=_params(("parallel", "parallel", "arbitrary")))(*args)


def mm_nt(a, b, *, tm, tn, tk, out_dtype, name, res=None):
    b3 = _as3(b)
    m, n = a.shape
    s, k, ns = b3.shape
    assert n == s * ns and m % tm == 0 and k % tn == 0 and ns % tk == 0, (a.shape, b3.shape, tm, tn, tk)
    rps = ns // tk
    nr = s * rps

    def body(*refs):
        if res is None:
            a_ref, b_ref, o_ref, acc = refs
        else:
            a_ref, b_ref, r_ref, o_ref, acc = refs
        def finish(tiles):
            r = tiles[0] if res is None else tiles[0] + r_ref[...].astype(F32)
            o_ref[...] = r.astype(out_dtype)

        _accumulate(pl.program_id(2), nr, (acc,),
                    lambda: (_dot(a_ref[...].astype(BF16), b_ref[...].astype(BF16), NT),), finish)

    in_specs = [pl.BlockSpec((tm, tk), lambda i, j, r: (i, r)),
                pl.BlockSpec((None, tn, tk), lambda i, j, r: (r // rps, j, r % rps))]
    args = [a, b3]
    if res is not None:
        in_specs.append(pl.BlockSpec((tm, tn), lambda i, j, r: (i, j)))
        args.append(res)
    return pl.pallas_call(
        body, name=name, grid=(m // tm, k // tn, nr), in_specs=in_specs,
        out_specs=pl.BlockSpec((tm, tn), lambda i, j, r: (i, j)),
        out_shape=jax.ShapeDtypeStruct((m, k), out_dtype),
        scratch_shapes=[pltpu.VMEM((tm, tn), F32)],
        compiler_params=_params(("parallel", "parallel", "arbitrary")))(*args)


def mm_tn(a, b, *, shards, tm, tn, tk, out_dtype, name):
    m, k = a.shape
    m2, n = b.shape
    ns = n // shards
    assert m2 == m and n == shards * ns and m % tm == 0 and k % tk == 0 and ns % tn == 0, (a.shape, b.shape)
    nps, nm = ns // tn, m // tm

    def body(a_ref, b_ref, o_ref, acc):
        def finish(tiles):
            o_ref[...] = tiles[0].astype(out_dtype)

        _accumulate(pl.program_id(2), nm, (acc,),
                    lambda: (_dot(a_ref[...].astype(BF16), b_ref[...].astype(BF16), TN),), finish)

    return pl.pallas_call(
        body, name=name, grid=(k // tk, shards * nps, nm),
        in_specs=[pl.BlockSpec((tm, tk), lambda i, j, mi: (mi, i)),
                  pl.BlockSpec((tm, tn), lambda i, j, mi: (mi, j))],
        out_specs=pl.BlockSpec((None, tk, tn), lambda i, j, mi: (j // nps, i, j % nps)),
        out_shape=jax.ShapeDtypeStruct((shards, k, ns), out_dtype),
        scratch_shapes=[pltpu.VMEM((tk, tn), F32)],
        compiler_params=_params(("parallel", "parallel", "arbitrary")))(a, b)


def mm_gate_up(hn, wg, wu, *, tm, tn, tk, name):
    wg3, wu3 = _as3(wg), _as3(wu)
    m, k = hn.shape
    s, _, ns = wg3.shape
    assert m % tm == 0 and ns % tn == 0 and k % tk == 0
    nps, nk = ns // tn, k // tk

    def body(a_ref, g_ref, u_ref, og_ref, ou_ref, oa_ref, accg, accu):
        def products():
            a = a_ref[...].astype(BF16)
            return _dot(a, g_ref[...].astype(BF16), NN), _dot(a, u_ref[...].astype(BF16), NN)

        def finish(tiles):
            g, u = tiles
            og_ref[...] = g.astype(BF16)
            ou_ref[...] = u.astype(BF16)
            oa_ref[...] = (_silu(g) * u).astype(BF16)

        _accumulate(pl.program_id(2), nk, (accg, accu), products, finish)

    wspec = pl.BlockSpec((None, tk, tn), lambda i, j, kk: (j // nps, kk, j % nps))
    ospec = pl.BlockSpec((tm, tn), lambda i, j, kk: (i, j))
    osh = jax.ShapeDtypeStruct((m, s * ns), BF16)
    return pl.pallas_call(
        body, name=name, grid=(m // tm, s * nps, nk),
        in_specs=[pl.BlockSpec((tm, tk), lambda i, j, kk: (i, kk)), wspec, wspec],
        out_specs=[ospec, ospec, ospec], out_shape=[osh, osh, osh],
        scratch_shapes=[pltpu.VMEM((tm, tn) if nk > 1 else (8, 128), F32)] * 2,
        compiler_params=_params(("parallel", "parallel", "arbitrary")))(hn, wg3, wu3)


def mm_down_bwd(dh, wd, gate, up, *, tm, tn, tk, name):
    m, d = dh.shape
    f, d2 = wd.shape
    assert d2 == d and m % tm == 0 and f % tn == 0 and tk == d and tn % MXU_COLS == 0

    def body(a_ref, b_ref, g_ref, u_ref, og_ref, ou_ref):
        a = a_ref[...].astype(BF16)
        for jj in range(tn // MXU_COLS):
            sl = slice(jj * MXU_COLS, (jj + 1) * MXU_COLS)
            da = _dot(a, b_ref[sl, :].astype(BF16), NT)
            g, u = g_ref[:, sl].astype(F32), u_ref[:, sl].astype(F32)
            s = _sigmoid(g)
            og_ref[:, sl] = (da * u * (s * (1.0 + g * (1.0 - s)))).astype(BF16)
            ou_ref[:, sl] = (da * (g * s)).astype(BF16)

    ospec = pl.BlockSpec((tm, tn), lambda i, j: (i, j))
    osh = jax.ShapeDtypeStruct((m, f), BF16)
    return pl.pallas_call(
        body, name=name, grid=(m // tm, f // tn),
        in_specs=[pl.BlockSpec((tm, tk), lambda i, j: (i, 0)),
                  pl.BlockSpec((tn, tk), lambda i, j: (j, 0)), ospec, ospec],
        out_specs=[ospec, ospec], out_shape=[osh, osh],
        compiler_params=_params(("parallel", "parallel")))(dh, wd, gate, up)


ROWS = 256


def rms_fwd(x, g, *, name):
    t, d = x.shape

    def body(x_ref, g_ref, o_ref):
        xv = x_ref[...]
        r = lax.rsqrt(jnp.mean(xv * xv, axis=-1, keepdims=True) + EPS)
        o_ref[...] = (xv * r * g_ref[...]).astype(BF16)

    return pl.pallas_call(
        body, name=name, grid=(t // ROWS,),
        in_specs=[pl.BlockSpec((ROWS, d), lambda i: (i, 0)), pl.BlockSpec((1, d), lambda i: (0, 0))],
        out_specs=pl.BlockSpec((ROWS, d), lambda i: (i, 0)),
        out_shape=jax.ShapeDtypeStruct((t, d), BF16), compiler_params=_params(("parallel",)))(x, g)


def dgrad_rms_bwd(a, b, form, x, g, dres, *, tm, tk, name, res=None):
    m, d = x.shape
    b3 = _as3(b)
    if form == NN:
        steps = a.shape[1] // tk
        a_spec = pl.BlockSpec((tm, tk), lambda i, r: (i, r))
        b_spec = pl.BlockSpec((None, tk, d), lambda i, r: (0, r, 0))
    else:
        s, d2, ns = b3.shape
        assert d2 == d and ns % tk == 0
        rps = ns // tk
        steps = s * rps
        a_spec = pl.BlockSpec((tm, tk), lambda i, r: (i, r))
        b_spec = pl.BlockSpec((None, d, tk), lambda i, r: (r // rps, 0, r % rps))
    assert m % tm == 0 and a.shape[1] == steps * tk

    def body(*refs):
        if res is None:
            a_ref, b_ref, x_ref, g_ref, dr_ref, dx_ref, dg_ref, acc = refs
        else:
            a_ref, b_ref, r_ref, x_ref, g_ref, dr_ref, dx_ref, dg_ref, acc = refs

        @pl.when((pl.program_id(0) == 0) & (pl.program_id(1) == 0))
        def _():
            dg_ref[...] = jnp.zeros_like(dg_ref)

        def finish(tiles):
            dyv = tiles[0] if res is None else tiles[0] + r_ref[...]
            xv = x_ref[...]
            r = lax.rsqrt(jnp.mean(xv * xv, axis=-1, keepdims=True) + EPS)
            dyg = dyv * g_ref[...]
            dx_ref[...] = r * dyg - xv * (r * r * r) * jnp.mean(dyg * xv, axis=-1, keepdims=True) + dr_ref[...]
            dg_ref[...] += jnp.sum(dyv * xv * r, axis=0, keepdims=True)

        _accumulate(pl.program_id(1), steps, (acc,),
                    lambda: (_dot(a_ref[...].astype(BF16), b_ref[...].astype(BF16), form),), finish)

    row = pl.BlockSpec((tm, d), lambda i, r: (i, 0))
    vec = pl.BlockSpec((1, d), lambda i, r: (0, 0))
    in_specs = [a_spec, b_spec] + ([row] if res is not None else []) + [row, vec, row]
    args = [a, b3] + ([res] if res is not None else []) + [x, g, dres]
    return pl.pallas_call(
        body, name=name, grid=(m // tm, steps), in_specs=in_specs, out_specs=[row, vec],
        out_shape=[jax.ShapeDtypeStruct((m, d), F32), jax.ShapeDtypeStruct((1, d), F32)],
        scratch_shapes=[pltpu.VMEM((tm, d), F32)],
        compiler_params=_params(("arbitrary", "arbitrary")))(*args)


def loss_head(h, g, target, *, name):
    t, d = h.shape

    def body(x_ref, g_ref, t_ref, loss_ref, dx_ref, dg_ref):
        @pl.when(pl.program_id(0) == 0)
        def _():
            dg_ref[...] = jnp.zeros_like(dg_ref)
            loss_ref[...] = jnp.zeros_like(loss_ref)

        xv, gv = x_ref[...], g_ref[...]
        r = lax.rsqrt(jnp.mean(xv * xv, axis=-1, keepdims=True) + EPS)
        e = xv * r * gv - t_ref[...]
        loss_ref[...] += 0.5 * jnp.sum(jnp.mean(e * e, axis=-1, keepdims=True), axis=0, keepdims=True)
        dyv = e * (1.0 / d)
        dyg = dyv * gv
        dx_ref[...] = r * dyg - xv * (r * r * r) * jnp.mean(dyg * xv, axis=-1, keepdims=True)
        dg_ref[...] += jnp.sum(dyv * xv * r, axis=0, keepdims=True)

    row = pl.BlockSpec((ROWS, d), lambda i: (i, 0))
    vec = pl.BlockSpec((1, d), lambda i: (0, 0))
    return pl.pallas_call(
        body, name=name, grid=(t // ROWS,), in_specs=[row, vec, row],
        out_specs=[pl.BlockSpec((1, 128), lambda i: (0, 0)), row, vec],
        out_shape=[jax.ShapeDtypeStruct((1, 128), F32), jax.ShapeDtypeStruct((t, d), F32),
                   jax.ShapeDtypeStruct((1, d), F32)],
        compiler_params=_params(("arbitrary",)))(h, g, target)


def _tril_mask():
    r = lax.broadcasted_iota(jnp.int32, (C_CHUNK, C_CHUNK), 0)
    c = lax.broadcasted_iota(jnp.int32, (C_CHUNK, C_CHUNK), 1)
    return r >= c


def _layer_norm_parts(v):
    mu = jnp.mean(v, axis=-1, keepdims=True)
    vc = v - mu
    rstd = lax.rsqrt(jnp.mean(vc * vc, axis=-1, keepdims=True) + EPS)
    return vc * rstd, rstd


def gmlp_fwd(zpre, ln_g, ln_b, ws, bs_t, *, name):
    t = zpre.shape[0]
    d = D_MODEL

    def body(zu_ref, zv_ref, g_ref, b_ref, ws_ref, bs_ref, o_ref):
        u = _gelu(zu_ref[...])
        vhat, _ = _layer_norm_parts(_gelu(zv_ref[...]))
        vln = (vhat * g_ref[...] + b_ref[...]).astype(BF16)
        mask = _tril_mask()
        for gi in range(C_GROUPS):
            sl = slice(gi * C_GROUP_DIM, (gi + 1) * C_GROUP_DIM)
            w = jnp.where(mask, ws_ref[gi], 0.0).astype(BF16)
            mixed = _dot(w, vln[:, sl], NN) + bs_ref[:, gi:gi + 1]
            o_ref[:, sl] = (u[:, sl] * mixed).astype(BF16)

    vec = pl.BlockSpec((1, d), lambda i: (0, 0))
    return pl.pallas_call(
        body, name=name, grid=(t // C_CHUNK,),
        in_specs=[pl.BlockSpec((C_CHUNK, d), lambda i: (i, 0)), pl.BlockSpec((C_CHUNK, d), lambda i: (i, 1)),
                  vec, vec, pl.BlockSpec((C_GROUPS, C_CHUNK, C_CHUNK), lambda i: (0, 0, 0)),
                  pl.BlockSpec((C_CHUNK, 128), lambda i: (0, 0))],
        out_specs=pl.BlockSpec((C_CHUNK, d), lambda i: (i, 0)),
        out_shape=jax.ShapeDtypeStruct((t, d), BF16), compiler_params=_params(("parallel",)))(
            zpre, zpre, ln_g, ln_b, ws, bs_t)


def gmlp_bwd(zpre, dgated, ln_g, ln_b, ws, bs_t, *, name):
    t = zpre.shape[0]
    d = D_MODEL

    def body(zu_ref, zv_ref, dg_ref, g_ref, b_ref, ws_ref, bs_ref, dz_ref, dws_ref, dbs_ref, dlg_ref, dlb_ref):
        @pl.when(pl.program_id(0) == 0)
        def _():
            dws_ref[...] = jnp.zeros_like(dws_ref)
            dbs_ref[...] = jnp.zeros_like(dbs_ref)
            dlg_ref[...] = jnp.zeros_like(dlg_ref)
            dlb_ref[...] = jnp.zeros_like(dlb_ref)

        zu, zv = zu_ref[...], zv_ref[...]
        u = _gelu(zu)
        vhat, rstd = _layer_norm_parts(_gelu(zv))
        gam = g_ref[...]
        vln = (vhat * gam + b_ref[...]).astype(BF16)
        dgt = dg_ref[...].astype(F32)
        mask = _tril_mask()
        lane = lax.broadcasted_iota(jnp.int32, (C_CHUNK, 128), 1)
        dbs = jnp.zeros((C_CHUNK, 128), F32)
        du_parts, dvln_parts = [], []
        for gi in range(C_GROUPS):
            sl = slice(gi * C_GROUP_DIM, (gi + 1) * C_GROUP_DIM)
            w = jnp.where(mask, ws_ref[gi], 0.0).astype(BF16)
            mixed = _dot(w, vln[:, sl], NN) + bs_ref[:, gi:gi + 1]
            du_parts.append(dgt[:, sl] * mixed)
            dmixed = dgt[:, sl] * u[:, sl]
            dmb = dmixed.astype(BF16)
            dws_ref[gi] += jnp.where(mask, _dot(dmb, vln[:, sl], NT), 0.0)
            dbs = dbs + jnp.where(lane == gi, jnp.sum(dmixed, axis=-1, keepdims=True), 0.0)
            dvln_parts.append(_dot(w, dmb, TN))
        dbs_ref[...] += dbs
        du = jnp.concatenate(du_parts, axis=-1)
        dvln = jnp.concatenate(dvln_parts, axis=-1)
        dlg_ref[...] += jnp.sum(dvln * vhat, axis=0, keepdims=True)
        dlb_ref[...] += jnp.sum(dvln, axis=0, keepdims=True)
        dvhat = dvln * gam
        dv = rstd * (dvhat - jnp.mean(dvhat, axis=-1, keepdims=True)
                     - vhat * jnp.mean(dvhat * vhat, axis=-1, keepdims=True))
        dz_ref[:, :d] = (du * _dgelu(zu)).astype(BF16)
        dz_ref[:, d:] = (dv * _dgelu(zv)).astype(BF16)

    vec = pl.BlockSpec((1, d), lambda i: (0, 0))
    wsp = pl.BlockSpec((C_GROUPS, C_CHUNK, C_CHUNK), lambda i: (0, 0, 0))
    bsp = pl.BlockSpec((C_CHUNK, 128), lambda i: (0, 0))
    return pl.pallas_call(
        body, name=name, grid=(t // C_CHUNK,),
        in_specs=[pl.BlockSpec((C_CHUNK, d), lambda i: (i, 0)), pl.BlockSpec((C_CHUNK, d), lambda i: (i, 1)),
                  pl.BlockSpec((C_CHUNK, d), lambda i: (i, 0)), vec, vec, wsp, bsp],
        out_specs=[pl.BlockSpec((C_CHUNK, 2 * d), lambda i: (i, 0)), wsp, bsp, vec, vec],
        out_shape=[jax.ShapeDtypeStruct((t, 2 * d), BF16), jax.ShapeDtypeStruct((C_GROUPS, C_CHUNK, C_CHUNK), F32),
                   jax.ShapeDtypeStruct((C_CHUNK, 128), F32), jax.ShapeDtypeStruct((1, d), F32),
                   jax.ShapeDtypeStruct((1, d), F32)],
        compiler_params=_params(("arbitrary",)))(zpre, zpre, dgated, ln_g, ln_b, ws, bs_t)


ATT_SCALE = A_HEAD_DIM ** -0.5
PAIRS = A_HEADS // 2
PAIRS_PER_KV = PAIRS // A_KV_HEADS


def _att_padded(tile):
    lo = lax.broadcasted_iota(jnp.int32, tile.shape, 1) < A_HEAD_DIM
    rolled = pltpu.roll(tile, A_HEAD_DIM, 1)
    zero = jnp.zeros_like(tile)
    return {(0, 0): jnp.where(lo, tile, zero).astype(BF16), (0, 1): jnp.where(lo, zero, rolled).astype(BF16),
            (1, 0): jnp.where(lo, rolled, zero).astype(BF16), (1, 1): jnp.where(lo, zero, tile).astype(BF16)}


def _att_valid(n):
    r = lax.broadcasted_iota(jnp.int32, (WINDOW, 2 * WINDOW), 0)
    c = lax.broadcasted_iota(jnp.int32, (WINDOW, 2 * WINDOW), 1)
    rel = r + WINDOW - c
    return (rel >= 0) & (rel < WINDOW) & ((c >= WINDOW) | (n > 0))


def _att_probs(qp, kpad, sink, valid):
    s = jnp.where(valid, _dot(qp, kpad, NT), NEG_INF)
    m = jnp.maximum(jnp.max(s, axis=-1, keepdims=True), sink)
    p = jnp.exp(s - m)
    e_sink = jnp.exp(sink - m)
    inv = 1.0 / (jnp.sum(p, axis=-1, keepdims=True) + e_sink)
    return p * inv, e_sink * inv


def _att_operands(q_ref, kvc_ref, kvp_ref, s_ref):
    kv = jnp.concatenate([kvp_ref[...], kvc_ref[...]], axis=0)
    kpad, vpad = _att_padded(kv[:, :128]), _att_padded(kv[:, 128:])
    key = lambda h: ((h // 2) // PAIRS_PER_KV, h % 2)
    pairs = [(q_ref[:, j * 128:(j + 1) * 128] * ATT_SCALE).astype(BF16) for j in range(PAIRS)]
    q = jnp.stack([pairs[h // 2] for h in range(A_HEADS)])
    k = jnp.stack([kpad[key(h)] for h in range(A_HEADS)])
    v = jnp.stack([vpad[key(h)] for h in range(A_HEADS)])
    sink = jnp.stack([s_ref[:, h:h + 1] for h in range(A_HEADS)])
    return q, k, v, sink


def _att_specs(t):
    return [pl.BlockSpec((WINDOW, A_Q), lambda n: (n, 0)),
            pl.BlockSpec((WINDOW, 2 * A_KV), lambda n: (n, COL_KV // (2 * A_KV))),
            pl.BlockSpec((WINDOW, 2 * A_KV), lambda n: (jnp.maximum(n - 1, 0), COL_KV // (2 * A_KV))),
            pl.BlockSpec((1, 128), lambda n: (0, 0))]


def att_fwd(proj, sinks, *, name):
    t = proj.shape[0]

    def body(q_ref, kvc_ref, kvp_ref, s_ref, o_ref):
        n = pl.program_id(0)
        q, k, v, sink = _att_operands(q_ref, kvc_ref, kvp_ref, s_ref)
        w, _ = _att_probs(q, k, sink, _att_valid(n))
        o = _dot(w.astype(BF16), v, NN)
        for j in range(PAIRS):
            o_ref[:, j * 128:(j + 1) * 128] = (o[2 * j] + o[2 * j + 1]).astype(BF16)

    return pl.pallas_call(
        body, name=name, grid=(t // WINDOW,), in_specs=_att_specs(t),
        out_specs=pl.BlockSpec((WINDOW, A_Q), lambda n: (n, 0)),
        out_shape=jax.ShapeDtypeStruct((t, A_Q), BF16), compiler_params=_params(("parallel",)))(
            proj, proj, proj, sinks)


def att_bwd(proj, sinks, dout, *, name):
    t = proj.shape[0]

    def body(q_ref, kvc_ref, kvp_ref, s_ref, do_ref, dq_ref, dkc_ref, dkp_ref, ds_ref):
        n = pl.program_id(0)

        @pl.when(n == 0)
        def _():
            ds_ref[...] = jnp.zeros_like(ds_ref)

        q, k, v, sink = _att_operands(q_ref, kvc_ref, kvp_ref, s_ref)
        dop = jnp.stack([do_ref[:, (h // 2) * 128:(h // 2 + 1) * 128] for h in range(A_HEADS)]).astype(BF16)
        w, w_sink = _att_probs(q, k, sink, _att_valid(n))
        dw = _dot(dop, v, NT)
        delta = jnp.sum(w * dw, axis=-1, keepdims=True)
        dsc = (w * (dw - delta)).astype(BF16)
        dsink_h = -jnp.sum(w_sink * delta, axis=1, keepdims=True)
        dq = _dot(dsc, k, NN)
        dk_h = _dot(dsc, q, TN)
        dv_h = _dot(w.astype(BF16), dop, TN)
        lane = lax.broadcasted_iota(jnp.int32, (1, 128), 1)
        dsink = jnp.zeros((1, 128), F32)
        for h in range(A_HEADS):
            dsink = dsink + jnp.where(lane == h, dsink_h[h], 0.0)
        ds_ref[...] += dsink
        for j in range(PAIRS):
            dq_ref[:, j * 128:(j + 1) * 128] = ((dq[2 * j] + dq[2 * j + 1]) * ATT_SCALE).astype(BF16)
        lo = lax.broadcasted_iota(jnp.int32, (2 * WINDOW, 128), 1) < A_HEAD_DIM
        heads_per_kv = A_HEADS // A_KV_HEADS

        def tile(per_head):
            acc = {}
            for kvh in range(A_KV_HEADS):
                for half in range(2):
                    hs = range(kvh * heads_per_kv + half, (kvh + 1) * heads_per_kv, 2)
                    acc[(kvh, half)] = functools.reduce(lambda a, b: a + b, [per_head[h] for h in hs])
            return jnp.where(lo, acc[(0, 0)] + pltpu.roll(acc[(0, 1)], A_HEAD_DIM, 1),
                             pltpu.roll(acc[(1, 0)], A_HEAD_DIM, 1) + acc[(1, 1)])

        dkv = jnp.concatenate([tile(dk_h), tile(dv_h)], axis=1)
        dkp_ref[...] = dkv[:WINDOW]
        dkc_ref[...] = dkv[WINDOW:]

    kvo = pl.BlockSpec((WINDOW, 2 * A_KV), lambda n: (n, 0))
    return pl.pallas_call(
        body, name=name, grid=(t // WINDOW,),
        in_specs=_att_specs(t) + [pl.BlockSpec((WINDOW, A_Q), lambda n: (n, 0))],
        out_specs=[pl.BlockSpec((WINDOW, A_Q), lambda n: (n, 0)), kvo, kvo, pl.BlockSpec((1, 128), lambda n: (0, 0))],
        out_shape=[jax.ShapeDtypeStruct((t, A_Q), BF16), jax.ShapeDtypeStruct((t, 2 * A_KV), F32),
                   jax.ShapeDtypeStruct((t, 2 * A_KV), F32), jax.ShapeDtypeStruct((1, 128), F32)],
        compiler_params=_params(("arbitrary",)))(proj, proj, proj, sinks, dout)


QK_SCALE = B_HEAD_DIM ** -0.5
PREP_COLS = 256
PREP_NCB = 3 * B_W // PREP_COLS
HALO = 8
PREP_ROWS = 512


def _roll_rows(x, shift):
    n = x.shape[0]
    return x if shift % n == 0 else pltpu.roll(x, shift % n, 0)


def _conv_taps(xe, w):
    xs = [_roll_rows(xe, CONV_K - 1 - i) for i in range(CONV_K)]
    c = w[0:1] * xs[0]
    for i in range(1, CONV_K):
        c = c + w[i:i + 1] * xs[i]
    return xs, c


def dprep_fwd(proj, conv_w, *, name):
    t = proj.shape[0]
    tt = min(PREP_ROWS, t)
    col0 = COL_QKVB // PREP_COLS

    def body(x_ref, h_ref, w_ref, o_ref):
        cb, n = pl.program_id(0), pl.program_id(1)
        halo = jnp.where(n > 0, h_ref[...], 0.0)
        xe = jnp.concatenate([halo, x_ref[...]], axis=0)
        _, c = _conv_taps(xe, w_ref[...])
        y = _silu(c)[HALO:]
        parts = []
        for hh in range(PREP_COLS // B_HEAD_DIM):
            yh = y[:, hh * B_HEAD_DIM:(hh + 1) * B_HEAD_DIM]
            parts.append(yh * lax.rsqrt(jnp.sum(yh * yh, axis=-1, keepdims=True) + EPS))
        nrm = jnp.concatenate(parts, axis=-1)
        o_ref[...] = jnp.where(cb < 4, nrm * QK_SCALE, jnp.where(cb < 8, nrm, y))

    return pl.pallas_call(
        body, name=name, grid=(PREP_NCB, t // tt),
        in_specs=[pl.BlockSpec((tt, PREP_COLS), lambda cb, n: (n, col0 + cb)),
                  pl.BlockSpec((HALO, PREP_COLS), lambda cb, n: (jnp.maximum(n * (tt // HALO) - 1, 0), col0 + cb)),
                  pl.BlockSpec((CONV_K, PREP_COLS), lambda cb, n: (0, cb))],
        out_specs=pl.BlockSpec((tt, PREP_COLS), lambda cb, n: (n, cb)),
        out_shape=jax.ShapeDtypeStruct((t, 3 * B_W), F32), compiler_params=_params(("parallel", "parallel")))(
            proj, proj, conv_w)


def dprep_bwd(proj, conv_w, dqkvn, *, name):
    t = proj.shape[0]
    tt = min(PREP_ROWS, t)
    nb = t // tt
    col0 = COL_QKVB // PREP_COLS
    n8 = t // HALO

    def body(xc_ref, xb_ref, xa_ref, dc_ref, da_ref, w_ref, dx_ref, dw_ref):
        cb, n = pl.program_id(0), pl.program_id(1)

        @pl.when(n == 0)
        def _():
            dw_ref[...] = jnp.zeros_like(dw_ref)

        w = w_ref[...]
        xe = jnp.concatenate([jnp.where(n > 0, xb_ref[...], 0.0), xc_ref[...], xa_ref[...]], axis=0)
        xs, c = _conv_taps(xe, w)
        sg = _sigmoid(c)
        y = c * sg
        dout = jnp.concatenate([jnp.zeros((HALO, PREP_COLS), F32), dc_ref[...],
                                jnp.where(n < nb - 1, da_ref[...], 0.0)], axis=0)
        dsc = jnp.where(cb < 4, QK_SCALE, 1.0)
        parts = []
        for hh in range(PREP_COLS // B_HEAD_DIM):
            sl = slice(hh * B_HEAD_DIM, (hh + 1) * B_HEAD_DIM)
            yh, doh = y[:, sl], dout[:, sl] * dsc
            r = lax.rsqrt(jnp.sum(yh * yh, axis=-1, keepdims=True) + EPS)
            parts.append(doh * r - yh * (r * r * r) * jnp.sum(doh * yh, axis=-1, keepdims=True))
        dy = jnp.where(cb < 8, jnp.concatenate(parts, axis=-1), dout)
        dcv = dy * sg * (1.0 + c * (1.0 - sg))
        dxe = w[CONV_K - 1:CONV_K] * dcv
        for i in range(CONV_K - 1):
            dxe = dxe + w[i:i + 1] * _roll_rows(dcv, -(CONV_K - 1 - i))
        dx_ref[...] = dxe[HALO:HALO + tt].astype(BF16)
        for i in range(CONV_K):
            dw_ref[i:i + 1, :] += jnp.sum((dcv * xs[i])[HALO:HALO + tt], axis=0, keepdims=True)

    def after(n):
        return jnp.minimum((n + 1) * (tt // HALO), n8 - 1)

    return pl.pallas_call(
        body, name=name, grid=(PREP_NCB, nb),
        in_specs=[pl.BlockSpec((tt, PREP_COLS), lambda cb, n: (n, col0 + cb)),
                  pl.BlockSpec((HALO, PREP_COLS), lambda cb, n: (jnp.maximum(n * (tt // HALO) - 1, 0), col0 + cb)),
                  pl.BlockSpec((HALO, PREP_COLS), lambda cb, n: (after(n), col0 + cb)),
                  pl.BlockSpec((tt, PREP_COLS), lambda cb, n: (n, cb)),
                  pl.BlockSpec((HALO, PREP_COLS), lambda cb, n: (after(n), cb)),
                  pl.BlockSpec((CONV_K, PREP_COLS), lambda cb, n: (0, cb))],
        out_specs=[pl.BlockSpec((tt, PREP_COLS), lambda cb, n: (n, cb)),
                   pl.BlockSpec((CONV_K, PREP_COLS), lambda cb, n: (0, cb))],
        out_shape=[jax.ShapeDtypeStruct((t, 3 * B_W), BF16), jax.ShapeDtypeStruct((CONV_K, 3 * B_W), F32)],
        compiler_params=_params(("parallel", "arbitrary")))(proj, proj, proj, dqkvn, dqkvn, conv_w)


def _softplus(z):
    return jnp.maximum(z, 0.0) + jnp.log(1.0 + jnp.exp(-jnp.abs(z)))


def gates_fwd(proj, alog_pad, dtb_pad, *, name):
    t = proj.shape[0]

    def body(x_ref, a_ref, b_ref, o_ref):
        raw = x_ref[...]
        lane = lax.broadcasted_iota(jnp.int32, raw.shape, 1)
        g = -jnp.exp(a_ref[...]) * _softplus(raw + b_ref[...])
        o_ref[...] = jnp.where(lane < B_HEADS, _sigmoid(raw), jnp.where(lane < 2 * B_HEADS, g, 0.0))

    vec = pl.BlockSpec((1, 128), lambda n: (0, 0))
    return pl.pallas_call(
        body, name=name, grid=(t // ROWS,),
        in_specs=[pl.BlockSpec((ROWS, 128), lambda n: (n, COL_GATE // 128)), vec, vec],
        out_specs=pl.BlockSpec((ROWS, 128), lambda n: (n, 0)),
        out_shape=jax.ShapeDtypeStruct((t, 128), F32), compiler_params=_params(("parallel",)))(
            proj, alog_pad, dtb_pad)


def gates_bwd(proj, alog_pad, dtb_pad, dgates, *, name):
    t = proj.shape[0]

    def body(x_ref, a_ref, b_ref, dg_ref, dx_ref, da_ref, db_ref):
        @pl.when(pl.program_id(0) == 0)
        def _():
            da_ref[...] = jnp.zeros_like(da_ref)
            db_ref[...] = jnp.zeros_like(db_ref)

        raw, dgt = x_ref[...], dg_ref[...]
        lane = lax.broadcasted_iota(jnp.int32, raw.shape, 1)
        is_beta, is_g = lane < B_HEADS, (lane >= B_HEADS) & (lane < 2 * B_HEADS)
        beta = _sigmoid(raw)
        z = raw + b_ref[...]
        neg_a = -jnp.exp(a_ref[...])
        d_z = jnp.where(is_g, dgt * neg_a * _sigmoid(z), 0.0)
        dx_ref[...] = jnp.where(is_beta, dgt * beta * (1.0 - beta), d_z).astype(BF16)
        db_ref[...] += jnp.sum(d_z, axis=0, keepdims=True)
        da_ref[...] += jnp.sum(jnp.where(is_g, dgt * neg_a * _softplus(z), 0.0), axis=0, keepdims=True)

    vec = pl.BlockSpec((1, 128), lambda n: (0, 0))
    row = pl.BlockSpec((ROWS, 128), lambda n: (n, 0))
    return pl.pallas_call(
        body, name=name, grid=(t // ROWS,),
        in_specs=[pl.BlockSpec((ROWS, 128), lambda n: (n, COL_GATE // 128)), vec, vec, row],
        out_specs=[row, vec, vec],
        out_shape=[jax.ShapeDtypeStruct((t, 128), BF16), jax.ShapeDtypeStruct((1, 128), F32),
                   jax.ShapeDtypeStruct((1, 128), F32)],
        compiler_params=_params(("arbitrary",)))(proj, alog_pad, dtb_pad, dgates)


def _split2(a):
    hi = a.astype(BF16)
    return hi, (a - hi.astype(F32)).astype(BF16)


def _dotp(a, b, dims, passes):
    if passes == 1:
        return _dot(a.astype(BF16), b.astype(BF16), dims)
    ah, al = _split2(a)
    bh, bl = _split2(b)
    return _dot(ah, bh, dims) + (_dot(ah, bl, dims) + _dot(al, bh, dims))


_GRAD_DIMS = {NN: ((NT, False), (TN, False)), NT: ((NN, False), (TN, True)), TN: ((NT, True), (NN, False))}


def _make_mm(dims, passes, grad_passes):
    (da_dims, da_swap), (db_dims, db_swap) = _GRAD_DIMS[dims]

    @jax.custom_vjp
    def mm(a, b):
        return _dotp(a, b, dims, passes)

    def fwd(a, b):
        return _dotp(a, b, dims, passes), (a, b)

    def bwd(saved, ct):
        a, b = saved
        da = _dotp(b, ct, da_dims, grad_passes) if da_swap else _dotp(ct, b, da_dims, grad_passes)
        db = _dotp(ct, a, db_dims, grad_passes) if db_swap else _dotp(a, ct, db_dims, grad_passes)
        return da, db

    mm.defvjp(fwd, bwd)
    return mm


MM1 = {d: _make_mm(d, 1, 1) for d in (NN, NT, TN)}
MM3 = {d: _make_mm(d, 3, 1) for d in (NN, NT, TN)}


def _neumann_value(n):
    c = n.shape[-1]
    eye = (lax.broadcasted_iota(jnp.int32, (c, c), 0) == lax.broadcasted_iota(jnp.int32, (c, c), 1)).astype(F32)
    inv, pw = eye + n, n
    for _ in range(5):
        pw = _dotp(pw, pw, NN, 3)
        inv = inv + _dotp(inv, pw, NN, 3)
    return inv


@jax.custom_vjp
def _neumann_inverse(n):
    return _neumann_value(n)


def _neumann_fwd(n):
    inv = _neumann_value(n)
    return inv, inv


def _neumann_bwd(inv, ct):
    return (_dotp(_dotp(inv, ct, TN, 1), inv, NT, 1),)


_neumann_inverse.defvjp(_neumann_fwd, _neumann_bwd)


def _tri_ones(lower):
    r = lax.broadcasted_iota(jnp.int32, (DN_CHUNK, DN_CHUNK), 0)
    c = lax.broadcasted_iota(jnp.int32, (DN_CHUNK, DN_CHUNK), 1)
    return (r >= c if lower else r <= c).astype(BF16)


def _tri_sum(x, lower):
    tri = _tri_ones(lower)
    hi = x.astype(BF16)
    r1 = x - hi.astype(F32)
    mid = r1.astype(BF16)
    lo = (r1 - mid.astype(F32)).astype(BF16)
    return _dot(tri, hi, NN) + (_dot(tri, mid, NN) + _dot(tri, lo, NN))


def _delta_chunk(s0, q, k, v, beta, gam_c, gam_r):
    c = DN_CHUNK
    r = lax.broadcasted_iota(jnp.int32, (c, c), 0)
    cc = lax.broadcasted_iota(jnp.int32, (c, c), 1)
    incl, strict = r >= cc, r > cc
    decay = jnp.exp(jnp.where(incl, gam_c - gam_r, NEG_INF))
    g_last = gam_c[:, c - 1:c, :]
    e_gam, e_rest, e_last = jnp.exp(gam_c), jnp.exp(g_last - gam_c), jnp.exp(g_last)
    a_neg = -jnp.where(strict, beta * MM1[NT](k, k) * decay, 0.0)
    inv = _neumann_inverse(a_neg)
    uw = MM3[NN](inv,jnp.concatenate([v * beta, k * (beta * e_gam)], axis=-1))
    u, w = uw[..., :B_HEAD_DIM], uw[..., B_HEAD_DIM:]
    qk = MM1[NT](q, k) * decay
    v_new = u - MM1[NN](w, s0)
    o = MM1[NN](q * e_gam, s0) + MM1[NN](qk, v_new)
    s1 = s0 * e_last + MM1[TN](k * e_rest, v_new)
    return s1, o


def _delta_operands(q_ref, k_ref, v_ref, gt):
    heads = lambda ref: jnp.stack([ref[:, h * B_HEAD_DIM:(h + 1) * B_HEAD_DIM] for h in range(B_HEADS)])
    gam = _tri_sum(gt, True)
    gam_t = gam.T
    beta = jnp.stack([gt[:, h:h + 1] for h in range(B_HEADS)])
    gam_c = jnp.stack([gam[:, B_HEADS + h:B_HEADS + h + 1] for h in range(B_HEADS)])
    gam_r = jnp.stack([gam_t[B_HEADS + h:B_HEADS + h + 1, :] for h in range(B_HEADS)])
    return heads(q_ref), heads(k_ref), heads(v_ref), beta, gam_c, gam_r


def delta_fwd(qkvn, gates, *, name):
    t = qkvn.shape[0]
    nc = t // DN_CHUNK

    def body(q_ref, k_ref, v_ref, g_ref, o_ref, ss_ref, state):
        @pl.when(pl.program_id(0) == 0)
        def _():
            state[...] = jnp.zeros_like(state)

        s0 = state[...]
        ss_ref[...] = s0
        s1, o = _delta_chunk(s0, *_delta_operands(q_ref, k_ref, v_ref, g_ref[...]))
        state[...] = s1
        for h in range(B_HEADS):
            o_ref[:, h * B_HEAD_DIM:(h + 1) * B_HEAD_DIM] = o[h]

    blk = lambda j: pl.BlockSpec((DN_CHUNK, B_W), lambda n: (n, j))
    return pl.pallas_call(
        body, name=name, grid=(nc,),
        in_specs=[blk(0), blk(1), blk(2), pl.BlockSpec((DN_CHUNK, 128), lambda n: (n, 0))],
        out_specs=[blk(0), pl.BlockSpec((None, B_HEADS, B_HEAD_DIM, B_HEAD_DIM), lambda n: (n, 0, 0, 0))],
        out_shape=[jax.ShapeDtypeStruct((t, B_W), F32),
                   jax.ShapeDtypeStruct((nc, B_HEADS, B_HEAD_DIM, B_HEAD_DIM), F32)],
        scratch_shapes=[pltpu.VMEM((B_HEADS, B_HEAD_DIM, B_HEAD_DIM), F32)],
        compiler_params=_params(("arbitrary",)))(qkvn, qkvn, qkvn, gates)


def delta_bwd(qkvn, gates, ssave, do, *, name):
    t = qkvn.shape[0]
    nc = t // DN_CHUNK

    def body(q_ref, k_ref, v_ref, g_ref, ss_ref, do_ref, dx_ref, dg_ref, dstate):
        @pl.when(pl.program_id(0) == 0)
        def _():
            dstate[...] = jnp.zeros_like(dstate)

        lane = lax.broadcasted_iota(jnp.int32, (DN_CHUNK, 128), 1)
        row = lax.broadcasted_iota(jnp.int32, (128, DN_CHUNK), 0)
        dbeta_all = jnp.zeros((DN_CHUNK, 128), F32)
        dgam_c_all = jnp.zeros((DN_CHUNK, 128), F32)
        dgam_r_all = jnp.zeros((128, DN_CHUNK), F32)
        _, vjp = jax.vjp(_delta_chunk, ss_ref[...], *_delta_operands(q_ref, k_ref, v_ref, g_ref[...]))
        do = jnp.stack([do_ref[:, h * B_HEAD_DIM:(h + 1) * B_HEAD_DIM] for h in range(B_HEADS)])
        ds0, dq, dk, dv, dbeta, dgam_c, dgam_r = vjp((dstate[...], do))
        dstate[...] = ds0
        for h in range(B_HEADS):
            dx_ref[:, h * B_HEAD_DIM:(h + 1) * B_HEAD_DIM] = dq[h]
            dx_ref[:, B_W + h * B_HEAD_DIM:B_W + (h + 1) * B_HEAD_DIM] = dk[h]
            dx_ref[:, 2 * B_W + h * B_HEAD_DIM:2 * B_W + (h + 1) * B_HEAD_DIM] = dv[h]
            dbeta_all = dbeta_all + jnp.where(lane == h, dbeta[h], 0.0)
            dgam_c_all = dgam_c_all + jnp.where(lane == B_HEADS + h, dgam_c[h], 0.0)
            dgam_r_all = dgam_r_all + jnp.where(row == B_HEADS + h, dgam_r[h], 0.0)
        dg_ref[...] = dbeta_all + _tri_sum(dgam_c_all + dgam_r_all.T, False)

    blk = lambda j: pl.BlockSpec((DN_CHUNK, B_W), lambda n: (nc - 1 - n, j))
    gsp = pl.BlockSpec((DN_CHUNK, 128), lambda n: (nc - 1 - n, 0))
    return pl.pallas_call(
        body, name=name, grid=(nc,),
        in_specs=[blk(0), blk(1), blk(2), gsp,
                  pl.BlockSpec((None, B_HEADS, B_HEAD_DIM, B_HEAD_DIM), lambda n: (nc - 1 - n, 0, 0, 0)), blk(0)],
        out_specs=[pl.BlockSpec((DN_CHUNK, 3 * B_W), lambda n: (nc - 1 - n, 0)), gsp],
        out_shape=[jax.ShapeDtypeStruct((t, 3 * B_W), F32), jax.ShapeDtypeStruct((t, 128), F32)],
        scratch_shapes=[pltpu.VMEM((B_HEADS, B_HEAD_DIM, B_HEAD_DIM), F32)],
        compiler_params=_params(("arbitrary",)))(qkvn, qkvn, qkvn, gates, ssave, do)


GNORM_ROWS = 1024


def gnorm_fwd(o, proj, onorm, *, name):
    t = o.shape[0]

    def body(o_ref, z_ref, w_ref, out_ref):
        ov = o_ref[...]
        r = lax.rsqrt(jnp.mean(ov * ov, axis=-1, keepdims=True) + EPS)
        out_ref[...] = (ov * r * w_ref[...] * _silu(z_ref[...])).astype(BF16)

    rows = min(GNORM_ROWS, t)
    blk = pl.BlockSpec((rows, B_HEAD_DIM), lambda n, h: (n, h))
    return pl.pallas_call(
        body, name=name, grid=(t // rows, B_HEADS),
        in_specs=[blk, pl.BlockSpec((rows, B_HEAD_DIM), lambda n, h: (n, COL_Z // B_HEAD_DIM + h)),
                  pl.BlockSpec((1, B_HEAD_DIM), lambda n, h: (0, 0))],
        out_specs=blk, out_shape=jax.ShapeDtypeStruct((t, B_W), BF16),
        compiler_params=_params(("parallel", "parallel")))(o, proj, onorm)


def gnorm_bwd(o, proj, onorm, dout, *, dcol0, name):
    t = o.shape[0]

    def body(o_ref, z_ref, w_ref, d_ref, do_ref, dz_ref, dw_ref):
        @pl.when((pl.program_id(0) == 0) & (pl.program_id(1) == 0))
        def _():
            dw_ref[...] = jnp.zeros_like(dw_ref)

        ov, zv, wv, dv = o_ref[...], z_ref[...], w_ref[...], d_ref[...].astype(F32)
        r = lax.rsqrt(jnp.mean(ov * ov, axis=-1, keepdims=True) + EPS)
        nrm = ov * r
        dz_ref[...] = (dv * nrm * wv * _dsilu(zv)).astype(BF16)
        da = dv * _silu(zv)
        dw_ref[...] += jnp.sum(da * nrm, axis=0, keepdims=True)
        dn = da * wv
        do_ref[...] = r * dn - ov * (r * r * r) * jnp.mean(dn * ov, axis=-1, keepdims=True)

    rows = min(GNORM_ROWS, t)
    blk = pl.BlockSpec((rows, B_HEAD_DIM), lambda n, h: (n, h))
    vec = pl.BlockSpec((1, B_HEAD_DIM), lambda n, h: (0, 0))
    return pl.pallas_call(
        body, name=name, grid=(t // rows, B_HEADS),
        in_specs=[blk, pl.BlockSpec((rows, B_HEAD_DIM), lambda n, h: (n, COL_Z // B_HEAD_DIM + h)), vec,
                  pl.BlockSpec((rows, B_HEAD_DIM), lambda n, h: (n, dcol0 // B_HEAD_DIM + h))],
        out_specs=[blk, blk, vec],
        out_shape=[jax.ShapeDtypeStruct((t, B_W), F32), jax.ShapeDtypeStruct((t, B_W), BF16),
                   jax.ShapeDtypeStruct((1, B_HEAD_DIM), F32)],
        compiler_params=_params(("arbitrary", "arbitrary")))(o, proj, onorm, dout)


def _ffn_fwd(h, norm_g, wg, wu, wd, tm, tag):
    hn = rms_fwd(h, norm_g, name=f"ffn{tag}_norm")
    gate, up, act = mm_gate_up(hn, wg, wu, tm=min(512, tm), tn=1408, tk=2048, name=f"ffn{tag}_gate_up")
    h_out = mm_nn(act, wd, tm=tm, tn=2048, tk=512, out_dtype=F32, res=h, name=f"ffn{tag}_down")
    return h_out, (hn, gate, up, act)


def _ffn_bwd(dh, h, norm_g, wg, wu, wd, saved, tm, tag, emit):
    hn, gate, up, act = saved
    dwd = mm_tn(act, dh, shards=1, tm=tm, tn=1024, tk=1408, out_dtype=BF16, name=f"ffn{tag}_dwd")[0]
    dgate, dup = mm_down_bwd(dh, wd, gate, up, tm=tm, tn=512, tk=2048, name=f"ffn{tag}_dact")
    dwg = mm_tn(hn, dgate, shards=N_SHARD, tm=tm, tn=1408, tk=1024, out_dtype=BF16, name=f"ffn{tag}_dwg")
    dwu = mm_tn(hn, dup, shards=N_SHARD, tm=tm, tn=1408, tk=1024, out_dtype=BF16, name=f"ffn{tag}_dwu")
    started = emit(f"ffn{tag}", {"gate": dwg, "up": dwu, "down": dwd})
    dhn = mm_nt(dgate, wg, tm=tm, tn=1024, tk=1408, out_dtype=F32, name=f"ffn{tag}_dhn_g")
    dh_in, dnorm = dgrad_rms_bwd(dup, wu, NT, h, norm_g + started, dh, tm=min(512, tm), tk=1408, res=dhn,
                                 name=f"ffn{tag}_dhn_u_dnorm")
    return dh_in, dnorm


def _local_step(x, target, w, get, emit):
    t = x.shape[0]
    tm = min(1024, t)
    g = {}

    hn0 = rms_fwd(x, w["even_norm"], name="l0_norm")
    w.update(get("even_in", hn0))
    proj = mm_nt(hn0, w["even_w_in"], tm=tm, tn=512, tk=2048, out_dtype=F32, name="l0_w_in")
    out_a = att_fwd(proj, w["sinks"], name="l0_att")
    qkvn = dprep_fwd(proj, w["even_conv"], name="l0_prep")
    gates = gates_fwd(proj, w["a_log"], w["dt_bias"], name="l0_gates")
    o_delta, ssave = delta_fwd(qkvn, gates, name="l0_delta")
    w.update(get("even_out", o_delta))
    out_b = gnorm_fwd(o_delta, proj, w["onorm"], name="l0_gnorm")
    mix0 = jnp.concatenate([out_a, out_b], axis=-1)
    h1 = mm_nn(mix0, w["even_w_out"], tm=tm, tn=1024, tk=2048, out_dtype=F32, res=x, name="l0_w_out")
    f0 = get("ffn0", h1)
    h2, ffn0 = _ffn_fwd(h1, w["ffn_norm"][0:1] + f0["tok"], f0["gate"], f0["up"], f0["down"], tm, 0)
    hn2 = rms_fwd(h2, w["odd_norm"], name="l1_norm")
    w.update(get("odd", hn2))
    zpre = mm_nn(hn2, w["odd_w_in"], tm=tm, tn=1024, tk=2048, out_dtype=F32, name="l1_w_in")
    gated = gmlp_fwd(zpre, w["odd_ln_g"], w["odd_ln_b"], w["odd_w_s"], w["odd_b_s"], name="l1_gmlp")
    h3 = mm_nn(gated, w["odd_w_out"], tm=tm, tn=1024, tk=2048, out_dtype=F32, res=h2, name="l1_w_out")
    f1 = get("ffn1", h3)
    h4, ffn1 = _ffn_fwd(h3, w["ffn_norm"][1:2] + f1["tok"], f1["gate"], f1["up"], f1["down"], tm, 1)
    loss, dh4, g["final_norm"] = loss_head(h4, w["final_norm"], target, name="loss_head")

    dh3, dn1 = _ffn_bwd(dh4, h3, w["ffn_norm"][1:2], f1["gate"], f1["up"], f1["down"], ffn1, tm, 1, emit)
    dw_out_o = mm_tn(gated, dh3, shards=1, tm=tm, tn=1024, tk=1024, out_dtype=BF16, name="l1_dw_out")[0]
    dgated = mm_nt(dh3, w["odd_w_out"], tm=tm, tn=1024, tk=2048, out_dtype=BF16, name="l1_dgated")
    dzpre, g["odd_w_s"], g["odd_b_s"], g["odd_ln_g"], g["odd_ln_b"] = gmlp_bwd(
        zpre, dgated, w["odd_ln_g"], w["odd_ln_b"], w["odd_w_s"], w["odd_b_s"], name="l1_dgmlp")
    dw_in_o = mm_tn(hn2, dzpre, shards=N_SHARD, tm=tm, tn=1024, tk=1024, out_dtype=BF16, name="l1_dw_in")
    started = emit("odd", {"odd_w_in": dw_in_o, "odd_w_out": dw_out_o})
    dh2, g["odd_norm"] = dgrad_rms_bwd(dzpre, w["odd_w_in"], NT, h2, w["odd_norm"] + started, dh3, tm=min(512, tm),
                                       tk=1024, name="l1_dhn_dnorm")
    dh1, dn0 = _ffn_bwd(dh2, h1, w["ffn_norm"][0:1], f0["gate"], f0["up"], f0["down"], ffn0, tm, 0, emit)
    g["ffn_norm"] = jnp.concatenate([dn0, dn1], axis=0)
    dw_out_e = mm_tn(mix0, dh1, shards=1, tm=tm, tn=1024, tk=1024, out_dtype=BF16, name="l0_dw_out")[0]
    started = emit("even_out", {"even_w_out": dw_out_e})
    dmix = mm_nt(dh1, w["even_w_out"], tm=tm, tn=1024, tk=2048, out_dtype=F32, name="l0_dmix")
    dq_a, dkv_cur, dkv_prev, g["sinks"] = att_bwd(proj, w["sinks"] + started, dmix, name="l0_datt")
    dkv = dkv_cur + jnp.concatenate([dkv_prev[WINDOW:], jnp.zeros((WINDOW, 2 * A_KV), F32)], axis=0)
    do_delta, dz, g["onorm"] = gnorm_bwd(o_delta, proj, w["onorm"], dmix, dcol0=A_Q, name="l0_dgnorm")
    dqkvn, dgates = delta_bwd(qkvn, gates, ssave, do_delta, name="l0_ddelta")
    dqkv_b, g["even_conv"] = dprep_bwd(proj, w["even_conv"], dqkvn, name="l0_dprep")
    draw, g["a_log"], g["dt_bias"] = gates_bwd(proj, w["a_log"], w["dt_bias"], dgates, name="l0_dgates")
    dproj = jnp.concatenate([dq_a, dkv.astype(BF16), dqkv_b, dz, draw,
                             jnp.zeros((t, EVEN_IN_PAD - COL_GATE - 128), BF16)], axis=-1)
    dw_in_e = mm_tn(dproj, hn0, shards=1, tm=tm, tn=1024, tk=1408, out_dtype=BF16, name="l0_dw_in")[0]
    grad_x, g["even_norm"] = dgrad_rms_bwd(dproj, w["even_w_in"], NN, x, w["even_norm"], dh1, tm=min(512, tm), tk=512,
                                           name="l0_dhn_dnorm")
    emit("even_in", {"even_w_in": dw_in_e, "small": g})
    return loss, grad_x


ANY = pl.BlockSpec(memory_space=pl.ANY)
N_DEV = 8


def _place():
    return lax.axis_index("x"), lax.axis_index("y"), lax.axis_index("c")


def _chip_peers(x, y, c):
    return [((1 - x, y, c), 2 * (1 - x) + y), ((x, 1 - y, c), 2 * x + 1 - y), ((1 - x, 1 - y, c), 2 * (1 - x) + 1 - y)]


HBM = pl.BlockSpec(memory_space=pltpu.HBM)
SEM = pl.BlockSpec(memory_space=pltpu.SEMAPHORE)
EFFECT = pltpu.SideEffectType.DATAFLOW_SIDE_EFFECTING
N_PEER = 3


def _half(ref, c):
    r, cols = ref.shape
    tile_rows = 32 // jnp.dtype(ref.dtype).itemsize
    if (r // 2) % tile_rows == 0:
        return ref.at[pl.ds(c * (r // 2), r // 2)]
    assert (cols // 2) % 128 == 0, ref.shape
    return ref.at[:, pl.ds(c * (cols // 2), cols // 2)]


def _gather_plan(srcs, lands, send, recv):
    x, y, c = _place()
    return [pltpu.make_async_remote_copy(src_ref=_half(srcs[i], c), dst_ref=_half(lands[i].at[2 * x + y], c),
                                         send_sem=send.at[N_PEER * i + k], recv_sem=recv.at[N_PEER * i + k],
                                         device_id=peer, device_id_type=MESH_ID)
            for i in range(len(srcs)) for k, (peer, _) in enumerate(_chip_peers(x, y, c))]


def _relay_plan(srcs, lands, send, recv):
    x, y, c = _place()
    return [pltpu.make_async_remote_copy(src_ref=_half(lands[i].at[idx], c), dst_ref=_half(lands[i].at[idx], c),
                                         send_sem=send.at[N_PEER * i + k], recv_sem=recv.at[N_PEER * i + k],
                                         device_id=(x, y, 1 - c), device_id_type=MESH_ID)
            for i in range(len(srcs)) for k, (_, idx) in enumerate(_chip_peers(x, y, c))]


def _scatter_plan(srcs, lands, send, recv):
    x, y, c = _place()
    return [pltpu.make_async_remote_copy(src_ref=srcs[i].at[idx], dst_ref=lands[i].at[k], send_sem=send.at[N_PEER * i + k],
                                         recv_sem=recv.at[N_PEER * i + k], device_id=peer, device_id_type=MESH_ID)
            for i in range(len(srcs)) for k, (peer, idx) in enumerate(_chip_peers(x, y, c))]


def _swap_plan(srcs, lands, send, recv):
    x, y, c = _place()
    return [pltpu.make_async_remote_copy(src_ref=srcs[i], dst_ref=lands[i], send_sem=send.at[N_PEER * i],
                                         recv_sem=recv.at[N_PEER * i], device_id=(x, y, 1 - c), device_id_type=MESH_ID)
            for i in range(len(srcs))]


def copies_start(plan, srcs, lands, after, *, name):
    n = len(srcs)
    both = list(srcs) + list(lands)

    def body(*refs):
        src_refs, land_refs = refs[:n], refs[n:2 * n]
        send, recv = refs[2 * n + 1], refs[2 * n + 2]
        for cp in plan(src_refs, land_refs, send, recv):
            cp.start()
        refs[-1][...] = jnp.zeros_like(refs[-1])

    res = pl.pallas_call(
        body, name=name,
        out_shape=(pltpu.SemaphoreType.DMA((n * N_PEER,)), pltpu.SemaphoreType.DMA((n * N_PEER,)),
                   *[pltpu.HBM(a.shape, a.dtype) for a in both], jax.ShapeDtypeStruct((8, 128), F32)),
        in_specs=[HBM] * (2 * n) + [ANY],
        out_specs=(SEM, SEM, *[HBM] * (2 * n), pl.BlockSpec(memory_space=pltpu.VMEM)),
        input_output_aliases={i: 2 + i for i in range(2 * n)},
        compiler_params=pltpu.CompilerParams(has_side_effects=EFFECT))(
            *[pltpu.with_memory_space_constraint(a, pltpu.HBM) for a in both], after)
    return {"send": res[0], "recv": res[1], "srcs": list(res[2:2 + n]), "lands": list(res[2 + n:2 + 2 * n]),
            "token": res[-1]}


def copies_relay(arrived_plan, next_plan, started, after, *, name):
    srcs, lands = started["srcs"], started["lands"]
    n = len(srcs)
    both = srcs + lands

    def body(*refs):
        src_refs, land_refs = refs[:n], refs[n:2 * n]
        send1, recv1 = refs[2 * n], refs[2 * n + 1]
        send2, recv2 = refs[2 * n + 3], refs[2 * n + 4]
        for cp in arrived_plan(src_refs, land_refs, send1, recv1):
            cp.wait_send()
            cp.wait_recv()
        for cp in next_plan(src_refs, land_refs, send2, recv2):
            cp.start()
        refs[-1][...] = jnp.zeros_like(refs[-1])

    res = pl.pallas_call(
        body, name=name,
        out_shape=(pltpu.SemaphoreType.DMA((n * N_PEER,)), pltpu.SemaphoreType.DMA((n * N_PEER,)),
                   *[pltpu.HBM(a.shape, a.dtype) for a in both], jax.ShapeDtypeStruct((8, 128), F32)),
        in_specs=[HBM] * (2 * n) + [SEM, SEM, ANY],
        out_specs=(SEM, SEM, *[HBM] * (2 * n), pl.BlockSpec(memory_space=pltpu.VMEM)),
        input_output_aliases={i: 2 + i for i in range(2 * n)},
        compiler_params=pltpu.CompilerParams(has_side_effects=EFFECT))(*both, started["send"], started["recv"], after)
    return {"send": res[0], "recv": res[1], "srcs": list(res[2:2 + n]), "lands": list(res[2 + n:2 + 2 * n]),
            "token": res[-1]}


def copies_wait(plan, started, after, *, name):
    srcs, lands = started["srcs"], started["lands"]
    n = len(srcs)
    both = srcs + lands

    def body(*refs):
        src_refs, land_refs = refs[:n], refs[n:2 * n]
        send, recv = refs[2 * n], refs[2 * n + 1]
        for cp in plan(src_refs, land_refs, send, recv):
            cp.wait_send()
            cp.wait_recv()

    res = pl.pallas_call(
        body, name=name, out_shape=tuple(pltpu.HBM(a.shape, a.dtype) for a in both),
        in_specs=[HBM] * (2 * n) + [SEM, SEM, ANY], out_specs=(HBM,) * (2 * n),
        input_output_aliases={i: i for i in range(2 * n)},
        compiler_params=pltpu.CompilerParams(has_side_effects=EFFECT))(*both, started["send"], started["recv"], after)
    return list(res[:n]), list(res[n:])


def allgather_small(small, *, name):
    def body(small_ref, out_ref, send, recv, loc):
        x, y, c = _place()
        dev = 4 * x + 2 * y + c
        local = pltpu.make_async_copy(small_ref, out_ref.at[dev], loc)
        remote = []
        for r in range(1, N_DEV):
            fx, fy, fc = (r >> 2) & 1, (r >> 1) & 1, r & 1
            peer = (1 - x if fx else x, 1 - y if fy else y, 1 - c if fc else c)
            remote.append(pltpu.make_async_remote_copy(
                src_ref=small_ref, dst_ref=out_ref.at[dev], send_sem=send.at[r - 1], recv_sem=recv.at[r - 1],
                device_id=peer, device_id_type=MESH_ID))
        local.start()
        for cp in remote:
            cp.start()
        for cp in remote:
            cp.wait()
        local.wait()

    return pl.pallas_call(
        body, name=name, in_specs=[ANY], out_specs=ANY,
        out_shape=jax.ShapeDtypeStruct((N_DEV,) + small.shape, small.dtype),
        scratch_shapes=[pltpu.SemaphoreType.DMA((N_DEV - 1,)), pltpu.SemaphoreType.DMA((N_DEV - 1,)),
                        pltpu.SemaphoreType.DMA(())])(small)


RED_ROWS = 256
RED_COLS = 256


def _red_block(r, c):
    if r % RED_ROWS == 0:
        return RED_ROWS, c
    if c > RED_COLS and c % RED_COLS == 0:
        return r, RED_COLS
    return r, c


def sum_chips(by_owner, me, got, *, name):
    _, r, c = by_owner.shape
    rb, cb = _red_block(r, c)

    def body(me_ref, o_ref, a_ref, b_ref, c_ref, out_ref):
        total = ((o_ref[...].astype(F32) + a_ref[...].astype(F32)) + b_ref[...].astype(F32)) + c_ref[...].astype(F32)
        out_ref[...] = total.astype(BF16)

    gk = lambda k: pl.BlockSpec((None, rb, cb), lambda i, j, me_ref: (k, i, j))
    grid_spec = pltpu.PrefetchScalarGridSpec(
        num_scalar_prefetch=1, grid=(r // rb, c // cb),
        in_specs=[pl.BlockSpec((None, rb, cb), lambda i, j, me_ref: (me_ref[0], i, j)), gk(0), gk(1), gk(2)],
        out_specs=pl.BlockSpec((rb, cb), lambda i, j, me_ref: (i, j)))
    return pl.pallas_call(
        body, name=name, grid_spec=grid_spec, out_shape=jax.ShapeDtypeStruct((r, c), BF16),
        compiler_params=_params(("parallel", "parallel")))(me, by_owner, got, got, got)


def sum_devices(small_all, *, name):
    _, p, c = small_all.shape

    def body(a_ref, out_ref):
        acc = a_ref[0]
        for d in range(1, N_DEV):
            acc = acc + a_ref[d]
        out_ref[...] = acc

    return pl.pallas_call(
        body, name=name, grid=(1,), in_specs=[pl.BlockSpec((N_DEV, p, c), lambda i: (0, 0, 0))],
        out_specs=pl.BlockSpec((p, c), lambda i: (0, 0)), out_shape=jax.ShapeDtypeStruct((p, c), F32),
        compiler_params=_params(("arbitrary",)))(small_all)


def adamw(parts, w, m, v, *, name):
    nl, r, c = w.shape
    assert len(parts) == nl
    npart = len(parts[0])
    rb, cb = _red_block(r, c)
    flat = [a for layer in parts for a in layer]

    def body(*refs):
        p_refs, (w_ref, m_ref, v_ref) = refs[:nl * npart], refs[nl * npart:nl * npart + 3]
        g_ref, d_ref, nm_ref, nv_ref = refs[nl * npart + 3:]
        layer = pl.program_id(0)
        grad = None
        for l in range(nl):
            gl = p_refs[l * npart][...].astype(F32)
            for j in range(1, npart):
                gl = gl + p_refs[l * npart + j][...].astype(F32)
            grad = gl if grad is None else jnp.where(layer == l, gl, grad)
        wv, mv, vv = w_ref[...], m_ref[...], v_ref[...]
        nm = ADAM_B1 * mv + (1.0 - ADAM_B1) * grad
        nv = ADAM_B2 * vv + (1.0 - ADAM_B2) * (grad * grad)
        m_hat = nm / (1.0 - ADAM_B1 ** ADAM_STEP)
        v_hat = nv / (1.0 - ADAM_B2 ** ADAM_STEP)
        g_ref[...] = grad
        d_ref[...] = -ADAM_LR * (m_hat / (jnp.sqrt(v_hat) + ADAM_EPS) + ADAM_WD * wv)
        nm_ref[...] = nm
        nv_ref[...] = nv

    pspec = pl.BlockSpec((rb, cb), lambda l, i, j: (i, j))
    wspec = pl.BlockSpec((None, rb, cb), lambda l, i, j: (l, i, j))
    osh = jax.ShapeDtypeStruct((nl, r, c), F32)
    return pl.pallas_call(
        body, name=name, grid=(nl, r // rb, c // cb), in_specs=[pspec] * (nl * npart) + [wspec] * 3,
        out_specs=[wspec] * 4, out_shape=[osh] * 4,
        compiler_params=_params(("parallel", "parallel", "parallel")))(*flat, w, m, v)


def _rows128(a):
    flat = a.reshape(-1)
    pad = (-flat.shape[0]) % 128
    return jnp.pad(flat, (0, pad)).reshape(-1, 128)


def _pack_rows(arrs, multiple=8):
    rows = jnp.concatenate([_rows128(a.astype(F32)) for a in arrs], axis=0)
    return jnp.pad(rows, ((0, (-rows.shape[0]) % multiple), (0, 0)))


def _unpack_rows(rows, shapes):
    out, r0 = [], 0
    for shp in shapes:
        size = 1
        for s in shp:
            size *= s
        nr = -(-size // 128)
        out.append(rows[r0:r0 + nr].reshape(-1)[:size].reshape(shp))
        r0 += nr
    return out


SMALL_LOCAL_GRADS = ["even_norm", "even_conv", "a_log", "dt_bias", "sinks", "onorm", "odd_norm", "odd_ln_g",
                     "odd_ln_b", "odd_w_s", "odd_b_s", "ffn_norm", "final_norm"]
BIG = ["even_w_in", "even_w_out", "odd_w_in", "odd_w_out", "ffn_w_gate", "ffn_w_up", "ffn_w_down"]
WEIGHTS = ["even_norm", "even_w_in", "even_conv", "even_a_log", "even_dt_bias", "even_sinks", "even_onorm",
           "even_w_out", "odd_norm", "odd_w_in", "odd_ln_g", "odd_ln_b", "odd_w_s", "odd_b_s", "odd_w_out",
           "ffn_norm", "ffn_w_gate", "ffn_w_up", "ffn_w_down", "final_norm"]
SMALL = [n for n in WEIGHTS if n not in BIG]


def kernel(x, even_norm, even_w_in, even_conv, even_a_log, even_dt_bias, even_sinks, even_onorm, even_w_out, odd_norm, odd_w_in, odd_ln_g, odd_ln_b, odd_w_s, odd_b_s, odd_w_out, ffn_norm, ffn_w_gate, ffn_w_up, ffn_w_down, final_norm, loss_target, m_even_norm, m_even_w_in, m_even_conv, m_even_a_log, m_even_dt_bias, m_even_sinks, m_even_onorm, m_even_w_out, m_odd_norm, m_odd_w_in, m_odd_ln_g, m_odd_ln_b, m_odd_w_s, m_odd_b_s, m_odd_w_out, m_ffn_norm, m_ffn_w_gate, m_ffn_w_up, m_ffn_w_down, m_final_norm, v_even_norm, v_even_w_in, v_even_conv, v_even_a_log, v_even_dt_bias, v_even_sinks, v_even_onorm, v_even_w_out, v_odd_norm, v_odd_w_in, v_odd_ln_g, v_odd_ln_b, v_odd_w_s, v_odd_b_s, v_odd_w_out, v_ffn_norm, v_ffn_w_gate, v_ffn_w_up, v_ffn_w_down, v_final_norm):
    args = dict(locals())
    wl = {n: args[n] for n in WEIGHTS}
    ml = {n: args["m_" + n] for n in WEIGHTS}
    vl = {n: args["v_" + n] for n in WEIGHTS}
    me = 2 * lax.axis_index("x") + lax.axis_index("y")

    def landing(a):
        return lax.dynamic_update_index_in_dim(lax.empty((N_SHARD,) + a.shape, a.dtype), a, me, 0)

    gather_groups = {
        "even_in": [even_w_in[0].T], "even_out": [even_w_out[0]],
        "ffn0": [ffn_w_gate[0], ffn_w_up[0], ffn_w_down[0]], "odd": [odd_w_in[0], odd_w_out[0]],
        "ffn1": [ffn_w_gate[1], ffn_w_up[1], ffn_w_down[1]],
    }
    gathering, after = {}, even_norm
    for group, arrs in gather_groups.items():
        srcs = [(a + after[0, 0] if gathering else a).astype(BF16) for a in arrs]
        if group == "even_in":
            srcs.append(_pack_rows([even_conv[0], odd_norm, odd_ln_g, odd_ln_b], multiple=16))
        gathering[group] = copies_start(_gather_plan, srcs, [landing(a) for a in srcs], after,
                                        name=f"gather_{group}_start")
        after = gathering[group]["token"]

    order = list(gather_groups)
    relayed, kept = {}, {}
    sinks_pad = jnp.pad(even_sinks, ((0, 0), (0, 128 - A_HEADS)))

    def relay(group, behind):
        relayed[group] = copies_relay(_gather_plan, _relay_plan, gathering[group], behind,
                                      name=f"gather_{group}_relay")
        return relayed[group]["token"][0:1, 0:1]

    def get(group, behind):
        if group not in relayed:
            relay(group, behind)
        _, lands = copies_wait(_relay_plan, relayed[group], behind, name=f"gather_{group}_wait")
        nxt = order.index(group) + 1
        tok = relay(order[nxt], lands[0]) if nxt < len(order) else jnp.zeros((1, 1), F32)
        if group == "even_in":
            parts = zip(*[_unpack_rows(lands[1][s], [(CONV_K, 768), (1, 512), (1, 512), (1, 512)])
                          for s in range(N_SHARD)])
            conv, onorm, lng, lnb = [jnp.concatenate(p, axis=1) for p in parts]
            w_in = jnp.pad(lands[0].reshape(EVEN_IN, D_MODEL), ((0, EVEN_IN_PAD - EVEN_IN), (0, 0)))
            kept["odd_ln_g"] = lng
            return {"even_w_in": w_in, "even_conv": conv + tok, "odd_norm": onorm, "odd_ln_b": lnb}
        if group == "even_out":
            return {"even_w_out": lands[0].reshape(D_MODEL, D_MODEL), "onorm": even_onorm + tok}
        if group == "odd":
            return {"odd_w_in": lands[0], "odd_w_out": lands[1].reshape(D_MODEL, D_MODEL),
                    "odd_ln_g": kept["odd_ln_g"] + tok}
        return {"gate": lands[0], "up": lands[1], "down": lands[2].reshape(D_FF, D_MODEL), "tok": tok}

    rows4 = lambda a: a.reshape(N_SHARD, a.shape[0] // N_SHARD, a.shape[1])
    scattering, small = {}, {}

    def emit(group, grads):
        behind = even_norm
        if group == "even_in":
            small["local"] = grads["small"]
            small["all"] = behind = allgather_small(_pack_rows([grads["small"][n] for n in SMALL_LOCAL_GRADS]),
                                                    name="allgather_small")
            srcs = [grads["even_w_in"][:EVEN_IN].reshape(N_SHARD, EVEN_IN // N_SHARD, D_MODEL)]
        elif group == "even_out":
            srcs = [rows4(grads["even_w_out"])]
        elif group == "odd":
            srcs = [grads["odd_w_in"], rows4(grads["odd_w_out"])]
        else:
            srcs = [grads["gate"], grads["up"], rows4(grads["down"])]
        lands = [lax.empty((N_PEER,) + a.shape[1:], a.dtype) for a in srcs]
        scattering[group] = copies_start(_scatter_plan, srcs, lands, behind, name=f"scatter_{group}_start")
        return scattering[group]["token"][0:1, 0:1]

    pad816 = lambda a: jnp.pad(a, ((0, 0), (B_HEADS, 128 - 2 * B_HEADS)))
    w = {
        "even_norm": even_norm + after[0:1, 0:1],
        "a_log": pad816(even_a_log), "dt_bias": pad816(even_dt_bias),
        "sinks": sinks_pad,
        "onorm": even_onorm,
        "odd_w_s": odd_w_s[0],
        "odd_b_s": jnp.pad(odd_b_s[0].T, ((0, 0), (0, 128 - C_GROUPS))),
        "ffn_norm": ffn_norm,
        "final_norm": final_norm[None],
    }
    loss_l, grad_x = _local_step(x[0], loss_target[0], w, get, emit)
    loss = lax.psum(loss_l[0, 0], ("x", "y", "c"))

    me1 = me.reshape(1).astype(jnp.int32)
    swapping = {}

    def reduce_chips(group, behind):
        srcs, lands = copies_wait(_scatter_plan, scattering[group], behind, name=f"scatter_{group}_wait")
        partial = [sum_chips(srcs[i], me1, lands[i], name=f"sum_chips_{group}_{i}") for i in range(len(srcs))]
        swapping[group] = copies_start(_swap_plan, partial, [lax.empty(p.shape, p.dtype) for p in partial],
                                       even_norm, name=f"swap_{group}_start")
        return swapping[group]["token"]

    def swapped(group, behind):
        mine, theirs = copies_wait(_swap_plan, swapping[group], behind, name=f"swap_{group}_wait")
        return list(zip(mine, theirs))

    behind = scattering["even_in"]["token"]
    for group in ("ffn1", "ffn0", "odd", "even_out"):
        behind = reduce_chips(group, behind)
    sums = {group: swapped(group, behind) for group in ("ffn1", "ffn0", "odd", "even_out")}
    outs = {}
    parts_of = {"even_w_out": [sums["even_out"][0]], "odd_w_in": [sums["odd"][0]], "odd_w_out": [sums["odd"][1]],
                "ffn_w_gate": [sums["ffn0"][0], sums["ffn1"][0]], "ffn_w_up": [sums["ffn0"][1], sums["ffn1"][1]],
                "ffn_w_down": [sums["ffn0"][2], sums["ffn1"][2]]}
    for n in parts_of:
        outs[n] = adamw(parts_of[n], wl[n], ml[n], vl[n], name=f"adamw_{n}")
    behind = reduce_chips("even_in", outs["ffn_w_down"][1])
    flip = lambda a: jnp.transpose(a, (0, 2, 1))
    outs["even_w_in"] = [flip(o) for o in adamw([swapped("even_in", behind)[0]], flip(wl["even_w_in"]),
                                                flip(ml["even_w_in"]), flip(vl["even_w_in"]),
                                                name="adamw_even_w_in")]

    g = small["local"]
    small_sum = sum_devices(small["all"], name="sum_devices")
    sg = dict(zip(SMALL_LOCAL_GRADS, _unpack_rows(small_sum, [g[n].shape for n in SMALL_LOCAL_GRADS])))
    own_cols = lambda a, width: lax.dynamic_slice_in_dim(a, me * width, width, axis=a.ndim - 1)
    small_grads = {
        "even_norm": sg["even_norm"], "even_conv": own_cols(sg["even_conv"], 768)[None],
        "even_a_log": sg["a_log"][:, B_HEADS:2 * B_HEADS], "even_dt_bias": sg["dt_bias"][:, B_HEADS:2 * B_HEADS],
        "even_sinks": sg["sinks"][:, :A_HEADS], "even_onorm": sg["onorm"],
        "odd_norm": own_cols(sg["odd_norm"], 512), "odd_ln_g": own_cols(sg["odd_ln_g"], 512),
        "odd_ln_b": own_cols(sg["odd_ln_b"], 512), "odd_w_s": sg["odd_w_s"][None],
        "odd_b_s": sg["odd_b_s"][:, :C_GROUPS].T[None], "ffn_norm": sg["ffn_norm"], "final_norm": sg["final_norm"][0],
    }
    packed = [_pack_rows([d[n] for n in SMALL])[None] for d in (small_grads, wl, ml, vl)]
    small_out = adamw([(packed[0][0],)], packed[1], packed[2], packed[3], name="adamw_small")
    shapes = [wl[n].shape for n in SMALL]
    for j in range(4):
        for n, a in zip(SMALL, _unpack_rows(small_out[j][0], shapes)):
            outs.setdefault(n, [None] * 4)[j] = a

    return (loss, grad_x[None], *[outs[n][0] for n in WEIGHTS], *[outs[n][1] for n in WEIGHTS],
            *[outs[n][2] for n in WEIGHTS], *[outs[n][3] for n in WEIGHTS])
```

```python
import functools

import jax
import jax.numpy as jnp
from jax import lax
from jax.experimental import pallas as pl
from jax.experimental.pallas import tpu as pltpu

F32 = jnp.float32
BF16 = jnp.bfloat16
NEG_INF = float("-inf")

D_MODEL = 2048
A_HEADS, A_KV_HEADS, A_HEAD_DIM, WINDOW = 16, 2, 64, 128
B_HEADS, B_HEAD_DIM, CONV_K, DN_CHUNK = 8, 128, 4, 64
C_GROUPS, C_CHUNK = 8, 128
C_GROUP_DIM = D_MODEL // C_GROUPS
D_FF = 5632
EPS = 1e-6
A_Q = A_HEADS * A_HEAD_DIM
A_KV = A_KV_HEADS * A_HEAD_DIM
B_W = B_HEADS * B_HEAD_DIM
EVEN_IN = A_Q + 2 * A_KV + 4 * B_W + 2 * B_HEADS
EVEN_IN_PAD = 5632
COL_KV = A_Q
COL_QKVB = A_Q + 2 * A_KV
COL_Z = COL_QKVB + 3 * B_W
COL_GATE = COL_Z + B_W
N_SHARD = 4

ADAM_LR, ADAM_B1, ADAM_B2, ADAM_EPS, ADAM_WD, ADAM_STEP = 0.001, 0.9, 0.999, 1e-08, 0.01, 10

VMEM_LIMIT_V7X = 56 * 1024 * 1024
MXU_COLS = 256
MESH_ID = pl.DeviceIdType.MESH


def _params(sem=None):
    return pltpu.CompilerParams(dimension_semantics=sem, vmem_limit_bytes=VMEM_LIMIT_V7X)


def _sigmoid(x):
    return 1.0 / (1.0 + jnp.exp(-x))


def _silu(x):
    return x * _sigmoid(x)


def _dsilu(x):
    s = _sigmoid(x)
    return s * (1.0 + x * (1.0 - s))


def _gelu(x):
    return 0.5 * x * (1.0 + lax.erf(x * 0.7071067811865476))


def _dgelu(x):
    return 0.5 * (1.0 + lax.erf(x * 0.7071067811865476)) + x * jnp.exp(-0.5 * x * x) * 0.3989422804014327


def _dot(a, b, dims):
    if a.ndim == 3:
        (ca,), (cb,) = dims
        return lax.dot_general(a, b, (((ca + 1,), (cb + 1,)), ((0,), (0,))), preferred_element_type=F32)
    return lax.dot_general(a, b, (dims, ((), ())), preferred_element_type=F32)


NN = ((1,), (0,))
NT = ((1,), (1,))
TN = ((0,), (0,))


def _as3(b):
    return b if b.ndim == 3 else b[None]


def _accumulate(step, nsteps, accs, products, finish):
    if nsteps == 1:
        finish(products())
        return

    @pl.when(step == 0)
    def _():
        for acc, p in zip(accs, products()):
            acc[...] = p

    if nsteps > 2:
        @pl.when((step > 0) & (step < nsteps - 1))
        def _():
            for acc, p in zip(accs, products()):
                acc[...] += p

    @pl.when(step == nsteps - 1)
    def _():
        finish(tuple(acc[...] + p for acc, p in zip(accs, products())))


def mm_nn(a, b, *, tm, tn, tk, out_dtype, name, res=None):
    b3 = _as3(b)
    m, k = a.shape
    s, k2, ns = b3.shape
    assert k2 == k and m % tm == 0 and ns % tn == 0 and k % tk == 0, (a.shape, b3.shape, tm, tn, tk)
    nps, nk = ns // tn, k // tk

    def body(*refs):
        if res is None:
            a_ref, b_ref, o_ref, acc = refs
        else:
            a_ref, b_ref, r_ref, o_ref, acc = refs
        def finish(tiles):
            r = tiles[0] if res is None else tiles[0] + r_ref[...].astype(F32)
            o_ref[...] = r.astype(out_dtype)

        _accumulate(pl.program_id(2), nk, (acc,),
                    lambda: (_dot(a_ref[...].astype(BF16), b_ref[...].astype(BF16), NN),), finish)

    in_specs = [pl.BlockSpec((tm, tk), lambda i, j, kk: (i, kk)),
                pl.BlockSpec((None, tk, tn), lambda i, j, kk: (j // nps, kk, j % nps))]
    args = [a, b3]
    if res is not None:
        in_specs.append(pl.BlockSpec((tm, tn), lambda i, j, kk: (i, j)))
        args.append(res)
    return pl.pallas_call(
        body, name=name, grid=(m // tm, s * nps, nk), in_specs=in_specs,
        out_specs=pl.BlockSpec((tm, tn), lambda i, j, kk: (i, j)),
        out_shape=jax.ShapeDtypeStruct((m, s * ns), out_dtype),
        scratch_shapes=[pltpu.VMEM((tm, tn), F32)],
        compiler_params=_params(("parallel", "parallel", "arbitrary")))(*args)


def mm_nt(a, b, *, tm, tn, tk, out_dtype, name, res=None):
    b3 = _as3(b)
    m, n = a.shape
    s, k, ns = b3.shape
    assert n == s * ns and m % tm == 0 and k % tn == 0 and ns % tk == 0, (a.shape, b3.shape, tm, tn, tk)
    rps = ns // tk
    nr = s * rps

    def body(*refs):
        if res is None:
            a_ref, b_ref, o_ref, acc = refs
        else:
            a_ref, b_ref, r_ref, o_ref, acc = refs
        def finish(tiles):
            r = tiles[0] if res is None else tiles[0] + r_ref[...].astype(F32)
            o_ref[...] = r.astype(out_dtype)

        _accumulate(pl.program_id(2), nr, (acc,),
                    lambda: (_dot(a_ref[...].astype(BF16), b_ref[...].astype(BF16), NT),), finish)

    in_specs = [pl.BlockSpec((tm, tk), lambda i, j, r: (i, r)),
                pl.BlockSpec((None, tn, tk), lambda i, j, r: (r // rps, j, r % rps))]
    args = [a, b3]
    if res is not None:
        in_specs.append(pl.BlockSpec((tm, tn), lambda i, j, r: (i, j)))
        args.append(res)
    return pl.pallas_call(
        body, name=name, grid=(m // tm, k // tn, nr), in_specs=in_specs,
        out_specs=pl.BlockSpec((tm, tn), lambda i, j, r: (i, j)),
        out_shape=jax.ShapeDtypeStruct((m, k), out_dtype),
        scratch_shapes=[pltpu.VMEM((tm, tn), F32)],
        compiler_params=_params(("parallel", "parallel", "arbitrary")))(*args)


def mm_tn(a, b, *, shards, tm, tn, tk, out_dtype, name):
    m, k = a.shape
    m2, n = b.shape
    ns = n // shards
    assert m2 == m and n == shards * ns and m % tm == 0 and k % tk == 0 and ns % tn == 0, (a.shape, b.shape)
    nps, nm = ns // tn, m // tm

    def body(a_ref, b_ref, o_ref, acc):
        def finish(tiles):
            o_ref[...] = tiles[0].astype(out_dtype)

        _accumulate(pl.program_id(2), nm, (acc,),
                    lambda: (_dot(a_ref[...].astype(BF16), b_ref[...].astype(BF16), TN),), finish)

    return pl.pallas_call(
        body, name=name, grid=(k // tk, shards * nps, nm),
        in_specs=[pl.BlockSpec((tm, tk), lambda i, j, mi: (mi, i)),
                  pl.BlockSpec((tm, tn), lambda i, j, mi: (mi, j))],
        out_specs=pl.BlockSpec((None, tk, tn), lambda i, j, mi: (j // nps, i, j % nps)),
        out_shape=jax.ShapeDtypeStruct((shards, k, ns), out_dtype),
        scratch_shapes=[pltpu.VMEM((tk, tn), F32)],
        compiler_params=_params(("parallel", "parallel", "arbitrary")))(a, b)


def mm_gate_up(hn, wg, wu, *, tm, tn, tk, name):
    wg3, wu3 = _as3(wg), _as3(wu)
    m, k = hn.shape
    s, _, ns = wg3.shape
    assert m % tm == 0 and ns % tn == 0 and k % tk == 0
    nps, nk = ns // tn, k // tk

    def body(a_ref, g_ref, u_ref, og_ref, ou_ref, oa_ref, accg, accu):
        def products():
            a = a_ref[...].astype(BF16)
            return _dot(a, g_ref[...].astype(BF16), NN), _dot(a, u_ref[...].astype(BF16), NN)

        def finish(tiles):
            g, u = tiles
            og_ref[...] = g.astype(BF16)
            ou_ref[...] = u.astype(BF16)
            oa_ref[...] = (_silu(g) * u).astype(BF16)

        _accumulate(pl.program_id(2), nk, (accg, accu), products, finish)

    wspec = pl.BlockSpec((None, tk, tn), lambda i, j, kk: (j // nps, kk, j % nps))
    ospec = pl.BlockSpec((tm, tn), lambda i, j, kk: (i, j))
    osh = jax.ShapeDtypeStruct((m, s * ns), BF16)
    return pl.pallas_call(
        body, name=name, grid=(m // tm, s * nps, nk),
        in_specs=[pl.BlockSpec((tm, tk), lambda i, j, kk: (i, kk)), wspec, wspec],
        out_specs=[ospec, ospec, ospec], out_shape=[osh, osh, osh],
        scratch_shapes=[pltpu.VMEM((tm, tn) if nk > 1 else (8, 128), F32)] * 2,
        compiler_params=_params(("parallel", "parallel", "arbitrary")))(hn, wg3, wu3)


def mm_down_bwd(dh, wd, gate, up, *, tm, tn, tk, name):
    m, d = dh.shape
    f, d2 = wd.shape
    assert d2 == d and m % tm == 0 and f % tn == 0 and tk == d and tn % MXU_COLS == 0

    def body(a_ref, b_ref, g_ref, u_ref, og_ref, ou_ref):
        a = a_ref[...].astype(BF16)
        for jj in range(tn // MXU_COLS):
            sl = slice(jj * MXU_COLS, (jj + 1) * MXU_COLS)
            da = _dot(a, b_ref[sl, :].astype(BF16), NT)
            g, u = g_ref[:, sl].astype(F32), u_ref[:, sl].astype(F32)
            s = _sigmoid(g)
            og_ref[:, sl] = (da * u * (s * (1.0 + g * (1.0 - s)))).astype(BF16)
            ou_ref[:, sl] = (da * (g * s)).astype(BF16)

    ospec = pl.BlockSpec((tm, tn), lambda i, j: (i, j))
    osh = jax.ShapeDtypeStruct((m, f), BF16)
    return pl.pallas_call(
        body, name=name, grid=(m // tm, f // tn),
        in_specs=[pl.BlockSpec((tm, tk), lambda i, j: (i, 0)),
                  pl.BlockSpec((tn, tk), lambda i, j: (j, 0)), ospec, ospec],
        out_specs=[ospec, ospec], out_shape=[osh, osh],
        compiler_params=_params(("parallel", "parallel")))(dh, wd, gate, up)


ROWS = 256


def rms_fwd(x, g, *, name):
    t, d = x.shape

    def body(x_ref, g_ref, o_ref):
        xv = x_ref[...]
        r = lax.rsqrt(jnp.mean(xv * xv, axis=-1, keepdims=True) + EPS)
        o_ref[...] = (xv * r * g_ref[...]).astype(BF16)

    return pl.pallas_call(
        body, name=name, grid=(t // ROWS,),
        in_specs=[pl.BlockSpec((ROWS, d), lambda i: (i, 0)), pl.BlockSpec((1, d), lambda i: (0, 0))],
        out_specs=pl.BlockSpec((ROWS, d), lambda i: (i, 0)),
        out_shape=jax.ShapeDtypeStruct((t, d), BF16), compiler_params=_params(("parallel",)))(x, g)


def dgrad_rms_bwd(a, b, form, x, g, dres, *, tm, tk, name, res=None):
    m, d = x.shape
    b3 = _as3(b)
    if form == NN:
        steps = a.shape[1] // tk
        a_spec = pl.BlockSpec((tm, tk), lambda i, r: (i, r))
        b_spec = pl.BlockSpec((None, tk, d), lambda i, r: (0, r, 0))
    else:
        s, d2, ns = b3.shape
        assert d2 == d and ns % tk == 0
        rps = ns // tk
        steps = s * rps
        a_spec = pl.BlockSpec((tm, tk), lambda i, r: (i, r))
        b_spec = pl.BlockSpec((None, d, tk), lambda i, r: (r // rps, 0, r % rps))
    assert m % tm == 0 and a.shape[1] == steps * tk

    def body(*refs):
        if res is None:
            a_ref, b_ref, x_ref, g_ref, dr_ref, dx_ref, dg_ref, acc = refs
        else:
            a_ref, b_ref, r_ref, x_ref, g_ref, dr_ref, dx_ref, dg_ref, acc = refs

        @pl.when((pl.program_id(0) == 0) & (pl.program_id(1) == 0))
        def _():
            dg_ref[...] = jnp.zeros_like(dg_ref)

        def finish(tiles):
            dyv = tiles[0] if res is None else tiles[0] + r_ref[...]
            xv = x_ref[...]
            r = lax.rsqrt(jnp.mean(xv * xv, axis=-1, keepdims=True) + EPS)
            dyg = dyv * g_ref[...]
            dx_ref[...] = r * dyg - xv * (r * r * r) * jnp.mean(dyg * xv, axis=-1, keepdims=True) + dr_ref[...]
            dg_ref[...] += jnp.sum(dyv * xv * r, axis=0, keepdims=True)

        _accumulate(pl.program_id(1), steps, (acc,),
                    lambda: (_dot(a_ref[...].astype(BF16), b_ref[...].astype(BF16), form),), finish)

    row = pl.BlockSpec((tm, d), lambda i, r: (i, 0))
    vec = pl.BlockSpec((1, d), lambda i, r: (0, 0))
    in_specs = [a_spec, b_spec] + ([row] if res is not None else []) + [row, vec, row]
    args = [a, b3] + ([res] if res is not None else []) + [x, g, dres]
    return pl.pallas_call(
        body, name=name, grid=(m // tm, steps), in_specs=in_specs, out_specs=[row, vec],
        out_shape=[jax.ShapeDtypeStruct((m, d), F32), jax.ShapeDtypeStruct((1, d), F32)],
        scratch_shapes=[pltpu.VMEM((tm, d), F32)],
        compiler_params=_params(("arbitrary", "arbitrary")))(*args)


def loss_head(h, g, target, *, name):
    t, d = h.shape

    def body(x_ref, g_ref, t_ref, loss_ref, dx_ref, dg_ref):
        @pl.when(pl.program_id(0) == 0)
        def _():
            dg_ref[...] = jnp.zeros_like(dg_ref)
            loss_ref[...] = jnp.zeros_like(loss_ref)

        xv, gv = x_ref[...], g_ref[...]
        r = lax.rsqrt(jnp.mean(xv * xv, axis=-1, keepdims=True) + EPS)
        e = xv * r * gv - t_ref[...]
        loss_ref[...] += 0.5 * jnp.sum(jnp.mean(e * e, axis=-1, keepdims=True), axis=0, keepdims=True)
        dyv = e * (1.0 / d)
        dyg = dyv * gv
        dx_ref[...] = r * dyg - xv * (r * r * r) * jnp.mean(dyg * xv, axis=-1, keepdims=True)
        dg_ref[...] += jnp.sum(dyv * xv * r, axis=0, keepdims=True)

    row = pl.BlockSpec((ROWS, d), lambda i: (i, 0))
    vec = pl.BlockSpec((1, d), lambda i: (0, 0))
    return pl.pallas_call(
        body, name=name, grid=(t // ROWS,), in_specs=[row, vec, row],
        out_specs=[pl.BlockSpec((1, 128), lambda i: (0, 0)), row, vec],
        out_shape=[jax.ShapeDtypeStruct((1, 128), F32), jax.ShapeDtypeStruct((t, d), F32),
                   jax.ShapeDtypeStruct((1, d), F32)],
        compiler_params=_params(("arbitrary",)))(h, g, target)


def _tril_mask():
    r = lax.broadcasted_iota(jnp.int32, (C_CHUNK, C_CHUNK), 0)
    c = lax.broadcasted_iota(jnp.int32, (C_CHUNK, C_CHUNK), 1)
    return r >= c


def _layer_norm_parts(v):
    mu = jnp.mean(v, axis=-1, keepdims=True)
    vc = v - mu
    rstd = lax.rsqrt(jnp.mean(vc * vc, axis=-1, keepdims=True) + EPS)
    return vc * rstd, rstd


def gmlp_fwd(zpre, ln_g, ln_b, ws, bs_t, *, name):
    t = zpre.shape[0]
    d = D_MODEL

    def body(zu_ref, zv_ref, g_ref, b_ref, ws_ref, bs_ref, o_ref):
        u = _gelu(zu_ref[...])
        vhat, _ = _layer_norm_parts(_gelu(zv_ref[...]))
        vln = (vhat * g_ref[...] + b_ref[...]).astype(BF16)
        mask = _tril_mask()
        for gi in range(C_GROUPS):
            sl = slice(gi * C_GROUP_DIM, (gi + 1) * C_GROUP_DIM)
            w = jnp.where(mask, ws_ref[gi], 0.0).astype(BF16)
            mixed = _dot(w, vln[:, sl], NN) + bs_ref[:, gi:gi + 1]
            o_ref[:, sl] = (u[:, sl] * mixed).astype(BF16)

    vec = pl.BlockSpec((1, d), lambda i: (0, 0))
    return pl.pallas_call(
        body, name=name, grid=(t // C_CHUNK,),
        in_specs=[pl.BlockSpec((C_CHUNK, d), lambda i: (i, 0)), pl.BlockSpec((C_CHUNK, d), lambda i: (i, 1)),
                  vec, vec, pl.BlockSpec((C_GROUPS, C_CHUNK, C_CHUNK), lambda i: (0, 0, 0)),
                  pl.BlockSpec((C_CHUNK, 128), lambda i: (0, 0))],
        out_specs=pl.BlockSpec((C_CHUNK, d), lambda i: (i, 0)),
        out_shape=jax.ShapeDtypeStruct((t, d), BF16), compiler_params=_params(("parallel",)))(
            zpre, zpre, ln_g, ln_b, ws, bs_t)


def gmlp_bwd(zpre, dgated, ln_g, ln_b, ws, bs_t, *, name):
    t = zpre.shape[0]
    d = D_MODEL

    def body(zu_ref, zv_ref, dg_ref, g_ref, b_ref, ws_ref, bs_ref, dz_ref, dws_ref, dbs_ref, dlg_ref, dlb_ref):
        @pl.when(pl.program_id(0) == 0)
        def _():
            dws_ref[...] = jnp.zeros_like(dws_ref)
            dbs_ref[...] = jnp.zeros_like(dbs_ref)
            dlg_ref[...] = jnp.zeros_like(dlg_ref)
            dlb_ref[...] = jnp.zeros_like(dlb_ref)

        zu, zv = zu_ref[...], zv_ref[...]
        u = _gelu(zu)
        vhat, rstd = _layer_norm_parts(_gelu(zv))
        gam = g_ref[...]
        vln = (vhat * gam + b_ref[...]).astype(BF16)
        dgt = dg_ref[...].astype(F32)
        mask = _tril_mask()
        lane = lax.broadcasted_iota(jnp.int32, (C_CHUNK, 128), 1)
        dbs = jnp.zeros((C_CHUNK, 128), F32)
        du_parts, dvln_parts = [], []
        for gi in range(C_GROUPS):
            sl = slice(gi * C_GROUP_DIM, (gi + 1) * C_GROUP_DIM)
            w = jnp.where(mask, ws_ref[gi], 0.0).astype(BF16)
            mixed = _dot(w, vln[:, sl], NN) + bs_ref[:, gi:gi + 1]
            du_parts.append(dgt[:, sl] * mixed)
            dmixed = dgt[:, sl] * u[:, sl]
            dmb = dmixed.astype(BF16)
            dws_ref[gi] += jnp.where(mask, _dot(dmb, vln[:, sl], NT), 0.0)
            dbs = dbs + jnp.where(lane == gi, jnp.sum(dmixed, axis=-1, keepdims=True), 0.0)
            dvln_parts.append(_dot(w, dmb, TN))
        dbs_ref[...] += dbs
        du = jnp.concatenate(du_parts, axis=-1)
        dvln = jnp.concatenate(dvln_parts, axis=-1)
        dlg_ref[...] += jnp.sum(dvln * vhat, axis=0, keepdims=True)
        dlb_ref[...] += jnp.sum(dvln, axis=0, keepdims=True)
        dvhat = dvln * gam
        dv = rstd * (dvhat - jnp.mean(dvhat, axis=-1, keepdims=True)
                     - vhat * jnp.mean(dvhat * vhat, axis=-1, keepdims=True))
        dz_ref[:, :d] = (du * _dgelu(zu)).astype(BF16)
        dz_ref[:, d:] = (dv * _dgelu(zv)).astype(BF16)

    vec = pl.BlockSpec((1, d), lambda i: (0, 0))
    wsp = pl.BlockSpec((C_GROUPS, C_CHUNK, C_CHUNK), lambda i: (0, 0, 0))
    bsp = pl.BlockSpec((C_CHUNK, 128), lambda i: (0, 0))
    return pl.pallas_call(
        body, name=name, grid=(t // C_CHUNK,),
        in_specs=[pl.BlockSpec((C_CHUNK, d), lambda i: (i, 0)), pl.BlockSpec((C_CHUNK, d), lambda i: (i, 1)),
                  pl.BlockSpec((C_CHUNK, d), lambda i: (i, 0)), vec, vec, wsp, bsp],
        out_specs=[pl.BlockSpec((C_CHUNK, 2 * d), lambda i: (i, 0)), wsp, bsp, vec, vec],
        out_shape=[jax.ShapeDtypeStruct((t, 2 * d), BF16), jax.ShapeDtypeStruct((C_GROUPS, C_CHUNK, C_CHUNK), F32),
                   jax.ShapeDtypeStruct((C_CHUNK, 128), F32), jax.ShapeDtypeStruct((1, d), F32),
                   jax.ShapeDtypeStruct((1, d), F32)],
        compiler_params=_params(("arbitrary",)))(zpre, zpre, dgated, ln_g, ln_b, ws, bs_t)


ATT_SCALE = A_HEAD_DIM ** -0.5
PAIRS = A_HEADS // 2
PAIRS_PER_KV = PAIRS // A_KV_HEADS


def _att_padded(tile):
    lo = lax.broadcasted_iota(jnp.int32, tile.shape, 1) < A_HEAD_DIM
    rolled = pltpu.roll(tile, A_HEAD_DIM, 1)
    zero = jnp.zeros_like(tile)
    return {(0, 0): jnp.where(lo, tile, zero).astype(BF16), (0, 1): jnp.where(lo, zero, rolled).astype(BF16),
            (1, 0): jnp.where(lo, rolled, zero).astype(BF16), (1, 1): jnp.where(lo, zero, tile).astype(BF16)}


def _att_valid(n):
    r = lax.broadcasted_iota(jnp.int32, (WINDOW, 2 * WINDOW), 0)
    c = lax.broadcasted_iota(jnp.int32, (WINDOW, 2 * WINDOW), 1)
    rel = r + WINDOW - c
    return (rel >= 0) & (rel < WINDOW) & ((c >= WINDOW) | (n > 0))


def _att_probs(qp, kpad, sink, valid):
    s = jnp.where(valid, _dot(qp, kpad, NT), NEG_INF)
    m = jnp.maximum(jnp.max(s, axis=-1, keepdims=True), sink)
    p = jnp.exp(s - m)
    e_sink = jnp.exp(sink - m)
    inv = 1.0 / (jnp.sum(p, axis=-1, keepdims=True) + e_sink)
    return p * inv, e_sink * inv


ATT_BLOCKS = 4


def _att_operands(q_ref, kvc_ref, kvp_ref, s_ref, step, nb):
    w = WINDOW
    kvs = [kvp_ref[...]] + [kvc_ref[b * w:(b + 1) * w, :] for b in range(nb)]
    key = lambda h: ((h // 2) // PAIRS_PER_KV, h % 2)
    qs, ks, vs, valids = [], [], [], []
    for b in range(nb):
        kv = jnp.concatenate([kvs[b], kvs[b + 1]], axis=0)
        kpad, vpad = _att_padded(kv[:, :128]), _att_padded(kv[:, 128:])
        pairs = [(q_ref[b * w:(b + 1) * w, j * 128:(j + 1) * 128] * ATT_SCALE).astype(BF16) for j in range(PAIRS)]
        qs += [pairs[h // 2] for h in range(A_HEADS)]
        ks += [kpad[key(h)] for h in range(A_HEADS)]
        vs += [vpad[key(h)] for h in range(A_HEADS)]
        valids += [_att_valid(step * nb + b)] * A_HEADS
    sink = jnp.stack([s_ref[:, h:h + 1] for h in range(A_HEADS)] * nb)
    return jnp.stack(qs), jnp.stack(ks), jnp.stack(vs), sink, jnp.stack(valids)


def _att_specs(nb):
    rows = nb * WINDOW
    return [pl.BlockSpec((rows, A_Q), lambda n: (n, 0)),
            pl.BlockSpec((rows, 2 * A_KV), lambda n: (n, COL_KV // (2 * A_KV))),
            pl.BlockSpec((WINDOW, 2 * A_KV), lambda n: (jnp.maximum(nb * n - 1, 0), COL_KV // (2 * A_KV))),
            pl.BlockSpec((1, 128), lambda n: (0, 0))]


def att_fwd(proj, sinks, *, name):
    t = proj.shape[0]
    nb = min(ATT_BLOCKS, t // WINDOW)
    rows = nb * WINDOW

    def body(q_ref, kvc_ref, kvp_ref, s_ref, o_ref):
        q, k, v, sink, valid = _att_operands(q_ref, kvc_ref, kvp_ref, s_ref, pl.program_id(0), nb)
        w, _ = _att_probs(q, k, sink, valid)
        o = _dot(w.astype(BF16), v, NN)
        for b in range(nb):
            for j in range(PAIRS):
                pair = o[b * A_HEADS + 2 * j] + o[b * A_HEADS + 2 * j + 1]
                o_ref[b * WINDOW:(b + 1) * WINDOW, j * 128:(j + 1) * 128] = pair.astype(BF16)

    return pl.pallas_call(
        body, name=name, grid=(t // rows,), in_specs=_att_specs(nb),
        out_specs=pl.BlockSpec((rows, A_Q), lambda n: (n, 0)),
        out_shape=jax.ShapeDtypeStruct((t, A_Q), BF16), compiler_params=_params(("parallel",)))(
            proj, proj, proj, sinks)


def att_bwd(proj, sinks, dout, *, name):
    t = proj.shape[0]
    nb = min(ATT_BLOCKS, t // WINDOW)
    rows = nb * WINDOW

    def body(q_ref, kvc_ref, kvp_ref, s_ref, do_ref, dq_ref, dkc_ref, dkp_ref, ds_ref):
        @pl.when(pl.program_id(0) == 0)
        def _():
            ds_ref[...] = jnp.zeros_like(ds_ref)

        q, k, v, sink, valid = _att_operands(q_ref, kvc_ref, kvp_ref, s_ref, pl.program_id(0), nb)
        dop = jnp.stack([do_ref[b * WINDOW:(b + 1) * WINDOW, (h // 2) * 128:(h // 2 + 1) * 128]
                         for b in range(nb) for h in range(A_HEADS)]).astype(BF16)
        w, w_sink = _att_probs(q, k, sink, valid)
        dw = _dot(dop, v, NT)
        delta = jnp.sum(w * dw, axis=-1, keepdims=True)
        dsc = (w * (dw - delta)).astype(BF16)
        dsink_h = -jnp.sum(w_sink * delta, axis=1, keepdims=True)
        dq = _dot(dsc, k, NN)
        dk_h = _dot(dsc, q, TN)
        dv_h = _dot(w.astype(BF16), dop, TN)
        lane = lax.broadcasted_iota(jnp.int32, (1, 128), 1)
        dsink = jnp.zeros((1, 128), F32)
        for b in range(nb):
            for h in range(A_HEADS):
                dsink = dsink + jnp.where(lane == h, dsink_h[b * A_HEADS + h], 0.0)
        ds_ref[...] += dsink
        lo = lax.broadcasted_iota(jnp.int32, (2 * WINDOW, 128), 1) < A_HEAD_DIM
        heads_per_kv = A_HEADS // A_KV_HEADS

        def tile(per_head, b):
            acc = {}
            for kvh in range(A_KV_HEADS):
                for half in range(2):
                    hs = range(kvh * heads_per_kv + half, (kvh + 1) * heads_per_kv, 2)
                    acc[(kvh, half)] = functools.reduce(lambda a, c: a + c, [per_head[b * A_HEADS + h] for h in hs])
            return jnp.where(lo, acc[(0, 0)] + pltpu.roll(acc[(0, 1)], A_HEAD_DIM, 1),
                             pltpu.roll(acc[(1, 0)], A_HEAD_DIM, 1) + acc[(1, 1)])

        for b in range(nb):
            blk = slice(b * WINDOW, (b + 1) * WINDOW)
            for j in range(PAIRS):
                pair = dq[b * A_HEADS + 2 * j] + dq[b * A_HEADS + 2 * j + 1]
                dq_ref[blk, j * 128:(j + 1) * 128] = (pair * ATT_SCALE).astype(BF16)
            dkv = jnp.concatenate([tile(dk_h, b), tile(dv_h, b)], axis=1)
            dkp_ref[blk, :] = dkv[:WINDOW]
            dkc_ref[blk, :] = dkv[WINDOW:]

    kvo = pl.BlockSpec((rows, 2 * A_KV), lambda n: (n, 0))
    return pl.pallas_call(
        body, name=name, grid=(t // rows,),
        in_specs=_att_specs(nb) + [pl.BlockSpec((rows, A_Q), lambda n: (n, 0))],
        out_specs=[pl.BlockSpec((rows, A_Q), lambda n: (n, 0)), kvo, kvo, pl.BlockSpec((1, 128), lambda n: (0, 0))],
        out_shape=[jax.ShapeDtypeStruct((t, A_Q), BF16), jax.ShapeDtypeStruct((t, 2 * A_KV), F32),
                   jax.ShapeDtypeStruct((t, 2 * A_KV), F32), jax.ShapeDtypeStruct((1, 128), F32)],
        compiler_params=_params(("arbitrary",)))(proj, proj, proj, sinks, dout)


QK_SCALE = B_HEAD_DIM ** -0.5
PREP_COLS = 256
PREP_NCB = 3 * B_W // PREP_COLS
HALO = 8
PREP_ROWS = 512


def _roll_rows(x, shift):
    n = x.shape[0]
    return x if shift % n == 0 else pltpu.roll(x, shift % n, 0)


def _conv_taps(xe, w):
    xs = [_roll_rows(xe, CONV_K - 1 - i) for i in range(CONV_K)]
    c = w[0:1] * xs[0]
    for i in range(1, CONV_K):
        c = c + w[i:i + 1] * xs[i]
    return xs, c


def dprep_fwd(proj, conv_w, *, name):
    t = proj.shape[0]
    tt = min(PREP_ROWS, t)
    col0 = COL_QKVB // PREP_COLS

    def body(x_ref, h_ref, w_ref, o_ref):
        cb, n = pl.program_id(0), pl.program_id(1)
        halo = jnp.where(n > 0, h_ref[...], 0.0)
        xe = jnp.concatenate([halo, x_ref[...]], axis=0)
        _, c = _conv_taps(xe, w_ref[...])
        y = _silu(c)[HALO:]
        parts = []
        for hh in range(PREP_COLS // B_HEAD_DIM):
            yh = y[:, hh * B_HEAD_DIM:(hh + 1) * B_HEAD_DIM]
            parts.append(yh * lax.rsqrt(jnp.sum(yh * yh, axis=-1, keepdims=True) + EPS))
        nrm = jnp.concatenate(parts, axis=-1)
        o_ref[...] = jnp.where(cb < 4, nrm * QK_SCALE, jnp.where(cb < 8, nrm, y))

    return pl.pallas_call(
        body, name=name, grid=(PREP_NCB, t // tt),
        in_specs=[pl.BlockSpec((tt, PREP_COLS), lambda cb, n: (n, col0 + cb)),
                  pl.BlockSpec((HALO, PREP_COLS), lambda cb, n: (jnp.maximum(n * (tt // HALO) - 1, 0), col0 + cb)),
                  pl.BlockSpec((CONV_K, PREP_COLS), lambda cb, n: (0, cb))],
        out_specs=pl.BlockSpec((tt, PREP_COLS), lambda cb, n: (n, cb)),
        out_shape=jax.ShapeDtypeStruct((t, 3 * B_W), F32), compiler_params=_params(("parallel", "parallel")))(
            proj, proj, conv_w)


def dprep_bwd(proj, conv_w, dqkvn, *, name):
    t = proj.shape[0]
    tt = min(PREP_ROWS, t)
    nb = t // tt
    col0 = COL_QKVB // PREP_COLS
    n8 = t // HALO

    def body(xc_ref, xb_ref, xa_ref, dc_ref, da_ref, w_ref, dx_ref, dw_ref):
        cb, n = pl.program_id(0), pl.program_id(1)

        @pl.when(n == 0)
        def _():
            dw_ref[...] = jnp.zeros_like(dw_ref)

        w = w_ref[...]
        xe = jnp.concatenate([jnp.where(n > 0, xb_ref[...], 0.0), xc_ref[...], xa_ref[...]], axis=0)
        xs, c = _conv_taps(xe, w)
        sg = _sigmoid(c)
        y = c * sg
        dout = jnp.concatenate([jnp.zeros((HALO, PREP_COLS), F32), dc_ref[...],
                                jnp.where(n < nb - 1, da_ref[...], 0.0)], axis=0)
        dsc = jnp.where(cb < 4, QK_SCALE, 1.0)
        parts = []
        for hh in range(PREP_COLS // B_HEAD_DIM):
            sl = slice(hh * B_HEAD_DIM, (hh + 1) * B_HEAD_DIM)
            yh, doh = y[:, sl], dout[:, sl] * dsc
            r = lax.rsqrt(jnp.sum(yh * yh, axis=-1, keepdims=True) + EPS)
            parts.append(doh * r - yh * (r * r * r) * jnp.sum(doh * yh, axis=-1, keepdims=True))
        dy = jnp.where(cb < 8, jnp.concatenate(parts, axis=-1), dout)
        dcv = dy * sg * (1.0 + c * (1.0 - sg))
        dxe = w[CONV_K - 1:CONV_K] * dcv
        for i in range(CONV_K - 1):
            dxe = dxe + w[i:i + 1] * _roll_rows(dcv, -(CONV_K - 1 - i))
        dx_ref[...] = dxe[HALO:HALO + tt].astype(BF16)
        for i in range(CONV_K):
            dw_ref[i:i + 1, :] += jnp.sum((dcv * xs[i])[HALO:HALO + tt], axis=0, keepdims=True)

    def after(n):
        return jnp.minimum((n + 1) * (tt // HALO), n8 - 1)

    return pl.pallas_call(
        body, name=name, grid=(PREP_NCB, nb),
        in_specs=[pl.BlockSpec((tt, PREP_COLS), lambda cb, n: (n, col0 + cb)),
                  pl.BlockSpec((HALO, PREP_COLS), lambda cb, n: (jnp.maximum(n * (tt // HALO) - 1, 0), col0 + cb)),
                  pl.BlockSpec((HALO, PREP_COLS), lambda cb, n: (after(n), col0 + cb)),
                  pl.BlockSpec((tt, PREP_COLS), lambda cb, n: (n, cb)),
                  pl.BlockSpec((HALO, PREP_COLS), lambda cb, n: (after(n), cb)),
                  pl.BlockSpec((CONV_K, PREP_COLS), lambda cb, n: (0, cb))],
        out_specs=[pl.BlockSpec((tt, PREP_COLS), lambda cb, n: (n, cb)),
                   pl.BlockSpec((CONV_K, PREP_COLS), lambda cb, n: (0, cb))],
        out_shape=[jax.ShapeDtypeStruct((t, 3 * B_W), BF16), jax.ShapeDtypeStruct((CONV_K, 3 * B_W), F32)],
        compiler_params=_params(("parallel", "arbitrary")))(proj, proj, proj, dqkvn, dqkvn, conv_w)


def _softplus(z):
    return jnp.maximum(z, 0.0) + jnp.log(1.0 + jnp.exp(-jnp.abs(z)))


def gates_fwd(proj, alog_pad, dtb_pad, *, name):
    t = proj.shape[0]

    def body(x_ref, a_ref, b_ref, o_ref):
        raw = x_ref[...]
        lane = lax.broadcasted_iota(jnp.int32, raw.shape, 1)
        g = -jnp.exp(a_ref[...]) * _softplus(raw + b_ref[...])
        o_ref[...] = jnp.where(lane < B_HEADS, _sigmoid(raw), jnp.where(lane < 2 * B_HEADS, g, 0.0))

    vec = pl.BlockSpec((1, 128), lambda n: (0, 0))
    return pl.pallas_call(
        body, name=name, grid=(t // ROWS,),
        in_specs=[pl.BlockSpec((ROWS, 128), lambda n: (n, COL_GATE // 128)), vec, vec],
        out_specs=pl.BlockSpec((ROWS, 128), lambda n: (n, 0)),
        out_shape=jax.ShapeDtypeStruct((t, 128), F32), compiler_params=_params(("parallel",)))(
            proj, alog_pad, dtb_pad)


def gates_bwd(proj, alog_pad, dtb_pad, dgates, *, name):
    t = proj.shape[0]

    def body(x_ref, a_ref, b_ref, dg_ref, dx_ref, da_ref, db_ref):
        @pl.when(pl.program_id(0) == 0)
        def _():
            da_ref[...] = jnp.zeros_like(da_ref)
            db_ref[...] = jnp.zeros_like(db_ref)

        raw, dgt = x_ref[...], dg_ref[...]
        lane = lax.broadcasted_iota(jnp.int32, raw.shape, 1)
        is_beta, is_g = lane < B_HEADS, (lane >= B_HEADS) & (lane < 2 * B_HEADS)
        beta = _sigmoid(raw)
        z = raw + b_ref[...]
        neg_a = -jnp.exp(a_ref[...])
        d_z = jnp.where(is_g, dgt * neg_a * _sigmoid(z), 0.0)
        dx_ref[...] = jnp.where(is_beta, dgt * beta * (1.0 - beta), d_z).astype(BF16)
        db_ref[...] += jnp.sum(d_z, axis=0, keepdims=True)
        da_ref[...] += jnp.sum(jnp.where(is_g, dgt * neg_a * _softplus(z), 0.0), axis=0, keepdims=True)

    vec = pl.BlockSpec((1, 128), lambda n: (0, 0))
    row = pl.BlockSpec((ROWS, 128), lambda n: (n, 0))
    return pl.pallas_call(
        body, name=name, grid=(t // ROWS,),
        in_specs=[pl.BlockSpec((ROWS, 128), lambda n: (n, COL_GATE // 128)), vec, vec, row],
        out_specs=[row, vec, vec],
        out_shape=[jax.ShapeDtypeStruct((t, 128), BF16), jax.ShapeDtypeStruct((1, 128), F32),
                   jax.ShapeDtypeStruct((1, 128), F32)],
        compiler_params=_params(("arbitrary",)))(proj, alog_pad, dtb_pad, dgates)


def _split2(a):
    hi = a.astype(BF16)
    return hi, (a - hi.astype(F32)).astype(BF16)


def _dotp(a, b, dims, passes):
    if passes == 1:
        return _dot(a.astype(BF16), b.astype(BF16), dims)
    ah, al = _split2(a)
    bh, bl = _split2(b)
    return _dot(ah, bh, dims) + (_dot(ah, bl, dims) + _dot(al, bh, dims))


_GRAD_DIMS = {NN: ((NT, False), (TN, False)), NT: ((NN, False), (TN, True)), TN: ((NT, True), (NN, False))}


def _make_mm(dims, passes, grad_passes):
    (da_dims, da_swap), (db_dims, db_swap) = _GRAD_DIMS[dims]

    @jax.custom_vjp
    def mm(a, b):
        return _dotp(a, b, dims, passes)

    def fwd(a, b):
        return _dotp(a, b, dims, passes), (a, b)

    def bwd(saved, ct):
        a, b = saved
        da = _dotp(b, ct, da_dims, grad_passes) if da_swap else _dotp(ct, b, da_dims, grad_passes)
        db = _dotp(ct, a, db_dims, grad_passes) if db_swap else _dotp(a, ct, db_dims, grad_passes)
        return da, db

    mm.defvjp(fwd, bwd)
    return mm


MM1 = {d: _make_mm(d, 1, 1) for d in (NN, NT, TN)}
MM3 = {d: _make_mm(d, 3, 1) for d in (NN, NT, TN)}


def _neumann_value(n):
    c = n.shape[-1]
    eye = (lax.broadcasted_iota(jnp.int32, (c, c), 0) == lax.broadcasted_iota(jnp.int32, (c, c), 1)).astype(F32)
    inv, pw = eye + n, n
    for _ in range(5):
        pw = _dotp(pw, pw, NN, 3)
        inv = inv + _dotp(inv, pw, NN, 3)
    return inv


@jax.custom_vjp
def _neumann_inverse(n):
    return _neumann_value(n)


def _neumann_fwd(n):
    inv = _neumann_value(n)
    return inv, inv


def _neumann_bwd(inv, ct):
    return (_dotp(_dotp(inv, ct, TN, 1), inv, NT, 1),)


_neumann_inverse.defvjp(_neumann_fwd, _neumann_bwd)


def _tri_ones(lower):
    r = lax.broadcasted_iota(jnp.int32, (DN_CHUNK, DN_CHUNK), 0)
    c = lax.broadcasted_iota(jnp.int32, (DN_CHUNK, DN_CHUNK), 1)
    return (r >= c if lower else r <= c).astype(BF16)


def _tri_sum(x, lower):
    tri = _tri_ones(lower)
    hi = x.astype(BF16)
    r1 = x - hi.astype(F32)
    mid = r1.astype(BF16)
    lo = (r1 - mid.astype(F32)).astype(BF16)
    return _dot(tri, hi, NN) + (_dot(tri, mid, NN) + _dot(tri, lo, NN))


def _delta_chunk(s0, q, k, v, beta, gam_c, gam_r):
    c = DN_CHUNK
    r = lax.broadcasted_iota(jnp.int32, (c, c), 0)
    cc = lax.broadcasted_iota(jnp.int32, (c, c), 1)
    incl, strict = r >= cc, r > cc
    decay = jnp.exp(jnp.where(incl, gam_c - gam_r, NEG_INF))
    g_last = gam_c[:, c - 1:c, :]
    e_gam, e_rest, e_last = jnp.exp(gam_c), jnp.exp(g_last - gam_c), jnp.exp(g_last)
    a_neg = -jnp.where(strict, beta * MM1[NT](k, k) * decay, 0.0)
    inv = _neumann_inverse(a_neg)
    uw = MM3[NN](inv,jnp.concatenate([v * beta, k * (beta * e_gam)], axis=-1))
    u, w = uw[..., :B_HEAD_DIM], uw[..., B_HEAD_DIM:]
    qk = MM1[NT](q, k) * decay
    v_new = u - MM1[NN](w, s0)
    o = MM1[NN](q * e_gam, s0) + MM1[NN](qk, v_new)
    s1 = s0 * e_last + MM1[TN](k * e_rest, v_new)
    return s1, o


def _delta_operands(q_ref, k_ref, v_ref, gt):
    heads = lambda ref: jnp.stack([ref[:, h * B_HEAD_DIM:(h + 1) * B_HEAD_DIM] for h in range(B_HEADS)])
    gam = _tri_sum(gt, True)
    gam_t = gam.T
    beta = jnp.stack([gt[:, h:h + 1] for h in range(B_HEADS)])
    gam_c = jnp.stack([gam[:, B_HEADS + h:B_HEADS + h + 1] for h in range(B_HEADS)])
    gam_r = jnp.stack([gam_t[B_HEADS + h:B_HEADS + h + 1, :] for h in range(B_HEADS)])
    return heads(q_ref), heads(k_ref), heads(v_ref), beta, gam_c, gam_r


def delta_fwd(qkvn, gates, *, name):
    t = qkvn.shape[0]
    nc = t // DN_CHUNK

    def body(q_ref, k_ref, v_ref, g_ref, o_ref, ss_ref, state):
        @pl.when(pl.program_id(0) == 0)
        def _():
            state[...] = jnp.zeros_like(state)

        s0 = state[...]
        ss_ref[...] = s0
        s1, o = _delta_chunk(s0, *_delta_operands(q_ref, k_ref, v_ref, g_ref[...]))
        state[...] = s1
        for h in range(B_HEADS):
            o_ref[:, h * B_HEAD_DIM:(h + 1) * B_HEAD_DIM] = o[h]

    blk = lambda j: pl.BlockSpec((DN_CHUNK, B_W), lambda n: (n, j))
    return pl.pallas_call(
        body, name=name, grid=(nc,),
        in_specs=[blk(0), blk(1), blk(2), pl.BlockSpec((DN_CHUNK, 128), lambda n: (n, 0))],
        out_specs=[blk(0), pl.BlockSpec((None, B_HEADS, B_HEAD_DIM, B_HEAD_DIM), lambda n: (n, 0, 0, 0))],
        out_shape=[jax.ShapeDtypeStruct((t, B_W), F32),
                   jax.ShapeDtypeStruct((nc, B_HEADS, B_HEAD_DIM, B_HEAD_DIM), F32)],
        scratch_shapes=[pltpu.VMEM((B_HEADS, B_HEAD_DIM, B_HEAD_DIM), F32)],
        compiler_params=_params(("arbitrary",)))(qkvn, qkvn, qkvn, gates)


def delta_bwd(qkvn, gates, ssave, do, *, name):
    t = qkvn.shape[0]
    nc = t // DN_CHUNK

    def body(q_ref, k_ref, v_ref, g_ref, ss_ref, do_ref, dx_ref, dg_ref, dstate):
        @pl.when(pl.program_id(0) == 0)
        def _():
            dstate[...] = jnp.zeros_like(dstate)

        lane = lax.broadcasted_iota(jnp.int32, (DN_CHUNK, 128), 1)
        row = lax.broadcasted_iota(jnp.int32, (128, DN_CHUNK), 0)
        dbeta_all = jnp.zeros((DN_CHUNK, 128), F32)
        dgam_c_all = jnp.zeros((DN_CHUNK, 128), F32)
        dgam_r_all = jnp.zeros((128, DN_CHUNK), F32)
        _, vjp = jax.vjp(_delta_chunk, ss_ref[...], *_delta_operands(q_ref, k_ref, v_ref, g_ref[...]))
        do = jnp.stack([do_ref[:, h * B_HEAD_DIM:(h + 1) * B_HEAD_DIM] for h in range(B_HEADS)])
        ds0, dq, dk, dv, dbeta, dgam_c, dgam_r = vjp((dstate[...], do))
        dstate[...] = ds0
        for h in range(B_HEADS):
            dx_ref[:, h * B_HEAD_DIM:(h + 1) * B_HEAD_DIM] = dq[h]
            dx_ref[:, B_W + h * B_HEAD_DIM:B_W + (h + 1) * B_HEAD_DIM] = dk[h]
            dx_ref[:, 2 * B_W + h * B_HEAD_DIM:2 * B_W + (h + 1) * B_HEAD_DIM] = dv[h]
            dbeta_all = dbeta_all + jnp.where(lane == h, dbeta[h], 0.0)
            dgam_c_all = dgam_c_all + jnp.where(lane == B_HEADS + h, dgam_c[h], 0.0)
            dgam_r_all = dgam_r_all + jnp.where(row == B_HEADS + h, dgam_r[h], 0.0)
        dg_ref[...] = dbeta_all + _tri_sum(dgam_c_all + dgam_r_all.T, False)

    blk = lambda j: pl.BlockSpec((DN_CHUNK, B_W), lambda n: (nc - 1 - n, j))
    gsp = pl.BlockSpec((DN_CHUNK, 128), lambda n: (nc - 1 - n, 0))
    return pl.pallas_call(
        body, name=name, grid=(nc,),
        in_specs=[blk(0), blk(1), blk(2), gsp,
                  pl.BlockSpec((None, B_HEADS, B_HEAD_DIM, B_HEAD_DIM), lambda n: (nc - 1 - n, 0, 0, 0)), blk(0)],
        out_specs=[pl.BlockSpec((DN_CHUNK, 3 * B_W), lambda n: (nc - 1 - n, 0)), gsp],
        out_shape=[jax.ShapeDtypeStruct((t, 3 * B_W), F32), jax.ShapeDtypeStruct((t, 128), F32)],
        scratch_shapes=[pltpu.VMEM((B_HEADS, B_HEAD_DIM, B_HEAD_DIM), F32)],
        compiler_params=_params(("arbitrary",)))(qkvn, qkvn, qkvn, gates, ssave, do)


GNORM_ROWS = 1024


def gnorm_fwd(o, proj, onorm, *, name):
    t = o.shape[0]

    def body(o_ref, z_ref, w_ref, out_ref):
        ov = o_ref[...]
        r = lax.rsqrt(jnp.mean(ov * ov, axis=-1, keepdims=True) + EPS)
        out_ref[...] = (ov * r * w_ref[...] * _silu(z_ref[...])).astype(BF16)

    rows = min(GNORM_ROWS, t)
    blk = pl.BlockSpec((rows, B_HEAD_DIM), lambda n, h: (n, h))
    return pl.pallas_call(
        body, name=name, grid=(t // rows, B_HEADS),
        in_specs=[blk, pl.BlockSpec((rows, B_HEAD_DIM), lambda n, h: (n, COL_Z // B_HEAD_DIM + h)),
                  pl.BlockSpec((1, B_HEAD_DIM), lambda n, h: (0, 0))],
        out_specs=blk, out_shape=jax.ShapeDtypeStruct((t, B_W), BF16),
        compiler_params=_params(("parallel", "parallel")))(o, proj, onorm)


def gnorm_bwd(o, proj, onorm, dout, *, dcol0, name):
    t = o.shape[0]

    def body(o_ref, z_ref, w_ref, d_ref, do_ref, dz_ref, dw_ref):
        @pl.when((pl.program_id(0) == 0) & (pl.program_id(1) == 0))
        def _():
            dw_ref[...] = jnp.zeros_like(dw_ref)

        ov, zv, wv, dv = o_ref[...], z_ref[...], w_ref[...], d_ref[...].astype(F32)
        r = lax.rsqrt(jnp.mean(ov * ov, axis=-1, keepdims=True) + EPS)
        nrm = ov * r
        dz_ref[...] = (dv * nrm * wv * _dsilu(zv)).astype(BF16)
        da = dv * _silu(zv)
        dw_ref[...] += jnp.sum(da * nrm, axis=0, keepdims=True)
        dn = da * wv
        do_ref[...] = r * dn - ov * (r * r * r) * jnp.mean(dn * ov, axis=-1, keepdims=True)

    rows = min(GNORM_ROWS, t)
    blk = pl.BlockSpec((rows, B_HEAD_DIM), lambda n, h: (n, h))
    vec = pl.BlockSpec((1, B_HEAD_DIM), lambda n, h: (0, 0))
    return pl.pallas_call(
        body, name=name, grid=(t // rows, B_HEADS),
        in_specs=[blk, pl.BlockSpec((rows, B_HEAD_DIM), lambda n, h: (n, COL_Z // B_HEAD_DIM + h)), vec,
                  pl.BlockSpec((rows, B_HEAD_DIM), lambda n, h: (n, dcol0 // B_HEAD_DIM + h))],
        out_specs=[blk, blk, vec],
        out_shape=[jax.ShapeDtypeStruct((t, B_W), F32), jax.ShapeDtypeStruct((t, B_W), BF16),
                   jax.ShapeDtypeStruct((1, B_HEAD_DIM), F32)],
        compiler_params=_params(("arbitrary", "arbitrary")))(o, proj, onorm, dout)


def _ffn_fwd(h, norm_g, wg, wu, wd, tm, tag):
    hn = rms_fwd(h, norm_g, name=f"ffn{tag}_norm")
    gate, up, act = mm_gate_up(hn, wg, wu, tm=min(512, tm), tn=1408, tk=2048, name=f"ffn{tag}_gate_up")
    h_out = mm_nn(act, wd, tm=tm, tn=2048, tk=512, out_dtype=F32, res=h, name=f"ffn{tag}_down")
    return h_out, (hn, gate, up, act)


def _ffn_bwd(dh, h, norm_g, wg, wu, wd, saved, tm, tag, emit):
    hn, gate, up, act = saved
    dwd = mm_tn(act, dh, shards=1, tm=tm, tn=1024, tk=1408, out_dtype=BF16, name=f"ffn{tag}_dwd")[0]
    dgate, dup = mm_down_bwd(dh, wd, gate, up, tm=tm, tn=512, tk=2048, name=f"ffn{tag}_dact")
    dwg = mm_tn(hn, dgate, shards=N_SHARD, tm=tm, tn=1408, tk=1024, out_dtype=BF16, name=f"ffn{tag}_dwg")
    dwu = mm_tn(hn, dup, shards=N_SHARD, tm=tm, tn=1408, tk=1024, out_dtype=BF16, name=f"ffn{tag}_dwu")
    started = emit(f"ffn{tag}", {"gate": dwg, "up": dwu, "down": dwd})
    dhn = mm_nt(dgate, wg, tm=tm, tn=1024, tk=1408, out_dtype=F32, name=f"ffn{tag}_dhn_g")
    dh_in, dnorm = dgrad_rms_bwd(dup, wu, NT, h, norm_g + started, dh, tm=min(512, tm), tk=1408, res=dhn,
                                 name=f"ffn{tag}_dhn_u_dnorm")
    return dh_in, dnorm


def _local_step(x, target, w, get, emit):
    t = x.shape[0]
    tm = min(1024, t)
    g = {}

    hn0 = rms_fwd(x, w["even_norm"], name="l0_norm")
    w.update(get("even_in", hn0))
    proj = mm_nt(hn0, w["even_w_in"], tm=tm, tn=512, tk=2048, out_dtype=F32, name="l0_w_in")
    out_a = att_fwd(proj, w["sinks"], name="l0_att")
    qkvn = dprep_fwd(proj, w["even_conv"], name="l0_prep")
    gates = gates_fwd(proj, w["a_log"], w["dt_bias"], name="l0_gates")
    o_delta, ssave = delta_fwd(qkvn, gates, name="l0_delta")
    w.update(get("even_out", o_delta))
    out_b = gnorm_fwd(o_delta, proj, w["onorm"], name="l0_gnorm")
    mix0 = jnp.concatenate([out_a, out_b], axis=-1)
    h1 = mm_nn(mix0, w["even_w_out"], tm=tm, tn=1024, tk=2048, out_dtype=F32, res=x, name="l0_w_out")
    f0 = get("ffn0", h1)
    h2, ffn0 = _ffn_fwd(h1, w["ffn_norm"][0:1] + f0["tok"], f0["gate"], f0["up"], f0["down"], tm, 0)
    hn2 = rms_fwd(h2, w["odd_norm"], name="l1_norm")
    w.update(get("odd", hn2))
    zpre = mm_nn(hn2, w["odd_w_in"], tm=tm, tn=1024, tk=2048, out_dtype=F32, name="l1_w_in")
    gated = gmlp_fwd(zpre, w["odd_ln_g"], w["odd_ln_b"], w["odd_w_s"], w["odd_b_s"], name="l1_gmlp")
    h3 = mm_nn(gated, w["odd_w_out"], tm=tm, tn=1024, tk=2048, out_dtype=F32, res=h2, name="l1_w_out")
    f1 = get("ffn1", h3)
    h4, ffn1 = _ffn_fwd(h3, w["ffn_norm"][1:2] + f1["tok"], f1["gate"], f1["up"], f1["down"], tm, 1)
    loss, dh4, g["final_norm"] = loss_head(h4, w["final_norm"], target, name="loss_head")

    dh3, dn1 = _ffn_bwd(dh4, h3, w["ffn_norm"][1:2], f1["gate"], f1["up"], f1["down"], ffn1, tm, 1, emit)
    dw_out_o = mm_tn(gated, dh3, shards=1, tm=tm, tn=1024, tk=1024, out_dtype=BF16, name="l1_dw_out")[0]
    dgated = mm_nt(dh3, w["odd_w_out"], tm=tm, tn=1024, tk=2048, out_dtype=BF16, name="l1_dgated")
    dzpre, g["odd_w_s"], g["odd_b_s"], g["odd_ln_g"], g["odd_ln_b"] = gmlp_bwd(
        zpre, dgated, w["odd_ln_g"], w["odd_ln_b"], w["odd_w_s"], w["odd_b_s"], name="l1_dgmlp")
    dw_in_o = mm_tn(hn2, dzpre, shards=N_SHARD, tm=tm, tn=1024, tk=1024, out_dtype=BF16, name="l1_dw_in")
    started = emit("odd", {"odd_w_in": dw_in_o, "odd_w_out": dw_out_o})
    dh2, g["odd_norm"] = dgrad_rms_bwd(dzpre, w["odd_w_in"], NT, h2, w["odd_norm"] + started, dh3, tm=min(512, tm),
                                       tk=1024, name="l1_dhn_dnorm")
    dh1, dn0 = _ffn_bwd(dh2, h1, w["ffn_norm"][0:1], f0["gate"], f0["up"], f0["down"], ffn0, tm, 0, emit)
    g["ffn_norm"] = jnp.concatenate([dn0, dn1], axis=0)
    dw_out_e = mm_tn(mix0, dh1, shards=1, tm=tm, tn=1024, tk=1024, out_dtype=BF16, name="l0_dw_out")[0]
    started = emit("even_out", {"even_w_out": dw_out_e})
    dmix = mm_nt(dh1, w["even_w_out"], tm=tm, tn=1024, tk=2048, out_dtype=F32, name="l0_dmix")
    dq_a, dkv_cur, dkv_prev, g["sinks"] = att_bwd(proj, w["sinks"] + started, dmix, name="l0_datt")
    dkv = dkv_cur + jnp.concatenate([dkv_prev[WINDOW:], jnp.zeros((WINDOW, 2 * A_KV), F32)], axis=0)
    do_delta, dz, g["onorm"] = gnorm_bwd(o_delta, proj, w["onorm"], dmix, dcol0=A_Q, name="l0_dgnorm")
    dqkvn, dgates = delta_bwd(qkvn, gates, ssave, do_delta, name="l0_ddelta")
    dqkv_b, g["even_conv"] = dprep_bwd(proj, w["even_conv"], dqkvn, name="l0_dprep")
    draw, g["a_log"], g["dt_bias"] = gates_bwd(proj, w["a_log"], w["dt_bias"], dgates, name="l0_dgates")
    dproj = jnp.concatenate([dq_a, dkv.astype(BF16), dqkv_b, dz, draw,
                             jnp.zeros((t, EVEN_IN_PAD - COL_GATE - 128), BF16)], axis=-1)
    dw_in_e = mm_tn(dproj, hn0, shards=1, tm=tm, tn=1024, tk=1408, out_dtype=BF16, name="l0_dw_in")[0]
    grad_x, g["even_norm"] = dgrad_rms_bwd(dproj, w["even_w_in"], NN, x, w["even_norm"], dh1, tm=min(512, tm), tk=512,
                                           name="l0_dhn_dnorm")
    emit("even_in", {"even_w_in": dw_in_e, "small": g})
    return loss, grad_x


ANY = pl.BlockSpec(memory_space=pl.ANY)
N_DEV = 8


def _place():
    return lax.axis_index("x"), lax.axis_index("y"), lax.axis_index("c")


def _chip_peers(x, y, c):
    return [((1 - x, y, c), 2 * (1 - x) + y), ((x, 1 - y, c), 2 * x + 1 - y), ((1 - x, 1 - y, c), 2 * (1 - x) + 1 - y)]


HBM = pl.BlockSpec(memory_space=pltpu.HBM)
SEM = pl.BlockSpec(memory_space=pltpu.SEMAPHORE)
EFFECT = pltpu.SideEffectType.DATAFLOW_SIDE_EFFECTING
N_PEER = 3


def _half(ref, c):
    r, cols = ref.shape
    tile_rows = 32 // jnp.dtype(ref.dtype).itemsize
    if (r // 2) % tile_rows == 0:
        return ref.at[pl.ds(c * (r // 2), r // 2)]
    assert (cols // 2) % 128 == 0, ref.shape
    return ref.at[:, pl.ds(c * (cols // 2), cols // 2)]


def _gather_plan(srcs, lands, send, recv):
    x, y, c = _place()
    return [pltpu.make_async_remote_copy(src_ref=_half(srcs[i], c), dst_ref=_half(lands[i].at[2 * x + y], c),
                                         send_sem=send.at[N_PEER * i + k], recv_sem=recv.at[N_PEER * i + k],
                                         device_id=peer, device_id_type=MESH_ID)
            for i in range(len(srcs)) for k, (peer, _) in enumerate(_chip_peers(x, y, c))]


def _relay_plan(srcs, lands, send, recv):
    x, y, c = _place()
    return [pltpu.make_async_remote_copy(src_ref=_half(lands[i].at[idx], c), dst_ref=_half(lands[i].at[idx], c),
                                         send_sem=send.at[N_PEER * i + k], recv_sem=recv.at[N_PEER * i + k],
                                         device_id=(x, y, 1 - c), device_id_type=MESH_ID)
            for i in range(len(srcs)) for k, (_, idx) in enumerate(_chip_peers(x, y, c))]


def _scatter_plan(srcs, lands, send, recv):
    x, y, c = _place()
    return [pltpu.make_async_remote_copy(src_ref=srcs[i].at[idx], dst_ref=lands[i].at[k], send_sem=send.at[N_PEER * i + k],
                                         recv_sem=recv.at[N_PEER * i + k], device_id=peer, device_id_type=MESH_ID)
            for i in range(len(srcs)) for k, (peer, idx) in enumerate(_chip_peers(x, y, c))]


def _swap_plan(srcs, lands, send, recv):
    x, y, c = _place()
    return [pltpu.make_async_remote_copy(src_ref=srcs[i], dst_ref=lands[i], send_sem=send.at[N_PEER * i],
                                         recv_sem=recv.at[N_PEER * i], device_id=(x, y, 1 - c), device_id_type=MESH_ID)
            for i in range(len(srcs))]


def copies_start(plan, srcs, lands, after, *, name):
    n = len(srcs)
    both = list(srcs) + list(lands)

    def body(*refs):
        src_refs, land_refs = refs[:n], refs[n:2 * n]
        send, recv = refs[2 * n + 1], refs[2 * n + 2]
        for cp in plan(src_refs, land_refs, send, recv):
            cp.start()
        refs[-1][...] = jnp.zeros_like(refs[-1])

    res = pl.pallas_call(
        body, name=name,
        out_shape=(pltpu.SemaphoreType.DMA((n * N_PEER,)), pltpu.SemaphoreType.DMA((n * N_PEER,)),
                   *[pltpu.HBM(a.shape, a.dtype) for a in both], jax.ShapeDtypeStruct((8, 128), F32)),
        in_specs=[HBM] * (2 * n) + [ANY],
        out_specs=(SEM, SEM, *[HBM] * (2 * n), pl.BlockSpec(memory_space=pltpu.VMEM)),
        input_output_aliases={i: 2 + i for i in range(2 * n)},
        compiler_params=pltpu.CompilerParams(has_side_effects=EFFECT))(
            *[pltpu.with_memory_space_constraint(a, pltpu.HBM) for a in both], after)
    return {"send": res[0], "recv": res[1], "srcs": list(res[2:2 + n]), "lands": list(res[2 + n:2 + 2 * n]),
            "token": res[-1]}


def copies_relay(arrived_plan, next_plan, started, after, *, name):
    srcs, lands = started["srcs"], started["lands"]
    n = len(srcs)
    both = srcs + lands

    def body(*refs):
        src_refs, land_refs = refs[:n], refs[n:2 * n]
        send1, recv1 = refs[2 * n], refs[2 * n + 1]
        send2, recv2 = refs[2 * n + 3], refs[2 * n + 4]
        for cp in arrived_plan(src_refs, land_refs, send1, recv1):
            cp.wait_send()
            cp.wait_recv()
        for cp in next_plan(src_refs, land_refs, send2, recv2):
            cp.start()
        refs[-1][...] = jnp.zeros_like(refs[-1])

    res = pl.pallas_call(
        body, name=name,
        out_shape=(pltpu.SemaphoreType.DMA((n * N_PEER,)), pltpu.SemaphoreType.DMA((n * N_PEER,)),
                   *[pltpu.HBM(a.shape, a.dtype) for a in both], jax.ShapeDtypeStruct((8, 128), F32)),
        in_specs=[HBM] * (2 * n) + [SEM, SEM, ANY],
        out_specs=(SEM, SEM, *[HBM] * (2 * n), pl.BlockSpec(memory_space=pltpu.VMEM)),
        input_output_aliases={i: 2 + i for i in range(2 * n)},
        compiler_params=pltpu.CompilerParams(has_side_effects=EFFECT))(*both, started["send"], started["recv"], after)
    return {"send": res[0], "recv": res[1], "srcs": list(res[2:2 + n]), "lands": list(res[2 + n:2 + 2 * n]),
            "token": res[-1]}


def copies_wait(plan, started, after, *, name):
    srcs, lands = started["srcs"], started["lands"]
    n = len(srcs)
    both = srcs + lands

    def body(*refs):
        src_refs, land_refs = refs[:n], refs[n:2 * n]
        send, recv = refs[2 * n], refs[2 * n + 1]
        for cp in plan(src_refs, land_refs, send, recv):
            cp.wait_send()
            cp.wait_recv()

    res = pl.pallas_call(
        body, name=name, out_shape=tuple(pltpu.HBM(a.shape, a.dtype) for a in both),
        in_specs=[HBM] * (2 * n) + [SEM, SEM, ANY], out_specs=(HBM,) * (2 * n),
        input_output_aliases={i: i for i in range(2 * n)},
        compiler_params=pltpu.CompilerParams(has_side_effects=EFFECT))(*both, started["send"], started["recv"], after)
    return list(res[:n]), list(res[n:])


def allgather_small(small, *, name):
    def body(small_ref, out_ref, send, recv, loc):
        x, y, c = _place()
        dev = 4 * x + 2 * y + c
        local = pltpu.make_async_copy(small_ref, out_ref.at[dev], loc)
        remote = []
        for r in range(1, N_DEV):
            fx, fy, fc = (r >> 2) & 1, (r >> 1) & 1, r & 1
            peer = (1 - x if fx else x, 1 - y if fy else y, 1 - c if fc else c)
            remote.append(pltpu.make_async_remote_copy(
                src_ref=small_ref, dst_ref=out_ref.at[dev], send_sem=send.at[r - 1], recv_sem=recv.at[r - 1],
                device_id=peer, device_id_type=MESH_ID))
        local.start()
        for cp in remote:
            cp.start()
        for cp in remote:
            cp.wait()
        local.wait()

    return pl.pallas_call(
        body, name=name, in_specs=[ANY], out_specs=ANY,
        out_shape=jax.ShapeDtypeStruct((N_DEV,) + small.shape, small.dtype),
        scratch_shapes=[pltpu.SemaphoreType.DMA((N_DEV - 1,)), pltpu.SemaphoreType.DMA((N_DEV - 1,)),
                        pltpu.SemaphoreType.DMA(())])(small)


RED_ROWS = 256
RED_COLS = 256


def _red_block(r, c):
    if r % RED_ROWS == 0:
        return RED_ROWS, c
    if c > RED_COLS and c % RED_COLS == 0:
        return r, RED_COLS
    return r, c


def sum_chips(by_owner, me, got, *, name):
    _, r, c = by_owner.shape
    rb, cb = _red_block(r, c)

    def body(me_ref, o_ref, a_ref, b_ref, c_ref, out_ref):
        total = ((o_ref[...].astype(F32) + a_ref[...].astype(F32)) + b_ref[...].astype(F32)) + c_ref[...].astype(F32)
        out_ref[...] = total.astype(BF16)

    gk = lambda k: pl.BlockSpec((None, rb, cb), lambda i, j, me_ref: (k, i, j))
    grid_spec = pltpu.PrefetchScalarGridSpec(
        num_scalar_prefetch=1, grid=(r // rb, c // cb),
        in_specs=[pl.BlockSpec((None, rb, cb), lambda i, j, me_ref: (me_ref[0], i, j)), gk(0), gk(1), gk(2)],
        out_specs=pl.BlockSpec((rb, cb), lambda i, j, me_ref: (i, j)))
    return pl.pallas_call(
        body, name=name, grid_spec=grid_spec, out_shape=jax.ShapeDtypeStruct((r, c), BF16),
        compiler_params=_params(("parallel", "parallel")))(me, by_owner, got, got, got)


def sum_devices(small_all, *, name):
    _, p, c = small_all.shape

    def body(a_ref, out_ref):
        acc = a_ref[0]
        for d in range(1, N_DEV):
            acc = acc + a_ref[d]
        out_ref[...] = acc

    return pl.pallas_call(
        body, name=name, grid=(1,), in_specs=[pl.BlockSpec((N_DEV, p, c), lambda i: (0, 0, 0))],
        out_specs=pl.BlockSpec((p, c), lambda i: (0, 0)), out_shape=jax.ShapeDtypeStruct((p, c), F32),
        compiler_params=_params(("arbitrary",)))(small_all)


def adamw(parts, w, m, v, *, name):
    nl, r, c = w.shape
    assert len(parts) == nl
    npart = len(parts[0])
    rb, cb = _red_block(r, c)
    flat = [a for layer in parts for a in layer]

    def body(*refs):
        p_refs, (w_ref, m_ref, v_ref) = refs[:nl * npart], refs[nl * npart:nl * npart + 3]
        g_ref, d_ref, nm_ref, nv_ref = refs[nl * npart + 3:]
        layer = pl.program_id(0)
        grad = None
        for l in range(nl):
            gl = p_refs[l * npart][...].astype(F32)
            for j in range(1, npart):
                gl = gl + p_refs[l * npart + j][...].astype(F32)
            grad = gl if grad is None else jnp.where(layer == l, gl, grad)
        wv, mv, vv = w_ref[...], m_ref[...], v_ref[...]
        nm = ADAM_B1 * mv + (1.0 - ADAM_B1) * grad
        nv = ADAM_B2 * vv + (1.0 - ADAM_B2) * (grad * grad)
        m_hat = nm / (1.0 - ADAM_B1 ** ADAM_STEP)
        v_hat = nv / (1.0 - ADAM_B2 ** ADAM_STEP)
        g_ref[...] = grad
        d_ref[...] = -ADAM_LR * (m_hat / (jnp.sqrt(v_hat) + ADAM_EPS) + ADAM_WD * wv)
        nm_ref[...] = nm
        nv_ref[...] = nv

    pspec = pl.BlockSpec((rb, cb), lambda l, i, j: (i, j))
    wspec = pl.BlockSpec((None, rb, cb), lambda l, i, j: (l, i, j))
    osh = jax.ShapeDtypeStruct((nl, r, c), F32)
    return pl.pallas_call(
        body, name=name, grid=(nl, r // rb, c // cb), in_specs=[pspec] * (nl * npart) + [wspec] * 3,
        out_specs=[wspec] * 4, out_shape=[osh] * 4,
        compiler_params=_params(("parallel", "parallel", "parallel")))(*flat, w, m, v)


def _rows128(a):
    flat = a.reshape(-1)
    pad = (-flat.shape[0]) % 128
    return jnp.pad(flat, (0, pad)).reshape(-1, 128)


def _pack_rows(arrs, multiple=8):
    rows = jnp.concatenate([_rows128(a.astype(F32)) for a in arrs], axis=0)
    return jnp.pad(rows, ((0, (-rows.shape[0]) % multiple), (0, 0)))


def _unpack_rows(rows, shapes):
    out, r0 = [], 0
    for shp in shapes:
        size = 1
        for s in shp:
            size *= s
        nr = -(-size // 128)
        out.append(rows[r0:r0 + nr].reshape(-1)[:size].reshape(shp))
        r0 += nr
    return out


SMALL_LOCAL_GRADS = ["even_norm", "even_conv", "a_log", "dt_bias", "sinks", "onorm", "odd_norm", "odd_ln_g",
                     "odd_ln_b", "odd_w_s", "odd_b_s", "ffn_norm", "final_norm"]
BIG = ["even_w_in", "even_w_out", "odd_w_in", "odd_w_out", "ffn_w_gate", "ffn_w_up", "ffn_w_down"]
WEIGHTS = ["even_norm", "even_w_in", "even_conv", "even_a_log", "even_dt_bias", "even_sinks", "even_onorm",
           "even_w_out", "odd_norm", "odd_w_in", "odd_ln_g", "odd_ln_b", "odd_w_s", "odd_b_s", "odd_w_out",
           "ffn_norm", "ffn_w_gate", "ffn_w_up", "ffn_w_down", "final_norm"]
SMALL = [n for n in WEIGHTS if n not in BIG]


def kernel(x, even_norm, even_w_in, even_conv, even_a_log, even_dt_bias, even_sinks, even_onorm, even_w_out, odd_norm, odd_w_in, odd_ln_g, odd_ln_b, odd_w_s, odd_b_s, odd_w_out, ffn_norm, ffn_w_gate, ffn_w_up, ffn_w_down, final_norm, loss_target, m_even_norm, m_even_w_in, m_even_conv, m_even_a_log, m_even_dt_bias, m_even_sinks, m_even_onorm, m_even_w_out, m_odd_norm, m_odd_w_in, m_odd_ln_g, m_odd_ln_b, m_odd_w_s, m_odd_b_s, m_odd_w_out, m_ffn_norm, m_ffn_w_gate, m_ffn_w_up, m_ffn_w_down, m_final_norm, v_even_norm, v_even_w_in, v_even_conv, v_even_a_log, v_even_dt_bias, v_even_sinks, v_even_onorm, v_even_w_out, v_odd_norm, v_odd_w_in, v_odd_ln_g, v_odd_ln_b, v_odd_w_s, v_odd_b_s, v_odd_w_out, v_ffn_norm, v_ffn_w_gate, v_ffn_w_up, v_ffn_w_down, v_final_norm):
    args = dict(locals())
    wl = {n: args[n] for n in WEIGHTS}
    ml = {n: args["m_" + n] for n in WEIGHTS}
    vl = {n: args["v_" + n] for n in WEIGHTS}
    me = 2 * lax.axis_index("x") + lax.axis_index("y")

    def landing(a):
        return lax.dynamic_update_index_in_dim(lax.empty((N_SHARD,) + a.shape, a.dtype), a, me, 0)

    gather_groups = {
        "even_in": [even_w_in[0].T], "even_out": [even_w_out[0]],
        "ffn0": [ffn_w_gate[0], ffn_w_up[0], ffn_w_down[0]], "odd": [odd_w_in[0], odd_w_out[0]],
        "ffn1": [ffn_w_gate[1], ffn_w_up[1], ffn_w_down[1]],
    }
    gathering, after = {}, even_norm
    for group, arrs in gather_groups.items():
        srcs = [(a + after[0, 0] if gathering else a).astype(BF16) for a in arrs]
        if group == "even_in":
            srcs.append(_pack_rows([even_conv[0], odd_norm, odd_ln_g, odd_ln_b], multiple=16))
        gathering[group] = copies_start(_gather_plan, srcs, [landing(a) for a in srcs], after,
                                        name=f"gather_{group}_start")
        after = gathering[group]["token"]

    order = list(gather_groups)
    relayed, kept = {}, {}
    sinks_pad = jnp.pad(even_sinks, ((0, 0), (0, 128 - A_HEADS)))

    def relay(group, behind):
        relayed[group] = copies_relay(_gather_plan, _relay_plan, gathering[group], behind,
                                      name=f"gather_{group}_relay")
        return relayed[group]["token"][0:1, 0:1]

    def get(group, behind):
        if group not in relayed:
            relay(group, behind)
        _, lands = copies_wait(_relay_plan, relayed[group], behind, name=f"gather_{group}_wait")
        nxt = order.index(group) + 1
        tok = relay(order[nxt], lands[0]) if nxt < len(order) else jnp.zeros((1, 1), F32)
        if group == "even_in":
            parts = zip(*[_unpack_rows(lands[1][s], [(CONV_K, 768), (1, 512), (1, 512), (1, 512)])
                          for s in range(N_SHARD)])
            conv, onorm, lng, lnb = [jnp.concatenate(p, axis=1) for p in parts]
            w_in = jnp.pad(lands[0].reshape(EVEN_IN, D_MODEL), ((0, EVEN_IN_PAD - EVEN_IN), (0, 0)))
            kept["odd_ln_g"] = lng
            return {"even_w_in": w_in, "even_conv": conv + tok, "odd_norm": onorm, "odd_ln_b": lnb}
        if group == "even_out":
            return {"even_w_out": lands[0].reshape(D_MODEL, D_MODEL), "onorm": even_onorm + tok}
        if group == "odd":
            return {"odd_w_in": lands[0], "odd_w_out": lands[1].reshape(D_MODEL, D_MODEL),
                    "odd_ln_g": kept["odd_ln_g"] + tok}
        return {"gate": lands[0], "up": lands[1], "down": lands[2].reshape(D_FF, D_MODEL), "tok": tok}

    rows4 = lambda a: a.reshape(N_SHARD, a.shape[0] // N_SHARD, a.shape[1])
    scattering, small = {}, {}

    def emit(group, grads):
        behind = even_norm
        if group == "even_in":
            small["local"] = grads["small"]
            small["all"] = behind = allgather_small(_pack_rows([grads["small"][n] for n in SMALL_LOCAL_GRADS]),
                                                    name="allgather_small")
            srcs = [grads["even_w_in"][:EVEN_IN].reshape(N_SHARD, EVEN_IN // N_SHARD, D_MODEL)]
        elif group == "even_out":
            srcs = [rows4(grads["even_w_out"])]
        elif group == "odd":
            srcs = [grads["odd_w_in"], rows4(grads["odd_w_out"])]
        else:
            srcs = [grads["gate"], grads["up"], rows4(grads["down"])]
        lands = [lax.empty((N_PEER,) + a.shape[1:], a.dtype) for a in srcs]
        scattering[group] = copies_start(_scatter_plan, srcs, lands, behind, name=f"scatter_{group}_start")
        return scattering[group]["token"][0:1, 0:1]

    pad816 = lambda a: jnp.pad(a, ((0, 0), (B_HEADS, 128 - 2 * B_HEADS)))
    w = {
        "even_norm": even_norm + after[0:1, 0:1],
        "a_log": pad816(even_a_log), "dt_bias": pad816(even_dt_bias),
        "sinks": sinks_pad,
        "onorm": even_onorm,
        "odd_w_s": odd_w_s[0],
        "odd_b_s": jnp.pad(odd_b_s[0].T, ((0, 0), (0, 128 - C_GROUPS))),
        "ffn_norm": ffn_norm,
        "final_norm": final_norm[None],
    }
    loss_l, grad_x = _local_step(x[0], loss_target[0], w, get, emit)
    loss = lax.psum(loss_l[0, 0], ("x", "y", "c"))

    me1 = me.reshape(1).astype(jnp.int32)
    swapping = {}

    def reduce_chips(group, behind):
        srcs, lands = copies_wait(_scatter_plan, scattering[group], behind, name=f"scatter_{group}_wait")
        partial = [sum_chips(srcs[i], me1, lands[i], name=f"sum_chips_{group}_{i}") for i in range(len(srcs))]
        swapping[group] = copies_start(_swap_plan, partial, [lax.empty(p.shape, p.dtype) for p in partial],
                                       even_norm, name=f"swap_{group}_start")
        return swapping[group]["token"]

    def swapped(group, behind):
        mine, theirs = copies_wait(_swap_plan, swapping[group], behind, name=f"swap_{group}_wait")
        return list(zip(mine, theirs))

    behind = scattering["even_in"]["token"]
    for group in ("ffn1", "ffn0", "odd", "even_out"):
        behind = reduce_chips(group, behind)
    sums = {group: swapped(group, behind) for group in ("ffn1", "ffn0", "odd", "even_out")}
    outs = {}
    parts_of = {"even_w_out": [sums["even_out"][0]], "odd_w_in": [sums["odd"][0]], "odd_w_out": [sums["odd"][1]],
                "ffn_w_gate": [sums["ffn0"][0], sums["ffn1"][0]], "ffn_w_up": [sums["ffn0"][1], sums["ffn1"][1]],
                "ffn_w_down": [sums["ffn0"][2], sums["ffn1"][2]]}
    for n in parts_of:
        outs[n] = adamw(parts_of[n], wl[n], ml[n], vl[n], name=f"adamw_{n}")
    behind = reduce_chips("even_in", outs["ffn_w_down"][1])
    flip = lambda a: jnp.transpose(a, (0, 2, 1))
    outs["even_w_in"] = [flip(o) for o in adamw([swapped("even_in", behind)[0]], flip(wl["even_w_in"]),
                                                flip(ml["even_w_in"]), flip(vl["even_w_in"]),
                                                name="adamw_even_w_in")]

    g = small["local"]
    small_sum = sum_devices(small["all"], name="sum_devices")
    sg = dict(zip(SMALL_LOCAL_GRADS, _unpack_rows(small_sum, [g[n].shape for n in SMALL_LOCAL_GRADS])))
    own_cols = lambda a, width: lax.dynamic_slice_in_dim(a, me * width, width, axis=a.ndim - 1)
    small_grads = {
        "even_norm": sg["even_norm"], "even_conv": own_cols(sg["even_conv"], 768)[None],
        "even_a_log": sg["a_log"][:, B_HEADS:2 * B_HEADS], "even_dt_bias": sg["dt_bias"][:, B_HEADS:2 * B_HEADS],
        "even_sinks": sg["sinks"][:, :A_HEADS], "even_onorm": sg["onorm"],
        "odd_norm": own_cols(sg["odd_norm"], 512), "odd_ln_g": own_cols(sg["odd_ln_g"], 512),
        "odd_ln_b": own_cols(sg["odd_ln_b"], 512), "odd_w_s": sg["odd_w_s"][None],
        "odd_b_s": sg["odd_b_s"][:, :C_GROUPS].T[None], "ffn_norm": sg["ffn_norm"], "final_norm": sg["final_norm"][0],
    }
    packed = [_pack_rows([d[n] for n in SMALL])[None] for d in (small_grads, wl, ml, vl)]
    small_out = adamw([(packed[0][0],)], packed[1], packed[2], packed[3], name="adamw_small")
    shapes = [wl[n].shape for n in SMALL]
    for j in range(4):
        for n, a in zip(SMALL, _unpack_rows(small_out[j][0], shapes)):
            outs.setdefault(n, [None] * 4)[j] = a

    return (loss, grad_x[None], *[outs[n][0] for n in WEIGHTS], *[outs[n][1] for n in WEIGHTS],
            *[outs[n][2] for n in WEIGHTS], *[outs[n][3] for n in WEIGHTS])
```

```python
import functools

import jax
import jax.numpy as jnp
from jax import lax
from jax.experimental import pallas as pl
from jax.experimental.pallas import tpu as pltpu

F32 = jnp.float32
BF16 = jnp.bfloat16
NEG_INF = float("-inf")

D_MODEL = 2048
A_HEADS, A_KV_HEADS, A_HEAD_DIM, WINDOW = 16, 2, 64, 128
B_HEADS, B_HEAD_DIM, CONV_K, DN_CHUNK = 8, 128, 4, 64
C_GROUPS, C_CHUNK = 8, 128
C_GROUP_DIM = D_MODEL // C_GROUPS
D_FF = 5632
EPS = 1e-6
A_Q = A_HEADS * A_HEAD_DIM
A_KV = A_KV_HEADS * A_HEAD_DIM
B_W = B_HEADS * B_HEAD_DIM
EVEN_IN = A_Q + 2 * A_KV + 4 * B_W + 2 * B_HEADS
EVEN_IN_PAD = 5632
COL_KV = A_Q
COL_QKVB = A_Q + 2 * A_KV
COL_Z = COL_QKVB + 3 * B_W
COL_GATE = COL_Z + B_W
N_SHARD = 4

ADAM_LR, ADAM_B1, ADAM_B2, ADAM_EPS, ADAM_WD, ADAM_STEP = 0.001, 0.9, 0.999, 1e-08, 0.01, 10

VMEM_LIMIT_V7X = 56 * 1024 * 1024
MXU_COLS = 256
MESH_ID = pl.DeviceIdType.MESH


def _params(sem=None):
    return pltpu.CompilerParams(dimension_semantics=sem, vmem_limit_bytes=VMEM_LIMIT_V7X)


def _sigmoid(x):
    return 1.0 / (1.0 + jnp.exp(-x))


def _silu(x):
    return x * _sigmoid(x)


def _dsilu(x):
    s = _sigmoid(x)
    return s * (1.0 + x * (1.0 - s))


def _gelu(x):
    return 0.5 * x * (1.0 + lax.erf(x * 0.7071067811865476))


def _dgelu(x):
    return 0.5 * (1.0 + lax.erf(x * 0.7071067811865476)) + x * jnp.exp(-0.5 * x * x) * 0.3989422804014327


def _dot(a, b, dims):
    if a.ndim == 3:
        (ca,), (cb,) = dims
        return lax.dot_general(a, b, (((ca + 1,), (cb + 1,)), ((0,), (0,))), preferred_element_type=F32)
    return lax.dot_general(a, b, (dims, ((), ())), preferred_element_type=F32)


NN = ((1,), (0,))
NT = ((1,), (1,))
TN = ((0,), (0,))


def _as3(b):
    return b if b.ndim == 3 else b[None]


def _accumulate(step, nsteps, accs, products, finish):
    if nsteps == 1:
        finish(products())
        return

    @pl.when(step == 0)
    def _():
        for acc, p in zip(accs, products()):
            acc[...] = p

    if nsteps > 2:
        @pl.when((step > 0) & (step < nsteps - 1))
        def _():
            for acc, p in zip(accs, products()):
                acc[...] += p

    @pl.when(step == nsteps - 1)
    def _():
        finish(tuple(acc[...] + p for acc, p in zip(accs, products())))


def mm_nn(a, b, *, tm, tn, tk, out_dtype, name, res=None):
    b3 = _as3(b)
    m, k = a.shape
    s, k2, ns = b3.shape
    assert k2 == k and m % tm == 0 and ns % tn == 0 and k % tk == 0, (a.shape, b3.shape, tm, tn, tk)
    nps, nk = ns // tn, k // tk

    def body(*refs):
        if res is None:
            a_ref, b_ref, o_ref, acc = refs
        else:
            a_ref, b_ref, r_ref, o_ref, acc = refs
        def finish(tiles):
            r = tiles[0] if res is None else tiles[0] + r_ref[...].astype(F32)
            o_ref[...] = r.astype(out_dtype)

        _accumulate(pl.program_id(2), nk, (acc,),
                    lambda: (_dot(a_ref[...].astype(BF16), b_ref[...].astype(BF16), NN),), finish)

    in_specs = [pl.BlockSpec((tm, tk), lambda i, j, kk: (i, kk)),
                pl.BlockSpec((None, tk, tn), lambda i, j, kk: (j // nps, kk, j % nps))]
    args = [a, b3]
    if res is not None:
        in_specs.append(pl.BlockSpec((tm, tn), lambda i, j, kk: (i, j)))
        args.append(res)
    return pl.pallas_call(
        body, name=name, grid=(m // tm, s * nps, nk), in_specs=in_specs,
        out_specs=pl.BlockSpec((tm, tn), lambda i, j, kk: (i, j)),
        out_shape=jax.ShapeDtypeStruct((m, s * ns), out_dtype),
        scratch_shapes=[pltpu.VMEM((tm, tn), F32)],
        compiler_params=_params(("parallel", "parallel", "arbitrary")))(*args)


def mm_nt(a, b, *, tm, tn, tk, out_dtype, name, res=None):
    b3 = _as3(b)
    m, n = a.shape
    s, k, ns = b3.shape
    assert n == s * ns and m % tm == 0 and k % tn == 0 and ns % tk == 0, (a.shape, b3.shape, tm, tn, tk)
    rps = ns // tk
    nr = s * rps

    def body(*refs):
        if res is None:
            a_ref, b_ref, o_ref, acc = refs
        else:
            a_ref, b_ref, r_ref, o_ref, acc = refs
        def finish(tiles):
            r = tiles[0] if res is None else tiles[0] + r_ref[...].astype(F32)
            o_ref[...] = r.astype(out_dtype)

        _accumulate(pl.program_id(2), nr, (acc,),
                    lambda: (_dot(a_ref[...].astype(BF16), b_ref[...].astype(BF16), NT),), finish)

    in_specs = [pl.BlockSpec((tm, tk), lambda i, j, r: (i, r)),
                pl.BlockSpec((None, tn, tk), lambda i, j, r: (r // rps, j, r % rps))]
    args = [a, b3]
    if res is not None:
        in_specs.append(pl.BlockSpec((tm, tn), lambda i, j, r: (i, j)))
        args.append(res)
    return pl.pallas_call(
        body, name=name, grid=(m // tm, k // tn, nr), in_specs=in_specs,
        out_specs=pl.BlockSpec((tm, tn), lambda i, j, r: (i, j)),
        out_shape=jax.ShapeDtypeStruct((m, k), out_dtype),
        scratch_shapes=[pltpu.VMEM((tm, tn), F32)],
        compiler_params=_params(("parallel", "parallel", "arbitrary")))(*args)


def mm_tn(a, b, *, shards, tm, tn, tk, out_dtype, name):
    m, k = a.shape
    m2, n = b.shape
    ns = n // shards
    assert m2 == m and n == shards * ns and m % tm == 0 and k % tk == 0 and ns % tn == 0, (a.shape, b.shape)
    nps, nm = ns // tn, m // tm

    def body(a_ref, b_ref, o_ref, acc):
        def finish(tiles):
            o_ref[...] = tiles[0].astype(out_dtype)

        _accumulate(pl.program_id(2), nm, (acc,),
                    lambda: (_dot(a_ref[...].astype(BF16), b_ref[...].astype(BF16), TN),), finish)

    return pl.pallas_call(
        body, name=name, grid=(k // tk, shards * nps, nm),
        in_specs=[pl.BlockSpec((tm, tk), lambda i, j, mi: (mi, i)),
                  pl.BlockSpec((tm, tn), lambda i, j, mi: (mi, j))],
        out_specs=pl.BlockSpec((None, tk, tn), lambda i, j, mi: (j // nps, i, j % nps)),
        out_shape=jax.ShapeDtypeStruct((shards, k, ns), out_dtype),
        scratch_shapes=[pltpu.VMEM((tk, tn), F32)],
        compiler_params=_params(("parallel", "parallel", "arbitrary")))(a, b)


def mm_gate_up(hn, wg, wu, *, tm, tn, tk, name):
    wg3, wu3 = _as3(wg), _as3(wu)
    m, k = hn.shape
    s, _, ns = wg3.shape
    assert m % tm == 0 and ns % tn == 0 and k % tk == 0
    nps, nk = ns // tn, k // tk

    def body(a_ref, g_ref, u_ref, og_ref, ou_ref, oa_ref, accg, accu):
        def products():
            a = a_ref[...].astype(BF16)
            return _dot(a, g_ref[...].astype(BF16), NN), _dot(a, u_ref[...].astype(BF16), NN)

        def finish(tiles):
            g, u = tiles
            og_ref[...] = g.astype(BF16)
            ou_ref[...] = u.astype(BF16)
            oa_ref[...] = (_silu(g) * u).astype(BF16)

        _accumulate(pl.program_id(2), nk, (accg, accu), products, finish)

    wspec = pl.BlockSpec((None, tk, tn), lambda i, j, kk: (j // nps, kk, j % nps))
    ospec = pl.BlockSpec((tm, tn), lambda i, j, kk: (i, j))
    osh = jax.ShapeDtypeStruct((m, s * ns), BF16)
    return pl.pallas_call(
        body, name=name, grid=(m // tm, s * nps, nk),
        in_specs=[pl.BlockSpec((tm, tk), lambda i, j, kk: (i, kk)), wspec, wspec],
        out_specs=[ospec, ospec, ospec], out_shape=[osh, osh, osh],
        scratch_shapes=[pltpu.VMEM((tm, tn) if nk > 1 else (8, 128), F32)] * 2,
        compiler_params=_params(("parallel", "parallel", "arbitrary")))(hn, wg3, wu3)


def mm_down_bwd(dh, wd, gate, up, *, tm, tn, tk, name):
    m, d = dh.shape
    f, d2 = wd.shape
    assert d2 == d and m % tm == 0 and f % tn == 0 and tk == d and tn % MXU_COLS == 0

    def body(a_ref, b_ref, g_ref, u_ref, og_ref, ou_ref):
        a = a_ref[...].astype(BF16)
        for jj in range(tn // MXU_COLS):
            sl = slice(jj * MXU_COLS, (jj + 1) * MXU_COLS)
            da = _dot(a, b_ref[sl, :].astype(BF16), NT)
            g, u = g_ref[:, sl].astype(F32), u_ref[:, sl].astype(F32)
            s = _sigmoid(g)
            og_ref[:, sl] = (da * u * (s * (1.0 + g * (1.0 - s)))).astype(BF16)
            ou_ref[:, sl] = (da * (g * s)).astype(BF16)

    ospec = pl.BlockSpec((tm, tn), lambda i, j: (i, j))
    osh = jax.ShapeDtypeStruct((m, f), BF16)
    return pl.pallas_call(
        body, name=name, grid=(m // tm, f // tn),
        in_specs=[pl.BlockSpec((tm, tk), lambda i, j: (i, 0)),
                  pl.BlockSpec((tn, tk), lambda i, j: (j, 0)), ospec, ospec],
        out_specs=[ospec, ospec], out_shape=[osh, osh],
        compiler_params=_params(("parallel", "parallel")))(dh, wd, gate, up)


ROWS = 256


def rms_fwd(x, g, *, name):
    t, d = x.shape

    def body(x_ref, g_ref, o_ref):
        xv = x_ref[...]
        r = lax.rsqrt(jnp.mean(xv * xv, axis=-1, keepdims=True) + EPS)
        o_ref[...] = (xv * r * g_ref[...]).astype(BF16)

    return pl.pallas_call(
        body, name=name, grid=(t // ROWS,),
        in_specs=[pl.BlockSpec((ROWS, d), lambda i: (i, 0)), pl.BlockSpec((1, d), lambda i: (0, 0))],
        out_specs=pl.BlockSpec((ROWS, d), lambda i: (i, 0)),
        out_shape=jax.ShapeDtypeStruct((t, d), BF16), compiler_params=_params(("parallel",)))(x, g)


def dgrad_rms_bwd(a, b, form, x, g, dres, *, tm, tk, name, res=None):
    m, d = x.shape
    b3 = _as3(b)
    if form == NN:
        steps = a.shape[1] // tk
        a_spec = pl.BlockSpec((tm, tk), lambda i, r: (i, r))
        b_spec = pl.BlockSpec((None, tk, d), lambda i, r: (0, r, 0))
    else:
        s, d2, ns = b3.shape
        assert d2 == d and ns % tk == 0
        rps = ns // tk
        steps = s * rps
        a_spec = pl.BlockSpec((tm, tk), lambda i, r: (i, r))
        b_spec = pl.BlockSpec((None, d, tk), lambda i, r: (r // rps, 0, r % rps))
    assert m % tm == 0 and a.shape[1] == steps * tk

    def body(*refs):
        if res is None:
            a_ref, b_ref, x_ref, g_ref, dr_ref, dx_ref, dg_ref, acc = refs
        else:
            a_ref, b_ref, r_ref, x_ref, g_ref, dr_ref, dx_ref, dg_ref, acc = refs

        @pl.when((pl.program_id(0) == 0) & (pl.program_id(1) == 0))
        def _():
            dg_ref[...] = jnp.zeros_like(dg_ref)

        def finish(tiles):
            dyv = tiles[0] if res is None else tiles[0] + r_ref[...]
            xv = x_ref[...]
            r = lax.rsqrt(jnp.mean(xv * xv, axis=-1, keepdims=True) + EPS)
            dyg = dyv * g_ref[...]
            dx_ref[...] = r * dyg - xv * (r * r * r) * jnp.mean(dyg * xv, axis=-1, keepdims=True) + dr_ref[...]
            dg_ref[...] += jnp.sum(dyv * xv * r, axis=0, keepdims=True)

        _accumulate(pl.program_id(1), steps, (acc,),
                    lambda: (_dot(a_ref[...].astype(BF16), b_ref[...].astype(BF16), form),), finish)

    row = pl.BlockSpec((tm, d), lambda i, r: (i, 0))
    vec = pl.BlockSpec((1, d), lambda i, r: (0, 0))
    in_specs = [a_spec, b_spec] + ([row] if res is not None else []) + [row, vec, row]
    args = [a, b3] + ([res] if res is not None else []) + [x, g, dres]
    return pl.pallas_call(
        body, name=name, grid=(m // tm, steps), in_specs=in_specs, out_specs=[row, vec],
        out_shape=[jax.ShapeDtypeStruct((m, d), F32), jax.ShapeDtypeStruct((1, d), F32)],
        scratch_shapes=[pltpu.VMEM((tm, d), F32)],
        compiler_params=_params(("arbitrary", "arbitrary")))(*args)


def loss_head(h, g, target, *, name):
    t, d = h.shape

    def body(x_ref, g_ref, t_ref, loss_ref, dx_ref, dg_ref):
        @pl.when(pl.program_id(0) == 0)
        def _():
            dg_ref[...] = jnp.zeros_like(dg_ref)
            loss_ref[...] = jnp.zeros_like(loss_ref)

        xv, gv = x_ref[...], g_ref[...]
        r = lax.rsqrt(jnp.mean(xv * xv, axis=-1, keepdims=True) + EPS)
        e = xv * r * gv - t_ref[...]
        loss_ref[...] += 0.5 * jnp.sum(jnp.mean(e * e, axis=-1, keepdims=True), axis=0, keepdims=True)
        dyv = e * (1.0 / d)
        dyg = dyv * gv
        dx_ref[...] = r * dyg - xv * (r * r * r) * jnp.mean(dyg * xv, axis=-1, keepdims=True)
        dg_ref[...] += jnp.sum(dyv * xv * r, axis=0, keepdims=True)

    row = pl.BlockSpec((ROWS, d), lambda i: (i, 0))
    vec = pl.BlockSpec((1, d), lambda i: (0, 0))
    return pl.pallas_call(
        body, name=name, grid=(t // ROWS,), in_specs=[row, vec, row],
        out_specs=[pl.BlockSpec((1, 128), lambda i: (0, 0)), row, vec],
        out_shape=[jax.ShapeDtypeStruct((1, 128), F32), jax.ShapeDtypeStruct((t, d), F32),
                   jax.ShapeDtypeStruct((1, d), F32)],
        compiler_params=_params(("arbitrary",)))(h, g, target)


def _tril_mask():
    r = lax.broadcasted_iota(jnp.int32, (C_CHUNK, C_CHUNK), 0)
    c = lax.broadcasted_iota(jnp.int32, (C_CHUNK, C_CHUNK), 1)
    return r >= c


def _layer_norm_parts(v):
    mu = jnp.mean(v, axis=-1, keepdims=True)
    vc = v - mu
    rstd = lax.rsqrt(jnp.mean(vc * vc, axis=-1, keepdims=True) + EPS)
    return vc * rstd, rstd


def gmlp_fwd(zpre, ln_g, ln_b, ws, bs_t, *, name):
    t = zpre.shape[0]
    d = D_MODEL

    def body(zu_ref, zv_ref, g_ref, b_ref, ws_ref, bs_ref, o_ref):
        u = _gelu(zu_ref[...])
        vhat, _ = _layer_norm_parts(_gelu(zv_ref[...]))
        vln = (vhat * g_ref[...] + b_ref[...]).astype(BF16)
        mask = _tril_mask()
        for gi in range(C_GROUPS):
            sl = slice(gi * C_GROUP_DIM, (gi + 1) * C_GROUP_DIM)
            w = jnp.where(mask, ws_ref[gi], 0.0).astype(BF16)
            mixed = _dot(w, vln[:, sl], NN) + bs_ref[:, gi:gi + 1]
            o_ref[:, sl] = (u[:, sl] * mixed).astype(BF16)

    vec = pl.BlockSpec((1, d), lambda i: (0, 0))
    return pl.pallas_call(
        body, name=name, grid=(t // C_CHUNK,),
        in_specs=[pl.BlockSpec((C_CHUNK, d), lambda i: (i, 0)), pl.BlockSpec((C_CHUNK, d), lambda i: (i, 1)),
                  vec, vec, pl.BlockSpec((C_GROUPS, C_CHUNK, C_CHUNK), lambda i: (0, 0, 0)),
                  pl.BlockSpec((C_CHUNK, 128), lambda i: (0, 0))],
        out_specs=pl.BlockSpec((C_CHUNK, d), lambda i: (i, 0)),
        out_shape=jax.ShapeDtypeStruct((t, d), BF16), compiler_params=_params(("parallel",)))(
            zpre, zpre, ln_g, ln_b, ws, bs_t)


def gmlp_bwd(zpre, dgated, ln_g, ln_b, ws, bs_t, *, name):
    t = zpre.shape[0]
    d = D_MODEL

    def body(zu_ref, zv_ref, dg_ref, g_ref, b_ref, ws_ref, bs_ref, dz_ref, dws_ref, dbs_ref, dlg_ref, dlb_ref):
        @pl.when(pl.program_id(0) == 0)
        def _():
            dws_ref[...] = jnp.zeros_like(dws_ref)
            dbs_ref[...] = jnp.zeros_like(dbs_ref)
            dlg_ref[...] = jnp.zeros_like(dlg_ref)
            dlb_ref[...] = jnp.zeros_like(dlb_ref)

        zu, zv = zu_ref[...], zv_ref[...]
        u = _gelu(zu)
        vhat, rstd = _layer_norm_parts(_gelu(zv))
        gam = g_ref[...]
        vln = (vhat * gam + b_ref[...]).astype(BF16)
        dgt = dg_ref[...].astype(F32)
        mask = _tril_mask()
        lane = lax.broadcasted_iota(jnp.int32, (C_CHUNK, 128), 1)
        dbs = jnp.zeros((C_CHUNK, 128), F32)
        du_parts, dvln_parts = [], []
        for gi in range(C_GROUPS):
            sl = slice(gi * C_GROUP_DIM, (gi + 1) * C_GROUP_DIM)
            w = jnp.where(mask, ws_ref[gi], 0.0).astype(BF16)
            mixed = _dot(w, vln[:, sl], NN) + bs_ref[:, gi:gi + 1]
            du_parts.append(dgt[:, sl] * mixed)
            dmixed = dgt[:, sl] * u[:, sl]
            dmb = dmixed.astype(BF16)
            dws_ref[gi] += jnp.where(mask, _dot(dmb, vln[:, sl], NT), 0.0)
            dbs = dbs + jnp.where(lane == gi, jnp.sum(dmixed, axis=-1, keepdims=True), 0.0)
            dvln_parts.append(_dot(w, dmb, TN))
        dbs_ref[...] += dbs
        du = jnp.concatenate(du_parts, axis=-1)
        dvln = jnp.concatenate(dvln_parts, axis=-1)
        dlg_ref[...] += jnp.sum(dvln * vhat, axis=0, keepdims=True)
        dlb_ref[...] += jnp.sum(dvln, axis=0, keepdims=True)
        dvhat = dvln * gam
        dv = rstd * (dvhat - jnp.mean(dvhat, axis=-1, keepdims=True)
                     - vhat * jnp.mean(dvhat * vhat, axis=-1, keepdims=True))
        dz_ref[:, :d] = (du * _dgelu(zu)).astype(BF16)
        dz_ref[:, d:] = (dv * _dgelu(zv)).astype(BF16)

    vec = pl.BlockSpec((1, d), lambda i: (0, 0))
    wsp = pl.BlockSpec((C_GROUPS, C_CHUNK, C_CHUNK), lambda i: (0, 0, 0))
    bsp = pl.BlockSpec((C_CHUNK, 128), lambda i: (0, 0))
    return pl.pallas_call(
        body, name=name, grid=(t // C_CHUNK,),
        in_specs=[pl.BlockSpec((C_CHUNK, d), lambda i: (i, 0)), pl.BlockSpec((C_CHUNK, d), lambda i: (i, 1)),
                  pl.BlockSpec((C_CHUNK, d), lambda i: (i, 0)), vec, vec, wsp, bsp],
        out_specs=[pl.BlockSpec((C_CHUNK, 2 * d), lambda i: (i, 0)), wsp, bsp, vec, vec],
        out_shape=[jax.ShapeDtypeStruct((t, 2 * d), BF16), jax.ShapeDtypeStruct((C_GROUPS, C_CHUNK, C_CHUNK), F32),
                   jax.ShapeDtypeStruct((C_CHUNK, 128), F32), jax.ShapeDtypeStruct((1, d), F32),
                   jax.ShapeDtypeStruct((1, d), F32)],
        compiler_params=_params(("arbitrary",)))(zpre, zpre, dgated, ln_g, ln_b, ws, bs_t)


ATT_SCALE = A_HEAD_DIM ** -0.5
PAIRS = A_HEADS // 2
PAIRS_PER_KV = PAIRS // A_KV_HEADS


def _att_padded(tile):
    lo = lax.broadcasted_iota(jnp.int32, tile.shape, 1) < A_HEAD_DIM
    rolled = pltpu.roll(tile, A_HEAD_DIM, 1)
    zero = jnp.zeros_like(tile)
    return {(0, 0): jnp.where(lo, tile, zero).astype(BF16), (0, 1): jnp.where(lo, zero, rolled).astype(BF16),
            (1, 0): jnp.where(lo, rolled, zero).astype(BF16), (1, 1): jnp.where(lo, zero, tile).astype(BF16)}


def _att_valid(n):
    r = lax.broadcasted_iota(jnp.int32, (WINDOW, 2 * WINDOW), 0)
    c = lax.broadcasted_iota(jnp.int32, (WINDOW, 2 * WINDOW), 1)
    rel = r + WINDOW - c
    return (rel >= 0) & (rel < WINDOW) & ((c >= WINDOW) | (n > 0))


def _att_probs(qp, kpad, sink, valid):
    s = jnp.where(valid, _dot(qp, kpad, NT), NEG_INF)
    m = jnp.maximum(jnp.max(s, axis=-1, keepdims=True), sink)
    p = jnp.exp(s - m)
    e_sink = jnp.exp(sink - m)
    inv = 1.0 / (jnp.sum(p, axis=-1, keepdims=True) + e_sink)
    return p * inv, e_sink * inv


ATT_BLOCKS = 4


def _att_operands(q_ref, kvc_ref, kvp_ref, s_ref, step, nb):
    w = WINDOW
    kvs = [kvp_ref[...]] + [kvc_ref[b * w:(b + 1) * w, :] for b in range(nb)]
    key = lambda h: ((h // 2) // PAIRS_PER_KV, h % 2)
    qs, ks, vs, valids = [], [], [], []
    for b in range(nb):
        kv = jnp.concatenate([kvs[b], kvs[b + 1]], axis=0)
        kpad, vpad = _att_padded(kv[:, :128]), _att_padded(kv[:, 128:])
        pairs = [(q_ref[b * w:(b + 1) * w, j * 128:(j + 1) * 128] * ATT_SCALE).astype(BF16) for j in range(PAIRS)]
        qs += [pairs[h // 2] for h in range(A_HEADS)]
        ks += [kpad[key(h)] for h in range(A_HEADS)]
        vs += [vpad[key(h)] for h in range(A_HEADS)]
        valids += [_att_valid(step * nb + b)] * A_HEADS
    sink = jnp.stack([s_ref[:, h:h + 1] for h in range(A_HEADS)] * nb)
    return jnp.stack(qs), jnp.stack(ks), jnp.stack(vs), sink, jnp.stack(valids)


def _att_specs(nb):
    rows = nb * WINDOW
    return [pl.BlockSpec((rows, A_Q), lambda n: (n, 0)),
            pl.BlockSpec((rows, 2 * A_KV), lambda n: (n, COL_KV // (2 * A_KV))),
            pl.BlockSpec((WINDOW, 2 * A_KV), lambda n: (jnp.maximum(nb * n - 1, 0), COL_KV // (2 * A_KV))),
            pl.BlockSpec((1, 128), lambda n: (0, 0))]


def att_fwd(proj, sinks, *, name):
    t = proj.shape[0]
    nb = min(ATT_BLOCKS, t // WINDOW)
    rows = nb * WINDOW

    def body(q_ref, kvc_ref, kvp_ref, s_ref, o_ref):
        q, k, v, sink, valid = _att_operands(q_ref, kvc_ref, kvp_ref, s_ref, pl.program_id(0), nb)
        w, _ = _att_probs(q, k, sink, valid)
        o = _dot(w.astype(BF16), v, NN)
        for b in range(nb):
            for j in range(PAIRS):
                pair = o[b * A_HEADS + 2 * j] + o[b * A_HEADS + 2 * j + 1]
                o_ref[b * WINDOW:(b + 1) * WINDOW, j * 128:(j + 1) * 128] = pair.astype(BF16)

    return pl.pallas_call(
        body, name=name, grid=(t // rows,), in_specs=_att_specs(nb),
        out_specs=pl.BlockSpec((rows, A_Q), lambda n: (n, 0)),
        out_shape=jax.ShapeDtypeStruct((t, A_Q), BF16), compiler_params=_params(("parallel",)))(
            proj, proj, proj, sinks)


def att_bwd(proj, sinks, dout, *, name):
    t = proj.shape[0]
    nb = min(ATT_BLOCKS, t // WINDOW)
    rows = nb * WINDOW

    def body(q_ref, kvc_ref, kvp_ref, s_ref, do_ref, dq_ref, dkc_ref, dkp_ref, ds_ref):
        @pl.when(pl.program_id(0) == 0)
        def _():
            ds_ref[...] = jnp.zeros_like(ds_ref)

        q, k, v, sink, valid = _att_operands(q_ref, kvc_ref, kvp_ref, s_ref, pl.program_id(0), nb)
        dop = jnp.stack([do_ref[b * WINDOW:(b + 1) * WINDOW, (h // 2) * 128:(h // 2 + 1) * 128]
                         for b in range(nb) for h in range(A_HEADS)]).astype(BF16)
        w, w_sink = _att_probs(q, k, sink, valid)
        dw = _dot(dop, v, NT)
        delta = jnp.sum(w * dw, axis=-1, keepdims=True)
        dsc = (w * (dw - delta)).astype(BF16)
        dsink_h = -jnp.sum(w_sink * delta, axis=1, keepdims=True)
        dq = _dot(dsc, k, NN)
        dk_h = _dot(dsc, q, TN)
        dv_h = _dot(w.astype(BF16), dop, TN)
        lane = lax.broadcasted_iota(jnp.int32, (1, 128), 1)
        dsink = jnp.zeros((1, 128), F32)
        for b in range(nb):
            for h in range(A_HEADS):
                dsink = dsink + jnp.where(lane == h, dsink_h[b * A_HEADS + h], 0.0)
        ds_ref[...] += dsink
        lo = lax.broadcasted_iota(jnp.int32, (2 * WINDOW, 128), 1) < A_HEAD_DIM
        heads_per_kv = A_HEADS // A_KV_HEADS

        def tile(per_head, b):
            acc = {}
            for kvh in range(A_KV_HEADS):
                for half in range(2):
                    hs = range(kvh * heads_per_kv + half, (kvh + 1) * heads_per_kv, 2)
                    acc[(kvh, half)] = functools.reduce(lambda a, c: a + c, [per_head[b * A_HEADS + h] for h in hs])
            return jnp.where(lo, acc[(0, 0)] + pltpu.roll(acc[(0, 1)], A_HEAD_DIM, 1),
                             pltpu.roll(acc[(1, 0)], A_HEAD_DIM, 1) + acc[(1, 1)])

        for b in range(nb):
            blk = slice(b * WINDOW, (b + 1) * WINDOW)
            for j in range(PAIRS):
                pair = dq[b * A_HEADS + 2 * j] + dq[b * A_HEADS + 2 * j + 1]
                dq_ref[blk, j * 128:(j + 1) * 128] = (pair * ATT_SCALE).astype(BF16)
            dkv = jnp.concatenate([tile(dk_h, b), tile(dv_h, b)], axis=1)
            dkp_ref[blk, :] = dkv[:WINDOW]
            dkc_ref[blk, :] = dkv[WINDOW:]

    kvo = pl.BlockSpec((rows, 2 * A_KV), lambda n: (n, 0))
    return pl.pallas_call(
        body, name=name, grid=(t // rows,),
        in_specs=_att_specs(nb) + [pl.BlockSpec((rows, A_Q), lambda n: (n, 0))],
        out_specs=[pl.BlockSpec((rows, A_Q), lambda n: (n, 0)), kvo, kvo, pl.BlockSpec((1, 128), lambda n: (0, 0))],
        out_shape=[jax.ShapeDtypeStruct((t, A_Q), BF16), jax.ShapeDtypeStruct((t, 2 * A_KV), F32),
                   jax.ShapeDtypeStruct((t, 2 * A_KV), F32), jax.ShapeDtypeStruct((1, 128), F32)],
        compiler_params=_params(("arbitrary",)))(proj, proj, proj, sinks, dout)


QK_SCALE = B_HEAD_DIM ** -0.5
PREP_COLS = 256
PREP_NCB = 3 * B_W // PREP_COLS
HALO = 8
PREP_ROWS = 512


def _roll_rows(x, shift):
    n = x.shape[0]
    return x if shift % n == 0 else pltpu.roll(x, shift % n, 0)


def _conv_taps(xe, w):
    xs = [_roll_rows(xe, CONV_K - 1 - i) for i in range(CONV_K)]
    c = w[0:1] * xs[0]
    for i in range(1, CONV_K):
        c = c + w[i:i + 1] * xs[i]
    return xs, c


def dprep_fwd(proj, conv_w, *, name):
    t = proj.shape[0]
    tt = min(PREP_ROWS, t)
    col0 = COL_QKVB // PREP_COLS

    def body(x_ref, h_ref, w_ref, o_ref):
        cb, n = pl.program_id(0), pl.program_id(1)
        halo = jnp.where(n > 0, h_ref[...], 0.0)
        xe = jnp.concatenate([halo, x_ref[...]], axis=0)
        _, c = _conv_taps(xe, w_ref[...])
        y = _silu(c)[HALO:]
        parts = []
        for hh in range(PREP_COLS // B_HEAD_DIM):
            yh = y[:, hh * B_HEAD_DIM:(hh + 1) * B_HEAD_DIM]
            parts.append(yh * lax.rsqrt(jnp.sum(yh * yh, axis=-1, keepdims=True) + EPS))
        nrm = jnp.concatenate(parts, axis=-1)
        o_ref[...] = jnp.where(cb < 4, nrm * QK_SCALE, jnp.where(cb < 8, nrm, y))

    return pl.pallas_call(
        body, name=name, grid=(PREP_NCB, t // tt),
        in_specs=[pl.BlockSpec((tt, PREP_COLS), lambda cb, n: (n, col0 + cb)),
                  pl.BlockSpec((HALO, PREP_COLS), lambda cb, n: (jnp.maximum(n * (tt // HALO) - 1, 0), col0 + cb)),
                  pl.BlockSpec((CONV_K, PREP_COLS), lambda cb, n: (0, cb))],
        out_specs=pl.BlockSpec((tt, PREP_COLS), lambda cb, n: (n, cb)),
        out_shape=jax.ShapeDtypeStruct((t, 3 * B_W), F32), compiler_params=_params(("parallel", "parallel")))(
            proj, proj, conv_w)


def dprep_bwd(proj, conv_w, dqkvn, *, name):
    t = proj.shape[0]
    tt = min(PREP_ROWS, t)
    nb = t // tt
    col0 = COL_QKVB // PREP_COLS
    n8 = t // HALO

    def body(xc_ref, xb_ref, xa_ref, dc_ref, da_ref, w_ref, dx_ref, dw_ref):
        cb, n = pl.program_id(0), pl.program_id(1)

        @pl.when(n == 0)
        def _():
            dw_ref[...] = jnp.zeros_like(dw_ref)

        w = w_ref[...]
        xe = jnp.concatenate([jnp.where(n > 0, xb_ref[...], 0.0), xc_ref[...], xa_ref[...]], axis=0)
        xs, c = _conv_taps(xe, w)
        sg = _sigmoid(c)
        y = c * sg
        dout = jnp.concatenate([jnp.zeros((HALO, PREP_COLS), F32), dc_ref[...],
                                jnp.where(n < nb - 1, da_ref[...], 0.0)], axis=0)
        dsc = jnp.where(cb < 4, QK_SCALE, 1.0)
        parts = []
        for hh in range(PREP_COLS // B_HEAD_DIM):
            sl = slice(hh * B_HEAD_DIM, (hh + 1) * B_HEAD_DIM)
            yh, doh = y[:, sl], dout[:, sl] * dsc
            r = lax.rsqrt(jnp.sum(yh * yh, axis=-1, keepdims=True) + EPS)
            parts.append(doh * r - yh * (r * r * r) * jnp.sum(doh * yh, axis=-1, keepdims=True))
        dy = jnp.where(cb < 8, jnp.concatenate(parts, axis=-1), dout)
        dcv = dy * sg * (1.0 + c * (1.0 - sg))
        dxe = w[CONV_K - 1:CONV_K] * dcv
        for i in range(CONV_K - 1):
            dxe = dxe + w[i:i + 1] * _roll_rows(dcv, -(CONV_K - 1 - i))
        dx_ref[...] = dxe[HALO:HALO + tt].astype(BF16)
        for i in range(CONV_K):
            dw_ref[i:i + 1, :] += jnp.sum((dcv * xs[i])[HALO:HALO + tt], axis=0, keepdims=True)

    def after(n):
        return jnp.minimum((n + 1) * (tt // HALO), n8 - 1)

    return pl.pallas_call(
        body, name=name, grid=(PREP_NCB, nb),
        in_specs=[pl.BlockSpec((tt, PREP_COLS), lambda cb, n: (n, col0 + cb)),
                  pl.BlockSpec((HALO, PREP_COLS), lambda cb, n: (jnp.maximum(n * (tt // HALO) - 1, 0), col0 + cb)),
                  pl.BlockSpec((HALO, PREP_COLS), lambda cb, n: (after(n), col0 + cb)),
                  pl.BlockSpec((tt, PREP_COLS), lambda cb, n: (n, cb)),
                  pl.BlockSpec((HALO, PREP_COLS), lambda cb, n: (after(n), cb)),
                  pl.BlockSpec((CONV_K, PREP_COLS), lambda cb, n: (0, cb))],
        out_specs=[pl.BlockSpec((tt, PREP_COLS), lambda cb, n: (n, cb)),
                   pl.BlockSpec((CONV_K, PREP_COLS), lambda cb, n: (0, cb))],
        out_shape=[jax.ShapeDtypeStruct((t, 3 * B_W), BF16), jax.ShapeDtypeStruct((CONV_K, 3 * B_W), F32)],
        compiler_params=_params(("parallel", "arbitrary")))(proj, proj, proj, dqkvn, dqkvn, conv_w)


def _softplus(z):
    return jnp.maximum(z, 0.0) + jnp.log(1.0 + jnp.exp(-jnp.abs(z)))


def gates_fwd(proj, alog_pad, dtb_pad, *, name):
    t = proj.shape[0]

    def body(x_ref, a_ref, b_ref, o_ref):
        raw = x_ref[...]
        lane = lax.broadcasted_iota(jnp.int32, raw.shape, 1)
        g = -jnp.exp(a_ref[...]) * _softplus(raw + b_ref[...])
        o_ref[...] = jnp.where(lane < B_HEADS, _sigmoid(raw), jnp.where(lane < 2 * B_HEADS, g, 0.0))

    vec = pl.BlockSpec((1, 128), lambda n: (0, 0))
    return pl.pallas_call(
        body, name=name, grid=(t // ROWS,),
        in_specs=[pl.BlockSpec((ROWS, 128), lambda n: (n, COL_GATE // 128)), vec, vec],
        out_specs=pl.BlockSpec((ROWS, 128), lambda n: (n, 0)),
        out_shape=jax.ShapeDtypeStruct((t, 128), F32), compiler_params=_params(("parallel",)))(
            proj, alog_pad, dtb_pad)


def gates_bwd(proj, alog_pad, dtb_pad, dgates, *, name):
    t = proj.shape[0]

    def body(x_ref, a_ref, b_ref, dg_ref, dx_ref, da_ref, db_ref):
        @pl.when(pl.program_id(0) == 0)
        def _():
            da_ref[...] = jnp.zeros_like(da_ref)
            db_ref[...] = jnp.zeros_like(db_ref)

        raw, dgt = x_ref[...], dg_ref[...]
        lane = lax.broadcasted_iota(jnp.int32, raw.shape, 1)
        is_beta, is_g = lane < B_HEADS, (lane >= B_HEADS) & (lane < 2 * B_HEADS)
        beta = _sigmoid(raw)
        z = raw + b_ref[...]
        neg_a = -jnp.exp(a_ref[...])
        d_z = jnp.where(is_g, dgt * neg_a * _sigmoid(z), 0.0)
        dx_ref[...] = jnp.where(is_beta, dgt * beta * (1.0 - beta), d_z).astype(BF16)
        db_ref[...] += jnp.sum(d_z, axis=0, keepdims=True)
        da_ref[...] += jnp.sum(jnp.where(is_g, dgt * neg_a * _softplus(z), 0.0), axis=0, keepdims=True)

    vec = pl.BlockSpec((1, 128), lambda n: (0, 0))
    row = pl.BlockSpec((ROWS, 128), lambda n: (n, 0))
    return pl.pallas_call(
        body, name=name, grid=(t // ROWS,),
        in_specs=[pl.BlockSpec((ROWS, 128), lambda n: (n, COL_GATE // 128)), vec, vec, row],
        out_specs=[row, vec, vec],
        out_shape=[jax.ShapeDtypeStruct((t, 128), BF16), jax.ShapeDtypeStruct((1, 128), F32),
                   jax.ShapeDtypeStruct((1, 128), F32)],
        compiler_params=_params(("arbitrary",)))(proj, alog_pad, dtb_pad, dgates)


def _split2(a):
    hi = a.astype(BF16)
    return hi, (a - hi.astype(F32)).astype(BF16)


def _dotp(a, b, dims, passes):
    if passes == 1:
        return _dot(a.astype(BF16), b.astype(BF16), dims)
    ah, al = _split2(a)
    bh, bl = _split2(b)
    return _dot(ah, bh, dims) + (_dot(ah, bl, dims) + _dot(al, bh, dims))


_GRAD_DIMS = {NN: ((NT, False), (TN, False)), NT: ((NN, False), (TN, True)), TN: ((NT, True), (NN, False))}


def _make_mm(dims, passes, grad_passes):
    (da_dims, da_swap), (db_dims, db_swap) = _GRAD_DIMS[dims]

    @jax.custom_vjp
    def mm(a, b):
        return _dotp(a, b, dims, passes)

    def fwd(a, b):
        return _dotp(a, b, dims, passes), (a, b)

    def bwd(saved, ct):
        a, b = saved
        da = _dotp(b, ct, da_dims, grad_passes) if da_swap else _dotp(ct, b, da_dims, grad_passes)
        db = _dotp(ct, a, db_dims, grad_passes) if db_swap else _dotp(a, ct, db_dims, grad_passes)
        return da, db

    mm.defvjp(fwd, bwd)
    return mm


MM1 = {d: _make_mm(d, 1, 1) for d in (NN, NT, TN)}
MM3 = {d: _make_mm(d, 3, 1) for d in (NN, NT, TN)}


def _neumann_value(n):
    c = n.shape[-1]
    eye = (lax.broadcasted_iota(jnp.int32, (c, c), 0) == lax.broadcasted_iota(jnp.int32, (c, c), 1)).astype(F32)
    inv, pw = eye + n, n
    for _ in range(5):
        pw = _dotp(pw, pw, NN, 3)
        inv = inv + _dotp(inv, pw, NN, 3)
    return inv


@jax.custom_vjp
def _neumann_inverse(n):
    return _neumann_value(n)


def _neumann_fwd(n):
    inv = _neumann_value(n)
    return inv, inv


def _neumann_bwd(inv, ct):
    return (_dotp(_dotp(inv, ct, TN, 1), inv, NT, 1),)


_neumann_inverse.defvjp(_neumann_fwd, _neumann_bwd)


def _tri_ones(lower):
    r = lax.broadcasted_iota(jnp.int32, (DN_CHUNK, DN_CHUNK), 0)
    c = lax.broadcasted_iota(jnp.int32, (DN_CHUNK, DN_CHUNK), 1)
    return (r >= c if lower else r <= c).astype(BF16)


def _tri_sum(x, lower):
    tri = _tri_ones(lower)
    hi = x.astype(BF16)
    r1 = x - hi.astype(F32)
    mid = r1.astype(BF16)
    lo = (r1 - mid.astype(F32)).astype(BF16)
    return _dot(tri, hi, NN) + (_dot(tri, mid, NN) + _dot(tri, lo, NN))


def _delta_chunk(s0, q, k, v, beta, gam_c, gam_r):
    c = DN_CHUNK
    nh = s0.shape[0]
    r = lax.broadcasted_iota(jnp.int32, (c, c), 0)
    cc = lax.broadcasted_iota(jnp.int32, (c, c), 1)
    incl, strict = r >= cc, r > cc
    decay = jnp.exp(jnp.where(incl, gam_c - gam_r, NEG_INF))
    g_last = gam_c[:, c - 1:c, :]
    e_gam, e_rest, e_last = jnp.exp(gam_c), jnp.exp(g_last - gam_c), jnp.exp(g_last)
    a_neg = -jnp.where(strict, beta * MM1[NT](k, k) * decay, 0.0)
    inv = _neumann_inverse(a_neg)
    uw = MM3[NN](inv,jnp.concatenate([v * beta, k * (beta * e_gam)], axis=-1))
    u, w = uw[..., :B_HEAD_DIM], uw[..., B_HEAD_DIM:]
    qk = MM1[NT](q, k) * decay
    q_dec, k_rest = q * e_gam, k * e_rest
    state, outs = s0, []
    for g in range(q.shape[0] // nh):
        sl = slice(g * nh, (g + 1) * nh)
        v_new = u[sl] - MM1[NN](w[sl], state)
        outs.append(MM1[NN](q_dec[sl], state) + MM1[NN](qk[sl], v_new))
        state = state * e_last[sl] + MM1[TN](k_rest[sl], v_new)
    return state, jnp.concatenate(outs, axis=0)


DN_GROUP = 4


def _delta_operands(q_ref, k_ref, v_ref, g_ref, ng):
    c = DN_CHUNK
    qs, ks, vs, betas, gam_cs, gam_rs = [], [], [], [], [], []
    for g in range(ng):
        rows = slice(g * c, (g + 1) * c)
        gt = g_ref[rows, :]
        gam = _tri_sum(gt, True)
        gam_t = gam.T
        for h in range(B_HEADS):
            cols = slice(h * B_HEAD_DIM, (h + 1) * B_HEAD_DIM)
            qs.append(q_ref[rows, cols])
            ks.append(k_ref[rows, cols])
            vs.append(v_ref[rows, cols])
            betas.append(gt[:, h:h + 1])
            gam_cs.append(gam[:, B_HEADS + h:B_HEADS + h + 1])
            gam_rs.append(gam_t[B_HEADS + h:B_HEADS + h + 1, :])
    return tuple(jnp.stack(a) for a in (qs, ks, vs, betas, gam_cs, gam_rs))


def delta_fwd(qkvn, gates, *, name):
    t = qkvn.shape[0]
    ng = min(DN_GROUP, t // DN_CHUNK)
    rows = ng * DN_CHUNK
    nc = t // rows

    def body(q_ref, k_ref, v_ref, g_ref, o_ref, ss_ref, state):
        @pl.when(pl.program_id(0) == 0)
        def _():
            state[...] = jnp.zeros_like(state)

        s0 = state[...]
        ss_ref[...] = s0
        s1, o = _delta_chunk(s0, *_delta_operands(q_ref, k_ref, v_ref, g_ref, ng))
        state[...] = s1
        for g in range(ng):
            for h in range(B_HEADS):
                o_ref[g * DN_CHUNK:(g + 1) * DN_CHUNK, h * B_HEAD_DIM:(h + 1) * B_HEAD_DIM] = o[g * B_HEADS + h]

    blk = lambda j: pl.BlockSpec((rows, B_W), lambda n: (n, j))
    return pl.pallas_call(
        body, name=name, grid=(nc,),
        in_specs=[blk(0), blk(1), blk(2), pl.BlockSpec((rows, 128), lambda n: (n, 0))],
        out_specs=[blk(0), pl.BlockSpec((None, B_HEADS, B_HEAD_DIM, B_HEAD_DIM), lambda n: (n, 0, 0, 0))],
        out_shape=[jax.ShapeDtypeStruct((t, B_W), F32),
                   jax.ShapeDtypeStruct((nc, B_HEADS, B_HEAD_DIM, B_HEAD_DIM), F32)],
        scratch_shapes=[pltpu.VMEM((B_HEADS, B_HEAD_DIM, B_HEAD_DIM), F32)],
        compiler_params=_params(("arbitrary",)))(qkvn, qkvn, qkvn, gates)


def delta_bwd(qkvn, gates, ssave, do, *, name):
    t = qkvn.shape[0]
    ng = min(DN_GROUP, t // DN_CHUNK)
    rows = ng * DN_CHUNK
    nc = t // rows

    def body(q_ref, k_ref, v_ref, g_ref, ss_ref, do_ref, dx_ref, dg_ref, dstate):
        @pl.when(pl.program_id(0) == 0)
        def _():
            dstate[...] = jnp.zeros_like(dstate)

        lane = lax.broadcasted_iota(jnp.int32, (DN_CHUNK, 128), 1)
        row = lax.broadcasted_iota(jnp.int32, (128, DN_CHUNK), 0)
        _, vjp = jax.vjp(_delta_chunk, ss_ref[...], *_delta_operands(q_ref, k_ref, v_ref, g_ref, ng))
        do = jnp.stack([do_ref[g * DN_CHUNK:(g + 1) * DN_CHUNK, h * B_HEAD_DIM:(h + 1) * B_HEAD_DIM]
                        for g in range(ng) for h in range(B_HEADS)])
        ds0, dq, dk, dv, dbeta, dgam_c, dgam_r = vjp((dstate[...], do))
        dstate[...] = ds0
        for g in range(ng):
            blk = slice(g * DN_CHUNK, (g + 1) * DN_CHUNK)
            dbeta_all = jnp.zeros((DN_CHUNK, 128), F32)
            dgam_c_all = jnp.zeros((DN_CHUNK, 128), F32)
            dgam_r_all = jnp.zeros((128, DN_CHUNK), F32)
            for h in range(B_HEADS):
                e = g * B_HEADS + h
                dx_ref[blk, h * B_HEAD_DIM:(h + 1) * B_HEAD_DIM] = dq[e]
                dx_ref[blk, B_W + h * B_HEAD_DIM:B_W + (h + 1) * B_HEAD_DIM] = dk[e]
                dx_ref[blk, 2 * B_W + h * B_HEAD_DIM:2 * B_W + (h + 1) * B_HEAD_DIM] = dv[e]
                dbeta_all = dbeta_all + jnp.where(lane == h, dbeta[e], 0.0)
                dgam_c_all = dgam_c_all + jnp.where(lane == B_HEADS + h, dgam_c[e], 0.0)
                dgam_r_all = dgam_r_all + jnp.where(row == B_HEADS + h, dgam_r[e], 0.0)
            dg_ref[blk, :] = dbeta_all + _tri_sum(dgam_c_all + dgam_r_all.T, False)

    blk = lambda j: pl.BlockSpec((rows, B_W), lambda n: (nc - 1 - n, j))
    gsp = pl.BlockSpec((rows, 128), lambda n: (nc - 1 - n, 0))
    return pl.pallas_call(
        body, name=name, grid=(nc,),
        in_specs=[blk(0), blk(1), blk(2), gsp,
                  pl.BlockSpec((None, B_HEADS, B_HEAD_DIM, B_HEAD_DIM), lambda n: (nc - 1 - n, 0, 0, 0)), blk(0)],
        out_specs=[pl.BlockSpec((rows, 3 * B_W), lambda n: (nc - 1 - n, 0)), gsp],
        out_shape=[jax.ShapeDtypeStruct((t, 3 * B_W), F32), jax.ShapeDtypeStruct((t, 128), F32)],
        scratch_shapes=[pltpu.VMEM((B_HEADS, B_HEAD_DIM, B_HEAD_DIM), F32)],
        compiler_params=_params(("arbitrary",)))(qkvn, qkvn, qkvn, gates, ssave, do)


GNORM_ROWS = 1024


def gnorm_fwd(o, proj, onorm, *, name):
    t = o.shape[0]

    def body(o_ref, z_ref, w_ref, out_ref):
        ov = o_ref[...]
        r = lax.rsqrt(jnp.mean(ov * ov, axis=-1, keepdims=True) + EPS)
        out_ref[...] = (ov * r * w_ref[...] * _silu(z_ref[...])).astype(BF16)

    rows = min(GNORM_ROWS, t)
    blk = pl.BlockSpec((rows, B_HEAD_DIM), lambda n, h: (n, h))
    return pl.pallas_call(
        body, name=name, grid=(t // rows, B_HEADS),
        in_specs=[blk, pl.BlockSpec((rows, B_HEAD_DIM), lambda n, h: (n, COL_Z // B_HEAD_DIM + h)),
                  pl.BlockSpec((1, B_HEAD_DIM), lambda n, h: (0, 0))],
        out_specs=blk, out_shape=jax.ShapeDtypeStruct((t, B_W), BF16),
        compiler_params=_params(("parallel", "parallel")))(o, proj, onorm)


def gnorm_bwd(o, proj, onorm, dout, *, dcol0, name):
    t = o.shape[0]

    def body(o_ref, z_ref, w_ref, d_ref, do_ref, dz_ref, dw_ref):
        @pl.when((pl.program_id(0) == 0) & (pl.program_id(1) == 0))
        def _():
            dw_ref[...] = jnp.zeros_like(dw_ref)

        ov, zv, wv, dv = o_ref[...], z_ref[...], w_ref[...], d_ref[...].astype(F32)
        r = lax.rsqrt(jnp.mean(ov * ov, axis=-1, keepdims=True) + EPS)
        nrm = ov * r
        dz_ref[...] = (dv * nrm * wv * _dsilu(zv)).astype(BF16)
        da = dv * _silu(zv)
        dw_ref[...] += jnp.sum(da * nrm, axis=0, keepdims=True)
        dn = da * wv
        do_ref[...] = r * dn - ov * (r * r * r) * jnp.mean(dn * ov, axis=-1, keepdims=True)

    rows = min(GNORM_ROWS, t)
    blk = pl.BlockSpec((rows, B_HEAD_DIM), lambda n, h: (n, h))
    vec = pl.BlockSpec((1, B_HEAD_DIM), lambda n, h: (0, 0))
    return pl.pallas_call(
        body, name=name, grid=(t // rows, B_HEADS),
        in_specs=[blk, pl.BlockSpec((rows, B_HEAD_DIM), lambda n, h: (n, COL_Z // B_HEAD_DIM + h)), vec,
                  pl.BlockSpec((rows, B_HEAD_DIM), lambda n, h: (n, dcol0 // B_HEAD_DIM + h))],
        out_specs=[blk, blk, vec],
        out_shape=[jax.ShapeDtypeStruct((t, B_W), F32), jax.ShapeDtypeStruct((t, B_W), BF16),
                   jax.ShapeDtypeStruct((1, B_HEAD_DIM), F32)],
        compiler_params=_params(("arbitrary", "arbitrary")))(o, proj, onorm, dout)


def _ffn_fwd(h, norm_g, wg, wu, wd, tm, tag):
    hn = rms_fwd(h, norm_g, name=f"ffn{tag}_norm")
    gate, up, act = mm_gate_up(hn, wg, wu, tm=min(512, tm), tn=1408, tk=2048, name=f"ffn{tag}_gate_up")
    h_out = mm_nn(act, wd, tm=tm, tn=2048, tk=512, out_dtype=F32, res=h, name=f"ffn{tag}_down")
    return h_out, (hn, gate, up, act)


def _ffn_bwd(dh, h, norm_g, wg, wu, wd, saved, tm, tag, emit):
    hn, gate, up, act = saved
    dwd = mm_tn(act, dh, shards=1, tm=tm, tn=1024, tk=1408, out_dtype=BF16, name=f"ffn{tag}_dwd")[0]
    dgate, dup = mm_down_bwd(dh, wd, gate, up, tm=tm, tn=512, tk=2048, name=f"ffn{tag}_dact")
    dwg = mm_tn(hn, dgate, shards=N_SHARD, tm=tm, tn=1408, tk=1024, out_dtype=BF16, name=f"ffn{tag}_dwg")
    dwu = mm_tn(hn, dup, shards=N_SHARD, tm=tm, tn=1408, tk=1024, out_dtype=BF16, name=f"ffn{tag}_dwu")
    started = emit(f"ffn{tag}", {"gate": dwg, "up": dwu, "down": dwd})
    dhn = mm_nt(dgate, wg, tm=tm, tn=1024, tk=1408, out_dtype=F32, name=f"ffn{tag}_dhn_g")
    dh_in, dnorm = dgrad_rms_bwd(dup, wu, NT, h, norm_g + started, dh, tm=min(512, tm), tk=1408, res=dhn,
                                 name=f"ffn{tag}_dhn_u_dnorm")
    return dh_in, dnorm


def _local_step(x, target, w, get, emit):
    t = x.shape[0]
    tm = min(1024, t)
    g = {}

    hn0 = rms_fwd(x, w["even_norm"], name="l0_norm")
    w.update(get("even_in", hn0))
    proj = mm_nt(hn0, w["even_w_in"], tm=tm, tn=512, tk=2048, out_dtype=F32, name="l0_w_in")
    out_a = att_fwd(proj, w["sinks"], name="l0_att")
    qkvn = dprep_fwd(proj, w["even_conv"], name="l0_prep")
    gates = gates_fwd(proj, w["a_log"], w["dt_bias"], name="l0_gates")
    o_delta, ssave = delta_fwd(qkvn, gates, name="l0_delta")
    w.update(get("even_out", o_delta))
    out_b = gnorm_fwd(o_delta, proj, w["onorm"], name="l0_gnorm")
    mix0 = jnp.concatenate([out_a, out_b], axis=-1)
    h1 = mm_nn(mix0, w["even_w_out"], tm=tm, tn=1024, tk=2048, out_dtype=F32, res=x, name="l0_w_out")
    f0 = get("ffn0", h1)
    h2, ffn0 = _ffn_fwd(h1, w["ffn_norm"][0:1] + f0["tok"], f0["gate"], f0["up"], f0["down"], tm, 0)
    hn2 = rms_fwd(h2, w["odd_norm"], name="l1_norm")
    w.update(get("odd", hn2))
    zpre = mm_nn(hn2, w["odd_w_in"], tm=tm, tn=1024, tk=2048, out_dtype=F32, name="l1_w_in")
    gated = gmlp_fwd(zpre, w["odd_ln_g"], w["odd_ln_b"], w["odd_w_s"], w["odd_b_s"], name="l1_gmlp")
    h3 = mm_nn(gated, w["odd_w_out"], tm=tm, tn=1024, tk=2048, out_dtype=F32, res=h2, name="l1_w_out")
    f1 = get("ffn1", h3)
    h4, ffn1 = _ffn_fwd(h3, w["ffn_norm"][1:2] + f1["tok"], f1["gate"], f1["up"], f1["down"], tm, 1)
    loss, dh4, g["final_norm"] = loss_head(h4, w["final_norm"], target, name="loss_head")

    dh3, dn1 = _ffn_bwd(dh4, h3, w["ffn_norm"][1:2], f1["gate"], f1["up"], f1["down"], ffn1, tm, 1, emit)
    dw_out_o = mm_tn(gated, dh3, shards=1, tm=tm, tn=1024, tk=1024, out_dtype=BF16, name="l1_dw_out")[0]
    dgated = mm_nt(dh3, w["odd_w_out"], tm=tm, tn=1024, tk=2048, out_dtype=BF16, name="l1_dgated")
    dzpre, g["odd_w_s"], g["odd_b_s"], g["odd_ln_g"], g["odd_ln_b"] = gmlp_bwd(
        zpre, dgated, w["odd_ln_g"], w["odd_ln_b"], w["odd_w_s"], w["odd_b_s"], name="l1_dgmlp")
    dw_in_o = mm_tn(hn2, dzpre, shards=N_SHARD, tm=tm, tn=1024, tk=1024, out_dtype=BF16, name="l1_dw_in")
    started = emit("odd", {"odd_w_in": dw_in_o, "odd_w_out": dw_out_o})
    dh2, g["odd_norm"] = dgrad_rms_bwd(dzpre, w["odd_w_in"], NT, h2, w["odd_norm"] + started, dh3, tm=min(512, tm),
                                       tk=1024, name="l1_dhn_dnorm")
    dh1, dn0 = _ffn_bwd(dh2, h1, w["ffn_norm"][0:1], f0["gate"], f0["up"], f0["down"], ffn0, tm, 0, emit)
    g["ffn_norm"] = jnp.concatenate([dn0, dn1], axis=0)
    dw_out_e = mm_tn(mix0, dh1, shards=1, tm=tm, tn=1024, tk=1024, out_dtype=BF16, name="l0_dw_out")[0]
    started = emit("even_out", {"even_w_out": dw_out_e})
    dmix = mm_nt(dh1, w["even_w_out"], tm=tm, tn=1024, tk=2048, out_dtype=F32, name="l0_dmix")
    dq_a, dkv_cur, dkv_prev, g["sinks"] = att_bwd(proj, w["sinks"] + started, dmix, name="l0_datt")
    dkv = dkv_cur + jnp.concatenate([dkv_prev[WINDOW:], jnp.zeros((WINDOW, 2 * A_KV), F32)], axis=0)
    do_delta, dz, g["onorm"] = gnorm_bwd(o_delta, proj, w["onorm"], dmix, dcol0=A_Q, name="l0_dgnorm")
    dqkvn, dgates = delta_bwd(qkvn, gates, ssave, do_delta, name="l0_ddelta")
    dqkv_b, g["even_conv"] = dprep_bwd(proj, w["even_conv"], dqkvn, name="l0_dprep")
    draw, g["a_log"], g["dt_bias"] = gates_bwd(proj, w["a_log"], w["dt_bias"], dgates, name="l0_dgates")
    dproj = jnp.concatenate([dq_a, dkv.astype(BF16), dqkv_b, dz, draw,
                             jnp.zeros((t, EVEN_IN_PAD - COL_GATE - 128), BF16)], axis=-1)
    dw_in_e = mm_tn(dproj, hn0, shards=1, tm=tm, tn=1024, tk=1408, out_dtype=BF16, name="l0_dw_in")[0]
    grad_x, g["even_norm"] = dgrad_rms_bwd(dproj, w["even_w_in"], NN, x, w["even_norm"], dh1, tm=min(512, tm), tk=512,
                                           name="l0_dhn_dnorm")
    emit("even_in", {"even_w_in": dw_in_e, "small": g})
    return loss, grad_x


ANY = pl.BlockSpec(memory_space=pl.ANY)
N_DEV = 8


def _place():
    return lax.axis_index("x"), lax.axis_index("y"), lax.axis_index("c")


def _chip_peers(x, y, c):
    return [((1 - x, y, c), 2 * (1 - x) + y), ((x, 1 - y, c), 2 * x + 1 - y), ((1 - x, 1 - y, c), 2 * (1 - x) + 1 - y)]


HBM = pl.BlockSpec(memory_space=pltpu.HBM)
SEM = pl.BlockSpec(memory_space=pltpu.SEMAPHORE)
EFFECT = pltpu.SideEffectType.DATAFLOW_SIDE_EFFECTING
N_PEER = 3


def _half(ref, c):
    r, cols = ref.shape
    tile_rows = 32 // jnp.dtype(ref.dtype).itemsize
    if (r // 2) % tile_rows == 0:
        return ref.at[pl.ds(c * (r // 2), r // 2)]
    assert (cols // 2) % 128 == 0, ref.shape
    return ref.at[:, pl.ds(c * (cols // 2), cols // 2)]


def _gather_plan(srcs, lands, send, recv):
    x, y, c = _place()
    return [pltpu.make_async_remote_copy(src_ref=_half(srcs[i], c), dst_ref=_half(lands[i].at[2 * x + y], c),
                                         send_sem=send.at[N_PEER * i + k], recv_sem=recv.at[N_PEER * i + k],
                                         device_id=peer, device_id_type=MESH_ID)
            for i in range(len(srcs)) for k, (peer, _) in enumerate(_chip_peers(x, y, c))]


def _relay_plan(srcs, lands, send, recv):
    x, y, c = _place()
    return [pltpu.make_async_remote_copy(src_ref=_half(lands[i].at[idx], c), dst_ref=_half(lands[i].at[idx], c),
                                         send_sem=send.at[N_PEER * i + k], recv_sem=recv.at[N_PEER * i + k],
                                         device_id=(x, y, 1 - c), device_id_type=MESH_ID)
            for i in range(len(srcs)) for k, (_, idx) in enumerate(_chip_peers(x, y, c))]


def _scatter_plan(srcs, lands, send, recv):
    x, y, c = _place()
    return [pltpu.make_async_remote_copy(src_ref=srcs[i].at[idx], dst_ref=lands[i].at[k], send_sem=send.at[N_PEER * i + k],
                                         recv_sem=recv.at[N_PEER * i + k], device_id=peer, device_id_type=MESH_ID)
            for i in range(len(srcs)) for k, (peer, idx) in enumerate(_chip_peers(x, y, c))]


def _swap_plan(srcs, lands, send, recv):
    x, y, c = _place()
    return [pltpu.make_async_remote_copy(src_ref=srcs[i], dst_ref=lands[i], send_sem=send.at[N_PEER * i],
                                         recv_sem=recv.at[N_PEER * i], device_id=(x, y, 1 - c), device_id_type=MESH_ID)
            for i in range(len(srcs))]


def copies_start(plan, srcs, lands, after, *, name):
    n = len(srcs)
    both = list(srcs) + list(lands)

    def body(*refs):
        src_refs, land_refs = refs[:n], refs[n:2 * n]
        send, recv = refs[2 * n + 1], refs[2 * n + 2]
        for cp in plan(src_refs, land_refs, send, recv):
            cp.start()
        refs[-1][...] = jnp.zeros_like(refs[-1])

    res = pl.pallas_call(
        body, name=name,
        out_shape=(pltpu.SemaphoreType.DMA((n * N_PEER,)), pltpu.SemaphoreType.DMA((n * N_PEER,)),
                   *[pltpu.HBM(a.shape, a.dtype) for a in both], jax.ShapeDtypeStruct((8, 128), F32)),
        in_specs=[HBM] * (2 * n) + [ANY],
        out_specs=(SEM, SEM, *[HBM] * (2 * n), pl.BlockSpec(memory_space=pltpu.VMEM)),
        input_output_aliases={i: 2 + i for i in range(2 * n)},
        compiler_params=pltpu.CompilerParams(has_side_effects=EFFECT))(
            *[pltpu.with_memory_space_constraint(a, pltpu.HBM) for a in both], after)
    return {"send": res[0], "recv": res[1], "srcs": list(res[2:2 + n]), "lands": list(res[2 + n:2 + 2 * n]),
            "token": res[-1]}


def copies_relay(arrived_plan, next_plan, started, after, *, name):
    srcs, lands = started["srcs"], started["lands"]
    n = len(srcs)
    both = srcs + lands

    def body(*refs):
        src_refs, land_refs = refs[:n], refs[n:2 * n]
        send1, recv1 = refs[2 * n], refs[2 * n + 1]
        send2, recv2 = refs[2 * n + 3], refs[2 * n + 4]
        for cp in arrived_plan(src_refs, land_refs, send1, recv1):
            cp.wait_send()
            cp.wait_recv()
        for cp in next_plan(src_refs, land_refs, send2, recv2):
            cp.start()
        refs[-1][...] = jnp.zeros_like(refs[-1])

    res = pl.pallas_call(
        body, name=name,
        out_shape=(pltpu.SemaphoreType.DMA((n * N_PEER,)), pltpu.SemaphoreType.DMA((n * N_PEER,)),
                   *[pltpu.HBM(a.shape, a.dtype) for a in both], jax.ShapeDtypeStruct((8, 128), F32)),
        in_specs=[HBM] * (2 * n) + [SEM, SEM, ANY],
        out_specs=(SEM, SEM, *[HBM] * (2 * n), pl.BlockSpec(memory_space=pltpu.VMEM)),
        input_output_aliases={i: 2 + i for i in range(2 * n)},
        compiler_params=pltpu.CompilerParams(has_side_effects=EFFECT))(*both, started["send"], started["recv"], after)
    return {"send": res[0], "recv": res[1], "srcs": list(res[2:2 + n]), "lands": list(res[2 + n:2 + 2 * n]),
            "token": res[-1]}


def copies_wait(plan, started, after, *, name):
    srcs, lands = started["srcs"], started["lands"]
    n = len(srcs)
    both = srcs + lands

    def body(*refs):
        src_refs, land_refs = refs[:n], refs[n:2 * n]
        send, recv = refs[2 * n], refs[2 * n + 1]
        for cp in plan(src_refs, land_refs, send, recv):
            cp.wait_send()
            cp.wait_recv()

    res = pl.pallas_call(
        body, name=name, out_shape=tuple(pltpu.HBM(a.shape, a.dtype) for a in both),
        in_specs=[HBM] * (2 * n) + [SEM, SEM, ANY], out_specs=(HBM,) * (2 * n),
        input_output_aliases={i: i for i in range(2 * n)},
        compiler_params=pltpu.CompilerParams(has_side_effects=EFFECT))(*both, started["send"], started["recv"], after)
    return list(res[:n]), list(res[n:])


def allgather_small(small, *, name):
    def body(small_ref, out_ref, send, recv, loc):
        x, y, c = _place()
        dev = 4 * x + 2 * y + c
        local = pltpu.make_async_copy(small_ref, out_ref.at[dev], loc)
        remote = []
        for r in range(1, N_DEV):
            fx, fy, fc = (r >> 2) & 1, (r >> 1) & 1, r & 1
            peer = (1 - x if fx else x, 1 - y if fy else y, 1 - c if fc else c)
            remote.append(pltpu.make_async_remote_copy(
                src_ref=small_ref, dst_ref=out_ref.at[dev], send_sem=send.at[r - 1], recv_sem=recv.at[r - 1],
                device_id=peer, device_id_type=MESH_ID))
        local.start()
        for cp in remote:
            cp.start()
        for cp in remote:
            cp.wait()
        local.wait()

    return pl.pallas_call(
        body, name=name, in_specs=[ANY], out_specs=ANY,
        out_shape=jax.ShapeDtypeStruct((N_DEV,) + small.shape, small.dtype),
        scratch_shapes=[pltpu.SemaphoreType.DMA((N_DEV - 1,)), pltpu.SemaphoreType.DMA((N_DEV - 1,)),
                        pltpu.SemaphoreType.DMA(())])(small)


RED_ROWS = 256
RED_COLS = 256


def _red_block(r, c):
    if r % RED_ROWS == 0:
        return RED_ROWS, c
    if c > RED_COLS and c % RED_COLS == 0:
        return r, RED_COLS
    return r, c


def sum_chips(by_owner, me, got, *, name):
    _, r, c = by_owner.shape
    rb, cb = _red_block(r, c)

    def body(me_ref, o_ref, a_ref, b_ref, c_ref, out_ref):
        total = ((o_ref[...].astype(F32) + a_ref[...].astype(F32)) + b_ref[...].astype(F32)) + c_ref[...].astype(F32)
        out_ref[...] = total.astype(BF16)

    gk = lambda k: pl.BlockSpec((None, rb, cb), lambda i, j, me_ref: (k, i, j))
    grid_spec = pltpu.PrefetchScalarGridSpec(
        num_scalar_prefetch=1, grid=(r // rb, c // cb),
        in_specs=[pl.BlockSpec((None, rb, cb), lambda i, j, me_ref: (me_ref[0], i, j)), gk(0), gk(1), gk(2)],
        out_specs=pl.BlockSpec((rb, cb), lambda i, j, me_ref: (i, j)))
    return pl.pallas_call(
        body, name=name, grid_spec=grid_spec, out_shape=jax.ShapeDtypeStruct((r, c), BF16),
        compiler_params=_params(("parallel", "parallel")))(me, by_owner, got, got, got)


def sum_devices(small_all, *, name):
    _, p, c = small_all.shape

    def body(a_ref, out_ref):
        acc = a_ref[0]
        for d in range(1, N_DEV):
            acc = acc + a_ref[d]
        out_ref[...] = acc

    return pl.pallas_call(
        body, name=name, grid=(1,), in_specs=[pl.BlockSpec((N_DEV, p, c), lambda i: (0, 0, 0))],
        out_specs=pl.BlockSpec((p, c), lambda i: (0, 0)), out_shape=jax.ShapeDtypeStruct((p, c), F32),
        compiler_params=_params(("arbitrary",)))(small_all)


def adamw(parts, w, m, v, *, name):
    nl, r, c = w.shape
    assert len(parts) == nl
    npart = len(parts[0])
    rb, cb = _red_block(r, c)
    flat = [a for layer in parts for a in layer]

    def body(*refs):
        p_refs, (w_ref, m_ref, v_ref) = refs[:nl * npart], refs[nl * npart:nl * npart + 3]
        g_ref, d_ref, nm_ref, nv_ref = refs[nl * npart + 3:]
        layer = pl.program_id(0)
        grad = None
        for l in range(nl):
            gl = p_refs[l * npart][...].astype(F32)
            for j in range(1, npart):
                gl = gl + p_refs[l * npart + j][...].astype(F32)
            grad = gl if grad is None else jnp.where(layer == l, gl, grad)
        wv, mv, vv = w_ref[...], m_ref[...], v_ref[...]
        nm = ADAM_B1 * mv + (1.0 - ADAM_B1) * grad
        nv = ADAM_B2 * vv + (1.0 - ADAM_B2) * (grad * grad)
        m_hat = nm / (1.0 - ADAM_B1 ** ADAM_STEP)
        v_hat = nv / (1.0 - ADAM_B2 ** ADAM_STEP)
        g_ref[...] = grad
        d_ref[...] = -ADAM_LR * (m_hat / (jnp.sqrt(v_hat) + ADAM_EPS) + ADAM_WD * wv)
        nm_ref[...] = nm
        nv_ref[...] = nv

    pspec = pl.BlockSpec((rb, cb), lambda l, i, j: (i, j))
    wspec = pl.BlockSpec((None, rb, cb), lambda l, i, j: (l, i, j))
    osh = jax.ShapeDtypeStruct((nl, r, c), F32)
    return pl.pallas_call(
        body, name=name, grid=(nl, r // rb, c // cb), in_specs=[pspec] * (nl * npart) + [wspec] * 3,
        out_specs=[wspec] * 4, out_shape=[osh] * 4,
        compiler_params=_params(("parallel", "parallel", "parallel")))(*flat, w, m, v)


def _rows128(a):
    flat = a.reshape(-1)
    pad = (-flat.shape[0]) % 128
    return jnp.pad(flat, (0, pad)).reshape(-1, 128)


def _pack_rows(arrs, multiple=8):
    rows = jnp.concatenate([_rows128(a.astype(F32)) for a in arrs], axis=0)
    return jnp.pad(rows, ((0, (-rows.shape[0]) % multiple), (0, 0)))


def _unpack_rows(rows, shapes):
    out, r0 = [], 0
    for shp in shapes:
        size = 1
        for s in shp:
            size *= s
        nr = -(-size // 128)
        out.append(rows[r0:r0 + nr].reshape(-1)[:size].reshape(shp))
        r0 += nr
    return out


SMALL_LOCAL_GRADS = ["even_norm", "even_conv", "a_log", "dt_bias", "sinks", "onorm", "odd_norm", "odd_ln_g",
                     "odd_ln_b", "odd_w_s", "odd_b_s", "ffn_norm", "final_norm"]
BIG = ["even_w_in", "even_w_out", "odd_w_in", "odd_w_out", "ffn_w_gate", "ffn_w_up", "ffn_w_down"]
WEIGHTS = ["even_norm", "even_w_in", "even_conv", "even_a_log", "even_dt_bias", "even_sinks", "even_onorm",
           "even_w_out", "odd_norm", "odd_w_in", "odd_ln_g", "odd_ln_b", "odd_w_s", "odd_b_s", "odd_w_out",
           "ffn_norm", "ffn_w_gate", "ffn_w_up", "ffn_w_down", "final_norm"]
SMALL = [n for n in WEIGHTS if n not in BIG]


def kernel(x, even_norm, even_w_in, even_conv, even_a_log, even_dt_bias, even_sinks, even_onorm, even_w_out, odd_norm, odd_w_in, odd_ln_g, odd_ln_b, odd_w_s, odd_b_s, odd_w_out, ffn_norm, ffn_w_gate, ffn_w_up, ffn_w_down, final_norm, loss_target, m_even_norm, m_even_w_in, m_even_conv, m_even_a_log, m_even_dt_bias, m_even_sinks, m_even_onorm, m_even_w_out, m_odd_norm, m_odd_w_in, m_odd_ln_g, m_odd_ln_b, m_odd_w_s, m_odd_b_s, m_odd_w_out, m_ffn_norm, m_ffn_w_gate, m_ffn_w_up, m_ffn_w_down, m_final_norm, v_even_norm, v_even_w_in, v_even_conv, v_even_a_log, v_even_dt_bias, v_even_sinks, v_even_onorm, v_even_w_out, v_odd_norm, v_odd_w_in, v_odd_ln_g, v_odd_ln_b, v_odd_w_s, v_odd_b_s, v_odd_w_out, v_ffn_norm, v_ffn_w_gate, v_ffn_w_up, v_ffn_w_down, v_final_norm):
    args = dict(locals())
    wl = {n: args[n] for n in WEIGHTS}
    ml = {n: args["m_" + n] for n in WEIGHTS}
    vl = {n: args["v_" + n] for n in WEIGHTS}
    me = 2 * lax.axis_index("x") + lax.axis_index("y")

    def landing(a):
        return lax.dynamic_update_index_in_dim(lax.empty((N_SHARD,) + a.shape, a.dtype), a, me, 0)

    gather_groups = {
        "even_in": [even_w_in[0].T], "even_out": [even_w_out[0]],
        "ffn0": [ffn_w_gate[0], ffn_w_up[0], ffn_w_down[0]], "odd": [odd_w_in[0], odd_w_out[0]],
        "ffn1": [ffn_w_gate[1], ffn_w_up[1], ffn_w_down[1]],
    }
    gathering, after = {}, even_norm
    for group, arrs in gather_groups.items():
        srcs = [(a + after[0, 0] if gathering else a).astype(BF16) for a in arrs]
        if group == "even_in":
            srcs.append(_pack_rows([even_conv[0], odd_norm, odd_ln_g, odd_ln_b], multiple=16))
        gathering[group] = copies_start(_gather_plan, srcs, [landing(a) for a in srcs], after,
                                        name=f"gather_{group}_start")
        after = gathering[group]["token"]

    order = list(gather_groups)
    relayed, kept = {}, {}
    sinks_pad = jnp.pad(even_sinks, ((0, 0), (0, 128 - A_HEADS)))

    def relay(group, behind):
        relayed[group] = copies_relay(_gather_plan, _relay_plan, gathering[group], behind,
                                      name=f"gather_{group}_relay")
        return relayed[group]["token"][0:1, 0:1]

    def get(group, behind):
        if group not in relayed:
            relay(group, behind)
        _, lands = copies_wait(_relay_plan, relayed[group], behind, name=f"gather_{group}_wait")
        nxt = order.index(group) + 1
        tok = relay(order[nxt], lands[0]) if nxt < len(order) else jnp.zeros((1, 1), F32)
        if group == "even_in":
            parts = zip(*[_unpack_rows(lands[1][s], [(CONV_K, 768), (1, 512), (1, 512), (1, 512)])
                          for s in range(N_SHARD)])
            conv, onorm, lng, lnb = [jnp.concatenate(p, axis=1) for p in parts]
            w_in = jnp.pad(lands[0].reshape(EVEN_IN, D_MODEL), ((0, EVEN_IN_PAD - EVEN_IN), (0, 0)))
            kept["odd_ln_g"] = lng
            return {"even_w_in": w_in, "even_conv": conv + tok, "odd_norm": onorm, "odd_ln_b": lnb}
        if group == "even_out":
            return {"even_w_out": lands[0].reshape(D_MODEL, D_MODEL), "onorm": even_onorm + tok}
        if group == "odd":
            return {"odd_w_in": lands[0], "odd_w_out": lands[1].reshape(D_MODEL, D_MODEL),
                    "odd_ln_g": kept["odd_ln_g"] + tok}
        return {"gate": lands[0], "up": lands[1], "down": lands[2].reshape(D_FF, D_MODEL), "tok": tok}

    rows4 = lambda a: a.reshape(N_SHARD, a.shape[0] // N_SHARD, a.shape[1])
    scattering, small = {}, {}

    def emit(group, grads):
        behind = even_norm
        if group == "even_in":
            small["local"] = grads["small"]
            small["all"] = behind = allgather_small(_pack_rows([grads["small"][n] for n in SMALL_LOCAL_GRADS]),
                                                    name="allgather_small")
            srcs = [grads["even_w_in"][:EVEN_IN].reshape(N_SHARD, EVEN_IN // N_SHARD, D_MODEL)]
        elif group == "even_out":
            srcs = [rows4(grads["even_w_out"])]
        elif group == "odd":
            srcs = [grads["odd_w_in"], rows4(grads["odd_w_out"])]
        else:
            srcs = [grads["gate"], grads["up"], rows4(grads["down"])]
        lands = [lax.empty((N_PEER,) + a.shape[1:], a.dtype) for a in srcs]
        scattering[group] = copies_start(_scatter_plan, srcs, lands, behind, name=f"scatter_{group}_start")
        return scattering[group]["token"][0:1, 0:1]

    pad816 = lambda a: jnp.pad(a, ((0, 0), (B_HEADS, 128 - 2 * B_HEADS)))
    w = {
        "even_norm": even_norm + after[0:1, 0:1],
        "a_log": pad816(even_a_log), "dt_bias": pad816(even_dt_bias),
        "sinks": sinks_pad,
        "onorm": even_onorm,
        "odd_w_s": odd_w_s[0],
        "odd_b_s": jnp.pad(odd_b_s[0].T, ((0, 0), (0, 128 - C_GROUPS))),
        "ffn_norm": ffn_norm,
        "final_norm": final_norm[None],
    }
    loss_l, grad_x = _local_step(x[0], loss_target[0], w, get, emit)
    loss = lax.psum(loss_l[0, 0], ("x", "y", "c"))

    me1 = me.reshape(1).astype(jnp.int32)
    swapping = {}

    def reduce_chips(group, behind):
        srcs, lands = copies_wait(_scatter_plan, scattering[group], behind, name=f"scatter_{group}_wait")
        partial = [sum_chips(srcs[i], me1, lands[i], name=f"sum_chips_{group}_{i}") for i in range(len(srcs))]
        swapping[group] = copies_start(_swap_plan, partial, [lax.empty(p.shape, p.dtype) for p in partial],
                                       even_norm, name=f"swap_{group}_start")
        return swapping[group]["token"]

    def swapped(group, behind):
        mine, theirs = copies_wait(_swap_plan, swapping[group], behind, name=f"swap_{group}_wait")
        return list(zip(mine, theirs))

    behind = scattering["even_in"]["token"]
    for group in ("ffn1", "ffn0", "odd", "even_out"):
        behind = reduce_chips(group, behind)
    sums = {group: swapped(group, behind) for group in ("ffn1", "ffn0", "odd", "even_out")}
    outs = {}
    parts_of = {"even_w_out": [sums["even_out"][0]], "odd_w_in": [sums["odd"][0]], "odd_w_out": [sums["odd"][1]],
                "ffn_w_gate": [sums["ffn0"][0], sums["ffn1"][0]], "ffn_w_up": [sums["ffn0"][1], sums["ffn1"][1]],
                "ffn_w_down": [sums["ffn0"][2], sums["ffn1"][2]]}
    for n in parts_of:
        outs[n] = adamw(parts_of[n], wl[n], ml[n], vl[n], name=f"adamw_{n}")
    behind = reduce_chips("even_in", outs["ffn_w_down"][1])
    flip = lambda a: jnp.transpose(a, (0, 2, 1))
    outs["even_w_in"] = [flip(o) for o in adamw([swapped("even_in", behind)[0]], flip(wl["even_w_in"]),
                                                flip(ml["even_w_in"]), flip(vl["even_w_in"]),
                                                name="adamw_even_w_in")]

    g = small["local"]
    small_sum = sum_devices(small["all"], name="sum_devices")
    sg = dict(zip(SMALL_LOCAL_GRADS, _unpack_rows(small_sum, [g[n].shape for n in SMALL_LOCAL_GRADS])))
    own_cols = lambda a, width: lax.dynamic_slice_in_dim(a, me * width, width, axis=a.ndim - 1)
    small_grads = {
        "even_norm": sg["even_norm"], "even_conv": own_cols(sg["even_conv"], 768)[None],
        "even_a_log": sg["a_log"][:, B_HEADS:2 * B_HEADS], "even_dt_bias": sg["dt_bias"][:, B_HEADS:2 * B_HEADS],
        "even_sinks": sg["sinks"][:, :A_HEADS], "even_onorm": sg["onorm"],
        "odd_norm": own_cols(sg["odd_norm"], 512), "odd_ln_g": own_cols(sg["odd_ln_g"], 512),
        "odd_ln_b": own_cols(sg["odd_ln_b"], 512), "odd_w_s": sg["odd_w_s"][None],
        "odd_b_s": sg["odd_b_s"][:, :C_GROUPS].T[None], "ffn_norm": sg["ffn_norm"], "final_norm": sg["final_norm"][0],
    }
    packed = [_pack_rows([d[n] for n in SMALL])[None] for d in (small_grads, wl, ml, vl)]
    small_out = adamw([(packed[0][0],)], packed[1], packed[2], packed[3], name="adamw_small")
    shapes = [wl[n].shape for n in SMALL]
    for j in range(4):
        for n, a in zip(SMALL, _unpack_rows(small_out[j][0], shapes)):
            outs.setdefault(n, [None] * 4)[j] = a

    return (loss, grad_x[None], *[outs[n][0] for n in WEIGHTS], *[outs[n][1] for n in WEIGHTS],
            *[outs[n][2] for n in WEIGHTS], *[outs[n][3] for n in WEIGHTS])
```

```python
import functools

import jax
import jax.numpy as jnp
from jax import lax
from jax.experimental import pallas as pl
from jax.experimental.pallas import tpu as pltpu

F32 = jnp.float32
BF16 = jnp.bfloat16
NEG_INF = float("-inf")

D_MODEL = 2048
A_HEADS, A_KV_HEADS, A_HEAD_DIM, WINDOW = 16, 2, 64, 128
B_HEADS, B_HEAD_DIM, CONV_K, DN_CHUNK = 8, 128, 4, 64
C_GROUPS, C_CHUNK = 8, 128
C_GROUP_DIM = D_MODEL // C_GROUPS
D_FF = 5632
EPS = 1e-6
A_Q = A_HEADS * A_HEAD_DIM
A_KV = A_KV_HEADS * A_HEAD_DIM
B_W = B_HEADS * B_HEAD_DIM
EVEN_IN = A_Q + 2 * A_KV + 4 * B_W + 2 * B_HEADS
EVEN_IN_PAD = 5632
COL_KV = A_Q
COL_QKVB = A_Q + 2 * A_KV
COL_Z = COL_QKVB + 3 * B_W
COL_GATE = COL_Z + B_W
N_SHARD = 4

ADAM_LR, ADAM_B1, ADAM_B2, ADAM_EPS, ADAM_WD, ADAM_STEP = 0.001, 0.9, 0.999, 1e-08, 0.01, 10

VMEM_LIMIT_V7X = 56 * 1024 * 1024
MXU_COLS = 256
MESH_ID = pl.DeviceIdType.MESH


def _params(sem=None):
    return pltpu.CompilerParams(dimension_semantics=sem, vmem_limit_bytes=VMEM_LIMIT_V7X)


def _sigmoid(x):
    return 1.0 / (1.0 + jnp.exp(-x))


def _silu(x):
    return x * _sigmoid(x)


def _dsilu(x):
    s = _sigmoid(x)
    return s * (1.0 + x * (1.0 - s))


def _gelu(x):
    return 0.5 * x * (1.0 + lax.erf(x * 0.7071067811865476))


def _dgelu(x):
    return 0.5 * (1.0 + lax.erf(x * 0.7071067811865476)) + x * jnp.exp(-0.5 * x * x) * 0.3989422804014327


def _dot(a, b, dims):
    if a.ndim == 3:
        (ca,), (cb,) = dims
        return lax.dot_general(a, b, (((ca + 1,), (cb + 1,)), ((0,), (0,))), preferred_element_type=F32)
    return lax.dot_general(a, b, (dims, ((), ())), preferred_element_type=F32)


NN = ((1,), (0,))
NT = ((1,), (1,))
TN = ((0,), (0,))


def _as3(b):
    return b if b.ndim == 3 else b[None]


def _accumulate(step, nsteps, accs, products, finish):
    if nsteps == 1:
        finish(products())
        return

    @pl.when(step == 0)
    def _():
        for acc, p in zip(accs, products()):
            acc[...] = p

    if nsteps > 2:
        @pl.when((step > 0) & (step < nsteps - 1))
        def _():
            for acc, p in zip(accs, products()):
                acc[...] += p

    @pl.when(step == nsteps - 1)
    def _():
        finish(tuple(acc[...] + p for acc, p in zip(accs, products())))


def mm_nn(a, b, *, tm, tn, tk, out_dtype, name, res=None):
    b3 = _as3(b)
    m, k = a.shape
    s, k2, ns = b3.shape
    assert k2 == k and m % tm == 0 and ns % tn == 0 and k % tk == 0, (a.shape, b3.shape, tm, tn, tk)
    nps, nk = ns // tn, k // tk

    def body(*refs):
        if res is None:
            a_ref, b_ref, o_ref, acc = refs
        else:
            a_ref, b_ref, r_ref, o_ref, acc = refs
        def finish(tiles):
            r = tiles[0] if res is None else tiles[0] + r_ref[...].astype(F32)
            o_ref[...] = r.astype(out_dtype)

        _accumulate(pl.program_id(2), nk, (acc,),
                    lambda: (_dot(a_ref[...].astype(BF16), b_ref[...].astype(BF16), NN),), finish)

    in_specs = [pl.BlockSpec((tm, tk), lambda i, j, kk: (i, kk)),
                pl.BlockSpec((None, tk, tn), lambda i, j, kk: (j // nps, kk, j % nps))]
    args = [a, b3]
    if res is not None:
        in_specs.append(pl.BlockSpec((tm, tn), lambda i, j, kk: (i, j)))
        args.append(res)
    return pl.pallas_call(
        body, name=name, grid=(m // tm, s * nps, nk), in_specs=in_specs,
        out_specs=pl.BlockSpec((tm, tn), lambda i, j, kk: (i, j)),
        out_shape=jax.ShapeDtypeStruct((m, s * ns), out_dtype),
        scratch_shapes=[pltpu.VMEM((tm, tn), F32)],
        compiler_params=_params(("parallel", "parallel", "arbitrary")))(*args)


def mm_nt(a, b, *, tm, tn, tk, out_dtype, name, res=None):
    b3 = _as3(b)
    m, n = a.shape
    s, k, ns = b3.shape
    assert n == s * ns and m % tm == 0 and k % tn == 0 and ns % tk == 0, (a.shape, b3.shape, tm, tn, tk)
    rps = ns // tk
    nr = s * rps

    def body(*refs):
        if res is None:
            a_ref, b_ref, o_ref, acc = refs
        else:
            a_ref, b_ref, r_ref, o_ref, acc = refs
        def finish(tiles):
            r = tiles[0] if res is None else tiles[0] + r_ref[...].astype(F32)
            o_ref[...] = r.astype(out_dtype)

        _accumulate(pl.program_id(2), nr, (acc,),
                    lambda: (_dot(a_ref[...].astype(BF16), b_ref[...].astype(BF16), NT),), finish)

    in_specs = [pl.BlockSpec((tm, tk), lambda i, j, r: (i, r)),
                pl.BlockSpec((None, tn, tk), lambda i, j, r: (r // rps, j, r % rps))]
    args = [a, b3]
    if res is not None:
        in_specs.append(pl.BlockSpec((tm, tn), lambda i, j, r: (i, j)))
        args.append(res)
    return pl.pallas_call(
        body, name=name, grid=(m // tm, k // tn, nr), in_specs=in_specs,
        out_specs=pl.BlockSpec((tm, tn), lambda i, j, r: (i, j)),
        out_shape=jax.ShapeDtypeStruct((m, k), out_dtype),
        scratch_shapes=[pltpu.VMEM((tm, tn), F32)],
        compiler_params=_params(("parallel", "parallel", "arbitrary")))(*args)


def mm_tn(a, b, *, shards, tm, tn, tk, out_dtype, name):
    m, k = a.shape
    m2, n = b.shape
    ns = n // shards
    assert m2 == m and n == shards * ns and m % tm == 0 and k % tk == 0 and ns % tn == 0, (a.shape, b.shape)
    nps, nm = ns // tn, m // tm

    def body(a_ref, b_ref, o_ref, acc):
        def finish(tiles):
            o_ref[...] = tiles[0].astype(out_dtype)

        _accumulate(pl.program_id(2), nm, (acc,),
                    lambda: (_dot(a_ref[...].astype(BF16), b_ref[...].astype(BF16), TN),), finish)

    return pl.pallas_call(
        body, name=name, grid=(k // tk, shards * nps, nm),
        in_specs=[pl.BlockSpec((tm, tk), lambda i, j, mi: (mi, i)),
                  pl.BlockSpec((tm, tn), lambda i, j, mi: (mi, j))],
        out_specs=pl.BlockSpec((None, tk, tn), lambda i, j, mi: (j // nps, i, j % nps)),
        out_shape=jax.ShapeDtypeStruct((shards, k, ns), out_dtype),
        scratch_shapes=[pltpu.VMEM((tk, tn), F32)],
        compiler_params=_params(("parallel", "parallel", "arbitrary")))(a, b)


def mm_gate_up(hn, wg, wu, *, tm, tn, tk, name):
    wg3, wu3 = _as3(wg), _as3(wu)
    m, k = hn.shape
    s, _, ns = wg3.shape
    assert m % tm == 0 and ns % tn == 0 and k % tk == 0
    nps, nk = ns // tn, k // tk

    def body(a_ref, g_ref, u_ref, og_ref, ou_ref, oa_ref, accg, accu):
        def products():
            a = a_ref[...].astype(BF16)
            return _dot(a, g_ref[...].astype(BF16), NN), _dot(a, u_ref[...].astype(BF16), NN)

        def finish(tiles):
            g, u = tiles
            og_ref[...] = g.astype(BF16)
            ou_ref[...] = u.astype(BF16)
            oa_ref[...] = (_silu(g) * u).astype(BF16)

        _accumulate(pl.program_id(2), nk, (accg, accu), products, finish)

    wspec = pl.BlockSpec((None, tk, tn), lambda i, j, kk: (j // nps, kk, j % nps))
    ospec = pl.BlockSpec((tm, tn), lambda i, j, kk: (i, j))
    osh = jax.ShapeDtypeStruct((m, s * ns), BF16)
    return pl.pallas_call(
        body, name=name, grid=(m // tm, s * nps, nk),
        in_specs=[pl.BlockSpec((tm, tk), lambda i, j, kk: (i, kk)), wspec, wspec],
        out_specs=[ospec, ospec, ospec], out_shape=[osh, osh, osh],
        scratch_shapes=[pltpu.VMEM((tm, tn) if nk > 1 else (8, 128), F32)] * 2,
        compiler_params=_params(("parallel", "parallel", "arbitrary")))(hn, wg3, wu3)


def mm_down_bwd(dh, wd, gate, up, *, tm, tn, tk, name):
    m, d = dh.shape
    f, d2 = wd.shape
    assert d2 == d and m % tm == 0 and f % tn == 0 and tk == d and tn % MXU_COLS == 0

    def body(a_ref, b_ref, g_ref, u_ref, og_ref, ou_ref):
        a = a_ref[...].astype(BF16)
        for jj in range(tn // MXU_COLS):
            sl = slice(jj * MXU_COLS, (jj + 1) * MXU_COLS)
            da = _dot(a, b_ref[sl, :].astype(BF16), NT)
            g, u = g_ref[:, sl].astype(F32), u_ref[:, sl].astype(F32)
            s = _sigmoid(g)
            og_ref[:, sl] = (da * u * (s * (1.0 + g * (1.0 - s)))).astype(BF16)
            ou_ref[:, sl] = (da * (g * s)).astype(BF16)

    ospec = pl.BlockSpec((tm, tn), lambda i, j: (i, j))
    osh = jax.ShapeDtypeStruct((m, f), BF16)
    return pl.pallas_call(
        body, name=name, grid=(m // tm, f // tn),
        in_specs=[pl.BlockSpec((tm, tk), lambda i, j: (i, 0)),
                  pl.BlockSpec((tn, tk), lambda i, j: (j, 0)), ospec, ospec],
        out_specs=[ospec, ospec], out_shape=[osh, osh],
        compiler_params=_params(("parallel", "parallel")))(dh, wd, gate, up)


ROWS = 256


def rms_fwd(x, g, *, name):
    t, d = x.shape

    def body(x_ref, g_ref, o_ref):
        xv = x_ref[...]
        r = lax.rsqrt(jnp.mean(xv * xv, axis=-1, keepdims=True) + EPS)
        o_ref[...] = (xv * r * g_ref[...]).astype(BF16)

    return pl.pallas_call(
        body, name=name, grid=(t // ROWS,),
        in_specs=[pl.BlockSpec((ROWS, d), lambda i: (i, 0)), pl.BlockSpec((1, d), lambda i: (0, 0))],
        out_specs=pl.BlockSpec((ROWS, d), lambda i: (i, 0)),
        out_shape=jax.ShapeDtypeStruct((t, d), BF16), compiler_params=_params(("parallel",)))(x, g)


def dgrad_rms_bwd(a, b, form, x, g, dres, *, tm, tk, name, res=None):
    m, d = x.shape
    b3 = _as3(b)
    if form == NN:
        steps = a.shape[1] // tk
        a_spec = pl.BlockSpec((tm, tk), lambda i, r: (i, r))
        b_spec = pl.BlockSpec((None, tk, d), lambda i, r: (0, r, 0))
    else:
        s, d2, ns = b3.shape
        assert d2 == d and ns % tk == 0
        rps = ns // tk
        steps = s * rps
        a_spec = pl.BlockSpec((tm, tk), lambda i, r: (i, r))
        b_spec = pl.BlockSpec((None, d, tk), lambda i, r: (r // rps, 0, r % rps))
    assert m % tm == 0 and a.shape[1] == steps * tk

    def body(*refs):
        if res is None:
            a_ref, b_ref, x_ref, g_ref, dr_ref, dx_ref, dg_ref, acc = refs
        else:
            a_ref, b_ref, r_ref, x_ref, g_ref, dr_ref, dx_ref, dg_ref, acc = refs

        @pl.when((pl.program_id(0) == 0) & (pl.program_id(1) == 0))
        def _():
            dg_ref[...] = jnp.zeros_like(dg_ref)

        def finish(tiles):
            dyv = tiles[0] if res is None else tiles[0] + r_ref[...]
            xv = x_ref[...]
            r = lax.rsqrt(jnp.mean(xv * xv, axis=-1, keepdims=True) + EPS)
            dyg = dyv * g_ref[...]
            dx_ref[...] = r * dyg - xv * (r * r * r) * jnp.mean(dyg * xv, axis=-1, keepdims=True) + dr_ref[...]
            dg_ref[...] += jnp.sum(dyv * xv * r, axis=0, keepdims=True)

        _accumulate(pl.program_id(1), steps, (acc,),
                    lambda: (_dot(a_ref[...].astype(BF16), b_ref[...].astype(BF16), form),), finish)

    row = pl.BlockSpec((tm, d), lambda i, r: (i, 0))
    vec = pl.BlockSpec((1, d), lambda i, r: (0, 0))
    in_specs = [a_spec, b_spec] + ([row] if res is not None else []) + [row, vec, row]
    args = [a, b3] + ([res] if res is not None else []) + [x, g, dres]
    return pl.pallas_call(
        body, name=name, grid=(m // tm, steps), in_specs=in_specs, out_specs=[row, vec],
        out_shape=[jax.ShapeDtypeStruct((m, d), F32), jax.ShapeDtypeStruct((1, d), F32)],
        scratch_shapes=[pltpu.VMEM((tm, d), F32)],
        compiler_params=_params(("arbitrary", "arbitrary")))(*args)


def loss_head(h, g, target, *, name):
    t, d = h.shape

    def body(x_ref, g_ref, t_ref, loss_ref, dx_ref, dg_ref):
        @pl.when(pl.program_id(0) == 0)
        def _():
            dg_ref[...] = jnp.zeros_like(dg_ref)
            loss_ref[...] = jnp.zeros_like(loss_ref)

        xv, gv = x_ref[...], g_ref[...]
        r = lax.rsqrt(jnp.mean(xv * xv, axis=-1, keepdims=True) + EPS)
        e = xv * r * gv - t_ref[...]
        loss_ref[...] += 0.5 * jnp.sum(jnp.mean(e * e, axis=-1, keepdims=True), axis=0, keepdims=True)
        dyv = e * (1.0 / d)
        dyg = dyv * gv
        dx_ref[...] = r * dyg - xv * (r * r * r) * jnp.mean(dyg * xv, axis=-1, keepdims=True)
        dg_ref[...] += jnp.sum(dyv * xv * r, axis=0, keepdims=True)

    row = pl.BlockSpec((ROWS, d), lambda i: (i, 0))
    vec = pl.BlockSpec((1, d), lambda i: (0, 0))
    return pl.pallas_call(
        body, name=name, grid=(t // ROWS,), in_specs=[row, vec, row],
        out_specs=[pl.BlockSpec((1, 128), lambda i: (0, 0)), row, vec],
        out_shape=[jax.ShapeDtypeStruct((1, 128), F32), jax.ShapeDtypeStruct((t, d), F32),
                   jax.ShapeDtypeStruct((1, d), F32)],
        compiler_params=_params(("arbitrary",)))(h, g, target)


def _tril_mask():
    r = lax.broadcasted_iota(jnp.int32, (C_CHUNK, C_CHUNK), 0)
    c = lax.broadcasted_iota(jnp.int32, (C_CHUNK, C_CHUNK), 1)
    return r >= c


def _layer_norm_parts(v):
    mu = jnp.mean(v, axis=-1, keepdims=True)
    vc = v - mu
    rstd = lax.rsqrt(jnp.mean(vc * vc, axis=-1, keepdims=True) + EPS)
    return vc * rstd, rstd


def gmlp_fwd(zpre, ln_g, ln_b, ws, bs_t, *, name):
    t = zpre.shape[0]
    d = D_MODEL

    def body(zu_ref, zv_ref, g_ref, b_ref, ws_ref, bs_ref, o_ref):
        u = _gelu(zu_ref[...])
        vhat, _ = _layer_norm_parts(_gelu(zv_ref[...]))
        vln = (vhat * g_ref[...] + b_ref[...]).astype(BF16)
        mask = _tril_mask()
        for gi in range(C_GROUPS):
            sl = slice(gi * C_GROUP_DIM, (gi + 1) * C_GROUP_DIM)
            w = jnp.where(mask, ws_ref[gi], 0.0).astype(BF16)
            mixed = _dot(w, vln[:, sl], NN) + bs_ref[:, gi:gi + 1]
            o_ref[:, sl] = (u[:, sl] * mixed).astype(BF16)

    vec = pl.BlockSpec((1, d), lambda i: (0, 0))
    return pl.pallas_call(
        body, name=name, grid=(t // C_CHUNK,),
        in_specs=[pl.BlockSpec((C_CHUNK, d), lambda i: (i, 0)), pl.BlockSpec((C_CHUNK, d), lambda i: (i, 1)),
                  vec, vec, pl.BlockSpec((C_GROUPS, C_CHUNK, C_CHUNK), lambda i: (0, 0, 0)),
                  pl.BlockSpec((C_CHUNK, 128), lambda i: (0, 0))],
        out_specs=pl.BlockSpec((C_CHUNK, d), lambda i: (i, 0)),
        out_shape=jax.ShapeDtypeStruct((t, d), BF16), compiler_params=_params(("parallel",)))(
            zpre, zpre, ln_g, ln_b, ws, bs_t)


def gmlp_bwd(zpre, dgated, ln_g, ln_b, ws, bs_t, *, name):
    t = zpre.shape[0]
    d = D_MODEL

    def body(zu_ref, zv_ref, dg_ref, g_ref, b_ref, ws_ref, bs_ref, dz_ref, dws_ref, dbs_ref, dlg_ref, dlb_ref):
        @pl.when(pl.program_id(0) == 0)
        def _():
            dws_ref[...] = jnp.zeros_like(dws_ref)
            dbs_ref[...] = jnp.zeros_like(dbs_ref)
            dlg_ref[...] = jnp.zeros_like(dlg_ref)
            dlb_ref[...] = jnp.zeros_like(dlb_ref)

        zu, zv = zu_ref[...], zv_ref[...]
        u = _gelu(zu)
        vhat, rstd = _layer_norm_parts(_gelu(zv))
        gam = g_ref[...]
        vln = (vhat * gam + b_ref[...]).astype(BF16)
        dgt = dg_ref[...].astype(F32)
        mask = _tril_mask()
        lane = lax.broadcasted_iota(jnp.int32, (C_CHUNK, 128), 1)
        dbs = jnp.zeros((C_CHUNK, 128), F32)
        du_parts, dvln_parts = [], []
        for gi in range(C_GROUPS):
            sl = slice(gi * C_GROUP_DIM, (gi + 1) * C_GROUP_DIM)
            w = jnp.where(mask, ws_ref[gi], 0.0).astype(BF16)
            mixed = _dot(w, vln[:, sl], NN) + bs_ref[:, gi:gi + 1]
            du_parts.append(dgt[:, sl] * mixed)
            dmixed = dgt[:, sl] * u[:, sl]
            dmb = dmixed.astype(BF16)
            dws_ref[gi] += jnp.where(mask, _dot(dmb, vln[:, sl], NT), 0.0)
            dbs = dbs + jnp.where(lane == gi, jnp.sum(dmixed, axis=-1, keepdims=True), 0.0)
            dvln_parts.append(_dot(w, dmb, TN))
        dbs_ref[...] += dbs
        du = jnp.concatenate(du_parts, axis=-1)
        dvln = jnp.concatenate(dvln_parts, axis=-1)
        dlg_ref[...] += jnp.sum(dvln * vhat, axis=0, keepdims=True)
        dlb_ref[...] += jnp.sum(dvln, axis=0, keepdims=True)
        dvhat = dvln * gam
        dv = rstd * (dvhat - jnp.mean(dvhat, axis=-1, keepdims=True)
                     - vhat * jnp.mean(dvhat * vhat, axis=-1, keepdims=True))
        dz_ref[:, :d] = (du * _dgelu(zu)).astype(BF16)
        dz_ref[:, d:] = (dv * _dgelu(zv)).astype(BF16)

    vec = pl.BlockSpec((1, d), lambda i: (0, 0))
    wsp = pl.BlockSpec((C_GROUPS, C_CHUNK, C_CHUNK), lambda i: (0, 0, 0))
    bsp = pl.BlockSpec((C_CHUNK, 128), lambda i: (0, 0))
    return pl.pallas_call(
        body, name=name, grid=(t // C_CHUNK,),
        in_specs=[pl.BlockSpec((C_CHUNK, d), lambda i: (i, 0)), pl.BlockSpec((C_CHUNK, d), lambda i: (i, 1)),
                  pl.BlockSpec((C_CHUNK, d), lambda i: (i, 0)), vec, vec, wsp, bsp],
        out_specs=[pl.BlockSpec((C_CHUNK, 2 * d), lambda i: (i, 0)), wsp, bsp, vec, vec],
        out_shape=[jax.ShapeDtypeStruct((t, 2 * d), BF16), jax.ShapeDtypeStruct((C_GROUPS, C_CHUNK, C_CHUNK), F32),
                   jax.ShapeDtypeStruct((C_CHUNK, 128), F32), jax.ShapeDtypeStruct((1, d), F32),
                   jax.ShapeDtypeStruct((1, d), F32)],
        compiler_params=_params(("arbitrary",)))(zpre, zpre, dgated, ln_g, ln_b, ws, bs_t)


ATT_SCALE = A_HEAD_DIM ** -0.5
PAIRS = A_HEADS // 2
PAIRS_PER_KV = PAIRS // A_KV_HEADS


def _att_padded(tile):
    lo = lax.broadcasted_iota(jnp.int32, tile.shape, 1) < A_HEAD_DIM
    rolled = pltpu.roll(tile, A_HEAD_DIM, 1)
    zero = jnp.zeros_like(tile)
    return {(0, 0): jnp.where(lo, tile, zero).astype(BF16), (0, 1): jnp.where(lo, zero, rolled).astype(BF16),
            (1, 0): jnp.where(lo, rolled, zero).astype(BF16), (1, 1): jnp.where(lo, zero, tile).astype(BF16)}


def _att_valid(n):
    r = lax.broadcasted_iota(jnp.int32, (WINDOW, 2 * WINDOW), 0)
    c = lax.broadcasted_iota(jnp.int32, (WINDOW, 2 * WINDOW), 1)
    rel = r + WINDOW - c
    return (rel >= 0) & (rel < WINDOW) & ((c >= WINDOW) | (n > 0))


def _att_probs(qp, kpad, sink, valid):
    s = jnp.where(valid, _dot(qp, kpad, NT), NEG_INF)
    m = jnp.maximum(jnp.max(s, axis=-1, keepdims=True), sink)
    p = jnp.exp(s - m)
    e_sink = jnp.exp(sink - m)
    inv = 1.0 / (jnp.sum(p, axis=-1, keepdims=True) + e_sink)
    return p * inv, e_sink * inv


ATT_BLOCKS = 4


def _att_operands(q_ref, kvc_ref, kvp_ref, s_ref, step, nb):
    w = WINDOW
    kvs = [kvp_ref[...]] + [kvc_ref[b * w:(b + 1) * w, :] for b in range(nb)]
    key = lambda h: ((h // 2) // PAIRS_PER_KV, h % 2)
    qs, ks, vs, valids = [], [], [], []
    for b in range(nb):
        kv = jnp.concatenate([kvs[b], kvs[b + 1]], axis=0)
        kpad, vpad = _att_padded(kv[:, :128]), _att_padded(kv[:, 128:])
        pairs = [(q_ref[b * w:(b + 1) * w, j * 128:(j + 1) * 128] * ATT_SCALE).astype(BF16) for j in range(PAIRS)]
        qs += [pairs[h // 2] for h in range(A_HEADS)]
        ks += [kpad[key(h)] for h in range(A_HEADS)]
        vs += [vpad[key(h)] for h in range(A_HEADS)]
        valids += [_att_valid(step * nb + b)] * A_HEADS
    sink = jnp.stack([s_ref[:, h:h + 1] for h in range(A_HEADS)] * nb)
    return jnp.stack(qs), jnp.stack(ks), jnp.stack(vs), sink, jnp.stack(valids)


def _att_specs(nb):
    rows = nb * WINDOW
    return [pl.BlockSpec((rows, A_Q), lambda n: (n, 0)),
            pl.BlockSpec((rows, 2 * A_KV), lambda n: (n, COL_KV // (2 * A_KV))),
            pl.BlockSpec((WINDOW, 2 * A_KV), lambda n: (jnp.maximum(nb * n - 1, 0), COL_KV // (2 * A_KV))),
            pl.BlockSpec((1, 128), lambda n: (0, 0))]


def att_fwd(proj, sinks, *, name):
    t = proj.shape[0]
    nb = min(ATT_BLOCKS, t // WINDOW)
    rows = nb * WINDOW

    def body(q_ref, kvc_ref, kvp_ref, s_ref, o_ref):
        q, k, v, sink, valid = _att_operands(q_ref, kvc_ref, kvp_ref, s_ref, pl.program_id(0), nb)
        w, _ = _att_probs(q, k, sink, valid)
        o = _dot(w.astype(BF16), v, NN)
        for b in range(nb):
            for j in range(PAIRS):
                pair = o[b * A_HEADS + 2 * j] + o[b * A_HEADS + 2 * j + 1]
                o_ref[b * WINDOW:(b + 1) * WINDOW, j * 128:(j + 1) * 128] = pair.astype(BF16)

    return pl.pallas_call(
        body, name=name, grid=(t // rows,), in_specs=_att_specs(nb),
        out_specs=pl.BlockSpec((rows, A_Q), lambda n: (n, 0)),
        out_shape=jax.ShapeDtypeStruct((t, A_Q), BF16), compiler_params=_params(("parallel",)))(
            proj, proj, proj, sinks)


def att_bwd(proj, sinks, dout, *, name):
    t = proj.shape[0]
    nb = min(ATT_BLOCKS, t // WINDOW)
    rows = nb * WINDOW

    def body(q_ref, kvc_ref, kvp_ref, s_ref, do_ref, dq_ref, dkc_ref, dkp_ref, ds_ref):
        @pl.when(pl.program_id(0) == 0)
        def _():
            ds_ref[...] = jnp.zeros_like(ds_ref)

        q, k, v, sink, valid = _att_operands(q_ref, kvc_ref, kvp_ref, s_ref, pl.program_id(0), nb)
        dop = jnp.stack([do_ref[b * WINDOW:(b + 1) * WINDOW, (h // 2) * 128:(h // 2 + 1) * 128]
                         for b in range(nb) for h in range(A_HEADS)]).astype(BF16)
        w, w_sink = _att_probs(q, k, sink, valid)
        dw = _dot(dop, v, NT)
        delta = jnp.sum(w * dw, axis=-1, keepdims=True)
        dsc = (w * (dw - delta)).astype(BF16)
        dsink_h = -jnp.sum(w_sink * delta, axis=1, keepdims=True)
        dq = _dot(dsc, k, NN)
        dk_h = _dot(dsc, q, TN)
        dv_h = _dot(w.astype(BF16), dop, TN)
        lane = lax.broadcasted_iota(jnp.int32, (1, 128), 1)
        dsink = jnp.zeros((1, 128), F32)
        for b in range(nb):
            for h in range(A_HEADS):
                dsink = dsink + jnp.where(lane == h, dsink_h[b * A_HEADS + h], 0.0)
        ds_ref[...] += dsink
        lo = lax.broadcasted_iota(jnp.int32, (2 * WINDOW, 128), 1) < A_HEAD_DIM
        heads_per_kv = A_HEADS // A_KV_HEADS

        def tile(per_head, b):
            acc = {}
            for kvh in range(A_KV_HEADS):
                for half in range(2):
                    hs = range(kvh * heads_per_kv + half, (kvh + 1) * heads_per_kv, 2)
                    acc[(kvh, half)] = functools.reduce(lambda a, c: a + c, [per_head[b * A_HEADS + h] for h in hs])
            return jnp.where(lo, acc[(0, 0)] + pltpu.roll(acc[(0, 1)], A_HEAD_DIM, 1),
                             pltpu.roll(acc[(1, 0)], A_HEAD_DIM, 1) + acc[(1, 1)])

        for b in range(nb):
            blk = slice(b * WINDOW, (b + 1) * WINDOW)
            for j in range(PAIRS):
                pair = dq[b * A_HEADS + 2 * j] + dq[b * A_HEADS + 2 * j + 1]
                dq_ref[blk, j * 128:(j + 1) * 128] = (pair * ATT_SCALE).astype(BF16)
            dkv = jnp.concatenate([tile(dk_h, b), tile(dv_h, b)], axis=1)
            dkp_ref[blk, :] = dkv[:WINDOW]
            dkc_ref[blk, :] = dkv[WINDOW:]

    kvo = pl.BlockSpec((rows, 2 * A_KV), lambda n: (n, 0))
    return pl.pallas_call(
        body, name=name, grid=(t // rows,),
        in_specs=_att_specs(nb) + [pl.BlockSpec((rows, A_Q), lambda n: (n, 0))],
        out_specs=[pl.BlockSpec((rows, A_Q), lambda n: (n, 0)), kvo, kvo, pl.BlockSpec((1, 128), lambda n: (0, 0))],
        out_shape=[jax.ShapeDtypeStruct((t, A_Q), BF16), jax.ShapeDtypeStruct((t, 2 * A_KV), F32),
                   jax.ShapeDtypeStruct((t, 2 * A_KV), F32), jax.ShapeDtypeStruct((1, 128), F32)],
        compiler_params=_params(("arbitrary",)))(proj, proj, proj, sinks, dout)


QK_SCALE = B_HEAD_DIM ** -0.5
PREP_COLS = 256
PREP_NCB = 3 * B_W // PREP_COLS
HALO = 8
PREP_ROWS = 512


def _roll_rows(x, shift):
    n = x.shape[0]
    return x if shift % n == 0 else pltpu.roll(x, shift % n, 0)


def _conv_taps(xe, w):
    xs = [_roll_rows(xe, CONV_K - 1 - i) for i in range(CONV_K)]
    c = w[0:1] * xs[0]
    for i in range(1, CONV_K):
        c = c + w[i:i + 1] * xs[i]
    return xs, c


def dprep_fwd(proj, conv_w, *, name):
    t = proj.shape[0]
    tt = min(PREP_ROWS, t)
    col0 = COL_QKVB // PREP_COLS

    def body(x_ref, h_ref, w_ref, o_ref):
        cb, n = pl.program_id(0), pl.program_id(1)
        halo = jnp.where(n > 0, h_ref[...], 0.0)
        xe = jnp.concatenate([halo, x_ref[...]], axis=0)
        _, c = _conv_taps(xe, w_ref[...])
        y = _silu(c)[HALO:]
        parts = []
        for hh in range(PREP_COLS // B_HEAD_DIM):
            yh = y[:, hh * B_HEAD_DIM:(hh + 1) * B_HEAD_DIM]
            parts.append(yh * lax.rsqrt(jnp.sum(yh * yh, axis=-1, keepdims=True) + EPS))
        nrm = jnp.concatenate(parts, axis=-1)
        o_ref[...] = jnp.where(cb < 4, nrm * QK_SCALE, jnp.where(cb < 8, nrm, y))

    return pl.pallas_call(
        body, name=name, grid=(PREP_NCB, t // tt),
        in_specs=[pl.BlockSpec((tt, PREP_COLS), lambda cb, n: (n, col0 + cb)),
                  pl.BlockSpec((HALO, PREP_COLS), lambda cb, n: (jnp.maximum(n * (tt // HALO) - 1, 0), col0 + cb)),
                  pl.BlockSpec((CONV_K, PREP_COLS), lambda cb, n: (0, cb))],
        out_specs=pl.BlockSpec((tt, PREP_COLS), lambda cb, n: (n, cb)),
        out_shape=jax.ShapeDtypeStruct((t, 3 * B_W), F32), compiler_params=_params(("parallel", "parallel")))(
            proj, proj, conv_w)


def dprep_bwd(proj, conv_w, dqkvn, *, name):
    t = proj.shape[0]
    tt = min(PREP_ROWS, t)
    nb = t // tt
    col0 = COL_QKVB // PREP_COLS
    n8 = t // HALO

    def body(xc_ref, xb_ref, xa_ref, dc_ref, da_ref, w_ref, dx_ref, dw_ref):
        cb, n = pl.program_id(0), pl.program_id(1)

        @pl.when(n == 0)
        def _():
            dw_ref[...] = jnp.zeros_like(dw_ref)

        w = w_ref[...]
        xe = jnp.concatenate([jnp.where(n > 0, xb_ref[...], 0.0), xc_ref[...], xa_ref[...]], axis=0)
        xs, c = _conv_taps(xe, w)
        sg = _sigmoid(c)
        y = c * sg
        dout = jnp.concatenate([jnp.zeros((HALO, PREP_COLS), F32), dc_ref[...],
                                jnp.where(n < nb - 1, da_ref[...], 0.0)], axis=0)
        dsc = jnp.where(cb < 4, QK_SCALE, 1.0)
        parts = []
        for hh in range(PREP_COLS // B_HEAD_DIM):
            sl = slice(hh * B_HEAD_DIM, (hh + 1) * B_HEAD_DIM)
            yh, doh = y[:, sl], dout[:, sl] * dsc
            r = lax.rsqrt(jnp.sum(yh * yh, axis=-1, keepdims=True) + EPS)
            parts.append(doh * r - yh * (r * r * r) * jnp.sum(doh * yh, axis=-1, keepdims=True))
        dy = jnp.where(cb < 8, jnp.concatenate(parts, axis=-1), dout)
        dcv = dy * sg * (1.0 + c * (1.0 - sg))
        dxe = w[CONV_K - 1:CONV_K] * dcv
        for i in range(CONV_K - 1):
            dxe = dxe + w[i:i + 1] * _roll_rows(dcv, -(CONV_K - 1 - i))
        dx_ref[...] = dxe[HALO:HALO + tt].astype(BF16)
        for i in range(CONV_K):
            dw_ref[i:i + 1, :] += jnp.sum((dcv * xs[i])[HALO:HALO + tt], axis=0, keepdims=True)

    def after(n):
        return jnp.minimum((n + 1) * (tt // HALO), n8 - 1)

    return pl.pallas_call(
        body, name=name, grid=(PREP_NCB, nb),
        in_specs=[pl.BlockSpec((tt, PREP_COLS), lambda cb, n: (n, col0 + cb)),
                  pl.BlockSpec((HALO, PREP_COLS), lambda cb, n: (jnp.maximum(n * (tt // HALO) - 1, 0), col0 + cb)),
                  pl.BlockSpec((HALO, PREP_COLS), lambda cb, n: (after(n), col0 + cb)),
                  pl.BlockSpec((tt, PREP_COLS), lambda cb, n: (n, cb)),
                  pl.BlockSpec((HALO, PREP_COLS), lambda cb, n: (after(n), cb)),
                  pl.BlockSpec((CONV_K, PREP_COLS), lambda cb, n: (0, cb))],
        out_specs=[pl.BlockSpec((tt, PREP_COLS), lambda cb, n: (n, cb)),
                   pl.BlockSpec((CONV_K, PREP_COLS), lambda cb, n: (0, cb))],
        out_shape=[jax.ShapeDtypeStruct((t, 3 * B_W), BF16), jax.ShapeDtypeStruct((CONV_K, 3 * B_W), F32)],
        compiler_params=_params(("parallel", "arbitrary")))(proj, proj, proj, dqkvn, dqkvn, conv_w)


def _softplus(z):
    return jnp.maximum(z, 0.0) + jnp.log(1.0 + jnp.exp(-jnp.abs(z)))


def gates_fwd(proj, alog_pad, dtb_pad, *, name):
    t = proj.shape[0]

    def body(x_ref, a_ref, b_ref, o_ref):
        raw = x_ref[...]
        lane = lax.broadcasted_iota(jnp.int32, raw.shape, 1)
        g = -jnp.exp(a_ref[...]) * _softplus(raw + b_ref[...])
        o_ref[...] = jnp.where(lane < B_HEADS, _sigmoid(raw), jnp.where(lane < 2 * B_HEADS, g, 0.0))

    vec = pl.BlockSpec((1, 128), lambda n: (0, 0))
    return pl.pallas_call(
        body, name=name, grid=(t // ROWS,),
        in_specs=[pl.BlockSpec((ROWS, 128), lambda n: (n, COL_GATE // 128)), vec, vec],
        out_specs=pl.BlockSpec((ROWS, 128), lambda n: (n, 0)),
        out_shape=jax.ShapeDtypeStruct((t, 128), F32), compiler_params=_params(("parallel",)))(
            proj, alog_pad, dtb_pad)


def gates_bwd(proj, alog_pad, dtb_pad, dgates, *, name):
    t = proj.shape[0]

    def body(x_ref, a_ref, b_ref, dg_ref, dx_ref, da_ref, db_ref):
        @pl.when(pl.program_id(0) == 0)
        def _():
            da_ref[...] = jnp.zeros_like(da_ref)
            db_ref[...] = jnp.zeros_like(db_ref)

        raw, dgt = x_ref[...], dg_ref[...]
        lane = lax.broadcasted_iota(jnp.int32, raw.shape, 1)
        is_beta, is_g = lane < B_HEADS, (lane >= B_HEADS) & (lane < 2 * B_HEADS)
        beta = _sigmoid(raw)
        z = raw + b_ref[...]
        neg_a = -jnp.exp(a_ref[...])
        d_z = jnp.where(is_g, dgt * neg_a * _sigmoid(z), 0.0)
        dx_ref[...] = jnp.where(is_beta, dgt * beta * (1.0 - beta), d_z).astype(BF16)
        db_ref[...] += jnp.sum(d_z, axis=0, keepdims=True)
        da_ref[...] += jnp.sum(jnp.where(is_g, dgt * neg_a * _softplus(z), 0.0), axis=0, keepdims=True)

    vec = pl.BlockSpec((1, 128), lambda n: (0, 0))
    row = pl.BlockSpec((ROWS, 128), lambda n: (n, 0))
    return pl.pallas_call(
        body, name=name, grid=(t // ROWS,),
        in_specs=[pl.BlockSpec((ROWS, 128), lambda n: (n, COL_GATE // 128)), vec, vec, row],
        out_specs=[row, vec, vec],
        out_shape=[jax.ShapeDtypeStruct((t, 128), BF16), jax.ShapeDtypeStruct((1, 128), F32),
                   jax.ShapeDtypeStruct((1, 128), F32)],
        compiler_params=_params(("arbitrary",)))(proj, alog_pad, dtb_pad, dgates)


def _split2(a):
    hi = a.astype(BF16)
    return hi, (a - hi.astype(F32)).astype(BF16)


def _dotp(a, b, dims, passes):
    if passes == 1:
        return _dot(a.astype(BF16), b.astype(BF16), dims)
    ah, al = _split2(a)
    bh, bl = _split2(b)
    return _dot(ah, bh, dims) + (_dot(ah, bl, dims) + _dot(al, bh, dims))


_GRAD_DIMS = {NN: ((NT, False), (TN, False)), NT: ((NN, False), (TN, True)), TN: ((NT, True), (NN, False))}


def _make_mm(dims, passes, grad_passes):
    (da_dims, da_swap), (db_dims, db_swap) = _GRAD_DIMS[dims]

    @jax.custom_vjp
    def mm(a, b):
        return _dotp(a, b, dims, passes)

    def fwd(a, b):
        return _dotp(a, b, dims, passes), (a, b)

    def bwd(saved, ct):
        a, b = saved
        da = _dotp(b, ct, da_dims, grad_passes) if da_swap else _dotp(ct, b, da_dims, grad_passes)
        db = _dotp(ct, a, db_dims, grad_passes) if db_swap else _dotp(a, ct, db_dims, grad_passes)
        return da, db

    mm.defvjp(fwd, bwd)
    return mm


MM1 = {d: _make_mm(d, 1, 1) for d in (NN, NT, TN)}
MM3 = {d: _make_mm(d, 3, 1) for d in (NN, NT, TN)}


def _neumann_value(n):
    c = n.shape[-1]
    eye = (lax.broadcasted_iota(jnp.int32, (c, c), 0) == lax.broadcasted_iota(jnp.int32, (c, c), 1)).astype(F32)
    inv, pw = eye + n, n
    for _ in range(5):
        pw = _dotp(pw, pw, NN, 3)
        inv = inv + _dotp(inv, pw, NN, 3)
    return inv


@jax.custom_vjp
def _neumann_inverse(n):
    return _neumann_value(n)


def _neumann_fwd(n):
    inv = _neumann_value(n)
    return inv, inv


def _neumann_bwd(inv, ct):
    return (_dotp(_dotp(inv, ct, TN, 1), inv, NT, 1),)


_neumann_inverse.defvjp(_neumann_fwd, _neumann_bwd)


def _tri_ones(lower):
    r = lax.broadcasted_iota(jnp.int32, (DN_CHUNK, DN_CHUNK), 0)
    c = lax.broadcasted_iota(jnp.int32, (DN_CHUNK, DN_CHUNK), 1)
    return (r >= c if lower else r <= c).astype(BF16)


def _tri_sum(x, lower):
    tri = _tri_ones(lower)
    hi = x.astype(BF16)
    r1 = x - hi.astype(F32)
    mid = r1.astype(BF16)
    lo = (r1 - mid.astype(F32)).astype(BF16)
    return _dot(tri, hi, NN) + (_dot(tri, mid, NN) + _dot(tri, lo, NN))


def _delta_chunk(s0, q, k, v, beta, gam_c, gam_r):
    c = DN_CHUNK
    nh = s0.shape[0]
    r = lax.broadcasted_iota(jnp.int32, (c, c), 0)
    cc = lax.broadcasted_iota(jnp.int32, (c, c), 1)
    incl, strict = r >= cc, r > cc
    decay = jnp.exp(jnp.where(incl, gam_c - gam_r, NEG_INF))
    g_last = gam_c[:, c - 1:c, :]
    e_gam, e_rest, e_last = jnp.exp(gam_c), jnp.exp(g_last - gam_c), jnp.exp(g_last)
    a_neg = -jnp.where(strict, beta * MM1[NT](k, k) * decay, 0.0)
    inv = _neumann_inverse(a_neg)
    uw = MM3[NN](inv,jnp.concatenate([v * beta, k * (beta * e_gam)], axis=-1))
    u, w = uw[..., :B_HEAD_DIM], uw[..., B_HEAD_DIM:]
    qk = MM1[NT](q, k) * decay
    q_dec, k_rest = q * e_gam, k * e_rest
    state, outs = s0, []
    for g in range(q.shape[0] // nh):
        sl = slice(g * nh, (g + 1) * nh)
        v_new = u[sl] - MM1[NN](w[sl], state)
        outs.append(MM1[NN](q_dec[sl], state) + MM1[NN](qk[sl], v_new))
        state = state * e_last[sl] + MM1[TN](k_rest[sl], v_new)
    return state, jnp.concatenate(outs, axis=0)


DN_GROUP = 4


def _delta_operands(q_ref, k_ref, v_ref, g_ref, ng):
    c = DN_CHUNK
    qs, ks, vs, betas, gam_cs, gam_rs = [], [], [], [], [], []
    for g in range(ng):
        rows = slice(g * c, (g + 1) * c)
        gt = g_ref[rows, :]
        gam = _tri_sum(gt, True)
        gam_t = gam.T
        for h in range(B_HEADS):
            cols = slice(h * B_HEAD_DIM, (h + 1) * B_HEAD_DIM)
            qs.append(q_ref[rows, cols])
            ks.append(k_ref[rows, cols])
            vs.append(v_ref[rows, cols])
            betas.append(gt[:, h:h + 1])
            gam_cs.append(gam[:, B_HEADS + h:B_HEADS + h + 1])
            gam_rs.append(gam_t[B_HEADS + h:B_HEADS + h + 1, :])
    return tuple(jnp.stack(a) for a in (qs, ks, vs, betas, gam_cs, gam_rs))


def delta_fwd(qkvn, gates, *, name):
    t = qkvn.shape[0]
    ng = min(DN_GROUP, t // DN_CHUNK)
    rows = ng * DN_CHUNK
    nc = t // rows

    def body(q_ref, k_ref, v_ref, g_ref, o_ref, ss_ref, state):
        @pl.when(pl.program_id(0) == 0)
        def _():
            state[...] = jnp.zeros_like(state)

        s0 = state[...]
        ss_ref[...] = s0
        s1, o = _delta_chunk(s0, *_delta_operands(q_ref, k_ref, v_ref, g_ref, ng))
        state[...] = s1
        for g in range(ng):
            for h in range(B_HEADS):
                o_ref[g * DN_CHUNK:(g + 1) * DN_CHUNK, h * B_HEAD_DIM:(h + 1) * B_HEAD_DIM] = o[g * B_HEADS + h]

    blk = lambda j: pl.BlockSpec((rows, B_W), lambda n: (n, j))
    return pl.pallas_call(
        body, name=name, grid=(nc,),
        in_specs=[blk(0), blk(1), blk(2), pl.BlockSpec((rows, 128), lambda n: (n, 0))],
        out_specs=[blk(0), pl.BlockSpec((None, B_HEADS, B_HEAD_DIM, B_HEAD_DIM), lambda n: (n, 0, 0, 0))],
        out_shape=[jax.ShapeDtypeStruct((t, B_W), F32),
                   jax.ShapeDtypeStruct((nc, B_HEADS, B_HEAD_DIM, B_HEAD_DIM), F32)],
        scratch_shapes=[pltpu.VMEM((B_HEADS, B_HEAD_DIM, B_HEAD_DIM), F32)],
        compiler_params=_params(("arbitrary",)))(qkvn, qkvn, qkvn, gates)


def delta_bwd(qkvn, gates, ssave, do, *, name):
    t = qkvn.shape[0]
    ng = min(DN_GROUP, t // DN_CHUNK)
    rows = ng * DN_CHUNK
    nc = t // rows

    def body(q_ref, k_ref, v_ref, g_ref, ss_ref, do_ref, dx_ref, dg_ref, dstate):
        @pl.when(pl.program_id(0) == 0)
        def _():
            dstate[...] = jnp.zeros_like(dstate)

        lane = lax.broadcasted_iota(jnp.int32, (DN_CHUNK, 128), 1)
        row = lax.broadcasted_iota(jnp.int32, (128, DN_CHUNK), 0)
        _, vjp = jax.vjp(_delta_chunk, ss_ref[...], *_delta_operands(q_ref, k_ref, v_ref, g_ref, ng))
        do = jnp.stack([do_ref[g * DN_CHUNK:(g + 1) * DN_CHUNK, h * B_HEAD_DIM:(h + 1) * B_HEAD_DIM]
                        for g in range(ng) for h in range(B_HEADS)])
        ds0, dq, dk, dv, dbeta, dgam_c, dgam_r = vjp((dstate[...], do))
        dstate[...] = ds0
        for g in range(ng):
            blk = slice(g * DN_CHUNK, (g + 1) * DN_CHUNK)
            dbeta_all = jnp.zeros((DN_CHUNK, 128), F32)
            dgam_c_all = jnp.zeros((DN_CHUNK, 128), F32)
            dgam_r_all = jnp.zeros((128, DN_CHUNK), F32)
            for h in range(B_HEADS):
                e = g * B_HEADS + h
                dx_ref[blk, h * B_HEAD_DIM:(h + 1) * B_HEAD_DIM] = dq[e]
                dx_ref[blk, B_W + h * B_HEAD_DIM:B_W + (h + 1) * B_HEAD_DIM] = dk[e]
                dx_ref[blk, 2 * B_W + h * B_HEAD_DIM:2 * B_W + (h + 1) * B_HEAD_DIM] = dv[e]
                dbeta_all = dbeta_all + jnp.where(lane == h, dbeta[e], 0.0)
                dgam_c_all = dgam_c_all + jnp.where(lane == B_HEADS + h, dgam_c[e], 0.0)
                dgam_r_all = dgam_r_all + jnp.where(row == B_HEADS + h, dgam_r[e], 0.0)
            dg_ref[blk, :] = dbeta_all + _tri_sum(dgam_c_all + dgam_r_all.T, False)

    blk = lambda j: pl.BlockSpec((rows, B_W), lambda n: (nc - 1 - n, j))
    gsp = pl.BlockSpec((rows, 128), lambda n: (nc - 1 - n, 0))
    return pl.pallas_call(
        body, name=name, grid=(nc,),
        in_specs=[blk(0), blk(1), blk(2), gsp,
                  pl.BlockSpec((None, B_HEADS, B_HEAD_DIM, B_HEAD_DIM), lambda n: (nc - 1 - n, 0, 0, 0)), blk(0)],
        out_specs=[pl.BlockSpec((rows, 3 * B_W), lambda n: (nc - 1 - n, 0)), gsp],
        out_shape=[jax.ShapeDtypeStruct((t, 3 * B_W), F32), jax.ShapeDtypeStruct((t, 128), F32)],
        scratch_shapes=[pltpu.VMEM((B_HEADS, B_HEAD_DIM, B_HEAD_DIM), F32)],
        compiler_params=_params(("arbitrary",)))(qkvn, qkvn, qkvn, gates, ssave, do)


GNORM_ROWS = 1024


def gnorm_fwd(o, proj, onorm, *, name):
    t = o.shape[0]

    def body(o_ref, z_ref, w_ref, out_ref):
        ov = o_ref[...]
        r = lax.rsqrt(jnp.mean(ov * ov, axis=-1, keepdims=True) + EPS)
        out_ref[...] = (ov * r * w_ref[...] * _silu(z_ref[...])).astype(BF16)

    rows = min(GNORM_ROWS, t)
    blk = pl.BlockSpec((rows, B_HEAD_DIM), lambda n, h: (n, h))
    return pl.pallas_call(
        body, name=name, grid=(t // rows, B_HEADS),
        in_specs=[blk, pl.BlockSpec((rows, B_HEAD_DIM), lambda n, h: (n, COL_Z // B_HEAD_DIM + h)),
                  pl.BlockSpec((1, B_HEAD_DIM), lambda n, h: (0, 0))],
        out_specs=blk, out_shape=jax.ShapeDtypeStruct((t, B_W), BF16),
        compiler_params=_params(("parallel", "parallel")))(o, proj, onorm)


def gnorm_bwd(o, proj, onorm, dout, *, dcol0, name):
    t = o.shape[0]

    def body(o_ref, z_ref, w_ref, d_ref, do_ref, dz_ref, dw_ref):
        @pl.when((pl.program_id(0) == 0) & (pl.program_id(1) == 0))
        def _():
            dw_ref[...] = jnp.zeros_like(dw_ref)

        ov, zv, wv, dv = o_ref[...], z_ref[...], w_ref[...], d_ref[...].astype(F32)
        r = lax.rsqrt(jnp.mean(ov * ov, axis=-1, keepdims=True) + EPS)
        nrm = ov * r
        dz_ref[...] = (dv * nrm * wv * _dsilu(zv)).astype(BF16)
        da = dv * _silu(zv)
        dw_ref[...] += jnp.sum(da * nrm, axis=0, keepdims=True)
        dn = da * wv
        do_ref[...] = r * dn - ov * (r * r * r) * jnp.mean(dn * ov, axis=-1, keepdims=True)

    rows = min(GNORM_ROWS, t)
    blk = pl.BlockSpec((rows, B_HEAD_DIM), lambda n, h: (n, h))
    vec = pl.BlockSpec((1, B_HEAD_DIM), lambda n, h: (0, 0))
    return pl.pallas_call(
        body, name=name, grid=(t // rows, B_HEADS),
        in_specs=[blk, pl.BlockSpec((rows, B_HEAD_DIM), lambda n, h: (n, COL_Z // B_HEAD_DIM + h)), vec,
                  pl.BlockSpec((rows, B_HEAD_DIM), lambda n, h: (n, dcol0 // B_HEAD_DIM + h))],
        out_specs=[blk, blk, vec],
        out_shape=[jax.ShapeDtypeStruct((t, B_W), F32), jax.ShapeDtypeStruct((t, B_W), BF16),
                   jax.ShapeDtypeStruct((1, B_HEAD_DIM), F32)],
        compiler_params=_params(("arbitrary", "arbitrary")))(o, proj, onorm, dout)


def _ffn_fwd(h, norm_g, wg, wu, wd, tm, tag):
    hn = rms_fwd(h, norm_g, name=f"ffn{tag}_norm")
    gate, up, act = mm_gate_up(hn, wg, wu, tm=min(512, tm), tn=1408, tk=2048, name=f"ffn{tag}_gate_up")
    h_out = mm_nn(act, wd, tm=tm, tn=2048, tk=512, out_dtype=F32, res=h, name=f"ffn{tag}_down")
    return h_out, (hn, gate, up, act)


def _ffn_bwd(dh, h, norm_g, wg, wu, wd, saved, tm, tag, emit):
    hn, gate, up, act = saved
    dwd = mm_tn(act, dh, shards=1, tm=tm, tn=1024, tk=1408, out_dtype=BF16, name=f"ffn{tag}_dwd")[0]
    dgate, dup = mm_down_bwd(dh, wd, gate, up, tm=tm, tn=512, tk=2048, name=f"ffn{tag}_dact")
    dwg = mm_tn(hn, dgate, shards=N_SHARD, tm=tm, tn=1408, tk=1024, out_dtype=BF16, name=f"ffn{tag}_dwg")
    dwu = mm_tn(hn, dup, shards=N_SHARD, tm=tm, tn=1408, tk=1024, out_dtype=BF16, name=f"ffn{tag}_dwu")
    started = emit(f"ffn{tag}", {"gate": dwg, "up": dwu, "down": dwd})
    dhn = mm_nt(dgate, wg, tm=tm, tn=1024, tk=1408, out_dtype=F32, name=f"ffn{tag}_dhn_g")
    dh_in, dnorm = dgrad_rms_bwd(dup, wu, NT, h, norm_g + started, dh, tm=min(512, tm), tk=1408, res=dhn,
                                 name=f"ffn{tag}_dhn_u_dnorm")
    return dh_in, dnorm


def _local_step(x, target, w, get, emit):
    t = x.shape[0]
    tm = min(1024, t)
    g = {}

    hn0 = rms_fwd(x, w["even_norm"], name="l0_norm")
    w.update(get("even_in", hn0))
    proj = mm_nt(hn0, w["even_w_in"], tm=tm, tn=512, tk=2048, out_dtype=F32, name="l0_w_in")
    out_a = att_fwd(proj, w["sinks"], name="l0_att")
    qkvn = dprep_fwd(proj, w["even_conv"], name="l0_prep")
    gates = gates_fwd(proj, w["a_log"], w["dt_bias"], name="l0_gates")
    o_delta, ssave = delta_fwd(qkvn, gates, name="l0_delta")
    w.update(get("even_out", o_delta))
    out_b = gnorm_fwd(o_delta, proj, w["onorm"], name="l0_gnorm")
    mix0 = jnp.concatenate([out_a, out_b], axis=-1)
    h1 = mm_nn(mix0, w["even_w_out"], tm=tm, tn=1024, tk=2048, out_dtype=F32, res=x, name="l0_w_out")
    f0 = get("ffn0", h1)
    h2, ffn0 = _ffn_fwd(h1, w["ffn_norm"][0:1] + f0["tok"], f0["gate"], f0["up"], f0["down"], tm, 0)
    hn2 = rms_fwd(h2, w["odd_norm"], name="l1_norm")
    w.update(get("odd", hn2))
    zpre = mm_nn(hn2, w["odd_w_in"], tm=tm, tn=1024, tk=2048, out_dtype=F32, name="l1_w_in")
    gated = gmlp_fwd(zpre, w["odd_ln_g"], w["odd_ln_b"], w["odd_w_s"], w["odd_b_s"], name="l1_gmlp")
    h3 = mm_nn(gated, w["odd_w_out"], tm=tm, tn=1024, tk=2048, out_dtype=F32, res=h2, name="l1_w_out")
    f1 = get("ffn1", h3)
    h4, ffn1 = _ffn_fwd(h3, w["ffn_norm"][1:2] + f1["tok"], f1["gate"], f1["up"], f1["down"], tm, 1)
    loss, dh4, g["final_norm"] = loss_head(h4, w["final_norm"], target, name="loss_head")

    dh3, dn1 = _ffn_bwd(dh4, h3, w["ffn_norm"][1:2], f1["gate"], f1["up"], f1["down"], ffn1, tm, 1, emit)
    dw_out_o = mm_tn(gated, dh3, shards=1, tm=tm, tn=1024, tk=1024, out_dtype=BF16, name="l1_dw_out")[0]
    dgated = mm_nt(dh3, w["odd_w_out"], tm=tm, tn=1024, tk=2048, out_dtype=BF16, name="l1_dgated")
    dzpre, g["odd_w_s"], g["odd_b_s"], g["odd_ln_g"], g["odd_ln_b"] = gmlp_bwd(
        zpre, dgated, w["odd_ln_g"], w["odd_ln_b"], w["odd_w_s"], w["odd_b_s"], name="l1_dgmlp")
    dw_in_o = mm_tn(hn2, dzpre, shards=N_SHARD, tm=tm, tn=1024, tk=1024, out_dtype=BF16, name="l1_dw_in")
    started = emit("odd", {"odd_w_in": dw_in_o, "odd_w_out": dw_out_o})
    dh2, g["odd_norm"] = dgrad_rms_bwd(dzpre, w["odd_w_in"], NT, h2, w["odd_norm"] + started, dh3, tm=min(512, tm),
                                       tk=1024, name="l1_dhn_dnorm")
    dh1, dn0 = _ffn_bwd(dh2, h1, w["ffn_norm"][0:1], f0["gate"], f0["up"], f0["down"], ffn0, tm, 0, emit)
    g["ffn_norm"] = jnp.concatenate([dn0, dn1], axis=0)
    dw_out_e = mm_tn(mix0, dh1, shards=1, tm=tm, tn=1024, tk=1024, out_dtype=BF16, name="l0_dw_out")[0]
    started = emit("even_out", {"even_w_out": dw_out_e})
    dmix = mm_nt(dh1, w["even_w_out"], tm=tm, tn=1024, tk=2048, out_dtype=F32, name="l0_dmix")
    dq_a, dkv_cur, dkv_prev, g["sinks"] = att_bwd(proj, w["sinks"] + started, dmix, name="l0_datt")
    dkv = dkv_cur + jnp.concatenate([dkv_prev[WINDOW:], jnp.zeros((WINDOW, 2 * A_KV), F32)], axis=0)
    do_delta, dz, g["onorm"] = gnorm_bwd(o_delta, proj, w["onorm"], dmix, dcol0=A_Q, name="l0_dgnorm")
    dqkvn, dgates = delta_bwd(qkvn, gates, ssave, do_delta, name="l0_ddelta")
    dqkv_b, g["even_conv"] = dprep_bwd(proj, w["even_conv"], dqkvn, name="l0_dprep")
    draw, g["a_log"], g["dt_bias"] = gates_bwd(proj, w["a_log"], w["dt_bias"], dgates, name="l0_dgates")
    dproj = jnp.concatenate([dq_a, dkv.astype(BF16), dqkv_b, dz, draw,
                             jnp.zeros((t, EVEN_IN_PAD - COL_GATE - 128), BF16)], axis=-1)
    dw_in_e = mm_tn(dproj, hn0, shards=1, tm=tm, tn=1024, tk=1408, out_dtype=BF16, name="l0_dw_in")[0]
    grad_x, g["even_norm"] = dgrad_rms_bwd(dproj, w["even_w_in"], NN, x, w["even_norm"], dh1, tm=min(512, tm), tk=512,
                                           name="l0_dhn_dnorm")
    emit("even_in", {"even_w_in": dw_in_e, "small": g})
    return loss, grad_x


ANY = pl.BlockSpec(memory_space=pl.ANY)
N_DEV = 8


def _place():
    return lax.axis_index("x"), lax.axis_index("y"), lax.axis_index("c")


def _chip_peers(x, y, c):
    return [((1 - x, y, c), 2 * (1 - x) + y), ((x, 1 - y, c), 2 * x + 1 - y), ((1 - x, 1 - y, c), 2 * (1 - x) + 1 - y)]


HBM = pl.BlockSpec(memory_space=pltpu.HBM)
SEM = pl.BlockSpec(memory_space=pltpu.SEMAPHORE)
EFFECT = pltpu.SideEffectType.DATAFLOW_SIDE_EFFECTING
N_PEER = 3


def _half(ref, c):
    r, cols = ref.shape
    tile_rows = 32 // jnp.dtype(ref.dtype).itemsize
    if (r // 2) % tile_rows == 0:
        return ref.at[pl.ds(c * (r // 2), r // 2)]
    assert (cols // 2) % 128 == 0, ref.shape
    return ref.at[:, pl.ds(c * (cols // 2), cols // 2)]


def _gather_plan(srcs, lands, send, recv):
    x, y, c = _place()
    return [pltpu.make_async_remote_copy(src_ref=_half(srcs[i], c), dst_ref=_half(lands[i].at[2 * x + y], c),
                                         send_sem=send.at[N_PEER * i + k], recv_sem=recv.at[N_PEER * i + k],
                                         device_id=peer, device_id_type=MESH_ID)
            for i in range(len(srcs)) for k, (peer, _) in enumerate(_chip_peers(x, y, c))]


def _relay_plan(srcs, lands, send, recv):
    x, y, c = _place()
    return [pltpu.make_async_remote_copy(src_ref=_half(lands[i].at[idx], c), dst_ref=_half(lands[i].at[idx], c),
                                         send_sem=send.at[N_PEER * i + k], recv_sem=recv.at[N_PEER * i + k],
                                         device_id=(x, y, 1 - c), device_id_type=MESH_ID)
            for i in range(len(srcs)) for k, (_, idx) in enumerate(_chip_peers(x, y, c))]


def _scatter_plan(srcs, lands, send, recv):
    x, y, c = _place()
    return [pltpu.make_async_remote_copy(src_ref=srcs[i].at[idx], dst_ref=lands[i].at[k], send_sem=send.at[N_PEER * i + k],
                                         recv_sem=recv.at[N_PEER * i + k], device_id=peer, device_id_type=MESH_ID)
            for i in range(len(srcs)) for k, (peer, idx) in enumerate(_chip_peers(x, y, c))]


def _swap_plan(srcs, lands, send, recv):
    x, y, c = _place()
    return [pltpu.make_async_remote_copy(src_ref=srcs[i], dst_ref=lands[i], send_sem=send.at[N_PEER * i],
                                         recv_sem=recv.at[N_PEER * i], device_id=(x, y, 1 - c), device_id_type=MESH_ID)
            for i in range(len(srcs))]


def copies_start(plan, srcs, lands, after, *, name):
    n = len(srcs)
    both = list(srcs) + list(lands)

    def body(*refs):
        src_refs, land_refs = refs[:n], refs[n:2 * n]
        send, recv = refs[2 * n + 1], refs[2 * n + 2]
        for cp in plan(src_refs, land_refs, send, recv):
            cp.start()
        refs[-1][...] = jnp.zeros_like(refs[-1])

    res = pl.pallas_call(
        body, name=name,
        out_shape=(pltpu.SemaphoreType.DMA((n * N_PEER,)), pltpu.SemaphoreType.DMA((n * N_PEER,)),
                   *[pltpu.HBM(a.shape, a.dtype) for a in both], jax.ShapeDtypeStruct((8, 128), F32)),
        in_specs=[HBM] * (2 * n) + [ANY],
        out_specs=(SEM, SEM, *[HBM] * (2 * n), pl.BlockSpec(memory_space=pltpu.VMEM)),
        input_output_aliases={i: 2 + i for i in range(2 * n)},
        compiler_params=pltpu.CompilerParams(has_side_effects=EFFECT))(
            *[pltpu.with_memory_space_constraint(a, pltpu.HBM) for a in both], after)
    return {"send": res[0], "recv": res[1], "srcs": list(res[2:2 + n]), "lands": list(res[2 + n:2 + 2 * n]),
            "token": res[-1]}


def copies_relay(arrived_plan, next_plan, started, after, *, name):
    srcs, lands = started["srcs"], started["lands"]
    n = len(srcs)
    both = srcs + lands

    def body(*refs):
        src_refs, land_refs = refs[:n], refs[n:2 * n]
        send1, recv1 = refs[2 * n], refs[2 * n + 1]
        send2, recv2 = refs[2 * n + 3], refs[2 * n + 4]
        for cp in arrived_plan(src_refs, land_refs, send1, recv1):
            cp.wait_send()
            cp.wait_recv()
        for cp in next_plan(src_refs, land_refs, send2, recv2):
            cp.start()
        refs[-1][...] = jnp.zeros_like(refs[-1])

    res = pl.pallas_call(
        body, name=name,
        out_shape=(pltpu.SemaphoreType.DMA((n * N_PEER,)), pltpu.SemaphoreType.DMA((n * N_PEER,)),
                   *[pltpu.HBM(a.shape, a.dtype) for a in both], jax.ShapeDtypeStruct((8, 128), F32)),
        in_specs=[HBM] * (2 * n) + [SEM, SEM, ANY],
        out_specs=(SEM, SEM, *[HBM] * (2 * n), pl.BlockSpec(memory_space=pltpu.VMEM)),
        input_output_aliases={i: 2 + i for i in range(2 * n)},
        compiler_params=pltpu.CompilerParams(has_side_effects=EFFECT))(*both, started["send"], started["recv"], after)
    return {"send": res[0], "recv": res[1], "srcs": list(res[2:2 + n]), "lands": list(res[2 + n:2 + 2 * n]),
            "token": res[-1]}


def copies_wait(plan, started, after, *, name):
    srcs, lands = started["srcs"], started["lands"]
    n = len(srcs)
    both = srcs + lands

    def body(*refs):
        src_refs, land_refs = refs[:n], refs[n:2 * n]
        send, recv = refs[2 * n], refs[2 * n + 1]
        for cp in plan(src_refs, land_refs, send, recv):
            cp.wait_send()
            cp.wait_recv()

    res = pl.pallas_call(
        body, name=name, out_shape=tuple(pltpu.HBM(a.shape, a.dtype) for a in both),
        in_specs=[HBM] * (2 * n) + [SEM, SEM, ANY], out_specs=(HBM,) * (2 * n),
        input_output_aliases={i: i for i in range(2 * n)},
        compiler_params=pltpu.CompilerParams(has_side_effects=EFFECT))(*both, started["send"], started["recv"], after)
    return list(res[:n]), list(res[n:])


def allgather_small(small, *, name):
    def body(small_ref, out_ref, send, recv, loc):
        x, y, c = _place()
        dev = 4 * x + 2 * y + c
        local = pltpu.make_async_copy(small_ref, out_ref.at[dev], loc)
        remote = []
        for r in range(1, N_DEV):
            fx, fy, fc = (r >> 2) & 1, (r >> 1) & 1, r & 1
            peer = (1 - x if fx else x, 1 - y if fy else y, 1 - c if fc else c)
            remote.append(pltpu.make_async_remote_copy(
                src_ref=small_ref, dst_ref=out_ref.at[dev], send_sem=send.at[r - 1], recv_sem=recv.at[r - 1],
                device_id=peer, device_id_type=MESH_ID))
        local.start()
        for cp in remote:
            cp.start()
        for cp in remote:
            cp.wait()
        local.wait()

    return pl.pallas_call(
        body, name=name, in_specs=[ANY], out_specs=ANY,
        out_shape=jax.ShapeDtypeStruct((N_DEV,) + small.shape, small.dtype),
        scratch_shapes=[pltpu.SemaphoreType.DMA((N_DEV - 1,)), pltpu.SemaphoreType.DMA((N_DEV - 1,)),
                        pltpu.SemaphoreType.DMA(())])(small)


RED_ROWS = 256
RED_COLS = 256


def _red_block(r, c):
    if r % RED_ROWS == 0:
        return RED_ROWS, c
    if c > RED_COLS and c % RED_COLS == 0:
        return r, RED_COLS
    return r, c


def sum_chips(by_owner, me, got, *, name):
    _, r, c = by_owner.shape
    rb, cb = _red_block(r, c)

    def body(me_ref, o_ref, a_ref, b_ref, c_ref, out_ref):
        total = ((o_ref[...].astype(F32) + a_ref[...].astype(F32)) + b_ref[...].astype(F32)) + c_ref[...].astype(F32)
        out_ref[...] = total.astype(BF16)

    gk = lambda k: pl.BlockSpec((None, rb, cb), lambda i, j, me_ref: (k, i, j))
    grid_spec = pltpu.PrefetchScalarGridSpec(
        num_scalar_prefetch=1, grid=(r // rb, c // cb),
        in_specs=[pl.BlockSpec((None, rb, cb), lambda i, j, me_ref: (me_ref[0], i, j)), gk(0), gk(1), gk(2)],
        out_specs=pl.BlockSpec((rb, cb), lambda i, j, me_ref: (i, j)))
    return pl.pallas_call(
        body, name=name, grid_spec=grid_spec, out_shape=jax.ShapeDtypeStruct((r, c), BF16),
        compiler_params=_params(("parallel", "parallel")))(me, by_owner, got, got, got)


def sum_devices(small_all, *, name):
    _, p, c = small_all.shape

    def body(a_ref, out_ref):
        acc = a_ref[0]
        for d in range(1, N_DEV):
            acc = acc + a_ref[d]
        out_ref[...] = acc

    return pl.pallas_call(
        body, name=name, grid=(1,), in_specs=[pl.BlockSpec((N_DEV, p, c), lambda i: (0, 0, 0))],
        out_specs=pl.BlockSpec((p, c), lambda i: (0, 0)), out_shape=jax.ShapeDtypeStruct((p, c), F32),
        compiler_params=_params(("arbitrary",)))(small_all)


def adamw(parts, w, m, v, *, name):
    nl, r, c = w.shape
    assert len(parts) == nl
    npart = len(parts[0])
    rb, cb = _red_block(r, c)
    flat = [a for layer in parts for a in layer]

    def body(*refs):
        p_refs, (w_ref, m_ref, v_ref) = refs[:nl * npart], refs[nl * npart:nl * npart + 3]
        g_ref, d_ref, nm_ref, nv_ref = refs[nl * npart + 3:]
        layer = pl.program_id(0)
        grad = None
        for l in range(nl):
            gl = p_refs[l * npart][...].astype(F32)
            for j in range(1, npart):
                gl = gl + p_refs[l * npart + j][...].astype(F32)
            grad = gl if grad is None else jnp.where(layer == l, gl, grad)
        wv, mv, vv = w_ref[...], m_ref[...], v_ref[...]
        nm = ADAM_B1 * mv + (1.0 - ADAM_B1) * grad
        nv = ADAM_B2 * vv + (1.0 - ADAM_B2) * (grad * grad)
        m_hat = nm / (1.0 - ADAM_B1 ** ADAM_STEP)
        v_hat = nv / (1.0 - ADAM_B2 ** ADAM_STEP)
        g_ref[...] = grad
        d_ref[...] = -ADAM_LR * (m_hat / (jnp.sqrt(v_hat) + ADAM_EPS) + ADAM_WD * wv)
        nm_ref[...] = nm
        nv_ref[...] = nv

    pspec = pl.BlockSpec((rb, cb), lambda l, i, j: (i, j))
    wspec = pl.BlockSpec((None, rb, cb), lambda l, i, j: (l, i, j))
    osh = jax.ShapeDtypeStruct((nl, r, c), F32)
    return pl.pallas_call(
        body, name=name, grid=(nl, r // rb, c // cb), in_specs=[pspec] * (nl * npart) + [wspec] * 3,
        out_specs=[wspec] * 4, out_shape=[osh] * 4,
        compiler_params=_params(("parallel", "parallel", "parallel")))(*flat, w, m, v)


def _rows128(a):
    flat = a.reshape(-1)
    pad = (-flat.shape[0]) % 128
    return jnp.pad(flat, (0, pad)).reshape(-1, 128)


def _pack_rows(arrs, multiple=8):
    rows = jnp.concatenate([_rows128(a.astype(F32)) for a in arrs], axis=0)
    return jnp.pad(rows, ((0, (-rows.shape[0]) % multiple), (0, 0)))


def _unpack_rows(rows, shapes):
    out, r0 = [], 0
    for shp in shapes:
        size = 1
        for s in shp:
            size *= s
        nr = -(-size // 128)
        out.append(rows[r0:r0 + nr].reshape(-1)[:size].reshape(shp))
        r0 += nr
    return out


SMALL_LOCAL_GRADS = ["even_norm", "even_conv", "a_log", "dt_bias", "sinks", "onorm", "odd_norm", "odd_ln_g",
                     "odd_ln_b", "odd_w_s", "odd_b_s", "ffn_norm", "final_norm"]
BIG = ["even_w_in", "even_w_out", "odd_w_in", "odd_w_out", "ffn_w_gate", "ffn_w_up", "ffn_w_down"]
WEIGHTS = ["even_norm", "even_w_in", "even_conv", "even_a_log", "even_dt_bias", "even_sinks", "even_onorm",
           "even_w_out", "odd_norm", "odd_w_in", "odd_ln_g", "odd_ln_b", "odd_w_s", "odd_b_s", "odd_w_out",
           "ffn_norm", "ffn_w_gate", "ffn_w_up", "ffn_w_down", "final_norm"]
SMALL = [n for n in WEIGHTS if n not in BIG]


def kernel(x, even_norm, even_w_in, even_conv, even_a_log, even_dt_bias, even_sinks, even_onorm, even_w_out, odd_norm, odd_w_in, odd_ln_g, odd_ln_b, odd_w_s, odd_b_s, odd_w_out, ffn_norm, ffn_w_gate, ffn_w_up, ffn_w_down, final_norm, loss_target, m_even_norm, m_even_w_in, m_even_conv, m_even_a_log, m_even_dt_bias, m_even_sinks, m_even_onorm, m_even_w_out, m_odd_norm, m_odd_w_in, m_odd_ln_g, m_odd_ln_b, m_odd_w_s, m_odd_b_s, m_odd_w_out, m_ffn_norm, m_ffn_w_gate, m_ffn_w_up, m_ffn_w_down, m_final_norm, v_even_norm, v_even_w_in, v_even_conv, v_even_a_log, v_even_dt_bias, v_even_sinks, v_even_onorm, v_even_w_out, v_odd_norm, v_odd_w_in, v_odd_ln_g, v_odd_ln_b, v_odd_w_s, v_odd_b_s, v_odd_w_out, v_ffn_norm, v_ffn_w_gate, v_ffn_w_up, v_ffn_w_down, v_final_norm):
    args = dict(locals())
    wl = {n: args[n] for n in WEIGHTS}
    ml = {n: args["m_" + n] for n in WEIGHTS}
    vl = {n: args["v_" + n] for n in WEIGHTS}
    me = 2 * lax.axis_index("x") + lax.axis_index("y")

    def landing(a):
        return lax.dynamic_update_index_in_dim(lax.empty((N_SHARD,) + a.shape, a.dtype), a, me, 0)

    gather_groups = {
        "even_in": [even_w_in[0].T], "even_out": [even_w_out[0]],
        "ffn0": [ffn_w_gate[0], ffn_w_up[0], ffn_w_down[0]], "odd": [odd_w_in[0], odd_w_out[0]],
        "ffn1": [ffn_w_gate[1], ffn_w_up[1], ffn_w_down[1]],
    }
    gathering, after = {}, even_norm
    for group, arrs in gather_groups.items():
        srcs = [(a + after[0, 0] if gathering else a).astype(BF16) for a in arrs]
        if group == "even_in":
            srcs.append(_pack_rows([even_conv[0], odd_norm, odd_ln_g, odd_ln_b], multiple=16))
        gathering[group] = copies_start(_gather_plan, srcs, [landing(a) for a in srcs], after,
                                        name=f"gather_{group}_start")
        after = gathering[group]["token"]

    order = list(gather_groups)
    relayed, kept = {}, {}
    sinks_pad = jnp.pad(even_sinks, ((0, 0), (0, 128 - A_HEADS)))

    def relay(group, behind):
        relayed[group] = copies_relay(_gather_plan, _relay_plan, gathering[group], behind,
                                      name=f"gather_{group}_relay")
        return relayed[group]["token"][0:1, 0:1]

    def get(group, behind):
        if group not in relayed:
            relay(group, behind)
        _, lands = copies_wait(_relay_plan, relayed[group], behind, name=f"gather_{group}_wait")
        nxt = order.index(group) + 1
        tok = relay(order[nxt], lands[0]) if nxt < len(order) else jnp.zeros((1, 1), F32)
        if group == "even_in":
            parts = zip(*[_unpack_rows(lands[1][s], [(CONV_K, 768), (1, 512), (1, 512), (1, 512)])
                          for s in range(N_SHARD)])
            conv, onorm, lng, lnb = [jnp.concatenate(p, axis=1) for p in parts]
            w_in = jnp.pad(lands[0].reshape(EVEN_IN, D_MODEL), ((0, EVEN_IN_PAD - EVEN_IN), (0, 0)))
            kept["odd_ln_g"] = lng
            return {"even_w_in": w_in, "even_conv": conv + tok, "odd_norm": onorm, "odd_ln_b": lnb}
        if group == "even_out":
            return {"even_w_out": lands[0].reshape(D_MODEL, D_MODEL), "onorm": even_onorm + tok}
        if group == "odd":
            return {"odd_w_in": lands[0], "odd_w_out": lands[1].reshape(D_MODEL, D_MODEL),
                    "odd_ln_g": kept["odd_ln_g"] + tok}
        return {"gate": lands[0], "up": lands[1], "down": lands[2].reshape(D_FF, D_MODEL), "tok": tok}

    rows4 = lambda a: a.reshape(N_SHARD, a.shape[0] // N_SHARD, a.shape[1])
    scattering, small = {}, {}

    def emit(group, grads):
        behind = even_norm
        if group == "even_in":
            small["local"] = grads["small"]
            small["all"] = behind = allgather_small(_pack_rows([grads["small"][n] for n in SMALL_LOCAL_GRADS]),
                                                    name="allgather_small")
            srcs = [grads["even_w_in"][:EVEN_IN].reshape(N_SHARD, EVEN_IN // N_SHARD, D_MODEL)]
        elif group == "even_out":
            srcs = [rows4(grads["even_w_out"])]
        elif group == "odd":
            srcs = [grads["odd_w_in"], rows4(grads["odd_w_out"])]
        else:
            srcs = [grads["gate"], grads["up"], rows4(grads["down"])]
        lands = [lax.empty((N_PEER,) + a.shape[1:], a.dtype) for a in srcs]
        scattering[group] = copies_start(_scatter_plan, srcs, lands, behind, name=f"scatter_{group}_start")
        return scattering[group]["token"][0:1, 0:1]

    pad816 = lambda a: jnp.pad(a, ((0, 0), (B_HEADS, 128 - 2 * B_HEADS)))
    w = {
        "even_norm": even_norm + after[0:1, 0:1],
        "a_log": pad816(even_a_log), "dt_bias": pad816(even_dt_bias),
        "sinks": sinks_pad,
        "onorm": even_onorm,
        "odd_w_s": odd_w_s[0],
        "odd_b_s": jnp.pad(odd_b_s[0].T, ((0, 0), (0, 128 - C_GROUPS))),
        "ffn_norm": ffn_norm,
        "final_norm": final_norm[None],
    }
    loss_l, grad_x = _local_step(x[0], loss_target[0], w, get, emit)
    loss = lax.psum(loss_l[0, 0], ("x", "y", "c"))

    me1 = me.reshape(1).astype(jnp.int32)
    swapping = {}

    def reduce_chips(group, behind):
        srcs, lands = copies_wait(_scatter_plan, scattering[group], behind, name=f"scatter_{group}_wait")
        partial = [sum_chips(srcs[i], me1, lands[i], name=f"sum_chips_{group}_{i}") for i in range(len(srcs))]
        swapping[group] = copies_start(_swap_plan, partial, [lax.empty(p.shape, p.dtype) for p in partial],
                                       even_norm, name=f"swap_{group}_start")
        return swapping[group]["token"]

    def swapped(group, behind):
        mine, theirs = copies_wait(_swap_plan, swapping[group], behind, name=f"swap_{group}_wait")
        return list(zip(mine, theirs))

    behind = scattering["even_in"]["token"]
    for group in ("ffn1", "ffn0", "odd", "even_out"):
        behind = reduce_chips(group, behind)
    sums = {group: swapped(group, behind) for group in ("ffn1", "ffn0", "odd", "even_out")}
    outs = {}
    parts_of = {"even_w_out": [sums["even_out"][0]], "odd_w_in": [sums["odd"][0]], "odd_w_out": [sums["odd"][1]],
                "ffn_w_gate": [sums["ffn0"][0], sums["ffn1"][0]], "ffn_w_up": [sums["ffn0"][1], sums["ffn1"][1]],
                "ffn_w_down": [sums["ffn0"][2], sums["ffn1"][2]]}
    for n in parts_of:
        outs[n] = adamw(parts_of[n], wl[n], ml[n], vl[n], name=f"adamw_{n}")
    behind = reduce_chips("even_in", outs["ffn_w_up"][1])
    flip = lambda a: jnp.transpose(a, (0, 2, 1))
    outs["even_w_in"] = [flip(o) for o in adamw([swapped("even_in", behind)[0]], flip(wl["even_w_in"]),
                                                flip(ml["even_w_in"]), flip(vl["even_w_in"]),
                                                name="adamw_even_w_in")]

    g = small["local"]
    small_sum = sum_devices(small["all"], name="sum_devices")
    sg = dict(zip(SMALL_LOCAL_GRADS, _unpack_rows(small_sum, [g[n].shape for n in SMALL_LOCAL_GRADS])))
    own_cols = lambda a, width: lax.dynamic_slice_in_dim(a, me * width, width, axis=a.ndim - 1)
    small_grads = {
        "even_norm": sg["even_norm"], "even_conv": own_cols(sg["even_conv"], 768)[None],
        "even_a_log": sg["a_log"][:, B_HEADS:2 * B_HEADS], "even_dt_bias": sg["dt_bias"][:, B_HEADS:2 * B_HEADS],
        "even_sinks": sg["sinks"][:, :A_HEADS], "even_onorm": sg["onorm"],
        "odd_norm": own_cols(sg["odd_norm"], 512), "odd_ln_g": own_cols(sg["odd_ln_g"], 512),
        "odd_ln_b": own_cols(sg["odd_ln_b"], 512), "odd_w_s": sg["odd_w_s"][None],
        "odd_b_s": sg["odd_b_s"][:, :C_GROUPS].T[None], "ffn_norm": sg["ffn_norm"], "final_norm": sg["final_norm"][0],
    }
    packed = [_pack_rows([d[n] for n in SMALL])[None] for d in (small_grads, wl, ml, vl)]
    small_out = adamw([(packed[0][0],)], packed[1], packed[2], packed[3], name="adamw_small")
    shapes = [wl[n].shape for n in SMALL]
    for j in range(4):
        for n, a in zip(SMALL, _unpack_rows(small_out[j][0], shapes)):
            outs.setdefault(n, [None] * 4)[j] = a

    return (loss, grad_x[None], *[outs[n][0] for n in WEIGHTS], *[outs[n][1] for n in WEIGHTS],
            *[outs[n][2] for n in WEIGHTS], *[outs[n][3] for n in WEIGHTS])
```

```python
import functools

import jax
import jax.numpy as jnp
from jax import lax
from jax.experimental import pallas as pl
from jax.experimental.pallas import tpu as pltpu

F32 = jnp.float32
BF16 = jnp.bfloat16
NEG_INF = float("-inf")

D_MODEL = 2048
A_HEADS, A_KV_HEADS, A_HEAD_DIM, WINDOW = 16, 2, 64, 128
B_HEADS, B_HEAD_DIM, CONV_K, DN_CHUNK = 8, 128, 4, 64
C_GROUPS, C_CHUNK = 8, 128
C_GROUP_DIM = D_MODEL // C_GROUPS
D_FF = 5632
EPS = 1e-6
A_Q = A_HEADS * A_HEAD_DIM
A_KV = A_KV_HEADS * A_HEAD_DIM
B_W = B_HEADS * B_HEAD_DIM
EVEN_IN = A_Q + 2 * A_KV + 4 * B_W + 2 * B_HEADS
EVEN_IN_PAD = 5632
COL_KV = A_Q
COL_QKVB = A_Q + 2 * A_KV
COL_Z = COL_QKVB + 3 * B_W
COL_GATE = COL_Z + B_W
N_SHARD = 4

ADAM_LR, ADAM_B1, ADAM_B2, ADAM_EPS, ADAM_WD, ADAM_STEP = 0.001, 0.9, 0.999, 1e-08, 0.01, 10

VMEM_LIMIT_V7X = 56 * 1024 * 1024
MXU_COLS = 256
MESH_ID = pl.DeviceIdType.MESH


def _params(sem=None):
    return pltpu.CompilerParams(dimension_semantics=sem, vmem_limit_bytes=VMEM_LIMIT_V7X)


def _sigmoid(x):
    return 1.0 / (1.0 + jnp.exp(-x))


def _silu(x):
    return x * _sigmoid(x)


def _dsilu(x):
    s = _sigmoid(x)
    return s * (1.0 + x * (1.0 - s))


def _gelu(x):
    return 0.5 * x * (1.0 + lax.erf(x * 0.7071067811865476))


def _dgelu(x):
    return 0.5 * (1.0 + lax.erf(x * 0.7071067811865476)) + x * jnp.exp(-0.5 * x * x) * 0.3989422804014327


def _dot(a, b, dims):
    if a.ndim == 3:
        (ca,), (cb,) = dims
        return lax.dot_general(a, b, (((ca + 1,), (cb + 1,)), ((0,), (0,))), preferred_element_type=F32)
    return lax.dot_general(a, b, (dims, ((), ())), preferred_element_type=F32)


NN = ((1,), (0,))
NT = ((1,), (1,))
TN = ((0,), (0,))


def _as3(b):
    return b if b.ndim == 3 else b[None]


def _accumulate(step, nsteps, accs, products, finish):
    if nsteps == 1:
        finish(products())
        return

    @pl.when(step == 0)
    def _():
        for acc, p in zip(accs, products()):
            acc[...] = p

    if nsteps > 2:
        @pl.when((step > 0) & (step < nsteps - 1))
        def _():
            for acc, p in zip(accs, products()):
                acc[...] += p

    @pl.when(step == nsteps - 1)
    def _():
        finish(tuple(acc[...] + p for acc, p in zip(accs, products())))


def mm_nn(a, b, *, tm, tn, tk, out_dtype, name, res=None):
    b3 = _as3(b)
    m, k = a.shape
    s, k2, ns = b3.shape
    assert k2 == k and m % tm == 0 and ns % tn == 0 and k % tk == 0, (a.shape, b3.shape, tm, tn, tk)
    nps, nk = ns // tn, k // tk

    def body(*refs):
        if res is None:
            a_ref, b_ref, o_ref, acc = refs
        else:
            a_ref, b_ref, r_ref, o_ref, acc = refs
        def finish(tiles):
            r = tiles[0] if res is None else tiles[0] + r_ref[...].astype(F32)
            o_ref[...] = r.astype(out_dtype)

        _accumulate(pl.program_id(2), nk, (acc,),
                    lambda: (_dot(a_ref[...].astype(BF16), b_ref[...].astype(BF16), NN),), finish)

    in_specs = [pl.BlockSpec((tm, tk), lambda i, j, kk: (i, kk)),
                pl.BlockSpec((None, tk, tn), lambda i, j, kk: (j // nps, kk, j % nps))]
    args = [a, b3]
    if res is not None:
        in_specs.append(pl.BlockSpec((tm, tn), lambda i, j, kk: (i, j)))
        args.append(res)
    return pl.pallas_call(
        body, name=name, grid=(m // tm, s * nps, nk), in_specs=in_specs,
        out_specs=pl.BlockSpec((tm, tn), lambda i, j, kk: (i, j)),
        out_shape=jax.ShapeDtypeStruct((m, s * ns), out_dtype),
        scratch_shapes=[pltpu.VMEM((tm, tn), F32)],
        compiler_params=_params(("parallel", "parallel", "arbitrary")))(*args)


def mm_nt(a, b, *, tm, tn, tk, out_dtype, name, res=None):
    b3 = _as3(b)
    m, n = a.shape
    s, k, ns = b3.shape
    assert n == s * ns and m % tm == 0 and k % tn == 0 and ns % tk == 0, (a.shape, b3.shape, tm, tn, tk)
    rps = ns // tk
    nr = s * rps

    def body(*refs):
        if res is None:
            a_ref, b_ref, o_ref, acc = refs
        else:
            a_ref, b_ref, r_ref, o_ref, acc = refs
        def finish(tiles):
            r = tiles[0] if res is None else tiles[0] + r_ref[...].astype(F32)
            o_ref[...] = r.astype(out_dtype)

        _accumulate(pl.program_id(2), nr, (acc,),
                    lambda: (_dot(a_ref[...].astype(BF16), b_ref[...].astype(BF16), NT),), finish)

    in_specs = [pl.BlockSpec((tm, tk), lambda i, j, r: (i, r)),
                pl.BlockSpec((None, tn, tk), lambda i, j, r: (r // rps, j, r % rps))]
    args = [a, b3]
    if res is not None:
        in_specs.append(pl.BlockSpec((tm, tn), lambda i, j, r: (i, j)))
        args.append(res)
    return pl.pallas_call(
        body, name=name, grid=(m // tm, k // tn, nr), in_specs=in_specs,
        out_specs=pl.BlockSpec((tm, tn), lambda i, j, r: (i, j)),
        out_shape=jax.ShapeDtypeStruct((m, k), out_dtype),
        scratch_shapes=[pltpu.VMEM((tm, tn), F32)],
        compiler_params=_params(("parallel", "parallel", "arbitrary")))(*args)


def mm_tn(a, b, *, shards, tm, tn, tk, out_dtype, name):
    m, k = a.shape
    m2, n = b.shape
    ns = n // shards
    assert m2 == m and n == shards * ns and m % tm == 0 and k % tk == 0 and ns % tn == 0, (a.shape, b.shape)
    nps, nm = ns // tn, m // tm

    def body(a_ref, b_ref, o_ref, acc):
        def finish(tiles):
            o_ref[...] = tiles[0].astype(out_dtype)

        _accumulate(pl.program_id(2), nm, (acc,),
                    lambda: (_dot(a_ref[...].astype(BF16), b_ref[...].astype(BF16), TN),), finish)

    return pl.pallas_call(
        body, name=name, grid=(k // tk, shards * nps, nm),
        in_specs=[pl.BlockSpec((tm, tk), lambda i, j, mi: (mi, i)),
                  pl.BlockSpec((tm, tn), lambda i, j, mi: (mi, j))],
        out_specs=pl.BlockSpec((None, tk, tn), lambda i, j, mi: (j // nps, i, j % nps)),
        out_shape=jax.ShapeDtypeStruct((shards, k, ns), out_dtype),
        scratch_shapes=[pltpu.VMEM((tk, tn), F32)],
        compiler_params=_params(("parallel", "parallel", "arbitrary")))(a, b)


def mm_gate_up(hn, wg, wu, *, tm, tn, tk, name):
    wg3, wu3 = _as3(wg), _as3(wu)
    m, k = hn.shape
    s, _, ns = wg3.shape
    assert m % tm == 0 and ns % tn == 0 and k % tk == 0
    nps, nk = ns // tn, k // tk

    def body(a_ref, g_ref, u_ref, og_ref, ou_ref, oa_ref, accg, accu):
        def products():
            a = a_ref[...].astype(BF16)
            return _dot(a, g_ref[...].astype(BF16), NN), _dot(a, u_ref[...].astype(BF16), NN)

        def finish(tiles):
            g, u = tiles
            og_ref[...] = g.astype(BF16)
            ou_ref[...] = u.astype(BF16)
            oa_ref[...] = (_silu(g) * u).astype(BF16)

        _accumulate(pl.program_id(2), nk, (accg, accu), products, finish)

    wspec = pl.BlockSpec((None, tk, tn), lambda i, j, kk: (j // nps, kk, j % nps))
    ospec = pl.BlockSpec((tm, tn), lambda i, j, kk: (i, j))
    osh = jax.ShapeDtypeStruct((m, s * ns), BF16)
    return pl.pallas_call(
        body, name=name, grid=(m // tm, s * nps, nk),
        in_specs=[pl.BlockSpec((tm, tk), lambda i, j, kk: (i, kk)), wspec, wspec],
        out_specs=[ospec, ospec, ospec], out_shape=[osh, osh, osh],
        scratch_shapes=[pltpu.VMEM((tm, tn) if nk > 1 else (8, 128), F32)] * 2,
        compiler_params=_params(("parallel", "parallel", "arbitrary")))(hn, wg3, wu3)


def mm_down_bwd(dh, wd, gate, up, *, tm, tn, tk, name):
    m, d = dh.shape
    f, d2 = wd.shape
    assert d2 == d and m % tm == 0 and f % tn == 0 and tk == d and tn % MXU_COLS == 0

    def body(a_ref, b_ref, g_ref, u_ref, og_ref, ou_ref):
        a = a_ref[...].astype(BF16)
        for jj in range(tn // MXU_COLS):
            sl = slice(jj * MXU_COLS, (jj + 1) * MXU_COLS)
            da = _dot(a, b_ref[sl, :].astype(BF16), NT)
            g, u = g_ref[:, sl].astype(F32), u_ref[:, sl].astype(F32)
            s = _sigmoid(g)
            og_ref[:, sl] = (da * u * (s * (1.0 + g * (1.0 - s)))).astype(BF16)
            ou_ref[:, sl] = (da * (g * s)).astype(BF16)

    ospec = pl.BlockSpec((tm, tn), lambda i, j: (i, j))
    osh = jax.ShapeDtypeStruct((m, f), BF16)
    return pl.pallas_call(
        body, name=name, grid=(m // tm, f // tn),
        in_specs=[pl.BlockSpec((tm, tk), lambda i, j: (i, 0)),
                  pl.BlockSpec((tn, tk), lambda i, j: (j, 0)), ospec, ospec],
        out_specs=[ospec, ospec], out_shape=[osh, osh],
        compiler_params=_params(("parallel", "parallel")))(dh, wd, gate, up)


ROWS = 256


def rms_fwd(x, g, *, name):
    t, d = x.shape

    def body(x_ref, g_ref, o_ref):
        xv = x_ref[...]
        r = lax.rsqrt(jnp.mean(xv * xv, axis=-1, keepdims=True) + EPS)
        o_ref[...] = (xv * r * g_ref[...]).astype(BF16)

    return pl.pallas_call(
        body, name=name, grid=(t // ROWS,),
        in_specs=[pl.BlockSpec((ROWS, d), lambda i: (i, 0)), pl.BlockSpec((1, d), lambda i: (0, 0))],
        out_specs=pl.BlockSpec((ROWS, d), lambda i: (i, 0)),
        out_shape=jax.ShapeDtypeStruct((t, d), BF16), compiler_params=_params(("parallel",)))(x, g)


def dgrad_rms_bwd(a, b, form, x, g, dres, *, tm, tk, name, res=None):
    m, d = x.shape
    b3 = _as3(b)
    if form == NN:
        steps = a.shape[1] // tk
        a_spec = pl.BlockSpec((tm, tk), lambda i, r: (i, r))
        b_spec = pl.BlockSpec((None, tk, d), lambda i, r: (0, r, 0))
    else:
        s, d2, ns = b3.shape
        assert d2 == d and ns % tk == 0
        rps = ns // tk
        steps = s * rps
        a_spec = pl.BlockSpec((tm, tk), lambda i, r: (i, r))
        b_spec = pl.BlockSpec((None, d, tk), lambda i, r: (r // rps, 0, r % rps))
    assert m % tm == 0 and a.shape[1] == steps * tk

    def body(*refs):
        if res is None:
            a_ref, b_ref, x_ref, g_ref, dr_ref, dx_ref, dg_ref, acc = refs
        else:
            a_ref, b_ref, r_ref, x_ref, g_ref, dr_ref, dx_ref, dg_ref, acc = refs

        @pl.when((pl.program_id(0) == 0) & (pl.program_id(1) == 0))
        def _():
            dg_ref[...] = jnp.zeros_like(dg_ref)

        def finish(tiles):
            dyv = tiles[0] if res is None else tiles[0] + r_ref[...]
            xv = x_ref[...]
            r = lax.rsqrt(jnp.mean(xv * xv, axis=-1, keepdims=True) + EPS)
            dyg = dyv * g_ref[...]
            dx_ref[...] = r * dyg - xv * (r * r * r) * jnp.mean(dyg * xv, axis=-1, keepdims=True) + dr_ref[...]
            dg_ref[...] += jnp.sum(dyv * xv * r, axis=0, keepdims=True)

        _accumulate(pl.program_id(1), steps, (acc,),
                    lambda: (_dot(a_ref[...].astype(BF16), b_ref[...].astype(BF16), form),), finish)

    row = pl.BlockSpec((tm, d), lambda i, r: (i, 0))
    vec = pl.BlockSpec((1, d), lambda i, r: (0, 0))
    in_specs = [a_spec, b_spec] + ([row] if res is not None else []) + [row, vec, row]
    args = [a, b3] + ([res] if res is not None else []) + [x, g, dres]
    return pl.pallas_call(
        body, name=name, grid=(m // tm, steps), in_specs=in_specs, out_specs=[row, vec],
        out_shape=[jax.ShapeDtypeStruct((m, d), F32), jax.ShapeDtypeStruct((1, d), F32)],
        scratch_shapes=[pltpu.VMEM((tm, d), F32)],
        compiler_params=_params(("arbitrary", "arbitrary")))(*args)


def loss_head(h, g, target, *, name):
    t, d = h.shape

    def body(x_ref, g_ref, t_ref, loss_ref, dx_ref, dg_ref):
        @pl.when(pl.program_id(0) == 0)
        def _():
            dg_ref[...] = jnp.zeros_like(dg_ref)
            loss_ref[...] = jnp.zeros_like(loss_ref)

        xv, gv = x_ref[...], g_ref[...]
        r = lax.rsqrt(jnp.mean(xv * xv, axis=-1, keepdims=True) + EPS)
        e = xv * r * gv - t_ref[...]
        loss_ref[...] += 0.5 * jnp.sum(jnp.mean(e * e, axis=-1, keepdims=True), axis=0, keepdims=True)
        dyv = e * (1.0 / d)
        dyg = dyv * gv
        dx_ref[...] = r * dyg - xv * (r * r * r) * jnp.mean(dyg * xv, axis=-1, keepdims=True)
        dg_ref[...] += jnp.sum(dyv * xv * r, axis=0, keepdims=True)

    row = pl.BlockSpec((ROWS, d), lambda i: (i, 0))
    vec = pl.BlockSpec((1, d), lambda i: (0, 0))
    return pl.pallas_call(
        body, name=name, grid=(t // ROWS,), in_specs=[row, vec, row],
        out_specs=[pl.BlockSpec((1, 128), lambda i: (0, 0)), row, vec],
        out_shape=[jax.ShapeDtypeStruct((1, 128), F32), jax.ShapeDtypeStruct((t, d), F32),
                   jax.ShapeDtypeStruct((1, d), F32)],
        compiler_params=_params(("arbitrary",)))(h, g, target)


def _tril_mask():
    r = lax.broadcasted_iota(jnp.int32, (C_CHUNK, C_CHUNK), 0)
    c = lax.broadcasted_iota(jnp.int32, (C_CHUNK, C_CHUNK), 1)
    return r >= c


def _layer_norm_parts(v):
    mu = jnp.mean(v, axis=-1, keepdims=True)
    vc = v - mu
    rstd = lax.rsqrt(jnp.mean(vc * vc, axis=-1, keepdims=True) + EPS)
    return vc * rstd, rstd


def gmlp_fwd(zpre, ln_g, ln_b, ws, bs_t, *, name):
    t = zpre.shape[0]
    d = D_MODEL

    def body(zu_ref, zv_ref, g_ref, b_ref, ws_ref, bs_ref, o_ref):
        u = _gelu(zu_ref[...])
        vhat, _ = _layer_norm_parts(_gelu(zv_ref[...]))
        vln = (vhat * g_ref[...] + b_ref[...]).astype(BF16)
        mask = _tril_mask()
        for gi in range(C_GROUPS):
            sl = slice(gi * C_GROUP_DIM, (gi + 1) * C_GROUP_DIM)
            w = jnp.where(mask, ws_ref[gi], 0.0).astype(BF16)
            mixed = _dot(w, vln[:, sl], NN) + bs_ref[:, gi:gi + 1]
            o_ref[:, sl] = (u[:, sl] * mixed).astype(BF16)

    vec = pl.BlockSpec((1, d), lambda i: (0, 0))
    return pl.pallas_call(
        body, name=name, grid=(t // C_CHUNK,),
        in_specs=[pl.BlockSpec((C_CHUNK, d), lambda i: (i, 0)), pl.BlockSpec((C_CHUNK, d), lambda i: (i, 1)),
                  vec, vec, pl.BlockSpec((C_GROUPS, C_CHUNK, C_CHUNK), lambda i: (0, 0, 0)),
                  pl.BlockSpec((C_CHUNK, 128), lambda i: (0, 0))],
        out_specs=pl.BlockSpec((C_CHUNK, d), lambda i: (i, 0)),
        out_shape=jax.ShapeDtypeStruct((t, d), BF16), compiler_params=_params(("parallel",)))(
            zpre, zpre, ln_g, ln_b, ws, bs_t)


def gmlp_bwd(zpre, dgated, ln_g, ln_b, ws, bs_t, *, name):
    t = zpre.shape[0]
    d = D_MODEL

    def body(zu_ref, zv_ref, dg_ref, g_ref, b_ref, ws_ref, bs_ref, dz_ref, dws_ref, dbs_ref, dlg_ref, dlb_ref):
        @pl.when(pl.program_id(0) == 0)
        def _():
            dws_ref[...] = jnp.zeros_like(dws_ref)
            dbs_ref[...] = jnp.zeros_like(dbs_ref)
            dlg_ref[...] = jnp.zeros_like(dlg_ref)
            dlb_ref[...] = jnp.zeros_like(dlb_ref)

        zu, zv = zu_ref[...], zv_ref[...]
        u = _gelu(zu)
        vhat, rstd = _layer_norm_parts(_gelu(zv))
        gam = g_ref[...]
        vln = (vhat * gam + b_ref[...]).astype(BF16)
        dgt = dg_ref[...].astype(F32)
        mask = _tril_mask()
        lane = lax.broadcasted_iota(jnp.int32, (C_CHUNK, 128), 1)
        dbs = jnp.zeros((C_CHUNK, 128), F32)
        du_parts, dvln_parts = [], []
        for gi in range(C_GROUPS):
            sl = slice(gi * C_GROUP_DIM, (gi + 1) * C_GROUP_DIM)
            w = jnp.where(mask, ws_ref[gi], 0.0).astype(BF16)
            mixed = _dot(w, vln[:, sl], NN) + bs_ref[:, gi:gi + 1]
            du_parts.append(dgt[:, sl] * mixed)
            dmixed = dgt[:, sl] * u[:, sl]
            dmb = dmixed.astype(BF16)
            dws_ref[gi] += jnp.where(mask, _dot(dmb, vln[:, sl], NT), 0.0)
            dbs = dbs + jnp.where(lane == gi, jnp.sum(dmixed, axis=-1, keepdims=True), 0.0)
            dvln_parts.append(_dot(w, dmb, TN))
        dbs_ref[...] += dbs
        du = jnp.concatenate(du_parts, axis=-1)
        dvln = jnp.concatenate(dvln_parts, axis=-1)
        dlg_ref[...] += jnp.sum(dvln * vhat, axis=0, keepdims=True)
        dlb_ref[...] += jnp.sum(dvln, axis=0, keepdims=True)
        dvhat = dvln * gam
        dv = rstd * (dvhat - jnp.mean(dvhat, axis=-1, keepdims=True)
                     - vhat * jnp.mean(dvhat * vhat, axis=-1, keepdims=True))
        dz_ref[:, :d] = (du * _dgelu(zu)).astype(BF16)
        dz_ref[:, d:] = (dv * _dgelu(zv)).astype(BF16)

    vec = pl.BlockSpec((1, d), lambda i: (0, 0))
    wsp = pl.BlockSpec((C_GROUPS, C_CHUNK, C_CHUNK), lambda i: (0, 0, 0))
    bsp = pl.BlockSpec((C_CHUNK, 128), lambda i: (0, 0))
    return pl.pallas_call(
        body, name=name, grid=(t // C_CHUNK,),
        in_specs=[pl.BlockSpec((C_CHUNK, d), lambda i: (i, 0)), pl.BlockSpec((C_CHUNK, d), lambda i: (i, 1)),
                  pl.BlockSpec((C_CHUNK, d), lambda i: (i, 0)), vec, vec, wsp, bsp],
        out_specs=[pl.BlockSpec((C_CHUNK, 2 * d), lambda i: (i, 0)), wsp, bsp, vec, vec],
        out_shape=[jax.ShapeDtypeStruct((t, 2 * d), BF16), jax.ShapeDtypeStruct((C_GROUPS, C_CHUNK, C_CHUNK), F32),
                   jax.ShapeDtypeStruct((C_CHUNK, 128), F32), jax.ShapeDtypeStruct((1, d), F32),
                   jax.ShapeDtypeStruct((1, d), F32)],
        compiler_params=_params(("arbitrary",)))(zpre, zpre, dgated, ln_g, ln_b, ws, bs_t)


ATT_SCALE = A_HEAD_DIM ** -0.5
PAIRS = A_HEADS // 2
PAIRS_PER_KV = PAIRS // A_KV_HEADS


def _att_padded(tile):
    lo = lax.broadcasted_iota(jnp.int32, tile.shape, 1) < A_HEAD_DIM
    rolled = pltpu.roll(tile, A_HEAD_DIM, 1)
    zero = jnp.zeros_like(tile)
    return {(0, 0): jnp.where(lo, tile, zero).astype(BF16), (0, 1): jnp.where(lo, zero, rolled).astype(BF16),
            (1, 0): jnp.where(lo, rolled, zero).astype(BF16), (1, 1): jnp.where(lo, zero, tile).astype(BF16)}


def _att_valid(n):
    r = lax.broadcasted_iota(jnp.int32, (WINDOW, 2 * WINDOW), 0)
    c = lax.broadcasted_iota(jnp.int32, (WINDOW, 2 * WINDOW), 1)
    rel = r + WINDOW - c
    return (rel >= 0) & (rel < WINDOW) & ((c >= WINDOW) | (n > 0))


def _att_probs(qp, kpad, sink, valid):
    s = jnp.where(valid, _dot(qp, kpad, NT), NEG_INF)
    m = jnp.maximum(jnp.max(s, axis=-1, keepdims=True), sink)
    p = jnp.exp(s - m)
    e_sink = jnp.exp(sink - m)
    inv = 1.0 / (jnp.sum(p, axis=-1, keepdims=True) + e_sink)
    return p * inv, e_sink * inv


ATT_BLOCKS = 4


def _att_operands(q_ref, kvc_ref, kvp_ref, s_ref, step, nb):
    w = WINDOW
    kvs = [kvp_ref[...]] + [kvc_ref[b * w:(b + 1) * w, :] for b in range(nb)]
    key = lambda h: ((h // 2) // PAIRS_PER_KV, h % 2)
    qs, ks, vs, valids = [], [], [], []
    for b in range(nb):
        kv = jnp.concatenate([kvs[b], kvs[b + 1]], axis=0)
        kpad, vpad = _att_padded(kv[:, :128]), _att_padded(kv[:, 128:])
        pairs = [(q_ref[b * w:(b + 1) * w, j * 128:(j + 1) * 128] * ATT_SCALE).astype(BF16) for j in range(PAIRS)]
        qs += [pairs[h // 2] for h in range(A_HEADS)]
        ks += [kpad[key(h)] for h in range(A_HEADS)]
        vs += [vpad[key(h)] for h in range(A_HEADS)]
        valids += [_att_valid(step * nb + b)] * A_HEADS
    sink = jnp.stack([s_ref[:, h:h + 1] for h in range(A_HEADS)] * nb)
    return jnp.stack(qs), jnp.stack(ks), jnp.stack(vs), sink, jnp.stack(valids)


def _att_specs(nb):
    rows = nb * WINDOW
    return [pl.BlockSpec((rows, A_Q), lambda n: (n, 0)),
            pl.BlockSpec((rows, 2 * A_KV), lambda n: (n, COL_KV // (2 * A_KV))),
            pl.BlockSpec((WINDOW, 2 * A_KV), lambda n: (jnp.maximum(nb * n - 1, 0), COL_KV // (2 * A_KV))),
            pl.BlockSpec((1, 128), lambda n: (0, 0))]


def att_fwd(proj, sinks, *, name):
    t = proj.shape[0]
    nb = min(ATT_BLOCKS, t // WINDOW)
    rows = nb * WINDOW

    def body(q_ref, kvc_ref, kvp_ref, s_ref, o_ref):
        q, k, v, sink, valid = _att_operands(q_ref, kvc_ref, kvp_ref, s_ref, pl.program_id(0), nb)
        w, _ = _att_probs(q, k, sink, valid)
        o = _dot(w.astype(BF16), v, NN)
        for b in range(nb):
            for j in range(PAIRS):
                pair = o[b * A_HEADS + 2 * j] + o[b * A_HEADS + 2 * j + 1]
                o_ref[b * WINDOW:(b + 1) * WINDOW, j * 128:(j + 1) * 128] = pair.astype(BF16)

    return pl.pallas_call(
        body, name=name, grid=(t // rows,), in_specs=_att_specs(nb),
        out_specs=pl.BlockSpec((rows, A_Q), lambda n: (n, 0)),
        out_shape=jax.ShapeDtypeStruct((t, A_Q), BF16), compiler_params=_params(("parallel",)))(
            proj, proj, proj, sinks)


def att_bwd(proj, sinks, dout, *, name):
    t = proj.shape[0]
    nb = min(ATT_BLOCKS, t // WINDOW)
    rows = nb * WINDOW

    def body(q_ref, kvc_ref, kvp_ref, s_ref, do_ref, dq_ref, dkc_ref, dkp_ref, ds_ref):
        @pl.when(pl.program_id(0) == 0)
        def _():
            ds_ref[...] = jnp.zeros_like(ds_ref)

        q, k, v, sink, valid = _att_operands(q_ref, kvc_ref, kvp_ref, s_ref, pl.program_id(0), nb)
        dop = jnp.stack([do_ref[b * WINDOW:(b + 1) * WINDOW, (h // 2) * 128:(h // 2 + 1) * 128]
                         for b in range(nb) for h in range(A_HEADS)]).astype(BF16)
        w, w_sink = _att_probs(q, k, sink, valid)
        dw = _dot(dop, v, NT)
        delta = jnp.sum(w * dw, axis=-1, keepdims=True)
        dsc = (w * (dw - delta)).astype(BF16)
        dsink_h = -jnp.sum(w_sink * delta, axis=1, keepdims=True)
        dq = _dot(dsc, k, NN)
        dk_h = _dot(dsc, q, TN)
        dv_h = _dot(w.astype(BF16), dop, TN)
        lane = lax.broadcasted_iota(jnp.int32, (1, 128), 1)
        dsink = jnp.zeros((1, 128), F32)
        for b in range(nb):
            for h in range(A_HEADS):
                dsink = dsink + jnp.where(lane == h, dsink_h[b * A_HEADS + h], 0.0)
        ds_ref[...] += dsink
        lo = lax.broadcasted_iota(jnp.int32, (2 * WINDOW, 128), 1) < A_HEAD_DIM
        heads_per_kv = A_HEADS // A_KV_HEADS

        def tile(per_head, b):
            acc = {}
            for kvh in range(A_KV_HEADS):
                for half in range(2):
                    hs = range(kvh * heads_per_kv + half, (kvh + 1) * heads_per_kv, 2)
                    acc[(kvh, half)] = functools.reduce(lambda a, c: a + c, [per_head[b * A_HEADS + h] for h in hs])
            return jnp.where(lo, acc[(0, 0)] + pltpu.roll(acc[(0, 1)], A_HEAD_DIM, 1),
                             pltpu.roll(acc[(1, 0)], A_HEAD_DIM, 1) + acc[(1, 1)])

        for b in range(nb):
            blk = slice(b * WINDOW, (b + 1) * WINDOW)
            for j in range(PAIRS):
                pair = dq[b * A_HEADS + 2 * j] + dq[b * A_HEADS + 2 * j + 1]
                dq_ref[blk, j * 128:(j + 1) * 128] = (pair * ATT_SCALE).astype(BF16)
            dkv = jnp.concatenate([tile(dk_h, b), tile(dv_h, b)], axis=1)
            dkp_ref[blk, :] = dkv[:WINDOW]
            dkc_ref[blk, :] = dkv[WINDOW:]

    kvo = pl.BlockSpec((rows, 2 * A_KV), lambda n: (n, 0))
    return pl.pallas_call(
        body, name=name, grid=(t // rows,),
        in_specs=_att_specs(nb) + [pl.BlockSpec((rows, A_Q), lambda n: (n, 0))],
        out_specs=[pl.BlockSpec((rows, A_Q), lambda n: (n, 0)), kvo, kvo, pl.BlockSpec((1, 128), lambda n: (0, 0))],
        out_shape=[jax.ShapeDtypeStruct((t, A_Q), BF16), jax.ShapeDtypeStruct((t, 2 * A_KV), F32),
                   jax.ShapeDtypeStruct((t, 2 * A_KV), F32), jax.ShapeDtypeStruct((1, 128), F32)],
        compiler_params=_params(("arbitrary",)))(proj, proj, proj, sinks, dout)


QK_SCALE = B_HEAD_DIM ** -0.5
PREP_COLS = 256
PREP_NCB = 3 * B_W // PREP_COLS
HALO = 8
PREP_ROWS = 512


def _roll_rows(x, shift):
    n = x.shape[0]
    return x if shift % n == 0 else pltpu.roll(x, shift % n, 0)


def _conv_taps(xe, w):
    xs = [_roll_rows(xe, CONV_K - 1 - i) for i in range(CONV_K)]
    c = w[0:1] * xs[0]
    for i in range(1, CONV_K):
        c = c + w[i:i + 1] * xs[i]
    return xs, c


def dprep_fwd(proj, conv_w, *, name):
    t = proj.shape[0]
    tt = min(PREP_ROWS, t)
    col0 = COL_QKVB // PREP_COLS

    def body(x_ref, h_ref, w_ref, o_ref):
        cb, n = pl.program_id(0), pl.program_id(1)
        halo = jnp.where(n > 0, h_ref[...], 0.0)
        xe = jnp.concatenate([halo, x_ref[...]], axis=0)
        _, c = _conv_taps(xe, w_ref[...])
        y = _silu(c)[HALO:]
        parts = []
        for hh in range(PREP_COLS // B_HEAD_DIM):
            yh = y[:, hh * B_HEAD_DIM:(hh + 1) * B_HEAD_DIM]
            parts.append(yh * lax.rsqrt(jnp.sum(yh * yh, axis=-1, keepdims=True) + EPS))
        nrm = jnp.concatenate(parts, axis=-1)
        o_ref[...] = jnp.where(cb < 4, nrm * QK_SCALE, jnp.where(cb < 8, nrm, y))

    return pl.pallas_call(
        body, name=name, grid=(PREP_NCB, t // tt),
        in_specs=[pl.BlockSpec((tt, PREP_COLS), lambda cb, n: (n, col0 + cb)),
                  pl.BlockSpec((HALO, PREP_COLS), lambda cb, n: (jnp.maximum(n * (tt // HALO) - 1, 0), col0 + cb)),
                  pl.BlockSpec((CONV_K, PREP_COLS), lambda cb, n: (0, cb))],
        out_specs=pl.BlockSpec((tt, PREP_COLS), lambda cb, n: (n, cb)),
        out_shape=jax.ShapeDtypeStruct((t, 3 * B_W), F32), compiler_params=_params(("parallel", "parallel")))(
            proj, proj, conv_w)


def dprep_bwd(proj, conv_w, dqkvn, *, name):
    t = proj.shape[0]
    tt = min(PREP_ROWS, t)
    nb = t // tt
    col0 = COL_QKVB // PREP_COLS
    n8 = t // HALO

    def body(xc_ref, xb_ref, xa_ref, dc_ref, da_ref, w_ref, dx_ref, dw_ref):
        cb, n = pl.program_id(0), pl.program_id(1)

        @pl.when(n == 0)
        def _():
            dw_ref[...] = jnp.zeros_like(dw_ref)

        w = w_ref[...]
        xe = jnp.concatenate([jnp.where(n > 0, xb_ref[...], 0.0), xc_ref[...], xa_ref[...]], axis=0)
        xs, c = _conv_taps(xe, w)
        sg = _sigmoid(c)
        y = c * sg
        dout = jnp.concatenate([jnp.zeros((HALO, PREP_COLS), F32), dc_ref[...],
                                jnp.where(n < nb - 1, da_ref[...], 0.0)], axis=0)
        dsc = jnp.where(cb < 4, QK_SCALE, 1.0)
        parts = []
        for hh in range(PREP_COLS // B_HEAD_DIM):
            sl = slice(hh * B_HEAD_DIM, (hh + 1) * B_HEAD_DIM)
            yh, doh = y[:, sl], dout[:, sl] * dsc
            r = lax.rsqrt(jnp.sum(yh * yh, axis=-1, keepdims=True) + EPS)
            parts.append(doh * r - yh * (r * r * r) * jnp.sum(doh * yh, axis=-1, keepdims=True))
        dy = jnp.where(cb < 8, jnp.concatenate(parts, axis=-1), dout)
        dcv = dy * sg * (1.0 + c * (1.0 - sg))
        dxe = w[CONV_K - 1:CONV_K] * dcv
        for i in range(CONV_K - 1):
            dxe = dxe + w[i:i + 1] * _roll_rows(dcv, -(CONV_K - 1 - i))
        dx_ref[...] = dxe[HALO:HALO + tt].astype(BF16)
        for i in range(CONV_K):
            dw_ref[i:i + 1, :] += jnp.sum((dcv * xs[i])[HALO:HALO + tt], axis=0, keepdims=True)

    def after(n):
        return jnp.minimum((n + 1) * (tt // HALO), n8 - 1)

    return pl.pallas_call(
        body, name=name, grid=(PREP_NCB, nb),
        in_specs=[pl.BlockSpec((tt, PREP_COLS), lambda cb, n: (n, col0 + cb)),
                  pl.BlockSpec((HALO, PREP_COLS), lambda cb, n: (jnp.maximum(n * (tt // HALO) - 1, 0), col0 + cb)),
                  pl.BlockSpec((HALO, PREP_COLS), lambda cb, n: (after(n), col0 + cb)),
                  pl.BlockSpec((tt, PREP_COLS), lambda cb, n: (n, cb)),
                  pl.BlockSpec((HALO, PREP_COLS), lambda cb, n: (after(n), cb)),
                  pl.BlockSpec((CONV_K, PREP_COLS), lambda cb, n: (0, cb))],
        out_specs=[pl.BlockSpec((tt, PREP_COLS), lambda cb, n: (n, cb)),
                   pl.BlockSpec((CONV_K, PREP_COLS), lambda cb, n: (0, cb))],
        out_shape=[jax.ShapeDtypeStruct((t, 3 * B_W), BF16), jax.ShapeDtypeStruct((CONV_K, 3 * B_W), F32)],
        compiler_params=_params(("parallel", "arbitrary")))(proj, proj, proj, dqkvn, dqkvn, conv_w)


def _softplus(z):
    return jnp.maximum(z, 0.0) + jnp.log(1.0 + jnp.exp(-jnp.abs(z)))


def gates_fwd(proj, alog_pad, dtb_pad, *, name):
    t = proj.shape[0]

    def body(x_ref, a_ref, b_ref, o_ref):
        raw = x_ref[...]
        lane = lax.broadcasted_iota(jnp.int32, raw.shape, 1)
        g = -jnp.exp(a_ref[...]) * _softplus(raw + b_ref[...])
        o_ref[...] = jnp.where(lane < B_HEADS, _sigmoid(raw), jnp.where(lane < 2 * B_HEADS, g, 0.0))

    vec = pl.BlockSpec((1, 128), lambda n: (0, 0))
    return pl.pallas_call(
        body, name=name, grid=(t // ROWS,),
        in_specs=[pl.BlockSpec((ROWS, 128), lambda n: (n, COL_GATE // 128)), vec, vec],
        out_specs=pl.BlockSpec((ROWS, 128), lambda n: (n, 0)),
        out_shape=jax.ShapeDtypeStruct((t, 128), F32), compiler_params=_params(("parallel",)))(
            proj, alog_pad, dtb_pad)


def gates_bwd(proj, alog_pad, dtb_pad, dgates, *, name):
    t = proj.shape[0]

    def body(x_ref, a_ref, b_ref, dg_ref, dx_ref, da_ref, db_ref):
        @pl.when(pl.program_id(0) == 0)
        def _():
            da_ref[...] = jnp.zeros_like(da_ref)
            db_ref[...] = jnp.zeros_like(db_ref)

        raw, dgt = x_ref[...], dg_ref[...]
        lane = lax.broadcasted_iota(jnp.int32, raw.shape, 1)
        is_beta, is_g = lane < B_HEADS, (lane >= B_HEADS) & (lane < 2 * B_HEADS)
        beta = _sigmoid(raw)
        z = raw + b_ref[...]
        neg_a = -jnp.exp(a_ref[...])
        d_z = jnp.where(is_g, dgt * neg_a * _sigmoid(z), 0.0)
        dx_ref[...] = jnp.where(is_beta, dgt * beta * (1.0 - beta), d_z).astype(BF16)
        db_ref[...] += jnp.sum(d_z, axis=0, keepdims=True)
        da_ref[...] += jnp.sum(jnp.where(is_g, dgt * neg_a * _softplus(z), 0.0), axis=0, keepdims=True)

    vec = pl.BlockSpec((1, 128), lambda n: (0, 0))
    row = pl.BlockSpec((ROWS, 128), lambda n: (n, 0))
    return pl.pallas_call(
        body, name=name, grid=(t // ROWS,),
        in_specs=[pl.BlockSpec((ROWS, 128), lambda n: (n, COL_GATE // 128)), vec, vec, row],
        out_specs=[row, vec, vec],
        out_shape=[jax.ShapeDtypeStruct((t, 128), BF16), jax.ShapeDtypeStruct((1, 128), F32),
                   jax.ShapeDtypeStruct((1, 128), F32)],
        compiler_params=_params(("arbitrary",)))(proj, alog_pad, dtb_pad, dgates)


def _split2(a):
    hi = a.astype(BF16)
    return hi, (a - hi.astype(F32)).astype(BF16)


def _dotp(a, b, dims, passes):
    if passes == 1:
        return _dot(a.astype(BF16), b.astype(BF16), dims)
    ah, al = _split2(a)
    bh, bl = _split2(b)
    return _dot(ah, bh, dims) + (_dot(ah, bl, dims) + _dot(al, bh, dims))


_GRAD_DIMS = {NN: ((NT, False), (TN, False)), NT: ((NN, False), (TN, True)), TN: ((NT, True), (NN, False))}


def _make_mm(dims, passes, grad_passes):
    (da_dims, da_swap), (db_dims, db_swap) = _GRAD_DIMS[dims]

    @jax.custom_vjp
    def mm(a, b):
        return _dotp(a, b, dims, passes)

    def fwd(a, b):
        return _dotp(a, b, dims, passes), (a, b)

    def bwd(saved, ct):
        a, b = saved
        da = _dotp(b, ct, da_dims, grad_passes) if da_swap else _dotp(ct, b, da_dims, grad_passes)
        db = _dotp(ct, a, db_dims, grad_passes) if db_swap else _dotp(a, ct, db_dims, grad_passes)
        return da, db

    mm.defvjp(fwd, bwd)
    return mm


MM1 = {d: _make_mm(d, 1, 1) for d in (NN, NT, TN)}
MM3 = {d: _make_mm(d, 3, 1) for d in (NN, NT, TN)}


def _neumann_value(n):
    c = n.shape[-1]
    eye = (lax.broadcasted_iota(jnp.int32, (c, c), 0) == lax.broadcasted_iota(jnp.int32, (c, c), 1)).astype(F32)
    inv, pw = eye + n, n
    for _ in range(5):
        pw = _dotp(pw, pw, NN, 3)
        inv = inv + _dotp(inv, pw, NN, 3)
    return inv


@jax.custom_vjp
def _neumann_inverse(n):
    return _neumann_value(n)


def _neumann_fwd(n):
    inv = _neumann_value(n)
    return inv, inv


def _neumann_bwd(inv, ct):
    return (_dotp(_dotp(inv, ct, TN, 1), inv, NT, 1),)


_neumann_inverse.defvjp(_neumann_fwd, _neumann_bwd)


def _tri_ones(lower):
    r = lax.broadcasted_iota(jnp.int32, (DN_CHUNK, DN_CHUNK), 0)
    c = lax.broadcasted_iota(jnp.int32, (DN_CHUNK, DN_CHUNK), 1)
    return (r >= c if lower else r <= c).astype(BF16)


def _tri_sum(x, lower):
    tri = _tri_ones(lower)
    hi = x.astype(BF16)
    r1 = x - hi.astype(F32)
    mid = r1.astype(BF16)
    lo = (r1 - mid.astype(F32)).astype(BF16)
    return _dot(tri, hi, NN) + (_dot(tri, mid, NN) + _dot(tri, lo, NN))


def _delta_chunk(s0, q, k, v, beta, gam_c, gam_r):
    c = DN_CHUNK
    nh = s0.shape[0]
    r = lax.broadcasted_iota(jnp.int32, (c, c), 0)
    cc = lax.broadcasted_iota(jnp.int32, (c, c), 1)
    incl, strict = r >= cc, r > cc
    decay = jnp.exp(jnp.where(incl, gam_c - gam_r, NEG_INF))
    g_last = gam_c[:, c - 1:c, :]
    e_gam, e_rest, e_last = jnp.exp(gam_c), jnp.exp(g_last - gam_c), jnp.exp(g_last)
    a_neg = -jnp.where(strict, beta * MM1[NT](k, k) * decay, 0.0)
    inv = _neumann_inverse(a_neg)
    uw = MM3[NN](inv,jnp.concatenate([v * beta, k * (beta * e_gam)], axis=-1))
    u, w = uw[..., :B_HEAD_DIM], uw[..., B_HEAD_DIM:]
    qk = MM1[NT](q, k) * decay
    q_dec, k_rest = q * e_gam, k * e_rest
    state, outs = s0, []
    for g in range(q.shape[0] // nh):
        sl = slice(g * nh, (g + 1) * nh)
        v_new = u[sl] - MM1[NN](w[sl], state)
        outs.append(MM1[NN](q_dec[sl], state) + MM1[NN](qk[sl], v_new))
        state = state * e_last[sl] + MM1[TN](k_rest[sl], v_new)
    return state, jnp.concatenate(outs, axis=0)


DN_GROUP = 4


def _delta_operands(q_ref, k_ref, v_ref, g_ref, ng):
    c = DN_CHUNK
    qs, ks, vs, betas, gam_cs, gam_rs = [], [], [], [], [], []
    for g in range(ng):
        rows = slice(g * c, (g + 1) * c)
        gt = g_ref[rows, :]
        gam = _tri_sum(gt, True)
        gam_t = gam.T
        for h in range(B_HEADS):
            cols = slice(h * B_HEAD_DIM, (h + 1) * B_HEAD_DIM)
            qs.append(q_ref[rows, cols])
            ks.append(k_ref[rows, cols])
            vs.append(v_ref[rows, cols])
            betas.append(gt[:, h:h + 1])
            gam_cs.append(gam[:, B_HEADS + h:B_HEADS + h + 1])
            gam_rs.append(gam_t[B_HEADS + h:B_HEADS + h + 1, :])
    return tuple(jnp.stack(a) for a in (qs, ks, vs, betas, gam_cs, gam_rs))


def delta_fwd(qkvn, gates, *, name):
    t = qkvn.shape[0]
    ng = min(DN_GROUP, t // DN_CHUNK)
    rows = ng * DN_CHUNK
    nc = t // rows

    def body(q_ref, k_ref, v_ref, g_ref, o_ref, ss_ref, state):
        @pl.when(pl.program_id(0) == 0)
        def _():
            state[...] = jnp.zeros_like(state)

        s0 = state[...]
        ss_ref[...] = s0
        s1, o = _delta_chunk(s0, *_delta_operands(q_ref, k_ref, v_ref, g_ref, ng))
        state[...] = s1
        for g in range(ng):
            for h in range(B_HEADS):
                o_ref[g * DN_CHUNK:(g + 1) * DN_CHUNK, h * B_HEAD_DIM:(h + 1) * B_HEAD_DIM] = o[g * B_HEADS + h]

    blk = lambda j: pl.BlockSpec((rows, B_W), lambda n: (n, j))
    return pl.pallas_call(
        body, name=name, grid=(nc,),
        in_specs=[blk(0), blk(1), blk(2), pl.BlockSpec((rows, 128), lambda n: (n, 0))],
        out_specs=[blk(0), pl.BlockSpec((None, B_HEADS, B_HEAD_DIM, B_HEAD_DIM), lambda n: (n, 0, 0, 0))],
        out_shape=[jax.ShapeDtypeStruct((t, B_W), F32),
                   jax.ShapeDtypeStruct((nc, B_HEADS, B_HEAD_DIM, B_HEAD_DIM), F32)],
        scratch_shapes=[pltpu.VMEM((B_HEADS, B_HEAD_DIM, B_HEAD_DIM), F32)],
        compiler_params=_params(("arbitrary",)))(qkvn, qkvn, qkvn, gates)


def delta_bwd(qkvn, gates, ssave, do, *, name):
    t = qkvn.shape[0]
    ng = min(DN_GROUP, t // DN_CHUNK)
    rows = ng * DN_CHUNK
    nc = t // rows

    def body(q_ref, k_ref, v_ref, g_ref, ss_ref, do_ref, dx_ref, dg_ref, dstate):
        @pl.when(pl.program_id(0) == 0)
        def _():
            dstate[...] = jnp.zeros_like(dstate)

        lane = lax.broadcasted_iota(jnp.int32, (DN_CHUNK, 128), 1)
        row = lax.broadcasted_iota(jnp.int32, (128, DN_CHUNK), 0)
        _, vjp = jax.vjp(_delta_chunk, ss_ref[...], *_delta_operands(q_ref, k_ref, v_ref, g_ref, ng))
        do = jnp.stack([do_ref[g * DN_CHUNK:(g + 1) * DN_CHUNK, h * B_HEAD_DIM:(h + 1) * B_HEAD_DIM]
                        for g in range(ng) for h in range(B_HEADS)])
        ds0, dq, dk, dv, dbeta, dgam_c, dgam_r = vjp((dstate[...], do))
        dstate[...] = ds0
        for g in range(ng):
            blk = slice(g * DN_CHUNK, (g + 1) * DN_CHUNK)
            dbeta_all = jnp.zeros((DN_CHUNK, 128), F32)
            dgam_c_all = jnp.zeros((DN_CHUNK, 128), F32)
            dgam_r_all = jnp.zeros((128, DN_CHUNK), F32)
            for h in range(B_HEADS):
                e = g * B_HEADS + h
                dx_ref[blk, h * B_HEAD_DIM:(h + 1) * B_HEAD_DIM] = dq[e]
                dx_ref[blk, B_W + h * B_HEAD_DIM:B_W + (h + 1) * B_HEAD_DIM] = dk[e]
                dx_ref[blk, 2 * B_W + h * B_HEAD_DIM:2 * B_W + (h + 1) * B_HEAD_DIM] = dv[e]
                dbeta_all = dbeta_all + jnp.where(lane == h, dbeta[e], 0.0)
                dgam_c_all = dgam_c_all + jnp.where(lane == B_HEADS + h, dgam_c[e], 0.0)
                dgam_r_all = dgam_r_all + jnp.where(row == B_HEADS + h, dgam_r[e], 0.0)
            dg_ref[blk, :] = dbeta_all + _tri_sum(dgam_c_all + dgam_r_all.T, False)

    blk = lambda j: pl.BlockSpec((rows, B_W), lambda n: (nc - 1 - n, j))
    gsp = pl.BlockSpec((rows, 128), lambda n: (nc - 1 - n, 0))
    return pl.pallas_call(
        body, name=name, grid=(nc,),
        in_specs=[blk(0), blk(1), blk(2), gsp,
                  pl.BlockSpec((None, B_HEADS, B_HEAD_DIM, B_HEAD_DIM), lambda n: (nc - 1 - n, 0, 0, 0)), blk(0)],
        out_specs=[pl.BlockSpec((rows, 3 * B_W), lambda n: (nc - 1 - n, 0)), gsp],
        out_shape=[jax.ShapeDtypeStruct((t, 3 * B_W), F32), jax.ShapeDtypeStruct((t, 128), F32)],
        scratch_shapes=[pltpu.VMEM((B_HEADS, B_HEAD_DIM, B_HEAD_DIM), F32)],
        compiler_params=_params(("arbitrary",)))(qkvn, qkvn, qkvn, gates, ssave, do)


GNORM_ROWS = 1024


def gnorm_fwd(o, proj, onorm, *, name):
    t = o.shape[0]

    def body(o_ref, z_ref, w_ref, out_ref):
        ov = o_ref[...]
        r = lax.rsqrt(jnp.mean(ov * ov, axis=-1, keepdims=True) + EPS)
        out_ref[...] = (ov * r * w_ref[...] * _silu(z_ref[...])).astype(BF16)

    rows = min(GNORM_ROWS, t)
    blk = pl.BlockSpec((rows, B_HEAD_DIM), lambda n, h: (n, h))
    return pl.pallas_call(
        body, name=name, grid=(t // rows, B_HEADS),
        in_specs=[blk, pl.BlockSpec((rows, B_HEAD_DIM), lambda n, h: (n, COL_Z // B_HEAD_DIM + h)),
                  pl.BlockSpec((1, B_HEAD_DIM), lambda n, h: (0, 0))],
        out_specs=blk, out_shape=jax.ShapeDtypeStruct((t, B_W), BF16),
        compiler_params=_params(("parallel", "parallel")))(o, proj, onorm)


def gnorm_bwd(o, proj, onorm, dout, *, dcol0, name):
    t = o.shape[0]

    def body(o_ref, z_ref, w_ref, d_ref, do_ref, dz_ref, dw_ref):
        @pl.when((pl.program_id(0) == 0) & (pl.program_id(1) == 0))
        def _():
            dw_ref[...] = jnp.zeros_like(dw_ref)

        ov, zv, wv, dv = o_ref[...], z_ref[...], w_ref[...], d_ref[...].astype(F32)
        r = lax.rsqrt(jnp.mean(ov * ov, axis=-1, keepdims=True) + EPS)
        nrm = ov * r
        dz_ref[...] = (dv * nrm * wv * _dsilu(zv)).astype(BF16)
        da = dv * _silu(zv)
        dw_ref[...] += jnp.sum(da * nrm, axis=0, keepdims=True)
        dn = da * wv
        do_ref[...] = r * dn - ov * (r * r * r) * jnp.mean(dn * ov, axis=-1, keepdims=True)

    rows = min(GNORM_ROWS, t)
    blk = pl.BlockSpec((rows, B_HEAD_DIM), lambda n, h: (n, h))
    vec = pl.BlockSpec((1, B_HEAD_DIM), lambda n, h: (0, 0))
    return pl.pallas_call(
        body, name=name, grid=(t // rows, B_HEADS),
        in_specs=[blk, pl.BlockSpec((rows, B_HEAD_DIM), lambda n, h: (n, COL_Z // B_HEAD_DIM + h)), vec,
                  pl.BlockSpec((rows, B_HEAD_DIM), lambda n, h: (n, dcol0 // B_HEAD_DIM + h))],
        out_specs=[blk, blk, vec],
        out_shape=[jax.ShapeDtypeStruct((t, B_W), F32), jax.ShapeDtypeStruct((t, B_W), BF16),
                   jax.ShapeDtypeStruct((1, B_HEAD_DIM), F32)],
        compiler_params=_params(("arbitrary", "arbitrary")))(o, proj, onorm, dout)


def _ffn_fwd(h, norm_g, wg, wu, wd, tm, tag):
    hn = rms_fwd(h, norm_g, name=f"ffn{tag}_norm")
    gate, up, act = mm_gate_up(hn, wg, wu, tm=min(512, tm), tn=1408, tk=2048, name=f"ffn{tag}_gate_up")
    h_out = mm_nn(act, wd, tm=tm, tn=2048, tk=512, out_dtype=F32, res=h, name=f"ffn{tag}_down")
    return h_out, (hn, gate, up, act)


def _ffn_bwd(dh, h, norm_g, wg, wu, wd, saved, tm, tag, emit):
    hn, gate, up, act = saved
    dwd = mm_tn(act, dh, shards=1, tm=tm, tn=1024, tk=1408, out_dtype=BF16, name=f"ffn{tag}_dwd")[0]
    dgate, dup = mm_down_bwd(dh, wd, gate, up, tm=tm, tn=512, tk=2048, name=f"ffn{tag}_dact")
    dwg = mm_tn(hn, dgate, shards=N_SHARD, tm=tm, tn=1408, tk=1024, out_dtype=BF16, name=f"ffn{tag}_dwg")
    dwu = mm_tn(hn, dup, shards=N_SHARD, tm=tm, tn=1408, tk=1024, out_dtype=BF16, name=f"ffn{tag}_dwu")
    started = emit(f"ffn{tag}", {"gate": dwg, "up": dwu, "down": dwd})
    dhn = mm_nt(dgate, wg, tm=tm, tn=1024, tk=1408, out_dtype=F32, name=f"ffn{tag}_dhn_g")
    dh_in, dnorm = dgrad_rms_bwd(dup, wu, NT, h, norm_g + started, dh, tm=min(512, tm), tk=1408, res=dhn,
                                 name=f"ffn{tag}_dhn_u_dnorm")
    return dh_in, dnorm


def _local_step(x, target, w, get, emit):
    t = x.shape[0]
    tm = min(1024, t)
    g = {}

    hn0 = rms_fwd(x, w["even_norm"], name="l0_norm")
    w.update(get("even_in", hn0))
    proj = mm_nt(hn0, w["even_w_in"], tm=tm, tn=512, tk=2048, out_dtype=F32, name="l0_w_in")
    out_a = att_fwd(proj, w["sinks"], name="l0_att")
    qkvn = dprep_fwd(proj, w["even_conv"], name="l0_prep")
    gates = gates_fwd(proj, w["a_log"], w["dt_bias"], name="l0_gates")
    o_delta, ssave = delta_fwd(qkvn, gates, name="l0_delta")
    w.update(get("even_out", o_delta))
    out_b = gnorm_fwd(o_delta, proj, w["onorm"], name="l0_gnorm")
    mix0 = jnp.concatenate([out_a, out_b], axis=-1)
    h1 = mm_nn(mix0, w["even_w_out"], tm=tm, tn=1024, tk=2048, out_dtype=F32, res=x, name="l0_w_out")
    f0 = get("ffn0", h1)
    h2, ffn0 = _ffn_fwd(h1, w["ffn_norm"][0:1] + f0["tok"], f0["gate"], f0["up"], f0["down"], tm, 0)
    hn2 = rms_fwd(h2, w["odd_norm"], name="l1_norm")
    w.update(get("odd", hn2))
    zpre = mm_nn(hn2, w["odd_w_in"], tm=tm, tn=1024, tk=2048, out_dtype=F32, name="l1_w_in")
    gated = gmlp_fwd(zpre, w["odd_ln_g"], w["odd_ln_b"], w["odd_w_s"], w["odd_b_s"], name="l1_gmlp")
    h3 = mm_nn(gated, w["odd_w_out"], tm=tm, tn=1024, tk=2048, out_dtype=F32, res=h2, name="l1_w_out")
    f1 = get("ffn1", h3)
    h4, ffn1 = _ffn_fwd(h3, w["ffn_norm"][1:2] + f1["tok"], f1["gate"], f1["up"], f1["down"], tm, 1)
    loss, dh4, g["final_norm"] = loss_head(h4, w["final_norm"], target, name="loss_head")

    dh3, dn1 = _ffn_bwd(dh4, h3, w["ffn_norm"][1:2], f1["gate"], f1["up"], f1["down"], ffn1, tm, 1, emit)
    dw_out_o = mm_tn(gated, dh3, shards=1, tm=tm, tn=1024, tk=1024, out_dtype=BF16, name="l1_dw_out")[0]
    dgated = mm_nt(dh3, w["odd_w_out"], tm=tm, tn=1024, tk=2048, out_dtype=BF16, name="l1_dgated")
    dzpre, g["odd_w_s"], g["odd_b_s"], g["odd_ln_g"], g["odd_ln_b"] = gmlp_bwd(
        zpre, dgated, w["odd_ln_g"], w["odd_ln_b"], w["odd_w_s"], w["odd_b_s"], name="l1_dgmlp")
    dw_in_o = mm_tn(hn2, dzpre, shards=N_SHARD, tm=tm, tn=1024, tk=1024, out_dtype=BF16, name="l1_dw_in")
    started = emit("odd", {"odd_w_in": dw_in_o, "odd_w_out": dw_out_o})
    dh2, g["odd_norm"] = dgrad_rms_bwd(dzpre, w["odd_w_in"], NT, h2, w["odd_norm"] + started, dh3, tm=min(512, tm),
                                       tk=1024, name="l1_dhn_dnorm")
    dh1, dn0 = _ffn_bwd(dh2, h1, w["ffn_norm"][0:1], f0["gate"], f0["up"], f0["down"], ffn0, tm, 0, emit)
    g["ffn_norm"] = jnp.concatenate([dn0, dn1], axis=0)
    dw_out_e = mm_tn(mix0, dh1, shards=1, tm=tm, tn=1024, tk=1024, out_dtype=BF16, name="l0_dw_out")[0]
    started = emit("even_out", {"even_w_out": dw_out_e})
    dmix = mm_nt(dh1, w["even_w_out"], tm=tm, tn=1024, tk=2048, out_dtype=F32, name="l0_dmix")
    dq_a, dkv_cur, dkv_prev, g["sinks"] = att_bwd(proj, w["sinks"] + started, dmix, name="l0_datt")
    dkv = dkv_cur + jnp.concatenate([dkv_prev[WINDOW:], jnp.zeros((WINDOW, 2 * A_KV), F32)], axis=0)
    do_delta, dz, g["onorm"] = gnorm_bwd(o_delta, proj, w["onorm"], dmix, dcol0=A_Q, name="l0_dgnorm")
    dqkvn, dgates = delta_bwd(qkvn, gates, ssave, do_delta, name="l0_ddelta")
    dqkv_b, g["even_conv"] = dprep_bwd(proj, w["even_conv"], dqkvn, name="l0_dprep")
    draw, g["a_log"], g["dt_bias"] = gates_bwd(proj, w["a_log"], w["dt_bias"], dgates, name="l0_dgates")
    dproj = jnp.concatenate([dq_a, dkv.astype(BF16), dqkv_b, dz, draw,
                             jnp.zeros((t, EVEN_IN_PAD - COL_GATE - 128), BF16)], axis=-1)
    dw_in_e = mm_tn(dproj, hn0, shards=1, tm=tm, tn=1024, tk=1408, out_dtype=BF16, name="l0_dw_in")[0]
    grad_x, g["even_norm"] = dgrad_rms_bwd(dproj, w["even_w_in"], NN, x, w["even_norm"], dh1, tm=min(512, tm), tk=512,
                                           name="l0_dhn_dnorm")
    emit("even_in", {"even_w_in": dw_in_e, "small": g})
    return loss, grad_x


ANY = pl.BlockSpec(memory_space=pl.ANY)
N_DEV = 8


def _place():
    return lax.axis_index("x"), lax.axis_index("y"), lax.axis_index("c")


def _chip_peers(x, y, c):
    return [((1 - x, y, c), 2 * (1 - x) + y), ((x, 1 - y, c), 2 * x + 1 - y), ((1 - x, 1 - y, c), 2 * (1 - x) + 1 - y)]


HBM = pl.BlockSpec(memory_space=pltpu.HBM)
SEM = pl.BlockSpec(memory_space=pltpu.SEMAPHORE)
EFFECT = pltpu.SideEffectType.DATAFLOW_SIDE_EFFECTING
N_PEER = 3


def _half(ref, c):
    r, cols = ref.shape
    tile_rows = 32 // jnp.dtype(ref.dtype).itemsize
    if (r // 2) % tile_rows == 0:
        return ref.at[pl.ds(c * (r // 2), r // 2)]
    assert (cols // 2) % 128 == 0, ref.shape
    return ref.at[:, pl.ds(c * (cols // 2), cols // 2)]


def _gather_plan(srcs, lands, send, recv):
    x, y, c = _place()
    return [pltpu.make_async_remote_copy(src_ref=_half(srcs[i], c), dst_ref=_half(lands[i].at[2 * x + y], c),
                                         send_sem=send.at[N_PEER * i + k], recv_sem=recv.at[N_PEER * i + k],
                                         device_id=peer, device_id_type=MESH_ID)
            for i in range(len(srcs)) for k, (peer, _) in enumerate(_chip_peers(x, y, c))]


def _relay_plan(srcs, lands, send, recv):
    x, y, c = _place()
    return [pltpu.make_async_remote_copy(src_ref=_half(lands[i].at[idx], c), dst_ref=_half(lands[i].at[idx], c),
                                         send_sem=send.at[N_PEER * i + k], recv_sem=recv.at[N_PEER * i + k],
                                         device_id=(x, y, 1 - c), device_id_type=MESH_ID)
            for i in range(len(srcs)) for k, (_, idx) in enumerate(_chip_peers(x, y, c))]


def _scatter_plan(srcs, lands, send, recv):
    x, y, c = _place()
    return [pltpu.make_async_remote_copy(src_ref=srcs[i].at[idx], dst_ref=lands[i].at[k], send_sem=send.at[N_PEER * i + k],
                                         recv_sem=recv.at[N_PEER * i + k], device_id=peer, device_id_type=MESH_ID)
            for i in range(len(srcs)) for k, (peer, idx) in enumerate(_chip_peers(x, y, c))]


def _swap_plan(srcs, lands, send, recv):
    x, y, c = _place()
    return [pltpu.make_async_remote_copy(src_ref=srcs[i], dst_ref=lands[i], send_sem=send.at[N_PEER * i],
                                         recv_sem=recv.at[N_PEER * i], device_id=(x, y, 1 - c), device_id_type=MESH_ID)
            for i in range(len(srcs))]


def copies_start(plan, srcs, lands, after, *, name):
    n = len(srcs)
    both = list(srcs) + list(lands)

    def body(*refs):
        src_refs, land_refs = refs[:n], refs[n:2 * n]
        send, recv = refs[2 * n + 1], refs[2 * n + 2]
        for cp in plan(src_refs, land_refs, send, recv):
            cp.start()
        refs[-1][...] = jnp.zeros_like(refs[-1])

    res = pl.pallas_call(
        body, name=name,
        out_shape=(pltpu.SemaphoreType.DMA((n * N_PEER,)), pltpu.SemaphoreType.DMA((n * N_PEER,)),
                   *[pltpu.HBM(a.shape, a.dtype) for a in both], jax.ShapeDtypeStruct((8, 128), F32)),
        in_specs=[HBM] * (2 * n) + [ANY],
        out_specs=(SEM, SEM, *[HBM] * (2 * n), pl.BlockSpec(memory_space=pltpu.VMEM)),
        input_output_aliases={i: 2 + i for i in range(2 * n)},
        compiler_params=pltpu.CompilerParams(has_side_effects=EFFECT))(
            *[pltpu.with_memory_space_constraint(a, pltpu.HBM) for a in both], after)
    return {"send": res[0], "recv": res[1], "srcs": list(res[2:2 + n]), "lands": list(res[2 + n:2 + 2 * n]),
            "token": res[-1]}


def copies_relay(arrived_plan, next_plan, started, after, *, name):
    srcs, lands = started["srcs"], started["lands"]
    n = len(srcs)
    both = srcs + lands

    def body(*refs):
        src_refs, land_refs = refs[:n], refs[n:2 * n]
        send1, recv1 = refs[2 * n], refs[2 * n + 1]
        send2, recv2 = refs[2 * n + 3], refs[2 * n + 4]
        for cp in arrived_plan(src_refs, land_refs, send1, recv1):
            cp.wait_send()
            cp.wait_recv()
        for cp in next_plan(src_refs, land_refs, send2, recv2):
            cp.start()
        refs[-1][...] = jnp.zeros_like(refs[-1])

    res = pl.pallas_call(
        body, name=name,
        out_shape=(pltpu.SemaphoreType.DMA((n * N_PEER,)), pltpu.SemaphoreType.DMA((n * N_PEER,)),
                   *[pltpu.HBM(a.shape, a.dtype) for a in both], jax.ShapeDtypeStruct((8, 128), F32)),
        in_specs=[HBM] * (2 * n) + [SEM, SEM, ANY],
        out_specs=(SEM, SEM, *[HBM] * (2 * n), pl.BlockSpec(memory_space=pltpu.VMEM)),
        input_output_aliases={i: 2 + i for i in range(2 * n)},
        compiler_params=pltpu.CompilerParams(has_side_effects=EFFECT))(*both, started["send"], started["recv"], after)
    return {"send": res[0], "recv": res[1], "srcs": list(res[2:2 + n]), "lands": list(res[2 + n:2 + 2 * n]),
            "token": res[-1]}


def copies_wait(plan, started, after, *, name):
    srcs, lands = started["srcs"], started["lands"]
    n = len(srcs)
    both = srcs + lands

    def body(*refs):
        src_refs, land_refs = refs[:n], refs[n:2 * n]
        send, recv = refs[2 * n], refs[2 * n + 1]
        for cp in plan(src_refs, land_refs, send, recv):
            cp.wait_send()
            cp.wait_recv()

    res = pl.pallas_call(
        body, name=name, out_shape=tuple(pltpu.HBM(a.shape, a.dtype) for a in both),
        in_specs=[HBM] * (2 * n) + [SEM, SEM, ANY], out_specs=(HBM,) * (2 * n),
        input_output_aliases={i: i for i in range(2 * n)},
        compiler_params=pltpu.CompilerParams(has_side_effects=EFFECT))(*both, started["send"], started["recv"], after)
    return list(res[:n]), list(res[n:])


def allgather_small(small, *, name):
    def body(small_ref, out_ref, send, recv, loc):
        x, y, c = _place()
        dev = 4 * x + 2 * y + c
        local = pltpu.make_async_copy(small_ref, out_ref.at[dev], loc)
        remote = []
        for r in range(1, N_DEV):
            fx, fy, fc = (r >> 2) & 1, (r >> 1) & 1, r & 1
            peer = (1 - x if fx else x, 1 - y if fy else y, 1 - c if fc else c)
            remote.append(pltpu.make_async_remote_copy(
                src_ref=small_ref, dst_ref=out_ref.at[dev], send_sem=send.at[r - 1], recv_sem=recv.at[r - 1],
                device_id=peer, device_id_type=MESH_ID))
        local.start()
        for cp in remote:
            cp.start()
        for cp in remote:
            cp.wait()
        local.wait()

    return pl.pallas_call(
        body, name=name, in_specs=[ANY], out_specs=ANY,
        out_shape=jax.ShapeDtypeStruct((N_DEV,) + small.shape, small.dtype),
        scratch_shapes=[pltpu.SemaphoreType.DMA((N_DEV - 1,)), pltpu.SemaphoreType.DMA((N_DEV - 1,)),
                        pltpu.SemaphoreType.DMA(())])(small)


RED_ROWS = 256
RED_COLS = 256


def _red_block(r, c):
    if r % RED_ROWS == 0:
        return RED_ROWS, c
    if c > RED_COLS and c % RED_COLS == 0:
        return r, RED_COLS
    return r, c


def sum_chips(by_owner, me, got, *, name):
    _, r, c = by_owner.shape
    rb, cb = _red_block(r, c)

    def body(me_ref, o_ref, a_ref, b_ref, c_ref, out_ref):
        total = ((o_ref[...].astype(F32) + a_ref[...].astype(F32)) + b_ref[...].astype(F32)) + c_ref[...].astype(F32)
        out_ref[...] = total.astype(BF16)

    gk = lambda k: pl.BlockSpec((None, rb, cb), lambda i, j, me_ref: (k, i, j))
    grid_spec = pltpu.PrefetchScalarGridSpec(
        num_scalar_prefetch=1, grid=(r // rb, c // cb),
        in_specs=[pl.BlockSpec((None, rb, cb), lambda i, j, me_ref: (me_ref[0], i, j)), gk(0), gk(1), gk(2)],
        out_specs=pl.BlockSpec((rb, cb), lambda i, j, me_ref: (i, j)))
    return pl.pallas_call(
        body, name=name, grid_spec=grid_spec, out_shape=jax.ShapeDtypeStruct((r, c), BF16),
        compiler_params=_params(("parallel", "parallel")))(me, by_owner, got, got, got)


def sum_devices(small_all, *, name):
    _, p, c = small_all.shape

    def body(a_ref, out_ref):
        acc = a_ref[0]
        for d in range(1, N_DEV):
            acc = acc + a_ref[d]
        out_ref[...] = acc

    return pl.pallas_call(
        body, name=name, grid=(1,), in_specs=[pl.BlockSpec((N_DEV, p, c), lambda i: (0, 0, 0))],
        out_specs=pl.BlockSpec((p, c), lambda i: (0, 0)), out_shape=jax.ShapeDtypeStruct((p, c), F32),
        compiler_params=_params(("arbitrary",)))(small_all)


def adamw(parts, w, m, v, *, name):
    nl, r, c = w.shape
    assert len(parts) == nl
    npart = len(parts[0])
    rb, cb = _red_block(r, c)
    flat = [a for layer in parts for a in layer]

    def body(*refs):
        p_refs, (w_ref, m_ref, v_ref) = refs[:nl * npart], refs[nl * npart:nl * npart + 3]
        g_ref, d_ref, nm_ref, nv_ref = refs[nl * npart + 3:]
        layer = pl.program_id(0)
        grad = None
        for l in range(nl):
            gl = p_refs[l * npart][...].astype(F32)
            for j in range(1, npart):
                gl = gl + p_refs[l * npart + j][...].astype(F32)
            grad = gl if grad is None else jnp.where(layer == l, gl, grad)
        wv, mv, vv = w_ref[...], m_ref[...], v_ref[...]
        nm = ADAM_B1 * mv + (1.0 - ADAM_B1) * grad
        nv = ADAM_B2 * vv + (1.0 - ADAM_B2) * (grad * grad)
        m_hat = nm / (1.0 - ADAM_B1 ** ADAM_STEP)
        v_hat = nv / (1.0 - ADAM_B2 ** ADAM_STEP)
        g_ref[...] = grad
        d_ref[...] = -ADAM_LR * (m_hat / (jnp.sqrt(v_hat) + ADAM_EPS) + ADAM_WD * wv)
        nm_ref[...] = nm
        nv_ref[...] = nv

    pspec = pl.BlockSpec((rb, cb), lambda l, i, j: (i, j))
    wspec = pl.BlockSpec((None, rb, cb), lambda l, i, j: (l, i, j))
    osh = jax.ShapeDtypeStruct((nl, r, c), F32)
    return pl.pallas_call(
        body, name=name, grid=(nl, r // rb, c // cb), in_specs=[pspec] * (nl * npart) + [wspec] * 3,
        out_specs=[wspec] * 4, out_shape=[osh] * 4,
        compiler_params=_params(("parallel", "parallel", "parallel")))(*flat, w, m, v)


def _rows128(a):
    flat = a.reshape(-1)
    pad = (-flat.shape[0]) % 128
    return jnp.pad(flat, (0, pad)).reshape(-1, 128)


def _pack_rows(arrs, multiple=8):
    rows = jnp.concatenate([_rows128(a.astype(F32)) for a in arrs], axis=0)
    return jnp.pad(rows, ((0, (-rows.shape[0]) % multiple), (0, 0)))


def _unpack_rows(rows, shapes):
    out, r0 = [], 0
    for shp in shapes:
        size = 1
        for s in shp:
            size *= s
        nr = -(-size // 128)
        out.append(rows[r0:r0 + nr].reshape(-1)[:size].reshape(shp))
        r0 += nr
    return out


SMALL_LOCAL_GRADS = ["even_norm", "even_conv", "a_log", "dt_bias", "sinks", "onorm", "odd_norm", "odd_ln_g",
                     "odd_ln_b", "odd_w_s", "odd_b_s", "ffn_norm", "final_norm"]
BIG = ["even_w_in", "even_w_out", "odd_w_in", "odd_w_out", "ffn_w_gate", "ffn_w_up", "ffn_w_down"]
WEIGHTS = ["even_norm", "even_w_in", "even_conv", "even_a_log", "even_dt_bias", "even_sinks", "even_onorm",
           "even_w_out", "odd_norm", "odd_w_in", "odd_ln_g", "odd_ln_b", "odd_w_s", "odd_b_s", "odd_w_out",
           "ffn_norm", "ffn_w_gate", "ffn_w_up", "ffn_w_down", "final_norm"]
SMALL = [n for n in WEIGHTS if n not in BIG]


def kernel(x, even_norm, even_w_in, even_conv, even_a_log, even_dt_bias, even_sinks, even_onorm, even_w_out, odd_norm, odd_w_in, odd_ln_g, odd_ln_b, odd_w_s, odd_b_s, odd_w_out, ffn_norm, ffn_w_gate, ffn_w_up, ffn_w_down, final_norm, loss_target, m_even_norm, m_even_w_in, m_even_conv, m_even_a_log, m_even_dt_bias, m_even_sinks, m_even_onorm, m_even_w_out, m_odd_norm, m_odd_w_in, m_odd_ln_g, m_odd_ln_b, m_odd_w_s, m_odd_b_s, m_odd_w_out, m_ffn_norm, m_ffn_w_gate, m_ffn_w_up, m_ffn_w_down, m_final_norm, v_even_norm, v_even_w_in, v_even_conv, v_even_a_log, v_even_dt_bias, v_even_sinks, v_even_onorm, v_even_w_out, v_odd_norm, v_odd_w_in, v_odd_ln_g, v_odd_ln_b, v_odd_w_s, v_odd_b_s, v_odd_w_out, v_ffn_norm, v_ffn_w_gate, v_ffn_w_up, v_ffn_w_down, v_final_norm):
    args = dict(locals())
    wl = {n: args[n] for n in WEIGHTS}
    ml = {n: args["m_" + n] for n in WEIGHTS}
    vl = {n: args["v_" + n] for n in WEIGHTS}
    me = 2 * lax.axis_index("x") + lax.axis_index("y")

    def landing(a):
        return lax.dynamic_update_index_in_dim(lax.empty((N_SHARD,) + a.shape, a.dtype), a, me, 0)

    gather_groups = {
        "even_in": [even_w_in[0].T], "even_out": [even_w_out[0]],
        "ffn0": [ffn_w_gate[0], ffn_w_up[0], ffn_w_down[0]], "odd": [odd_w_in[0], odd_w_out[0]],
        "ffn1": [ffn_w_gate[1], ffn_w_up[1], ffn_w_down[1]],
    }
    gathering, after = {}, even_norm
    for group, arrs in gather_groups.items():
        srcs = [(a + after[0, 0] if gathering else a).astype(BF16) for a in arrs]
        if group == "even_in":
            srcs.append(_pack_rows([even_conv[0], odd_norm, odd_ln_g, odd_ln_b], multiple=16))
        gathering[group] = copies_start(_gather_plan, srcs, [landing(a) for a in srcs], after,
                                        name=f"gather_{group}_start")
        after = gathering[group]["token"]

    order = list(gather_groups)
    relayed, kept = {}, {}
    sinks_pad = jnp.pad(even_sinks, ((0, 0), (0, 128 - A_HEADS)))

    def relay(group, behind):
        relayed[group] = copies_relay(_gather_plan, _relay_plan, gathering[group], behind,
                                      name=f"gather_{group}_relay")
        return relayed[group]["token"][0:1, 0:1]

    def get(group, behind):
        if group not in relayed:
            relay(group, behind)
        _, lands = copies_wait(_relay_plan, relayed[group], behind, name=f"gather_{group}_wait")
        nxt = order.index(group) + 1
        tok = relay(order[nxt], lands[0]) if nxt < len(order) else jnp.zeros((1, 1), F32)
        if group == "even_in":
            parts = zip(*[_unpack_rows(lands[1][s], [(CONV_K, 768), (1, 512), (1, 512), (1, 512)])
                          for s in range(N_SHARD)])
            conv, onorm, lng, lnb = [jnp.concatenate(p, axis=1) for p in parts]
            w_in = jnp.pad(lands[0].reshape(EVEN_IN, D_MODEL), ((0, EVEN_IN_PAD - EVEN_IN), (0, 0)))
            kept["odd_ln_g"] = lng
            return {"even_w_in": w_in, "even_conv": conv + tok, "odd_norm": onorm, "odd_ln_b": lnb}
        if group == "even_out":
            return {"even_w_out": lands[0].reshape(D_MODEL, D_MODEL), "onorm": even_onorm + tok}
        if group == "odd":
            return {"odd_w_in": lands[0], "odd_w_out": lands[1].reshape(D_MODEL, D_MODEL),
                    "odd_ln_g": kept["odd_ln_g"] + tok}
        return {"gate": lands[0], "up": lands[1], "down": lands[2].reshape(D_FF, D_MODEL), "tok": tok}

    rows4 = lambda a: a.reshape(N_SHARD, a.shape[0] // N_SHARD, a.shape[1])
    scattering, small = {}, {}

    def emit(group, grads):
        behind = even_norm
        if group == "even_in":
            small["local"] = grads["small"]
            small["all"] = behind = allgather_small(_pack_rows([grads["small"][n] for n in SMALL_LOCAL_GRADS]),
                                                    name="allgather_small")
            srcs = [grads["even_w_in"][:EVEN_IN].reshape(N_SHARD, EVEN_IN // N_SHARD, D_MODEL)]
        elif group == "even_out":
            srcs = [rows4(grads["even_w_out"])]
        elif group == "odd":
            srcs = [grads["odd_w_in"], rows4(grads["odd_w_out"])]
        else:
            srcs = [grads["gate"], grads["up"], rows4(grads["down"])]
        lands = [lax.empty((N_PEER,) + a.shape[1:], a.dtype) for a in srcs]
        scattering[group] = copies_start(_scatter_plan, srcs, lands, behind, name=f"scatter_{group}_start")
        return scattering[group]["token"][0:1, 0:1]

    pad816 = lambda a: jnp.pad(a, ((0, 0), (B_HEADS, 128 - 2 * B_HEADS)))
    w = {
        "even_norm": even_norm + after[0:1, 0:1],
        "a_log": pad816(even_a_log), "dt_bias": pad816(even_dt_bias),
        "sinks": sinks_pad,
        "onorm": even_onorm,
        "odd_w_s": odd_w_s[0],
        "odd_b_s": jnp.pad(odd_b_s[0].T, ((0, 0), (0, 128 - C_GROUPS))),
        "ffn_norm": ffn_norm,
        "final_norm": final_norm[None],
    }
    loss_l, grad_x = _local_step(x[0], loss_target[0], w, get, emit)
    loss = lax.psum(loss_l[0, 0], ("x", "y", "c"))

    me1 = me.reshape(1).astype(jnp.int32)
    swapping = {}

    def reduce_chips(group, behind):
        srcs, lands = copies_wait(_scatter_plan, scattering[group], behind, name=f"scatter_{group}_wait")
        partial = [sum_chips(srcs[i], me1, lands[i], name=f"sum_chips_{group}_{i}") for i in range(len(srcs))]
        swapping[group] = copies_start(_swap_plan, partial, [lax.empty(p.shape, p.dtype) for p in partial],
                                       even_norm, name=f"swap_{group}_start")
        return swapping[group]["token"]

    def swapped(group, behind):
        mine, theirs = copies_wait(_swap_plan, swapping[group], behind, name=f"swap_{group}_wait")
        return list(zip(mine, theirs))

    behind = scattering["even_in"]["token"]
    for group in ("ffn1", "ffn0", "odd", "even_out"):
        behind = reduce_chips(group, behind)
    sums = {group: swapped(group, behind) for group in ("ffn1", "ffn0", "odd", "even_out")}
    outs = {}
    parts_of = {"even_w_out": [sums["even_out"][0]], "odd_w_in": [sums["odd"][0]], "odd_w_out": [sums["odd"][1]],
                "ffn_w_gate": [sums["ffn0"][0], sums["ffn1"][0]], "ffn_w_up": [sums["ffn0"][1], sums["ffn1"][1]],
                "ffn_w_down": [sums["ffn0"][2], sums["ffn1"][2]]}
    for n in parts_of:
        outs[n] = adamw(parts_of[n], wl[n], ml[n], vl[n], name=f"adamw_{n}")
    all_updated = sum(outs[n][1][0, 0, 0] for n in parts_of).reshape(1, 1)
    behind = reduce_chips("even_in", all_updated)
    flip = lambda a: jnp.transpose(a, (0, 2, 1))
    outs["even_w_in"] = [flip(o) for o in adamw([swapped("even_in", behind)[0]], flip(wl["even_w_in"]),
                                                flip(ml["even_w_in"]), flip(vl["even_w_in"]),
                                                name="adamw_even_w_in")]

    g = small["local"]
    small_sum = sum_devices(small["all"], name="sum_devices")
    sg = dict(zip(SMALL_LOCAL_GRADS, _unpack_rows(small_sum, [g[n].shape for n in SMALL_LOCAL_GRADS])))
    own_cols = lambda a, width: lax.dynamic_slice_in_dim(a, me * width, width, axis=a.ndim - 1)
    small_grads = {
        "even_norm": sg["even_norm"], "even_conv": own_cols(sg["even_conv"], 768)[None],
        "even_a_log": sg["a_log"][:, B_HEADS:2 * B_HEADS], "even_dt_bias": sg["dt_bias"][:, B_HEADS:2 * B_HEADS],
        "even_sinks": sg["sinks"][:, :A_HEADS], "even_onorm": sg["onorm"],
        "odd_norm": own_cols(sg["odd_norm"], 512), "odd_ln_g": own_cols(sg["odd_ln_g"], 512),
        "odd_ln_b": own_cols(sg["odd_ln_b"], 512), "odd_w_s": sg["odd_w_s"][None],
        "odd_b_s": sg["odd_b_s"][:, :C_GROUPS].T[None], "ffn_norm": sg["ffn_norm"], "final_norm": sg["final_norm"][0],
    }
    packed = [_pack_rows([d[n] for n in SMALL])[None] for d in (small_grads, wl, ml, vl)]
    small_out = adamw([(packed[0][0],)], packed[1], packed[2], packed[3], name="adamw_small")
    shapes = [wl[n].shape for n in SMALL]
    for j in range(4):
        for n, a in zip(SMALL, _unpack_rows(small_out[j][0], shapes)):
            outs.setdefault(n, [None] * 4)[j] = a

    return (loss, grad_x[None], *[outs[n][0] for n in WEIGHTS], *[outs[n][1] for n in WEIGHTS],
            *[outs[n][2] for n in WEIGHTS], *[outs[n][3] for n in WEIGHTS])
```

```python
import functools

import jax
import jax.numpy as jnp
from jax import lax
from jax.experimental import pallas as pl
from jax.experimental.pallas import tpu as pltpu

F32 = jnp.float32
BF16 = jnp.bfloat16
NEG_INF = float("-inf")

D_MODEL = 2048
A_HEADS, A_KV_HEADS, A_HEAD_DIM, WINDOW = 16, 2, 64, 128
B_HEADS, B_HEAD_DIM, CONV_K, DN_CHUNK = 8, 128, 4, 64
C_GROUPS, C_CHUNK = 8, 128
C_GROUP_DIM = D_MODEL // C_GROUPS
D_FF = 5632
EPS = 1e-6
A_Q = A_HEADS * A_HEAD_DIM
A_KV = A_KV_HEADS * A_HEAD_DIM
B_W = B_HEADS * B_HEAD_DIM
EVEN_IN = A_Q + 2 * A_KV + 4 * B_W + 2 * B_HEADS
EVEN_IN_PAD = 5632
COL_KV = A_Q
COL_QKVB = A_Q + 2 * A_KV
COL_Z = COL_QKVB + 3 * B_W
COL_GATE = COL_Z + B_W
N_SHARD = 4

ADAM_LR, ADAM_B1, ADAM_B2, ADAM_EPS, ADAM_WD, ADAM_STEP = 0.001, 0.9, 0.999, 1e-08, 0.01, 10

VMEM_LIMIT_V7X = 56 * 1024 * 1024
MXU_COLS = 256
MESH_ID = pl.DeviceIdType.MESH


def _params(sem=None):
    return pltpu.CompilerParams(dimension_semantics=sem, vmem_limit_bytes=VMEM_LIMIT_V7X)


def _sigmoid(x):
    return 1.0 / (1.0 + jnp.exp(-x))


def _silu(x):
    return x * _sigmoid(x)


def _dsilu(x):
    s = _sigmoid(x)
    return s * (1.0 + x * (1.0 - s))


def _gelu(x):
    return 0.5 * x * (1.0 + lax.erf(x * 0.7071067811865476))


def _dgelu(x):
    return 0.5 * (1.0 + lax.erf(x * 0.7071067811865476)) + x * jnp.exp(-0.5 * x * x) * 0.3989422804014327


def _dot(a, b, dims):
    if a.ndim == 3:
        (ca,), (cb,) = dims
        return lax.dot_general(a, b, (((ca + 1,), (cb + 1,)), ((0,), (0,))), preferred_element_type=F32)
    return lax.dot_general(a, b, (dims, ((), ())), preferred_element_type=F32)


NN = ((1,), (0,))
NT = ((1,), (1,))
TN = ((0,), (0,))


def _as3(b):
    return b if b.ndim == 3 else b[None]


def _accumulate(step, nsteps, accs, products, finish):
    if nsteps == 1:
        finish(products())
        return

    @pl.when(step == 0)
    def _():
        for acc, p in zip(accs, products()):
            acc[...] = p

    if nsteps > 2:
        @pl.when((step > 0) & (step < nsteps - 1))
        def _():
            for acc, p in zip(accs, products()):
                acc[...] += p

    @pl.when(step == nsteps - 1)
    def _():
        finish(tuple(acc[...] + p for acc, p in zip(accs, products())))


def mm_nn(a, b, *, tm, tn, tk, out_dtype, name, res=None):
    b3 = _as3(b)
    m, k = a.shape
    s, k2, ns = b3.shape
    assert k2 == k and m % tm == 0 and ns % tn == 0 and k % tk == 0, (a.shape, b3.shape, tm, tn, tk)
    nps, nk = ns // tn, k // tk

    def body(*refs):
        if res is None:
            a_ref, b_ref, o_ref, acc = refs
        else:
            a_ref, b_ref, r_ref, o_ref, acc = refs
        def finish(tiles):
            r = tiles[0] if res is None else tiles[0] + r_ref[...].astype(F32)
            o_ref[...] = r.astype(out_dtype)

        _accumulate(pl.program_id(2), nk, (acc,),
                    lambda: (_dot(a_ref[...].astype(BF16), b_ref[...].astype(BF16), NN),), finish)

    in_specs = [pl.BlockSpec((tm, tk), lambda i, j, kk: (i, kk)),
                pl.BlockSpec((None, tk, tn), lambda i, j, kk: (j // nps, kk, j % nps))]
    args = [a, b3]
    if res is not None:
        in_specs.append(pl.BlockSpec((tm, tn), lambda i, j, kk: (i, j)))
        args.append(res)
    return pl.pallas_call(
        body, name=name, grid=(m // tm, s * nps, nk), in_specs=in_specs,
        out_specs=pl.BlockSpec((tm, tn), lambda i, j, kk: (i, j)),
        out_shape=jax.ShapeDtypeStruct((m, s * ns), out_dtype),
        scratch_shapes=[pltpu.VMEM((tm, tn), F32)],
        compiler_params=_params(("parallel", "parallel", "arbitrary")))(*args)


def mm_nt(a, b, *, tm, tn, tk, out_dtype, name, res=None):
    b3 = _as3(b)
    m, n = a.shape
    s, k, ns = b3.shape
    assert n == s * ns and m % tm == 0 and k % tn == 0 and ns % tk == 0, (a.shape, b3.shape, tm, tn, tk)
    rps = ns // tk
    nr = s * rps

    def body(*refs):
        if res is None:
            a_ref, b_ref, o_ref, acc = refs
        else:
            a_ref, b_ref, r_ref, o_ref, acc = refs
        def finish(tiles):
            r = tiles[0] if res is None else tiles[0] + r_ref[...].astype(F32)
            o_ref[...] = r.astype(out_dtype)

        _accumulate(pl.program_id(2), nr, (acc,),
                    lambda: (_dot(a_ref[...].astype(BF16), b_ref[...].astype(BF16), NT),), finish)

    in_specs = [pl.BlockSpec((tm, tk), lambda i, j, r: (i, r)),
                pl.BlockSpec((None, tn, tk), lambda i, j, r: (r // rps, j, r % rps))]
    args = [a, b3]
    if res is not None:
        in_specs.append(pl.BlockSpec((tm, tn), lambda i, j, r: (i, j)))
        args.append(res)
    return pl.pallas_call(
        body, name=name, grid=(m // tm, k // tn, nr), in_specs=in_specs,
        out_specs=pl.BlockSpec((tm, tn), lambda i, j, r: (i, j)),
        out_shape=jax.ShapeDtypeStruct((m, k), out_dtype),
        scratch_shapes=[pltpu.VMEM((tm, tn), F32)],
        compiler_params=_params(("parallel", "parallel", "arbitrary")))(*args)


def mm_tn(a, b, *, shards, tm, tn, tk, out_dtype, name):
    m, k = a.shape
    m2, n = b.shape
    ns = n // shards
    assert m2 == m and n == shards * ns and m % tm == 0 and k % tk == 0 and ns % tn == 0, (a.shape, b.shape)
    nps, nm = ns // tn, m // tm

    def body(a_ref, b_ref, o_ref, acc):
        def finish(tiles):
            o_ref[...] = tiles[0].astype(out_dtype)

        _accumulate(pl.program_id(2), nm, (acc,),
                    lambda: (_dot(a_ref[...].astype(BF16), b_ref[...].astype(BF16), TN),), finish)

    return pl.pallas_call(
        body, name=name, grid=(k // tk, shards * nps, nm),
        in_specs=[pl.BlockSpec((tm, tk), lambda i, j, mi: (mi, i)),
                  pl.BlockSpec((tm, tn), lambda i, j, mi: (mi, j))],
        out_specs=pl.BlockSpec((None, tk, tn), lambda i, j, mi: (j // nps, i, j % nps)),
        out_shape=jax.ShapeDtypeStruct((shards, k, ns), out_dtype),
        scratch_shapes=[pltpu.VMEM((tk, tn), F32)],
        compiler_params=_params(("parallel", "parallel", "arbitrary")))(a, b)


def mm_gate_up(hn, wg, wu, *, tm, tn, tk, name):
    wg3, wu3 = _as3(wg), _as3(wu)
    m, k = hn.shape
    s, _, ns = wg3.shape
    assert m % tm == 0 and ns % tn == 0 and k % tk == 0
    nps, nk = ns // tn, k // tk

    def body(a_ref, g_ref, u_ref, og_ref, ou_ref, oa_ref, accg, accu):
        def products():
            a = a_ref[...].astype(BF16)
            return _dot(a, g_ref[...].astype(BF16), NN), _dot(a, u_ref[...].astype(BF16), NN)

        def finish(tiles):
            g, u = tiles
            og_ref[...] = g.astype(BF16)
            ou_ref[...] = u.astype(BF16)
            oa_ref[...] = (_silu(g) * u).astype(BF16)

        _accumulate(pl.program_id(2), nk, (accg, accu), products, finish)

    wspec = pl.BlockSpec((None, tk, tn), lambda i, j, kk: (j // nps, kk, j % nps))
    ospec = pl.BlockSpec((tm, tn), lambda i, j, kk: (i, j))
    osh = jax.ShapeDtypeStruct((m, s * ns), BF16)
    return pl.pallas_call(
        body, name=name, grid=(m // tm, s * nps, nk),
        in_specs=[pl.BlockSpec((tm, tk), lambda i, j, kk: (i, kk)), wspec, wspec],
        out_specs=[ospec, ospec, ospec], out_shape=[osh, osh, osh],
        scratch_shapes=[pltpu.VMEM((tm, tn) if nk > 1 else (8, 128), F32)] * 2,
        compiler_params=_params(("parallel", "parallel", "arbitrary")))(hn, wg3, wu3)


def mm_down_bwd(dh, wd, gate, up, *, tm, tn, tk, name):
    m, d = dh.shape
    f, d2 = wd.shape
    assert d2 == d and m % tm == 0 and f % tn == 0 and tk == d and tn % MXU_COLS == 0

    def body(a_ref, b_ref, g_ref, u_ref, og_ref, ou_ref):
        a = a_ref[...].astype(BF16)
        for jj in range(tn // MXU_COLS):
            sl = slice(jj * MXU_COLS, (jj + 1) * MXU_COLS)
            da = _dot(a, b_ref[sl, :].astype(BF16), NT)
            g, u = g_ref[:, sl].astype(F32), u_ref[:, sl].astype(F32)
            s = _sigmoid(g)
            og_ref[:, sl] = (da * u * (s * (1.0 + g * (1.0 - s)))).astype(BF16)
            ou_ref[:, sl] = (da * (g * s)).astype(BF16)

    ospec = pl.BlockSpec((tm, tn), lambda i, j: (i, j))
    osh = jax.ShapeDtypeStruct((m, f), BF16)
    return pl.pallas_call(
        body, name=name, grid=(m // tm, f // tn),
        in_specs=[pl.BlockSpec((tm, tk), lambda i, j: (i, 0)),
                  pl.BlockSpec((tn, tk), lambda i, j: (j, 0)), ospec, ospec],
        out_specs=[ospec, ospec], out_shape=[osh, osh],
        compiler_params=_params(("parallel", "parallel")))(dh, wd, gate, up)


ROWS = 512


def rms_fwd(x, g, *, name):
    t, d = x.shape

    def body(x_ref, g_ref, o_ref):
        xv = x_ref[...]
        r = lax.rsqrt(jnp.mean(xv * xv, axis=-1, keepdims=True) + EPS)
        o_ref[...] = (xv * r * g_ref[...]).astype(BF16)

    return pl.pallas_call(
        body, name=name, grid=(t // ROWS,),
        in_specs=[pl.BlockSpec((ROWS, d), lambda i: (i, 0)), pl.BlockSpec((1, d), lambda i: (0, 0))],
        out_specs=pl.BlockSpec((ROWS, d), lambda i: (i, 0)),
        out_shape=jax.ShapeDtypeStruct((t, d), BF16), compiler_params=_params(("parallel",)))(x, g)


def dgrad_rms_bwd(a, b, form, x, g, dres, *, tm, tk, name, res=None):
    m, d = x.shape
    b3 = _as3(b)
    if form == NN:
        steps = a.shape[1] // tk
        a_spec = pl.BlockSpec((tm, tk), lambda i, r: (i, r))
        b_spec = pl.BlockSpec((None, tk, d), lambda i, r: (0, r, 0))
    else:
        s, d2, ns = b3.shape
        assert d2 == d and ns % tk == 0
        rps = ns // tk
        steps = s * rps
        a_spec = pl.BlockSpec((tm, tk), lambda i, r: (i, r))
        b_spec = pl.BlockSpec((None, d, tk), lambda i, r: (r // rps, 0, r % rps))
    assert m % tm == 0 and a.shape[1] == steps * tk

    def body(*refs):
        if res is None:
            a_ref, b_ref, x_ref, g_ref, dr_ref, dx_ref, dg_ref, acc = refs
        else:
            a_ref, b_ref, r_ref, x_ref, g_ref, dr_ref, dx_ref, dg_ref, acc = refs

        @pl.when((pl.program_id(0) == 0) & (pl.program_id(1) == 0))
        def _():
            dg_ref[...] = jnp.zeros_like(dg_ref)

        def finish(tiles):
            dyv = tiles[0] if res is None else tiles[0] + r_ref[...]
            xv = x_ref[...]
            r = lax.rsqrt(jnp.mean(xv * xv, axis=-1, keepdims=True) + EPS)
            dyg = dyv * g_ref[...]
            dx_ref[...] = r * dyg - xv * (r * r * r) * jnp.mean(dyg * xv, axis=-1, keepdims=True) + dr_ref[...]
            dg_ref[...] += jnp.sum(dyv * xv * r, axis=0, keepdims=True)

        _accumulate(pl.program_id(1), steps, (acc,),
                    lambda: (_dot(a_ref[...].astype(BF16), b_ref[...].astype(BF16), form),), finish)

    row = pl.BlockSpec((tm, d), lambda i, r: (i, 0))
    vec = pl.BlockSpec((1, d), lambda i, r: (0, 0))
    in_specs = [a_spec, b_spec] + ([row] if res is not None else []) + [row, vec, row]
    args = [a, b3] + ([res] if res is not None else []) + [x, g, dres]
    return pl.pallas_call(
        body, name=name, grid=(m // tm, steps), in_specs=in_specs, out_specs=[row, vec],
        out_shape=[jax.ShapeDtypeStruct((m, d), F32), jax.ShapeDtypeStruct((1, d), F32)],
        scratch_shapes=[pltpu.VMEM((tm, d), F32)],
        compiler_params=_params(("arbitrary", "arbitrary")))(*args)


def loss_head(h, g, target, *, name):
    t, d = h.shape

    def body(x_ref, g_ref, t_ref, loss_ref, dx_ref, dg_ref):
        @pl.when(pl.program_id(0) == 0)
        def _():
            dg_ref[...] = jnp.zeros_like(dg_ref)
            loss_ref[...] = jnp.zeros_like(loss_ref)

        xv, gv = x_ref[...], g_ref[...]
        r = lax.rsqrt(jnp.mean(xv * xv, axis=-1, keepdims=True) + EPS)
        e = xv * r * gv - t_ref[...]
        loss_ref[...] += 0.5 * jnp.sum(jnp.mean(e * e, axis=-1, keepdims=True), axis=0, keepdims=True)
        dyv = e * (1.0 / d)
        dyg = dyv * gv
        dx_ref[...] = r * dyg - xv * (r * r * r) * jnp.mean(dyg * xv, axis=-1, keepdims=True)
        dg_ref[...] += jnp.sum(dyv * xv * r, axis=0, keepdims=True)

    row = pl.BlockSpec((ROWS, d), lambda i: (i, 0))
    vec = pl.BlockSpec((1, d), lambda i: (0, 0))
    return pl.pallas_call(
        body, name=name, grid=(t // ROWS,), in_specs=[row, vec, row],
        out_specs=[pl.BlockSpec((1, 128), lambda i: (0, 0)), row, vec],
        out_shape=[jax.ShapeDtypeStruct((1, 128), F32), jax.ShapeDtypeStruct((t, d), F32),
                   jax.ShapeDtypeStruct((1, d), F32)],
        compiler_params=_params(("arbitrary",)))(h, g, target)


def _tril_mask():
    r = lax.broadcasted_iota(jnp.int32, (C_CHUNK, C_CHUNK), 0)
    c = lax.broadcasted_iota(jnp.int32, (C_CHUNK, C_CHUNK), 1)
    return r >= c


def _layer_norm_parts(v):
    mu = jnp.mean(v, axis=-1, keepdims=True)
    vc = v - mu
    rstd = lax.rsqrt(jnp.mean(vc * vc, axis=-1, keepdims=True) + EPS)
    return vc * rstd, rstd


def gmlp_fwd(zpre, ln_g, ln_b, ws, bs_t, *, name):
    t = zpre.shape[0]
    d = D_MODEL

    def body(zu_ref, zv_ref, g_ref, b_ref, ws_ref, bs_ref, o_ref):
        u = _gelu(zu_ref[...])
        vhat, _ = _layer_norm_parts(_gelu(zv_ref[...]))
        vln = (vhat * g_ref[...] + b_ref[...]).astype(BF16)
        mask = _tril_mask()
        for gi in range(C_GROUPS):
            sl = slice(gi * C_GROUP_DIM, (gi + 1) * C_GROUP_DIM)
            w = jnp.where(mask, ws_ref[gi], 0.0).astype(BF16)
            mixed = _dot(w, vln[:, sl], NN) + bs_ref[:, gi:gi + 1]
            o_ref[:, sl] = (u[:, sl] * mixed).astype(BF16)

    vec = pl.BlockSpec((1, d), lambda i: (0, 0))
    return pl.pallas_call(
        body, name=name, grid=(t // C_CHUNK,),
        in_specs=[pl.BlockSpec((C_CHUNK, d), lambda i: (i, 0)), pl.BlockSpec((C_CHUNK, d), lambda i: (i, 1)),
                  vec, vec, pl.BlockSpec((C_GROUPS, C_CHUNK, C_CHUNK), lambda i: (0, 0, 0)),
                  pl.BlockSpec((C_CHUNK, 128), lambda i: (0, 0))],
        out_specs=pl.BlockSpec((C_CHUNK, d), lambda i: (i, 0)),
        out_shape=jax.ShapeDtypeStruct((t, d), BF16), compiler_params=_params(("parallel",)))(
            zpre, zpre, ln_g, ln_b, ws, bs_t)


def gmlp_bwd(zpre, dgated, ln_g, ln_b, ws, bs_t, *, name):
    t = zpre.shape[0]
    d = D_MODEL

    def body(zu_ref, zv_ref, dg_ref, g_ref, b_ref, ws_ref, bs_ref, dz_ref, dws_ref, dbs_ref, dlg_ref, dlb_ref):
        @pl.when(pl.program_id(0) == 0)
        def _():
            dws_ref[...] = jnp.zeros_like(dws_ref)
            dbs_ref[...] = jnp.zeros_like(dbs_ref)
            dlg_ref[...] = jnp.zeros_like(dlg_ref)
            dlb_ref[...] = jnp.zeros_like(dlb_ref)

        zu, zv = zu_ref[...], zv_ref[...]
        u = _gelu(zu)
        vhat, rstd = _layer_norm_parts(_gelu(zv))
        gam = g_ref[...]
        vln = (vhat * gam + b_ref[...]).astype(BF16)
        dgt = dg_ref[...].astype(F32)
        mask = _tril_mask()
        lane = lax.broadcasted_iota(jnp.int32, (C_CHUNK, 128), 1)
        dbs = jnp.zeros((C_CHUNK, 128), F32)
        du_parts, dvln_parts = [], []
        for gi in range(C_GROUPS):
            sl = slice(gi * C_GROUP_DIM, (gi + 1) * C_GROUP_DIM)
            w = jnp.where(mask, ws_ref[gi], 0.0).astype(BF16)
            mixed = _dot(w, vln[:, sl], NN) + bs_ref[:, gi:gi + 1]
            du_parts.append(dgt[:, sl] * mixed)
            dmixed = dgt[:, sl] * u[:, sl]
            dmb = dmixed.astype(BF16)
            dws_ref[gi] += jnp.where(mask, _dot(dmb, vln[:, sl], NT), 0.0)
            dbs = dbs + jnp.where(lane == gi, jnp.sum(dmixed, axis=-1, keepdims=True), 0.0)
            dvln_parts.append(_dot(w, dmb, TN))
        dbs_ref[...] += dbs
        du = jnp.concatenate(du_parts, axis=-1)
        dvln = jnp.concatenate(dvln_parts, axis=-1)
        dlg_ref[...] += jnp.sum(dvln * vhat, axis=0, keepdims=True)
        dlb_ref[...] += jnp.sum(dvln, axis=0, keepdims=True)
        dvhat = dvln * gam
        dv = rstd * (dvhat - jnp.mean(dvhat, axis=-1, keepdims=True)
                     - vhat * jnp.mean(dvhat * vhat, axis=-1, keepdims=True))
        dz_ref[:, :d] = (du * _dgelu(zu)).astype(BF16)
        dz_ref[:, d:] = (dv * _dgelu(zv)).astype(BF16)

    vec = pl.BlockSpec((1, d), lambda i: (0, 0))
    wsp = pl.BlockSpec((C_GROUPS, C_CHUNK, C_CHUNK), lambda i: (0, 0, 0))
    bsp = pl.BlockSpec((C_CHUNK, 128), lambda i: (0, 0))
    return pl.pallas_call(
        body, name=name, grid=(t // C_CHUNK,),
        in_specs=[pl.BlockSpec((C_CHUNK, d), lambda i: (i, 0)), pl.BlockSpec((C_CHUNK, d), lambda i: (i, 1)),
                  pl.BlockSpec((C_CHUNK, d), lambda i: (i, 0)), vec, vec, wsp, bsp],
        out_specs=[pl.BlockSpec((C_CHUNK, 2 * d), lambda i: (i, 0)), wsp, bsp, vec, vec],
        out_shape=[jax.ShapeDtypeStruct((t, 2 * d), BF16), jax.ShapeDtypeStruct((C_GROUPS, C_CHUNK, C_CHUNK), F32),
                   jax.ShapeDtypeStruct((C_CHUNK, 128), F32), jax.ShapeDtypeStruct((1, d), F32),
                   jax.ShapeDtypeStruct((1, d), F32)],
        compiler_params=_params(("arbitrary",)))(zpre, zpre, dgated, ln_g, ln_b, ws, bs_t)


ATT_SCALE = A_HEAD_DIM ** -0.5
PAIRS = A_HEADS // 2
PAIRS_PER_KV = PAIRS // A_KV_HEADS


def _att_padded(tile):
    lo = lax.broadcasted_iota(jnp.int32, tile.shape, 1) < A_HEAD_DIM
    rolled = pltpu.roll(tile, A_HEAD_DIM, 1)
    zero = jnp.zeros_like(tile)
    return {(0, 0): jnp.where(lo, tile, zero).astype(BF16), (0, 1): jnp.where(lo, zero, rolled).astype(BF16),
            (1, 0): jnp.where(lo, rolled, zero).astype(BF16), (1, 1): jnp.where(lo, zero, tile).astype(BF16)}


def _att_valid(n):
    r = lax.broadcasted_iota(jnp.int32, (WINDOW, 2 * WINDOW), 0)
    c = lax.broadcasted_iota(jnp.int32, (WINDOW, 2 * WINDOW), 1)
    rel = r + WINDOW - c
    return (rel >= 0) & (rel < WINDOW) & ((c >= WINDOW) | (n > 0))


def _att_probs(qp, kpad, sink, valid):
    s = jnp.where(valid, _dot(qp, kpad, NT), NEG_INF)
    m = jnp.maximum(jnp.max(s, axis=-1, keepdims=True), sink)
    p = jnp.exp(s - m)
    e_sink = jnp.exp(sink - m)
    inv = 1.0 / (jnp.sum(p, axis=-1, keepdims=True) + e_sink)
    return p * inv, e_sink * inv


ATT_BLOCKS = 4


def _att_operands(q_ref, kvc_ref, kvp_ref, s_ref, step, nb):
    w = WINDOW
    kvs = [kvp_ref[...]] + [kvc_ref[b * w:(b + 1) * w, :] for b in range(nb)]
    key = lambda h: ((h // 2) // PAIRS_PER_KV, h % 2)
    qs, ks, vs, valids = [], [], [], []
    for b in range(nb):
        kv = jnp.concatenate([kvs[b], kvs[b + 1]], axis=0)
        kpad, vpad = _att_padded(kv[:, :128]), _att_padded(kv[:, 128:])
        pairs = [(q_ref[b * w:(b + 1) * w, j * 128:(j + 1) * 128] * ATT_SCALE).astype(BF16) for j in range(PAIRS)]
        qs += [pairs[h // 2] for h in range(A_HEADS)]
        ks += [kpad[key(h)] for h in range(A_HEADS)]
        vs += [vpad[key(h)] for h in range(A_HEADS)]
        valids += [_att_valid(step * nb + b)] * A_HEADS
    sink = jnp.stack([s_ref[:, h:h + 1] for h in range(A_HEADS)] * nb)
    return jnp.stack(qs), jnp.stack(ks), jnp.stack(vs), sink, jnp.stack(valids)


def _att_specs(nb):
    rows = nb * WINDOW
    return [pl.BlockSpec((rows, A_Q), lambda n: (n, 0)),
            pl.BlockSpec((rows, 2 * A_KV), lambda n: (n, COL_KV // (2 * A_KV))),
            pl.BlockSpec((WINDOW, 2 * A_KV), lambda n: (jnp.maximum(nb * n - 1, 0), COL_KV // (2 * A_KV))),
            pl.BlockSpec((1, 128), lambda n: (0, 0))]


def att_fwd(proj, sinks, *, name):
    t = proj.shape[0]
    nb = min(ATT_BLOCKS, t // WINDOW)
    rows = nb * WINDOW

    def body(q_ref, kvc_ref, kvp_ref, s_ref, o_ref):
        q, k, v, sink, valid = _att_operands(q_ref, kvc_ref, kvp_ref, s_ref, pl.program_id(0), nb)
        w, _ = _att_probs(q, k, sink, valid)
        o = _dot(w.astype(BF16), v, NN)
        for b in range(nb):
            for j in range(PAIRS):
                pair = o[b * A_HEADS + 2 * j] + o[b * A_HEADS + 2 * j + 1]
                o_ref[b * WINDOW:(b + 1) * WINDOW, j * 128:(j + 1) * 128] = pair.astype(BF16)

    return pl.pallas_call(
        body, name=name, grid=(t // rows,), in_specs=_att_specs(nb),
        out_specs=pl.BlockSpec((rows, A_Q), lambda n: (n, 0)),
        out_shape=jax.ShapeDtypeStruct((t, A_Q), BF16), compiler_params=_params(("parallel",)))(
            proj, proj, proj, sinks)


def att_bwd(proj, sinks, dout, *, name):
    t = proj.shape[0]
    nb = min(ATT_BLOCKS, t // WINDOW)
    rows = nb * WINDOW

    def body(q_ref, kvc_ref, kvp_ref, s_ref, do_ref, dq_ref, dkc_ref, dkp_ref, ds_ref):
        @pl.when(pl.program_id(0) == 0)
        def _():
            ds_ref[...] = jnp.zeros_like(ds_ref)

        q, k, v, sink, valid = _att_operands(q_ref, kvc_ref, kvp_ref, s_ref, pl.program_id(0), nb)
        dop = jnp.stack([do_ref[b * WINDOW:(b + 1) * WINDOW, (h // 2) * 128:(h // 2 + 1) * 128]
                         for b in range(nb) for h in range(A_HEADS)]).astype(BF16)
        w, w_sink = _att_probs(q, k, sink, valid)
        dw = _dot(dop, v, NT)
        delta = jnp.sum(w * dw, axis=-1, keepdims=True)
        dsc = (w * (dw - delta)).astype(BF16)
        dsink_h = -jnp.sum(w_sink * delta, axis=1, keepdims=True)
        dq = _dot(dsc, k, NN)
        dk_h = _dot(dsc, q, TN)
        dv_h = _dot(w.astype(BF16), dop, TN)
        lane = lax.broadcasted_iota(jnp.int32, (1, 128), 1)
        dsink = jnp.zeros((1, 128), F32)
        for b in range(nb):
            for h in range(A_HEADS):
                dsink = dsink + jnp.where(lane == h, dsink_h[b * A_HEADS + h], 0.0)
        ds_ref[...] += dsink
        lo = lax.broadcasted_iota(jnp.int32, (2 * WINDOW, 128), 1) < A_HEAD_DIM
        heads_per_kv = A_HEADS // A_KV_HEADS

        def tile(per_head, b):
            acc = {}
            for kvh in range(A_KV_HEADS):
                for half in range(2):
                    hs = range(kvh * heads_per_kv + half, (kvh + 1) * heads_per_kv, 2)
                    acc[(kvh, half)] = functools.reduce(lambda a, c: a + c, [per_head[b * A_HEADS + h] for h in hs])
            return jnp.where(lo, acc[(0, 0)] + pltpu.roll(acc[(0, 1)], A_HEAD_DIM, 1),
                             pltpu.roll(acc[(1, 0)], A_HEAD_DIM, 1) + acc[(1, 1)])

        for b in range(nb):
            blk = slice(b * WINDOW, (b + 1) * WINDOW)
            for j in range(PAIRS):
                pair = dq[b * A_HEADS + 2 * j] + dq[b * A_HEADS + 2 * j + 1]
                dq_ref[blk, j * 128:(j + 1) * 128] = (pair * ATT_SCALE).astype(BF16)
            dkv = jnp.concatenate([tile(dk_h, b), tile(dv_h, b)], axis=1)
            dkp_ref[blk, :] = dkv[:WINDOW]
            dkc_ref[blk, :] = dkv[WINDOW:]

    kvo = pl.BlockSpec((rows, 2 * A_KV), lambda n: (n, 0))
    return pl.pallas_call(
        body, name=name, grid=(t // rows,),
        in_specs=_att_specs(nb) + [pl.BlockSpec((rows, A_Q), lambda n: (n, 0))],
        out_specs=[pl.BlockSpec((rows, A_Q), lambda n: (n, 0)), kvo, kvo, pl.BlockSpec((1, 128), lambda n: (0, 0))],
        out_shape=[jax.ShapeDtypeStruct((t, A_Q), BF16), jax.ShapeDtypeStruct((t, 2 * A_KV), F32),
                   jax.ShapeDtypeStruct((t, 2 * A_KV), F32), jax.ShapeDtypeStruct((1, 128), F32)],
        compiler_params=_params(("arbitrary",)))(proj, proj, proj, sinks, dout)


QK_SCALE = B_HEAD_DIM ** -0.5
PREP_COLS = 256
PREP_NCB = 3 * B_W // PREP_COLS
HALO = 8
PREP_ROWS = 1024


def _roll_rows(x, shift):
    n = x.shape[0]
    return x if shift % n == 0 else pltpu.roll(x, shift % n, 0)


def _conv_taps(xe, w):
    xs = [_roll_rows(xe, CONV_K - 1 - i) for i in range(CONV_K)]
    c = w[0:1] * xs[0]
    for i in range(1, CONV_K):
        c = c + w[i:i + 1] * xs[i]
    return xs, c


def dprep_fwd(proj, conv_w, *, name):
    t = proj.shape[0]
    tt = min(PREP_ROWS, t)
    col0 = COL_QKVB // PREP_COLS

    def body(x_ref, h_ref, w_ref, o_ref):
        cb, n = pl.program_id(0), pl.program_id(1)
        halo = jnp.where(n > 0, h_ref[...], 0.0)
        xe = jnp.concatenate([halo, x_ref[...]], axis=0)
        _, c = _conv_taps(xe, w_ref[...])
        y = _silu(c)[HALO:]
        parts = []
        for hh in range(PREP_COLS // B_HEAD_DIM):
            yh = y[:, hh * B_HEAD_DIM:(hh + 1) * B_HEAD_DIM]
            parts.append(yh * lax.rsqrt(jnp.sum(yh * yh, axis=-1, keepdims=True) + EPS))
        nrm = jnp.concatenate(parts, axis=-1)
        o_ref[...] = jnp.where(cb < 4, nrm * QK_SCALE, jnp.where(cb < 8, nrm, y))

    return pl.pallas_call(
        body, name=name, grid=(PREP_NCB, t // tt),
        in_specs=[pl.BlockSpec((tt, PREP_COLS), lambda cb, n: (n, col0 + cb)),
                  pl.BlockSpec((HALO, PREP_COLS), lambda cb, n: (jnp.maximum(n * (tt // HALO) - 1, 0), col0 + cb)),
                  pl.BlockSpec((CONV_K, PREP_COLS), lambda cb, n: (0, cb))],
        out_specs=pl.BlockSpec((tt, PREP_COLS), lambda cb, n: (n, cb)),
        out_shape=jax.ShapeDtypeStruct((t, 3 * B_W), F32), compiler_params=_params(("parallel", "parallel")))(
            proj, proj, conv_w)


def dprep_bwd(proj, conv_w, dqkvn, *, name):
    t = proj.shape[0]
    tt = min(PREP_ROWS, t)
    nb = t // tt
    col0 = COL_QKVB // PREP_COLS
    n8 = t // HALO

    def body(xc_ref, xb_ref, xa_ref, dc_ref, da_ref, w_ref, dx_ref, dw_ref):
        cb, n = pl.program_id(0), pl.program_id(1)

        @pl.when(n == 0)
        def _():
            dw_ref[...] = jnp.zeros_like(dw_ref)

        w = w_ref[...]
        xe = jnp.concatenate([jnp.where(n > 0, xb_ref[...], 0.0), xc_ref[...], xa_ref[...]], axis=0)
        xs, c = _conv_taps(xe, w)
        sg = _sigmoid(c)
        y = c * sg
        dout = jnp.concatenate([jnp.zeros((HALO, PREP_COLS), F32), dc_ref[...],
                                jnp.where(n < nb - 1, da_ref[...], 0.0)], axis=0)
        dsc = jnp.where(cb < 4, QK_SCALE, 1.0)
        parts = []
        for hh in range(PREP_COLS // B_HEAD_DIM):
            sl = slice(hh * B_HEAD_DIM, (hh + 1) * B_HEAD_DIM)
            yh, doh = y[:, sl], dout[:, sl] * dsc
            r = lax.rsqrt(jnp.sum(yh * yh, axis=-1, keepdims=True) + EPS)
            parts.append(doh * r - yh * (r * r * r) * jnp.sum(doh * yh, axis=-1, keepdims=True))
        dy = jnp.where(cb < 8, jnp.concatenate(parts, axis=-1), dout)
        dcv = dy * sg * (1.0 + c * (1.0 - sg))
        dxe = w[CONV_K - 1:CONV_K] * dcv
        for i in range(CONV_K - 1):
            dxe = dxe + w[i:i + 1] * _roll_rows(dcv, -(CONV_K - 1 - i))
        dx_ref[...] = dxe[HALO:HALO + tt].astype(BF16)
        for i in range(CONV_K):
            dw_ref[i:i + 1, :] += jnp.sum((dcv * xs[i])[HALO:HALO + tt], axis=0, keepdims=True)

    def after(n):
        return jnp.minimum((n + 1) * (tt // HALO), n8 - 1)

    return pl.pallas_call(
        body, name=name, grid=(PREP_NCB, nb),
        in_specs=[pl.BlockSpec((tt, PREP_COLS), lambda cb, n: (n, col0 + cb)),
                  pl.BlockSpec((HALO, PREP_COLS), lambda cb, n: (jnp.maximum(n * (tt // HALO) - 1, 0), col0 + cb)),
                  pl.BlockSpec((HALO, PREP_COLS), lambda cb, n: (after(n), col0 + cb)),
                  pl.BlockSpec((tt, PREP_COLS), lambda cb, n: (n, cb)),
                  pl.BlockSpec((HALO, PREP_COLS), lambda cb, n: (after(n), cb)),
                  pl.BlockSpec((CONV_K, PREP_COLS), lambda cb, n: (0, cb))],
        out_specs=[pl.BlockSpec((tt, PREP_COLS), lambda cb, n: (n, cb)),
                   pl.BlockSpec((CONV_K, PREP_COLS), lambda cb, n: (0, cb))],
        out_shape=[jax.ShapeDtypeStruct((t, 3 * B_W), BF16), jax.ShapeDtypeStruct((CONV_K, 3 * B_W), F32)],
        compiler_params=_params(("parallel", "arbitrary")))(proj, proj, proj, dqkvn, dqkvn, conv_w)


def _softplus(z):
    return jnp.maximum(z, 0.0) + jnp.log(1.0 + jnp.exp(-jnp.abs(z)))


def gates_fwd(proj, alog_pad, dtb_pad, *, name):
    t = proj.shape[0]

    def body(x_ref, a_ref, b_ref, o_ref):
        raw = x_ref[...]
        lane = lax.broadcasted_iota(jnp.int32, raw.shape, 1)
        g = -jnp.exp(a_ref[...]) * _softplus(raw + b_ref[...])
        o_ref[...] = jnp.where(lane < B_HEADS, _sigmoid(raw), jnp.where(lane < 2 * B_HEADS, g, 0.0))

    vec = pl.BlockSpec((1, 128), lambda n: (0, 0))
    return pl.pallas_call(
        body, name=name, grid=(t // ROWS,),
        in_specs=[pl.BlockSpec((ROWS, 128), lambda n: (n, COL_GATE // 128)), vec, vec],
        out_specs=pl.BlockSpec((ROWS, 128), lambda n: (n, 0)),
        out_shape=jax.ShapeDtypeStruct((t, 128), F32), compiler_params=_params(("parallel",)))(
            proj, alog_pad, dtb_pad)


def gates_bwd(proj, alog_pad, dtb_pad, dgates, *, name):
    t = proj.shape[0]

    def body(x_ref, a_ref, b_ref, dg_ref, dx_ref, da_ref, db_ref):
        @pl.when(pl.program_id(0) == 0)
        def _():
            da_ref[...] = jnp.zeros_like(da_ref)
            db_ref[...] = jnp.zeros_like(db_ref)

        raw, dgt = x_ref[...], dg_ref[...]
        lane = lax.broadcasted_iota(jnp.int32, raw.shape, 1)
        is_beta, is_g = lane < B_HEADS, (lane >= B_HEADS) & (lane < 2 * B_HEADS)
        beta = _sigmoid(raw)
        z = raw + b_ref[...]
        neg_a = -jnp.exp(a_ref[...])
        d_z = jnp.where(is_g, dgt * neg_a * _sigmoid(z), 0.0)
        dx_ref[...] = jnp.where(is_beta, dgt * beta * (1.0 - beta), d_z).astype(BF16)
        db_ref[...] += jnp.sum(d_z, axis=0, keepdims=True)
        da_ref[...] += jnp.sum(jnp.where(is_g, dgt * neg_a * _softplus(z), 0.0), axis=0, keepdims=True)

    vec = pl.BlockSpec((1, 128), lambda n: (0, 0))
    row = pl.BlockSpec((ROWS, 128), lambda n: (n, 0))
    return pl.pallas_call(
        body, name=name, grid=(t // ROWS,),
        in_specs=[pl.BlockSpec((ROWS, 128), lambda n: (n, COL_GATE // 128)), vec, vec, row],
        out_specs=[row, vec, vec],
        out_shape=[jax.ShapeDtypeStruct((t, 128), BF16), jax.ShapeDtypeStruct((1, 128), F32),
                   jax.ShapeDtypeStruct((1, 128), F32)],
        compiler_params=_params(("arbitrary",)))(proj, alog_pad, dtb_pad, dgates)


def _split2(a):
    hi = a.astype(BF16)
    return hi, (a - hi.astype(F32)).astype(BF16)


def _dotp(a, b, dims, passes):
    if passes == 1:
        return _dot(a.astype(BF16), b.astype(BF16), dims)
    ah, al = _split2(a)
    bh, bl = _split2(b)
    return _dot(ah, bh, dims) + (_dot(ah, bl, dims) + _dot(al, bh, dims))


_GRAD_DIMS = {NN: ((NT, False), (TN, False)), NT: ((NN, False), (TN, True)), TN: ((NT, True), (NN, False))}


def _make_mm(dims, passes, grad_passes):
    (da_dims, da_swap), (db_dims, db_swap) = _GRAD_DIMS[dims]

    @jax.custom_vjp
    def mm(a, b):
        return _dotp(a, b, dims, passes)

    def fwd(a, b):
        return _dotp(a, b, dims, passes), (a, b)

    def bwd(saved, ct):
        a, b = saved
        da = _dotp(b, ct, da_dims, grad_passes) if da_swap else _dotp(ct, b, da_dims, grad_passes)
        db = _dotp(ct, a, db_dims, grad_passes) if db_swap else _dotp(a, ct, db_dims, grad_passes)
        return da, db

    mm.defvjp(fwd, bwd)
    return mm


MM1 = {d: _make_mm(d, 1, 1) for d in (NN, NT, TN)}
MM3 = {d: _make_mm(d, 3, 1) for d in (NN, NT, TN)}


def _neumann_value(n):
    c = n.shape[-1]
    eye = (lax.broadcasted_iota(jnp.int32, (c, c), 0) == lax.broadcasted_iota(jnp.int32, (c, c), 1)).astype(F32)
    inv, pw = eye + n, n
    for _ in range(5):
        pw = _dotp(pw, pw, NN, 3)
        inv = inv + _dotp(inv, pw, NN, 3)
    return inv


@jax.custom_vjp
def _neumann_inverse(n):
    return _neumann_value(n)


def _neumann_fwd(n):
    inv = _neumann_value(n)
    return inv, inv


def _neumann_bwd(inv, ct):
    return (_dotp(_dotp(inv, ct, TN, 1), inv, NT, 1),)


_neumann_inverse.defvjp(_neumann_fwd, _neumann_bwd)


def _tri_ones(lower):
    r = lax.broadcasted_iota(jnp.int32, (DN_CHUNK, DN_CHUNK), 0)
    c = lax.broadcasted_iota(jnp.int32, (DN_CHUNK, DN_CHUNK), 1)
    return (r >= c if lower else r <= c).astype(BF16)


def _tri_sum(x, lower):
    tri = _tri_ones(lower)
    hi = x.astype(BF16)
    r1 = x - hi.astype(F32)
    mid = r1.astype(BF16)
    lo = (r1 - mid.astype(F32)).astype(BF16)
    return _dot(tri, hi, NN) + (_dot(tri, mid, NN) + _dot(tri, lo, NN))


def _delta_chunk(s0, q, k, v, beta, gam_c, gam_r):
    c = DN_CHUNK
    nh = s0.shape[0]
    r = lax.broadcasted_iota(jnp.int32, (c, c), 0)
    cc = lax.broadcasted_iota(jnp.int32, (c, c), 1)
    incl, strict = r >= cc, r > cc
    decay = jnp.exp(jnp.where(incl, gam_c - gam_r, NEG_INF))
    g_last = gam_c[:, c - 1:c, :]
    e_gam, e_rest, e_last = jnp.exp(gam_c), jnp.exp(g_last - gam_c), jnp.exp(g_last)
    a_neg = -jnp.where(strict, beta * MM1[NT](k, k) * decay, 0.0)
    inv = _neumann_inverse(a_neg)
    uw = MM3[NN](inv,jnp.concatenate([v * beta, k * (beta * e_gam)], axis=-1))
    u, w = uw[..., :B_HEAD_DIM], uw[..., B_HEAD_DIM:]
    qk = MM1[NT](q, k) * decay
    q_dec, k_rest = q * e_gam, k * e_rest
    state, outs = s0, []
    for g in range(q.shape[0] // nh):
        sl = slice(g * nh, (g + 1) * nh)
        v_new = u[sl] - MM1[NN](w[sl], state)
        outs.append(MM1[NN](q_dec[sl], state) + MM1[NN](qk[sl], v_new))
        state = state * e_last[sl] + MM1[TN](k_rest[sl], v_new)
    return state, jnp.concatenate(outs, axis=0)


DN_GROUP = 4


def _delta_operands(q_ref, k_ref, v_ref, g_ref, ng):
    c = DN_CHUNK
    qs, ks, vs, betas, gam_cs, gam_rs = [], [], [], [], [], []
    for g in range(ng):
        rows = slice(g * c, (g + 1) * c)
        gt = g_ref[rows, :]
        gam = _tri_sum(gt, True)
        gam_t = gam.T
        for h in range(B_HEADS):
            cols = slice(h * B_HEAD_DIM, (h + 1) * B_HEAD_DIM)
            qs.append(q_ref[rows, cols])
            ks.append(k_ref[rows, cols])
            vs.append(v_ref[rows, cols])
            betas.append(gt[:, h:h + 1])
            gam_cs.append(gam[:, B_HEADS + h:B_HEADS + h + 1])
            gam_rs.append(gam_t[B_HEADS + h:B_HEADS + h + 1, :])
    return tuple(jnp.stack(a) for a in (qs, ks, vs, betas, gam_cs, gam_rs))


def delta_fwd(qkvn, gates, *, name):
    t = qkvn.shape[0]
    ng = min(DN_GROUP, t // DN_CHUNK)
    rows = ng * DN_CHUNK
    nc = t // rows

    def body(q_ref, k_ref, v_ref, g_ref, o_ref, ss_ref, state):
        @pl.when(pl.program_id(0) == 0)
        def _():
            state[...] = jnp.zeros_like(state)

        s0 = state[...]
        ss_ref[...] = s0
        s1, o = _delta_chunk(s0, *_delta_operands(q_ref, k_ref, v_ref, g_ref, ng))
        state[...] = s1
        for g in range(ng):
            for h in range(B_HEADS):
                o_ref[g * DN_CHUNK:(g + 1) * DN_CHUNK, h * B_HEAD_DIM:(h + 1) * B_HEAD_DIM] = o[g * B_HEADS + h]

    blk = lambda j: pl.BlockSpec((rows, B_W), lambda n: (n, j))
    return pl.pallas_call(
        body, name=name, grid=(nc,),
        in_specs=[blk(0), blk(1), blk(2), pl.BlockSpec((rows, 128), lambda n: (n, 0))],
        out_specs=[blk(0), pl.BlockSpec((None, B_HEADS, B_HEAD_DIM, B_HEAD_DIM), lambda n: (n, 0, 0, 0))],
        out_shape=[jax.ShapeDtypeStruct((t, B_W), F32),
                   jax.ShapeDtypeStruct((nc, B_HEADS, B_HEAD_DIM, B_HEAD_DIM), F32)],
        scratch_shapes=[pltpu.VMEM((B_HEADS, B_HEAD_DIM, B_HEAD_DIM), F32)],
        compiler_params=_params(("arbitrary",)))(qkvn, qkvn, qkvn, gates)


def delta_bwd(qkvn, gates, ssave, do, *, name):
    t = qkvn.shape[0]
    ng = min(DN_GROUP, t // DN_CHUNK)
    rows = ng * DN_CHUNK
    nc = t // rows

    def body(q_ref, k_ref, v_ref, g_ref, ss_ref, do_ref, dx_ref, dg_ref, dstate):
        @pl.when(pl.program_id(0) == 0)
        def _():
            dstate[...] = jnp.zeros_like(dstate)

        lane = lax.broadcasted_iota(jnp.int32, (DN_CHUNK, 128), 1)
        row = lax.broadcasted_iota(jnp.int32, (128, DN_CHUNK), 0)
        _, vjp = jax.vjp(_delta_chunk, ss_ref[...], *_delta_operands(q_ref, k_ref, v_ref, g_ref, ng))
        do = jnp.stack([do_ref[g * DN_CHUNK:(g + 1) * DN_CHUNK, h * B_HEAD_DIM:(h + 1) * B_HEAD_DIM]
                        for g in range(ng) for h in range(B_HEADS)])
        ds0, dq, dk, dv, dbeta, dgam_c, dgam_r = vjp((dstate[...], do))
        dstate[...] = ds0
        for g in range(ng):
            blk = slice(g * DN_CHUNK, (g + 1) * DN_CHUNK)
            dbeta_all = jnp.zeros((DN_CHUNK, 128), F32)
            dgam_c_all = jnp.zeros((DN_CHUNK, 128), F32)
            dgam_r_all = jnp.zeros((128, DN_CHUNK), F32)
            for h in range(B_HEADS):
                e = g * B_HEADS + h
                dx_ref[blk, h * B_HEAD_DIM:(h + 1) * B_HEAD_DIM] = dq[e]
                dx_ref[blk, B_W + h * B_HEAD_DIM:B_W + (h + 1) * B_HEAD_DIM] = dk[e]
                dx_ref[blk, 2 * B_W + h * B_HEAD_DIM:2 * B_W + (h + 1) * B_HEAD_DIM] = dv[e]
                dbeta_all = dbeta_all + jnp.where(lane == h, dbeta[e], 0.0)
                dgam_c_all = dgam_c_all + jnp.where(lane == B_HEADS + h, dgam_c[e], 0.0)
                dgam_r_all = dgam_r_all + jnp.where(row == B_HEADS + h, dgam_r[e], 0.0)
            dg_ref[blk, :] = dbeta_all + _tri_sum(dgam_c_all + dgam_r_all.T, False)

    blk = lambda j: pl.BlockSpec((rows, B_W), lambda n: (nc - 1 - n, j))
    gsp = pl.BlockSpec((rows, 128), lambda n: (nc - 1 - n, 0))
    return pl.pallas_call(
        body, name=name, grid=(nc,),
        in_specs=[blk(0), blk(1), blk(2), gsp,
                  pl.BlockSpec((None, B_HEADS, B_HEAD_DIM, B_HEAD_DIM), lambda n: (nc - 1 - n, 0, 0, 0)), blk(0)],
        out_specs=[pl.BlockSpec((rows, 3 * B_W), lambda n: (nc - 1 - n, 0)), gsp],
        out_shape=[jax.ShapeDtypeStruct((t, 3 * B_W), F32), jax.ShapeDtypeStruct((t, 128), F32)],
        scratch_shapes=[pltpu.VMEM((B_HEADS, B_HEAD_DIM, B_HEAD_DIM), F32)],
        compiler_params=_params(("arbitrary",)))(qkvn, qkvn, qkvn, gates, ssave, do)


GNORM_ROWS = 1024


def gnorm_fwd(o, proj, onorm, *, name):
    t = o.shape[0]

    def body(o_ref, z_ref, w_ref, out_ref):
        ov = o_ref[...]
        r = lax.rsqrt(jnp.mean(ov * ov, axis=-1, keepdims=True) + EPS)
        out_ref[...] = (ov * r * w_ref[...] * _silu(z_ref[...])).astype(BF16)

    rows = min(GNORM_ROWS, t)
    blk = pl.BlockSpec((rows, B_HEAD_DIM), lambda n, h: (n, h))
    return pl.pallas_call(
        body, name=name, grid=(t // rows, B_HEADS),
        in_specs=[blk, pl.BlockSpec((rows, B_HEAD_DIM), lambda n, h: (n, COL_Z // B_HEAD_DIM + h)),
                  pl.BlockSpec((1, B_HEAD_DIM), lambda n, h: (0, 0))],
        out_specs=blk, out_shape=jax.ShapeDtypeStruct((t, B_W), BF16),
        compiler_params=_params(("parallel", "parallel")))(o, proj, onorm)


def gnorm_bwd(o, proj, onorm, dout, *, dcol0, name):
    t = o.shape[0]

    def body(o_ref, z_ref, w_ref, d_ref, do_ref, dz_ref, dw_ref):
        @pl.when((pl.program_id(0) == 0) & (pl.program_id(1) == 0))
        def _():
            dw_ref[...] = jnp.zeros_like(dw_ref)

        ov, zv, wv, dv = o_ref[...], z_ref[...], w_ref[...], d_ref[...].astype(F32)
        r = lax.rsqrt(jnp.mean(ov * ov, axis=-1, keepdims=True) + EPS)
        nrm = ov * r
        dz_ref[...] = (dv * nrm * wv * _dsilu(zv)).astype(BF16)
        da = dv * _silu(zv)
        dw_ref[...] += jnp.sum(da * nrm, axis=0, keepdims=True)
        dn = da * wv
        do_ref[...] = r * dn - ov * (r * r * r) * jnp.mean(dn * ov, axis=-1, keepdims=True)

    rows = min(GNORM_ROWS, t)
    blk = pl.BlockSpec((rows, B_HEAD_DIM), lambda n, h: (n, h))
    vec = pl.BlockSpec((1, B_HEAD_DIM), lambda n, h: (0, 0))
    return pl.pallas_call(
        body, name=name, grid=(t // rows, B_HEADS),
        in_specs=[blk, pl.BlockSpec((rows, B_HEAD_DIM), lambda n, h: (n, COL_Z // B_HEAD_DIM + h)), vec,
                  pl.BlockSpec((rows, B_HEAD_DIM), lambda n, h: (n, dcol0 // B_HEAD_DIM + h))],
        out_specs=[blk, blk, vec],
        out_shape=[jax.ShapeDtypeStruct((t, B_W), F32), jax.ShapeDtypeStruct((t, B_W), BF16),
                   jax.ShapeDtypeStruct((1, B_HEAD_DIM), F32)],
        compiler_params=_params(("arbitrary", "arbitrary")))(o, proj, onorm, dout)


def _ffn_fwd(h, norm_g, wg, wu, wd, tm, tag):
    hn = rms_fwd(h, norm_g, name=f"ffn{tag}_norm")
    gate, up, act = mm_gate_up(hn, wg, wu, tm=min(512, tm), tn=1408, tk=2048, name=f"ffn{tag}_gate_up")
    h_out = mm_nn(act, wd, tm=tm, tn=2048, tk=512, out_dtype=F32, res=h, name=f"ffn{tag}_down")
    return h_out, (hn, gate, up, act)


def _ffn_bwd(dh, h, norm_g, wg, wu, wd, saved, tm, tag, emit):
    hn, gate, up, act = saved
    dwd = mm_tn(act, dh, shards=1, tm=tm, tn=1024, tk=1408, out_dtype=BF16, name=f"ffn{tag}_dwd")[0]
    dgate, dup = mm_down_bwd(dh, wd, gate, up, tm=tm, tn=512, tk=2048, name=f"ffn{tag}_dact")
    dwg = mm_tn(hn, dgate, shards=N_SHARD, tm=tm, tn=1408, tk=1024, out_dtype=BF16, name=f"ffn{tag}_dwg")
    dwu = mm_tn(hn, dup, shards=N_SHARD, tm=tm, tn=1408, tk=1024, out_dtype=BF16, name=f"ffn{tag}_dwu")
    started = emit(f"ffn{tag}", {"gate": dwg, "up": dwu, "down": dwd})
    dhn = mm_nt(dgate, wg, tm=tm, tn=1024, tk=1408, out_dtype=F32, name=f"ffn{tag}_dhn_g")
    dh_in, dnorm = dgrad_rms_bwd(dup, wu, NT, h, norm_g + started, dh, tm=min(512, tm), tk=1408, res=dhn,
                                 name=f"ffn{tag}_dhn_u_dnorm")
    return dh_in, dnorm


def _local_step(x, target, w, get, emit):
    t = x.shape[0]
    tm = min(1024, t)
    g = {}

    hn0 = rms_fwd(x, w["even_norm"], name="l0_norm")
    w.update(get("even_in", hn0))
    proj = mm_nt(hn0, w["even_w_in"], tm=tm, tn=512, tk=2048, out_dtype=F32, name="l0_w_in")
    out_a = att_fwd(proj, w["sinks"], name="l0_att")
    qkvn = dprep_fwd(proj, w["even_conv"], name="l0_prep")
    gates = gates_fwd(proj, w["a_log"], w["dt_bias"], name="l0_gates")
    o_delta, ssave = delta_fwd(qkvn, gates, name="l0_delta")
    w.update(get("even_out", o_delta))
    out_b = gnorm_fwd(o_delta, proj, w["onorm"], name="l0_gnorm")
    mix0 = jnp.concatenate([out_a, out_b], axis=-1)
    h1 = mm_nn(mix0, w["even_w_out"], tm=tm, tn=1024, tk=2048, out_dtype=F32, res=x, name="l0_w_out")
    f0 = get("ffn0", h1)
    h2, ffn0 = _ffn_fwd(h1, w["ffn_norm"][0:1] + f0["tok"], f0["gate"], f0["up"], f0["down"], tm, 0)
    hn2 = rms_fwd(h2, w["odd_norm"], name="l1_norm")
    w.update(get("odd", hn2))
    zpre = mm_nn(hn2, w["odd_w_in"], tm=tm, tn=1024, tk=2048, out_dtype=F32, name="l1_w_in")
    gated = gmlp_fwd(zpre, w["odd_ln_g"], w["odd_ln_b"], w["odd_w_s"], w["odd_b_s"], name="l1_gmlp")
    h3 = mm_nn(gated, w["odd_w_out"], tm=tm, tn=1024, tk=2048, out_dtype=F32, res=h2, name="l1_w_out")
    f1 = get("ffn1", h3)
    h4, ffn1 = _ffn_fwd(h3, w["ffn_norm"][1:2] + f1["tok"], f1["gate"], f1["up"], f1["down"], tm, 1)
    loss, dh4, g["final_norm"] = loss_head(h4, w["final_norm"], target, name="loss_head")

    dh3, dn1 = _ffn_bwd(dh4, h3, w["ffn_norm"][1:2], f1["gate"], f1["up"], f1["down"], ffn1, tm, 1, emit)
    dw_out_o = mm_tn(gated, dh3, shards=1, tm=tm, tn=1024, tk=1024, out_dtype=BF16, name="l1_dw_out")[0]
    dgated = mm_nt(dh3, w["odd_w_out"], tm=tm, tn=1024, tk=2048, out_dtype=BF16, name="l1_dgated")
    dzpre, g["odd_w_s"], g["odd_b_s"], g["odd_ln_g"], g["odd_ln_b"] = gmlp_bwd(
        zpre, dgated, w["odd_ln_g"], w["odd_ln_b"], w["odd_w_s"], w["odd_b_s"], name="l1_dgmlp")
    dw_in_o = mm_tn(hn2, dzpre, shards=N_SHARD, tm=tm, tn=1024, tk=1024, out_dtype=BF16, name="l1_dw_in")
    started = emit("odd", {"odd_w_in": dw_in_o, "odd_w_out": dw_out_o})
    dh2, g["odd_norm"] = dgrad_rms_bwd(dzpre, w["odd_w_in"], NT, h2, w["odd_norm"] + started, dh3, tm=min(512, tm),
                                       tk=1024, name="l1_dhn_dnorm")
    dh1, dn0 = _ffn_bwd(dh2, h1, w["ffn_norm"][0:1], f0["gate"], f0["up"], f0["down"], ffn0, tm, 0, emit)
    g["ffn_norm"] = jnp.concatenate([dn0, dn1], axis=0)
    dw_out_e = mm_tn(mix0, dh1, shards=1, tm=tm, tn=1024, tk=1024, out_dtype=BF16, name="l0_dw_out")[0]
    started = emit("even_out", {"even_w_out": dw_out_e})
    dmix = mm_nt(dh1, w["even_w_out"], tm=tm, tn=1024, tk=2048, out_dtype=F32, name="l0_dmix")
    dq_a, dkv_cur, dkv_prev, g["sinks"] = att_bwd(proj, w["sinks"] + started, dmix, name="l0_datt")
    dkv = dkv_cur + jnp.concatenate([dkv_prev[WINDOW:], jnp.zeros((WINDOW, 2 * A_KV), F32)], axis=0)
    do_delta, dz, g["onorm"] = gnorm_bwd(o_delta, proj, w["onorm"], dmix, dcol0=A_Q, name="l0_dgnorm")
    dqkvn, dgates = delta_bwd(qkvn, gates, ssave, do_delta, name="l0_ddelta")
    dqkv_b, g["even_conv"] = dprep_bwd(proj, w["even_conv"], dqkvn, name="l0_dprep")
    draw, g["a_log"], g["dt_bias"] = gates_bwd(proj, w["a_log"], w["dt_bias"], dgates, name="l0_dgates")
    dproj = jnp.concatenate([dq_a, dkv.astype(BF16), dqkv_b, dz, draw,
                             jnp.zeros((t, EVEN_IN_PAD - COL_GATE - 128), BF16)], axis=-1)
    dw_in_e = mm_tn(dproj, hn0, shards=1, tm=tm, tn=1024, tk=1408, out_dtype=BF16, name="l0_dw_in")[0]
    grad_x, g["even_norm"] = dgrad_rms_bwd(dproj, w["even_w_in"], NN, x, w["even_norm"], dh1, tm=min(512, tm), tk=512,
                                           name="l0_dhn_dnorm")
    emit("even_in", {"even_w_in": dw_in_e, "small": g})
    return loss, grad_x


ANY = pl.BlockSpec(memory_space=pl.ANY)
N_DEV = 8


def _place():
    return lax.axis_index("x"), lax.axis_index("y"), lax.axis_index("c")


def _chip_peers(x, y, c):
    return [((1 - x, y, c), 2 * (1 - x) + y), ((x, 1 - y, c), 2 * x + 1 - y), ((1 - x, 1 - y, c), 2 * (1 - x) + 1 - y)]


HBM = pl.BlockSpec(memory_space=pltpu.HBM)
SEM = pl.BlockSpec(memory_space=pltpu.SEMAPHORE)
EFFECT = pltpu.SideEffectType.DATAFLOW_SIDE_EFFECTING
N_PEER = 3


def _half(ref, c):
    r, cols = ref.shape
    tile_rows = 32 // jnp.dtype(ref.dtype).itemsize
    if (r // 2) % tile_rows == 0:
        return ref.at[pl.ds(c * (r // 2), r // 2)]
    assert (cols // 2) % 128 == 0, ref.shape
    return ref.at[:, pl.ds(c * (cols // 2), cols // 2)]


def _gather_plan(srcs, lands, send, recv):
    x, y, c = _place()
    return [pltpu.make_async_remote_copy(src_ref=_half(srcs[i], c), dst_ref=_half(lands[i].at[2 * x + y], c),
                                         send_sem=send.at[N_PEER * i + k], recv_sem=recv.at[N_PEER * i + k],
                                         device_id=peer, device_id_type=MESH_ID)
            for i in range(len(srcs)) for k, (peer, _) in enumerate(_chip_peers(x, y, c))]


def _relay_plan(srcs, lands, send, recv):
    x, y, c = _place()
    return [pltpu.make_async_remote_copy(src_ref=_half(lands[i].at[idx], c), dst_ref=_half(lands[i].at[idx], c),
                                         send_sem=send.at[N_PEER * i + k], recv_sem=recv.at[N_PEER * i + k],
                                         device_id=(x, y, 1 - c), device_id_type=MESH_ID)
            for i in range(len(srcs)) for k, (_, idx) in enumerate(_chip_peers(x, y, c))]


def _scatter_plan(srcs, lands, send, recv):
    x, y, c = _place()
    return [pltpu.make_async_remote_copy(src_ref=srcs[i].at[idx], dst_ref=lands[i].at[k], send_sem=send.at[N_PEER * i + k],
                                         recv_sem=recv.at[N_PEER * i + k], device_id=peer, device_id_type=MESH_ID)
            for i in range(len(srcs)) for k, (peer, idx) in enumerate(_chip_peers(x, y, c))]


def _swap_plan(srcs, lands, send, recv):
    x, y, c = _place()
    return [pltpu.make_async_remote_copy(src_ref=srcs[i], dst_ref=lands[i], send_sem=send.at[N_PEER * i],
                                         recv_sem=recv.at[N_PEER * i], device_id=(x, y, 1 - c), device_id_type=MESH_ID)
            for i in range(len(srcs))]


def copies_start(plan, srcs, lands, after, *, name):
    n = len(srcs)
    both = list(srcs) + list(lands)

    def body(*refs):
        src_refs, land_refs = refs[:n], refs[n:2 * n]
        send, recv = refs[2 * n + 1], refs[2 * n + 2]
        for cp in plan(src_refs, land_refs, send, recv):
            cp.start()
        refs[-1][...] = jnp.zeros_like(refs[-1])

    res = pl.pallas_call(
        body, name=name,
        out_shape=(pltpu.SemaphoreType.DMA((n * N_PEER,)), pltpu.SemaphoreType.DMA((n * N_PEER,)),
                   *[pltpu.HBM(a.shape, a.dtype) for a in both], jax.ShapeDtypeStruct((8, 128), F32)),
        in_specs=[HBM] * (2 * n) + [ANY],
        out_specs=(SEM, SEM, *[HBM] * (2 * n), pl.BlockSpec(memory_space=pltpu.VMEM)),
        input_output_aliases={i: 2 + i for i in range(2 * n)},
        compiler_params=pltpu.CompilerParams(has_side_effects=EFFECT))(
            *[pltpu.with_memory_space_constraint(a, pltpu.HBM) for a in both], after)
    return {"send": res[0], "recv": res[1], "srcs": list(res[2:2 + n]), "lands": list(res[2 + n:2 + 2 * n]),
            "token": res[-1]}


def copies_relay(arrived_plan, next_plan, started, after, *, name):
    srcs, lands = started["srcs"], started["lands"]
    n = len(srcs)
    both = srcs + lands

    def body(*refs):
        src_refs, land_refs = refs[:n], refs[n:2 * n]
        send1, recv1 = refs[2 * n], refs[2 * n + 1]
        send2, recv2 = refs[2 * n + 3], refs[2 * n + 4]
        for cp in arrived_plan(src_refs, land_refs, send1, recv1):
            cp.wait_send()
            cp.wait_recv()
        for cp in next_plan(src_refs, land_refs, send2, recv2):
            cp.start()
        refs[-1][...] = jnp.zeros_like(refs[-1])

    res = pl.pallas_call(
        body, name=name,
        out_shape=(pltpu.SemaphoreType.DMA((n * N_PEER,)), pltpu.SemaphoreType.DMA((n * N_PEER,)),
                   *[pltpu.HBM(a.shape, a.dtype) for a in both], jax.ShapeDtypeStruct((8, 128), F32)),
        in_specs=[HBM] * (2 * n) + [SEM, SEM, ANY],
        out_specs=(SEM, SEM, *[HBM] * (2 * n), pl.BlockSpec(memory_space=pltpu.VMEM)),
        input_output_aliases={i: 2 + i for i in range(2 * n)},
        compiler_params=pltpu.CompilerParams(has_side_effects=EFFECT))(*both, started["send"], started["recv"], after)
    return {"send": res[0], "recv": res[1], "srcs": list(res[2:2 + n]), "lands": list(res[2 + n:2 + 2 * n]),
            "token": res[-1]}


def copies_wait(plan, started, after, *, name):
    srcs, lands = started["srcs"], started["lands"]
    n = len(srcs)
    both = srcs + lands

    def body(*refs):
        src_refs, land_refs = refs[:n], refs[n:2 * n]
        send, recv = refs[2 * n], refs[2 * n + 1]
        for cp in plan(src_refs, land_refs, send, recv):
            cp.wait_send()
            cp.wait_recv()

    res = pl.pallas_call(
        body, name=name, out_shape=tuple(pltpu.HBM(a.shape, a.dtype) for a in both),
        in_specs=[HBM] * (2 * n) + [SEM, SEM, ANY], out_specs=(HBM,) * (2 * n),
        input_output_aliases={i: i for i in range(2 * n)},
        compiler_params=pltpu.CompilerParams(has_side_effects=EFFECT))(*both, started["send"], started["recv"], after)
    return list(res[:n]), list(res[n:])


def allgather_small(small, *, name):
    def body(small_ref, out_ref, send, recv, loc):
        x, y, c = _place()
        dev = 4 * x + 2 * y + c
        local = pltpu.make_async_copy(small_ref, out_ref.at[dev], loc)
        remote = []
        for r in range(1, N_DEV):
            fx, fy, fc = (r >> 2) & 1, (r >> 1) & 1, r & 1
            peer = (1 - x if fx else x, 1 - y if fy else y, 1 - c if fc else c)
            remote.append(pltpu.make_async_remote_copy(
                src_ref=small_ref, dst_ref=out_ref.at[dev], send_sem=send.at[r - 1], recv_sem=recv.at[r - 1],
                device_id=peer, device_id_type=MESH_ID))
        local.start()
        for cp in remote:
            cp.start()
        for cp in remote:
            cp.wait()
        local.wait()

    return pl.pallas_call(
        body, name=name, in_specs=[ANY], out_specs=ANY,
        out_shape=jax.ShapeDtypeStruct((N_DEV,) + small.shape, small.dtype),
        scratch_shapes=[pltpu.SemaphoreType.DMA((N_DEV - 1,)), pltpu.SemaphoreType.DMA((N_DEV - 1,)),
                        pltpu.SemaphoreType.DMA(())])(small)


RED_ROWS = 256
RED_COLS = 256


def _red_block(r, c):
    if r % RED_ROWS == 0:
        return RED_ROWS, c
    if c > RED_COLS and c % RED_COLS == 0:
        return r, RED_COLS
    return r, c


def sum_chips(by_owner, me, got, *, name):
    _, r, c = by_owner.shape
    rb, cb = _red_block(r, c)

    def body(me_ref, o_ref, a_ref, b_ref, c_ref, out_ref):
        total = ((o_ref[...].astype(F32) + a_ref[...].astype(F32)) + b_ref[...].astype(F32)) + c_ref[...].astype(F32)
        out_ref[...] = total.astype(BF16)

    gk = lambda k: pl.BlockSpec((None, rb, cb), lambda i, j, me_ref: (k, i, j))
    grid_spec = pltpu.PrefetchScalarGridSpec(
        num_scalar_prefetch=1, grid=(r // rb, c // cb),
        in_specs=[pl.BlockSpec((None, rb, cb), lambda i, j, me_ref: (me_ref[0], i, j)), gk(0), gk(1), gk(2)],
        out_specs=pl.BlockSpec((rb, cb), lambda i, j, me_ref: (i, j)))
    return pl.pallas_call(
        body, name=name, grid_spec=grid_spec, out_shape=jax.ShapeDtypeStruct((r, c), BF16),
        compiler_params=_params(("parallel", "parallel")))(me, by_owner, got, got, got)


def sum_devices(small_all, *, name):
    _, p, c = small_all.shape

    def body(a_ref, out_ref):
        acc = a_ref[0]
        for d in range(1, N_DEV):
            acc = acc + a_ref[d]
        out_ref[...] = acc

    return pl.pallas_call(
        body, name=name, grid=(1,), in_specs=[pl.BlockSpec((N_DEV, p, c), lambda i: (0, 0, 0))],
        out_specs=pl.BlockSpec((p, c), lambda i: (0, 0)), out_shape=jax.ShapeDtypeStruct((p, c), F32),
        compiler_params=_params(("arbitrary",)))(small_all)


def adamw(parts, w, m, v, *, name):
    nl, r, c = w.shape
    assert len(parts) == nl
    npart = len(parts[0])
    rb, cb = _red_block(r, c)
    flat = [a for layer in parts for a in layer]

    def body(*refs):
        p_refs, (w_ref, m_ref, v_ref) = refs[:nl * npart], refs[nl * npart:nl * npart + 3]
        g_ref, d_ref, nm_ref, nv_ref = refs[nl * npart + 3:]
        layer = pl.program_id(0)
        grad = None
        for l in range(nl):
            gl = p_refs[l * npart][...].astype(F32)
            for j in range(1, npart):
                gl = gl + p_refs[l * npart + j][...].astype(F32)
            grad = gl if grad is None else jnp.where(layer == l, gl, grad)
        wv, mv, vv = w_ref[...], m_ref[...], v_ref[...]
        nm = ADAM_B1 * mv + (1.0 - ADAM_B1) * grad
        nv = ADAM_B2 * vv + (1.0 - ADAM_B2) * (grad * grad)
        m_hat = nm / (1.0 - ADAM_B1 ** ADAM_STEP)
        v_hat = nv / (1.0 - ADAM_B2 ** ADAM_STEP)
        g_ref[...] = grad
        d_ref[...] = -ADAM_LR * (m_hat / (jnp.sqrt(v_hat) + ADAM_EPS) + ADAM_WD * wv)
        nm_ref[...] = nm
        nv_ref[...] = nv

    pspec = pl.BlockSpec((rb, cb), lambda l, i, j: (i, j))
    wspec = pl.BlockSpec((None, rb, cb), lambda l, i, j: (l, i, j))
    osh = jax.ShapeDtypeStruct((nl, r, c), F32)
    return pl.pallas_call(
        body, name=name, grid=(nl, r // rb, c // cb), in_specs=[pspec] * (nl * npart) + [wspec] * 3,
        out_specs=[wspec] * 4, out_shape=[osh] * 4,
        compiler_params=_params(("parallel", "parallel", "parallel")))(*flat, w, m, v)


def _rows128(a):
    flat = a.reshape(-1)
    pad = (-flat.shape[0]) % 128
    return jnp.pad(flat, (0, pad)).reshape(-1, 128)


def _pack_rows(arrs, multiple=8):
    rows = jnp.concatenate([_rows128(a.astype(F32)) for a in arrs], axis=0)
    return jnp.pad(rows, ((0, (-rows.shape[0]) % multiple), (0, 0)))


def _unpack_rows(rows, shapes):
    out, r0 = [], 0
    for shp in shapes:
        size = 1
        for s in shp:
            size *= s
        nr = -(-size // 128)
        out.append(rows[r0:r0 + nr].reshape(-1)[:size].reshape(shp))
        r0 += nr
    return out


SMALL_LOCAL_GRADS = ["even_norm", "even_conv", "a_log", "dt_bias", "sinks", "onorm", "odd_norm", "odd_ln_g",
                     "odd_ln_b", "odd_w_s", "odd_b_s", "ffn_norm", "final_norm"]
BIG = ["even_w_in", "even_w_out", "odd_w_in", "odd_w_out", "ffn_w_gate", "ffn_w_up", "ffn_w_down"]
WEIGHTS = ["even_norm", "even_w_in", "even_conv", "even_a_log", "even_dt_bias", "even_sinks", "even_onorm",
           "even_w_out", "odd_norm", "odd_w_in", "odd_ln_g", "odd_ln_b", "odd_w_s", "odd_b_s", "odd_w_out",
           "ffn_norm", "ffn_w_gate", "ffn_w_up", "ffn_w_down", "final_norm"]
SMALL = [n for n in WEIGHTS if n not in BIG]


def kernel(x, even_norm, even_w_in, even_conv, even_a_log, even_dt_bias, even_sinks, even_onorm, even_w_out, odd_norm, odd_w_in, odd_ln_g, odd_ln_b, odd_w_s, odd_b_s, odd_w_out, ffn_norm, ffn_w_gate, ffn_w_up, ffn_w_down, final_norm, loss_target, m_even_norm, m_even_w_in, m_even_conv, m_even_a_log, m_even_dt_bias, m_even_sinks, m_even_onorm, m_even_w_out, m_odd_norm, m_odd_w_in, m_odd_ln_g, m_odd_ln_b, m_odd_w_s, m_odd_b_s, m_odd_w_out, m_ffn_norm, m_ffn_w_gate, m_ffn_w_up, m_ffn_w_down, m_final_norm, v_even_norm, v_even_w_in, v_even_conv, v_even_a_log, v_even_dt_bias, v_even_sinks, v_even_onorm, v_even_w_out, v_odd_norm, v_odd_w_in, v_odd_ln_g, v_odd_ln_b, v_odd_w_s, v_odd_b_s, v_odd_w_out, v_ffn_norm, v_ffn_w_gate, v_ffn_w_up, v_ffn_w_down, v_final_norm):
    args = dict(locals())
    wl = {n: args[n] for n in WEIGHTS}
    ml = {n: args["m_" + n] for n in WEIGHTS}
    vl = {n: args["v_" + n] for n in WEIGHTS}
    me = 2 * lax.axis_index("x") + lax.axis_index("y")

    def landing(a):
        return lax.dynamic_update_index_in_dim(lax.empty((N_SHARD,) + a.shape, a.dtype), a, me, 0)

    gather_groups = {
        "even_in": [even_w_in[0].T], "even_out": [even_w_out[0]],
        "ffn0": [ffn_w_gate[0], ffn_w_up[0], ffn_w_down[0]], "odd": [odd_w_in[0], odd_w_out[0]],
        "ffn1": [ffn_w_gate[1], ffn_w_up[1], ffn_w_down[1]],
    }
    gathering, after = {}, even_norm
    for group, arrs in gather_groups.items():
        srcs = [(a + after[0, 0] if gathering else a).astype(BF16) for a in arrs]
        if group == "even_in":
            srcs.append(_pack_rows([even_conv[0], odd_norm, odd_ln_g, odd_ln_b], multiple=16))
        gathering[group] = copies_start(_gather_plan, srcs, [landing(a) for a in srcs], after,
                                        name=f"gather_{group}_start")
        after = gathering[group]["token"]

    order = list(gather_groups)
    relayed, kept = {}, {}
    sinks_pad = jnp.pad(even_sinks, ((0, 0), (0, 128 - A_HEADS)))

    def relay(group, behind):
        relayed[group] = copies_relay(_gather_plan, _relay_plan, gathering[group], behind,
                                      name=f"gather_{group}_relay")
        return relayed[group]["token"][0:1, 0:1]

    def get(group, behind):
        if group not in relayed:
            relay(group, behind)
        _, lands = copies_wait(_relay_plan, relayed[group], behind, name=f"gather_{group}_wait")
        nxt = order.index(group) + 1
        tok = relay(order[nxt], lands[0]) if nxt < len(order) else jnp.zeros((1, 1), F32)
        if group == "even_in":
            parts = zip(*[_unpack_rows(lands[1][s], [(CONV_K, 768), (1, 512), (1, 512), (1, 512)])
                          for s in range(N_SHARD)])
            conv, onorm, lng, lnb = [jnp.concatenate(p, axis=1) for p in parts]
            w_in = jnp.pad(lands[0].reshape(EVEN_IN, D_MODEL), ((0, EVEN_IN_PAD - EVEN_IN), (0, 0)))
            kept["odd_ln_g"] = lng
            return {"even_w_in": w_in, "even_conv": conv + tok, "odd_norm": onorm, "odd_ln_b": lnb}
        if group == "even_out":
            return {"even_w_out": lands[0].reshape(D_MODEL, D_MODEL), "onorm": even_onorm + tok}
        if group == "odd":
            return {"odd_w_in": lands[0], "odd_w_out": lands[1].reshape(D_MODEL, D_MODEL),
                    "odd_ln_g": kept["odd_ln_g"] + tok}
        return {"gate": lands[0], "up": lands[1], "down": lands[2].reshape(D_FF, D_MODEL), "tok": tok}

    rows4 = lambda a: a.reshape(N_SHARD, a.shape[0] // N_SHARD, a.shape[1])
    scattering, small = {}, {}

    def emit(group, grads):
        behind = even_norm
        if group == "even_in":
            small["local"] = grads["small"]
            small["all"] = behind = allgather_small(_pack_rows([grads["small"][n] for n in SMALL_LOCAL_GRADS]),
                                                    name="allgather_small")
            srcs = [grads["even_w_in"][:EVEN_IN].reshape(N_SHARD, EVEN_IN // N_SHARD, D_MODEL)]
        elif group == "even_out":
            srcs = [rows4(grads["even_w_out"])]
        elif group == "odd":
            srcs = [grads["odd_w_in"], rows4(grads["odd_w_out"])]
        else:
            srcs = [grads["gate"], grads["up"], rows4(grads["down"])]
        lands = [lax.empty((N_PEER,) + a.shape[1:], a.dtype) for a in srcs]
        scattering[group] = copies_start(_scatter_plan, srcs, lands, behind, name=f"scatter_{group}_start")
        return scattering[group]["token"][0:1, 0:1]

    pad816 = lambda a: jnp.pad(a, ((0, 0), (B_HEADS, 128 - 2 * B_HEADS)))
    w = {
        "even_norm": even_norm + after[0:1, 0:1],
        "a_log": pad816(even_a_log), "dt_bias": pad816(even_dt_bias),
        "sinks": sinks_pad,
        "onorm": even_onorm,
        "odd_w_s": odd_w_s[0],
        "odd_b_s": jnp.pad(odd_b_s[0].T, ((0, 0), (0, 128 - C_GROUPS))),
        "ffn_norm": ffn_norm,
        "final_norm": final_norm[None],
    }
    loss_l, grad_x = _local_step(x[0], loss_target[0], w, get, emit)
    loss = lax.psum(loss_l[0, 0], ("x", "y", "c"))

    me1 = me.reshape(1).astype(jnp.int32)
    swapping = {}

    def reduce_chips(group, behind):
        srcs, lands = copies_wait(_scatter_plan, scattering[group], behind, name=f"scatter_{group}_wait")
        partial = [sum_chips(srcs[i], me1, lands[i], name=f"sum_chips_{group}_{i}") for i in range(len(srcs))]
        swapping[group] = copies_start(_swap_plan, partial, [lax.empty(p.shape, p.dtype) for p in partial],
                                       even_norm, name=f"swap_{group}_start")
        return swapping[group]["token"]

    def swapped(group, behind):
        mine, theirs = copies_wait(_swap_plan, swapping[group], behind, name=f"swap_{group}_wait")
        return list(zip(mine, theirs))

    behind = scattering["even_in"]["token"]
    for group in ("ffn1", "ffn0", "odd", "even_out"):
        behind = reduce_chips(group, behind)
    sums = {group: swapped(group, behind) for group in ("ffn1", "ffn0", "odd", "even_out")}
    outs = {}
    parts_of = {"even_w_out": [sums["even_out"][0]], "odd_w_in": [sums["odd"][0]], "odd_w_out": [sums["odd"][1]],
                "ffn_w_gate": [sums["ffn0"][0], sums["ffn1"][0]], "ffn_w_up": [sums["ffn0"][1], sums["ffn1"][1]],
                "ffn_w_down": [sums["ffn0"][2], sums["ffn1"][2]]}
    for n in parts_of:
        outs[n] = adamw(parts_of[n], wl[n], ml[n], vl[n], name=f"adamw_{n}")
    all_updated = sum(outs[n][1][0, 0, 0] for n in parts_of).reshape(1, 1)
    behind = reduce_chips("even_in", all_updated)
    flip = lambda a: jnp.transpose(a, (0, 2, 1))
    outs["even_w_in"] = [flip(o) for o in adamw([swapped("even_in", behind)[0]], flip(wl["even_w_in"]),
                                                flip(ml["even_w_in"]), flip(vl["even_w_in"]),
                                                name="adamw_even_w_in")]

    g = small["local"]
    small_sum = sum_devices(small["all"], name="sum_devices")
    sg = dict(zip(SMALL_LOCAL_GRADS, _unpack_rows(small_sum, [g[n].shape for n in SMALL_LOCAL_GRADS])))
    own_cols = lambda a, width: lax.dynamic_slice_in_dim(a, me * width, width, axis=a.ndim - 1)
    small_grads = {
        "even_norm": sg["even_norm"], "even_conv": own_cols(sg["even_conv"], 768)[None],
        "even_a_log": sg["a_log"][:, B_HEADS:2 * B_HEADS], "even_dt_bias": sg["dt_bias"][:, B_HEADS:2 * B_HEADS],
        "even_sinks": sg["sinks"][:, :A_HEADS], "even_onorm": sg["onorm"],
        "odd_norm": own_cols(sg["odd_norm"], 512), "odd_ln_g": own_cols(sg["odd_ln_g"], 512),
        "odd_ln_b": own_cols(sg["odd_ln_b"], 512), "odd_w_s": sg["odd_w_s"][None],
        "odd_b_s": sg["odd_b_s"][:, :C_GROUPS].T[None], "ffn_norm": sg["ffn_norm"], "final_norm": sg["final_norm"][0],
    }
    packed = [_pack_rows([d[n] for n in SMALL])[None] for d in (small_grads, wl, ml, vl)]
    small_out = adamw([(packed[0][0],)], packed[1], packed[2], packed[3], name="adamw_small")
    shapes = [wl[n].shape for n in SMALL]
    for j in range(4):
        for n, a in zip(SMALL, _unpack_rows(small_out[j][0], shapes)):
            outs.setdefault(n, [None] * 4)[j] = a

    return (loss, grad_x[None], *[outs[n][0] for n in WEIGHTS], *[outs[n][1] for n in WEIGHTS],
            *[outs[n][2] for n in WEIGHTS], *[outs[n][3] for n in WEIGHTS])
```

```python
import functools

import jax
import jax.numpy as jnp
from jax import lax
from jax.experimental import pallas as pl
from jax.experimental.pallas import tpu as pltpu

F32 = jnp.float32
BF16 = jnp.bfloat16
NEG_INF = float("-inf")

D_MODEL = 2048
A_HEADS, A_KV_HEADS, A_HEAD_DIM, WINDOW = 16, 2, 64, 128
B_HEADS, B_HEAD_DIM, CONV_K, DN_CHUNK = 8, 128, 4, 64
C_GROUPS, C_CHUNK = 8, 128
C_GROUP_DIM = D_MODEL // C_GROUPS
D_FF = 5632
EPS = 1e-6
A_Q = A_HEADS * A_HEAD_DIM
A_KV = A_KV_HEADS * A_HEAD_DIM
B_W = B_HEADS * B_HEAD_DIM
EVEN_IN = A_Q + 2 * A_KV + 4 * B_W + 2 * B_HEADS
EVEN_IN_PAD = 5632
COL_KV = A_Q
COL_QKVB = A_Q + 2 * A_KV
COL_Z = COL_QKVB + 3 * B_W
COL_GATE = COL_Z + B_W
N_SHARD = 4

ADAM_LR, ADAM_B1, ADAM_B2, ADAM_EPS, ADAM_WD, ADAM_STEP = 0.001, 0.9, 0.999, 1e-08, 0.01, 10

VMEM_LIMIT_V7X = 56 * 1024 * 1024
MXU_COLS = 256
MESH_ID = pl.DeviceIdType.MESH


def _params(sem=None):
    return pltpu.CompilerParams(dimension_semantics=sem, vmem_limit_bytes=VMEM_LIMIT_V7X)


def _sigmoid(x):
    return 1.0 / (1.0 + jnp.exp(-x))


def _silu(x):
    return x * _sigmoid(x)


def _dsilu(x):
    s = _sigmoid(x)
    return s * (1.0 + x * (1.0 - s))


def _gelu(x):
    return 0.5 * x * (1.0 + lax.erf(x * 0.7071067811865476))


def _dgelu(x):
    return 0.5 * (1.0 + lax.erf(x * 0.7071067811865476)) + x * jnp.exp(-0.5 * x * x) * 0.3989422804014327


def _dot(a, b, dims):
    if a.ndim == 3:
        (ca,), (cb,) = dims
        return lax.dot_general(a, b, (((ca + 1,), (cb + 1,)), ((0,), (0,))), preferred_element_type=F32)
    return lax.dot_general(a, b, (dims, ((), ())), preferred_element_type=F32)


NN = ((1,), (0,))
NT = ((1,), (1,))
TN = ((0,), (0,))


def _as3(b):
    return b if b.ndim == 3 else b[None]


def _accumulate(step, nsteps, accs, products, finish):
    if nsteps == 1:
        finish(products())
        return

    @pl.when(step == 0)
    def _():
        for acc, p in zip(accs, products()):
            acc[...] = p

    if nsteps > 2:
        @pl.when((step > 0) & (step < nsteps - 1))
        def _():
            for acc, p in zip(accs, products()):
                acc[...] += p

    @pl.when(step == nsteps - 1)
    def _():
        finish(tuple(acc[...] + p for acc, p in zip(accs, products())))


def mm_nn(a, b, *, tm, tn, tk, out_dtype, name, res=None):
    b3 = _as3(b)
    m, k = a.shape
    s, k2, ns = b3.shape
    assert k2 == k and m % tm == 0 and ns % tn == 0 and k % tk == 0, (a.shape, b3.shape, tm, tn, tk)
    nps, nk = ns // tn, k // tk

    def body(*refs):
        if res is None:
            a_ref, b_ref, o_ref, acc = refs
        else:
            a_ref, b_ref, r_ref, o_ref, acc = refs
        def finish(tiles):
            r = tiles[0] if res is None else tiles[0] + r_ref[...].astype(F32)
            o_ref[...] = r.astype(out_dtype)

        _accumulate(pl.program_id(2), nk, (acc,),
                    lambda: (_dot(a_ref[...].astype(BF16), b_ref[...].astype(BF16), NN),), finish)

    in_specs = [pl.BlockSpec((tm, tk), lambda i, j, kk: (i, kk)),
                pl.BlockSpec((None, tk, tn), lambda i, j, kk: (j // nps, kk, j % nps))]
    args = [a, b3]
    if res is not None:
        in_specs.append(pl.BlockSpec((tm, tn), lambda i, j, kk: (i, j)))
        args.append(res)
    return pl.pallas_call(
        body, name=name, grid=(m // tm, s * nps, nk), in_specs=in_specs,
        out_specs=pl.BlockSpec((tm, tn), lambda i, j, kk: (i, j)),
        out_shape=jax.ShapeDtypeStruct((m, s * ns), out_dtype),
        scratch_shapes=[pltpu.VMEM((tm, tn), F32)],
        compiler_params=_params(("parallel", "parallel", "arbitrary")))(*args)


def mm_nt(a, b, *, tm, tn, tk, out_dtype, name, res=None):
    b3 = _as3(b)
    m, n = a.shape
    s, k, ns = b3.shape
    assert n == s * ns and m % tm == 0 and k % tn == 0 and ns % tk == 0, (a.shape, b3.shape, tm, tn, tk)
    rps = ns // tk
    nr = s * rps

    def body(*refs):
        if res is None:
            a_ref, b_ref, o_ref, acc = refs
        else:
            a_ref, b_ref, r_ref, o_ref, acc = refs
        def finish(tiles):
            r = tiles[0] if res is None else tiles[0] + r_ref[...].astype(F32)
            o_ref[...] = r.astype(out_dtype)

        _accumulate(pl.program_id(2), nr, (acc,),
                    lambda: (_dot(a_ref[...].astype(BF16), b_ref[...].astype(BF16), NT),), finish)

    in_specs = [pl.BlockSpec((tm, tk), lambda i, j, r: (i, r)),
                pl.BlockSpec((None, tn, tk), lambda i, j, r: (r // rps, j, r % rps))]
    args = [a, b3]
    if res is not None:
        in_specs.append(pl.BlockSpec((tm, tn), lambda i, j, r: (i, j)))
        args.append(res)
    return pl.pallas_call(
        body, name=name, grid=(m // tm, k // tn, nr), in_specs=in_specs,
        out_specs=pl.BlockSpec((tm, tn), lambda i, j, r: (i, j)),
        out_shape=jax.ShapeDtypeStruct((m, k), out_dtype),
        scratch_shapes=[pltpu.VMEM((tm, tn), F32)],
        compiler_params=_params(("parallel", "parallel", "arbitrary")))(*args)


def mm_tn(a, b, *, shards, tm, tn, tk, out_dtype, name):
    m, k = a.shape
    m2, n = b.shape
    ns = n // shards
    assert m2 == m and n == shards * ns and m % tm == 0 and k % tk == 0 and ns % tn == 0, (a.shape, b.shape)
    nps, nm = ns // tn, m // tm

    def body(a_ref, b_ref, o_ref, acc):
        def finish(tiles):
            o_ref[...] = tiles[0].astype(out_dtype)

        _accumulate(pl.program_id(2), nm, (acc,),
                    lambda: (_dot(a_ref[...].astype(BF16), b_ref[...].astype(BF16), TN),), finish)

    return pl.pallas_call(
        body, name=name, grid=(k // tk, shards * nps, nm),
        in_specs=[pl.BlockSpec((tm, tk), lambda i, j, mi: (mi, i)),
                  pl.BlockSpec((tm, tn), lambda i, j, mi: (mi, j))],
        out_specs=pl.BlockSpec((None, tk, tn), lambda i, j, mi: (j // nps, i, j % nps)),
        out_shape=jax.ShapeDtypeStruct((shards, k, ns), out_dtype),
        scratch_shapes=[pltpu.VMEM((tk, tn), F32)],
        compiler_params=_params(("parallel", "parallel", "arbitrary")))(a, b)


def mm_gate_up(hn, wg, wu, *, tm, tn, tk, name):
    wg3, wu3 = _as3(wg), _as3(wu)
    m, k = hn.shape
    s, _, ns = wg3.shape
    assert m % tm == 0 and ns % tn == 0 and k % tk == 0
    nps, nk = ns // tn, k // tk

    def body(a_ref, g_ref, u_ref, og_ref, ou_ref, oa_ref, accg, accu):
        def products():
            a = a_ref[...].astype(BF16)
            return _dot(a, g_ref[...].astype(BF16), NN), _dot(a, u_ref[...].astype(BF16), NN)

        def finish(tiles):
            g, u = tiles
            og_ref[...] = g.astype(BF16)
            ou_ref[...] = u.astype(BF16)
            oa_ref[...] = (_silu(g) * u).astype(BF16)

        _accumulate(pl.program_id(2), nk, (accg, accu), products, finish)

    wspec = pl.BlockSpec((None, tk, tn), lambda i, j, kk: (j // nps, kk, j % nps))
    ospec = pl.BlockSpec((tm, tn), lambda i, j, kk: (i, j))
    osh = jax.ShapeDtypeStruct((m, s * ns), BF16)
    return pl.pallas_call(
        body, name=name, grid=(m // tm, s * nps, nk),
        in_specs=[pl.BlockSpec((tm, tk), lambda i, j, kk: (i, kk)), wspec, wspec],
        out_specs=[ospec, ospec, ospec], out_shape=[osh, osh, osh],
        scratch_shapes=[pltpu.VMEM((tm, tn) if nk > 1 else (8, 128), F32)] * 2,
        compiler_params=_params(("parallel", "parallel", "arbitrary")))(hn, wg3, wu3)


def mm_down_bwd(dh, wd, gate, up, *, tm, tn, tk, name):
    m, d = dh.shape
    f, d2 = wd.shape
    assert d2 == d and m % tm == 0 and f % tn == 0 and tk == d and tn % MXU_COLS == 0

    def body(a_ref, b_ref, g_ref, u_ref, og_ref, ou_ref):
        a = a_ref[...].astype(BF16)
        for jj in range(tn // MXU_COLS):
            sl = slice(jj * MXU_COLS, (jj + 1) * MXU_COLS)
            da = _dot(a, b_ref[sl, :].astype(BF16), NT)
            g, u = g_ref[:, sl].astype(F32), u_ref[:, sl].astype(F32)
            s = _sigmoid(g)
            og_ref[:, sl] = (da * u * (s * (1.0 + g * (1.0 - s)))).astype(BF16)
            ou_ref[:, sl] = (da * (g * s)).astype(BF16)

    ospec = pl.BlockSpec((tm, tn), lambda i, j: (i, j))
    osh = jax.ShapeDtypeStruct((m, f), BF16)
    return pl.pallas_call(
        body, name=name, grid=(m // tm, f // tn),
        in_specs=[pl.BlockSpec((tm, tk), lambda i, j: (i, 0)),
                  pl.BlockSpec((tn, tk), lambda i, j: (j, 0)), ospec, ospec],
        out_specs=[ospec, ospec], out_shape=[osh, osh],
        compiler_params=_params(("parallel", "parallel")))(dh, wd, gate, up)


ROWS = 512


def rms_fwd(x, g, *, name):
    t, d = x.shape

    def body(x_ref, g_ref, o_ref):
        xv = x_ref[...]
        r = lax.rsqrt(jnp.mean(xv * xv, axis=-1, keepdims=True) + EPS)
        o_ref[...] = (xv * r * g_ref[...]).astype(BF16)

    return pl.pallas_call(
        body, name=name, grid=(t // ROWS,),
        in_specs=[pl.BlockSpec((ROWS, d), lambda i: (i, 0)), pl.BlockSpec((1, d), lambda i: (0, 0))],
        out_specs=pl.BlockSpec((ROWS, d), lambda i: (i, 0)),
        out_shape=jax.ShapeDtypeStruct((t, d), BF16), compiler_params=_params(("parallel",)))(x, g)


def dgrad_rms_bwd(a, b, form, x, g, dres, *, tm, tk, name, res=None):
    m, d = x.shape
    b3 = _as3(b)
    if form == NN:
        steps = a.shape[1] // tk
        a_spec = pl.BlockSpec((tm, tk), lambda i, r: (i, r))
        b_spec = pl.BlockSpec((None, tk, d), lambda i, r: (0, r, 0))
    else:
        s, d2, ns = b3.shape
        assert d2 == d and ns % tk == 0
        rps = ns // tk
        steps = s * rps
        a_spec = pl.BlockSpec((tm, tk), lambda i, r: (i, r))
        b_spec = pl.BlockSpec((None, d, tk), lambda i, r: (r // rps, 0, r % rps))
    assert m % tm == 0 and a.shape[1] == steps * tk

    def body(*refs):
        if res is None:
            a_ref, b_ref, x_ref, g_ref, dr_ref, dx_ref, dg_ref, acc = refs
        else:
            a_ref, b_ref, r_ref, x_ref, g_ref, dr_ref, dx_ref, dg_ref, acc = refs

        @pl.when((pl.program_id(0) == 0) & (pl.program_id(1) == 0))
        def _():
            dg_ref[...] = jnp.zeros_like(dg_ref)

        def finish(tiles):
            dyv = tiles[0] if res is None else tiles[0] + r_ref[...]
            xv = x_ref[...]
            r = lax.rsqrt(jnp.mean(xv * xv, axis=-1, keepdims=True) + EPS)
            dyg = dyv * g_ref[...]
            dx_ref[...] = r * dyg - xv * (r * r * r) * jnp.mean(dyg * xv, axis=-1, keepdims=True) + dr_ref[...]
            dg_ref[...] += jnp.sum(dyv * xv * r, axis=0, keepdims=True)

        _accumulate(pl.program_id(1), steps, (acc,),
                    lambda: (_dot(a_ref[...].astype(BF16), b_ref[...].astype(BF16), form),), finish)

    row = pl.BlockSpec((tm, d), lambda i, r: (i, 0))
    vec = pl.BlockSpec((1, d), lambda i, r: (0, 0))
    in_specs = [a_spec, b_spec] + ([row] if res is not None else []) + [row, vec, row]
    args = [a, b3] + ([res] if res is not None else []) + [x, g, dres]
    return pl.pallas_call(
        body, name=name, grid=(m // tm, steps), in_specs=in_specs, out_specs=[row, vec],
        out_shape=[jax.ShapeDtypeStruct((m, d), F32), jax.ShapeDtypeStruct((1, d), F32)],
        scratch_shapes=[pltpu.VMEM((tm, d), F32)],
        compiler_params=_params(("arbitrary", "arbitrary")))(*args)


def loss_head(h, g, target, *, name):
    t, d = h.shape

    def body(x_ref, g_ref, t_ref, loss_ref, dx_ref, dg_ref):
        @pl.when(pl.program_id(0) == 0)
        def _():
            dg_ref[...] = jnp.zeros_like(dg_ref)
            loss_ref[...] = jnp.zeros_like(loss_ref)

        xv, gv = x_ref[...], g_ref[...]
        r = lax.rsqrt(jnp.mean(xv * xv, axis=-1, keepdims=True) + EPS)
        e = xv * r * gv - t_ref[...]
        loss_ref[...] += 0.5 * jnp.sum(jnp.mean(e * e, axis=-1, keepdims=True), axis=0, keepdims=True)
        dyv = e * (1.0 / d)
        dyg = dyv * gv
        dx_ref[...] = r * dyg - xv * (r * r * r) * jnp.mean(dyg * xv, axis=-1, keepdims=True)
        dg_ref[...] += jnp.sum(dyv * xv * r, axis=0, keepdims=True)

    row = pl.BlockSpec((ROWS, d), lambda i: (i, 0))
    vec = pl.BlockSpec((1, d), lambda i: (0, 0))
    return pl.pallas_call(
        body, name=name, grid=(t // ROWS,), in_specs=[row, vec, row],
        out_specs=[pl.BlockSpec((1, 128), lambda i: (0, 0)), row, vec],
        out_shape=[jax.ShapeDtypeStruct((1, 128), F32), jax.ShapeDtypeStruct((t, d), F32),
                   jax.ShapeDtypeStruct((1, d), F32)],
        compiler_params=_params(("arbitrary",)))(h, g, target)


def _tril_mask():
    r = lax.broadcasted_iota(jnp.int32, (C_CHUNK, C_CHUNK), 0)
    c = lax.broadcasted_iota(jnp.int32, (C_CHUNK, C_CHUNK), 1)
    return r >= c


def _layer_norm_parts(v):
    mu = jnp.mean(v, axis=-1, keepdims=True)
    vc = v - mu
    rstd = lax.rsqrt(jnp.mean(vc * vc, axis=-1, keepdims=True) + EPS)
    return vc * rstd, rstd


def gmlp_fwd(zpre, ln_g, ln_b, ws, bs_t, *, name):
    t = zpre.shape[0]
    d = D_MODEL

    def body(zu_ref, zv_ref, g_ref, b_ref, ws_ref, bs_ref, o_ref):
        u = _gelu(zu_ref[...])
        vhat, _ = _layer_norm_parts(_gelu(zv_ref[...]))
        vln = (vhat * g_ref[...] + b_ref[...]).astype(BF16)
        mask = _tril_mask()
        for gi in range(C_GROUPS):
            sl = slice(gi * C_GROUP_DIM, (gi + 1) * C_GROUP_DIM)
            w = jnp.where(mask, ws_ref[gi], 0.0).astype(BF16)
            mixed = _dot(w, vln[:, sl], NN) + bs_ref[:, gi:gi + 1]
            o_ref[:, sl] = (u[:, sl] * mixed).astype(BF16)

    vec = pl.BlockSpec((1, d), lambda i: (0, 0))
    return pl.pallas_call(
        body, name=name, grid=(t // C_CHUNK,),
        in_specs=[pl.BlockSpec((C_CHUNK, d), lambda i: (i, 0)), pl.BlockSpec((C_CHUNK, d), lambda i: (i, 1)),
                  vec, vec, pl.BlockSpec((C_GROUPS, C_CHUNK, C_CHUNK), lambda i: (0, 0, 0)),
                  pl.BlockSpec((C_CHUNK, 128), lambda i: (0, 0))],
        out_specs=pl.BlockSpec((C_CHUNK, d), lambda i: (i, 0)),
        out_shape=jax.ShapeDtypeStruct((t, d), BF16), compiler_params=_params(("parallel",)))(
            zpre, zpre, ln_g, ln_b, ws, bs_t)


def gmlp_bwd(zpre, dgated, ln_g, ln_b, ws, bs_t, *, name):
    t = zpre.shape[0]
    d = D_MODEL

    def body(zu_ref, zv_ref, dg_ref, g_ref, b_ref, ws_ref, bs_ref, dz_ref, dws_ref, dbs_ref, dlg_ref, dlb_ref):
        @pl.when(pl.program_id(0) == 0)
        def _():
            dws_ref[...] = jnp.zeros_like(dws_ref)
            dbs_ref[...] = jnp.zeros_like(dbs_ref)
            dlg_ref[...] = jnp.zeros_like(dlg_ref)
            dlb_ref[...] = jnp.zeros_like(dlb_ref)

        zu, zv = zu_ref[...], zv_ref[...]
        u = _gelu(zu)
        vhat, rstd = _layer_norm_parts(_gelu(zv))
        gam = g_ref[...]
        vln = (vhat * gam + b_ref[...]).astype(BF16)
        dgt = dg_ref[...].astype(F32)
        mask = _tril_mask()
        lane = lax.broadcasted_iota(jnp.int32, (C_CHUNK, 128), 1)
        dbs = jnp.zeros((C_CHUNK, 128), F32)
        du_parts, dvln_parts = [], []
        for gi in range(C_GROUPS):
            sl = slice(gi * C_GROUP_DIM, (gi + 1) * C_GROUP_DIM)
            w = jnp.where(mask, ws_ref[gi], 0.0).astype(BF16)
            mixed = _dot(w, vln[:, sl], NN) + bs_ref[:, gi:gi + 1]
            du_parts.append(dgt[:, sl] * mixed)
            dmixed = dgt[:, sl] * u[:, sl]
            dmb = dmixed.astype(BF16)
            dws_ref[gi] += jnp.where(mask, _dot(dmb, vln[:, sl], NT), 0.0)
            dbs = dbs + jnp.where(lane == gi, jnp.sum(dmixed, axis=-1, keepdims=True), 0.0)
            dvln_parts.append(_dot(w, dmb, TN))
        dbs_ref[...] += dbs
        du = jnp.concatenate(du_parts, axis=-1)
        dvln = jnp.concatenate(dvln_parts, axis=-1)
        dlg_ref[...] += jnp.sum(dvln * vhat, axis=0, keepdims=True)
        dlb_ref[...] += jnp.sum(dvln, axis=0, keepdims=True)
        dvhat = dvln * gam
        dv = rstd * (dvhat - jnp.mean(dvhat, axis=-1, keepdims=True)
                     - vhat * jnp.mean(dvhat * vhat, axis=-1, keepdims=True))
        dz_ref[:, :d] = (du * _dgelu(zu)).astype(BF16)
        dz_ref[:, d:] = (dv * _dgelu(zv)).astype(BF16)

    vec = pl.BlockSpec((1, d), lambda i: (0, 0))
    wsp = pl.BlockSpec((C_GROUPS, C_CHUNK, C_CHUNK), lambda i: (0, 0, 0))
    bsp = pl.BlockSpec((C_CHUNK, 128), lambda i: (0, 0))
    return pl.pallas_call(
        body, name=name, grid=(t // C_CHUNK,),
        in_specs=[pl.BlockSpec((C_CHUNK, d), lambda i: (i, 0)), pl.BlockSpec((C_CHUNK, d), lambda i: (i, 1)),
                  pl.BlockSpec((C_CHUNK, d), lambda i: (i, 0)), vec, vec, wsp, bsp],
        out_specs=[pl.BlockSpec((C_CHUNK, 2 * d), lambda i: (i, 0)), wsp, bsp, vec, vec],
        out_shape=[jax.ShapeDtypeStruct((t, 2 * d), BF16), jax.ShapeDtypeStruct((C_GROUPS, C_CHUNK, C_CHUNK), F32),
                   jax.ShapeDtypeStruct((C_CHUNK, 128), F32), jax.ShapeDtypeStruct((1, d), F32),
                   jax.ShapeDtypeStruct((1, d), F32)],
        compiler_params=_params(("arbitrary",)))(zpre, zpre, dgated, ln_g, ln_b, ws, bs_t)


ATT_SCALE = A_HEAD_DIM ** -0.5
PAIRS = A_HEADS // 2
PAIRS_PER_KV = PAIRS // A_KV_HEADS


def _att_padded(tile):
    lo = lax.broadcasted_iota(jnp.int32, tile.shape, 1) < A_HEAD_DIM
    rolled = pltpu.roll(tile, A_HEAD_DIM, 1)
    zero = jnp.zeros_like(tile)
    return {(0, 0): jnp.where(lo, tile, zero).astype(BF16), (0, 1): jnp.where(lo, zero, rolled).astype(BF16),
            (1, 0): jnp.where(lo, rolled, zero).astype(BF16), (1, 1): jnp.where(lo, zero, tile).astype(BF16)}


def _att_valid(n):
    r = lax.broadcasted_iota(jnp.int32, (WINDOW, 2 * WINDOW), 0)
    c = lax.broadcasted_iota(jnp.int32, (WINDOW, 2 * WINDOW), 1)
    rel = r + WINDOW - c
    return (rel >= 0) & (rel < WINDOW) & ((c >= WINDOW) | (n > 0))


def _att_probs(qp, kpad, sink, valid):
    s = jnp.where(valid, _dot(qp, kpad, NT), NEG_INF)
    m = jnp.maximum(jnp.max(s, axis=-1, keepdims=True), sink)
    p = jnp.exp(s - m)
    e_sink = jnp.exp(sink - m)
    inv = 1.0 / (jnp.sum(p, axis=-1, keepdims=True) + e_sink)
    return p * inv, e_sink * inv


ATT_BLOCKS = 4


def _att_operands(q_ref, kvc_ref, kvp_ref, s_ref, step, nb):
    w = WINDOW
    kvs = [kvp_ref[...]] + [kvc_ref[b * w:(b + 1) * w, :] for b in range(nb)]
    key = lambda h: ((h // 2) // PAIRS_PER_KV, h % 2)
    qs, ks, vs, valids = [], [], [], []
    for b in range(nb):
        kv = jnp.concatenate([kvs[b], kvs[b + 1]], axis=0)
        kpad, vpad = _att_padded(kv[:, :128]), _att_padded(kv[:, 128:])
        pairs = [(q_ref[b * w:(b + 1) * w, j * 128:(j + 1) * 128] * ATT_SCALE).astype(BF16) for j in range(PAIRS)]
        qs += [pairs[h // 2] for h in range(A_HEADS)]
        ks += [kpad[key(h)] for h in range(A_HEADS)]
        vs += [vpad[key(h)] for h in range(A_HEADS)]
        valids += [_att_valid(step * nb + b)] * A_HEADS
    sink = jnp.stack([s_ref[:, h:h + 1] for h in range(A_HEADS)] * nb)
    return jnp.stack(qs), jnp.stack(ks), jnp.stack(vs), sink, jnp.stack(valids)


def _att_specs(nb):
    rows = nb * WINDOW
    return [pl.BlockSpec((rows, A_Q), lambda n: (n, 0)),
            pl.BlockSpec((rows, 2 * A_KV), lambda n: (n, COL_KV // (2 * A_KV))),
            pl.BlockSpec((WINDOW, 2 * A_KV), lambda n: (jnp.maximum(nb * n - 1, 0), COL_KV // (2 * A_KV))),
            pl.BlockSpec((1, 128), lambda n: (0, 0))]


def att_fwd(proj, sinks, *, name):
    t = proj.shape[0]
    nb = min(ATT_BLOCKS, t // WINDOW)
    rows = nb * WINDOW

    def body(q_ref, kvc_ref, kvp_ref, s_ref, o_ref):
        q, k, v, sink, valid = _att_operands(q_ref, kvc_ref, kvp_ref, s_ref, pl.program_id(0), nb)
        w, _ = _att_probs(q, k, sink, valid)
        o = _dot(w.astype(BF16), v, NN)
        for b in range(nb):
            for j in range(PAIRS):
                pair = o[b * A_HEADS + 2 * j] + o[b * A_HEADS + 2 * j + 1]
                o_ref[b * WINDOW:(b + 1) * WINDOW, j * 128:(j + 1) * 128] = pair.astype(BF16)

    return pl.pallas_call(
        body, name=name, grid=(t // rows,), in_specs=_att_specs(nb),
        out_specs=pl.BlockSpec((rows, A_Q), lambda n: (n, 0)),
        out_shape=jax.ShapeDtypeStruct((t, A_Q), BF16), compiler_params=_params(("parallel",)))(
            proj, proj, proj, sinks)


def att_bwd(proj, sinks, dout, *, name):
    t = proj.shape[0]
    nb = min(ATT_BLOCKS, t // WINDOW)
    rows = nb * WINDOW

    def body(q_ref, kvc_ref, kvp_ref, s_ref, do_ref, dq_ref, dkc_ref, dkp_ref, ds_ref):
        @pl.when(pl.program_id(0) == 0)
        def _():
            ds_ref[...] = jnp.zeros_like(ds_ref)

        q, k, v, sink, valid = _att_operands(q_ref, kvc_ref, kvp_ref, s_ref, pl.program_id(0), nb)
        dop = jnp.stack([do_ref[b * WINDOW:(b + 1) * WINDOW, (h // 2) * 128:(h // 2 + 1) * 128]
                         for b in range(nb) for h in range(A_HEADS)]).astype(BF16)
        w, w_sink = _att_probs(q, k, sink, valid)
        dw = _dot(dop, v, NT)
        delta = jnp.sum(w * dw, axis=-1, keepdims=True)
        dsc = (w * (dw - delta)).astype(BF16)
        dsink_h = -jnp.sum(w_sink * delta, axis=1, keepdims=True)
        dq = _dot(dsc, k, NN)
        dk_h = _dot(dsc, q, TN)
        dv_h = _dot(w.astype(BF16), dop, TN)
        lane = lax.broadcasted_iota(jnp.int32, (1, 128), 1)
        dsink = jnp.zeros((1, 128), F32)
        for b in range(nb):
            for h in range(A_HEADS):
                dsink = dsink + jnp.where(lane == h, dsink_h[b * A_HEADS + h], 0.0)
        ds_ref[...] += dsink
        lo = lax.broadcasted_iota(jnp.int32, (2 * WINDOW, 128), 1) < A_HEAD_DIM
        heads_per_kv = A_HEADS // A_KV_HEADS

        def tile(per_head, b):
            acc = {}
            for kvh in range(A_KV_HEADS):
                for half in range(2):
                    hs = range(kvh * heads_per_kv + half, (kvh + 1) * heads_per_kv, 2)
                    acc[(kvh, half)] = functools.reduce(lambda a, c: a + c, [per_head[b * A_HEADS + h] for h in hs])
            return jnp.where(lo, acc[(0, 0)] + pltpu.roll(acc[(0, 1)], A_HEAD_DIM, 1),
                             pltpu.roll(acc[(1, 0)], A_HEAD_DIM, 1) + acc[(1, 1)])

        for b in range(nb):
            blk = slice(b * WINDOW, (b + 1) * WINDOW)
            for j in range(PAIRS):
                pair = dq[b * A_HEADS + 2 * j] + dq[b * A_HEADS + 2 * j + 1]
                dq_ref[blk, j * 128:(j + 1) * 128] = (pair * ATT_SCALE).astype(BF16)
            dkv = jnp.concatenate([tile(dk_h, b), tile(dv_h, b)], axis=1)
            dkp_ref[blk, :] = dkv[:WINDOW]
            dkc_ref[blk, :] = dkv[WINDOW:]

    kvo = pl.BlockSpec((rows, 2 * A_KV), lambda n: (n, 0))
    return pl.pallas_call(
        body, name=name, grid=(t // rows,),
        in_specs=_att_specs(nb) + [pl.BlockSpec((rows, A_Q), lambda n: (n, 0))],
        out_specs=[pl.BlockSpec((rows, A_Q), lambda n: (n, 0)), kvo, kvo, pl.BlockSpec((1, 128), lambda n: (0, 0))],
        out_shape=[jax.ShapeDtypeStruct((t, A_Q), BF16), jax.ShapeDtypeStruct((t, 2 * A_KV), F32),
                   jax.ShapeDtypeStruct((t, 2 * A_KV), F32), jax.ShapeDtypeStruct((1, 128), F32)],
        compiler_params=_params(("arbitrary",)))(proj, proj, proj, sinks, dout)


QK_SCALE = B_HEAD_DIM ** -0.5
PREP_COLS = 256
PREP_NCB = 3 * B_W // PREP_COLS
HALO = 8
PREP_ROWS = 2048


def _roll_rows(x, shift):
    n = x.shape[0]
    return x if shift % n == 0 else pltpu.roll(x, shift % n, 0)


def _conv_taps(xe, w):
    xs = [_roll_rows(xe, CONV_K - 1 - i) for i in range(CONV_K)]
    c = w[0:1] * xs[0]
    for i in range(1, CONV_K):
        c = c + w[i:i + 1] * xs[i]
    return xs, c


def dprep_fwd(proj, conv_w, *, name):
    t = proj.shape[0]
    tt = min(PREP_ROWS, t)
    col0 = COL_QKVB // PREP_COLS

    def body(x_ref, h_ref, w_ref, o_ref):
        cb, n = pl.program_id(0), pl.program_id(1)
        halo = jnp.where(n > 0, h_ref[...], 0.0)
        xe = jnp.concatenate([halo, x_ref[...]], axis=0)
        _, c = _conv_taps(xe, w_ref[...])
        y = _silu(c)[HALO:]
        parts = []
        for hh in range(PREP_COLS // B_HEAD_DIM):
            yh = y[:, hh * B_HEAD_DIM:(hh + 1) * B_HEAD_DIM]
            parts.append(yh * lax.rsqrt(jnp.sum(yh * yh, axis=-1, keepdims=True) + EPS))
        nrm = jnp.concatenate(parts, axis=-1)
        o_ref[...] = jnp.where(cb < 4, nrm * QK_SCALE, jnp.where(cb < 8, nrm, y))

    return pl.pallas_call(
        body, name=name, grid=(PREP_NCB, t // tt),
        in_specs=[pl.BlockSpec((tt, PREP_COLS), lambda cb, n: (n, col0 + cb)),
                  pl.BlockSpec((HALO, PREP_COLS), lambda cb, n: (jnp.maximum(n * (tt // HALO) - 1, 0), col0 + cb)),
                  pl.BlockSpec((CONV_K, PREP_COLS), lambda cb, n: (0, cb))],
        out_specs=pl.BlockSpec((tt, PREP_COLS), lambda cb, n: (n, cb)),
        out_shape=jax.ShapeDtypeStruct((t, 3 * B_W), F32), compiler_params=_params(("parallel", "parallel")))(
            proj, proj, conv_w)


def dprep_bwd(proj, conv_w, dqkvn, *, name):
    t = proj.shape[0]
    tt = min(PREP_ROWS, t)
    nb = t // tt
    col0 = COL_QKVB // PREP_COLS
    n8 = t // HALO

    def body(xc_ref, xb_ref, xa_ref, dc_ref, da_ref, w_ref, dx_ref, dw_ref):
        cb, n = pl.program_id(0), pl.program_id(1)

        @pl.when(n == 0)
        def _():
            dw_ref[...] = jnp.zeros_like(dw_ref)

        w = w_ref[...]
        xe = jnp.concatenate([jnp.where(n > 0, xb_ref[...], 0.0), xc_ref[...], xa_ref[...]], axis=0)
        xs, c = _conv_taps(xe, w)
        sg = _sigmoid(c)
        y = c * sg
        dout = jnp.concatenate([jnp.zeros((HALO, PREP_COLS), F32), dc_ref[...],
                                jnp.where(n < nb - 1, da_ref[...], 0.0)], axis=0)
        dsc = jnp.where(cb < 4, QK_SCALE, 1.0)
        parts = []
        for hh in range(PREP_COLS // B_HEAD_DIM):
            sl = slice(hh * B_HEAD_DIM, (hh + 1) * B_HEAD_DIM)
            yh, doh = y[:, sl], dout[:, sl] * dsc
            r = lax.rsqrt(jnp.sum(yh * yh, axis=-1, keepdims=True) + EPS)
            parts.append(doh * r - yh * (r * r * r) * jnp.sum(doh * yh, axis=-1, keepdims=True))
        dy = jnp.where(cb < 8, jnp.concatenate(parts, axis=-1), dout)
        dcv = dy * sg * (1.0 + c * (1.0 - sg))
        dxe = w[CONV_K - 1:CONV_K] * dcv
        for i in range(CONV_K - 1):
            dxe = dxe + w[i:i + 1] * _roll_rows(dcv, -(CONV_K - 1 - i))
        dx_ref[...] = dxe[HALO:HALO + tt].astype(BF16)
        for i in range(CONV_K):
            dw_ref[i:i + 1, :] += jnp.sum((dcv * xs[i])[HALO:HALO + tt], axis=0, keepdims=True)

    def after(n):
        return jnp.minimum((n + 1) * (tt // HALO), n8 - 1)

    return pl.pallas_call(
        body, name=name, grid=(PREP_NCB, nb),
        in_specs=[pl.BlockSpec((tt, PREP_COLS), lambda cb, n: (n, col0 + cb)),
                  pl.BlockSpec((HALO, PREP_COLS), lambda cb, n: (jnp.maximum(n * (tt // HALO) - 1, 0), col0 + cb)),
                  pl.BlockSpec((HALO, PREP_COLS), lambda cb, n: (after(n), col0 + cb)),
                  pl.BlockSpec((tt, PREP_COLS), lambda cb, n: (n, cb)),
                  pl.BlockSpec((HALO, PREP_COLS), lambda cb, n: (after(n), cb)),
                  pl.BlockSpec((CONV_K, PREP_COLS), lambda cb, n: (0, cb))],
        out_specs=[pl.BlockSpec((tt, PREP_COLS), lambda cb, n: (n, cb)),
                   pl.BlockSpec((CONV_K, PREP_COLS), lambda cb, n: (0, cb))],
        out_shape=[jax.ShapeDtypeStruct((t, 3 * B_W), BF16), jax.ShapeDtypeStruct((CONV_K, 3 * B_W), F32)],
        compiler_params=_params(("parallel", "arbitrary")))(proj, proj, proj, dqkvn, dqkvn, conv_w)


def _softplus(z):
    return jnp.maximum(z, 0.0) + jnp.log(1.0 + jnp.exp(-jnp.abs(z)))


def gates_fwd(proj, alog_pad, dtb_pad, *, name):
    t = proj.shape[0]

    def body(x_ref, a_ref, b_ref, o_ref):
        raw = x_ref[...]
        lane = lax.broadcasted_iota(jnp.int32, raw.shape, 1)
        g = -jnp.exp(a_ref[...]) * _softplus(raw + b_ref[...])
        o_ref[...] = jnp.where(lane < B_HEADS, _sigmoid(raw), jnp.where(lane < 2 * B_HEADS, g, 0.0))

    vec = pl.BlockSpec((1, 128), lambda n: (0, 0))
    return pl.pallas_call(
        body, name=name, grid=(t // ROWS,),
        in_specs=[pl.BlockSpec((ROWS, 128), lambda n: (n, COL_GATE // 128)), vec, vec],
        out_specs=pl.BlockSpec((ROWS, 128), lambda n: (n, 0)),
        out_shape=jax.ShapeDtypeStruct((t, 128), F32), compiler_params=_params(("parallel",)))(
            proj, alog_pad, dtb_pad)


def gates_bwd(proj, alog_pad, dtb_pad, dgates, *, name):
    t = proj.shape[0]

    def body(x_ref, a_ref, b_ref, dg_ref, dx_ref, da_ref, db_ref):
        @pl.when(pl.program_id(0) == 0)
        def _():
            da_ref[...] = jnp.zeros_like(da_ref)
            db_ref[...] = jnp.zeros_like(db_ref)

        raw, dgt = x_ref[...], dg_ref[...]
        lane = lax.broadcasted_iota(jnp.int32, raw.shape, 1)
        is_beta, is_g = lane < B_HEADS, (lane >= B_HEADS) & (lane < 2 * B_HEADS)
        beta = _sigmoid(raw)
        z = raw + b_ref[...]
        neg_a = -jnp.exp(a_ref[...])
        d_z = jnp.where(is_g, dgt * neg_a * _sigmoid(z), 0.0)
        dx_ref[...] = jnp.where(is_beta, dgt * beta * (1.0 - beta), d_z).astype(BF16)
        db_ref[...] += jnp.sum(d_z, axis=0, keepdims=True)
        da_ref[...] += jnp.sum(jnp.where(is_g, dgt * neg_a * _softplus(z), 0.0), axis=0, keepdims=True)

    vec = pl.BlockSpec((1, 128), lambda n: (0, 0))
    row = pl.BlockSpec((ROWS, 128), lambda n: (n, 0))
    return pl.pallas_call(
        body, name=name, grid=(t // ROWS,),
        in_specs=[pl.BlockSpec((ROWS, 128), lambda n: (n, COL_GATE // 128)), vec, vec, row],
        out_specs=[row, vec, vec],
        out_shape=[jax.ShapeDtypeStruct((t, 128), BF16), jax.ShapeDtypeStruct((1, 128), F32),
                   jax.ShapeDtypeStruct((1, 128), F32)],
        compiler_params=_params(("arbitrary",)))(proj, alog_pad, dtb_pad, dgates)


def _split2(a):
    hi = a.astype(BF16)
    return hi, (a - hi.astype(F32)).astype(BF16)


def _dotp(a, b, dims, passes):
    if passes == 1:
        return _dot(a.astype(BF16), b.astype(BF16), dims)
    ah, al = _split2(a)
    bh, bl = _split2(b)
    return _dot(ah, bh, dims) + (_dot(ah, bl, dims) + _dot(al, bh, dims))


_GRAD_DIMS = {NN: ((NT, False), (TN, False)), NT: ((NN, False), (TN, True)), TN: ((NT, True), (NN, False))}


def _make_mm(dims, passes, grad_passes):
    (da_dims, da_swap), (db_dims, db_swap) = _GRAD_DIMS[dims]

    @jax.custom_vjp
    def mm(a, b):
        return _dotp(a, b, dims, passes)

    def fwd(a, b):
        return _dotp(a, b, dims, passes), (a, b)

    def bwd(saved, ct):
        a, b = saved
        da = _dotp(b, ct, da_dims, grad_passes) if da_swap else _dotp(ct, b, da_dims, grad_passes)
        db = _dotp(ct, a, db_dims, grad_passes) if db_swap else _dotp(a, ct, db_dims, grad_passes)
        return da, db

    mm.defvjp(fwd, bwd)
    return mm


MM1 = {d: _make_mm(d, 1, 1) for d in (NN, NT, TN)}
MM3 = {d: _make_mm(d, 3, 1) for d in (NN, NT, TN)}


def _neumann_value(n):
    c = n.shape[-1]
    eye = (lax.broadcasted_iota(jnp.int32, (c, c), 0) == lax.broadcasted_iota(jnp.int32, (c, c), 1)).astype(F32)
    inv, pw = eye + n, n
    for _ in range(5):
        pw = _dotp(pw, pw, NN, 3)
        inv = inv + _dotp(inv, pw, NN, 3)
    return inv


@jax.custom_vjp
def _neumann_inverse(n):
    return _neumann_value(n)


def _neumann_fwd(n):
    inv = _neumann_value(n)
    return inv, inv


def _neumann_bwd(inv, ct):
    return (_dotp(_dotp(inv, ct, TN, 1), inv, NT, 1),)


_neumann_inverse.defvjp(_neumann_fwd, _neumann_bwd)


def _tri_ones(lower):
    r = lax.broadcasted_iota(jnp.int32, (DN_CHUNK, DN_CHUNK), 0)
    c = lax.broadcasted_iota(jnp.int32, (DN_CHUNK, DN_CHUNK), 1)
    return (r >= c if lower else r <= c).astype(BF16)


def _tri_sum(x, lower):
    tri = _tri_ones(lower)
    hi = x.astype(BF16)
    r1 = x - hi.astype(F32)
    mid = r1.astype(BF16)
    lo = (r1 - mid.astype(F32)).astype(BF16)
    return _dot(tri, hi, NN) + (_dot(tri, mid, NN) + _dot(tri, lo, NN))


def _delta_chunk(s0, q, k, v, beta, gam_c, gam_r):
    c = DN_CHUNK
    nh = s0.shape[0]
    r = lax.broadcasted_iota(jnp.int32, (c, c), 0)
    cc = lax.broadcasted_iota(jnp.int32, (c, c), 1)
    incl, strict = r >= cc, r > cc
    decay = jnp.exp(jnp.where(incl, gam_c - gam_r, NEG_INF))
    g_last = gam_c[:, c - 1:c, :]
    e_gam, e_rest, e_last = jnp.exp(gam_c), jnp.exp(g_last - gam_c), jnp.exp(g_last)
    a_neg = -jnp.where(strict, beta * MM1[NT](k, k) * decay, 0.0)
    inv = _neumann_inverse(a_neg)
    uw = MM3[NN](inv,jnp.concatenate([v * beta, k * (beta * e_gam)], axis=-1))
    u, w = uw[..., :B_HEAD_DIM], uw[..., B_HEAD_DIM:]
    qk = MM1[NT](q, k) * decay
    q_dec, k_rest = q * e_gam, k * e_rest
    state, outs = s0, []
    for g in range(q.shape[0] // nh):
        sl = slice(g * nh, (g + 1) * nh)
        v_new = u[sl] - MM1[NN](w[sl], state)
        outs.append(MM1[NN](q_dec[sl], state) + MM1[NN](qk[sl], v_new))
        state = state * e_last[sl] + MM1[TN](k_rest[sl], v_new)
    return state, jnp.concatenate(outs, axis=0)


DN_GROUP = 4


def _delta_operands(q_ref, k_ref, v_ref, g_ref, ng):
    c = DN_CHUNK
    qs, ks, vs, betas, gam_cs, gam_rs = [], [], [], [], [], []
    for g in range(ng):
        rows = slice(g * c, (g + 1) * c)
        gt = g_ref[rows, :]
        gam = _tri_sum(gt, True)
        gam_t = gam.T
        for h in range(B_HEADS):
            cols = slice(h * B_HEAD_DIM, (h + 1) * B_HEAD_DIM)
            qs.append(q_ref[rows, cols])
            ks.append(k_ref[rows, cols])
            vs.append(v_ref[rows, cols])
            betas.append(gt[:, h:h + 1])
            gam_cs.append(gam[:, B_HEADS + h:B_HEADS + h + 1])
            gam_rs.append(gam_t[B_HEADS + h:B_HEADS + h + 1, :])
    return tuple(jnp.stack(a) for a in (qs, ks, vs, betas, gam_cs, gam_rs))


def delta_fwd(qkvn, gates, *, name):
    t = qkvn.shape[0]
    ng = min(DN_GROUP, t // DN_CHUNK)
    rows = ng * DN_CHUNK
    nc = t // rows

    def body(q_ref, k_ref, v_ref, g_ref, o_ref, ss_ref, state):
        @pl.when(pl.program_id(0) == 0)
        def _():
            state[...] = jnp.zeros_like(state)

        s0 = state[...]
        ss_ref[...] = s0
        s1, o = _delta_chunk(s0, *_delta_operands(q_ref, k_ref, v_ref, g_ref, ng))
        state[...] = s1
        for g in range(ng):
            for h in range(B_HEADS):
                o_ref[g * DN_CHUNK:(g + 1) * DN_CHUNK, h * B_HEAD_DIM:(h + 1) * B_HEAD_DIM] = o[g * B_HEADS + h]

    blk = lambda j: pl.BlockSpec((rows, B_W), lambda n: (n, j))
    return pl.pallas_call(
        body, name=name, grid=(nc,),
        in_specs=[blk(0), blk(1), blk(2), pl.BlockSpec((rows, 128), lambda n: (n, 0))],
        out_specs=[blk(0), pl.BlockSpec((None, B_HEADS, B_HEAD_DIM, B_HEAD_DIM), lambda n: (n, 0, 0, 0))],
        out_shape=[jax.ShapeDtypeStruct((t, B_W), F32),
                   jax.ShapeDtypeStruct((nc, B_HEADS, B_HEAD_DIM, B_HEAD_DIM), F32)],
        scratch_shapes=[pltpu.VMEM((B_HEADS, B_HEAD_DIM, B_HEAD_DIM), F32)],
        compiler_params=_params(("arbitrary",)))(qkvn, qkvn, qkvn, gates)


def delta_bwd(qkvn, gates, ssave, do, *, name):
    t = qkvn.shape[0]
    ng = min(DN_GROUP, t // DN_CHUNK)
    rows = ng * DN_CHUNK
    nc = t // rows

    def body(q_ref, k_ref, v_ref, g_ref, ss_ref, do_ref, dx_ref, dg_ref, dstate):
        @pl.when(pl.program_id(0) == 0)
        def _():
            dstate[...] = jnp.zeros_like(dstate)

        lane = lax.broadcasted_iota(jnp.int32, (DN_CHUNK, 128), 1)
        row = lax.broadcasted_iota(jnp.int32, (128, DN_CHUNK), 0)
        _, vjp = jax.vjp(_delta_chunk, ss_ref[...], *_delta_operands(q_ref, k_ref, v_ref, g_ref, ng))
        do = jnp.stack([do_ref[g * DN_CHUNK:(g + 1) * DN_CHUNK, h * B_HEAD_DIM:(h + 1) * B_HEAD_DIM]
                        for g in range(ng) for h in range(B_HEADS)])
        ds0, dq, dk, dv, dbeta, dgam_c, dgam_r = vjp((dstate[...], do))
        dstate[...] = ds0
        for g in range(ng):
            blk = slice(g * DN_CHUNK, (g + 1) * DN_CHUNK)
            dbeta_all = jnp.zeros((DN_CHUNK, 128), F32)
            dgam_c_all = jnp.zeros((DN_CHUNK, 128), F32)
            dgam_r_all = jnp.zeros((128, DN_CHUNK), F32)
            for h in range(B_HEADS):
                e = g * B_HEADS + h
                dx_ref[blk, h * B_HEAD_DIM:(h + 1) * B_HEAD_DIM] = dq[e]
                dx_ref[blk, B_W + h * B_HEAD_DIM:B_W + (h + 1) * B_HEAD_DIM] = dk[e]
                dx_ref[blk, 2 * B_W + h * B_HEAD_DIM:2 * B_W + (h + 1) * B_HEAD_DIM] = dv[e]
                dbeta_all = dbeta_all + jnp.where(lane == h, dbeta[e], 0.0)
                dgam_c_all = dgam_c_all + jnp.where(lane == B_HEADS + h, dgam_c[e], 0.0)
                dgam_r_all = dgam_r_all + jnp.where(row == B_HEADS + h, dgam_r[e], 0.0)
            dg_ref[blk, :] = dbeta_all + _tri_sum(dgam_c_all + dgam_r_all.T, False)

    blk = lambda j: pl.BlockSpec((rows, B_W), lambda n: (nc - 1 - n, j))
    gsp = pl.BlockSpec((rows, 128), lambda n: (nc - 1 - n, 0))
    return pl.pallas_call(
        body, name=name, grid=(nc,),
        in_specs=[blk(0), blk(1), blk(2), gsp,
                  pl.BlockSpec((None, B_HEADS, B_HEAD_DIM, B_HEAD_DIM), lambda n: (nc - 1 - n, 0, 0, 0)), blk(0)],
        out_specs=[pl.BlockSpec((rows, 3 * B_W), lambda n: (nc - 1 - n, 0)), gsp],
        out_shape=[jax.ShapeDtypeStruct((t, 3 * B_W), F32), jax.ShapeDtypeStruct((t, 128), F32)],
        scratch_shapes=[pltpu.VMEM((B_HEADS, B_HEAD_DIM, B_HEAD_DIM), F32)],
        compiler_params=_params(("arbitrary",)))(qkvn, qkvn, qkvn, gates, ssave, do)


GNORM_ROWS = 2048


def gnorm_fwd(o, proj, onorm, *, name):
    t = o.shape[0]

    def body(o_ref, z_ref, w_ref, out_ref):
        ov = o_ref[...]
        r = lax.rsqrt(jnp.mean(ov * ov, axis=-1, keepdims=True) + EPS)
        out_ref[...] = (ov * r * w_ref[...] * _silu(z_ref[...])).astype(BF16)

    rows = min(GNORM_ROWS, t)
    blk = pl.BlockSpec((rows, B_HEAD_DIM), lambda n, h: (n, h))
    return pl.pallas_call(
        body, name=name, grid=(t // rows, B_HEADS),
        in_specs=[blk, pl.BlockSpec((rows, B_HEAD_DIM), lambda n, h: (n, COL_Z // B_HEAD_DIM + h)),
                  pl.BlockSpec((1, B_HEAD_DIM), lambda n, h: (0, 0))],
        out_specs=blk, out_shape=jax.ShapeDtypeStruct((t, B_W), BF16),
        compiler_params=_params(("parallel", "parallel")))(o, proj, onorm)


def gnorm_bwd(o, proj, onorm, dout, *, dcol0, name):
    t = o.shape[0]

    def body(o_ref, z_ref, w_ref, d_ref, do_ref, dz_ref, dw_ref):
        @pl.when((pl.program_id(0) == 0) & (pl.program_id(1) == 0))
        def _():
            dw_ref[...] = jnp.zeros_like(dw_ref)

        ov, zv, wv, dv = o_ref[...], z_ref[...], w_ref[...], d_ref[...].astype(F32)
        r = lax.rsqrt(jnp.mean(ov * ov, axis=-1, keepdims=True) + EPS)
        nrm = ov * r
        dz_ref[...] = (dv * nrm * wv * _dsilu(zv)).astype(BF16)
        da = dv * _silu(zv)
        dw_ref[...] += jnp.sum(da * nrm, axis=0, keepdims=True)
        dn = da * wv
        do_ref[...] = r * dn - ov * (r * r * r) * jnp.mean(dn * ov, axis=-1, keepdims=True)

    rows = min(GNORM_ROWS, t)
    blk = pl.BlockSpec((rows, B_HEAD_DIM), lambda n, h: (n, h))
    vec = pl.BlockSpec((1, B_HEAD_DIM), lambda n, h: (0, 0))
    return pl.pallas_call(
        body, name=name, grid=(t // rows, B_HEADS),
        in_specs=[blk, pl.BlockSpec((rows, B_HEAD_DIM), lambda n, h: (n, COL_Z // B_HEAD_DIM + h)), vec,
                  pl.BlockSpec((rows, B_HEAD_DIM), lambda n, h: (n, dcol0 // B_HEAD_DIM + h))],
        out_specs=[blk, blk, vec],
        out_shape=[jax.ShapeDtypeStruct((t, B_W), F32), jax.ShapeDtypeStruct((t, B_W), BF16),
                   jax.ShapeDtypeStruct((1, B_HEAD_DIM), F32)],
        compiler_params=_params(("arbitrary", "arbitrary")))(o, proj, onorm, dout)


def _ffn_fwd(h, norm_g, wg, wu, wd, tm, tag):
    hn = rms_fwd(h, norm_g, name=f"ffn{tag}_norm")
    gate, up, act = mm_gate_up(hn, wg, wu, tm=min(512, tm), tn=1408, tk=2048, name=f"ffn{tag}_gate_up")
    h_out = mm_nn(act, wd, tm=tm, tn=2048, tk=512, out_dtype=F32, res=h, name=f"ffn{tag}_down")
    return h_out, (hn, gate, up, act)


def _ffn_bwd(dh, h, norm_g, wg, wu, wd, saved, tm, tag, emit):
    hn, gate, up, act = saved
    dwd = mm_tn(act, dh, shards=1, tm=tm, tn=1024, tk=1408, out_dtype=BF16, name=f"ffn{tag}_dwd")[0]
    dgate, dup = mm_down_bwd(dh, wd, gate, up, tm=tm, tn=512, tk=2048, name=f"ffn{tag}_dact")
    dwg = mm_tn(hn, dgate, shards=N_SHARD, tm=tm, tn=1408, tk=1024, out_dtype=BF16, name=f"ffn{tag}_dwg")
    dwu = mm_tn(hn, dup, shards=N_SHARD, tm=tm, tn=1408, tk=1024, out_dtype=BF16, name=f"ffn{tag}_dwu")
    started = emit(f"ffn{tag}", {"gate": dwg, "up": dwu, "down": dwd})
    dhn = mm_nt(dgate, wg, tm=tm, tn=1024, tk=1408, out_dtype=F32, name=f"ffn{tag}_dhn_g")
    dh_in, dnorm = dgrad_rms_bwd(dup, wu, NT, h, norm_g + started, dh, tm=min(512, tm), tk=1408, res=dhn,
                                 name=f"ffn{tag}_dhn_u_dnorm")
    return dh_in, dnorm


def _local_step(x, target, w, get, emit):
    t = x.shape[0]
    tm = min(1024, t)
    g = {}

    hn0 = rms_fwd(x, w["even_norm"], name="l0_norm")
    w.update(get("even_in", hn0))
    proj = mm_nt(hn0, w["even_w_in"], tm=tm, tn=512, tk=2048, out_dtype=F32, name="l0_w_in")
    out_a = att_fwd(proj, w["sinks"], name="l0_att")
    qkvn = dprep_fwd(proj, w["even_conv"], name="l0_prep")
    gates = gates_fwd(proj, w["a_log"], w["dt_bias"], name="l0_gates")
    o_delta, ssave = delta_fwd(qkvn, gates, name="l0_delta")
    w.update(get("even_out", o_delta))
    out_b = gnorm_fwd(o_delta, proj, w["onorm"], name="l0_gnorm")
    mix0 = jnp.concatenate([out_a, out_b], axis=-1)
    h1 = mm_nn(mix0, w["even_w_out"], tm=tm, tn=1024, tk=2048, out_dtype=F32, res=x, name="l0_w_out")
    f0 = get("ffn0", h1)
    h2, ffn0 = _ffn_fwd(h1, w["ffn_norm"][0:1] + f0["tok"], f0["gate"], f0["up"], f0["down"], tm, 0)
    hn2 = rms_fwd(h2, w["odd_norm"], name="l1_norm")
    w.update(get("odd", hn2))
    zpre = mm_nn(hn2, w["odd_w_in"], tm=tm, tn=1024, tk=2048, out_dtype=F32, name="l1_w_in")
    gated = gmlp_fwd(zpre, w["odd_ln_g"], w["odd_ln_b"], w["odd_w_s"], w["odd_b_s"], name="l1_gmlp")
    h3 = mm_nn(gated, w["odd_w_out"], tm=tm, tn=1024, tk=2048, out_dtype=F32, res=h2, name="l1_w_out")
    f1 = get("ffn1", h3)
    h4, ffn1 = _ffn_fwd(h3, w["ffn_norm"][1:2] + f1["tok"], f1["gate"], f1["up"], f1["down"], tm, 1)
    loss, dh4, g["final_norm"] = loss_head(h4, w["final_norm"], target, name="loss_head")

    dh3, dn1 = _ffn_bwd(dh4, h3, w["ffn_norm"][1:2], f1["gate"], f1["up"], f1["down"], ffn1, tm, 1, emit)
    dw_out_o = mm_tn(gated, dh3, shards=1, tm=tm, tn=1024, tk=1024, out_dtype=BF16, name="l1_dw_out")[0]
    dgated = mm_nt(dh3, w["odd_w_out"], tm=tm, tn=1024, tk=2048, out_dtype=BF16, name="l1_dgated")
    dzpre, g["odd_w_s"], g["odd_b_s"], g["odd_ln_g"], g["odd_ln_b"] = gmlp_bwd(
        zpre, dgated, w["odd_ln_g"], w["odd_ln_b"], w["odd_w_s"], w["odd_b_s"], name="l1_dgmlp")
    dw_in_o = mm_tn(hn2, dzpre, shards=N_SHARD, tm=tm, tn=1024, tk=1024, out_dtype=BF16, name="l1_dw_in")
    started = emit("odd", {"odd_w_in": dw_in_o, "odd_w_out": dw_out_o})
    dh2, g["odd_norm"] = dgrad_rms_bwd(dzpre, w["odd_w_in"], NT, h2, w["odd_norm"] + started, dh3, tm=min(512, tm),
                                       tk=1024, name="l1_dhn_dnorm")
    dh1, dn0 = _ffn_bwd(dh2, h1, w["ffn_norm"][0:1], f0["gate"], f0["up"], f0["down"], ffn0, tm, 0, emit)
    g["ffn_norm"] = jnp.concatenate([dn0, dn1], axis=0)
    dw_out_e = mm_tn(mix0, dh1, shards=1, tm=tm, tn=1024, tk=1024, out_dtype=BF16, name="l0_dw_out")[0]
    started = emit("even_out", {"even_w_out": dw_out_e})
    dmix = mm_nt(dh1, w["even_w_out"], tm=tm, tn=1024, tk=2048, out_dtype=F32, name="l0_dmix")
    dq_a, dkv_cur, dkv_prev, g["sinks"] = att_bwd(proj, w["sinks"] + started, dmix, name="l0_datt")
    dkv = dkv_cur + jnp.concatenate([dkv_prev[WINDOW:], jnp.zeros((WINDOW, 2 * A_KV), F32)], axis=0)
    do_delta, dz, g["onorm"] = gnorm_bwd(o_delta, proj, w["onorm"], dmix, dcol0=A_Q, name="l0_dgnorm")
    dqkvn, dgates = delta_bwd(qkvn, gates, ssave, do_delta, name="l0_ddelta")
    dqkv_b, g["even_conv"] = dprep_bwd(proj, w["even_conv"], dqkvn, name="l0_dprep")
    draw, g["a_log"], g["dt_bias"] = gates_bwd(proj, w["a_log"], w["dt_bias"], dgates, name="l0_dgates")
    dproj = jnp.concatenate([dq_a, dkv.astype(BF16), dqkv_b, dz, draw,
                             jnp.zeros((t, EVEN_IN_PAD - COL_GATE - 128), BF16)], axis=-1)
    dw_in_e = mm_tn(dproj, hn0, shards=1, tm=tm, tn=1024, tk=1408, out_dtype=BF16, name="l0_dw_in")[0]
    grad_x, g["even_norm"] = dgrad_rms_bwd(dproj, w["even_w_in"], NN, x, w["even_norm"], dh1, tm=min(512, tm), tk=512,
                                           name="l0_dhn_dnorm")
    emit("even_in", {"even_w_in": dw_in_e, "small": g})
    return loss, grad_x


ANY = pl.BlockSpec(memory_space=pl.ANY)
N_DEV = 8


def _place():
    return lax.axis_index("x"), lax.axis_index("y"), lax.axis_index("c")


def _chip_peers(x, y, c):
    return [((1 - x, y, c), 2 * (1 - x) + y), ((x, 1 - y, c), 2 * x + 1 - y), ((1 - x, 1 - y, c), 2 * (1 - x) + 1 - y)]


HBM = pl.BlockSpec(memory_space=pltpu.HBM)
SEM = pl.BlockSpec(memory_space=pltpu.SEMAPHORE)
EFFECT = pltpu.SideEffectType.DATAFLOW_SIDE_EFFECTING
N_PEER = 3


def _half(ref, c):
    r, cols = ref.shape
    tile_rows = 32 // jnp.dtype(ref.dtype).itemsize
    if (r // 2) % tile_rows == 0:
        return ref.at[pl.ds(c * (r // 2), r // 2)]
    assert (cols // 2) % 128 == 0, ref.shape
    return ref.at[:, pl.ds(c * (cols // 2), cols // 2)]


def _gather_plan(srcs, lands, send, recv):
    x, y, c = _place()
    return [pltpu.make_async_remote_copy(src_ref=_half(srcs[i], c), dst_ref=_half(lands[i].at[2 * x + y], c),
                                         send_sem=send.at[N_PEER * i + k], recv_sem=recv.at[N_PEER * i + k],
                                         device_id=peer, device_id_type=MESH_ID)
            for i in range(len(srcs)) for k, (peer, _) in enumerate(_chip_peers(x, y, c))]


def _relay_plan(srcs, lands, send, recv):
    x, y, c = _place()
    return [pltpu.make_async_remote_copy(src_ref=_half(lands[i].at[idx], c), dst_ref=_half(lands[i].at[idx], c),
                                         send_sem=send.at[N_PEER * i + k], recv_sem=recv.at[N_PEER * i + k],
                                         device_id=(x, y, 1 - c), device_id_type=MESH_ID)
            for i in range(len(srcs)) for k, (_, idx) in enumerate(_chip_peers(x, y, c))]


def _scatter_plan(srcs, lands, send, recv):
    x, y, c = _place()
    return [pltpu.make_async_remote_copy(src_ref=srcs[i].at[idx], dst_ref=lands[i].at[k], send_sem=send.at[N_PEER * i + k],
                                         recv_sem=recv.at[N_PEER * i + k], device_id=peer, device_id_type=MESH_ID)
            for i in range(len(srcs)) for k, (peer, idx) in enumerate(_chip_peers(x, y, c))]


def _swap_plan(srcs, lands, send, recv):
    x, y, c = _place()
    return [pltpu.make_async_remote_copy(src_ref=srcs[i], dst_ref=lands[i], send_sem=send.at[N_PEER * i],
                                         recv_sem=recv.at[N_PEER * i], device_id=(x, y, 1 - c), device_id_type=MESH_ID)
            for i in range(len(srcs))]


def copies_start(plan, srcs, lands, after, *, name):
    n = len(srcs)
    both = list(srcs) + list(lands)

    def body(*refs):
        src_refs, land_refs = refs[:n], refs[n:2 * n]
        send, recv = refs[2 * n + 1], refs[2 * n + 2]
        for cp in plan(src_refs, land_refs, send, recv):
            cp.start()
        refs[-1][...] = jnp.zeros_like(refs[-1])

    res = pl.pallas_call(
        body, name=name,
        out_shape=(pltpu.SemaphoreType.DMA((n * N_PEER,)), pltpu.SemaphoreType.DMA((n * N_PEER,)),
                   *[pltpu.HBM(a.shape, a.dtype) for a in both], jax.ShapeDtypeStruct((8, 128), F32)),
        in_specs=[HBM] * (2 * n) + [ANY],
        out_specs=(SEM, SEM, *[HBM] * (2 * n), pl.BlockSpec(memory_space=pltpu.VMEM)),
        input_output_aliases={i: 2 + i for i in range(2 * n)},
        compiler_params=pltpu.CompilerParams(has_side_effects=EFFECT))(
            *[pltpu.with_memory_space_constraint(a, pltpu.HBM) for a in both], after)
    return {"send": res[0], "recv": res[1], "srcs": list(res[2:2 + n]), "lands": list(res[2 + n:2 + 2 * n]),
            "token": res[-1]}


def copies_relay(arrived_plan, next_plan, started, after, *, name):
    srcs, lands = started["srcs"], started["lands"]
    n = len(srcs)
    both = srcs + lands

    def body(*refs):
        src_refs, land_refs = refs[:n], refs[n:2 * n]
        send1, recv1 = refs[2 * n], refs[2 * n + 1]
        send2, recv2 = refs[2 * n + 3], refs[2 * n + 4]
        for cp in arrived_plan(src_refs, land_refs, send1, recv1):
            cp.wait_send()
            cp.wait_recv()
        for cp in next_plan(src_refs, land_refs, send2, recv2):
            cp.start()
        refs[-1][...] = jnp.zeros_like(refs[-1])

    res = pl.pallas_call(
        body, name=name,
        out_shape=(pltpu.SemaphoreType.DMA((n * N_PEER,)), pltpu.SemaphoreType.DMA((n * N_PEER,)),
                   *[pltpu.HBM(a.shape, a.dtype) for a in both], jax.ShapeDtypeStruct((8, 128), F32)),
        in_specs=[HBM] * (2 * n) + [SEM, SEM, ANY],
        out_specs=(SEM, SEM, *[HBM] * (2 * n), pl.BlockSpec(memory_space=pltpu.VMEM)),
        input_output_aliases={i: 2 + i for i in range(2 * n)},
        compiler_params=pltpu.CompilerParams(has_side_effects=EFFECT))(*both, started["send"], started["recv"], after)
    return {"send": res[0], "recv": res[1], "srcs": list(res[2:2 + n]), "lands": list(res[2 + n:2 + 2 * n]),
            "token": res[-1]}


def copies_wait(plan, started, after, *, name):
    srcs, lands = started["srcs"], started["lands"]
    n = len(srcs)
    both = srcs + lands

    def body(*refs):
        src_refs, land_refs = refs[:n], refs[n:2 * n]
        send, recv = refs[2 * n], refs[2 * n + 1]
        for cp in plan(src_refs, land_refs, send, recv):
            cp.wait_send()
            cp.wait_recv()

    res = pl.pallas_call(
        body, name=name, out_shape=tuple(pltpu.HBM(a.shape, a.dtype) for a in both),
        in_specs=[HBM] * (2 * n) + [SEM, SEM, ANY], out_specs=(HBM,) * (2 * n),
        input_output_aliases={i: i for i in range(2 * n)},
        compiler_params=pltpu.CompilerParams(has_side_effects=EFFECT))(*both, started["send"], started["recv"], after)
    return list(res[:n]), list(res[n:])


def allgather_small(small, *, name):
    def body(small_ref, out_ref, send, recv, loc):
        x, y, c = _place()
        dev = 4 * x + 2 * y + c
        local = pltpu.make_async_copy(small_ref, out_ref.at[dev], loc)
        remote = []
        for r in range(1, N_DEV):
            fx, fy, fc = (r >> 2) & 1, (r >> 1) & 1, r & 1
            peer = (1 - x if fx else x, 1 - y if fy else y, 1 - c if fc else c)
            remote.append(pltpu.make_async_remote_copy(
                src_ref=small_ref, dst_ref=out_ref.at[dev], send_sem=send.at[r - 1], recv_sem=recv.at[r - 1],
                device_id=peer, device_id_type=MESH_ID))
        local.start()
        for cp in remote:
            cp.start()
        for cp in remote:
            cp.wait()
        local.wait()

    return pl.pallas_call(
        body, name=name, in_specs=[ANY], out_specs=ANY,
        out_shape=jax.ShapeDtypeStruct((N_DEV,) + small.shape, small.dtype),
        scratch_shapes=[pltpu.SemaphoreType.DMA((N_DEV - 1,)), pltpu.SemaphoreType.DMA((N_DEV - 1,)),
                        pltpu.SemaphoreType.DMA(())])(small)


RED_ROWS = 256
RED_COLS = 256


def _red_block(r, c):
    if r % RED_ROWS == 0:
        return RED_ROWS, c
    if c > RED_COLS and c % RED_COLS == 0:
        return r, RED_COLS
    return r, c


def sum_chips(by_owner, me, got, *, name):
    _, r, c = by_owner.shape
    rb, cb = _red_block(r, c)

    def body(me_ref, o_ref, a_ref, b_ref, c_ref, out_ref):
        total = ((o_ref[...].astype(F32) + a_ref[...].astype(F32)) + b_ref[...].astype(F32)) + c_ref[...].astype(F32)
        out_ref[...] = total.astype(BF16)

    gk = lambda k: pl.BlockSpec((None, rb, cb), lambda i, j, me_ref: (k, i, j))
    grid_spec = pltpu.PrefetchScalarGridSpec(
        num_scalar_prefetch=1, grid=(r // rb, c // cb),
        in_specs=[pl.BlockSpec((None, rb, cb), lambda i, j, me_ref: (me_ref[0], i, j)), gk(0), gk(1), gk(2)],
        out_specs=pl.BlockSpec((rb, cb), lambda i, j, me_ref: (i, j)))
    return pl.pallas_call(
        body, name=name, grid_spec=grid_spec, out_shape=jax.ShapeDtypeStruct((r, c), BF16),
        compiler_params=_params(("parallel", "parallel")))(me, by_owner, got, got, got)


def sum_devices(small_all, *, name):
    _, p, c = small_all.shape

    def body(a_ref, out_ref):
        acc = a_ref[0]
        for d in range(1, N_DEV):
            acc = acc + a_ref[d]
        out_ref[...] = acc

    return pl.pallas_call(
        body, name=name, grid=(1,), in_specs=[pl.BlockSpec((N_DEV, p, c), lambda i: (0, 0, 0))],
        out_specs=pl.BlockSpec((p, c), lambda i: (0, 0)), out_shape=jax.ShapeDtypeStruct((p, c), F32),
        compiler_params=_params(("arbitrary",)))(small_all)


def adamw(parts, w, m, v, *, name):
    nl, r, c = w.shape
    assert len(parts) == nl
    npart = len(parts[0])
    rb, cb = _red_block(r, c)
    flat = [a for layer in parts for a in layer]

    def body(*refs):
        p_refs, (w_ref, m_ref, v_ref) = refs[:nl * npart], refs[nl * npart:nl * npart + 3]
        g_ref, d_ref, nm_ref, nv_ref = refs[nl * npart + 3:]
        layer = pl.program_id(0)
        grad = None
        for l in range(nl):
            gl = p_refs[l * npart][...].astype(F32)
            for j in range(1, npart):
                gl = gl + p_refs[l * npart + j][...].astype(F32)
            grad = gl if grad is None else jnp.where(layer == l, gl, grad)
        wv, mv, vv = w_ref[...], m_ref[...], v_ref[...]
        nm = ADAM_B1 * mv + (1.0 - ADAM_B1) * grad
        nv = ADAM_B2 * vv + (1.0 - ADAM_B2) * (grad * grad)
        m_hat = nm / (1.0 - ADAM_B1 ** ADAM_STEP)
        v_hat = nv / (1.0 - ADAM_B2 ** ADAM_STEP)
        g_ref[...] = grad
        d_ref[...] = -ADAM_LR * (m_hat / (jnp.sqrt(v_hat) + ADAM_EPS) + ADAM_WD * wv)
        nm_ref[...] = nm
        nv_ref[...] = nv

    pspec = pl.BlockSpec((rb, cb), lambda l, i, j: (i, j))
    wspec = pl.BlockSpec((None, rb, cb), lambda l, i, j: (l, i, j))
    osh = jax.ShapeDtypeStruct((nl, r, c), F32)
    return pl.pallas_call(
        body, name=name, grid=(nl, r // rb, c // cb), in_specs=[pspec] * (nl * npart) + [wspec] * 3,
        out_specs=[wspec] * 4, out_shape=[osh] * 4,
        compiler_params=_params(("parallel", "parallel", "parallel")))(*flat, w, m, v)


def _rows128(a):
    flat = a.reshape(-1)
    pad = (-flat.shape[0]) % 128
    return jnp.pad(flat, (0, pad)).reshape(-1, 128)


def _pack_rows(arrs, multiple=8):
    rows = jnp.concatenate([_rows128(a.astype(F32)) for a in arrs], axis=0)
    return jnp.pad(rows, ((0, (-rows.shape[0]) % multiple), (0, 0)))


def _unpack_rows(rows, shapes):
    out, r0 = [], 0
    for shp in shapes:
        size = 1
        for s in shp:
            size *= s
        nr = -(-size // 128)
        out.append(rows[r0:r0 + nr].reshape(-1)[:size].reshape(shp))
        r0 += nr
    return out


SMALL_LOCAL_GRADS = ["even_norm", "even_conv", "a_log", "dt_bias", "sinks", "onorm", "odd_norm", "odd_ln_g",
                     "odd_ln_b", "odd_w_s", "odd_b_s", "ffn_norm", "final_norm"]
BIG = ["even_w_in", "even_w_out", "odd_w_in", "odd_w_out", "ffn_w_gate", "ffn_w_up", "ffn_w_down"]
WEIGHTS = ["even_norm", "even_w_in", "even_conv", "even_a_log", "even_dt_bias", "even_sinks", "even_onorm",
           "even_w_out", "odd_norm", "odd_w_in", "odd_ln_g", "odd_ln_b", "odd_w_s", "odd_b_s", "odd_w_out",
           "ffn_norm", "ffn_w_gate", "ffn_w_up", "ffn_w_down", "final_norm"]
SMALL = [n for n in WEIGHTS if n not in BIG]


def kernel(x, even_norm, even_w_in, even_conv, even_a_log, even_dt_bias, even_sinks, even_onorm, even_w_out, odd_norm, odd_w_in, odd_ln_g, odd_ln_b, odd_w_s, odd_b_s, odd_w_out, ffn_norm, ffn_w_gate, ffn_w_up, ffn_w_down, final_norm, loss_target, m_even_norm, m_even_w_in, m_even_conv, m_even_a_log, m_even_dt_bias, m_even_sinks, m_even_onorm, m_even_w_out, m_odd_norm, m_odd_w_in, m_odd_ln_g, m_odd_ln_b, m_odd_w_s, m_odd_b_s, m_odd_w_out, m_ffn_norm, m_ffn_w_gate, m_ffn_w_up, m_ffn_w_down, m_final_norm, v_even_norm, v_even_w_in, v_even_conv, v_even_a_log, v_even_dt_bias, v_even_sinks, v_even_onorm, v_even_w_out, v_odd_norm, v_odd_w_in, v_odd_ln_g, v_odd_ln_b, v_odd_w_s, v_odd_b_s, v_odd_w_out, v_ffn_norm, v_ffn_w_gate, v_ffn_w_up, v_ffn_w_down, v_final_norm):
    args = dict(locals())
    wl = {n: args[n] for n in WEIGHTS}
    ml = {n: args["m_" + n] for n in WEIGHTS}
    vl = {n: args["v_" + n] for n in WEIGHTS}
    me = 2 * lax.axis_index("x") + lax.axis_index("y")

    def landing(a):
        return lax.dynamic_update_index_in_dim(lax.empty((N_SHARD,) + a.shape, a.dtype), a, me, 0)

    gather_groups = {
        "even_in": [even_w_in[0].T], "even_out": [even_w_out[0]],
        "ffn0": [ffn_w_gate[0], ffn_w_up[0], ffn_w_down[0]], "odd": [odd_w_in[0], odd_w_out[0]],
        "ffn1": [ffn_w_gate[1], ffn_w_up[1], ffn_w_down[1]],
    }
    gathering, after = {}, even_norm
    for group, arrs in gather_groups.items():
        srcs = [(a + after[0, 0] if gathering else a).astype(BF16) for a in arrs]
        if group == "even_in":
            srcs.append(_pack_rows([even_conv[0], odd_norm, odd_ln_g, odd_ln_b], multiple=16))
        gathering[group] = copies_start(_gather_plan, srcs, [landing(a) for a in srcs], after,
                                        name=f"gather_{group}_start")
        after = gathering[group]["token"]

    order = list(gather_groups)
    relayed, kept = {}, {}
    sinks_pad = jnp.pad(even_sinks, ((0, 0), (0, 128 - A_HEADS)))

    def relay(group, behind):
        relayed[group] = copies_relay(_gather_plan, _relay_plan, gathering[group], behind,
                                      name=f"gather_{group}_relay")
        return relayed[group]["token"][0:1, 0:1]

    def get(group, behind):
        if group not in relayed:
            relay(group, behind)
        _, lands = copies_wait(_relay_plan, relayed[group], behind, name=f"gather_{group}_wait")
        nxt = order.index(group) + 1
        tok = relay(order[nxt], lands[0]) if nxt < len(order) else jnp.zeros((1, 1), F32)
        if group == "even_in":
            parts = zip(*[_unpack_rows(lands[1][s], [(CONV_K, 768), (1, 512), (1, 512), (1, 512)])
                          for s in range(N_SHARD)])
            conv, onorm, lng, lnb = [jnp.concatenate(p, axis=1) for p in parts]
            w_in = jnp.pad(lands[0].reshape(EVEN_IN, D_MODEL), ((0, EVEN_IN_PAD - EVEN_IN), (0, 0)))
            kept["odd_ln_g"] = lng
            return {"even_w_in": w_in, "even_conv": conv + tok, "odd_norm": onorm, "odd_ln_b": lnb}
        if group == "even_out":
            return {"even_w_out": lands[0].reshape(D_MODEL, D_MODEL), "onorm": even_onorm + tok}
        if group == "odd":
            return {"odd_w_in": lands[0], "odd_w_out": lands[1].reshape(D_MODEL, D_MODEL),
                    "odd_ln_g": kept["odd_ln_g"] + tok}
        return {"gate": lands[0], "up": lands[1], "down": lands[2].reshape(D_FF, D_MODEL), "tok": tok}

    rows4 = lambda a: a.reshape(N_SHARD, a.shape[0] // N_SHARD, a.shape[1])
    scattering, small = {}, {}

    def emit(group, grads):
        behind = even_norm
        if group == "even_in":
            small["local"] = grads["small"]
            small["all"] = behind = allgather_small(_pack_rows([grads["small"][n] for n in SMALL_LOCAL_GRADS]),
                                                    name="allgather_small")
            srcs = [grads["even_w_in"][:EVEN_IN].reshape(N_SHARD, EVEN_IN // N_SHARD, D_MODEL)]
        elif group == "even_out":
            srcs = [rows4(grads["even_w_out"])]
        elif group == "odd":
            srcs = [grads["odd_w_in"], rows4(grads["odd_w_out"])]
        else:
            srcs = [grads["gate"], grads["up"], rows4(grads["down"])]
        lands = [lax.empty((N_PEER,) + a.shape[1:], a.dtype) for a in srcs]
        scattering[group] = copies_start(_scatter_plan, srcs, lands, behind, name=f"scatter_{group}_start")
        return scattering[group]["token"][0:1, 0:1]

    pad816 = lambda a: jnp.pad(a, ((0, 0), (B_HEADS, 128 - 2 * B_HEADS)))
    w = {
        "even_norm": even_norm + after[0:1, 0:1],
        "a_log": pad816(even_a_log), "dt_bias": pad816(even_dt_bias),
        "sinks": sinks_pad,
        "onorm": even_onorm,
        "odd_w_s": odd_w_s[0],
        "odd_b_s": jnp.pad(odd_b_s[0].T, ((0, 0), (0, 128 - C_GROUPS))),
        "ffn_norm": ffn_norm,
        "final_norm": final_norm[None],
    }
    loss_l, grad_x = _local_step(x[0], loss_target[0], w, get, emit)
    loss = lax.psum(loss_l[0, 0], ("x", "y", "c"))

    me1 = me.reshape(1).astype(jnp.int32)
    swapping = {}

    def reduce_chips(group, behind):
        srcs, lands = copies_wait(_scatter_plan, scattering[group], behind, name=f"scatter_{group}_wait")
        partial = [sum_chips(srcs[i], me1, lands[i], name=f"sum_chips_{group}_{i}") for i in range(len(srcs))]
        swapping[group] = copies_start(_swap_plan, partial, [lax.empty(p.shape, p.dtype) for p in partial],
                                       even_norm, name=f"swap_{group}_start")
        return swapping[group]["token"]

    def swapped(group, behind):
        mine, theirs = copies_wait(_swap_plan, swapping[group], behind, name=f"swap_{group}_wait")
        return list(zip(mine, theirs))

    behind = scattering["even_in"]["token"]
    for group in ("ffn1", "ffn0", "odd", "even_out"):
        behind = reduce_chips(group, behind)
    sums = {group: swapped(group, behind) for group in ("ffn1", "ffn0", "odd", "even_out")}
    outs = {}
    parts_of = {"even_w_out": [sums["even_out"][0]], "odd_w_in": [sums["odd"][0]], "odd_w_out": [sums["odd"][1]],
                "ffn_w_gate": [sums["ffn0"][0], sums["ffn1"][0]], "ffn_w_up": [sums["ffn0"][1], sums["ffn1"][1]],
                "ffn_w_down": [sums["ffn0"][2], sums["ffn1"][2]]}
    for n in parts_of:
        outs[n] = adamw(parts_of[n], wl[n], ml[n], vl[n], name=f"adamw_{n}")
    all_updated = sum(outs[n][1][0, 0, 0] for n in parts_of).reshape(1, 1)
    behind = reduce_chips("even_in", all_updated)
    flip = lambda a: jnp.transpose(a, (0, 2, 1))
    outs["even_w_in"] = [flip(o) for o in adamw([swapped("even_in", behind)[0]], flip(wl["even_w_in"]),
                                                flip(ml["even_w_in"]), flip(vl["even_w_in"]),
                                                name="adamw_even_w_in")]

    g = small["local"]
    small_sum = sum_devices(small["all"], name="sum_devices")
    sg = dict(zip(SMALL_LOCAL_GRADS, _unpack_rows(small_sum, [g[n].shape for n in SMALL_LOCAL_GRADS])))
    own_cols = lambda a, width: lax.dynamic_slice_in_dim(a, me * width, width, axis=a.ndim - 1)
    small_grads = {
        "even_norm": sg["even_norm"], "even_conv": own_cols(sg["even_conv"], 768)[None],
        "even_a_log": sg["a_log"][:, B_HEADS:2 * B_HEADS], "even_dt_bias": sg["dt_bias"][:, B_HEADS:2 * B_HEADS],
        "even_sinks": sg["sinks"][:, :A_HEADS], "even_onorm": sg["onorm"],
        "odd_norm": own_cols(sg["odd_norm"], 512), "odd_ln_g": own_cols(sg["odd_ln_g"], 512),
        "odd_ln_b": own_cols(sg["odd_ln_b"], 512), "odd_w_s": sg["odd_w_s"][None],
        "odd_b_s": sg["odd_b_s"][:, :C_GROUPS].T[None], "ffn_norm": sg["ffn_norm"], "final_norm": sg["final_norm"][0],
    }
    packed = [_pack_rows([d[n] for n in SMALL])[None] for d in (small_grads, wl, ml, vl)]
    small_out = adamw([(packed[0][0],)], packed[1], packed[2], packed[3], name="adamw_small")
    shapes = [wl[n].shape for n in SMALL]
    for j in range(4):
        for n, a in zip(SMALL, _unpack_rows(small_out[j][0], shapes)):
            outs.setdefault(n, [None] * 4)[j] = a

    return (loss, grad_x[None], *[outs[n][0] for n in WEIGHTS], *[outs[n][1] for n in WEIGHTS],
            *[outs[n][2] for n in WEIGHTS], *[outs[n][3] for n in WEIGHTS])
```
